```python
import math
import jax, jax.numpy as jnp
from jax import lax
import numpy as np


D_MODEL = 1024
BATCH = 8
SEQ = 8192
DEPTH = 1

SSM_WIDTH = D_MODEL // 2
SSM_GROUP = 16
SSM_GROUPS = SSM_WIDTH // SSM_GROUP
SSM_STATE = 64
SB_HEAD_DIM = 64
SB_WIDTH = D_MODEL // 2
SB_HEADS = SB_WIDTH // SB_HEAD_DIM
MIX_WIDTH = SSM_WIDTH + SB_WIDTH
IN_WIDTH = SSM_WIDTH + 3 * SB_WIDTH
Q_BLOCK = 128
MEM_LEN = 256
XA_HEADS = 4
XA_HEAD_DIM = 128
XA_WIDTH = XA_HEADS * XA_HEAD_DIM
D_FF = 4 * D_MODEL
NORM_EPS = 1e-6
DT_MIN = 1e-3
DT_MAX = 1e-1

kernel_name = "hymba_s5_stickbreaking_block"


def rmsnorm(x, g):
    xf = x.astype(jnp.float32)
    xf = xf * lax.rsqrt(jnp.mean(xf * xf, axis=-1, keepdims=True) + NORM_EPS)
    return (xf * g.astype(jnp.float32)).astype(x.dtype)


def _linear_recurrence_combine(left, right):
    a_l, b_l = left
    a_r, b_r = right
    return a_r * a_l, a_r * b_l + b_r


def s5_mixer(u, a_re, a_im, log_dt, b_re, b_im, c_re, c_im, d_skip, w_glu):
    bsz, seq, _ = u.shape
    f32 = jnp.float32
    uf = u.astype(f32).reshape(bsz, seq, SSM_GROUPS, SSM_GROUP)
    lam = lax.complex(a_re.astype(f32), a_im.astype(f32))
    dt = jnp.exp(log_dt.astype(f32))[:, None]
    a_bar = jnp.exp(lam * dt)
    b = lax.complex(b_re.astype(f32), b_im.astype(f32))
    b_bar = ((a_bar - 1.0) / lam)[..., None] * b
    c = lax.complex(c_re.astype(f32), c_im.astype(f32))
    bu = jnp.einsum('bsgc,gpc->bsgp', uf.astype(jnp.complex64), b_bar)
    a_seq = jnp.broadcast_to(a_bar[None, None], (1, seq, SSM_GROUPS, SSM_STATE))
    _, states = lax.associative_scan(_linear_recurrence_combine, (a_seq, bu), axis=1)
    y = jnp.einsum('bsgp,gcp->bsgc', states, c).real + d_skip.astype(f32).reshape(SSM_GROUPS, SSM_GROUP) * uf
    y = jax.nn.gelu(y.reshape(bsz, seq, SSM_WIDTH))
    y = y * jax.nn.sigmoid(y @ w_glu.astype(f32))
    return y.astype(u.dtype)


def stick_breaking_attention(q, k, v):
    bsz, seq, h, dh = q.shape
    nblk = seq // Q_BLOCK
    scale = dh ** -0.5
    qb = q.reshape(bsz, nblk, Q_BLOCK, h, dh).transpose(1, 0, 2, 3, 4)
    key_pos = jnp.arange(seq)

    def block(args):
        qi, i = args
        logits = jnp.einsum('bqhd,bkhd->bhqk', qi, k, preferred_element_type=jnp.float32) * scale
        q_pos = i * Q_BLOCK + jnp.arange(Q_BLOCK)
        mask = key_pos[None, :] < q_pos[:, None]
        log_beta = jax.nn.log_sigmoid(logits)
        log_1m_beta = jnp.where(mask, jax.nn.log_sigmoid(-logits), 0.0)
        after = lax.cumsum(log_1m_beta, axis=3, reverse=True) - log_1m_beta
        w = jnp.where(mask, jnp.exp(log_beta + after), 0.0)
        return jnp.einsum('bhqk,bkhd->bqhd', w.astype(v.dtype), v)

    out = lax.map(block, (qb, jnp.arange(nblk)))
    return out.transpose(1, 0, 2, 3, 4).reshape(bsz, seq, h, dh)


def memory_cross_attention(h, mem_n, w_q, w_kv, g_q, g_k, w_o):
    bsz, seq, _ = h.shape
    mlen = mem_n.shape[1]
    q = (h @ w_q).reshape(bsz, seq, XA_HEADS, XA_HEAD_DIM)
    kv = mem_n @ w_kv
    k = kv[..., :XA_WIDTH].reshape(bsz, mlen, XA_HEADS, XA_HEAD_DIM)
    v = kv[..., XA_WIDTH:].reshape(bsz, mlen, XA_HEADS, XA_HEAD_DIM)
    q = rmsnorm(q, g_q)
    k = rmsnorm(k, g_k)
    scores = jnp.einsum('bqhd,bkhd->bhqk', q, k, preferred_element_type=jnp.float32) * (XA_HEAD_DIM ** -0.5)
    p = jax.nn.softmax(scores, axis=-1).astype(v.dtype)
    o = jnp.einsum('bhqk,bkhd->bqhd', p, v).reshape(bsz, seq, XA_WIDTH)
    return o @ w_o


def _fwd_setup_inputs(seed: int = 0) -> dict:
    key = jax.random.key(seed)
    ks = jax.random.split(key, 32)

    def nrm(k, shape, scale):
        return jax.random.normal(k, shape, jnp.float32) * scale

    def gain(k, shape):
        return 1.0 + 0.01 * jax.random.normal(k, shape, jnp.float32)

    L = DEPTH
    n = jnp.arange(SSM_STATE, dtype=jnp.float32)
    return {
        "x": nrm(ks[0], (BATCH, SEQ, D_MODEL), 1.0),
        "mem": nrm(ks[1], (BATCH, MEM_LEN, D_MODEL), 1.0),
        "g_mix": gain(ks[2], (L, D_MODEL)),
        "w_in": nrm(ks[3], (L, D_MODEL, IN_WIDTH), D_MODEL ** -0.5),
        "ssm_a_re": -0.5 + 0.01 * jax.random.normal(ks[4], (L, SSM_GROUPS, SSM_STATE), jnp.float32),
        "ssm_a_im": jnp.pi * n + 0.01 * jax.random.normal(ks[5], (L, SSM_GROUPS, SSM_STATE), jnp.float32),
        "ssm_log_dt": jax.random.uniform(ks[6], (L, SSM_GROUPS), jnp.float32, math.log(DT_MIN), math.log(DT_MAX)),
        "ssm_b_re": nrm(ks[7], (L, SSM_GROUPS, SSM_STATE, SSM_GROUP), (2 * SSM_GROUP) ** -0.5),
        "ssm_b_im": nrm(ks[8], (L, SSM_GROUPS, SSM_STATE, SSM_GROUP), (2 * SSM_GROUP) ** -0.5),
        "ssm_c_re": nrm(ks[9], (L, SSM_GROUPS, SSM_GROUP, SSM_STATE), 0.5),
        "ssm_c_im": nrm(ks[10], (L, SSM_GROUPS, SSM_GROUP, SSM_STATE), 0.5),
        "ssm_d": nrm(ks[11], (L, SSM_WIDTH), 1.0),
        "ssm_w_glu": nrm(ks[12], (L, SSM_WIDTH, SSM_WIDTH), SSM_WIDTH ** -0.5),
        "sb_g_q": gain(ks[13], (L, SB_HEAD_DIM)),
        "sb_g_k": gain(ks[14], (L, SB_HEAD_DIM)),
        "g_out_ssm": gain(ks[15], (L, SSM_WIDTH)),
        "g_out_sb": gain(ks[16], (L, SB_WIDTH)),
        "w_out": nrm(ks[17], (L, MIX_WIDTH, D_MODEL), MIX_WIDTH ** -0.5),
        "g_xa": gain(ks[18], (L, D_MODEL)),
        "g_mem": gain(ks[19], (L, D_MODEL)),
        "xa_w_q": nrm(ks[20], (L, D_MODEL, XA_WIDTH), D_MODEL ** -0.5),
        "xa_w_kv": nrm(ks[21], (L, D_MODEL, 2 * XA_WIDTH), D_MODEL ** -0.5),
        "xa_g_q": gain(ks[22], (L, XA_HEAD_DIM)),
        "xa_g_k": gain(ks[23], (L, XA_HEAD_DIM)),
        "xa_w_o": nrm(ks[24], (L, XA_WIDTH, D_MODEL), XA_WIDTH ** -0.5),
        "g_mlp": gain(ks[25], (L, D_MODEL)),
        "w_up": nrm(ks[26], (L, D_MODEL, D_FF), D_MODEL ** -0.5),
        "w_down": nrm(ks[27], (L, D_FF, D_MODEL), D_FF ** -0.5),
    }


def _fwd_reference(x, mem, g_mix, w_in, ssm_a_re, ssm_a_im, ssm_log_dt, ssm_b_re, ssm_b_im,
              ssm_c_re, ssm_c_im, ssm_d, ssm_w_glu, sb_g_q, sb_g_k, g_out_ssm, g_out_sb,
              w_out, g_xa, g_mem, xa_w_q, xa_w_kv, xa_g_q, xa_g_k, xa_w_o, g_mlp, w_up, w_down):
    bsz, seq, _ = x.shape
    for l in range(DEPTH):
        h = rmsnorm(x, g_mix[l])
        proj = h @ w_in[l]
        u = proj[..., :SSM_WIDTH]
        q = proj[..., SSM_WIDTH:SSM_WIDTH + SB_WIDTH].reshape(bsz, seq, SB_HEADS, SB_HEAD_DIM)
        k = proj[..., SSM_WIDTH + SB_WIDTH:SSM_WIDTH + 2 * SB_WIDTH].reshape(bsz, seq, SB_HEADS, SB_HEAD_DIM)
        v = proj[..., SSM_WIDTH + 2 * SB_WIDTH:].reshape(bsz, seq, SB_HEADS, SB_HEAD_DIM)
        y_ssm = s5_mixer(u, ssm_a_re[l], ssm_a_im[l], ssm_log_dt[l], ssm_b_re[l], ssm_b_im[l],
                         ssm_c_re[l], ssm_c_im[l], ssm_d[l], ssm_w_glu[l])
        q = rmsnorm(q, sb_g_q[l])
        k = rmsnorm(k, sb_g_k[l])
        y_sb = stick_breaking_attention(q, k, v).reshape(bsz, seq, SB_WIDTH)
        y = jnp.concatenate([rmsnorm(y_ssm, g_out_ssm[l]), rmsnorm(y_sb, g_out_sb[l])], axis=-1)
        x = x + y @ w_out[l]
        h = rmsnorm(x, g_xa[l])
        mem_n = rmsnorm(mem, g_mem[l])
        x = x + memory_cross_attention(h, mem_n, xa_w_q[l], xa_w_kv[l], xa_g_q[l], xa_g_k[l], xa_w_o[l])
        h = rmsnorm(x, g_mlp[l])
        x = x + jnp.square(jax.nn.relu(h @ w_up[l])) @ w_down[l]
    return x


import jax as _jax
import jax.numpy as _jnp

TWIN_FORMAT = 'train_step'
FWD_PARAMS = ['x', 'mem', 'g_mix', 'w_in', 'ssm_a_re', 'ssm_a_im', 'ssm_log_dt', 'ssm_b_re', 'ssm_b_im', 'ssm_c_re', 'ssm_c_im', 'ssm_d', 'ssm_w_glu', 'sb_g_q', 'sb_g_k', 'g_out_ssm', 'g_out_sb', 'w_out', 'g_xa', 'g_mem', 'xa_w_q', 'xa_w_kv', 'xa_g_q', 'xa_g_k', 'xa_w_o', 'g_mlp', 'w_up', 'w_down']
TWIN_WEIGHTS = ['g_mix', 'w_in', 'ssm_a_re', 'ssm_a_im', 'ssm_log_dt', 'ssm_b_re', 'ssm_b_im', 'ssm_c_re', 'ssm_c_im', 'ssm_d', 'ssm_w_glu', 'sb_g_q', 'sb_g_k', 'g_out_ssm', 'g_out_sb', 'w_out', 'g_xa', 'g_mem', 'xa_w_q', 'xa_w_kv', 'xa_g_q', 'xa_g_k', 'xa_w_o', 'g_mlp', 'w_up', 'w_down']
TWIN_DIFF_INPUT = 'x'
TWIN_INPUTS = ['x', 'mem', 'g_mix', 'w_in', 'ssm_a_re', 'ssm_a_im', 'ssm_log_dt', 'ssm_b_re', 'ssm_b_im', 'ssm_c_re', 'ssm_c_im', 'ssm_d', 'ssm_w_glu', 'sb_g_q', 'sb_g_k', 'g_out_ssm', 'g_out_sb', 'w_out', 'g_xa', 'g_mem', 'xa_w_q', 'xa_w_kv', 'xa_g_q', 'xa_g_k', 'xa_w_o', 'g_mlp', 'w_up', 'w_down', 'loss_target', 'm_g_mix', 'm_w_in', 'm_ssm_a_re', 'm_ssm_a_im', 'm_ssm_log_dt', 'm_ssm_b_re', 'm_ssm_b_im', 'm_ssm_c_re', 'm_ssm_c_im', 'm_ssm_d', 'm_ssm_w_glu', 'm_sb_g_q', 'm_sb_g_k', 'm_g_out_ssm', 'm_g_out_sb', 'm_w_out', 'm_g_xa', 'm_g_mem', 'm_xa_w_q', 'm_xa_w_kv', 'm_xa_g_q', 'm_xa_g_k', 'm_xa_w_o', 'm_g_mlp', 'm_w_up', 'm_w_down', 'v_g_mix', 'v_w_in', 'v_ssm_a_re', 'v_ssm_a_im', 'v_ssm_log_dt', 'v_ssm_b_re', 'v_ssm_b_im', 'v_ssm_c_re', 'v_ssm_c_im', 'v_ssm_d', 'v_ssm_w_glu', 'v_sb_g_q', 'v_sb_g_k', 'v_g_out_ssm', 'v_g_out_sb', 'v_w_out', 'v_g_xa', 'v_g_mem', 'v_xa_w_q', 'v_xa_w_kv', 'v_xa_g_q', 'v_xa_g_k', 'v_xa_w_o', 'v_g_mlp', 'v_w_up', 'v_w_down']
TWIN_OUTPUTS = ['loss', 'grad_x', 'grad_g_mix', 'grad_w_in', 'grad_ssm_a_re', 'grad_ssm_a_im', 'grad_ssm_log_dt', 'grad_ssm_b_re', 'grad_ssm_b_im', 'grad_ssm_c_re', 'grad_ssm_c_im', 'grad_ssm_d', 'grad_ssm_w_glu', 'grad_sb_g_q', 'grad_sb_g_k', 'grad_g_out_ssm', 'grad_g_out_sb', 'grad_w_out', 'grad_g_xa', 'grad_g_mem', 'grad_xa_w_q', 'grad_xa_w_kv', 'grad_xa_g_q', 'grad_xa_g_k', 'grad_xa_w_o', 'grad_g_mlp', 'grad_w_up', 'grad_w_down', 'delta_g_mix', 'delta_w_in', 'delta_ssm_a_re', 'delta_ssm_a_im', 'delta_ssm_log_dt', 'delta_ssm_b_re', 'delta_ssm_b_im', 'delta_ssm_c_re', 'delta_ssm_c_im', 'delta_ssm_d', 'delta_ssm_w_glu', 'delta_sb_g_q', 'delta_sb_g_k', 'delta_g_out_ssm', 'delta_g_out_sb', 'delta_w_out', 'delta_g_xa', 'delta_g_mem', 'delta_xa_w_q', 'delta_xa_w_kv', 'delta_xa_g_q', 'delta_xa_g_k', 'delta_xa_w_o', 'delta_g_mlp', 'delta_w_up', 'delta_w_down', 'new_m_g_mix', 'new_m_w_in', 'new_m_ssm_a_re', 'new_m_ssm_a_im', 'new_m_ssm_log_dt', 'new_m_ssm_b_re', 'new_m_ssm_b_im', 'new_m_ssm_c_re', 'new_m_ssm_c_im', 'new_m_ssm_d', 'new_m_ssm_w_glu', 'new_m_sb_g_q', 'new_m_sb_g_k', 'new_m_g_out_ssm', 'new_m_g_out_sb', 'new_m_w_out', 'new_m_g_xa', 'new_m_g_mem', 'new_m_xa_w_q', 'new_m_xa_w_kv', 'new_m_xa_g_q', 'new_m_xa_g_k', 'new_m_xa_w_o', 'new_m_g_mlp', 'new_m_w_up', 'new_m_w_down', 'new_v_g_mix', 'new_v_w_in', 'new_v_ssm_a_re', 'new_v_ssm_a_im', 'new_v_ssm_log_dt', 'new_v_ssm_b_re', 'new_v_ssm_b_im', 'new_v_ssm_c_re', 'new_v_ssm_c_im', 'new_v_ssm_d', 'new_v_ssm_w_glu', 'new_v_sb_g_q', 'new_v_sb_g_k', 'new_v_g_out_ssm', 'new_v_g_out_sb', 'new_v_w_out', 'new_v_g_xa', 'new_v_g_mem', 'new_v_xa_w_q', 'new_v_xa_w_kv', 'new_v_xa_g_q', 'new_v_xa_g_k', 'new_v_xa_w_o', 'new_v_g_mlp', 'new_v_w_up', 'new_v_w_down']
TWIN_LEAF_KINDS = {'loss': 'loss', 'grad_x': 'grad_x', 'grad_g_mix': 'grad_w', 'grad_w_in': 'grad_w', 'grad_ssm_a_re': 'grad_w', 'grad_ssm_a_im': 'grad_w', 'grad_ssm_log_dt': 'grad_w', 'grad_ssm_b_re': 'grad_w', 'grad_ssm_b_im': 'grad_w', 'grad_ssm_c_re': 'grad_w', 'grad_ssm_c_im': 'grad_w', 'grad_ssm_d': 'grad_w', 'grad_ssm_w_glu': 'grad_w', 'grad_sb_g_q': 'grad_w', 'grad_sb_g_k': 'grad_w', 'grad_g_out_ssm': 'grad_w', 'grad_g_out_sb': 'grad_w', 'grad_w_out': 'grad_w', 'grad_g_xa': 'grad_w', 'grad_g_mem': 'grad_w', 'grad_xa_w_q': 'grad_w', 'grad_xa_w_kv': 'grad_w', 'grad_xa_g_q': 'grad_w', 'grad_xa_g_k': 'grad_w', 'grad_xa_w_o': 'grad_w', 'grad_g_mlp': 'grad_w', 'grad_w_up': 'grad_w', 'grad_w_down': 'grad_w', 'delta_g_mix': 'delta_w', 'delta_w_in': 'delta_w', 'delta_ssm_a_re': 'delta_w', 'delta_ssm_a_im': 'delta_w', 'delta_ssm_log_dt': 'delta_w', 'delta_ssm_b_re': 'delta_w', 'delta_ssm_b_im': 'delta_w', 'delta_ssm_c_re': 'delta_w', 'delta_ssm_c_im': 'delta_w', 'delta_ssm_d': 'delta_w', 'delta_ssm_w_glu': 'delta_w', 'delta_sb_g_q': 'delta_w', 'delta_sb_g_k': 'delta_w', 'delta_g_out_ssm': 'delta_w', 'delta_g_out_sb': 'delta_w', 'delta_w_out': 'delta_w', 'delta_g_xa': 'delta_w', 'delta_g_mem': 'delta_w', 'delta_xa_w_q': 'delta_w', 'delta_xa_w_kv': 'delta_w', 'delta_xa_g_q': 'delta_w', 'delta_xa_g_k': 'delta_w', 'delta_xa_w_o': 'delta_w', 'delta_g_mlp': 'delta_w', 'delta_w_up': 'delta_w', 'delta_w_down': 'delta_w', 'new_m_g_mix': 'new_m', 'new_m_w_in': 'new_m', 'new_m_ssm_a_re': 'new_m', 'new_m_ssm_a_im': 'new_m', 'new_m_ssm_log_dt': 'new_m', 'new_m_ssm_b_re': 'new_m', 'new_m_ssm_b_im': 'new_m', 'new_m_ssm_c_re': 'new_m', 'new_m_ssm_c_im': 'new_m', 'new_m_ssm_d': 'new_m', 'new_m_ssm_w_glu': 'new_m', 'new_m_sb_g_q': 'new_m', 'new_m_sb_g_k': 'new_m', 'new_m_g_out_ssm': 'new_m', 'new_m_g_out_sb': 'new_m', 'new_m_w_out': 'new_m', 'new_m_g_xa': 'new_m', 'new_m_g_mem': 'new_m', 'new_m_xa_w_q': 'new_m', 'new_m_xa_w_kv': 'new_m', 'new_m_xa_g_q': 'new_m', 'new_m_xa_g_k': 'new_m', 'new_m_xa_w_o': 'new_m', 'new_m_g_mlp': 'new_m', 'new_m_w_up': 'new_m', 'new_m_w_down': 'new_m', 'new_v_g_mix': 'new_v', 'new_v_w_in': 'new_v', 'new_v_ssm_a_re': 'new_v', 'new_v_ssm_a_im': 'new_v', 'new_v_ssm_log_dt': 'new_v', 'new_v_ssm_b_re': 'new_v', 'new_v_ssm_b_im': 'new_v', 'new_v_ssm_c_re': 'new_v', 'new_v_ssm_c_im': 'new_v', 'new_v_ssm_d': 'new_v', 'new_v_ssm_w_glu': 'new_v', 'new_v_sb_g_q': 'new_v', 'new_v_sb_g_k': 'new_v', 'new_v_g_out_ssm': 'new_v', 'new_v_g_out_sb': 'new_v', 'new_v_w_out': 'new_v', 'new_v_g_xa': 'new_v', 'new_v_g_mem': 'new_v', 'new_v_xa_w_q': 'new_v', 'new_v_xa_w_kv': 'new_v', 'new_v_xa_g_q': 'new_v', 'new_v_xa_g_k': 'new_v', 'new_v_xa_w_o': 'new_v', 'new_v_g_mlp': 'new_v', 'new_v_w_up': 'new_v', 'new_v_w_down': 'new_v'}


def _forward(args):
    return _fwd_reference(*[args[k] for k in FWD_PARAMS])


def _output_shape():
    def fwd():
        inp = _fwd_setup_inputs(0)
        return _fwd_reference(*[inp[k] for k in FWD_PARAMS])
    out = _jax.eval_shape(fwd)
    return out.shape, out.dtype

N_MICROBATCH = 1
ADAM_LR = 0.001
ADAM_B1 = 0.9
ADAM_B2 = 0.999
ADAM_EPS = 1e-08
ADAM_WD = 0.01
ADAM_STEP = 10
PER_EXAMPLE_BATCH_AXIS = {'x': 0, 'mem': 0, 'loss_target': 0}
SHARED_INPUTS = []
_WEIGHT_DTYPES = {'g_mix': _jnp.float32, 'w_in': _jnp.float32, 'ssm_a_re': _jnp.float32, 'ssm_a_im': _jnp.float32, 'ssm_log_dt': _jnp.float32, 'ssm_b_re': _jnp.float32, 'ssm_b_im': _jnp.float32, 'ssm_c_re': _jnp.float32, 'ssm_c_im': _jnp.float32, 'ssm_d': _jnp.float32, 'ssm_w_glu': _jnp.float32, 'sb_g_q': _jnp.float32, 'sb_g_k': _jnp.float32, 'g_out_ssm': _jnp.float32, 'g_out_sb': _jnp.float32, 'w_out': _jnp.float32, 'g_xa': _jnp.float32, 'g_mem': _jnp.float32, 'xa_w_q': _jnp.float32, 'xa_w_kv': _jnp.float32, 'xa_g_q': _jnp.float32, 'xa_g_k': _jnp.float32, 'xa_w_o': _jnp.float32, 'g_mlp': _jnp.float32, 'w_up': _jnp.float32, 'w_down': _jnp.float32}
MOMENT_SCALE = {'g_mix': 2.158118e+00, 'w_in': 1.203054e+00, 'ssm_a_re': 8.584140e-01, 'ssm_a_im': 3.650025e-01, 'ssm_log_dt': 2.404486e+02, 'ssm_b_re': 2.909406e-01, 'ssm_b_im': 2.900642e-01, 'ssm_c_re': 1.441310e-01, 'ssm_c_im': 1.536295e-01, 'ssm_d': 2.617133e+01, 'ssm_w_glu': 4.439309e+00, 'sb_g_q': 1.210405e+00, 'sb_g_k': 1.222093e+00, 'g_out_ssm': 1.105886e+02, 'g_out_sb': 6.430722e+01, 'w_out': 1.982877e+01, 'g_xa': 4.827765e-01, 'g_mem': 1.635996e+00, 'xa_w_q': 6.756889e-01, 'xa_w_kv': 1.473371e+00, 'xa_g_q': 5.389637e+00, 'xa_g_k': 5.392469e+00, 'xa_w_o': 1.299424e+00, 'g_mlp': 1.924984e+02, 'w_up': 7.698983e+00, 'w_down': 2.134542e+01}


def _to_microbatches(a, axis):
    t = _jnp.moveaxis(a, axis, 0)
    t = t.reshape((N_MICROBATCH, t.shape[0] // N_MICROBATCH) + t.shape[1:])
    return _jnp.moveaxis(t, 1, axis + 1)


def setup_inputs(seed: int = 0) -> dict:
    inp = _fwd_setup_inputs(seed)
    key = _jax.random.fold_in(_jax.random.key(seed), 7919)
    shape, _ = _output_shape()
    out = dict(inp)
    out["loss_target"] = _jax.random.normal(_jax.random.fold_in(key, 0), shape, _jnp.float32)
    for i, name in enumerate(TWIN_WEIGHTS):
        w = inp[name].astype(_jnp.float32)
        if MOMENT_SCALE is None:
            s = _jnp.sqrt(_jnp.mean(_jnp.square(w)) + 1e-30)
        else:
            s = MOMENT_SCALE[name]
        km, kv = _jax.random.split(_jax.random.fold_in(key, i + 1))
        out[name] = w
        out["m_" + name] = s * _jax.random.normal(km, w.shape, _jnp.float32)
        out["v_" + name] = (s * s) * _jax.random.uniform(kv, w.shape, _jnp.float32, 0.5, 1.5)
    if N_MICROBATCH > 1:
        for name, axis in PER_EXAMPLE_BATCH_AXIS.items():
            out[name] = _to_microbatches(out[name], axis)
    return {'x': out['x'], 'mem': out['mem'], 'g_mix': out['g_mix'], 'w_in': out['w_in'], 'ssm_a_re': out['ssm_a_re'], 'ssm_a_im': out['ssm_a_im'], 'ssm_log_dt': out['ssm_log_dt'], 'ssm_b_re': out['ssm_b_re'], 'ssm_b_im': out['ssm_b_im'], 'ssm_c_re': out['ssm_c_re'], 'ssm_c_im': out['ssm_c_im'], 'ssm_d': out['ssm_d'], 'ssm_w_glu': out['ssm_w_glu'], 'sb_g_q': out['sb_g_q'], 'sb_g_k': out['sb_g_k'], 'g_out_ssm': out['g_out_ssm'], 'g_out_sb': out['g_out_sb'], 'w_out': out['w_out'], 'g_xa': out['g_xa'], 'g_mem': out['g_mem'], 'xa_w_q': out['xa_w_q'], 'xa_w_kv': out['xa_w_kv'], 'xa_g_q': out['xa_g_q'], 'xa_g_k': out['xa_g_k'], 'xa_w_o': out['xa_w_o'], 'g_mlp': out['g_mlp'], 'w_up': out['w_up'], 'w_down': out['w_down'], 'loss_target': out['loss_target'], 'm_g_mix': out['m_g_mix'], 'm_w_in': out['m_w_in'], 'm_ssm_a_re': out['m_ssm_a_re'], 'm_ssm_a_im': out['m_ssm_a_im'], 'm_ssm_log_dt': out['m_ssm_log_dt'], 'm_ssm_b_re': out['m_ssm_b_re'], 'm_ssm_b_im': out['m_ssm_b_im'], 'm_ssm_c_re': out['m_ssm_c_re'], 'm_ssm_c_im': out['m_ssm_c_im'], 'm_ssm_d': out['m_ssm_d'], 'm_ssm_w_glu': out['m_ssm_w_glu'], 'm_sb_g_q': out['m_sb_g_q'], 'm_sb_g_k': out['m_sb_g_k'], 'm_g_out_ssm': out['m_g_out_ssm'], 'm_g_out_sb': out['m_g_out_sb'], 'm_w_out': out['m_w_out'], 'm_g_xa': out['m_g_xa'], 'm_g_mem': out['m_g_mem'], 'm_xa_w_q': out['m_xa_w_q'], 'm_xa_w_kv': out['m_xa_w_kv'], 'm_xa_g_q': out['m_xa_g_q'], 'm_xa_g_k': out['m_xa_g_k'], 'm_xa_w_o': out['m_xa_w_o'], 'm_g_mlp': out['m_g_mlp'], 'm_w_up': out['m_w_up'], 'm_w_down': out['m_w_down'], 'v_g_mix': out['v_g_mix'], 'v_w_in': out['v_w_in'], 'v_ssm_a_re': out['v_ssm_a_re'], 'v_ssm_a_im': out['v_ssm_a_im'], 'v_ssm_log_dt': out['v_ssm_log_dt'], 'v_ssm_b_re': out['v_ssm_b_re'], 'v_ssm_b_im': out['v_ssm_b_im'], 'v_ssm_c_re': out['v_ssm_c_re'], 'v_ssm_c_im': out['v_ssm_c_im'], 'v_ssm_d': out['v_ssm_d'], 'v_ssm_w_glu': out['v_ssm_w_glu'], 'v_sb_g_q': out['v_sb_g_q'], 'v_sb_g_k': out['v_sb_g_k'], 'v_g_out_ssm': out['v_g_out_ssm'], 'v_g_out_sb': out['v_g_out_sb'], 'v_w_out': out['v_w_out'], 'v_g_xa': out['v_g_xa'], 'v_g_mem': out['v_g_mem'], 'v_xa_w_q': out['v_xa_w_q'], 'v_xa_w_kv': out['v_xa_w_kv'], 'v_xa_g_q': out['v_xa_g_q'], 'v_xa_g_k': out['v_xa_g_k'], 'v_xa_w_o': out['v_xa_w_o'], 'v_g_mlp': out['v_g_mlp'], 'v_w_up': out['v_w_up'], 'v_w_down': out['v_w_down']}


def _loss(weights, diff, rest, loss_target):
    with _jax.named_scope("forward"):
        args = {**rest, TWIN_DIFF_INPUT: diff, **{k: w.astype(_WEIGHT_DTYPES[k]) for k, w in weights.items()}}
        y = _forward(args)
    with _jax.named_scope("loss_head"):
        err = _jnp.square(y.astype(_jnp.float32) - loss_target)
        return 0.5 * _jnp.sum(_jnp.mean(err, axis=-1)) if err.ndim else 0.5 * err


def _adamw(w, g, m, v):
    m = ADAM_B1 * m + (1.0 - ADAM_B1) * g
    v = ADAM_B2 * v + (1.0 - ADAM_B2) * _jnp.square(g)
    m_hat = m / (1.0 - ADAM_B1 ** ADAM_STEP)
    v_hat = v / (1.0 - ADAM_B2 ** ADAM_STEP)
    delta = -ADAM_LR * (m_hat / (_jnp.sqrt(v_hat) + ADAM_EPS) + ADAM_WD * w)
    return delta, m, v


def reference(x, mem, g_mix, w_in, ssm_a_re, ssm_a_im, ssm_log_dt, ssm_b_re, ssm_b_im, ssm_c_re, ssm_c_im, ssm_d, ssm_w_glu, sb_g_q, sb_g_k, g_out_ssm, g_out_sb, w_out, g_xa, g_mem, xa_w_q, xa_w_kv, xa_g_q, xa_g_k, xa_w_o, g_mlp, w_up, w_down, loss_target, m_g_mix, m_w_in, m_ssm_a_re, m_ssm_a_im, m_ssm_log_dt, m_ssm_b_re, m_ssm_b_im, m_ssm_c_re, m_ssm_c_im, m_ssm_d, m_ssm_w_glu, m_sb_g_q, m_sb_g_k, m_g_out_ssm, m_g_out_sb, m_w_out, m_g_xa, m_g_mem, m_xa_w_q, m_xa_w_kv, m_xa_g_q, m_xa_g_k, m_xa_w_o, m_g_mlp, m_w_up, m_w_down, v_g_mix, v_w_in, v_ssm_a_re, v_ssm_a_im, v_ssm_log_dt, v_ssm_b_re, v_ssm_b_im, v_ssm_c_re, v_ssm_c_im, v_ssm_d, v_ssm_w_glu, v_sb_g_q, v_sb_g_k, v_g_out_ssm, v_g_out_sb, v_w_out, v_g_xa, v_g_mem, v_xa_w_q, v_xa_w_kv, v_xa_g_q, v_xa_g_k, v_xa_w_o, v_g_mlp, v_w_up, v_w_down):
    given = dict(x=x, mem=mem, g_mix=g_mix, w_in=w_in, ssm_a_re=ssm_a_re, ssm_a_im=ssm_a_im, ssm_log_dt=ssm_log_dt, ssm_b_re=ssm_b_re, ssm_b_im=ssm_b_im, ssm_c_re=ssm_c_re, ssm_c_im=ssm_c_im, ssm_d=ssm_d, ssm_w_glu=ssm_w_glu, sb_g_q=sb_g_q, sb_g_k=sb_g_k, g_out_ssm=g_out_ssm, g_out_sb=g_out_sb, w_out=w_out, g_xa=g_xa, g_mem=g_mem, xa_w_q=xa_w_q, xa_w_kv=xa_w_kv, xa_g_q=xa_g_q, xa_g_k=xa_g_k, xa_w_o=xa_w_o, g_mlp=g_mlp, w_up=w_up, w_down=w_down, loss_target=loss_target, m_g_mix=m_g_mix, m_w_in=m_w_in, m_ssm_a_re=m_ssm_a_re, m_ssm_a_im=m_ssm_a_im, m_ssm_log_dt=m_ssm_log_dt, m_ssm_b_re=m_ssm_b_re, m_ssm_b_im=m_ssm_b_im, m_ssm_c_re=m_ssm_c_re, m_ssm_c_im=m_ssm_c_im, m_ssm_d=m_ssm_d, m_ssm_w_glu=m_ssm_w_glu, m_sb_g_q=m_sb_g_q, m_sb_g_k=m_sb_g_k, m_g_out_ssm=m_g_out_ssm, m_g_out_sb=m_g_out_sb, m_w_out=m_w_out, m_g_xa=m_g_xa, m_g_mem=m_g_mem, m_xa_w_q=m_xa_w_q, m_xa_w_kv=m_xa_w_kv, m_xa_g_q=m_xa_g_q, m_xa_g_k=m_xa_g_k, m_xa_w_o=m_xa_w_o, m_g_mlp=m_g_mlp, m_w_up=m_w_up, m_w_down=m_w_down, v_g_mix=v_g_mix, v_w_in=v_w_in, v_ssm_a_re=v_ssm_a_re, v_ssm_a_im=v_ssm_a_im, v_ssm_log_dt=v_ssm_log_dt, v_ssm_b_re=v_ssm_b_re, v_ssm_b_im=v_ssm_b_im, v_ssm_c_re=v_ssm_c_re, v_ssm_c_im=v_ssm_c_im, v_ssm_d=v_ssm_d, v_ssm_w_glu=v_ssm_w_glu, v_sb_g_q=v_sb_g_q, v_sb_g_k=v_sb_g_k, v_g_out_ssm=v_g_out_ssm, v_g_out_sb=v_g_out_sb, v_w_out=v_w_out, v_g_xa=v_g_xa, v_g_mem=v_g_mem, v_xa_w_q=v_xa_w_q, v_xa_w_kv=v_xa_w_kv, v_xa_g_q=v_xa_g_q, v_xa_g_k=v_xa_g_k, v_xa_w_o=v_xa_w_o, v_g_mlp=v_g_mlp, v_w_up=v_w_up, v_w_down=v_w_down)
    weights = {n: given[n] for n in TWIN_WEIGHTS}
    shared = {n: given[n] for n in SHARED_INPUTS}
    per_example = {n: given[n] for n in ['x', 'mem']}
    grad_fn = _jax.value_and_grad(_loss, argnums=(0, 1))

    def one_microbatch(ex, loss_target):
        ex = dict(ex)
        diff = ex.pop(TWIN_DIFF_INPUT)
        return grad_fn(weights, diff, {**shared, **ex}, loss_target)

    if N_MICROBATCH == 1:
        loss, (grad_w, grad_x) = one_microbatch(per_example, given["loss_target"])
    else:
        def body(carry, xs):
            loss_sum, grad_sum = carry
            l_k, (gw_k, gx_k) = one_microbatch(xs[0], xs[1])
            with _jax.named_scope("update"):
                return (loss_sum + l_k, _jax.tree.map(_jnp.add, grad_sum, gw_k)), gx_k

        init = (_jnp.zeros((), _jnp.float32), _jax.tree.map(_jnp.zeros_like, weights))
        (loss, grad_w), grad_x = _jax.lax.scan(body, init, (per_example, given["loss_target"]))
    with _jax.named_scope("update"):
        delta_w, new_m, new_v = {}, {}, {}
        for n in TWIN_WEIGHTS:
            delta_w[n], new_m[n], new_v[n] = _adamw(weights[n], grad_w[n], given["m_" + n], given["v_" + n])
    return (loss, grad_x, *[grad_w[n] for n in TWIN_WEIGHTS], *[delta_w[n] for n in TWIN_WEIGHTS],
            *[new_m[n] for n in TWIN_WEIGHTS], *[new_v[n] for n in TWIN_WEIGHTS])
```

```python
import functools
import math

import jax
import jax.numpy as jnp
from jax import lax
from jax.experimental import pallas as pl
from jax.experimental.pallas import tpu as pltpu

F32 = jnp.float32
BF16 = jnp.bfloat16
MESH = pl.DeviceIdType.MESH

N_DEV = 8
D_MODEL = 1024
SSM_WIDTH = 512
SSM_GROUP = 16
SSM_GROUPS = 32
SSM_STATE = 64
N_STATE = SSM_GROUPS * SSM_STATE
SB_HEADS = 8
SB_HEAD_DIM = 64
SB_WIDTH = 512
IN_WIDTH = 2048
XA_HEADS = 4
XA_HEAD_DIM = 128
XA_WIDTH = 512
D_FF = 4096
NORM_EPS = 1e-6
ADAM_LR = 0.001
ADAM_B1 = 0.9
ADAM_B2 = 0.999
ADAM_EPS = 1e-08
ADAM_WD = 0.01
ADAM_STEP = 10

LANES = 128
SUBLANES = 8
VMEM_LIMIT = 48 * 1024 * 1024
SCAN_LANES = 512
SB_BLOCK = 256

NN = (((1,), (0,)), ((), ()))
NT = (((1,), (1,)), ((), ()))
TN = (((0,), (0,)), ((), ()))


def _params(sem=None):
    return pltpu.CompilerParams(dimension_semantics=sem, vmem_limit_bytes=VMEM_LIMIT)


def _dot(a, b, dims=NN):
    return lax.dot_general(a.astype(BF16), b.astype(BF16), dims, preferred_element_type=F32)


def _rms(x, g):
    return x * lax.rsqrt(jnp.mean(x * x, axis=-1, keepdims=True) + NORM_EPS) * g


def _mm(a, b, mode, name, *, epi=None, extras=(), out_dtypes=(F32,), tm=512, tn=1024, tk=1024):
    if mode == "nn":
        (m, k), (k2, n) = a.shape, b.shape
    elif mode == "nt":
        (m, k), (n, k2) = a.shape, b.shape
    else:
        (k, m), (k2, n) = a.shape, b.shape
    assert k == k2, (name, a.shape, b.shape)
    tm, tn, tk = min(tm, m), min(tn, n), min(tk, k)
    assert m % tm == 0 and n % tn == 0 and k % tk == 0, (name, m, n, k)
    nk = k // tk
    dims = {"nn": NN, "nt": NT, "tn": TN}[mode]
    if mode == "tn":
        a_spec = pl.BlockSpec((tk, tm), lambda i, j, kk: (kk, i))
    else:
        a_spec = pl.BlockSpec((tm, tk), lambda i, j, kk: (i, kk))
    if mode == "nt":
        b_spec = pl.BlockSpec((tn, tk), lambda i, j, kk: (j, kk))
    else:
        b_spec = pl.BlockSpec((tk, tn), lambda i, j, kk: (kk, j))
    mn_spec = pl.BlockSpec((tm, tn), lambda i, j, kk: (i, j))
    n_ex, n_out = len(extras), len(out_dtypes)

    def body(*refs):
        a_ref, b_ref = refs[:2]
        ex = refs[2:2 + n_ex]
        outs = refs[2 + n_ex:2 + n_ex + n_out]
        acc = refs[-1]
        kk = pl.program_id(2)

        @pl.when(kk == 0)
        def _():
            acc[...] = jnp.zeros_like(acc)

        acc[...] += _dot(a_ref[...], b_ref[...], dims)

        @pl.when(kk == nk - 1)
        def _():
            r = acc[...]
            vals = epi(r, *[e[...] for e in ex]) if epi is not None else (r,)
            for o, v in zip(outs, vals):
                o[...] = v.astype(o.dtype)

    res = pl.pallas_call(
        body, name=name, grid=(m // tm, n // tn, nk),
        in_specs=[a_spec, b_spec] + [mn_spec] * n_ex,
        out_specs=[mn_spec] * n_out,
        out_shape=[jax.ShapeDtypeStruct((m, n), dt) for dt in out_dtypes],
        scratch_shapes=[pltpu.VMEM((tm, tn), F32)],
        compiler_params=_params(("parallel", "parallel", "arbitrary")),
    )(a, b, *extras)
    return res[0] if n_out == 1 else res


def _row_tile(s, target):
    if s <= target:
        return s
    return max(t for t in range(16, target + 1, 16) if s % t == 0)


def _rw(fn, rows, fulls, row_out, acc_out, name, tm=512):
    s = rows[0].shape[0]
    tm = _row_tile(s, tm)
    nr, nf, nro, nao = len(rows), len(fulls), len(row_out), len(acc_out)

    def body(*refs):
        r = refs[:nr]
        f = refs[nr:nr + nf]
        ro = refs[nr + nf:nr + nf + nro]
        ao = refs[nr + nf + nro:]
        outs, accs = fn(*[x[...] for x in r], *[x[...] for x in f])
        for o, v in zip(ro, outs):
            o[...] = v.astype(o.dtype)
        if nao:
            @pl.when(pl.program_id(0) == 0)
            def _():
                for a in ao:
                    a[...] = jnp.zeros_like(a)

            for a, v in zip(ao, accs):
                a[...] += v

    full_spec = lambda shape: pl.BlockSpec(shape, lambda i: (0,) * len(shape))
    res = pl.pallas_call(
        body, name=name, grid=(s // tm,),
        in_specs=[pl.BlockSpec((tm, x.shape[1]), lambda i: (i, 0)) for x in rows]
        + [full_spec(x.shape) for x in fulls],
        out_specs=[pl.BlockSpec((tm, d), lambda i: (i, 0)) for d, _ in row_out]
        + [full_spec(shape) for shape in acc_out],
        out_shape=[jax.ShapeDtypeStruct((s, d), dt) for d, dt in row_out]
        + [jax.ShapeDtypeStruct(shape, F32) for shape in acc_out],
        compiler_params=_params(("arbitrary",)),
    )(*rows, *fulls)
    return res


def _norm_fwd(x, g, name):
    return _rw(lambda xt, gt: ((_rms(xt, gt),), ()), [x], [g], [(x.shape[1], BF16)], [], name)[0]


def _norm_bwd(x, g, dh, dres, name):
    def fn(xt, dht, drt, gt):
        _, vjp = jax.vjp(_rms, xt, gt)
        dx, dg = vjp(dht)
        return (dx + drt,), (dg,)

    return _rw(fn, [x, dh, dres], [g], [(x.shape[1], F32)], [g.shape], name)


def _log_sigmoid(z):
    return jnp.minimum(z, 0.0) - jnp.log(1.0 + jnp.exp(-jnp.abs(z)))


def _split_dot(x, u):
    hi = x.astype(BF16)
    lo = (x - hi.astype(F32)).astype(BF16)
    return (jnp.dot(hi, u, preferred_element_type=F32) + jnp.dot(lo, u, preferred_element_type=F32))


def _sb_block(q, kb, n, a_run, scale, tri, u_after):
    z = lax.dot_general(q, kb, NT, preferred_element_type=F32) * scale
    lb = _log_sigmoid(z)
    valid = jnp.logical_or(n > 0, tri)
    l = jnp.where(valid, lb - z, 0.0)
    after = a_run + _split_dot(l, u_after)
    w = jnp.where(valid, jnp.exp(lb + after), 0.0)
    return valid, lb, l, w


def _sb_fwd(qn, kn, v, name):
    h, s, dh = qn.shape
    b = min(SB_BLOCK, s)
    scale = dh ** -0.5

    def body(q_ref, k_ref, v_ref, o_ref):
        qi = pl.program_id(1)
        q = q_ref[0]
        row = lax.broadcasted_iota(jnp.int32, (b, b), 0)
        col = lax.broadcasted_iota(jnp.int32, (b, b), 1)
        tri = col < row
        u_after = (row > col).astype(BF16)

        def step(n, carry):
            a_run, acc = carry
            off = pl.multiple_of((qi - n) * b, b)
            kb = k_ref[0, pl.ds(off, b), :]
            vb = v_ref[0, pl.ds(off, b), :]
            _, _, l, w = _sb_block(q, kb, n, a_run, scale, tri, u_after)
            acc = acc + jnp.dot(w.astype(BF16), vb, preferred_element_type=F32)
            return a_run + jnp.sum(l, axis=1, keepdims=True), acc

        _, acc = lax.fori_loop(0, qi + 1, step, (jnp.zeros((b, 1), F32), jnp.zeros((b, dh), F32)))
        o_ref[0] = acc

    return pl.pallas_call(
        body, name=name, grid=(h, s // b),
        in_specs=[pl.BlockSpec((1, b, dh), lambda hh, i: (hh, i, 0)),
                  pl.BlockSpec((1, s, dh), lambda hh, i: (hh, 0, 0)),
                  pl.BlockSpec((1, s, dh), lambda hh, i: (hh, 0, 0))],
        out_specs=pl.BlockSpec((1, b, dh), lambda hh, i: (hh, i, 0)),
        out_shape=jax.ShapeDtypeStruct((h, s, dh), F32),
        compiler_params=_params(("parallel", "arbitrary")),
    )(qn, kn, v)


def _sb_bwd(qn, kn, v, out, dout, name):
    h, s, dh = qn.shape
    b = min(SB_BLOCK, s)
    scale = dh ** -0.5

    def body(q_ref, k_ref, v_ref, o_ref, do_ref, dq_ref, dk_ref, dv_ref):
        qi = pl.program_id(1)

        @pl.when(qi == 0)
        def _():
            dk_ref[...] = jnp.zeros_like(dk_ref)
            dv_ref[...] = jnp.zeros_like(dv_ref)

        q = q_ref[0]
        do = do_ref[0]
        dob = do.astype(BF16)
        d_tot = jnp.sum(dob.astype(F32) * o_ref[0], axis=1, keepdims=True)
        row = lax.broadcasted_iota(jnp.int32, (b, b), 0)
        col = lax.broadcasted_iota(jnp.int32, (b, b), 1)
        tri = col < row
        u_after = (row > col).astype(BF16)
        u_from = (row >= col).astype(BF16)

        def step(n, carry):
            a_run, r_run, dq = carry
            off = pl.multiple_of((qi - n) * b, b)
            kb = k_ref[0, pl.ds(off, b), :]
            vb = v_ref[0, pl.ds(off, b), :]
            valid, lb, l, w = _sb_block(q, kb, n, a_run, scale, tri, u_after)
            dw = lax.dot_general(dob, vb, NT, preferred_element_type=F32)
            wb = w.astype(BF16)
            g = dw * wb.astype(F32)
            g_before = d_tot - (r_run + _split_dot(g, u_from))
            dz = jnp.where(valid, g - (g + g_before) * jnp.exp(lb), 0.0) * scale
            dzb = dz.astype(BF16)
            dq = dq + jnp.dot(dzb, kb, preferred_element_type=F32)
            dk_ref[0, pl.ds(off, b), :] += lax.dot_general(dzb, q, TN, preferred_element_type=F32)
            dv_ref[0, pl.ds(off, b), :] += lax.dot_general(wb, dob, TN, preferred_element_type=F32)
            return (a_run + jnp.sum(l, axis=1, keepdims=True),
                    r_run + jnp.sum(g, axis=1, keepdims=True), dq)

        zero = jnp.zeros((b, 1), F32)
        _, _, dq = lax.fori_loop(0, qi + 1, step, (zero, zero, jnp.zeros((b, dh), F32)))
        dq_ref[0] = dq

    blk = pl.BlockSpec((1, b, dh), lambda hh, i: (hh, i, 0))
    full = pl.BlockSpec((1, s, dh), lambda hh, i: (hh, 0, 0))
    return pl.pallas_call(
        body, name=name, grid=(h, s // b),
        in_specs=[blk, full, full, blk, blk],
        out_specs=[blk, full, full],
        out_shape=[jax.ShapeDtypeStruct((h, s, dh), F32)] * 3,
        compiler_params=_params(("parallel", "arbitrary")),
    )(qn, kn, v, out, dout)


def _cmul(xr, xi, yr, yi):
    return xr * yr - xi * yi, xr * yi + xi * yr


def _scan_consts(ar, ai, reverse, lc):
    rowi = lax.broadcasted_iota(jnp.int32, (SUBLANES, lc), 0)
    pows = [(ar, ai)]
    for _ in range(SUBLANES - 1):
        pows.append(_cmul(*pows[-1], ar, ai))
    steps = []
    for d in (1, 2, 4):
        keep = (rowi < SUBLANES - d) if reverse else (rowi >= d)
        pr, pi = pows[d - 1]
        steps.append((SUBLANES - d if reverse else d, jnp.where(keep, pr, 0.0), jnp.where(keep, pi, 0.0)))
    cr = jnp.zeros((SUBLANES, lc), F32)
    ci = jnp.zeros((SUBLANES, lc), F32)
    for r in range(SUBLANES):
        pr, pi = pows[SUBLANES - 1 - r] if reverse else pows[r]
        cr = jnp.where(rowi == r, pr, cr)
        ci = jnp.where(rowi == r, pi, ci)
    return steps, cr, ci


def _scan_tile(xr, xi, steps, pr, pi, cr, ci):
    for shift, ar, ai in steps:
        rr = pltpu.roll(xr, shift, 0)
        ri = pltpu.roll(xi, shift, 0)
        xr, xi = xr + ar * rr - ai * ri, xi + ar * ri + ai * rr
    return xr + pr * cr - pi * ci, xi + pr * ci + pi * cr


def _scan_fwd(bu, acat, name, tt=1024):
    s, width = bu.shape
    lc = SCAN_LANES
    tt = min(tt, s)
    nl, nt = width // (2 * lc), s // tt

    def body(bu_ref, a_ref, s_ref, carry):
        @pl.when(pl.program_id(1) == 0)
        def _():
            carry[...] = jnp.zeros_like(carry)

        steps, pr, pi = _scan_consts(a_ref[:, :lc], a_ref[:, lc:], False, lc)

        def tile(i, c):
            off = pl.multiple_of(i * SUBLANES, SUBLANES)
            xr, xi = _scan_tile(bu_ref[pl.ds(off, SUBLANES), :lc], bu_ref[pl.ds(off, SUBLANES), lc:],
                                steps, pr, pi, c[0], c[1])
            s_ref[pl.ds(off, SUBLANES), :lc] = xr
            s_ref[pl.ds(off, SUBLANES), lc:] = xi
            return (jnp.broadcast_to(xr[SUBLANES - 1:, :], (SUBLANES, lc)),
                    jnp.broadcast_to(xi[SUBLANES - 1:, :], (SUBLANES, lc)))

        cr, ci = lax.fori_loop(0, tt // SUBLANES, tile, (carry[:, :lc], carry[:, lc:]))
        carry[:, :lc] = cr
        carry[:, lc:] = ci

    return pl.pallas_call(
        body, name=name, grid=(nl, nt),
        in_specs=[pl.BlockSpec((tt, 2 * lc), lambda j, c: (c, j)), pl.BlockSpec((1, 2 * lc), lambda j, c: (0, j))],
        out_specs=pl.BlockSpec((tt, 2 * lc), lambda j, c: (c, j)),
        out_shape=jax.ShapeDtypeStruct((s, width), F32),
        scratch_shapes=[pltpu.VMEM((SUBLANES, 2 * lc), F32)],
        compiler_params=_params(("parallel", "arbitrary")),
    )(bu, acat)


def _scan_bwd(gs, states, acat, name, tt=1024):
    s, width = gs.shape
    lc = SCAN_LANES
    tt = min(tt, s)
    nl, nt = width // (2 * lc), s // tt
    nt8 = tt // SUBLANES

    def body(gs_ref, s_ref, sp_ref, a_ref, lam_ref, da_ref, carry):
        c = pl.program_id(1)

        @pl.when(c == 0)
        def _():
            carry[...] = jnp.zeros_like(carry)
            da_ref[...] = jnp.zeros_like(da_ref)

        steps, pr, pi = _scan_consts(a_ref[:, :lc], -a_ref[:, lc:], True, lc)
        rowi = lax.broadcasted_iota(jnp.int32, (SUBLANES, lc), 0)
        first_chunk = c == nt - 1

        def tile(i, carry_v):
            cr, ci, dar, dai = carry_v
            t = nt8 - 1 - i
            off = pl.multiple_of(t * SUBLANES, SUBLANES)
            lr, li = _scan_tile(gs_ref[pl.ds(off, SUBLANES), :lc], gs_ref[pl.ds(off, SUBLANES), lc:],
                                steps, pr, pi, cr, ci)
            lam_ref[pl.ds(off, SUBLANES), :lc] = lr
            lam_ref[pl.ds(off, SUBLANES), lc:] = li
            offp = pl.multiple_of(jnp.maximum(t - 1, 0) * SUBLANES, SUBLANES)
            in_chunk = t > 0
            use = jnp.logical_or(in_chunk, jnp.logical_not(first_chunk))
            prev_r = jnp.where(in_chunk, s_ref[pl.ds(offp, SUBLANES), :lc], sp_ref[:, :lc])
            prev_i = jnp.where(in_chunk, s_ref[pl.ds(offp, SUBLANES), lc:], sp_ref[:, lc:])
            last_r = jnp.where(use, jnp.broadcast_to(prev_r[SUBLANES - 1:, :], (SUBLANES, lc)), 0.0)
            last_i = jnp.where(use, jnp.broadcast_to(prev_i[SUBLANES - 1:, :], (SUBLANES, lc)), 0.0)
            sr = jnp.where(rowi == 0, last_r, pltpu.roll(s_ref[pl.ds(off, SUBLANES), :lc], 1, 0))
            si = jnp.where(rowi == 0, last_i, pltpu.roll(s_ref[pl.ds(off, SUBLANES), lc:], 1, 0))
            dar = dar + lr * sr + li * si
            dai = dai + li * sr - lr * si
            return (jnp.broadcast_to(lr[:1, :], (SUBLANES, lc)), jnp.broadcast_to(li[:1, :], (SUBLANES, lc)),
                    dar, dai)

        zero = jnp.zeros((SUBLANES, lc), F32)
        cr, ci, dar, dai = lax.fori_loop(0, nt8, tile, (carry[:, :lc], carry[:, lc:], zero, zero))
        carry[:, :lc] = cr
        carry[:, lc:] = ci
        da_ref[:, :lc] += dar
        da_ref[:, lc:] += dai

    rev = lambda j, c: (nt - 1 - c, j)
    return pl.pallas_call(
        body, name=name, grid=(nl, nt),
        in_specs=[pl.BlockSpec((tt, 2 * lc), rev), pl.BlockSpec((tt, 2 * lc), rev),
                  pl.BlockSpec((SUBLANES, 2 * lc), lambda j, c: (jnp.maximum((nt - 1 - c) * nt8 - 1, 0), j)),
                  pl.BlockSpec((1, 2 * lc), lambda j, c: (0, j))],
        out_specs=[pl.BlockSpec((tt, 2 * lc), rev), pl.BlockSpec((SUBLANES, 2 * lc), lambda j, c: (0, j))],
        out_shape=[jax.ShapeDtypeStruct((s, width), F32), jax.ShapeDtypeStruct((SUBLANES, width), F32)],
        scratch_shapes=[pltpu.VMEM((SUBLANES, 2 * lc), F32)],
        compiler_params=_params(("parallel", "arbitrary")),
    )(gs, states, states, acat)


def _state_cols(x):
    xr, xi = x
    lead = xr.shape[:-1]
    nl = N_STATE // SCAN_LANES
    both = jnp.stack([xr.reshape(lead + (nl, SCAN_LANES)), xi.reshape(lead + (nl, SCAN_LANES))], axis=-2)
    return both.reshape(lead + (2 * N_STATE,))


def _ssm_mats(a_re, a_im, log_dt, b_re, b_im, c_re, c_im):
    dt = jnp.exp(log_dt)[:, None]
    lr, li = a_re * dt, a_im * dt
    e = jnp.exp(lr)
    abar_r, abar_i = e * jnp.cos(li), e * jnp.sin(li)
    den = a_re * a_re + a_im * a_im
    coef_r = ((abar_r - 1.0) * a_re + abar_i * a_im) / den
    coef_i = (abar_i * a_re - (abar_r - 1.0) * a_im) / den
    bbar_r = coef_r[..., None] * b_re - coef_i[..., None] * b_im
    bbar_i = coef_r[..., None] * b_im + coef_i[..., None] * b_re
    eye = jnp.eye(SSM_GROUPS, dtype=bool)[:, None, :, None]

    def in_mat(bb):
        t = jnp.transpose(bb, (0, 2, 1))[:, :, None, :]
        return jnp.where(eye, t, 0.0).reshape(SSM_WIDTH, N_STATE)

    def out_mat(cc):
        return jnp.where(eye, cc[:, :, None, :], 0.0).reshape(SSM_WIDTH, N_STATE)

    acat = _state_cols((abar_r.reshape(1, N_STATE), abar_i.reshape(1, N_STATE)))
    bmat = _state_cols((in_mat(bbar_r), in_mat(bbar_i)))
    cmat = _state_cols((out_mat(c_re), -out_mat(c_im)))
    return acat, bmat, cmat


def _mem_fwd(mem, g_mem, w_kv, g_k, name):
    ml = mem.shape[0]

    def body(mem_ref, gm_ref, w_ref, gk_ref, memn_ref, kv_ref, kn_ref, vv_ref):
        memn = _rms(mem_ref[...], gm_ref[...])
        memn_ref[...] = memn.astype(BF16)
        kv = _dot(memn, w_ref[...])
        kv_ref[...] = kv
        for hh in range(XA_HEADS):
            sl = slice(hh * XA_HEAD_DIM, (hh + 1) * XA_HEAD_DIM)
            kn_ref[:, sl] = _rms(kv[:, sl], gk_ref[...]).astype(BF16)
        vv_ref[...] = kv[:, XA_WIDTH:].astype(BF16)

    return pl.pallas_call(
        body, name=name,
        out_shape=[jax.ShapeDtypeStruct((ml, D_MODEL), BF16), jax.ShapeDtypeStruct((ml, 2 * XA_WIDTH), F32),
                   jax.ShapeDtypeStruct((ml, XA_WIDTH), BF16), jax.ShapeDtypeStruct((ml, XA_WIDTH), BF16)],
        compiler_params=_params(),
    )(mem, g_mem, w_kv, g_k)


def _mem_bwd(mem, g_mem, memn, w_kv, kv, g_k, dkn, dvv, name):
    def body(mem_ref, gm_ref, memn_ref, w_ref, kv_ref, gk_ref, dkn_ref, dvv_ref, dw_ref, dgm_ref, dgk_ref):
        kv = kv_ref[...]
        dgk = jnp.zeros(dgk_ref.shape, F32)
        parts = []
        for hh in range(XA_HEADS):
            sl = slice(hh * XA_HEAD_DIM, (hh + 1) * XA_HEAD_DIM)
            _, vjp = jax.vjp(_rms, kv[:, sl], gk_ref[...])
            dk, dg = vjp(dkn_ref[:, sl])
            parts.append(dk)
            dgk = dgk + dg
        dgk_ref[...] = dgk
        dkv = jnp.concatenate(parts + [dvv_ref[...]], axis=1)
        dw_ref[...] = _dot(memn_ref[...], dkv, TN)
        dmemn = _dot(dkv, w_ref[...], NT)
        _, vjp = jax.vjp(_rms, mem_ref[...], gm_ref[...])
        dgm_ref[...] = vjp(dmemn)[1]

    return pl.pallas_call(
        body, name=name,
        out_shape=[jax.ShapeDtypeStruct((D_MODEL, 2 * XA_WIDTH), F32), jax.ShapeDtypeStruct(g_mem.shape, F32),
                   jax.ShapeDtypeStruct(g_k.shape, F32)],
        compiler_params=_params(),
    )(mem, g_mem, memn, w_kv, kv, g_k, dkn, dvv)


def _xa_head(qx_h, g_q, kn_h, vv_h):
    qn = _rms(qx_h, g_q)
    sc = _dot(qn, kn_h, NT) * (XA_HEAD_DIM ** -0.5)
    sc = sc - jnp.max(sc, axis=-1, keepdims=True)
    e = jnp.exp(sc)
    p = e / jnp.sum(e, axis=-1, keepdims=True)
    return qn, p


def _xa_fwd(qx, g_q, kn, vv, name):
    def fn(qt, gq, knt, vvt):
        outs = []
        for hh in range(XA_HEADS):
            sl = slice(hh * XA_HEAD_DIM, (hh + 1) * XA_HEAD_DIM)
            _, p = _xa_head(qt[:, sl], gq, knt[:, sl], vvt[:, sl])
            outs.append(_dot(p, vvt[:, sl]))
        return (jnp.concatenate(outs, axis=1),), ()

    return _rw(fn, [qx], [g_q, kn, vv], [(XA_WIDTH, BF16)], [], name)[0]


def _xa_bwd(qx, g_q, kn, vv, do, name):
    def fn(qt, dot_, gq, knt, vvt):
        dqs, dks, dvs = [], [], []
        dgq = jnp.zeros_like(gq)
        for hh in range(XA_HEADS):
            sl = slice(hh * XA_HEAD_DIM, (hh + 1) * XA_HEAD_DIM)
            qn, p = _xa_head(qt[:, sl], gq, knt[:, sl], vvt[:, sl])
            doh = dot_[:, sl]
            dp = _dot(doh, vvt[:, sl], NT)
            dvs.append(_dot(p, doh, TN))
            ds = p * (dp - jnp.sum(dp * p, axis=-1, keepdims=True)) * (XA_HEAD_DIM ** -0.5)
            dqn = _dot(ds, knt[:, sl])
            dks.append(_dot(ds, qn, TN))
            _, vjp = jax.vjp(_rms, qt[:, sl], gq)
            dq, dg = vjp(dqn)
            dqs.append(dq)
            dgq = dgq + dg
        return ((jnp.concatenate(dqs, axis=1),),
                (jnp.concatenate(dks, axis=1), jnp.concatenate(dvs, axis=1), dgq))

    return _rw(fn, [qx, do], [g_q, kn, vv], [(XA_WIDTH, F32)], [kn.shape, vv.shape, g_q.shape], name)


ANY = pl.BlockSpec(memory_space=pl.ANY)


def _all_gather(block, name):
    m_per, n = block.shape

    def body(x_ref, out_ref, send_sems, recv_sems, local_sem):
        x, y, c = lax.axis_index("x"), lax.axis_index("y"), lax.axis_index("c")
        me, sibling = (x, y, c), (x, y, 1 - c)
        chips = [(1 - x, y), (x, 1 - y), (1 - x, 1 - y)]

        def rows(px, py, pc):
            return out_ref.at[pl.ds((4 * px + 2 * py + pc) * m_per, m_per), :]

        def copy(k, blk, to, src=None):
            return pltpu.make_async_remote_copy(
                src_ref=rows(*blk) if src is None else src, dst_ref=rows(*blk),
                send_sem=send_sems.at[k], recv_sem=recv_sems.at[k], device_id=to, device_id_type=MESH)

        mine = pltpu.make_async_copy(x_ref, rows(*me), local_sem)
        mine.start()
        first = [copy(0, me, sibling, src=x_ref)]
        first += [copy(1 + j, me, (*chip, c), src=x_ref) for j, chip in enumerate(chips)]
        for cp in first:
            cp.start()
        passed = [copy(4 + j, (*chip, c), sibling) for j, chip in enumerate(chips)]
        for j, chip in enumerate(chips):
            copy(1 + j, (*chip, c), me).wait_recv()
            passed[j].start()
        copy(0, sibling, me).wait_recv()
        for j, chip in enumerate(chips):
            copy(4 + j, (*chip, 1 - c), me).wait_recv()
        for cp in first + passed:
            cp.wait_send()
        mine.wait()

    return pl.pallas_call(
        body, name=name, in_specs=[ANY], out_specs=ANY,
        out_shape=jax.ShapeDtypeStruct((N_DEV * m_per, n), block.dtype),
        scratch_shapes=[pltpu.SemaphoreType.DMA((7,)), pltpu.SemaphoreType.DMA((7,)), pltpu.SemaphoreType.DMA],
    )(block)


def _swap_sibling(g, name):
    _, r, n = g.shape

    def body(g_ref, out_ref, send_sem, recv_sem):
        x, y, c = lax.axis_index("x"), lax.axis_index("y"), lax.axis_index("c")
        cp = pltpu.make_async_remote_copy(
            src_ref=g_ref.at[1 - c], dst_ref=out_ref, send_sem=send_sem, recv_sem=recv_sem,
            device_id=(x, y, 1 - c), device_id_type=MESH)
        cp.start()
        cp.wait()

    return pl.pallas_call(
        body, name=name, in_specs=[ANY], out_specs=ANY,
        out_shape=jax.ShapeDtypeStruct((r, n), g.dtype),
        scratch_shapes=[pltpu.SemaphoreType.DMA, pltpu.SemaphoreType.DMA],
    )(g)


def _exchange_chips(p, name):
    _, r, n = p.shape

    def body(p_ref, out_ref, send_sems, recv_sems):
        x, y, c = lax.axis_index("x"), lax.axis_index("y"), lax.axis_index("c")
        chips = [(1 - x, y), (x, 1 - y), (1 - x, 1 - y)]
        cps = [pltpu.make_async_remote_copy(
            src_ref=p_ref.at[2 * cx + cy], dst_ref=out_ref.at[j], send_sem=send_sems.at[j],
            recv_sem=recv_sems.at[j], device_id=(cx, cy, c), device_id_type=MESH)
            for j, (cx, cy) in enumerate(chips)]
        for cp in cps:
            cp.start()
        for cp in cps:
            cp.wait()

    return pl.pallas_call(
        body, name=name, in_specs=[ANY], out_specs=ANY,
        out_shape=jax.ShapeDtypeStruct((3, r, n), p.dtype),
        scratch_shapes=[pltpu.SemaphoreType.DMA((3,)), pltpu.SemaphoreType.DMA((3,))],
    )(p)


def _adam_math(w, g, m, v):
    m = ADAM_B1 * m + (1.0 - ADAM_B1) * g
    v = ADAM_B2 * v + (1.0 - ADAM_B2) * (g * g)
    m_hat = m / (1.0 - ADAM_B1 ** ADAM_STEP)
    v_hat = v / (1.0 - ADAM_B2 ** ADAM_STEP)
    delta = -ADAM_LR * (m_hat / (jnp.sqrt(v_hat) + ADAM_EPS) + ADAM_WD * w)
    return delta, m, v


def _adam(parts, w, m, v, name):
    def fn(*t):
        g = t[0]
        for p in t[1:len(parts)]:
            g = g + p
        wt, mt, vt = t[len(parts):]
        d, mn, vn = _adam_math(wt, g, mt, vt)
        return (g, d, mn, vn), ()

    n = w.shape[1]
    return _rw(fn, list(parts) + [w, m, v], [], [(n, F32)] * 4, [], name, tm=64)


BIG = [
    ("w_in", (D_MODEL, IN_WIDTH), 1), ("ssm_w_glu", (SSM_WIDTH, SSM_WIDTH), 0), ("w_out", (D_MODEL, D_MODEL), 0),
    ("xa_w_q", (D_MODEL, XA_WIDTH), 0), ("xa_w_kv", (D_MODEL, 2 * XA_WIDTH), 0), ("xa_w_o", (XA_WIDTH, D_MODEL), 1),
    ("w_up", (D_MODEL, D_FF), 1), ("w_down", (D_FF, D_MODEL), 0),
]
SMALL = ["g_mix", "ssm_a_re", "ssm_a_im", "ssm_log_dt", "ssm_b_re", "ssm_b_im", "ssm_c_re", "ssm_c_im", "ssm_d",
         "sb_g_q", "sb_g_k", "g_out_ssm", "g_out_sb", "g_xa", "g_mem", "xa_g_q", "xa_g_k", "g_mlp"]
PACK_COLS = 1024
PACK_ALIGN = 16 * PACK_COLS


def _shard_shape(shape, axis):
    return tuple(d // N_DEV if i == axis else d for i, d in enumerate(shape))


def _pad_flat(a):
    flat = a.reshape(-1)
    pad = (-flat.shape[0]) % PACK_ALIGN
    return jnp.pad(flat, (0, pad)) if pad else flat


def _pack(arrs):
    return jnp.concatenate([_pad_flat(a) for a in arrs]).reshape(-1, PACK_COLS)


def _unpack(packed, shapes):
    flat = packed.reshape(-1)
    out, off = [], 0
    for shp in shapes:
        n = math.prod(shp)
        out.append(flat[off:off + n].reshape(shp))
        off += n + (-n) % PACK_ALIGN
    return out


def _full_from_shards(g, shape, axis):
    if axis == 0:
        return g.reshape(shape)
    return jnp.transpose(g, (1, 0, 2)).reshape(shape)


def _shards_of_full(a, axis):
    if axis == 0:
        return a.reshape((N_DEV, a.shape[0] // N_DEV, a.shape[1]))
    return jnp.transpose(a.reshape(a.shape[0], N_DEV, a.shape[1] // N_DEV), (1, 0, 2))


def _heads(a, nh):
    s, wd = a.shape
    return jnp.transpose(a.reshape(s, nh, wd // nh), (1, 0, 2))


def _unheads(a):
    nh, s, dh = a.shape
    return jnp.transpose(a, (1, 0, 2)).reshape(s, nh * dh)


def _local_step(x, mem, target, w, sm):
    s = x.shape[0]
    g = {}
    row = lambda a: a.reshape(1, -1)
    g_mix, g_xa, g_mlp, g_mem = row(sm["g_mix"]), row(sm["g_xa"]), row(sm["g_mlp"]), row(sm["g_mem"])
    g_os, g_ob = row(sm["g_out_ssm"]), row(sm["g_out_sb"])
    sb_gq, sb_gk = row(sm["sb_g_q"]), row(sm["sb_g_k"])
    xa_gq, xa_gk = row(sm["xa_g_q"]), row(sm["xa_g_k"])
    d_skip = row(sm["ssm_d"])

    h1 = _norm_fwd(x, g_mix, "norm_mix")
    proj = _mm(h1, w["w_in"], "nn", "in_proj")
    u = proj[:, :SSM_WIDTH]
    q_raw = _heads(proj[:, SSM_WIDTH:SSM_WIDTH + SB_WIDTH], SB_HEADS).reshape(SB_HEADS * s, SB_HEAD_DIM)
    k_raw = _heads(proj[:, SSM_WIDTH + SB_WIDTH:SSM_WIDTH + 2 * SB_WIDTH], SB_HEADS).reshape(SB_HEADS * s, SB_HEAD_DIM)
    v_h = _heads(proj[:, SSM_WIDTH + 2 * SB_WIDTH:], SB_HEADS).astype(BF16)
    qn = _norm_fwd(q_raw, sb_gq, "sb_qnorm").reshape(SB_HEADS, s, SB_HEAD_DIM)
    kn = _norm_fwd(k_raw, sb_gk, "sb_knorm").reshape(SB_HEADS, s, SB_HEAD_DIM)
    y_sb_h = _sb_fwd(qn, kn, v_h, "sb_fwd")
    y_sb = _unheads(y_sb_h)

    ssm_args = (sm["ssm_a_re"], sm["ssm_a_im"], sm["ssm_log_dt"], sm["ssm_b_re"], sm["ssm_b_im"],
                sm["ssm_c_re"], sm["ssm_c_im"])
    (acat, bmat, cmat), mats_vjp = jax.vjp(_ssm_mats, *ssm_args)
    bu = _mm(u, bmat, "nn", "ssm_bu")
    states = _scan_fwd(bu, acat, "ssm_scan")

    def gelu_epi(r, ut, dt):
        y0 = r + dt * ut
        y1 = jax.nn.gelu(y0)
        return y0, y1

    y0, y1 = _mm(states, cmat, "nt", "ssm_out", epi=gelu_epi, extras=(u, jnp.broadcast_to(d_skip, u.shape)),
                 out_dtypes=(F32, F32))
    z_glu, y_ssm = _mm(y1, w["ssm_w_glu"], "nn", "ssm_glu", epi=lambda r, yt: (r, yt * jax.nn.sigmoid(r)),
                       extras=(y1,), out_dtypes=(F32, F32))

    def cat_norm(a, b, ga, gb):
        return jnp.concatenate([_rms(a, ga), _rms(b, gb)], axis=1)

    ycat = _rw(lambda a, b, ga, gb: ((cat_norm(a, b, ga, gb),), ()), [y_ssm, y_sb], [g_os, g_ob],
               [(D_MODEL, BF16)], [], "norm_out")[0]
    x1 = _mm(ycat, w["w_out"], "nn", "out_proj", epi=lambda r, xt: (r + xt,), extras=(x,))
    h2 = _norm_fwd(x1, g_xa, "norm_xa")
    qx = _mm(h2, w["xa_w_q"], "nn", "xa_q")
    memn, kv, kn_x, vv_x = _mem_fwd(mem, g_mem, w["xa_w_kv"], xa_gk, "xa_mem")
    o_xa = _xa_fwd(qx, xa_gq, kn_x, vv_x, "xa_fwd")
    x2 = _mm(o_xa, w["xa_w_o"], "nn", "xa_o", epi=lambda r, xt: (r + xt,), extras=(x1,))
    h3 = _norm_fwd(x2, g_mlp, "norm_mlp")

    def up_epi(r):
        rl = jnp.maximum(r, 0.0)
        return r, rl * rl

    a_up, r_up = _mm(h3, w["w_up"], "nn", "mlp_up", epi=up_epi, out_dtypes=(F32, BF16))

    def loss_epi(r, xt, tt):
        d = r + xt - tt
        return (d * (1.0 / D_MODEL),)

    dx3 = _mm(r_up, w["w_down"], "nn", "mlp_down", epi=loss_epi, extras=(x2, target))
    loss = _rw(lambda d: ((), (jnp.sum(d * d, axis=0, keepdims=True),)), [dx3], [], [], [(1, D_MODEL)], "loss")[0]
    loss = jnp.sum(loss) * (0.5 * D_MODEL)

    g["w_down"] = _mm(r_up, dx3, "tn", "d_w_down")
    da = _mm(dx3, w["w_down"], "nt", "d_r", epi=lambda r, at: (r * 2.0 * jnp.maximum(at, 0.0),), extras=(a_up,),
             out_dtypes=(BF16,))
    g["w_up"] = _mm(h3, da, "tn", "d_w_up")
    dh3 = _mm(da, w["w_up"], "nt", "d_h3")
    dx2, g["g_mlp"] = _norm_bwd(x2, g_mlp, dh3, dx3, "d_norm_mlp")
    g["xa_w_o"] = _mm(o_xa, dx2, "tn", "d_xa_w_o")
    do_xa = _mm(dx2, w["xa_w_o"], "nt", "d_o_xa")
    dqx, dkn_x, dvv_x, g["xa_g_q"] = _xa_bwd(qx, xa_gq, kn_x, vv_x, do_xa, "xa_bwd")
    g["xa_w_kv"], g["g_mem"], g["xa_g_k"] = _mem_bwd(mem, g_mem, memn, w["xa_w_kv"], kv, xa_gk, dkn_x, dvv_x,
                                                     "xa_mem_bwd")
    g["xa_w_q"] = _mm(h2, dqx, "tn", "d_xa_w_q")
    dh2 = _mm(dqx, w["xa_w_q"], "nt", "d_h2")
    dx1, g["g_xa"] = _norm_bwd(x1, g_xa, dh2, dx2, "d_norm_xa")
    g["w_out"] = _mm(ycat, dx1, "tn", "d_w_out")
    dycat = _mm(dx1, w["w_out"], "nt", "d_ycat")

    def cat_bwd(a, b, dy, ga, gb):
        _, vjp = jax.vjp(cat_norm, a, b, ga, gb)
        da_, db_, dga, dgb = vjp(dy)
        return (da_, db_), (dga, dgb)

    dy_ssm, dy_sb, g["g_out_ssm"], g["g_out_sb"] = _rw(
        cat_bwd, [y_ssm, y_sb, dycat], [g_os, g_ob], [(SSM_WIDTH, F32), (SB_WIDTH, F32)], [g_os.shape, g_ob.shape],
        "d_norm_out")

    def glu_bwd(dy, yt, zt):
        sg = jax.nn.sigmoid(zt)
        return (dy * sg, dy * yt * sg * (1.0 - sg)), ()

    dy1_a, dz = _rw(glu_bwd, [dy_ssm, y1, z_glu], [], [(SSM_WIDTH, F32), (SSM_WIDTH, BF16)], [], "d_glu")
    g["ssm_w_glu"] = _mm(y1, dz, "tn", "d_w_glu")

    def gelu_bwd_epi(r, da_, y0t):
        _, vjp = jax.vjp(jax.nn.gelu, y0t)
        return (vjp(r + da_)[0],)

    dy0 = _mm(dz, w["ssm_w_glu"], "nt", "d_y1", epi=gelu_bwd_epi, extras=(dy1_a, y0))
    g["ssm_d"] = _rw(lambda d, ut: ((), (jnp.sum(d * ut, axis=0, keepdims=True),)), [dy0, u], [], [],
                     [(1, SSM_WIDTH)], "d_skip")[0]
    d_cmat = _mm(dy0, states, "tn", "d_cmat")
    gs = _mm(dy0, cmat, "nn", "d_states")
    lam, da8 = _scan_bwd(gs, states, acat, "ssm_scan_bwd")
    d_acat = jnp.sum(da8, axis=0, keepdims=True)
    d_bmat = _mm(u, lam, "tn", "d_bmat")
    du = _mm(lam, bmat, "nt", "d_u", epi=lambda r, d, dt: (r + d * dt,),
             extras=(dy0, jnp.broadcast_to(d_skip, u.shape)))
    for nm, val in zip(("ssm_a_re", "ssm_a_im", "ssm_log_dt", "ssm_b_re", "ssm_b_im", "ssm_c_re", "ssm_c_im"),
                       mats_vjp((d_acat, d_bmat, d_cmat))):
        g[nm] = val

    dq_h, dk_h, dv_h = _sb_bwd(qn, kn, v_h, y_sb_h, _heads(dy_sb, SB_HEADS), "sb_bwd")
    zeros_q = jnp.zeros_like(q_raw)
    dq_raw, g["sb_g_q"] = _norm_bwd(q_raw, sb_gq, dq_h.reshape(q_raw.shape), zeros_q, "d_sb_qnorm")
    dk_raw, g["sb_g_k"] = _norm_bwd(k_raw, sb_gk, dk_h.reshape(k_raw.shape), zeros_q, "d_sb_knorm")
    dproj = jnp.concatenate([du, _unheads(dq_raw.reshape(SB_HEADS, s, SB_HEAD_DIM)),
                             _unheads(dk_raw.reshape(SB_HEADS, s, SB_HEAD_DIM)), _unheads(dv_h)], axis=1)
    g["w_in"] = _mm(h1, dproj, "tn", "d_w_in")
    dh1 = _mm(dproj, w["w_in"], "nt", "d_h1")
    dx, g["g_mix"] = _norm_bwd(x, g_mix, dh1, dx1, "d_norm_mix")
    return loss, dx, g


def kernel(x, mem, g_mix, w_in, ssm_a_re, ssm_a_im, ssm_log_dt, ssm_b_re, ssm_b_im, ssm_c_re, ssm_c_im, ssm_d, ssm_w_glu, sb_g_q, sb_g_k, g_out_ssm, g_out_sb, w_out, g_xa, g_mem, xa_w_q, xa_w_kv, xa_g_q, xa_g_k, xa_w_o, g_mlp, w_up, w_down, loss_target, m_g_mix, m_w_in, m_ssm_a_re, m_ssm_a_im, m_ssm_log_dt, m_ssm_b_re, m_ssm_b_im, m_ssm_c_re, m_ssm_c_im, m_ssm_d, m_ssm_w_glu, m_sb_g_q, m_sb_g_k, m_g_out_ssm, m_g_out_sb, m_w_out, m_g_xa, m_g_mem, m_xa_w_q, m_xa_w_kv, m_xa_g_q, m_xa_g_k, m_xa_w_o, m_g_mlp, m_w_up, m_w_down, v_g_mix, v_w_in, v_ssm_a_re, v_ssm_a_im, v_ssm_log_dt, v_ssm_b_re, v_ssm_b_im, v_ssm_c_re, v_ssm_c_im, v_ssm_d, v_ssm_w_glu, v_sb_g_q, v_sb_g_k, v_g_out_ssm, v_g_out_sb, v_w_out, v_g_xa, v_g_mem, v_xa_w_q, v_xa_w_kv, v_xa_g_q, v_xa_g_k, v_xa_w_o, v_g_mlp, v_w_up, v_w_down):
    given = dict(locals())
    names = [n for n, _, _ in BIG] + SMALL
    order = ["g_mix", "w_in", "ssm_a_re", "ssm_a_im", "ssm_log_dt", "ssm_b_re", "ssm_b_im", "ssm_c_re", "ssm_c_im",
             "ssm_d", "ssm_w_glu", "sb_g_q", "sb_g_k", "g_out_ssm", "g_out_sb", "w_out", "g_xa", "g_mem", "xa_w_q",
             "xa_w_kv", "xa_g_q", "xa_g_k", "xa_w_o", "g_mlp", "w_up", "w_down"]
    assert sorted(names) == sorted(order)
    c = lax.axis_index("c")
    chip = 2 * lax.axis_index("x") + lax.axis_index("y")

    shard_shapes = [_shard_shape(shape, axis) for _, shape, axis in BIG]
    wpack = _pack([given[n][0].astype(BF16) for n, _, _ in BIG])
    r_big = wpack.shape[0]
    gathered = _all_gather(wpack, "gather_weights").reshape(N_DEV, r_big, PACK_COLS)
    w_full = {}
    off = 0
    for (n, shape, axis), shp in zip(BIG, shard_shapes):
        cnt = math.prod(shp)
        part = gathered.reshape(N_DEV, -1)[:, off:off + cnt].reshape((N_DEV,) + shp)
        w_full[n] = _full_from_shards(part, shape, axis)
        off += cnt + (-cnt) % PACK_ALIGN

    sm = {n: given[n][0] for n in SMALL}
    loss, dx, g = _local_step(x[0], mem[0], loss_target[0], w_full, sm)

    shards = [_shards_of_full(g[n], axis) for n, _, axis in BIG]
    per_dev = [jnp.concatenate([_pad_flat(sh[d]) for sh in shards]).reshape(r_big, PACK_COLS) for d in range(N_DEV)]
    gpack = jnp.stack([jnp.stack([per_dev[2 * k + cc] for k in range(4)]) for cc in range(2)])
    from_sibling = _swap_sibling(gpack.reshape(2, 4 * r_big, PACK_COLS), "reduce_sibling")
    kept = lax.dynamic_index_in_dim(gpack, c, 0, keepdims=False).reshape(4 * r_big, PACK_COLS)
    chip_sum = _rw(lambda a, b: ((a + b,), ()), [kept, from_sibling], [], [(PACK_COLS, F32)], [], "reduce_add",
                   tm=1024)[0].reshape(4, r_big, PACK_COLS)
    from_chips = _exchange_chips(chip_sum, "reduce_chips")
    own = lax.dynamic_index_in_dim(chip_sum, chip, 0, keepdims=False)
    wsh = _pack([given[n][0] for n, _, _ in BIG])
    msh = _pack([given["m_" + n][0] for n, _, _ in BIG])
    vsh = _pack([given["v_" + n][0] for n, _, _ in BIG])
    big_out = _adam([own, from_chips[0], from_chips[1], from_chips[2]], wsh, msh, vsh, "adam_sharded")
    big_out = [_unpack(o, shard_shapes) for o in big_out]

    small_shapes = [sm[n].shape for n in SMALL] + [(1,)]
    spack = _pack([g[n].reshape(sm[n].shape) for n in SMALL] + [loss.reshape(1)])
    r_small = spack.shape[0]
    sg = _all_gather(spack, "gather_small").reshape(N_DEV, r_small, PACK_COLS)
    zpad = jnp.zeros((1,), F32)
    wsm = _pack([sm[n] for n in SMALL] + [zpad])
    msm = _pack([given["m_" + n][0] for n in SMALL] + [zpad])
    vsm = _pack([given["v_" + n][0] for n in SMALL] + [zpad + 1.0])
    small_out = _adam([sg[d] for d in range(N_DEV)], wsm, msm, vsm, "adam_replicated")
    small_out = [_unpack(o, small_shapes) for o in small_out]

    res = {}
    for kind, idx in (("grad", 0), ("delta", 1), ("new_m", 2), ("new_v", 3)):
        for i, (n, _, _) in enumerate(BIG):
            res[kind + "_" + n] = big_out[idx][i][None]
        for i, n in enumerate(SMALL):
            res[kind + "_" + n] = small_out[idx][i][None]
    loss_out = small_out[0][len(SMALL)].reshape(())
    return (loss_out, dx[None], *[res["grad_" + n] for n in order], *[res["delta_" + n] for n in order],
            *[res["new_m_" + n] for n in order], *[res["new_v_" + n] for n in order])
```

```python
import functools
import math

import jax
import jax.numpy as jnp
from jax import lax
from jax.experimental import pallas as pl
from jax.experimental.pallas import tpu as pltpu

F32 = jnp.float32
BF16 = jnp.bfloat16
MESH = pl.DeviceIdType.MESH

N_DEV = 8
D_MODEL = 1024
SSM_WIDTH = 512
SSM_GROUP = 16
SSM_GROUPS = 32
SSM_STATE = 64
N_STATE = SSM_GROUPS * SSM_STATE
SB_HEADS = 8
SB_HEAD_DIM = 64
SB_WIDTH = 512
IN_WIDTH = 2048
XA_HEADS = 4
XA_HEAD_DIM = 128
XA_WIDTH = 512
D_FF = 4096
NORM_EPS = 1e-6
ADAM_LR = 0.001
ADAM_B1 = 0.9
ADAM_B2 = 0.999
ADAM_EPS = 1e-08
ADAM_WD = 0.01
ADAM_STEP = 10

LANES = 128
SUBLANES = 8
VMEM_LIMIT = 48 * 1024 * 1024
SCAN_LANES = 512
SB_BLOCK = 256

NN = (((1,), (0,)), ((), ()))
NT = (((1,), (1,)), ((), ()))
TN = (((0,), (0,)), ((), ()))


def _params(sem=None):
    return pltpu.CompilerParams(dimension_semantics=sem, vmem_limit_bytes=VMEM_LIMIT)


def _dot(a, b, dims=NN):
    return lax.dot_general(a.astype(BF16), b.astype(BF16), dims, preferred_element_type=F32)


def _rms(x, g):
    return x * lax.rsqrt(jnp.mean(x * x, axis=-1, keepdims=True) + NORM_EPS) * g


def _mm(a, b, mode, name, *, epi=None, extras=(), out_dtypes=(F32,), tm=512, tn=1024, tk=1024):
    if mode == "nn":
        (m, k), (k2, n) = a.shape, b.shape
    elif mode == "nt":
        (m, k), (n, k2) = a.shape, b.shape
    else:
        (k, m), (k2, n) = a.shape, b.shape
    assert k == k2, (name, a.shape, b.shape)
    tm, tn, tk = min(tm, m), min(tn, n), min(tk, k)
    assert m % tm == 0 and n % tn == 0 and k % tk == 0, (name, m, n, k)
    nk = k // tk
    dims = {"nn": NN, "nt": NT, "tn": TN}[mode]
    if mode == "tn":
        a_spec = pl.BlockSpec((tk, tm), lambda i, j, kk: (kk, i))
    else:
        a_spec = pl.BlockSpec((tm, tk), lambda i, j, kk: (i, kk))
    if mode == "nt":
        b_spec = pl.BlockSpec((tn, tk), lambda i, j, kk: (j, kk))
    else:
        b_spec = pl.BlockSpec((tk, tn), lambda i, j, kk: (kk, j))
    mn_spec = pl.BlockSpec((tm, tn), lambda i, j, kk: (i, j))
    n_ex, n_out = len(extras), len(out_dtypes)

    def body(*refs):
        a_ref, b_ref = refs[:2]
        ex = refs[2:2 + n_ex]
        outs = refs[2 + n_ex:2 + n_ex + n_out]
        acc = refs[-1]
        kk = pl.program_id(2)

        @pl.when(kk == 0)
        def _():
            acc[...] = jnp.zeros_like(acc)

        acc[...] += _dot(a_ref[...], b_ref[...], dims)

        @pl.when(kk == nk - 1)
        def _():
            r = acc[...]
            vals = epi(r, *[e[...] for e in ex]) if epi is not None else (r,)
            for o, v in zip(outs, vals):
                o[...] = v.astype(o.dtype)

    res = pl.pallas_call(
        body, name=name, grid=(m // tm, n // tn, nk),
        in_specs=[a_spec, b_spec] + [mn_spec] * n_ex,
        out_specs=[mn_spec] * n_out,
        out_shape=[jax.ShapeDtypeStruct((m, n), dt) for dt in out_dtypes],
        scratch_shapes=[pltpu.VMEM((tm, tn), F32)],
        compiler_params=_params(("parallel", "parallel", "arbitrary")),
    )(a, b, *extras)
    return res[0] if n_out == 1 else res


def _row_tile(s, target):
    if s <= target:
        return s
    return max(t for t in range(16, target + 1, 16) if s % t == 0)


def _rw(fn, rows, fulls, row_out, acc_out, name, tm=512):
    s = rows[0].shape[0]
    tm = _row_tile(s, tm)
    nr, nf, nro, nao = len(rows), len(fulls), len(row_out), len(acc_out)

    def body(*refs):
        r = refs[:nr]
        f = refs[nr:nr + nf]
        ro = refs[nr + nf:nr + nf + nro]
        ao = refs[nr + nf + nro:]
        outs, accs = fn(*[x[...] for x in r], *[x[...] for x in f])
        for o, v in zip(ro, outs):
            o[...] = v.astype(o.dtype)
        if nao:
            @pl.when(pl.program_id(0) == 0)
            def _():
                for a in ao:
                    a[...] = jnp.zeros_like(a)

            for a, v in zip(ao, accs):
                a[...] += v

    full_spec = lambda shape: pl.BlockSpec(shape, lambda i: (0,) * len(shape))
    res = pl.pallas_call(
        body, name=name, grid=(s // tm,),
        in_specs=[pl.BlockSpec((tm, x.shape[1]), lambda i: (i, 0)) for x in rows]
        + [full_spec(x.shape) for x in fulls],
        out_specs=[pl.BlockSpec((tm, d), lambda i: (i, 0)) for d, _ in row_out]
        + [full_spec(shape) for shape in acc_out],
        out_shape=[jax.ShapeDtypeStruct((s, d), dt) for d, dt in row_out]
        + [jax.ShapeDtypeStruct(shape, F32) for shape in acc_out],
        compiler_params=_params(("arbitrary",)),
    )(*rows, *fulls)
    return res


def _norm_fwd(x, g, name):
    return _rw(lambda xt, gt: ((_rms(xt, gt),), ()), [x], [g], [(x.shape[1], BF16)], [], name)[0]


def _norm_bwd(x, g, dh, dres, name):
    def fn(xt, dht, drt, gt):
        _, vjp = jax.vjp(_rms, xt, gt)
        dx, dg = vjp(dht)
        return (dx + drt,), (dg,)

    return _rw(fn, [x, dh, dres], [g], [(x.shape[1], F32)], [g.shape], name)


def _rms_groups(x, g, scale):
    lo = lax.broadcasted_iota(jnp.int32, (1, LANES), 1) < SB_HEAD_DIM
    x2 = x * x
    outs = []
    for cb in range(x.shape[1] // LANES):
        sl = slice(cb * LANES, (cb + 1) * LANES)
        s_lo = jnp.sum(jnp.where(lo, x2[:, sl], 0.0), axis=-1, keepdims=True)
        s_hi = jnp.sum(jnp.where(lo, 0.0, x2[:, sl]), axis=-1, keepdims=True)
        r = jnp.where(lo, lax.rsqrt(s_lo * (1.0 / SB_HEAD_DIM) + NORM_EPS),
                      lax.rsqrt(s_hi * (1.0 / SB_HEAD_DIM) + NORM_EPS))
        outs.append(x[:, sl] * r)
    return jnp.concatenate(outs, axis=1) * g * scale


def _log_sigmoid(z):
    return jnp.minimum(z, 0.0) - jnp.log(1.0 + jnp.exp(-jnp.abs(z)))


def _split_dot(x, u2):
    hi = x.astype(BF16)
    lo = (x - hi.astype(F32)).astype(BF16)
    return jnp.dot(jnp.concatenate([hi, lo], axis=1), u2, preferred_element_type=F32)


def _sb_consts(b):
    row = lax.broadcasted_iota(jnp.int32, (b, b), 0)
    col = lax.broadcasted_iota(jnp.int32, (b, b), 1)
    tri = col < row
    u_after = (row > col).astype(BF16)
    u_from = (row >= col).astype(BF16)
    stack = lambda u: jnp.concatenate([u, u], axis=0)
    lane_lo = lax.broadcasted_iota(jnp.int32, (b, LANES), 1) < SB_HEAD_DIM
    return tri, stack(u_after), stack(u_from), lane_lo


def _sb_scores(qh, kb, a_run, tri, u2_after, diag):
    z = lax.dot_general(qh, kb, NT, preferred_element_type=F32)
    lb = _log_sigmoid(z)
    l = lb - z
    if diag:
        l = jnp.where(tri, l, 0.0)
    w = jnp.exp(lb + (a_run + _split_dot(l, u2_after)))
    if diag:
        w = jnp.where(tri, w, 0.0)
    return lb, l, w


def _two_heads(x, lane_lo):
    zero = jnp.zeros_like(x)
    return jnp.where(lane_lo, x, zero), jnp.where(lane_lo, zero, x)


def _sb_fwd(qs, ks, v, name):
    s, width = qs.shape
    b = min(SB_BLOCK, s)

    def body(q_ref, k_ref, v_ref, o_ref):
        qi = pl.program_id(1)
        tri, u2_after, _, lane_lo = _sb_consts(b)
        q_a, q_b = _two_heads(q_ref[...], lane_lo)

        def step(n, carry, diag):
            a_a, a_b, acc = carry
            off = pl.multiple_of((qi - n) * b, b)
            kb = k_ref[pl.ds(off, b), :]
            v_a, v_b = _two_heads(v_ref[pl.ds(off, b), :], lane_lo)
            _, l_a, w_a = _sb_scores(q_a, kb, a_a, tri, u2_after, diag)
            _, l_b, w_b = _sb_scores(q_b, kb, a_b, tri, u2_after, diag)
            acc = acc + jnp.dot(jnp.concatenate([w_a.astype(BF16), w_b.astype(BF16)], axis=1),
                                jnp.concatenate([v_a, v_b], axis=0), preferred_element_type=F32)
            return (a_a + jnp.sum(l_a, axis=1, keepdims=True), a_b + jnp.sum(l_b, axis=1, keepdims=True), acc)

        zero = jnp.zeros((b, 1), F32)
        carry = step(0, (zero, zero, jnp.zeros((b, LANES), F32)), True)
        carry = lax.fori_loop(1, qi + 1, lambda n, c: step(n, c, False), carry)
        o_ref[...] = carry[2]

    blk = pl.BlockSpec((b, LANES), lambda hp, i: (i, hp))
    full = pl.BlockSpec((s, LANES), lambda hp, i: (0, hp))
    return pl.pallas_call(
        body, name=name, grid=(width // LANES, s // b),
        in_specs=[blk, full, full], out_specs=blk,
        out_shape=jax.ShapeDtypeStruct((s, width), F32),
        compiler_params=_params(("parallel", "arbitrary")),
    )(qs, ks, v)


def _sb_bwd(qs, ks, v, out, dout, name):
    s, width = qs.shape
    b = min(SB_BLOCK, s)
    nkb = s // b

    def body(q_ref, k_ref, v_ref, o_ref, do_ref, dq_ref, dkt_ref, dvt_ref):
        qi = pl.program_id(1)

        @pl.when(qi == 0)
        def _():
            dkt_ref[...] = jnp.zeros_like(dkt_ref)
            dvt_ref[...] = jnp.zeros_like(dvt_ref)

        tri, u2_after, u2_from, lane_lo = _sb_consts(b)
        q_a, q_b = _two_heads(q_ref[...], lane_lo)
        dob = do_ref[...].astype(BF16)
        do_a, do_b = _two_heads(dob, lane_lo)
        prod = dob.astype(F32) * o_ref[...]
        d_a = jnp.sum(jnp.where(lane_lo, prod, 0.0), axis=1, keepdims=True)
        d_b = jnp.sum(jnp.where(lane_lo, 0.0, prod), axis=1, keepdims=True)
        tr = lambda x: jnp.transpose(x.astype(F32)).astype(BF16)
        qt = jnp.concatenate([tr(q_a), tr(q_b)], axis=1)
        dot_ = jnp.concatenate([tr(do_a), tr(do_b)], axis=1)

        def head(qh, doh, kb, vb, a_run, d_rem, diag):
            lb, l, w = _sb_scores(qh, kb, a_run, tri, u2_after, diag)
            wb = w.astype(BF16)
            g = lax.dot_general(doh, vb, NT, preferred_element_type=F32) * wb.astype(F32)
            g_before = d_rem - _split_dot(g, u2_from)
            dz = g - (g + g_before) * jnp.exp(lb)
            if diag:
                dz = jnp.where(tri, dz, 0.0)
            return (dz.astype(BF16), wb, a_run + jnp.sum(l, axis=1, keepdims=True),
                    d_rem - jnp.sum(g, axis=1, keepdims=True))

        def step(n, carry, diag):
            a_a, a_b, r_a, r_b, dq = carry
            jb = qi - n
            off = pl.multiple_of(jb * b, b)
            kb = k_ref[pl.ds(off, b), :]
            vb = v_ref[pl.ds(off, b), :]
            k_a, k_b = _two_heads(kb, lane_lo)
            dz_a, w_a, a_a, r_a = head(q_a, do_a, kb, vb, a_a, r_a, diag)
            dz_b, w_b, a_b, r_b = head(q_b, do_b, kb, vb, a_b, r_b, diag)
            dq = dq + jnp.dot(jnp.concatenate([dz_a, dz_b], axis=1), jnp.concatenate([k_a, k_b], axis=0),
                              preferred_element_type=F32)
            dkt_ref[0, jb] += jnp.dot(qt, jnp.concatenate([dz_a, dz_b], axis=0), preferred_element_type=F32)
            dvt_ref[0, jb] += jnp.dot(dot_, jnp.concatenate([w_a, w_b], axis=0), preferred_element_type=F32)
            return a_a, a_b, r_a, r_b, dq

        zero = jnp.zeros((b, 1), F32)
        carry = step(0, (zero, zero, d_a, d_b, jnp.zeros((b, LANES), F32)), True)
        carry = lax.fori_loop(1, qi + 1, lambda n, c: step(n, c, False), carry)
        dq_ref[...] = carry[4]

    blk = pl.BlockSpec((b, LANES), lambda hp, i: (i, hp))
    full = pl.BlockSpec((s, LANES), lambda hp, i: (0, hp))
    acc = pl.BlockSpec((1, nkb, LANES, b), lambda hp, i: (hp, 0, 0, 0))
    return pl.pallas_call(
        body, name=name, grid=(width // LANES, nkb),
        in_specs=[blk, full, full, blk, blk], out_specs=[blk, acc, acc],
        out_shape=[jax.ShapeDtypeStruct((s, width), F32)]
        + [jax.ShapeDtypeStruct((width // LANES, nkb, LANES, b), F32)] * 2,
        compiler_params=_params(("parallel", "arbitrary")),
    )(qs, ks, v, out, dout)


def _from_key_blocks(t):
    hp, nkb, lanes, b = t.shape
    return jnp.transpose(t, (1, 3, 0, 2)).reshape(nkb * b, hp * lanes)


def _cmul(xr, xi, yr, yi):
    return xr * yr - xi * yi, xr * yi + xi * yr


def _scan_consts(ar, ai, reverse, lc):
    rowi = lax.broadcasted_iota(jnp.int32, (SUBLANES, lc), 0)
    pows = [(ar, ai)]
    for _ in range(SUBLANES - 1):
        pows.append(_cmul(*pows[-1], ar, ai))
    steps = []
    for d in (1, 2, 4):
        keep = (rowi < SUBLANES - d) if reverse else (rowi >= d)
        pr, pi = pows[d - 1]
        steps.append((SUBLANES - d if reverse else d, jnp.where(keep, pr, 0.0), jnp.where(keep, pi, 0.0)))
    cr = jnp.zeros((SUBLANES, lc), F32)
    ci = jnp.zeros((SUBLANES, lc), F32)
    for r in range(SUBLANES):
        pr, pi = pows[SUBLANES - 1 - r] if reverse else pows[r]
        cr = jnp.where(rowi == r, pr, cr)
        ci = jnp.where(rowi == r, pi, ci)
    return steps, cr, ci


def _scan_tile(xr, xi, steps, pr, pi, cr, ci):
    for shift, ar, ai in steps:
        rr = pltpu.roll(xr, shift, 0)
        ri = pltpu.roll(xi, shift, 0)
        xr, xi = xr + ar * rr - ai * ri, xi + ar * ri + ai * rr
    return xr + pr * cr - pi * ci, xi + pr * ci + pi * cr


def _scan_fwd(bu, acat, name, tt=1024):
    s, width = bu.shape
    lc = SCAN_LANES
    tt = min(tt, s)
    nl, nt = width // (2 * lc), s // tt

    def body(bu_ref, a_ref, s_ref, carry):
        @pl.when(pl.program_id(1) == 0)
        def _():
            carry[...] = jnp.zeros_like(carry)

        steps, pr, pi = _scan_consts(a_ref[:, :lc], a_ref[:, lc:], False, lc)

        def tile(i, c):
            off = pl.multiple_of(i * SUBLANES, SUBLANES)
            xr, xi = _scan_tile(bu_ref[pl.ds(off, SUBLANES), :lc], bu_ref[pl.ds(off, SUBLANES), lc:],
                                steps, pr, pi, c[0], c[1])
            s_ref[pl.ds(off, SUBLANES), :lc] = xr
            s_ref[pl.ds(off, SUBLANES), lc:] = xi
            return (jnp.broadcast_to(xr[SUBLANES - 1:, :], (SUBLANES, lc)),
                    jnp.broadcast_to(xi[SUBLANES - 1:, :], (SUBLANES, lc)))

        cr, ci = lax.fori_loop(0, tt // SUBLANES, tile, (carry[:, :lc], carry[:, lc:]))
        carry[:, :lc] = cr
        carry[:, lc:] = ci

    return pl.pallas_call(
        body, name=name, grid=(nl, nt),
        in_specs=[pl.BlockSpec((tt, 2 * lc), lambda j, c: (c, j)), pl.BlockSpec((1, 2 * lc), lambda j, c: (0, j))],
        out_specs=pl.BlockSpec((tt, 2 * lc), lambda j, c: (c, j)),
        out_shape=jax.ShapeDtypeStruct((s, width), F32),
        scratch_shapes=[pltpu.VMEM((SUBLANES, 2 * lc), F32)],
        compiler_params=_params(("parallel", "arbitrary")),
    )(bu, acat)


def _scan_bwd(gs, states, acat, name, tt=1024):
    s, width = gs.shape
    lc = SCAN_LANES
    tt = min(tt, s)
    nl, nt = width // (2 * lc), s // tt
    nt8 = tt // SUBLANES

    def body(gs_ref, s_ref, sp_ref, a_ref, lam_ref, da_ref, carry):
        c = pl.program_id(1)

        @pl.when(c == 0)
        def _():
            carry[...] = jnp.zeros_like(carry)
            da_ref[...] = jnp.zeros_like(da_ref)

        steps, pr, pi = _scan_consts(a_ref[:, :lc], -a_ref[:, lc:], True, lc)
        rowi = lax.broadcasted_iota(jnp.int32, (SUBLANES, lc), 0)
        first_chunk = c == nt - 1

        def tile(i, carry_v):
            cr, ci, dar, dai = carry_v
            t = nt8 - 1 - i
            off = pl.multiple_of(t * SUBLANES, SUBLANES)
            lr, li = _scan_tile(gs_ref[pl.ds(off, SUBLANES), :lc], gs_ref[pl.ds(off, SUBLANES), lc:],
                                steps, pr, pi, cr, ci)
            lam_ref[pl.ds(off, SUBLANES), :lc] = lr
            lam_ref[pl.ds(off, SUBLANES), lc:] = li
            offp = pl.multiple_of(jnp.maximum(t - 1, 0) * SUBLANES, SUBLANES)
            in_chunk = t > 0
            use = jnp.logical_or(in_chunk, jnp.logical_not(first_chunk))
            prev_r = jnp.where(in_chunk, s_ref[pl.ds(offp, SUBLANES), :lc], sp_ref[:, :lc])
            prev_i = jnp.where(in_chunk, s_ref[pl.ds(offp, SUBLANES), lc:], sp_ref[:, lc:])
            last_r = jnp.where(use, jnp.broadcast_to(prev_r[SUBLANES - 1:, :], (SUBLANES, lc)), 0.0)
            last_i = jnp.where(use, jnp.broadcast_to(prev_i[SUBLANES - 1:, :], (SUBLANES, lc)), 0.0)
            sr = jnp.where(rowi == 0, last_r, pltpu.roll(s_ref[pl.ds(off, SUBLANES), :lc], 1, 0))
            si = jnp.where(rowi == 0, last_i, pltpu.roll(s_ref[pl.ds(off, SUBLANES), lc:], 1, 0))
            dar = dar + lr * sr + li * si
            dai = dai + li * sr - lr * si
            return (jnp.broadcast_to(lr[:1, :], (SUBLANES, lc)), jnp.broadcast_to(li[:1, :], (SUBLANES, lc)),
                    dar, dai)

        zero = jnp.zeros((SUBLANES, lc), F32)
        cr, ci, dar, dai = lax.fori_loop(0, nt8, tile, (carry[:, :lc], carry[:, lc:], zero, zero))
        carry[:, :lc] = cr
        carry[:, lc:] = ci
        da_ref[:, :lc] += dar
        da_ref[:, lc:] += dai

    rev = lambda j, c: (nt - 1 - c, j)
    return pl.pallas_call(
        body, name=name, grid=(nl, nt),
        in_specs=[pl.BlockSpec((tt, 2 * lc), rev), pl.BlockSpec((tt, 2 * lc), rev),
                  pl.BlockSpec((SUBLANES, 2 * lc), lambda j, c: (jnp.maximum((nt - 1 - c) * nt8 - 1, 0), j)),
                  pl.BlockSpec((1, 2 * lc), lambda j, c: (0, j))],
        out_specs=[pl.BlockSpec((tt, 2 * lc), rev), pl.BlockSpec((SUBLANES, 2 * lc), lambda j, c: (0, j))],
        out_shape=[jax.ShapeDtypeStruct((s, width), F32), jax.ShapeDtypeStruct((SUBLANES, width), F32)],
        scratch_shapes=[pltpu.VMEM((SUBLANES, 2 * lc), F32)],
        compiler_params=_params(("parallel", "arbitrary")),
    )(gs, states, states, acat)


def _state_cols(x):
    xr, xi = x
    lead = xr.shape[:-1]
    nl = N_STATE // SCAN_LANES
    both = jnp.stack([xr.reshape(lead + (nl, SCAN_LANES)), xi.reshape(lead + (nl, SCAN_LANES))], axis=-2)
    return both.reshape(lead + (2 * N_STATE,))


def _ssm_mats(a_re, a_im, log_dt, b_re, b_im, c_re, c_im):
    dt = jnp.exp(log_dt)[:, None]
    lr, li = a_re * dt, a_im * dt
    e = jnp.exp(lr)
    abar_r, abar_i = e * jnp.cos(li), e * jnp.sin(li)
    den = a_re * a_re + a_im * a_im
    coef_r = ((abar_r - 1.0) * a_re + abar_i * a_im) / den
    coef_i = (abar_i * a_re - (abar_r - 1.0) * a_im) / den
    bbar_r = coef_r[..., None] * b_re - coef_i[..., None] * b_im
    bbar_i = coef_r[..., None] * b_im + coef_i[..., None] * b_re
    eye = jnp.eye(SSM_GROUPS, dtype=bool)[:, None, :, None]

    def in_mat(bb):
        t = jnp.transpose(bb, (0, 2, 1))[:, :, None, :]
        return jnp.where(eye, t, 0.0).reshape(SSM_WIDTH, N_STATE)

    def out_mat(cc):
        return jnp.where(eye, cc[:, :, None, :], 0.0).reshape(SSM_WIDTH, N_STATE)

    acat = _state_cols((abar_r.reshape(1, N_STATE), abar_i.reshape(1, N_STATE)))
    bmat = _state_cols((in_mat(bbar_r), in_mat(bbar_i)))
    cmat = _state_cols((out_mat(c_re), -out_mat(c_im)))
    return acat, bmat, cmat


def _mem_fwd(mem, g_mem, w_kv, g_k, name):
    ml = mem.shape[0]

    def body(mem_ref, gm_ref, w_ref, gk_ref, memn_ref, kv_ref, kn_ref, vv_ref):
        memn = _rms(mem_ref[...], gm_ref[...])
        memn_ref[...] = memn.astype(BF16)
        kv = _dot(memn, w_ref[...])
        kv_ref[...] = kv
        for hh in range(XA_HEADS):
            sl = slice(hh * XA_HEAD_DIM, (hh + 1) * XA_HEAD_DIM)
            kn_ref[:, sl] = _rms(kv[:, sl], gk_ref[...]).astype(BF16)
        vv_ref[...] = kv[:, XA_WIDTH:].astype(BF16)

    return pl.pallas_call(
        body, name=name,
        out_shape=[jax.ShapeDtypeStruct((ml, D_MODEL), BF16), jax.ShapeDtypeStruct((ml, 2 * XA_WIDTH), F32),
                   jax.ShapeDtypeStruct((ml, XA_WIDTH), BF16), jax.ShapeDtypeStruct((ml, XA_WIDTH), BF16)],
        compiler_params=_params(),
    )(mem, g_mem, w_kv, g_k)


def _mem_bwd(mem, g_mem, memn, w_kv, kv, g_k, dkn, dvv, name):
    def body(mem_ref, gm_ref, memn_ref, w_ref, kv_ref, gk_ref, dkn_ref, dvv_ref, dw_ref, dgm_ref, dgk_ref):
        kv = kv_ref[...]
        dgk = jnp.zeros(dgk_ref.shape, F32)
        parts = []
        for hh in range(XA_HEADS):
            sl = slice(hh * XA_HEAD_DIM, (hh + 1) * XA_HEAD_DIM)
            _, vjp = jax.vjp(_rms, kv[:, sl], gk_ref[...])
            dk, dg = vjp(dkn_ref[:, sl])
            parts.append(dk)
            dgk = dgk + dg
        dgk_ref[...] = dgk
        dkv = jnp.concatenate(parts + [dvv_ref[...]], axis=1)
        dw_ref[...] = _dot(memn_ref[...], dkv, TN)
        dmemn = _dot(dkv, w_ref[...], NT)
        _, vjp = jax.vjp(_rms, mem_ref[...], gm_ref[...])
        dgm_ref[...] = vjp(dmemn)[1]

    return pl.pallas_call(
        body, name=name,
        out_shape=[jax.ShapeDtypeStruct((D_MODEL, 2 * XA_WIDTH), F32), jax.ShapeDtypeStruct(g_mem.shape, F32),
                   jax.ShapeDtypeStruct(g_k.shape, F32)],
        compiler_params=_params(),
    )(mem, g_mem, memn, w_kv, kv, g_k, dkn, dvv)


def _xa_head(qx_h, g_q, kn_h, vv_h):
    qn = _rms(qx_h, g_q)
    sc = _dot(qn, kn_h, NT) * (XA_HEAD_DIM ** -0.5)
    sc = sc - jnp.max(sc, axis=-1, keepdims=True)
    e = jnp.exp(sc)
    p = e / jnp.sum(e, axis=-1, keepdims=True)
    return qn, p


def _xa_fwd(qx, g_q, kn, vv, name):
    def fn(qt, gq, knt, vvt):
        outs = []
        for hh in range(XA_HEADS):
            sl = slice(hh * XA_HEAD_DIM, (hh + 1) * XA_HEAD_DIM)
            _, p = _xa_head(qt[:, sl], gq, knt[:, sl], vvt[:, sl])
            outs.append(_dot(p, vvt[:, sl]))
        return (jnp.concatenate(outs, axis=1),), ()

    return _rw(fn, [qx], [g_q, kn, vv], [(XA_WIDTH, BF16)], [], name)[0]


def _xa_bwd(qx, g_q, kn, vv, do, name):
    def fn(qt, dot_, gq, knt, vvt):
        dqs, dks, dvs = [], [], []
        dgq = jnp.zeros_like(gq)
        for hh in range(XA_HEADS):
            sl = slice(hh * XA_HEAD_DIM, (hh + 1) * XA_HEAD_DIM)
            qn, p = _xa_head(qt[:, sl], gq, knt[:, sl], vvt[:, sl])
            doh = dot_[:, sl]
            dp = _dot(doh, vvt[:, sl], NT)
            dvs.append(_dot(p, doh, TN))
            ds = p * (dp - jnp.sum(dp * p, axis=-1, keepdims=True)) * (XA_HEAD_DIM ** -0.5)
            dqn = _dot(ds, knt[:, sl])
            dks.append(_dot(ds, qn, TN))
            _, vjp = jax.vjp(_rms, qt[:, sl], gq)
            dq, dg = vjp(dqn)
            dqs.append(dq)
            dgq = dgq + dg
        return ((jnp.concatenate(dqs, axis=1),),
                (jnp.concatenate(dks, axis=1), jnp.concatenate(dvs, axis=1), dgq))

    return _rw(fn, [qx, do], [g_q, kn, vv], [(XA_WIDTH, F32)], [kn.shape, vv.shape, g_q.shape], name)


ANY = pl.BlockSpec(memory_space=pl.ANY)


def _all_gather(block, name):
    m_per, n = block.shape

    def body(x_ref, out_ref, send_sems, recv_sems, local_sem):
        x, y, c = lax.axis_index("x"), lax.axis_index("y"), lax.axis_index("c")
        me, sibling = (x, y, c), (x, y, 1 - c)
        chips = [(1 - x, y), (x, 1 - y), (1 - x, 1 - y)]

        def rows(px, py, pc):
            return out_ref.at[pl.ds((4 * px + 2 * py + pc) * m_per, m_per), :]

        def copy(k, blk, to, src=None):
            return pltpu.make_async_remote_copy(
                src_ref=rows(*blk) if src is None else src, dst_ref=rows(*blk),
                send_sem=send_sems.at[k], recv_sem=recv_sems.at[k], device_id=to, device_id_type=MESH)

        mine = pltpu.make_async_copy(x_ref, rows(*me), local_sem)
        mine.start()
        first = [copy(0, me, sibling, src=x_ref)]
        first += [copy(1 + j, me, (*chip, c), src=x_ref) for j, chip in enumerate(chips)]
        for cp in first:
            cp.start()
        passed = [copy(4 + j, (*chip, c), sibling) for j, chip in enumerate(chips)]
        for j, chip in enumerate(chips):
            copy(1 + j, (*chip, c), me).wait_recv()
            passed[j].start()
        copy(0, sibling, me).wait_recv()
        for j, chip in enumerate(chips):
            copy(4 + j, (*chip, 1 - c), me).wait_recv()
        for cp in first + passed:
            cp.wait_send()
        mine.wait()

    return pl.pallas_call(
        body, name=name, in_specs=[ANY], out_specs=ANY,
        out_shape=jax.ShapeDtypeStruct((N_DEV * m_per, n), block.dtype),
        scratch_shapes=[pltpu.SemaphoreType.DMA((7,)), pltpu.SemaphoreType.DMA((7,)), pltpu.SemaphoreType.DMA],
    )(block)


def _swap_sibling(g, name):
    _, r, n = g.shape

    def body(g_ref, out_ref, send_sem, recv_sem):
        x, y, c = lax.axis_index("x"), lax.axis_index("y"), lax.axis_index("c")
        cp = pltpu.make_async_remote_copy(
            src_ref=g_ref.at[1 - c], dst_ref=out_ref, send_sem=send_sem, recv_sem=recv_sem,
            device_id=(x, y, 1 - c), device_id_type=MESH)
        cp.start()
        cp.wait()

    return pl.pallas_call(
        body, name=name, in_specs=[ANY], out_specs=ANY,
        out_shape=jax.ShapeDtypeStruct((r, n), g.dtype),
        scratch_shapes=[pltpu.SemaphoreType.DMA, pltpu.SemaphoreType.DMA],
    )(g)


def _exchange_chips(p, name):
    _, r, n = p.shape

    def body(p_ref, out_ref, send_sems, recv_sems):
        x, y, c = lax.axis_index("x"), lax.axis_index("y"), lax.axis_index("c")
        chips = [(1 - x, y), (x, 1 - y), (1 - x, 1 - y)]
        cps = [pltpu.make_async_remote_copy(
            src_ref=p_ref.at[2 * cx + cy], dst_ref=out_ref.at[j], send_sem=send_sems.at[j],
            recv_sem=recv_sems.at[j], device_id=(cx, cy, c), device_id_type=MESH)
            for j, (cx, cy) in enumerate(chips)]
        for cp in cps:
            cp.start()
        for cp in cps:
            cp.wait()

    return pl.pallas_call(
        body, name=name, in_specs=[ANY], out_specs=ANY,
        out_shape=jax.ShapeDtypeStruct((3, r, n), p.dtype),
        scratch_shapes=[pltpu.SemaphoreType.DMA((3,)), pltpu.SemaphoreType.DMA((3,))],
    )(p)


def _adam_math(w, g, m, v):
    m = ADAM_B1 * m + (1.0 - ADAM_B1) * g
    v = ADAM_B2 * v + (1.0 - ADAM_B2) * (g * g)
    m_hat = m / (1.0 - ADAM_B1 ** ADAM_STEP)
    v_hat = v / (1.0 - ADAM_B2 ** ADAM_STEP)
    delta = -ADAM_LR * (m_hat / (jnp.sqrt(v_hat) + ADAM_EPS) + ADAM_WD * w)
    return delta, m, v


def _adam(parts, w, m, v, name):
    def fn(*t):
        g = t[0]
        for p in t[1:len(parts)]:
            g = g + p
        wt, mt, vt = t[len(parts):]
        d, mn, vn = _adam_math(wt, g, mt, vt)
        return (g, d, mn, vn), ()

    n = w.shape[1]
    return _rw(fn, list(parts) + [w, m, v], [], [(n, F32)] * 4, [], name, tm=64)


BIG = [
    ("w_in", (D_MODEL, IN_WIDTH), 1), ("ssm_w_glu", (SSM_WIDTH, SSM_WIDTH), 0), ("w_out", (D_MODEL, D_MODEL), 0),
    ("xa_w_q", (D_MODEL, XA_WIDTH), 0), ("xa_w_kv", (D_MODEL, 2 * XA_WIDTH), 0), ("xa_w_o", (XA_WIDTH, D_MODEL), 1),
    ("w_up", (D_MODEL, D_FF), 1), ("w_down", (D_FF, D_MODEL), 0),
]
SMALL = ["g_mix", "ssm_a_re", "ssm_a_im", "ssm_log_dt", "ssm_b_re", "ssm_b_im", "ssm_c_re", "ssm_c_im", "ssm_d",
         "sb_g_q", "sb_g_k", "g_out_ssm", "g_out_sb", "g_xa", "g_mem", "xa_g_q", "xa_g_k", "g_mlp"]
PACK_COLS = 1024
PACK_ALIGN = 16 * PACK_COLS


def _shard_shape(shape, axis):
    return tuple(d // N_DEV if i == axis else d for i, d in enumerate(shape))


def _pad_flat(a):
    flat = a.reshape(-1)
    pad = (-flat.shape[0]) % PACK_ALIGN
    return jnp.pad(flat, (0, pad)) if pad else flat


def _pack(arrs):
    return jnp.concatenate([_pad_flat(a) for a in arrs]).reshape(-1, PACK_COLS)


def _unpack(packed, shapes):
    flat = packed.reshape(-1)
    out, off = [], 0
    for shp in shapes:
        n = math.prod(shp)
        out.append(flat[off:off + n].reshape(shp))
        off += n + (-n) % PACK_ALIGN
    return out


def _full_from_shards(g, shape, axis):
    if axis == 0:
        return g.reshape(shape)
    return jnp.transpose(g, (1, 0, 2)).reshape(shape)


def _shards_of_full(a, axis):
    if axis == 0:
        return a.reshape((N_DEV, a.shape[0] // N_DEV, a.shape[1]))
    return jnp.transpose(a.reshape(a.shape[0], N_DEV, a.shape[1] // N_DEV), (1, 0, 2))


def _local_step(x, mem, target, w, sm):
    s = x.shape[0]
    g = {}
    row = lambda a: a.reshape(1, -1)
    g_mix, g_xa, g_mlp, g_mem = row(sm["g_mix"]), row(sm["g_xa"]), row(sm["g_mlp"]), row(sm["g_mem"])
    g_os, g_ob = row(sm["g_out_ssm"]), row(sm["g_out_sb"])
    sb_gq, sb_gk = jnp.tile(row(sm["sb_g_q"]), (1, SB_HEADS)), jnp.tile(row(sm["sb_g_k"]), (1, SB_HEADS))
    xa_gq, xa_gk = row(sm["xa_g_q"]), row(sm["xa_g_k"])
    d_skip = row(sm["ssm_d"])

    h1 = _norm_fwd(x, g_mix, "norm_mix")
    proj = _mm(h1, w["w_in"], "nn", "in_proj")
    u = proj[:, :SSM_WIDTH]
    q_raw = proj[:, SSM_WIDTH:SSM_WIDTH + SB_WIDTH]
    k_raw = proj[:, SSM_WIDTH + SB_WIDTH:SSM_WIDTH + 2 * SB_WIDTH]
    v_sb = proj[:, SSM_WIDTH + 2 * SB_WIDTH:].astype(BF16)
    sb_scale = SB_HEAD_DIM ** -0.5
    qk_norm = lambda scale: (lambda xt, gt: ((_rms_groups(xt, gt, scale),), ()))
    qs = _rw(qk_norm(sb_scale), [q_raw], [sb_gq], [(SB_WIDTH, BF16)], [], "sb_qnorm")[0]
    ks = _rw(qk_norm(1.0), [k_raw], [sb_gk], [(SB_WIDTH, BF16)], [], "sb_knorm")[0]
    y_sb = _sb_fwd(qs, ks, v_sb, "sb_fwd")

    ssm_args = (sm["ssm_a_re"], sm["ssm_a_im"], sm["ssm_log_dt"], sm["ssm_b_re"], sm["ssm_b_im"],
                sm["ssm_c_re"], sm["ssm_c_im"])
    (acat, bmat, cmat), mats_vjp = jax.vjp(_ssm_mats, *ssm_args)
    bu = _mm(u, bmat, "nn", "ssm_bu")
    states = _scan_fwd(bu, acat, "ssm_scan")

    def gelu_epi(r, ut, dt):
        y0 = r + dt * ut
        y1 = jax.nn.gelu(y0)
        return y0, y1

    y0, y1 = _mm(states, cmat, "nt", "ssm_out", epi=gelu_epi, extras=(u, jnp.broadcast_to(d_skip, u.shape)),
                 out_dtypes=(F32, F32))
    z_glu, y_ssm = _mm(y1, w["ssm_w_glu"], "nn", "ssm_glu", epi=lambda r, yt: (r, yt * jax.nn.sigmoid(r)),
                       extras=(y1,), out_dtypes=(F32, F32))

    def cat_norm(a, b, ga, gb):
        return jnp.concatenate([_rms(a, ga), _rms(b, gb)], axis=1)

    ycat = _rw(lambda a, b, ga, gb: ((cat_norm(a, b, ga, gb),), ()), [y_ssm, y_sb], [g_os, g_ob],
               [(D_MODEL, BF16)], [], "norm_out")[0]
    x1 = _mm(ycat, w["w_out"], "nn", "out_proj", epi=lambda r, xt: (r + xt,), extras=(x,))
    h2 = _norm_fwd(x1, g_xa, "norm_xa")
    qx = _mm(h2, w["xa_w_q"], "nn", "xa_q")
    memn, kv, kn_x, vv_x = _mem_fwd(mem, g_mem, w["xa_w_kv"], xa_gk, "xa_mem")
    o_xa = _xa_fwd(qx, xa_gq, kn_x, vv_x, "xa_fwd")
    x2 = _mm(o_xa, w["xa_w_o"], "nn", "xa_o", epi=lambda r, xt: (r + xt,), extras=(x1,))
    h3 = _norm_fwd(x2, g_mlp, "norm_mlp")

    def up_epi(r):
        rl = jnp.maximum(r, 0.0)
        return r, rl * rl

    a_up, r_up = _mm(h3, w["w_up"], "nn", "mlp_up", epi=up_epi, out_dtypes=(F32, BF16))

    def loss_epi(r, xt, tt):
        d = r + xt - tt
        return (d * (1.0 / D_MODEL),)

    dx3 = _mm(r_up, w["w_down"], "nn", "mlp_down", epi=loss_epi, extras=(x2, target))
    loss = _rw(lambda d: ((), (jnp.sum(d * d, axis=0, keepdims=True),)), [dx3], [], [], [(1, D_MODEL)], "loss")[0]
    loss = jnp.sum(loss) * (0.5 * D_MODEL)

    g["w_down"] = _mm(r_up, dx3, "tn", "d_w_down")
    da = _mm(dx3, w["w_down"], "nt", "d_r", epi=lambda r, at: (r * 2.0 * jnp.maximum(at, 0.0),), extras=(a_up,),
             out_dtypes=(BF16,))
    g["w_up"] = _mm(h3, da, "tn", "d_w_up")
    dh3 = _mm(da, w["w_up"], "nt", "d_h3")
    dx2, g["g_mlp"] = _norm_bwd(x2, g_mlp, dh3, dx3, "d_norm_mlp")
    g["xa_w_o"] = _mm(o_xa, dx2, "tn", "d_xa_w_o")
    do_xa = _mm(dx2, w["xa_w_o"], "nt", "d_o_xa")
    dqx, dkn_x, dvv_x, g["xa_g_q"] = _xa_bwd(qx, xa_gq, kn_x, vv_x, do_xa, "xa_bwd")
    g["xa_w_kv"], g["g_mem"], g["xa_g_k"] = _mem_bwd(mem, g_mem, memn, w["xa_w_kv"], kv, xa_gk, dkn_x, dvv_x,
                                                     "xa_mem_bwd")
    g["xa_w_q"] = _mm(h2, dqx, "tn", "d_xa_w_q")
    dh2 = _mm(dqx, w["xa_w_q"], "nt", "d_h2")
    dx1, g["g_xa"] = _norm_bwd(x1, g_xa, dh2, dx2, "d_norm_xa")
    g["w_out"] = _mm(ycat, dx1, "tn", "d_w_out")
    dycat = _mm(dx1, w["w_out"], "nt", "d_ycat")

    def cat_bwd(a, b, dy, ga, gb):
        _, vjp = jax.vjp(cat_norm, a, b, ga, gb)
        da_, db_, dga, dgb = vjp(dy)
        return (da_, db_), (dga, dgb)

    dy_ssm, dy_sb, g["g_out_ssm"], g["g_out_sb"] = _rw(
        cat_bwd, [y_ssm, y_sb, dycat], [g_os, g_ob], [(SSM_WIDTH, F32), (SB_WIDTH, F32)], [g_os.shape, g_ob.shape],
        "d_norm_out")

    def glu_bwd(dy, yt, zt):
        sg = jax.nn.sigmoid(zt)
        return (dy * sg, dy * yt * sg * (1.0 - sg)), ()

    dy1_a, dz = _rw(glu_bwd, [dy_ssm, y1, z_glu], [], [(SSM_WIDTH, F32), (SSM_WIDTH, BF16)], [], "d_glu")
    g["ssm_w_glu"] = _mm(y1, dz, "tn", "d_w_glu")

    def gelu_bwd_epi(r, da_, y0t):
        _, vjp = jax.vjp(jax.nn.gelu, y0t)
        return (vjp(r + da_)[0],)

    dy0 = _mm(dz, w["ssm_w_glu"], "nt", "d_y1", epi=gelu_bwd_epi, extras=(dy1_a, y0))
    g["ssm_d"] = _rw(lambda d, ut: ((), (jnp.sum(d * ut, axis=0, keepdims=True),)), [dy0, u], [], [],
                     [(1, SSM_WIDTH)], "d_skip")[0]
    d_cmat = _mm(dy0, states, "tn", "d_cmat")
    gs = _mm(dy0, cmat, "nn", "d_states")
    lam, da8 = _scan_bwd(gs, states, acat, "ssm_scan_bwd")
    d_acat = jnp.sum(da8, axis=0, keepdims=True)
    d_bmat = _mm(u, lam, "tn", "d_bmat")
    du = _mm(lam, bmat, "nt", "d_u", epi=lambda r, d, dt: (r + d * dt,),
             extras=(dy0, jnp.broadcast_to(d_skip, u.shape)))
    for nm, val in zip(("ssm_a_re", "ssm_a_im", "ssm_log_dt", "ssm_b_re", "ssm_b_im", "ssm_c_re", "ssm_c_im"),
                       mats_vjp((d_acat, d_bmat, d_cmat))):
        g[nm] = val

    dqs, dkt, dvt = _sb_bwd(qs, ks, v_sb, y_sb, dy_sb, "sb_bwd")

    def qk_norm_bwd(scale):
        def fn(xt, dt, gt):
            _, vjp = jax.vjp(lambda a, b_: _rms_groups(a, b_, scale), xt, gt)
            dx_, dg_ = vjp(dt)
            return (dx_,), (dg_,)
        return fn

    dq_raw, dgq = _rw(qk_norm_bwd(sb_scale), [q_raw, dqs], [sb_gq], [(SB_WIDTH, F32)], [sb_gq.shape], "d_sb_qnorm")
    dk_raw, dgk = _rw(qk_norm_bwd(1.0), [k_raw, _from_key_blocks(dkt)], [sb_gk], [(SB_WIDTH, F32)], [sb_gk.shape],
                      "d_sb_knorm")
    g["sb_g_q"] = jnp.sum(dgq.reshape(SB_HEADS, SB_HEAD_DIM), axis=0)
    g["sb_g_k"] = jnp.sum(dgk.reshape(SB_HEADS, SB_HEAD_DIM), axis=0)
    dproj = jnp.concatenate([du, dq_raw, dk_raw, _from_key_blocks(dvt)], axis=1)
    g["w_in"] = _mm(h1, dproj, "tn", "d_w_in")
    dh1 = _mm(dproj, w["w_in"], "nt", "d_h1")
    dx, g["g_mix"] = _norm_bwd(x, g_mix, dh1, dx1, "d_norm_mix")
    return loss, dx, g


def kernel(x, mem, g_mix, w_in, ssm_a_re, ssm_a_im, ssm_log_dt, ssm_b_re, ssm_b_im, ssm_c_re, ssm_c_im, ssm_d, ssm_w_glu, sb_g_q, sb_g_k, g_out_ssm, g_out_sb, w_out, g_xa, g_mem, xa_w_q, xa_w_kv, xa_g_q, xa_g_k, xa_w_o, g_mlp, w_up, w_down, loss_target, m_g_mix, m_w_in, m_ssm_a_re, m_ssm_a_im, m_ssm_log_dt, m_ssm_b_re, m_ssm_b_im, m_ssm_c_re, m_ssm_c_im, m_ssm_d, m_ssm_w_glu, m_sb_g_q, m_sb_g_k, m_g_out_ssm, m_g_out_sb, m_w_out, m_g_xa, m_g_mem, m_xa_w_q, m_xa_w_kv, m_xa_g_q, m_xa_g_k, m_xa_w_o, m_g_mlp, m_w_up, m_w_down, v_g_mix, v_w_in, v_ssm_a_re, v_ssm_a_im, v_ssm_log_dt, v_ssm_b_re, v_ssm_b_im, v_ssm_c_re, v_ssm_c_im, v_ssm_d, v_ssm_w_glu, v_sb_g_q, v_sb_g_k, v_g_out_ssm, v_g_out_sb, v_w_out, v_g_xa, v_g_mem, v_xa_w_q, v_xa_w_kv, v_xa_g_q, v_xa_g_k, v_xa_w_o, v_g_mlp, v_w_up, v_w_down):
    given = dict(locals())
    names = [n for n, _, _ in BIG] + SMALL
    order = ["g_mix", "w_in", "ssm_a_re", "ssm_a_im", "ssm_log_dt", "ssm_b_re", "ssm_b_im", "ssm_c_re", "ssm_c_im",
             "ssm_d", "ssm_w_glu", "sb_g_q", "sb_g_k", "g_out_ssm", "g_out_sb", "w_out", "g_xa", "g_mem", "xa_w_q",
             "xa_w_kv", "xa_g_q", "xa_g_k", "xa_w_o", "g_mlp", "w_up", "w_down"]
    assert sorted(names) == sorted(order)
    c = lax.axis_index("c")
    chip = 2 * lax.axis_index("x") + lax.axis_index("y")

    shard_shapes = [_shard_shape(shape, axis) for _, shape, axis in BIG]
    wpack = _pack([given[n][0].astype(BF16) for n, _, _ in BIG])
    r_big = wpack.shape[0]
    gathered = _all_gather(wpack, "gather_weights").reshape(N_DEV, r_big, PACK_COLS)
    w_full = {}
    off = 0
    for (n, shape, axis), shp in zip(BIG, shard_shapes):
        cnt = math.prod(shp)
        part = gathered.reshape(N_DEV, -1)[:, off:off + cnt].reshape((N_DEV,) + shp)
        w_full[n] = _full_from_shards(part, shape, axis)
        off += cnt + (-cnt) % PACK_ALIGN

    sm = {n: given[n][0] for n in SMALL}
    loss, dx, g = _local_step(x[0], mem[0], loss_target[0], w_full, sm)

    shards = [_shards_of_full(g[n], axis) for n, _, axis in BIG]
    per_dev = [jnp.concatenate([_pad_flat(sh[d]) for sh in shards]).reshape(r_big, PACK_COLS) for d in range(N_DEV)]
    gpack = jnp.stack([jnp.stack([per_dev[2 * k + cc] for k in range(4)]) for cc in range(2)])
    from_sibling = _swap_sibling(gpack.reshape(2, 4 * r_big, PACK_COLS), "reduce_sibling")
    kept = lax.dynamic_index_in_dim(gpack, c, 0, keepdims=False).reshape(4 * r_big, PACK_COLS)
    chip_sum = _rw(lambda a, b: ((a + b,), ()), [kept, from_sibling], [], [(PACK_COLS, F32)], [], "reduce_add",
                   tm=1024)[0].reshape(4, r_big, PACK_COLS)
    from_chips = _exchange_chips(chip_sum, "reduce_chips")
    own = lax.dynamic_index_in_dim(chip_sum, chip, 0, keepdims=False)
    wsh = _pack([given[n][0] for n, _, _ in BIG])
    msh = _pack([given["m_" + n][0] for n, _, _ in BIG])
    vsh = _pack([given["v_" + n][0] for n, _, _ in BIG])
    big_out = _adam([own, from_chips[0], from_chips[1], from_chips[2]], wsh, msh, vsh, "adam_sharded")
    big_out = [_unpack(o, shard_shapes) for o in big_out]

    small_shapes = [sm[n].shape for n in SMALL] + [(1,)]
    spack = _pack([g[n].reshape(sm[n].shape) for n in SMALL] + [loss.reshape(1)])
    r_small = spack.shape[0]
    sg = _all_gather(spack, "gather_small").reshape(N_DEV, r_small, PACK_COLS)
    zpad = jnp.zeros((1,), F32)
    wsm = _pack([sm[n] for n in SMALL] + [zpad])
    msm = _pack([given["m_" + n][0] for n in SMALL] + [zpad])
    vsm = _pack([given["v_" + n][0] for n in SMALL] + [zpad + 1.0])
    small_out = _adam([sg[d] for d in range(N_DEV)], wsm, msm, vsm, "adam_replicated")
    small_out = [_unpack(o, small_shapes) for o in small_out]

    res = {}
    for kind, idx in (("grad", 0), ("delta", 1), ("new_m", 2), ("new_v", 3)):
        for i, (n, _, _) in enumerate(BIG):
            res[kind + "_" + n] = big_out[idx][i][None]
        for i, n in enumerate(SMALL):
            res[kind + "_" + n] = small_out[idx][i][None]
    loss_out = small_out[0][len(SMALL)].reshape(())
    return (loss_out, dx[None], *[res["grad_" + n] for n in order], *[res["delta_" + n] for n in order],
            *[res["new_m_" + n] for n in order], *[res["new_v_" + n] for n in order])
```

```python
import functools
import math

import jax
import jax.numpy as jnp
from jax import lax
from jax.experimental import pallas as pl
from jax.experimental.pallas import tpu as pltpu

F32 = jnp.float32
BF16 = jnp.bfloat16
MESH = pl.DeviceIdType.MESH

N_DEV = 8
D_MODEL = 1024
SSM_WIDTH = 512
SSM_GROUP = 16
SSM_GROUPS = 32
SSM_STATE = 64
N_STATE = SSM_GROUPS * SSM_STATE
SB_HEADS = 8
SB_HEAD_DIM = 64
SB_WIDTH = 512
IN_WIDTH = 2048
XA_HEADS = 4
XA_HEAD_DIM = 128
XA_WIDTH = 512
D_FF = 4096
NORM_EPS = 1e-6
ADAM_LR = 0.001
ADAM_B1 = 0.9
ADAM_B2 = 0.999
ADAM_EPS = 1e-08
ADAM_WD = 0.01
ADAM_STEP = 10

LANES = 128
SUBLANES = 8
VMEM_LIMIT = 48 * 1024 * 1024
SCAN_LANES = 512
SB_BLOCK = 256
SB_UNDERFLOW = -110.0

NN = (((1,), (0,)), ((), ()))
NT = (((1,), (1,)), ((), ()))
TN = (((0,), (0,)), ((), ()))


def _params(sem=None):
    return pltpu.CompilerParams(dimension_semantics=sem, vmem_limit_bytes=VMEM_LIMIT)


def _dot(a, b, dims=NN):
    return lax.dot_general(a.astype(BF16), b.astype(BF16), dims, preferred_element_type=F32)


def _rms(x, g):
    return x * lax.rsqrt(jnp.mean(x * x, axis=-1, keepdims=True) + NORM_EPS) * g


def _mm(a, b, mode, name, *, epi=None, extras=(), out_dtypes=(F32,), tm=512, tn=1024, tk=1024):
    if mode == "nn":
        (m, k), (k2, n) = a.shape, b.shape
    elif mode == "nt":
        (m, k), (n, k2) = a.shape, b.shape
    else:
        (k, m), (k2, n) = a.shape, b.shape
    assert k == k2, (name, a.shape, b.shape)
    tm, tn, tk = min(tm, m), min(tn, n), min(tk, k)
    assert m % tm == 0 and n % tn == 0 and k % tk == 0, (name, m, n, k)
    nk = k // tk
    dims = {"nn": NN, "nt": NT, "tn": TN}[mode]
    if mode == "tn":
        a_spec = pl.BlockSpec((tk, tm), lambda i, j, kk: (kk, i))
    else:
        a_spec = pl.BlockSpec((tm, tk), lambda i, j, kk: (i, kk))
    if mode == "nt":
        b_spec = pl.BlockSpec((tn, tk), lambda i, j, kk: (j, kk))
    else:
        b_spec = pl.BlockSpec((tk, tn), lambda i, j, kk: (kk, j))
    mn_spec = pl.BlockSpec((tm, tn), lambda i, j, kk: (i, j))
    n_ex, n_out = len(extras), len(out_dtypes)

    def body(*refs):
        a_ref, b_ref = refs[:2]
        ex = refs[2:2 + n_ex]
        outs = refs[2 + n_ex:2 + n_ex + n_out]
        acc = refs[-1]
        kk = pl.program_id(2)

        @pl.when(kk == 0)
        def _():
            acc[...] = jnp.zeros_like(acc)

        acc[...] += _dot(a_ref[...], b_ref[...], dims)

        @pl.when(kk == nk - 1)
        def _():
            r = acc[...]
            vals = epi(r, *[e[...] for e in ex]) if epi is not None else (r,)
            for o, v in zip(outs, vals):
                o[...] = v.astype(o.dtype)

    res = pl.pallas_call(
        body, name=name, grid=(m // tm, n // tn, nk),
        in_specs=[a_spec, b_spec] + [mn_spec] * n_ex,
        out_specs=[mn_spec] * n_out,
        out_shape=[jax.ShapeDtypeStruct((m, n), dt) for dt in out_dtypes],
        scratch_shapes=[pltpu.VMEM((tm, tn), F32)],
        compiler_params=_params(("parallel", "parallel", "arbitrary")),
    )(a, b, *extras)
    return res[0] if n_out == 1 else res


def _row_tile(s, target):
    if s <= target:
        return s
    return max(t for t in range(16, target + 1, 16) if s % t == 0)


def _rw(fn, rows, fulls, row_out, acc_out, name, tm=512):
    s = rows[0].shape[0]
    tm = _row_tile(s, tm)
    nr, nf, nro, nao = len(rows), len(fulls), len(row_out), len(acc_out)

    def body(*refs):
        r = refs[:nr]
        f = refs[nr:nr + nf]
        ro = refs[nr + nf:nr + nf + nro]
        ao = refs[nr + nf + nro:]
        outs, accs = fn(*[x[...] for x in r], *[x[...] for x in f])
        for o, v in zip(ro, outs):
            o[...] = v.astype(o.dtype)
        if nao:
            @pl.when(pl.program_id(0) == 0)
            def _():
                for a in ao:
                    a[...] = jnp.zeros_like(a)

            for a, v in zip(ao, accs):
                a[...] += v

    full_spec = lambda shape: pl.BlockSpec(shape, lambda i: (0,) * len(shape))
    res = pl.pallas_call(
        body, name=name, grid=(s // tm,),
        in_specs=[pl.BlockSpec((tm, x.shape[1]), lambda i: (i, 0)) for x in rows]
        + [full_spec(x.shape) for x in fulls],
        out_specs=[pl.BlockSpec((tm, d), lambda i: (i, 0)) for d, _ in row_out]
        + [full_spec(shape) for shape in acc_out],
        out_shape=[jax.ShapeDtypeStruct((s, d), dt) for d, dt in row_out]
        + [jax.ShapeDtypeStruct(shape, F32) for shape in acc_out],
        compiler_params=_params(("arbitrary",)),
    )(*rows, *fulls)
    return res


def _norm_fwd(x, g, name):
    return _rw(lambda xt, gt: ((_rms(xt, gt),), ()), [x], [g], [(x.shape[1], BF16)], [], name)[0]


def _norm_bwd(x, g, dh, dres, name):
    def fn(xt, dht, drt, gt):
        _, vjp = jax.vjp(_rms, xt, gt)
        dx, dg = vjp(dht)
        return (dx + drt,), (dg,)

    return _rw(fn, [x, dh, dres], [g], [(x.shape[1], F32)], [g.shape], name)


def _rms_groups(x, g, scale):
    lo = lax.broadcasted_iota(jnp.int32, (1, LANES), 1) < SB_HEAD_DIM
    x2 = x * x
    outs = []
    for cb in range(x.shape[1] // LANES):
        sl = slice(cb * LANES, (cb + 1) * LANES)
        s_lo = jnp.sum(jnp.where(lo, x2[:, sl], 0.0), axis=-1, keepdims=True)
        s_hi = jnp.sum(jnp.where(lo, 0.0, x2[:, sl]), axis=-1, keepdims=True)
        r = jnp.where(lo, lax.rsqrt(s_lo * (1.0 / SB_HEAD_DIM) + NORM_EPS),
                      lax.rsqrt(s_hi * (1.0 / SB_HEAD_DIM) + NORM_EPS))
        outs.append(x[:, sl] * r)
    return jnp.concatenate(outs, axis=1) * g * scale


def _log_sigmoid(z):
    return jnp.minimum(z, 0.0) - jnp.log(1.0 + jnp.exp(-jnp.abs(z)))


def _split_dot(x, u2):
    hi = x.astype(BF16)
    lo = (x - hi.astype(F32)).astype(BF16)
    return jnp.dot(jnp.concatenate([hi, lo], axis=1), u2, preferred_element_type=F32)


def _sb_consts(b):
    row = lax.broadcasted_iota(jnp.int32, (b, b), 0)
    col = lax.broadcasted_iota(jnp.int32, (b, b), 1)
    tri = col < row
    u_after = (row > col).astype(BF16)
    u_from = (row >= col).astype(BF16)
    stack = lambda u: jnp.concatenate([u, u], axis=0)
    lane_lo = lax.broadcasted_iota(jnp.int32, (b, LANES), 1) < SB_HEAD_DIM
    return tri, stack(u_after), stack(u_from), lane_lo


def _sb_scores(qh, kb, a_run, tri, u2_after, diag):
    z = lax.dot_general(qh, kb, NT, preferred_element_type=F32)
    lb = _log_sigmoid(z)
    l = lb - z
    if diag:
        l = jnp.where(tri, l, 0.0)
    w = jnp.exp(lb + (a_run + _split_dot(l, u2_after)))
    if diag:
        w = jnp.where(tri, w, 0.0)
    return lb, l, w


def _sb_walk(qi, carry, step):
    def cond(state):
        n, c = state
        return jnp.logical_and(n <= qi, jnp.max(jnp.maximum(c[0], c[1])) > SB_UNDERFLOW)

    def body(state):
        n, c = state
        return n + 1, step(n, c)

    return lax.while_loop(cond, body, (jnp.int32(1), carry))[1]


def _two_heads(x, lane_lo):
    zero = jnp.zeros_like(x)
    return jnp.where(lane_lo, x, zero), jnp.where(lane_lo, zero, x)


def _sb_fwd(qs, ks, v, name):
    s, width = qs.shape
    b = min(SB_BLOCK, s)

    def body(q_ref, k_ref, v_ref, o_ref):
        qi = pl.program_id(1)
        tri, u2_after, _, lane_lo = _sb_consts(b)
        q_a, q_b = _two_heads(q_ref[...], lane_lo)

        def step(n, carry, diag):
            a_a, a_b, acc = carry
            off = pl.multiple_of((qi - n) * b, b)
            kb = k_ref[pl.ds(off, b), :]
            v_a, v_b = _two_heads(v_ref[pl.ds(off, b), :], lane_lo)
            _, l_a, w_a = _sb_scores(q_a, kb, a_a, tri, u2_after, diag)
            _, l_b, w_b = _sb_scores(q_b, kb, a_b, tri, u2_after, diag)
            acc = acc + jnp.dot(jnp.concatenate([w_a.astype(BF16), w_b.astype(BF16)], axis=1),
                                jnp.concatenate([v_a, v_b], axis=0), preferred_element_type=F32)
            return (a_a + jnp.sum(l_a, axis=1, keepdims=True), a_b + jnp.sum(l_b, axis=1, keepdims=True), acc)

        zero = jnp.zeros((b, 1), F32)
        carry = step(0, (zero, zero, jnp.zeros((b, LANES), F32)), True)
        carry = _sb_walk(qi, carry, lambda n, c: step(n, c, False))
        o_ref[...] = carry[2]

    blk = pl.BlockSpec((b, LANES), lambda hp, i: (i, hp))
    full = pl.BlockSpec((s, LANES), lambda hp, i: (0, hp))
    return pl.pallas_call(
        body, name=name, grid=(width // LANES, s // b),
        in_specs=[blk, full, full], out_specs=blk,
        out_shape=jax.ShapeDtypeStruct((s, width), F32),
        compiler_params=_params(("parallel", "arbitrary")),
    )(qs, ks, v)


def _sb_bwd(qs, ks, v, out, dout, name):
    s, width = qs.shape
    b = min(SB_BLOCK, s)
    nkb = s // b

    def body(q_ref, k_ref, v_ref, o_ref, do_ref, dq_ref, dkt_ref, dvt_ref):
        qi = pl.program_id(1)

        @pl.when(qi == 0)
        def _():
            dkt_ref[...] = jnp.zeros_like(dkt_ref)
            dvt_ref[...] = jnp.zeros_like(dvt_ref)

        tri, u2_after, u2_from, lane_lo = _sb_consts(b)
        q_a, q_b = _two_heads(q_ref[...], lane_lo)
        dob = do_ref[...].astype(BF16)
        do_a, do_b = _two_heads(dob, lane_lo)
        prod = dob.astype(F32) * o_ref[...]
        d_a = jnp.sum(jnp.where(lane_lo, prod, 0.0), axis=1, keepdims=True)
        d_b = jnp.sum(jnp.where(lane_lo, 0.0, prod), axis=1, keepdims=True)
        tr = lambda x: jnp.transpose(x.astype(F32)).astype(BF16)
        qt = jnp.concatenate([tr(q_a), tr(q_b)], axis=1)
        dot_ = jnp.concatenate([tr(do_a), tr(do_b)], axis=1)

        def head(qh, doh, kb, vb, a_run, d_rem, diag):
            lb, l, w = _sb_scores(qh, kb, a_run, tri, u2_after, diag)
            wb = w.astype(BF16)
            g = lax.dot_general(doh, vb, NT, preferred_element_type=F32) * wb.astype(F32)
            g_before = d_rem - _split_dot(g, u2_from)
            dz = g - (g + g_before) * jnp.exp(lb)
            if diag:
                dz = jnp.where(tri, dz, 0.0)
            return (dz.astype(BF16), wb, a_run + jnp.sum(l, axis=1, keepdims=True),
                    d_rem - jnp.sum(g, axis=1, keepdims=True))

        def step(n, carry, diag):
            a_a, a_b, r_a, r_b, dq = carry
            jb = qi - n
            off = pl.multiple_of(jb * b, b)
            kb = k_ref[pl.ds(off, b), :]
            vb = v_ref[pl.ds(off, b), :]
            k_a, k_b = _two_heads(kb, lane_lo)
            dz_a, w_a, a_a, r_a = head(q_a, do_a, kb, vb, a_a, r_a, diag)
            dz_b, w_b, a_b, r_b = head(q_b, do_b, kb, vb, a_b, r_b, diag)
            dq = dq + jnp.dot(jnp.concatenate([dz_a, dz_b], axis=1), jnp.concatenate([k_a, k_b], axis=0),
                              preferred_element_type=F32)
            dkt_ref[0, jb] += jnp.dot(qt, jnp.concatenate([dz_a, dz_b], axis=0), preferred_element_type=F32)
            dvt_ref[0, jb] += jnp.dot(dot_, jnp.concatenate([w_a, w_b], axis=0), preferred_element_type=F32)
            return a_a, a_b, r_a, r_b, dq

        zero = jnp.zeros((b, 1), F32)
        carry = step(0, (zero, zero, d_a, d_b, jnp.zeros((b, LANES), F32)), True)
        carry = _sb_walk(qi, carry, lambda n, c: step(n, c, False))
        dq_ref[...] = carry[4]

    blk = pl.BlockSpec((b, LANES), lambda hp, i: (i, hp))
    full = pl.BlockSpec((s, LANES), lambda hp, i: (0, hp))
    acc = pl.BlockSpec((1, nkb, LANES, b), lambda hp, i: (hp, 0, 0, 0))
    return pl.pallas_call(
        body, name=name, grid=(width // LANES, nkb),
        in_specs=[blk, full, full, blk, blk], out_specs=[blk, acc, acc],
        out_shape=[jax.ShapeDtypeStruct((s, width), F32)]
        + [jax.ShapeDtypeStruct((width // LANES, nkb, LANES, b), F32)] * 2,
        compiler_params=_params(("parallel", "arbitrary")),
    )(qs, ks, v, out, dout)


def _from_key_blocks(t):
    hp, nkb, lanes, b = t.shape
    return jnp.transpose(t, (1, 3, 0, 2)).reshape(nkb * b, hp * lanes)


def _cmul(xr, xi, yr, yi):
    return xr * yr - xi * yi, xr * yi + xi * yr


def _scan_consts(ar, ai, reverse, lc):
    rowi = lax.broadcasted_iota(jnp.int32, (SUBLANES, lc), 0)
    pows = [(ar, ai)]
    for _ in range(SUBLANES - 1):
        pows.append(_cmul(*pows[-1], ar, ai))
    steps = []
    for d in (1, 2, 4):
        keep = (rowi < SUBLANES - d) if reverse else (rowi >= d)
        pr, pi = pows[d - 1]
        steps.append((SUBLANES - d if reverse else d, jnp.where(keep, pr, 0.0), jnp.where(keep, pi, 0.0)))
    cr = jnp.zeros((SUBLANES, lc), F32)
    ci = jnp.zeros((SUBLANES, lc), F32)
    for r in range(SUBLANES):
        pr, pi = pows[SUBLANES - 1 - r] if reverse else pows[r]
        cr = jnp.where(rowi == r, pr, cr)
        ci = jnp.where(rowi == r, pi, ci)
    return steps, cr, ci


def _scan_tile(xr, xi, steps, pr, pi, cr, ci):
    for shift, ar, ai in steps:
        rr = pltpu.roll(xr, shift, 0)
        ri = pltpu.roll(xi, shift, 0)
        xr, xi = xr + ar * rr - ai * ri, xi + ar * ri + ai * rr
    return xr + pr * cr - pi * ci, xi + pr * ci + pi * cr


def _scan_fwd(bu, acat, name, tt=1024):
    s, width = bu.shape
    lc = SCAN_LANES
    tt = min(tt, s)
    nl, nt = width // (2 * lc), s // tt

    def body(bu_ref, a_ref, s_ref, carry):
        @pl.when(pl.program_id(1) == 0)
        def _():
            carry[...] = jnp.zeros_like(carry)

        steps, pr, pi = _scan_consts(a_ref[:, :lc], a_ref[:, lc:], False, lc)

        def tile(i, c):
            off = pl.multiple_of(i * SUBLANES, SUBLANES)
            xr, xi = _scan_tile(bu_ref[pl.ds(off, SUBLANES), :lc], bu_ref[pl.ds(off, SUBLANES), lc:],
                                steps, pr, pi, c[0], c[1])
            s_ref[pl.ds(off, SUBLANES), :lc] = xr
            s_ref[pl.ds(off, SUBLANES), lc:] = xi
            return (jnp.broadcast_to(xr[SUBLANES - 1:, :], (SUBLANES, lc)),
                    jnp.broadcast_to(xi[SUBLANES - 1:, :], (SUBLANES, lc)))

        cr, ci = lax.fori_loop(0, tt // SUBLANES, tile, (carry[:, :lc], carry[:, lc:]))
        carry[:, :lc] = cr
        carry[:, lc:] = ci

    return pl.pallas_call(
        body, name=name, grid=(nl, nt),
        in_specs=[pl.BlockSpec((tt, 2 * lc), lambda j, c: (c, j)), pl.BlockSpec((1, 2 * lc), lambda j, c: (0, j))],
        out_specs=pl.BlockSpec((tt, 2 * lc), lambda j, c: (c, j)),
        out_shape=jax.ShapeDtypeStruct((s, width), F32),
        scratch_shapes=[pltpu.VMEM((SUBLANES, 2 * lc), F32)],
        compiler_params=_params(("parallel", "arbitrary")),
    )(bu, acat)


def _scan_bwd(gs, states, acat, name, tt=1024):
    s, width = gs.shape
    lc = SCAN_LANES
    tt = min(tt, s)
    nl, nt = width // (2 * lc), s // tt
    nt8 = tt // SUBLANES

    def body(gs_ref, s_ref, sp_ref, a_ref, lam_ref, da_ref, carry):
        c = pl.program_id(1)

        @pl.when(c == 0)
        def _():
            carry[...] = jnp.zeros_like(carry)
            da_ref[...] = jnp.zeros_like(da_ref)

        steps, pr, pi = _scan_consts(a_ref[:, :lc], -a_ref[:, lc:], True, lc)
        rowi = lax.broadcasted_iota(jnp.int32, (SUBLANES, lc), 0)
        first_chunk = c == nt - 1

        def tile(i, carry_v):
            cr, ci, dar, dai = carry_v
            t = nt8 - 1 - i
            off = pl.multiple_of(t * SUBLANES, SUBLANES)
            lr, li = _scan_tile(gs_ref[pl.ds(off, SUBLANES), :lc], gs_ref[pl.ds(off, SUBLANES), lc:],
                                steps, pr, pi, cr, ci)
            lam_ref[pl.ds(off, SUBLANES), :lc] = lr
            lam_ref[pl.ds(off, SUBLANES), lc:] = li
            offp = pl.multiple_of(jnp.maximum(t - 1, 0) * SUBLANES, SUBLANES)
            in_chunk = t > 0
            use = jnp.logical_or(in_chunk, jnp.logical_not(first_chunk))
            prev_r = jnp.where(in_chunk, s_ref[pl.ds(offp, SUBLANES), :lc], sp_ref[:, :lc])
            prev_i = jnp.where(in_chunk, s_ref[pl.ds(offp, SUBLANES), lc:], sp_ref[:, lc:])
            last_r = jnp.where(use, jnp.broadcast_to(prev_r[SUBLANES - 1:, :], (SUBLANES, lc)), 0.0)
            last_i = jnp.where(use, jnp.broadcast_to(prev_i[SUBLANES - 1:, :], (SUBLANES, lc)), 0.0)
            sr = jnp.where(rowi == 0, last_r, pltpu.roll(s_ref[pl.ds(off, SUBLANES), :lc], 1, 0))
            si = jnp.where(rowi == 0, last_i, pltpu.roll(s_ref[pl.ds(off, SUBLANES), lc:], 1, 0))
            dar = dar + lr * sr + li * si
            dai = dai + li * sr - lr * si
            return (jnp.broadcast_to(lr[:1, :], (SUBLANES, lc)), jnp.broadcast_to(li[:1, :], (SUBLANES, lc)),
                    dar, dai)

        zero = jnp.zeros((SUBLANES, lc), F32)
        cr, ci, dar, dai = lax.fori_loop(0, nt8, tile, (carry[:, :lc], carry[:, lc:], zero, zero))
        carry[:, :lc] = cr
        carry[:, lc:] = ci
        da_ref[:, :lc] += dar
        da_ref[:, lc:] += dai

    rev = lambda j, c: (nt - 1 - c, j)
    return pl.pallas_call(
        body, name=name, grid=(nl, nt),
        in_specs=[pl.BlockSpec((tt, 2 * lc), rev), pl.BlockSpec((tt, 2 * lc), rev),
                  pl.BlockSpec((SUBLANES, 2 * lc), lambda j, c: (jnp.maximum((nt - 1 - c) * nt8 - 1, 0), j)),
                  pl.BlockSpec((1, 2 * lc), lambda j, c: (0, j))],
        out_specs=[pl.BlockSpec((tt, 2 * lc), rev), pl.BlockSpec((SUBLANES, 2 * lc), lambda j, c: (0, j))],
        out_shape=[jax.ShapeDtypeStruct((s, width), F32), jax.ShapeDtypeStruct((SUBLANES, width), F32)],
        scratch_shapes=[pltpu.VMEM((SUBLANES, 2 * lc), F32)],
        compiler_params=_params(("parallel", "arbitrary")),
    )(gs, states, states, acat)


def _state_cols(x):
    xr, xi = x
    lead = xr.shape[:-1]
    nl = N_STATE // SCAN_LANES
    both = jnp.stack([xr.reshape(lead + (nl, SCAN_LANES)), xi.reshape(lead + (nl, SCAN_LANES))], axis=-2)
    return both.reshape(lead + (2 * N_STATE,))


def _ssm_mats(a_re, a_im, log_dt, b_re, b_im, c_re, c_im):
    dt = jnp.exp(log_dt)[:, None]
    lr, li = a_re * dt, a_im * dt
    e = jnp.exp(lr)
    abar_r, abar_i = e * jnp.cos(li), e * jnp.sin(li)
    den = a_re * a_re + a_im * a_im
    coef_r = ((abar_r - 1.0) * a_re + abar_i * a_im) / den
    coef_i = (abar_i * a_re - (abar_r - 1.0) * a_im) / den
    bbar_r = coef_r[..., None] * b_re - coef_i[..., None] * b_im
    bbar_i = coef_r[..., None] * b_im + coef_i[..., None] * b_re
    eye = jnp.eye(SSM_GROUPS, dtype=bool)[:, None, :, None]

    def in_mat(bb):
        t = jnp.transpose(bb, (0, 2, 1))[:, :, None, :]
        return jnp.where(eye, t, 0.0).reshape(SSM_WIDTH, N_STATE)

    def out_mat(cc):
        return jnp.where(eye, cc[:, :, None, :], 0.0).reshape(SSM_WIDTH, N_STATE)

    acat = _state_cols((abar_r.reshape(1, N_STATE), abar_i.reshape(1, N_STATE)))
    bmat = _state_cols((in_mat(bbar_r), in_mat(bbar_i)))
    cmat = _state_cols((out_mat(c_re), -out_mat(c_im)))
    return acat, bmat, cmat


def _mem_fwd(mem, g_mem, w_kv, g_k, name):
    ml = mem.shape[0]

    def body(mem_ref, gm_ref, w_ref, gk_ref, memn_ref, kv_ref, kn_ref, vv_ref):
        memn = _rms(mem_ref[...], gm_ref[...])
        memn_ref[...] = memn.astype(BF16)
        kv = _dot(memn, w_ref[...])
        kv_ref[...] = kv
        for hh in range(XA_HEADS):
            sl = slice(hh * XA_HEAD_DIM, (hh + 1) * XA_HEAD_DIM)
            kn_ref[:, sl] = _rms(kv[:, sl], gk_ref[...]).astype(BF16)
        vv_ref[...] = kv[:, XA_WIDTH:].astype(BF16)

    return pl.pallas_call(
        body, name=name,
        out_shape=[jax.ShapeDtypeStruct((ml, D_MODEL), BF16), jax.ShapeDtypeStruct((ml, 2 * XA_WIDTH), F32),
                   jax.ShapeDtypeStruct((ml, XA_WIDTH), BF16), jax.ShapeDtypeStruct((ml, XA_WIDTH), BF16)],
        compiler_params=_params(),
    )(mem, g_mem, w_kv, g_k)


def _mem_bwd(mem, g_mem, memn, w_kv, kv, g_k, dkn, dvv, name):
    def body(mem_ref, gm_ref, memn_ref, w_ref, kv_ref, gk_ref, dkn_ref, dvv_ref, dw_ref, dgm_ref, dgk_ref):
        kv = kv_ref[...]
        dgk = jnp.zeros(dgk_ref.shape, F32)
        parts = []
        for hh in range(XA_HEADS):
            sl = slice(hh * XA_HEAD_DIM, (hh + 1) * XA_HEAD_DIM)
            _, vjp = jax.vjp(_rms, kv[:, sl], gk_ref[...])
            dk, dg = vjp(dkn_ref[:, sl])
            parts.append(dk)
            dgk = dgk + dg
        dgk_ref[...] = dgk
        dkv = jnp.concatenate(parts + [dvv_ref[...]], axis=1)
        dw_ref[...] = _dot(memn_ref[...], dkv, TN)
        dmemn = _dot(dkv, w_ref[...], NT)
        _, vjp = jax.vjp(_rms, mem_ref[...], gm_ref[...])
        dgm_ref[...] = vjp(dmemn)[1]

    return pl.pallas_call(
        body, name=name,
        out_shape=[jax.ShapeDtypeStruct((D_MODEL, 2 * XA_WIDTH), F32), jax.ShapeDtypeStruct(g_mem.shape, F32),
                   jax.ShapeDtypeStruct(g_k.shape, F32)],
        compiler_params=_params(),
    )(mem, g_mem, memn, w_kv, kv, g_k, dkn, dvv)


def _xa_head(qx_h, g_q, kn_h, vv_h):
    qn = _rms(qx_h, g_q)
    sc = _dot(qn, kn_h, NT) * (XA_HEAD_DIM ** -0.5)
    sc = sc - jnp.max(sc, axis=-1, keepdims=True)
    e = jnp.exp(sc)
    p = e / jnp.sum(e, axis=-1, keepdims=True)
    return qn, p


def _xa_fwd(qx, g_q, kn, vv, name):
    def fn(qt, gq, knt, vvt):
        outs = []
        for hh in range(XA_HEADS):
            sl = slice(hh * XA_HEAD_DIM, (hh + 1) * XA_HEAD_DIM)
            _, p = _xa_head(qt[:, sl], gq, knt[:, sl], vvt[:, sl])
            outs.append(_dot(p, vvt[:, sl]))
        return (jnp.concatenate(outs, axis=1),), ()

    return _rw(fn, [qx], [g_q, kn, vv], [(XA_WIDTH, BF16)], [], name)[0]


def _xa_bwd(qx, g_q, kn, vv, do, name):
    def fn(qt, dot_, gq, knt, vvt):
        dqs, dks, dvs = [], [], []
        dgq = jnp.zeros_like(gq)
        for hh in range(XA_HEADS):
            sl = slice(hh * XA_HEAD_DIM, (hh + 1) * XA_HEAD_DIM)
            qn, p = _xa_head(qt[:, sl], gq, knt[:, sl], vvt[:, sl])
            doh = dot_[:, sl]
            dp = _dot(doh, vvt[:, sl], NT)
            dvs.append(_dot(p, doh, TN))
            ds = p * (dp - jnp.sum(dp * p, axis=-1, keepdims=True)) * (XA_HEAD_DIM ** -0.5)
            dqn = _dot(ds, knt[:, sl])
            dks.append(_dot(ds, qn, TN))
            _, vjp = jax.vjp(_rms, qt[:, sl], gq)
            dq, dg = vjp(dqn)
            dqs.append(dq)
            dgq = dgq + dg
        return ((jnp.concatenate(dqs, axis=1),),
                (jnp.concatenate(dks, axis=1), jnp.concatenate(dvs, axis=1), dgq))

    return _rw(fn, [qx, do], [g_q, kn, vv], [(XA_WIDTH, F32)], [kn.shape, vv.shape, g_q.shape], name)


ANY = pl.BlockSpec(memory_space=pl.ANY)


def _all_gather(block, name):
    m_per, n = block.shape

    def body(x_ref, out_ref, send_sems, recv_sems, local_sem):
        x, y, c = lax.axis_index("x"), lax.axis_index("y"), lax.axis_index("c")
        me, sibling = (x, y, c), (x, y, 1 - c)
        chips = [(1 - x, y), (x, 1 - y), (1 - x, 1 - y)]

        def rows(px, py, pc):
            return out_ref.at[pl.ds((4 * px + 2 * py + pc) * m_per, m_per), :]

        def copy(k, blk, to, src=None):
            return pltpu.make_async_remote_copy(
                src_ref=rows(*blk) if src is None else src, dst_ref=rows(*blk),
                send_sem=send_sems.at[k], recv_sem=recv_sems.at[k], device_id=to, device_id_type=MESH)

        mine = pltpu.make_async_copy(x_ref, rows(*me), local_sem)
        mine.start()
        first = [copy(0, me, sibling, src=x_ref)]
        first += [copy(1 + j, me, (*chip, c), src=x_ref) for j, chip in enumerate(chips)]
        for cp in first:
            cp.start()
        passed = [copy(4 + j, (*chip, c), sibling) for j, chip in enumerate(chips)]
        for j, chip in enumerate(chips):
            copy(1 + j, (*chip, c), me).wait_recv()
            passed[j].start()
        copy(0, sibling, me).wait_recv()
        for j, chip in enumerate(chips):
            copy(4 + j, (*chip, 1 - c), me).wait_recv()
        for cp in first + passed:
            cp.wait_send()
        mine.wait()

    return pl.pallas_call(
        body, name=name, in_specs=[ANY], out_specs=ANY,
        out_shape=jax.ShapeDtypeStruct((N_DEV * m_per, n), block.dtype),
        scratch_shapes=[pltpu.SemaphoreType.DMA((7,)), pltpu.SemaphoreType.DMA((7,)), pltpu.SemaphoreType.DMA],
    )(block)


def _swap_sibling(g, name):
    _, r, n = g.shape

    def body(g_ref, out_ref, send_sem, recv_sem):
        x, y, c = lax.axis_index("x"), lax.axis_index("y"), lax.axis_index("c")
        cp = pltpu.make_async_remote_copy(
            src_ref=g_ref.at[1 - c], dst_ref=out_ref, send_sem=send_sem, recv_sem=recv_sem,
            device_id=(x, y, 1 - c), device_id_type=MESH)
        cp.start()
        cp.wait()

    return pl.pallas_call(
        body, name=name, in_specs=[ANY], out_specs=ANY,
        out_shape=jax.ShapeDtypeStruct((r, n), g.dtype),
        scratch_shapes=[pltpu.SemaphoreType.DMA, pltpu.SemaphoreType.DMA],
    )(g)


def _exchange_chips(p, name):
    _, r, n = p.shape

    def body(p_ref, out_ref, send_sems, recv_sems):
        x, y, c = lax.axis_index("x"), lax.axis_index("y"), lax.axis_index("c")
        chips = [(1 - x, y), (x, 1 - y), (1 - x, 1 - y)]
        cps = [pltpu.make_async_remote_copy(
            src_ref=p_ref.at[2 * cx + cy], dst_ref=out_ref.at[j], send_sem=send_sems.at[j],
            recv_sem=recv_sems.at[j], device_id=(cx, cy, c), device_id_type=MESH)
            for j, (cx, cy) in enumerate(chips)]
        for cp in cps:
            cp.start()
        for cp in cps:
            cp.wait()

    return pl.pallas_call(
        body, name=name, in_specs=[ANY], out_specs=ANY,
        out_shape=jax.ShapeDtypeStruct((3, r, n), p.dtype),
        scratch_shapes=[pltpu.SemaphoreType.DMA((3,)), pltpu.SemaphoreType.DMA((3,))],
    )(p)


def _adam_math(w, g, m, v):
    m = ADAM_B1 * m + (1.0 - ADAM_B1) * g
    v = ADAM_B2 * v + (1.0 - ADAM_B2) * (g * g)
    m_hat = m / (1.0 - ADAM_B1 ** ADAM_STEP)
    v_hat = v / (1.0 - ADAM_B2 ** ADAM_STEP)
    delta = -ADAM_LR * (m_hat / (jnp.sqrt(v_hat) + ADAM_EPS) + ADAM_WD * w)
    return delta, m, v


def _adam(parts, w, m, v, name):
    def fn(*t):
        g = t[0]
        for p in t[1:len(parts)]:
            g = g + p
        wt, mt, vt = t[len(parts):]
        d, mn, vn = _adam_math(wt, g, mt, vt)
        return (g, d, mn, vn), ()

    n = w.shape[1]
    return _rw(fn, list(parts) + [w, m, v], [], [(n, F32)] * 4, [], name, tm=64)


BIG = [
    ("w_in", (D_MODEL, IN_WIDTH), 1), ("ssm_w_glu", (SSM_WIDTH, SSM_WIDTH), 0), ("w_out", (D_MODEL, D_MODEL), 0),
    ("xa_w_q", (D_MODEL, XA_WIDTH), 0), ("xa_w_kv", (D_MODEL, 2 * XA_WIDTH), 0), ("xa_w_o", (XA_WIDTH, D_MODEL), 1),
    ("w_up", (D_MODEL, D_FF), 1), ("w_down", (D_FF, D_MODEL), 0),
]
SMALL = ["g_mix", "ssm_a_re", "ssm_a_im", "ssm_log_dt", "ssm_b_re", "ssm_b_im", "ssm_c_re", "ssm_c_im", "ssm_d",
         "sb_g_q", "sb_g_k", "g_out_ssm", "g_out_sb", "g_xa", "g_mem", "xa_g_q", "xa_g_k", "g_mlp"]
PACK_COLS = 1024
PACK_ALIGN = 16 * PACK_COLS


def _shard_shape(shape, axis):
    return tuple(d // N_DEV if i == axis else d for i, d in enumerate(shape))


def _pad_flat(a):
    flat = a.reshape(-1)
    pad = (-flat.shape[0]) % PACK_ALIGN
    return jnp.pad(flat, (0, pad)) if pad else flat


def _pack(arrs):
    return jnp.concatenate([_pad_flat(a) for a in arrs]).reshape(-1, PACK_COLS)


def _unpack(packed, shapes):
    flat = packed.reshape(-1)
    out, off = [], 0
    for shp in shapes:
        n = math.prod(shp)
        out.append(flat[off:off + n].reshape(shp))
        off += n + (-n) % PACK_ALIGN
    return out


def _full_from_shards(g, shape, axis):
    if axis == 0:
        return g.reshape(shape)
    return jnp.transpose(g, (1, 0, 2)).reshape(shape)


def _shards_of_full(a, axis):
    if axis == 0:
        return a.reshape((N_DEV, a.shape[0] // N_DEV, a.shape[1]))
    return jnp.transpose(a.reshape(a.shape[0], N_DEV, a.shape[1] // N_DEV), (1, 0, 2))


def _local_step(x, mem, target, w, sm):
    s = x.shape[0]
    g = {}
    row = lambda a: a.reshape(1, -1)
    g_mix, g_xa, g_mlp, g_mem = row(sm["g_mix"]), row(sm["g_xa"]), row(sm["g_mlp"]), row(sm["g_mem"])
    g_os, g_ob = row(sm["g_out_ssm"]), row(sm["g_out_sb"])
    sb_gq, sb_gk = jnp.tile(row(sm["sb_g_q"]), (1, SB_HEADS)), jnp.tile(row(sm["sb_g_k"]), (1, SB_HEADS))
    xa_gq, xa_gk = row(sm["xa_g_q"]), row(sm["xa_g_k"])
    d_skip = row(sm["ssm_d"])

    h1 = _norm_fwd(x, g_mix, "norm_mix")
    proj = _mm(h1, w["w_in"], "nn", "in_proj")
    u = proj[:, :SSM_WIDTH]
    q_raw = proj[:, SSM_WIDTH:SSM_WIDTH + SB_WIDTH]
    k_raw = proj[:, SSM_WIDTH + SB_WIDTH:SSM_WIDTH + 2 * SB_WIDTH]
    v_sb = proj[:, SSM_WIDTH + 2 * SB_WIDTH:].astype(BF16)
    sb_scale = SB_HEAD_DIM ** -0.5
    qk_norm = lambda scale: (lambda xt, gt: ((_rms_groups(xt, gt, scale),), ()))
    qs = _rw(qk_norm(sb_scale), [q_raw], [sb_gq], [(SB_WIDTH, BF16)], [], "sb_qnorm")[0]
    ks = _rw(qk_norm(1.0), [k_raw], [sb_gk], [(SB_WIDTH, BF16)], [], "sb_knorm")[0]
    y_sb = _sb_fwd(qs, ks, v_sb, "sb_fwd")

    ssm_args = (sm["ssm_a_re"], sm["ssm_a_im"], sm["ssm_log_dt"], sm["ssm_b_re"], sm["ssm_b_im"],
                sm["ssm_c_re"], sm["ssm_c_im"])
    (acat, bmat, cmat), mats_vjp = jax.vjp(_ssm_mats, *ssm_args)
    bu = _mm(u, bmat, "nn", "ssm_bu")
    states = _scan_fwd(bu, acat, "ssm_scan")

    def gelu_epi(r, ut, dt):
        y0 = r + dt * ut
        y1 = jax.nn.gelu(y0)
        return y0, y1

    y0, y1 = _mm(states, cmat, "nt", "ssm_out", epi=gelu_epi, extras=(u, jnp.broadcast_to(d_skip, u.shape)),
                 out_dtypes=(F32, F32))
    z_glu, y_ssm = _mm(y1, w["ssm_w_glu"], "nn", "ssm_glu", epi=lambda r, yt: (r, yt * jax.nn.sigmoid(r)),
                       extras=(y1,), out_dtypes=(F32, F32))

    def cat_norm(a, b, ga, gb):
        return jnp.concatenate([_rms(a, ga), _rms(b, gb)], axis=1)

    ycat = _rw(lambda a, b, ga, gb: ((cat_norm(a, b, ga, gb),), ()), [y_ssm, y_sb], [g_os, g_ob],
               [(D_MODEL, BF16)], [], "norm_out")[0]
    x1 = _mm(ycat, w["w_out"], "nn", "out_proj", epi=lambda r, xt: (r + xt,), extras=(x,))
    h2 = _norm_fwd(x1, g_xa, "norm_xa")
    qx = _mm(h2, w["xa_w_q"], "nn", "xa_q")
    memn, kv, kn_x, vv_x = _mem_fwd(mem, g_mem, w["xa_w_kv"], xa_gk, "xa_mem")
    o_xa = _xa_fwd(qx, xa_gq, kn_x, vv_x, "xa_fwd")
    x2 = _mm(o_xa, w["xa_w_o"], "nn", "xa_o", epi=lambda r, xt: (r + xt,), extras=(x1,))
    h3 = _norm_fwd(x2, g_mlp, "norm_mlp")

    def up_epi(r):
        rl = jnp.maximum(r, 0.0)
        return r, rl * rl

    a_up, r_up = _mm(h3, w["w_up"], "nn", "mlp_up", epi=up_epi, out_dtypes=(F32, BF16))

    def loss_epi(r, xt, tt):
        d = r + xt - tt
        return (d * (1.0 / D_MODEL),)

    dx3 = _mm(r_up, w["w_down"], "nn", "mlp_down", epi=loss_epi, extras=(x2, target))
    loss = _rw(lambda d: ((), (jnp.sum(d * d, axis=0, keepdims=True),)), [dx3], [], [], [(1, D_MODEL)], "loss")[0]
    loss = jnp.sum(loss) * (0.5 * D_MODEL)

    g["w_down"] = _mm(r_up, dx3, "tn", "d_w_down")
    da = _mm(dx3, w["w_down"], "nt", "d_r", epi=lambda r, at: (r * 2.0 * jnp.maximum(at, 0.0),), extras=(a_up,),
             out_dtypes=(BF16,))
    g["w_up"] = _mm(h3, da, "tn", "d_w_up")
    dh3 = _mm(da, w["w_up"], "nt", "d_h3")
    dx2, g["g_mlp"] = _norm_bwd(x2, g_mlp, dh3, dx3, "d_norm_mlp")
    g["xa_w_o"] = _mm(o_xa, dx2, "tn", "d_xa_w_o")
    do_xa = _mm(dx2, w["xa_w_o"], "nt", "d_o_xa")
    dqx, dkn_x, dvv_x, g["xa_g_q"] = _xa_bwd(qx, xa_gq, kn_x, vv_x, do_xa, "xa_bwd")
    g["xa_w_kv"], g["g_mem"], g["xa_g_k"] = _mem_bwd(mem, g_mem, memn, w["xa_w_kv"], kv, xa_gk, dkn_x, dvv_x,
                                                     "xa_mem_bwd")
    g["xa_w_q"] = _mm(h2, dqx, "tn", "d_xa_w_q")
    dh2 = _mm(dqx, w["xa_w_q"], "nt", "d_h2")
    dx1, g["g_xa"] = _norm_bwd(x1, g_xa, dh2, dx2, "d_norm_xa")
    g["w_out"] = _mm(ycat, dx1, "tn", "d_w_out")
    dycat = _mm(dx1, w["w_out"], "nt", "d_ycat")

    def cat_bwd(a, b, dy, ga, gb):
        _, vjp = jax.vjp(cat_norm, a, b, ga, gb)
        da_, db_, dga, dgb = vjp(dy)
        return (da_, db_), (dga, dgb)

    dy_ssm, dy_sb, g["g_out_ssm"], g["g_out_sb"] = _rw(
        cat_bwd, [y_ssm, y_sb, dycat], [g_os, g_ob], [(SSM_WIDTH, F32), (SB_WIDTH, F32)], [g_os.shape, g_ob.shape],
        "d_norm_out")

    def glu_bwd(dy, yt, zt):
        sg = jax.nn.sigmoid(zt)
        return (dy * sg, dy * yt * sg * (1.0 - sg)), ()

    dy1_a, dz = _rw(glu_bwd, [dy_ssm, y1, z_glu], [], [(SSM_WIDTH, F32), (SSM_WIDTH, BF16)], [], "d_glu")
    g["ssm_w_glu"] = _mm(y1, dz, "tn", "d_w_glu")

    def gelu_bwd_epi(r, da_, y0t):
        _, vjp = jax.vjp(jax.nn.gelu, y0t)
        return (vjp(r + da_)[0],)

    dy0 = _mm(dz, w["ssm_w_glu"], "nt", "d_y1", epi=gelu_bwd_epi, extras=(dy1_a, y0))
    g["ssm_d"] = _rw(lambda d, ut: ((), (jnp.sum(d * ut, axis=0, keepdims=True),)), [dy0, u], [], [],
                     [(1, SSM_WIDTH)], "d_skip")[0]
    d_cmat = _mm(dy0, states, "tn", "d_cmat")
    gs = _mm(dy0, cmat, "nn", "d_states")
    lam, da8 = _scan_bwd(gs, states, acat, "ssm_scan_bwd")
    d_acat = jnp.sum(da8, axis=0, keepdims=True)
    d_bmat = _mm(u, lam, "tn", "d_bmat")
    du = _mm(lam, bmat, "nt", "d_u", epi=lambda r, d, dt: (r + d * dt,),
             extras=(dy0, jnp.broadcast_to(d_skip, u.shape)))
    for nm, val in zip(("ssm_a_re", "ssm_a_im", "ssm_log_dt", "ssm_b_re", "ssm_b_im", "ssm_c_re", "ssm_c_im"),
                       mats_vjp((d_acat, d_bmat, d_cmat))):
        g[nm] = val

    dqs, dkt, dvt = _sb_bwd(qs, ks, v_sb, y_sb, dy_sb, "sb_bwd")

    def qk_norm_bwd(scale):
        def fn(xt, dt, gt):
            _, vjp = jax.vjp(lambda a, b_: _rms_groups(a, b_, scale), xt, gt)
            dx_, dg_ = vjp(dt)
            return (dx_,), (dg_,)
        return fn

    dq_raw, dgq = _rw(qk_norm_bwd(sb_scale), [q_raw, dqs], [sb_gq], [(SB_WIDTH, F32)], [sb_gq.shape], "d_sb_qnorm")
    dk_raw, dgk = _rw(qk_norm_bwd(1.0), [k_raw, _from_key_blocks(dkt)], [sb_gk], [(SB_WIDTH, F32)], [sb_gk.shape],
                      "d_sb_knorm")
    g["sb_g_q"] = jnp.sum(dgq.reshape(SB_HEADS, SB_HEAD_DIM), axis=0)
    g["sb_g_k"] = jnp.sum(dgk.reshape(SB_HEADS, SB_HEAD_DIM), axis=0)
    dproj = jnp.concatenate([du, dq_raw, dk_raw, _from_key_blocks(dvt)], axis=1)
    g["w_in"] = _mm(h1, dproj, "tn", "d_w_in")
    dh1 = _mm(dproj, w["w_in"], "nt", "d_h1")
    dx, g["g_mix"] = _norm_bwd(x, g_mix, dh1, dx1, "d_norm_mix")
    return loss, dx, g


def kernel(x, mem, g_mix, w_in, ssm_a_re, ssm_a_im, ssm_log_dt, ssm_b_re, ssm_b_im, ssm_c_re, ssm_c_im, ssm_d, ssm_w_glu, sb_g_q, sb_g_k, g_out_ssm, g_out_sb, w_out, g_xa, g_mem, xa_w_q, xa_w_kv, xa_g_q, xa_g_k, xa_w_o, g_mlp, w_up, w_down, loss_target, m_g_mix, m_w_in, m_ssm_a_re, m_ssm_a_im, m_ssm_log_dt, m_ssm_b_re, m_ssm_b_im, m_ssm_c_re, m_ssm_c_im, m_ssm_d, m_ssm_w_glu, m_sb_g_q, m_sb_g_k, m_g_out_ssm, m_g_out_sb, m_w_out, m_g_xa, m_g_mem, m_xa_w_q, m_xa_w_kv, m_xa_g_q, m_xa_g_k, m_xa_w_o, m_g_mlp, m_w_up, m_w_down, v_g_mix, v_w_in, v_ssm_a_re, v_ssm_a_im, v_ssm_log_dt, v_ssm_b_re, v_ssm_b_im, v_ssm_c_re, v_ssm_c_im, v_ssm_d, v_ssm_w_glu, v_sb_g_q, v_sb_g_k, v_g_out_ssm, v_g_out_sb, v_w_out, v_g_xa, v_g_mem, v_xa_w_q, v_xa_w_kv, v_xa_g_q, v_xa_g_k, v_xa_w_o, v_g_mlp, v_w_up, v_w_down):
    given = dict(locals())
    names = [n for n, _, _ in BIG] + SMALL
    order = ["g_mix", "w_in", "ssm_a_re", "ssm_a_im", "ssm_log_dt", "ssm_b_re", "ssm_b_im", "ssm_c_re", "ssm_c_im",
             "ssm_d", "ssm_w_glu", "sb_g_q", "sb_g_k", "g_out_ssm", "g_out_sb", "w_out", "g_xa", "g_mem", "xa_w_q",
             "xa_w_kv", "xa_g_q", "xa_g_k", "xa_w_o", "g_mlp", "w_up", "w_down"]
    assert sorted(names) == sorted(order)
    c = lax.axis_index("c")
    chip = 2 * lax.axis_index("x") + lax.axis_index("y")

    shard_shapes = [_shard_shape(shape, axis) for _, shape, axis in BIG]
    wpack = _pack([given[n][0].astype(BF16) for n, _, _ in BIG])
    r_big = wpack.shape[0]
    gathered = _all_gather(wpack, "gather_weights").reshape(N_DEV, r_big, PACK_COLS)
    w_full = {}
    off = 0
    for (n, shape, axis), shp in zip(BIG, shard_shapes):
        cnt = math.prod(shp)
        part = gathered.reshape(N_DEV, -1)[:, off:off + cnt].reshape((N_DEV,) + shp)
        w_full[n] = _full_from_shards(part, shape, axis)
        off += cnt + (-cnt) % PACK_ALIGN

    sm = {n: given[n][0] for n in SMALL}
    loss, dx, g = _local_step(x[0], mem[0], loss_target[0], w_full, sm)

    shards = [_shards_of_full(g[n], axis) for n, _, axis in BIG]
    per_dev = [jnp.concatenate([_pad_flat(sh[d]) for sh in shards]).reshape(r_big, PACK_COLS) for d in range(N_DEV)]
    gpack = jnp.stack([jnp.stack([per_dev[2 * k + cc] for k in range(4)]) for cc in range(2)])
    from_sibling = _swap_sibling(gpack.reshape(2, 4 * r_big, PACK_COLS), "reduce_sibling")
    kept = lax.dynamic_index_in_dim(gpack, c, 0, keepdims=False).reshape(4 * r_big, PACK_COLS)
    chip_sum = _rw(lambda a, b: ((a + b,), ()), [kept, from_sibling], [], [(PACK_COLS, F32)], [], "reduce_add",
                   tm=1024)[0].reshape(4, r_big, PACK_COLS)
    from_chips = _exchange_chips(chip_sum, "reduce_chips")
    own = lax.dynamic_index_in_dim(chip_sum, chip, 0, keepdims=False)
    wsh = _pack([given[n][0] for n, _, _ in BIG])
    msh = _pack([given["m_" + n][0] for n, _, _ in BIG])
    vsh = _pack([given["v_" + n][0] for n, _, _ in BIG])
    big_out = _adam([own, from_chips[0], from_chips[1], from_chips[2]], wsh, msh, vsh, "adam_sharded")
    big_out = [_unpack(o, shard_shapes) for o in big_out]

    small_shapes = [sm[n].shape for n in SMALL] + [(1,)]
    spack = _pack([g[n].reshape(sm[n].shape) for n in SMALL] + [loss.reshape(1)])
    r_small = spack.shape[0]
    sg = _all_gather(spack, "gather_small").reshape(N_DEV, r_small, PACK_COLS)
    zpad = jnp.zeros((1,), F32)
    wsm = _pack([sm[n] for n in SMALL] + [zpad])
    msm = _pack([given["m_" + n][0] for n in SMALL] + [zpad])
    vsm = _pack([given["v_" + n][0] for n in SMALL] + [zpad + 1.0])
    small_out = _adam([sg[d] for d in range(N_DEV)], wsm, msm, vsm, "adam_replicated")
    small_out = [_unpack(o, small_shapes) for o in small_out]

    res = {}
    for kind, idx in (("grad", 0), ("delta", 1), ("new_m", 2), ("new_v", 3)):
        for i, (n, _, _) in enumerate(BIG):
            res[kind + "_" + n] = big_out[idx][i][None]
        for i, n in enumerate(SMALL):
            res[kind + "_" + n] = small_out[idx][i][None]
    loss_out = small_out[0][len(SMALL)].reshape(())
    return (loss_out, dx[None], *[res["grad_" + n] for n in order], *[res["delta_" + n] for n in order],
            *[res["new_m_" + n] for n in order], *[res["new_v_" + n] for n in order])
```

```python
import functools
import math

import jax
import jax.numpy as jnp
from jax import lax
from jax.experimental import pallas as pl
from jax.experimental.pallas import tpu as pltpu

F32 = jnp.float32
BF16 = jnp.bfloat16
MESH = pl.DeviceIdType.MESH

N_DEV = 8
D_MODEL = 1024
SSM_WIDTH = 512
SSM_GROUP = 16
SSM_GROUPS = 32
SSM_STATE = 64
N_STATE = SSM_GROUPS * SSM_STATE
SB_HEADS = 8
SB_HEAD_DIM = 64
SB_WIDTH = 512
IN_WIDTH = 2048
XA_HEADS = 4
XA_HEAD_DIM = 128
XA_WIDTH = 512
D_FF = 4096
NORM_EPS = 1e-6
ADAM_LR = 0.001
ADAM_B1 = 0.9
ADAM_B2 = 0.999
ADAM_EPS = 1e-08
ADAM_WD = 0.01
ADAM_STEP = 10

LANES = 128
SUBLANES = 8
VMEM_LIMIT = 48 * 1024 * 1024
SCAN_LANES = 512
SB_BLOCK = 256
SB_UNDERFLOW = -110.0

NN = (((1,), (0,)), ((), ()))
NT = (((1,), (1,)), ((), ()))
TN = (((0,), (0,)), ((), ()))


def _params(sem=None):
    return pltpu.CompilerParams(dimension_semantics=sem, vmem_limit_bytes=VMEM_LIMIT)


def _dot(a, b, dims=NN):
    return lax.dot_general(a.astype(BF16), b.astype(BF16), dims, preferred_element_type=F32)


def _rms(x, g):
    return x * lax.rsqrt(jnp.mean(x * x, axis=-1, keepdims=True) + NORM_EPS) * g


def _mm(a, b, mode, name, *, epi=None, extras=(), out_dtypes=(F32,), tm=1024, tn=1024, tk=1024):
    if mode == "nn":
        (m, k), (k2, n) = a.shape, b.shape
    elif mode == "nt":
        (m, k), (n, k2) = a.shape, b.shape
    else:
        (k, m), (k2, n) = a.shape, b.shape
    assert k == k2, (name, a.shape, b.shape)
    tm, tn, tk = min(tm, m), min(tn, n), min(tk, k)
    assert m % tm == 0 and n % tn == 0 and k % tk == 0, (name, m, n, k)
    nk = k // tk
    dims = {"nn": NN, "nt": NT, "tn": TN}[mode]
    if mode == "tn":
        a_spec = pl.BlockSpec((tk, tm), lambda i, j, kk: (kk, i))
    else:
        a_spec = pl.BlockSpec((tm, tk), lambda i, j, kk: (i, kk))
    if mode == "nt":
        b_spec = pl.BlockSpec((tn, tk), lambda i, j, kk: (j, kk))
    else:
        b_spec = pl.BlockSpec((tk, tn), lambda i, j, kk: (kk, j))
    mn_spec = pl.BlockSpec((tm, tn), lambda i, j, kk: (i, j))
    n_ex, n_out = len(extras), len(out_dtypes)

    def body(*refs):
        a_ref, b_ref = refs[:2]
        ex = refs[2:2 + n_ex]
        outs = refs[2 + n_ex:2 + n_ex + n_out]
        acc = refs[-1]
        kk = pl.program_id(2)

        @pl.when(kk == 0)
        def _():
            acc[...] = jnp.zeros_like(acc)

        acc[...] += _dot(a_ref[...], b_ref[...], dims)

        @pl.when(kk == nk - 1)
        def _():
            r = acc[...]
            vals = epi(r, *[e[...] for e in ex]) if epi is not None else (r,)
            for o, v in zip(outs, vals):
                o[...] = v.astype(o.dtype)

    res = pl.pallas_call(
        body, name=name, grid=(m // tm, n // tn, nk),
        in_specs=[a_spec, b_spec] + [mn_spec] * n_ex,
        out_specs=[mn_spec] * n_out,
        out_shape=[jax.ShapeDtypeStruct((m, n), dt) for dt in out_dtypes],
        scratch_shapes=[pltpu.VMEM((tm, tn), F32)],
        compiler_params=_params(("parallel", "parallel", "arbitrary")),
    )(a, b, *extras)
    return res[0] if n_out == 1 else res


def _row_tile(s, target):
    if s <= target:
        return s
    return max(t for t in range(16, target + 1, 16) if s % t == 0)


def _rw(fn, rows, fulls, row_out, acc_out, name, tm=512):
    cols = [r[1:] if isinstance(r, tuple) else (r.shape[1], 0) for r in rows]
    rows = [r[0] if isinstance(r, tuple) else r for r in rows]
    s = rows[0].shape[0]
    tm = _row_tile(s, tm)
    nr, nf, nro, nao = len(rows), len(fulls), len(row_out), len(acc_out)

    def body(*refs):
        r = refs[:nr]
        f = refs[nr:nr + nf]
        ro = refs[nr + nf:nr + nf + nro]
        ao = refs[nr + nf + nro:]
        outs, accs = fn(*[x[...] for x in r], *[x[...] for x in f])
        for o, v in zip(ro, outs):
            o[...] = v.astype(o.dtype)
        if nao:
            @pl.when(pl.program_id(0) == 0)
            def _():
                for a in ao:
                    a[...] = jnp.zeros_like(a)

            for a, v in zip(ao, accs):
                a[...] += v

    full_spec = lambda shape: pl.BlockSpec(shape, lambda i: (0,) * len(shape))
    res = pl.pallas_call(
        body, name=name, grid=(s // tm,),
        in_specs=[pl.BlockSpec((tm, wd), functools.partial(lambda i, cb: (i, cb), cb=cb)) for wd, cb in cols]
        + [full_spec(x.shape) for x in fulls],
        out_specs=[pl.BlockSpec((tm, d), lambda i: (i, 0)) for d, _ in row_out]
        + [full_spec(shape) for shape in acc_out],
        out_shape=[jax.ShapeDtypeStruct((s, d), dt) for d, dt in row_out]
        + [jax.ShapeDtypeStruct(shape, F32) for shape in acc_out],
        compiler_params=_params(("arbitrary",)),
    )(*rows, *fulls)
    return res


def _norm_fwd(x, g, name):
    return _rw(lambda xt, gt: ((_rms(xt, gt),), ()), [x], [g], [(x.shape[1], BF16)], [], name)[0]


def _norm_bwd(x, g, dh, dres, name):
    def fn(xt, dht, drt, gt):
        _, vjp = jax.vjp(_rms, xt, gt)
        dx, dg = vjp(dht)
        return (dx + drt,), (dg,)

    return _rw(fn, [x, dh, dres], [g], [(x.shape[1], F32)], [g.shape], name)


def _rms_groups(x, g, scale):
    lo = lax.broadcasted_iota(jnp.int32, (1, LANES), 1) < SB_HEAD_DIM
    x2 = x * x
    outs = []
    for cb in range(x.shape[1] // LANES):
        sl = slice(cb * LANES, (cb + 1) * LANES)
        s_lo = jnp.sum(jnp.where(lo, x2[:, sl], 0.0), axis=-1, keepdims=True)
        s_hi = jnp.sum(jnp.where(lo, 0.0, x2[:, sl]), axis=-1, keepdims=True)
        r = jnp.where(lo, lax.rsqrt(s_lo * (1.0 / SB_HEAD_DIM) + NORM_EPS),
                      lax.rsqrt(s_hi * (1.0 / SB_HEAD_DIM) + NORM_EPS))
        outs.append(x[:, sl] * r)
    return jnp.concatenate(outs, axis=1) * g * scale


def _log_sigmoid(z):
    return jnp.minimum(z, 0.0) - jnp.log(1.0 + jnp.exp(-jnp.abs(z)))


def _split_dot(x, u2):
    hi = x.astype(BF16)
    lo = (x - hi.astype(F32)).astype(BF16)
    return jnp.dot(jnp.concatenate([hi, lo], axis=1), u2, preferred_element_type=F32)


def _sb_consts(b):
    row = lax.broadcasted_iota(jnp.int32, (b, b), 0)
    col = lax.broadcasted_iota(jnp.int32, (b, b), 1)
    tri = col < row
    u_after = (row > col).astype(BF16)
    u_from = (row >= col).astype(BF16)
    stack = lambda u: jnp.concatenate([u, u], axis=0)
    lane_lo = lax.broadcasted_iota(jnp.int32, (b, LANES), 1) < SB_HEAD_DIM
    return tri, stack(u_after), stack(u_from), lane_lo


def _sb_scores(qh, kb, a_run, tri, u2_after, diag):
    z = lax.dot_general(qh, kb, NT, preferred_element_type=F32)
    lb = _log_sigmoid(z)
    l = lb - z
    if diag:
        l = jnp.where(tri, l, 0.0)
    w = jnp.exp(lb + (a_run + _split_dot(l, u2_after)))
    if diag:
        w = jnp.where(tri, w, 0.0)
    return lb, l, w


def _sb_walk(qi, carry, step):
    def cond(state):
        n, c = state
        return jnp.logical_and(n <= qi, jnp.max(jnp.maximum(c[0], c[1])) > SB_UNDERFLOW)

    def body(state):
        n, c = state
        return n + 1, step(n, c)

    return lax.while_loop(cond, body, (jnp.int32(1), carry))[1]


def _two_heads(x, lane_lo):
    zero = jnp.zeros_like(x)
    return jnp.where(lane_lo, x, zero), jnp.where(lane_lo, zero, x)


def _sb_fwd(qs, ks, v, name):
    s, width = qs.shape
    b = min(SB_BLOCK, s)

    def body(q_ref, k_ref, v_ref, o_ref):
        qi = pl.program_id(1)
        tri, u2_after, _, lane_lo = _sb_consts(b)
        q_a, q_b = _two_heads(q_ref[...], lane_lo)

        def step(n, carry, diag):
            a_a, a_b, acc = carry
            off = pl.multiple_of((qi - n) * b, b)
            kb = k_ref[pl.ds(off, b), :]
            v_a, v_b = _two_heads(v_ref[pl.ds(off, b), :], lane_lo)
            _, l_a, w_a = _sb_scores(q_a, kb, a_a, tri, u2_after, diag)
            _, l_b, w_b = _sb_scores(q_b, kb, a_b, tri, u2_after, diag)
            acc = acc + jnp.dot(jnp.concatenate([w_a.astype(BF16), w_b.astype(BF16)], axis=1),
                                jnp.concatenate([v_a, v_b], axis=0), preferred_element_type=F32)
            return (a_a + jnp.sum(l_a, axis=1, keepdims=True), a_b + jnp.sum(l_b, axis=1, keepdims=True), acc)

        zero = jnp.zeros((b, 1), F32)
        carry = step(0, (zero, zero, jnp.zeros((b, LANES), F32)), True)
        carry = _sb_walk(qi, carry, lambda n, c: step(n, c, False))
        o_ref[...] = carry[2]

    blk = pl.BlockSpec((b, LANES), lambda hp, i: (i, hp))
    full = pl.BlockSpec((s, LANES), lambda hp, i: (0, hp))
    return pl.pallas_call(
        body, name=name, grid=(width // LANES, s // b),
        in_specs=[blk, full, full], out_specs=blk,
        out_shape=jax.ShapeDtypeStruct((s, width), F32),
        compiler_params=_params(("parallel", "arbitrary")),
    )(qs, ks, v)


def _sb_bwd(qs, ks, v, out, dout, name):
    s, width = qs.shape
    b = min(SB_BLOCK, s)
    nkb = s // b

    def body(q_ref, k_ref, v_ref, o_ref, do_ref, dq_ref, dkt_ref, dvt_ref):
        qi = pl.program_id(1)

        @pl.when(qi == 0)
        def _():
            dkt_ref[...] = jnp.zeros_like(dkt_ref)
            dvt_ref[...] = jnp.zeros_like(dvt_ref)

        tri, u2_after, u2_from, lane_lo = _sb_consts(b)
        q_a, q_b = _two_heads(q_ref[...], lane_lo)
        dob = do_ref[...].astype(BF16)
        do_a, do_b = _two_heads(dob, lane_lo)
        prod = dob.astype(F32) * o_ref[...]
        d_a = jnp.sum(jnp.where(lane_lo, prod, 0.0), axis=1, keepdims=True)
        d_b = jnp.sum(jnp.where(lane_lo, 0.0, prod), axis=1, keepdims=True)
        tr = lambda x: jnp.transpose(x.astype(F32)).astype(BF16)
        qt = jnp.concatenate([tr(q_a), tr(q_b)], axis=1)
        dot_ = jnp.concatenate([tr(do_a), tr(do_b)], axis=1)

        def head(qh, doh, kb, vb, a_run, d_rem, diag):
            lb, l, w = _sb_scores(qh, kb, a_run, tri, u2_after, diag)
            wb = w.astype(BF16)
            g = lax.dot_general(doh, vb, NT, preferred_element_type=F32) * wb.astype(F32)
            g_before = d_rem - _split_dot(g, u2_from)
            dz = g - (g + g_before) * jnp.exp(lb)
            if diag:
                dz = jnp.where(tri, dz, 0.0)
            return (dz.astype(BF16), wb, a_run + jnp.sum(l, axis=1, keepdims=True),
                    d_rem - jnp.sum(g, axis=1, keepdims=True))

        def step(n, carry, diag):
            a_a, a_b, r_a, r_b, dq = carry
            jb = qi - n
            off = pl.multiple_of(jb * b, b)
            kb = k_ref[pl.ds(off, b), :]
            vb = v_ref[pl.ds(off, b), :]
            k_a, k_b = _two_heads(kb, lane_lo)
            dz_a, w_a, a_a, r_a = head(q_a, do_a, kb, vb, a_a, r_a, diag)
            dz_b, w_b, a_b, r_b = head(q_b, do_b, kb, vb, a_b, r_b, diag)
            dq = dq + jnp.dot(jnp.concatenate([dz_a, dz_b], axis=1), jnp.concatenate([k_a, k_b], axis=0),
                              preferred_element_type=F32)
            dkt_ref[0, jb] += jnp.dot(qt, jnp.concatenate([dz_a, dz_b], axis=0), preferred_element_type=F32)
            dvt_ref[0, jb] += jnp.dot(dot_, jnp.concatenate([w_a, w_b], axis=0), preferred_element_type=F32)
            return a_a, a_b, r_a, r_b, dq

        zero = jnp.zeros((b, 1), F32)
        carry = step(0, (zero, zero, d_a, d_b, jnp.zeros((b, LANES), F32)), True)
        carry = _sb_walk(qi, carry, lambda n, c: step(n, c, False))
        dq_ref[...] = carry[4]

    blk = pl.BlockSpec((b, LANES), lambda hp, i: (i, hp))
    full = pl.BlockSpec((s, LANES), lambda hp, i: (0, hp))
    acc = pl.BlockSpec((1, nkb, LANES, b), lambda hp, i: (hp, 0, 0, 0))
    return pl.pallas_call(
        body, name=name, grid=(width // LANES, nkb),
        in_specs=[blk, full, full, blk, blk], out_specs=[blk, acc, acc],
        out_shape=[jax.ShapeDtypeStruct((s, width), F32)]
        + [jax.ShapeDtypeStruct((width // LANES, nkb, LANES, b), F32)] * 2,
        compiler_params=_params(("parallel", "arbitrary")),
    )(qs, ks, v, out, dout)


def _from_key_blocks(t):
    hp, nkb, lanes, b = t.shape
    return jnp.transpose(t, (1, 3, 0, 2)).reshape(nkb * b, hp * lanes)


def _cmul(xr, xi, yr, yi):
    return xr * yr - xi * yi, xr * yi + xi * yr


def _scan_consts(ar, ai, reverse, lc):
    rowi = lax.broadcasted_iota(jnp.int32, (SUBLANES, lc), 0)
    pows = [(ar, ai)]
    for _ in range(SUBLANES - 1):
        pows.append(_cmul(*pows[-1], ar, ai))
    steps = []
    for d in (1, 2, 4):
        keep = (rowi < SUBLANES - d) if reverse else (rowi >= d)
        pr, pi = pows[d - 1]
        steps.append((SUBLANES - d if reverse else d, jnp.where(keep, pr, 0.0), jnp.where(keep, pi, 0.0)))
    cr = jnp.zeros((SUBLANES, lc), F32)
    ci = jnp.zeros((SUBLANES, lc), F32)
    for r in range(SUBLANES):
        pr, pi = pows[SUBLANES - 1 - r] if reverse else pows[r]
        cr = jnp.where(rowi == r, pr, cr)
        ci = jnp.where(rowi == r, pi, ci)
    return steps, cr, ci


def _scan_tile(xr, xi, steps, pr, pi, cr, ci):
    for shift, ar, ai in steps:
        rr = pltpu.roll(xr, shift, 0)
        ri = pltpu.roll(xi, shift, 0)
        xr, xi = xr + ar * rr - ai * ri, xi + ar * ri + ai * rr
    return xr + pr * cr - pi * ci, xi + pr * ci + pi * cr


def _ssm_fwd(u, acat, bsup, csup, d_skip, name, tt=1024):
    s = u.shape[0]
    lc = SCAN_LANES
    tt = min(tt, s)
    nl, nt = N_STATE // lc, s // tt

    def body(u_ref, a_ref, b_ref, c_ref, d_ref, s_ref, y0_ref, y1_ref, carry):
        @pl.when(pl.program_id(1) == 0)
        def _():
            carry[...] = jnp.zeros_like(carry)

        ut = u_ref[...]
        s_ref[...] = _dot(ut, b_ref[0])
        steps, pr, pi = _scan_consts(a_ref[:, :lc], a_ref[:, lc:], False, lc)

        def tile(i, c):
            off = pl.multiple_of(i * SUBLANES, SUBLANES)
            xr, xi = _scan_tile(s_ref[pl.ds(off, SUBLANES), :lc], s_ref[pl.ds(off, SUBLANES), lc:],
                                steps, pr, pi, c[0], c[1])
            s_ref[pl.ds(off, SUBLANES), :lc] = xr
            s_ref[pl.ds(off, SUBLANES), lc:] = xi
            return (jnp.broadcast_to(xr[SUBLANES - 1:, :], (SUBLANES, lc)),
                    jnp.broadcast_to(xi[SUBLANES - 1:, :], (SUBLANES, lc)))

        cr, ci = lax.fori_loop(0, tt // SUBLANES, tile, (carry[:, :lc], carry[:, lc:]))
        carry[:, :lc] = cr
        carry[:, lc:] = ci
        y0 = _dot(s_ref[...], c_ref[0], NT) + d_ref[...] * ut
        y0_ref[...] = y0
        y1_ref[...] = jax.nn.gelu(y0)

    chan = pl.BlockSpec((tt, LANES), lambda j, c: (c, j))
    sup = pl.BlockSpec((1, LANES, 2 * lc), lambda j, c: (j, 0, 0))
    return pl.pallas_call(
        body, name=name, grid=(nl, nt),
        in_specs=[chan, pl.BlockSpec((1, 2 * lc), lambda j, c: (0, j)), sup, sup,
                  pl.BlockSpec((1, LANES), lambda j, c: (0, j))],
        out_specs=[pl.BlockSpec((tt, 2 * lc), lambda j, c: (c, j)), chan, chan],
        out_shape=[jax.ShapeDtypeStruct((s, 2 * N_STATE), F32), jax.ShapeDtypeStruct((s, SSM_WIDTH), F32),
                   jax.ShapeDtypeStruct((s, SSM_WIDTH), F32)],
        scratch_shapes=[pltpu.VMEM((SUBLANES, 2 * lc), F32)],
        compiler_params=_params(("parallel", "arbitrary")),
    )(u, acat, bsup, csup, d_skip)


def _ssm_bwd(dy0, states, u, acat, bsup, csup, d_skip, name, tt=1024):
    s = u.shape[0]
    lc = SCAN_LANES
    tt = min(tt, s)
    nl, nt = N_STATE // lc, s // tt
    nt8 = tt // SUBLANES

    def body(dy_ref, s_ref, sp_ref, u_ref, a_ref, b_ref, c_ref, d_ref,
             du_ref, da_ref, db_ref, dc_ref, dd_ref, lam_ref, carry):
        c = pl.program_id(1)

        @pl.when(c == 0)
        def _():
            carry[...] = jnp.zeros_like(carry)
            for r in (da_ref, db_ref, dc_ref, dd_ref):
                r[...] = jnp.zeros_like(r)

        dy = dy_ref[...]
        ut = u_ref[...]
        lam_ref[...] = _dot(dy, c_ref[0])
        steps, pr, pi = _scan_consts(a_ref[:, :lc], -a_ref[:, lc:], True, lc)
        rowi = lax.broadcasted_iota(jnp.int32, (SUBLANES, lc), 0)
        first_chunk = c == nt - 1

        def tile(i, carry_v):
            cr, ci, dar, dai = carry_v
            t = nt8 - 1 - i
            off = pl.multiple_of(t * SUBLANES, SUBLANES)
            lr, li = _scan_tile(lam_ref[pl.ds(off, SUBLANES), :lc], lam_ref[pl.ds(off, SUBLANES), lc:],
                                steps, pr, pi, cr, ci)
            lam_ref[pl.ds(off, SUBLANES), :lc] = lr
            lam_ref[pl.ds(off, SUBLANES), lc:] = li
            offp = pl.multiple_of(jnp.maximum(t - 1, 0) * SUBLANES, SUBLANES)
            in_chunk = t > 0
            use = jnp.logical_or(in_chunk, jnp.logical_not(first_chunk))
            prev_r = jnp.where(in_chunk, s_ref[pl.ds(offp, SUBLANES), :lc], sp_ref[:, :lc])
            prev_i = jnp.where(in_chunk, s_ref[pl.ds(offp, SUBLANES), lc:], sp_ref[:, lc:])
            last_r = jnp.where(use, jnp.broadcast_to(prev_r[SUBLANES - 1:, :], (SUBLANES, lc)), 0.0)
            last_i = jnp.where(use, jnp.broadcast_to(prev_i[SUBLANES - 1:, :], (SUBLANES, lc)), 0.0)
            sr = jnp.where(rowi == 0, last_r, pltpu.roll(s_ref[pl.ds(off, SUBLANES), :lc], 1, 0))
            si = jnp.where(rowi == 0, last_i, pltpu.roll(s_ref[pl.ds(off, SUBLANES), lc:], 1, 0))
            dar = dar + lr * sr + li * si
            dai = dai + li * sr - lr * si
            return (jnp.broadcast_to(lr[:1, :], (SUBLANES, lc)), jnp.broadcast_to(li[:1, :], (SUBLANES, lc)),
                    dar, dai)

        zero = jnp.zeros((SUBLANES, lc), F32)
        cr, ci, dar, dai = lax.fori_loop(0, nt8, tile, (carry[:, :lc], carry[:, lc:], zero, zero))
        carry[:, :lc] = cr
        carry[:, lc:] = ci
        da_ref[:, :lc] += dar
        da_ref[:, lc:] += dai
        lam = lam_ref[...].astype(BF16)
        du_ref[...] = _dot(lam, b_ref[0], NT) + d_ref[...] * dy
        db_ref[0] += _dot(ut, lam, TN)
        dc_ref[0] += _dot(dy, s_ref[...], TN)
        dd_ref[...] += jnp.sum(dy * ut, axis=0, keepdims=True)

    rev = lambda j, c: (nt - 1 - c, j)
    chan = pl.BlockSpec((tt, LANES), rev)
    sup = pl.BlockSpec((1, LANES, 2 * lc), lambda j, c: (j, 0, 0))
    row = pl.BlockSpec((1, LANES), lambda j, c: (0, j))
    return pl.pallas_call(
        body, name=name, grid=(nl, nt),
        in_specs=[chan, pl.BlockSpec((tt, 2 * lc), rev),
                  pl.BlockSpec((SUBLANES, 2 * lc), lambda j, c: (jnp.maximum((nt - 1 - c) * nt8 - 1, 0), j)),
                  chan, pl.BlockSpec((1, 2 * lc), lambda j, c: (0, j)), sup, sup, row],
        out_specs=[chan, pl.BlockSpec((SUBLANES, 2 * lc), lambda j, c: (0, j)), sup, sup, row],
        out_shape=[jax.ShapeDtypeStruct((s, SSM_WIDTH), F32), jax.ShapeDtypeStruct((SUBLANES, 2 * N_STATE), F32),
                   jax.ShapeDtypeStruct(bsup.shape, F32), jax.ShapeDtypeStruct(csup.shape, F32),
                   jax.ShapeDtypeStruct((1, SSM_WIDTH), F32)],
        scratch_shapes=[pltpu.VMEM((tt, 2 * lc), F32), pltpu.VMEM((SUBLANES, 2 * lc), F32)],
        compiler_params=_params(("parallel", "arbitrary")),
    )(dy0, states, states, u, acat, bsup, csup, d_skip)


def _state_cols(xr, xi):
    lead = xr.shape[:-1]
    nl = N_STATE // SCAN_LANES
    both = jnp.stack([xr.reshape(lead + (nl, SCAN_LANES)), xi.reshape(lead + (nl, SCAN_LANES))], axis=-2)
    return both.reshape(lead + (2 * N_STATE,))


def _ssm_mats(a_re, a_im, log_dt, b_re, b_im, c_re, c_im):
    dt = jnp.exp(log_dt)[:, None]
    lr, li = a_re * dt, a_im * dt
    e = jnp.exp(lr)
    abar_r, abar_i = e * jnp.cos(li), e * jnp.sin(li)
    den = a_re * a_re + a_im * a_im
    coef_r = ((abar_r - 1.0) * a_re + abar_i * a_im) / den
    coef_i = (abar_i * a_re - (abar_r - 1.0) * a_im) / den
    bbar_r = coef_r[..., None] * b_re - coef_i[..., None] * b_im
    bbar_i = coef_r[..., None] * b_im + coef_i[..., None] * b_re
    nl = N_STATE // SCAN_LANES
    gpb = SSM_GROUPS // nl
    eye = jnp.eye(gpb, dtype=bool)[None, :, None, :, None]

    def sup(m_r, m_i):
        def one(m):
            m = m.reshape(nl, gpb, SSM_GROUP, 1, SSM_STATE)
            return jnp.where(eye, m, 0.0).reshape(nl, gpb * SSM_GROUP, SCAN_LANES)
        return jnp.concatenate([one(m_r), one(m_i)], axis=-1)

    acat = _state_cols(abar_r.reshape(1, N_STATE), abar_i.reshape(1, N_STATE))
    bsup = sup(jnp.transpose(bbar_r, (0, 2, 1)), jnp.transpose(bbar_i, (0, 2, 1)))
    csup = sup(c_re, -c_im)
    return acat, bsup, csup


def _mem_fwd(mem, g_mem, w_kv, g_k, name):
    ml = mem.shape[0]

    def body(mem_ref, gm_ref, w_ref, gk_ref, memn_ref, kv_ref, kn_ref, vv_ref):
        memn = _rms(mem_ref[...], gm_ref[...])
        memn_ref[...] = memn.astype(BF16)
        kv = _dot(memn, w_ref[...])
        kv_ref[...] = kv
        for hh in range(XA_HEADS):
            sl = slice(hh * XA_HEAD_DIM, (hh + 1) * XA_HEAD_DIM)
            kn_ref[:, sl] = _rms(kv[:, sl], gk_ref[...]).astype(BF16)
        vv_ref[...] = kv[:, XA_WIDTH:].astype(BF16)

    return pl.pallas_call(
        body, name=name,
        out_shape=[jax.ShapeDtypeStruct((ml, D_MODEL), BF16), jax.ShapeDtypeStruct((ml, 2 * XA_WIDTH), F32),
                   jax.ShapeDtypeStruct((ml, XA_WIDTH), BF16), jax.ShapeDtypeStruct((ml, XA_WIDTH), BF16)],
        compiler_params=_params(),
    )(mem, g_mem, w_kv, g_k)


def _mem_bwd(mem, g_mem, memn, w_kv, kv, g_k, dkn, dvv, name):
    def body(mem_ref, gm_ref, memn_ref, w_ref, kv_ref, gk_ref, dkn_ref, dvv_ref, dw_ref, dgm_ref, dgk_ref):
        kv = kv_ref[...]
        dgk = jnp.zeros(dgk_ref.shape, F32)
        parts = []
        for hh in range(XA_HEADS):
            sl = slice(hh * XA_HEAD_DIM, (hh + 1) * XA_HEAD_DIM)
            _, vjp = jax.vjp(_rms, kv[:, sl], gk_ref[...])
            dk, dg = vjp(dkn_ref[:, sl])
            parts.append(dk)
            dgk = dgk + dg
        dgk_ref[...] = dgk
        dkv = jnp.concatenate(parts + [dvv_ref[...]], axis=1)
        dw_ref[...] = _dot(memn_ref[...], dkv, TN)
        dmemn = _dot(dkv, w_ref[...], NT)
        _, vjp = jax.vjp(_rms, mem_ref[...], gm_ref[...])
        dgm_ref[...] = vjp(dmemn)[1]

    return pl.pallas_call(
        body, name=name,
        out_shape=[jax.ShapeDtypeStruct((D_MODEL, 2 * XA_WIDTH), F32), jax.ShapeDtypeStruct(g_mem.shape, F32),
                   jax.ShapeDtypeStruct(g_k.shape, F32)],
        compiler_params=_params(),
    )(mem, g_mem, memn, w_kv, kv, g_k, dkn, dvv)


def _xa_head(qx_h, g_q, kn_h, vv_h):
    qn = _rms(qx_h, g_q)
    sc = _dot(qn, kn_h, NT) * (XA_HEAD_DIM ** -0.5)
    sc = sc - jnp.max(sc, axis=-1, keepdims=True)
    e = jnp.exp(sc)
    p = e / jnp.sum(e, axis=-1, keepdims=True)
    return qn, p


def _xa_fwd(qx, g_q, kn, vv, name):
    def fn(qt, gq, knt, vvt):
        outs = []
        for hh in range(XA_HEADS):
            sl = slice(hh * XA_HEAD_DIM, (hh + 1) * XA_HEAD_DIM)
            _, p = _xa_head(qt[:, sl], gq, knt[:, sl], vvt[:, sl])
            outs.append(_dot(p, vvt[:, sl]))
        return (jnp.concatenate(outs, axis=1),), ()

    return _rw(fn, [qx], [g_q, kn, vv], [(XA_WIDTH, BF16)], [], name)[0]


def _xa_bwd(qx, g_q, kn, vv, do, name):
    def fn(qt, dot_, gq, knt, vvt):
        dqs, dks, dvs = [], [], []
        dgq = jnp.zeros_like(gq)
        for hh in range(XA_HEADS):
            sl = slice(hh * XA_HEAD_DIM, (hh + 1) * XA_HEAD_DIM)
            qn, p = _xa_head(qt[:, sl], gq, knt[:, sl], vvt[:, sl])
            doh = dot_[:, sl]
            dp = _dot(doh, vvt[:, sl], NT)
            dvs.append(_dot(p, doh, TN))
            ds = p * (dp - jnp.sum(dp * p, axis=-1, keepdims=True)) * (XA_HEAD_DIM ** -0.5)
            dqn = _dot(ds, knt[:, sl])
            dks.append(_dot(ds, qn, TN))
            _, vjp = jax.vjp(_rms, qt[:, sl], gq)
            dq, dg = vjp(dqn)
            dqs.append(dq)
            dgq = dgq + dg
        return ((jnp.concatenate(dqs, axis=1),),
                (jnp.concatenate(dks, axis=1), jnp.concatenate(dvs, axis=1), dgq))

    return _rw(fn, [qx, do], [g_q, kn, vv], [(XA_WIDTH, F32)], [kn.shape, vv.shape, g_q.shape], name)


ANY = pl.BlockSpec(memory_space=pl.ANY)
BIG = [
    ("w_in", (D_MODEL, IN_WIDTH), 1), ("ssm_w_glu", (SSM_WIDTH, SSM_WIDTH), 0), ("w_out", (D_MODEL, D_MODEL), 0),
    ("xa_w_q", (D_MODEL, XA_WIDTH), 0), ("xa_w_kv", (D_MODEL, 2 * XA_WIDTH), 0), ("xa_w_o", (XA_WIDTH, D_MODEL), 1),
    ("w_up", (D_MODEL, D_FF), 1), ("w_down", (D_FF, D_MODEL), 0),
]
N_BIG = len(BIG)


def _shard_shape(shape, axis):
    return tuple(d // N_DEV if i == axis else d for i, d in enumerate(shape))


def _shard_of(ref, axis, d):
    n = ref.shape[axis] // N_DEV
    return ref.at[pl.ds(d * n, n), :] if axis == 0 else ref.at[:, pl.ds(d * n, n)]


def _gather_weights(shards, name):
    def body(*refs):
        ins, outs = refs[:N_BIG], refs[N_BIG:2 * N_BIG]
        send_sems, recv_sems, local_sems = refs[2 * N_BIG:]
        x, y, c = lax.axis_index("x"), lax.axis_index("y"), lax.axis_index("c")
        me, sibling = (x, y, c), (x, y, 1 - c)
        chips = [(1 - x, y), (x, 1 - y), (1 - x, 1 - y)]

        def place(i, dev):
            return _shard_of(outs[i], BIG[i][2], 4 * dev[0] + 2 * dev[1] + dev[2])

        def copy(i, k, blk, to, src=None):
            return pltpu.make_async_remote_copy(
                src_ref=place(i, blk) if src is None else src, dst_ref=place(i, blk),
                send_sem=send_sems.at[7 * i + k], recv_sem=recv_sems.at[7 * i + k], device_id=to,
                device_id_type=MESH)

        started = []
        mine = [pltpu.make_async_copy(ins[i], place(i, me), local_sems.at[i]) for i in range(N_BIG)]
        for i in range(N_BIG):
            mine[i].start()
            started.append(copy(i, 0, me, sibling, src=ins[i]))
            started += [copy(i, 1 + j, me, (*chip, c), src=ins[i]) for j, chip in enumerate(chips)]
        for cp in started:
            cp.start()
        for j, chip in enumerate(chips):
            for i in range(N_BIG):
                copy(i, 1 + j, (*chip, c), me).wait_recv()
                passed = copy(i, 4 + j, (*chip, c), sibling)
                passed.start()
                started.append(passed)
        for i in range(N_BIG):
            copy(i, 0, sibling, me).wait_recv()
            for j, chip in enumerate(chips):
                copy(i, 4 + j, (*chip, 1 - c), me).wait_recv()
        for cp in started:
            cp.wait_send()
        for cp in mine:
            cp.wait()

    return pl.pallas_call(
        body, name=name, in_specs=[ANY] * N_BIG, out_specs=[ANY] * N_BIG,
        out_shape=[jax.ShapeDtypeStruct(shape, BF16) for _, shape, _ in BIG],
        scratch_shapes=[pltpu.SemaphoreType.DMA((7 * N_BIG,)), pltpu.SemaphoreType.DMA((7 * N_BIG,)),
                        pltpu.SemaphoreType.DMA((N_BIG,))],
    )(*shards)


def _reduce_sibling(grads, name):
    def body(*refs):
        ins, outs = refs[:N_BIG], refs[N_BIG:2 * N_BIG]
        send_sems, recv_sems = refs[2 * N_BIG:]
        x, y, c = lax.axis_index("x"), lax.axis_index("y"), lax.axis_index("c")
        cps = [pltpu.make_async_remote_copy(
            src_ref=_shard_of(ins[i], BIG[i][2], 2 * k + (1 - c)), dst_ref=outs[i].at[k],
            send_sem=send_sems.at[4 * i + k], recv_sem=recv_sems.at[4 * i + k], device_id=(x, y, 1 - c),
            device_id_type=MESH) for i in range(N_BIG) for k in range(4)]
        for cp in cps:
            cp.start()
        for cp in cps:
            cp.wait()

    return pl.pallas_call(
        body, name=name, in_specs=[ANY] * N_BIG, out_specs=[ANY] * N_BIG,
        out_shape=[jax.ShapeDtypeStruct((4,) + _shard_shape(shape, axis), F32) for _, shape, axis in BIG],
        scratch_shapes=[pltpu.SemaphoreType.DMA((4 * N_BIG,)), pltpu.SemaphoreType.DMA((4 * N_BIG,))],
    )(*grads)


def _reduce_add(grad, recv, axis, core, name):
    rs, cs = recv.shape[1:]
    rt = _row_tile(rs, 256)
    nt = rs // rt

    def body(c_ref, g_ref, r_ref, p_ref, pb_ref):
        sm = g_ref[...] + r_ref[0]
        p_ref[0] = sm
        pb_ref[0] = sm.astype(BF16)

    if axis == 0:
        g_spec = pl.BlockSpec((rt, cs), lambda k, t, c_ref: ((2 * k + c_ref[0]) * nt + t, 0))
    else:
        g_spec = pl.BlockSpec((rt, cs), lambda k, t, c_ref: (t, 2 * k + c_ref[0]))
    slab = pl.BlockSpec((1, rt, cs), lambda k, t, c_ref: (k, t, 0))
    return pl.pallas_call(
        body, name=name,
        grid_spec=pltpu.PrefetchScalarGridSpec(num_scalar_prefetch=1, grid=(4, nt), in_specs=[g_spec, slab],
                                               out_specs=[slab, slab]),
        out_shape=[jax.ShapeDtypeStruct(recv.shape, F32), jax.ShapeDtypeStruct(recv.shape, BF16)],
        compiler_params=_params(("parallel", "parallel")),
    )(core, grad, recv)


def _all_gather(block, name):
    m_per, n = block.shape

    def body(x_ref, out_ref, send_sems, recv_sems, local_sem):
        x, y, c = lax.axis_index("x"), lax.axis_index("y"), lax.axis_index("c")
        me, sibling = (x, y, c), (x, y, 1 - c)
        chips = [(1 - x, y), (x, 1 - y), (1 - x, 1 - y)]

        def rows(px, py, pc):
            return out_ref.at[pl.ds((4 * px + 2 * py + pc) * m_per, m_per), :]

        def copy(k, blk, to, src=None):
            return pltpu.make_async_remote_copy(
                src_ref=rows(*blk) if src is None else src, dst_ref=rows(*blk),
                send_sem=send_sems.at[k], recv_sem=recv_sems.at[k], device_id=to, device_id_type=MESH)

        mine = pltpu.make_async_copy(x_ref, rows(*me), local_sem)
        mine.start()
        first = [copy(0, me, sibling, src=x_ref)]
        first += [copy(1 + j, me, (*chip, c), src=x_ref) for j, chip in enumerate(chips)]
        for cp in first:
            cp.start()
        passed = [copy(4 + j, (*chip, c), sibling) for j, chip in enumerate(chips)]
        for j, chip in enumerate(chips):
            copy(1 + j, (*chip, c), me).wait_recv()
            passed[j].start()
        copy(0, sibling, me).wait_recv()
        for j, chip in enumerate(chips):
            copy(4 + j, (*chip, 1 - c), me).wait_recv()
        for cp in first + passed:
            cp.wait_send()
        mine.wait()

    return pl.pallas_call(
        body, name=name, in_specs=[ANY], out_specs=ANY,
        out_shape=jax.ShapeDtypeStruct((N_DEV * m_per, n), block.dtype),
        scratch_shapes=[pltpu.SemaphoreType.DMA((7,)), pltpu.SemaphoreType.DMA((7,)), pltpu.SemaphoreType.DMA],
    )(block)


def _exchange_chips(parts, name):
    def body(*refs):
        ins, outs = refs[:N_BIG], refs[N_BIG:2 * N_BIG]
        send_sems, recv_sems = refs[2 * N_BIG:]
        x, y, c = lax.axis_index("x"), lax.axis_index("y"), lax.axis_index("c")
        chips = [(1 - x, y), (x, 1 - y), (1 - x, 1 - y)]
        cps = [pltpu.make_async_remote_copy(
            src_ref=ins[i].at[2 * cx + cy], dst_ref=outs[i].at[j], send_sem=send_sems.at[3 * i + j],
            recv_sem=recv_sems.at[3 * i + j], device_id=(cx, cy, c), device_id_type=MESH)
            for j, (cx, cy) in enumerate(chips) for i in range(N_BIG)]
        for cp in cps:
            cp.start()
        for cp in cps:
            cp.wait()

    return pl.pallas_call(
        body, name=name, in_specs=[ANY] * N_BIG, out_specs=[ANY] * N_BIG,
        out_shape=[jax.ShapeDtypeStruct((3,) + p.shape[1:], p.dtype) for p in parts],
        scratch_shapes=[pltpu.SemaphoreType.DMA((3 * N_BIG,)), pltpu.SemaphoreType.DMA((3 * N_BIG,))],
    )(*parts)


def _adam_math(w, g, m, v):
    m = ADAM_B1 * m + (1.0 - ADAM_B1) * g
    v = ADAM_B2 * v + (1.0 - ADAM_B2) * (g * g)
    m_hat = m / (1.0 - ADAM_B1 ** ADAM_STEP)
    v_hat = v / (1.0 - ADAM_B2 ** ADAM_STEP)
    delta = -ADAM_LR * (m_hat / (jnp.sqrt(v_hat) + ADAM_EPS) + ADAM_WD * w)
    return delta, m, v


def _adam_sharded(own, recv, w, m, v, chip, name):
    rs, cs = w.shape
    rt = _row_tile(rs, 256)

    def body(chip_ref, p_ref, r_ref, w_ref, m_ref, v_ref, g_out, d_out, m_out, v_out):
        g = p_ref[0] + r_ref[0].astype(F32) + r_ref[1].astype(F32) + r_ref[2].astype(F32)
        d, mn, vn = _adam_math(w_ref[...], g, m_ref[...], v_ref[...])
        g_out[...] = g
        d_out[...] = d
        m_out[...] = mn
        v_out[...] = vn

    tile = pl.BlockSpec((rt, cs), lambda t, chip_ref: (t, 0))
    return pl.pallas_call(
        body, name=name,
        grid_spec=pltpu.PrefetchScalarGridSpec(
            num_scalar_prefetch=1, grid=(rs // rt,),
            in_specs=[pl.BlockSpec((1, rt, cs), lambda t, chip_ref: (chip_ref[0], t, 0)),
                      pl.BlockSpec((3, rt, cs), lambda t, chip_ref: (0, t, 0)), tile, tile, tile],
            out_specs=[tile] * 4),
        out_shape=[jax.ShapeDtypeStruct((rs, cs), F32)] * 4,
        compiler_params=_params(("parallel",)),
    )(chip, own, recv, w, m, v)


SMALL = ["g_mix", "ssm_a_re", "ssm_a_im", "ssm_log_dt", "ssm_b_re", "ssm_b_im", "ssm_c_re", "ssm_c_im", "ssm_d",
         "sb_g_q", "sb_g_k", "g_out_ssm", "g_out_sb", "g_xa", "g_mem", "xa_g_q", "xa_g_k", "g_mlp"]
PACK_TILE = SUBLANES * LANES


def _natural_2d(n):
    return (n // LANES, LANES) if n % LANES == 0 else (1, n)


def _pack_small(arrs):
    parts = []
    for a in arrs:
        flat = a.reshape(-1)
        parts.append(jnp.pad(flat, (0, (-flat.shape[0]) % PACK_TILE)))
    return jnp.concatenate(parts).reshape(-1, LANES)


def _adam_replicated(gathered, sizes, ws, ms, vs, name):
    n_w = len(ws)
    r_dev = gathered.shape[0] // N_DEV
    offs, off = [], 0
    for n in sizes:
        offs.append(off)
        off += (n + PACK_TILE - 1) // PACK_TILE * SUBLANES
    assert off == r_dev

    def body(*refs):
        g_ref = refs[0]
        w_refs, m_refs, v_refs = refs[1:1 + n_w], refs[1 + n_w:1 + 2 * n_w], refs[1 + 2 * n_w:1 + 3 * n_w]
        outs = refs[1 + 3 * n_w:]

        def total(i, shape):
            r, cdim = shape
            acc = g_ref[pl.ds(offs[i], r), :cdim]
            for d in range(1, N_DEV):
                acc = acc + g_ref[pl.ds(d * r_dev + offs[i], r), :cdim]
            return acc

        for i in range(n_w):
            g = total(i, w_refs[i].shape)
            d, mn, vn = _adam_math(w_refs[i][...], g, m_refs[i][...], v_refs[i][...])
            for o, val in zip(outs[4 * i:4 * i + 4], (g, d, mn, vn)):
                o[...] = val
        outs[4 * n_w][...] = total(n_w, (SUBLANES, LANES))

    shapes = [w.shape for w in ws]
    return pl.pallas_call(
        body, name=name,
        out_shape=[jax.ShapeDtypeStruct(shp, F32) for shp in shapes for _ in range(4)]
        + [jax.ShapeDtypeStruct((SUBLANES, LANES), F32)],
        compiler_params=_params(),
    )(gathered, *ws, *ms, *vs)


def _local_step(x, mem, target, w, sm):
    s = x.shape[0]
    g = {}
    row = lambda a: a.reshape(1, -1)
    g_mix, g_xa, g_mlp, g_mem = row(sm["g_mix"]), row(sm["g_xa"]), row(sm["g_mlp"]), row(sm["g_mem"])
    g_os, g_ob = row(sm["g_out_ssm"]), row(sm["g_out_sb"])
    sb_gq, sb_gk = jnp.tile(row(sm["sb_g_q"]), (1, SB_HEADS)), jnp.tile(row(sm["sb_g_k"]), (1, SB_HEADS))
    xa_gq, xa_gk = row(sm["xa_g_q"]), row(sm["xa_g_k"])
    d_skip = row(sm["ssm_d"])

    h1 = _norm_fwd(x, g_mix, "norm_mix")
    proj = _mm(h1, w["w_in"], "nn", "in_proj")
    u = proj
    q_raw, k_raw = (proj, SB_WIDTH, 1), (proj, SB_WIDTH, 2)
    v_sb = proj[:, SSM_WIDTH + 2 * SB_WIDTH:].astype(BF16)
    sb_scale = SB_HEAD_DIM ** -0.5
    qk_norm = lambda scale: (lambda xt, gt: ((_rms_groups(xt, gt, scale),), ()))
    qs = _rw(qk_norm(sb_scale), [q_raw], [sb_gq], [(SB_WIDTH, BF16)], [], "sb_qnorm")[0]
    ks = _rw(qk_norm(1.0), [k_raw], [sb_gk], [(SB_WIDTH, BF16)], [], "sb_knorm")[0]
    y_sb = _sb_fwd(qs, ks, v_sb, "sb_fwd")

    ssm_args = (sm["ssm_a_re"], sm["ssm_a_im"], sm["ssm_log_dt"], sm["ssm_b_re"], sm["ssm_b_im"],
                sm["ssm_c_re"], sm["ssm_c_im"])
    (acat, bsup, csup), mats_vjp = jax.vjp(_ssm_mats, *ssm_args)
    states, y0, y1 = _ssm_fwd(u, acat, bsup, csup, d_skip, "ssm_fwd")
    z_glu, y_ssm = _mm(y1, w["ssm_w_glu"], "nn", "ssm_glu", epi=lambda r, yt: (r, yt * jax.nn.sigmoid(r)),
                       extras=(y1,), out_dtypes=(F32, F32))

    def cat_norm(a, b, ga, gb):
        return jnp.concatenate([_rms(a, ga), _rms(b, gb)], axis=1)

    ycat = _rw(lambda a, b, ga, gb: ((cat_norm(a, b, ga, gb),), ()), [y_ssm, y_sb], [g_os, g_ob],
               [(D_MODEL, BF16)], [], "norm_out")[0]
    x1 = _mm(ycat, w["w_out"], "nn", "out_proj", epi=lambda r, xt: (r + xt,), extras=(x,))
    h2 = _norm_fwd(x1, g_xa, "norm_xa")
    qx = _mm(h2, w["xa_w_q"], "nn", "xa_q")
    memn, kv, kn_x, vv_x = _mem_fwd(mem, g_mem, w["xa_w_kv"], xa_gk, "xa_mem")
    o_xa = _xa_fwd(qx, xa_gq, kn_x, vv_x, "xa_fwd")
    x2 = _mm(o_xa, w["xa_w_o"], "nn", "xa_o", epi=lambda r, xt: (r + xt,), extras=(x1,))
    h3 = _norm_fwd(x2, g_mlp, "norm_mlp")

    def up_epi(r):
        rl = jnp.maximum(r, 0.0)
        return (rl * rl,)

    r_up = _mm(h3, w["w_up"], "nn", "mlp_up", epi=up_epi, out_dtypes=(BF16,))

    def loss_epi(r, xt, tt):
        d = r + xt - tt
        return (d * (1.0 / D_MODEL),)

    dx3 = _mm(r_up, w["w_down"], "nn", "mlp_down", epi=loss_epi, extras=(x2, target))
    loss = _rw(lambda d: ((), (jnp.sum(d * d, axis=0, keepdims=True),)), [dx3], [], [], [(1, D_MODEL)], "loss")[0]
    loss = jnp.sum(loss) * (0.5 * D_MODEL)

    g["w_down"] = _mm(r_up, dx3, "tn", "d_w_down")
    da = _mm(dx3, w["w_down"], "nt", "d_r", epi=lambda r, rt: (r * 2.0 * jnp.sqrt(rt.astype(F32)),), extras=(r_up,),
             out_dtypes=(BF16,))
    g["w_up"] = _mm(h3, da, "tn", "d_w_up")
    dh3 = _mm(da, w["w_up"], "nt", "d_h3")
    dx2, g["g_mlp"] = _norm_bwd(x2, g_mlp, dh3, dx3, "d_norm_mlp")
    g["xa_w_o"] = _mm(o_xa, dx2, "tn", "d_xa_w_o")
    do_xa = _mm(dx2, w["xa_w_o"], "nt", "d_o_xa")
    dqx, dkn_x, dvv_x, g["xa_g_q"] = _xa_bwd(qx, xa_gq, kn_x, vv_x, do_xa, "xa_bwd")
    g["xa_w_kv"], g["g_mem"], g["xa_g_k"] = _mem_bwd(mem, g_mem, memn, w["xa_w_kv"], kv, xa_gk, dkn_x, dvv_x,
                                                     "xa_mem_bwd")
    g["xa_w_q"] = _mm(h2, dqx, "tn", "d_xa_w_q")
    dh2 = _mm(dqx, w["xa_w_q"], "nt", "d_h2")
    dx1, g["g_xa"] = _norm_bwd(x1, g_xa, dh2, dx2, "d_norm_xa")
    g["w_out"] = _mm(ycat, dx1, "tn", "d_w_out")
    dycat = _mm(dx1, w["w_out"], "nt", "d_ycat")

    def cat_bwd(a, b, dy, ga, gb):
        _, vjp = jax.vjp(cat_norm, a, b, ga, gb)
        da_, db_, dga, dgb = vjp(dy)
        return (da_, db_), (dga, dgb)

    dy_ssm, dy_sb, g["g_out_ssm"], g["g_out_sb"] = _rw(
        cat_bwd, [y_ssm, y_sb, dycat], [g_os, g_ob], [(SSM_WIDTH, F32), (SB_WIDTH, F32)], [g_os.shape, g_ob.shape],
        "d_norm_out")

    def glu_bwd(dy, yt, zt):
        sg = jax.nn.sigmoid(zt)
        return (dy * sg, dy * yt * sg * (1.0 - sg)), ()

    dy1_a, dz = _rw(glu_bwd, [dy_ssm, y1, z_glu], [], [(SSM_WIDTH, F32), (SSM_WIDTH, BF16)], [], "d_glu")
    g["ssm_w_glu"] = _mm(y1, dz, "tn", "d_w_glu")

    def gelu_bwd_epi(r, da_, y0t):
        _, vjp = jax.vjp(jax.nn.gelu, y0t)
        return (vjp(r + da_)[0],)

    dy0 = _mm(dz, w["ssm_w_glu"], "nt", "d_y1", epi=gelu_bwd_epi, extras=(dy1_a, y0))
    du, da8, d_bsup, d_csup, g["ssm_d"] = _ssm_bwd(dy0, states, u, acat, bsup, csup, d_skip, "ssm_bwd")
    d_acat = jnp.sum(da8, axis=0, keepdims=True)
    for nm, val in zip(("ssm_a_re", "ssm_a_im", "ssm_log_dt", "ssm_b_re", "ssm_b_im", "ssm_c_re", "ssm_c_im"),
                       mats_vjp((d_acat, d_bsup, d_csup))):
        g[nm] = val

    dqs, dkt, dvt = _sb_bwd(qs, ks, v_sb, y_sb, dy_sb, "sb_bwd")

    def qk_norm_bwd(scale):
        def fn(xt, dt, gt):
            _, vjp = jax.vjp(lambda a, b_: _rms_groups(a, b_, scale), xt, gt)
            dx_, dg_ = vjp(dt)
            return (dx_,), (dg_,)
        return fn

    dq_raw, dgq = _rw(qk_norm_bwd(sb_scale), [q_raw, dqs], [sb_gq], [(SB_WIDTH, F32)], [sb_gq.shape], "d_sb_qnorm")
    dk_raw, dgk = _rw(qk_norm_bwd(1.0), [k_raw, _from_key_blocks(dkt)], [sb_gk], [(SB_WIDTH, F32)], [sb_gk.shape],
                      "d_sb_knorm")
    g["sb_g_q"] = jnp.sum(dgq.reshape(SB_HEADS, SB_HEAD_DIM), axis=0)
    g["sb_g_k"] = jnp.sum(dgk.reshape(SB_HEADS, SB_HEAD_DIM), axis=0)
    dproj = jnp.concatenate([du, dq_raw, dk_raw, _from_key_blocks(dvt)], axis=1)
    g["w_in"] = _mm(h1, dproj, "tn", "d_w_in")
    dh1 = _mm(dproj, w["w_in"], "nt", "d_h1")
    dx, g["g_mix"] = _norm_bwd(x, g_mix, dh1, dx1, "d_norm_mix")
    return loss, dx, g


def kernel(x, mem, g_mix, w_in, ssm_a_re, ssm_a_im, ssm_log_dt, ssm_b_re, ssm_b_im, ssm_c_re, ssm_c_im, ssm_d, ssm_w_glu, sb_g_q, sb_g_k, g_out_ssm, g_out_sb, w_out, g_xa, g_mem, xa_w_q, xa_w_kv, xa_g_q, xa_g_k, xa_w_o, g_mlp, w_up, w_down, loss_target, m_g_mix, m_w_in, m_ssm_a_re, m_ssm_a_im, m_ssm_log_dt, m_ssm_b_re, m_ssm_b_im, m_ssm_c_re, m_ssm_c_im, m_ssm_d, m_ssm_w_glu, m_sb_g_q, m_sb_g_k, m_g_out_ssm, m_g_out_sb, m_w_out, m_g_xa, m_g_mem, m_xa_w_q, m_xa_w_kv, m_xa_g_q, m_xa_g_k, m_xa_w_o, m_g_mlp, m_w_up, m_w_down, v_g_mix, v_w_in, v_ssm_a_re, v_ssm_a_im, v_ssm_log_dt, v_ssm_b_re, v_ssm_b_im, v_ssm_c_re, v_ssm_c_im, v_ssm_d, v_ssm_w_glu, v_sb_g_q, v_sb_g_k, v_g_out_ssm, v_g_out_sb, v_w_out, v_g_xa, v_g_mem, v_xa_w_q, v_xa_w_kv, v_xa_g_q, v_xa_g_k, v_xa_w_o, v_g_mlp, v_w_up, v_w_down):
    given = dict(locals())
    order = ["g_mix", "w_in", "ssm_a_re", "ssm_a_im", "ssm_log_dt", "ssm_b_re", "ssm_b_im", "ssm_c_re", "ssm_c_im",
             "ssm_d", "ssm_w_glu", "sb_g_q", "sb_g_k", "g_out_ssm", "g_out_sb", "w_out", "g_xa", "g_mem", "xa_w_q",
             "xa_w_kv", "xa_g_q", "xa_g_k", "xa_w_o", "g_mlp", "w_up", "w_down"]
    assert sorted([n for n, _, _ in BIG] + SMALL) == sorted(order)
    core = lax.axis_index("c").astype(jnp.int32).reshape(1)
    chip = (2 * lax.axis_index("x") + lax.axis_index("y")).astype(jnp.int32).reshape(1)

    gathered = _gather_weights([given[n][0].astype(BF16) for n, _, _ in BIG], "gather_weights")
    w_full = {n: a for (n, _, _), a in zip(BIG, gathered)}
    sm = {n: given[n][0] for n in SMALL}
    loss, dx, g = _local_step(x[0], mem[0], loss_target[0], w_full, sm)

    from_sibling = _reduce_sibling([g[n] for n, _, _ in BIG], "reduce_sibling")
    sums = [_reduce_add(g[n], r, axis, core, "reduce_add_" + n) for (n, _, axis), r in zip(BIG, from_sibling)]
    from_chips = _exchange_chips([pb for _, pb in sums], "reduce_chips")
    res = {}
    for (n, _, _), (own, _), recv in zip(BIG, sums, from_chips):
        outs = _adam_sharded(own, recv, given[n][0], given["m_" + n][0], given["v_" + n][0], chip, "adam_" + n)
        for kind, val in zip(("grad", "delta", "new_m", "new_v"), outs):
            res[kind + "_" + n] = val[None]

    sizes = [math.prod(sm[n].shape) for n in SMALL] + [1]
    packed = _pack_small([g[n] for n in SMALL] + [loss.reshape(1)])
    everyone = _all_gather(packed, "gather_small")
    nat = lambda a: a.reshape(_natural_2d(math.prod(a.shape)))
    outs = _adam_replicated(everyone, sizes, [nat(sm[n]) for n in SMALL], [nat(given["m_" + n][0]) for n in SMALL],
                            [nat(given["v_" + n][0]) for n in SMALL], "adam_replicated")
    for i, n in enumerate(SMALL):
        for kind, val in zip(("grad", "delta", "new_m", "new_v"), outs[4 * i:4 * i + 4]):
            res[kind + "_" + n] = val.reshape(given[n].shape)
    loss_out = outs[-1][0, 0]
    return (loss_out, dx[None], *[res["grad_" + n] for n in order], *[res["delta_" + n] for n in order],
            *[res["new_m_" + n] for n in order], *[res["new_v_" + n] for n in order])
```

```python
import functools
import math

import jax
import jax.numpy as jnp
from jax import lax
from jax.experimental import pallas as pl
from jax.experimental.pallas import tpu as pltpu

F32 = jnp.float32
BF16 = jnp.bfloat16
MESH = pl.DeviceIdType.MESH

N_DEV = 8
D_MODEL = 1024
SSM_WIDTH = 512
SSM_GROUP = 16
SSM_GROUPS = 32
SSM_STATE = 64
N_STATE = SSM_GROUPS * SSM_STATE
SB_HEADS = 8
SB_HEAD_DIM = 64
SB_WIDTH = 512
IN_WIDTH = 2048
XA_HEADS = 4
XA_HEAD_DIM = 128
XA_WIDTH = 512
D_FF = 4096
NORM_EPS = 1e-6
ADAM_LR = 0.001
ADAM_B1 = 0.9
ADAM_B2 = 0.999
ADAM_EPS = 1e-08
ADAM_WD = 0.01
ADAM_STEP = 10

LANES = 128
SUBLANES = 8
VMEM_LIMIT = 48 * 1024 * 1024
SCAN_LANES = 512
SB_BLOCK = 256
SB_UNDERFLOW = -110.0

NN = (((1,), (0,)), ((), ()))
NT = (((1,), (1,)), ((), ()))
TN = (((0,), (0,)), ((), ()))


def _params(sem=None):
    return pltpu.CompilerParams(dimension_semantics=sem, vmem_limit_bytes=VMEM_LIMIT)


def _dot(a, b, dims=NN):
    return lax.dot_general(a.astype(BF16), b.astype(BF16), dims, preferred_element_type=F32)


def _rms(x, g):
    return x * lax.rsqrt(jnp.mean(x * x, axis=-1, keepdims=True) + NORM_EPS) * g


ANY = pl.BlockSpec(memory_space=pl.ANY)


class _Side:
    def __init__(self, ins, out_shapes, n_sem, make):
        self.ins, self.out_shapes, self.n_sem, self.make = list(ins), list(out_shapes), n_sem, make

    def sems(self):
        return [pltpu.SemaphoreType.DMA((self.n_sem,)), pltpu.SemaphoreType.DMA((self.n_sem,))]


def _hosted(body, side, n_in, n_out, grid):
    if side is None:
        return body
    ns_in, ns_out = len(side.ins), len(side.out_shapes)

    def wrapped(*refs):
        ins, refs = refs[:n_in], refs[n_in:]
        s_ins, refs = refs[:ns_in], refs[ns_in:]
        outs, refs = refs[:n_out], refs[n_out:]
        s_outs, refs = refs[:ns_out], refs[ns_out:]
        scratch, sems = refs[:-2], refs[-2:]
        ids = [pl.program_id(d) for d in range(len(grid))]
        first = functools.reduce(jnp.logical_and, [i == 0 for i in ids])
        last = functools.reduce(jnp.logical_and, [i == n - 1 for i, n in zip(ids, grid)])

        @pl.when(first)
        def _():
            for cp in side.make(s_ins, s_outs, *sems):
                cp.start()

        body(*ins, *outs, *scratch)

        @pl.when(last)
        def _():
            for cp in side.make(s_ins, s_outs, *sems):
                cp.wait()

    return wrapped


def _side_args(side):
    if side is None:
        return [], [], [], [], []
    return ([ANY] * len(side.ins), [ANY] * len(side.out_shapes), side.out_shapes, side.sems(), side.ins)


def _split_side(res, n_out, side):
    res = list(res)
    main = res[0] if n_out == 1 else res[:n_out]
    return main if side is None else (main, res[n_out:])


def _mm(a, b, mode, name, *, epi=None, extras=(), out_dtypes=(F32,), tm=1024, tn=1024, tk=1024, side=None):
    if mode == "nn":
        (m, k), (k2, n) = a.shape, b.shape
    elif mode == "nt":
        (m, k), (n, k2) = a.shape, b.shape
    else:
        (k, m), (k2, n) = a.shape, b.shape
    assert k == k2, (name, a.shape, b.shape)
    tm, tn, tk = min(tm, m), min(tn, n), min(tk, k)
    assert m % tm == 0 and n % tn == 0 and k % tk == 0, (name, m, n, k)
    nk = k // tk
    dims = {"nn": NN, "nt": NT, "tn": TN}[mode]
    if mode == "tn":
        a_spec = pl.BlockSpec((tk, tm), lambda i, j, kk: (kk, i))
    else:
        a_spec = pl.BlockSpec((tm, tk), lambda i, j, kk: (i, kk))
    if mode == "nt":
        b_spec = pl.BlockSpec((tn, tk), lambda i, j, kk: (j, kk))
    else:
        b_spec = pl.BlockSpec((tk, tn), lambda i, j, kk: (kk, j))
    mn_spec = pl.BlockSpec((tm, tn), lambda i, j, kk: (i, j))
    n_ex, n_out = len(extras), len(out_dtypes)

    def body(*refs):
        a_ref, b_ref = refs[:2]
        ex = refs[2:2 + n_ex]
        outs = refs[2 + n_ex:2 + n_ex + n_out]
        acc = refs[-1]
        kk = pl.program_id(2)

        @pl.when(kk == 0)
        def _():
            acc[...] = jnp.zeros_like(acc)

        acc[...] += _dot(a_ref[...], b_ref[...], dims)

        @pl.when(kk == nk - 1)
        def _():
            r = acc[...]
            vals = epi(r, *[e[...] for e in ex]) if epi is not None else (r,)
            for o, v in zip(outs, vals):
                o[...] = v.astype(o.dtype)

    grid = (m // tm, n // tn, nk)
    s_in, s_out, s_shape, s_scratch, s_ops = _side_args(side)
    res = pl.pallas_call(
        _hosted(body, side, 2 + n_ex, n_out, grid), name=name, grid=grid,
        in_specs=[a_spec, b_spec] + [mn_spec] * n_ex + s_in,
        out_specs=[mn_spec] * n_out + s_out,
        out_shape=[jax.ShapeDtypeStruct((m, n), dt) for dt in out_dtypes] + s_shape,
        scratch_shapes=[pltpu.VMEM((tm, tn), F32)] + s_scratch,
        compiler_params=_params(("arbitrary",) * 3 if side else ("parallel", "parallel", "arbitrary")),
    )(a, b, *extras, *s_ops)
    return _split_side(res, n_out, side)


def _row_tile(s, target):
    if s <= target:
        return s
    return max(t for t in range(16, target + 1, 16) if s % t == 0)


def _rw(fn, rows, fulls, row_out, acc_out, name, tm=512, side=None):
    cols = [r[1:] if isinstance(r, tuple) else (r.shape[1], 0) for r in rows]
    rows = [r[0] if isinstance(r, tuple) else r for r in rows]
    s = rows[0].shape[0]
    tm = _row_tile(s, tm)
    nr, nf, nro, nao = len(rows), len(fulls), len(row_out), len(acc_out)

    def body(*refs):
        r = refs[:nr]
        f = refs[nr:nr + nf]
        ro = refs[nr + nf:nr + nf + nro]
        ao = refs[nr + nf + nro:]
        outs, accs = fn(*[x[...] for x in r], *[x[...] for x in f])
        for o, v in zip(ro, outs):
            o[...] = v.astype(o.dtype)
        if nao:
            @pl.when(pl.program_id(0) == 0)
            def _():
                for a in ao:
                    a[...] = jnp.zeros_like(a)

            for a, v in zip(ao, accs):
                a[...] += v

    full_spec = lambda shape: pl.BlockSpec(shape, lambda i: (0,) * len(shape))
    s_in, s_out, s_shape, s_scratch, s_ops = _side_args(side)
    res = pl.pallas_call(
        _hosted(body, side, nr + nf, nro + nao, (s // tm,)), name=name, grid=(s // tm,),
        in_specs=[pl.BlockSpec((tm, wd), functools.partial(lambda i, cb: (i, cb), cb=cb)) for wd, cb in cols]
        + [full_spec(x.shape) for x in fulls] + s_in,
        out_specs=[pl.BlockSpec((tm, d), lambda i: (i, 0)) for d, _ in row_out]
        + [full_spec(shape) for shape in acc_out] + s_out,
        out_shape=[jax.ShapeDtypeStruct((s, d), dt) for d, dt in row_out]
        + [jax.ShapeDtypeStruct(shape, F32) for shape in acc_out] + s_shape,
        scratch_shapes=s_scratch,
        compiler_params=_params(("arbitrary",)),
    )(*rows, *fulls, *s_ops)
    res = list(res)
    return res if side is None else (res[:nro + nao], res[nro + nao:])


def _norm_fwd(x, g, name, side=None):
    res = _rw(lambda xt, gt: ((_rms(xt, gt),), ()), [x], [g], [(x.shape[1], BF16)], [], name, side=side)
    return res[0] if side is None else (res[0][0], res[1])


def _norm_bwd(x, g, dh, dres, name, side=None):
    def fn(xt, dht, drt, gt):
        _, vjp = jax.vjp(_rms, xt, gt)
        dx, dg = vjp(dht)
        return (dx + drt,), (dg,)

    return _rw(fn, [x, dh, dres], [g], [(x.shape[1], F32)], [g.shape], name, side=side)


def _rms_groups(x, g, scale):
    lo = lax.broadcasted_iota(jnp.int32, (1, LANES), 1) < SB_HEAD_DIM
    x2 = x * x
    outs = []
    for cb in range(x.shape[1] // LANES):
        sl = slice(cb * LANES, (cb + 1) * LANES)
        s_lo = jnp.sum(jnp.where(lo, x2[:, sl], 0.0), axis=-1, keepdims=True)
        s_hi = jnp.sum(jnp.where(lo, 0.0, x2[:, sl]), axis=-1, keepdims=True)
        r = jnp.where(lo, lax.rsqrt(s_lo * (1.0 / SB_HEAD_DIM) + NORM_EPS),
                      lax.rsqrt(s_hi * (1.0 / SB_HEAD_DIM) + NORM_EPS))
        outs.append(x[:, sl] * r)
    return jnp.concatenate(outs, axis=1) * g * scale


def _log_sigmoid(z):
    return jnp.minimum(z, 0.0) - jnp.log(1.0 + jnp.exp(-jnp.abs(z)))


def _split_dot(x, u2):
    hi = x.astype(BF16)
    lo = (x - hi.astype(F32)).astype(BF16)
    return jnp.dot(jnp.concatenate([hi, lo], axis=1), u2, preferred_element_type=F32)


def _sb_consts(b):
    row = lax.broadcasted_iota(jnp.int32, (b, b), 0)
    col = lax.broadcasted_iota(jnp.int32, (b, b), 1)
    tri = col < row
    u_after = (row > col).astype(BF16)
    u_from = (row >= col).astype(BF16)
    stack = lambda u: jnp.concatenate([u, u], axis=0)
    lane_lo = lax.broadcasted_iota(jnp.int32, (b, LANES), 1) < SB_HEAD_DIM
    return tri, stack(u_after), stack(u_from), lane_lo


def _sb_scores(qh, kb, a_run, tri, u2_after, diag):
    z = lax.dot_general(qh, kb, NT, preferred_element_type=F32)
    lb = _log_sigmoid(z)
    l = lb - z
    if diag:
        l = jnp.where(tri, l, 0.0)
    w = jnp.exp(lb + (a_run + _split_dot(l, u2_after)))
    if diag:
        w = jnp.where(tri, w, 0.0)
    return lb, l, w


def _sb_walk(qi, carry, step):
    def cond(state):
        n, c = state
        return jnp.logical_and(n <= qi, jnp.max(jnp.maximum(c[0], c[1])) > SB_UNDERFLOW)

    def body(state):
        n, c = state
        return n + 1, step(n, c)

    return lax.while_loop(cond, body, (jnp.int32(1), carry))[1]


def _two_heads(x, lane_lo):
    zero = jnp.zeros_like(x)
    return jnp.where(lane_lo, x, zero), jnp.where(lane_lo, zero, x)


def _sb_fwd(qs, ks, v, name, side=None):
    s, width = qs.shape
    b = min(SB_BLOCK, s)

    def body(q_ref, k_ref, v_ref, o_ref):
        qi = pl.program_id(1)
        tri, u2_after, _, lane_lo = _sb_consts(b)
        q_a, q_b = _two_heads(q_ref[...], lane_lo)

        def step(n, carry, diag):
            a_a, a_b, acc = carry
            off = pl.multiple_of((qi - n) * b, b)
            kb = k_ref[pl.ds(off, b), :]
            v_a, v_b = _two_heads(v_ref[pl.ds(off, b), :], lane_lo)
            _, l_a, w_a = _sb_scores(q_a, kb, a_a, tri, u2_after, diag)
            _, l_b, w_b = _sb_scores(q_b, kb, a_b, tri, u2_after, diag)
            acc = acc + jnp.dot(jnp.concatenate([w_a.astype(BF16), w_b.astype(BF16)], axis=1),
                                jnp.concatenate([v_a, v_b], axis=0), preferred_element_type=F32)
            return (a_a + jnp.sum(l_a, axis=1, keepdims=True), a_b + jnp.sum(l_b, axis=1, keepdims=True), acc)

        zero = jnp.zeros((b, 1), F32)
        carry = step(0, (zero, zero, jnp.zeros((b, LANES), F32)), True)
        carry = _sb_walk(qi, carry, lambda n, c: step(n, c, False))
        o_ref[...] = carry[2]

    blk = pl.BlockSpec((b, LANES), lambda hp, i: (i, hp))
    full = pl.BlockSpec((s, LANES), lambda hp, i: (0, hp))
    grid = (width // LANES, s // b)
    s_in, s_out, s_shape, s_scratch, s_ops = _side_args(side)
    res = pl.pallas_call(
        _hosted(body, side, 3, 1, grid), name=name, grid=grid,
        in_specs=[blk, full, full] + s_in, out_specs=[blk] + s_out,
        out_shape=[jax.ShapeDtypeStruct((s, width), F32)] + s_shape, scratch_shapes=s_scratch,
        compiler_params=_params(("arbitrary", "arbitrary")),
    )(qs, ks, v, *s_ops)
    return _split_side(res, 1, side)


def _sb_bwd(qs, ks, v, out, dout, name, side=None):
    s, width = qs.shape
    b = min(SB_BLOCK, s)
    nkb = s // b

    def body(q_ref, k_ref, v_ref, o_ref, do_ref, dq_ref, dkt_ref, dvt_ref):
        qi = pl.program_id(1)

        @pl.when(qi == 0)
        def _():
            dkt_ref[...] = jnp.zeros_like(dkt_ref)
            dvt_ref[...] = jnp.zeros_like(dvt_ref)

        tri, u2_after, u2_from, lane_lo = _sb_consts(b)
        q_a, q_b = _two_heads(q_ref[...], lane_lo)
        dob = do_ref[...].astype(BF16)
        do_a, do_b = _two_heads(dob, lane_lo)
        prod = dob.astype(F32) * o_ref[...]
        d_a = jnp.sum(jnp.where(lane_lo, prod, 0.0), axis=1, keepdims=True)
        d_b = jnp.sum(jnp.where(lane_lo, 0.0, prod), axis=1, keepdims=True)
        tr = lambda x: jnp.transpose(x.astype(F32)).astype(BF16)
        qt = jnp.concatenate([tr(q_a), tr(q_b)], axis=1)
        dot_ = jnp.concatenate([tr(do_a), tr(do_b)], axis=1)

        def head(qh, doh, kb, vb, a_run, d_rem, diag):
            lb, l, w = _sb_scores(qh, kb, a_run, tri, u2_after, diag)
            wb = w.astype(BF16)
            g = lax.dot_general(doh, vb, NT, preferred_element_type=F32) * wb.astype(F32)
            g_before = d_rem - _split_dot(g, u2_from)
            dz = g - (g + g_before) * jnp.exp(lb)
            if diag:
                dz = jnp.where(tri, dz, 0.0)
            return (dz.astype(BF16), wb, a_run + jnp.sum(l, axis=1, keepdims=True),
                    d_rem - jnp.sum(g, axis=1, keepdims=True))

        def step(n, carry, diag):
            a_a, a_b, r_a, r_b, dq = carry
            jb = qi - n
            off = pl.multiple_of(jb * b, b)
            kb = k_ref[pl.ds(off, b), :]
            vb = v_ref[pl.ds(off, b), :]
            k_a, k_b = _two_heads(kb, lane_lo)
            dz_a, w_a, a_a, r_a = head(q_a, do_a, kb, vb, a_a, r_a, diag)
            dz_b, w_b, a_b, r_b = head(q_b, do_b, kb, vb, a_b, r_b, diag)
            dq = dq + jnp.dot(jnp.concatenate([dz_a, dz_b], axis=1), jnp.concatenate([k_a, k_b], axis=0),
                              preferred_element_type=F32)
            dkt_ref[0, jb] += jnp.dot(qt, jnp.concatenate([dz_a, dz_b], axis=0), preferred_element_type=F32)
            dvt_ref[0, jb] += jnp.dot(dot_, jnp.concatenate([w_a, w_b], axis=0), preferred_element_type=F32)
            return a_a, a_b, r_a, r_b, dq

        zero = jnp.zeros((b, 1), F32)
        carry = step(0, (zero, zero, d_a, d_b, jnp.zeros((b, LANES), F32)), True)
        carry = _sb_walk(qi, carry, lambda n, c: step(n, c, False))
        dq_ref[...] = carry[4]

    blk = pl.BlockSpec((b, LANES), lambda hp, i: (i, hp))
    full = pl.BlockSpec((s, LANES), lambda hp, i: (0, hp))
    acc = pl.BlockSpec((1, nkb, LANES, b), lambda hp, i: (hp, 0, 0, 0))
    grid = (width // LANES, nkb)
    s_in, s_out, s_shape, s_scratch, s_ops = _side_args(side)
    res = pl.pallas_call(
        _hosted(body, side, 5, 3, grid), name=name, grid=grid,
        in_specs=[blk, full, full, blk, blk] + s_in, out_specs=[blk, acc, acc] + s_out,
        out_shape=[jax.ShapeDtypeStruct((s, width), F32)]
        + [jax.ShapeDtypeStruct((width // LANES, nkb, LANES, b), F32)] * 2 + s_shape,
        scratch_shapes=s_scratch,
        compiler_params=_params(("arbitrary", "arbitrary")),
    )(qs, ks, v, out, dout, *s_ops)
    return _split_side(res, 3, side)


def _from_key_blocks(t):
    hp, nkb, lanes, b = t.shape
    return jnp.transpose(t, (1, 3, 0, 2)).reshape(nkb * b, hp * lanes)


def _cmul(xr, xi, yr, yi):
    return xr * yr - xi * yi, xr * yi + xi * yr


def _scan_consts(ar, ai, reverse, lc):
    rowi = lax.broadcasted_iota(jnp.int32, (SUBLANES, lc), 0)
    pows = [(ar, ai)]
    for _ in range(SUBLANES - 1):
        pows.append(_cmul(*pows[-1], ar, ai))
    steps = []
    for d in (1, 2, 4):
        keep = (rowi < SUBLANES - d) if reverse else (rowi >= d)
        pr, pi = pows[d - 1]
        steps.append((SUBLANES - d if reverse else d, jnp.where(keep, pr, 0.0), jnp.where(keep, pi, 0.0)))
    cr = jnp.zeros((SUBLANES, lc), F32)
    ci = jnp.zeros((SUBLANES, lc), F32)
    for r in range(SUBLANES):
        pr, pi = pows[SUBLANES - 1 - r] if reverse else pows[r]
        cr = jnp.where(rowi == r, pr, cr)
        ci = jnp.where(rowi == r, pi, ci)
    return steps, cr, ci


def _scan_tile(xr, xi, steps, pr, pi, cr, ci):
    for shift, ar, ai in steps:
        rr = pltpu.roll(xr, shift, 0)
        ri = pltpu.roll(xi, shift, 0)
        xr, xi = xr + ar * rr - ai * ri, xi + ar * ri + ai * rr
    return xr + pr * cr - pi * ci, xi + pr * ci + pi * cr


def _ssm_fwd(u, acat, bsup, csup, d_skip, name, tt=1024, side=None):
    s = u.shape[0]
    lc = SCAN_LANES
    tt = min(tt, s)
    nl, nt = N_STATE // lc, s // tt

    def body(u_ref, a_ref, b_ref, c_ref, d_ref, s_ref, y0_ref, y1_ref, carry):
        @pl.when(pl.program_id(1) == 0)
        def _():
            carry[...] = jnp.zeros_like(carry)

        ut = u_ref[...]
        s_ref[...] = _dot(ut, b_ref[0])
        steps, pr, pi = _scan_consts(a_ref[:, :lc], a_ref[:, lc:], False, lc)

        def tile(i, c):
            off = pl.multiple_of(i * SUBLANES, SUBLANES)
            xr, xi = _scan_tile(s_ref[pl.ds(off, SUBLANES), :lc], s_ref[pl.ds(off, SUBLANES), lc:],
                                steps, pr, pi, c[0], c[1])
            s_ref[pl.ds(off, SUBLANES), :lc] = xr
            s_ref[pl.ds(off, SUBLANES), lc:] = xi
            return (jnp.broadcast_to(xr[SUBLANES - 1:, :], (SUBLANES, lc)),
                    jnp.broadcast_to(xi[SUBLANES - 1:, :], (SUBLANES, lc)))

        cr, ci = lax.fori_loop(0, tt // SUBLANES, tile, (carry[:, :lc], carry[:, lc:]))
        carry[:, :lc] = cr
        carry[:, lc:] = ci
        y0 = _dot(s_ref[...], c_ref[0], NT) + d_ref[...] * ut
        y0_ref[...] = y0
        y1_ref[...] = jax.nn.gelu(y0)

    chan = pl.BlockSpec((tt, LANES), lambda j, c: (c, j))
    sup = pl.BlockSpec((1, LANES, 2 * lc), lambda j, c: (j, 0, 0))
    s_in, s_out, s_shape, s_scratch, s_ops = _side_args(side)
    res = pl.pallas_call(
        _hosted(body, side, 5, 3, (nl, nt)), name=name, grid=(nl, nt),
        in_specs=[chan, pl.BlockSpec((1, 2 * lc), lambda j, c: (0, j)), sup, sup,
                  pl.BlockSpec((1, LANES), lambda j, c: (0, j))] + s_in,
        out_specs=[pl.BlockSpec((tt, 2 * lc), lambda j, c: (c, j)), chan, chan] + s_out,
        out_shape=[jax.ShapeDtypeStruct((s, 2 * N_STATE), F32), jax.ShapeDtypeStruct((s, SSM_WIDTH), F32),
                   jax.ShapeDtypeStruct((s, SSM_WIDTH), F32)] + s_shape,
        scratch_shapes=[pltpu.VMEM((SUBLANES, 2 * lc), F32)] + s_scratch,
        compiler_params=_params(("arbitrary", "arbitrary")),
    )(u, acat, bsup, csup, d_skip, *s_ops)
    return _split_side(res, 3, side)


def _ssm_bwd(dy0, states, u, acat, bsup, csup, d_skip, name, tt=1024, side=None):
    s = u.shape[0]
    lc = SCAN_LANES
    tt = min(tt, s)
    nl, nt = N_STATE // lc, s // tt
    nt8 = tt // SUBLANES

    def body(dy_ref, s_ref, sp_ref, u_ref, a_ref, b_ref, c_ref, d_ref,
             du_ref, da_ref, db_ref, dc_ref, dd_ref, lam_ref, carry):
        c = pl.program_id(1)

        @pl.when(c == 0)
        def _():
            carry[...] = jnp.zeros_like(carry)
            for r in (da_ref, db_ref, dc_ref, dd_ref):
                r[...] = jnp.zeros_like(r)

        dy = dy_ref[...]
        ut = u_ref[...]
        lam_ref[...] = _dot(dy, c_ref[0])
        steps, pr, pi = _scan_consts(a_ref[:, :lc], -a_ref[:, lc:], True, lc)
        rowi = lax.broadcasted_iota(jnp.int32, (SUBLANES, lc), 0)
        first_chunk = c == nt - 1

        def tile(i, carry_v):
            cr, ci, dar, dai = carry_v
            t = nt8 - 1 - i
            off = pl.multiple_of(t * SUBLANES, SUBLANES)
            lr, li = _scan_tile(lam_ref[pl.ds(off, SUBLANES), :lc], lam_ref[pl.ds(off, SUBLANES), lc:],
                                steps, pr, pi, cr, ci)
            lam_ref[pl.ds(off, SUBLANES), :lc] = lr
            lam_ref[pl.ds(off, SUBLANES), lc:] = li
            offp = pl.multiple_of(jnp.maximum(t - 1, 0) * SUBLANES, SUBLANES)
            in_chunk = t > 0
            use = jnp.logical_or(in_chunk, jnp.logical_not(first_chunk))
            prev_r = jnp.where(in_chunk, s_ref[pl.ds(offp, SUBLANES), :lc], sp_ref[:, :lc])
            prev_i = jnp.where(in_chunk, s_ref[pl.ds(offp, SUBLANES), lc:], sp_ref[:, lc:])
            last_r = jnp.where(use, jnp.broadcast_to(prev_r[SUBLANES - 1:, :], (SUBLANES, lc)), 0.0)
            last_i = jnp.where(use, jnp.broadcast_to(prev_i[SUBLANES - 1:, :], (SUBLANES, lc)), 0.0)
            sr = jnp.where(rowi == 0, last_r, pltpu.roll(s_ref[pl.ds(off, SUBLANES), :lc], 1, 0))
            si = jnp.where(rowi == 0, last_i, pltpu.roll(s_ref[pl.ds(off, SUBLANES), lc:], 1, 0))
            dar = dar + lr * sr + li * si
            dai = dai + li * sr - lr * si
            return (jnp.broadcast_to(lr[:1, :], (SUBLANES, lc)), jnp.broadcast_to(li[:1, :], (SUBLANES, lc)),
                    dar, dai)

        zero = jnp.zeros((SUBLANES, lc), F32)
        cr, ci, dar, dai = lax.fori_loop(0, nt8, tile, (carry[:, :lc], carry[:, lc:], zero, zero))
        carry[:, :lc] = cr
        carry[:, lc:] = ci
        da_ref[:, :lc] += dar
        da_ref[:, lc:] += dai
        lam = lam_ref[...].astype(BF16)
        du_ref[...] = _dot(lam, b_ref[0], NT) + d_ref[...] * dy
        db_ref[0] += _dot(ut, lam, TN)
        dc_ref[0] += _dot(dy, s_ref[...], TN)
        dd_ref[...] += jnp.sum(dy * ut, axis=0, keepdims=True)

    rev = lambda j, c: (nt - 1 - c, j)
    chan = pl.BlockSpec((tt, LANES), rev)
    sup = pl.BlockSpec((1, LANES, 2 * lc), lambda j, c: (j, 0, 0))
    row = pl.BlockSpec((1, LANES), lambda j, c: (0, j))
    s_in, s_out, s_shape, s_scratch, s_ops = _side_args(side)
    res = pl.pallas_call(
        _hosted(body, side, 8, 5, (nl, nt)), name=name, grid=(nl, nt),
        in_specs=[chan, pl.BlockSpec((tt, 2 * lc), rev),
                  pl.BlockSpec((SUBLANES, 2 * lc), lambda j, c: (jnp.maximum((nt - 1 - c) * nt8 - 1, 0), j)),
                  chan, pl.BlockSpec((1, 2 * lc), lambda j, c: (0, j)), sup, sup, row] + s_in,
        out_specs=[chan, pl.BlockSpec((SUBLANES, 2 * lc), lambda j, c: (0, j)), sup, sup, row] + s_out,
        out_shape=[jax.ShapeDtypeStruct((s, SSM_WIDTH), F32), jax.ShapeDtypeStruct((SUBLANES, 2 * N_STATE), F32),
                   jax.ShapeDtypeStruct(bsup.shape, F32), jax.ShapeDtypeStruct(csup.shape, F32),
                   jax.ShapeDtypeStruct((1, SSM_WIDTH), F32)] + s_shape,
        scratch_shapes=[pltpu.VMEM((tt, 2 * lc), F32), pltpu.VMEM((SUBLANES, 2 * lc), F32)] + s_scratch,
        compiler_params=_params(("arbitrary", "arbitrary")),
    )(dy0, states, states, u, acat, bsup, csup, d_skip, *s_ops)
    return _split_side(res, 5, side)


def _state_cols(xr, xi):
    lead = xr.shape[:-1]
    nl = N_STATE // SCAN_LANES
    both = jnp.stack([xr.reshape(lead + (nl, SCAN_LANES)), xi.reshape(lead + (nl, SCAN_LANES))], axis=-2)
    return both.reshape(lead + (2 * N_STATE,))


def _ssm_mats(a_re, a_im, log_dt, b_re, b_im, c_re, c_im):
    dt = jnp.exp(log_dt)[:, None]
    lr, li = a_re * dt, a_im * dt
    e = jnp.exp(lr)
    abar_r, abar_i = e * jnp.cos(li), e * jnp.sin(li)
    den = a_re * a_re + a_im * a_im
    coef_r = ((abar_r - 1.0) * a_re + abar_i * a_im) / den
    coef_i = (abar_i * a_re - (abar_r - 1.0) * a_im) / den
    bbar_r = coef_r[..., None] * b_re - coef_i[..., None] * b_im
    bbar_i = coef_r[..., None] * b_im + coef_i[..., None] * b_re
    nl = N_STATE // SCAN_LANES
    gpb = SSM_GROUPS // nl
    eye = jnp.eye(gpb, dtype=bool)[None, :, None, :, None]

    def sup(m_r, m_i):
        def one(m):
            m = m.reshape(nl, gpb, SSM_GROUP, 1, SSM_STATE)
            return jnp.where(eye, m, 0.0).reshape(nl, gpb * SSM_GROUP, SCAN_LANES)
        return jnp.concatenate([one(m_r), one(m_i)], axis=-1)

    acat = _state_cols(abar_r.reshape(1, N_STATE), abar_i.reshape(1, N_STATE))
    bsup = sup(jnp.transpose(bbar_r, (0, 2, 1)), jnp.transpose(bbar_i, (0, 2, 1)))
    csup = sup(c_re, -c_im)
    return acat, bsup, csup


def _mem_fwd(mem, g_mem, w_kv, g_k, name):
    ml = mem.shape[0]

    def body(mem_ref, gm_ref, w_ref, gk_ref, memn_ref, kv_ref, kn_ref, vv_ref):
        memn = _rms(mem_ref[...], gm_ref[...])
        memn_ref[...] = memn.astype(BF16)
        kv = _dot(memn, w_ref[...])
        kv_ref[...] = kv
        for hh in range(XA_HEADS):
            sl = slice(hh * XA_HEAD_DIM, (hh + 1) * XA_HEAD_DIM)
            kn_ref[:, sl] = _rms(kv[:, sl], gk_ref[...]).astype(BF16)
        vv_ref[...] = kv[:, XA_WIDTH:].astype(BF16)

    return pl.pallas_call(
        body, name=name,
        out_shape=[jax.ShapeDtypeStruct((ml, D_MODEL), BF16), jax.ShapeDtypeStruct((ml, 2 * XA_WIDTH), F32),
                   jax.ShapeDtypeStruct((ml, XA_WIDTH), BF16), jax.ShapeDtypeStruct((ml, XA_WIDTH), BF16)],
        compiler_params=_params(),
    )(mem, g_mem, w_kv, g_k)


def _mem_bwd(mem, g_mem, memn, w_kv, kv, g_k, dkn, dvv, name):
    def body(mem_ref, gm_ref, memn_ref, w_ref, kv_ref, gk_ref, dkn_ref, dvv_ref, dw_ref, dgm_ref, dgk_ref):
        kv = kv_ref[...]
        dgk = jnp.zeros(dgk_ref.shape, F32)
        parts = []
        for hh in range(XA_HEADS):
            sl = slice(hh * XA_HEAD_DIM, (hh + 1) * XA_HEAD_DIM)
            _, vjp = jax.vjp(_rms, kv[:, sl], gk_ref[...])
            dk, dg = vjp(dkn_ref[:, sl])
            parts.append(dk)
            dgk = dgk + dg
        dgk_ref[...] = dgk
        dkv = jnp.concatenate(parts + [dvv_ref[...]], axis=1)
        dw_ref[...] = _dot(memn_ref[...], dkv, TN)
        dmemn = _dot(dkv, w_ref[...], NT)
        _, vjp = jax.vjp(_rms, mem_ref[...], gm_ref[...])
        dgm_ref[...] = vjp(dmemn)[1]

    return pl.pallas_call(
        body, name=name,
        out_shape=[jax.ShapeDtypeStruct((D_MODEL, 2 * XA_WIDTH), F32), jax.ShapeDtypeStruct(g_mem.shape, F32),
                   jax.ShapeDtypeStruct(g_k.shape, F32)],
        compiler_params=_params(),
    )(mem, g_mem, memn, w_kv, kv, g_k, dkn, dvv)


def _xa_head(qx_h, g_q, kn_h, vv_h):
    qn = _rms(qx_h, g_q)
    sc = _dot(qn, kn_h, NT) * (XA_HEAD_DIM ** -0.5)
    sc = sc - jnp.max(sc, axis=-1, keepdims=True)
    e = jnp.exp(sc)
    p = e / jnp.sum(e, axis=-1, keepdims=True)
    return qn, p


def _xa_fwd(qx, g_q, kn, vv, name):
    def fn(qt, gq, knt, vvt):
        outs = []
        for hh in range(XA_HEADS):
            sl = slice(hh * XA_HEAD_DIM, (hh + 1) * XA_HEAD_DIM)
            _, p = _xa_head(qt[:, sl], gq, knt[:, sl], vvt[:, sl])
            outs.append(_dot(p, vvt[:, sl]))
        return (jnp.concatenate(outs, axis=1),), ()

    return _rw(fn, [qx], [g_q, kn, vv], [(XA_WIDTH, BF16)], [], name)[0]


def _xa_bwd(qx, g_q, kn, vv, do, name):
    def fn(qt, dot_, gq, knt, vvt):
        dqs, dks, dvs = [], [], []
        dgq = jnp.zeros_like(gq)
        for hh in range(XA_HEADS):
            sl = slice(hh * XA_HEAD_DIM, (hh + 1) * XA_HEAD_DIM)
            qn, p = _xa_head(qt[:, sl], gq, knt[:, sl], vvt[:, sl])
            doh = dot_[:, sl]
            dp = _dot(doh, vvt[:, sl], NT)
            dvs.append(_dot(p, doh, TN))
            ds = p * (dp - jnp.sum(dp * p, axis=-1, keepdims=True)) * (XA_HEAD_DIM ** -0.5)
            dqn = _dot(ds, knt[:, sl])
            dks.append(_dot(ds, qn, TN))
            _, vjp = jax.vjp(_rms, qt[:, sl], gq)
            dq, dg = vjp(dqn)
            dqs.append(dq)
            dgq = dgq + dg
        return ((jnp.concatenate(dqs, axis=1),),
                (jnp.concatenate(dks, axis=1), jnp.concatenate(dvs, axis=1), dgq))

    return _rw(fn, [qx, do], [g_q, kn, vv], [(XA_WIDTH, F32)], [kn.shape, vv.shape, g_q.shape], name)


BIG = [
    ("w_in", (D_MODEL, IN_WIDTH), 1), ("ssm_w_glu", (SSM_WIDTH, SSM_WIDTH), 0), ("w_out", (D_MODEL, D_MODEL), 0),
    ("xa_w_q", (D_MODEL, XA_WIDTH), 0), ("xa_w_kv", (D_MODEL, 2 * XA_WIDTH), 0), ("xa_w_o", (XA_WIDTH, D_MODEL), 1),
    ("w_up", (D_MODEL, D_FF), 1), ("w_down", (D_FF, D_MODEL), 0),
]
BIG_INDEX = {n: i for i, (n, _, _) in enumerate(BIG)}


def _shard_shape(shape, axis):
    return tuple(d // N_DEV if i == axis else d for i, d in enumerate(shape))


def _shard_of(ref, axis, d):
    n = ref.shape[axis] // N_DEV
    return ref.at[pl.ds(d * n, n), :] if axis == 0 else ref.at[:, pl.ds(d * n, n)]


def _gather_side(names, shards):
    idxs = [BIG_INDEX[n] for n in names]

    def make(ins, outs, send_sems, recv_sems):
        x, y, c = lax.axis_index("x"), lax.axis_index("y"), lax.axis_index("c")
        cps = []
        for j, i in enumerate(idxs):
            mine = _shard_of(outs[j], BIG[i][2], 4 * x + 2 * y + c)
            cps.append(pltpu.make_async_copy(ins[j], mine, send_sems.at[N_DEV * j]))
            for rel in range(1, N_DEV):
                to = tuple(1 - p if rel >> bit & 1 else p for p, bit in ((x, 2), (y, 1), (c, 0)))
                cps.append(pltpu.make_async_remote_copy(
                    src_ref=ins[j], dst_ref=mine, send_sem=send_sems.at[N_DEV * j + rel],
                    recv_sem=recv_sems.at[N_DEV * j + rel], device_id=to, device_id_type=MESH))
        return cps

    return _Side(shards, [jax.ShapeDtypeStruct(BIG[i][1], BF16) for i in idxs], N_DEV * len(idxs), make)


def _sibling_side(names, grads):
    idxs = [BIG_INDEX[n] for n in names]

    def make(ins, outs, send_sems, recv_sems):
        x, y, c = lax.axis_index("x"), lax.axis_index("y"), lax.axis_index("c")
        return [pltpu.make_async_remote_copy(
            src_ref=_shard_of(ins[j], BIG[i][2], 2 * k + (1 - c)), dst_ref=outs[j].at[k],
            send_sem=send_sems.at[4 * j + k], recv_sem=recv_sems.at[4 * j + k], device_id=(x, y, 1 - c),
            device_id_type=MESH) for j, i in enumerate(idxs) for k in range(4)]

    shapes = [jax.ShapeDtypeStruct((4,) + _shard_shape(BIG[i][1], BIG[i][2]), F32) for i in idxs]
    return _Side(grads, shapes, 4 * len(idxs), make)


def _chips_side(parts):
    def make(ins, outs, send_sems, recv_sems):
        x, y, c = lax.axis_index("x"), lax.axis_index("y"), lax.axis_index("c")
        chips = [(1 - x, y), (x, 1 - y), (1 - x, 1 - y)]
        return [pltpu.make_async_remote_copy(
            src_ref=ins[j].at[2 * cx + cy], dst_ref=outs[j].at[r], send_sem=send_sems.at[3 * j + r],
            recv_sem=recv_sems.at[3 * j + r], device_id=(cx, cy, c), device_id_type=MESH)
            for r, (cx, cy) in enumerate(chips) for j in range(len(parts))]

    return _Side(parts, [jax.ShapeDtypeStruct((3,) + p.shape[1:], p.dtype) for p in parts], 3 * len(parts), make)


def _reduce_add(grad, recv, axis, core, name):
    rs, cs = recv.shape[1:]
    rt = _row_tile(rs, 256)
    nt = rs // rt

    def body(c_ref, g_ref, r_ref, p_ref, pb_ref):
        sm = g_ref[...] + r_ref[0]
        p_ref[0] = sm
        pb_ref[0] = sm.astype(BF16)

    if axis == 0:
        g_spec = pl.BlockSpec((rt, cs), lambda k, t, c_ref: ((2 * k + c_ref[0]) * nt + t, 0))
    else:
        g_spec = pl.BlockSpec((rt, cs), lambda k, t, c_ref: (t, 2 * k + c_ref[0]))
    slab = pl.BlockSpec((1, rt, cs), lambda k, t, c_ref: (k, t, 0))
    return pl.pallas_call(
        body, name=name,
        grid_spec=pltpu.PrefetchScalarGridSpec(num_scalar_prefetch=1, grid=(4, nt), in_specs=[g_spec, slab],
                                               out_specs=[slab, slab]),
        out_shape=[jax.ShapeDtypeStruct(recv.shape, F32), jax.ShapeDtypeStruct(recv.shape, BF16)],
        compiler_params=_params(("parallel", "parallel")),
    )(core, grad, recv)


def _all_gather(block, name):
    m_per, n = block.shape

    def body(x_ref, out_ref, send_sems, recv_sems, local_sem):
        x, y, c = lax.axis_index("x"), lax.axis_index("y"), lax.axis_index("c")
        me, sibling = (x, y, c), (x, y, 1 - c)
        chips = [(1 - x, y), (x, 1 - y), (1 - x, 1 - y)]

        def rows(px, py, pc):
            return out_ref.at[pl.ds((4 * px + 2 * py + pc) * m_per, m_per), :]

        def copy(k, blk, to, src=None):
            return pltpu.make_async_remote_copy(
                src_ref=rows(*blk) if src is None else src, dst_ref=rows(*blk),
                send_sem=send_sems.at[k], recv_sem=recv_sems.at[k], device_id=to, device_id_type=MESH)

        mine = pltpu.make_async_copy(x_ref, rows(*me), local_sem)
        mine.start()
        first = [copy(0, me, sibling, src=x_ref)]
        first += [copy(1 + j, me, (*chip, c), src=x_ref) for j, chip in enumerate(chips)]
        for cp in first:
            cp.start()
        passed = [copy(4 + j, (*chip, c), sibling) for j, chip in enumerate(chips)]
        for j, chip in enumerate(chips):
            copy(1 + j, (*chip, c), me).wait_recv()
            passed[j].start()
        copy(0, sibling, me).wait_recv()
        for j, chip in enumerate(chips):
            copy(4 + j, (*chip, 1 - c), me).wait_recv()
        for cp in first + passed:
            cp.wait_send()
        mine.wait()

    return pl.pallas_call(
        body, name=name, in_specs=[ANY], out_specs=ANY,
        out_shape=jax.ShapeDtypeStruct((N_DEV * m_per, n), block.dtype),
        scratch_shapes=[pltpu.SemaphoreType.DMA((7,)), pltpu.SemaphoreType.DMA((7,)), pltpu.SemaphoreType.DMA],
    )(block)


def _adam_math(w, g, m, v):
    m = ADAM_B1 * m + (1.0 - ADAM_B1) * g
    v = ADAM_B2 * v + (1.0 - ADAM_B2) * (g * g)
    m_hat = m / (1.0 - ADAM_B1 ** ADAM_STEP)
    v_hat = v / (1.0 - ADAM_B2 ** ADAM_STEP)
    delta = -ADAM_LR * (m_hat / (jnp.sqrt(v_hat) + ADAM_EPS) + ADAM_WD * w)
    return delta, m, v


def _adam_sharded(own, recv, w, m, v, chip, name):
    rs, cs = w.shape
    rt = _row_tile(rs, 256)

    def body(chip_ref, p_ref, r_ref, w_ref, m_ref, v_ref, g_out, d_out, m_out, v_out):
        g = p_ref[0] + r_ref[0].astype(F32) + r_ref[1].astype(F32) + r_ref[2].astype(F32)
        d, mn, vn = _adam_math(w_ref[...], g, m_ref[...], v_ref[...])
        g_out[...] = g
        d_out[...] = d
        m_out[...] = mn
        v_out[...] = vn

    tile = pl.BlockSpec((rt, cs), lambda t, chip_ref: (t, 0))
    return pl.pallas_call(
        body, name=name,
        grid_spec=pltpu.PrefetchScalarGridSpec(
            num_scalar_prefetch=1, grid=(rs // rt,),
            in_specs=[pl.BlockSpec((1, rt, cs), lambda t, chip_ref: (chip_ref[0], t, 0)),
                      pl.BlockSpec((3, rt, cs), lambda t, chip_ref: (0, t, 0)), tile, tile, tile],
            out_specs=[tile] * 4),
        out_shape=[jax.ShapeDtypeStruct((rs, cs), F32)] * 4,
        compiler_params=_params(("parallel",)),
    )(chip, own, recv, w, m, v)


SMALL = ["g_mix", "ssm_a_re", "ssm_a_im", "ssm_log_dt", "ssm_b_re", "ssm_b_im", "ssm_c_re", "ssm_c_im", "ssm_d",
         "sb_g_q", "sb_g_k", "g_out_ssm", "g_out_sb", "g_xa", "g_mem", "xa_g_q", "xa_g_k", "g_mlp"]
PACK_TILE = SUBLANES * LANES


def _natural_2d(n):
    return (n // LANES, LANES) if n % LANES == 0 else (1, n)


def _pack_small(arrs):
    parts = []
    for a in arrs:
        flat = a.reshape(-1)
        parts.append(jnp.pad(flat, (0, (-flat.shape[0]) % PACK_TILE)))
    return jnp.concatenate(parts).reshape(-1, LANES)


def _adam_replicated(gathered, sizes, ws, ms, vs, name):
    n_w = len(ws)
    r_dev = gathered.shape[0] // N_DEV
    offs, off = [], 0
    for n in sizes:
        offs.append(off)
        off += (n + PACK_TILE - 1) // PACK_TILE * SUBLANES
    assert off == r_dev

    def body(*refs):
        g_ref = refs[0]
        w_refs, m_refs, v_refs = refs[1:1 + n_w], refs[1 + n_w:1 + 2 * n_w], refs[1 + 2 * n_w:1 + 3 * n_w]
        outs = refs[1 + 3 * n_w:]

        def total(i, shape):
            r, cdim = shape
            acc = g_ref[pl.ds(offs[i], r), :cdim]
            for d in range(1, N_DEV):
                acc = acc + g_ref[pl.ds(d * r_dev + offs[i], r), :cdim]
            return acc

        for i in range(n_w):
            g = total(i, w_refs[i].shape)
            d, mn, vn = _adam_math(w_refs[i][...], g, m_refs[i][...], v_refs[i][...])
            for o, val in zip(outs[4 * i:4 * i + 4], (g, d, mn, vn)):
                o[...] = val
        outs[4 * n_w][...] = total(n_w, (SUBLANES, LANES))

    shapes = [w.shape for w in ws]
    return pl.pallas_call(
        body, name=name,
        out_shape=[jax.ShapeDtypeStruct(shp, F32) for shp in shapes for _ in range(4)]
        + [jax.ShapeDtypeStruct((SUBLANES, LANES), F32)],
        compiler_params=_params(),
    )(gathered, *ws, *ms, *vs)


def _step(x, mem, target, shards, sm, core):
    g, w, sums, reduced = {}, {}, {}, {}

    def gather(names):
        return _gather_side(names, [shards[n] for n in names])

    def to_sibling(names):
        return _sibling_side(names, [g[n] for n in names])

    def add_sibling(names, received):
        for n, r in zip(names, received):
            sums[n] = _reduce_add(g[n], r, BIG[BIG_INDEX[n]][2], core, "reduce_add_" + n)

    def to_chips(names):
        return _chips_side([sums[n][1] for n in names])

    def keep(names, received):
        for n, r in zip(names, received):
            reduced[n] = (sums[n][0], r)

    row = lambda a: a.reshape(1, -1)
    g_mix, g_xa, g_mlp, g_mem = row(sm["g_mix"]), row(sm["g_xa"]), row(sm["g_mlp"]), row(sm["g_mem"])
    g_os, g_ob = row(sm["g_out_ssm"]), row(sm["g_out_sb"])
    sb_gq, sb_gk = jnp.tile(row(sm["sb_g_q"]), (1, SB_HEADS)), jnp.tile(row(sm["sb_g_k"]), (1, SB_HEADS))
    xa_gq, xa_gk = row(sm["xa_g_q"]), row(sm["xa_g_k"])
    d_skip = row(sm["ssm_d"])

    h1, (w["w_in"],) = _norm_fwd(x, g_mix, "norm_mix", side=gather(["w_in"]))
    proj = _mm(h1, w["w_in"], "nn", "in_proj")
    u = proj
    q_raw, k_raw = (proj, SB_WIDTH, 1), (proj, SB_WIDTH, 2)
    v_sb = proj[:, SSM_WIDTH + 2 * SB_WIDTH:].astype(BF16)
    sb_scale = SB_HEAD_DIM ** -0.5
    qk_norm = lambda scale: (lambda xt, gt: ((_rms_groups(xt, gt, scale),), ()))
    qs = _rw(qk_norm(sb_scale), [q_raw], [sb_gq], [(SB_WIDTH, BF16)], [], "sb_qnorm")[0]
    ks = _rw(qk_norm(1.0), [k_raw], [sb_gk], [(SB_WIDTH, BF16)], [], "sb_knorm")[0]
    early = ["ssm_w_glu", "w_out", "xa_w_q", "xa_w_kv", "xa_w_o", "w_up"]
    y_sb, got = _sb_fwd(qs, ks, v_sb, "sb_fwd", side=gather(early))
    w.update(zip(early, got))

    ssm_args = (sm["ssm_a_re"], sm["ssm_a_im"], sm["ssm_log_dt"], sm["ssm_b_re"], sm["ssm_b_im"],
                sm["ssm_c_re"], sm["ssm_c_im"])
    (acat, bsup, csup), mats_vjp = jax.vjp(_ssm_mats, *ssm_args)
    (states, y0, y1), (w["w_down"],) = _ssm_fwd(u, acat, bsup, csup, d_skip, "ssm_fwd", side=gather(["w_down"]))
    z_glu, y_ssm = _mm(y1, w["ssm_w_glu"], "nn", "ssm_glu", epi=lambda r, yt: (r, yt * jax.nn.sigmoid(r)),
                       extras=(y1,), out_dtypes=(F32, F32))

    def cat_norm(a, b, ga, gb):
        return jnp.concatenate([_rms(a, ga), _rms(b, gb)], axis=1)

    ycat = _rw(lambda a, b, ga, gb: ((cat_norm(a, b, ga, gb),), ()), [y_ssm, y_sb], [g_os, g_ob],
               [(D_MODEL, BF16)], [], "norm_out")[0]
    x1 = _mm(ycat, w["w_out"], "nn", "out_proj", epi=lambda r, xt: (r + xt,), extras=(x,))
    h2 = _norm_fwd(x1, g_xa, "norm_xa")
    qx = _mm(h2, w["xa_w_q"], "nn", "xa_q")
    memn, kv, kn_x, vv_x = _mem_fwd(mem, g_mem, w["xa_w_kv"], xa_gk, "xa_mem")
    o_xa = _xa_fwd(qx, xa_gq, kn_x, vv_x, "xa_fwd")
    x2 = _mm(o_xa, w["xa_w_o"], "nn", "xa_o", epi=lambda r, xt: (r + xt,), extras=(x1,))
    h3 = _norm_fwd(x2, g_mlp, "norm_mlp")

    def up_epi(r):
        rl = jnp.maximum(r, 0.0)
        return (rl * rl,)

    r_up = _mm(h3, w["w_up"], "nn", "mlp_up", epi=up_epi, out_dtypes=(BF16,))

    def loss_epi(r, xt, tt):
        d = r + xt - tt
        return (d * (1.0 / D_MODEL),)

    dx3 = _mm(r_up, w["w_down"], "nn", "mlp_down", epi=loss_epi, extras=(x2, target))
    loss = _rw(lambda d: ((), (jnp.sum(d * d, axis=0, keepdims=True),)), [dx3], [], [], [(1, D_MODEL)], "loss")[0]
    loss = jnp.sum(loss) * (0.5 * D_MODEL)

    g["w_down"] = _mm(r_up, dx3, "tn", "d_w_down")
    da = _mm(dx3, w["w_down"], "nt", "d_r", epi=lambda r, rt: (r * 2.0 * jnp.sqrt(rt.astype(F32)),), extras=(r_up,),
             out_dtypes=(BF16,))
    g["w_up"] = _mm(h3, da, "tn", "d_w_up")
    mlp = ["w_down", "w_up"]
    dh3, got = _mm(da, w["w_up"], "nt", "d_h3", side=to_sibling(mlp))
    add_sibling(mlp, got)
    dx2, g["g_mlp"] = _norm_bwd(x2, g_mlp, dh3, dx3, "d_norm_mlp")
    g["xa_w_o"] = _mm(o_xa, dx2, "tn", "d_xa_w_o")
    do_xa = _mm(dx2, w["xa_w_o"], "nt", "d_o_xa")
    dqx, dkn_x, dvv_x, g["xa_g_q"] = _xa_bwd(qx, xa_gq, kn_x, vv_x, do_xa, "xa_bwd")
    g["xa_w_kv"], g["g_mem"], g["xa_g_k"] = _mem_bwd(mem, g_mem, memn, w["xa_w_kv"], kv, xa_gk, dkn_x, dvv_x,
                                                     "xa_mem_bwd")
    g["xa_w_q"] = _mm(h2, dqx, "tn", "d_xa_w_q")
    dh2 = _mm(dqx, w["xa_w_q"], "nt", "d_h2")
    dx1, g["g_xa"] = _norm_bwd(x1, g_xa, dh2, dx2, "d_norm_xa")
    g["w_out"] = _mm(ycat, dx1, "tn", "d_w_out")
    dycat = _mm(dx1, w["w_out"], "nt", "d_ycat")

    def cat_bwd(a, b, dy, ga, gb):
        _, vjp = jax.vjp(cat_norm, a, b, ga, gb)
        da_, db_, dga, dgb = vjp(dy)
        return (da_, db_), (dga, dgb)

    dy_ssm, dy_sb, g["g_out_ssm"], g["g_out_sb"] = _rw(
        cat_bwd, [y_ssm, y_sb, dycat], [g_os, g_ob], [(SSM_WIDTH, F32), (SB_WIDTH, F32)], [g_os.shape, g_ob.shape],
        "d_norm_out")

    def glu_bwd(dy, yt, zt):
        sg = jax.nn.sigmoid(zt)
        return (dy * sg, dy * yt * sg * (1.0 - sg)), ()

    dy1_a, dz = _rw(glu_bwd, [dy_ssm, y1, z_glu], [], [(SSM_WIDTH, F32), (SSM_WIDTH, BF16)], [], "d_glu")
    g["ssm_w_glu"] = _mm(y1, dz, "tn", "d_w_glu")

    def gelu_bwd_epi(r, da_, y0t):
        _, vjp = jax.vjp(jax.nn.gelu, y0t)
        return (vjp(r + da_)[0],)

    mid = ["w_out", "xa_w_q", "xa_w_kv", "xa_w_o", "ssm_w_glu"]
    dy0, got = _mm(dz, w["ssm_w_glu"], "nt", "d_y1", epi=gelu_bwd_epi, extras=(dy1_a, y0), side=to_sibling(mid))
    add_sibling(mid, got)
    (du, da8, d_bsup, d_csup, g["ssm_d"]), got = _ssm_bwd(dy0, states, u, acat, bsup, csup, d_skip, "ssm_bwd",
                                                          side=to_chips(mlp))
    keep(mlp, got)
    d_acat = jnp.sum(da8, axis=0, keepdims=True)
    for nm, val in zip(("ssm_a_re", "ssm_a_im", "ssm_log_dt", "ssm_b_re", "ssm_b_im", "ssm_c_re", "ssm_c_im"),
                       mats_vjp((d_acat, d_bsup, d_csup))):
        g[nm] = val

    (dqs, dkt, dvt), got = _sb_bwd(qs, ks, v_sb, y_sb, dy_sb, "sb_bwd", side=to_chips(mid))
    keep(mid, got)

    def qk_norm_bwd(scale):
        def fn(xt, dt, gt):
            _, vjp = jax.vjp(lambda a, b_: _rms_groups(a, b_, scale), xt, gt)
            dx_, dg_ = vjp(dt)
            return (dx_,), (dg_,)
        return fn

    dq_raw, dgq = _rw(qk_norm_bwd(sb_scale), [q_raw, dqs], [sb_gq], [(SB_WIDTH, F32)], [sb_gq.shape], "d_sb_qnorm")
    dk_raw, dgk = _rw(qk_norm_bwd(1.0), [k_raw, _from_key_blocks(dkt)], [sb_gk], [(SB_WIDTH, F32)], [sb_gk.shape],
                      "d_sb_knorm")
    g["sb_g_q"] = jnp.sum(dgq.reshape(SB_HEADS, SB_HEAD_DIM), axis=0)
    g["sb_g_k"] = jnp.sum(dgk.reshape(SB_HEADS, SB_HEAD_DIM), axis=0)
    dproj = jnp.concatenate([du, dq_raw, dk_raw, _from_key_blocks(dvt)], axis=1)
    g["w_in"] = _mm(h1, dproj, "tn", "d_w_in")
    dh1, got = _mm(dproj, w["w_in"], "nt", "d_h1", side=to_sibling(["w_in"]))
    add_sibling(["w_in"], got)
    (dx, g["g_mix"]), got = _norm_bwd(x, g_mix, dh1, dx1, "d_norm_mix", side=to_chips(["w_in"]))
    keep(["w_in"], got)
    return loss, dx, g, reduced


def kernel(x, mem, g_mix, w_in, ssm_a_re, ssm_a_im, ssm_log_dt, ssm_b_re, ssm_b_im, ssm_c_re, ssm_c_im, ssm_d, ssm_w_glu, sb_g_q, sb_g_k, g_out_ssm, g_out_sb, w_out, g_xa, g_mem, xa_w_q, xa_w_kv, xa_g_q, xa_g_k, xa_w_o, g_mlp, w_up, w_down, loss_target, m_g_mix, m_w_in, m_ssm_a_re, m_ssm_a_im, m_ssm_log_dt, m_ssm_b_re, m_ssm_b_im, m_ssm_c_re, m_ssm_c_im, m_ssm_d, m_ssm_w_glu, m_sb_g_q, m_sb_g_k, m_g_out_ssm, m_g_out_sb, m_w_out, m_g_xa, m_g_mem, m_xa_w_q, m_xa_w_kv, m_xa_g_q, m_xa_g_k, m_xa_w_o, m_g_mlp, m_w_up, m_w_down, v_g_mix, v_w_in, v_ssm_a_re, v_ssm_a_im, v_ssm_log_dt, v_ssm_b_re, v_ssm_b_im, v_ssm_c_re, v_ssm_c_im, v_ssm_d, v_ssm_w_glu, v_sb_g_q, v_sb_g_k, v_g_out_ssm, v_g_out_sb, v_w_out, v_g_xa, v_g_mem, v_xa_w_q, v_xa_w_kv, v_xa_g_q, v_xa_g_k, v_xa_w_o, v_g_mlp, v_w_up, v_w_down):
    given = dict(locals())
    order = ["g_mix", "w_in", "ssm_a_re", "ssm_a_im", "ssm_log_dt", "ssm_b_re", "ssm_b_im", "ssm_c_re", "ssm_c_im",
             "ssm_d", "ssm_w_glu", "sb_g_q", "sb_g_k", "g_out_ssm", "g_out_sb", "w_out", "g_xa", "g_mem", "xa_w_q",
             "xa_w_kv", "xa_g_q", "xa_g_k", "xa_w_o", "g_mlp", "w_up", "w_down"]
    assert sorted([n for n, _, _ in BIG] + SMALL) == sorted(order)
    core = lax.axis_index("c").astype(jnp.int32).reshape(1)
    chip = (2 * lax.axis_index("x") + lax.axis_index("y")).astype(jnp.int32).reshape(1)

    shards = {n: given[n][0].astype(BF16) for n, _, _ in BIG}
    sm = {n: given[n][0] for n in SMALL}
    loss, dx, g, reduced = _step(x[0], mem[0], loss_target[0], shards, sm, core)

    res = {}
    for n, _, _ in BIG:
        own, recv = reduced[n]
        outs = _adam_sharded(own, recv, given[n][0], given["m_" + n][0], given["v_" + n][0], chip, "adam_" + n)
        for kind, val in zip(("grad", "delta", "new_m", "new_v"), outs):
            res[kind + "_" + n] = val[None]

    sizes = [math.prod(sm[n].shape) for n in SMALL] + [1]
    packed = _pack_small([g[n] for n in SMALL] + [loss.reshape(1)])
    everyone = _all_gather(packed, "gather_small")
    nat = lambda a: a.reshape(_natural_2d(math.prod(a.shape)))
    outs = _adam_replicated(everyone, sizes, [nat(sm[n]) for n in SMALL], [nat(given["m_" + n][0]) for n in SMALL],
                            [nat(given["v_" + n][0]) for n in SMALL], "adam_replicated")
    for i, n in enumerate(SMALL):
        for kind, val in zip(("grad", "delta", "new_m", "new_v"), outs[4 * i:4 * i + 4]):
            res[kind + "_" + n] = val.reshape(given[n].shape)
    loss_out = outs[-1][0, 0]
    return (loss_out, dx[None], *[res["grad_" + n] for n in order], *[res["delta_" + n] for n in order],
            *[res["new_m_" + n] for n in order], *[res["new_v_" + n] for n in order])
```

```python
import functools
import math

import jax
import jax.numpy as jnp
from jax import lax
from jax.experimental import pallas as pl
from jax.experimental.pallas import tpu as pltpu

F32 = jnp.float32
BF16 = jnp.bfloat16
MESH = pl.DeviceIdType.MESH

N_DEV = 8
D_MODEL = 1024
SSM_WIDTH = 512
SSM_GROUP = 16
SSM_GROUPS = 32
SSM_STATE = 64
N_STATE = SSM_GROUPS * SSM_STATE
SB_HEADS = 8
SB_HEAD_DIM = 64
SB_WIDTH = 512
IN_WIDTH = 2048
XA_HEADS = 4
XA_HEAD_DIM = 128
XA_WIDTH = 512
D_FF = 4096
NORM_EPS = 1e-6
ADAM_LR = 0.001
ADAM_B1 = 0.9
ADAM_B2 = 0.999
ADAM_EPS = 1e-08
ADAM_WD = 0.01
ADAM_STEP = 10

LANES = 128
SUBLANES = 8
VMEM_LIMIT = 48 * 1024 * 1024
SCAN_LANES = 512
SB_BLOCK = 256
SB_UNDERFLOW = -110.0

NN = (((1,), (0,)), ((), ()))
NT = (((1,), (1,)), ((), ()))
TN = (((0,), (0,)), ((), ()))


def _params(sem=None):
    return pltpu.CompilerParams(dimension_semantics=sem, vmem_limit_bytes=VMEM_LIMIT)


def _dot(a, b, dims=NN):
    return lax.dot_general(a.astype(BF16), b.astype(BF16), dims, preferred_element_type=F32)


def _rms(x, g):
    return x * lax.rsqrt(jnp.mean(x * x, axis=-1, keepdims=True) + NORM_EPS) * g


ANY = pl.BlockSpec(memory_space=pl.ANY)


class _Side:
    def __init__(self, ins, out_shapes, n_sem, make):
        self.ins, self.out_shapes, self.n_sem, self.make = list(ins), list(out_shapes), n_sem, make

    def sems(self):
        return [pltpu.SemaphoreType.DMA((self.n_sem,)), pltpu.SemaphoreType.DMA((self.n_sem,))]


def _hosted(body, side, n_in, n_out, grid):
    if side is None:
        return body
    ns_in, ns_out = len(side.ins), len(side.out_shapes)

    def wrapped(*refs):
        ins, refs = refs[:n_in], refs[n_in:]
        s_ins, refs = refs[:ns_in], refs[ns_in:]
        outs, refs = refs[:n_out], refs[n_out:]
        s_outs, refs = refs[:ns_out], refs[ns_out:]
        scratch, sems = refs[:-2], refs[-2:]
        ids = [pl.program_id(d) for d in range(len(grid))]
        first = functools.reduce(jnp.logical_and, [i == 0 for i in ids])
        last = functools.reduce(jnp.logical_and, [i == n - 1 for i, n in zip(ids, grid)])

        @pl.when(first)
        def _():
            for cp in side.make(s_ins, s_outs, *sems):
                cp.start()

        body(*ins, *outs, *scratch)

        @pl.when(last)
        def _():
            for cp in side.make(s_ins, s_outs, *sems):
                cp.wait()

    return wrapped


def _side_args(side):
    if side is None:
        return [], [], [], [], []
    return ([ANY] * len(side.ins), [ANY] * len(side.out_shapes), side.out_shapes, side.sems(), side.ins)


def _split_side(res, n_out, side):
    res = list(res)
    main = res[0] if n_out == 1 else res[:n_out]
    return main if side is None else (main, res[n_out:])


def _mm(a, b, mode, name, *, epi=None, extras=(), fulls=(), out_dtypes=(F32,), sums=(), tm=1024, tn=1024, tk=1024,
        side=None):
    if mode == "nn":
        (m, k), (k2, n) = a.shape, b.shape
    elif mode == "nt":
        (m, k), (n, k2) = a.shape, b.shape
    else:
        (k, m), (k2, n) = a.shape, b.shape
    assert k == k2, (name, a.shape, b.shape)
    tm, tn, tk = min(tm, m), min(tn, n), min(tk, k)
    assert m % tm == 0 and n % tn == 0 and k % tk == 0, (name, m, n, k)
    nk = k // tk
    dims = {"nn": NN, "nt": NT, "tn": TN}[mode]
    if mode == "tn":
        a_spec = pl.BlockSpec((tk, tm), lambda i, j, kk: (kk, i))
    else:
        a_spec = pl.BlockSpec((tm, tk), lambda i, j, kk: (i, kk))
    if mode == "nt":
        b_spec = pl.BlockSpec((tn, tk), lambda i, j, kk: (j, kk))
    else:
        b_spec = pl.BlockSpec((tk, tn), lambda i, j, kk: (kk, j))
    mn_spec = pl.BlockSpec((tm, tn), lambda i, j, kk: (i, j))
    n_ex, n_full, n_out, n_sum = len(extras), len(fulls), len(out_dtypes), len(sums)
    n_in = 2 + n_ex + n_full

    def body(*refs):
        a_ref, b_ref = refs[:2]
        ex = refs[2:n_in]
        outs = refs[n_in:n_in + n_out]
        sum_refs = refs[n_in + n_out:n_in + n_out + n_sum]
        kk = pl.program_id(2)
        first_tile = jnp.logical_and(pl.program_id(0) == 0, pl.program_id(1) == 0)

        def finish(r):
            vals = epi(r, *[e[...] for e in ex]) if epi is not None else (r,)
            if n_sum:
                vals, parts = vals

                @pl.when(first_tile)
                def _():
                    for sr in sum_refs:
                        sr[...] = jnp.zeros_like(sr)

                for sr, p in zip(sum_refs, parts):
                    sr[...] += p
            for o, v in zip(outs, vals):
                o[...] = v.astype(o.dtype)

        if nk == 1:
            finish(_dot(a_ref[...], b_ref[...], dims))
        else:
            acc = refs[n_in + n_out + n_sum]

            @pl.when(kk == 0)
            def _():
                acc[...] = jnp.zeros_like(acc)

            acc[...] += _dot(a_ref[...], b_ref[...], dims)

            @pl.when(kk == nk - 1)
            def _():
                finish(acc[...])

    grid = (m // tm, n // tn, nk)
    whole = lambda shape: pl.BlockSpec(shape, lambda i, j, kk: (0,) * len(shape))
    s_in, s_out, s_shape, s_scratch, s_ops = _side_args(side)
    seq = bool(side) or n_sum > 0
    res = pl.pallas_call(
        _hosted(body, side, n_in, n_out + n_sum, grid), name=name, grid=grid,
        in_specs=[a_spec, b_spec] + [mn_spec] * n_ex + [whole(f.shape) for f in fulls] + s_in,
        out_specs=[mn_spec] * n_out + [whole(shape) for shape in sums] + s_out,
        out_shape=[jax.ShapeDtypeStruct((m, n), dt) for dt in out_dtypes]
        + [jax.ShapeDtypeStruct(shape, F32) for shape in sums] + s_shape,
        scratch_shapes=([pltpu.VMEM((tm, tn), F32)] if nk > 1 else []) + s_scratch,
        compiler_params=_params(("arbitrary",) * 3 if seq else ("parallel", "parallel", "arbitrary")),
    )(a, b, *extras, *fulls, *s_ops)
    return _split_side(res, n_out + n_sum, side)


def _row_tile(s, target):
    if s <= target:
        return s
    return max(t for t in range(16, target + 1, 16) if s % t == 0)


def _rw(fn, rows, fulls, row_out, acc_out, name, tm=512, side=None):
    cols = [r[1:] if isinstance(r, tuple) else (r.shape[1], 0) for r in rows]
    rows = [r[0] if isinstance(r, tuple) else r for r in rows]
    s = rows[0].shape[0]
    tm = _row_tile(s, tm)
    nr, nf, nro, nao = len(rows), len(fulls), len(row_out), len(acc_out)

    def body(*refs):
        r = refs[:nr]
        f = refs[nr:nr + nf]
        ro = refs[nr + nf:nr + nf + nro]
        ao = refs[nr + nf + nro:]
        outs, accs = fn(*[x[...] for x in r], *[x[...] for x in f])
        for o, v in zip(ro, outs):
            o[...] = v.astype(o.dtype)
        if nao:
            @pl.when(pl.program_id(0) == 0)
            def _():
                for a in ao:
                    a[...] = jnp.zeros_like(a)

            for a, v in zip(ao, accs):
                a[...] += v

    full_spec = lambda shape: pl.BlockSpec(shape, lambda i: (0,) * len(shape))
    s_in, s_out, s_shape, s_scratch, s_ops = _side_args(side)
    res = pl.pallas_call(
        _hosted(body, side, nr + nf, nro + nao, (s // tm,)), name=name, grid=(s // tm,),
        in_specs=[pl.BlockSpec((tm, wd), functools.partial(lambda i, cb: (i, cb), cb=cb)) for wd, cb in cols]
        + [full_spec(x.shape) for x in fulls] + s_in,
        out_specs=[pl.BlockSpec((tm, d), lambda i: (i, 0)) for d, _ in row_out]
        + [full_spec(shape) for shape in acc_out] + s_out,
        out_shape=[jax.ShapeDtypeStruct((s, d), dt) for d, dt in row_out]
        + [jax.ShapeDtypeStruct(shape, F32) for shape in acc_out] + s_shape,
        scratch_shapes=s_scratch,
        compiler_params=_params(("arbitrary",)),
    )(*rows, *fulls, *s_ops)
    res = list(res)
    return res if side is None else (res[:nro + nao], res[nro + nao:])


def _norm_fwd(x, g, name, side=None):
    res = _rw(lambda xt, gt: ((_rms(xt, gt),), ()), [x], [g], [(x.shape[1], BF16)], [], name, side=side)
    return res[0] if side is None else (res[0][0], res[1])


def _norm_bwd(x, g, dh, dres, name, side=None):
    def fn(xt, dht, drt, gt):
        _, vjp = jax.vjp(_rms, xt, gt)
        dx, dg = vjp(dht)
        return (dx + drt,), (dg,)

    return _rw(fn, [x, dh, dres], [g], [(x.shape[1], F32)], [g.shape], name, side=side)


def _rms_groups(x, g, scale):
    lo = lax.broadcasted_iota(jnp.int32, (1, LANES), 1) < SB_HEAD_DIM
    x2 = x * x
    outs = []
    for cb in range(x.shape[1] // LANES):
        sl = slice(cb * LANES, (cb + 1) * LANES)
        s_lo = jnp.sum(jnp.where(lo, x2[:, sl], 0.0), axis=-1, keepdims=True)
        s_hi = jnp.sum(jnp.where(lo, 0.0, x2[:, sl]), axis=-1, keepdims=True)
        r = jnp.where(lo, lax.rsqrt(s_lo * (1.0 / SB_HEAD_DIM) + NORM_EPS),
                      lax.rsqrt(s_hi * (1.0 / SB_HEAD_DIM) + NORM_EPS))
        outs.append(x[:, sl] * r)
    return jnp.concatenate(outs, axis=1) * g * scale


def _log_sigmoid(z):
    return jnp.minimum(z, 0.0) - jnp.log(1.0 + jnp.exp(-jnp.abs(z)))


def _split_dot(x, u2):
    hi = x.astype(BF16)
    lo = (x - hi.astype(F32)).astype(BF16)
    return jnp.dot(jnp.concatenate([hi, lo], axis=1), u2, preferred_element_type=F32)


def _sb_consts(b):
    row = lax.broadcasted_iota(jnp.int32, (b, b), 0)
    col = lax.broadcasted_iota(jnp.int32, (b, b), 1)
    tri = col < row
    u_after = (row > col).astype(BF16)
    u_from = (row >= col).astype(BF16)
    stack = lambda u: jnp.concatenate([u, u], axis=0)
    lane_lo = lax.broadcasted_iota(jnp.int32, (b, LANES), 1) < SB_HEAD_DIM
    return tri, stack(u_after), stack(u_from), lane_lo


def _sb_scores(qh, kb, a_run, tri, u2_after, diag):
    z = lax.dot_general(qh, kb, NT, preferred_element_type=F32)
    lb = _log_sigmoid(z)
    l = lb - z
    if diag:
        l = jnp.where(tri, l, 0.0)
    w = jnp.exp(lb + (a_run + _split_dot(l, u2_after)))
    if diag:
        w = jnp.where(tri, w, 0.0)
    return lb, l, w


def _sb_walk(qi, carry, step):
    def cond(state):
        n, c = state
        return jnp.logical_and(n <= qi, jnp.max(jnp.maximum(c[0], c[1])) > SB_UNDERFLOW)

    def body(state):
        n, c = state
        return n + 1, step(n, c)

    return lax.while_loop(cond, body, (jnp.int32(1), carry))[1]


def _two_heads(x, lane_lo):
    zero = jnp.zeros_like(x)
    return jnp.where(lane_lo, x, zero), jnp.where(lane_lo, zero, x)


def _sb_fwd(qs, ks, v, name, side=None):
    s, width = qs.shape
    b = min(SB_BLOCK, s)

    def body(q_ref, k_ref, v_ref, o_ref):
        qi = pl.program_id(1)
        tri, u2_after, _, lane_lo = _sb_consts(b)
        q_a, q_b = _two_heads(q_ref[...], lane_lo)

        def step(n, carry, diag):
            a_a, a_b, acc = carry
            off = pl.multiple_of((qi - n) * b, b)
            kb = k_ref[pl.ds(off, b), :]
            v_a, v_b = _two_heads(v_ref[pl.ds(off, b), :], lane_lo)
            _, l_a, w_a = _sb_scores(q_a, kb, a_a, tri, u2_after, diag)
            _, l_b, w_b = _sb_scores(q_b, kb, a_b, tri, u2_after, diag)
            acc = acc + jnp.dot(jnp.concatenate([w_a.astype(BF16), w_b.astype(BF16)], axis=1),
                                jnp.concatenate([v_a, v_b], axis=0), preferred_element_type=F32)
            return (a_a + jnp.sum(l_a, axis=1, keepdims=True), a_b + jnp.sum(l_b, axis=1, keepdims=True), acc)

        zero = jnp.zeros((b, 1), F32)
        carry = step(0, (zero, zero, jnp.zeros((b, LANES), F32)), True)
        carry = _sb_walk(qi, carry, lambda n, c: step(n, c, False))
        o_ref[...] = carry[2]

    blk = pl.BlockSpec((b, LANES), lambda hp, i: (i, hp))
    full = pl.BlockSpec((s, LANES), lambda hp, i: (0, hp))
    grid = (width // LANES, s // b)
    s_in, s_out, s_shape, s_scratch, s_ops = _side_args(side)
    res = pl.pallas_call(
        _hosted(body, side, 3, 1, grid), name=name, grid=grid,
        in_specs=[blk, full, full] + s_in, out_specs=[blk] + s_out,
        out_shape=[jax.ShapeDtypeStruct((s, width), F32)] + s_shape, scratch_shapes=s_scratch,
        compiler_params=_params(("arbitrary", "arbitrary")),
    )(qs, ks, v, *s_ops)
    return _split_side(res, 1, side)


def _sb_bwd(qs, ks, v, out, dout, name, side=None):
    s, width = qs.shape
    b = min(SB_BLOCK, s)
    nkb = s // b

    def body(q_ref, k_ref, v_ref, o_ref, do_ref, dq_ref, dkt_ref, dvt_ref):
        qi = pl.program_id(1)

        @pl.when(qi == 0)
        def _():
            dkt_ref[...] = jnp.zeros_like(dkt_ref)
            dvt_ref[...] = jnp.zeros_like(dvt_ref)

        tri, u2_after, u2_from, lane_lo = _sb_consts(b)
        q_a, q_b = _two_heads(q_ref[...], lane_lo)
        dob = do_ref[...].astype(BF16)
        do_a, do_b = _two_heads(dob, lane_lo)
        prod = dob.astype(F32) * o_ref[...]
        d_a = jnp.sum(jnp.where(lane_lo, prod, 0.0), axis=1, keepdims=True)
        d_b = jnp.sum(jnp.where(lane_lo, 0.0, prod), axis=1, keepdims=True)
        tr = lambda x: jnp.transpose(x.astype(F32)).astype(BF16)
        qt = jnp.concatenate([tr(q_a), tr(q_b)], axis=1)
        dot_ = jnp.concatenate([tr(do_a), tr(do_b)], axis=1)

        def head(qh, doh, kb, vb, a_run, d_rem, diag):
            lb, l, w = _sb_scores(qh, kb, a_run, tri, u2_after, diag)
            wb = w.astype(BF16)
            g = lax.dot_general(doh, vb, NT, preferred_element_type=F32) * wb.astype(F32)
            g_before = d_rem - _split_dot(g, u2_from)
            dz = g - (g + g_before) * jnp.exp(lb)
            if diag:
                dz = jnp.where(tri, dz, 0.0)
            return (dz.astype(BF16), wb, a_run + jnp.sum(l, axis=1, keepdims=True),
                    d_rem - jnp.sum(g, axis=1, keepdims=True))

        def step(n, carry, diag):
            a_a, a_b, r_a, r_b, dq = carry
            jb = qi - n
            off = pl.multiple_of(jb * b, b)
            kb = k_ref[pl.ds(off, b), :]
            vb = v_ref[pl.ds(off, b), :]
            k_a, k_b = _two_heads(kb, lane_lo)
            dz_a, w_a, a_a, r_a = head(q_a, do_a, kb, vb, a_a, r_a, diag)
            dz_b, w_b, a_b, r_b = head(q_b, do_b, kb, vb, a_b, r_b, diag)
            dq = dq + jnp.dot(jnp.concatenate([dz_a, dz_b], axis=1), jnp.concatenate([k_a, k_b], axis=0),
                              preferred_element_type=F32)
            dkt_ref[0, jb] += jnp.dot(qt, jnp.concatenate([dz_a, dz_b], axis=0), preferred_element_type=F32)
            dvt_ref[0, jb] += jnp.dot(dot_, jnp.concatenate([w_a, w_b], axis=0), preferred_element_type=F32)
            return a_a, a_b, r_a, r_b, dq

        zero = jnp.zeros((b, 1), F32)
        carry = step(0, (zero, zero, d_a, d_b, jnp.zeros((b, LANES), F32)), True)
        carry = _sb_walk(qi, carry, lambda n, c: step(n, c, False))
        dq_ref[...] = carry[4]

    blk = pl.BlockSpec((b, LANES), lambda hp, i: (i, hp))
    full = pl.BlockSpec((s, LANES), lambda hp, i: (0, hp))
    acc = pl.BlockSpec((1, nkb, LANES, b), lambda hp, i: (hp, 0, 0, 0))
    grid = (width // LANES, nkb)
    s_in, s_out, s_shape, s_scratch, s_ops = _side_args(side)
    res = pl.pallas_call(
        _hosted(body, side, 5, 3, grid), name=name, grid=grid,
        in_specs=[blk, full, full, blk, blk] + s_in, out_specs=[blk, acc, acc] + s_out,
        out_shape=[jax.ShapeDtypeStruct((s, width), F32)]
        + [jax.ShapeDtypeStruct((width // LANES, nkb, LANES, b), F32)] * 2 + s_shape,
        scratch_shapes=s_scratch,
        compiler_params=_params(("arbitrary", "arbitrary")),
    )(qs, ks, v, out, dout, *s_ops)
    return _split_side(res, 3, side)


def _from_key_blocks(t):
    hp, nkb, lanes, b = t.shape
    return jnp.transpose(t, (1, 3, 0, 2)).reshape(nkb * b, hp * lanes)


def _cmul(xr, xi, yr, yi):
    return xr * yr - xi * yi, xr * yi + xi * yr


def _scan_consts(ar, ai, reverse, lc):
    rowi = lax.broadcasted_iota(jnp.int32, (SUBLANES, lc), 0)
    pows = [(ar, ai)]
    for _ in range(SUBLANES - 1):
        pows.append(_cmul(*pows[-1], ar, ai))
    steps = []
    for d in (1, 2, 4):
        keep = (rowi < SUBLANES - d) if reverse else (rowi >= d)
        pr, pi = pows[d - 1]
        steps.append((SUBLANES - d if reverse else d, jnp.where(keep, pr, 0.0), jnp.where(keep, pi, 0.0)))
    cr = jnp.zeros((SUBLANES, lc), F32)
    ci = jnp.zeros((SUBLANES, lc), F32)
    for r in range(SUBLANES):
        pr, pi = pows[SUBLANES - 1 - r] if reverse else pows[r]
        cr = jnp.where(rowi == r, pr, cr)
        ci = jnp.where(rowi == r, pi, ci)
    return steps, cr, ci


def _scan_tile(xr, xi, steps, pr, pi, cr, ci):
    for shift, ar, ai in steps:
        rr = pltpu.roll(xr, shift, 0)
        ri = pltpu.roll(xi, shift, 0)
        xr, xi = xr + ar * rr - ai * ri, xi + ar * ri + ai * rr
    return xr + pr * cr - pi * ci, xi + pr * ci + pi * cr


def _ssm_fwd(u, acat, bsup, csup, d_skip, name, tt=1024, side=None):
    s = u.shape[0]
    lc = SCAN_LANES
    tt = min(tt, s)
    nl, nt = N_STATE // lc, s // tt

    def body(u_ref, a_ref, b_ref, c_ref, d_ref, s_ref, y0_ref, y1_ref, carry):
        @pl.when(pl.program_id(1) == 0)
        def _():
            carry[...] = jnp.zeros_like(carry)

        ut = u_ref[...]
        s_ref[...] = _dot(ut, b_ref[0])
        steps, pr, pi = _scan_consts(a_ref[:, :lc], a_ref[:, lc:], False, lc)

        def tile(i, c):
            off = pl.multiple_of(i * SUBLANES, SUBLANES)
            xr, xi = _scan_tile(s_ref[pl.ds(off, SUBLANES), :lc], s_ref[pl.ds(off, SUBLANES), lc:],
                                steps, pr, pi, c[0], c[1])
            s_ref[pl.ds(off, SUBLANES), :lc] = xr
            s_ref[pl.ds(off, SUBLANES), lc:] = xi
            return (jnp.broadcast_to(xr[SUBLANES - 1:, :], (SUBLANES, lc)),
                    jnp.broadcast_to(xi[SUBLANES - 1:, :], (SUBLANES, lc)))

        cr, ci = lax.fori_loop(0, tt // SUBLANES, tile, (carry[:, :lc], carry[:, lc:]))
        carry[:, :lc] = cr
        carry[:, lc:] = ci
        y0 = _dot(s_ref[...], c_ref[0], NT) + d_ref[...] * ut
        y0_ref[...] = y0
        y1_ref[...] = jax.nn.gelu(y0)

    chan = pl.BlockSpec((tt, LANES), lambda j, c: (c, j))
    sup = pl.BlockSpec((1, LANES, 2 * lc), lambda j, c: (j, 0, 0))
    s_in, s_out, s_shape, s_scratch, s_ops = _side_args(side)
    res = pl.pallas_call(
        _hosted(body, side, 5, 3, (nl, nt)), name=name, grid=(nl, nt),
        in_specs=[chan, pl.BlockSpec((1, 2 * lc), lambda j, c: (0, j)), sup, sup,
                  pl.BlockSpec((1, LANES), lambda j, c: (0, j))] + s_in,
        out_specs=[pl.BlockSpec((tt, 2 * lc), lambda j, c: (c, j)), chan, chan] + s_out,
        out_shape=[jax.ShapeDtypeStruct((s, 2 * N_STATE), F32), jax.ShapeDtypeStruct((s, SSM_WIDTH), F32),
                   jax.ShapeDtypeStruct((s, SSM_WIDTH), F32)] + s_shape,
        scratch_shapes=[pltpu.VMEM((SUBLANES, 2 * lc), F32)] + s_scratch,
        compiler_params=_params(("arbitrary", "arbitrary")),
    )(u, acat, bsup, csup, d_skip, *s_ops)
    return _split_side(res, 3, side)


def _ssm_bwd(dy0, states, u, acat, bsup, csup, d_skip, name, tt=1024, side=None):
    s = u.shape[0]
    lc = SCAN_LANES
    tt = min(tt, s)
    nl, nt = N_STATE // lc, s // tt
    nt8 = tt // SUBLANES

    def body(dy_ref, s_ref, sp_ref, u_ref, a_ref, b_ref, c_ref, d_ref,
             du_ref, da_ref, db_ref, dc_ref, dd_ref, lam_ref, carry):
        c = pl.program_id(1)

        @pl.when(c == 0)
        def _():
            carry[...] = jnp.zeros_like(carry)
            for r in (da_ref, db_ref, dc_ref, dd_ref):
                r[...] = jnp.zeros_like(r)

        dy = dy_ref[...]
        ut = u_ref[...]
        lam_ref[...] = _dot(dy, c_ref[0])
        steps, pr, pi = _scan_consts(a_ref[:, :lc], -a_ref[:, lc:], True, lc)
        rowi = lax.broadcasted_iota(jnp.int32, (SUBLANES, lc), 0)
        first_chunk = c == nt - 1

        def tile(i, carry_v):
            cr, ci, dar, dai = carry_v
            t = nt8 - 1 - i
            off = pl.multiple_of(t * SUBLANES, SUBLANES)
            lr, li = _scan_tile(lam_ref[pl.ds(off, SUBLANES), :lc], lam_ref[pl.ds(off, SUBLANES), lc:],
                                steps, pr, pi, cr, ci)
            lam_ref[pl.ds(off, SUBLANES), :lc] = lr
            lam_ref[pl.ds(off, SUBLANES), lc:] = li
            offp = pl.multiple_of(jnp.maximum(t - 1, 0) * SUBLANES, SUBLANES)
            in_chunk = t > 0
            use = jnp.logical_or(in_chunk, jnp.logical_not(first_chunk))
            prev_r = jnp.where(in_chunk, s_ref[pl.ds(offp, SUBLANES), :lc], sp_ref[:, :lc])
            prev_i = jnp.where(in_chunk, s_ref[pl.ds(offp, SUBLANES), lc:], sp_ref[:, lc:])
            last_r = jnp.where(use, jnp.broadcast_to(prev_r[SUBLANES - 1:, :], (SUBLANES, lc)), 0.0)
            last_i = jnp.where(use, jnp.broadcast_to(prev_i[SUBLANES - 1:, :], (SUBLANES, lc)), 0.0)
            sr = jnp.where(rowi == 0, last_r, pltpu.roll(s_ref[pl.ds(off, SUBLANES), :lc], 1, 0))
            si = jnp.where(rowi == 0, last_i, pltpu.roll(s_ref[pl.ds(off, SUBLANES), lc:], 1, 0))
            dar = dar + lr * sr + li * si
            dai = dai + li * sr - lr * si
            return (jnp.broadcast_to(lr[:1, :], (SUBLANES, lc)), jnp.broadcast_to(li[:1, :], (SUBLANES, lc)),
                    dar, dai)

        zero = jnp.zeros((SUBLANES, lc), F32)
        cr, ci, dar, dai = lax.fori_loop(0, nt8, tile, (carry[:, :lc], carry[:, lc:], zero, zero))
        carry[:, :lc] = cr
        carry[:, lc:] = ci
        da_ref[:, :lc] += dar
        da_ref[:, lc:] += dai
        lam = lam_ref[...].astype(BF16)
        du_ref[...] = _dot(lam, b_ref[0], NT) + d_ref[...] * dy
        db_ref[0] += _dot(ut, lam, TN)
        dc_ref[0] += _dot(dy, s_ref[...], TN)
        dd_ref[...] += jnp.sum(dy * ut, axis=0, keepdims=True)

    rev = lambda j, c: (nt - 1 - c, j)
    chan = pl.BlockSpec((tt, LANES), rev)
    sup = pl.BlockSpec((1, LANES, 2 * lc), lambda j, c: (j, 0, 0))
    row = pl.BlockSpec((1, LANES), lambda j, c: (0, j))
    s_in, s_out, s_shape, s_scratch, s_ops = _side_args(side)
    res = pl.pallas_call(
        _hosted(body, side, 8, 5, (nl, nt)), name=name, grid=(nl, nt),
        in_specs=[chan, pl.BlockSpec((tt, 2 * lc), rev),
                  pl.BlockSpec((SUBLANES, 2 * lc), lambda j, c: (jnp.maximum((nt - 1 - c) * nt8 - 1, 0), j)),
                  chan, pl.BlockSpec((1, 2 * lc), lambda j, c: (0, j)), sup, sup, row] + s_in,
        out_specs=[chan, pl.BlockSpec((SUBLANES, 2 * lc), lambda j, c: (0, j)), sup, sup, row] + s_out,
        out_shape=[jax.ShapeDtypeStruct((s, SSM_WIDTH), F32), jax.ShapeDtypeStruct((SUBLANES, 2 * N_STATE), F32),
                   jax.ShapeDtypeStruct(bsup.shape, F32), jax.ShapeDtypeStruct(csup.shape, F32),
                   jax.ShapeDtypeStruct((1, SSM_WIDTH), F32)] + s_shape,
        scratch_shapes=[pltpu.VMEM((tt, 2 * lc), F32), pltpu.VMEM((SUBLANES, 2 * lc), F32)] + s_scratch,
        compiler_params=_params(("arbitrary", "arbitrary")),
    )(dy0, states, states, u, acat, bsup, csup, d_skip, *s_ops)
    return _split_side(res, 5, side)


def _state_cols(xr, xi):
    lead = xr.shape[:-1]
    nl = N_STATE // SCAN_LANES
    both = jnp.stack([xr.reshape(lead + (nl, SCAN_LANES)), xi.reshape(lead + (nl, SCAN_LANES))], axis=-2)
    return both.reshape(lead + (2 * N_STATE,))


def _ssm_mats(a_re, a_im, log_dt, b_re, b_im, c_re, c_im):
    dt = jnp.exp(log_dt)[:, None]
    lr, li = a_re * dt, a_im * dt
    e = jnp.exp(lr)
    abar_r, abar_i = e * jnp.cos(li), e * jnp.sin(li)
    den = a_re * a_re + a_im * a_im
    coef_r = ((abar_r - 1.0) * a_re + abar_i * a_im) / den
    coef_i = (abar_i * a_re - (abar_r - 1.0) * a_im) / den
    bbar_r = coef_r[..., None] * b_re - coef_i[..., None] * b_im
    bbar_i = coef_r[..., None] * b_im + coef_i[..., None] * b_re
    nl = N_STATE // SCAN_LANES
    gpb = SSM_GROUPS // nl
    eye = jnp.eye(gpb, dtype=bool)[None, :, None, :, None]

    def sup(m_r, m_i):
        def one(m):
            m = m.reshape(nl, gpb, SSM_GROUP, 1, SSM_STATE)
            return jnp.where(eye, m, 0.0).reshape(nl, gpb * SSM_GROUP, SCAN_LANES)
        return jnp.concatenate([one(m_r), one(m_i)], axis=-1)

    acat = _state_cols(abar_r.reshape(1, N_STATE), abar_i.reshape(1, N_STATE))
    bsup = sup(jnp.transpose(bbar_r, (0, 2, 1)), jnp.transpose(bbar_i, (0, 2, 1)))
    csup = sup(c_re, -c_im)
    return acat, bsup, csup


def _mem_fwd(mem, g_mem, w_kv, g_k, name):
    ml = mem.shape[0]

    def body(mem_ref, gm_ref, w_ref, gk_ref, memn_ref, kv_ref, kn_ref, vv_ref):
        memn = _rms(mem_ref[...], gm_ref[...])
        memn_ref[...] = memn.astype(BF16)
        kv = _dot(memn, w_ref[...])
        kv_ref[...] = kv
        for hh in range(XA_HEADS):
            sl = slice(hh * XA_HEAD_DIM, (hh + 1) * XA_HEAD_DIM)
            kn_ref[:, sl] = _rms(kv[:, sl], gk_ref[...]).astype(BF16)
        vv_ref[...] = kv[:, XA_WIDTH:].astype(BF16)

    return pl.pallas_call(
        body, name=name,
        out_shape=[jax.ShapeDtypeStruct((ml, D_MODEL), BF16), jax.ShapeDtypeStruct((ml, 2 * XA_WIDTH), F32),
                   jax.ShapeDtypeStruct((ml, XA_WIDTH), BF16), jax.ShapeDtypeStruct((ml, XA_WIDTH), BF16)],
        compiler_params=_params(),
    )(mem, g_mem, w_kv, g_k)


def _mem_bwd(mem, g_mem, memn, w_kv, kv, g_k, dkn, dvv, name):
    def body(mem_ref, gm_ref, memn_ref, w_ref, kv_ref, gk_ref, dkn_ref, dvv_ref, dw_ref, dgm_ref, dgk_ref):
        kv = kv_ref[...]
        dgk = jnp.zeros(dgk_ref.shape, F32)
        parts = []
        for hh in range(XA_HEADS):
            sl = slice(hh * XA_HEAD_DIM, (hh + 1) * XA_HEAD_DIM)
            _, vjp = jax.vjp(_rms, kv[:, sl], gk_ref[...])
            dk, dg = vjp(dkn_ref[:, sl])
            parts.append(dk)
            dgk = dgk + dg
        dgk_ref[...] = dgk
        dkv = jnp.concatenate(parts + [dvv_ref[...]], axis=1)
        dw_ref[...] = _dot(memn_ref[...], dkv, TN)
        dmemn = _dot(dkv, w_ref[...], NT)
        _, vjp = jax.vjp(_rms, mem_ref[...], gm_ref[...])
        dgm_ref[...] = vjp(dmemn)[1]

    return pl.pallas_call(
        body, name=name,
        out_shape=[jax.ShapeDtypeStruct((D_MODEL, 2 * XA_WIDTH), F32), jax.ShapeDtypeStruct(g_mem.shape, F32),
                   jax.ShapeDtypeStruct(g_k.shape, F32)],
        compiler_params=_params(),
    )(mem, g_mem, memn, w_kv, kv, g_k, dkn, dvv)


def _xa_head(qx_h, g_q, kn_h, vv_h):
    qn = _rms(qx_h, g_q)
    sc = _dot(qn, kn_h, NT) * (XA_HEAD_DIM ** -0.5)
    sc = sc - jnp.max(sc, axis=-1, keepdims=True)
    e = jnp.exp(sc)
    p = e / jnp.sum(e, axis=-1, keepdims=True)
    return qn, p


def _xa_fwd(qx, g_q, kn, vv, name):
    def fn(qt, gq, knt, vvt):
        outs = []
        for hh in range(XA_HEADS):
            sl = slice(hh * XA_HEAD_DIM, (hh + 1) * XA_HEAD_DIM)
            _, p = _xa_head(qt[:, sl], gq, knt[:, sl], vvt[:, sl])
            outs.append(_dot(p, vvt[:, sl]))
        return (jnp.concatenate(outs, axis=1),), ()

    return _rw(fn, [qx], [g_q, kn, vv], [(XA_WIDTH, BF16)], [], name)[0]


def _xa_bwd(qx, g_q, kn, vv, do, name):
    def fn(qt, dot_, gq, knt, vvt):
        dqs, dks, dvs = [], [], []
        dgq = jnp.zeros_like(gq)
        for hh in range(XA_HEADS):
            sl = slice(hh * XA_HEAD_DIM, (hh + 1) * XA_HEAD_DIM)
            qn, p = _xa_head(qt[:, sl], gq, knt[:, sl], vvt[:, sl])
            doh = dot_[:, sl]
            dp = _dot(doh, vvt[:, sl], NT)
            dvs.append(_dot(p, doh, TN))
            ds = p * (dp - jnp.sum(dp * p, axis=-1, keepdims=True)) * (XA_HEAD_DIM ** -0.5)
            dqn = _dot(ds, knt[:, sl])
            dks.append(_dot(ds, qn, TN))
            _, vjp = jax.vjp(_rms, qt[:, sl], gq)
            dq, dg = vjp(dqn)
            dqs.append(dq)
            dgq = dgq + dg
        return ((jnp.concatenate(dqs, axis=1),),
                (jnp.concatenate(dks, axis=1), jnp.concatenate(dvs, axis=1), dgq))

    return _rw(fn, [qx, do], [g_q, kn, vv], [(XA_WIDTH, F32)], [kn.shape, vv.shape, g_q.shape], name)


BIG = [
    ("w_in", (D_MODEL, IN_WIDTH), 1), ("ssm_w_glu", (SSM_WIDTH, SSM_WIDTH), 0), ("w_out", (D_MODEL, D_MODEL), 0),
    ("xa_w_q", (D_MODEL, XA_WIDTH), 0), ("xa_w_kv", (D_MODEL, 2 * XA_WIDTH), 0), ("xa_w_o", (XA_WIDTH, D_MODEL), 1),
    ("w_up", (D_MODEL, D_FF), 1), ("w_down", (D_FF, D_MODEL), 0),
]
BIG_INDEX = {n: i for i, (n, _, _) in enumerate(BIG)}


def _shard_shape(shape, axis):
    return tuple(d // N_DEV if i == axis else d for i, d in enumerate(shape))


def _shard_of(ref, axis, d):
    n = ref.shape[axis] // N_DEV
    return ref.at[pl.ds(d * n, n), :] if axis == 0 else ref.at[:, pl.ds(d * n, n)]


def _gather_side(names, shards):
    idxs = [BIG_INDEX[n] for n in names]

    def make(ins, outs, send_sems, recv_sems):
        x, y, c = lax.axis_index("x"), lax.axis_index("y"), lax.axis_index("c")
        cps = []
        for j, i in enumerate(idxs):
            mine = _shard_of(outs[j], BIG[i][2], 4 * x + 2 * y + c)
            cps.append(pltpu.make_async_copy(ins[j], mine, send_sems.at[N_DEV * j]))
            for rel in range(1, N_DEV):
                to = tuple(1 - p if rel >> bit & 1 else p for p, bit in ((x, 2), (y, 1), (c, 0)))
                cps.append(pltpu.make_async_remote_copy(
                    src_ref=ins[j], dst_ref=mine, send_sem=send_sems.at[N_DEV * j + rel],
                    recv_sem=recv_sems.at[N_DEV * j + rel], device_id=to, device_id_type=MESH))
        return cps

    return _Side(shards, [jax.ShapeDtypeStruct(BIG[i][1], BF16) for i in idxs], N_DEV * len(idxs), make)


def _sibling_side(names, grads):
    idxs = [BIG_INDEX[n] for n in names]

    def make(ins, outs, send_sems, recv_sems):
        x, y, c = lax.axis_index("x"), lax.axis_index("y"), lax.axis_index("c")
        return [pltpu.make_async_remote_copy(
            src_ref=_shard_of(ins[j], BIG[i][2], 2 * k + (1 - c)), dst_ref=outs[j].at[k],
            send_sem=send_sems.at[4 * j + k], recv_sem=recv_sems.at[4 * j + k], device_id=(x, y, 1 - c),
            device_id_type=MESH) for j, i in enumerate(idxs) for k in range(4)]

    shapes = [jax.ShapeDtypeStruct((4,) + _shard_shape(BIG[i][1], BIG[i][2]), F32) for i in idxs]
    return _Side(grads, shapes, 4 * len(idxs), make)


def _chips_side(parts):
    def make(ins, outs, send_sems, recv_sems):
        x, y, c = lax.axis_index("x"), lax.axis_index("y"), lax.axis_index("c")
        chips = [(1 - x, y), (x, 1 - y), (1 - x, 1 - y)]
        return [pltpu.make_async_remote_copy(
            src_ref=ins[j].at[2 * cx + cy], dst_ref=outs[j].at[r], send_sem=send_sems.at[3 * j + r],
            recv_sem=recv_sems.at[3 * j + r], device_id=(cx, cy, c), device_id_type=MESH)
            for r, (cx, cy) in enumerate(chips) for j in range(len(parts))]

    return _Side(parts, [jax.ShapeDtypeStruct((3,) + p.shape[1:], p.dtype) for p in parts], 3 * len(parts), make)


def _reduce_add(grad, recv, axis, core, name):
    rs, cs = recv.shape[1:]
    rt = _row_tile(rs, 256)
    nt = rs // rt

    def body(c_ref, g_ref, r_ref, p_ref, pb_ref):
        sm = g_ref[...] + r_ref[0]
        p_ref[0] = sm
        pb_ref[0] = sm.astype(BF16)

    if axis == 0:
        g_spec = pl.BlockSpec((rt, cs), lambda k, t, c_ref: ((2 * k + c_ref[0]) * nt + t, 0))
    else:
        g_spec = pl.BlockSpec((rt, cs), lambda k, t, c_ref: (t, 2 * k + c_ref[0]))
    slab = pl.BlockSpec((1, rt, cs), lambda k, t, c_ref: (k, t, 0))
    return pl.pallas_call(
        body, name=name,
        grid_spec=pltpu.PrefetchScalarGridSpec(num_scalar_prefetch=1, grid=(4, nt), in_specs=[g_spec, slab],
                                               out_specs=[slab, slab]),
        out_shape=[jax.ShapeDtypeStruct(recv.shape, F32), jax.ShapeDtypeStruct(recv.shape, BF16)],
        compiler_params=_params(("parallel", "parallel")),
    )(core, grad, recv)


def _all_gather(block, name):
    m_per, n = block.shape

    def body(x_ref, out_ref, send_sems, recv_sems, local_sem):
        x, y, c = lax.axis_index("x"), lax.axis_index("y"), lax.axis_index("c")
        me, sibling = (x, y, c), (x, y, 1 - c)
        chips = [(1 - x, y), (x, 1 - y), (1 - x, 1 - y)]

        def rows(px, py, pc):
            return out_ref.at[pl.ds((4 * px + 2 * py + pc) * m_per, m_per), :]

        def copy(k, blk, to, src=None):
            return pltpu.make_async_remote_copy(
                src_ref=rows(*blk) if src is None else src, dst_ref=rows(*blk),
                send_sem=send_sems.at[k], recv_sem=recv_sems.at[k], device_id=to, device_id_type=MESH)

        mine = pltpu.make_async_copy(x_ref, rows(*me), local_sem)
        mine.start()
        first = [copy(0, me, sibling, src=x_ref)]
        first += [copy(1 + j, me, (*chip, c), src=x_ref) for j, chip in enumerate(chips)]
        for cp in first:
            cp.start()
        passed = [copy(4 + j, (*chip, c), sibling) for j, chip in enumerate(chips)]
        for j, chip in enumerate(chips):
            copy(1 + j, (*chip, c), me).wait_recv()
            passed[j].start()
        copy(0, sibling, me).wait_recv()
        for j, chip in enumerate(chips):
            copy(4 + j, (*chip, 1 - c), me).wait_recv()
        for cp in first + passed:
            cp.wait_send()
        mine.wait()

    return pl.pallas_call(
        body, name=name, in_specs=[ANY], out_specs=ANY,
        out_shape=jax.ShapeDtypeStruct((N_DEV * m_per, n), block.dtype),
        scratch_shapes=[pltpu.SemaphoreType.DMA((7,)), pltpu.SemaphoreType.DMA((7,)), pltpu.SemaphoreType.DMA],
    )(block)


def _adam_math(w, g, m, v):
    m = ADAM_B1 * m + (1.0 - ADAM_B1) * g
    v = ADAM_B2 * v + (1.0 - ADAM_B2) * (g * g)
    m_hat = m / (1.0 - ADAM_B1 ** ADAM_STEP)
    v_hat = v / (1.0 - ADAM_B2 ** ADAM_STEP)
    delta = -ADAM_LR * (m_hat / (jnp.sqrt(v_hat) + ADAM_EPS) + ADAM_WD * w)
    return delta, m, v


def _adam_sharded(own, recv, w, m, v, chip, name):
    rs, cs = w.shape
    rt = _row_tile(rs, 256)

    def body(chip_ref, p_ref, r_ref, w_ref, m_ref, v_ref, g_out, d_out, m_out, v_out):
        g = p_ref[0] + r_ref[0].astype(F32) + r_ref[1].astype(F32) + r_ref[2].astype(F32)
        d, mn, vn = _adam_math(w_ref[...], g, m_ref[...], v_ref[...])
        g_out[...] = g
        d_out[...] = d
        m_out[...] = mn
        v_out[...] = vn

    tile = pl.BlockSpec((rt, cs), lambda t, chip_ref: (t, 0))
    return pl.pallas_call(
        body, name=name,
        grid_spec=pltpu.PrefetchScalarGridSpec(
            num_scalar_prefetch=1, grid=(rs // rt,),
            in_specs=[pl.BlockSpec((1, rt, cs), lambda t, chip_ref: (chip_ref[0], t, 0)),
                      pl.BlockSpec((3, rt, cs), lambda t, chip_ref: (0, t, 0)), tile, tile, tile],
            out_specs=[tile] * 4),
        out_shape=[jax.ShapeDtypeStruct((rs, cs), F32)] * 4,
        compiler_params=_params(("parallel",)),
    )(chip, own, recv, w, m, v)


SMALL = ["g_mix", "ssm_a_re", "ssm_a_im", "ssm_log_dt", "ssm_b_re", "ssm_b_im", "ssm_c_re", "ssm_c_im", "ssm_d",
         "sb_g_q", "sb_g_k", "g_out_ssm", "g_out_sb", "g_xa", "g_mem", "xa_g_q", "xa_g_k", "g_mlp"]
PACK_TILE = SUBLANES * LANES


def _natural_2d(n):
    return (n // LANES, LANES) if n % LANES == 0 else (1, n)


def _pack_small(arrs):
    parts = []
    for a in arrs:
        flat = a.reshape(-1)
        parts.append(jnp.pad(flat, (0, (-flat.shape[0]) % PACK_TILE)))
    return jnp.concatenate(parts).reshape(-1, LANES)


def _adam_replicated(gathered, sizes, ws, ms, vs, name):
    n_w = len(ws)
    r_dev = gathered.shape[0] // N_DEV
    offs, off = [], 0
    for n in sizes:
        offs.append(off)
        off += (n + PACK_TILE - 1) // PACK_TILE * SUBLANES
    assert off == r_dev

    def body(*refs):
        g_ref = refs[0]
        w_refs, m_refs, v_refs = refs[1:1 + n_w], refs[1 + n_w:1 + 2 * n_w], refs[1 + 2 * n_w:1 + 3 * n_w]
        outs = refs[1 + 3 * n_w:]

        def total(i, shape):
            r, cdim = shape
            acc = g_ref[pl.ds(offs[i], r), :cdim]
            for d in range(1, N_DEV):
                acc = acc + g_ref[pl.ds(d * r_dev + offs[i], r), :cdim]
            return acc

        for i in range(n_w):
            g = total(i, w_refs[i].shape)
            d, mn, vn = _adam_math(w_refs[i][...], g, m_refs[i][...], v_refs[i][...])
            for o, val in zip(outs[4 * i:4 * i + 4], (g, d, mn, vn)):
                o[...] = val
        outs[4 * n_w][...] = total(n_w, (SUBLANES, LANES))

    shapes = [w.shape for w in ws]
    return pl.pallas_call(
        body, name=name,
        out_shape=[jax.ShapeDtypeStruct(shp, F32) for shp in shapes for _ in range(4)]
        + [jax.ShapeDtypeStruct((SUBLANES, LANES), F32)],
        compiler_params=_params(),
    )(gathered, *ws, *ms, *vs)


def _step(x, mem, target, shards, sm, core):
    g, w, sums, reduced = {}, {}, {}, {}

    def gather(names):
        return _gather_side(names, [shards[n] for n in names])

    def to_sibling(names):
        return _sibling_side(names, [g[n] for n in names])

    def add_sibling(names, received):
        for n, r in zip(names, received):
            sums[n] = _reduce_add(g[n], r, BIG[BIG_INDEX[n]][2], core, "reduce_add_" + n)

    def to_chips(names):
        return _chips_side([sums[n][1] for n in names])

    def keep(names, received):
        for n, r in zip(names, received):
            reduced[n] = (sums[n][0], r)

    row = lambda a: a.reshape(1, -1)
    g_mix, g_xa, g_mlp, g_mem = row(sm["g_mix"]), row(sm["g_xa"]), row(sm["g_mlp"]), row(sm["g_mem"])
    g_os, g_ob = row(sm["g_out_ssm"]), row(sm["g_out_sb"])
    sb_gq, sb_gk = jnp.tile(row(sm["sb_g_q"]), (1, SB_HEADS)), jnp.tile(row(sm["sb_g_k"]), (1, SB_HEADS))
    xa_gq, xa_gk = row(sm["xa_g_q"]), row(sm["xa_g_k"])
    d_skip = row(sm["ssm_d"])

    h1, (w["w_in"],) = _norm_fwd(x, g_mix, "norm_mix", side=gather(["w_in"]))
    proj = _mm(h1, w["w_in"], "nn", "in_proj")
    u = proj
    q_raw, k_raw = (proj, SB_WIDTH, 1), (proj, SB_WIDTH, 2)
    v_sb = proj[:, SSM_WIDTH + 2 * SB_WIDTH:].astype(BF16)
    sb_scale = SB_HEAD_DIM ** -0.5
    qk_norm = lambda scale: (lambda xt, gt: ((_rms_groups(xt, gt, scale),), ()))
    qs = _rw(qk_norm(sb_scale), [q_raw], [sb_gq], [(SB_WIDTH, BF16)], [], "sb_qnorm")[0]
    ks = _rw(qk_norm(1.0), [k_raw], [sb_gk], [(SB_WIDTH, BF16)], [], "sb_knorm")[0]
    early = ["ssm_w_glu", "w_out", "xa_w_q", "xa_w_kv", "xa_w_o", "w_up"]
    y_sb, got = _sb_fwd(qs, ks, v_sb, "sb_fwd", side=gather(early))
    w.update(zip(early, got))

    ssm_args = (sm["ssm_a_re"], sm["ssm_a_im"], sm["ssm_log_dt"], sm["ssm_b_re"], sm["ssm_b_im"],
                sm["ssm_c_re"], sm["ssm_c_im"])
    (acat, bsup, csup), mats_vjp = jax.vjp(_ssm_mats, *ssm_args)
    (states, y0, y1), (w["w_down"],) = _ssm_fwd(u, acat, bsup, csup, d_skip, "ssm_fwd", side=gather(["w_down"]))
    z_glu, y_ssm = _mm(y1, w["ssm_w_glu"], "nn", "ssm_glu", epi=lambda r, yt: (r, yt * jax.nn.sigmoid(r)),
                       extras=(y1,), out_dtypes=(F32, F32))

    def cat_norm(a, b, ga, gb):
        return jnp.concatenate([_rms(a, ga), _rms(b, gb)], axis=1)

    ycat = _rw(lambda a, b, ga, gb: ((cat_norm(a, b, ga, gb),), ()), [y_ssm, y_sb], [g_os, g_ob],
               [(D_MODEL, BF16)], [], "norm_out")[0]
    x1 = _mm(ycat, w["w_out"], "nn", "out_proj", epi=lambda r, xt: (r + xt,), extras=(x,))
    h2 = _norm_fwd(x1, g_xa, "norm_xa")
    qx = _mm(h2, w["xa_w_q"], "nn", "xa_q")
    memn, kv, kn_x, vv_x = _mem_fwd(mem, g_mem, w["xa_w_kv"], xa_gk, "xa_mem")
    o_xa = _xa_fwd(qx, xa_gq, kn_x, vv_x, "xa_fwd")
    x2 = _mm(o_xa, w["xa_w_o"], "nn", "xa_o", epi=lambda r, xt: (r + xt,), extras=(x1,))
    h3 = _norm_fwd(x2, g_mlp, "norm_mlp")

    def up_epi(r):
        rl = jnp.maximum(r, 0.0)
        return (rl * rl,)

    r_up = _mm(h3, w["w_up"], "nn", "mlp_up", epi=up_epi, out_dtypes=(BF16,))

    def loss_epi(r, xt, tt):
        d = r + xt - tt
        return (d * (1.0 / D_MODEL),), (jnp.sum(d * d, axis=0, keepdims=True),)

    dx3, sq = _mm(r_up, w["w_down"], "nn", "mlp_down", epi=loss_epi, extras=(x2, target), sums=[(1, D_MODEL)])
    loss = jnp.sum(sq) * (0.5 / D_MODEL)

    def norm_bwd_epi(r, xt, drt, gt):
        _, vjp = jax.vjp(_rms, xt, gt)
        dx_, dg_ = vjp(r)
        return (dx_ + drt,), (dg_,)

    g["w_down"] = _mm(r_up, dx3, "tn", "d_w_down", tk=2048)
    da = _mm(dx3, w["w_down"], "nt", "d_r", epi=lambda r, rt: (r * 2.0 * jnp.sqrt(rt.astype(F32)),), extras=(r_up,),
             out_dtypes=(BF16,))
    g["w_up"] = _mm(h3, da, "tn", "d_w_up", tk=2048)
    mlp = ["w_down", "w_up"]
    (dx2, g["g_mlp"]), got = _mm(da, w["w_up"], "nt", "d_h3", epi=norm_bwd_epi, extras=(x2, dx3), fulls=(g_mlp,),
                                 sums=[g_mlp.shape], side=to_sibling(mlp))
    add_sibling(mlp, got)
    g["xa_w_o"] = _mm(o_xa, dx2, "tn", "d_xa_w_o", tk=2048)
    do_xa = _mm(dx2, w["xa_w_o"], "nt", "d_o_xa")
    dqx, dkn_x, dvv_x, g["xa_g_q"] = _xa_bwd(qx, xa_gq, kn_x, vv_x, do_xa, "xa_bwd")
    g["xa_w_kv"], g["g_mem"], g["xa_g_k"] = _mem_bwd(mem, g_mem, memn, w["xa_w_kv"], kv, xa_gk, dkn_x, dvv_x,
                                                     "xa_mem_bwd")
    g["xa_w_q"] = _mm(h2, dqx, "tn", "d_xa_w_q", tk=2048)
    dx1, g["g_xa"] = _mm(dqx, w["xa_w_q"], "nt", "d_h2", epi=norm_bwd_epi, extras=(x1, dx2), fulls=(g_xa,),
                         sums=[g_xa.shape])
    g["w_out"] = _mm(ycat, dx1, "tn", "d_w_out", tk=2048)
    dycat = _mm(dx1, w["w_out"], "nt", "d_ycat")

    def cat_bwd(a, b, dy, ga, gb):
        _, vjp = jax.vjp(cat_norm, a, b, ga, gb)
        da_, db_, dga, dgb = vjp(dy)
        return (da_, db_), (dga, dgb)

    dy_ssm, dy_sb, g["g_out_ssm"], g["g_out_sb"] = _rw(
        cat_bwd, [y_ssm, y_sb, dycat], [g_os, g_ob], [(SSM_WIDTH, F32), (SB_WIDTH, F32)], [g_os.shape, g_ob.shape],
        "d_norm_out")

    def glu_bwd(dy, yt, zt):
        sg = jax.nn.sigmoid(zt)
        return (dy * sg, dy * yt * sg * (1.0 - sg)), ()

    dy1_a, dz = _rw(glu_bwd, [dy_ssm, y1, z_glu], [], [(SSM_WIDTH, F32), (SSM_WIDTH, BF16)], [], "d_glu")
    g["ssm_w_glu"] = _mm(y1, dz, "tn", "d_w_glu", tk=2048)

    def gelu_bwd_epi(r, da_, y0t):
        _, vjp = jax.vjp(jax.nn.gelu, y0t)
        return (vjp(r + da_)[0],)

    mid = ["w_out", "xa_w_q", "xa_w_kv", "xa_w_o", "ssm_w_glu"]
    dy0, got = _mm(dz, w["ssm_w_glu"], "nt", "d_y1", epi=gelu_bwd_epi, extras=(dy1_a, y0), side=to_sibling(mid))
    add_sibling(mid, got)
    (du, da8, d_bsup, d_csup, g["ssm_d"]), got = _ssm_bwd(dy0, states, u, acat, bsup, csup, d_skip, "ssm_bwd",
                                                          side=to_chips(mlp))
    keep(mlp, got)
    d_acat = jnp.sum(da8, axis=0, keepdims=True)
    for nm, val in zip(("ssm_a_re", "ssm_a_im", "ssm_log_dt", "ssm_b_re", "ssm_b_im", "ssm_c_re", "ssm_c_im"),
                       mats_vjp((d_acat, d_bsup, d_csup))):
        g[nm] = val

    (dqs, dkt, dvt), got = _sb_bwd(qs, ks, v_sb, y_sb, dy_sb, "sb_bwd", side=to_chips(mid))
    keep(mid, got)

    def qk_norm_bwd(scale):
        def fn(xt, dt, gt):
            _, vjp = jax.vjp(lambda a, b_: _rms_groups(a, b_, scale), xt, gt)
            dx_, dg_ = vjp(dt)
            return (dx_,), (dg_,)
        return fn

    dq_raw, dgq = _rw(qk_norm_bwd(sb_scale), [q_raw, dqs], [sb_gq], [(SB_WIDTH, F32)], [sb_gq.shape], "d_sb_qnorm")
    dk_raw, dgk = _rw(qk_norm_bwd(1.0), [k_raw, _from_key_blocks(dkt)], [sb_gk], [(SB_WIDTH, F32)], [sb_gk.shape],
                      "d_sb_knorm")
    g["sb_g_q"] = jnp.sum(dgq.reshape(SB_HEADS, SB_HEAD_DIM), axis=0)
    g["sb_g_k"] = jnp.sum(dgk.reshape(SB_HEADS, SB_HEAD_DIM), axis=0)
    dproj = jnp.concatenate([du, dq_raw, dk_raw, _from_key_blocks(dvt)], axis=1)
    g["w_in"] = _mm(h1, dproj, "tn", "d_w_in", tk=2048)
    dh1, got = _mm(dproj, w["w_in"], "nt", "d_h1", side=to_sibling(["w_in"]))
    add_sibling(["w_in"], got)
    (dx, g["g_mix"]), got = _norm_bwd(x, g_mix, dh1, dx1, "d_norm_mix", side=to_chips(["w_in"]))
    keep(["w_in"], got)
    return loss, dx, g, reduced


def kernel(x, mem, g_mix, w_in, ssm_a_re, ssm_a_im, ssm_log_dt, ssm_b_re, ssm_b_im, ssm_c_re, ssm_c_im, ssm_d, ssm_w_glu, sb_g_q, sb_g_k, g_out_ssm, g_out_sb, w_out, g_xa, g_mem, xa_w_q, xa_w_kv, xa_g_q, xa_g_k, xa_w_o, g_mlp, w_up, w_down, loss_target, m_g_mix, m_w_in, m_ssm_a_re, m_ssm_a_im, m_ssm_log_dt, m_ssm_b_re, m_ssm_b_im, m_ssm_c_re, m_ssm_c_im, m_ssm_d, m_ssm_w_glu, m_sb_g_q, m_sb_g_k, m_g_out_ssm, m_g_out_sb, m_w_out, m_g_xa, m_g_mem, m_xa_w_q, m_xa_w_kv, m_xa_g_q, m_xa_g_k, m_xa_w_o, m_g_mlp, m_w_up, m_w_down, v_g_mix, v_w_in, v_ssm_a_re, v_ssm_a_im, v_ssm_log_dt, v_ssm_b_re, v_ssm_b_im, v_ssm_c_re, v_ssm_c_im, v_ssm_d, v_ssm_w_glu, v_sb_g_q, v_sb_g_k, v_g_out_ssm, v_g_out_sb, v_w_out, v_g_xa, v_g_mem, v_xa_w_q, v_xa_w_kv, v_xa_g_q, v_xa_g_k, v_xa_w_o, v_g_mlp, v_w_up, v_w_down):
    given = dict(locals())
    order = ["g_mix", "w_in", "ssm_a_re", "ssm_a_im", "ssm_log_dt", "ssm_b_re", "ssm_b_im", "ssm_c_re", "ssm_c_im",
             "ssm_d", "ssm_w_glu", "sb_g_q", "sb_g_k", "g_out_ssm", "g_out_sb", "w_out", "g_xa", "g_mem", "xa_w_q",
             "xa_w_kv", "xa_g_q", "xa_g_k", "xa_w_o", "g_mlp", "w_up", "w_down"]
    assert sorted([n for n, _, _ in BIG] + SMALL) == sorted(order)
    core = lax.axis_index("c").astype(jnp.int32).reshape(1)
    chip = (2 * lax.axis_index("x") + lax.axis_index("y")).astype(jnp.int32).reshape(1)

    shards = {n: given[n][0].astype(BF16) for n, _, _ in BIG}
    sm = {n: given[n][0] for n in SMALL}
    loss, dx, g, reduced = _step(x[0], mem[0], loss_target[0], shards, sm, core)

    res = {}
    for n, _, _ in BIG:
        own, recv = reduced[n]
        outs = _adam_sharded(own, recv, given[n][0], given["m_" + n][0], given["v_" + n][0], chip, "adam_" + n)
        for kind, val in zip(("grad", "delta", "new_m", "new_v"), outs):
            res[kind + "_" + n] = val[None]

    sizes = [math.prod(sm[n].shape) for n in SMALL] + [1]
    packed = _pack_small([g[n] for n in SMALL] + [loss.reshape(1)])
    everyone = _all_gather(packed, "gather_small")
    nat = lambda a: a.reshape(_natural_2d(math.prod(a.shape)))
    outs = _adam_replicated(everyone, sizes, [nat(sm[n]) for n in SMALL], [nat(given["m_" + n][0]) for n in SMALL],
                            [nat(given["v_" + n][0]) for n in SMALL], "adam_replicated")
    for i, n in enumerate(SMALL):
        for kind, val in zip(("grad", "delta", "new_m", "new_v"), outs[4 * i:4 * i + 4]):
            res[kind + "_" + n] = val.reshape(given[n].shape)
    loss_out = outs[-1][0, 0]
    return (loss_out, dx[None], *[res["grad_" + n] for n in order], *[res["delta_" + n] for n in order],
            *[res["new_m_" + n] for n in order], *[res["new_v_" + n] for n in order])
```

```python
import functools
import math

import jax
import jax.numpy as jnp
from jax import lax
from jax.experimental import pallas as pl
from jax.experimental.pallas import tpu as pltpu

F32 = jnp.float32
BF16 = jnp.bfloat16
MESH = pl.DeviceIdType.MESH

N_DEV = 8
D_MODEL = 1024
SSM_WIDTH = 512
SSM_GROUP = 16
SSM_GROUPS = 32
SSM_STATE = 64
N_STATE = SSM_GROUPS * SSM_STATE
SB_HEADS = 8
SB_HEAD_DIM = 64
SB_WIDTH = 512
IN_WIDTH = 2048
XA_HEADS = 4
XA_HEAD_DIM = 128
XA_WIDTH = 512
D_FF = 4096
NORM_EPS = 1e-6
ADAM_LR = 0.001
ADAM_B1 = 0.9
ADAM_B2 = 0.999
ADAM_EPS = 1e-08
ADAM_WD = 0.01
ADAM_STEP = 10

LANES = 128
SUBLANES = 8
VMEM_LIMIT = 48 * 1024 * 1024
SCAN_LANES = 512
SB_BLOCK = 256
SB_UNDERFLOW = -110.0

NN = (((1,), (0,)), ((), ()))
NT = (((1,), (1,)), ((), ()))
TN = (((0,), (0,)), ((), ()))


def _params(sem=None):
    return pltpu.CompilerParams(dimension_semantics=sem, vmem_limit_bytes=VMEM_LIMIT)


def _dot(a, b, dims=NN):
    return lax.dot_general(a.astype(BF16), b.astype(BF16), dims, preferred_element_type=F32)


def _rms(x, g):
    return x * lax.rsqrt(jnp.mean(x * x, axis=-1, keepdims=True) + NORM_EPS) * g


ANY = pl.BlockSpec(memory_space=pl.ANY)


class _Side:
    def __init__(self, ins, out_shapes, n_sem, make):
        self.ins, self.out_shapes, self.n_sem, self.make = list(ins), list(out_shapes), n_sem, make

    def sems(self):
        return [pltpu.SemaphoreType.DMA((self.n_sem,)), pltpu.SemaphoreType.DMA((self.n_sem,))]


def _hosted(body, side, n_in, n_out, grid):
    if side is None:
        return body
    ns_in, ns_out = len(side.ins), len(side.out_shapes)

    def wrapped(*refs):
        ins, refs = refs[:n_in], refs[n_in:]
        s_ins, refs = refs[:ns_in], refs[ns_in:]
        outs, refs = refs[:n_out], refs[n_out:]
        s_outs, refs = refs[:ns_out], refs[ns_out:]
        scratch, sems = refs[:-2], refs[-2:]
        ids = [pl.program_id(d) for d in range(len(grid))]
        first = functools.reduce(jnp.logical_and, [i == 0 for i in ids])
        last = functools.reduce(jnp.logical_and, [i == n - 1 for i, n in zip(ids, grid)])

        @pl.when(first)
        def _():
            for cp in side.make(s_ins, s_outs, *sems):
                cp.start()

        body(*ins, *outs, *scratch)

        @pl.when(last)
        def _():
            for cp in side.make(s_ins, s_outs, *sems):
                cp.wait()

    return wrapped


def _side_args(side):
    if side is None:
        return [], [], [], [], []
    return ([ANY] * len(side.ins), [ANY] * len(side.out_shapes), side.out_shapes, side.sems(), side.ins)


def _split_side(res, n_out, side):
    res = list(res)
    main = res[0] if n_out == 1 else res[:n_out]
    return main if side is None else (main, res[n_out:])


def _mm(a, b, mode, name, *, epi=None, extras=(), fulls=(), out_dtypes=(F32,), sums=(), tm=1024, tn=1024, tk=1024,
        side=None):
    if mode == "nn":
        (m, k), (k2, n) = a.shape, b.shape
    elif mode == "nt":
        (m, k), (n, k2) = a.shape, b.shape
    else:
        (k, m), (k2, n) = a.shape, b.shape
    assert k == k2, (name, a.shape, b.shape)
    tm, tn, tk = min(tm, m), min(tn, n), min(tk, k)
    assert m % tm == 0 and n % tn == 0 and k % tk == 0, (name, m, n, k)
    nk = k // tk
    dims = {"nn": NN, "nt": NT, "tn": TN}[mode]
    if mode == "tn":
        a_spec = pl.BlockSpec((tk, tm), lambda i, j, kk: (kk, i))
    else:
        a_spec = pl.BlockSpec((tm, tk), lambda i, j, kk: (i, kk))
    if mode == "nt":
        b_spec = pl.BlockSpec((tn, tk), lambda i, j, kk: (j, kk))
    else:
        b_spec = pl.BlockSpec((tk, tn), lambda i, j, kk: (kk, j))
    mn_spec = pl.BlockSpec((tm, tn), lambda i, j, kk: (i, j))
    n_ex, n_full, n_out, n_sum = len(extras), len(fulls), len(out_dtypes), len(sums)
    n_in = 2 + n_ex + n_full

    def body(*refs):
        a_ref, b_ref = refs[:2]
        ex = refs[2:n_in]
        outs = refs[n_in:n_in + n_out]
        sum_refs = refs[n_in + n_out:n_in + n_out + n_sum]
        kk = pl.program_id(2)
        first_tile = jnp.logical_and(pl.program_id(0) == 0, pl.program_id(1) == 0)

        def finish(r):
            vals = epi(r, *[e[...] for e in ex]) if epi is not None else (r,)
            if n_sum:
                vals, parts = vals

                @pl.when(first_tile)
                def _():
                    for sr in sum_refs:
                        sr[...] = jnp.zeros_like(sr)

                for sr, p in zip(sum_refs, parts):
                    sr[...] += p
            for o, v in zip(outs, vals):
                o[...] = v.astype(o.dtype)

        if nk == 1:
            finish(_dot(a_ref[...], b_ref[...], dims))
        else:
            acc = refs[n_in + n_out + n_sum]

            @pl.when(kk == 0)
            def _():
                acc[...] = jnp.zeros_like(acc)

            acc[...] += _dot(a_ref[...], b_ref[...], dims)

            @pl.when(kk == nk - 1)
            def _():
                finish(acc[...])

    grid = (m // tm, n // tn, nk)
    whole = lambda shape: pl.BlockSpec(shape, lambda i, j, kk: (0,) * len(shape))
    s_in, s_out, s_shape, s_scratch, s_ops = _side_args(side)
    seq = bool(side) or n_sum > 0
    res = pl.pallas_call(
        _hosted(body, side, n_in, n_out + n_sum, grid), name=name, grid=grid,
        in_specs=[a_spec, b_spec] + [mn_spec] * n_ex + [whole(f.shape) for f in fulls] + s_in,
        out_specs=[mn_spec] * n_out + [whole(shape) for shape in sums] + s_out,
        out_shape=[jax.ShapeDtypeStruct((m, n), dt) for dt in out_dtypes]
        + [jax.ShapeDtypeStruct(shape, F32) for shape in sums] + s_shape,
        scratch_shapes=([pltpu.VMEM((tm, tn), F32)] if nk > 1 else []) + s_scratch,
        compiler_params=_params(("arbitrary",) * 3 if seq else ("parallel", "parallel", "arbitrary")),
    )(a, b, *extras, *fulls, *s_ops)
    return _split_side(res, n_out + n_sum, side)


def _row_tile(s, target):
    if s <= target:
        return s
    return max(t for t in range(16, target + 1, 16) if s % t == 0)


def _rw(fn, rows, fulls, row_out, acc_out, name, tm=512, side=None):
    cols = [r[1:] if isinstance(r, tuple) else (r.shape[1], 0) for r in rows]
    rows = [r[0] if isinstance(r, tuple) else r for r in rows]
    s = rows[0].shape[0]
    tm = _row_tile(s, tm)
    nr, nf, nro, nao = len(rows), len(fulls), len(row_out), len(acc_out)

    def body(*refs):
        r = refs[:nr]
        f = refs[nr:nr + nf]
        ro = refs[nr + nf:nr + nf + nro]
        ao = refs[nr + nf + nro:]
        outs, accs = fn(*[x[...] for x in r], *[x[...] for x in f])
        for o, v in zip(ro, outs):
            o[...] = v.astype(o.dtype)
        if nao:
            @pl.when(pl.program_id(0) == 0)
            def _():
                for a in ao:
                    a[...] = jnp.zeros_like(a)

            for a, v in zip(ao, accs):
                a[...] += v

    full_spec = lambda shape: pl.BlockSpec(shape, lambda i: (0,) * len(shape))
    s_in, s_out, s_shape, s_scratch, s_ops = _side_args(side)
    res = pl.pallas_call(
        _hosted(body, side, nr + nf, nro + nao, (s // tm,)), name=name, grid=(s // tm,),
        in_specs=[pl.BlockSpec((tm, wd), functools.partial(lambda i, cb: (i, cb), cb=cb)) for wd, cb in cols]
        + [full_spec(x.shape) for x in fulls] + s_in,
        out_specs=[pl.BlockSpec((tm, d), lambda i: (i, 0)) for d, _ in row_out]
        + [full_spec(shape) for shape in acc_out] + s_out,
        out_shape=[jax.ShapeDtypeStruct((s, d), dt) for d, dt in row_out]
        + [jax.ShapeDtypeStruct(shape, F32) for shape in acc_out] + s_shape,
        scratch_shapes=s_scratch,
        compiler_params=_params(("arbitrary",)),
    )(*rows, *fulls, *s_ops)
    res = list(res)
    return res if side is None else (res[:nro + nao], res[nro + nao:])


def _norm_fwd(x, g, name, side=None):
    res = _rw(lambda xt, gt: ((_rms(xt, gt),), ()), [x], [g], [(x.shape[1], BF16)], [], name, side=side)
    return res[0] if side is None else (res[0][0], res[1])


def _norm_bwd(x, g, dh, dres, name, side=None):
    def fn(xt, dht, drt, gt):
        _, vjp = jax.vjp(_rms, xt, gt)
        dx, dg = vjp(dht)
        return (dx + drt,), (dg,)

    return _rw(fn, [x, dh, dres], [g], [(x.shape[1], F32)], [g.shape], name, side=side)


def _rms_groups(x, g, scale):
    lo = lax.broadcasted_iota(jnp.int32, (1, LANES), 1) < SB_HEAD_DIM
    x2 = x * x
    outs = []
    for cb in range(x.shape[1] // LANES):
        sl = slice(cb * LANES, (cb + 1) * LANES)
        s_lo = jnp.sum(jnp.where(lo, x2[:, sl], 0.0), axis=-1, keepdims=True)
        s_hi = jnp.sum(jnp.where(lo, 0.0, x2[:, sl]), axis=-1, keepdims=True)
        r = jnp.where(lo, lax.rsqrt(s_lo * (1.0 / SB_HEAD_DIM) + NORM_EPS),
                      lax.rsqrt(s_hi * (1.0 / SB_HEAD_DIM) + NORM_EPS))
        outs.append(x[:, sl] * r)
    return jnp.concatenate(outs, axis=1) * g * scale


def _log_sigmoid(z):
    return jnp.minimum(z, 0.0) - jnp.log(1.0 + jnp.exp(-jnp.abs(z)))


def _split_dot(x, u2):
    hi = x.astype(BF16)
    lo = (x - hi.astype(F32)).astype(BF16)
    return jnp.dot(jnp.concatenate([hi, lo], axis=1), u2, preferred_element_type=F32)


def _sb_consts(b):
    row = lax.broadcasted_iota(jnp.int32, (b, b), 0)
    col = lax.broadcasted_iota(jnp.int32, (b, b), 1)
    tri = col < row
    u_after = (row > col).astype(BF16)
    u_from = (row >= col).astype(BF16)
    stack = lambda u: jnp.concatenate([u, u], axis=0)
    lane_lo = lax.broadcasted_iota(jnp.int32, (b, LANES), 1) < SB_HEAD_DIM
    return tri, stack(u_after), stack(u_from), lane_lo


def _sb_scores(qh, kb, a_run, keep, u2_after):
    z = lax.dot_general(qh, kb, NT, preferred_element_type=F32)
    lb = _log_sigmoid(z)
    l = lb - z
    if keep is not None:
        l = jnp.where(keep, l, 0.0)
    w = jnp.exp(lb + (a_run + _split_dot(l, u2_after)))
    if keep is not None:
        w = jnp.where(keep, w, 0.0)
    return lb, l, w


def _sb_walk(qi, carry, step):
    def cond(state):
        n, c = state
        return jnp.logical_and(n <= qi, jnp.max(jnp.maximum(c[0], c[1])) > SB_UNDERFLOW)

    def body(state):
        n, c = state
        return n + 1, step(n, c)

    return lax.while_loop(cond, body, (jnp.int32(2), carry))[1]


def _two_heads(x, lane_lo):
    zero = jnp.zeros_like(x)
    return jnp.where(lane_lo, x, zero), jnp.where(lane_lo, zero, x)


def _sb_fwd(qs, ks, v, name, side=None):
    s, width = qs.shape
    b = min(SB_BLOCK, s)

    def body(q_ref, k_ref, v_ref, o_ref):
        qi = pl.program_id(1)
        tri, u2_after, _, lane_lo = _sb_consts(b)
        q_a, q_b = _two_heads(q_ref[...], lane_lo)

        def step(n, carry, keep):
            a_a, a_b, acc = carry
            off = pl.multiple_of(jnp.maximum(qi - n, 0) * b, b)
            kb = k_ref[pl.ds(off, b), :]
            v_a, v_b = _two_heads(v_ref[pl.ds(off, b), :], lane_lo)
            _, l_a, w_a = _sb_scores(q_a, kb, a_a, keep, u2_after)
            _, l_b, w_b = _sb_scores(q_b, kb, a_b, keep, u2_after)
            acc = acc + jnp.dot(jnp.concatenate([w_a.astype(BF16), w_b.astype(BF16)], axis=1),
                                jnp.concatenate([v_a, v_b], axis=0), preferred_element_type=F32)
            return (a_a + jnp.sum(l_a, axis=1, keepdims=True), a_b + jnp.sum(l_b, axis=1, keepdims=True), acc)

        zero = jnp.zeros((b, 1), F32)
        carry = step(0, (zero, zero, jnp.zeros((b, LANES), F32)), tri)
        carry = step(1, carry, jnp.broadcast_to(qi > 0, tri.shape))
        carry = _sb_walk(qi, carry, lambda n, c: step(n, c, None))
        o_ref[...] = carry[2]

    blk = pl.BlockSpec((b, LANES), lambda hp, i: (i, hp))
    full = pl.BlockSpec((s, LANES), lambda hp, i: (0, hp))
    grid = (width // LANES, s // b)
    s_in, s_out, s_shape, s_scratch, s_ops = _side_args(side)
    res = pl.pallas_call(
        _hosted(body, side, 3, 1, grid), name=name, grid=grid,
        in_specs=[blk, full, full] + s_in, out_specs=[blk] + s_out,
        out_shape=[jax.ShapeDtypeStruct((s, width), F32)] + s_shape, scratch_shapes=s_scratch,
        compiler_params=_params(("arbitrary", "arbitrary")),
    )(qs, ks, v, *s_ops)
    return _split_side(res, 1, side)


def _sb_bwd(qs, ks, v, out, dout, name, side=None):
    s, width = qs.shape
    b = min(SB_BLOCK, s)
    nkb = s // b

    def body(q_ref, k_ref, v_ref, o_ref, do_ref, dq_ref, dkt_ref, dvt_ref):
        qi = pl.program_id(1)

        @pl.when(qi == 0)
        def _():
            dkt_ref[...] = jnp.zeros_like(dkt_ref)
            dvt_ref[...] = jnp.zeros_like(dvt_ref)

        tri, u2_after, u2_from, lane_lo = _sb_consts(b)
        q_a, q_b = _two_heads(q_ref[...], lane_lo)
        dob = do_ref[...].astype(BF16)
        do_a, do_b = _two_heads(dob, lane_lo)
        prod = dob.astype(F32) * o_ref[...]
        d_a = jnp.sum(jnp.where(lane_lo, prod, 0.0), axis=1, keepdims=True)
        d_b = jnp.sum(jnp.where(lane_lo, 0.0, prod), axis=1, keepdims=True)
        tr = lambda x: jnp.transpose(x.astype(F32)).astype(BF16)
        qt = jnp.concatenate([tr(q_a), tr(q_b)], axis=1)
        dot_ = jnp.concatenate([tr(do_a), tr(do_b)], axis=1)

        def head(qh, doh, kb, vb, a_run, d_rem, keep):
            lb, l, w = _sb_scores(qh, kb, a_run, keep, u2_after)
            wb = w.astype(BF16)
            g = lax.dot_general(doh, vb, NT, preferred_element_type=F32) * wb.astype(F32)
            g_before = d_rem - _split_dot(g, u2_from)
            dz = g - (g + g_before) * jnp.exp(lb)
            if keep is not None:
                dz = jnp.where(keep, dz, 0.0)
            return (dz.astype(BF16), wb, a_run + jnp.sum(l, axis=1, keepdims=True),
                    d_rem - jnp.sum(g, axis=1, keepdims=True))

        def step(n, carry, keep):
            a_a, a_b, r_a, r_b, dq = carry
            jb = jnp.maximum(qi - n, 0)
            off = pl.multiple_of(jb * b, b)
            kb = k_ref[pl.ds(off, b), :]
            vb = v_ref[pl.ds(off, b), :]
            k_a, k_b = _two_heads(kb, lane_lo)
            dz_a, w_a, a_a, r_a = head(q_a, do_a, kb, vb, a_a, r_a, keep)
            dz_b, w_b, a_b, r_b = head(q_b, do_b, kb, vb, a_b, r_b, keep)
            dq = dq + jnp.dot(jnp.concatenate([dz_a, dz_b], axis=1), jnp.concatenate([k_a, k_b], axis=0),
                              preferred_element_type=F32)
            dkt_ref[0, jb] += jnp.dot(qt, jnp.concatenate([dz_a, dz_b], axis=0), preferred_element_type=F32)
            dvt_ref[0, jb] += jnp.dot(dot_, jnp.concatenate([w_a, w_b], axis=0), preferred_element_type=F32)
            return a_a, a_b, r_a, r_b, dq

        zero = jnp.zeros((b, 1), F32)
        carry = step(0, (zero, zero, d_a, d_b, jnp.zeros((b, LANES), F32)), tri)
        carry = step(1, carry, jnp.broadcast_to(qi > 0, tri.shape))
        carry = _sb_walk(qi, carry, lambda n, c: step(n, c, None))
        dq_ref[...] = carry[4]

    blk = pl.BlockSpec((b, LANES), lambda hp, i: (i, hp))
    full = pl.BlockSpec((s, LANES), lambda hp, i: (0, hp))
    acc = pl.BlockSpec((1, nkb, LANES, b), lambda hp, i: (hp, 0, 0, 0))
    grid = (width // LANES, nkb)
    s_in, s_out, s_shape, s_scratch, s_ops = _side_args(side)
    res = pl.pallas_call(
        _hosted(body, side, 5, 3, grid), name=name, grid=grid,
        in_specs=[blk, full, full, blk, blk] + s_in, out_specs=[blk, acc, acc] + s_out,
        out_shape=[jax.ShapeDtypeStruct((s, width), F32)]
        + [jax.ShapeDtypeStruct((width // LANES, nkb, LANES, b), F32)] * 2 + s_shape,
        scratch_shapes=s_scratch,
        compiler_params=_params(("arbitrary", "arbitrary")),
    )(qs, ks, v, out, dout, *s_ops)
    return _split_side(res, 3, side)


def _from_key_blocks(t):
    hp, nkb, lanes, b = t.shape
    return jnp.transpose(t, (1, 3, 0, 2)).reshape(nkb * b, hp * lanes)


def _cmul(xr, xi, yr, yi):
    return xr * yr - xi * yi, xr * yi + xi * yr


def _scan_consts(ar, ai, reverse, lc):
    rowi = lax.broadcasted_iota(jnp.int32, (SUBLANES, lc), 0)
    pows = [(ar, ai)]
    for _ in range(SUBLANES - 1):
        pows.append(_cmul(*pows[-1], ar, ai))
    steps = []
    for d in (1, 2, 4):
        keep = (rowi < SUBLANES - d) if reverse else (rowi >= d)
        pr, pi = pows[d - 1]
        steps.append((SUBLANES - d if reverse else d, jnp.where(keep, pr, 0.0), jnp.where(keep, pi, 0.0)))
    cr = jnp.zeros((SUBLANES, lc), F32)
    ci = jnp.zeros((SUBLANES, lc), F32)
    for r in range(SUBLANES):
        pr, pi = pows[SUBLANES - 1 - r] if reverse else pows[r]
        cr = jnp.where(rowi == r, pr, cr)
        ci = jnp.where(rowi == r, pi, ci)
    return steps, cr, ci


def _scan_tile(xr, xi, steps, pr, pi, cr, ci):
    for shift, ar, ai in steps:
        rr = pltpu.roll(xr, shift, 0)
        ri = pltpu.roll(xi, shift, 0)
        xr, xi = xr + ar * rr - ai * ri, xi + ar * ri + ai * rr
    return xr + pr * cr - pi * ci, xi + pr * ci + pi * cr


def _ssm_fwd(u, acat, bsup, csup, d_skip, name, tt=1024, side=None):
    s = u.shape[0]
    lc = SCAN_LANES
    tt = min(tt, s)
    nl, nt = N_STATE // lc, s // tt

    def body(u_ref, a_ref, b_ref, c_ref, d_ref, s_ref, y0_ref, y1_ref, carry):
        @pl.when(pl.program_id(1) == 0)
        def _():
            carry[...] = jnp.zeros_like(carry)

        ut = u_ref[...]
        s_ref[...] = _dot(ut, b_ref[0])
        steps, pr, pi = _scan_consts(a_ref[:, :lc], a_ref[:, lc:], False, lc)

        def tile(i, c):
            off = pl.multiple_of(i * SUBLANES, SUBLANES)
            xr, xi = _scan_tile(s_ref[pl.ds(off, SUBLANES), :lc], s_ref[pl.ds(off, SUBLANES), lc:],
                                steps, pr, pi, c[0], c[1])
            s_ref[pl.ds(off, SUBLANES), :lc] = xr
            s_ref[pl.ds(off, SUBLANES), lc:] = xi
            return (jnp.broadcast_to(xr[SUBLANES - 1:, :], (SUBLANES, lc)),
                    jnp.broadcast_to(xi[SUBLANES - 1:, :], (SUBLANES, lc)))

        cr, ci = lax.fori_loop(0, tt // SUBLANES, tile, (carry[:, :lc], carry[:, lc:]))
        carry[:, :lc] = cr
        carry[:, lc:] = ci
        y0 = _dot(s_ref[...], c_ref[0], NT) + d_ref[...] * ut
        y0_ref[...] = y0
        y1_ref[...] = jax.nn.gelu(y0)

    chan = pl.BlockSpec((tt, LANES), lambda j, c: (c, j))
    sup = pl.BlockSpec((1, LANES, 2 * lc), lambda j, c: (j, 0, 0))
    s_in, s_out, s_shape, s_scratch, s_ops = _side_args(side)
    res = pl.pallas_call(
        _hosted(body, side, 5, 3, (nl, nt)), name=name, grid=(nl, nt),
        in_specs=[chan, pl.BlockSpec((1, 2 * lc), lambda j, c: (0, j)), sup, sup,
                  pl.BlockSpec((1, LANES), lambda j, c: (0, j))] + s_in,
        out_specs=[pl.BlockSpec((tt, 2 * lc), lambda j, c: (c, j)), chan, chan] + s_out,
        out_shape=[jax.ShapeDtypeStruct((s, 2 * N_STATE), F32), jax.ShapeDtypeStruct((s, SSM_WIDTH), F32),
                   jax.ShapeDtypeStruct((s, SSM_WIDTH), F32)] + s_shape,
        scratch_shapes=[pltpu.VMEM((SUBLANES, 2 * lc), F32)] + s_scratch,
        compiler_params=_params(("arbitrary", "arbitrary")),
    )(u, acat, bsup, csup, d_skip, *s_ops)
    return _split_side(res, 3, side)


def _ssm_bwd(dy0, states, u, acat, bsup, csup, d_skip, name, tt=1024, side=None):
    s = u.shape[0]
    lc = SCAN_LANES
    tt = min(tt, s)
    nl, nt = N_STATE // lc, s // tt
    nt8 = tt // SUBLANES

    def body(dy_ref, s_ref, sp_ref, u_ref, a_ref, b_ref, c_ref, d_ref,
             du_ref, da_ref, db_ref, dc_ref, dd_ref, lam_ref, carry):
        c = pl.program_id(1)

        @pl.when(c == 0)
        def _():
            carry[...] = jnp.zeros_like(carry)
            for r in (da_ref, db_ref, dc_ref, dd_ref):
                r[...] = jnp.zeros_like(r)

        dy = dy_ref[...]
        ut = u_ref[...]
        lam_ref[...] = _dot(dy, c_ref[0])
        steps, pr, pi = _scan_consts(a_ref[:, :lc], -a_ref[:, lc:], True, lc)
        rowi = lax.broadcasted_iota(jnp.int32, (SUBLANES, lc), 0)
        first_chunk = c == nt - 1

        def tile(i, carry_v):
            cr, ci, dar, dai = carry_v
            t = nt8 - 1 - i
            off = pl.multiple_of(t * SUBLANES, SUBLANES)
            lr, li = _scan_tile(lam_ref[pl.ds(off, SUBLANES), :lc], lam_ref[pl.ds(off, SUBLANES), lc:],
                                steps, pr, pi, cr, ci)
            lam_ref[pl.ds(off, SUBLANES), :lc] = lr
            lam_ref[pl.ds(off, SUBLANES), lc:] = li
            offp = pl.multiple_of(jnp.maximum(t - 1, 0) * SUBLANES, SUBLANES)
            in_chunk = t > 0
            use = jnp.logical_or(in_chunk, jnp.logical_not(first_chunk))
            prev_r = jnp.where(in_chunk, s_ref[pl.ds(offp, SUBLANES), :lc], sp_ref[:, :lc])
            prev_i = jnp.where(in_chunk, s_ref[pl.ds(offp, SUBLANES), lc:], sp_ref[:, lc:])
            last_r = jnp.where(use, jnp.broadcast_to(prev_r[SUBLANES - 1:, :], (SUBLANES, lc)), 0.0)
            last_i = jnp.where(use, jnp.broadcast_to(prev_i[SUBLANES - 1:, :], (SUBLANES, lc)), 0.0)
            sr = jnp.where(rowi == 0, last_r, pltpu.roll(s_ref[pl.ds(off, SUBLANES), :lc], 1, 0))
            si = jnp.where(rowi == 0, last_i, pltpu.roll(s_ref[pl.ds(off, SUBLANES), lc:], 1, 0))
            dar = dar + lr * sr + li * si
            dai = dai + li * sr - lr * si
            return (jnp.broadcast_to(lr[:1, :], (SUBLANES, lc)), jnp.broadcast_to(li[:1, :], (SUBLANES, lc)),
                    dar, dai)

        zero = jnp.zeros((SUBLANES, lc), F32)
        cr, ci, dar, dai = lax.fori_loop(0, nt8, tile, (carry[:, :lc], carry[:, lc:], zero, zero))
        carry[:, :lc] = cr
        carry[:, lc:] = ci
        da_ref[:, :lc] += dar
        da_ref[:, lc:] += dai
        lam = lam_ref[...].astype(BF16)
        du_ref[...] = _dot(lam, b_ref[0], NT) + d_ref[...] * dy
        db_ref[0] += _dot(ut, lam, TN)
        dc_ref[0] += _dot(dy, s_ref[...], TN)
        dd_ref[...] += jnp.sum(dy * ut, axis=0, keepdims=True)

    rev = lambda j, c: (nt - 1 - c, j)
    chan = pl.BlockSpec((tt, LANES), rev)
    sup = pl.BlockSpec((1, LANES, 2 * lc), lambda j, c: (j, 0, 0))
    row = pl.BlockSpec((1, LANES), lambda j, c: (0, j))
    s_in, s_out, s_shape, s_scratch, s_ops = _side_args(side)
    res = pl.pallas_call(
        _hosted(body, side, 8, 5, (nl, nt)), name=name, grid=(nl, nt),
        in_specs=[chan, pl.BlockSpec((tt, 2 * lc), rev),
                  pl.BlockSpec((SUBLANES, 2 * lc), lambda j, c: (jnp.maximum((nt - 1 - c) * nt8 - 1, 0), j)),
                  chan, pl.BlockSpec((1, 2 * lc), lambda j, c: (0, j)), sup, sup, row] + s_in,
        out_specs=[chan, pl.BlockSpec((SUBLANES, 2 * lc), lambda j, c: (0, j)), sup, sup, row] + s_out,
        out_shape=[jax.ShapeDtypeStruct((s, SSM_WIDTH), F32), jax.ShapeDtypeStruct((SUBLANES, 2 * N_STATE), F32),
                   jax.ShapeDtypeStruct(bsup.shape, F32), jax.ShapeDtypeStruct(csup.shape, F32),
                   jax.ShapeDtypeStruct((1, SSM_WIDTH), F32)] + s_shape,
        scratch_shapes=[pltpu.VMEM((tt, 2 * lc), F32), pltpu.VMEM((SUBLANES, 2 * lc), F32)] + s_scratch,
        compiler_params=_params(("arbitrary", "arbitrary")),
    )(dy0, states, states, u, acat, bsup, csup, d_skip, *s_ops)
    return _split_side(res, 5, side)


def _state_cols(xr, xi):
    lead = xr.shape[:-1]
    nl = N_STATE // SCAN_LANES
    both = jnp.stack([xr.reshape(lead + (nl, SCAN_LANES)), xi.reshape(lead + (nl, SCAN_LANES))], axis=-2)
    return both.reshape(lead + (2 * N_STATE,))


def _ssm_mats(a_re, a_im, log_dt, b_re, b_im, c_re, c_im):
    dt = jnp.exp(log_dt)[:, None]
    lr, li = a_re * dt, a_im * dt
    e = jnp.exp(lr)
    abar_r, abar_i = e * jnp.cos(li), e * jnp.sin(li)
    den = a_re * a_re + a_im * a_im
    coef_r = ((abar_r - 1.0) * a_re + abar_i * a_im) / den
    coef_i = (abar_i * a_re - (abar_r - 1.0) * a_im) / den
    bbar_r = coef_r[..., None] * b_re - coef_i[..., None] * b_im
    bbar_i = coef_r[..., None] * b_im + coef_i[..., None] * b_re
    nl = N_STATE // SCAN_LANES
    gpb = SSM_GROUPS // nl
    eye = jnp.eye(gpb, dtype=bool)[None, :, None, :, None]

    def sup(m_r, m_i):
        def one(m):
            m = m.reshape(nl, gpb, SSM_GROUP, 1, SSM_STATE)
            return jnp.where(eye, m, 0.0).reshape(nl, gpb * SSM_GROUP, SCAN_LANES)
        return jnp.concatenate([one(m_r), one(m_i)], axis=-1)

    acat = _state_cols(abar_r.reshape(1, N_STATE), abar_i.reshape(1, N_STATE))
    bsup = sup(jnp.transpose(bbar_r, (0, 2, 1)), jnp.transpose(bbar_i, (0, 2, 1)))
    csup = sup(c_re, -c_im)
    return acat, bsup, csup


def _mem_fwd(mem, g_mem, w_kv, g_k, name):
    ml = mem.shape[0]

    def body(mem_ref, gm_ref, w_ref, gk_ref, memn_ref, kv_ref, kn_ref, vv_ref):
        memn = _rms(mem_ref[...], gm_ref[...])
        memn_ref[...] = memn.astype(BF16)
        kv = _dot(memn, w_ref[...])
        kv_ref[...] = kv
        for hh in range(XA_HEADS):
            sl = slice(hh * XA_HEAD_DIM, (hh + 1) * XA_HEAD_DIM)
            kn_ref[:, sl] = _rms(kv[:, sl], gk_ref[...]).astype(BF16)
        vv_ref[...] = kv[:, XA_WIDTH:].astype(BF16)

    return pl.pallas_call(
        body, name=name,
        out_shape=[jax.ShapeDtypeStruct((ml, D_MODEL), BF16), jax.ShapeDtypeStruct((ml, 2 * XA_WIDTH), F32),
                   jax.ShapeDtypeStruct((ml, XA_WIDTH), BF16), jax.ShapeDtypeStruct((ml, XA_WIDTH), BF16)],
        compiler_params=_params(),
    )(mem, g_mem, w_kv, g_k)


def _mem_bwd(mem, g_mem, memn, w_kv, kv, g_k, dkn, dvv, name):
    def body(mem_ref, gm_ref, memn_ref, w_ref, kv_ref, gk_ref, dkn_ref, dvv_ref, dw_ref, dgm_ref, dgk_ref):
        kv = kv_ref[...]
        dgk = jnp.zeros(dgk_ref.shape, F32)
        parts = []
        for hh in range(XA_HEADS):
            sl = slice(hh * XA_HEAD_DIM, (hh + 1) * XA_HEAD_DIM)
            _, vjp = jax.vjp(_rms, kv[:, sl], gk_ref[...])
            dk, dg = vjp(dkn_ref[:, sl])
            parts.append(dk)
            dgk = dgk + dg
        dgk_ref[...] = dgk
        dkv = jnp.concatenate(parts + [dvv_ref[...]], axis=1)
        dw_ref[...] = _dot(memn_ref[...], dkv, TN)
        dmemn = _dot(dkv, w_ref[...], NT)
        _, vjp = jax.vjp(_rms, mem_ref[...], gm_ref[...])
        dgm_ref[...] = vjp(dmemn)[1]

    return pl.pallas_call(
        body, name=name,
        out_shape=[jax.ShapeDtypeStruct((D_MODEL, 2 * XA_WIDTH), F32), jax.ShapeDtypeStruct(g_mem.shape, F32),
                   jax.ShapeDtypeStruct(g_k.shape, F32)],
        compiler_params=_params(),
    )(mem, g_mem, memn, w_kv, kv, g_k, dkn, dvv)


def _xa_head(qx_h, g_q, kn_h, vv_h):
    qn = _rms(qx_h, g_q)
    sc = _dot(qn, kn_h, NT) * (XA_HEAD_DIM ** -0.5)
    sc = sc - jnp.max(sc, axis=-1, keepdims=True)
    e = jnp.exp(sc)
    p = e / jnp.sum(e, axis=-1, keepdims=True)
    return qn, p


def _xa_fwd(qx, g_q, kn, vv, name):
    def fn(qt, gq, knt, vvt):
        outs = []
        for hh in range(XA_HEADS):
            sl = slice(hh * XA_HEAD_DIM, (hh + 1) * XA_HEAD_DIM)
            _, p = _xa_head(qt[:, sl], gq, knt[:, sl], vvt[:, sl])
            outs.append(_dot(p, vvt[:, sl]))
        return (jnp.concatenate(outs, axis=1),), ()

    return _rw(fn, [qx], [g_q, kn, vv], [(XA_WIDTH, BF16)], [], name)[0]


def _xa_bwd(qx, g_q, kn, vv, do, name):
    def fn(qt, dot_, gq, knt, vvt):
        dqs, dks, dvs = [], [], []
        dgq = jnp.zeros_like(gq)
        for hh in range(XA_HEADS):
            sl = slice(hh * XA_HEAD_DIM, (hh + 1) * XA_HEAD_DIM)
            qn, p = _xa_head(qt[:, sl], gq, knt[:, sl], vvt[:, sl])
            doh = dot_[:, sl]
            dp = _dot(doh, vvt[:, sl], NT)
            dvs.append(_dot(p, doh, TN))
            ds = p * (dp - jnp.sum(dp * p, axis=-1, keepdims=True)) * (XA_HEAD_DIM ** -0.5)
            dqn = _dot(ds, knt[:, sl])
            dks.append(_dot(ds, qn, TN))
            _, vjp = jax.vjp(_rms, qt[:, sl], gq)
            dq, dg = vjp(dqn)
            dqs.append(dq)
            dgq = dgq + dg
        return ((jnp.concatenate(dqs, axis=1),),
                (jnp.concatenate(dks, axis=1), jnp.concatenate(dvs, axis=1), dgq))

    return _rw(fn, [qx, do], [g_q, kn, vv], [(XA_WIDTH, F32)], [kn.shape, vv.shape, g_q.shape], name)


BIG = [
    ("w_in", (D_MODEL, IN_WIDTH), 1), ("ssm_w_glu", (SSM_WIDTH, SSM_WIDTH), 0), ("w_out", (D_MODEL, D_MODEL), 0),
    ("xa_w_q", (D_MODEL, XA_WIDTH), 0), ("xa_w_kv", (D_MODEL, 2 * XA_WIDTH), 0), ("xa_w_o", (XA_WIDTH, D_MODEL), 1),
    ("w_up", (D_MODEL, D_FF), 1), ("w_down", (D_FF, D_MODEL), 0),
]
BIG_INDEX = {n: i for i, (n, _, _) in enumerate(BIG)}


def _shard_shape(shape, axis):
    return tuple(d // N_DEV if i == axis else d for i, d in enumerate(shape))


def _shard_of(ref, axis, d):
    n = ref.shape[axis] // N_DEV
    return ref.at[pl.ds(d * n, n), :] if axis == 0 else ref.at[:, pl.ds(d * n, n)]


def _gather_side(names, shards):
    idxs = [BIG_INDEX[n] for n in names]

    def make(ins, outs, send_sems, recv_sems):
        x, y, c = lax.axis_index("x"), lax.axis_index("y"), lax.axis_index("c")
        cps = []
        for j, i in enumerate(idxs):
            mine = _shard_of(outs[j], BIG[i][2], 4 * x + 2 * y + c)
            cps.append(pltpu.make_async_copy(ins[j], mine, send_sems.at[N_DEV * j]))
            for rel in range(1, N_DEV):
                to = tuple(1 - p if rel >> bit & 1 else p for p, bit in ((x, 2), (y, 1), (c, 0)))
                cps.append(pltpu.make_async_remote_copy(
                    src_ref=ins[j], dst_ref=mine, send_sem=send_sems.at[N_DEV * j + rel],
                    recv_sem=recv_sems.at[N_DEV * j + rel], device_id=to, device_id_type=MESH))
        return cps

    return _Side(shards, [jax.ShapeDtypeStruct(BIG[i][1], BF16) for i in idxs], N_DEV * len(idxs), make)


def _sibling_side(names, grads):
    idxs = [BIG_INDEX[n] for n in names]

    def make(ins, outs, send_sems, recv_sems):
        x, y, c = lax.axis_index("x"), lax.axis_index("y"), lax.axis_index("c")
        return [pltpu.make_async_remote_copy(
            src_ref=_shard_of(ins[j], BIG[i][2], 2 * k + (1 - c)), dst_ref=outs[j].at[k],
            send_sem=send_sems.at[4 * j + k], recv_sem=recv_sems.at[4 * j + k], device_id=(x, y, 1 - c),
            device_id_type=MESH) for j, i in enumerate(idxs) for k in range(4)]

    shapes = [jax.ShapeDtypeStruct((4,) + _shard_shape(BIG[i][1], BIG[i][2]), F32) for i in idxs]
    return _Side(grads, shapes, 4 * len(idxs), make)


def _chips_side(parts):
    def make(ins, outs, send_sems, recv_sems):
        x, y, c = lax.axis_index("x"), lax.axis_index("y"), lax.axis_index("c")
        chips = [(1 - x, y), (x, 1 - y), (1 - x, 1 - y)]
        return [pltpu.make_async_remote_copy(
            src_ref=ins[j].at[2 * cx + cy], dst_ref=outs[j].at[r], send_sem=send_sems.at[3 * j + r],
            recv_sem=recv_sems.at[3 * j + r], device_id=(cx, cy, c), device_id_type=MESH)
            for r, (cx, cy) in enumerate(chips) for j in range(len(parts))]

    return _Side(parts, [jax.ShapeDtypeStruct((3,) + p.shape[1:], p.dtype) for p in parts], 3 * len(parts), make)


def _reduce_add(grad, recv, axis, core, name):
    rs, cs = recv.shape[1:]
    rt = _row_tile(rs, 256)
    nt = rs // rt

    def body(c_ref, g_ref, r_ref, p_ref, pb_ref):
        sm = g_ref[...] + r_ref[0]
        p_ref[0] = sm
        pb_ref[0] = sm.astype(BF16)

    if axis == 0:
        g_spec = pl.BlockSpec((rt, cs), lambda k, t, c_ref: ((2 * k + c_ref[0]) * nt + t, 0))
    else:
        g_spec = pl.BlockSpec((rt, cs), lambda k, t, c_ref: (t, 2 * k + c_ref[0]))
    slab = pl.BlockSpec((1, rt, cs), lambda k, t, c_ref: (k, t, 0))
    return pl.pallas_call(
        body, name=name,
        grid_spec=pltpu.PrefetchScalarGridSpec(num_scalar_prefetch=1, grid=(4, nt), in_specs=[g_spec, slab],
                                               out_specs=[slab, slab]),
        out_shape=[jax.ShapeDtypeStruct(recv.shape, F32), jax.ShapeDtypeStruct(recv.shape, BF16)],
        compiler_params=_params(("parallel", "parallel")),
    )(core, grad, recv)


def _all_gather(block, name):
    m_per, n = block.shape

    def body(x_ref, out_ref, send_sems, recv_sems, local_sem):
        x, y, c = lax.axis_index("x"), lax.axis_index("y"), lax.axis_index("c")
        me, sibling = (x, y, c), (x, y, 1 - c)
        chips = [(1 - x, y), (x, 1 - y), (1 - x, 1 - y)]

        def rows(px, py, pc):
            return out_ref.at[pl.ds((4 * px + 2 * py + pc) * m_per, m_per), :]

        def copy(k, blk, to, src=None):
            return pltpu.make_async_remote_copy(
                src_ref=rows(*blk) if src is None else src, dst_ref=rows(*blk),
                send_sem=send_sems.at[k], recv_sem=recv_sems.at[k], device_id=to, device_id_type=MESH)

        mine = pltpu.make_async_copy(x_ref, rows(*me), local_sem)
        mine.start()
        first = [copy(0, me, sibling, src=x_ref)]
        first += [copy(1 + j, me, (*chip, c), src=x_ref) for j, chip in enumerate(chips)]
        for cp in first:
            cp.start()
        passed = [copy(4 + j, (*chip, c), sibling) for j, chip in enumerate(chips)]
        for j, chip in enumerate(chips):
            copy(1 + j, (*chip, c), me).wait_recv()
            passed[j].start()
        copy(0, sibling, me).wait_recv()
        for j, chip in enumerate(chips):
            copy(4 + j, (*chip, 1 - c), me).wait_recv()
        for cp in first + passed:
            cp.wait_send()
        mine.wait()

    return pl.pallas_call(
        body, name=name, in_specs=[ANY], out_specs=ANY,
        out_shape=jax.ShapeDtypeStruct((N_DEV * m_per, n), block.dtype),
        scratch_shapes=[pltpu.SemaphoreType.DMA((7,)), pltpu.SemaphoreType.DMA((7,)), pltpu.SemaphoreType.DMA],
    )(block)


def _adam_math(w, g, m, v):
    m = ADAM_B1 * m + (1.0 - ADAM_B1) * g
    v = ADAM_B2 * v + (1.0 - ADAM_B2) * (g * g)
    m_hat = m / (1.0 - ADAM_B1 ** ADAM_STEP)
    v_hat = v / (1.0 - ADAM_B2 ** ADAM_STEP)
    delta = -ADAM_LR * (m_hat / (jnp.sqrt(v_hat) + ADAM_EPS) + ADAM_WD * w)
    return delta, m, v


def _adam_sharded(own, recv, w, m, v, chip, name):
    rs, cs = w.shape
    rt = _row_tile(rs, 256)

    def body(chip_ref, p_ref, r_ref, w_ref, m_ref, v_ref, g_out, d_out, m_out, v_out):
        g = p_ref[0] + r_ref[0].astype(F32) + r_ref[1].astype(F32) + r_ref[2].astype(F32)
        d, mn, vn = _adam_math(w_ref[...], g, m_ref[...], v_ref[...])
        g_out[...] = g
        d_out[...] = d
        m_out[...] = mn
        v_out[...] = vn

    tile = pl.BlockSpec((rt, cs), lambda t, chip_ref: (t, 0))
    return pl.pallas_call(
        body, name=name,
        grid_spec=pltpu.PrefetchScalarGridSpec(
            num_scalar_prefetch=1, grid=(rs // rt,),
            in_specs=[pl.BlockSpec((1, rt, cs), lambda t, chip_ref: (chip_ref[0], t, 0)),
                      pl.BlockSpec((3, rt, cs), lambda t, chip_ref: (0, t, 0)), tile, tile, tile],
            out_specs=[tile] * 4),
        out_shape=[jax.ShapeDtypeStruct((rs, cs), F32)] * 4,
        compiler_params=_params(("parallel",)),
    )(chip, own, recv, w, m, v)


SMALL = ["g_mix", "ssm_a_re", "ssm_a_im", "ssm_log_dt", "ssm_b_re", "ssm_b_im", "ssm_c_re", "ssm_c_im", "ssm_d",
         "sb_g_q", "sb_g_k", "g_out_ssm", "g_out_sb", "g_xa", "g_mem", "xa_g_q", "xa_g_k", "g_mlp"]
PACK_TILE = SUBLANES * LANES


def _natural_2d(n):
    return (n // LANES, LANES) if n % LANES == 0 else (1, n)


def _pack_small(arrs):
    parts = []
    for a in arrs:
        flat = a.reshape(-1)
        parts.append(jnp.pad(flat, (0, (-flat.shape[0]) % PACK_TILE)))
    return jnp.concatenate(parts).reshape(-1, LANES)


def _adam_replicated(gathered, sizes, ws, ms, vs, name):
    n_w = len(ws)
    r_dev = gathered.shape[0] // N_DEV
    offs, off = [], 0
    for n in sizes:
        offs.append(off)
        off += (n + PACK_TILE - 1) // PACK_TILE * SUBLANES
    assert off == r_dev

    def body(*refs):
        g_ref = refs[0]
        w_refs, m_refs, v_refs = refs[1:1 + n_w], refs[1 + n_w:1 + 2 * n_w], refs[1 + 2 * n_w:1 + 3 * n_w]
        outs = refs[1 + 3 * n_w:]

        def total(i, shape):
            r, cdim = shape
            acc = g_ref[pl.ds(offs[i], r), :cdim]
            for d in range(1, N_DEV):
                acc = acc + g_ref[pl.ds(d * r_dev + offs[i], r), :cdim]
            return acc

        for i in range(n_w):
            g = total(i, w_refs[i].shape)
            d, mn, vn = _adam_math(w_refs[i][...], g, m_refs[i][...], v_refs[i][...])
            for o, val in zip(outs[4 * i:4 * i + 4], (g, d, mn, vn)):
                o[...] = val
        outs[4 * n_w][...] = total(n_w, (SUBLANES, LANES))

    shapes = [w.shape for w in ws]
    return pl.pallas_call(
        body, name=name,
        out_shape=[jax.ShapeDtypeStruct(shp, F32) for shp in shapes for _ in range(4)]
        + [jax.ShapeDtypeStruct((SUBLANES, LANES), F32)],
        compiler_params=_params(),
    )(gathered, *ws, *ms, *vs)


def _step(x, mem, target, shards, sm, core):
    g, w, sums, reduced = {}, {}, {}, {}

    def gather(names):
        return _gather_side(names, [shards[n] for n in names])

    def to_sibling(names):
        return _sibling_side(names, [g[n] for n in names])

    def add_sibling(names, received):
        for n, r in zip(names, received):
            sums[n] = _reduce_add(g[n], r, BIG[BIG_INDEX[n]][2], core, "reduce_add_" + n)

    def to_chips(names):
        return _chips_side([sums[n][1] for n in names])

    def keep(names, received):
        for n, r in zip(names, received):
            reduced[n] = (sums[n][0], r)

    row = lambda a: a.reshape(1, -1)
    g_mix, g_xa, g_mlp, g_mem = row(sm["g_mix"]), row(sm["g_xa"]), row(sm["g_mlp"]), row(sm["g_mem"])
    g_os, g_ob = row(sm["g_out_ssm"]), row(sm["g_out_sb"])
    sb_gq, sb_gk = jnp.tile(row(sm["sb_g_q"]), (1, SB_HEADS)), jnp.tile(row(sm["sb_g_k"]), (1, SB_HEADS))
    xa_gq, xa_gk = row(sm["xa_g_q"]), row(sm["xa_g_k"])
    d_skip = row(sm["ssm_d"])

    h1, (w["w_in"],) = _norm_fwd(x, g_mix, "norm_mix", side=gather(["w_in"]))
    proj = _mm(h1, w["w_in"], "nn", "in_proj")
    u = proj
    q_raw, k_raw = (proj, SB_WIDTH, 1), (proj, SB_WIDTH, 2)
    v_sb = proj[:, SSM_WIDTH + 2 * SB_WIDTH:].astype(BF16)
    sb_scale = SB_HEAD_DIM ** -0.5
    qk_norm = lambda scale: (lambda xt, gt: ((_rms_groups(xt, gt, scale),), ()))
    qs = _rw(qk_norm(sb_scale), [q_raw], [sb_gq], [(SB_WIDTH, BF16)], [], "sb_qnorm")[0]
    ks = _rw(qk_norm(1.0), [k_raw], [sb_gk], [(SB_WIDTH, BF16)], [], "sb_knorm")[0]
    early = ["ssm_w_glu", "w_out", "xa_w_q", "xa_w_kv", "xa_w_o", "w_up"]
    y_sb, got = _sb_fwd(qs, ks, v_sb, "sb_fwd", side=gather(early))
    w.update(zip(early, got))

    ssm_args = (sm["ssm_a_re"], sm["ssm_a_im"], sm["ssm_log_dt"], sm["ssm_b_re"], sm["ssm_b_im"],
                sm["ssm_c_re"], sm["ssm_c_im"])
    (acat, bsup, csup), mats_vjp = jax.vjp(_ssm_mats, *ssm_args)
    (states, y0, y1), (w["w_down"],) = _ssm_fwd(u, acat, bsup, csup, d_skip, "ssm_fwd", side=gather(["w_down"]))
    z_glu, y_ssm = _mm(y1, w["ssm_w_glu"], "nn", "ssm_glu", epi=lambda r, yt: (r, yt * jax.nn.sigmoid(r)),
                       extras=(y1,), out_dtypes=(F32, F32))

    def cat_norm(a, b, ga, gb):
        return jnp.concatenate([_rms(a, ga), _rms(b, gb)], axis=1)

    ycat = _rw(lambda a, b, ga, gb: ((cat_norm(a, b, ga, gb),), ()), [y_ssm, y_sb], [g_os, g_ob],
               [(D_MODEL, BF16)], [], "norm_out")[0]
    x1 = _mm(ycat, w["w_out"], "nn", "out_proj", epi=lambda r, xt: (r + xt,), extras=(x,))
    h2 = _norm_fwd(x1, g_xa, "norm_xa")
    qx = _mm(h2, w["xa_w_q"], "nn", "xa_q")
    memn, kv, kn_x, vv_x = _mem_fwd(mem, g_mem, w["xa_w_kv"], xa_gk, "xa_mem")
    o_xa = _xa_fwd(qx, xa_gq, kn_x, vv_x, "xa_fwd")
    x2 = _mm(o_xa, w["xa_w_o"], "nn", "xa_o", epi=lambda r, xt: (r + xt,), extras=(x1,))
    h3 = _norm_fwd(x2, g_mlp, "norm_mlp")

    def up_epi(r):
        rl = jnp.maximum(r, 0.0)
        return (rl * rl,)

    r_up = _mm(h3, w["w_up"], "nn", "mlp_up", epi=up_epi, out_dtypes=(BF16,))

    def loss_epi(r, xt, tt):
        d = r + xt - tt
        return (d * (1.0 / D_MODEL),), (jnp.sum(d * d, axis=0, keepdims=True),)

    dx3, sq = _mm(r_up, w["w_down"], "nn", "mlp_down", epi=loss_epi, extras=(x2, target), sums=[(1, D_MODEL)])
    loss = jnp.sum(sq) * (0.5 / D_MODEL)

    def norm_bwd_epi(r, xt, drt, gt):
        _, vjp = jax.vjp(_rms, xt, gt)
        dx_, dg_ = vjp(r)
        return (dx_ + drt,), (dg_,)

    g["w_down"] = _mm(r_up, dx3, "tn", "d_w_down", tk=2048)
    da = _mm(dx3, w["w_down"], "nt", "d_r", epi=lambda r, rt: (r * 2.0 * jnp.sqrt(rt.astype(F32)),), extras=(r_up,),
             out_dtypes=(BF16,))
    g["w_up"] = _mm(h3, da, "tn", "d_w_up", tk=2048)
    mlp = ["w_down", "w_up"]
    (dx2, g["g_mlp"]), got = _mm(da, w["w_up"], "nt", "d_h3", epi=norm_bwd_epi, extras=(x2, dx3), fulls=(g_mlp,),
                                 sums=[g_mlp.shape], side=to_sibling(mlp))
    add_sibling(mlp, got)
    g["xa_w_o"] = _mm(o_xa, dx2, "tn", "d_xa_w_o", tk=2048)
    do_xa = _mm(dx2, w["xa_w_o"], "nt", "d_o_xa")
    dqx, dkn_x, dvv_x, g["xa_g_q"] = _xa_bwd(qx, xa_gq, kn_x, vv_x, do_xa, "xa_bwd")
    g["xa_w_kv"], g["g_mem"], g["xa_g_k"] = _mem_bwd(mem, g_mem, memn, w["xa_w_kv"], kv, xa_gk, dkn_x, dvv_x,
                                                     "xa_mem_bwd")
    g["xa_w_q"] = _mm(h2, dqx, "tn", "d_xa_w_q", tk=2048)
    dx1, g["g_xa"] = _mm(dqx, w["xa_w_q"], "nt", "d_h2", epi=norm_bwd_epi, extras=(x1, dx2), fulls=(g_xa,),
                         sums=[g_xa.shape])
    g["w_out"] = _mm(ycat, dx1, "tn", "d_w_out", tk=2048)
    dycat = _mm(dx1, w["w_out"], "nt", "d_ycat")

    def cat_bwd(a, b, dy, ga, gb):
        _, vjp = jax.vjp(cat_norm, a, b, ga, gb)
        da_, db_, dga, dgb = vjp(dy)
        return (da_, db_), (dga, dgb)

    dy_ssm, dy_sb, g["g_out_ssm"], g["g_out_sb"] = _rw(
        cat_bwd, [y_ssm, y_sb, dycat], [g_os, g_ob], [(SSM_WIDTH, F32), (SB_WIDTH, F32)], [g_os.shape, g_ob.shape],
        "d_norm_out")

    def glu_bwd(dy, yt, zt):
        sg = jax.nn.sigmoid(zt)
        return (dy * sg, dy * yt * sg * (1.0 - sg)), ()

    dy1_a, dz = _rw(glu_bwd, [dy_ssm, y1, z_glu], [], [(SSM_WIDTH, F32), (SSM_WIDTH, BF16)], [], "d_glu")
    g["ssm_w_glu"] = _mm(y1, dz, "tn", "d_w_glu", tk=2048)

    def gelu_bwd_epi(r, da_, y0t):
        _, vjp = jax.vjp(jax.nn.gelu, y0t)
        return (vjp(r + da_)[0],)

    mid = ["w_out", "xa_w_q", "xa_w_kv", "xa_w_o", "ssm_w_glu"]
    dy0, got = _mm(dz, w["ssm_w_glu"], "nt", "d_y1", epi=gelu_bwd_epi, extras=(dy1_a, y0), side=to_sibling(mid))
    add_sibling(mid, got)
    (du, da8, d_bsup, d_csup, g["ssm_d"]), got = _ssm_bwd(dy0, states, u, acat, bsup, csup, d_skip, "ssm_bwd",
                                                          side=to_chips(mlp))
    keep(mlp, got)
    d_acat = jnp.sum(da8, axis=0, keepdims=True)
    for nm, val in zip(("ssm_a_re", "ssm_a_im", "ssm_log_dt", "ssm_b_re", "ssm_b_im", "ssm_c_re", "ssm_c_im"),
                       mats_vjp((d_acat, d_bsup, d_csup))):
        g[nm] = val

    (dqs, dkt, dvt), got = _sb_bwd(qs, ks, v_sb, y_sb, dy_sb, "sb_bwd", side=to_chips(mid))
    keep(mid, got)

    def qk_norm_bwd(scale):
        def fn(xt, dt, gt):
            _, vjp = jax.vjp(lambda a, b_: _rms_groups(a, b_, scale), xt, gt)
            dx_, dg_ = vjp(dt)
            return (dx_,), (dg_,)
        return fn

    dq_raw, dgq = _rw(qk_norm_bwd(sb_scale), [q_raw, dqs], [sb_gq], [(SB_WIDTH, F32)], [sb_gq.shape], "d_sb_qnorm")
    dk_raw, dgk = _rw(qk_norm_bwd(1.0), [k_raw, _from_key_blocks(dkt)], [sb_gk], [(SB_WIDTH, F32)], [sb_gk.shape],
                      "d_sb_knorm")
    g["sb_g_q"] = jnp.sum(dgq.reshape(SB_HEADS, SB_HEAD_DIM), axis=0)
    g["sb_g_k"] = jnp.sum(dgk.reshape(SB_HEADS, SB_HEAD_DIM), axis=0)
    dproj = jnp.concatenate([du, dq_raw, dk_raw, _from_key_blocks(dvt)], axis=1)
    g["w_in"] = _mm(h1, dproj, "tn", "d_w_in", tk=2048)
    dh1, got = _mm(dproj, w["w_in"], "nt", "d_h1", side=to_sibling(["w_in"]))
    add_sibling(["w_in"], got)
    (dx, g["g_mix"]), got = _norm_bwd(x, g_mix, dh1, dx1, "d_norm_mix", side=to_chips(["w_in"]))
    keep(["w_in"], got)
    return loss, dx, g, reduced


def kernel(x, mem, g_mix, w_in, ssm_a_re, ssm_a_im, ssm_log_dt, ssm_b_re, ssm_b_im, ssm_c_re, ssm_c_im, ssm_d, ssm_w_glu, sb_g_q, sb_g_k, g_out_ssm, g_out_sb, w_out, g_xa, g_mem, xa_w_q, xa_w_kv, xa_g_q, xa_g_k, xa_w_o, g_mlp, w_up, w_down, loss_target, m_g_mix, m_w_in, m_ssm_a_re, m_ssm_a_im, m_ssm_log_dt, m_ssm_b_re, m_ssm_b_im, m_ssm_c_re, m_ssm_c_im, m_ssm_d, m_ssm_w_glu, m_sb_g_q, m_sb_g_k, m_g_out_ssm, m_g_out_sb, m_w_out, m_g_xa, m_g_mem, m_xa_w_q, m_xa_w_kv, m_xa_g_q, m_xa_g_k, m_xa_w_o, m_g_mlp, m_w_up, m_w_down, v_g_mix, v_w_in, v_ssm_a_re, v_ssm_a_im, v_ssm_log_dt, v_ssm_b_re, v_ssm_b_im, v_ssm_c_re, v_ssm_c_im, v_ssm_d, v_ssm_w_glu, v_sb_g_q, v_sb_g_k, v_g_out_ssm, v_g_out_sb, v_w_out, v_g_xa, v_g_mem, v_xa_w_q, v_xa_w_kv, v_xa_g_q, v_xa_g_k, v_xa_w_o, v_g_mlp, v_w_up, v_w_down):
    given = dict(locals())
    order = ["g_mix", "w_in", "ssm_a_re", "ssm_a_im", "ssm_log_dt", "ssm_b_re", "ssm_b_im", "ssm_c_re", "ssm_c_im",
             "ssm_d", "ssm_w_glu", "sb_g_q", "sb_g_k", "g_out_ssm", "g_out_sb", "w_out", "g_xa", "g_mem", "xa_w_q",
             "xa_w_kv", "xa_g_q", "xa_g_k", "xa_w_o", "g_mlp", "w_up", "w_down"]
    assert sorted([n for n, _, _ in BIG] + SMALL) == sorted(order)
    core = lax.axis_index("c").astype(jnp.int32).reshape(1)
    chip = (2 * lax.axis_index("x") + lax.axis_index("y")).astype(jnp.int32).reshape(1)

    shards = {n: given[n][0].astype(BF16) for n, _, _ in BIG}
    sm = {n: given[n][0] for n in SMALL}
    loss, dx, g, reduced = _step(x[0], mem[0], loss_target[0], shards, sm, core)

    res = {}
    for n, _, _ in BIG:
        own, recv = reduced[n]
        outs = _adam_sharded(own, recv, given[n][0], given["m_" + n][0], given["v_" + n][0], chip, "adam_" + n)
        for kind, val in zip(("grad", "delta", "new_m", "new_v"), outs):
            res[kind + "_" + n] = val[None]

    sizes = [math.prod(sm[n].shape) for n in SMALL] + [1]
    packed = _pack_small([g[n] for n in SMALL] + [loss.reshape(1)])
    everyone = _all_gather(packed, "gather_small")
    nat = lambda a: a.reshape(_natural_2d(math.prod(a.shape)))
    outs = _adam_replicated(everyone, sizes, [nat(sm[n]) for n in SMALL], [nat(given["m_" + n][0]) for n in SMALL],
                            [nat(given["v_" + n][0]) for n in SMALL], "adam_replicated")
    for i, n in enumerate(SMALL):
        for kind, val in zip(("grad", "delta", "new_m", "new_v"), outs[4 * i:4 * i + 4]):
            res[kind + "_" + n] = val.reshape(given[n].shape)
    loss_out = outs[-1][0, 0]
    return (loss_out, dx[None], *[res["grad_" + n] for n in order], *[res["delta_" + n] for n in order],
            *[res["new_m_" + n] for n in order], *[res["new_v_" + n] for n in order])
```

```python
import functools
import math

import jax
import jax.numpy as jnp
from jax import lax
from jax.experimental import pallas as pl
from jax.experimental.pallas import tpu as pltpu

F32 = jnp.float32
BF16 = jnp.bfloat16
MESH = pl.DeviceIdType.MESH

N_DEV = 8
D_MODEL = 1024
SSM_WIDTH = 512
SSM_GROUP = 16
SSM_GROUPS = 32
SSM_STATE = 64
N_STATE = SSM_GROUPS * SSM_STATE
SB_HEADS = 8
SB_HEAD_DIM = 64
SB_WIDTH = 512
IN_WIDTH = 2048
XA_HEADS = 4
XA_HEAD_DIM = 128
XA_WIDTH = 512
D_FF = 4096
NORM_EPS = 1e-6
ADAM_LR = 0.001
ADAM_B1 = 0.9
ADAM_B2 = 0.999
ADAM_EPS = 1e-08
ADAM_WD = 0.01
ADAM_STEP = 10

LANES = 128
SUBLANES = 8
VMEM_LIMIT = 48 * 1024 * 1024
SCAN_LANES = 512
SB_BLOCK = 256
SB_UNDERFLOW = -110.0

NN = (((1,), (0,)), ((), ()))
NT = (((1,), (1,)), ((), ()))
TN = (((0,), (0,)), ((), ()))


def _params(sem=None):
    return pltpu.CompilerParams(dimension_semantics=sem, vmem_limit_bytes=VMEM_LIMIT)


def _dot(a, b, dims=NN):
    return lax.dot_general(a.astype(BF16), b.astype(BF16), dims, preferred_element_type=F32)


def _rms(x, g):
    return x * lax.rsqrt(jnp.mean(x * x, axis=-1, keepdims=True) + NORM_EPS) * g


ANY = pl.BlockSpec(memory_space=pl.ANY)


class _Side:
    def __init__(self, ins, out_shapes, n_sem, make):
        self.ins, self.out_shapes, self.n_sem, self.make = list(ins), list(out_shapes), n_sem, make

    def sems(self):
        return [pltpu.SemaphoreType.DMA((self.n_sem,)), pltpu.SemaphoreType.DMA((self.n_sem,))]


def _hosted(body, side, n_in, n_out, grid):
    if side is None:
        return body
    ns_in, ns_out = len(side.ins), len(side.out_shapes)

    def wrapped(*refs):
        ins, refs = refs[:n_in], refs[n_in:]
        s_ins, refs = refs[:ns_in], refs[ns_in:]
        outs, refs = refs[:n_out], refs[n_out:]
        s_outs, refs = refs[:ns_out], refs[ns_out:]
        scratch, sems = refs[:-2], refs[-2:]
        ids = [pl.program_id(d) for d in range(len(grid))]
        first = functools.reduce(jnp.logical_and, [i == 0 for i in ids])
        last = functools.reduce(jnp.logical_and, [i == n - 1 for i, n in zip(ids, grid)])

        @pl.when(first)
        def _():
            for cp in side.make(s_ins, s_outs, *sems):
                cp.start()

        body(*ins, *outs, *scratch)

        @pl.when(last)
        def _():
            for cp in side.make(s_ins, s_outs, *sems):
                cp.wait()

    return wrapped


def _side_args(side):
    if side is None:
        return [], [], [], [], []
    return ([ANY] * len(side.ins), [ANY] * len(side.out_shapes), side.out_shapes, side.sems(), side.ins)


def _split_side(res, n_out, side):
    res = list(res)
    main = res[0] if n_out == 1 else res[:n_out]
    return main if side is None else (main, res[n_out:])


def _mm(a, b, mode, name, *, epi=None, extras=(), fulls=(), out_dtypes=(F32,), sums=(), tm=1024, tn=1024, tk=1024,
        side=None):
    if mode == "nn":
        (m, k), (k2, n) = a.shape, b.shape
    elif mode == "nt":
        (m, k), (n, k2) = a.shape, b.shape
    else:
        (k, m), (k2, n) = a.shape, b.shape
    assert k == k2, (name, a.shape, b.shape)
    tm, tn, tk = min(tm, m), min(tn, n), min(tk, k)
    assert m % tm == 0 and n % tn == 0 and k % tk == 0, (name, m, n, k)
    nk = k // tk
    dims = {"nn": NN, "nt": NT, "tn": TN}[mode]
    if mode == "tn":
        a_spec = pl.BlockSpec((tk, tm), lambda i, j, kk: (kk, i))
    else:
        a_spec = pl.BlockSpec((tm, tk), lambda i, j, kk: (i, kk))
    if mode == "nt":
        b_spec = pl.BlockSpec((tn, tk), lambda i, j, kk: (j, kk))
    else:
        b_spec = pl.BlockSpec((tk, tn), lambda i, j, kk: (kk, j))
    mn_spec = pl.BlockSpec((tm, tn), lambda i, j, kk: (i, j))
    n_ex, n_full, n_out, n_sum = len(extras), len(fulls), len(out_dtypes), len(sums)
    n_in = 2 + n_ex + n_full

    def body(*refs):
        a_ref, b_ref = refs[:2]
        ex = refs[2:n_in]
        outs = refs[n_in:n_in + n_out]
        sum_refs = refs[n_in + n_out:n_in + n_out + n_sum]
        kk = pl.program_id(2)
        first_tile = jnp.logical_and(pl.program_id(0) == 0, pl.program_id(1) == 0)

        def finish(r):
            vals = epi(r, *[e[...] for e in ex]) if epi is not None else (r,)
            if n_sum:
                vals, parts = vals

                @pl.when(first_tile)
                def _():
                    for sr in sum_refs:
                        sr[...] = jnp.zeros_like(sr)

                for sr, p in zip(sum_refs, parts):
                    sr[...] += p
            for o, v in zip(outs, vals):
                o[...] = v.astype(o.dtype)

        if nk == 1:
            finish(_dot(a_ref[...], b_ref[...], dims))
        else:
            acc = refs[n_in + n_out + n_sum]

            @pl.when(kk == 0)
            def _():
                acc[...] = jnp.zeros_like(acc)

            acc[...] += _dot(a_ref[...], b_ref[...], dims)

            @pl.when(kk == nk - 1)
            def _():
                finish(acc[...])

    grid = (m // tm, n // tn, nk)
    whole = lambda shape: pl.BlockSpec(shape, lambda i, j, kk: (0,) * len(shape))
    s_in, s_out, s_shape, s_scratch, s_ops = _side_args(side)
    seq = bool(side) or n_sum > 0
    res = pl.pallas_call(
        _hosted(body, side, n_in, n_out + n_sum, grid), name=name, grid=grid,
        in_specs=[a_spec, b_spec] + [mn_spec] * n_ex + [whole(f.shape) for f in fulls] + s_in,
        out_specs=[mn_spec] * n_out + [whole(shape) for shape in sums] + s_out,
        out_shape=[jax.ShapeDtypeStruct((m, n), dt) for dt in out_dtypes]
        + [jax.ShapeDtypeStruct(shape, F32) for shape in sums] + s_shape,
        scratch_shapes=([pltpu.VMEM((tm, tn), F32)] if nk > 1 else []) + s_scratch,
        compiler_params=_params(("arbitrary",) * 3 if seq else ("parallel", "parallel", "arbitrary")),
    )(a, b, *extras, *fulls, *s_ops)
    return _split_side(res, n_out + n_sum, side)


def _row_tile(s, target):
    if s <= target:
        return s
    return max(t for t in range(16, target + 1, 16) if s % t == 0)


def _rw(fn, rows, fulls, row_out, acc_out, name, tm=512, side=None):
    cols = [r[1:] if isinstance(r, tuple) else (r.shape[1], 0) for r in rows]
    rows = [r[0] if isinstance(r, tuple) else r for r in rows]
    s = rows[0].shape[0]
    tm = _row_tile(s, tm)
    nr, nf, nro, nao = len(rows), len(fulls), len(row_out), len(acc_out)

    def body(*refs):
        r = refs[:nr]
        f = refs[nr:nr + nf]
        ro = refs[nr + nf:nr + nf + nro]
        ao = refs[nr + nf + nro:]
        outs, accs = fn(*[x[...] for x in r], *[x[...] for x in f])
        for o, v in zip(ro, outs):
            o[...] = v.astype(o.dtype)
        if nao:
            @pl.when(pl.program_id(0) == 0)
            def _():
                for a in ao:
                    a[...] = jnp.zeros_like(a)

            for a, v in zip(ao, accs):
                a[...] += v

    full_spec = lambda shape: pl.BlockSpec(shape, lambda i: (0,) * len(shape))
    s_in, s_out, s_shape, s_scratch, s_ops = _side_args(side)
    res = pl.pallas_call(
        _hosted(body, side, nr + nf, nro + nao, (s // tm,)), name=name, grid=(s // tm,),
        in_specs=[pl.BlockSpec((tm, wd), functools.partial(lambda i, cb: (i, cb), cb=cb)) for wd, cb in cols]
        + [full_spec(x.shape) for x in fulls] + s_in,
        out_specs=[pl.BlockSpec((tm, d), lambda i: (i, 0)) for d, _ in row_out]
        + [full_spec(shape) for shape in acc_out] + s_out,
        out_shape=[jax.ShapeDtypeStruct((s, d), dt) for d, dt in row_out]
        + [jax.ShapeDtypeStruct(shape, F32) for shape in acc_out] + s_shape,
        scratch_shapes=s_scratch,
        compiler_params=_params(("arbitrary",)),
    )(*rows, *fulls, *s_ops)
    res = list(res)
    return res if side is None else (res[:nro + nao], res[nro + nao:])


def _norm_fwd(x, g, name, side=None):
    res = _rw(lambda xt, gt: ((_rms(xt, gt),), ()), [x], [g], [(x.shape[1], BF16)], [], name, side=side)
    return res[0] if side is None else (res[0][0], res[1])


def _norm_bwd(x, g, dh, dres, name, side=None):
    def fn(xt, dht, drt, gt):
        _, vjp = jax.vjp(_rms, xt, gt)
        dx, dg = vjp(dht)
        return (dx + drt,), (dg,)

    return _rw(fn, [x, dh, dres], [g], [(x.shape[1], F32)], [g.shape], name, side=side)


def _rms_groups(x, g, scale):
    lo = lax.broadcasted_iota(jnp.int32, (1, LANES), 1) < SB_HEAD_DIM
    x2 = x * x
    outs = []
    for cb in range(x.shape[1] // LANES):
        sl = slice(cb * LANES, (cb + 1) * LANES)
        s_lo = jnp.sum(jnp.where(lo, x2[:, sl], 0.0), axis=-1, keepdims=True)
        s_hi = jnp.sum(jnp.where(lo, 0.0, x2[:, sl]), axis=-1, keepdims=True)
        r = jnp.where(lo, lax.rsqrt(s_lo * (1.0 / SB_HEAD_DIM) + NORM_EPS),
                      lax.rsqrt(s_hi * (1.0 / SB_HEAD_DIM) + NORM_EPS))
        outs.append(x[:, sl] * r)
    return jnp.concatenate(outs, axis=1) * g * scale


def _log_sigmoid(z):
    return jnp.minimum(z, 0.0) - jnp.log(1.0 + jnp.exp(-jnp.abs(z)))


def _split_dot(x, u2):
    hi = x.astype(BF16)
    lo = (x - hi.astype(F32)).astype(BF16)
    return jnp.dot(jnp.concatenate([hi, lo], axis=1), u2, preferred_element_type=F32)


def _sb_consts(b):
    row = lax.broadcasted_iota(jnp.int32, (b, b), 0)
    col = lax.broadcasted_iota(jnp.int32, (b, b), 1)
    tri = col < row
    u_after = (row > col).astype(BF16)
    u_from = (row >= col).astype(BF16)
    stack = lambda u: jnp.concatenate([u, u], axis=0)
    lane_lo = lax.broadcasted_iota(jnp.int32, (b, LANES), 1) < SB_HEAD_DIM
    return tri, stack(u_after), stack(u_from), lane_lo


def _sb_scores(qh, kb, a_run, keep, u2_after):
    z = lax.dot_general(qh, kb, NT, preferred_element_type=F32)
    lb = _log_sigmoid(z)
    l = lb - z
    if keep is not None:
        l = jnp.where(keep, l, 0.0)
    w = jnp.exp(lb + (a_run + _split_dot(l, u2_after)))
    if keep is not None:
        w = jnp.where(keep, w, 0.0)
    return lb, l, w


def _sb_walk(qi, carry, step):
    def cond(state):
        n, c = state
        return jnp.logical_and(n <= qi, jnp.max(jnp.maximum(c[0], c[1])) > SB_UNDERFLOW)

    def body(state):
        n, c = state
        return n + 1, step(n, c)

    return lax.while_loop(cond, body, (jnp.int32(2), carry))[1]


def _two_heads(x, lane_lo):
    zero = jnp.zeros_like(x)
    return jnp.where(lane_lo, x, zero), jnp.where(lane_lo, zero, x)


def _sb_fwd(qs, ks, v, name, v_col=0, side=None):
    s, width = qs.shape
    b = min(SB_BLOCK, s)

    def body(q_ref, k_ref, v_ref, o_ref):
        qi = pl.program_id(1)
        tri, u2_after, _, lane_lo = _sb_consts(b)
        q_a, q_b = _two_heads(q_ref[...], lane_lo)

        def step(n, carry, keep):
            a_a, a_b, acc = carry
            off = pl.multiple_of(jnp.maximum(qi - n, 0) * b, b)
            kb = k_ref[pl.ds(off, b), :]
            v_a, v_b = _two_heads(v_ref[pl.ds(off, b), :].astype(BF16), lane_lo)
            _, l_a, w_a = _sb_scores(q_a, kb, a_a, keep, u2_after)
            _, l_b, w_b = _sb_scores(q_b, kb, a_b, keep, u2_after)
            acc = acc + jnp.dot(jnp.concatenate([w_a.astype(BF16), w_b.astype(BF16)], axis=1),
                                jnp.concatenate([v_a, v_b], axis=0), preferred_element_type=F32)
            return (a_a + jnp.sum(l_a, axis=1, keepdims=True), a_b + jnp.sum(l_b, axis=1, keepdims=True), acc)

        zero = jnp.zeros((b, 1), F32)
        carry = step(0, (zero, zero, jnp.zeros((b, LANES), F32)), tri)
        carry = step(1, carry, jnp.broadcast_to(qi > 0, tri.shape))
        carry = _sb_walk(qi, carry, lambda n, c: step(n, c, None))
        o_ref[...] = carry[2]

    blk = pl.BlockSpec((b, LANES), lambda hp, i: (i, hp))
    full = pl.BlockSpec((s, LANES), lambda hp, i: (0, hp))
    full_v = pl.BlockSpec((s, LANES), lambda hp, i: (0, hp + v_col))
    grid = (width // LANES, s // b)
    s_in, s_out, s_shape, s_scratch, s_ops = _side_args(side)
    res = pl.pallas_call(
        _hosted(body, side, 3, 1, grid), name=name, grid=grid,
        in_specs=[blk, full, full_v] + s_in, out_specs=[blk] + s_out,
        out_shape=[jax.ShapeDtypeStruct((s, width), F32)] + s_shape, scratch_shapes=s_scratch,
        compiler_params=_params(("arbitrary", "arbitrary")),
    )(qs, ks, v, *s_ops)
    return _split_side(res, 1, side)


def _sb_bwd(qs, ks, v, out, dout, name, v_col=0, side=None):
    s, width = qs.shape
    b = min(SB_BLOCK, s)
    nkb = s // b

    def body(q_ref, k_ref, v_ref, o_ref, do_ref, dq_ref, dkt_ref, dvt_ref):
        qi = pl.program_id(1)

        @pl.when(qi == 0)
        def _():
            dkt_ref[...] = jnp.zeros_like(dkt_ref)
            dvt_ref[...] = jnp.zeros_like(dvt_ref)

        tri, u2_after, u2_from, lane_lo = _sb_consts(b)
        q_a, q_b = _two_heads(q_ref[...], lane_lo)
        dob = do_ref[...].astype(BF16)
        do_a, do_b = _two_heads(dob, lane_lo)
        prod = dob.astype(F32) * o_ref[...]
        d_a = jnp.sum(jnp.where(lane_lo, prod, 0.0), axis=1, keepdims=True)
        d_b = jnp.sum(jnp.where(lane_lo, 0.0, prod), axis=1, keepdims=True)
        tr = lambda x: jnp.transpose(x.astype(F32)).astype(BF16)
        qt = jnp.concatenate([tr(q_a), tr(q_b)], axis=1)
        dot_ = jnp.concatenate([tr(do_a), tr(do_b)], axis=1)

        def head(qh, doh, kb, vb, a_run, d_rem, keep):
            lb, l, w = _sb_scores(qh, kb, a_run, keep, u2_after)
            wb = w.astype(BF16)
            g = lax.dot_general(doh, vb, NT, preferred_element_type=F32) * wb.astype(F32)
            g_before = d_rem - _split_dot(g, u2_from)
            dz = g - (g + g_before) * jnp.exp(lb)
            if keep is not None:
                dz = jnp.where(keep, dz, 0.0)
            return (dz.astype(BF16), wb, a_run + jnp.sum(l, axis=1, keepdims=True),
                    d_rem - jnp.sum(g, axis=1, keepdims=True))

        def step(n, carry, keep):
            a_a, a_b, r_a, r_b, dq = carry
            jb = jnp.maximum(qi - n, 0)
            off = pl.multiple_of(jb * b, b)
            kb = k_ref[pl.ds(off, b), :]
            vb = v_ref[pl.ds(off, b), :].astype(BF16)
            k_a, k_b = _two_heads(kb, lane_lo)
            dz_a, w_a, a_a, r_a = head(q_a, do_a, kb, vb, a_a, r_a, keep)
            dz_b, w_b, a_b, r_b = head(q_b, do_b, kb, vb, a_b, r_b, keep)
            dq = dq + jnp.dot(jnp.concatenate([dz_a, dz_b], axis=1), jnp.concatenate([k_a, k_b], axis=0),
                              preferred_element_type=F32)
            dkt_ref[0, jb] += jnp.dot(qt, jnp.concatenate([dz_a, dz_b], axis=0), preferred_element_type=F32)
            dvt_ref[0, jb] += jnp.dot(dot_, jnp.concatenate([w_a, w_b], axis=0), preferred_element_type=F32)
            return a_a, a_b, r_a, r_b, dq

        zero = jnp.zeros((b, 1), F32)
        carry = step(0, (zero, zero, d_a, d_b, jnp.zeros((b, LANES), F32)), tri)
        carry = step(1, carry, jnp.broadcast_to(qi > 0, tri.shape))
        carry = _sb_walk(qi, carry, lambda n, c: step(n, c, None))
        dq_ref[...] = carry[4]

    blk = pl.BlockSpec((b, LANES), lambda hp, i: (i, hp))
    full = pl.BlockSpec((s, LANES), lambda hp, i: (0, hp))
    full_v = pl.BlockSpec((s, LANES), lambda hp, i: (0, hp + v_col))
    acc = pl.BlockSpec((1, nkb, LANES, b), lambda hp, i: (hp, 0, 0, 0))
    grid = (width // LANES, nkb)
    s_in, s_out, s_shape, s_scratch, s_ops = _side_args(side)
    res = pl.pallas_call(
        _hosted(body, side, 5, 3, grid), name=name, grid=grid,
        in_specs=[blk, full, full_v, blk, blk] + s_in, out_specs=[blk, acc, acc] + s_out,
        out_shape=[jax.ShapeDtypeStruct((s, width), F32)]
        + [jax.ShapeDtypeStruct((width // LANES, nkb, LANES, b), F32)] * 2 + s_shape,
        scratch_shapes=s_scratch,
        compiler_params=_params(("arbitrary", "arbitrary")),
    )(qs, ks, v, out, dout, *s_ops)
    return _split_side(res, 3, side)


def _from_key_blocks(t):
    hp, nkb, lanes, b = t.shape
    return jnp.transpose(t, (1, 3, 0, 2)).reshape(nkb * b, hp * lanes)


def _cmul(xr, xi, yr, yi):
    return xr * yr - xi * yi, xr * yi + xi * yr


def _scan_consts(ar, ai, reverse, lc):
    rowi = lax.broadcasted_iota(jnp.int32, (SUBLANES, lc), 0)
    pows = [(ar, ai)]
    for _ in range(SUBLANES - 1):
        pows.append(_cmul(*pows[-1], ar, ai))
    steps = []
    for d in (1, 2, 4):
        keep = (rowi < SUBLANES - d) if reverse else (rowi >= d)
        pr, pi = pows[d - 1]
        steps.append((SUBLANES - d if reverse else d, jnp.where(keep, pr, 0.0), jnp.where(keep, pi, 0.0)))
    cr = jnp.zeros((SUBLANES, lc), F32)
    ci = jnp.zeros((SUBLANES, lc), F32)
    for r in range(SUBLANES):
        pr, pi = pows[SUBLANES - 1 - r] if reverse else pows[r]
        cr = jnp.where(rowi == r, pr, cr)
        ci = jnp.where(rowi == r, pi, ci)
    return steps, cr, ci


def _scan_tile(xr, xi, steps, pr, pi, cr, ci):
    for shift, ar, ai in steps:
        rr = pltpu.roll(xr, shift, 0)
        ri = pltpu.roll(xi, shift, 0)
        xr, xi = xr + ar * rr - ai * ri, xi + ar * ri + ai * rr
    return xr + pr * cr - pi * ci, xi + pr * ci + pi * cr


def _ssm_fwd(u, acat, bsup, csup, d_skip, name, tt=1024, side=None):
    s = u.shape[0]
    lc = SCAN_LANES
    tt = min(tt, s)
    nl, nt = N_STATE // lc, s // tt

    def body(u_ref, a_ref, b_ref, c_ref, d_ref, s_ref, y0_ref, y1_ref, carry):
        @pl.when(pl.program_id(1) == 0)
        def _():
            carry[...] = jnp.zeros_like(carry)

        ut = u_ref[...]
        s_ref[...] = _dot(ut, b_ref[0])
        steps, pr, pi = _scan_consts(a_ref[:, :lc], a_ref[:, lc:], False, lc)

        def tile(i, c):
            off = pl.multiple_of(i * SUBLANES, SUBLANES)
            xr, xi = _scan_tile(s_ref[pl.ds(off, SUBLANES), :lc], s_ref[pl.ds(off, SUBLANES), lc:],
                                steps, pr, pi, c[0], c[1])
            s_ref[pl.ds(off, SUBLANES), :lc] = xr
            s_ref[pl.ds(off, SUBLANES), lc:] = xi
            return (jnp.broadcast_to(xr[SUBLANES - 1:, :], (SUBLANES, lc)),
                    jnp.broadcast_to(xi[SUBLANES - 1:, :], (SUBLANES, lc)))

        cr, ci = lax.fori_loop(0, tt // SUBLANES, tile, (carry[:, :lc], carry[:, lc:]))
        carry[:, :lc] = cr
        carry[:, lc:] = ci
        y0 = _dot(s_ref[...], c_ref[0], NT) + d_ref[...] * ut
        y0_ref[...] = y0
        y1_ref[...] = jax.nn.gelu(y0)

    chan = pl.BlockSpec((tt, LANES), lambda j, c: (c, j))
    sup = pl.BlockSpec((1, LANES, 2 * lc), lambda j, c: (j, 0, 0))
    s_in, s_out, s_shape, s_scratch, s_ops = _side_args(side)
    res = pl.pallas_call(
        _hosted(body, side, 5, 3, (nl, nt)), name=name, grid=(nl, nt),
        in_specs=[chan, pl.BlockSpec((1, 2 * lc), lambda j, c: (0, j)), sup, sup,
                  pl.BlockSpec((1, LANES), lambda j, c: (0, j))] + s_in,
        out_specs=[pl.BlockSpec((tt, 2 * lc), lambda j, c: (c, j)), chan, chan] + s_out,
        out_shape=[jax.ShapeDtypeStruct((s, 2 * N_STATE), F32), jax.ShapeDtypeStruct((s, SSM_WIDTH), F32),
                   jax.ShapeDtypeStruct((s, SSM_WIDTH), F32)] + s_shape,
        scratch_shapes=[pltpu.VMEM((SUBLANES, 2 * lc), F32)] + s_scratch,
        compiler_params=_params(("arbitrary", "arbitrary")),
    )(u, acat, bsup, csup, d_skip, *s_ops)
    return _split_side(res, 3, side)


def _ssm_bwd(dy0, states, u, acat, bsup, csup, d_skip, name, tt=1024, side=None):
    s = u.shape[0]
    lc = SCAN_LANES
    tt = min(tt, s)
    nl, nt = N_STATE // lc, s // tt
    nt8 = tt // SUBLANES

    def body(dy_ref, s_ref, sp_ref, u_ref, a_ref, b_ref, c_ref, d_ref,
             du_ref, da_ref, db_ref, dc_ref, dd_ref, lam_ref, carry):
        c = pl.program_id(1)

        @pl.when(c == 0)
        def _():
            carry[...] = jnp.zeros_like(carry)
            for r in (da_ref, db_ref, dc_ref, dd_ref):
                r[...] = jnp.zeros_like(r)

        dy = dy_ref[...]
        ut = u_ref[...]
        lam_ref[...] = _dot(dy, c_ref[0])
        steps, pr, pi = _scan_consts(a_ref[:, :lc], -a_ref[:, lc:], True, lc)
        rowi = lax.broadcasted_iota(jnp.int32, (SUBLANES, lc), 0)
        first_chunk = c == nt - 1

        def tile(i, carry_v):
            cr, ci, dar, dai = carry_v
            t = nt8 - 1 - i
            off = pl.multiple_of(t * SUBLANES, SUBLANES)
            lr, li = _scan_tile(lam_ref[pl.ds(off, SUBLANES), :lc], lam_ref[pl.ds(off, SUBLANES), lc:],
                                steps, pr, pi, cr, ci)
            lam_ref[pl.ds(off, SUBLANES), :lc] = lr
            lam_ref[pl.ds(off, SUBLANES), lc:] = li
            offp = pl.multiple_of(jnp.maximum(t - 1, 0) * SUBLANES, SUBLANES)
            in_chunk = t > 0
            use = jnp.logical_or(in_chunk, jnp.logical_not(first_chunk))
            prev_r = jnp.where(in_chunk, s_ref[pl.ds(offp, SUBLANES), :lc], sp_ref[:, :lc])
            prev_i = jnp.where(in_chunk, s_ref[pl.ds(offp, SUBLANES), lc:], sp_ref[:, lc:])
            last_r = jnp.where(use, jnp.broadcast_to(prev_r[SUBLANES - 1:, :], (SUBLANES, lc)), 0.0)
            last_i = jnp.where(use, jnp.broadcast_to(prev_i[SUBLANES - 1:, :], (SUBLANES, lc)), 0.0)
            sr = jnp.where(rowi == 0, last_r, pltpu.roll(s_ref[pl.ds(off, SUBLANES), :lc], 1, 0))
            si = jnp.where(rowi == 0, last_i, pltpu.roll(s_ref[pl.ds(off, SUBLANES), lc:], 1, 0))
            dar = dar + lr * sr + li * si
            dai = dai + li * sr - lr * si
            return (jnp.broadcast_to(lr[:1, :], (SUBLANES, lc)), jnp.broadcast_to(li[:1, :], (SUBLANES, lc)),
                    dar, dai)

        zero = jnp.zeros((SUBLANES, lc), F32)
        cr, ci, dar, dai = lax.fori_loop(0, nt8, tile, (carry[:, :lc], carry[:, lc:], zero, zero))
        carry[:, :lc] = cr
        carry[:, lc:] = ci
        da_ref[:, :lc] += dar
        da_ref[:, lc:] += dai
        lam = lam_ref[...].astype(BF16)
        du_ref[...] = (_dot(lam, b_ref[0], NT) + d_ref[...] * dy).astype(du_ref.dtype)
        db_ref[0] += _dot(ut, lam, TN)
        dc_ref[0] += _dot(dy, s_ref[...], TN)
        dd_ref[...] += jnp.sum(dy * ut, axis=0, keepdims=True)

    rev = lambda j, c: (nt - 1 - c, j)
    chan = pl.BlockSpec((tt, LANES), rev)
    sup = pl.BlockSpec((1, LANES, 2 * lc), lambda j, c: (j, 0, 0))
    row = pl.BlockSpec((1, LANES), lambda j, c: (0, j))
    s_in, s_out, s_shape, s_scratch, s_ops = _side_args(side)
    res = pl.pallas_call(
        _hosted(body, side, 8, 5, (nl, nt)), name=name, grid=(nl, nt),
        in_specs=[chan, pl.BlockSpec((tt, 2 * lc), rev),
                  pl.BlockSpec((SUBLANES, 2 * lc), lambda j, c: (jnp.maximum((nt - 1 - c) * nt8 - 1, 0), j)),
                  chan, pl.BlockSpec((1, 2 * lc), lambda j, c: (0, j)), sup, sup, row] + s_in,
        out_specs=[chan, pl.BlockSpec((SUBLANES, 2 * lc), lambda j, c: (0, j)), sup, sup, row] + s_out,
        out_shape=[jax.ShapeDtypeStruct((s, SSM_WIDTH), BF16), jax.ShapeDtypeStruct((SUBLANES, 2 * N_STATE), F32),
                   jax.ShapeDtypeStruct(bsup.shape, F32), jax.ShapeDtypeStruct(csup.shape, F32),
                   jax.ShapeDtypeStruct((1, SSM_WIDTH), F32)] + s_shape,
        scratch_shapes=[pltpu.VMEM((tt, 2 * lc), F32), pltpu.VMEM((SUBLANES, 2 * lc), F32)] + s_scratch,
        compiler_params=_params(("arbitrary", "arbitrary")),
    )(dy0, states, states, u, acat, bsup, csup, d_skip, *s_ops)
    return _split_side(res, 5, side)


def _state_cols(xr, xi):
    lead = xr.shape[:-1]
    nl = N_STATE // SCAN_LANES
    both = jnp.stack([xr.reshape(lead + (nl, SCAN_LANES)), xi.reshape(lead + (nl, SCAN_LANES))], axis=-2)
    return both.reshape(lead + (2 * N_STATE,))


def _ssm_mats(a_re, a_im, log_dt, b_re, b_im, c_re, c_im):
    dt = jnp.exp(log_dt)[:, None]
    lr, li = a_re * dt, a_im * dt
    e = jnp.exp(lr)
    abar_r, abar_i = e * jnp.cos(li), e * jnp.sin(li)
    den = a_re * a_re + a_im * a_im
    coef_r = ((abar_r - 1.0) * a_re + abar_i * a_im) / den
    coef_i = (abar_i * a_re - (abar_r - 1.0) * a_im) / den
    bbar_r = coef_r[..., None] * b_re - coef_i[..., None] * b_im
    bbar_i = coef_r[..., None] * b_im + coef_i[..., None] * b_re
    nl = N_STATE // SCAN_LANES
    gpb = SSM_GROUPS // nl
    eye = jnp.eye(gpb, dtype=bool)[None, :, None, :, None]

    def sup(m_r, m_i):
        def one(m):
            m = m.reshape(nl, gpb, SSM_GROUP, 1, SSM_STATE)
            return jnp.where(eye, m, 0.0).reshape(nl, gpb * SSM_GROUP, SCAN_LANES)
        return jnp.concatenate([one(m_r), one(m_i)], axis=-1)

    acat = _state_cols(abar_r.reshape(1, N_STATE), abar_i.reshape(1, N_STATE))
    bsup = sup(jnp.transpose(bbar_r, (0, 2, 1)), jnp.transpose(bbar_i, (0, 2, 1)))
    csup = sup(c_re, -c_im)
    return acat, bsup, csup


def _mem_fwd(mem, g_mem, w_kv, g_k, name):
    ml = mem.shape[0]

    def body(mem_ref, gm_ref, w_ref, gk_ref, memn_ref, kv_ref, kn_ref, vv_ref):
        memn = _rms(mem_ref[...], gm_ref[...])
        memn_ref[...] = memn.astype(BF16)
        kv = _dot(memn, w_ref[...])
        kv_ref[...] = kv
        for hh in range(XA_HEADS):
            sl = slice(hh * XA_HEAD_DIM, (hh + 1) * XA_HEAD_DIM)
            kn_ref[:, sl] = _rms(kv[:, sl], gk_ref[...]).astype(BF16)
        vv_ref[...] = kv[:, XA_WIDTH:].astype(BF16)

    return pl.pallas_call(
        body, name=name,
        out_shape=[jax.ShapeDtypeStruct((ml, D_MODEL), BF16), jax.ShapeDtypeStruct((ml, 2 * XA_WIDTH), F32),
                   jax.ShapeDtypeStruct((ml, XA_WIDTH), BF16), jax.ShapeDtypeStruct((ml, XA_WIDTH), BF16)],
        compiler_params=_params(),
    )(mem, g_mem, w_kv, g_k)


def _mem_bwd(mem, g_mem, memn, w_kv, kv, g_k, dkn, dvv, name):
    def body(mem_ref, gm_ref, memn_ref, w_ref, kv_ref, gk_ref, dkn_ref, dvv_ref, dw_ref, dgm_ref, dgk_ref):
        kv = kv_ref[...]
        dgk = jnp.zeros(dgk_ref.shape, F32)
        parts = []
        for hh in range(XA_HEADS):
            sl = slice(hh * XA_HEAD_DIM, (hh + 1) * XA_HEAD_DIM)
            _, vjp = jax.vjp(_rms, kv[:, sl], gk_ref[...])
            dk, dg = vjp(dkn_ref[:, sl])
            parts.append(dk)
            dgk = dgk + dg
        dgk_ref[...] = dgk
        dkv = jnp.concatenate(parts + [dvv_ref[...]], axis=1)
        dw_ref[...] = _dot(memn_ref[...], dkv, TN)
        dmemn = _dot(dkv, w_ref[...], NT)
        _, vjp = jax.vjp(_rms, mem_ref[...], gm_ref[...])
        dgm_ref[...] = vjp(dmemn)[1]

    return pl.pallas_call(
        body, name=name,
        out_shape=[jax.ShapeDtypeStruct((D_MODEL, 2 * XA_WIDTH), F32), jax.ShapeDtypeStruct(g_mem.shape, F32),
                   jax.ShapeDtypeStruct(g_k.shape, F32)],
        compiler_params=_params(),
    )(mem, g_mem, memn, w_kv, kv, g_k, dkn, dvv)


def _xa_head(qx_h, g_q, kn_h, vv_h):
    qn = _rms(qx_h, g_q)
    sc = _dot(qn, kn_h, NT) * (XA_HEAD_DIM ** -0.5)
    sc = sc - jnp.max(sc, axis=-1, keepdims=True)
    e = jnp.exp(sc)
    p = e / jnp.sum(e, axis=-1, keepdims=True)
    return qn, p


def _xa_fwd(qx, g_q, kn, vv, name):
    def fn(qt, gq, knt, vvt):
        outs = []
        for hh in range(XA_HEADS):
            sl = slice(hh * XA_HEAD_DIM, (hh + 1) * XA_HEAD_DIM)
            _, p = _xa_head(qt[:, sl], gq, knt[:, sl], vvt[:, sl])
            outs.append(_dot(p, vvt[:, sl]))
        return (jnp.concatenate(outs, axis=1),), ()

    return _rw(fn, [qx], [g_q, kn, vv], [(XA_WIDTH, BF16)], [], name)[0]


def _xa_bwd(qx, g_q, kn, vv, do, name):
    def fn(qt, dot_, gq, knt, vvt):
        dqs, dks, dvs = [], [], []
        dgq = jnp.zeros_like(gq)
        for hh in range(XA_HEADS):
            sl = slice(hh * XA_HEAD_DIM, (hh + 1) * XA_HEAD_DIM)
            qn, p = _xa_head(qt[:, sl], gq, knt[:, sl], vvt[:, sl])
            doh = dot_[:, sl]
            dp = _dot(doh, vvt[:, sl], NT)
            dvs.append(_dot(p, doh, TN))
            ds = p * (dp - jnp.sum(dp * p, axis=-1, keepdims=True)) * (XA_HEAD_DIM ** -0.5)
            dqn = _dot(ds, knt[:, sl])
            dks.append(_dot(ds, qn, TN))
            _, vjp = jax.vjp(_rms, qt[:, sl], gq)
            dq, dg = vjp(dqn)
            dqs.append(dq)
            dgq = dgq + dg
        return ((jnp.concatenate(dqs, axis=1),),
                (jnp.concatenate(dks, axis=1), jnp.concatenate(dvs, axis=1), dgq))

    return _rw(fn, [qx, do], [g_q, kn, vv], [(XA_WIDTH, BF16)], [kn.shape, vv.shape, g_q.shape], name)


BIG = [
    ("w_in", (D_MODEL, IN_WIDTH), 1), ("ssm_w_glu", (SSM_WIDTH, SSM_WIDTH), 0), ("w_out", (D_MODEL, D_MODEL), 0),
    ("xa_w_q", (D_MODEL, XA_WIDTH), 0), ("xa_w_kv", (D_MODEL, 2 * XA_WIDTH), 0), ("xa_w_o", (XA_WIDTH, D_MODEL), 1),
    ("w_up", (D_MODEL, D_FF), 1), ("w_down", (D_FF, D_MODEL), 0),
]
BIG_INDEX = {n: i for i, (n, _, _) in enumerate(BIG)}


def _shard_shape(shape, axis):
    return tuple(d // N_DEV if i == axis else d for i, d in enumerate(shape))


def _shard_of(ref, axis, d):
    n = ref.shape[axis] // N_DEV
    return ref.at[pl.ds(d * n, n), :] if axis == 0 else ref.at[:, pl.ds(d * n, n)]


def _gather_side(names, shards):
    idxs = [BIG_INDEX[n] for n in names]

    def make(ins, outs, send_sems, recv_sems):
        x, y, c = lax.axis_index("x"), lax.axis_index("y"), lax.axis_index("c")
        cps = []
        for j, i in enumerate(idxs):
            mine = _shard_of(outs[j], BIG[i][2], 4 * x + 2 * y + c)
            cps.append(pltpu.make_async_copy(ins[j], mine, send_sems.at[N_DEV * j]))
            for rel in range(1, N_DEV):
                to = tuple(1 - p if rel >> bit & 1 else p for p, bit in ((x, 2), (y, 1), (c, 0)))
                cps.append(pltpu.make_async_remote_copy(
                    src_ref=ins[j], dst_ref=mine, send_sem=send_sems.at[N_DEV * j + rel],
                    recv_sem=recv_sems.at[N_DEV * j + rel], device_id=to, device_id_type=MESH))
        return cps

    return _Side(shards, [jax.ShapeDtypeStruct(BIG[i][1], BF16) for i in idxs], N_DEV * len(idxs), make)


def _sibling_side(names, grads):
    idxs = [BIG_INDEX[n] for n in names]

    def make(ins, outs, send_sems, recv_sems):
        x, y, c = lax.axis_index("x"), lax.axis_index("y"), lax.axis_index("c")
        return [pltpu.make_async_remote_copy(
            src_ref=_shard_of(ins[j], BIG[i][2], 2 * k + (1 - c)), dst_ref=outs[j].at[k],
            send_sem=send_sems.at[4 * j + k], recv_sem=recv_sems.at[4 * j + k], device_id=(x, y, 1 - c),
            device_id_type=MESH) for j, i in enumerate(idxs) for k in range(4)]

    shapes = [jax.ShapeDtypeStruct((4,) + _shard_shape(BIG[i][1], BIG[i][2]), F32) for i in idxs]
    return _Side(grads, shapes, 4 * len(idxs), make)


def _chips_side(parts):
    def make(ins, outs, send_sems, recv_sems):
        x, y, c = lax.axis_index("x"), lax.axis_index("y"), lax.axis_index("c")
        chips = [(1 - x, y), (x, 1 - y), (1 - x, 1 - y)]
        return [pltpu.make_async_remote_copy(
            src_ref=ins[j].at[2 * cx + cy], dst_ref=outs[j].at[r], send_sem=send_sems.at[3 * j + r],
            recv_sem=recv_sems.at[3 * j + r], device_id=(cx, cy, c), device_id_type=MESH)
            for r, (cx, cy) in enumerate(chips) for j in range(len(parts))]

    return _Side(parts, [jax.ShapeDtypeStruct((3,) + p.shape[1:], p.dtype) for p in parts], 3 * len(parts), make)


def _reduce_add(grad, recv, axis, core, name):
    rs, cs = recv.shape[1:]
    rt = _row_tile(rs, 256)
    nt = rs // rt

    def body(c_ref, g_ref, r_ref, p_ref, pb_ref):
        sm = g_ref[...] + r_ref[0]
        p_ref[0] = sm
        pb_ref[0] = sm.astype(BF16)

    if axis == 0:
        g_spec = pl.BlockSpec((rt, cs), lambda k, t, c_ref: ((2 * k + c_ref[0]) * nt + t, 0))
    else:
        g_spec = pl.BlockSpec((rt, cs), lambda k, t, c_ref: (t, 2 * k + c_ref[0]))
    slab = pl.BlockSpec((1, rt, cs), lambda k, t, c_ref: (k, t, 0))
    return pl.pallas_call(
        body, name=name,
        grid_spec=pltpu.PrefetchScalarGridSpec(num_scalar_prefetch=1, grid=(4, nt), in_specs=[g_spec, slab],
                                               out_specs=[slab, slab]),
        out_shape=[jax.ShapeDtypeStruct(recv.shape, F32), jax.ShapeDtypeStruct(recv.shape, BF16)],
        compiler_params=_params(("parallel", "parallel")),
    )(core, grad, recv)


def _all_gather(block, name):
    m_per, n = block.shape

    def body(x_ref, out_ref, send_sems, recv_sems, local_sem):
        x, y, c = lax.axis_index("x"), lax.axis_index("y"), lax.axis_index("c")
        me, sibling = (x, y, c), (x, y, 1 - c)
        chips = [(1 - x, y), (x, 1 - y), (1 - x, 1 - y)]

        def rows(px, py, pc):
            return out_ref.at[pl.ds((4 * px + 2 * py + pc) * m_per, m_per), :]

        def copy(k, blk, to, src=None):
            return pltpu.make_async_remote_copy(
                src_ref=rows(*blk) if src is None else src, dst_ref=rows(*blk),
                send_sem=send_sems.at[k], recv_sem=recv_sems.at[k], device_id=to, device_id_type=MESH)

        mine = pltpu.make_async_copy(x_ref, rows(*me), local_sem)
        mine.start()
        first = [copy(0, me, sibling, src=x_ref)]
        first += [copy(1 + j, me, (*chip, c), src=x_ref) for j, chip in enumerate(chips)]
        for cp in first:
            cp.start()
        passed = [copy(4 + j, (*chip, c), sibling) for j, chip in enumerate(chips)]
        for j, chip in enumerate(chips):
            copy(1 + j, (*chip, c), me).wait_recv()
            passed[j].start()
        copy(0, sibling, me).wait_recv()
        for j, chip in enumerate(chips):
            copy(4 + j, (*chip, 1 - c), me).wait_recv()
        for cp in first + passed:
            cp.wait_send()
        mine.wait()

    return pl.pallas_call(
        body, name=name, in_specs=[ANY], out_specs=ANY,
        out_shape=jax.ShapeDtypeStruct((N_DEV * m_per, n), block.dtype),
        scratch_shapes=[pltpu.SemaphoreType.DMA((7,)), pltpu.SemaphoreType.DMA((7,)), pltpu.SemaphoreType.DMA],
    )(block)


def _adam_math(w, g, m, v):
    m = ADAM_B1 * m + (1.0 - ADAM_B1) * g
    v = ADAM_B2 * v + (1.0 - ADAM_B2) * (g * g)
    m_hat = m / (1.0 - ADAM_B1 ** ADAM_STEP)
    v_hat = v / (1.0 - ADAM_B2 ** ADAM_STEP)
    delta = -ADAM_LR * (m_hat / (jnp.sqrt(v_hat) + ADAM_EPS) + ADAM_WD * w)
    return delta, m, v


def _adam_sharded(own, recv, w, m, v, chip, name):
    rs, cs = w.shape
    rt = _row_tile(rs, 256)

    def body(chip_ref, p_ref, r_ref, w_ref, m_ref, v_ref, g_out, d_out, m_out, v_out):
        g = p_ref[0] + r_ref[0].astype(F32) + r_ref[1].astype(F32) + r_ref[2].astype(F32)
        d, mn, vn = _adam_math(w_ref[...], g, m_ref[...], v_ref[...])
        g_out[...] = g
        d_out[...] = d
        m_out[...] = mn
        v_out[...] = vn

    tile = pl.BlockSpec((rt, cs), lambda t, chip_ref: (t, 0))
    return pl.pallas_call(
        body, name=name,
        grid_spec=pltpu.PrefetchScalarGridSpec(
            num_scalar_prefetch=1, grid=(rs // rt,),
            in_specs=[pl.BlockSpec((1, rt, cs), lambda t, chip_ref: (chip_ref[0], t, 0)),
                      pl.BlockSpec((3, rt, cs), lambda t, chip_ref: (0, t, 0)), tile, tile, tile],
            out_specs=[tile] * 4),
        out_shape=[jax.ShapeDtypeStruct((rs, cs), F32)] * 4,
        compiler_params=_params(("parallel",)),
    )(chip, own, recv, w, m, v)


SMALL = ["g_mix", "ssm_a_re", "ssm_a_im", "ssm_log_dt", "ssm_b_re", "ssm_b_im", "ssm_c_re", "ssm_c_im", "ssm_d",
         "sb_g_q", "sb_g_k", "g_out_ssm", "g_out_sb", "g_xa", "g_mem", "xa_g_q", "xa_g_k", "g_mlp"]
PACK_TILE = SUBLANES * LANES


def _natural_2d(n):
    return (n // LANES, LANES) if n % LANES == 0 else (1, n)


def _pack_small(arrs):
    parts = []
    for a in arrs:
        flat = a.reshape(-1)
        parts.append(jnp.pad(flat, (0, (-flat.shape[0]) % PACK_TILE)))
    return jnp.concatenate(parts).reshape(-1, LANES)


def _adam_replicated(gathered, sizes, ws, ms, vs, name):
    n_w = len(ws)
    r_dev = gathered.shape[0] // N_DEV
    offs, off = [], 0
    for n in sizes:
        offs.append(off)
        off += (n + PACK_TILE - 1) // PACK_TILE * SUBLANES
    assert off == r_dev

    def body(*refs):
        g_ref = refs[0]
        w_refs, m_refs, v_refs = refs[1:1 + n_w], refs[1 + n_w:1 + 2 * n_w], refs[1 + 2 * n_w:1 + 3 * n_w]
        outs = refs[1 + 3 * n_w:]

        def total(i, shape):
            r, cdim = shape
            acc = g_ref[pl.ds(offs[i], r), :cdim]
            for d in range(1, N_DEV):
                acc = acc + g_ref[pl.ds(d * r_dev + offs[i], r), :cdim]
            return acc

        for i in range(n_w):
            g = total(i, w_refs[i].shape)
            d, mn, vn = _adam_math(w_refs[i][...], g, m_refs[i][...], v_refs[i][...])
            for o, val in zip(outs[4 * i:4 * i + 4], (g, d, mn, vn)):
                o[...] = val
        outs[4 * n_w][...] = total(n_w, (SUBLANES, LANES))

    shapes = [w.shape for w in ws]
    return pl.pallas_call(
        body, name=name,
        out_shape=[jax.ShapeDtypeStruct(shp, F32) for shp in shapes for _ in range(4)]
        + [jax.ShapeDtypeStruct((SUBLANES, LANES), F32)],
        compiler_params=_params(),
    )(gathered, *ws, *ms, *vs)


def _step(x, mem, target, shards, sm, core):
    g, w, sums, reduced = {}, {}, {}, {}

    def gather(names):
        return _gather_side(names, [shards[n] for n in names])

    def to_sibling(names):
        return _sibling_side(names, [g[n] for n in names])

    def add_sibling(names, received):
        for n, r in zip(names, received):
            sums[n] = _reduce_add(g[n], r, BIG[BIG_INDEX[n]][2], core, "reduce_add_" + n)

    def to_chips(names):
        return _chips_side([sums[n][1] for n in names])

    def keep(names, received):
        for n, r in zip(names, received):
            reduced[n] = (sums[n][0], r)

    row = lambda a: a.reshape(1, -1)
    g_mix, g_xa, g_mlp, g_mem = row(sm["g_mix"]), row(sm["g_xa"]), row(sm["g_mlp"]), row(sm["g_mem"])
    g_os, g_ob = row(sm["g_out_ssm"]), row(sm["g_out_sb"])
    sb_gq, sb_gk = jnp.tile(row(sm["sb_g_q"]), (1, SB_HEADS)), jnp.tile(row(sm["sb_g_k"]), (1, SB_HEADS))
    xa_gq, xa_gk = row(sm["xa_g_q"]), row(sm["xa_g_k"])
    d_skip = row(sm["ssm_d"])

    h1, (w["w_in"],) = _norm_fwd(x, g_mix, "norm_mix", side=gather(["w_in"]))
    proj = _mm(h1, w["w_in"], "nn", "in_proj")
    u = proj
    q_raw, k_raw = (proj, SB_WIDTH, 1), (proj, SB_WIDTH, 2)
    v_col = (SSM_WIDTH + 2 * SB_WIDTH) // LANES
    sb_scale = SB_HEAD_DIM ** -0.5
    qk_norm = lambda scale: (lambda xt, gt: ((_rms_groups(xt, gt, scale),), ()))
    qs = _rw(qk_norm(sb_scale), [q_raw], [sb_gq], [(SB_WIDTH, BF16)], [], "sb_qnorm")[0]
    ks = _rw(qk_norm(1.0), [k_raw], [sb_gk], [(SB_WIDTH, BF16)], [], "sb_knorm")[0]
    early = ["ssm_w_glu", "w_out", "xa_w_q", "xa_w_kv", "xa_w_o", "w_up"]
    y_sb, got = _sb_fwd(qs, ks, proj, "sb_fwd", v_col=v_col, side=gather(early))
    w.update(zip(early, got))

    ssm_args = (sm["ssm_a_re"], sm["ssm_a_im"], sm["ssm_log_dt"], sm["ssm_b_re"], sm["ssm_b_im"],
                sm["ssm_c_re"], sm["ssm_c_im"])
    (acat, bsup, csup), mats_vjp = jax.vjp(_ssm_mats, *ssm_args)
    (states, y0, y1), (w["w_down"],) = _ssm_fwd(u, acat, bsup, csup, d_skip, "ssm_fwd", side=gather(["w_down"]))
    z_glu, y_ssm = _mm(y1, w["ssm_w_glu"], "nn", "ssm_glu", epi=lambda r, yt: (r, yt * jax.nn.sigmoid(r)),
                       extras=(y1,), out_dtypes=(F32, F32))

    def cat_norm(a, b, ga, gb):
        return jnp.concatenate([_rms(a, ga), _rms(b, gb)], axis=1)

    ycat = _rw(lambda a, b, ga, gb: ((cat_norm(a, b, ga, gb),), ()), [y_ssm, y_sb], [g_os, g_ob],
               [(D_MODEL, BF16)], [], "norm_out")[0]
    x1 = _mm(ycat, w["w_out"], "nn", "out_proj", epi=lambda r, xt: (r + xt,), extras=(x,))
    h2 = _norm_fwd(x1, g_xa, "norm_xa")
    qx = _mm(h2, w["xa_w_q"], "nn", "xa_q")
    memn, kv, kn_x, vv_x = _mem_fwd(mem, g_mem, w["xa_w_kv"], xa_gk, "xa_mem")
    o_xa = _xa_fwd(qx, xa_gq, kn_x, vv_x, "xa_fwd")
    x2 = _mm(o_xa, w["xa_w_o"], "nn", "xa_o", epi=lambda r, xt: (r + xt,), extras=(x1,))
    h3 = _norm_fwd(x2, g_mlp, "norm_mlp")

    def up_epi(r):
        rl = jnp.maximum(r, 0.0)
        return (rl * rl,)

    r_up = _mm(h3, w["w_up"], "nn", "mlp_up", epi=up_epi, out_dtypes=(BF16,))

    def loss_epi(r, xt, tt):
        d = r + xt - tt
        return (d * (1.0 / D_MODEL),) * 2, (jnp.sum(d * d, axis=0, keepdims=True),)

    dx3, dx3_b, sq = _mm(r_up, w["w_down"], "nn", "mlp_down", epi=loss_epi, extras=(x2, target),
                         out_dtypes=(F32, BF16), sums=[(1, D_MODEL)], tm=512)
    loss = jnp.sum(sq) * (0.5 / D_MODEL)

    def norm_bwd_epi(r, xt, drt, gt):
        _, vjp = jax.vjp(_rms, xt, gt)
        dx_, dg_ = vjp(r)
        return (dx_ + drt,) * 2, (dg_,)

    g["w_down"] = _mm(r_up, dx3_b, "tn", "d_w_down", tk=2048)
    da = _mm(dx3_b, w["w_down"], "nt", "d_r", epi=lambda r, rt: (r * 2.0 * jnp.sqrt(rt.astype(F32)),), extras=(r_up,),
             out_dtypes=(BF16,))
    g["w_up"] = _mm(h3, da, "tn", "d_w_up", tk=2048)
    mlp = ["w_down", "w_up"]
    (dx2, dx2_b, g["g_mlp"]), got = _mm(da, w["w_up"], "nt", "d_h3", epi=norm_bwd_epi, extras=(x2, dx3),
                                        fulls=(g_mlp,), out_dtypes=(F32, BF16), sums=[g_mlp.shape],
                                        side=to_sibling(mlp))
    add_sibling(mlp, got)
    g["xa_w_o"] = _mm(o_xa, dx2_b, "tn", "d_xa_w_o", tk=2048)
    do_xa = _mm(dx2_b, w["xa_w_o"], "nt", "d_o_xa")
    dqx, dkn_x, dvv_x, g["xa_g_q"] = _xa_bwd(qx, xa_gq, kn_x, vv_x, do_xa, "xa_bwd")
    g["xa_w_kv"], g["g_mem"], g["xa_g_k"] = _mem_bwd(mem, g_mem, memn, w["xa_w_kv"], kv, xa_gk, dkn_x, dvv_x,
                                                     "xa_mem_bwd")
    g["xa_w_q"] = _mm(h2, dqx, "tn", "d_xa_w_q", tk=2048)
    dx1, dx1_b, g["g_xa"] = _mm(dqx, w["xa_w_q"], "nt", "d_h2", epi=norm_bwd_epi, extras=(x1, dx2), fulls=(g_xa,),
                                out_dtypes=(F32, BF16), sums=[g_xa.shape])
    g["w_out"] = _mm(ycat, dx1_b, "tn", "d_w_out", tk=2048)
    dycat = _mm(dx1_b, w["w_out"], "nt", "d_ycat")

    def cat_bwd(a, b, dy, ga, gb):
        _, vjp = jax.vjp(cat_norm, a, b, ga, gb)
        da_, db_, dga, dgb = vjp(dy)
        return (da_, db_), (dga, dgb)

    dy_ssm, dy_sb, g["g_out_ssm"], g["g_out_sb"] = _rw(
        cat_bwd, [y_ssm, y_sb, dycat], [g_os, g_ob], [(SSM_WIDTH, F32), (SB_WIDTH, F32)], [g_os.shape, g_ob.shape],
        "d_norm_out")

    def glu_bwd(dy, yt, zt):
        sg = jax.nn.sigmoid(zt)
        return (dy * sg, dy * yt * sg * (1.0 - sg)), ()

    dy1_a, dz = _rw(glu_bwd, [dy_ssm, y1, z_glu], [], [(SSM_WIDTH, F32), (SSM_WIDTH, BF16)], [], "d_glu")
    g["ssm_w_glu"] = _mm(y1, dz, "tn", "d_w_glu", tk=2048)

    def gelu_bwd_epi(r, da_, y0t):
        _, vjp = jax.vjp(jax.nn.gelu, y0t)
        return (vjp(r + da_)[0],)

    mid = ["w_out", "xa_w_q", "xa_w_kv", "xa_w_o", "ssm_w_glu"]
    dy0, got = _mm(dz, w["ssm_w_glu"], "nt", "d_y1", epi=gelu_bwd_epi, extras=(dy1_a, y0), side=to_sibling(mid))
    add_sibling(mid, got)
    (du, da8, d_bsup, d_csup, g["ssm_d"]), got = _ssm_bwd(dy0, states, u, acat, bsup, csup, d_skip, "ssm_bwd",
                                                          side=to_chips(mlp))
    keep(mlp, got)
    d_acat = jnp.sum(da8, axis=0, keepdims=True)
    for nm, val in zip(("ssm_a_re", "ssm_a_im", "ssm_log_dt", "ssm_b_re", "ssm_b_im", "ssm_c_re", "ssm_c_im"),
                       mats_vjp((d_acat, d_bsup, d_csup))):
        g[nm] = val

    (dqs, dkt, dvt), got = _sb_bwd(qs, ks, proj, y_sb, dy_sb, "sb_bwd", v_col=v_col, side=to_chips(mid))
    keep(mid, got)

    def qk_norm_bwd(scale):
        def fn(xt, dt, gt):
            _, vjp = jax.vjp(lambda a, b_: _rms_groups(a, b_, scale), xt, gt)
            dx_, dg_ = vjp(dt)
            return (dx_,), (dg_,)
        return fn

    dq_raw, dgq = _rw(qk_norm_bwd(sb_scale), [q_raw, dqs], [sb_gq], [(SB_WIDTH, BF16)], [sb_gq.shape], "d_sb_qnorm")
    dk_raw, dgk = _rw(qk_norm_bwd(1.0), [k_raw, _from_key_blocks(dkt)], [sb_gk], [(SB_WIDTH, BF16)], [sb_gk.shape],
                      "d_sb_knorm")
    g["sb_g_q"] = jnp.sum(dgq.reshape(SB_HEADS, SB_HEAD_DIM), axis=0)
    g["sb_g_k"] = jnp.sum(dgk.reshape(SB_HEADS, SB_HEAD_DIM), axis=0)
    dproj = jnp.concatenate([du, dq_raw, dk_raw, _from_key_blocks(dvt).astype(BF16)], axis=1)
    g["w_in"] = _mm(h1, dproj, "tn", "d_w_in", tk=2048)
    dh1, got = _mm(dproj, w["w_in"], "nt", "d_h1", side=to_sibling(["w_in"]))
    add_sibling(["w_in"], got)
    (dx, g["g_mix"]), got = _norm_bwd(x, g_mix, dh1, dx1, "d_norm_mix", side=to_chips(["w_in"]))
    keep(["w_in"], got)
    return loss, dx, g, reduced


def kernel(x, mem, g_mix, w_in, ssm_a_re, ssm_a_im, ssm_log_dt, ssm_b_re, ssm_b_im, ssm_c_re, ssm_c_im, ssm_d, ssm_w_glu, sb_g_q, sb_g_k, g_out_ssm, g_out_sb, w_out, g_xa, g_mem, xa_w_q, xa_w_kv, xa_g_q, xa_g_k, xa_w_o, g_mlp, w_up, w_down, loss_target, m_g_mix, m_w_in, m_ssm_a_re, m_ssm_a_im, m_ssm_log_dt, m_ssm_b_re, m_ssm_b_im, m_ssm_c_re, m_ssm_c_im, m_ssm_d, m_ssm_w_glu, m_sb_g_q, m_sb_g_k, m_g_out_ssm, m_g_out_sb, m_w_out, m_g_xa, m_g_mem, m_xa_w_q, m_xa_w_kv, m_xa_g_q, m_xa_g_k, m_xa_w_o, m_g_mlp, m_w_up, m_w_down, v_g_mix, v_w_in, v_ssm_a_re, v_ssm_a_im, v_ssm_log_dt, v_ssm_b_re, v_ssm_b_im, v_ssm_c_re, v_ssm_c_im, v_ssm_d, v_ssm_w_glu, v_sb_g_q, v_sb_g_k, v_g_out_ssm, v_g_out_sb, v_w_out, v_g_xa, v_g_mem, v_xa_w_q, v_xa_w_kv, v_xa_g_q, v_xa_g_k, v_xa_w_o, v_g_mlp, v_w_up, v_w_down):
    given = dict(locals())
    order = ["g_mix", "w_in", "ssm_a_re", "ssm_a_im", "ssm_log_dt", "ssm_b_re", "ssm_b_im", "ssm_c_re", "ssm_c_im",
             "ssm_d", "ssm_w_glu", "sb_g_q", "sb_g_k", "g_out_ssm", "g_out_sb", "w_out", "g_xa", "g_mem", "xa_w_q",
             "xa_w_kv", "xa_g_q", "xa_g_k", "xa_w_o", "g_mlp", "w_up", "w_down"]
    assert sorted([n for n, _, _ in BIG] + SMALL) == sorted(order)
    core = lax.axis_index("c").astype(jnp.int32).reshape(1)
    chip = (2 * lax.axis_index("x") + lax.axis_index("y")).astype(jnp.int32).reshape(1)

    shards = {n: given[n][0].astype(BF16) for n, _, _ in BIG}
    sm = {n: given[n][0] for n in SMALL}
    loss, dx, g, reduced = _step(x[0], mem[0], loss_target[0], shards, sm, core)

    res = {}
    for n, _, _ in BIG:
        own, recv = reduced[n]
        outs = _adam_sharded(own, recv, given[n][0], given["m_" + n][0], given["v_" + n][0], chip, "adam_" + n)
        for kind, val in zip(("grad", "delta", "new_m", "new_v"), outs):
            res[kind + "_" + n] = val[None]

    sizes = [math.prod(sm[n].shape) for n in SMALL] + [1]
    packed = _pack_small([g[n] for n in SMALL] + [loss.reshape(1)])
    everyone = _all_gather(packed, "gather_small")
    nat = lambda a: a.reshape(_natural_2d(math.prod(a.shape)))
    outs = _adam_replicated(everyone, sizes, [nat(sm[n]) for n in SMALL], [nat(given["m_" + n][0]) for n in SMALL],
                            [nat(given["v_" + n][0]) for n in SMALL], "adam_replicated")
    for i, n in enumerate(SMALL):
        for kind, val in zip(("grad", "delta", "new_m", "new_v"), outs[4 * i:4 * i + 4]):
            res[kind + "_" + n] = val.reshape(given[n].shape)
    loss_out = outs[-1][0, 0]
    return (loss_out, dx[None], *[res["grad_" + n] for n in order], *[res["delta_" + n] for n in order],
            *[res["new_m_" + n] for n in order], *[res["new_v_" + n] for n in order])
```

```python
import functools
import math

import jax
import jax.numpy as jnp
from jax import lax
from jax.experimental import pallas as pl
from jax.experimental.pallas import tpu as pltpu

F32 = jnp.float32
BF16 = jnp.bfloat16
MESH = pl.DeviceIdType.MESH

N_DEV = 8
D_MODEL = 1024
SSM_WIDTH = 512
SSM_GROUP = 16
SSM_GROUPS = 32
SSM_STATE = 64
N_STATE = SSM_GROUPS * SSM_STATE
SB_HEADS = 8
SB_HEAD_DIM = 64
SB_WIDTH = 512
IN_WIDTH = 2048
XA_HEADS = 4
XA_HEAD_DIM = 128
XA_WIDTH = 512
D_FF = 4096
NORM_EPS = 1e-6
ADAM_LR = 0.001
ADAM_B1 = 0.9
ADAM_B2 = 0.999
ADAM_EPS = 1e-08
ADAM_WD = 0.01
ADAM_STEP = 10

LANES = 128
SUBLANES = 8
VMEM_LIMIT = 56 * 1024 * 1024
SCAN_LANES = 512
SB_BLOCK = 256
SB_UNDERFLOW = -110.0

NN = (((1,), (0,)), ((), ()))
NT = (((1,), (1,)), ((), ()))
TN = (((0,), (0,)), ((), ()))


def _params(sem=None):
    return pltpu.CompilerParams(dimension_semantics=sem, vmem_limit_bytes=VMEM_LIMIT)


def _dot(a, b, dims=NN):
    return lax.dot_general(a.astype(BF16), b.astype(BF16), dims, preferred_element_type=F32)


def _rms(x, g):
    return x * lax.rsqrt(jnp.mean(x * x, axis=-1, keepdims=True) + NORM_EPS) * g


ANY = pl.BlockSpec(memory_space=pl.ANY)


class _Side:
    def __init__(self, ins, out_shapes, n_sem, make):
        self.ins, self.out_shapes, self.n_sem, self.make = list(ins), list(out_shapes), n_sem, make

    def sems(self):
        return [pltpu.SemaphoreType.DMA((self.n_sem,)), pltpu.SemaphoreType.DMA((self.n_sem,))]


def _hosted(body, side, n_in, n_out, grid):
    if side is None:
        return body
    ns_in, ns_out = len(side.ins), len(side.out_shapes)

    def wrapped(*refs):
        ins, refs = refs[:n_in], refs[n_in:]
        s_ins, refs = refs[:ns_in], refs[ns_in:]
        outs, refs = refs[:n_out], refs[n_out:]
        s_outs, refs = refs[:ns_out], refs[ns_out:]
        scratch, sems = refs[:-2], refs[-2:]
        ids = [pl.program_id(d) for d in range(len(grid))]
        first = functools.reduce(jnp.logical_and, [i == 0 for i in ids])
        last = functools.reduce(jnp.logical_and, [i == n - 1 for i, n in zip(ids, grid)])

        @pl.when(first)
        def _():
            for cp in side.make(s_ins, s_outs, *sems):
                cp.start()

        body(*ins, *outs, *scratch)

        @pl.when(last)
        def _():
            for cp in side.make(s_ins, s_outs, *sems):
                cp.wait()

    return wrapped


def _side_args(side):
    if side is None:
        return [], [], [], [], []
    return ([ANY] * len(side.ins), [ANY] * len(side.out_shapes), side.out_shapes, side.sems(), side.ins)


def _split_side(res, n_out, side):
    res = list(res)
    main = res[0] if n_out == 1 else res[:n_out]
    return main if side is None else (main, res[n_out:])


def _mm(a, b, mode, name, *, epi=None, extras=(), fulls=(), out_dtypes=(F32,), sums=(), tm=1024, tn=1024, tk=1024,
        side=None):
    if mode == "nn":
        (m, k), (k2, n) = a.shape, b.shape
    elif mode == "nt":
        (m, k), (n, k2) = a.shape, b.shape
    else:
        (k, m), (k2, n) = a.shape, b.shape
    assert k == k2, (name, a.shape, b.shape)
    tm, tn, tk = min(tm, m), min(tn, n), min(tk, k)
    assert m % tm == 0 and n % tn == 0 and k % tk == 0, (name, m, n, k)
    nk = k // tk
    dims = {"nn": NN, "nt": NT, "tn": TN}[mode]
    if mode == "tn":
        a_spec = pl.BlockSpec((tk, tm), lambda i, j, kk: (kk, i))
    else:
        a_spec = pl.BlockSpec((tm, tk), lambda i, j, kk: (i, kk))
    if mode == "nt":
        b_spec = pl.BlockSpec((tn, tk), lambda i, j, kk: (j, kk))
    else:
        b_spec = pl.BlockSpec((tk, tn), lambda i, j, kk: (kk, j))
    mn_spec = pl.BlockSpec((tm, tn), lambda i, j, kk: (i, j))
    n_ex, n_full, n_out, n_sum = len(extras), len(fulls), len(out_dtypes), len(sums)
    n_in = 2 + n_ex + n_full

    def body(*refs):
        a_ref, b_ref = refs[:2]
        ex = refs[2:n_in]
        outs = refs[n_in:n_in + n_out]
        sum_refs = refs[n_in + n_out:n_in + n_out + n_sum]
        kk = pl.program_id(2)
        first_tile = jnp.logical_and(pl.program_id(0) == 0, pl.program_id(1) == 0)

        def finish(r):
            vals = epi(r, *[e[...] for e in ex]) if epi is not None else (r,)
            if n_sum:
                vals, parts = vals

                @pl.when(first_tile)
                def _():
                    for sr in sum_refs:
                        sr[...] = jnp.zeros_like(sr)

                for sr, p in zip(sum_refs, parts):
                    sr[...] += p
            for o, v in zip(outs, vals):
                o[...] = v.astype(o.dtype)

        if nk == 1:
            finish(_dot(a_ref[...], b_ref[...], dims))
        else:
            acc = refs[n_in + n_out + n_sum]

            @pl.when(kk == 0)
            def _():
                acc[...] = jnp.zeros_like(acc)

            acc[...] += _dot(a_ref[...], b_ref[...], dims)

            @pl.when(kk == nk - 1)
            def _():
                finish(acc[...])

    grid = (m // tm, n // tn, nk)
    whole = lambda shape: pl.BlockSpec(shape, lambda i, j, kk: (0,) * len(shape))
    s_in, s_out, s_shape, s_scratch, s_ops = _side_args(side)
    seq = bool(side) or n_sum > 0
    res = pl.pallas_call(
        _hosted(body, side, n_in, n_out + n_sum, grid), name=name, grid=grid,
        in_specs=[a_spec, b_spec] + [mn_spec] * n_ex + [whole(f.shape) for f in fulls] + s_in,
        out_specs=[mn_spec] * n_out + [whole(shape) for shape in sums] + s_out,
        out_shape=[jax.ShapeDtypeStruct((m, n), dt) for dt in out_dtypes]
        + [jax.ShapeDtypeStruct(shape, F32) for shape in sums] + s_shape,
        scratch_shapes=([pltpu.VMEM((tm, tn), F32)] if nk > 1 else []) + s_scratch,
        compiler_params=_params(("arbitrary",) * 3 if seq else ("parallel", "parallel", "arbitrary")),
    )(a, b, *extras, *fulls, *s_ops)
    return _split_side(res, n_out + n_sum, side)


def _row_tile(s, target):
    if s <= target:
        return s
    return max(t for t in range(16, target + 1, 16) if s % t == 0)


def _rw(fn, rows, fulls, row_out, acc_out, name, tm=512, side=None):
    cols = [r[1:] if isinstance(r, tuple) else (r.shape[1], 0) for r in rows]
    rows = [r[0] if isinstance(r, tuple) else r for r in rows]
    s = rows[0].shape[0]
    tm = _row_tile(s, tm)
    nr, nf, nro, nao = len(rows), len(fulls), len(row_out), len(acc_out)

    def body(*refs):
        r = refs[:nr]
        f = refs[nr:nr + nf]
        ro = refs[nr + nf:nr + nf + nro]
        ao = refs[nr + nf + nro:]
        outs, accs = fn(*[x[...] for x in r], *[x[...] for x in f])
        for o, v in zip(ro, outs):
            o[...] = v.astype(o.dtype)
        if nao:
            @pl.when(pl.program_id(0) == 0)
            def _():
                for a in ao:
                    a[...] = jnp.zeros_like(a)

            for a, v in zip(ao, accs):
                a[...] += v

    full_spec = lambda shape: pl.BlockSpec(shape, lambda i: (0,) * len(shape))
    s_in, s_out, s_shape, s_scratch, s_ops = _side_args(side)
    res = pl.pallas_call(
        _hosted(body, side, nr + nf, nro + nao, (s // tm,)), name=name, grid=(s // tm,),
        in_specs=[pl.BlockSpec((tm, wd), functools.partial(lambda i, cb: (i, cb), cb=cb)) for wd, cb in cols]
        + [full_spec(x.shape) for x in fulls] + s_in,
        out_specs=[pl.BlockSpec((tm, d), lambda i: (i, 0)) for d, _ in row_out]
        + [full_spec(shape) for shape in acc_out] + s_out,
        out_shape=[jax.ShapeDtypeStruct((s, d), dt) for d, dt in row_out]
        + [jax.ShapeDtypeStruct(shape, F32) for shape in acc_out] + s_shape,
        scratch_shapes=s_scratch,
        compiler_params=_params(("arbitrary",)),
    )(*rows, *fulls, *s_ops)
    res = list(res)
    return res if side is None else (res[:nro + nao], res[nro + nao:])


def _norm_fwd(x, g, name, side=None):
    res = _rw(lambda xt, gt: ((_rms(xt, gt),), ()), [x], [g], [(x.shape[1], BF16)], [], name, side=side)
    return res[0] if side is None else (res[0][0], res[1])


def _norm_bwd(x, g, dh, dres, name, side=None):
    def fn(xt, dht, drt, gt):
        _, vjp = jax.vjp(_rms, xt, gt)
        dx, dg = vjp(dht)
        return (dx + drt,), (dg,)

    return _rw(fn, [x, dh, dres], [g], [(x.shape[1], F32)], [g.shape], name, side=side)


def _rms_groups(x, g, scale):
    lo = lax.broadcasted_iota(jnp.int32, (1, LANES), 1) < SB_HEAD_DIM
    x2 = x * x
    outs = []
    for cb in range(x.shape[1] // LANES):
        sl = slice(cb * LANES, (cb + 1) * LANES)
        s_lo = jnp.sum(jnp.where(lo, x2[:, sl], 0.0), axis=-1, keepdims=True)
        s_hi = jnp.sum(jnp.where(lo, 0.0, x2[:, sl]), axis=-1, keepdims=True)
        r = jnp.where(lo, lax.rsqrt(s_lo * (1.0 / SB_HEAD_DIM) + NORM_EPS),
                      lax.rsqrt(s_hi * (1.0 / SB_HEAD_DIM) + NORM_EPS))
        outs.append(x[:, sl] * r)
    return jnp.concatenate(outs, axis=1) * g * scale


def _log_sigmoid(z):
    return jnp.minimum(z, 0.0) - jnp.log(1.0 + jnp.exp(-jnp.abs(z)))


def _split_dot(x, u2):
    hi = x.astype(BF16)
    lo = (x - hi.astype(F32)).astype(BF16)
    return jnp.dot(jnp.concatenate([hi, lo], axis=1), u2, preferred_element_type=F32)


def _sb_consts(b):
    row = lax.broadcasted_iota(jnp.int32, (b, b), 0)
    col = lax.broadcasted_iota(jnp.int32, (b, b), 1)
    tri = col < row
    u_after = (row > col).astype(BF16)
    u_from = (row >= col).astype(BF16)
    stack = lambda u: jnp.concatenate([u, u], axis=0)
    lane_lo = lax.broadcasted_iota(jnp.int32, (b, LANES), 1) < SB_HEAD_DIM
    return tri, stack(u_after), stack(u_from), lane_lo


def _sb_scores(qh, kb, a_run, keep, u2_after):
    z = lax.dot_general(qh, kb, NT, preferred_element_type=F32)
    lb = _log_sigmoid(z)
    l = lb - z
    if keep is not None:
        l = jnp.where(keep, l, 0.0)
    w = jnp.exp(lb + (a_run + _split_dot(l, u2_after)))
    if keep is not None:
        w = jnp.where(keep, w, 0.0)
    return lb, l, w


def _sb_walk(qi, carry, step):
    def cond(state):
        n, c = state
        return jnp.logical_and(n <= qi, jnp.max(jnp.maximum(c[0], c[1])) > SB_UNDERFLOW)

    def body(state):
        n, c = state
        return n + 1, step(n, c)

    return lax.while_loop(cond, body, (jnp.int32(2), carry))[1]


def _two_heads(x, lane_lo):
    zero = jnp.zeros_like(x)
    return jnp.where(lane_lo, x, zero), jnp.where(lane_lo, zero, x)


def _sb_fwd(qs, ks, v, name, v_col=0, side=None):
    s, width = qs.shape
    b = min(SB_BLOCK, s)

    def body(q_ref, k_ref, v_ref, o_ref):
        qi = pl.program_id(1)
        tri, u2_after, _, lane_lo = _sb_consts(b)
        q_a, q_b = _two_heads(q_ref[...], lane_lo)

        def step(n, carry, keep):
            a_a, a_b, acc = carry
            off = pl.multiple_of(jnp.maximum(qi - n, 0) * b, b)
            kb = k_ref[pl.ds(off, b), :]
            v_a, v_b = _two_heads(v_ref[pl.ds(off, b), :].astype(BF16), lane_lo)
            _, l_a, w_a = _sb_scores(q_a, kb, a_a, keep, u2_after)
            _, l_b, w_b = _sb_scores(q_b, kb, a_b, keep, u2_after)
            acc = acc + jnp.dot(jnp.concatenate([w_a.astype(BF16), w_b.astype(BF16)], axis=1),
                                jnp.concatenate([v_a, v_b], axis=0), preferred_element_type=F32)
            return (a_a + jnp.sum(l_a, axis=1, keepdims=True), a_b + jnp.sum(l_b, axis=1, keepdims=True), acc)

        zero = jnp.zeros((b, 1), F32)
        carry = step(0, (zero, zero, jnp.zeros((b, LANES), F32)), tri)
        carry = step(1, carry, jnp.broadcast_to(qi > 0, tri.shape))
        carry = _sb_walk(qi, carry, lambda n, c: step(n, c, None))
        o_ref[...] = carry[2]

    blk = pl.BlockSpec((b, LANES), lambda hp, i: (i, hp))
    full = pl.BlockSpec((s, LANES), lambda hp, i: (0, hp))
    full_v = pl.BlockSpec((s, LANES), lambda hp, i: (0, hp + v_col))
    grid = (width // LANES, s // b)
    s_in, s_out, s_shape, s_scratch, s_ops = _side_args(side)
    res = pl.pallas_call(
        _hosted(body, side, 3, 1, grid), name=name, grid=grid,
        in_specs=[blk, full, full_v] + s_in, out_specs=[blk] + s_out,
        out_shape=[jax.ShapeDtypeStruct((s, width), F32)] + s_shape, scratch_shapes=s_scratch,
        compiler_params=_params(("arbitrary", "arbitrary")),
    )(qs, ks, v, *s_ops)
    return _split_side(res, 1, side)


def _sb_bwd(qs, ks, v, out, dout, name, v_col=0, side=None):
    s, width = qs.shape
    b = min(SB_BLOCK, s)
    nkb = s // b

    def body(q_ref, k_ref, v_ref, o_ref, do_ref, dq_ref, dkt_ref, dvt_ref):
        qi = pl.program_id(1)

        @pl.when(qi == 0)
        def _():
            dkt_ref[...] = jnp.zeros_like(dkt_ref)
            dvt_ref[...] = jnp.zeros_like(dvt_ref)

        tri, u2_after, u2_from, lane_lo = _sb_consts(b)
        q_a, q_b = _two_heads(q_ref[...], lane_lo)
        dob = do_ref[...].astype(BF16)
        do_a, do_b = _two_heads(dob, lane_lo)
        prod = dob.astype(F32) * o_ref[...]
        d_a = jnp.sum(jnp.where(lane_lo, prod, 0.0), axis=1, keepdims=True)
        d_b = jnp.sum(jnp.where(lane_lo, 0.0, prod), axis=1, keepdims=True)
        tr = lambda x: jnp.transpose(x.astype(F32)).astype(BF16)
        qt = jnp.concatenate([tr(q_a), tr(q_b)], axis=1)
        dot_ = jnp.concatenate([tr(do_a), tr(do_b)], axis=1)

        def head(qh, doh, kb, vb, a_run, d_rem, keep):
            lb, l, w = _sb_scores(qh, kb, a_run, keep, u2_after)
            wb = w.astype(BF16)
            g = lax.dot_general(doh, vb, NT, preferred_element_type=F32) * wb.astype(F32)
            g_before = d_rem - _split_dot(g, u2_from)
            dz = g - (g + g_before) * jnp.exp(lb)
            if keep is not None:
                dz = jnp.where(keep, dz, 0.0)
            return (dz.astype(BF16), wb, a_run + jnp.sum(l, axis=1, keepdims=True),
                    d_rem - jnp.sum(g, axis=1, keepdims=True))

        def step(n, carry, keep):
            a_a, a_b, r_a, r_b, dq = carry
            jb = jnp.maximum(qi - n, 0)
            off = pl.multiple_of(jb * b, b)
            kb = k_ref[pl.ds(off, b), :]
            vb = v_ref[pl.ds(off, b), :].astype(BF16)
            k_a, k_b = _two_heads(kb, lane_lo)
            dz_a, w_a, a_a, r_a = head(q_a, do_a, kb, vb, a_a, r_a, keep)
            dz_b, w_b, a_b, r_b = head(q_b, do_b, kb, vb, a_b, r_b, keep)
            dq = dq + jnp.dot(jnp.concatenate([dz_a, dz_b], axis=1), jnp.concatenate([k_a, k_b], axis=0),
                              preferred_element_type=F32)
            dkt_ref[0, jb] += jnp.dot(qt, jnp.concatenate([dz_a, dz_b], axis=0), preferred_element_type=F32)
            dvt_ref[0, jb] += jnp.dot(dot_, jnp.concatenate([w_a, w_b], axis=0), preferred_element_type=F32)
            return a_a, a_b, r_a, r_b, dq

        zero = jnp.zeros((b, 1), F32)
        carry = step(0, (zero, zero, d_a, d_b, jnp.zeros((b, LANES), F32)), tri)
        carry = step(1, carry, jnp.broadcast_to(qi > 0, tri.shape))
        carry = _sb_walk(qi, carry, lambda n, c: step(n, c, None))
        dq_ref[...] = carry[4]

    blk = pl.BlockSpec((b, LANES), lambda hp, i: (i, hp))
    full = pl.BlockSpec((s, LANES), lambda hp, i: (0, hp))
    full_v = pl.BlockSpec((s, LANES), lambda hp, i: (0, hp + v_col))
    acc = pl.BlockSpec((1, nkb, LANES, b), lambda hp, i: (hp, 0, 0, 0))
    grid = (width // LANES, nkb)
    s_in, s_out, s_shape, s_scratch, s_ops = _side_args(side)
    res = pl.pallas_call(
        _hosted(body, side, 5, 3, grid), name=name, grid=grid,
        in_specs=[blk, full, full_v, blk, blk] + s_in, out_specs=[blk, acc, acc] + s_out,
        out_shape=[jax.ShapeDtypeStruct((s, width), F32)]
        + [jax.ShapeDtypeStruct((width // LANES, nkb, LANES, b), F32)] * 2 + s_shape,
        scratch_shapes=s_scratch,
        compiler_params=_params(("arbitrary", "arbitrary")),
    )(qs, ks, v, out, dout, *s_ops)
    return _split_side(res, 3, side)


def _from_key_blocks(t):
    hp, nkb, lanes, b = t.shape
    return jnp.transpose(t, (1, 3, 0, 2)).reshape(nkb * b, hp * lanes)


def _cmul(xr, xi, yr, yi):
    return xr * yr - xi * yi, xr * yi + xi * yr


def _scan_consts(ar, ai, reverse, lc):
    rowi = lax.broadcasted_iota(jnp.int32, (SUBLANES, lc), 0)
    pows = [(ar, ai)]
    for _ in range(SUBLANES - 1):
        pows.append(_cmul(*pows[-1], ar, ai))
    steps = []
    for d in (1, 2, 4):
        keep = (rowi < SUBLANES - d) if reverse else (rowi >= d)
        pr, pi = pows[d - 1]
        steps.append((SUBLANES - d if reverse else d, jnp.where(keep, pr, 0.0), jnp.where(keep, pi, 0.0)))
    cr = jnp.zeros((SUBLANES, lc), F32)
    ci = jnp.zeros((SUBLANES, lc), F32)
    for r in range(SUBLANES):
        pr, pi = pows[SUBLANES - 1 - r] if reverse else pows[r]
        cr = jnp.where(rowi == r, pr, cr)
        ci = jnp.where(rowi == r, pi, ci)
    return steps, cr, ci


def _scan_tile(xr, xi, steps, pr, pi, cr, ci):
    for shift, ar, ai in steps:
        rr = pltpu.roll(xr, shift, 0)
        ri = pltpu.roll(xi, shift, 0)
        xr, xi = xr + ar * rr - ai * ri, xi + ar * ri + ai * rr
    return xr + pr * cr - pi * ci, xi + pr * ci + pi * cr


def _ssm_fwd(u, acat, bsup, csup, d_skip, name, tt=1024, side=None):
    s = u.shape[0]
    lc = SCAN_LANES
    tt = min(tt, s)
    nl, nt = N_STATE // lc, s // tt

    def body(u_ref, a_ref, b_ref, c_ref, d_ref, s_ref, y0_ref, y1_ref, carry):
        @pl.when(pl.program_id(1) == 0)
        def _():
            carry[...] = jnp.zeros_like(carry)

        ut = u_ref[...]
        s_ref[...] = _dot(ut, b_ref[0])
        steps, pr, pi = _scan_consts(a_ref[:, :lc], a_ref[:, lc:], False, lc)

        def tile(i, c):
            off = pl.multiple_of(i * SUBLANES, SUBLANES)
            xr, xi = _scan_tile(s_ref[pl.ds(off, SUBLANES), :lc], s_ref[pl.ds(off, SUBLANES), lc:],
                                steps, pr, pi, c[0], c[1])
            s_ref[pl.ds(off, SUBLANES), :lc] = xr
            s_ref[pl.ds(off, SUBLANES), lc:] = xi
            return (jnp.broadcast_to(xr[SUBLANES - 1:, :], (SUBLANES, lc)),
                    jnp.broadcast_to(xi[SUBLANES - 1:, :], (SUBLANES, lc)))

        cr, ci = lax.fori_loop(0, tt // SUBLANES, tile, (carry[:, :lc], carry[:, lc:]))
        carry[:, :lc] = cr
        carry[:, lc:] = ci
        y0 = _dot(s_ref[...], c_ref[0], NT) + d_ref[...] * ut
        y0_ref[...] = y0
        y1_ref[...] = jax.nn.gelu(y0)

    chan = pl.BlockSpec((tt, LANES), lambda j, c: (c, j))
    sup = pl.BlockSpec((1, LANES, 2 * lc), lambda j, c: (j, 0, 0))
    s_in, s_out, s_shape, s_scratch, s_ops = _side_args(side)
    res = pl.pallas_call(
        _hosted(body, side, 5, 3, (nl, nt)), name=name, grid=(nl, nt),
        in_specs=[chan, pl.BlockSpec((1, 2 * lc), lambda j, c: (0, j)), sup, sup,
                  pl.BlockSpec((1, LANES), lambda j, c: (0, j))] + s_in,
        out_specs=[pl.BlockSpec((tt, 2 * lc), lambda j, c: (c, j)), chan, chan] + s_out,
        out_shape=[jax.ShapeDtypeStruct((s, 2 * N_STATE), F32), jax.ShapeDtypeStruct((s, SSM_WIDTH), F32),
                   jax.ShapeDtypeStruct((s, SSM_WIDTH), F32)] + s_shape,
        scratch_shapes=[pltpu.VMEM((SUBLANES, 2 * lc), F32)] + s_scratch,
        compiler_params=_params(("arbitrary", "arbitrary")),
    )(u, acat, bsup, csup, d_skip, *s_ops)
    return _split_side(res, 3, side)


def _ssm_bwd(dy0, states, u, acat, bsup, csup, d_skip, name, tt=1024, side=None):
    s = u.shape[0]
    lc = SCAN_LANES
    tt = min(tt, s)
    nl, nt = N_STATE // lc, s // tt
    nt8 = tt // SUBLANES

    def body(dy_ref, s_ref, sp_ref, u_ref, a_ref, b_ref, c_ref, d_ref,
             du_ref, da_ref, db_ref, dc_ref, dd_ref, lam_ref, carry):
        c = pl.program_id(1)

        @pl.when(c == 0)
        def _():
            carry[...] = jnp.zeros_like(carry)
            for r in (da_ref, db_ref, dc_ref, dd_ref):
                r[...] = jnp.zeros_like(r)

        dy = dy_ref[...]
        ut = u_ref[...]
        lam_ref[...] = _dot(dy, c_ref[0])
        steps, pr, pi = _scan_consts(a_ref[:, :lc], -a_ref[:, lc:], True, lc)
        rowi = lax.broadcasted_iota(jnp.int32, (SUBLANES, lc), 0)
        first_chunk = c == nt - 1

        def tile(i, carry_v):
            cr, ci, dar, dai = carry_v
            t = nt8 - 1 - i
            off = pl.multiple_of(t * SUBLANES, SUBLANES)
            lr, li = _scan_tile(lam_ref[pl.ds(off, SUBLANES), :lc], lam_ref[pl.ds(off, SUBLANES), lc:],
                                steps, pr, pi, cr, ci)
            lam_ref[pl.ds(off, SUBLANES), :lc] = lr
            lam_ref[pl.ds(off, SUBLANES), lc:] = li
            offp = pl.multiple_of(jnp.maximum(t - 1, 0) * SUBLANES, SUBLANES)
            in_chunk = t > 0
            use = jnp.logical_or(in_chunk, jnp.logical_not(first_chunk))
            prev_r = jnp.where(in_chunk, s_ref[pl.ds(offp, SUBLANES), :lc], sp_ref[:, :lc])
            prev_i = jnp.where(in_chunk, s_ref[pl.ds(offp, SUBLANES), lc:], sp_ref[:, lc:])
            last_r = jnp.where(use, jnp.broadcast_to(prev_r[SUBLANES - 1:, :], (SUBLANES, lc)), 0.0)
            last_i = jnp.where(use, jnp.broadcast_to(prev_i[SUBLANES - 1:, :], (SUBLANES, lc)), 0.0)
            sr = jnp.where(rowi == 0, last_r, pltpu.roll(s_ref[pl.ds(off, SUBLANES), :lc], 1, 0))
            si = jnp.where(rowi == 0, last_i, pltpu.roll(s_ref[pl.ds(off, SUBLANES), lc:], 1, 0))
            dar = dar + lr * sr + li * si
            dai = dai + li * sr - lr * si
            return (jnp.broadcast_to(lr[:1, :], (SUBLANES, lc)), jnp.broadcast_to(li[:1, :], (SUBLANES, lc)),
                    dar, dai)

        zero = jnp.zeros((SUBLANES, lc), F32)
        cr, ci, dar, dai = lax.fori_loop(0, nt8, tile, (carry[:, :lc], carry[:, lc:], zero, zero))
        carry[:, :lc] = cr
        carry[:, lc:] = ci
        da_ref[:, :lc] += dar
        da_ref[:, lc:] += dai
        lam = lam_ref[...].astype(BF16)
        du_ref[...] = (_dot(lam, b_ref[0], NT) + d_ref[...] * dy).astype(du_ref.dtype)
        db_ref[0] += _dot(ut, lam, TN)
        dc_ref[0] += _dot(dy, s_ref[...], TN)
        dd_ref[...] += jnp.sum(dy * ut, axis=0, keepdims=True)

    rev = lambda j, c: (nt - 1 - c, j)
    chan = pl.BlockSpec((tt, LANES), rev)
    sup = pl.BlockSpec((1, LANES, 2 * lc), lambda j, c: (j, 0, 0))
    row = pl.BlockSpec((1, LANES), lambda j, c: (0, j))
    s_in, s_out, s_shape, s_scratch, s_ops = _side_args(side)
    res = pl.pallas_call(
        _hosted(body, side, 8, 5, (nl, nt)), name=name, grid=(nl, nt),
        in_specs=[chan, pl.BlockSpec((tt, 2 * lc), rev),
                  pl.BlockSpec((SUBLANES, 2 * lc), lambda j, c: (jnp.maximum((nt - 1 - c) * nt8 - 1, 0), j)),
                  chan, pl.BlockSpec((1, 2 * lc), lambda j, c: (0, j)), sup, sup, row] + s_in,
        out_specs=[chan, pl.BlockSpec((SUBLANES, 2 * lc), lambda j, c: (0, j)), sup, sup, row] + s_out,
        out_shape=[jax.ShapeDtypeStruct((s, SSM_WIDTH), BF16), jax.ShapeDtypeStruct((SUBLANES, 2 * N_STATE), F32),
                   jax.ShapeDtypeStruct(bsup.shape, F32), jax.ShapeDtypeStruct(csup.shape, F32),
                   jax.ShapeDtypeStruct((1, SSM_WIDTH), F32)] + s_shape,
        scratch_shapes=[pltpu.VMEM((tt, 2 * lc), F32), pltpu.VMEM((SUBLANES, 2 * lc), F32)] + s_scratch,
        compiler_params=_params(("arbitrary", "arbitrary")),
    )(dy0, states, states, u, acat, bsup, csup, d_skip, *s_ops)
    return _split_side(res, 5, side)


def _state_cols(xr, xi):
    lead = xr.shape[:-1]
    nl = N_STATE // SCAN_LANES
    both = jnp.stack([xr.reshape(lead + (nl, SCAN_LANES)), xi.reshape(lead + (nl, SCAN_LANES))], axis=-2)
    return both.reshape(lead + (2 * N_STATE,))


def _ssm_mats(a_re, a_im, log_dt, b_re, b_im, c_re, c_im):
    dt = jnp.exp(log_dt)[:, None]
    lr, li = a_re * dt, a_im * dt
    e = jnp.exp(lr)
    abar_r, abar_i = e * jnp.cos(li), e * jnp.sin(li)
    den = a_re * a_re + a_im * a_im
    coef_r = ((abar_r - 1.0) * a_re + abar_i * a_im) / den
    coef_i = (abar_i * a_re - (abar_r - 1.0) * a_im) / den
    bbar_r = coef_r[..., None] * b_re - coef_i[..., None] * b_im
    bbar_i = coef_r[..., None] * b_im + coef_i[..., None] * b_re
    nl = N_STATE // SCAN_LANES
    gpb = SSM_GROUPS // nl
    eye = jnp.eye(gpb, dtype=bool)[None, :, None, :, None]

    def sup(m_r, m_i):
        def one(m):
            m = m.reshape(nl, gpb, SSM_GROUP, 1, SSM_STATE)
            return jnp.where(eye, m, 0.0).reshape(nl, gpb * SSM_GROUP, SCAN_LANES)
        return jnp.concatenate([one(m_r), one(m_i)], axis=-1)

    acat = _state_cols(abar_r.reshape(1, N_STATE), abar_i.reshape(1, N_STATE))
    bsup = sup(jnp.transpose(bbar_r, (0, 2, 1)), jnp.transpose(bbar_i, (0, 2, 1)))
    csup = sup(c_re, -c_im)
    return acat, bsup, csup


def _mem_fwd(mem, g_mem, w_kv, g_k, name):
    ml = mem.shape[0]

    def body(mem_ref, gm_ref, w_ref, gk_ref, memn_ref, kv_ref, kn_ref, vv_ref):
        memn = _rms(mem_ref[...], gm_ref[...])
        memn_ref[...] = memn.astype(BF16)
        kv = _dot(memn, w_ref[...])
        kv_ref[...] = kv
        for hh in range(XA_HEADS):
            sl = slice(hh * XA_HEAD_DIM, (hh + 1) * XA_HEAD_DIM)
            kn_ref[:, sl] = _rms(kv[:, sl], gk_ref[...]).astype(BF16)
        vv_ref[...] = kv[:, XA_WIDTH:].astype(BF16)

    return pl.pallas_call(
        body, name=name,
        out_shape=[jax.ShapeDtypeStruct((ml, D_MODEL), BF16), jax.ShapeDtypeStruct((ml, 2 * XA_WIDTH), F32),
                   jax.ShapeDtypeStruct((ml, XA_WIDTH), BF16), jax.ShapeDtypeStruct((ml, XA_WIDTH), BF16)],
        compiler_params=_params(),
    )(mem, g_mem, w_kv, g_k)


def _mem_bwd(mem, g_mem, memn, w_kv, kv, g_k, dkn, dvv, name):
    def body(mem_ref, gm_ref, memn_ref, w_ref, kv_ref, gk_ref, dkn_ref, dvv_ref, dw_ref, dgm_ref, dgk_ref):
        kv = kv_ref[...]
        dgk = jnp.zeros(dgk_ref.shape, F32)
        parts = []
        for hh in range(XA_HEADS):
            sl = slice(hh * XA_HEAD_DIM, (hh + 1) * XA_HEAD_DIM)
            _, vjp = jax.vjp(_rms, kv[:, sl], gk_ref[...])
            dk, dg = vjp(dkn_ref[:, sl])
            parts.append(dk)
            dgk = dgk + dg
        dgk_ref[...] = dgk
        dkv = jnp.concatenate(parts + [dvv_ref[...]], axis=1)
        dw_ref[...] = _dot(memn_ref[...], dkv, TN)
        dmemn = _dot(dkv, w_ref[...], NT)
        _, vjp = jax.vjp(_rms, mem_ref[...], gm_ref[...])
        dgm_ref[...] = vjp(dmemn)[1]

    return pl.pallas_call(
        body, name=name,
        out_shape=[jax.ShapeDtypeStruct((D_MODEL, 2 * XA_WIDTH), F32), jax.ShapeDtypeStruct(g_mem.shape, F32),
                   jax.ShapeDtypeStruct(g_k.shape, F32)],
        compiler_params=_params(),
    )(mem, g_mem, memn, w_kv, kv, g_k, dkn, dvv)


def _xa_head(qx_h, g_q, kn_h, vv_h):
    qn = _rms(qx_h, g_q)
    sc = _dot(qn, kn_h, NT) * (XA_HEAD_DIM ** -0.5)
    sc = sc - jnp.max(sc, axis=-1, keepdims=True)
    e = jnp.exp(sc)
    p = e / jnp.sum(e, axis=-1, keepdims=True)
    return qn, p


def _xa_fwd(qx, g_q, kn, vv, name):
    def fn(qt, gq, knt, vvt):
        outs = []
        for hh in range(XA_HEADS):
            sl = slice(hh * XA_HEAD_DIM, (hh + 1) * XA_HEAD_DIM)
            _, p = _xa_head(qt[:, sl], gq, knt[:, sl], vvt[:, sl])
            outs.append(_dot(p, vvt[:, sl]))
        return (jnp.concatenate(outs, axis=1),), ()

    return _rw(fn, [qx], [g_q, kn, vv], [(XA_WIDTH, BF16)], [], name)[0]


def _xa_bwd(qx, g_q, kn, vv, do, name):
    def fn(qt, dot_, gq, knt, vvt):
        dqs, dks, dvs = [], [], []
        dgq = jnp.zeros_like(gq)
        for hh in range(XA_HEADS):
            sl = slice(hh * XA_HEAD_DIM, (hh + 1) * XA_HEAD_DIM)
            qn, p = _xa_head(qt[:, sl], gq, knt[:, sl], vvt[:, sl])
            doh = dot_[:, sl]
            dp = _dot(doh, vvt[:, sl], NT)
            dvs.append(_dot(p, doh, TN))
            ds = p * (dp - jnp.sum(dp * p, axis=-1, keepdims=True)) * (XA_HEAD_DIM ** -0.5)
            dqn = _dot(ds, knt[:, sl])
            dks.append(_dot(ds, qn, TN))
            _, vjp = jax.vjp(_rms, qt[:, sl], gq)
            dq, dg = vjp(dqn)
            dqs.append(dq)
            dgq = dgq + dg
        return ((jnp.concatenate(dqs, axis=1),),
                (jnp.concatenate(dks, axis=1), jnp.concatenate(dvs, axis=1), dgq))

    return _rw(fn, [qx, do], [g_q, kn, vv], [(XA_WIDTH, BF16)], [kn.shape, vv.shape, g_q.shape], name)


BIG = [
    ("w_in", (D_MODEL, IN_WIDTH), 1), ("ssm_w_glu", (SSM_WIDTH, SSM_WIDTH), 0), ("w_out", (D_MODEL, D_MODEL), 0),
    ("xa_w_q", (D_MODEL, XA_WIDTH), 0), ("xa_w_kv", (D_MODEL, 2 * XA_WIDTH), 0), ("xa_w_o", (XA_WIDTH, D_MODEL), 1),
    ("w_up", (D_MODEL, D_FF), 1), ("w_down", (D_FF, D_MODEL), 0),
]
BIG_INDEX = {n: i for i, (n, _, _) in enumerate(BIG)}


def _shard_shape(shape, axis):
    return tuple(d // N_DEV if i == axis else d for i, d in enumerate(shape))


def _shard_of(ref, axis, d):
    n = ref.shape[axis] // N_DEV
    return ref.at[pl.ds(d * n, n), :] if axis == 0 else ref.at[:, pl.ds(d * n, n)]


def _gather_side(names, shards):
    idxs = [BIG_INDEX[n] for n in names]

    def make(ins, outs, send_sems, recv_sems):
        x, y, c = lax.axis_index("x"), lax.axis_index("y"), lax.axis_index("c")
        cps = []
        for j, i in enumerate(idxs):
            mine = _shard_of(outs[j], BIG[i][2], 4 * x + 2 * y + c)
            cps.append(pltpu.make_async_copy(ins[j], mine, send_sems.at[N_DEV * j]))
            for rel in range(1, N_DEV):
                to = tuple(1 - p if rel >> bit & 1 else p for p, bit in ((x, 2), (y, 1), (c, 0)))
                cps.append(pltpu.make_async_remote_copy(
                    src_ref=ins[j], dst_ref=mine, send_sem=send_sems.at[N_DEV * j + rel],
                    recv_sem=recv_sems.at[N_DEV * j + rel], device_id=to, device_id_type=MESH))
        return cps

    return _Side(shards, [jax.ShapeDtypeStruct(BIG[i][1], BF16) for i in idxs], N_DEV * len(idxs), make)


def _sibling_side(names, grads):
    idxs = [BIG_INDEX[n] for n in names]

    def make(ins, outs, send_sems, recv_sems):
        x, y, c = lax.axis_index("x"), lax.axis_index("y"), lax.axis_index("c")
        return [pltpu.make_async_remote_copy(
            src_ref=_shard_of(ins[j], BIG[i][2], 2 * k + (1 - c)), dst_ref=outs[j].at[k],
            send_sem=send_sems.at[4 * j + k], recv_sem=recv_sems.at[4 * j + k], device_id=(x, y, 1 - c),
            device_id_type=MESH) for j, i in enumerate(idxs) for k in range(4)]

    shapes = [jax.ShapeDtypeStruct((4,) + _shard_shape(BIG[i][1], BIG[i][2]), F32) for i in idxs]
    return _Side(grads, shapes, 4 * len(idxs), make)


def _chips_side(parts):
    def make(ins, outs, send_sems, recv_sems):
        x, y, c = lax.axis_index("x"), lax.axis_index("y"), lax.axis_index("c")
        chips = [(1 - x, y), (x, 1 - y), (1 - x, 1 - y)]
        return [pltpu.make_async_remote_copy(
            src_ref=ins[j].at[2 * cx + cy], dst_ref=outs[j].at[r], send_sem=send_sems.at[3 * j + r],
            recv_sem=recv_sems.at[3 * j + r], device_id=(cx, cy, c), device_id_type=MESH)
            for r, (cx, cy) in enumerate(chips) for j in range(len(parts))]

    return _Side(parts, [jax.ShapeDtypeStruct((3,) + p.shape[1:], p.dtype) for p in parts], 3 * len(parts), make)


def _reduce_add(grad, recv, axis, core, name):
    rs, cs = recv.shape[1:]
    rt = _row_tile(rs, 256)
    nt = rs // rt

    def body(c_ref, g_ref, r_ref, p_ref, pb_ref):
        sm = g_ref[...] + r_ref[0]
        p_ref[0] = sm
        pb_ref[0] = sm.astype(BF16)

    if axis == 0:
        g_spec = pl.BlockSpec((rt, cs), lambda k, t, c_ref: ((2 * k + c_ref[0]) * nt + t, 0))
    else:
        g_spec = pl.BlockSpec((rt, cs), lambda k, t, c_ref: (t, 2 * k + c_ref[0]))
    slab = pl.BlockSpec((1, rt, cs), lambda k, t, c_ref: (k, t, 0))
    return pl.pallas_call(
        body, name=name,
        grid_spec=pltpu.PrefetchScalarGridSpec(num_scalar_prefetch=1, grid=(4, nt), in_specs=[g_spec, slab],
                                               out_specs=[slab, slab]),
        out_shape=[jax.ShapeDtypeStruct(recv.shape, F32), jax.ShapeDtypeStruct(recv.shape, BF16)],
        compiler_params=_params(("parallel", "parallel")),
    )(core, grad, recv)


def _all_gather(block, name, side):
    m_per, n = block.shape
    ns_in, ns_out = len(side.ins), len(side.out_shapes)

    def body(*refs):
        x_ref, s_ins, out_ref = refs[0], refs[1:1 + ns_in], refs[1 + ns_in]
        s_outs = refs[2 + ns_in:2 + ns_in + ns_out]
        send_sems, recv_sems, local_sem, s_send, s_recv = refs[2 + ns_in + ns_out:]
        others = side.make(s_ins, s_outs, s_send, s_recv)
        for cp in others:
            cp.start()
        x, y, c = lax.axis_index("x"), lax.axis_index("y"), lax.axis_index("c")
        me, sibling = (x, y, c), (x, y, 1 - c)
        chips = [(1 - x, y), (x, 1 - y), (1 - x, 1 - y)]

        def rows(px, py, pc):
            return out_ref.at[pl.ds((4 * px + 2 * py + pc) * m_per, m_per), :]

        def copy(k, blk, to, src=None):
            return pltpu.make_async_remote_copy(
                src_ref=rows(*blk) if src is None else src, dst_ref=rows(*blk),
                send_sem=send_sems.at[k], recv_sem=recv_sems.at[k], device_id=to, device_id_type=MESH)

        mine = pltpu.make_async_copy(x_ref, rows(*me), local_sem)
        mine.start()
        first = [copy(0, me, sibling, src=x_ref)]
        first += [copy(1 + j, me, (*chip, c), src=x_ref) for j, chip in enumerate(chips)]
        for cp in first:
            cp.start()
        passed = [copy(4 + j, (*chip, c), sibling) for j, chip in enumerate(chips)]
        for j, chip in enumerate(chips):
            copy(1 + j, (*chip, c), me).wait_recv()
            passed[j].start()
        copy(0, sibling, me).wait_recv()
        for j, chip in enumerate(chips):
            copy(4 + j, (*chip, 1 - c), me).wait_recv()
        for cp in first + passed:
            cp.wait_send()
        mine.wait()
        for cp in others:
            cp.wait()

    res = pl.pallas_call(
        body, name=name, in_specs=[ANY] * (1 + ns_in), out_specs=[ANY] * (1 + ns_out),
        out_shape=[jax.ShapeDtypeStruct((N_DEV * m_per, n), block.dtype)] + side.out_shapes,
        scratch_shapes=[pltpu.SemaphoreType.DMA((7,)), pltpu.SemaphoreType.DMA((7,)), pltpu.SemaphoreType.DMA]
        + side.sems(),
    )(block, *side.ins)
    return res[0], list(res[1:])


def _adam_math(w, g, m, v):
    m = ADAM_B1 * m + (1.0 - ADAM_B1) * g
    v = ADAM_B2 * v + (1.0 - ADAM_B2) * (g * g)
    m_hat = m / (1.0 - ADAM_B1 ** ADAM_STEP)
    v_hat = v / (1.0 - ADAM_B2 ** ADAM_STEP)
    delta = -ADAM_LR * (m_hat / (jnp.sqrt(v_hat) + ADAM_EPS) + ADAM_WD * w)
    return delta, m, v


def _adam_sharded(own, recv, w, m, v, chip, name):
    rs, cs = w.shape
    rt = _row_tile(rs, 256)

    def body(chip_ref, p_ref, r_ref, w_ref, m_ref, v_ref, g_out, d_out, m_out, v_out):
        g = p_ref[0] + r_ref[0].astype(F32) + r_ref[1].astype(F32) + r_ref[2].astype(F32)
        d, mn, vn = _adam_math(w_ref[...], g, m_ref[...], v_ref[...])
        g_out[...] = g
        d_out[...] = d
        m_out[...] = mn
        v_out[...] = vn

    tile = pl.BlockSpec((rt, cs), lambda t, chip_ref: (t, 0))
    return pl.pallas_call(
        body, name=name,
        grid_spec=pltpu.PrefetchScalarGridSpec(
            num_scalar_prefetch=1, grid=(rs // rt,),
            in_specs=[pl.BlockSpec((1, rt, cs), lambda t, chip_ref: (chip_ref[0], t, 0)),
                      pl.BlockSpec((3, rt, cs), lambda t, chip_ref: (0, t, 0)), tile, tile, tile],
            out_specs=[tile] * 4),
        out_shape=[jax.ShapeDtypeStruct((rs, cs), F32)] * 4,
        compiler_params=_params(("parallel",)),
    )(chip, own, recv, w, m, v)


SMALL = ["g_mix", "ssm_a_re", "ssm_a_im", "ssm_log_dt", "ssm_b_re", "ssm_b_im", "ssm_c_re", "ssm_c_im", "ssm_d",
         "sb_g_q", "sb_g_k", "g_out_ssm", "g_out_sb", "g_xa", "g_mem", "xa_g_q", "xa_g_k", "g_mlp"]
PACK_TILE = SUBLANES * LANES


def _natural_2d(n):
    return (n // LANES, LANES) if n % LANES == 0 else (1, n)


def _pack_small(arrs):
    parts = []
    for a in arrs:
        flat = a.reshape(-1)
        parts.append(jnp.pad(flat, (0, (-flat.shape[0]) % PACK_TILE)))
    return jnp.concatenate(parts).reshape(-1, LANES)


def _adam_replicated(gathered, sizes, ws, ms, vs, name):
    n_w = len(ws)
    r_dev = gathered.shape[0] // N_DEV
    offs, off = [], 0
    for n in sizes:
        offs.append(off)
        off += (n + PACK_TILE - 1) // PACK_TILE * SUBLANES
    assert off == r_dev

    def body(*refs):
        g_ref = refs[0]
        w_refs, m_refs, v_refs = refs[1:1 + n_w], refs[1 + n_w:1 + 2 * n_w], refs[1 + 2 * n_w:1 + 3 * n_w]
        outs = refs[1 + 3 * n_w:]

        def total(i, shape):
            r, cdim = shape
            acc = g_ref[pl.ds(offs[i], r), :cdim]
            for d in range(1, N_DEV):
                acc = acc + g_ref[pl.ds(d * r_dev + offs[i], r), :cdim]
            return acc

        for i in range(n_w):
            g = total(i, w_refs[i].shape)
            d, mn, vn = _adam_math(w_refs[i][...], g, m_refs[i][...], v_refs[i][...])
            for o, val in zip(outs[4 * i:4 * i + 4], (g, d, mn, vn)):
                o[...] = val
        outs[4 * n_w][...] = total(n_w, (SUBLANES, LANES))

    shapes = [w.shape for w in ws]
    return pl.pallas_call(
        body, name=name,
        out_shape=[jax.ShapeDtypeStruct(shp, F32) for shp in shapes for _ in range(4)]
        + [jax.ShapeDtypeStruct((SUBLANES, LANES), F32)],
        compiler_params=_params(),
    )(gathered, *ws, *ms, *vs)


def _step(x, mem, target, shards, sm, core):
    g, w, sums, reduced = {}, {}, {}, {}

    def gather(names):
        return _gather_side(names, [shards[n] for n in names])

    def to_sibling(names):
        return _sibling_side(names, [g[n] for n in names])

    def add_sibling(names, received):
        for n, r in zip(names, received):
            sums[n] = _reduce_add(g[n], r, BIG[BIG_INDEX[n]][2], core, "reduce_add_" + n)

    def to_chips(names):
        return _chips_side([sums[n][1] for n in names])

    def keep(names, received):
        for n, r in zip(names, received):
            reduced[n] = (sums[n][0], r)

    row = lambda a: a.reshape(1, -1)
    g_mix, g_xa, g_mlp, g_mem = row(sm["g_mix"]), row(sm["g_xa"]), row(sm["g_mlp"]), row(sm["g_mem"])
    g_os, g_ob = row(sm["g_out_ssm"]), row(sm["g_out_sb"])
    sb_gq, sb_gk = jnp.tile(row(sm["sb_g_q"]), (1, SB_HEADS)), jnp.tile(row(sm["sb_g_k"]), (1, SB_HEADS))
    xa_gq, xa_gk = row(sm["xa_g_q"]), row(sm["xa_g_k"])
    d_skip = row(sm["ssm_d"])

    h1, (w["w_in"],) = _norm_fwd(x, g_mix, "norm_mix", side=gather(["w_in"]))
    proj = _mm(h1, w["w_in"], "nn", "in_proj")
    u = proj
    q_raw, k_raw = (proj, SB_WIDTH, 1), (proj, SB_WIDTH, 2)
    v_col = (SSM_WIDTH + 2 * SB_WIDTH) // LANES
    sb_scale = SB_HEAD_DIM ** -0.5
    qk_norm = lambda scale: (lambda xt, gt: ((_rms_groups(xt, gt, scale),), ()))
    qs = _rw(qk_norm(sb_scale), [q_raw], [sb_gq], [(SB_WIDTH, BF16)], [], "sb_qnorm")[0]
    ks = _rw(qk_norm(1.0), [k_raw], [sb_gk], [(SB_WIDTH, BF16)], [], "sb_knorm")[0]
    early = ["ssm_w_glu", "w_out", "xa_w_q", "xa_w_kv", "xa_w_o", "w_up"]
    y_sb, got = _sb_fwd(qs, ks, proj, "sb_fwd", v_col=v_col, side=gather(early))
    w.update(zip(early, got))

    ssm_args = (sm["ssm_a_re"], sm["ssm_a_im"], sm["ssm_log_dt"], sm["ssm_b_re"], sm["ssm_b_im"],
                sm["ssm_c_re"], sm["ssm_c_im"])
    (acat, bsup, csup), mats_vjp = jax.vjp(_ssm_mats, *ssm_args)
    (states, y0, y1), (w["w_down"],) = _ssm_fwd(u, acat, bsup, csup, d_skip, "ssm_fwd", side=gather(["w_down"]))
    z_glu, y_ssm = _mm(y1, w["ssm_w_glu"], "nn", "ssm_glu", epi=lambda r, yt: (r, yt * jax.nn.sigmoid(r)),
                       extras=(y1,), out_dtypes=(F32, F32))

    def cat_norm(a, b, ga, gb):
        return jnp.concatenate([_rms(a, ga), _rms(b, gb)], axis=1)

    ycat = _rw(lambda a, b, ga, gb: ((cat_norm(a, b, ga, gb),), ()), [y_ssm, y_sb], [g_os, g_ob],
               [(D_MODEL, BF16)], [], "norm_out")[0]
    def residual_norm_epi(r, xt, gt):
        xn = r + xt
        return xn, _rms(xn, gt)

    x1, h2 = _mm(ycat, w["w_out"], "nn", "out_proj", epi=residual_norm_epi, extras=(x,), fulls=(g_xa,),
                 out_dtypes=(F32, BF16))
    qx = _mm(h2, w["xa_w_q"], "nn", "xa_q")
    memn, kv, kn_x, vv_x = _mem_fwd(mem, g_mem, w["xa_w_kv"], xa_gk, "xa_mem")
    o_xa = _xa_fwd(qx, xa_gq, kn_x, vv_x, "xa_fwd")
    x2, h3 = _mm(o_xa, w["xa_w_o"], "nn", "xa_o", epi=residual_norm_epi, extras=(x1,), fulls=(g_mlp,),
                 out_dtypes=(F32, BF16))

    def up_epi(r):
        rl = jnp.maximum(r, 0.0)
        return (rl * rl,)

    r_up = _mm(h3, w["w_up"], "nn", "mlp_up", epi=up_epi, out_dtypes=(BF16,))

    def loss_epi(r, xt, tt):
        d = r + xt - tt
        return (d * (1.0 / D_MODEL),) * 2, (jnp.sum(d * d, axis=0, keepdims=True),)

    dx3, dx3_b, sq = _mm(r_up, w["w_down"], "nn", "mlp_down", epi=loss_epi, extras=(x2, target),
                         out_dtypes=(F32, BF16), sums=[(1, D_MODEL)])
    loss = jnp.sum(sq) * (0.5 / D_MODEL)

    def norm_bwd_epi(r, xt, drt, gt):
        _, vjp = jax.vjp(_rms, xt, gt)
        dx_, dg_ = vjp(r)
        return (dx_ + drt,) * 2, (dg_,)

    g["w_down"] = _mm(r_up, dx3_b, "tn", "d_w_down", tk=2048)
    da = _mm(dx3_b, w["w_down"], "nt", "d_r", epi=lambda r, rt: (r * 2.0 * jnp.sqrt(rt.astype(F32)),), extras=(r_up,),
             out_dtypes=(BF16,))
    g["w_up"] = _mm(h3, da, "tn", "d_w_up", tk=2048)
    mlp = ["w_down", "w_up"]
    (dx2, dx2_b, g["g_mlp"]), got = _mm(da, w["w_up"], "nt", "d_h3", epi=norm_bwd_epi, extras=(x2, dx3),
                                        fulls=(g_mlp,), out_dtypes=(F32, BF16), sums=[g_mlp.shape],
                                        side=to_sibling(mlp))
    add_sibling(mlp, got)
    g["xa_w_o"] = _mm(o_xa, dx2_b, "tn", "d_xa_w_o", tk=2048)
    do_xa = _mm(dx2_b, w["xa_w_o"], "nt", "d_o_xa")
    dqx, dkn_x, dvv_x, g["xa_g_q"] = _xa_bwd(qx, xa_gq, kn_x, vv_x, do_xa, "xa_bwd")
    g["xa_w_kv"], g["g_mem"], g["xa_g_k"] = _mem_bwd(mem, g_mem, memn, w["xa_w_kv"], kv, xa_gk, dkn_x, dvv_x,
                                                     "xa_mem_bwd")
    g["xa_w_q"] = _mm(h2, dqx, "tn", "d_xa_w_q", tk=2048)
    dx1, dx1_b, g["g_xa"] = _mm(dqx, w["xa_w_q"], "nt", "d_h2", epi=norm_bwd_epi, extras=(x1, dx2), fulls=(g_xa,),
                                out_dtypes=(F32, BF16), sums=[g_xa.shape])
    g["w_out"] = _mm(ycat, dx1_b, "tn", "d_w_out", tk=2048)
    dycat = _mm(dx1_b, w["w_out"], "nt", "d_ycat")

    def cat_bwd(a, b, dy, ga, gb):
        _, vjp = jax.vjp(cat_norm, a, b, ga, gb)
        da_, db_, dga, dgb = vjp(dy)
        return (da_, db_), (dga, dgb)

    dy_ssm, dy_sb, g["g_out_ssm"], g["g_out_sb"] = _rw(
        cat_bwd, [y_ssm, y_sb, dycat], [g_os, g_ob], [(SSM_WIDTH, F32), (SB_WIDTH, F32)], [g_os.shape, g_ob.shape],
        "d_norm_out")

    def glu_bwd(dy, yt, zt):
        sg = jax.nn.sigmoid(zt)
        return (dy * sg, dy * yt * sg * (1.0 - sg)), ()

    dy1_a, dz = _rw(glu_bwd, [dy_ssm, y1, z_glu], [], [(SSM_WIDTH, F32), (SSM_WIDTH, BF16)], [], "d_glu")
    g["ssm_w_glu"] = _mm(y1, dz, "tn", "d_w_glu", tk=2048)

    def gelu_bwd_epi(r, da_, y0t):
        _, vjp = jax.vjp(jax.nn.gelu, y0t)
        return (vjp(r + da_)[0],)

    mid = ["w_out", "xa_w_q", "xa_w_kv", "xa_w_o", "ssm_w_glu"]
    dy0, got = _mm(dz, w["ssm_w_glu"], "nt", "d_y1", epi=gelu_bwd_epi, extras=(dy1_a, y0), side=to_sibling(mid))
    add_sibling(mid, got)
    (du, da8, d_bsup, d_csup, g["ssm_d"]), got = _ssm_bwd(dy0, states, u, acat, bsup, csup, d_skip, "ssm_bwd",
                                                          side=to_chips(mlp))
    keep(mlp, got)
    d_acat = jnp.sum(da8, axis=0, keepdims=True)
    for nm, val in zip(("ssm_a_re", "ssm_a_im", "ssm_log_dt", "ssm_b_re", "ssm_b_im", "ssm_c_re", "ssm_c_im"),
                       mats_vjp((d_acat, d_bsup, d_csup))):
        g[nm] = val

    (dqs, dkt, dvt), got = _sb_bwd(qs, ks, proj, y_sb, dy_sb, "sb_bwd", v_col=v_col, side=to_chips(mid))
    keep(mid, got)

    def qk_norm_bwd(scale):
        def fn(xt, dt, gt):
            _, vjp = jax.vjp(lambda a, b_: _rms_groups(a, b_, scale), xt, gt)
            dx_, dg_ = vjp(dt)
            return (dx_,), (dg_,)
        return fn

    dq_raw, dgq = _rw(qk_norm_bwd(sb_scale), [q_raw, dqs], [sb_gq], [(SB_WIDTH, BF16)], [sb_gq.shape], "d_sb_qnorm")
    dk_raw, dgk = _rw(qk_norm_bwd(1.0), [k_raw, _from_key_blocks(dkt)], [sb_gk], [(SB_WIDTH, BF16)], [sb_gk.shape],
                      "d_sb_knorm")
    g["sb_g_q"] = jnp.sum(dgq.reshape(SB_HEADS, SB_HEAD_DIM), axis=0)
    g["sb_g_k"] = jnp.sum(dgk.reshape(SB_HEADS, SB_HEAD_DIM), axis=0)
    dproj = jnp.concatenate([du, dq_raw, dk_raw, _from_key_blocks(dvt).astype(BF16)], axis=1)
    g["w_in"] = _mm(h1, dproj, "tn", "d_w_in", tk=2048)
    dh1, got = _mm(dproj, w["w_in"], "nt", "d_h1", side=to_sibling(["w_in"]))
    add_sibling(["w_in"], got)
    dx, g["g_mix"] = _norm_bwd(x, g_mix, dh1, dx1, "d_norm_mix")

    packed = _pack_small([g[n] for n in SMALL] + [loss.reshape(1)])
    everyone, got = _all_gather(packed, "gather_small", to_chips(["w_in"]))
    keep(["w_in"], got)
    return dx, everyone, reduced


def kernel(x, mem, g_mix, w_in, ssm_a_re, ssm_a_im, ssm_log_dt, ssm_b_re, ssm_b_im, ssm_c_re, ssm_c_im, ssm_d, ssm_w_glu, sb_g_q, sb_g_k, g_out_ssm, g_out_sb, w_out, g_xa, g_mem, xa_w_q, xa_w_kv, xa_g_q, xa_g_k, xa_w_o, g_mlp, w_up, w_down, loss_target, m_g_mix, m_w_in, m_ssm_a_re, m_ssm_a_im, m_ssm_log_dt, m_ssm_b_re, m_ssm_b_im, m_ssm_c_re, m_ssm_c_im, m_ssm_d, m_ssm_w_glu, m_sb_g_q, m_sb_g_k, m_g_out_ssm, m_g_out_sb, m_w_out, m_g_xa, m_g_mem, m_xa_w_q, m_xa_w_kv, m_xa_g_q, m_xa_g_k, m_xa_w_o, m_g_mlp, m_w_up, m_w_down, v_g_mix, v_w_in, v_ssm_a_re, v_ssm_a_im, v_ssm_log_dt, v_ssm_b_re, v_ssm_b_im, v_ssm_c_re, v_ssm_c_im, v_ssm_d, v_ssm_w_glu, v_sb_g_q, v_sb_g_k, v_g_out_ssm, v_g_out_sb, v_w_out, v_g_xa, v_g_mem, v_xa_w_q, v_xa_w_kv, v_xa_g_q, v_xa_g_k, v_xa_w_o, v_g_mlp, v_w_up, v_w_down):
    given = dict(locals())
    order = ["g_mix", "w_in", "ssm_a_re", "ssm_a_im", "ssm_log_dt", "ssm_b_re", "ssm_b_im", "ssm_c_re", "ssm_c_im",
             "ssm_d", "ssm_w_glu", "sb_g_q", "sb_g_k", "g_out_ssm", "g_out_sb", "w_out", "g_xa", "g_mem", "xa_w_q",
             "xa_w_kv", "xa_g_q", "xa_g_k", "xa_w_o", "g_mlp", "w_up", "w_down"]
    assert sorted([n for n, _, _ in BIG] + SMALL) == sorted(order)
    core = lax.axis_index("c").astype(jnp.int32).reshape(1)
    chip = (2 * lax.axis_index("x") + lax.axis_index("y")).astype(jnp.int32).reshape(1)

    shards = {n: given[n][0].astype(BF16) for n, _, _ in BIG}
    sm = {n: given[n][0] for n in SMALL}
    dx, everyone, reduced = _step(x[0], mem[0], loss_target[0], shards, sm, core)

    res = {}
    for n, _, _ in BIG:
        own, recv = reduced[n]
        outs = _adam_sharded(own, recv, given[n][0], given["m_" + n][0], given["v_" + n][0], chip, "adam_" + n)
        for kind, val in zip(("grad", "delta", "new_m", "new_v"), outs):
            res[kind + "_" + n] = val[None]

    sizes = [math.prod(sm[n].shape) for n in SMALL] + [1]
    nat = lambda a: a.reshape(_natural_2d(math.prod(a.shape)))
    outs = _adam_replicated(everyone, sizes, [nat(sm[n]) for n in SMALL], [nat(given["m_" + n][0]) for n in SMALL],
                            [nat(given["v_" + n][0]) for n in SMALL], "adam_replicated")
    for i, n in enumerate(SMALL):
        for kind, val in zip(("grad", "delta", "new_m", "new_v"), outs[4 * i:4 * i + 4]):
            res[kind + "_" + n] = val.reshape(given[n].shape)
    loss_out = outs[-1][0, 0]
    return (loss_out, dx[None], *[res["grad_" + n] for n in order], *[res["delta_" + n] for n in order],
            *[res["new_m_" + n] for n in order], *[res["new_v_" + n] for n in order])
```

```python
import functools
import math

import jax
import jax.numpy as jnp
from jax import lax
from jax.experimental import pallas as pl
from jax.experimental.pallas import tpu as pltpu

F32 = jnp.float32
BF16 = jnp.bfloat16
MESH = pl.DeviceIdType.MESH

N_DEV = 8
D_MODEL = 1024
SSM_WIDTH = 512
SSM_GROUP = 16
SSM_GROUPS = 32
SSM_STATE = 64
N_STATE = SSM_GROUPS * SSM_STATE
SB_HEADS = 8
SB_HEAD_DIM = 64
SB_WIDTH = 512
IN_WIDTH = 2048
XA_HEADS = 4
XA_HEAD_DIM = 128
XA_WIDTH = 512
D_FF = 4096
NORM_EPS = 1e-6
ADAM_LR = 0.001
ADAM_B1 = 0.9
ADAM_B2 = 0.999
ADAM_EPS = 1e-08
ADAM_WD = 0.01
ADAM_STEP = 10

LANES = 128
SUBLANES = 8
VMEM_LIMIT = 56 * 1024 * 1024
SCAN_LANES = 512
SB_BLOCK = 256
SB_UNDERFLOW = -110.0

NN = (((1,), (0,)), ((), ()))
NT = (((1,), (1,)), ((), ()))
TN = (((0,), (0,)), ((), ()))


def _params(sem=None):
    return pltpu.CompilerParams(dimension_semantics=sem, vmem_limit_bytes=VMEM_LIMIT)


def _dot(a, b, dims=NN):
    return lax.dot_general(a.astype(BF16), b.astype(BF16), dims, preferred_element_type=F32)


def _rms(x, g):
    return x * lax.rsqrt(jnp.mean(x * x, axis=-1, keepdims=True) + NORM_EPS) * g


ANY = pl.BlockSpec(memory_space=pl.ANY)


class _Side:
    def __init__(self, ins, out_shapes, n_sem, make):
        self.ins, self.out_shapes, self.n_sem, self.make = list(ins), list(out_shapes), n_sem, make

    def sems(self):
        return [pltpu.SemaphoreType.DMA((self.n_sem,)), pltpu.SemaphoreType.DMA((self.n_sem,))]


def _hosted(body, side, n_in, n_out, grid):
    if side is None:
        return body
    ns_in, ns_out = len(side.ins), len(side.out_shapes)

    def wrapped(*refs):
        ins, refs = refs[:n_in], refs[n_in:]
        s_ins, refs = refs[:ns_in], refs[ns_in:]
        outs, refs = refs[:n_out], refs[n_out:]
        s_outs, refs = refs[:ns_out], refs[ns_out:]
        scratch, sems = refs[:-2], refs[-2:]
        ids = [pl.program_id(d) for d in range(len(grid))]
        first = functools.reduce(jnp.logical_and, [i == 0 for i in ids])
        last = functools.reduce(jnp.logical_and, [i == n - 1 for i, n in zip(ids, grid)])

        @pl.when(first)
        def _():
            for cp in side.make(s_ins, s_outs, *sems):
                cp.start()

        body(*ins, *outs, *scratch)

        @pl.when(last)
        def _():
            for cp in side.make(s_ins, s_outs, *sems):
                cp.wait()

    return wrapped


def _side_args(side):
    if side is None:
        return [], [], [], [], []
    return ([ANY] * len(side.ins), [ANY] * len(side.out_shapes), side.out_shapes, side.sems(), side.ins)


def _split_side(res, n_out, side):
    res = list(res)
    main = res[0] if n_out == 1 else res[:n_out]
    return main if side is None else (main, res[n_out:])


def _mm(a, b, mode, name, *, epi=None, extras=(), fulls=(), out_dtypes=(F32,), sums=(), tm=1024, tn=1024, tk=1024,
        side=None):
    if mode == "nn":
        (m, k), (k2, n) = a.shape, b.shape
    elif mode == "nt":
        (m, k), (n, k2) = a.shape, b.shape
    else:
        (k, m), (k2, n) = a.shape, b.shape
    assert k == k2, (name, a.shape, b.shape)
    tm, tn, tk = min(tm, m), min(tn, n), min(tk, k)
    assert m % tm == 0 and n % tn == 0 and k % tk == 0, (name, m, n, k)
    nk = k // tk
    dims = {"nn": NN, "nt": NT, "tn": TN}[mode]
    if mode == "tn":
        a_spec = pl.BlockSpec((tk, tm), lambda i, j, kk: (kk, i))
    else:
        a_spec = pl.BlockSpec((tm, tk), lambda i, j, kk: (i, kk))
    if mode == "nt":
        b_spec = pl.BlockSpec((tn, tk), lambda i, j, kk: (j, kk))
    else:
        b_spec = pl.BlockSpec((tk, tn), lambda i, j, kk: (kk, j))
    mn_spec = pl.BlockSpec((tm, tn), lambda i, j, kk: (i, j))
    n_ex, n_full, n_out, n_sum = len(extras), len(fulls), len(out_dtypes), len(sums)
    n_in = 2 + n_ex + n_full

    def body(*refs):
        a_ref, b_ref = refs[:2]
        ex = refs[2:n_in]
        outs = refs[n_in:n_in + n_out]
        sum_refs = refs[n_in + n_out:n_in + n_out + n_sum]
        kk = pl.program_id(2)
        first_tile = jnp.logical_and(pl.program_id(0) == 0, pl.program_id(1) == 0)

        def finish(r):
            vals = epi(r, *[e[...] for e in ex]) if epi is not None else (r,)
            if n_sum:
                vals, parts = vals

                @pl.when(first_tile)
                def _():
                    for sr in sum_refs:
                        sr[...] = jnp.zeros_like(sr)

                for sr, p in zip(sum_refs, parts):
                    sr[...] += p
            for o, v in zip(outs, vals):
                o[...] = v.astype(o.dtype)

        if nk == 1:
            finish(_dot(a_ref[...], b_ref[...], dims))
        else:
            acc = refs[n_in + n_out + n_sum]

            @pl.when(kk == 0)
            def _():
                acc[...] = jnp.zeros_like(acc)

            acc[...] += _dot(a_ref[...], b_ref[...], dims)

            @pl.when(kk == nk - 1)
            def _():
                finish(acc[...])

    grid = (m // tm, n // tn, nk)
    whole = lambda shape: pl.BlockSpec(shape, lambda i, j, kk: (0,) * len(shape))
    s_in, s_out, s_shape, s_scratch, s_ops = _side_args(side)
    seq = bool(side) or n_sum > 0
    res = pl.pallas_call(
        _hosted(body, side, n_in, n_out + n_sum, grid), name=name, grid=grid,
        in_specs=[a_spec, b_spec] + [mn_spec] * n_ex + [whole(f.shape) for f in fulls] + s_in,
        out_specs=[mn_spec] * n_out + [whole(shape) for shape in sums] + s_out,
        out_shape=[jax.ShapeDtypeStruct((m, n), dt) for dt in out_dtypes]
        + [jax.ShapeDtypeStruct(shape, F32) for shape in sums] + s_shape,
        scratch_shapes=([pltpu.VMEM((tm, tn), F32)] if nk > 1 else []) + s_scratch,
        compiler_params=_params(("arbitrary",) * 3 if seq else ("parallel", "parallel", "arbitrary")),
    )(a, b, *extras, *fulls, *s_ops)
    return _split_side(res, n_out + n_sum, side)


def _row_tile(s, target):
    if s <= target:
        return s
    return max(t for t in range(16, target + 1, 16) if s % t == 0)


def _rw(fn, rows, fulls, row_out, acc_out, name, tm=512, side=None):
    cols = [r[1:] if isinstance(r, tuple) else (r.shape[1], 0) for r in rows]
    rows = [r[0] if isinstance(r, tuple) else r for r in rows]
    s = rows[0].shape[0]
    tm = _row_tile(s, tm)
    nr, nf, nro, nao = len(rows), len(fulls), len(row_out), len(acc_out)

    def body(*refs):
        r = refs[:nr]
        f = refs[nr:nr + nf]
        ro = refs[nr + nf:nr + nf + nro]
        ao = refs[nr + nf + nro:]
        outs, accs = fn(*[x[...] for x in r], *[x[...] for x in f])
        for o, v in zip(ro, outs):
            o[...] = v.astype(o.dtype)
        if nao:
            @pl.when(pl.program_id(0) == 0)
            def _():
                for a in ao:
                    a[...] = jnp.zeros_like(a)

            for a, v in zip(ao, accs):
                a[...] += v

    full_spec = lambda shape: pl.BlockSpec(shape, lambda i: (0,) * len(shape))
    s_in, s_out, s_shape, s_scratch, s_ops = _side_args(side)
    res = pl.pallas_call(
        _hosted(body, side, nr + nf, nro + nao, (s // tm,)), name=name, grid=(s // tm,),
        in_specs=[pl.BlockSpec((tm, wd), functools.partial(lambda i, cb: (i, cb), cb=cb)) for wd, cb in cols]
        + [full_spec(x.shape) for x in fulls] + s_in,
        out_specs=[pl.BlockSpec((tm, d), lambda i: (i, 0)) for d, _ in row_out]
        + [full_spec(shape) for shape in acc_out] + s_out,
        out_shape=[jax.ShapeDtypeStruct((s, d), dt) for d, dt in row_out]
        + [jax.ShapeDtypeStruct(shape, F32) for shape in acc_out] + s_shape,
        scratch_shapes=s_scratch,
        compiler_params=_params(("arbitrary",)),
    )(*rows, *fulls, *s_ops)
    res = list(res)
    return res if side is None else (res[:nro + nao], res[nro + nao:])


def _norm_fwd(x, g, name, side=None):
    res = _rw(lambda xt, gt: ((_rms(xt, gt),), ()), [x], [g], [(x.shape[1], BF16)], [], name, side=side)
    return res[0] if side is None else (res[0][0], res[1])


def _norm_bwd(x, g, dh, dres, name, side=None):
    def fn(xt, dht, drt, gt):
        _, vjp = jax.vjp(_rms, xt, gt)
        dx, dg = vjp(dht)
        return (dx + drt,), (dg,)

    return _rw(fn, [x, dh, dres], [g], [(x.shape[1], F32)], [g.shape], name, side=side)


def _rms_groups(x, g, scale):
    lo = lax.broadcasted_iota(jnp.int32, (1, LANES), 1) < SB_HEAD_DIM
    x2 = x * x
    outs = []
    for cb in range(x.shape[1] // LANES):
        sl = slice(cb * LANES, (cb + 1) * LANES)
        s_lo = jnp.sum(jnp.where(lo, x2[:, sl], 0.0), axis=-1, keepdims=True)
        s_hi = jnp.sum(jnp.where(lo, 0.0, x2[:, sl]), axis=-1, keepdims=True)
        r = jnp.where(lo, lax.rsqrt(s_lo * (1.0 / SB_HEAD_DIM) + NORM_EPS),
                      lax.rsqrt(s_hi * (1.0 / SB_HEAD_DIM) + NORM_EPS))
        outs.append(x[:, sl] * r)
    return jnp.concatenate(outs, axis=1) * g * scale


def _log_sigmoid(z):
    return jnp.minimum(z, 0.0) - jnp.log(1.0 + jnp.exp(-jnp.abs(z)))


def _split_dot(x, u2):
    hi = x.astype(BF16)
    lo = (x - hi.astype(F32)).astype(BF16)
    return jnp.dot(jnp.concatenate([hi, lo], axis=1), u2, preferred_element_type=F32)


def _sb_consts(b):
    row = lax.broadcasted_iota(jnp.int32, (b, b), 0)
    col = lax.broadcasted_iota(jnp.int32, (b, b), 1)
    tri = col < row
    u_after = (row > col).astype(BF16)
    u_from = (row >= col).astype(BF16)
    stack = lambda u: jnp.concatenate([u, u], axis=0)
    lane_lo = lax.broadcasted_iota(jnp.int32, (b, LANES), 1) < SB_HEAD_DIM
    return tri, stack(u_after), stack(u_from), lane_lo


def _sb_scores(qh, kb, a_run, keep, u2_after):
    z = lax.dot_general(qh, kb, NT, preferred_element_type=F32)
    lb = _log_sigmoid(z)
    l = lb - z
    if keep is not None:
        l = jnp.where(keep, l, 0.0)
    w = jnp.exp(lb + (a_run + _split_dot(l, u2_after)))
    if keep is not None:
        w = jnp.where(keep, w, 0.0)
    return lb, l, w


def _sb_walk(qi, carry, step):
    def cond(state):
        n, c = state
        return jnp.logical_and(n <= qi, jnp.max(jnp.maximum(c[0], c[1])) > SB_UNDERFLOW)

    def body(state):
        n, c = state
        return n + 1, step(n, c)

    return lax.while_loop(cond, body, (jnp.int32(2), carry))[1]


def _two_heads(x, lane_lo):
    zero = jnp.zeros_like(x)
    return jnp.where(lane_lo, x, zero), jnp.where(lane_lo, zero, x)


def _sb_fwd(qs, ks, v, name, v_col=0, side=None):
    s, width = qs.shape
    b = min(SB_BLOCK, s)

    def body(q_ref, k_ref, v_ref, o_ref):
        qi = pl.program_id(1)
        tri, u2_after, _, lane_lo = _sb_consts(b)
        q_a, q_b = _two_heads(q_ref[...], lane_lo)

        def step(n, carry, keep):
            a_a, a_b, acc = carry
            off = pl.multiple_of(jnp.maximum(qi - n, 0) * b, b)
            kb = k_ref[pl.ds(off, b), :]
            v_a, v_b = _two_heads(v_ref[pl.ds(off, b), :].astype(BF16), lane_lo)
            _, l_a, w_a = _sb_scores(q_a, kb, a_a, keep, u2_after)
            _, l_b, w_b = _sb_scores(q_b, kb, a_b, keep, u2_after)
            acc = acc + jnp.dot(jnp.concatenate([w_a.astype(BF16), w_b.astype(BF16)], axis=1),
                                jnp.concatenate([v_a, v_b], axis=0), preferred_element_type=F32)
            return (a_a + jnp.sum(l_a, axis=1, keepdims=True), a_b + jnp.sum(l_b, axis=1, keepdims=True), acc)

        zero = jnp.zeros((b, 1), F32)
        carry = step(0, (zero, zero, jnp.zeros((b, LANES), F32)), tri)
        carry = step(1, carry, jnp.broadcast_to(qi > 0, tri.shape))
        carry = _sb_walk(qi, carry, lambda n, c: step(n, c, None))
        o_ref[...] = carry[2]

    blk = pl.BlockSpec((b, LANES), lambda hp, i: (i, hp))
    full = pl.BlockSpec((s, LANES), lambda hp, i: (0, hp))
    full_v = pl.BlockSpec((s, LANES), lambda hp, i: (0, hp + v_col))
    grid = (width // LANES, s // b)
    s_in, s_out, s_shape, s_scratch, s_ops = _side_args(side)
    res = pl.pallas_call(
        _hosted(body, side, 3, 1, grid), name=name, grid=grid,
        in_specs=[blk, full, full_v] + s_in, out_specs=[blk] + s_out,
        out_shape=[jax.ShapeDtypeStruct((s, width), F32)] + s_shape, scratch_shapes=s_scratch,
        compiler_params=_params(("arbitrary", "arbitrary")),
    )(qs, ks, v, *s_ops)
    return _split_side(res, 1, side)


def _sb_bwd(qs, ks, v, out, dout, name, v_col=0, side=None):
    s, width = qs.shape
    b = min(SB_BLOCK, s)
    nkb = s // b

    def body(q_ref, k_ref, v_ref, o_ref, do_ref, dq_ref, dk_ref, dv_ref):
        qi = pl.program_id(1)

        @pl.when(qi == 0)
        def _():
            dk_ref[...] = jnp.zeros_like(dk_ref)
            dv_ref[...] = jnp.zeros_like(dv_ref)

        tri, u2_after, u2_from, lane_lo = _sb_consts(b)
        q_a, q_b = _two_heads(q_ref[...], lane_lo)
        dob = do_ref[...].astype(BF16)
        do_a, do_b = _two_heads(dob, lane_lo)
        prod = dob.astype(F32) * o_ref[...]
        d_a = jnp.sum(jnp.where(lane_lo, prod, 0.0), axis=1, keepdims=True)
        d_b = jnp.sum(jnp.where(lane_lo, 0.0, prod), axis=1, keepdims=True)
        q_rows = jnp.concatenate([q_a, q_b], axis=0)
        do_rows = jnp.concatenate([do_a, do_b], axis=0)

        def head(qh, doh, kb, vb, a_run, d_rem, keep):
            lb, l, w = _sb_scores(qh, kb, a_run, keep, u2_after)
            wb = w.astype(BF16)
            g = lax.dot_general(doh, vb, NT, preferred_element_type=F32) * wb.astype(F32)
            g_before = d_rem - _split_dot(g, u2_from)
            dz = g - (g + g_before) * jnp.exp(lb)
            if keep is not None:
                dz = jnp.where(keep, dz, 0.0)
            return (dz.astype(BF16), wb, a_run + jnp.sum(l, axis=1, keepdims=True),
                    d_rem - jnp.sum(g, axis=1, keepdims=True))

        def step(n, carry, keep):
            a_a, a_b, r_a, r_b, dq = carry
            jb = jnp.maximum(qi - n, 0)
            off = pl.multiple_of(jb * b, b)
            kb = k_ref[pl.ds(off, b), :]
            vb = v_ref[pl.ds(off, b), :].astype(BF16)
            k_a, k_b = _two_heads(kb, lane_lo)
            dz_a, w_a, a_a, r_a = head(q_a, do_a, kb, vb, a_a, r_a, keep)
            dz_b, w_b, a_b, r_b = head(q_b, do_b, kb, vb, a_b, r_b, keep)
            dq = dq + jnp.dot(jnp.concatenate([dz_a, dz_b], axis=1), jnp.concatenate([k_a, k_b], axis=0),
                              preferred_element_type=F32)
            dk_ref[pl.ds(off, b), :] += lax.dot_general(jnp.concatenate([dz_a, dz_b], axis=0), q_rows, TN,
                                                        preferred_element_type=F32)
            dv_ref[pl.ds(off, b), :] += lax.dot_general(jnp.concatenate([w_a, w_b], axis=0), do_rows, TN,
                                                        preferred_element_type=F32)
            return a_a, a_b, r_a, r_b, dq

        zero = jnp.zeros((b, 1), F32)
        carry = step(0, (zero, zero, d_a, d_b, jnp.zeros((b, LANES), F32)), tri)
        carry = step(1, carry, jnp.broadcast_to(qi > 0, tri.shape))
        carry = _sb_walk(qi, carry, lambda n, c: step(n, c, None))
        dq_ref[...] = carry[4]

    blk = pl.BlockSpec((b, LANES), lambda hp, i: (i, hp))
    full = pl.BlockSpec((s, LANES), lambda hp, i: (0, hp))
    full_v = pl.BlockSpec((s, LANES), lambda hp, i: (0, hp + v_col))
    grid = (width // LANES, nkb)
    s_in, s_out, s_shape, s_scratch, s_ops = _side_args(side)
    res = pl.pallas_call(
        _hosted(body, side, 5, 3, grid), name=name, grid=grid,
        in_specs=[blk, full, full_v, blk, blk] + s_in, out_specs=[blk, full, full] + s_out,
        out_shape=[jax.ShapeDtypeStruct((s, width), F32)] * 3 + s_shape,
        scratch_shapes=s_scratch,
        compiler_params=_params(("arbitrary", "arbitrary")),
    )(qs, ks, v, out, dout, *s_ops)
    return _split_side(res, 3, side)


def _cmul(xr, xi, yr, yi):
    return xr * yr - xi * yi, xr * yi + xi * yr


def _scan_consts(ar, ai, reverse, lc):
    rowi = lax.broadcasted_iota(jnp.int32, (SUBLANES, lc), 0)
    pows = [(ar, ai)]
    for _ in range(SUBLANES - 1):
        pows.append(_cmul(*pows[-1], ar, ai))
    steps = []
    for d in (1, 2, 4):
        keep = (rowi < SUBLANES - d) if reverse else (rowi >= d)
        pr, pi = pows[d - 1]
        steps.append((SUBLANES - d if reverse else d, jnp.where(keep, pr, 0.0), jnp.where(keep, pi, 0.0)))
    cr = jnp.zeros((SUBLANES, lc), F32)
    ci = jnp.zeros((SUBLANES, lc), F32)
    for r in range(SUBLANES):
        pr, pi = pows[SUBLANES - 1 - r] if reverse else pows[r]
        cr = jnp.where(rowi == r, pr, cr)
        ci = jnp.where(rowi == r, pi, ci)
    return steps, cr, ci


def _scan_tile(xr, xi, steps, pr, pi, cr, ci):
    for shift, ar, ai in steps:
        rr = pltpu.roll(xr, shift, 0)
        ri = pltpu.roll(xi, shift, 0)
        xr, xi = xr + ar * rr - ai * ri, xi + ar * ri + ai * rr
    return xr + pr * cr - pi * ci, xi + pr * ci + pi * cr


def _ssm_fwd(u, acat, bsup, csup, d_skip, name, tt=1024, side=None):
    s = u.shape[0]
    lc = SCAN_LANES
    tt = min(tt, s)
    nl, nt = N_STATE // lc, s // tt

    def body(u_ref, a_ref, b_ref, c_ref, d_ref, s_ref, y0_ref, y1_ref, carry):
        @pl.when(pl.program_id(1) == 0)
        def _():
            carry[...] = jnp.zeros_like(carry)

        ut = u_ref[...]
        s_ref[...] = _dot(ut, b_ref[0])
        steps, pr, pi = _scan_consts(a_ref[:, :lc], a_ref[:, lc:], False, lc)

        def tile(i, c):
            off = pl.multiple_of(i * SUBLANES, SUBLANES)
            xr, xi = _scan_tile(s_ref[pl.ds(off, SUBLANES), :lc], s_ref[pl.ds(off, SUBLANES), lc:],
                                steps, pr, pi, c[0], c[1])
            s_ref[pl.ds(off, SUBLANES), :lc] = xr
            s_ref[pl.ds(off, SUBLANES), lc:] = xi
            return (jnp.broadcast_to(xr[SUBLANES - 1:, :], (SUBLANES, lc)),
                    jnp.broadcast_to(xi[SUBLANES - 1:, :], (SUBLANES, lc)))

        cr, ci = lax.fori_loop(0, tt // SUBLANES, tile, (carry[:, :lc], carry[:, lc:]))
        carry[:, :lc] = cr
        carry[:, lc:] = ci
        y0 = _dot(s_ref[...], c_ref[0], NT) + d_ref[...] * ut
        y0_ref[...] = y0
        y1_ref[...] = jax.nn.gelu(y0)

    chan = pl.BlockSpec((tt, LANES), lambda j, c: (c, j))
    sup = pl.BlockSpec((1, LANES, 2 * lc), lambda j, c: (j, 0, 0))
    s_in, s_out, s_shape, s_scratch, s_ops = _side_args(side)
    res = pl.pallas_call(
        _hosted(body, side, 5, 3, (nl, nt)), name=name, grid=(nl, nt),
        in_specs=[chan, pl.BlockSpec((1, 2 * lc), lambda j, c: (0, j)), sup, sup,
                  pl.BlockSpec((1, LANES), lambda j, c: (0, j))] + s_in,
        out_specs=[pl.BlockSpec((tt, 2 * lc), lambda j, c: (c, j)), chan, chan] + s_out,
        out_shape=[jax.ShapeDtypeStruct((s, 2 * N_STATE), F32), jax.ShapeDtypeStruct((s, SSM_WIDTH), F32),
                   jax.ShapeDtypeStruct((s, SSM_WIDTH), F32)] + s_shape,
        scratch_shapes=[pltpu.VMEM((SUBLANES, 2 * lc), F32)] + s_scratch,
        compiler_params=_params(("arbitrary", "arbitrary")),
    )(u, acat, bsup, csup, d_skip, *s_ops)
    return _split_side(res, 3, side)


def _ssm_bwd(dy0, states, u, acat, bsup, csup, d_skip, name, tt=1024, side=None):
    s = u.shape[0]
    lc = SCAN_LANES
    tt = min(tt, s)
    nl, nt = N_STATE // lc, s // tt
    nt8 = tt // SUBLANES

    def body(dy_ref, s_ref, sp_ref, u_ref, a_ref, b_ref, c_ref, d_ref,
             du_ref, da_ref, db_ref, dc_ref, dd_ref, lam_ref, carry):
        c = pl.program_id(1)

        @pl.when(c == 0)
        def _():
            carry[...] = jnp.zeros_like(carry)
            for r in (da_ref, db_ref, dc_ref, dd_ref):
                r[...] = jnp.zeros_like(r)

        dy = dy_ref[...]
        ut = u_ref[...]
        lam_ref[...] = _dot(dy, c_ref[0])
        steps, pr, pi = _scan_consts(a_ref[:, :lc], -a_ref[:, lc:], True, lc)
        rowi = lax.broadcasted_iota(jnp.int32, (SUBLANES, lc), 0)
        first_chunk = c == nt - 1

        def tile(i, carry_v):
            cr, ci, dar, dai = carry_v
            t = nt8 - 1 - i
            off = pl.multiple_of(t * SUBLANES, SUBLANES)
            lr, li = _scan_tile(lam_ref[pl.ds(off, SUBLANES), :lc], lam_ref[pl.ds(off, SUBLANES), lc:],
                                steps, pr, pi, cr, ci)
            lam_ref[pl.ds(off, SUBLANES), :lc] = lr
            lam_ref[pl.ds(off, SUBLANES), lc:] = li
            offp = pl.multiple_of(jnp.maximum(t - 1, 0) * SUBLANES, SUBLANES)
            in_chunk = t > 0
            use = jnp.logical_or(in_chunk, jnp.logical_not(first_chunk))
            prev_r = jnp.where(in_chunk, s_ref[pl.ds(offp, SUBLANES), :lc], sp_ref[:, :lc])
            prev_i = jnp.where(in_chunk, s_ref[pl.ds(offp, SUBLANES), lc:], sp_ref[:, lc:])
            last_r = jnp.where(use, jnp.broadcast_to(prev_r[SUBLANES - 1:, :], (SUBLANES, lc)), 0.0)
            last_i = jnp.where(use, jnp.broadcast_to(prev_i[SUBLANES - 1:, :], (SUBLANES, lc)), 0.0)
            sr = jnp.where(rowi == 0, last_r, pltpu.roll(s_ref[pl.ds(off, SUBLANES), :lc], 1, 0))
            si = jnp.where(rowi == 0, last_i, pltpu.roll(s_ref[pl.ds(off, SUBLANES), lc:], 1, 0))
            dar = dar + lr * sr + li * si
            dai = dai + li * sr - lr * si
            return (jnp.broadcast_to(lr[:1, :], (SUBLANES, lc)), jnp.broadcast_to(li[:1, :], (SUBLANES, lc)),
                    dar, dai)

        zero = jnp.zeros((SUBLANES, lc), F32)
        cr, ci, dar, dai = lax.fori_loop(0, nt8, tile, (carry[:, :lc], carry[:, lc:], zero, zero))
        carry[:, :lc] = cr
        carry[:, lc:] = ci
        da_ref[:, :lc] += dar
        da_ref[:, lc:] += dai
        lam = lam_ref[...].astype(BF16)
        du_ref[...] = (_dot(lam, b_ref[0], NT) + d_ref[...] * dy).astype(du_ref.dtype)
        db_ref[0] += _dot(ut, lam, TN)
        dc_ref[0] += _dot(dy, s_ref[...], TN)
        dd_ref[...] += jnp.sum(dy * ut, axis=0, keepdims=True)

    rev = lambda j, c: (nt - 1 - c, j)
    chan = pl.BlockSpec((tt, LANES), rev)
    sup = pl.BlockSpec((1, LANES, 2 * lc), lambda j, c: (j, 0, 0))
    row = pl.BlockSpec((1, LANES), lambda j, c: (0, j))
    s_in, s_out, s_shape, s_scratch, s_ops = _side_args(side)
    res = pl.pallas_call(
        _hosted(body, side, 8, 5, (nl, nt)), name=name, grid=(nl, nt),
        in_specs=[chan, pl.BlockSpec((tt, 2 * lc), rev),
                  pl.BlockSpec((SUBLANES, 2 * lc), lambda j, c: (jnp.maximum((nt - 1 - c) * nt8 - 1, 0), j)),
                  chan, pl.BlockSpec((1, 2 * lc), lambda j, c: (0, j)), sup, sup, row] + s_in,
        out_specs=[chan, pl.BlockSpec((SUBLANES, 2 * lc), lambda j, c: (0, j)), sup, sup, row] + s_out,
        out_shape=[jax.ShapeDtypeStruct((s, SSM_WIDTH), BF16), jax.ShapeDtypeStruct((SUBLANES, 2 * N_STATE), F32),
                   jax.ShapeDtypeStruct(bsup.shape, F32), jax.ShapeDtypeStruct(csup.shape, F32),
                   jax.ShapeDtypeStruct((1, SSM_WIDTH), F32)] + s_shape,
        scratch_shapes=[pltpu.VMEM((tt, 2 * lc), F32), pltpu.VMEM((SUBLANES, 2 * lc), F32)] + s_scratch,
        compiler_params=_params(("arbitrary", "arbitrary")),
    )(dy0, states, states, u, acat, bsup, csup, d_skip, *s_ops)
    return _split_side(res, 5, side)


def _state_cols(xr, xi):
    lead = xr.shape[:-1]
    nl = N_STATE // SCAN_LANES
    both = jnp.stack([xr.reshape(lead + (nl, SCAN_LANES)), xi.reshape(lead + (nl, SCAN_LANES))], axis=-2)
    return both.reshape(lead + (2 * N_STATE,))


def _ssm_mats(a_re, a_im, log_dt, b_re, b_im, c_re, c_im):
    dt = jnp.exp(log_dt)[:, None]
    lr, li = a_re * dt, a_im * dt
    e = jnp.exp(lr)
    abar_r, abar_i = e * jnp.cos(li), e * jnp.sin(li)
    den = a_re * a_re + a_im * a_im
    coef_r = ((abar_r - 1.0) * a_re + abar_i * a_im) / den
    coef_i = (abar_i * a_re - (abar_r - 1.0) * a_im) / den
    bbar_r = coef_r[..., None] * b_re - coef_i[..., None] * b_im
    bbar_i = coef_r[..., None] * b_im + coef_i[..., None] * b_re
    nl = N_STATE // SCAN_LANES
    gpb = SSM_GROUPS // nl
    eye = jnp.eye(gpb, dtype=bool)[None, :, None, :, None]

    def sup(m_r, m_i):
        def one(m):
            m = m.reshape(nl, gpb, SSM_GROUP, 1, SSM_STATE)
            return jnp.where(eye, m, 0.0).reshape(nl, gpb * SSM_GROUP, SCAN_LANES)
        return jnp.concatenate([one(m_r), one(m_i)], axis=-1)

    acat = _state_cols(abar_r.reshape(1, N_STATE), abar_i.reshape(1, N_STATE))
    bsup = sup(jnp.transpose(bbar_r, (0, 2, 1)), jnp.transpose(bbar_i, (0, 2, 1)))
    csup = sup(c_re, -c_im)
    return acat, bsup, csup


def _mem_fwd(mem, g_mem, w_kv, g_k, name):
    ml = mem.shape[0]

    def body(mem_ref, gm_ref, w_ref, gk_ref, memn_ref, kv_ref, kn_ref, vv_ref):
        memn = _rms(mem_ref[...], gm_ref[...])
        memn_ref[...] = memn.astype(BF16)
        kv = _dot(memn, w_ref[...])
        kv_ref[...] = kv
        for hh in range(XA_HEADS):
            sl = slice(hh * XA_HEAD_DIM, (hh + 1) * XA_HEAD_DIM)
            kn_ref[:, sl] = _rms(kv[:, sl], gk_ref[...]).astype(BF16)
        vv_ref[...] = kv[:, XA_WIDTH:].astype(BF16)

    return pl.pallas_call(
        body, name=name,
        out_shape=[jax.ShapeDtypeStruct((ml, D_MODEL), BF16), jax.ShapeDtypeStruct((ml, 2 * XA_WIDTH), F32),
                   jax.ShapeDtypeStruct((ml, XA_WIDTH), BF16), jax.ShapeDtypeStruct((ml, XA_WIDTH), BF16)],
        compiler_params=_params(),
    )(mem, g_mem, w_kv, g_k)


def _mem_bwd(mem, g_mem, memn, w_kv, kv, g_k, dkn, dvv, name):
    def body(mem_ref, gm_ref, memn_ref, w_ref, kv_ref, gk_ref, dkn_ref, dvv_ref, dw_ref, dgm_ref, dgk_ref):
        kv = kv_ref[...]
        dgk = jnp.zeros(dgk_ref.shape, F32)
        parts = []
        for hh in range(XA_HEADS):
            sl = slice(hh * XA_HEAD_DIM, (hh + 1) * XA_HEAD_DIM)
            _, vjp = jax.vjp(_rms, kv[:, sl], gk_ref[...])
            dk, dg = vjp(dkn_ref[:, sl])
            parts.append(dk)
            dgk = dgk + dg
        dgk_ref[...] = dgk
        dkv = jnp.concatenate(parts + [dvv_ref[...]], axis=1)
        dw_ref[...] = _dot(memn_ref[...], dkv, TN)
        dmemn = _dot(dkv, w_ref[...], NT)
        _, vjp = jax.vjp(_rms, mem_ref[...], gm_ref[...])
        dgm_ref[...] = vjp(dmemn)[1]

    return pl.pallas_call(
        body, name=name,
        out_shape=[jax.ShapeDtypeStruct((D_MODEL, 2 * XA_WIDTH), F32), jax.ShapeDtypeStruct(g_mem.shape, F32),
                   jax.ShapeDtypeStruct(g_k.shape, F32)],
        compiler_params=_params(),
    )(mem, g_mem, memn, w_kv, kv, g_k, dkn, dvv)


def _xa_head(qx_h, g_q, kn_h, vv_h):
    qn = _rms(qx_h, g_q)
    sc = _dot(qn, kn_h, NT) * (XA_HEAD_DIM ** -0.5)
    sc = sc - jnp.max(sc, axis=-1, keepdims=True)
    e = jnp.exp(sc)
    p = e / jnp.sum(e, axis=-1, keepdims=True)
    return qn, p


def _xa_fwd(qx, g_q, kn, vv, name):
    def fn(qt, gq, knt, vvt):
        outs = []
        for hh in range(XA_HEADS):
            sl = slice(hh * XA_HEAD_DIM, (hh + 1) * XA_HEAD_DIM)
            _, p = _xa_head(qt[:, sl], gq, knt[:, sl], vvt[:, sl])
            outs.append(_dot(p, vvt[:, sl]))
        return (jnp.concatenate(outs, axis=1),), ()

    return _rw(fn, [qx], [g_q, kn, vv], [(XA_WIDTH, BF16)], [], name)[0]


def _xa_bwd(qx, g_q, kn, vv, do, name):
    def fn(qt, dot_, gq, knt, vvt):
        dqs, dks, dvs = [], [], []
        dgq = jnp.zeros_like(gq)
        for hh in range(XA_HEADS):
            sl = slice(hh * XA_HEAD_DIM, (hh + 1) * XA_HEAD_DIM)
            qn, p = _xa_head(qt[:, sl], gq, knt[:, sl], vvt[:, sl])
            doh = dot_[:, sl]
            dp = _dot(doh, vvt[:, sl], NT)
            dvs.append(_dot(p, doh, TN))
            ds = p * (dp - jnp.sum(dp * p, axis=-1, keepdims=True)) * (XA_HEAD_DIM ** -0.5)
            dqn = _dot(ds, knt[:, sl])
            dks.append(_dot(ds, qn, TN))
            _, vjp = jax.vjp(_rms, qt[:, sl], gq)
            dq, dg = vjp(dqn)
            dqs.append(dq)
            dgq = dgq + dg
        return ((jnp.concatenate(dqs, axis=1),),
                (jnp.concatenate(dks, axis=1), jnp.concatenate(dvs, axis=1), dgq))

    return _rw(fn, [qx, do], [g_q, kn, vv], [(XA_WIDTH, BF16)], [kn.shape, vv.shape, g_q.shape], name)


BIG = [
    ("w_in", (D_MODEL, IN_WIDTH), 1), ("ssm_w_glu", (SSM_WIDTH, SSM_WIDTH), 0), ("w_out", (D_MODEL, D_MODEL), 0),
    ("xa_w_q", (D_MODEL, XA_WIDTH), 0), ("xa_w_kv", (D_MODEL, 2 * XA_WIDTH), 0), ("xa_w_o", (XA_WIDTH, D_MODEL), 1),
    ("w_up", (D_MODEL, D_FF), 1), ("w_down", (D_FF, D_MODEL), 0),
]
BIG_INDEX = {n: i for i, (n, _, _) in enumerate(BIG)}


def _shard_shape(shape, axis):
    return tuple(d // N_DEV if i == axis else d for i, d in enumerate(shape))


def _shard_of(ref, axis, d):
    n = ref.shape[axis] // N_DEV
    return ref.at[pl.ds(d * n, n), :] if axis == 0 else ref.at[:, pl.ds(d * n, n)]


def _gather_side(names, shards):
    idxs = [BIG_INDEX[n] for n in names]

    def make(ins, outs, send_sems, recv_sems):
        x, y, c = lax.axis_index("x"), lax.axis_index("y"), lax.axis_index("c")
        cps = []
        for j, i in enumerate(idxs):
            mine = _shard_of(outs[j], BIG[i][2], 4 * x + 2 * y + c)
            cps.append(pltpu.make_async_copy(ins[j], mine, send_sems.at[N_DEV * j]))
            for rel in range(1, N_DEV):
                to = tuple(1 - p if rel >> bit & 1 else p for p, bit in ((x, 2), (y, 1), (c, 0)))
                cps.append(pltpu.make_async_remote_copy(
                    src_ref=ins[j], dst_ref=mine, send_sem=send_sems.at[N_DEV * j + rel],
                    recv_sem=recv_sems.at[N_DEV * j + rel], device_id=to, device_id_type=MESH))
        return cps

    return _Side(shards, [jax.ShapeDtypeStruct(BIG[i][1], BF16) for i in idxs], N_DEV * len(idxs), make)


def _sibling_side(names, grads):
    idxs = [BIG_INDEX[n] for n in names]

    def make(ins, outs, send_sems, recv_sems):
        x, y, c = lax.axis_index("x"), lax.axis_index("y"), lax.axis_index("c")
        return [pltpu.make_async_remote_copy(
            src_ref=_shard_of(ins[j], BIG[i][2], 2 * k + (1 - c)), dst_ref=outs[j].at[k],
            send_sem=send_sems.at[4 * j + k], recv_sem=recv_sems.at[4 * j + k], device_id=(x, y, 1 - c),
            device_id_type=MESH) for j, i in enumerate(idxs) for k in range(4)]

    shapes = [jax.ShapeDtypeStruct((4,) + _shard_shape(BIG[i][1], BIG[i][2]), F32) for i in idxs]
    return _Side(grads, shapes, 4 * len(idxs), make)


def _chips_side(parts):
    def make(ins, outs, send_sems, recv_sems):
        x, y, c = lax.axis_index("x"), lax.axis_index("y"), lax.axis_index("c")
        chips = [(1 - x, y), (x, 1 - y), (1 - x, 1 - y)]
        return [pltpu.make_async_remote_copy(
            src_ref=ins[j].at[2 * cx + cy], dst_ref=outs[j].at[r], send_sem=send_sems.at[3 * j + r],
            recv_sem=recv_sems.at[3 * j + r], device_id=(cx, cy, c), device_id_type=MESH)
            for r, (cx, cy) in enumerate(chips) for j in range(len(parts))]

    return _Side(parts, [jax.ShapeDtypeStruct((3,) + p.shape[1:], p.dtype) for p in parts], 3 * len(parts), make)


def _reduce_add(grad, recv, axis, core, name):
    rs, cs = recv.shape[1:]
    rt = _row_tile(rs, 256)
    nt = rs // rt

    def body(c_ref, g_ref, r_ref, p_ref, pb_ref):
        sm = g_ref[...] + r_ref[0]
        p_ref[0] = sm
        pb_ref[0] = sm.astype(BF16)

    if axis == 0:
        g_spec = pl.BlockSpec((rt, cs), lambda k, t, c_ref: ((2 * k + c_ref[0]) * nt + t, 0))
    else:
        g_spec = pl.BlockSpec((rt, cs), lambda k, t, c_ref: (t, 2 * k + c_ref[0]))
    slab = pl.BlockSpec((1, rt, cs), lambda k, t, c_ref: (k, t, 0))
    return pl.pallas_call(
        body, name=name,
        grid_spec=pltpu.PrefetchScalarGridSpec(num_scalar_prefetch=1, grid=(4, nt), in_specs=[g_spec, slab],
                                               out_specs=[slab, slab]),
        out_shape=[jax.ShapeDtypeStruct(recv.shape, F32), jax.ShapeDtypeStruct(recv.shape, BF16)],
        compiler_params=_params(("parallel", "parallel")),
    )(core, grad, recv)


def _all_gather(block, name, side):
    m_per, n = block.shape
    ns_in, ns_out = len(side.ins), len(side.out_shapes)

    def body(*refs):
        x_ref, s_ins, out_ref = refs[0], refs[1:1 + ns_in], refs[1 + ns_in]
        s_outs = refs[2 + ns_in:2 + ns_in + ns_out]
        send_sems, recv_sems, local_sem, s_send, s_recv = refs[2 + ns_in + ns_out:]
        others = side.make(s_ins, s_outs, s_send, s_recv)
        for cp in others:
            cp.start()
        x, y, c = lax.axis_index("x"), lax.axis_index("y"), lax.axis_index("c")
        me, sibling = (x, y, c), (x, y, 1 - c)
        chips = [(1 - x, y), (x, 1 - y), (1 - x, 1 - y)]

        def rows(px, py, pc):
            return out_ref.at[pl.ds((4 * px + 2 * py + pc) * m_per, m_per), :]

        def copy(k, blk, to, src=None):
            return pltpu.make_async_remote_copy(
                src_ref=rows(*blk) if src is None else src, dst_ref=rows(*blk),
                send_sem=send_sems.at[k], recv_sem=recv_sems.at[k], device_id=to, device_id_type=MESH)

        mine = pltpu.make_async_copy(x_ref, rows(*me), local_sem)
        mine.start()
        first = [copy(0, me, sibling, src=x_ref)]
        first += [copy(1 + j, me, (*chip, c), src=x_ref) for j, chip in enumerate(chips)]
        for cp in first:
            cp.start()
        passed = [copy(4 + j, (*chip, c), sibling) for j, chip in enumerate(chips)]
        for j, chip in enumerate(chips):
            copy(1 + j, (*chip, c), me).wait_recv()
            passed[j].start()
        copy(0, sibling, me).wait_recv()
        for j, chip in enumerate(chips):
            copy(4 + j, (*chip, 1 - c), me).wait_recv()
        for cp in first + passed:
            cp.wait_send()
        mine.wait()
        for cp in others:
            cp.wait()

    res = pl.pallas_call(
        body, name=name, in_specs=[ANY] * (1 + ns_in), out_specs=[ANY] * (1 + ns_out),
        out_shape=[jax.ShapeDtypeStruct((N_DEV * m_per, n), block.dtype)] + side.out_shapes,
        scratch_shapes=[pltpu.SemaphoreType.DMA((7,)), pltpu.SemaphoreType.DMA((7,)), pltpu.SemaphoreType.DMA]
        + side.sems(),
    )(block, *side.ins)
    return res[0], list(res[1:])


def _adam_math(w, g, m, v):
    m = ADAM_B1 * m + (1.0 - ADAM_B1) * g
    v = ADAM_B2 * v + (1.0 - ADAM_B2) * (g * g)
    m_hat = m / (1.0 - ADAM_B1 ** ADAM_STEP)
    v_hat = v / (1.0 - ADAM_B2 ** ADAM_STEP)
    delta = -ADAM_LR * (m_hat / (jnp.sqrt(v_hat) + ADAM_EPS) + ADAM_WD * w)
    return delta, m, v


def _adam_sharded(own, recv, w, m, v, chip, name):
    rs, cs = w.shape
    rt = _row_tile(rs, 256)

    def body(chip_ref, p_ref, r_ref, w_ref, m_ref, v_ref, g_out, d_out, m_out, v_out):
        g = p_ref[0] + r_ref[0].astype(F32) + r_ref[1].astype(F32) + r_ref[2].astype(F32)
        d, mn, vn = _adam_math(w_ref[...], g, m_ref[...], v_ref[...])
        g_out[...] = g
        d_out[...] = d
        m_out[...] = mn
        v_out[...] = vn

    tile = pl.BlockSpec((rt, cs), lambda t, chip_ref: (t, 0))
    return pl.pallas_call(
        body, name=name,
        grid_spec=pltpu.PrefetchScalarGridSpec(
            num_scalar_prefetch=1, grid=(rs // rt,),
            in_specs=[pl.BlockSpec((1, rt, cs), lambda t, chip_ref: (chip_ref[0], t, 0)),
                      pl.BlockSpec((3, rt, cs), lambda t, chip_ref: (0, t, 0)), tile, tile, tile],
            out_specs=[tile] * 4),
        out_shape=[jax.ShapeDtypeStruct((rs, cs), F32)] * 4,
        compiler_params=_params(("parallel",)),
    )(chip, own, recv, w, m, v)


SMALL = ["g_mix", "ssm_a_re", "ssm_a_im", "ssm_log_dt", "ssm_b_re", "ssm_b_im", "ssm_c_re", "ssm_c_im", "ssm_d",
         "sb_g_q", "sb_g_k", "g_out_ssm", "g_out_sb", "g_xa", "g_mem", "xa_g_q", "xa_g_k", "g_mlp"]
PACK_TILE = SUBLANES * LANES


def _natural_2d(n):
    return (n // LANES, LANES) if n % LANES == 0 else (1, n)


def _pack_small(arrs):
    parts = []
    for a in arrs:
        flat = a.reshape(-1)
        parts.append(jnp.pad(flat, (0, (-flat.shape[0]) % PACK_TILE)))
    return jnp.concatenate(parts).reshape(-1, LANES)


def _adam_replicated(gathered, sizes, ws, ms, vs, name):
    n_w = len(ws)
    r_dev = gathered.shape[0] // N_DEV
    offs, off = [], 0
    for n in sizes:
        offs.append(off)
        off += (n + PACK_TILE - 1) // PACK_TILE * SUBLANES
    assert off == r_dev

    def body(*refs):
        g_ref = refs[0]
        w_refs, m_refs, v_refs = refs[1:1 + n_w], refs[1 + n_w:1 + 2 * n_w], refs[1 + 2 * n_w:1 + 3 * n_w]
        outs = refs[1 + 3 * n_w:]

        def total(i, shape):
            r, cdim = shape
            acc = g_ref[pl.ds(offs[i], r), :cdim]
            for d in range(1, N_DEV):
                acc = acc + g_ref[pl.ds(d * r_dev + offs[i], r), :cdim]
            return acc

        for i in range(n_w):
            g = total(i, w_refs[i].shape)
            d, mn, vn = _adam_math(w_refs[i][...], g, m_refs[i][...], v_refs[i][...])
            for o, val in zip(outs[4 * i:4 * i + 4], (g, d, mn, vn)):
                o[...] = val
        outs[4 * n_w][...] = total(n_w, (SUBLANES, LANES))

    shapes = [w.shape for w in ws]
    return pl.pallas_call(
        body, name=name,
        out_shape=[jax.ShapeDtypeStruct(shp, F32) for shp in shapes for _ in range(4)]
        + [jax.ShapeDtypeStruct((SUBLANES, LANES), F32)],
        compiler_params=_params(),
    )(gathered, *ws, *ms, *vs)


def _step(x, mem, target, shards, sm, core):
    g, w, sums, reduced = {}, {}, {}, {}

    def gather(names):
        return _gather_side(names, [shards[n] for n in names])

    def to_sibling(names):
        return _sibling_side(names, [g[n] for n in names])

    def add_sibling(names, received):
        for n, r in zip(names, received):
            sums[n] = _reduce_add(g[n], r, BIG[BIG_INDEX[n]][2], core, "reduce_add_" + n)

    def to_chips(names):
        return _chips_side([sums[n][1] for n in names])

    def keep(names, received):
        for n, r in zip(names, received):
            reduced[n] = (sums[n][0], r)

    row = lambda a: a.reshape(1, -1)
    g_mix, g_xa, g_mlp, g_mem = row(sm["g_mix"]), row(sm["g_xa"]), row(sm["g_mlp"]), row(sm["g_mem"])
    g_os, g_ob = row(sm["g_out_ssm"]), row(sm["g_out_sb"])
    sb_gq, sb_gk = jnp.tile(row(sm["sb_g_q"]), (1, SB_HEADS)), jnp.tile(row(sm["sb_g_k"]), (1, SB_HEADS))
    xa_gq, xa_gk = row(sm["xa_g_q"]), row(sm["xa_g_k"])
    d_skip = row(sm["ssm_d"])

    h1, (w["w_in"],) = _norm_fwd(x, g_mix, "norm_mix", side=gather(["w_in"]))
    proj = _mm(h1, w["w_in"], "nn", "in_proj")
    u = proj
    q_raw, k_raw = (proj, SB_WIDTH, 1), (proj, SB_WIDTH, 2)
    v_col = (SSM_WIDTH + 2 * SB_WIDTH) // LANES
    sb_scale = SB_HEAD_DIM ** -0.5
    qk_norm = lambda scale: (lambda xt, gt: ((_rms_groups(xt, gt, scale),), ()))
    qs = _rw(qk_norm(sb_scale), [q_raw], [sb_gq], [(SB_WIDTH, BF16)], [], "sb_qnorm")[0]
    ks = _rw(qk_norm(1.0), [k_raw], [sb_gk], [(SB_WIDTH, BF16)], [], "sb_knorm")[0]
    early = ["ssm_w_glu", "w_out", "xa_w_q", "xa_w_kv", "xa_w_o", "w_up"]
    y_sb, got = _sb_fwd(qs, ks, proj, "sb_fwd", v_col=v_col, side=gather(early))
    w.update(zip(early, got))

    ssm_args = (sm["ssm_a_re"], sm["ssm_a_im"], sm["ssm_log_dt"], sm["ssm_b_re"], sm["ssm_b_im"],
                sm["ssm_c_re"], sm["ssm_c_im"])
    (acat, bsup, csup), mats_vjp = jax.vjp(_ssm_mats, *ssm_args)
    (states, y0, y1), (w["w_down"],) = _ssm_fwd(u, acat, bsup, csup, d_skip, "ssm_fwd", side=gather(["w_down"]))
    z_glu, y_ssm = _mm(y1, w["ssm_w_glu"], "nn", "ssm_glu", epi=lambda r, yt: (r, yt * jax.nn.sigmoid(r)),
                       extras=(y1,), out_dtypes=(F32, F32))

    def cat_norm(a, b, ga, gb):
        return jnp.concatenate([_rms(a, ga), _rms(b, gb)], axis=1)

    ycat = _rw(lambda a, b, ga, gb: ((cat_norm(a, b, ga, gb),), ()), [y_ssm, y_sb], [g_os, g_ob],
               [(D_MODEL, BF16)], [], "norm_out")[0]
    def residual_norm_epi(r, xt, gt):
        xn = r + xt
        return xn, _rms(xn, gt)

    x1, h2 = _mm(ycat, w["w_out"], "nn", "out_proj", epi=residual_norm_epi, extras=(x,), fulls=(g_xa,),
                 out_dtypes=(F32, BF16))
    qx = _mm(h2, w["xa_w_q"], "nn", "xa_q")
    memn, kv, kn_x, vv_x = _mem_fwd(mem, g_mem, w["xa_w_kv"], xa_gk, "xa_mem")
    o_xa = _xa_fwd(qx, xa_gq, kn_x, vv_x, "xa_fwd")
    x2, h3 = _mm(o_xa, w["xa_w_o"], "nn", "xa_o", epi=residual_norm_epi, extras=(x1,), fulls=(g_mlp,),
                 out_dtypes=(F32, BF16))

    def up_epi(r):
        rl = jnp.maximum(r, 0.0)
        return (rl * rl,)

    r_up = _mm(h3, w["w_up"], "nn", "mlp_up", epi=up_epi, out_dtypes=(BF16,))

    def loss_epi(r, xt, tt):
        d = r + xt - tt
        return (d * (1.0 / D_MODEL),) * 2, (jnp.sum(d * d, axis=0, keepdims=True),)

    dx3, dx3_b, sq = _mm(r_up, w["w_down"], "nn", "mlp_down", epi=loss_epi, extras=(x2, target),
                         out_dtypes=(F32, BF16), sums=[(1, D_MODEL)])
    loss = jnp.sum(sq) * (0.5 / D_MODEL)

    def norm_bwd_epi(r, xt, drt, gt):
        _, vjp = jax.vjp(_rms, xt, gt)
        dx_, dg_ = vjp(r)
        return (dx_ + drt,) * 2, (dg_,)

    g["w_down"] = _mm(r_up, dx3_b, "tn", "d_w_down", tk=2048)
    da = _mm(dx3_b, w["w_down"], "nt", "d_r", epi=lambda r, rt: (r * 2.0 * jnp.sqrt(rt.astype(F32)),), extras=(r_up,),
             out_dtypes=(BF16,))
    g["w_up"] = _mm(h3, da, "tn", "d_w_up", tk=2048)
    mlp = ["w_down", "w_up"]
    (dx2, dx2_b, g["g_mlp"]), got = _mm(da, w["w_up"], "nt", "d_h3", epi=norm_bwd_epi, extras=(x2, dx3),
                                        fulls=(g_mlp,), out_dtypes=(F32, BF16), sums=[g_mlp.shape],
                                        side=to_sibling(mlp))
    add_sibling(mlp, got)
    g["xa_w_o"] = _mm(o_xa, dx2_b, "tn", "d_xa_w_o", tk=2048)
    do_xa = _mm(dx2_b, w["xa_w_o"], "nt", "d_o_xa")
    dqx, dkn_x, dvv_x, g["xa_g_q"] = _xa_bwd(qx, xa_gq, kn_x, vv_x, do_xa, "xa_bwd")
    g["xa_w_kv"], g["g_mem"], g["xa_g_k"] = _mem_bwd(mem, g_mem, memn, w["xa_w_kv"], kv, xa_gk, dkn_x, dvv_x,
                                                     "xa_mem_bwd")
    g["xa_w_q"] = _mm(h2, dqx, "tn", "d_xa_w_q", tk=2048)
    dx1, dx1_b, g["g_xa"] = _mm(dqx, w["xa_w_q"], "nt", "d_h2", epi=norm_bwd_epi, extras=(x1, dx2), fulls=(g_xa,),
                                out_dtypes=(F32, BF16), sums=[g_xa.shape])
    g["w_out"] = _mm(ycat, dx1_b, "tn", "d_w_out", tk=2048)
    dycat = _mm(dx1_b, w["w_out"], "nt", "d_ycat")

    def cat_bwd(a, b, dy, ga, gb):
        _, vjp = jax.vjp(cat_norm, a, b, ga, gb)
        da_, db_, dga, dgb = vjp(dy)
        return (da_, db_), (dga, dgb)

    dy_ssm, dy_sb, g["g_out_ssm"], g["g_out_sb"] = _rw(
        cat_bwd, [y_ssm, y_sb, dycat], [g_os, g_ob], [(SSM_WIDTH, F32), (SB_WIDTH, F32)], [g_os.shape, g_ob.shape],
        "d_norm_out")

    def glu_bwd(dy, yt, zt):
        sg = jax.nn.sigmoid(zt)
        return (dy * sg, dy * yt * sg * (1.0 - sg)), ()

    dy1_a, dz = _rw(glu_bwd, [dy_ssm, y1, z_glu], [], [(SSM_WIDTH, F32), (SSM_WIDTH, BF16)], [], "d_glu")
    g["ssm_w_glu"] = _mm(y1, dz, "tn", "d_w_glu", tk=2048)

    def gelu_bwd_epi(r, da_, y0t):
        _, vjp = jax.vjp(jax.nn.gelu, y0t)
        return (vjp(r + da_)[0],)

    mid = ["w_out", "xa_w_q", "xa_w_kv", "xa_w_o", "ssm_w_glu"]
    dy0, got = _mm(dz, w["ssm_w_glu"], "nt", "d_y1", epi=gelu_bwd_epi, extras=(dy1_a, y0), side=to_sibling(mid))
    add_sibling(mid, got)
    (du, da8, d_bsup, d_csup, g["ssm_d"]), got = _ssm_bwd(dy0, states, u, acat, bsup, csup, d_skip, "ssm_bwd",
                                                          side=to_chips(mlp))
    keep(mlp, got)
    d_acat = jnp.sum(da8, axis=0, keepdims=True)
    for nm, val in zip(("ssm_a_re", "ssm_a_im", "ssm_log_dt", "ssm_b_re", "ssm_b_im", "ssm_c_re", "ssm_c_im"),
                       mats_vjp((d_acat, d_bsup, d_csup))):
        g[nm] = val

    (dqs, dks, dvs), got = _sb_bwd(qs, ks, proj, y_sb, dy_sb, "sb_bwd", v_col=v_col, side=to_chips(mid))
    keep(mid, got)

    def qk_norm_bwd(scale):
        def fn(xt, dt, gt):
            _, vjp = jax.vjp(lambda a, b_: _rms_groups(a, b_, scale), xt, gt)
            dx_, dg_ = vjp(dt)
            return (dx_,), (dg_,)
        return fn

    dq_raw, dgq = _rw(qk_norm_bwd(sb_scale), [q_raw, dqs], [sb_gq], [(SB_WIDTH, BF16)], [sb_gq.shape], "d_sb_qnorm")
    dk_raw, dgk = _rw(qk_norm_bwd(1.0), [k_raw, dks], [sb_gk], [(SB_WIDTH, BF16)], [sb_gk.shape],
                      "d_sb_knorm")
    g["sb_g_q"] = jnp.sum(dgq.reshape(SB_HEADS, SB_HEAD_DIM), axis=0)
    g["sb_g_k"] = jnp.sum(dgk.reshape(SB_HEADS, SB_HEAD_DIM), axis=0)
    dproj = jnp.concatenate([du, dq_raw, dk_raw, dvs.astype(BF16)], axis=1)
    g["w_in"] = _mm(h1, dproj, "tn", "d_w_in", tk=2048)
    dh1, got = _mm(dproj, w["w_in"], "nt", "d_h1", side=to_sibling(["w_in"]))
    add_sibling(["w_in"], got)
    dx, g["g_mix"] = _norm_bwd(x, g_mix, dh1, dx1, "d_norm_mix")

    packed = _pack_small([g[n] for n in SMALL] + [loss.reshape(1)])
    everyone, got = _all_gather(packed, "gather_small", to_chips(["w_in"]))
    keep(["w_in"], got)
    return dx, everyone, reduced


def kernel(x, mem, g_mix, w_in, ssm_a_re, ssm_a_im, ssm_log_dt, ssm_b_re, ssm_b_im, ssm_c_re, ssm_c_im, ssm_d, ssm_w_glu, sb_g_q, sb_g_k, g_out_ssm, g_out_sb, w_out, g_xa, g_mem, xa_w_q, xa_w_kv, xa_g_q, xa_g_k, xa_w_o, g_mlp, w_up, w_down, loss_target, m_g_mix, m_w_in, m_ssm_a_re, m_ssm_a_im, m_ssm_log_dt, m_ssm_b_re, m_ssm_b_im, m_ssm_c_re, m_ssm_c_im, m_ssm_d, m_ssm_w_glu, m_sb_g_q, m_sb_g_k, m_g_out_ssm, m_g_out_sb, m_w_out, m_g_xa, m_g_mem, m_xa_w_q, m_xa_w_kv, m_xa_g_q, m_xa_g_k, m_xa_w_o, m_g_mlp, m_w_up, m_w_down, v_g_mix, v_w_in, v_ssm_a_re, v_ssm_a_im, v_ssm_log_dt, v_ssm_b_re, v_ssm_b_im, v_ssm_c_re, v_ssm_c_im, v_ssm_d, v_ssm_w_glu, v_sb_g_q, v_sb_g_k, v_g_out_ssm, v_g_out_sb, v_w_out, v_g_xa, v_g_mem, v_xa_w_q, v_xa_w_kv, v_xa_g_q, v_xa_g_k, v_xa_w_o, v_g_mlp, v_w_up, v_w_down):
    given = dict(locals())
    order = ["g_mix", "w_in", "ssm_a_re", "ssm_a_im", "ssm_log_dt", "ssm_b_re", "ssm_b_im", "ssm_c_re", "ssm_c_im",
             "ssm_d", "ssm_w_glu", "sb_g_q", "sb_g_k", "g_out_ssm", "g_out_sb", "w_out", "g_xa", "g_mem", "xa_w_q",
             "xa_w_kv", "xa_g_q", "xa_g_k", "xa_w_o", "g_mlp", "w_up", "w_down"]
    assert sorted([n for n, _, _ in BIG] + SMALL) == sorted(order)
    core = lax.axis_index("c").astype(jnp.int32).reshape(1)
    chip = (2 * lax.axis_index("x") + lax.axis_index("y")).astype(jnp.int32).reshape(1)

    shards = {n: given[n][0].astype(BF16) for n, _, _ in BIG}
    sm = {n: given[n][0] for n in SMALL}
    dx, everyone, reduced = _step(x[0], mem[0], loss_target[0], shards, sm, core)

    res = {}
    for n, _, _ in BIG:
        own, recv = reduced[n]
        outs = _adam_sharded(own, recv, given[n][0], given["m_" + n][0], given["v_" + n][0], chip, "adam_" + n)
        for kind, val in zip(("grad", "delta", "new_m", "new_v"), outs):
            res[kind + "_" + n] = val[None]

    sizes = [math.prod(sm[n].shape) for n in SMALL] + [1]
    nat = lambda a: a.reshape(_natural_2d(math.prod(a.shape)))
    outs = _adam_replicated(everyone, sizes, [nat(sm[n]) for n in SMALL], [nat(given["m_" + n][0]) for n in SMALL],
                            [nat(given["v_" + n][0]) for n in SMALL], "adam_replicated")
    for i, n in enumerate(SMALL):
        for kind, val in zip(("grad", "delta", "new_m", "new_v"), outs[4 * i:4 * i + 4]):
            res[kind + "_" + n] = val.reshape(given[n].shape)
    loss_out = outs[-1][0, 0]
    return (loss_out, dx[None], *[res["grad_" + n] for n in order], *[res["delta_" + n] for n in order],
            *[res["new_m_" + n] for n in order], *[res["new_v_" + n] for n in order])
```

```python
import functools
import math

import jax
import jax.numpy as jnp
from jax import lax
from jax.experimental import pallas as pl
from jax.experimental.pallas import tpu as pltpu

F32 = jnp.float32
BF16 = jnp.bfloat16
MESH = pl.DeviceIdType.MESH

N_DEV = 8
D_MODEL = 1024
SSM_WIDTH = 512
SSM_GROUP = 16
SSM_GROUPS = 32
SSM_STATE = 64
N_STATE = SSM_GROUPS * SSM_STATE
SB_HEADS = 8
SB_HEAD_DIM = 64
SB_WIDTH = 512
IN_WIDTH = 2048
XA_HEADS = 4
XA_HEAD_DIM = 128
XA_WIDTH = 512
D_FF = 4096
NORM_EPS = 1e-6
ADAM_LR = 0.001
ADAM_B1 = 0.9
ADAM_B2 = 0.999
ADAM_EPS = 1e-08
ADAM_WD = 0.01
ADAM_STEP = 10

LANES = 128
SUBLANES = 8
VMEM_LIMIT = 56 * 1024 * 1024
SCAN_LANES = 512
SB_BLOCK = 256
SB_UNDERFLOW = -110.0

NN = (((1,), (0,)), ((), ()))
NT = (((1,), (1,)), ((), ()))
TN = (((0,), (0,)), ((), ()))


def _params(sem=None):
    return pltpu.CompilerParams(dimension_semantics=sem, vmem_limit_bytes=VMEM_LIMIT)


def _dot(a, b, dims=NN):
    return lax.dot_general(a.astype(BF16), b.astype(BF16), dims, preferred_element_type=F32)


def _rms(x, g):
    return x * lax.rsqrt(jnp.mean(x * x, axis=-1, keepdims=True) + NORM_EPS) * g


ANY = pl.BlockSpec(memory_space=pl.ANY)


class _Side:
    def __init__(self, ins, out_shapes, n_sem, make):
        self.ins, self.out_shapes, self.n_sem, self.make = list(ins), list(out_shapes), n_sem, make

    def sems(self):
        return [pltpu.SemaphoreType.DMA((self.n_sem,)), pltpu.SemaphoreType.DMA((self.n_sem,))]


def _hosted(body, side, n_in, n_out, grid):
    if side is None:
        return body
    ns_in, ns_out = len(side.ins), len(side.out_shapes)

    def wrapped(*refs):
        ins, refs = refs[:n_in], refs[n_in:]
        s_ins, refs = refs[:ns_in], refs[ns_in:]
        outs, refs = refs[:n_out], refs[n_out:]
        s_outs, refs = refs[:ns_out], refs[ns_out:]
        scratch, sems = refs[:-2], refs[-2:]
        ids = [pl.program_id(d) for d in range(len(grid))]
        first = functools.reduce(jnp.logical_and, [i == 0 for i in ids])
        last = functools.reduce(jnp.logical_and, [i == n - 1 for i, n in zip(ids, grid)])

        @pl.when(first)
        def _():
            for cp in side.make(s_ins, s_outs, *sems):
                cp.start()

        body(*ins, *outs, *scratch)

        @pl.when(last)
        def _():
            for cp in side.make(s_ins, s_outs, *sems):
                cp.wait()

    return wrapped


def _side_args(side):
    if side is None:
        return [], [], [], [], []
    return ([ANY] * len(side.ins), [ANY] * len(side.out_shapes), side.out_shapes, side.sems(), side.ins)


def _split_side(res, n_out, side):
    res = list(res)
    main = res[0] if n_out == 1 else res[:n_out]
    return main if side is None else (main, res[n_out:])


def _mm(a, b, mode, name, *, epi=None, extras=(), fulls=(), out_dtypes=(F32,), sums=(), tm=1024, tn=1024, tk=1024,
        side=None):
    if mode == "nn":
        (m, k), (k2, n) = a.shape, b.shape
    elif mode == "nt":
        (m, k), (n, k2) = a.shape, b.shape
    else:
        (k, m), (k2, n) = a.shape, b.shape
    assert k == k2, (name, a.shape, b.shape)
    tm, tn, tk = min(tm, m), min(tn, n), min(tk, k)
    assert m % tm == 0 and n % tn == 0 and k % tk == 0, (name, m, n, k)
    nk = k // tk
    dims = {"nn": NN, "nt": NT, "tn": TN}[mode]
    if mode == "tn":
        a_spec = pl.BlockSpec((tk, tm), lambda i, j, kk: (kk, i))
    else:
        a_spec = pl.BlockSpec((tm, tk), lambda i, j, kk: (i, kk))
    if mode == "nt":
        b_spec = pl.BlockSpec((tn, tk), lambda i, j, kk: (j, kk))
    else:
        b_spec = pl.BlockSpec((tk, tn), lambda i, j, kk: (kk, j))
    mn_spec = pl.BlockSpec((tm, tn), lambda i, j, kk: (i, j))
    n_ex, n_full, n_out, n_sum = len(extras), len(fulls), len(out_dtypes), len(sums)
    n_in = 2 + n_ex + n_full

    def body(*refs):
        a_ref, b_ref = refs[:2]
        ex = refs[2:n_in]
        outs = refs[n_in:n_in + n_out]
        sum_refs = refs[n_in + n_out:n_in + n_out + n_sum]
        kk = pl.program_id(2)
        first_tile = jnp.logical_and(pl.program_id(0) == 0, pl.program_id(1) == 0)

        def finish(r):
            vals = epi(r, *[e[...] for e in ex]) if epi is not None else (r,)
            if n_sum:
                vals, parts = vals

                @pl.when(first_tile)
                def _():
                    for sr in sum_refs:
                        sr[...] = jnp.zeros_like(sr)

                for sr, p in zip(sum_refs, parts):
                    sr[...] += p
            for o, v in zip(outs, vals):
                o[...] = v.astype(o.dtype)

        if nk == 1:
            finish(_dot(a_ref[...], b_ref[...], dims))
        else:
            acc = refs[n_in + n_out + n_sum]

            @pl.when(kk == 0)
            def _():
                acc[...] = jnp.zeros_like(acc)

            acc[...] += _dot(a_ref[...], b_ref[...], dims)

            @pl.when(kk == nk - 1)
            def _():
                finish(acc[...])

    grid = (m // tm, n // tn, nk)
    whole = lambda shape: pl.BlockSpec(shape, lambda i, j, kk: (0,) * len(shape))
    s_in, s_out, s_shape, s_scratch, s_ops = _side_args(side)
    seq = bool(side) or n_sum > 0
    res = pl.pallas_call(
        _hosted(body, side, n_in, n_out + n_sum, grid), name=name, grid=grid,
        in_specs=[a_spec, b_spec] + [mn_spec] * n_ex + [whole(f.shape) for f in fulls] + s_in,
        out_specs=[mn_spec] * n_out + [whole(shape) for shape in sums] + s_out,
        out_shape=[jax.ShapeDtypeStruct((m, n), dt) for dt in out_dtypes]
        + [jax.ShapeDtypeStruct(shape, F32) for shape in sums] + s_shape,
        scratch_shapes=([pltpu.VMEM((tm, tn), F32)] if nk > 1 else []) + s_scratch,
        compiler_params=_params(("arbitrary",) * 3 if seq else ("parallel", "parallel", "arbitrary")),
    )(a, b, *extras, *fulls, *s_ops)
    return _split_side(res, n_out + n_sum, side)


def _row_tile(s, target):
    if s <= target:
        return s
    return max(t for t in range(16, target + 1, 16) if s % t == 0)


def _rw(fn, rows, fulls, row_out, acc_out, name, tm=512, side=None):
    cols = [r[1:] if isinstance(r, tuple) else (r.shape[1], 0) for r in rows]
    rows = [r[0] if isinstance(r, tuple) else r for r in rows]
    s = rows[0].shape[0]
    tm = _row_tile(s, tm)
    nr, nf, nro, nao = len(rows), len(fulls), len(row_out), len(acc_out)

    def body(*refs):
        r = refs[:nr]
        f = refs[nr:nr + nf]
        ro = refs[nr + nf:nr + nf + nro]
        ao = refs[nr + nf + nro:]
        outs, accs = fn(*[x[...] for x in r], *[x[...] for x in f])
        for o, v in zip(ro, outs):
            o[...] = v.astype(o.dtype)
        if nao:
            @pl.when(pl.program_id(0) == 0)
            def _():
                for a in ao:
                    a[...] = jnp.zeros_like(a)

            for a, v in zip(ao, accs):
                a[...] += v

    full_spec = lambda shape: pl.BlockSpec(shape, lambda i: (0,) * len(shape))
    s_in, s_out, s_shape, s_scratch, s_ops = _side_args(side)
    res = pl.pallas_call(
        _hosted(body, side, nr + nf, nro + nao, (s // tm,)), name=name, grid=(s // tm,),
        in_specs=[pl.BlockSpec((tm, wd), functools.partial(lambda i, cb: (i, cb), cb=cb)) for wd, cb in cols]
        + [full_spec(x.shape) for x in fulls] + s_in,
        out_specs=[pl.BlockSpec((tm, d), lambda i: (i, 0)) for d, _ in row_out]
        + [full_spec(shape) for shape in acc_out] + s_out,
        out_shape=[jax.ShapeDtypeStruct((s, d), dt) for d, dt in row_out]
        + [jax.ShapeDtypeStruct(shape, F32) for shape in acc_out] + s_shape,
        scratch_shapes=s_scratch,
        compiler_params=_params(("arbitrary",)),
    )(*rows, *fulls, *s_ops)
    res = list(res)
    return res if side is None else (res[:nro + nao], res[nro + nao:])


def _norm_fwd(x, g, name, side=None):
    res = _rw(lambda xt, gt: ((_rms(xt, gt),), ()), [x], [g], [(x.shape[1], BF16)], [], name, side=side)
    return res[0] if side is None else (res[0][0], res[1])


def _norm_bwd(x, g, dh, dres, name, side=None):
    def fn(xt, dht, drt, gt):
        _, vjp = jax.vjp(_rms, xt, gt)
        dx, dg = vjp(dht)
        return (dx + drt,), (dg,)

    return _rw(fn, [x, dh, dres], [g], [(x.shape[1], F32)], [g.shape], name, side=side)


def _rms_groups(x, g, scale):
    lo = lax.broadcasted_iota(jnp.int32, (1, LANES), 1) < SB_HEAD_DIM
    x2 = x * x
    outs = []
    for cb in range(x.shape[1] // LANES):
        sl = slice(cb * LANES, (cb + 1) * LANES)
        s_lo = jnp.sum(jnp.where(lo, x2[:, sl], 0.0), axis=-1, keepdims=True)
        s_hi = jnp.sum(jnp.where(lo, 0.0, x2[:, sl]), axis=-1, keepdims=True)
        r = jnp.where(lo, lax.rsqrt(s_lo * (1.0 / SB_HEAD_DIM) + NORM_EPS),
                      lax.rsqrt(s_hi * (1.0 / SB_HEAD_DIM) + NORM_EPS))
        outs.append(x[:, sl] * r)
    return jnp.concatenate(outs, axis=1) * g * scale


def _log_sigmoid(z):
    return jnp.minimum(z, 0.0) - jnp.log(1.0 + jnp.exp(-jnp.abs(z)))


def _split_dot(x, u2):
    hi = x.astype(BF16)
    lo = (x - hi.astype(F32)).astype(BF16)
    return jnp.dot(jnp.concatenate([hi, lo], axis=1), u2, preferred_element_type=F32)


def _sb_consts(b):
    row = lax.broadcasted_iota(jnp.int32, (b, b), 0)
    col = lax.broadcasted_iota(jnp.int32, (b, b), 1)
    tri = col < row
    u_after = (row > col).astype(BF16)
    u_from = (row >= col).astype(BF16)
    stack = lambda u: jnp.concatenate([u, u], axis=0)
    lane_lo = lax.broadcasted_iota(jnp.int32, (b, LANES), 1) < SB_HEAD_DIM
    return tri, stack(u_after), stack(u_from), lane_lo


def _sb_scores(qh, kb, a_run, keep, u2_after):
    z = lax.dot_general(qh, kb, NT, preferred_element_type=F32)
    lb = _log_sigmoid(z)
    l = lb - z
    if keep is not None:
        l = jnp.where(keep, l, 0.0)
    w = jnp.exp(lb + (a_run + _split_dot(l, u2_after)))
    if keep is not None:
        w = jnp.where(keep, w, 0.0)
    return lb, l, w


def _sb_walk(qi, carry, step):
    def cond(state):
        n, c = state
        return jnp.logical_and(n <= qi, jnp.max(jnp.maximum(c[0], c[1])) > SB_UNDERFLOW)

    def body(state):
        n, c = state
        return n + 1, step(n, c)

    return lax.while_loop(cond, body, (jnp.int32(2), carry))[1]


def _two_heads(x, lane_lo):
    zero = jnp.zeros_like(x)
    return jnp.where(lane_lo, x, zero), jnp.where(lane_lo, zero, x)


def _sb_fwd(qs, ks, v, name, v_col=0, side=None):
    s, width = qs.shape
    b = min(SB_BLOCK, s)

    def body(q_ref, k_ref, v_ref, o_ref):
        qi = pl.program_id(1)
        tri, u2_after, _, lane_lo = _sb_consts(b)
        q_a, q_b = _two_heads(q_ref[...], lane_lo)

        def step(n, carry, keep):
            a_a, a_b, acc = carry
            off = pl.multiple_of(jnp.maximum(qi - n, 0) * b, b)
            kb = k_ref[pl.ds(off, b), :]
            v_a, v_b = _two_heads(v_ref[pl.ds(off, b), :].astype(BF16), lane_lo)
            _, l_a, w_a = _sb_scores(q_a, kb, a_a, keep, u2_after)
            _, l_b, w_b = _sb_scores(q_b, kb, a_b, keep, u2_after)
            acc = acc + jnp.dot(jnp.concatenate([w_a.astype(BF16), w_b.astype(BF16)], axis=1),
                                jnp.concatenate([v_a, v_b], axis=0), preferred_element_type=F32)
            return (a_a + jnp.sum(l_a, axis=1, keepdims=True), a_b + jnp.sum(l_b, axis=1, keepdims=True), acc)

        zero = jnp.zeros((b, 1), F32)
        carry = step(0, (zero, zero, jnp.zeros((b, LANES), F32)), tri)
        carry = step(1, carry, jnp.broadcast_to(qi > 0, tri.shape))
        carry = _sb_walk(qi, carry, lambda n, c: step(n, c, None))
        o_ref[...] = carry[2]

    blk = pl.BlockSpec((b, LANES), lambda hp, i: (i, hp))
    full = pl.BlockSpec((s, LANES), lambda hp, i: (0, hp))
    full_v = pl.BlockSpec((s, LANES), lambda hp, i: (0, hp + v_col))
    grid = (width // LANES, s // b)
    s_in, s_out, s_shape, s_scratch, s_ops = _side_args(side)
    res = pl.pallas_call(
        _hosted(body, side, 3, 1, grid), name=name, grid=grid,
        in_specs=[blk, full, full_v] + s_in, out_specs=[blk] + s_out,
        out_shape=[jax.ShapeDtypeStruct((s, width), F32)] + s_shape, scratch_shapes=s_scratch,
        compiler_params=_params(("arbitrary", "arbitrary")),
    )(qs, ks, v, *s_ops)
    return _split_side(res, 1, side)


def _sb_bwd(qs, ks, v, out, dout, name, v_col=0, side=None):
    s, width = qs.shape
    b = min(SB_BLOCK, s)
    nkb = s // b

    def body(q_ref, k_ref, v_ref, o_ref, do_ref, dq_ref, dk_ref, dv_ref):
        qi = pl.program_id(1)

        @pl.when(qi == 0)
        def _():
            dk_ref[...] = jnp.zeros_like(dk_ref)
            dv_ref[...] = jnp.zeros_like(dv_ref)

        tri, u2_after, u2_from, lane_lo = _sb_consts(b)
        q_a, q_b = _two_heads(q_ref[...], lane_lo)
        dob = do_ref[...].astype(BF16)
        do_a, do_b = _two_heads(dob, lane_lo)
        prod = dob.astype(F32) * o_ref[...]
        d_a = jnp.sum(jnp.where(lane_lo, prod, 0.0), axis=1, keepdims=True)
        d_b = jnp.sum(jnp.where(lane_lo, 0.0, prod), axis=1, keepdims=True)
        q_rows = jnp.concatenate([q_a, q_b], axis=0)
        do_rows = jnp.concatenate([do_a, do_b], axis=0)

        def head(qh, doh, kb, vb, a_run, d_rem, keep):
            lb, l, w = _sb_scores(qh, kb, a_run, keep, u2_after)
            wb = w.astype(BF16)
            g = lax.dot_general(doh, vb, NT, preferred_element_type=F32) * wb.astype(F32)
            g_before = d_rem - _split_dot(g, u2_from)
            dz = g - (g + g_before) * jnp.exp(lb)
            if keep is not None:
                dz = jnp.where(keep, dz, 0.0)
            return (dz.astype(BF16), wb, a_run + jnp.sum(l, axis=1, keepdims=True),
                    d_rem - jnp.sum(g, axis=1, keepdims=True))

        def step(n, carry, keep):
            a_a, a_b, r_a, r_b, dq = carry
            jb = jnp.maximum(qi - n, 0)
            off = pl.multiple_of(jb * b, b)
            kb = k_ref[pl.ds(off, b), :]
            vb = v_ref[pl.ds(off, b), :].astype(BF16)
            k_a, k_b = _two_heads(kb, lane_lo)
            dz_a, w_a, a_a, r_a = head(q_a, do_a, kb, vb, a_a, r_a, keep)
            dz_b, w_b, a_b, r_b = head(q_b, do_b, kb, vb, a_b, r_b, keep)
            dq = dq + jnp.dot(jnp.concatenate([dz_a, dz_b], axis=1), jnp.concatenate([k_a, k_b], axis=0),
                              preferred_element_type=F32)
            dk_ref[pl.ds(off, b), :] += lax.dot_general(jnp.concatenate([dz_a, dz_b], axis=0), q_rows, TN,
                                                        preferred_element_type=F32)
            dv_ref[pl.ds(off, b), :] += lax.dot_general(jnp.concatenate([w_a, w_b], axis=0), do_rows, TN,
                                                        preferred_element_type=F32)
            return a_a, a_b, r_a, r_b, dq

        zero = jnp.zeros((b, 1), F32)
        carry = step(0, (zero, zero, d_a, d_b, jnp.zeros((b, LANES), F32)), tri)
        carry = step(1, carry, jnp.broadcast_to(qi > 0, tri.shape))
        carry = _sb_walk(qi, carry, lambda n, c: step(n, c, None))
        dq_ref[...] = carry[4]

    blk = pl.BlockSpec((b, LANES), lambda hp, i: (i, hp))
    full = pl.BlockSpec((s, LANES), lambda hp, i: (0, hp))
    full_v = pl.BlockSpec((s, LANES), lambda hp, i: (0, hp + v_col))
    grid = (width // LANES, nkb)
    s_in, s_out, s_shape, s_scratch, s_ops = _side_args(side)
    res = pl.pallas_call(
        _hosted(body, side, 5, 3, grid), name=name, grid=grid,
        in_specs=[blk, full, full_v, blk, blk] + s_in, out_specs=[blk, full, full] + s_out,
        out_shape=[jax.ShapeDtypeStruct((s, width), F32)] * 3 + s_shape,
        scratch_shapes=s_scratch,
        compiler_params=_params(("arbitrary", "arbitrary")),
    )(qs, ks, v, out, dout, *s_ops)
    return _split_side(res, 3, side)


def _cmul(xr, xi, yr, yi):
    return xr * yr - xi * yi, xr * yi + xi * yr


def _scan_consts(ar, ai, reverse, lc):
    rowi = lax.broadcasted_iota(jnp.int32, (SUBLANES, lc), 0)
    pows = [(ar, ai)]
    for _ in range(SUBLANES - 1):
        pows.append(_cmul(*pows[-1], ar, ai))
    steps = []
    for d in (1, 2, 4):
        keep = (rowi < SUBLANES - d) if reverse else (rowi >= d)
        pr, pi = pows[d - 1]
        steps.append((SUBLANES - d if reverse else d, jnp.where(keep, pr, 0.0), jnp.where(keep, pi, 0.0)))
    cr = jnp.zeros((SUBLANES, lc), F32)
    ci = jnp.zeros((SUBLANES, lc), F32)
    for r in range(SUBLANES):
        pr, pi = pows[SUBLANES - 1 - r] if reverse else pows[r]
        cr = jnp.where(rowi == r, pr, cr)
        ci = jnp.where(rowi == r, pi, ci)
    return steps, cr, ci


def _scan_tile(xr, xi, steps, pr, pi, cr, ci):
    for shift, ar, ai in steps:
        rr = pltpu.roll(xr, shift, 0)
        ri = pltpu.roll(xi, shift, 0)
        xr, xi = xr + ar * rr - ai * ri, xi + ar * ri + ai * rr
    return xr + pr * cr - pi * ci, xi + pr * ci + pi * cr


SCAN_ROWS = 1024


def _scan_chunk(s):
    tt = min(SCAN_ROWS, s)
    seg = tt // SUBLANES
    assert s % tt == 0 and seg % SUBLANES == 0 and seg & (seg - 1) == 0, s
    return tt, seg


def _to_segments(a):
    s, wd = a.shape
    tt, seg = _scan_chunk(s)
    return jnp.transpose(a.reshape(s // tt, SUBLANES, seg, wd), (0, 2, 1, 3)).reshape(s, wd)


def _from_segments(a):
    s, wd = a.shape
    tt, seg = _scan_chunk(s)
    return jnp.transpose(a.reshape(s // tt, seg, SUBLANES, wd), (0, 2, 1, 3)).reshape(s, wd)


def _cpow2(xr, xi, k):
    for _ in range(k):
        xr, xi = _cmul(xr, xi, xr, xi)
    return xr, xi


def _fill_powers(pw_ref, ar, ai, seg, lc):
    _, p8r, p8i = _scan_consts(ar, ai, False, lc)
    a8r, a8i = _cpow2(ar, ai, 3)
    qr, qi = jnp.ones_like(ar), jnp.zeros_like(ai)
    for k in range(seg // SUBLANES):
        tr, ti = _cmul(p8r, p8i, qr, qi)
        for r in range(SUBLANES):
            rows = pl.ds((SUBLANES * k + r) * SUBLANES, SUBLANES)
            pw_ref[rows, :lc] = jnp.broadcast_to(tr[r:r + 1, :], (SUBLANES, lc))
            pw_ref[rows, lc:] = jnp.broadcast_to(ti[r:r + 1, :], (SUBLANES, lc))
        qr, qi = _cmul(qr, qi, a8r, a8i)


def _ssm_fwd(u, acat, bsup, csup, d_skip, name, side=None):
    s = u.shape[0]
    lc = SCAN_LANES
    tt, seg = _scan_chunk(s)
    nl, nt = N_STATE // lc, s // tt
    tile = lambda j: pl.ds(pl.multiple_of(j * SUBLANES, SUBLANES), SUBLANES)

    def body(u_ref, a_ref, b_ref, c_ref, d_ref, s_ref, y0_ref, y1_ref, carry, pw_ref):
        ar, ai = a_ref[:, :lc], a_ref[:, lc:]

        @pl.when(pl.program_id(1) == 0)
        def _():
            carry[...] = jnp.zeros_like(carry)
            _fill_powers(pw_ref, ar, ai, seg, lc)

        ut = u_ref[...]
        s_ref[...] = _dot(ut, b_ref[0])

        ar8, ai8 = jnp.broadcast_to(ar, (SUBLANES, lc)), jnp.broadcast_to(ai, (SUBLANES, lc))

        def local(j, x):
            xr = ar8 * x[0] - ai8 * x[1] + s_ref[tile(j), :lc]
            xi = ar8 * x[1] + ai8 * x[0] + s_ref[tile(j), lc:]
            s_ref[tile(j), :lc] = xr
            s_ref[tile(j), lc:] = xi
            return xr, xi

        zero = jnp.zeros((SUBLANES, lc), F32)
        er, ei = lax.fori_loop(0, seg, local, (zero, zero))
        steps, pr, pi = _scan_consts(*_cpow2(ar, ai, seg.bit_length() - 1), False, lc)
        cr, ci = carry[:, :lc], carry[:, lc:]
        tr, ti = _scan_tile(er, ei, steps, pr, pi, cr, ci)
        rowi = lax.broadcasted_iota(jnp.int32, (SUBLANES, lc), 0)
        before_r = jnp.where(rowi == 0, cr, pltpu.roll(tr, 1, 0))
        before_i = jnp.where(rowi == 0, ci, pltpu.roll(ti, 1, 0))
        carry[:, :lc] = jnp.broadcast_to(tr[SUBLANES - 1:, :], (SUBLANES, lc))
        carry[:, lc:] = jnp.broadcast_to(ti[SUBLANES - 1:, :], (SUBLANES, lc))

        def fix(j, _):
            pwr, pwi = pw_ref[tile(j), :lc], pw_ref[tile(j), lc:]
            s_ref[tile(j), :lc] += pwr * before_r - pwi * before_i
            s_ref[tile(j), lc:] += pwr * before_i + pwi * before_r
            return 0

        lax.fori_loop(0, seg, fix, 0)
        y0 = _dot(s_ref[...], c_ref[0], NT) + d_ref[...] * ut
        y0_ref[...] = y0
        y1_ref[...] = jax.nn.gelu(y0)

    chan = pl.BlockSpec((tt, LANES), lambda j, c: (c, j))
    sup = pl.BlockSpec((1, LANES, 2 * lc), lambda j, c: (j, 0, 0))
    s_in, s_out, s_shape, s_scratch, s_ops = _side_args(side)
    res = pl.pallas_call(
        _hosted(body, side, 5, 3, (nl, nt)), name=name, grid=(nl, nt),
        in_specs=[chan, pl.BlockSpec((1, 2 * lc), lambda j, c: (0, j)), sup, sup,
                  pl.BlockSpec((1, LANES), lambda j, c: (0, j))] + s_in,
        out_specs=[pl.BlockSpec((tt, 2 * lc), lambda j, c: (c, j)), chan, chan] + s_out,
        out_shape=[jax.ShapeDtypeStruct((s, 2 * N_STATE), F32), jax.ShapeDtypeStruct((s, SSM_WIDTH), F32),
                   jax.ShapeDtypeStruct((s, SSM_WIDTH), F32)] + s_shape,
        scratch_shapes=[pltpu.VMEM((SUBLANES, 2 * lc), F32), pltpu.VMEM((seg * SUBLANES, 2 * lc), F32)] + s_scratch,
        compiler_params=_params(("arbitrary", "arbitrary")),
    )(u, acat, bsup, csup, d_skip, *s_ops)
    return _split_side(res, 3, side)


def _ssm_bwd(dy0, states, u, acat, bsup, csup, d_skip, name, side=None):
    s = u.shape[0]
    lc = SCAN_LANES
    tt, seg = _scan_chunk(s)
    nl, nt = N_STATE // lc, s // tt
    tile = lambda j: pl.ds(pl.multiple_of(j * SUBLANES, SUBLANES), SUBLANES)

    def body(dy_ref, s_ref, sp_ref, u_ref, a_ref, b_ref, c_ref, d_ref,
             du_ref, da_ref, db_ref, dc_ref, dd_ref, lam_ref, carry, pw_ref):
        c = pl.program_id(1)
        ar, ai = a_ref[:, :lc], a_ref[:, lc:]

        @pl.when(c == 0)
        def _():
            carry[...] = jnp.zeros_like(carry)
            for r in (da_ref, db_ref, dc_ref, dd_ref):
                r[...] = jnp.zeros_like(r)
            _fill_powers(pw_ref, ar, ai, seg, lc)

        dy = dy_ref[...]
        ut = u_ref[...]
        lam_ref[...] = _dot(dy, c_ref[0])

        ar8, ai8 = jnp.broadcast_to(ar, (SUBLANES, lc)), jnp.broadcast_to(ai, (SUBLANES, lc))

        def local(i, x):
            j = seg - 1 - i
            xr = ar8 * x[0] + ai8 * x[1] + lam_ref[tile(j), :lc]
            xi = ar8 * x[1] - ai8 * x[0] + lam_ref[tile(j), lc:]
            lam_ref[tile(j), :lc] = xr
            lam_ref[tile(j), lc:] = xi
            return xr, xi

        zero = jnp.zeros((SUBLANES, lc), F32)
        er, ei = lax.fori_loop(0, seg, local, (zero, zero))
        big_r, big_i = _cpow2(ar, ai, seg.bit_length() - 1)
        steps, pr, pi = _scan_consts(big_r, -big_i, True, lc)
        cr, ci = carry[:, :lc], carry[:, lc:]
        tr, ti = _scan_tile(er, ei, steps, pr, pi, cr, ci)
        rowi = lax.broadcasted_iota(jnp.int32, (SUBLANES, lc), 0)
        after_r = jnp.where(rowi == SUBLANES - 1, cr, pltpu.roll(tr, SUBLANES - 1, 0))
        after_i = jnp.where(rowi == SUBLANES - 1, ci, pltpu.roll(ti, SUBLANES - 1, 0))
        carry[:, :lc] = jnp.broadcast_to(tr[:1, :], (SUBLANES, lc))
        carry[:, lc:] = jnp.broadcast_to(ti[:1, :], (SUBLANES, lc))

        start = c != nt - 1
        last_r = jnp.where(start, jnp.broadcast_to(sp_ref[SUBLANES - 1:, :lc], (SUBLANES, lc)), 0.0)
        last_i = jnp.where(start, jnp.broadcast_to(sp_ref[SUBLANES - 1:, lc:], (SUBLANES, lc)), 0.0)
        first_r = jnp.where(rowi == 0, last_r, pltpu.roll(s_ref[tile(seg - 1), :lc], 1, 0))
        first_i = jnp.where(rowi == 0, last_i, pltpu.roll(s_ref[tile(seg - 1), lc:], 1, 0))

        def fix(j, acc):
            dar, dai = acc
            k = seg - 1 - j
            pwr, pwi = pw_ref[tile(k), :lc], pw_ref[tile(k), lc:]
            lr = lam_ref[tile(j), :lc] + pwr * after_r + pwi * after_i
            li = lam_ref[tile(j), lc:] + pwr * after_i - pwi * after_r
            lam_ref[tile(j), :lc] = lr
            lam_ref[tile(j), lc:] = li
            jp = jnp.maximum(j - 1, 0)
            sr = jnp.where(j > 0, s_ref[tile(jp), :lc], first_r)
            si = jnp.where(j > 0, s_ref[tile(jp), lc:], first_i)
            return dar + lr * sr + li * si, dai + li * sr - lr * si

        dar, dai = lax.fori_loop(0, seg, fix, (zero, zero))
        da_ref[:, :lc] += dar
        da_ref[:, lc:] += dai
        lam = lam_ref[...].astype(BF16)
        du_ref[...] = (_dot(lam, b_ref[0], NT) + d_ref[...] * dy).astype(du_ref.dtype)
        db_ref[0] += _dot(ut, lam, TN)
        dc_ref[0] += _dot(dy, s_ref[...], TN)
        dd_ref[...] += jnp.sum(dy * ut, axis=0, keepdims=True)

    rev = lambda j, c: (nt - 1 - c, j)
    chan = pl.BlockSpec((tt, LANES), rev)
    sup = pl.BlockSpec((1, LANES, 2 * lc), lambda j, c: (j, 0, 0))
    row = pl.BlockSpec((1, LANES), lambda j, c: (0, j))
    s_in, s_out, s_shape, s_scratch, s_ops = _side_args(side)
    res = pl.pallas_call(
        _hosted(body, side, 8, 5, (nl, nt)), name=name, grid=(nl, nt),
        in_specs=[chan, pl.BlockSpec((tt, 2 * lc), rev),
                  pl.BlockSpec((SUBLANES, 2 * lc), lambda j, c: (jnp.maximum((nt - 1 - c) * seg - 1, 0), j)),
                  chan, pl.BlockSpec((1, 2 * lc), lambda j, c: (0, j)), sup, sup, row] + s_in,
        out_specs=[chan, pl.BlockSpec((SUBLANES, 2 * lc), lambda j, c: (0, j)), sup, sup, row] + s_out,
        out_shape=[jax.ShapeDtypeStruct((s, SSM_WIDTH), BF16), jax.ShapeDtypeStruct((SUBLANES, 2 * N_STATE), F32),
                   jax.ShapeDtypeStruct(bsup.shape, F32), jax.ShapeDtypeStruct(csup.shape, F32),
                   jax.ShapeDtypeStruct((1, SSM_WIDTH), F32)] + s_shape,
        scratch_shapes=[pltpu.VMEM((tt, 2 * lc), F32), pltpu.VMEM((SUBLANES, 2 * lc), F32),
                        pltpu.VMEM((seg * SUBLANES, 2 * lc), F32)] + s_scratch,
        compiler_params=_params(("arbitrary", "arbitrary")),
    )(dy0, states, states, u, acat, bsup, csup, d_skip, *s_ops)
    return _split_side(res, 5, side)


def _state_cols(xr, xi):
    lead = xr.shape[:-1]
    nl = N_STATE // SCAN_LANES
    both = jnp.stack([xr.reshape(lead + (nl, SCAN_LANES)), xi.reshape(lead + (nl, SCAN_LANES))], axis=-2)
    return both.reshape(lead + (2 * N_STATE,))


def _ssm_mats(a_re, a_im, log_dt, b_re, b_im, c_re, c_im):
    dt = jnp.exp(log_dt)[:, None]
    lr, li = a_re * dt, a_im * dt
    e = jnp.exp(lr)
    abar_r, abar_i = e * jnp.cos(li), e * jnp.sin(li)
    den = a_re * a_re + a_im * a_im
    coef_r = ((abar_r - 1.0) * a_re + abar_i * a_im) / den
    coef_i = (abar_i * a_re - (abar_r - 1.0) * a_im) / den
    bbar_r = coef_r[..., None] * b_re - coef_i[..., None] * b_im
    bbar_i = coef_r[..., None] * b_im + coef_i[..., None] * b_re
    nl = N_STATE // SCAN_LANES
    gpb = SSM_GROUPS // nl
    eye = jnp.eye(gpb, dtype=bool)[None, :, None, :, None]

    def sup(m_r, m_i):
        def one(m):
            m = m.reshape(nl, gpb, SSM_GROUP, 1, SSM_STATE)
            return jnp.where(eye, m, 0.0).reshape(nl, gpb * SSM_GROUP, SCAN_LANES)
        return jnp.concatenate([one(m_r), one(m_i)], axis=-1)

    acat = _state_cols(abar_r.reshape(1, N_STATE), abar_i.reshape(1, N_STATE))
    bsup = sup(jnp.transpose(bbar_r, (0, 2, 1)), jnp.transpose(bbar_i, (0, 2, 1)))
    csup = sup(c_re, -c_im)
    return acat, bsup, csup


def _mem_fwd(mem, g_mem, w_kv, g_k, name):
    ml = mem.shape[0]

    def body(mem_ref, gm_ref, w_ref, gk_ref, memn_ref, kv_ref, kn_ref, vv_ref):
        memn = _rms(mem_ref[...], gm_ref[...])
        memn_ref[...] = memn.astype(BF16)
        kv = _dot(memn, w_ref[...])
        kv_ref[...] = kv
        for hh in range(XA_HEADS):
            sl = slice(hh * XA_HEAD_DIM, (hh + 1) * XA_HEAD_DIM)
            kn_ref[:, sl] = _rms(kv[:, sl], gk_ref[...]).astype(BF16)
        vv_ref[...] = kv[:, XA_WIDTH:].astype(BF16)

    return pl.pallas_call(
        body, name=name,
        out_shape=[jax.ShapeDtypeStruct((ml, D_MODEL), BF16), jax.ShapeDtypeStruct((ml, 2 * XA_WIDTH), F32),
                   jax.ShapeDtypeStruct((ml, XA_WIDTH), BF16), jax.ShapeDtypeStruct((ml, XA_WIDTH), BF16)],
        compiler_params=_params(),
    )(mem, g_mem, w_kv, g_k)


def _mem_bwd(mem, g_mem, memn, w_kv, kv, g_k, dkn, dvv, name):
    def body(mem_ref, gm_ref, memn_ref, w_ref, kv_ref, gk_ref, dkn_ref, dvv_ref, dw_ref, dgm_ref, dgk_ref):
        kv = kv_ref[...]
        dgk = jnp.zeros(dgk_ref.shape, F32)
        parts = []
        for hh in range(XA_HEADS):
            sl = slice(hh * XA_HEAD_DIM, (hh + 1) * XA_HEAD_DIM)
            _, vjp = jax.vjp(_rms, kv[:, sl], gk_ref[...])
            dk, dg = vjp(dkn_ref[:, sl])
            parts.append(dk)
            dgk = dgk + dg
        dgk_ref[...] = dgk
        dkv = jnp.concatenate(parts + [dvv_ref[...]], axis=1)
        dw_ref[...] = _dot(memn_ref[...], dkv, TN)
        dmemn = _dot(dkv, w_ref[...], NT)
        _, vjp = jax.vjp(_rms, mem_ref[...], gm_ref[...])
        dgm_ref[...] = vjp(dmemn)[1]

    return pl.pallas_call(
        body, name=name,
        out_shape=[jax.ShapeDtypeStruct((D_MODEL, 2 * XA_WIDTH), F32), jax.ShapeDtypeStruct(g_mem.shape, F32),
                   jax.ShapeDtypeStruct(g_k.shape, F32)],
        compiler_params=_params(),
    )(mem, g_mem, memn, w_kv, kv, g_k, dkn, dvv)


def _xa_head(qx_h, g_q, kn_h, vv_h):
    qn = _rms(qx_h, g_q)
    sc = _dot(qn, kn_h, NT) * (XA_HEAD_DIM ** -0.5)
    sc = sc - jnp.max(sc, axis=-1, keepdims=True)
    e = jnp.exp(sc)
    p = e / jnp.sum(e, axis=-1, keepdims=True)
    return qn, p


def _xa_fwd(qx, g_q, kn, vv, name):
    def fn(qt, gq, knt, vvt):
        outs = []
        for hh in range(XA_HEADS):
            sl = slice(hh * XA_HEAD_DIM, (hh + 1) * XA_HEAD_DIM)
            _, p = _xa_head(qt[:, sl], gq, knt[:, sl], vvt[:, sl])
            outs.append(_dot(p, vvt[:, sl]))
        return (jnp.concatenate(outs, axis=1),), ()

    return _rw(fn, [qx], [g_q, kn, vv], [(XA_WIDTH, BF16)], [], name)[0]


def _xa_bwd(qx, g_q, kn, vv, do, name):
    def fn(qt, dot_, gq, knt, vvt):
        dqs, dks, dvs = [], [], []
        dgq = jnp.zeros_like(gq)
        for hh in range(XA_HEADS):
            sl = slice(hh * XA_HEAD_DIM, (hh + 1) * XA_HEAD_DIM)
            qn, p = _xa_head(qt[:, sl], gq, knt[:, sl], vvt[:, sl])
            doh = dot_[:, sl]
            dp = _dot(doh, vvt[:, sl], NT)
            dvs.append(_dot(p, doh, TN))
            ds = p * (dp - jnp.sum(dp * p, axis=-1, keepdims=True)) * (XA_HEAD_DIM ** -0.5)
            dqn = _dot(ds, knt[:, sl])
            dks.append(_dot(ds, qn, TN))
            _, vjp = jax.vjp(_rms, qt[:, sl], gq)
            dq, dg = vjp(dqn)
            dqs.append(dq)
            dgq = dgq + dg
        return ((jnp.concatenate(dqs, axis=1),),
                (jnp.concatenate(dks, axis=1), jnp.concatenate(dvs, axis=1), dgq))

    return _rw(fn, [qx, do], [g_q, kn, vv], [(XA_WIDTH, BF16)], [kn.shape, vv.shape, g_q.shape], name)


BIG = [
    ("w_in", (D_MODEL, IN_WIDTH), 1), ("ssm_w_glu", (SSM_WIDTH, SSM_WIDTH), 0), ("w_out", (D_MODEL, D_MODEL), 0),
    ("xa_w_q", (D_MODEL, XA_WIDTH), 0), ("xa_w_kv", (D_MODEL, 2 * XA_WIDTH), 0), ("xa_w_o", (XA_WIDTH, D_MODEL), 1),
    ("w_up", (D_MODEL, D_FF), 1), ("w_down", (D_FF, D_MODEL), 0),
]
BIG_INDEX = {n: i for i, (n, _, _) in enumerate(BIG)}


def _shard_shape(shape, axis):
    return tuple(d // N_DEV if i == axis else d for i, d in enumerate(shape))


def _shard_of(ref, axis, d):
    n = ref.shape[axis] // N_DEV
    return ref.at[pl.ds(d * n, n), :] if axis == 0 else ref.at[:, pl.ds(d * n, n)]


def _gather_side(names, shards):
    idxs = [BIG_INDEX[n] for n in names]

    def make(ins, outs, send_sems, recv_sems):
        x, y, c = lax.axis_index("x"), lax.axis_index("y"), lax.axis_index("c")
        cps = []
        for j, i in enumerate(idxs):
            mine = _shard_of(outs[j], BIG[i][2], 4 * x + 2 * y + c)
            cps.append(pltpu.make_async_copy(ins[j], mine, send_sems.at[N_DEV * j]))
            for rel in range(1, N_DEV):
                to = tuple(1 - p if rel >> bit & 1 else p for p, bit in ((x, 2), (y, 1), (c, 0)))
                cps.append(pltpu.make_async_remote_copy(
                    src_ref=ins[j], dst_ref=mine, send_sem=send_sems.at[N_DEV * j + rel],
                    recv_sem=recv_sems.at[N_DEV * j + rel], device_id=to, device_id_type=MESH))
        return cps

    return _Side(shards, [jax.ShapeDtypeStruct(BIG[i][1], BF16) for i in idxs], N_DEV * len(idxs), make)


def _sibling_side(names, grads):
    idxs = [BIG_INDEX[n] for n in names]

    def make(ins, outs, send_sems, recv_sems):
        x, y, c = lax.axis_index("x"), lax.axis_index("y"), lax.axis_index("c")
        return [pltpu.make_async_remote_copy(
            src_ref=_shard_of(ins[j], BIG[i][2], 2 * k + (1 - c)), dst_ref=outs[j].at[k],
            send_sem=send_sems.at[4 * j + k], recv_sem=recv_sems.at[4 * j + k], device_id=(x, y, 1 - c),
            device_id_type=MESH) for j, i in enumerate(idxs) for k in range(4)]

    shapes = [jax.ShapeDtypeStruct((4,) + _shard_shape(BIG[i][1], BIG[i][2]), F32) for i in idxs]
    return _Side(grads, shapes, 4 * len(idxs), make)


def _chips_side(parts):
    def make(ins, outs, send_sems, recv_sems):
        x, y, c = lax.axis_index("x"), lax.axis_index("y"), lax.axis_index("c")
        chips = [(1 - x, y), (x, 1 - y), (1 - x, 1 - y)]
        return [pltpu.make_async_remote_copy(
            src_ref=ins[j].at[2 * cx + cy], dst_ref=outs[j].at[r], send_sem=send_sems.at[3 * j + r],
            recv_sem=recv_sems.at[3 * j + r], device_id=(cx, cy, c), device_id_type=MESH)
            for r, (cx, cy) in enumerate(chips) for j in range(len(parts))]

    return _Side(parts, [jax.ShapeDtypeStruct((3,) + p.shape[1:], p.dtype) for p in parts], 3 * len(parts), make)


def _reduce_add(grad, recv, axis, core, name):
    rs, cs = recv.shape[1:]
    rt = _row_tile(rs, 256)
    nt = rs // rt

    def body(c_ref, g_ref, r_ref, p_ref, pb_ref):
        sm = g_ref[...] + r_ref[0]
        p_ref[0] = sm
        pb_ref[0] = sm.astype(BF16)

    if axis == 0:
        g_spec = pl.BlockSpec((rt, cs), lambda k, t, c_ref: ((2 * k + c_ref[0]) * nt + t, 0))
    else:
        g_spec = pl.BlockSpec((rt, cs), lambda k, t, c_ref: (t, 2 * k + c_ref[0]))
    slab = pl.BlockSpec((1, rt, cs), lambda k, t, c_ref: (k, t, 0))
    return pl.pallas_call(
        body, name=name,
        grid_spec=pltpu.PrefetchScalarGridSpec(num_scalar_prefetch=1, grid=(4, nt), in_specs=[g_spec, slab],
                                               out_specs=[slab, slab]),
        out_shape=[jax.ShapeDtypeStruct(recv.shape, F32), jax.ShapeDtypeStruct(recv.shape, BF16)],
        compiler_params=_params(("parallel", "parallel")),
    )(core, grad, recv)


def _all_gather(block, name, side):
    m_per, n = block.shape
    ns_in, ns_out = len(side.ins), len(side.out_shapes)

    def body(*refs):
        x_ref, s_ins, out_ref = refs[0], refs[1:1 + ns_in], refs[1 + ns_in]
        s_outs = refs[2 + ns_in:2 + ns_in + ns_out]
        send_sems, recv_sems, local_sem, s_send, s_recv = refs[2 + ns_in + ns_out:]
        others = side.make(s_ins, s_outs, s_send, s_recv)
        for cp in others:
            cp.start()
        x, y, c = lax.axis_index("x"), lax.axis_index("y"), lax.axis_index("c")
        me, sibling = (x, y, c), (x, y, 1 - c)
        chips = [(1 - x, y), (x, 1 - y), (1 - x, 1 - y)]

        def rows(px, py, pc):
            return out_ref.at[pl.ds((4 * px + 2 * py + pc) * m_per, m_per), :]

        def copy(k, blk, to, src=None):
            return pltpu.make_async_remote_copy(
                src_ref=rows(*blk) if src is None else src, dst_ref=rows(*blk),
                send_sem=send_sems.at[k], recv_sem=recv_sems.at[k], device_id=to, device_id_type=MESH)

        mine = pltpu.make_async_copy(x_ref, rows(*me), local_sem)
        mine.start()
        first = [copy(0, me, sibling, src=x_ref)]
        first += [copy(1 + j, me, (*chip, c), src=x_ref) for j, chip in enumerate(chips)]
        for cp in first:
            cp.start()
        passed = [copy(4 + j, (*chip, c), sibling) for j, chip in enumerate(chips)]
        for j, chip in enumerate(chips):
            copy(1 + j, (*chip, c), me).wait_recv()
            passed[j].start()
        copy(0, sibling, me).wait_recv()
        for j, chip in enumerate(chips):
            copy(4 + j, (*chip, 1 - c), me).wait_recv()
        for cp in first + passed:
            cp.wait_send()
        mine.wait()
        for cp in others:
            cp.wait()

    res = pl.pallas_call(
        body, name=name, in_specs=[ANY] * (1 + ns_in), out_specs=[ANY] * (1 + ns_out),
        out_shape=[jax.ShapeDtypeStruct((N_DEV * m_per, n), block.dtype)] + side.out_shapes,
        scratch_shapes=[pltpu.SemaphoreType.DMA((7,)), pltpu.SemaphoreType.DMA((7,)), pltpu.SemaphoreType.DMA]
        + side.sems(),
    )(block, *side.ins)
    return res[0], list(res[1:])


def _adam_math(w, g, m, v):
    m = ADAM_B1 * m + (1.0 - ADAM_B1) * g
    v = ADAM_B2 * v + (1.0 - ADAM_B2) * (g * g)
    m_hat = m / (1.0 - ADAM_B1 ** ADAM_STEP)
    v_hat = v / (1.0 - ADAM_B2 ** ADAM_STEP)
    delta = -ADAM_LR * (m_hat / (jnp.sqrt(v_hat) + ADAM_EPS) + ADAM_WD * w)
    return delta, m, v


def _adam_sharded(own, recv, w, m, v, chip, name):
    rs, cs = w.shape
    rt = _row_tile(rs, 256)

    def body(chip_ref, p_ref, r_ref, w_ref, m_ref, v_ref, g_out, d_out, m_out, v_out):
        g = p_ref[0] + r_ref[0].astype(F32) + r_ref[1].astype(F32) + r_ref[2].astype(F32)
        d, mn, vn = _adam_math(w_ref[...], g, m_ref[...], v_ref[...])
        g_out[...] = g
        d_out[...] = d
        m_out[...] = mn
        v_out[...] = vn

    tile = pl.BlockSpec((rt, cs), lambda t, chip_ref: (t, 0))
    return pl.pallas_call(
        body, name=name,
        grid_spec=pltpu.PrefetchScalarGridSpec(
            num_scalar_prefetch=1, grid=(rs // rt,),
            in_specs=[pl.BlockSpec((1, rt, cs), lambda t, chip_ref: (chip_ref[0], t, 0)),
                      pl.BlockSpec((3, rt, cs), lambda t, chip_ref: (0, t, 0)), tile, tile, tile],
            out_specs=[tile] * 4),
        out_shape=[jax.ShapeDtypeStruct((rs, cs), F32)] * 4,
        compiler_params=_params(("parallel",)),
    )(chip, own, recv, w, m, v)


SMALL = ["g_mix", "ssm_a_re", "ssm_a_im", "ssm_log_dt", "ssm_b_re", "ssm_b_im", "ssm_c_re", "ssm_c_im", "ssm_d",
         "sb_g_q", "sb_g_k", "g_out_ssm", "g_out_sb", "g_xa", "g_mem", "xa_g_q", "xa_g_k", "g_mlp"]
PACK_TILE = SUBLANES * LANES


def _natural_2d(n):
    return (n // LANES, LANES) if n % LANES == 0 else (1, n)


def _pack_small(arrs):
    parts = []
    for a in arrs:
        flat = a.reshape(-1)
        parts.append(jnp.pad(flat, (0, (-flat.shape[0]) % PACK_TILE)))
    return jnp.concatenate(parts).reshape(-1, LANES)


def _adam_replicated(gathered, sizes, ws, ms, vs, name):
    n_w = len(ws)
    r_dev = gathered.shape[0] // N_DEV
    offs, off = [], 0
    for n in sizes:
        offs.append(off)
        off += (n + PACK_TILE - 1) // PACK_TILE * SUBLANES
    assert off == r_dev

    def body(*refs):
        g_ref = refs[0]
        w_refs, m_refs, v_refs = refs[1:1 + n_w], refs[1 + n_w:1 + 2 * n_w], refs[1 + 2 * n_w:1 + 3 * n_w]
        outs = refs[1 + 3 * n_w:]

        def total(i, shape):
            r, cdim = shape
            acc = g_ref[pl.ds(offs[i], r), :cdim]
            for d in range(1, N_DEV):
                acc = acc + g_ref[pl.ds(d * r_dev + offs[i], r), :cdim]
            return acc

        for i in range(n_w):
            g = total(i, w_refs[i].shape)
            d, mn, vn = _adam_math(w_refs[i][...], g, m_refs[i][...], v_refs[i][...])
            for o, val in zip(outs[4 * i:4 * i + 4], (g, d, mn, vn)):
                o[...] = val
        outs[4 * n_w][...] = total(n_w, (SUBLANES, LANES))

    shapes = [w.shape for w in ws]
    return pl.pallas_call(
        body, name=name,
        out_shape=[jax.ShapeDtypeStruct(shp, F32) for shp in shapes for _ in range(4)]
        + [jax.ShapeDtypeStruct((SUBLANES, LANES), F32)],
        compiler_params=_params(),
    )(gathered, *ws, *ms, *vs)


def _step(x, mem, target, shards, sm, core):
    g, w, sums, reduced = {}, {}, {}, {}

    def gather(names):
        return _gather_side(names, [shards[n] for n in names])

    def to_sibling(names):
        return _sibling_side(names, [g[n] for n in names])

    def add_sibling(names, received):
        for n, r in zip(names, received):
            sums[n] = _reduce_add(g[n], r, BIG[BIG_INDEX[n]][2], core, "reduce_add_" + n)

    def to_chips(names):
        return _chips_side([sums[n][1] for n in names])

    def keep(names, received):
        for n, r in zip(names, received):
            reduced[n] = (sums[n][0], r)

    row = lambda a: a.reshape(1, -1)
    g_mix, g_xa, g_mlp, g_mem = row(sm["g_mix"]), row(sm["g_xa"]), row(sm["g_mlp"]), row(sm["g_mem"])
    g_os, g_ob = row(sm["g_out_ssm"]), row(sm["g_out_sb"])
    sb_gq, sb_gk = jnp.tile(row(sm["sb_g_q"]), (1, SB_HEADS)), jnp.tile(row(sm["sb_g_k"]), (1, SB_HEADS))
    xa_gq, xa_gk = row(sm["xa_g_q"]), row(sm["xa_g_k"])
    d_skip = row(sm["ssm_d"])

    h1, (w["w_in"],) = _norm_fwd(x, g_mix, "norm_mix", side=gather(["w_in"]))
    proj = _mm(h1, w["w_in"], "nn", "in_proj")
    u = _to_segments(proj[:, :SSM_WIDTH])
    q_raw, k_raw = (proj, SB_WIDTH, 1), (proj, SB_WIDTH, 2)
    v_col = (SSM_WIDTH + 2 * SB_WIDTH) // LANES
    sb_scale = SB_HEAD_DIM ** -0.5
    qk_norm = lambda scale: (lambda xt, gt: ((_rms_groups(xt, gt, scale),), ()))
    qs = _rw(qk_norm(sb_scale), [q_raw], [sb_gq], [(SB_WIDTH, BF16)], [], "sb_qnorm")[0]
    ks = _rw(qk_norm(1.0), [k_raw], [sb_gk], [(SB_WIDTH, BF16)], [], "sb_knorm")[0]
    early = ["ssm_w_glu", "w_out", "xa_w_q", "xa_w_kv", "xa_w_o", "w_up"]
    y_sb, got = _sb_fwd(qs, ks, proj, "sb_fwd", v_col=v_col, side=gather(early))
    w.update(zip(early, got))

    ssm_args = (sm["ssm_a_re"], sm["ssm_a_im"], sm["ssm_log_dt"], sm["ssm_b_re"], sm["ssm_b_im"],
                sm["ssm_c_re"], sm["ssm_c_im"])
    (acat, bsup, csup), mats_vjp = jax.vjp(_ssm_mats, *ssm_args)
    (states, y0, y1), (w["w_down"],) = _ssm_fwd(u, acat, bsup, csup, d_skip, "ssm_fwd", side=gather(["w_down"]))
    z_glu, y_ssm = _mm(y1, w["ssm_w_glu"], "nn", "ssm_glu", epi=lambda r, yt: (r, yt * jax.nn.sigmoid(r)),
                       extras=(y1,), out_dtypes=(F32, F32))
    y_ssm = _from_segments(y_ssm)

    def cat_norm(a, b, ga, gb):
        return jnp.concatenate([_rms(a, ga), _rms(b, gb)], axis=1)

    ycat = _rw(lambda a, b, ga, gb: ((cat_norm(a, b, ga, gb),), ()), [y_ssm, y_sb], [g_os, g_ob],
               [(D_MODEL, BF16)], [], "norm_out")[0]
    def residual_norm_epi(r, xt, gt):
        xn = r + xt
        return xn, _rms(xn, gt)

    x1, h2 = _mm(ycat, w["w_out"], "nn", "out_proj", epi=residual_norm_epi, extras=(x,), fulls=(g_xa,),
                 out_dtypes=(F32, BF16))
    qx = _mm(h2, w["xa_w_q"], "nn", "xa_q")
    memn, kv, kn_x, vv_x = _mem_fwd(mem, g_mem, w["xa_w_kv"], xa_gk, "xa_mem")
    o_xa = _xa_fwd(qx, xa_gq, kn_x, vv_x, "xa_fwd")
    x2, h3 = _mm(o_xa, w["xa_w_o"], "nn", "xa_o", epi=residual_norm_epi, extras=(x1,), fulls=(g_mlp,),
                 out_dtypes=(F32, BF16))

    def up_epi(r):
        rl = jnp.maximum(r, 0.0)
        return (rl * rl,)

    r_up = _mm(h3, w["w_up"], "nn", "mlp_up", epi=up_epi, out_dtypes=(BF16,))

    def loss_epi(r, xt, tt):
        d = r + xt - tt
        return (d * (1.0 / D_MODEL),) * 2, (jnp.sum(d * d, axis=0, keepdims=True),)

    dx3, dx3_b, sq = _mm(r_up, w["w_down"], "nn", "mlp_down", epi=loss_epi, extras=(x2, target),
                         out_dtypes=(F32, BF16), sums=[(1, D_MODEL)])
    loss = jnp.sum(sq) * (0.5 / D_MODEL)

    def norm_bwd_epi(r, xt, drt, gt):
        _, vjp = jax.vjp(_rms, xt, gt)
        dx_, dg_ = vjp(r)
        return (dx_ + drt,) * 2, (dg_,)

    g["w_down"] = _mm(r_up, dx3_b, "tn", "d_w_down", tk=2048)
    da = _mm(dx3_b, w["w_down"], "nt", "d_r", epi=lambda r, rt: (r * 2.0 * jnp.sqrt(rt.astype(F32)),), extras=(r_up,),
             out_dtypes=(BF16,))
    g["w_up"] = _mm(h3, da, "tn", "d_w_up", tk=2048)
    mlp = ["w_down", "w_up"]
    (dx2, dx2_b, g["g_mlp"]), got = _mm(da, w["w_up"], "nt", "d_h3", epi=norm_bwd_epi, extras=(x2, dx3),
                                        fulls=(g_mlp,), out_dtypes=(F32, BF16), sums=[g_mlp.shape],
                                        side=to_sibling(mlp))
    add_sibling(mlp, got)
    g["xa_w_o"] = _mm(o_xa, dx2_b, "tn", "d_xa_w_o", tk=2048)
    do_xa = _mm(dx2_b, w["xa_w_o"], "nt", "d_o_xa")
    dqx, dkn_x, dvv_x, g["xa_g_q"] = _xa_bwd(qx, xa_gq, kn_x, vv_x, do_xa, "xa_bwd")
    g["xa_w_kv"], g["g_mem"], g["xa_g_k"] = _mem_bwd(mem, g_mem, memn, w["xa_w_kv"], kv, xa_gk, dkn_x, dvv_x,
                                                     "xa_mem_bwd")
    g["xa_w_q"] = _mm(h2, dqx, "tn", "d_xa_w_q", tk=2048)
    dx1, dx1_b, g["g_xa"] = _mm(dqx, w["xa_w_q"], "nt", "d_h2", epi=norm_bwd_epi, extras=(x1, dx2), fulls=(g_xa,),
                                out_dtypes=(F32, BF16), sums=[g_xa.shape])
    g["w_out"] = _mm(ycat, dx1_b, "tn", "d_w_out", tk=2048)
    dycat = _mm(dx1_b, w["w_out"], "nt", "d_ycat")

    def cat_bwd(a, b, dy, ga, gb):
        _, vjp = jax.vjp(cat_norm, a, b, ga, gb)
        da_, db_, dga, dgb = vjp(dy)
        return (da_, db_), (dga, dgb)

    dy_ssm, dy_sb, g["g_out_ssm"], g["g_out_sb"] = _rw(
        cat_bwd, [y_ssm, y_sb, dycat], [g_os, g_ob], [(SSM_WIDTH, F32), (SB_WIDTH, F32)], [g_os.shape, g_ob.shape],
        "d_norm_out")

    def glu_bwd(dy, yt, zt):
        sg = jax.nn.sigmoid(zt)
        return (dy * sg, dy * yt * sg * (1.0 - sg)), ()

    dy1_a, dz = _rw(glu_bwd, [_to_segments(dy_ssm), y1, z_glu], [], [(SSM_WIDTH, F32), (SSM_WIDTH, BF16)], [], "d_glu")
    g["ssm_w_glu"] = _mm(y1, dz, "tn", "d_w_glu", tk=2048)

    def gelu_bwd_epi(r, da_, y0t):
        _, vjp = jax.vjp(jax.nn.gelu, y0t)
        return (vjp(r + da_)[0],)

    mid = ["w_out", "xa_w_q", "xa_w_kv", "xa_w_o", "ssm_w_glu"]
    dy0, got = _mm(dz, w["ssm_w_glu"], "nt", "d_y1", epi=gelu_bwd_epi, extras=(dy1_a, y0), side=to_sibling(mid))
    add_sibling(mid, got)
    (du, da8, d_bsup, d_csup, g["ssm_d"]), got = _ssm_bwd(dy0, states, u, acat, bsup, csup, d_skip, "ssm_bwd",
                                                          side=to_chips(mlp))
    keep(mlp, got)
    d_acat = jnp.sum(da8, axis=0, keepdims=True)
    for nm, val in zip(("ssm_a_re", "ssm_a_im", "ssm_log_dt", "ssm_b_re", "ssm_b_im", "ssm_c_re", "ssm_c_im"),
                       mats_vjp((d_acat, d_bsup, d_csup))):
        g[nm] = val

    (dqs, dks, dvs), got = _sb_bwd(qs, ks, proj, y_sb, dy_sb, "sb_bwd", v_col=v_col, side=to_chips(mid))
    keep(mid, got)

    def qk_norm_bwd(scale):
        def fn(xt, dt, gt):
            _, vjp = jax.vjp(lambda a, b_: _rms_groups(a, b_, scale), xt, gt)
            dx_, dg_ = vjp(dt)
            return (dx_,), (dg_,)
        return fn

    dq_raw, dgq = _rw(qk_norm_bwd(sb_scale), [q_raw, dqs], [sb_gq], [(SB_WIDTH, BF16)], [sb_gq.shape], "d_sb_qnorm")
    dk_raw, dgk = _rw(qk_norm_bwd(1.0), [k_raw, dks], [sb_gk], [(SB_WIDTH, BF16)], [sb_gk.shape],
                      "d_sb_knorm")
    g["sb_g_q"] = jnp.sum(dgq.reshape(SB_HEADS, SB_HEAD_DIM), axis=0)
    g["sb_g_k"] = jnp.sum(dgk.reshape(SB_HEADS, SB_HEAD_DIM), axis=0)
    dproj = jnp.concatenate([_from_segments(du), dq_raw, dk_raw, dvs.astype(BF16)], axis=1)
    g["w_in"] = _mm(h1, dproj, "tn", "d_w_in", tk=2048)
    dh1, got = _mm(dproj, w["w_in"], "nt", "d_h1", side=to_sibling(["w_in"]))
    add_sibling(["w_in"], got)
    dx, g["g_mix"] = _norm_bwd(x, g_mix, dh1, dx1, "d_norm_mix")

    packed = _pack_small([g[n] for n in SMALL] + [loss.reshape(1)])
    everyone, got = _all_gather(packed, "gather_small", to_chips(["w_in"]))
    keep(["w_in"], got)
    return dx, everyone, reduced


def kernel(x, mem, g_mix, w_in, ssm_a_re, ssm_a_im, ssm_log_dt, ssm_b_re, ssm_b_im, ssm_c_re, ssm_c_im, ssm_d, ssm_w_glu, sb_g_q, sb_g_k, g_out_ssm, g_out_sb, w_out, g_xa, g_mem, xa_w_q, xa_w_kv, xa_g_q, xa_g_k, xa_w_o, g_mlp, w_up, w_down, loss_target, m_g_mix, m_w_in, m_ssm_a_re, m_ssm_a_im, m_ssm_log_dt, m_ssm_b_re, m_ssm_b_im, m_ssm_c_re, m_ssm_c_im, m_ssm_d, m_ssm_w_glu, m_sb_g_q, m_sb_g_k, m_g_out_ssm, m_g_out_sb, m_w_out, m_g_xa, m_g_mem, m_xa_w_q, m_xa_w_kv, m_xa_g_q, m_xa_g_k, m_xa_w_o, m_g_mlp, m_w_up, m_w_down, v_g_mix, v_w_in, v_ssm_a_re, v_ssm_a_im, v_ssm_log_dt, v_ssm_b_re, v_ssm_b_im, v_ssm_c_re, v_ssm_c_im, v_ssm_d, v_ssm_w_glu, v_sb_g_q, v_sb_g_k, v_g_out_ssm, v_g_out_sb, v_w_out, v_g_xa, v_g_mem, v_xa_w_q, v_xa_w_kv, v_xa_g_q, v_xa_g_k, v_xa_w_o, v_g_mlp, v_w_up, v_w_down):
    given = dict(locals())
    order = ["g_mix", "w_in", "ssm_a_re", "ssm_a_im", "ssm_log_dt", "ssm_b_re", "ssm_b_im", "ssm_c_re", "ssm_c_im",
             "ssm_d", "ssm_w_glu", "sb_g_q", "sb_g_k", "g_out_ssm", "g_out_sb", "w_out", "g_xa", "g_mem", "xa_w_q",
             "xa_w_kv", "xa_g_q", "xa_g_k", "xa_w_o", "g_mlp", "w_up", "w_down"]
    assert sorted([n for n, _, _ in BIG] + SMALL) == sorted(order)
    core = lax.axis_index("c").astype(jnp.int32).reshape(1)
    chip = (2 * lax.axis_index("x") + lax.axis_index("y")).astype(jnp.int32).reshape(1)

    shards = {n: given[n][0].astype(BF16) for n, _, _ in BIG}
    sm = {n: given[n][0] for n in SMALL}
    dx, everyone, reduced = _step(x[0], mem[0], loss_target[0], shards, sm, core)

    res = {}
    for n, _, _ in BIG:
        own, recv = reduced[n]
        outs = _adam_sharded(own, recv, given[n][0], given["m_" + n][0], given["v_" + n][0], chip, "adam_" + n)
        for kind, val in zip(("grad", "delta", "new_m", "new_v"), outs):
            res[kind + "_" + n] = val[None]

    sizes = [math.prod(sm[n].shape) for n in SMALL] + [1]
    nat = lambda a: a.reshape(_natural_2d(math.prod(a.shape)))
    outs = _adam_replicated(everyone, sizes, [nat(sm[n]) for n in SMALL], [nat(given["m_" + n][0]) for n in SMALL],
                            [nat(given["v_" + n][0]) for n in SMALL], "adam_replicated")
    for i, n in enumerate(SMALL):
        for kind, val in zip(("grad", "delta", "new_m", "new_v"), outs[4 * i:4 * i + 4]):
            res[kind + "_" + n] = val.reshape(given[n].shape)
    loss_out = outs[-1][0, 0]
    return (loss_out, dx[None], *[res["grad_" + n] for n in order], *[res["delta_" + n] for n in order],
            *[res["new_m_" + n] for n in order], *[res["new_v_" + n] for n in order])
```

```python
import functools
import math

import jax
import jax.numpy as jnp
from jax import lax
from jax.experimental import pallas as pl
from jax.experimental.pallas import tpu as pltpu

F32 = jnp.float32
BF16 = jnp.bfloat16
MESH = pl.DeviceIdType.MESH

N_DEV = 8
D_MODEL = 1024
SSM_WIDTH = 512
SSM_GROUP = 16
SSM_GROUPS = 32
SSM_STATE = 64
N_STATE = SSM_GROUPS * SSM_STATE
SB_HEADS = 8
SB_HEAD_DIM = 64
SB_WIDTH = 512
IN_WIDTH = 2048
XA_HEADS = 4
XA_HEAD_DIM = 128
XA_WIDTH = 512
D_FF = 4096
NORM_EPS = 1e-6
ADAM_LR = 0.001
ADAM_B1 = 0.9
ADAM_B2 = 0.999
ADAM_EPS = 1e-08
ADAM_WD = 0.01
ADAM_STEP = 10

LANES = 128
SUBLANES = 8
VMEM_LIMIT = 56 * 1024 * 1024
SCAN_LANES = 512
SB_BLOCK = 256
SB_UNDERFLOW = -110.0

NN = (((1,), (0,)), ((), ()))
NT = (((1,), (1,)), ((), ()))
TN = (((0,), (0,)), ((), ()))


def _params(sem=None):
    return pltpu.CompilerParams(dimension_semantics=sem, vmem_limit_bytes=VMEM_LIMIT)


def _dot(a, b, dims=NN):
    return lax.dot_general(a.astype(BF16), b.astype(BF16), dims, preferred_element_type=F32)


def _rms(x, g):
    return x * lax.rsqrt(jnp.mean(x * x, axis=-1, keepdims=True) + NORM_EPS) * g


ANY = pl.BlockSpec(memory_space=pl.ANY)


class _Side:
    def __init__(self, ins, out_shapes, n_sem, make):
        self.ins, self.out_shapes, self.n_sem, self.make = list(ins), list(out_shapes), n_sem, make

    def sems(self):
        return [pltpu.SemaphoreType.DMA((self.n_sem,)), pltpu.SemaphoreType.DMA((self.n_sem,))]


def _hosted(body, side, n_in, n_out, grid):
    if side is None:
        return body
    ns_in, ns_out = len(side.ins), len(side.out_shapes)

    def wrapped(*refs):
        ins, refs = refs[:n_in], refs[n_in:]
        s_ins, refs = refs[:ns_in], refs[ns_in:]
        outs, refs = refs[:n_out], refs[n_out:]
        s_outs, refs = refs[:ns_out], refs[ns_out:]
        scratch, sems = refs[:-2], refs[-2:]
        ids = [pl.program_id(d) for d in range(len(grid))]
        first = functools.reduce(jnp.logical_and, [i == 0 for i in ids])
        last = functools.reduce(jnp.logical_and, [i == n - 1 for i, n in zip(ids, grid)])

        @pl.when(first)
        def _():
            for cp in side.make(s_ins, s_outs, *sems):
                cp.start()

        body(*ins, *outs, *scratch)

        @pl.when(last)
        def _():
            for cp in side.make(s_ins, s_outs, *sems):
                cp.wait()

    return wrapped


def _side_args(side):
    if side is None:
        return [], [], [], [], []
    return ([ANY] * len(side.ins), [ANY] * len(side.out_shapes), side.out_shapes, side.sems(), side.ins)


def _split_side(res, n_out, side):
    res = list(res)
    main = res[0] if n_out == 1 else res[:n_out]
    return main if side is None else (main, res[n_out:])


def _mm(a, b, mode, name, *, epi=None, extras=(), fulls=(), out_dtypes=(F32,), sums=(), tm=1024, tn=1024, tk=1024,
        side=None):
    if mode == "nn":
        (m, k), (k2, n) = a.shape, b.shape
    elif mode == "nt":
        (m, k), (n, k2) = a.shape, b.shape
    else:
        (k, m), (k2, n) = a.shape, b.shape
    assert k == k2, (name, a.shape, b.shape)
    tm, tn, tk = min(tm, m), min(tn, n), min(tk, k)
    assert m % tm == 0 and n % tn == 0 and k % tk == 0, (name, m, n, k)
    nk = k // tk
    dims = {"nn": NN, "nt": NT, "tn": TN}[mode]
    if mode == "tn":
        a_spec = pl.BlockSpec((tk, tm), lambda i, j, kk: (kk, i))
    else:
        a_spec = pl.BlockSpec((tm, tk), lambda i, j, kk: (i, kk))
    if mode == "nt":
        b_spec = pl.BlockSpec((tn, tk), lambda i, j, kk: (j, kk))
    else:
        b_spec = pl.BlockSpec((tk, tn), lambda i, j, kk: (kk, j))
    mn_spec = pl.BlockSpec((tm, tn), lambda i, j, kk: (i, j))
    n_ex, n_full, n_out, n_sum = len(extras), len(fulls), len(out_dtypes), len(sums)
    n_in = 2 + n_ex + n_full

    def body(*refs):
        a_ref, b_ref = refs[:2]
        ex = refs[2:n_in]
        outs = refs[n_in:n_in + n_out]
        sum_refs = refs[n_in + n_out:n_in + n_out + n_sum]
        kk = pl.program_id(2)
        first_tile = jnp.logical_and(pl.program_id(0) == 0, pl.program_id(1) == 0)

        def finish(r):
            vals = epi(r, *[e[...] for e in ex]) if epi is not None else (r,)
            if n_sum:
                vals, parts = vals

                @pl.when(first_tile)
                def _():
                    for sr in sum_refs:
                        sr[...] = jnp.zeros_like(sr)

                for sr, p in zip(sum_refs, parts):
                    sr[...] += p
            for o, v in zip(outs, vals):
                o[...] = v.astype(o.dtype)

        if nk == 1:
            finish(_dot(a_ref[...], b_ref[...], dims))
        else:
            acc = refs[n_in + n_out + n_sum]

            @pl.when(kk == 0)
            def _():
                acc[...] = jnp.zeros_like(acc)

            acc[...] += _dot(a_ref[...], b_ref[...], dims)

            @pl.when(kk == nk - 1)
            def _():
                finish(acc[...])

    grid = (m // tm, n // tn, nk)
    whole = lambda shape: pl.BlockSpec(shape, lambda i, j, kk: (0,) * len(shape))
    s_in, s_out, s_shape, s_scratch, s_ops = _side_args(side)
    seq = bool(side) or n_sum > 0
    res = pl.pallas_call(
        _hosted(body, side, n_in, n_out + n_sum, grid), name=name, grid=grid,
        in_specs=[a_spec, b_spec] + [mn_spec] * n_ex + [whole(f.shape) for f in fulls] + s_in,
        out_specs=[mn_spec] * n_out + [whole(shape) for shape in sums] + s_out,
        out_shape=[jax.ShapeDtypeStruct((m, n), dt) for dt in out_dtypes]
        + [jax.ShapeDtypeStruct(shape, F32) for shape in sums] + s_shape,
        scratch_shapes=([pltpu.VMEM((tm, tn), F32)] if nk > 1 else []) + s_scratch,
        compiler_params=_params(("arbitrary",) * 3 if seq else ("parallel", "parallel", "arbitrary")),
    )(a, b, *extras, *fulls, *s_ops)
    return _split_side(res, n_out + n_sum, side)


def _row_tile(s, target):
    if s <= target:
        return s
    return max(t for t in range(16, target + 1, 16) if s % t == 0)


def _rw(fn, rows, fulls, row_out, acc_out, name, tm=512, side=None):
    cols = [r[1:] if isinstance(r, tuple) else (r.shape[1], 0) for r in rows]
    rows = [r[0] if isinstance(r, tuple) else r for r in rows]
    s = rows[0].shape[0]
    tm = _row_tile(s, tm)
    nr, nf, nro, nao = len(rows), len(fulls), len(row_out), len(acc_out)

    def body(*refs):
        r = refs[:nr]
        f = refs[nr:nr + nf]
        ro = refs[nr + nf:nr + nf + nro]
        ao = refs[nr + nf + nro:]
        outs, accs = fn(*[x[...] for x in r], *[x[...] for x in f])
        for o, v in zip(ro, outs):
            o[...] = v.astype(o.dtype)
        if nao:
            @pl.when(pl.program_id(0) == 0)
            def _():
                for a in ao:
                    a[...] = jnp.zeros_like(a)

            for a, v in zip(ao, accs):
                a[...] += v

    full_spec = lambda shape: pl.BlockSpec(shape, lambda i: (0,) * len(shape))
    s_in, s_out, s_shape, s_scratch, s_ops = _side_args(side)
    res = pl.pallas_call(
        _hosted(body, side, nr + nf, nro + nao, (s // tm,)), name=name, grid=(s // tm,),
        in_specs=[pl.BlockSpec((tm, wd), functools.partial(lambda i, cb: (i, cb), cb=cb)) for wd, cb in cols]
        + [full_spec(x.shape) for x in fulls] + s_in,
        out_specs=[pl.BlockSpec((tm, d), lambda i: (i, 0)) for d, _ in row_out]
        + [full_spec(shape) for shape in acc_out] + s_out,
        out_shape=[jax.ShapeDtypeStruct((s, d), dt) for d, dt in row_out]
        + [jax.ShapeDtypeStruct(shape, F32) for shape in acc_out] + s_shape,
        scratch_shapes=s_scratch,
        compiler_params=_params(("arbitrary",)),
    )(*rows, *fulls, *s_ops)
    res = list(res)
    return res if side is None else (res[:nro + nao], res[nro + nao:])


def _norm_fwd(x, g, name, side=None):
    res = _rw(lambda xt, gt: ((_rms(xt, gt),), ()), [x], [g], [(x.shape[1], BF16)], [], name, side=side)
    return res[0] if side is None else (res[0][0], res[1])


def _norm_bwd(x, g, dh, dres, name, side=None):
    def fn(xt, dht, drt, gt):
        _, vjp = jax.vjp(_rms, xt, gt)
        dx, dg = vjp(dht)
        return (dx + drt,), (dg,)

    return _rw(fn, [x, dh, dres], [g], [(x.shape[1], F32)], [g.shape], name, side=side)


def _rms_groups(x, g, scale):
    lo = lax.broadcasted_iota(jnp.int32, (1, LANES), 1) < SB_HEAD_DIM
    x2 = x * x
    outs = []
    for cb in range(x.shape[1] // LANES):
        sl = slice(cb * LANES, (cb + 1) * LANES)
        s_lo = jnp.sum(jnp.where(lo, x2[:, sl], 0.0), axis=-1, keepdims=True)
        s_hi = jnp.sum(jnp.where(lo, 0.0, x2[:, sl]), axis=-1, keepdims=True)
        r = jnp.where(lo, lax.rsqrt(s_lo * (1.0 / SB_HEAD_DIM) + NORM_EPS),
                      lax.rsqrt(s_hi * (1.0 / SB_HEAD_DIM) + NORM_EPS))
        outs.append(x[:, sl] * r)
    return jnp.concatenate(outs, axis=1) * g * scale


def _log_sigmoid(z):
    return jnp.minimum(z, 0.0) - jnp.log(1.0 + jnp.exp(-jnp.abs(z)))


def _split_dot(x, u2):
    hi = x.astype(BF16)
    lo = (x - hi.astype(F32)).astype(BF16)
    return jnp.dot(jnp.concatenate([hi, lo], axis=1), u2, preferred_element_type=F32)


def _sb_consts(b):
    row = lax.broadcasted_iota(jnp.int32, (b, b), 0)
    col = lax.broadcasted_iota(jnp.int32, (b, b), 1)
    tri = col < row
    u_after = (row > col).astype(BF16)
    u_from = (row >= col).astype(BF16)
    stack = lambda u: jnp.concatenate([u, u], axis=0)
    lane_lo = lax.broadcasted_iota(jnp.int32, (b, LANES), 1) < SB_HEAD_DIM
    return tri, stack(u_after), stack(u_from), lane_lo


def _sb_scores(qh, kb, a_run, keep, u2_after):
    z = lax.dot_general(qh, kb, NT, preferred_element_type=F32)
    lb = _log_sigmoid(z)
    l = lb - z
    if keep is not None:
        l = jnp.where(keep, l, 0.0)
    w = jnp.exp(lb + (a_run + _split_dot(l, u2_after)))
    if keep is not None:
        w = jnp.where(keep, w, 0.0)
    return lb, l, w


def _sb_walk(qi, carry, step):
    def cond(state):
        n, c = state
        return jnp.logical_and(n <= qi, jnp.max(jnp.maximum(c[0], c[1])) > SB_UNDERFLOW)

    def body(state):
        n, c = state
        return n + 1, step(n, c)

    return lax.while_loop(cond, body, (jnp.int32(2), carry))[1]


def _two_heads(x, lane_lo):
    zero = jnp.zeros_like(x)
    return jnp.where(lane_lo, x, zero), jnp.where(lane_lo, zero, x)


def _sb_fwd(qs, ks, v, name, v_col=0, side=None):
    s, width = qs.shape
    b = min(SB_BLOCK, s)

    def body(q_ref, k_ref, v_ref, o_ref):
        qi = pl.program_id(1)
        tri, u2_after, _, lane_lo = _sb_consts(b)
        q_a, q_b = _two_heads(q_ref[...], lane_lo)

        def step(n, carry, keep):
            a_a, a_b, acc = carry
            off = pl.multiple_of(jnp.maximum(qi - n, 0) * b, b)
            kb = k_ref[pl.ds(off, b), :]
            v_a, v_b = _two_heads(v_ref[pl.ds(off, b), :].astype(BF16), lane_lo)
            _, l_a, w_a = _sb_scores(q_a, kb, a_a, keep, u2_after)
            _, l_b, w_b = _sb_scores(q_b, kb, a_b, keep, u2_after)
            acc = acc + jnp.dot(jnp.concatenate([w_a.astype(BF16), w_b.astype(BF16)], axis=1),
                                jnp.concatenate([v_a, v_b], axis=0), preferred_element_type=F32)
            return (a_a + jnp.sum(l_a, axis=1, keepdims=True), a_b + jnp.sum(l_b, axis=1, keepdims=True), acc)

        zero = jnp.zeros((b, 1), F32)
        carry = step(0, (zero, zero, jnp.zeros((b, LANES), F32)), tri)
        carry = step(1, carry, jnp.broadcast_to(qi > 0, tri.shape))
        carry = _sb_walk(qi, carry, lambda n, c: step(n, c, None))
        o_ref[...] = carry[2]

    blk = pl.BlockSpec((b, LANES), lambda hp, i: (i, hp))
    full = pl.BlockSpec((s, LANES), lambda hp, i: (0, hp))
    full_v = pl.BlockSpec((s, LANES), lambda hp, i: (0, hp + v_col))
    grid = (width // LANES, s // b)
    s_in, s_out, s_shape, s_scratch, s_ops = _side_args(side)
    res = pl.pallas_call(
        _hosted(body, side, 3, 1, grid), name=name, grid=grid,
        in_specs=[blk, full, full_v] + s_in, out_specs=[blk] + s_out,
        out_shape=[jax.ShapeDtypeStruct((s, width), F32)] + s_shape, scratch_shapes=s_scratch,
        compiler_params=_params(("arbitrary", "arbitrary")),
    )(qs, ks, v, *s_ops)
    return _split_side(res, 1, side)


def _sb_bwd(qs, ks, v, out, dout, name, v_col=0, side=None):
    s, width = qs.shape
    b = min(SB_BLOCK, s)
    nkb = s // b

    def body(q_ref, k_ref, v_ref, o_ref, do_ref, dq_ref, dk_ref, dv_ref):
        qi = pl.program_id(1)

        @pl.when(qi == 0)
        def _():
            dk_ref[...] = jnp.zeros_like(dk_ref)
            dv_ref[...] = jnp.zeros_like(dv_ref)

        tri, u2_after, u2_from, lane_lo = _sb_consts(b)
        q_a, q_b = _two_heads(q_ref[...], lane_lo)
        dob = do_ref[...].astype(BF16)
        do_a, do_b = _two_heads(dob, lane_lo)
        prod = dob.astype(F32) * o_ref[...]
        d_a = jnp.sum(jnp.where(lane_lo, prod, 0.0), axis=1, keepdims=True)
        d_b = jnp.sum(jnp.where(lane_lo, 0.0, prod), axis=1, keepdims=True)
        q_rows = jnp.concatenate([q_a, q_b], axis=0)
        do_rows = jnp.concatenate([do_a, do_b], axis=0)

        def head(qh, doh, kb, vb, a_run, d_rem, keep):
            lb, l, w = _sb_scores(qh, kb, a_run, keep, u2_after)
            wb = w.astype(BF16)
            g = lax.dot_general(doh, vb, NT, preferred_element_type=F32) * wb.astype(F32)
            g_before = d_rem - _split_dot(g, u2_from)
            dz = g - (g + g_before) * jnp.exp(lb)
            if keep is not None:
                dz = jnp.where(keep, dz, 0.0)
            return (dz.astype(BF16), wb, a_run + jnp.sum(l, axis=1, keepdims=True),
                    d_rem - jnp.sum(g, axis=1, keepdims=True))

        def step(n, carry, keep):
            a_a, a_b, r_a, r_b, dq = carry
            jb = jnp.maximum(qi - n, 0)
            off = pl.multiple_of(jb * b, b)
            kb = k_ref[pl.ds(off, b), :]
            vb = v_ref[pl.ds(off, b), :].astype(BF16)
            k_a, k_b = _two_heads(kb, lane_lo)
            dz_a, w_a, a_a, r_a = head(q_a, do_a, kb, vb, a_a, r_a, keep)
            dz_b, w_b, a_b, r_b = head(q_b, do_b, kb, vb, a_b, r_b, keep)
            dq = dq + jnp.dot(jnp.concatenate([dz_a, dz_b], axis=1), jnp.concatenate([k_a, k_b], axis=0),
                              preferred_element_type=F32)
            dk_ref[pl.ds(off, b), :] += lax.dot_general(jnp.concatenate([dz_a, dz_b], axis=0), q_rows, TN,
                                                        preferred_element_type=F32)
            dv_ref[pl.ds(off, b), :] += lax.dot_general(jnp.concatenate([w_a, w_b], axis=0), do_rows, TN,
                                                        preferred_element_type=F32)
            return a_a, a_b, r_a, r_b, dq

        zero = jnp.zeros((b, 1), F32)
        carry = step(0, (zero, zero, d_a, d_b, jnp.zeros((b, LANES), F32)), tri)
        carry = step(1, carry, jnp.broadcast_to(qi > 0, tri.shape))
        carry = _sb_walk(qi, carry, lambda n, c: step(n, c, None))
        dq_ref[...] = carry[4]

    blk = pl.BlockSpec((b, LANES), lambda hp, i: (i, hp))
    full = pl.BlockSpec((s, LANES), lambda hp, i: (0, hp))
    full_v = pl.BlockSpec((s, LANES), lambda hp, i: (0, hp + v_col))
    grid = (width // LANES, nkb)
    s_in, s_out, s_shape, s_scratch, s_ops = _side_args(side)
    res = pl.pallas_call(
        _hosted(body, side, 5, 3, grid), name=name, grid=grid,
        in_specs=[blk, full, full_v, blk, blk] + s_in, out_specs=[blk, full, full] + s_out,
        out_shape=[jax.ShapeDtypeStruct((s, width), F32)] * 3 + s_shape,
        scratch_shapes=s_scratch,
        compiler_params=_params(("arbitrary", "arbitrary")),
    )(qs, ks, v, out, dout, *s_ops)
    return _split_side(res, 3, side)


def _cmul(xr, xi, yr, yi):
    return xr * yr - xi * yi, xr * yi + xi * yr


def _scan_consts(ar, ai, reverse, lc):
    rowi = lax.broadcasted_iota(jnp.int32, (SUBLANES, lc), 0)
    pows = [(ar, ai)]
    for _ in range(SUBLANES - 1):
        pows.append(_cmul(*pows[-1], ar, ai))
    steps = []
    for d in (1, 2, 4):
        keep = (rowi < SUBLANES - d) if reverse else (rowi >= d)
        pr, pi = pows[d - 1]
        steps.append((SUBLANES - d if reverse else d, jnp.where(keep, pr, 0.0), jnp.where(keep, pi, 0.0)))
    cr = jnp.zeros((SUBLANES, lc), F32)
    ci = jnp.zeros((SUBLANES, lc), F32)
    for r in range(SUBLANES):
        pr, pi = pows[SUBLANES - 1 - r] if reverse else pows[r]
        cr = jnp.where(rowi == r, pr, cr)
        ci = jnp.where(rowi == r, pi, ci)
    return steps, cr, ci


def _scan_tile(xr, xi, steps, pr, pi, cr, ci):
    for shift, ar, ai in steps:
        rr = pltpu.roll(xr, shift, 0)
        ri = pltpu.roll(xi, shift, 0)
        xr, xi = xr + ar * rr - ai * ri, xi + ar * ri + ai * rr
    return xr + pr * cr - pi * ci, xi + pr * ci + pi * cr


SCAN_ROWS = 1024


def _scan_chunk(s):
    tt = min(SCAN_ROWS, s)
    seg = tt // SUBLANES
    assert s % tt == 0 and seg % SUBLANES == 0 and seg & (seg - 1) == 0, s
    return tt, seg


def _to_segments(a):
    s, wd = a.shape
    tt, seg = _scan_chunk(s)
    return jnp.transpose(a.reshape(s // tt, SUBLANES, seg, wd), (0, 2, 1, 3)).reshape(s, wd)


def _from_segments(a):
    s, wd = a.shape
    tt, seg = _scan_chunk(s)
    return jnp.transpose(a.reshape(s // tt, seg, SUBLANES, wd), (0, 2, 1, 3)).reshape(s, wd)


def _cpow2(xr, xi, k):
    for _ in range(k):
        xr, xi = _cmul(xr, xi, xr, xi)
    return xr, xi


def _fill_powers(pw_ref, ar, ai, seg, lc):
    _, p8r, p8i = _scan_consts(ar, ai, False, lc)
    a8r, a8i = _cpow2(ar, ai, 3)
    qr, qi = jnp.ones_like(ar), jnp.zeros_like(ai)
    for k in range(seg // SUBLANES):
        tr, ti = _cmul(p8r, p8i, qr, qi)
        for r in range(SUBLANES):
            rows = pl.ds((SUBLANES * k + r) * SUBLANES, SUBLANES)
            pw_ref[rows, :lc] = jnp.broadcast_to(tr[r:r + 1, :], (SUBLANES, lc))
            pw_ref[rows, lc:] = jnp.broadcast_to(ti[r:r + 1, :], (SUBLANES, lc))
        qr, qi = _cmul(qr, qi, a8r, a8i)


def _ssm_fwd(u, acat, bsup, csup, d_skip, name, side=None):
    s = u.shape[0]
    lc = SCAN_LANES
    tt, seg = _scan_chunk(s)
    nl, nt = N_STATE // lc, s // tt
    tile = lambda j: pl.ds(pl.multiple_of(j * SUBLANES, SUBLANES), SUBLANES)

    def body(u_ref, a_ref, b_ref, c_ref, d_ref, s_ref, y0_ref, y1_ref, carry, pw_ref):
        ar, ai = a_ref[:, :lc], a_ref[:, lc:]

        @pl.when(pl.program_id(1) == 0)
        def _():
            carry[...] = jnp.zeros_like(carry)
            _fill_powers(pw_ref, ar, ai, seg, lc)

        ut = u_ref[...]
        s_ref[...] = _dot(ut, b_ref[0])

        ar8, ai8 = jnp.broadcast_to(ar, (SUBLANES, lc)), jnp.broadcast_to(ai, (SUBLANES, lc))

        def local(j, x):
            xr = ar8 * x[0] - ai8 * x[1] + s_ref[tile(j), :lc]
            xi = ar8 * x[1] + ai8 * x[0] + s_ref[tile(j), lc:]
            s_ref[tile(j), :lc] = xr
            s_ref[tile(j), lc:] = xi
            return xr, xi

        zero = jnp.zeros((SUBLANES, lc), F32)
        er, ei = lax.fori_loop(0, seg, local, (zero, zero))
        steps, pr, pi = _scan_consts(*_cpow2(ar, ai, seg.bit_length() - 1), False, lc)
        cr, ci = carry[:, :lc], carry[:, lc:]
        tr, ti = _scan_tile(er, ei, steps, pr, pi, cr, ci)
        rowi = lax.broadcasted_iota(jnp.int32, (SUBLANES, lc), 0)
        before_r = jnp.where(rowi == 0, cr, pltpu.roll(tr, 1, 0))
        before_i = jnp.where(rowi == 0, ci, pltpu.roll(ti, 1, 0))
        carry[:, :lc] = jnp.broadcast_to(tr[SUBLANES - 1:, :], (SUBLANES, lc))
        carry[:, lc:] = jnp.broadcast_to(ti[SUBLANES - 1:, :], (SUBLANES, lc))

        def fix(j, _):
            pwr, pwi = pw_ref[tile(j), :lc], pw_ref[tile(j), lc:]
            s_ref[tile(j), :lc] += pwr * before_r - pwi * before_i
            s_ref[tile(j), lc:] += pwr * before_i + pwi * before_r
            return 0

        lax.fori_loop(0, seg, fix, 0)
        y0 = _dot(s_ref[...], c_ref[0], NT) + d_ref[...] * ut
        y0_ref[...] = y0
        y1_ref[...] = jax.nn.gelu(y0)

    chan = pl.BlockSpec((tt, LANES), lambda j, c: (c, j))
    sup = pl.BlockSpec((1, LANES, 2 * lc), lambda j, c: (j, 0, 0))
    s_in, s_out, s_shape, s_scratch, s_ops = _side_args(side)
    res = pl.pallas_call(
        _hosted(body, side, 5, 3, (nl, nt)), name=name, grid=(nl, nt),
        in_specs=[chan, pl.BlockSpec((1, 2 * lc), lambda j, c: (0, j)), sup, sup,
                  pl.BlockSpec((1, LANES), lambda j, c: (0, j))] + s_in,
        out_specs=[pl.BlockSpec((tt, 2 * lc), lambda j, c: (c, j)), chan, chan] + s_out,
        out_shape=[jax.ShapeDtypeStruct((s, 2 * N_STATE), F32), jax.ShapeDtypeStruct((s, SSM_WIDTH), F32),
                   jax.ShapeDtypeStruct((s, SSM_WIDTH), F32)] + s_shape,
        scratch_shapes=[pltpu.VMEM((SUBLANES, 2 * lc), F32), pltpu.VMEM((seg * SUBLANES, 2 * lc), F32)] + s_scratch,
        compiler_params=_params(("arbitrary", "arbitrary")),
    )(u, acat, bsup, csup, d_skip, *s_ops)
    return _split_side(res, 3, side)


def _ssm_bwd(dy0, states, u, acat, bsup, csup, d_skip, name, side=None):
    s = u.shape[0]
    lc = SCAN_LANES
    tt, seg = _scan_chunk(s)
    nl, nt = N_STATE // lc, s // tt
    tile = lambda j: pl.ds(pl.multiple_of(j * SUBLANES, SUBLANES), SUBLANES)

    def body(dy_ref, s_ref, sp_ref, u_ref, a_ref, b_ref, c_ref, d_ref,
             du_ref, da_ref, db_ref, dc_ref, dd_ref, lam_ref, carry, pw_ref):
        c = pl.program_id(1)
        ar, ai = a_ref[:, :lc], a_ref[:, lc:]

        @pl.when(c == 0)
        def _():
            carry[...] = jnp.zeros_like(carry)
            for r in (da_ref, db_ref, dc_ref, dd_ref):
                r[...] = jnp.zeros_like(r)
            _fill_powers(pw_ref, ar, ai, seg, lc)

        dy = dy_ref[...]
        ut = u_ref[...]
        lam_ref[...] = _dot(dy, c_ref[0])

        ar8, ai8 = jnp.broadcast_to(ar, (SUBLANES, lc)), jnp.broadcast_to(ai, (SUBLANES, lc))

        def local(i, x):
            j = seg - 1 - i
            xr = ar8 * x[0] + ai8 * x[1] + lam_ref[tile(j), :lc]
            xi = ar8 * x[1] - ai8 * x[0] + lam_ref[tile(j), lc:]
            lam_ref[tile(j), :lc] = xr
            lam_ref[tile(j), lc:] = xi
            return xr, xi

        zero = jnp.zeros((SUBLANES, lc), F32)
        er, ei = lax.fori_loop(0, seg, local, (zero, zero))
        big_r, big_i = _cpow2(ar, ai, seg.bit_length() - 1)
        steps, pr, pi = _scan_consts(big_r, -big_i, True, lc)
        cr, ci = carry[:, :lc], carry[:, lc:]
        tr, ti = _scan_tile(er, ei, steps, pr, pi, cr, ci)
        rowi = lax.broadcasted_iota(jnp.int32, (SUBLANES, lc), 0)
        after_r = jnp.where(rowi == SUBLANES - 1, cr, pltpu.roll(tr, SUBLANES - 1, 0))
        after_i = jnp.where(rowi == SUBLANES - 1, ci, pltpu.roll(ti, SUBLANES - 1, 0))
        carry[:, :lc] = jnp.broadcast_to(tr[:1, :], (SUBLANES, lc))
        carry[:, lc:] = jnp.broadcast_to(ti[:1, :], (SUBLANES, lc))

        start = c != nt - 1
        last_r = jnp.where(start, jnp.broadcast_to(sp_ref[SUBLANES - 1:, :lc], (SUBLANES, lc)), 0.0)
        last_i = jnp.where(start, jnp.broadcast_to(sp_ref[SUBLANES - 1:, lc:], (SUBLANES, lc)), 0.0)
        first_r = jnp.where(rowi == 0, last_r, pltpu.roll(s_ref[tile(seg - 1), :lc], 1, 0))
        first_i = jnp.where(rowi == 0, last_i, pltpu.roll(s_ref[tile(seg - 1), lc:], 1, 0))

        def fix(j, acc):
            dar, dai = acc
            k = seg - 1 - j
            pwr, pwi = pw_ref[tile(k), :lc], pw_ref[tile(k), lc:]
            lr = lam_ref[tile(j), :lc] + pwr * after_r + pwi * after_i
            li = lam_ref[tile(j), lc:] + pwr * after_i - pwi * after_r
            lam_ref[tile(j), :lc] = lr
            lam_ref[tile(j), lc:] = li
            jp = jnp.maximum(j - 1, 0)
            sr = jnp.where(j > 0, s_ref[tile(jp), :lc], first_r)
            si = jnp.where(j > 0, s_ref[tile(jp), lc:], first_i)
            return dar + lr * sr + li * si, dai + li * sr - lr * si

        dar, dai = lax.fori_loop(0, seg, fix, (zero, zero))
        da_ref[:, :lc] += dar
        da_ref[:, lc:] += dai
        lam = lam_ref[...].astype(BF16)
        du_ref[...] = (_dot(lam, b_ref[0], NT) + d_ref[...] * dy).astype(du_ref.dtype)
        db_ref[0] += _dot(ut, lam, TN)
        dc_ref[0] += _dot(dy, s_ref[...], TN)
        dd_ref[...] += jnp.sum(dy * ut, axis=0, keepdims=True)

    rev = lambda j, c: (nt - 1 - c, j)
    chan = pl.BlockSpec((tt, LANES), rev)
    sup = pl.BlockSpec((1, LANES, 2 * lc), lambda j, c: (j, 0, 0))
    row = pl.BlockSpec((1, LANES), lambda j, c: (0, j))
    s_in, s_out, s_shape, s_scratch, s_ops = _side_args(side)
    res = pl.pallas_call(
        _hosted(body, side, 8, 5, (nl, nt)), name=name, grid=(nl, nt),
        in_specs=[chan, pl.BlockSpec((tt, 2 * lc), rev),
                  pl.BlockSpec((SUBLANES, 2 * lc), lambda j, c: (jnp.maximum((nt - 1 - c) * seg - 1, 0), j)),
                  chan, pl.BlockSpec((1, 2 * lc), lambda j, c: (0, j)), sup, sup, row] + s_in,
        out_specs=[chan, pl.BlockSpec((SUBLANES, 2 * lc), lambda j, c: (0, j)), sup, sup, row] + s_out,
        out_shape=[jax.ShapeDtypeStruct((s, SSM_WIDTH), BF16), jax.ShapeDtypeStruct((SUBLANES, 2 * N_STATE), F32),
                   jax.ShapeDtypeStruct(bsup.shape, F32), jax.ShapeDtypeStruct(csup.shape, F32),
                   jax.ShapeDtypeStruct((1, SSM_WIDTH), F32)] + s_shape,
        scratch_shapes=[pltpu.VMEM((tt, 2 * lc), F32), pltpu.VMEM((SUBLANES, 2 * lc), F32),
                        pltpu.VMEM((seg * SUBLANES, 2 * lc), F32)] + s_scratch,
        compiler_params=_params(("arbitrary", "arbitrary")),
    )(dy0, states, states, u, acat, bsup, csup, d_skip, *s_ops)
    return _split_side(res, 5, side)


def _state_cols(xr, xi):
    lead = xr.shape[:-1]
    nl = N_STATE // SCAN_LANES
    both = jnp.stack([xr.reshape(lead + (nl, SCAN_LANES)), xi.reshape(lead + (nl, SCAN_LANES))], axis=-2)
    return both.reshape(lead + (2 * N_STATE,))


def _ssm_mats(a_re, a_im, log_dt, b_re, b_im, c_re, c_im):
    dt = jnp.exp(log_dt)[:, None]
    lr, li = a_re * dt, a_im * dt
    e = jnp.exp(lr)
    abar_r, abar_i = e * jnp.cos(li), e * jnp.sin(li)
    den = a_re * a_re + a_im * a_im
    coef_r = ((abar_r - 1.0) * a_re + abar_i * a_im) / den
    coef_i = (abar_i * a_re - (abar_r - 1.0) * a_im) / den
    bbar_r = coef_r[..., None] * b_re - coef_i[..., None] * b_im
    bbar_i = coef_r[..., None] * b_im + coef_i[..., None] * b_re
    nl = N_STATE // SCAN_LANES
    gpb = SSM_GROUPS // nl
    eye = jnp.eye(gpb, dtype=bool)[None, :, None, :, None]

    def sup(m_r, m_i):
        def one(m):
            m = m.reshape(nl, gpb, SSM_GROUP, 1, SSM_STATE)
            return jnp.where(eye, m, 0.0).reshape(nl, gpb * SSM_GROUP, SCAN_LANES)
        return jnp.concatenate([one(m_r), one(m_i)], axis=-1)

    acat = _state_cols(abar_r.reshape(1, N_STATE), abar_i.reshape(1, N_STATE))
    bsup = sup(jnp.transpose(bbar_r, (0, 2, 1)), jnp.transpose(bbar_i, (0, 2, 1)))
    csup = sup(c_re, -c_im)
    return acat, bsup, csup


def _mem_fwd(mem, g_mem, w_kv, g_k, name):
    ml = mem.shape[0]

    def body(mem_ref, gm_ref, w_ref, gk_ref, memn_ref, kv_ref, kn_ref, vv_ref):
        memn = _rms(mem_ref[...], gm_ref[...])
        memn_ref[...] = memn.astype(BF16)
        kv = _dot(memn, w_ref[...])
        kv_ref[...] = kv
        for hh in range(XA_HEADS):
            sl = slice(hh * XA_HEAD_DIM, (hh + 1) * XA_HEAD_DIM)
            kn_ref[:, sl] = _rms(kv[:, sl], gk_ref[...]).astype(BF16)
        vv_ref[...] = kv[:, XA_WIDTH:].astype(BF16)

    return pl.pallas_call(
        body, name=name,
        out_shape=[jax.ShapeDtypeStruct((ml, D_MODEL), BF16), jax.ShapeDtypeStruct((ml, 2 * XA_WIDTH), F32),
                   jax.ShapeDtypeStruct((ml, XA_WIDTH), BF16), jax.ShapeDtypeStruct((ml, XA_WIDTH), BF16)],
        compiler_params=_params(),
    )(mem, g_mem, w_kv, g_k)


def _mem_bwd(mem, g_mem, memn, w_kv, kv, g_k, dkn, dvv, name):
    def body(mem_ref, gm_ref, memn_ref, w_ref, kv_ref, gk_ref, dkn_ref, dvv_ref, dw_ref, dgm_ref, dgk_ref):
        kv = kv_ref[...]
        dgk = jnp.zeros(dgk_ref.shape, F32)
        parts = []
        for hh in range(XA_HEADS):
            sl = slice(hh * XA_HEAD_DIM, (hh + 1) * XA_HEAD_DIM)
            _, vjp = jax.vjp(_rms, kv[:, sl], gk_ref[...])
            dk, dg = vjp(dkn_ref[:, sl])
            parts.append(dk)
            dgk = dgk + dg
        dgk_ref[...] = dgk
        dkv = jnp.concatenate(parts + [dvv_ref[...]], axis=1)
        dw_ref[...] = _dot(memn_ref[...], dkv, TN)
        dmemn = _dot(dkv, w_ref[...], NT)
        _, vjp = jax.vjp(_rms, mem_ref[...], gm_ref[...])
        dgm_ref[...] = vjp(dmemn)[1]

    return pl.pallas_call(
        body, name=name,
        out_shape=[jax.ShapeDtypeStruct((D_MODEL, 2 * XA_WIDTH), F32), jax.ShapeDtypeStruct(g_mem.shape, F32),
                   jax.ShapeDtypeStruct(g_k.shape, F32)],
        compiler_params=_params(),
    )(mem, g_mem, memn, w_kv, kv, g_k, dkn, dvv)


def _xa_head(qx_h, g_q, kn_h, vv_h):
    qn = _rms(qx_h, g_q)
    sc = _dot(qn, kn_h, NT) * (XA_HEAD_DIM ** -0.5)
    sc = sc - jnp.max(sc, axis=-1, keepdims=True)
    e = jnp.exp(sc)
    p = e / jnp.sum(e, axis=-1, keepdims=True)
    return qn, p


def _xa_fwd(qx, g_q, kn, vv, name):
    def fn(qt, gq, knt, vvt):
        outs = []
        for hh in range(XA_HEADS):
            sl = slice(hh * XA_HEAD_DIM, (hh + 1) * XA_HEAD_DIM)
            _, p = _xa_head(qt[:, sl], gq, knt[:, sl], vvt[:, sl])
            outs.append(_dot(p, vvt[:, sl]))
        return (jnp.concatenate(outs, axis=1),), ()

    return _rw(fn, [qx], [g_q, kn, vv], [(XA_WIDTH, BF16)], [], name)[0]


def _xa_bwd(qx, g_q, kn, vv, do, name):
    def fn(qt, dot_, gq, knt, vvt):
        dqs, dks, dvs = [], [], []
        dgq = jnp.zeros_like(gq)
        for hh in range(XA_HEADS):
            sl = slice(hh * XA_HEAD_DIM, (hh + 1) * XA_HEAD_DIM)
            qn, p = _xa_head(qt[:, sl], gq, knt[:, sl], vvt[:, sl])
            doh = dot_[:, sl]
            dp = _dot(doh, vvt[:, sl], NT)
            dvs.append(_dot(p, doh, TN))
            ds = p * (dp - jnp.sum(dp * p, axis=-1, keepdims=True)) * (XA_HEAD_DIM ** -0.5)
            dqn = _dot(ds, knt[:, sl])
            dks.append(_dot(ds, qn, TN))
            _, vjp = jax.vjp(_rms, qt[:, sl], gq)
            dq, dg = vjp(dqn)
            dqs.append(dq)
            dgq = dgq + dg
        return ((jnp.concatenate(dqs, axis=1),),
                (jnp.concatenate(dks, axis=1), jnp.concatenate(dvs, axis=1), dgq))

    return _rw(fn, [qx, do], [g_q, kn, vv], [(XA_WIDTH, BF16)], [kn.shape, vv.shape, g_q.shape], name)


BIG = [
    ("w_in", (D_MODEL, IN_WIDTH), 1), ("ssm_w_glu", (SSM_WIDTH, SSM_WIDTH), 0), ("w_out", (D_MODEL, D_MODEL), 0),
    ("xa_w_q", (D_MODEL, XA_WIDTH), 0), ("xa_w_kv", (D_MODEL, 2 * XA_WIDTH), 0), ("xa_w_o", (XA_WIDTH, D_MODEL), 1),
    ("w_up", (D_MODEL, D_FF), 1), ("w_down", (D_FF, D_MODEL), 0),
]
BIG_INDEX = {n: i for i, (n, _, _) in enumerate(BIG)}


def _shard_shape(shape, axis):
    return tuple(d // N_DEV if i == axis else d for i, d in enumerate(shape))


def _shard_of(ref, axis, d):
    n = ref.shape[axis] // N_DEV
    return ref.at[pl.ds(d * n, n), :] if axis == 0 else ref.at[:, pl.ds(d * n, n)]


def _gather_side(names, shards):
    idxs = [BIG_INDEX[n] for n in names]

    def make(ins, outs, send_sems, recv_sems):
        x, y, c = lax.axis_index("x"), lax.axis_index("y"), lax.axis_index("c")
        cps = []
        for j, i in enumerate(idxs):
            mine = _shard_of(outs[j], BIG[i][2], 4 * x + 2 * y + c)
            cps.append(pltpu.make_async_copy(ins[j], mine, send_sems.at[N_DEV * j]))
            for rel in range(1, N_DEV):
                to = tuple(1 - p if rel >> bit & 1 else p for p, bit in ((x, 2), (y, 1), (c, 0)))
                cps.append(pltpu.make_async_remote_copy(
                    src_ref=ins[j], dst_ref=mine, send_sem=send_sems.at[N_DEV * j + rel],
                    recv_sem=recv_sems.at[N_DEV * j + rel], device_id=to, device_id_type=MESH))
        return cps

    return _Side(shards, [jax.ShapeDtypeStruct(BIG[i][1], BF16) for i in idxs], N_DEV * len(idxs), make)


def _sibling_side(names, grads):
    idxs = [BIG_INDEX[n] for n in names]

    def make(ins, outs, send_sems, recv_sems):
        x, y, c = lax.axis_index("x"), lax.axis_index("y"), lax.axis_index("c")
        return [pltpu.make_async_remote_copy(
            src_ref=_shard_of(ins[j], BIG[i][2], 2 * k + (1 - c)), dst_ref=outs[j].at[k],
            send_sem=send_sems.at[4 * j + k], recv_sem=recv_sems.at[4 * j + k], device_id=(x, y, 1 - c),
            device_id_type=MESH) for j, i in enumerate(idxs) for k in range(4)]

    shapes = [jax.ShapeDtypeStruct((4,) + _shard_shape(BIG[i][1], BIG[i][2]), F32) for i in idxs]
    return _Side(grads, shapes, 4 * len(idxs), make)


def _chips_side(parts):
    def make(ins, outs, send_sems, recv_sems):
        x, y, c = lax.axis_index("x"), lax.axis_index("y"), lax.axis_index("c")
        chips = [(1 - x, y), (x, 1 - y), (1 - x, 1 - y)]
        return [pltpu.make_async_remote_copy(
            src_ref=ins[j].at[2 * cx + cy], dst_ref=outs[j].at[r], send_sem=send_sems.at[3 * j + r],
            recv_sem=recv_sems.at[3 * j + r], device_id=(cx, cy, c), device_id_type=MESH)
            for r, (cx, cy) in enumerate(chips) for j in range(len(parts))]

    return _Side(parts, [jax.ShapeDtypeStruct((3,) + p.shape[1:], p.dtype) for p in parts], 3 * len(parts), make)


def _reduce_add(grad, recv, axis, core, name):
    rs, cs = recv.shape[1:]
    rt = _row_tile(rs, 256)
    nt = rs // rt

    def body(c_ref, g_ref, r_ref, p_ref, pb_ref):
        sm = g_ref[...] + r_ref[0]
        p_ref[0] = sm
        pb_ref[0] = sm.astype(BF16)

    if axis == 0:
        g_spec = pl.BlockSpec((rt, cs), lambda k, t, c_ref: ((2 * k + c_ref[0]) * nt + t, 0))
    else:
        g_spec = pl.BlockSpec((rt, cs), lambda k, t, c_ref: (t, 2 * k + c_ref[0]))
    slab = pl.BlockSpec((1, rt, cs), lambda k, t, c_ref: (k, t, 0))
    return pl.pallas_call(
        body, name=name,
        grid_spec=pltpu.PrefetchScalarGridSpec(num_scalar_prefetch=1, grid=(4, nt), in_specs=[g_spec, slab],
                                               out_specs=[slab, slab]),
        out_shape=[jax.ShapeDtypeStruct(recv.shape, F32), jax.ShapeDtypeStruct(recv.shape, BF16)],
        compiler_params=_params(("parallel", "parallel")),
    )(core, grad, recv)


def _all_gather(block, name, side):
    m_per, n = block.shape
    ns_in, ns_out = len(side.ins), len(side.out_shapes)

    def body(*refs):
        x_ref, s_ins, out_ref = refs[0], refs[1:1 + ns_in], refs[1 + ns_in]
        s_outs = refs[2 + ns_in:2 + ns_in + ns_out]
        send_sems, recv_sems, local_sem, s_send, s_recv = refs[2 + ns_in + ns_out:]
        others = side.make(s_ins, s_outs, s_send, s_recv)
        for cp in others:
            cp.start()
        x, y, c = lax.axis_index("x"), lax.axis_index("y"), lax.axis_index("c")
        me, sibling = (x, y, c), (x, y, 1 - c)
        chips = [(1 - x, y), (x, 1 - y), (1 - x, 1 - y)]

        def rows(px, py, pc):
            return out_ref.at[pl.ds((4 * px + 2 * py + pc) * m_per, m_per), :]

        def copy(k, blk, to, src=None):
            return pltpu.make_async_remote_copy(
                src_ref=rows(*blk) if src is None else src, dst_ref=rows(*blk),
                send_sem=send_sems.at[k], recv_sem=recv_sems.at[k], device_id=to, device_id_type=MESH)

        mine = pltpu.make_async_copy(x_ref, rows(*me), local_sem)
        mine.start()
        first = [copy(0, me, sibling, src=x_ref)]
        first += [copy(1 + j, me, (*chip, c), src=x_ref) for j, chip in enumerate(chips)]
        for cp in first:
            cp.start()
        passed = [copy(4 + j, (*chip, c), sibling) for j, chip in enumerate(chips)]
        for j, chip in enumerate(chips):
            copy(1 + j, (*chip, c), me).wait_recv()
            passed[j].start()
        copy(0, sibling, me).wait_recv()
        for j, chip in enumerate(chips):
            copy(4 + j, (*chip, 1 - c), me).wait_recv()
        for cp in first + passed:
            cp.wait_send()
        mine.wait()
        for cp in others:
            cp.wait()

    res = pl.pallas_call(
        body, name=name, in_specs=[ANY] * (1 + ns_in), out_specs=[ANY] * (1 + ns_out),
        out_shape=[jax.ShapeDtypeStruct((N_DEV * m_per, n), block.dtype)] + side.out_shapes,
        scratch_shapes=[pltpu.SemaphoreType.DMA((7,)), pltpu.SemaphoreType.DMA((7,)), pltpu.SemaphoreType.DMA]
        + side.sems(),
    )(block, *side.ins)
    return res[0], list(res[1:])


def _adam_math(w, g, m, v):
    m = ADAM_B1 * m + (1.0 - ADAM_B1) * g
    v = ADAM_B2 * v + (1.0 - ADAM_B2) * (g * g)
    m_hat = m / (1.0 - ADAM_B1 ** ADAM_STEP)
    v_hat = v / (1.0 - ADAM_B2 ** ADAM_STEP)
    delta = -ADAM_LR * (m_hat / (jnp.sqrt(v_hat) + ADAM_EPS) + ADAM_WD * w)
    return delta, m, v


def _adam_sharded(own, recv, w, m, v, chip, name):
    rs, cs = w.shape
    rt = _row_tile(rs, 256)

    def body(chip_ref, p_ref, r_ref, w_ref, m_ref, v_ref, g_out, d_out, m_out, v_out):
        g = p_ref[0] + r_ref[0].astype(F32) + r_ref[1].astype(F32) + r_ref[2].astype(F32)
        d, mn, vn = _adam_math(w_ref[...], g, m_ref[...], v_ref[...])
        g_out[...] = g
        d_out[...] = d
        m_out[...] = mn
        v_out[...] = vn

    tile = pl.BlockSpec((rt, cs), lambda t, chip_ref: (t, 0))
    return pl.pallas_call(
        body, name=name,
        grid_spec=pltpu.PrefetchScalarGridSpec(
            num_scalar_prefetch=1, grid=(rs // rt,),
            in_specs=[pl.BlockSpec((1, rt, cs), lambda t, chip_ref: (chip_ref[0], t, 0)),
                      pl.BlockSpec((3, rt, cs), lambda t, chip_ref: (0, t, 0)), tile, tile, tile],
            out_specs=[tile] * 4),
        out_shape=[jax.ShapeDtypeStruct((rs, cs), F32)] * 4,
        compiler_params=_params(("parallel",)),
    )(chip, own, recv, w, m, v)


SMALL = ["g_mix", "ssm_a_re", "ssm_a_im", "ssm_log_dt", "ssm_b_re", "ssm_b_im", "ssm_c_re", "ssm_c_im", "ssm_d",
         "sb_g_q", "sb_g_k", "g_out_ssm", "g_out_sb", "g_xa", "g_mem", "xa_g_q", "xa_g_k", "g_mlp"]
PACK_TILE = SUBLANES * LANES


def _natural_2d(n):
    return (n // LANES, LANES) if n % LANES == 0 else (1, n)


def _pack_small(arrs):
    parts = []
    for a in arrs:
        flat = a.reshape(-1)
        parts.append(jnp.pad(flat, (0, (-flat.shape[0]) % PACK_TILE)))
    return jnp.concatenate(parts).reshape(-1, LANES)


def _adam_replicated(gathered, sizes, ws, ms, vs, name):
    n_w = len(ws)
    r_dev = gathered.shape[0] // N_DEV
    offs, off = [], 0
    for n in sizes:
        offs.append(off)
        off += (n + PACK_TILE - 1) // PACK_TILE * SUBLANES
    assert off == r_dev

    def body(*refs):
        g_ref = refs[0]
        w_refs, m_refs, v_refs = refs[1:1 + n_w], refs[1 + n_w:1 + 2 * n_w], refs[1 + 2 * n_w:1 + 3 * n_w]
        outs = refs[1 + 3 * n_w:]

        def total(i, shape):
            r, cdim = shape
            acc = g_ref[pl.ds(offs[i], r), :cdim]
            for d in range(1, N_DEV):
                acc = acc + g_ref[pl.ds(d * r_dev + offs[i], r), :cdim]
            return acc

        for i in range(n_w):
            g = total(i, w_refs[i].shape)
            d, mn, vn = _adam_math(w_refs[i][...], g, m_refs[i][...], v_refs[i][...])
            for o, val in zip(outs[4 * i:4 * i + 4], (g, d, mn, vn)):
                o[...] = val
        outs[4 * n_w][...] = total(n_w, (SUBLANES, LANES))

    shapes = [w.shape for w in ws]
    return pl.pallas_call(
        body, name=name,
        out_shape=[jax.ShapeDtypeStruct(shp, F32) for shp in shapes for _ in range(4)]
        + [jax.ShapeDtypeStruct((SUBLANES, LANES), F32)],
        compiler_params=_params(),
    )(gathered, *ws, *ms, *vs)


def _step(x, mem, target, shards, sm, core):
    g, w, sums, reduced = {}, {}, {}, {}

    def gather(names):
        return _gather_side(names, [shards[n] for n in names])

    def to_sibling(names):
        return _sibling_side(names, [g[n] for n in names])

    def add_sibling(names, received):
        for n, r in zip(names, received):
            sums[n] = _reduce_add(g[n], r, BIG[BIG_INDEX[n]][2], core, "reduce_add_" + n)

    def to_chips(names):
        return _chips_side([sums[n][1] for n in names])

    def keep(names, received):
        for n, r in zip(names, received):
            reduced[n] = (sums[n][0], r)

    row = lambda a: a.reshape(1, -1)
    g_mix, g_xa, g_mlp, g_mem = row(sm["g_mix"]), row(sm["g_xa"]), row(sm["g_mlp"]), row(sm["g_mem"])
    g_os, g_ob = row(sm["g_out_ssm"]), row(sm["g_out_sb"])
    sb_gq, sb_gk = jnp.tile(row(sm["sb_g_q"]), (1, SB_HEADS)), jnp.tile(row(sm["sb_g_k"]), (1, SB_HEADS))
    xa_gq, xa_gk = row(sm["xa_g_q"]), row(sm["xa_g_k"])
    d_skip = row(sm["ssm_d"])

    h1, (w["w_in"],) = _norm_fwd(x, g_mix, "norm_mix", side=gather(["w_in"]))
    proj = _mm(h1, w["w_in"], "nn", "in_proj")
    u = _to_segments(proj[:, :SSM_WIDTH])
    q_raw, k_raw = (proj, SB_WIDTH, 1), (proj, SB_WIDTH, 2)
    v_col = (SSM_WIDTH + 2 * SB_WIDTH) // LANES
    sb_scale = SB_HEAD_DIM ** -0.5
    qs, ks = _rw(lambda qt, kt, gq, gk: ((_rms_groups(qt, gq, sb_scale), _rms_groups(kt, gk, 1.0)), ()),
                 [q_raw, k_raw], [sb_gq, sb_gk], [(SB_WIDTH, BF16)] * 2, [], "sb_qk_norm")
    early = ["ssm_w_glu", "w_out", "xa_w_q", "xa_w_kv", "xa_w_o", "w_up"]
    y_sb, got = _sb_fwd(qs, ks, proj, "sb_fwd", v_col=v_col, side=gather(early))
    w.update(zip(early, got))

    ssm_args = (sm["ssm_a_re"], sm["ssm_a_im"], sm["ssm_log_dt"], sm["ssm_b_re"], sm["ssm_b_im"],
                sm["ssm_c_re"], sm["ssm_c_im"])
    (acat, bsup, csup), mats_vjp = jax.vjp(_ssm_mats, *ssm_args)
    (states, y0, y1), (w["w_down"],) = _ssm_fwd(u, acat, bsup, csup, d_skip, "ssm_fwd", side=gather(["w_down"]))
    z_glu, y_ssm = _mm(y1, w["ssm_w_glu"], "nn", "ssm_glu", epi=lambda r, yt: (r, yt * jax.nn.sigmoid(r)),
                       extras=(y1,), out_dtypes=(F32, F32))
    y_ssm = _from_segments(y_ssm)

    def cat_norm(a, b, ga, gb):
        return jnp.concatenate([_rms(a, ga), _rms(b, gb)], axis=1)

    ycat = _rw(lambda a, b, ga, gb: ((cat_norm(a, b, ga, gb),), ()), [y_ssm, y_sb], [g_os, g_ob],
               [(D_MODEL, BF16)], [], "norm_out")[0]

    def residual_norm_epi(r, xt, gt):
        xn = r + xt
        return xn, _rms(xn, gt)

    x1, h2 = _mm(ycat, w["w_out"], "nn", "out_proj", epi=residual_norm_epi, extras=(x,), fulls=(g_xa,),
                 out_dtypes=(F32, BF16))
    qx = _mm(h2, w["xa_w_q"], "nn", "xa_q")
    memn, kv, kn_x, vv_x = _mem_fwd(mem, g_mem, w["xa_w_kv"], xa_gk, "xa_mem")
    o_xa = _xa_fwd(qx, xa_gq, kn_x, vv_x, "xa_fwd")
    x2, h3 = _mm(o_xa, w["xa_w_o"], "nn", "xa_o", epi=residual_norm_epi, extras=(x1,), fulls=(g_mlp,),
                 out_dtypes=(F32, BF16))

    def up_epi(r):
        rl = jnp.maximum(r, 0.0)
        return (rl * rl,)

    r_up = _mm(h3, w["w_up"], "nn", "mlp_up", epi=up_epi, out_dtypes=(BF16,))

    def loss_epi(r, xt, tt):
        d = r + xt - tt
        return (d * (1.0 / D_MODEL),) * 2, (jnp.sum(d * d, axis=0, keepdims=True),)

    dx3, dx3_b, sq = _mm(r_up, w["w_down"], "nn", "mlp_down", epi=loss_epi, extras=(x2, target),
                         out_dtypes=(F32, BF16), sums=[(1, D_MODEL)])
    loss = jnp.sum(sq) * (0.5 / D_MODEL)

    def norm_bwd_epi(r, xt, drt, gt):
        _, vjp = jax.vjp(_rms, xt, gt)
        dx_, dg_ = vjp(r)
        return (dx_ + drt,) * 2, (dg_,)

    g["w_down"] = _mm(r_up, dx3_b, "tn", "d_w_down", tk=2048)
    da = _mm(dx3_b, w["w_down"], "nt", "d_r", epi=lambda r, rt: (r * 2.0 * jnp.sqrt(rt.astype(F32)),), extras=(r_up,),
             out_dtypes=(BF16,))
    g["w_up"] = _mm(h3, da, "tn", "d_w_up", tk=2048)
    mlp = ["w_down", "w_up"]
    (dx2, dx2_b, g["g_mlp"]), got = _mm(da, w["w_up"], "nt", "d_h3", epi=norm_bwd_epi, extras=(x2, dx3),
                                        fulls=(g_mlp,), out_dtypes=(F32, BF16), sums=[g_mlp.shape],
                                        side=to_sibling(mlp))
    add_sibling(mlp, got)
    g["xa_w_o"] = _mm(o_xa, dx2_b, "tn", "d_xa_w_o", tk=2048)
    do_xa = _mm(dx2_b, w["xa_w_o"], "nt", "d_o_xa")
    dqx, dkn_x, dvv_x, g["xa_g_q"] = _xa_bwd(qx, xa_gq, kn_x, vv_x, do_xa, "xa_bwd")
    g["xa_w_kv"], g["g_mem"], g["xa_g_k"] = _mem_bwd(mem, g_mem, memn, w["xa_w_kv"], kv, xa_gk, dkn_x, dvv_x,
                                                     "xa_mem_bwd")
    g["xa_w_q"] = _mm(h2, dqx, "tn", "d_xa_w_q", tk=2048)
    dx1, dx1_b, g["g_xa"] = _mm(dqx, w["xa_w_q"], "nt", "d_h2", epi=norm_bwd_epi, extras=(x1, dx2), fulls=(g_xa,),
                                out_dtypes=(F32, BF16), sums=[g_xa.shape])
    g["w_out"] = _mm(ycat, dx1_b, "tn", "d_w_out", tk=2048)
    dycat = _mm(dx1_b, w["w_out"], "nt", "d_ycat")

    def cat_bwd(a, b, dy, ga, gb):
        _, vjp = jax.vjp(cat_norm, a, b, ga, gb)
        da_, db_, dga, dgb = vjp(dy)
        return (da_, db_), (dga, dgb)

    dy_ssm, dy_sb, g["g_out_ssm"], g["g_out_sb"] = _rw(
        cat_bwd, [y_ssm, y_sb, dycat], [g_os, g_ob], [(SSM_WIDTH, F32), (SB_WIDTH, F32)], [g_os.shape, g_ob.shape],
        "d_norm_out")

    def glu_bwd(dy, yt, zt):
        sg = jax.nn.sigmoid(zt)
        return (dy * sg, dy * yt * sg * (1.0 - sg)), ()

    dy1_a, dz = _rw(glu_bwd, [_to_segments(dy_ssm), y1, z_glu], [], [(SSM_WIDTH, F32), (SSM_WIDTH, BF16)], [], "d_glu")
    g["ssm_w_glu"] = _mm(y1, dz, "tn", "d_w_glu", tk=2048)

    def gelu_bwd_epi(r, da_, y0t):
        _, vjp = jax.vjp(jax.nn.gelu, y0t)
        return (vjp(r + da_)[0],)

    mid = ["w_out", "xa_w_q", "xa_w_kv", "xa_w_o", "ssm_w_glu"]
    dy0, got = _mm(dz, w["ssm_w_glu"], "nt", "d_y1", epi=gelu_bwd_epi, extras=(dy1_a, y0), side=to_sibling(mid))
    add_sibling(mid, got)
    (du, da8, d_bsup, d_csup, g["ssm_d"]), got = _ssm_bwd(dy0, states, u, acat, bsup, csup, d_skip, "ssm_bwd",
                                                          side=to_chips(mlp))
    keep(mlp, got)
    d_acat = jnp.sum(da8, axis=0, keepdims=True)
    for nm, val in zip(("ssm_a_re", "ssm_a_im", "ssm_log_dt", "ssm_b_re", "ssm_b_im", "ssm_c_re", "ssm_c_im"),
                       mats_vjp((d_acat, d_bsup, d_csup))):
        g[nm] = val

    (dqs, dks, dvs), got = _sb_bwd(qs, ks, proj, y_sb, dy_sb, "sb_bwd", v_col=v_col, side=to_chips(mid))
    keep(mid, got)

    def d_proj_rows(du_t, qt, dqt, kt, dkt, dvt, gq, gk):
        _, vjp_q = jax.vjp(lambda a, b_: _rms_groups(a, b_, sb_scale), qt, gq)
        _, vjp_k = jax.vjp(lambda a, b_: _rms_groups(a, b_, 1.0), kt, gk)
        (dq_, dgq_), (dk_, dgk_) = vjp_q(dqt), vjp_k(dkt)
        rows = jnp.concatenate([du_t, dq_.astype(BF16), dk_.astype(BF16), dvt.astype(BF16)], axis=1)
        return (rows,), (dgq_, dgk_)

    dproj, dgq, dgk = _rw(d_proj_rows, [_from_segments(du), q_raw, dqs, k_raw, dks, dvs], [sb_gq, sb_gk],
                          [(IN_WIDTH, BF16)], [sb_gq.shape, sb_gk.shape], "d_proj")
    g["sb_g_q"] = jnp.sum(dgq.reshape(SB_HEADS, SB_HEAD_DIM), axis=0)
    g["sb_g_k"] = jnp.sum(dgk.reshape(SB_HEADS, SB_HEAD_DIM), axis=0)
    g["w_in"] = _mm(h1, dproj, "tn", "d_w_in", tk=2048)
    dh1, got = _mm(dproj, w["w_in"], "nt", "d_h1", side=to_sibling(["w_in"]))
    add_sibling(["w_in"], got)
    dx, g["g_mix"] = _norm_bwd(x, g_mix, dh1, dx1, "d_norm_mix")

    packed = _pack_small([g[n] for n in SMALL] + [loss.reshape(1)])
    everyone, got = _all_gather(packed, "gather_small", to_chips(["w_in"]))
    keep(["w_in"], got)
    return dx, everyone, reduced


def kernel(x, mem, g_mix, w_in, ssm_a_re, ssm_a_im, ssm_log_dt, ssm_b_re, ssm_b_im, ssm_c_re, ssm_c_im, ssm_d, ssm_w_glu, sb_g_q, sb_g_k, g_out_ssm, g_out_sb, w_out, g_xa, g_mem, xa_w_q, xa_w_kv, xa_g_q, xa_g_k, xa_w_o, g_mlp, w_up, w_down, loss_target, m_g_mix, m_w_in, m_ssm_a_re, m_ssm_a_im, m_ssm_log_dt, m_ssm_b_re, m_ssm_b_im, m_ssm_c_re, m_ssm_c_im, m_ssm_d, m_ssm_w_glu, m_sb_g_q, m_sb_g_k, m_g_out_ssm, m_g_out_sb, m_w_out, m_g_xa, m_g_mem, m_xa_w_q, m_xa_w_kv, m_xa_g_q, m_xa_g_k, m_xa_w_o, m_g_mlp, m_w_up, m_w_down, v_g_mix, v_w_in, v_ssm_a_re, v_ssm_a_im, v_ssm_log_dt, v_ssm_b_re, v_ssm_b_im, v_ssm_c_re, v_ssm_c_im, v_ssm_d, v_ssm_w_glu, v_sb_g_q, v_sb_g_k, v_g_out_ssm, v_g_out_sb, v_w_out, v_g_xa, v_g_mem, v_xa_w_q, v_xa_w_kv, v_xa_g_q, v_xa_g_k, v_xa_w_o, v_g_mlp, v_w_up, v_w_down):
    given = dict(locals())
    order = ["g_mix", "w_in", "ssm_a_re", "ssm_a_im", "ssm_log_dt", "ssm_b_re", "ssm_b_im", "ssm_c_re", "ssm_c_im",
             "ssm_d", "ssm_w_glu", "sb_g_q", "sb_g_k", "g_out_ssm", "g_out_sb", "w_out", "g_xa", "g_mem", "xa_w_q",
             "xa_w_kv", "xa_g_q", "xa_g_k", "xa_w_o", "g_mlp", "w_up", "w_down"]
    assert sorted([n for n, _, _ in BIG] + SMALL) == sorted(order)
    core = lax.axis_index("c").astype(jnp.int32).reshape(1)
    chip = (2 * lax.axis_index("x") + lax.axis_index("y")).astype(jnp.int32).reshape(1)

    shards = {n: given[n][0].astype(BF16) for n, _, _ in BIG}
    sm = {n: given[n][0] for n in SMALL}
    dx, everyone, reduced = _step(x[0], mem[0], loss_target[0], shards, sm, core)

    res = {}
    for n, _, _ in BIG:
        own, recv = reduced[n]
        outs = _adam_sharded(own, recv, given[n][0], given["m_" + n][0], given["v_" + n][0], chip, "adam_" + n)
        for kind, val in zip(("grad", "delta", "new_m", "new_v"), outs):
            res[kind + "_" + n] = val[None]

    sizes = [math.prod(sm[n].shape) for n in SMALL] + [1]
    nat = lambda a: a.reshape(_natural_2d(math.prod(a.shape)))
    outs = _adam_replicated(everyone, sizes, [nat(sm[n]) for n in SMALL], [nat(given["m_" + n][0]) for n in SMALL],
                            [nat(given["v_" + n][0]) for n in SMALL], "adam_replicated")
    for i, n in enumerate(SMALL):
        for kind, val in zip(("grad", "delta", "new_m", "new_v"), outs[4 * i:4 * i + 4]):
            res[kind + "_" + n] = val.reshape(given[n].shape)
    loss_out = outs[-1][0, 0]
    return (loss_out, dx[None], *[res["grad_" + n] for n in order], *[res["delta_" + n] for n in order],
            *[res["new_m_" + n] for n in order], *[res["new_v_" + n] for n in order])
```

```python
import functools
import math

import jax
import jax.numpy as jnp
from jax import lax
from jax.experimental import pallas as pl
from jax.experimental.pallas import tpu as pltpu

F32 = jnp.float32
BF16 = jnp.bfloat16
MESH = pl.DeviceIdType.MESH

N_DEV = 8
D_MODEL = 1024
SSM_WIDTH = 512
SSM_GROUP = 16
SSM_GROUPS = 32
SSM_STATE = 64
N_STATE = SSM_GROUPS * SSM_STATE
SB_HEADS = 8
SB_HEAD_DIM = 64
SB_WIDTH = 512
IN_WIDTH = 2048
XA_HEADS = 4
XA_HEAD_DIM = 128
XA_WIDTH = 512
D_FF = 4096
NORM_EPS = 1e-6
ADAM_LR = 0.001
ADAM_B1 = 0.9
ADAM_B2 = 0.999
ADAM_EPS = 1e-08
ADAM_WD = 0.01
ADAM_STEP = 10

LANES = 128
SUBLANES = 8
VMEM_LIMIT = 56 * 1024 * 1024
SCAN_LANES = 512
SB_BLOCK = 256
SB_UNDERFLOW = -110.0

NN = (((1,), (0,)), ((), ()))
NT = (((1,), (1,)), ((), ()))
TN = (((0,), (0,)), ((), ()))


def _params(sem=None):
    return pltpu.CompilerParams(dimension_semantics=sem, vmem_limit_bytes=VMEM_LIMIT)


def _dot(a, b, dims=NN):
    return lax.dot_general(a.astype(BF16), b.astype(BF16), dims, preferred_element_type=F32)


def _rms(x, g):
    return x * lax.rsqrt(jnp.mean(x * x, axis=-1, keepdims=True) + NORM_EPS) * g


ANY = pl.BlockSpec(memory_space=pl.ANY)


class _Side:
    def __init__(self, ins, out_shapes, n_sem, make, finish=None):
        self.ins, self.out_shapes, self.n_sem, self.make = list(ins), list(out_shapes), n_sem, make
        self.finish = finish

    def sems(self):
        return [pltpu.SemaphoreType.DMA((self.n_sem,)), pltpu.SemaphoreType.DMA((self.n_sem,))]


def _hosted(body, side, n_in, n_out, grid):
    if side is None:
        return body
    ns_in, ns_out = len(side.ins), len(side.out_shapes)

    def wrapped(*refs):
        ins, refs = refs[:n_in], refs[n_in:]
        s_ins, refs = refs[:ns_in], refs[ns_in:]
        outs, refs = refs[:n_out], refs[n_out:]
        s_outs, refs = refs[:ns_out], refs[ns_out:]
        scratch, sems = refs[:-2], refs[-2:]
        ids = [pl.program_id(d) for d in range(len(grid))]
        first = functools.reduce(jnp.logical_and, [i == 0 for i in ids])
        last = functools.reduce(jnp.logical_and, [i == n - 1 for i, n in zip(ids, grid)])

        @pl.when(first)
        def _():
            for cp in side.make(s_ins, s_outs, *sems):
                cp.start()

        body(*ins, *outs, *scratch)

        @pl.when(last)
        def _():
            if side.finish is not None:
                side.finish(s_ins, s_outs, *sems)
            else:
                for cp in side.make(s_ins, s_outs, *sems):
                    cp.wait()

    return wrapped


def _side_args(side):
    if side is None:
        return [], [], [], [], []
    return ([ANY] * len(side.ins), [ANY] * len(side.out_shapes), side.out_shapes, side.sems(), side.ins)


def _split_side(res, n_out, side):
    res = list(res)
    main = res[0] if n_out == 1 else res[:n_out]
    return main if side is None else (main, res[n_out:])


def _mm(a, b, mode, name, *, epi=None, extras=(), fulls=(), out_dtypes=(F32,), sums=(), tm=1024, tn=1024, tk=1024,
        side=None):
    if mode == "nn":
        (m, k), (k2, n) = a.shape, b.shape
    elif mode == "nt":
        (m, k), (n, k2) = a.shape, b.shape
    else:
        (k, m), (k2, n) = a.shape, b.shape
    assert k == k2, (name, a.shape, b.shape)
    tm, tn, tk = min(tm, m), min(tn, n), min(tk, k)
    assert m % tm == 0 and n % tn == 0 and k % tk == 0, (name, m, n, k)
    nk = k // tk
    dims = {"nn": NN, "nt": NT, "tn": TN}[mode]
    if mode == "tn":
        a_spec = pl.BlockSpec((tk, tm), lambda i, j, kk: (kk, i))
    else:
        a_spec = pl.BlockSpec((tm, tk), lambda i, j, kk: (i, kk))
    if mode == "nt":
        b_spec = pl.BlockSpec((tn, tk), lambda i, j, kk: (j, kk))
    else:
        b_spec = pl.BlockSpec((tk, tn), lambda i, j, kk: (kk, j))
    mn_spec = pl.BlockSpec((tm, tn), lambda i, j, kk: (i, j))
    n_ex, n_full, n_out, n_sum = len(extras), len(fulls), len(out_dtypes), len(sums)
    n_in = 2 + n_ex + n_full

    def body(*refs):
        a_ref, b_ref = refs[:2]
        ex = refs[2:n_in]
        outs = refs[n_in:n_in + n_out]
        sum_refs = refs[n_in + n_out:n_in + n_out + n_sum]
        kk = pl.program_id(2)
        first_tile = jnp.logical_and(pl.program_id(0) == 0, pl.program_id(1) == 0)

        def finish(r):
            vals = epi(r, *[e[...] for e in ex]) if epi is not None else (r,)
            if n_sum:
                vals, parts = vals

                @pl.when(first_tile)
                def _():
                    for sr in sum_refs:
                        sr[...] = jnp.zeros_like(sr)

                for sr, p in zip(sum_refs, parts):
                    sr[...] += p
            for o, v in zip(outs, vals):
                o[...] = v.astype(o.dtype)

        if nk == 1:
            finish(_dot(a_ref[...], b_ref[...], dims))
        else:
            acc = refs[n_in + n_out + n_sum]

            @pl.when(kk == 0)
            def _():
                acc[...] = jnp.zeros_like(acc)

            acc[...] += _dot(a_ref[...], b_ref[...], dims)

            @pl.when(kk == nk - 1)
            def _():
                finish(acc[...])

    grid = (m // tm, n // tn, nk)
    whole = lambda shape: pl.BlockSpec(shape, lambda i, j, kk: (0,) * len(shape))
    s_in, s_out, s_shape, s_scratch, s_ops = _side_args(side)
    seq = bool(side) or n_sum > 0
    res = pl.pallas_call(
        _hosted(body, side, n_in, n_out + n_sum, grid), name=name, grid=grid,
        in_specs=[a_spec, b_spec] + [mn_spec] * n_ex + [whole(f.shape) for f in fulls] + s_in,
        out_specs=[mn_spec] * n_out + [whole(shape) for shape in sums] + s_out,
        out_shape=[jax.ShapeDtypeStruct((m, n), dt) for dt in out_dtypes]
        + [jax.ShapeDtypeStruct(shape, F32) for shape in sums] + s_shape,
        scratch_shapes=([pltpu.VMEM((tm, tn), F32)] if nk > 1 else []) + s_scratch,
        compiler_params=_params(("arbitrary",) * 3 if seq else ("parallel", "parallel", "arbitrary")),
    )(a, b, *extras, *fulls, *s_ops)
    return _split_side(res, n_out + n_sum, side)


def _row_tile(s, target):
    if s <= target:
        return s
    return max(t for t in range(16, target + 1, 16) if s % t == 0)


def _rw(fn, rows, fulls, row_out, acc_out, name, tm=512, side=None):
    cols = [r[1:] if isinstance(r, tuple) else (r.shape[1], 0) for r in rows]
    rows = [r[0] if isinstance(r, tuple) else r for r in rows]
    s = rows[0].shape[0]
    tm = _row_tile(s, tm)
    nr, nf, nro, nao = len(rows), len(fulls), len(row_out), len(acc_out)

    def body(*refs):
        r = refs[:nr]
        f = refs[nr:nr + nf]
        ro = refs[nr + nf:nr + nf + nro]
        ao = refs[nr + nf + nro:]
        outs, accs = fn(*[x[...] for x in r], *[x[...] for x in f])
        for o, v in zip(ro, outs):
            o[...] = v.astype(o.dtype)
        if nao:
            @pl.when(pl.program_id(0) == 0)
            def _():
                for a in ao:
                    a[...] = jnp.zeros_like(a)

            for a, v in zip(ao, accs):
                a[...] += v

    full_spec = lambda shape: pl.BlockSpec(shape, lambda i: (0,) * len(shape))
    s_in, s_out, s_shape, s_scratch, s_ops = _side_args(side)
    res = pl.pallas_call(
        _hosted(body, side, nr + nf, nro + nao, (s // tm,)), name=name, grid=(s // tm,),
        in_specs=[pl.BlockSpec((tm, wd), functools.partial(lambda i, cb: (i, cb), cb=cb)) for wd, cb in cols]
        + [full_spec(x.shape) for x in fulls] + s_in,
        out_specs=[pl.BlockSpec((tm, d), lambda i: (i, 0)) for d, _ in row_out]
        + [full_spec(shape) for shape in acc_out] + s_out,
        out_shape=[jax.ShapeDtypeStruct((s, d), dt) for d, dt in row_out]
        + [jax.ShapeDtypeStruct(shape, F32) for shape in acc_out] + s_shape,
        scratch_shapes=s_scratch,
        compiler_params=_params(("arbitrary",)),
    )(*rows, *fulls, *s_ops)
    res = list(res)
    return res if side is None else (res[:nro + nao], res[nro + nao:])


def _norm_fwd(x, g, name, side=None):
    res = _rw(lambda xt, gt: ((_rms(xt, gt),), ()), [x], [g], [(x.shape[1], BF16)], [], name, side=side)
    return res[0] if side is None else (res[0][0], res[1])


def _norm_bwd(x, g, dh, dres, name, side=None):
    def fn(xt, dht, drt, gt):
        _, vjp = jax.vjp(_rms, xt, gt)
        dx, dg = vjp(dht)
        return (dx + drt,), (dg,)

    return _rw(fn, [x, dh, dres], [g], [(x.shape[1], F32)], [g.shape], name, side=side)


def _rms_groups(x, g, scale):
    lo = lax.broadcasted_iota(jnp.int32, (1, LANES), 1) < SB_HEAD_DIM
    x2 = x * x
    outs = []
    for cb in range(x.shape[1] // LANES):
        sl = slice(cb * LANES, (cb + 1) * LANES)
        s_lo = jnp.sum(jnp.where(lo, x2[:, sl], 0.0), axis=-1, keepdims=True)
        s_hi = jnp.sum(jnp.where(lo, 0.0, x2[:, sl]), axis=-1, keepdims=True)
        r = jnp.where(lo, lax.rsqrt(s_lo * (1.0 / SB_HEAD_DIM) + NORM_EPS),
                      lax.rsqrt(s_hi * (1.0 / SB_HEAD_DIM) + NORM_EPS))
        outs.append(x[:, sl] * r)
    return jnp.concatenate(outs, axis=1) * g * scale


def _log_sigmoid(z):
    return jnp.minimum(z, 0.0) - jnp.log(1.0 + jnp.exp(-jnp.abs(z)))


def _split_dot(x, u2):
    hi = x.astype(BF16)
    lo = (x - hi.astype(F32)).astype(BF16)
    return jnp.dot(jnp.concatenate([hi, lo], axis=1), u2, preferred_element_type=F32)


def _sb_consts(b):
    row = lax.broadcasted_iota(jnp.int32, (b, b), 0)
    col = lax.broadcasted_iota(jnp.int32, (b, b), 1)
    tri = col < row
    u_after = (row > col).astype(BF16)
    u_from = (row >= col).astype(BF16)
    stack = lambda u: jnp.concatenate([u, u], axis=0)
    lane_lo = lax.broadcasted_iota(jnp.int32, (b, LANES), 1) < SB_HEAD_DIM
    return tri, stack(u_after), stack(u_from), lane_lo


def _sb_scores(qh, kb, a_run, keep, u2_after):
    z = lax.dot_general(qh, kb, NT, preferred_element_type=F32)
    lb = _log_sigmoid(z)
    l = lb - z
    if keep is not None:
        l = jnp.where(keep, l, 0.0)
    w = jnp.exp(lb + (a_run + _split_dot(l, u2_after)))
    if keep is not None:
        w = jnp.where(keep, w, 0.0)
    return lb, l, w


def _sb_walk(qi, carry, step):
    def cond(state):
        n, c = state
        return jnp.logical_and(n <= qi, jnp.max(jnp.maximum(c[0], c[1])) > SB_UNDERFLOW)

    def body(state):
        n, c = state
        return n + 1, step(n, c)

    return lax.while_loop(cond, body, (jnp.int32(2), carry))[1]


def _two_heads(x, lane_lo):
    zero = jnp.zeros_like(x)
    return jnp.where(lane_lo, x, zero), jnp.where(lane_lo, zero, x)


def _sb_fwd(qs, ks, v, name, v_col=0, side=None):
    s, width = qs.shape
    b = min(SB_BLOCK, s)

    def body(q_ref, k_ref, v_ref, o_ref):
        qi = pl.program_id(1)
        tri, u2_after, _, lane_lo = _sb_consts(b)
        q_a, q_b = _two_heads(q_ref[...], lane_lo)

        def step(n, carry, keep):
            a_a, a_b, acc = carry
            off = pl.multiple_of(jnp.maximum(qi - n, 0) * b, b)
            kb = k_ref[pl.ds(off, b), :]
            v_a, v_b = _two_heads(v_ref[pl.ds(off, b), :].astype(BF16), lane_lo)
            _, l_a, w_a = _sb_scores(q_a, kb, a_a, keep, u2_after)
            _, l_b, w_b = _sb_scores(q_b, kb, a_b, keep, u2_after)
            acc = acc + jnp.dot(jnp.concatenate([w_a.astype(BF16), w_b.astype(BF16)], axis=1),
                                jnp.concatenate([v_a, v_b], axis=0), preferred_element_type=F32)
            return (a_a + jnp.sum(l_a, axis=1, keepdims=True), a_b + jnp.sum(l_b, axis=1, keepdims=True), acc)

        zero = jnp.zeros((b, 1), F32)
        carry = step(0, (zero, zero, jnp.zeros((b, LANES), F32)), tri)
        carry = step(1, carry, jnp.broadcast_to(qi > 0, tri.shape))
        carry = _sb_walk(qi, carry, lambda n, c: step(n, c, None))
        o_ref[...] = carry[2]

    blk = pl.BlockSpec((b, LANES), lambda hp, i: (i, hp))
    full = pl.BlockSpec((s, LANES), lambda hp, i: (0, hp))
    full_v = pl.BlockSpec((s, LANES), lambda hp, i: (0, hp + v_col))
    grid = (width // LANES, s // b)
    s_in, s_out, s_shape, s_scratch, s_ops = _side_args(side)
    res = pl.pallas_call(
        _hosted(body, side, 3, 1, grid), name=name, grid=grid,
        in_specs=[blk, full, full_v] + s_in, out_specs=[blk] + s_out,
        out_shape=[jax.ShapeDtypeStruct((s, width), F32)] + s_shape, scratch_shapes=s_scratch,
        compiler_params=_params(("arbitrary", "arbitrary")),
    )(qs, ks, v, *s_ops)
    return _split_side(res, 1, side)


def _sb_bwd(qs, ks, v, out, dout, name, v_col=0, side=None):
    s, width = qs.shape
    b = min(SB_BLOCK, s)
    nkb = s // b

    def body(q_ref, k_ref, v_ref, o_ref, do_ref, dq_ref, dk_ref, dv_ref):
        qi = pl.program_id(1)

        @pl.when(qi == 0)
        def _():
            dk_ref[...] = jnp.zeros_like(dk_ref)
            dv_ref[...] = jnp.zeros_like(dv_ref)

        tri, u2_after, u2_from, lane_lo = _sb_consts(b)
        q_a, q_b = _two_heads(q_ref[...], lane_lo)
        dob = do_ref[...].astype(BF16)
        do_a, do_b = _two_heads(dob, lane_lo)
        prod = dob.astype(F32) * o_ref[...]
        d_a = jnp.sum(jnp.where(lane_lo, prod, 0.0), axis=1, keepdims=True)
        d_b = jnp.sum(jnp.where(lane_lo, 0.0, prod), axis=1, keepdims=True)
        q_rows = jnp.concatenate([q_a, q_b], axis=0)
        do_rows = jnp.concatenate([do_a, do_b], axis=0)

        def head(qh, doh, kb, vb, a_run, d_rem, keep):
            lb, l, w = _sb_scores(qh, kb, a_run, keep, u2_after)
            wb = w.astype(BF16)
            g = lax.dot_general(doh, vb, NT, preferred_element_type=F32) * wb.astype(F32)
            g_before = d_rem - _split_dot(g, u2_from)
            dz = g - (g + g_before) * jnp.exp(lb)
            if keep is not None:
                dz = jnp.where(keep, dz, 0.0)
            return (dz.astype(BF16), wb, a_run + jnp.sum(l, axis=1, keepdims=True),
                    d_rem - jnp.sum(g, axis=1, keepdims=True))

        def step(n, carry, keep):
            a_a, a_b, r_a, r_b, dq = carry
            jb = jnp.maximum(qi - n, 0)
            off = pl.multiple_of(jb * b, b)
            kb = k_ref[pl.ds(off, b), :]
            vb = v_ref[pl.ds(off, b), :].astype(BF16)
            k_a, k_b = _two_heads(kb, lane_lo)
            dz_a, w_a, a_a, r_a = head(q_a, do_a, kb, vb, a_a, r_a, keep)
            dz_b, w_b, a_b, r_b = head(q_b, do_b, kb, vb, a_b, r_b, keep)
            dq = dq + jnp.dot(jnp.concatenate([dz_a, dz_b], axis=1), jnp.concatenate([k_a, k_b], axis=0),
                              preferred_element_type=F32)
            dk_ref[pl.ds(off, b), :] += lax.dot_general(jnp.concatenate([dz_a, dz_b], axis=0), q_rows, TN,
                                                        preferred_element_type=F32)
            dv_ref[pl.ds(off, b), :] += lax.dot_general(jnp.concatenate([w_a, w_b], axis=0), do_rows, TN,
                                                        preferred_element_type=F32)
            return a_a, a_b, r_a, r_b, dq

        zero = jnp.zeros((b, 1), F32)
        carry = step(0, (zero, zero, d_a, d_b, jnp.zeros((b, LANES), F32)), tri)
        carry = step(1, carry, jnp.broadcast_to(qi > 0, tri.shape))
        carry = _sb_walk(qi, carry, lambda n, c: step(n, c, None))
        dq_ref[...] = carry[4]

    blk = pl.BlockSpec((b, LANES), lambda hp, i: (i, hp))
    full = pl.BlockSpec((s, LANES), lambda hp, i: (0, hp))
    full_v = pl.BlockSpec((s, LANES), lambda hp, i: (0, hp + v_col))
    grid = (width // LANES, nkb)
    s_in, s_out, s_shape, s_scratch, s_ops = _side_args(side)
    res = pl.pallas_call(
        _hosted(body, side, 5, 3, grid), name=name, grid=grid,
        in_specs=[blk, full, full_v, blk, blk] + s_in, out_specs=[blk, full, full] + s_out,
        out_shape=[jax.ShapeDtypeStruct((s, width), F32)] * 3 + s_shape,
        scratch_shapes=s_scratch,
        compiler_params=_params(("arbitrary", "arbitrary")),
    )(qs, ks, v, out, dout, *s_ops)
    return _split_side(res, 3, side)


def _cmul(xr, xi, yr, yi):
    return xr * yr - xi * yi, xr * yi + xi * yr


def _scan_consts(ar, ai, reverse, lc):
    rowi = lax.broadcasted_iota(jnp.int32, (SUBLANES, lc), 0)
    pows = [(ar, ai)]
    for _ in range(SUBLANES - 1):
        pows.append(_cmul(*pows[-1], ar, ai))
    steps = []
    for d in (1, 2, 4):
        keep = (rowi < SUBLANES - d) if reverse else (rowi >= d)
        pr, pi = pows[d - 1]
        steps.append((SUBLANES - d if reverse else d, jnp.where(keep, pr, 0.0), jnp.where(keep, pi, 0.0)))
    cr = jnp.zeros((SUBLANES, lc), F32)
    ci = jnp.zeros((SUBLANES, lc), F32)
    for r in range(SUBLANES):
        pr, pi = pows[SUBLANES - 1 - r] if reverse else pows[r]
        cr = jnp.where(rowi == r, pr, cr)
        ci = jnp.where(rowi == r, pi, ci)
    return steps, cr, ci


def _scan_tile(xr, xi, steps, pr, pi, cr, ci):
    for shift, ar, ai in steps:
        rr = pltpu.roll(xr, shift, 0)
        ri = pltpu.roll(xi, shift, 0)
        xr, xi = xr + ar * rr - ai * ri, xi + ar * ri + ai * rr
    return xr + pr * cr - pi * ci, xi + pr * ci + pi * cr


SCAN_ROWS = 1024


def _scan_chunk(s):
    tt = min(SCAN_ROWS, s)
    seg = tt // SUBLANES
    assert s % tt == 0 and seg % SUBLANES == 0 and seg & (seg - 1) == 0, s
    return tt, seg


def _to_segments(a):
    s, wd = a.shape
    tt, seg = _scan_chunk(s)
    return jnp.transpose(a.reshape(s // tt, SUBLANES, seg, wd), (0, 2, 1, 3)).reshape(s, wd)


def _from_segments(a):
    s, wd = a.shape
    tt, seg = _scan_chunk(s)
    return jnp.transpose(a.reshape(s // tt, seg, SUBLANES, wd), (0, 2, 1, 3)).reshape(s, wd)


def _cpow2(xr, xi, k):
    for _ in range(k):
        xr, xi = _cmul(xr, xi, xr, xi)
    return xr, xi


def _fill_powers(pw_ref, ar, ai, seg, lc):
    _, p8r, p8i = _scan_consts(ar, ai, False, lc)
    a8r, a8i = _cpow2(ar, ai, 3)
    qr, qi = jnp.ones_like(ar), jnp.zeros_like(ai)
    for k in range(seg // SUBLANES):
        tr, ti = _cmul(p8r, p8i, qr, qi)
        for r in range(SUBLANES):
            rows = pl.ds((SUBLANES * k + r) * SUBLANES, SUBLANES)
            pw_ref[rows, :lc] = jnp.broadcast_to(tr[r:r + 1, :], (SUBLANES, lc))
            pw_ref[rows, lc:] = jnp.broadcast_to(ti[r:r + 1, :], (SUBLANES, lc))
        qr, qi = _cmul(qr, qi, a8r, a8i)


def _ssm_fwd(u, acat, bsup, csup, d_skip, name, side=None):
    s = u.shape[0]
    lc = SCAN_LANES
    tt, seg = _scan_chunk(s)
    nl, nt = N_STATE // lc, s // tt
    tile = lambda j: pl.ds(pl.multiple_of(j * SUBLANES, SUBLANES), SUBLANES)

    def body(u_ref, a_ref, b_ref, c_ref, d_ref, s_ref, y0_ref, y1_ref, carry, pw_ref):
        ar, ai = a_ref[:, :lc], a_ref[:, lc:]

        @pl.when(pl.program_id(1) == 0)
        def _():
            carry[...] = jnp.zeros_like(carry)
            _fill_powers(pw_ref, ar, ai, seg, lc)

        ut = u_ref[...]
        s_ref[...] = _dot(ut, b_ref[0])

        ar8, ai8 = jnp.broadcast_to(ar, (SUBLANES, lc)), jnp.broadcast_to(ai, (SUBLANES, lc))

        def local(j, x):
            xr = ar8 * x[0] - ai8 * x[1] + s_ref[tile(j), :lc]
            xi = ar8 * x[1] + ai8 * x[0] + s_ref[tile(j), lc:]
            s_ref[tile(j), :lc] = xr
            s_ref[tile(j), lc:] = xi
            return xr, xi

        zero = jnp.zeros((SUBLANES, lc), F32)
        er, ei = lax.fori_loop(0, seg, local, (zero, zero))
        steps, pr, pi = _scan_consts(*_cpow2(ar, ai, seg.bit_length() - 1), False, lc)
        cr, ci = carry[:, :lc], carry[:, lc:]
        tr, ti = _scan_tile(er, ei, steps, pr, pi, cr, ci)
        rowi = lax.broadcasted_iota(jnp.int32, (SUBLANES, lc), 0)
        before_r = jnp.where(rowi == 0, cr, pltpu.roll(tr, 1, 0))
        before_i = jnp.where(rowi == 0, ci, pltpu.roll(ti, 1, 0))
        carry[:, :lc] = jnp.broadcast_to(tr[SUBLANES - 1:, :], (SUBLANES, lc))
        carry[:, lc:] = jnp.broadcast_to(ti[SUBLANES - 1:, :], (SUBLANES, lc))

        def fix(j, _):
            pwr, pwi = pw_ref[tile(j), :lc], pw_ref[tile(j), lc:]
            s_ref[tile(j), :lc] += pwr * before_r - pwi * before_i
            s_ref[tile(j), lc:] += pwr * before_i + pwi * before_r
            return 0

        lax.fori_loop(0, seg, fix, 0)
        y0 = _dot(s_ref[...], c_ref[0], NT) + d_ref[...] * ut
        y0_ref[...] = y0
        y1_ref[...] = jax.nn.gelu(y0)

    chan = pl.BlockSpec((tt, LANES), lambda j, c: (c, j))
    sup = pl.BlockSpec((1, LANES, 2 * lc), lambda j, c: (j, 0, 0))
    s_in, s_out, s_shape, s_scratch, s_ops = _side_args(side)
    res = pl.pallas_call(
        _hosted(body, side, 5, 3, (nl, nt)), name=name, grid=(nl, nt),
        in_specs=[chan, pl.BlockSpec((1, 2 * lc), lambda j, c: (0, j)), sup, sup,
                  pl.BlockSpec((1, LANES), lambda j, c: (0, j))] + s_in,
        out_specs=[pl.BlockSpec((tt, 2 * lc), lambda j, c: (c, j)), chan, chan] + s_out,
        out_shape=[jax.ShapeDtypeStruct((s, 2 * N_STATE), F32), jax.ShapeDtypeStruct((s, SSM_WIDTH), F32),
                   jax.ShapeDtypeStruct((s, SSM_WIDTH), F32)] + s_shape,
        scratch_shapes=[pltpu.VMEM((SUBLANES, 2 * lc), F32), pltpu.VMEM((seg * SUBLANES, 2 * lc), F32)] + s_scratch,
        compiler_params=_params(("arbitrary", "arbitrary")),
    )(u, acat, bsup, csup, d_skip, *s_ops)
    return _split_side(res, 3, side)


def _ssm_bwd(dy0, states, u, acat, bsup, csup, d_skip, name, side=None):
    s = u.shape[0]
    lc = SCAN_LANES
    tt, seg = _scan_chunk(s)
    nl, nt = N_STATE // lc, s // tt
    tile = lambda j: pl.ds(pl.multiple_of(j * SUBLANES, SUBLANES), SUBLANES)

    def body(dy_ref, s_ref, sp_ref, u_ref, a_ref, b_ref, c_ref, d_ref,
             du_ref, da_ref, db_ref, dc_ref, dd_ref, lam_ref, carry, pw_ref):
        c = pl.program_id(1)
        ar, ai = a_ref[:, :lc], a_ref[:, lc:]

        @pl.when(c == 0)
        def _():
            carry[...] = jnp.zeros_like(carry)
            for r in (da_ref, db_ref, dc_ref, dd_ref):
                r[...] = jnp.zeros_like(r)
            _fill_powers(pw_ref, ar, ai, seg, lc)

        dy = dy_ref[...]
        ut = u_ref[...]
        lam_ref[...] = _dot(dy, c_ref[0])

        ar8, ai8 = jnp.broadcast_to(ar, (SUBLANES, lc)), jnp.broadcast_to(ai, (SUBLANES, lc))

        def local(i, x):
            j = seg - 1 - i
            xr = ar8 * x[0] + ai8 * x[1] + lam_ref[tile(j), :lc]
            xi = ar8 * x[1] - ai8 * x[0] + lam_ref[tile(j), lc:]
            lam_ref[tile(j), :lc] = xr
            lam_ref[tile(j), lc:] = xi
            return xr, xi

        zero = jnp.zeros((SUBLANES, lc), F32)
        er, ei = lax.fori_loop(0, seg, local, (zero, zero))
        big_r, big_i = _cpow2(ar, ai, seg.bit_length() - 1)
        steps, pr, pi = _scan_consts(big_r, -big_i, True, lc)
        cr, ci = carry[:, :lc], carry[:, lc:]
        tr, ti = _scan_tile(er, ei, steps, pr, pi, cr, ci)
        rowi = lax.broadcasted_iota(jnp.int32, (SUBLANES, lc), 0)
        after_r = jnp.where(rowi == SUBLANES - 1, cr, pltpu.roll(tr, SUBLANES - 1, 0))
        after_i = jnp.where(rowi == SUBLANES - 1, ci, pltpu.roll(ti, SUBLANES - 1, 0))
        carry[:, :lc] = jnp.broadcast_to(tr[:1, :], (SUBLANES, lc))
        carry[:, lc:] = jnp.broadcast_to(ti[:1, :], (SUBLANES, lc))

        start = c != nt - 1
        last_r = jnp.where(start, jnp.broadcast_to(sp_ref[SUBLANES - 1:, :lc], (SUBLANES, lc)), 0.0)
        last_i = jnp.where(start, jnp.broadcast_to(sp_ref[SUBLANES - 1:, lc:], (SUBLANES, lc)), 0.0)
        first_r = jnp.where(rowi == 0, last_r, pltpu.roll(s_ref[tile(seg - 1), :lc], 1, 0))
        first_i = jnp.where(rowi == 0, last_i, pltpu.roll(s_ref[tile(seg - 1), lc:], 1, 0))

        def fix(j, acc):
            dar, dai = acc
            k = seg - 1 - j
            pwr, pwi = pw_ref[tile(k), :lc], pw_ref[tile(k), lc:]
            lr = lam_ref[tile(j), :lc] + pwr * after_r + pwi * after_i
            li = lam_ref[tile(j), lc:] + pwr * after_i - pwi * after_r
            lam_ref[tile(j), :lc] = lr
            lam_ref[tile(j), lc:] = li
            jp = jnp.maximum(j - 1, 0)
            sr = jnp.where(j > 0, s_ref[tile(jp), :lc], first_r)
            si = jnp.where(j > 0, s_ref[tile(jp), lc:], first_i)
            return dar + lr * sr + li * si, dai + li * sr - lr * si

        dar, dai = lax.fori_loop(0, seg, fix, (zero, zero))
        da_ref[:, :lc] += dar
        da_ref[:, lc:] += dai
        lam = lam_ref[...].astype(BF16)
        du_ref[...] = (_dot(lam, b_ref[0], NT) + d_ref[...] * dy).astype(du_ref.dtype)
        db_ref[0] += _dot(ut, lam, TN)
        dc_ref[0] += _dot(dy, s_ref[...], TN)
        dd_ref[...] += jnp.sum(dy * ut, axis=0, keepdims=True)

    rev = lambda j, c: (nt - 1 - c, j)
    chan = pl.BlockSpec((tt, LANES), rev)
    sup = pl.BlockSpec((1, LANES, 2 * lc), lambda j, c: (j, 0, 0))
    row = pl.BlockSpec((1, LANES), lambda j, c: (0, j))
    s_in, s_out, s_shape, s_scratch, s_ops = _side_args(side)
    res = pl.pallas_call(
        _hosted(body, side, 8, 5, (nl, nt)), name=name, grid=(nl, nt),
        in_specs=[chan, pl.BlockSpec((tt, 2 * lc), rev),
                  pl.BlockSpec((SUBLANES, 2 * lc), lambda j, c: (jnp.maximum((nt - 1 - c) * seg - 1, 0), j)),
                  chan, pl.BlockSpec((1, 2 * lc), lambda j, c: (0, j)), sup, sup, row] + s_in,
        out_specs=[chan, pl.BlockSpec((SUBLANES, 2 * lc), lambda j, c: (0, j)), sup, sup, row] + s_out,
        out_shape=[jax.ShapeDtypeStruct((s, SSM_WIDTH), BF16), jax.ShapeDtypeStruct((SUBLANES, 2 * N_STATE), F32),
                   jax.ShapeDtypeStruct(bsup.shape, F32), jax.ShapeDtypeStruct(csup.shape, F32),
                   jax.ShapeDtypeStruct((1, SSM_WIDTH), F32)] + s_shape,
        scratch_shapes=[pltpu.VMEM((tt, 2 * lc), F32), pltpu.VMEM((SUBLANES, 2 * lc), F32),
                        pltpu.VMEM((seg * SUBLANES, 2 * lc), F32)] + s_scratch,
        compiler_params=_params(("arbitrary", "arbitrary")),
    )(dy0, states, states, u, acat, bsup, csup, d_skip, *s_ops)
    return _split_side(res, 5, side)


def _state_cols(xr, xi):
    lead = xr.shape[:-1]
    nl = N_STATE // SCAN_LANES
    both = jnp.stack([xr.reshape(lead + (nl, SCAN_LANES)), xi.reshape(lead + (nl, SCAN_LANES))], axis=-2)
    return both.reshape(lead + (2 * N_STATE,))


def _ssm_mats(a_re, a_im, log_dt, b_re, b_im, c_re, c_im):
    dt = jnp.exp(log_dt)[:, None]
    lr, li = a_re * dt, a_im * dt
    e = jnp.exp(lr)
    abar_r, abar_i = e * jnp.cos(li), e * jnp.sin(li)
    den = a_re * a_re + a_im * a_im
    coef_r = ((abar_r - 1.0) * a_re + abar_i * a_im) / den
    coef_i = (abar_i * a_re - (abar_r - 1.0) * a_im) / den
    bbar_r = coef_r[..., None] * b_re - coef_i[..., None] * b_im
    bbar_i = coef_r[..., None] * b_im + coef_i[..., None] * b_re
    nl = N_STATE // SCAN_LANES
    gpb = SSM_GROUPS // nl
    eye = jnp.eye(gpb, dtype=bool)[None, :, None, :, None]

    def sup(m_r, m_i):
        def one(m):
            m = m.reshape(nl, gpb, SSM_GROUP, 1, SSM_STATE)
            return jnp.where(eye, m, 0.0).reshape(nl, gpb * SSM_GROUP, SCAN_LANES)
        return jnp.concatenate([one(m_r), one(m_i)], axis=-1)

    acat = _state_cols(abar_r.reshape(1, N_STATE), abar_i.reshape(1, N_STATE))
    bsup = sup(jnp.transpose(bbar_r, (0, 2, 1)), jnp.transpose(bbar_i, (0, 2, 1)))
    csup = sup(c_re, -c_im)
    return acat, bsup, csup


def _mem_fwd(mem, g_mem, w_kv, g_k, name):
    ml = mem.shape[0]

    def body(mem_ref, gm_ref, w_ref, gk_ref, memn_ref, kv_ref, kn_ref, vv_ref):
        memn = _rms(mem_ref[...], gm_ref[...])
        memn_ref[...] = memn.astype(BF16)
        kv = _dot(memn, w_ref[...])
        kv_ref[...] = kv
        for hh in range(XA_HEADS):
            sl = slice(hh * XA_HEAD_DIM, (hh + 1) * XA_HEAD_DIM)
            kn_ref[:, sl] = _rms(kv[:, sl], gk_ref[...]).astype(BF16)
        vv_ref[...] = kv[:, XA_WIDTH:].astype(BF16)

    return pl.pallas_call(
        body, name=name,
        out_shape=[jax.ShapeDtypeStruct((ml, D_MODEL), BF16), jax.ShapeDtypeStruct((ml, 2 * XA_WIDTH), F32),
                   jax.ShapeDtypeStruct((ml, XA_WIDTH), BF16), jax.ShapeDtypeStruct((ml, XA_WIDTH), BF16)],
        compiler_params=_params(),
    )(mem, g_mem, w_kv, g_k)


def _mem_bwd(mem, g_mem, memn, w_kv, kv, g_k, dkn, dvv, name):
    def body(mem_ref, gm_ref, memn_ref, w_ref, kv_ref, gk_ref, dkn_ref, dvv_ref, dw_ref, dgm_ref, dgk_ref):
        kv = kv_ref[...]
        dgk = jnp.zeros(dgk_ref.shape, F32)
        parts = []
        for hh in range(XA_HEADS):
            sl = slice(hh * XA_HEAD_DIM, (hh + 1) * XA_HEAD_DIM)
            _, vjp = jax.vjp(_rms, kv[:, sl], gk_ref[...])
            dk, dg = vjp(dkn_ref[:, sl])
            parts.append(dk)
            dgk = dgk + dg
        dgk_ref[...] = dgk
        dkv = jnp.concatenate(parts + [dvv_ref[...]], axis=1)
        dw_ref[...] = _dot(memn_ref[...], dkv, TN)
        dmemn = _dot(dkv, w_ref[...], NT)
        _, vjp = jax.vjp(_rms, mem_ref[...], gm_ref[...])
        dgm_ref[...] = vjp(dmemn)[1]

    return pl.pallas_call(
        body, name=name,
        out_shape=[jax.ShapeDtypeStruct((D_MODEL, 2 * XA_WIDTH), F32), jax.ShapeDtypeStruct(g_mem.shape, F32),
                   jax.ShapeDtypeStruct(g_k.shape, F32)],
        compiler_params=_params(),
    )(mem, g_mem, memn, w_kv, kv, g_k, dkn, dvv)


def _xa_head(qx_h, g_q, kn_h, vv_h):
    qn = _rms(qx_h, g_q)
    sc = _dot(qn, kn_h, NT) * (XA_HEAD_DIM ** -0.5)
    sc = sc - jnp.max(sc, axis=-1, keepdims=True)
    e = jnp.exp(sc)
    p = e / jnp.sum(e, axis=-1, keepdims=True)
    return qn, p


def _xa_fwd(qx, g_q, kn, vv, name):
    def fn(qt, gq, knt, vvt):
        outs = []
        for hh in range(XA_HEADS):
            sl = slice(hh * XA_HEAD_DIM, (hh + 1) * XA_HEAD_DIM)
            _, p = _xa_head(qt[:, sl], gq, knt[:, sl], vvt[:, sl])
            outs.append(_dot(p, vvt[:, sl]))
        return (jnp.concatenate(outs, axis=1),), ()

    return _rw(fn, [qx], [g_q, kn, vv], [(XA_WIDTH, BF16)], [], name)[0]


def _xa_bwd(qx, g_q, kn, vv, do, name):
    def fn(qt, dot_, gq, knt, vvt):
        dqs, dks, dvs = [], [], []
        dgq = jnp.zeros_like(gq)
        for hh in range(XA_HEADS):
            sl = slice(hh * XA_HEAD_DIM, (hh + 1) * XA_HEAD_DIM)
            qn, p = _xa_head(qt[:, sl], gq, knt[:, sl], vvt[:, sl])
            doh = dot_[:, sl]
            dp = _dot(doh, vvt[:, sl], NT)
            dvs.append(_dot(p, doh, TN))
            ds = p * (dp - jnp.sum(dp * p, axis=-1, keepdims=True)) * (XA_HEAD_DIM ** -0.5)
            dqn = _dot(ds, knt[:, sl])
            dks.append(_dot(ds, qn, TN))
            _, vjp = jax.vjp(_rms, qt[:, sl], gq)
            dq, dg = vjp(dqn)
            dqs.append(dq)
            dgq = dgq + dg
        return ((jnp.concatenate(dqs, axis=1),),
                (jnp.concatenate(dks, axis=1), jnp.concatenate(dvs, axis=1), dgq))

    return _rw(fn, [qx, do], [g_q, kn, vv], [(XA_WIDTH, BF16)], [kn.shape, vv.shape, g_q.shape], name)


BIG = [
    ("w_in", (D_MODEL, IN_WIDTH), 1), ("ssm_w_glu", (SSM_WIDTH, SSM_WIDTH), 0), ("w_out", (D_MODEL, D_MODEL), 0),
    ("xa_w_q", (D_MODEL, XA_WIDTH), 0), ("xa_w_kv", (D_MODEL, 2 * XA_WIDTH), 0), ("xa_w_o", (XA_WIDTH, D_MODEL), 1),
    ("w_up", (D_MODEL, D_FF), 1), ("w_down", (D_FF, D_MODEL), 0),
]
BIG_INDEX = {n: i for i, (n, _, _) in enumerate(BIG)}


def _shard_shape(shape, axis):
    return tuple(d // N_DEV if i == axis else d for i, d in enumerate(shape))


def _shard_of(ref, axis, d):
    n = ref.shape[axis] // N_DEV
    return ref.at[pl.ds(d * n, n), :] if axis == 0 else ref.at[:, pl.ds(d * n, n)]


def _gather_side(names, shards):
    idxs = [BIG_INDEX[n] for n in names]

    def make(ins, outs, send_sems, recv_sems):
        x, y, c = lax.axis_index("x"), lax.axis_index("y"), lax.axis_index("c")
        cps = []
        for j, i in enumerate(idxs):
            mine = _shard_of(outs[j], BIG[i][2], 4 * x + 2 * y + c)
            cps.append(pltpu.make_async_copy(ins[j], mine, send_sems.at[N_DEV * j]))
            for rel in range(1, N_DEV):
                to = tuple(1 - p if rel >> bit & 1 else p for p, bit in ((x, 2), (y, 1), (c, 0)))
                cps.append(pltpu.make_async_remote_copy(
                    src_ref=ins[j], dst_ref=mine, send_sem=send_sems.at[N_DEV * j + rel],
                    recv_sem=recv_sems.at[N_DEV * j + rel], device_id=to, device_id_type=MESH))
        return cps

    return _Side(shards, [jax.ShapeDtypeStruct(BIG[i][1], BF16) for i in idxs], N_DEV * len(idxs), make)


def _gather_two_level_side(name, shard):
    i = BIG_INDEX[name]

    def parts(ins, outs, send_sems, recv_sems):
        x, y, c = lax.axis_index("x"), lax.axis_index("y"), lax.axis_index("c")
        sibling = (x, y, 1 - c)
        chips = [(1 - x, y), (x, 1 - y), (1 - x, 1 - y)]

        def place(dev):
            return _shard_of(outs[0], BIG[i][2], 4 * dev[0] + 2 * dev[1] + dev[2])

        def copy(k, blk, to, src=None):
            return pltpu.make_async_remote_copy(
                src_ref=place(blk) if src is None else src, dst_ref=place(blk), send_sem=send_sems.at[k],
                recv_sem=recv_sems.at[k], device_id=to, device_id_type=MESH)

        mine = pltpu.make_async_copy(ins[0], place((x, y, c)), send_sems.at[7])
        first = [copy(0, (x, y, c), sibling, src=ins[0])]
        first += [copy(1 + j, (x, y, c), (*chip, c), src=ins[0]) for j, chip in enumerate(chips)]
        passed = [copy(4 + j, (*chip, c), sibling) for j, chip in enumerate(chips)]
        arrived = [copy(1 + j, (*chip, c), (x, y, c)) for j, chip in enumerate(chips)]
        from_sibling = [copy(0, sibling, (x, y, c))] + [copy(4 + j, (*chip, 1 - c), (x, y, c))
                                                       for j, chip in enumerate(chips)]
        return mine, first, passed, arrived, from_sibling

    def make(ins, outs, send_sems, recv_sems):
        mine, first, _, _, _ = parts(ins, outs, send_sems, recv_sems)
        return [mine] + first

    def finish(ins, outs, send_sems, recv_sems):
        mine, first, passed, arrived, from_sibling = parts(ins, outs, send_sems, recv_sems)
        for got, onward in zip(arrived, passed):
            got.wait_recv()
            onward.start()
        for cp in from_sibling:
            cp.wait_recv()
        for cp in first + passed:
            cp.wait_send()
        mine.wait()

    return _Side([shard], [jax.ShapeDtypeStruct(BIG[i][1], BF16)], N_DEV, make, finish)


def _sibling_side(names, grads):
    idxs = [BIG_INDEX[n] for n in names]

    def make(ins, outs, send_sems, recv_sems):
        x, y, c = lax.axis_index("x"), lax.axis_index("y"), lax.axis_index("c")
        return [pltpu.make_async_remote_copy(
            src_ref=_shard_of(ins[j], BIG[i][2], 2 * k + (1 - c)), dst_ref=outs[j].at[k],
            send_sem=send_sems.at[4 * j + k], recv_sem=recv_sems.at[4 * j + k], device_id=(x, y, 1 - c),
            device_id_type=MESH) for j, i in enumerate(idxs) for k in range(4)]

    shapes = [jax.ShapeDtypeStruct((4,) + _shard_shape(BIG[i][1], BIG[i][2]), F32) for i in idxs]
    return _Side(grads, shapes, 4 * len(idxs), make)


def _chips_side(parts):
    def make(ins, outs, send_sems, recv_sems):
        x, y, c = lax.axis_index("x"), lax.axis_index("y"), lax.axis_index("c")
        chips = [(1 - x, y), (x, 1 - y), (1 - x, 1 - y)]
        return [pltpu.make_async_remote_copy(
            src_ref=ins[j].at[2 * cx + cy], dst_ref=outs[j].at[r], send_sem=send_sems.at[3 * j + r],
            recv_sem=recv_sems.at[3 * j + r], device_id=(cx, cy, c), device_id_type=MESH)
            for r, (cx, cy) in enumerate(chips) for j in range(len(parts))]

    return _Side(parts, [jax.ShapeDtypeStruct((3,) + p.shape[1:], p.dtype) for p in parts], 3 * len(parts), make)


def _reduce_add(grad, recv, axis, core, name):
    rs, cs = recv.shape[1:]
    rt = _row_tile(rs, 256)
    nt = rs // rt

    def body(c_ref, g_ref, r_ref, p_ref, pb_ref):
        sm = g_ref[...] + r_ref[0]
        p_ref[0] = sm
        pb_ref[0] = sm.astype(BF16)

    if axis == 0:
        g_spec = pl.BlockSpec((rt, cs), lambda k, t, c_ref: ((2 * k + c_ref[0]) * nt + t, 0))
    else:
        g_spec = pl.BlockSpec((rt, cs), lambda k, t, c_ref: (t, 2 * k + c_ref[0]))
    slab = pl.BlockSpec((1, rt, cs), lambda k, t, c_ref: (k, t, 0))
    return pl.pallas_call(
        body, name=name,
        grid_spec=pltpu.PrefetchScalarGridSpec(num_scalar_prefetch=1, grid=(4, nt), in_specs=[g_spec, slab],
                                               out_specs=[slab, slab]),
        out_shape=[jax.ShapeDtypeStruct(recv.shape, F32), jax.ShapeDtypeStruct(recv.shape, BF16)],
        compiler_params=_params(("parallel", "parallel")),
    )(core, grad, recv)


def _all_gather(block, name, side):
    m_per, n = block.shape
    ns_in, ns_out = len(side.ins), len(side.out_shapes)

    def body(*refs):
        x_ref, s_ins, out_ref = refs[0], refs[1:1 + ns_in], refs[1 + ns_in]
        s_outs = refs[2 + ns_in:2 + ns_in + ns_out]
        send_sems, recv_sems, s_send, s_recv = refs[2 + ns_in + ns_out:]
        x, y, c = lax.axis_index("x"), lax.axis_index("y"), lax.axis_index("c")
        mine = out_ref.at[pl.ds((4 * x + 2 * y + c) * m_per, m_per), :]
        cps = side.make(s_ins, s_outs, s_send, s_recv) + [pltpu.make_async_copy(x_ref, mine, send_sems.at[0])]
        for rel in range(1, N_DEV):
            to = tuple(1 - p if rel >> bit & 1 else p for p, bit in ((x, 2), (y, 1), (c, 0)))
            cps.append(pltpu.make_async_remote_copy(
                src_ref=x_ref, dst_ref=mine, send_sem=send_sems.at[rel], recv_sem=recv_sems.at[rel],
                device_id=to, device_id_type=MESH))
        for cp in cps:
            cp.start()
        for cp in cps:
            cp.wait()

    res = pl.pallas_call(
        body, name=name, in_specs=[ANY] * (1 + ns_in), out_specs=[ANY] * (1 + ns_out),
        out_shape=[jax.ShapeDtypeStruct((N_DEV * m_per, n), block.dtype)] + side.out_shapes,
        scratch_shapes=[pltpu.SemaphoreType.DMA((N_DEV,)), pltpu.SemaphoreType.DMA((N_DEV,))] + side.sems(),
    )(block, *side.ins)
    return res[0], list(res[1:])


def _adam_math(w, g, m, v):
    m = ADAM_B1 * m + (1.0 - ADAM_B1) * g
    v = ADAM_B2 * v + (1.0 - ADAM_B2) * (g * g)
    m_hat = m / (1.0 - ADAM_B1 ** ADAM_STEP)
    v_hat = v / (1.0 - ADAM_B2 ** ADAM_STEP)
    delta = -ADAM_LR * (m_hat / (jnp.sqrt(v_hat) + ADAM_EPS) + ADAM_WD * w)
    return delta, m, v


def _adam_sharded(own, recv, w, m, v, chip, name):
    rs, cs = w.shape
    rt = _row_tile(rs, 256)

    def body(chip_ref, p_ref, r_ref, w_ref, m_ref, v_ref, g_out, d_out, m_out, v_out):
        g = p_ref[0] + r_ref[0].astype(F32) + r_ref[1].astype(F32) + r_ref[2].astype(F32)
        d, mn, vn = _adam_math(w_ref[...], g, m_ref[...], v_ref[...])
        g_out[...] = g
        d_out[...] = d
        m_out[...] = mn
        v_out[...] = vn

    tile = pl.BlockSpec((rt, cs), lambda t, chip_ref: (t, 0))
    return pl.pallas_call(
        body, name=name,
        grid_spec=pltpu.PrefetchScalarGridSpec(
            num_scalar_prefetch=1, grid=(rs // rt,),
            in_specs=[pl.BlockSpec((1, rt, cs), lambda t, chip_ref: (chip_ref[0], t, 0)),
                      pl.BlockSpec((3, rt, cs), lambda t, chip_ref: (0, t, 0)), tile, tile, tile],
            out_specs=[tile] * 4),
        out_shape=[jax.ShapeDtypeStruct((rs, cs), F32)] * 4,
        compiler_params=_params(("parallel",)),
    )(chip, own, recv, w, m, v)


SMALL = ["g_mix", "ssm_a_re", "ssm_a_im", "ssm_log_dt", "ssm_b_re", "ssm_b_im", "ssm_c_re", "ssm_c_im", "ssm_d",
         "sb_g_q", "sb_g_k", "g_out_ssm", "g_out_sb", "g_xa", "g_mem", "xa_g_q", "xa_g_k", "g_mlp"]
PACK_TILE = SUBLANES * LANES


def _natural_2d(n):
    return (n // LANES, LANES) if n % LANES == 0 else (1, n)


def _pack_small(arrs):
    parts = []
    for a in arrs:
        flat = a.reshape(-1)
        parts.append(jnp.pad(flat, (0, (-flat.shape[0]) % PACK_TILE)))
    return jnp.concatenate(parts).reshape(-1, LANES)


def _adam_replicated(gathered, sizes, ws, ms, vs, name):
    n_w = len(ws)
    r_dev = gathered.shape[0] // N_DEV
    offs, off = [], 0
    for n in sizes:
        offs.append(off)
        off += (n + PACK_TILE - 1) // PACK_TILE * SUBLANES
    assert off == r_dev

    def body(*refs):
        g_ref = refs[0]
        w_refs, m_refs, v_refs = refs[1:1 + n_w], refs[1 + n_w:1 + 2 * n_w], refs[1 + 2 * n_w:1 + 3 * n_w]
        outs = refs[1 + 3 * n_w:]

        def total(i, shape):
            r, cdim = shape
            acc = g_ref[pl.ds(offs[i], r), :cdim]
            for d in range(1, N_DEV):
                acc = acc + g_ref[pl.ds(d * r_dev + offs[i], r), :cdim]
            return acc

        for i in range(n_w):
            g = total(i, w_refs[i].shape)
            d, mn, vn = _adam_math(w_refs[i][...], g, m_refs[i][...], v_refs[i][...])
            for o, val in zip(outs[4 * i:4 * i + 4], (g, d, mn, vn)):
                o[...] = val
        outs[4 * n_w][...] = total(n_w, (SUBLANES, LANES))

    shapes = [w.shape for w in ws]
    return pl.pallas_call(
        body, name=name,
        out_shape=[jax.ShapeDtypeStruct(shp, F32) for shp in shapes for _ in range(4)]
        + [jax.ShapeDtypeStruct((SUBLANES, LANES), F32)],
        compiler_params=_params(),
    )(gathered, *ws, *ms, *vs)


def _step(x, mem, target, shards, sm, core):
    g, w, sums, reduced = {}, {}, {}, {}

    def gather(names):
        return _gather_side(names, [shards[n] for n in names])

    def to_sibling(names):
        return _sibling_side(names, [g[n] for n in names])

    def add_sibling(names, received):
        for n, r in zip(names, received):
            sums[n] = _reduce_add(g[n], r, BIG[BIG_INDEX[n]][2], core, "reduce_add_" + n)

    def to_chips(names):
        return _chips_side([sums[n][1] for n in names])

    def keep(names, received):
        for n, r in zip(names, received):
            reduced[n] = (sums[n][0], r)

    row = lambda a: a.reshape(1, -1)
    g_mix, g_xa, g_mlp, g_mem = row(sm["g_mix"]), row(sm["g_xa"]), row(sm["g_mlp"]), row(sm["g_mem"])
    g_os, g_ob = row(sm["g_out_ssm"]), row(sm["g_out_sb"])
    sb_gq, sb_gk = jnp.tile(row(sm["sb_g_q"]), (1, SB_HEADS)), jnp.tile(row(sm["sb_g_k"]), (1, SB_HEADS))
    xa_gq, xa_gk = row(sm["xa_g_q"]), row(sm["xa_g_k"])
    d_skip = row(sm["ssm_d"])

    h1, (w["w_in"],) = _norm_fwd(x, g_mix, "norm_mix", side=_gather_two_level_side("w_in", shards["w_in"]))
    proj = _mm(h1, w["w_in"], "nn", "in_proj")
    u = _to_segments(proj[:, :SSM_WIDTH])
    q_raw, k_raw = (proj, SB_WIDTH, 1), (proj, SB_WIDTH, 2)
    v_col = (SSM_WIDTH + 2 * SB_WIDTH) // LANES
    sb_scale = SB_HEAD_DIM ** -0.5
    qs, ks = _rw(lambda qt, kt, gq, gk: ((_rms_groups(qt, gq, sb_scale), _rms_groups(kt, gk, 1.0)), ()),
                 [q_raw, k_raw], [sb_gq, sb_gk], [(SB_WIDTH, BF16)] * 2, [], "sb_qk_norm")
    early = ["ssm_w_glu", "w_out", "xa_w_q", "xa_w_kv", "xa_w_o", "w_up"]
    y_sb, got = _sb_fwd(qs, ks, proj, "sb_fwd", v_col=v_col, side=gather(early))
    w.update(zip(early, got))

    ssm_args = (sm["ssm_a_re"], sm["ssm_a_im"], sm["ssm_log_dt"], sm["ssm_b_re"], sm["ssm_b_im"],
                sm["ssm_c_re"], sm["ssm_c_im"])
    (acat, bsup, csup), mats_vjp = jax.vjp(_ssm_mats, *ssm_args)
    (states, y0, y1), (w["w_down"],) = _ssm_fwd(u, acat, bsup, csup, d_skip, "ssm_fwd", side=gather(["w_down"]))
    z_glu, y_ssm = _mm(y1, w["ssm_w_glu"], "nn", "ssm_glu", epi=lambda r, yt: (r, yt * jax.nn.sigmoid(r)),
                       extras=(y1,), out_dtypes=(F32, F32))
    y_ssm = _from_segments(y_ssm)

    def cat_norm(a, b, ga, gb):
        return jnp.concatenate([_rms(a, ga), _rms(b, gb)], axis=1)

    ycat = _rw(lambda a, b, ga, gb: ((cat_norm(a, b, ga, gb),), ()), [y_ssm, y_sb], [g_os, g_ob],
               [(D_MODEL, BF16)], [], "norm_out")[0]

    def residual_norm_epi(r, xt, gt):
        xn = r + xt
        return xn, _rms(xn, gt)

    x1, h2 = _mm(ycat, w["w_out"], "nn", "out_proj", epi=residual_norm_epi, extras=(x,), fulls=(g_xa,),
                 out_dtypes=(F32, BF16))
    qx = _mm(h2, w["xa_w_q"], "nn", "xa_q")
    memn, kv, kn_x, vv_x = _mem_fwd(mem, g_mem, w["xa_w_kv"], xa_gk, "xa_mem")
    o_xa = _xa_fwd(qx, xa_gq, kn_x, vv_x, "xa_fwd")
    x2, h3 = _mm(o_xa, w["xa_w_o"], "nn", "xa_o", epi=residual_norm_epi, extras=(x1,), fulls=(g_mlp,),
                 out_dtypes=(F32, BF16))

    def up_epi(r):
        rl = jnp.maximum(r, 0.0)
        return (rl * rl,)

    r_up = _mm(h3, w["w_up"], "nn", "mlp_up", epi=up_epi, out_dtypes=(BF16,))

    def loss_epi(r, xt, tt):
        d = r + xt - tt
        return (d * (1.0 / D_MODEL),) * 2, (jnp.sum(d * d, axis=0, keepdims=True),)

    dx3, dx3_b, sq = _mm(r_up, w["w_down"], "nn", "mlp_down", epi=loss_epi, extras=(x2, target),
                         out_dtypes=(F32, BF16), sums=[(1, D_MODEL)])
    loss = jnp.sum(sq) * (0.5 / D_MODEL)

    def norm_bwd_epi(r, xt, drt, gt):
        _, vjp = jax.vjp(_rms, xt, gt)
        dx_, dg_ = vjp(r)
        return (dx_ + drt,) * 2, (dg_,)

    g["w_down"] = _mm(r_up, dx3_b, "tn", "d_w_down", tk=2048)
    da = _mm(dx3_b, w["w_down"], "nt", "d_r", epi=lambda r, rt: (r * 2.0 * jnp.sqrt(rt.astype(F32)),), extras=(r_up,),
             out_dtypes=(BF16,))
    g["w_up"] = _mm(h3, da, "tn", "d_w_up", tk=2048)
    mlp = ["w_down", "w_up"]
    (dx2, dx2_b, g["g_mlp"]), got = _mm(da, w["w_up"], "nt", "d_h3", epi=norm_bwd_epi, extras=(x2, dx3),
                                        fulls=(g_mlp,), out_dtypes=(F32, BF16), sums=[g_mlp.shape],
                                        side=to_sibling(mlp))
    add_sibling(mlp, got)
    g["xa_w_o"] = _mm(o_xa, dx2_b, "tn", "d_xa_w_o", tk=2048)
    do_xa = _mm(dx2_b, w["xa_w_o"], "nt", "d_o_xa")
    dqx, dkn_x, dvv_x, g["xa_g_q"] = _xa_bwd(qx, xa_gq, kn_x, vv_x, do_xa, "xa_bwd")
    g["xa_w_kv"], g["g_mem"], g["xa_g_k"] = _mem_bwd(mem, g_mem, memn, w["xa_w_kv"], kv, xa_gk, dkn_x, dvv_x,
                                                     "xa_mem_bwd")
    g["xa_w_q"] = _mm(h2, dqx, "tn", "d_xa_w_q", tk=2048)
    dx1, dx1_b, g["g_xa"] = _mm(dqx, w["xa_w_q"], "nt", "d_h2", epi=norm_bwd_epi, extras=(x1, dx2), fulls=(g_xa,),
                                out_dtypes=(F32, BF16), sums=[g_xa.shape])
    g["w_out"] = _mm(ycat, dx1_b, "tn", "d_w_out", tk=2048)
    dycat = _mm(dx1_b, w["w_out"], "nt", "d_ycat")

    def cat_bwd(a, b, dy, ga, gb):
        _, vjp = jax.vjp(cat_norm, a, b, ga, gb)
        da_, db_, dga, dgb = vjp(dy)
        return (da_, db_), (dga, dgb)

    dy_ssm, dy_sb, g["g_out_ssm"], g["g_out_sb"] = _rw(
        cat_bwd, [y_ssm, y_sb, dycat], [g_os, g_ob], [(SSM_WIDTH, F32), (SB_WIDTH, F32)], [g_os.shape, g_ob.shape],
        "d_norm_out")

    def glu_bwd(dy, yt, zt):
        sg = jax.nn.sigmoid(zt)
        return (dy * sg, dy * yt * sg * (1.0 - sg)), ()

    dy1_a, dz = _rw(glu_bwd, [_to_segments(dy_ssm), y1, z_glu], [], [(SSM_WIDTH, F32), (SSM_WIDTH, BF16)], [], "d_glu")
    g["ssm_w_glu"] = _mm(y1, dz, "tn", "d_w_glu", tk=2048)

    def gelu_bwd_epi(r, da_, y0t):
        _, vjp = jax.vjp(jax.nn.gelu, y0t)
        return (vjp(r + da_)[0],)

    mid = ["w_out", "xa_w_q", "xa_w_kv", "xa_w_o", "ssm_w_glu"]
    dy0, got = _mm(dz, w["ssm_w_glu"], "nt", "d_y1", epi=gelu_bwd_epi, extras=(dy1_a, y0), side=to_sibling(mid))
    add_sibling(mid, got)
    (du, da8, d_bsup, d_csup, g["ssm_d"]), got = _ssm_bwd(dy0, states, u, acat, bsup, csup, d_skip, "ssm_bwd",
                                                          side=to_chips(mlp))
    keep(mlp, got)
    d_acat = jnp.sum(da8, axis=0, keepdims=True)
    for nm, val in zip(("ssm_a_re", "ssm_a_im", "ssm_log_dt", "ssm_b_re", "ssm_b_im", "ssm_c_re", "ssm_c_im"),
                       mats_vjp((d_acat, d_bsup, d_csup))):
        g[nm] = val

    (dqs, dks, dvs), got = _sb_bwd(qs, ks, proj, y_sb, dy_sb, "sb_bwd", v_col=v_col, side=to_chips(mid))
    keep(mid, got)

    def d_proj_rows(du_t, qt, dqt, kt, dkt, dvt, gq, gk):
        _, vjp_q = jax.vjp(lambda a, b_: _rms_groups(a, b_, sb_scale), qt, gq)
        _, vjp_k = jax.vjp(lambda a, b_: _rms_groups(a, b_, 1.0), kt, gk)
        (dq_, dgq_), (dk_, dgk_) = vjp_q(dqt), vjp_k(dkt)
        rows = jnp.concatenate([du_t, dq_.astype(BF16), dk_.astype(BF16), dvt.astype(BF16)], axis=1)
        return (rows,), (dgq_, dgk_)

    dproj, dgq, dgk = _rw(d_proj_rows, [_from_segments(du), q_raw, dqs, k_raw, dks, dvs], [sb_gq, sb_gk],
                          [(IN_WIDTH, BF16)], [sb_gq.shape, sb_gk.shape], "d_proj")
    g["sb_g_q"] = jnp.sum(dgq.reshape(SB_HEADS, SB_HEAD_DIM), axis=0)
    g["sb_g_k"] = jnp.sum(dgk.reshape(SB_HEADS, SB_HEAD_DIM), axis=0)
    g["w_in"] = _mm(h1, dproj, "tn", "d_w_in", tk=2048)
    dh1, got = _mm(dproj, w["w_in"], "nt", "d_h1", side=to_sibling(["w_in"]))
    add_sibling(["w_in"], got)
    dx, g["g_mix"] = _norm_bwd(x, g_mix, dh1, dx1, "d_norm_mix")

    packed = _pack_small([g[n] for n in SMALL] + [loss.reshape(1)])
    everyone, got = _all_gather(packed, "gather_small", to_chips(["w_in"]))
    keep(["w_in"], got)
    return dx, everyone, reduced


def kernel(x, mem, g_mix, w_in, ssm_a_re, ssm_a_im, ssm_log_dt, ssm_b_re, ssm_b_im, ssm_c_re, ssm_c_im, ssm_d, ssm_w_glu, sb_g_q, sb_g_k, g_out_ssm, g_out_sb, w_out, g_xa, g_mem, xa_w_q, xa_w_kv, xa_g_q, xa_g_k, xa_w_o, g_mlp, w_up, w_down, loss_target, m_g_mix, m_w_in, m_ssm_a_re, m_ssm_a_im, m_ssm_log_dt, m_ssm_b_re, m_ssm_b_im, m_ssm_c_re, m_ssm_c_im, m_ssm_d, m_ssm_w_glu, m_sb_g_q, m_sb_g_k, m_g_out_ssm, m_g_out_sb, m_w_out, m_g_xa, m_g_mem, m_xa_w_q, m_xa_w_kv, m_xa_g_q, m_xa_g_k, m_xa_w_o, m_g_mlp, m_w_up, m_w_down, v_g_mix, v_w_in, v_ssm_a_re, v_ssm_a_im, v_ssm_log_dt, v_ssm_b_re, v_ssm_b_im, v_ssm_c_re, v_ssm_c_im, v_ssm_d, v_ssm_w_glu, v_sb_g_q, v_sb_g_k, v_g_out_ssm, v_g_out_sb, v_w_out, v_g_xa, v_g_mem, v_xa_w_q, v_xa_w_kv, v_xa_g_q, v_xa_g_k, v_xa_w_o, v_g_mlp, v_w_up, v_w_down):
    given = dict(locals())
    order = ["g_mix", "w_in", "ssm_a_re", "ssm_a_im", "ssm_log_dt", "ssm_b_re", "ssm_b_im", "ssm_c_re", "ssm_c_im",
             "ssm_d", "ssm_w_glu", "sb_g_q", "sb_g_k", "g_out_ssm", "g_out_sb", "w_out", "g_xa", "g_mem", "xa_w_q",
             "xa_w_kv", "xa_g_q", "xa_g_k", "xa_w_o", "g_mlp", "w_up", "w_down"]
    assert sorted([n for n, _, _ in BIG] + SMALL) == sorted(order)
    core = lax.axis_index("c").astype(jnp.int32).reshape(1)
    chip = (2 * lax.axis_index("x") + lax.axis_index("y")).astype(jnp.int32).reshape(1)

    shards = {n: given[n][0].astype(BF16) for n, _, _ in BIG}
    sm = {n: given[n][0] for n in SMALL}
    dx, everyone, reduced = _step(x[0], mem[0], loss_target[0], shards, sm, core)

    res = {}
    for n, _, _ in BIG:
        own, recv = reduced[n]
        outs = _adam_sharded(own, recv, given[n][0], given["m_" + n][0], given["v_" + n][0], chip, "adam_" + n)
        for kind, val in zip(("grad", "delta", "new_m", "new_v"), outs):
            res[kind + "_" + n] = val[None]

    sizes = [math.prod(sm[n].shape) for n in SMALL] + [1]
    nat = lambda a: a.reshape(_natural_2d(math.prod(a.shape)))
    outs = _adam_replicated(everyone, sizes, [nat(sm[n]) for n in SMALL], [nat(given["m_" + n][0]) for n in SMALL],
                            [nat(given["v_" + n][0]) for n in SMALL], "adam_replicated")
    for i, n in enumerate(SMALL):
        for kind, val in zip(("grad", "delta", "new_m", "new_v"), outs[4 * i:4 * i + 4]):
            res[kind + "_" + n] = val.reshape(given[n].shape)
    loss_out = outs[-1][0, 0]
    return (loss_out, dx[None], *[res["grad_" + n] for n in order], *[res["delta_" + n] for n in order],
            *[res["new_m_" + n] for n in order], *[res["new_v_" + n] for n in order])
```

```python
import functools
import math

import jax
import jax.numpy as jnp
from jax import lax
from jax.experimental import pallas as pl
from jax.experimental.pallas import tpu as pltpu

F32 = jnp.float32
BF16 = jnp.bfloat16
MESH = pl.DeviceIdType.MESH

N_DEV = 8
D_MODEL = 1024
SSM_WIDTH = 512
SSM_GROUP = 16
SSM_GROUPS = 32
SSM_STATE = 64
N_STATE = SSM_GROUPS * SSM_STATE
SB_HEADS = 8
SB_HEAD_DIM = 64
SB_WIDTH = 512
IN_WIDTH = 2048
XA_HEADS = 4
XA_HEAD_DIM = 128
XA_WIDTH = 512
D_FF = 4096
NORM_EPS = 1e-6
ADAM_LR = 0.001
ADAM_B1 = 0.9
ADAM_B2 = 0.999
ADAM_EPS = 1e-08
ADAM_WD = 0.01
ADAM_STEP = 10

LANES = 128
SUBLANES = 8
VMEM_LIMIT = 56 * 1024 * 1024
SCAN_LANES = 512
SB_BLOCK = 256
SB_UNDERFLOW = -110.0

NN = (((1,), (0,)), ((), ()))
NT = (((1,), (1,)), ((), ()))
TN = (((0,), (0,)), ((), ()))


def _params(sem=None):
    return pltpu.CompilerParams(dimension_semantics=sem, vmem_limit_bytes=VMEM_LIMIT)


def _dot(a, b, dims=NN):
    return lax.dot_general(a.astype(BF16), b.astype(BF16), dims, preferred_element_type=F32)


def _rms(x, g):
    return x * lax.rsqrt(jnp.mean(x * x, axis=-1, keepdims=True) + NORM_EPS) * g


ANY = pl.BlockSpec(memory_space=pl.ANY)


class _Side:
    def __init__(self, ins, out_shapes, n_sem, make, finish=None):
        self.ins, self.out_shapes, self.n_sem, self.make = list(ins), list(out_shapes), n_sem, make
        self.finish = finish

    def sems(self):
        return [pltpu.SemaphoreType.DMA((self.n_sem,)), pltpu.SemaphoreType.DMA((self.n_sem,))]


def _hosted(body, side, n_in, n_out, grid):
    if side is None:
        return body
    ns_in, ns_out = len(side.ins), len(side.out_shapes)

    def wrapped(*refs):
        ins, refs = refs[:n_in], refs[n_in:]
        s_ins, refs = refs[:ns_in], refs[ns_in:]
        outs, refs = refs[:n_out], refs[n_out:]
        s_outs, refs = refs[:ns_out], refs[ns_out:]
        scratch, sems = refs[:-2], refs[-2:]
        ids = [pl.program_id(d) for d in range(len(grid))]
        first = functools.reduce(jnp.logical_and, [i == 0 for i in ids])
        last = functools.reduce(jnp.logical_and, [i == n - 1 for i, n in zip(ids, grid)])

        @pl.when(first)
        def _():
            for cp in side.make(s_ins, s_outs, *sems):
                cp.start()

        body(*ins, *outs, *scratch)

        @pl.when(last)
        def _():
            if side.finish is not None:
                side.finish(s_ins, s_outs, *sems)
            else:
                for cp in side.make(s_ins, s_outs, *sems):
                    cp.wait()

    return wrapped


def _side_args(side):
    if side is None:
        return [], [], [], [], []
    return ([ANY] * len(side.ins), [ANY] * len(side.out_shapes), side.out_shapes, side.sems(), side.ins)


def _split_side(res, n_out, side):
    res = list(res)
    main = res[0] if n_out == 1 else res[:n_out]
    return main if side is None else (main, res[n_out:])


def _mm(a, b, mode, name, *, epi=None, extras=(), fulls=(), out_dtypes=(F32,), sums=(), tm=1024, tn=1024, tk=1024,
        side=None):
    if mode == "nn":
        (m, k), (k2, n) = a.shape, b.shape
    elif mode == "nt":
        (m, k), (n, k2) = a.shape, b.shape
    else:
        (k, m), (k2, n) = a.shape, b.shape
    assert k == k2, (name, a.shape, b.shape)
    tm, tn, tk = min(tm, m), min(tn, n), min(tk, k)
    assert m % tm == 0 and n % tn == 0 and k % tk == 0, (name, m, n, k)
    nk = k // tk
    dims = {"nn": NN, "nt": NT, "tn": TN}[mode]
    if mode == "tn":
        a_spec = pl.BlockSpec((tk, tm), lambda i, j, kk: (kk, i))
    else:
        a_spec = pl.BlockSpec((tm, tk), lambda i, j, kk: (i, kk))
    if mode == "nt":
        b_spec = pl.BlockSpec((tn, tk), lambda i, j, kk: (j, kk))
    else:
        b_spec = pl.BlockSpec((tk, tn), lambda i, j, kk: (kk, j))
    mn_spec = pl.BlockSpec((tm, tn), lambda i, j, kk: (i, j))
    n_ex, n_full, n_out, n_sum = len(extras), len(fulls), len(out_dtypes), len(sums)
    n_in = 2 + n_ex + n_full

    def body(*refs):
        a_ref, b_ref = refs[:2]
        ex = refs[2:n_in]
        outs = refs[n_in:n_in + n_out]
        sum_refs = refs[n_in + n_out:n_in + n_out + n_sum]
        kk = pl.program_id(2)
        first_tile = jnp.logical_and(pl.program_id(0) == 0, pl.program_id(1) == 0)

        def finish(r):
            vals = epi(r, *[e[...] for e in ex]) if epi is not None else (r,)
            if n_sum:
                vals, parts = vals

                @pl.when(first_tile)
                def _():
                    for sr in sum_refs:
                        sr[...] = jnp.zeros_like(sr)

                for sr, p in zip(sum_refs, parts):
                    sr[...] += p
            for o, v in zip(outs, vals):
                o[...] = v.astype(o.dtype)

        if nk == 1:
            finish(_dot(a_ref[...], b_ref[...], dims))
        else:
            acc = refs[n_in + n_out + n_sum]

            @pl.when(kk == 0)
            def _():
                acc[...] = jnp.zeros_like(acc)

            acc[...] += _dot(a_ref[...], b_ref[...], dims)

            @pl.when(kk == nk - 1)
            def _():
                finish(acc[...])

    grid = (m // tm, n // tn, nk)
    whole = lambda shape: pl.BlockSpec(shape, lambda i, j, kk: (0,) * len(shape))
    s_in, s_out, s_shape, s_scratch, s_ops = _side_args(side)
    seq = bool(side) or n_sum > 0
    res = pl.pallas_call(
        _hosted(body, side, n_in, n_out + n_sum, grid), name=name, grid=grid,
        in_specs=[a_spec, b_spec] + [mn_spec] * n_ex + [whole(f.shape) for f in fulls] + s_in,
        out_specs=[mn_spec] * n_out + [whole(shape) for shape in sums] + s_out,
        out_shape=[jax.ShapeDtypeStruct((m, n), dt) for dt in out_dtypes]
        + [jax.ShapeDtypeStruct(shape, F32) for shape in sums] + s_shape,
        scratch_shapes=([pltpu.VMEM((tm, tn), F32)] if nk > 1 else []) + s_scratch,
        compiler_params=_params(("arbitrary",) * 3 if seq else ("parallel", "parallel", "arbitrary")),
    )(a, b, *extras, *fulls, *s_ops)
    return _split_side(res, n_out + n_sum, side)


def _row_tile(s, target):
    if s <= target:
        return s
    return max(t for t in range(16, target + 1, 16) if s % t == 0)


def _rw(fn, rows, fulls, row_out, acc_out, name, tm=512, side=None):
    cols = [r[1:] if isinstance(r, tuple) else (r.shape[1], 0) for r in rows]
    rows = [r[0] if isinstance(r, tuple) else r for r in rows]
    s = rows[0].shape[0]
    tm = _row_tile(s, tm)
    nr, nf, nro, nao = len(rows), len(fulls), len(row_out), len(acc_out)

    def body(*refs):
        r = refs[:nr]
        f = refs[nr:nr + nf]
        ro = refs[nr + nf:nr + nf + nro]
        ao = refs[nr + nf + nro:]
        outs, accs = fn(*[x[...] for x in r], *[x[...] for x in f])
        for o, v in zip(ro, outs):
            o[...] = v.astype(o.dtype)
        if nao:
            @pl.when(pl.program_id(0) == 0)
            def _():
                for a in ao:
                    a[...] = jnp.zeros_like(a)

            for a, v in zip(ao, accs):
                a[...] += v

    full_spec = lambda shape: pl.BlockSpec(shape, lambda i: (0,) * len(shape))
    s_in, s_out, s_shape, s_scratch, s_ops = _side_args(side)
    res = pl.pallas_call(
        _hosted(body, side, nr + nf, nro + nao, (s // tm,)), name=name, grid=(s // tm,),
        in_specs=[pl.BlockSpec((tm, wd), functools.partial(lambda i, cb: (i, cb), cb=cb)) for wd, cb in cols]
        + [full_spec(x.shape) for x in fulls] + s_in,
        out_specs=[pl.BlockSpec((tm, d), lambda i: (i, 0)) for d, _ in row_out]
        + [full_spec(shape) for shape in acc_out] + s_out,
        out_shape=[jax.ShapeDtypeStruct((s, d), dt) for d, dt in row_out]
        + [jax.ShapeDtypeStruct(shape, F32) for shape in acc_out] + s_shape,
        scratch_shapes=s_scratch,
        compiler_params=_params(("arbitrary",)),
    )(*rows, *fulls, *s_ops)
    res = list(res)
    return res if side is None else (res[:nro + nao], res[nro + nao:])


def _norm_fwd(x, g, name, side=None):
    res = _rw(lambda xt, gt: ((_rms(xt, gt),), ()), [x], [g], [(x.shape[1], BF16)], [], name, side=side)
    return res[0] if side is None else (res[0][0], res[1])


def _norm_bwd(x, g, dh, dres, name, side=None):
    def fn(xt, dht, drt, gt):
        _, vjp = jax.vjp(_rms, xt, gt)
        dx, dg = vjp(dht)
        return (dx + drt,), (dg,)

    return _rw(fn, [x, dh, dres], [g], [(x.shape[1], F32)], [g.shape], name, side=side)


def _rms_groups(x, g, scale):
    lo = lax.broadcasted_iota(jnp.int32, (1, LANES), 1) < SB_HEAD_DIM
    x2 = x * x
    outs = []
    for cb in range(x.shape[1] // LANES):
        sl = slice(cb * LANES, (cb + 1) * LANES)
        s_lo = jnp.sum(jnp.where(lo, x2[:, sl], 0.0), axis=-1, keepdims=True)
        s_hi = jnp.sum(jnp.where(lo, 0.0, x2[:, sl]), axis=-1, keepdims=True)
        r = jnp.where(lo, lax.rsqrt(s_lo * (1.0 / SB_HEAD_DIM) + NORM_EPS),
                      lax.rsqrt(s_hi * (1.0 / SB_HEAD_DIM) + NORM_EPS))
        outs.append(x[:, sl] * r)
    return jnp.concatenate(outs, axis=1) * g * scale


def _log_sigmoid(z):
    return jnp.minimum(z, 0.0) - jnp.log(1.0 + jnp.exp(-jnp.abs(z)))


def _split_dot(x, u2):
    hi = x.astype(BF16)
    lo = (x - hi.astype(F32)).astype(BF16)
    return jnp.dot(jnp.concatenate([hi, lo], axis=1), u2, preferred_element_type=F32)


def _sb_consts(b):
    row = lax.broadcasted_iota(jnp.int32, (b, b), 0)
    col = lax.broadcasted_iota(jnp.int32, (b, b), 1)
    tri = col < row
    u_after = (row > col).astype(BF16)
    u_from = (row >= col).astype(BF16)
    stack = lambda u: jnp.concatenate([u, u], axis=0)
    lane_lo = lax.broadcasted_iota(jnp.int32, (b, LANES), 1) < SB_HEAD_DIM
    return tri, stack(u_after), stack(u_from), lane_lo


def _sb_scores(qh, kb, a_run, keep, u2_after, mask_l=True):
    z = lax.dot_general(qh, kb, NT, preferred_element_type=F32)
    lb = _log_sigmoid(z)
    l = lb - z
    if keep is not None and mask_l:
        l = jnp.where(keep, l, 0.0)
    w = jnp.exp(lb + (a_run + _split_dot(l, u2_after)))
    if keep is not None:
        w = jnp.where(keep, w, 0.0)
    return lb, l, w


def _sb_walk(qi, carry, step):
    def cond(state):
        n, c = state
        return jnp.logical_and(n <= qi, jnp.max(jnp.maximum(c[0], c[1])) > SB_UNDERFLOW)

    def body(state):
        n, c = state
        return n + 1, step(n, c)

    return lax.while_loop(cond, body, (jnp.int32(2), carry))[1]


def _two_heads(x, lane_lo):
    zero = jnp.zeros_like(x)
    return jnp.where(lane_lo, x, zero), jnp.where(lane_lo, zero, x)


def _sb_fwd(qs, ks, v, name, v_col=0, side=None):
    s, width = qs.shape
    b = min(SB_BLOCK, s)

    def body(q_ref, k_ref, v_ref, o_ref):
        qi = pl.program_id(1)
        tri, u2_after, _, lane_lo = _sb_consts(b)
        q_a, q_b = _two_heads(q_ref[...], lane_lo)

        def step(n, carry, keep, mask_l=True):
            a_a, a_b, acc = carry
            off = pl.multiple_of(jnp.maximum(qi - n, 0) * b, b)
            kb = k_ref[pl.ds(off, b), :]
            v_a, v_b = _two_heads(v_ref[pl.ds(off, b), :].astype(BF16), lane_lo)
            _, l_a, w_a = _sb_scores(q_a, kb, a_a, keep, u2_after, mask_l)
            _, l_b, w_b = _sb_scores(q_b, kb, a_b, keep, u2_after, mask_l)
            acc = acc + jnp.dot(jnp.concatenate([w_a.astype(BF16), w_b.astype(BF16)], axis=1),
                                jnp.concatenate([v_a, v_b], axis=0), preferred_element_type=F32)
            return (a_a + jnp.sum(l_a, axis=1, keepdims=True), a_b + jnp.sum(l_b, axis=1, keepdims=True), acc)

        zero = jnp.zeros((b, 1), F32)
        carry = step(0, (zero, zero, jnp.zeros((b, LANES), F32)), tri)
        carry = step(1, carry, jnp.broadcast_to(qi > 0, tri.shape), mask_l=False)
        carry = _sb_walk(qi, carry, lambda n, c: step(n, c, None))
        o_ref[...] = carry[2]

    blk = pl.BlockSpec((b, LANES), lambda hp, i: (i, hp))
    full = pl.BlockSpec((s, LANES), lambda hp, i: (0, hp))
    full_v = pl.BlockSpec((s, LANES), lambda hp, i: (0, hp + v_col))
    grid = (width // LANES, s // b)
    s_in, s_out, s_shape, s_scratch, s_ops = _side_args(side)
    res = pl.pallas_call(
        _hosted(body, side, 3, 1, grid), name=name, grid=grid,
        in_specs=[blk, full, full_v] + s_in, out_specs=[blk] + s_out,
        out_shape=[jax.ShapeDtypeStruct((s, width), F32)] + s_shape, scratch_shapes=s_scratch,
        compiler_params=_params(("arbitrary", "arbitrary")),
    )(qs, ks, v, *s_ops)
    return _split_side(res, 1, side)


def _sb_bwd(qs, ks, v, out, dout, name, v_col=0, side=None):
    s, width = qs.shape
    b = min(SB_BLOCK, s)
    nkb = s // b

    def body(q_ref, k_ref, v_ref, o_ref, do_ref, dq_ref, dk_ref, dv_ref):
        qi = pl.program_id(1)

        @pl.when(qi == 0)
        def _():
            dk_ref[...] = jnp.zeros_like(dk_ref)
            dv_ref[...] = jnp.zeros_like(dv_ref)

        tri, u2_after, u2_from, lane_lo = _sb_consts(b)
        q_a, q_b = _two_heads(q_ref[...], lane_lo)
        dob = do_ref[...].astype(BF16)
        do_a, do_b = _two_heads(dob, lane_lo)
        prod = dob.astype(F32) * o_ref[...]
        d_a = jnp.sum(jnp.where(lane_lo, prod, 0.0), axis=1, keepdims=True)
        d_b = jnp.sum(jnp.where(lane_lo, 0.0, prod), axis=1, keepdims=True)
        q_rows = jnp.concatenate([q_a, q_b], axis=0)
        do_rows = jnp.concatenate([do_a, do_b], axis=0)

        def head(qh, doh, kb, vb, a_run, d_rem, keep, mask_l):
            lb, l, w = _sb_scores(qh, kb, a_run, keep, u2_after, mask_l)
            wb = w.astype(BF16)
            g = lax.dot_general(doh, vb, NT, preferred_element_type=F32) * wb.astype(F32)
            g_before = d_rem - _split_dot(g, u2_from)
            dz = g - (g + g_before) * jnp.exp(lb)
            if keep is not None:
                dz = jnp.where(keep, dz, 0.0)
            return (dz.astype(BF16), wb, a_run + jnp.sum(l, axis=1, keepdims=True),
                    d_rem - jnp.sum(g, axis=1, keepdims=True))

        def step(n, carry, keep, mask_l=True):
            a_a, a_b, r_a, r_b, dq = carry
            jb = jnp.maximum(qi - n, 0)
            off = pl.multiple_of(jb * b, b)
            kb = k_ref[pl.ds(off, b), :]
            vb = v_ref[pl.ds(off, b), :].astype(BF16)
            k_a, k_b = _two_heads(kb, lane_lo)
            dz_a, w_a, a_a, r_a = head(q_a, do_a, kb, vb, a_a, r_a, keep, mask_l)
            dz_b, w_b, a_b, r_b = head(q_b, do_b, kb, vb, a_b, r_b, keep, mask_l)
            dq = dq + jnp.dot(jnp.concatenate([dz_a, dz_b], axis=1), jnp.concatenate([k_a, k_b], axis=0),
                              preferred_element_type=F32)
            dk_ref[pl.ds(off, b), :] += lax.dot_general(jnp.concatenate([dz_a, dz_b], axis=0), q_rows, TN,
                                                        preferred_element_type=F32)
            dv_ref[pl.ds(off, b), :] += lax.dot_general(jnp.concatenate([w_a, w_b], axis=0), do_rows, TN,
                                                        preferred_element_type=F32)
            return a_a, a_b, r_a, r_b, dq

        zero = jnp.zeros((b, 1), F32)
        carry = step(0, (zero, zero, d_a, d_b, jnp.zeros((b, LANES), F32)), tri)
        carry = step(1, carry, jnp.broadcast_to(qi > 0, tri.shape), mask_l=False)
        carry = _sb_walk(qi, carry, lambda n, c: step(n, c, None))
        dq_ref[...] = carry[4]

    blk = pl.BlockSpec((b, LANES), lambda hp, i: (i, hp))
    full = pl.BlockSpec((s, LANES), lambda hp, i: (0, hp))
    full_v = pl.BlockSpec((s, LANES), lambda hp, i: (0, hp + v_col))
    grid = (width // LANES, nkb)
    s_in, s_out, s_shape, s_scratch, s_ops = _side_args(side)
    res = pl.pallas_call(
        _hosted(body, side, 5, 3, grid), name=name, grid=grid,
        in_specs=[blk, full, full_v, blk, blk] + s_in, out_specs=[blk, full, full] + s_out,
        out_shape=[jax.ShapeDtypeStruct((s, width), F32)] * 3 + s_shape,
        scratch_shapes=s_scratch,
        compiler_params=_params(("arbitrary", "arbitrary")),
    )(qs, ks, v, out, dout, *s_ops)
    return _split_side(res, 3, side)


def _cmul(xr, xi, yr, yi):
    return xr * yr - xi * yi, xr * yi + xi * yr


def _scan_consts(ar, ai, reverse, lc):
    rowi = lax.broadcasted_iota(jnp.int32, (SUBLANES, lc), 0)
    pows = [(ar, ai)]
    for _ in range(SUBLANES - 1):
        pows.append(_cmul(*pows[-1], ar, ai))
    steps = []
    for d in (1, 2, 4):
        keep = (rowi < SUBLANES - d) if reverse else (rowi >= d)
        pr, pi = pows[d - 1]
        steps.append((SUBLANES - d if reverse else d, jnp.where(keep, pr, 0.0), jnp.where(keep, pi, 0.0)))
    cr = jnp.zeros((SUBLANES, lc), F32)
    ci = jnp.zeros((SUBLANES, lc), F32)
    for r in range(SUBLANES):
        pr, pi = pows[SUBLANES - 1 - r] if reverse else pows[r]
        cr = jnp.where(rowi == r, pr, cr)
        ci = jnp.where(rowi == r, pi, ci)
    return steps, cr, ci


def _scan_tile(xr, xi, steps, pr, pi, cr, ci):
    for shift, ar, ai in steps:
        rr = pltpu.roll(xr, shift, 0)
        ri = pltpu.roll(xi, shift, 0)
        xr, xi = xr + ar * rr - ai * ri, xi + ar * ri + ai * rr
    return xr + pr * cr - pi * ci, xi + pr * ci + pi * cr


SCAN_ROWS = 1024


def _scan_chunk(s):
    tt = min(SCAN_ROWS, s)
    seg = tt // SUBLANES
    assert s % tt == 0 and seg % SUBLANES == 0 and seg & (seg - 1) == 0, s
    return tt, seg


def _to_segments(a):
    s, wd = a.shape
    tt, seg = _scan_chunk(s)
    return jnp.transpose(a.reshape(s // tt, SUBLANES, seg, wd), (0, 2, 1, 3)).reshape(s, wd)


def _from_segments(a):
    s, wd = a.shape
    tt, seg = _scan_chunk(s)
    return jnp.transpose(a.reshape(s // tt, seg, SUBLANES, wd), (0, 2, 1, 3)).reshape(s, wd)


def _cpow2(xr, xi, k):
    for _ in range(k):
        xr, xi = _cmul(xr, xi, xr, xi)
    return xr, xi


def _fill_powers(pw_ref, ar, ai, seg, lc):
    _, p8r, p8i = _scan_consts(ar, ai, False, lc)
    a8r, a8i = _cpow2(ar, ai, 3)
    qr, qi = jnp.ones_like(ar), jnp.zeros_like(ai)
    for k in range(seg // SUBLANES):
        tr, ti = _cmul(p8r, p8i, qr, qi)
        for r in range(SUBLANES):
            rows = pl.ds((SUBLANES * k + r) * SUBLANES, SUBLANES)
            pw_ref[rows, :lc] = jnp.broadcast_to(tr[r:r + 1, :], (SUBLANES, lc))
            pw_ref[rows, lc:] = jnp.broadcast_to(ti[r:r + 1, :], (SUBLANES, lc))
        qr, qi = _cmul(qr, qi, a8r, a8i)


def _ssm_fwd(u, acat, bsup, csup, d_skip, name, side=None):
    s = u.shape[0]
    lc = SCAN_LANES
    tt, seg = _scan_chunk(s)
    nl, nt = N_STATE // lc, s // tt
    tile = lambda j: pl.ds(pl.multiple_of(j * SUBLANES, SUBLANES), SUBLANES)

    def body(u_ref, a_ref, b_ref, c_ref, d_ref, s_ref, y0_ref, y1_ref, carry, pw_ref):
        ar, ai = a_ref[:, :lc], a_ref[:, lc:]

        @pl.when(pl.program_id(1) == 0)
        def _():
            carry[...] = jnp.zeros_like(carry)
            _fill_powers(pw_ref, ar, ai, seg, lc)

        ut = u_ref[...]
        s_ref[...] = _dot(ut, b_ref[0])

        ar8, ai8 = jnp.broadcast_to(ar, (SUBLANES, lc)), jnp.broadcast_to(ai, (SUBLANES, lc))

        def local(j, x):
            xr = ar8 * x[0] - ai8 * x[1] + s_ref[tile(j), :lc]
            xi = ar8 * x[1] + ai8 * x[0] + s_ref[tile(j), lc:]
            s_ref[tile(j), :lc] = xr
            s_ref[tile(j), lc:] = xi
            return xr, xi

        zero = jnp.zeros((SUBLANES, lc), F32)
        er, ei = lax.fori_loop(0, seg, local, (zero, zero))
        steps, pr, pi = _scan_consts(*_cpow2(ar, ai, seg.bit_length() - 1), False, lc)
        cr, ci = carry[:, :lc], carry[:, lc:]
        tr, ti = _scan_tile(er, ei, steps, pr, pi, cr, ci)
        rowi = lax.broadcasted_iota(jnp.int32, (SUBLANES, lc), 0)
        before_r = jnp.where(rowi == 0, cr, pltpu.roll(tr, 1, 0))
        before_i = jnp.where(rowi == 0, ci, pltpu.roll(ti, 1, 0))
        carry[:, :lc] = jnp.broadcast_to(tr[SUBLANES - 1:, :], (SUBLANES, lc))
        carry[:, lc:] = jnp.broadcast_to(ti[SUBLANES - 1:, :], (SUBLANES, lc))

        def fix(j, _):
            pwr, pwi = pw_ref[tile(j), :lc], pw_ref[tile(j), lc:]
            s_ref[tile(j), :lc] += pwr * before_r - pwi * before_i
            s_ref[tile(j), lc:] += pwr * before_i + pwi * before_r
            return 0

        lax.fori_loop(0, seg, fix, 0)
        y0 = _dot(s_ref[...], c_ref[0], NT) + d_ref[...] * ut
        y0_ref[...] = y0
        y1_ref[...] = jax.nn.gelu(y0)

    chan = pl.BlockSpec((tt, LANES), lambda j, c: (c, j))
    sup = pl.BlockSpec((1, LANES, 2 * lc), lambda j, c: (j, 0, 0))
    s_in, s_out, s_shape, s_scratch, s_ops = _side_args(side)
    res = pl.pallas_call(
        _hosted(body, side, 5, 3, (nl, nt)), name=name, grid=(nl, nt),
        in_specs=[chan, pl.BlockSpec((1, 2 * lc), lambda j, c: (0, j)), sup, sup,
                  pl.BlockSpec((1, LANES), lambda j, c: (0, j))] + s_in,
        out_specs=[pl.BlockSpec((tt, 2 * lc), lambda j, c: (c, j)), chan, chan] + s_out,
        out_shape=[jax.ShapeDtypeStruct((s, 2 * N_STATE), F32), jax.ShapeDtypeStruct((s, SSM_WIDTH), F32),
                   jax.ShapeDtypeStruct((s, SSM_WIDTH), F32)] + s_shape,
        scratch_shapes=[pltpu.VMEM((SUBLANES, 2 * lc), F32), pltpu.VMEM((seg * SUBLANES, 2 * lc), F32)] + s_scratch,
        compiler_params=_params(("arbitrary", "arbitrary")),
    )(u, acat, bsup, csup, d_skip, *s_ops)
    return _split_side(res, 3, side)


def _ssm_bwd(dy0, states, u, acat, bsup, csup, d_skip, name, side=None):
    s = u.shape[0]
    lc = SCAN_LANES
    tt, seg = _scan_chunk(s)
    nl, nt = N_STATE // lc, s // tt
    tile = lambda j: pl.ds(pl.multiple_of(j * SUBLANES, SUBLANES), SUBLANES)

    def body(dy_ref, s_ref, sp_ref, u_ref, a_ref, b_ref, c_ref, d_ref,
             du_ref, da_ref, db_ref, dc_ref, dd_ref, lam_ref, carry, pw_ref):
        c = pl.program_id(1)
        ar, ai = a_ref[:, :lc], a_ref[:, lc:]

        @pl.when(c == 0)
        def _():
            carry[...] = jnp.zeros_like(carry)
            for r in (da_ref, db_ref, dc_ref, dd_ref):
                r[...] = jnp.zeros_like(r)
            _fill_powers(pw_ref, ar, ai, seg, lc)

        dy = dy_ref[...]
        ut = u_ref[...]
        lam_ref[...] = _dot(dy, c_ref[0])

        ar8, ai8 = jnp.broadcast_to(ar, (SUBLANES, lc)), jnp.broadcast_to(ai, (SUBLANES, lc))

        def local(i, x):
            j = seg - 1 - i
            xr = ar8 * x[0] + ai8 * x[1] + lam_ref[tile(j), :lc]
            xi = ar8 * x[1] - ai8 * x[0] + lam_ref[tile(j), lc:]
            lam_ref[tile(j), :lc] = xr
            lam_ref[tile(j), lc:] = xi
            return xr, xi

        zero = jnp.zeros((SUBLANES, lc), F32)
        er, ei = lax.fori_loop(0, seg, local, (zero, zero))
        big_r, big_i = _cpow2(ar, ai, seg.bit_length() - 1)
        steps, pr, pi = _scan_consts(big_r, -big_i, True, lc)
        cr, ci = carry[:, :lc], carry[:, lc:]
        tr, ti = _scan_tile(er, ei, steps, pr, pi, cr, ci)
        rowi = lax.broadcasted_iota(jnp.int32, (SUBLANES, lc), 0)
        after_r = jnp.where(rowi == SUBLANES - 1, cr, pltpu.roll(tr, SUBLANES - 1, 0))
        after_i = jnp.where(rowi == SUBLANES - 1, ci, pltpu.roll(ti, SUBLANES - 1, 0))
        carry[:, :lc] = jnp.broadcast_to(tr[:1, :], (SUBLANES, lc))
        carry[:, lc:] = jnp.broadcast_to(ti[:1, :], (SUBLANES, lc))

        start = c != nt - 1
        last_r = jnp.where(start, jnp.broadcast_to(sp_ref[SUBLANES - 1:, :lc], (SUBLANES, lc)), 0.0)
        last_i = jnp.where(start, jnp.broadcast_to(sp_ref[SUBLANES - 1:, lc:], (SUBLANES, lc)), 0.0)
        first_r = jnp.where(rowi == 0, last_r, pltpu.roll(s_ref[tile(seg - 1), :lc], 1, 0))
        first_i = jnp.where(rowi == 0, last_i, pltpu.roll(s_ref[tile(seg - 1), lc:], 1, 0))

        def fix(j, acc):
            dar, dai = acc
            k = seg - 1 - j
            pwr, pwi = pw_ref[tile(k), :lc], pw_ref[tile(k), lc:]
            lr = lam_ref[tile(j), :lc] + pwr * after_r + pwi * after_i
            li = lam_ref[tile(j), lc:] + pwr * after_i - pwi * after_r
            lam_ref[tile(j), :lc] = lr
            lam_ref[tile(j), lc:] = li
            jp = jnp.maximum(j - 1, 0)
            sr = jnp.where(j > 0, s_ref[tile(jp), :lc], first_r)
            si = jnp.where(j > 0, s_ref[tile(jp), lc:], first_i)
            return dar + lr * sr + li * si, dai + li * sr - lr * si

        dar, dai = lax.fori_loop(0, seg, fix, (zero, zero))
        da_ref[:, :lc] += dar
        da_ref[:, lc:] += dai
        lam = lam_ref[...].astype(BF16)
        du_ref[...] = (_dot(lam, b_ref[0], NT) + d_ref[...] * dy).astype(du_ref.dtype)
        db_ref[0] += _dot(ut, lam, TN)
        dc_ref[0] += _dot(dy, s_ref[...], TN)
        dd_ref[...] += jnp.sum(dy * ut, axis=0, keepdims=True)

    rev = lambda j, c: (nt - 1 - c, j)
    chan = pl.BlockSpec((tt, LANES), rev)
    sup = pl.BlockSpec((1, LANES, 2 * lc), lambda j, c: (j, 0, 0))
    row = pl.BlockSpec((1, LANES), lambda j, c: (0, j))
    s_in, s_out, s_shape, s_scratch, s_ops = _side_args(side)
    res = pl.pallas_call(
        _hosted(body, side, 8, 5, (nl, nt)), name=name, grid=(nl, nt),
        in_specs=[chan, pl.BlockSpec((tt, 2 * lc), rev),
                  pl.BlockSpec((SUBLANES, 2 * lc), lambda j, c: (jnp.maximum((nt - 1 - c) * seg - 1, 0), j)),
                  chan, pl.BlockSpec((1, 2 * lc), lambda j, c: (0, j)), sup, sup, row] + s_in,
        out_specs=[chan, pl.BlockSpec((SUBLANES, 2 * lc), lambda j, c: (0, j)), sup, sup, row] + s_out,
        out_shape=[jax.ShapeDtypeStruct((s, SSM_WIDTH), BF16), jax.ShapeDtypeStruct((SUBLANES, 2 * N_STATE), F32),
                   jax.ShapeDtypeStruct(bsup.shape, F32), jax.ShapeDtypeStruct(csup.shape, F32),
                   jax.ShapeDtypeStruct((1, SSM_WIDTH), F32)] + s_shape,
        scratch_shapes=[pltpu.VMEM((tt, 2 * lc), F32), pltpu.VMEM((SUBLANES, 2 * lc), F32),
                        pltpu.VMEM((seg * SUBLANES, 2 * lc), F32)] + s_scratch,
        compiler_params=_params(("arbitrary", "arbitrary")),
    )(dy0, states, states, u, acat, bsup, csup, d_skip, *s_ops)
    return _split_side(res, 5, side)


def _state_cols(xr, xi):
    lead = xr.shape[:-1]
    nl = N_STATE // SCAN_LANES
    both = jnp.stack([xr.reshape(lead + (nl, SCAN_LANES)), xi.reshape(lead + (nl, SCAN_LANES))], axis=-2)
    return both.reshape(lead + (2 * N_STATE,))


def _ssm_mats(a_re, a_im, log_dt, b_re, b_im, c_re, c_im):
    dt = jnp.exp(log_dt)[:, None]
    lr, li = a_re * dt, a_im * dt
    e = jnp.exp(lr)
    abar_r, abar_i = e * jnp.cos(li), e * jnp.sin(li)
    den = a_re * a_re + a_im * a_im
    coef_r = ((abar_r - 1.0) * a_re + abar_i * a_im) / den
    coef_i = (abar_i * a_re - (abar_r - 1.0) * a_im) / den
    bbar_r = coef_r[..., None] * b_re - coef_i[..., None] * b_im
    bbar_i = coef_r[..., None] * b_im + coef_i[..., None] * b_re
    nl = N_STATE // SCAN_LANES
    gpb = SSM_GROUPS // nl
    eye = jnp.eye(gpb, dtype=bool)[None, :, None, :, None]

    def sup(m_r, m_i):
        def one(m):
            m = m.reshape(nl, gpb, SSM_GROUP, 1, SSM_STATE)
            return jnp.where(eye, m, 0.0).reshape(nl, gpb * SSM_GROUP, SCAN_LANES)
        return jnp.concatenate([one(m_r), one(m_i)], axis=-1)

    acat = _state_cols(abar_r.reshape(1, N_STATE), abar_i.reshape(1, N_STATE))
    bsup = sup(jnp.transpose(bbar_r, (0, 2, 1)), jnp.transpose(bbar_i, (0, 2, 1)))
    csup = sup(c_re, -c_im)
    return acat, bsup, csup


def _mem_fwd(mem, g_mem, w_kv, g_k, name):
    ml = mem.shape[0]

    def body(mem_ref, gm_ref, w_ref, gk_ref, memn_ref, kv_ref, kn_ref, vv_ref):
        memn = _rms(mem_ref[...], gm_ref[...])
        memn_ref[...] = memn.astype(BF16)
        kv = _dot(memn, w_ref[...])
        kv_ref[...] = kv
        for hh in range(XA_HEADS):
            sl = slice(hh * XA_HEAD_DIM, (hh + 1) * XA_HEAD_DIM)
            kn_ref[:, sl] = _rms(kv[:, sl], gk_ref[...]).astype(BF16)
        vv_ref[...] = kv[:, XA_WIDTH:].astype(BF16)

    return pl.pallas_call(
        body, name=name,
        out_shape=[jax.ShapeDtypeStruct((ml, D_MODEL), BF16), jax.ShapeDtypeStruct((ml, 2 * XA_WIDTH), F32),
                   jax.ShapeDtypeStruct((ml, XA_WIDTH), BF16), jax.ShapeDtypeStruct((ml, XA_WIDTH), BF16)],
        compiler_params=_params(),
    )(mem, g_mem, w_kv, g_k)


def _mem_bwd(mem, g_mem, memn, w_kv, kv, g_k, dkn, dvv, name):
    def body(mem_ref, gm_ref, memn_ref, w_ref, kv_ref, gk_ref, dkn_ref, dvv_ref, dw_ref, dgm_ref, dgk_ref):
        kv = kv_ref[...]
        dgk = jnp.zeros(dgk_ref.shape, F32)
        parts = []
        for hh in range(XA_HEADS):
            sl = slice(hh * XA_HEAD_DIM, (hh + 1) * XA_HEAD_DIM)
            _, vjp = jax.vjp(_rms, kv[:, sl], gk_ref[...])
            dk, dg = vjp(dkn_ref[:, sl])
            parts.append(dk)
            dgk = dgk + dg
        dgk_ref[...] = dgk
        dkv = jnp.concatenate(parts + [dvv_ref[...]], axis=1)
        dw_ref[...] = _dot(memn_ref[...], dkv, TN)
        dmemn = _dot(dkv, w_ref[...], NT)
        _, vjp = jax.vjp(_rms, mem_ref[...], gm_ref[...])
        dgm_ref[...] = vjp(dmemn)[1]

    return pl.pallas_call(
        body, name=name,
        out_shape=[jax.ShapeDtypeStruct((D_MODEL, 2 * XA_WIDTH), F32), jax.ShapeDtypeStruct(g_mem.shape, F32),
                   jax.ShapeDtypeStruct(g_k.shape, F32)],
        compiler_params=_params(),
    )(mem, g_mem, memn, w_kv, kv, g_k, dkn, dvv)


def _xa_head(qx_h, g_q, kn_h, vv_h):
    qn = _rms(qx_h, g_q)
    sc = _dot(qn, kn_h, NT) * (XA_HEAD_DIM ** -0.5)
    sc = sc - jnp.max(sc, axis=-1, keepdims=True)
    e = jnp.exp(sc)
    p = e / jnp.sum(e, axis=-1, keepdims=True)
    return qn, p


def _xa_fwd(qx, g_q, kn, vv, name):
    def fn(qt, gq, knt, vvt):
        outs = []
        for hh in range(XA_HEADS):
            sl = slice(hh * XA_HEAD_DIM, (hh + 1) * XA_HEAD_DIM)
            _, p = _xa_head(qt[:, sl], gq, knt[:, sl], vvt[:, sl])
            outs.append(_dot(p, vvt[:, sl]))
        return (jnp.concatenate(outs, axis=1),), ()

    return _rw(fn, [qx], [g_q, kn, vv], [(XA_WIDTH, BF16)], [], name)[0]


def _xa_bwd(qx, g_q, kn, vv, do, name):
    def fn(qt, dot_, gq, knt, vvt):
        dqs, dks, dvs = [], [], []
        dgq = jnp.zeros_like(gq)
        for hh in range(XA_HEADS):
            sl = slice(hh * XA_HEAD_DIM, (hh + 1) * XA_HEAD_DIM)
            qn, p = _xa_head(qt[:, sl], gq, knt[:, sl], vvt[:, sl])
            doh = dot_[:, sl]
            dp = _dot(doh, vvt[:, sl], NT)
            dvs.append(_dot(p, doh, TN))
            ds = p * (dp - jnp.sum(dp * p, axis=-1, keepdims=True)) * (XA_HEAD_DIM ** -0.5)
            dqn = _dot(ds, knt[:, sl])
            dks.append(_dot(ds, qn, TN))
            _, vjp = jax.vjp(_rms, qt[:, sl], gq)
            dq, dg = vjp(dqn)
            dqs.append(dq)
            dgq = dgq + dg
        return ((jnp.concatenate(dqs, axis=1),),
                (jnp.concatenate(dks, axis=1), jnp.concatenate(dvs, axis=1), dgq))

    return _rw(fn, [qx, do], [g_q, kn, vv], [(XA_WIDTH, BF16)], [kn.shape, vv.shape, g_q.shape], name)


BIG = [
    ("w_in", (D_MODEL, IN_WIDTH), 1), ("ssm_w_glu", (SSM_WIDTH, SSM_WIDTH), 0), ("w_out", (D_MODEL, D_MODEL), 0),
    ("xa_w_q", (D_MODEL, XA_WIDTH), 0), ("xa_w_kv", (D_MODEL, 2 * XA_WIDTH), 0), ("xa_w_o", (XA_WIDTH, D_MODEL), 1),
    ("w_up", (D_MODEL, D_FF), 1), ("w_down", (D_FF, D_MODEL), 0),
]
BIG_INDEX = {n: i for i, (n, _, _) in enumerate(BIG)}


def _shard_shape(shape, axis):
    return tuple(d // N_DEV if i == axis else d for i, d in enumerate(shape))


def _shard_of(ref, axis, d):
    n = ref.shape[axis] // N_DEV
    return ref.at[pl.ds(d * n, n), :] if axis == 0 else ref.at[:, pl.ds(d * n, n)]


def _gather_side(names, shards):
    idxs = [BIG_INDEX[n] for n in names]

    def make(ins, outs, send_sems, recv_sems):
        x, y, c = lax.axis_index("x"), lax.axis_index("y"), lax.axis_index("c")
        cps = []
        for j, i in enumerate(idxs):
            mine = _shard_of(outs[j], BIG[i][2], 4 * x + 2 * y + c)
            cps.append(pltpu.make_async_copy(ins[j], mine, send_sems.at[N_DEV * j]))
            for rel in range(1, N_DEV):
                to = tuple(1 - p if rel >> bit & 1 else p for p, bit in ((x, 2), (y, 1), (c, 0)))
                cps.append(pltpu.make_async_remote_copy(
                    src_ref=ins[j], dst_ref=mine, send_sem=send_sems.at[N_DEV * j + rel],
                    recv_sem=recv_sems.at[N_DEV * j + rel], device_id=to, device_id_type=MESH))
        return cps

    return _Side(shards, [jax.ShapeDtypeStruct(BIG[i][1], BF16) for i in idxs], N_DEV * len(idxs), make)


def _gather_two_level_side(name, shard):
    i = BIG_INDEX[name]

    def parts(ins, outs, send_sems, recv_sems):
        x, y, c = lax.axis_index("x"), lax.axis_index("y"), lax.axis_index("c")
        sibling = (x, y, 1 - c)
        chips = [(1 - x, y), (x, 1 - y), (1 - x, 1 - y)]

        def place(dev):
            return _shard_of(outs[0], BIG[i][2], 4 * dev[0] + 2 * dev[1] + dev[2])

        def copy(k, blk, to, src=None):
            return pltpu.make_async_remote_copy(
                src_ref=place(blk) if src is None else src, dst_ref=place(blk), send_sem=send_sems.at[k],
                recv_sem=recv_sems.at[k], device_id=to, device_id_type=MESH)

        mine = pltpu.make_async_copy(ins[0], place((x, y, c)), send_sems.at[7])
        first = [copy(0, (x, y, c), sibling, src=ins[0])]
        first += [copy(1 + j, (x, y, c), (*chip, c), src=ins[0]) for j, chip in enumerate(chips)]
        passed = [copy(4 + j, (*chip, c), sibling) for j, chip in enumerate(chips)]
        arrived = [copy(1 + j, (*chip, c), (x, y, c)) for j, chip in enumerate(chips)]
        from_sibling = [copy(0, sibling, (x, y, c))] + [copy(4 + j, (*chip, 1 - c), (x, y, c))
                                                       for j, chip in enumerate(chips)]
        return mine, first, passed, arrived, from_sibling

    def make(ins, outs, send_sems, recv_sems):
        mine, first, _, _, _ = parts(ins, outs, send_sems, recv_sems)
        return [mine] + first

    def finish(ins, outs, send_sems, recv_sems):
        mine, first, passed, arrived, from_sibling = parts(ins, outs, send_sems, recv_sems)
        for got, onward in zip(arrived, passed):
            got.wait_recv()
            onward.start()
        for cp in from_sibling:
            cp.wait_recv()
        for cp in first + passed:
            cp.wait_send()
        mine.wait()

    return _Side([shard], [jax.ShapeDtypeStruct(BIG[i][1], BF16)], N_DEV, make, finish)


def _sibling_side(names, grads):
    idxs = [BIG_INDEX[n] for n in names]

    def make(ins, outs, send_sems, recv_sems):
        x, y, c = lax.axis_index("x"), lax.axis_index("y"), lax.axis_index("c")
        return [pltpu.make_async_remote_copy(
            src_ref=_shard_of(ins[j], BIG[i][2], 2 * k + (1 - c)), dst_ref=outs[j].at[k],
            send_sem=send_sems.at[4 * j + k], recv_sem=recv_sems.at[4 * j + k], device_id=(x, y, 1 - c),
            device_id_type=MESH) for j, i in enumerate(idxs) for k in range(4)]

    shapes = [jax.ShapeDtypeStruct((4,) + _shard_shape(BIG[i][1], BIG[i][2]), F32) for i in idxs]
    return _Side(grads, shapes, 4 * len(idxs), make)


def _chips_side(parts):
    def make(ins, outs, send_sems, recv_sems):
        x, y, c = lax.axis_index("x"), lax.axis_index("y"), lax.axis_index("c")
        chips = [(1 - x, y), (x, 1 - y), (1 - x, 1 - y)]
        return [pltpu.make_async_remote_copy(
            src_ref=ins[j].at[2 * cx + cy], dst_ref=outs[j].at[r], send_sem=send_sems.at[3 * j + r],
            recv_sem=recv_sems.at[3 * j + r], device_id=(cx, cy, c), device_id_type=MESH)
            for r, (cx, cy) in enumerate(chips) for j in range(len(parts))]

    return _Side(parts, [jax.ShapeDtypeStruct((3,) + p.shape[1:], p.dtype) for p in parts], 3 * len(parts), make)


def _reduce_add(grad, recv, axis, core, name):
    rs, cs = recv.shape[1:]
    rt = _row_tile(rs, 256)
    nt = rs // rt

    def body(c_ref, g_ref, r_ref, p_ref, pb_ref):
        sm = g_ref[...] + r_ref[0]
        p_ref[0] = sm
        pb_ref[0] = sm.astype(BF16)

    if axis == 0:
        g_spec = pl.BlockSpec((rt, cs), lambda k, t, c_ref: ((2 * k + c_ref[0]) * nt + t, 0))
    else:
        g_spec = pl.BlockSpec((rt, cs), lambda k, t, c_ref: (t, 2 * k + c_ref[0]))
    slab = pl.BlockSpec((1, rt, cs), lambda k, t, c_ref: (k, t, 0))
    return pl.pallas_call(
        body, name=name,
        grid_spec=pltpu.PrefetchScalarGridSpec(num_scalar_prefetch=1, grid=(4, nt), in_specs=[g_spec, slab],
                                               out_specs=[slab, slab]),
        out_shape=[jax.ShapeDtypeStruct(recv.shape, F32), jax.ShapeDtypeStruct(recv.shape, BF16)],
        compiler_params=_params(("parallel", "parallel")),
    )(core, grad, recv)


def _all_gather(block, name, side):
    m_per, n = block.shape
    ns_in, ns_out = len(side.ins), len(side.out_shapes)

    def body(*refs):
        x_ref, s_ins, out_ref = refs[0], refs[1:1 + ns_in], refs[1 + ns_in]
        s_outs = refs[2 + ns_in:2 + ns_in + ns_out]
        send_sems, recv_sems, local_sem, s_send, s_recv = refs[2 + ns_in + ns_out:]
        others = side.make(s_ins, s_outs, s_send, s_recv)
        for cp in others:
            cp.start()
        x, y, c = lax.axis_index("x"), lax.axis_index("y"), lax.axis_index("c")
        me, sibling = (x, y, c), (x, y, 1 - c)
        chips = [(1 - x, y), (x, 1 - y), (1 - x, 1 - y)]

        def rows(px, py, pc):
            return out_ref.at[pl.ds((4 * px + 2 * py + pc) * m_per, m_per), :]

        def copy(k, blk, to, src=None):
            return pltpu.make_async_remote_copy(
                src_ref=rows(*blk) if src is None else src, dst_ref=rows(*blk),
                send_sem=send_sems.at[k], recv_sem=recv_sems.at[k], device_id=to, device_id_type=MESH)

        mine = pltpu.make_async_copy(x_ref, rows(*me), local_sem)
        mine.start()
        first = [copy(0, me, sibling, src=x_ref)]
        first += [copy(1 + j, me, (*chip, c), src=x_ref) for j, chip in enumerate(chips)]
        for cp in first:
            cp.start()
        passed = [copy(4 + j, (*chip, c), sibling) for j, chip in enumerate(chips)]
        for j, chip in enumerate(chips):
            copy(1 + j, (*chip, c), me).wait_recv()
            passed[j].start()
        copy(0, sibling, me).wait_recv()
        for j, chip in enumerate(chips):
            copy(4 + j, (*chip, 1 - c), me).wait_recv()
        for cp in first + passed:
            cp.wait_send()
        mine.wait()
        for cp in others:
            cp.wait()

    res = pl.pallas_call(
        body, name=name, in_specs=[ANY] * (1 + ns_in), out_specs=[ANY] * (1 + ns_out),
        out_shape=[jax.ShapeDtypeStruct((N_DEV * m_per, n), block.dtype)] + side.out_shapes,
        scratch_shapes=[pltpu.SemaphoreType.DMA((7,)), pltpu.SemaphoreType.DMA((7,)), pltpu.SemaphoreType.DMA]
        + side.sems(),
    )(block, *side.ins)
    return res[0], list(res[1:])


def _adam_math(w, g, m, v):
    m = ADAM_B1 * m + (1.0 - ADAM_B1) * g
    v = ADAM_B2 * v + (1.0 - ADAM_B2) * (g * g)
    m_hat = m / (1.0 - ADAM_B1 ** ADAM_STEP)
    v_hat = v / (1.0 - ADAM_B2 ** ADAM_STEP)
    delta = -ADAM_LR * (m_hat / (jnp.sqrt(v_hat) + ADAM_EPS) + ADAM_WD * w)
    return delta, m, v


def _adam_sharded(own, recv, w, m, v, chip, name):
    rs, cs = w.shape
    rt = _row_tile(rs, 256)

    def body(chip_ref, p_ref, r_ref, w_ref, m_ref, v_ref, g_out, d_out, m_out, v_out):
        g = p_ref[0] + r_ref[0].astype(F32) + r_ref[1].astype(F32) + r_ref[2].astype(F32)
        d, mn, vn = _adam_math(w_ref[...], g, m_ref[...], v_ref[...])
        g_out[...] = g
        d_out[...] = d
        m_out[...] = mn
        v_out[...] = vn

    tile = pl.BlockSpec((rt, cs), lambda t, chip_ref: (t, 0))
    return pl.pallas_call(
        body, name=name,
        grid_spec=pltpu.PrefetchScalarGridSpec(
            num_scalar_prefetch=1, grid=(rs // rt,),
            in_specs=[pl.BlockSpec((1, rt, cs), lambda t, chip_ref: (chip_ref[0], t, 0)),
                      pl.BlockSpec((3, rt, cs), lambda t, chip_ref: (0, t, 0)), tile, tile, tile],
            out_specs=[tile] * 4),
        out_shape=[jax.ShapeDtypeStruct((rs, cs), F32)] * 4,
        compiler_params=_params(("parallel",)),
    )(chip, own, recv, w, m, v)


SMALL = ["g_mix", "ssm_a_re", "ssm_a_im", "ssm_log_dt", "ssm_b_re", "ssm_b_im", "ssm_c_re", "ssm_c_im", "ssm_d",
         "sb_g_q", "sb_g_k", "g_out_ssm", "g_out_sb", "g_xa", "g_mem", "xa_g_q", "xa_g_k", "g_mlp"]
PACK_TILE = SUBLANES * LANES


def _natural_2d(n):
    return (n // LANES, LANES) if n % LANES == 0 else (1, n)


def _pack_small(arrs):
    parts = []
    for a in arrs:
        flat = a.reshape(-1)
        parts.append(jnp.pad(flat, (0, (-flat.shape[0]) % PACK_TILE)))
    return jnp.concatenate(parts).reshape(-1, LANES)


def _adam_replicated(gathered, sizes, ws, ms, vs, name):
    n_w = len(ws)
    r_dev = gathered.shape[0] // N_DEV
    offs, off = [], 0
    for n in sizes:
        offs.append(off)
        off += (n + PACK_TILE - 1) // PACK_TILE * SUBLANES
    assert off == r_dev

    def body(*refs):
        g_ref = refs[0]
        w_refs, m_refs, v_refs = refs[1:1 + n_w], refs[1 + n_w:1 + 2 * n_w], refs[1 + 2 * n_w:1 + 3 * n_w]
        outs = refs[1 + 3 * n_w:]

        def total(i, shape):
            r, cdim = shape
            acc = g_ref[pl.ds(offs[i], r), :cdim]
            for d in range(1, N_DEV):
                acc = acc + g_ref[pl.ds(d * r_dev + offs[i], r), :cdim]
            return acc

        for i in range(n_w):
            g = total(i, w_refs[i].shape)
            d, mn, vn = _adam_math(w_refs[i][...], g, m_refs[i][...], v_refs[i][...])
            for o, val in zip(outs[4 * i:4 * i + 4], (g, d, mn, vn)):
                o[...] = val
        outs[4 * n_w][...] = total(n_w, (SUBLANES, LANES))

    shapes = [w.shape for w in ws]
    return pl.pallas_call(
        body, name=name,
        out_shape=[jax.ShapeDtypeStruct(shp, F32) for shp in shapes for _ in range(4)]
        + [jax.ShapeDtypeStruct((SUBLANES, LANES), F32)],
        compiler_params=_params(),
    )(gathered, *ws, *ms, *vs)


def _step(x, mem, target, shards, sm, core):
    g, w, sums, reduced = {}, {}, {}, {}

    def gather(names):
        return _gather_side(names, [shards[n] for n in names])

    def to_sibling(names):
        return _sibling_side(names, [g[n] for n in names])

    def add_sibling(names, received):
        for n, r in zip(names, received):
            sums[n] = _reduce_add(g[n], r, BIG[BIG_INDEX[n]][2], core, "reduce_add_" + n)

    def to_chips(names):
        return _chips_side([sums[n][1] for n in names])

    def keep(names, received):
        for n, r in zip(names, received):
            reduced[n] = (sums[n][0], r)

    row = lambda a: a.reshape(1, -1)
    g_mix, g_xa, g_mlp, g_mem = row(sm["g_mix"]), row(sm["g_xa"]), row(sm["g_mlp"]), row(sm["g_mem"])
    g_os, g_ob = row(sm["g_out_ssm"]), row(sm["g_out_sb"])
    sb_gq, sb_gk = jnp.tile(row(sm["sb_g_q"]), (1, SB_HEADS)), jnp.tile(row(sm["sb_g_k"]), (1, SB_HEADS))
    xa_gq, xa_gk = row(sm["xa_g_q"]), row(sm["xa_g_k"])
    d_skip = row(sm["ssm_d"])

    h1, (w["w_in"],) = _norm_fwd(x, g_mix, "norm_mix", side=_gather_two_level_side("w_in", shards["w_in"]))
    proj = _mm(h1, w["w_in"], "nn", "in_proj")
    u = _to_segments(proj[:, :SSM_WIDTH])
    q_raw, k_raw = (proj, SB_WIDTH, 1), (proj, SB_WIDTH, 2)
    v_col = (SSM_WIDTH + 2 * SB_WIDTH) // LANES
    sb_scale = SB_HEAD_DIM ** -0.5
    qs, ks = _rw(lambda qt, kt, gq, gk: ((_rms_groups(qt, gq, sb_scale), _rms_groups(kt, gk, 1.0)), ()),
                 [q_raw, k_raw], [sb_gq, sb_gk], [(SB_WIDTH, BF16)] * 2, [], "sb_qk_norm")
    early = ["ssm_w_glu", "w_out", "xa_w_q", "xa_w_kv", "xa_w_o", "w_up"]
    y_sb, got = _sb_fwd(qs, ks, proj, "sb_fwd", v_col=v_col, side=gather(early))
    w.update(zip(early, got))

    ssm_args = (sm["ssm_a_re"], sm["ssm_a_im"], sm["ssm_log_dt"], sm["ssm_b_re"], sm["ssm_b_im"],
                sm["ssm_c_re"], sm["ssm_c_im"])
    (acat, bsup, csup), mats_vjp = jax.vjp(_ssm_mats, *ssm_args)
    (states, y0, y1), (w["w_down"],) = _ssm_fwd(u, acat, bsup, csup, d_skip, "ssm_fwd", side=gather(["w_down"]))
    z_glu, y_ssm = _mm(y1, w["ssm_w_glu"], "nn", "ssm_glu", epi=lambda r, yt: (r, yt * jax.nn.sigmoid(r)),
                       extras=(y1,), out_dtypes=(F32, F32))
    y_ssm = _from_segments(y_ssm)

    def cat_norm(a, b, ga, gb):
        return jnp.concatenate([_rms(a, ga), _rms(b, gb)], axis=1)

    ycat = _rw(lambda a, b, ga, gb: ((cat_norm(a, b, ga, gb),), ()), [y_ssm, y_sb], [g_os, g_ob],
               [(D_MODEL, BF16)], [], "norm_out")[0]

    def residual_norm_epi(r, xt, gt):
        xn = r + xt
        return xn, _rms(xn, gt)

    x1, h2 = _mm(ycat, w["w_out"], "nn", "out_proj", epi=residual_norm_epi, extras=(x,), fulls=(g_xa,),
                 out_dtypes=(F32, BF16))
    qx = _mm(h2, w["xa_w_q"], "nn", "xa_q")
    memn, kv, kn_x, vv_x = _mem_fwd(mem, g_mem, w["xa_w_kv"], xa_gk, "xa_mem")
    o_xa = _xa_fwd(qx, xa_gq, kn_x, vv_x, "xa_fwd")
    x2, h3 = _mm(o_xa, w["xa_w_o"], "nn", "xa_o", epi=residual_norm_epi, extras=(x1,), fulls=(g_mlp,),
                 out_dtypes=(F32, BF16))

    def up_epi(r):
        rl = jnp.maximum(r, 0.0)
        return (rl * rl,)

    r_up = _mm(h3, w["w_up"], "nn", "mlp_up", epi=up_epi, out_dtypes=(BF16,))

    def loss_epi(r, xt, tt):
        d = r + xt - tt
        return (d * (1.0 / D_MODEL),) * 2, (jnp.sum(d * d, axis=0, keepdims=True),)

    dx3, dx3_b, sq = _mm(r_up, w["w_down"], "nn", "mlp_down", epi=loss_epi, extras=(x2, target),
                         out_dtypes=(F32, BF16), sums=[(1, D_MODEL)])
    loss = jnp.sum(sq) * (0.5 / D_MODEL)

    def norm_bwd_epi(r, xt, drt, gt):
        _, vjp = jax.vjp(_rms, xt, gt)
        dx_, dg_ = vjp(r)
        return (dx_ + drt,) * 2, (dg_,)

    g["w_down"] = _mm(r_up, dx3_b, "tn", "d_w_down", tk=2048)
    da = _mm(dx3_b, w["w_down"], "nt", "d_r", epi=lambda r, rt: (r * 2.0 * jnp.sqrt(rt.astype(F32)),), extras=(r_up,),
             out_dtypes=(BF16,))
    g["w_up"] = _mm(h3, da, "tn", "d_w_up", tk=2048)
    mlp = ["w_down", "w_up"]
    (dx2, dx2_b, g["g_mlp"]), got = _mm(da, w["w_up"], "nt", "d_h3", epi=norm_bwd_epi, extras=(x2, dx3),
                                        fulls=(g_mlp,), out_dtypes=(F32, BF16), sums=[g_mlp.shape],
                                        side=to_sibling(mlp))
    add_sibling(mlp, got)
    g["xa_w_o"] = _mm(o_xa, dx2_b, "tn", "d_xa_w_o", tk=2048)
    do_xa = _mm(dx2_b, w["xa_w_o"], "nt", "d_o_xa")
    dqx, dkn_x, dvv_x, g["xa_g_q"] = _xa_bwd(qx, xa_gq, kn_x, vv_x, do_xa, "xa_bwd")
    g["xa_w_kv"], g["g_mem"], g["xa_g_k"] = _mem_bwd(mem, g_mem, memn, w["xa_w_kv"], kv, xa_gk, dkn_x, dvv_x,
                                                     "xa_mem_bwd")
    g["xa_w_q"] = _mm(h2, dqx, "tn", "d_xa_w_q", tk=2048)
    dx1, dx1_b, g["g_xa"] = _mm(dqx, w["xa_w_q"], "nt", "d_h2", epi=norm_bwd_epi, extras=(x1, dx2), fulls=(g_xa,),
                                out_dtypes=(F32, BF16), sums=[g_xa.shape])
    g["w_out"] = _mm(ycat, dx1_b, "tn", "d_w_out", tk=2048)
    dycat = _mm(dx1_b, w["w_out"], "nt", "d_ycat")

    def cat_bwd(a, b, dy, ga, gb):
        _, vjp = jax.vjp(cat_norm, a, b, ga, gb)
        da_, db_, dga, dgb = vjp(dy)
        return (da_, db_), (dga, dgb)

    dy_ssm, dy_sb, g["g_out_ssm"], g["g_out_sb"] = _rw(
        cat_bwd, [y_ssm, y_sb, dycat], [g_os, g_ob], [(SSM_WIDTH, F32), (SB_WIDTH, F32)], [g_os.shape, g_ob.shape],
        "d_norm_out")

    def glu_bwd(dy, yt, zt):
        sg = jax.nn.sigmoid(zt)
        return (dy * sg, dy * yt * sg * (1.0 - sg)), ()

    dy1_a, dz = _rw(glu_bwd, [_to_segments(dy_ssm), y1, z_glu], [], [(SSM_WIDTH, F32), (SSM_WIDTH, BF16)], [], "d_glu")
    g["ssm_w_glu"] = _mm(y1, dz, "tn", "d_w_glu", tk=2048)

    def gelu_bwd_epi(r, da_, y0t):
        _, vjp = jax.vjp(jax.nn.gelu, y0t)
        return (vjp(r + da_)[0],)

    mid = ["w_out", "xa_w_q", "xa_w_kv", "xa_w_o", "ssm_w_glu"]
    dy0, got = _mm(dz, w["ssm_w_glu"], "nt", "d_y1", epi=gelu_bwd_epi, extras=(dy1_a, y0), side=to_sibling(mid))
    add_sibling(mid, got)
    (du, da8, d_bsup, d_csup, g["ssm_d"]), got = _ssm_bwd(dy0, states, u, acat, bsup, csup, d_skip, "ssm_bwd",
                                                          side=to_chips(mlp))
    keep(mlp, got)
    d_acat = jnp.sum(da8, axis=0, keepdims=True)
    for nm, val in zip(("ssm_a_re", "ssm_a_im", "ssm_log_dt", "ssm_b_re", "ssm_b_im", "ssm_c_re", "ssm_c_im"),
                       mats_vjp((d_acat, d_bsup, d_csup))):
        g[nm] = val

    (dqs, dks, dvs), got = _sb_bwd(qs, ks, proj, y_sb, dy_sb, "sb_bwd", v_col=v_col, side=to_chips(mid))
    keep(mid, got)

    def d_proj_rows(du_t, qt, dqt, kt, dkt, dvt, gq, gk):
        _, vjp_q = jax.vjp(lambda a, b_: _rms_groups(a, b_, sb_scale), qt, gq)
        _, vjp_k = jax.vjp(lambda a, b_: _rms_groups(a, b_, 1.0), kt, gk)
        (dq_, dgq_), (dk_, dgk_) = vjp_q(dqt), vjp_k(dkt)
        rows = jnp.concatenate([du_t, dq_.astype(BF16), dk_.astype(BF16), dvt.astype(BF16)], axis=1)
        return (rows,), (dgq_, dgk_)

    dproj, dgq, dgk = _rw(d_proj_rows, [_from_segments(du), q_raw, dqs, k_raw, dks, dvs], [sb_gq, sb_gk],
                          [(IN_WIDTH, BF16)], [sb_gq.shape, sb_gk.shape], "d_proj")
    g["sb_g_q"] = jnp.sum(dgq.reshape(SB_HEADS, SB_HEAD_DIM), axis=0)
    g["sb_g_k"] = jnp.sum(dgk.reshape(SB_HEADS, SB_HEAD_DIM), axis=0)
    g["w_in"] = _mm(h1, dproj, "tn", "d_w_in", tk=2048)
    dh1, got = _mm(dproj, w["w_in"], "nt", "d_h1", side=to_sibling(["w_in"]))
    add_sibling(["w_in"], got)
    dx, g["g_mix"] = _norm_bwd(x, g_mix, dh1, dx1, "d_norm_mix")

    packed = _pack_small([g[n] for n in SMALL] + [loss.reshape(1)])
    everyone, got = _all_gather(packed, "gather_small", to_chips(["w_in"]))
    keep(["w_in"], got)
    return dx, everyone, reduced


def kernel(x, mem, g_mix, w_in, ssm_a_re, ssm_a_im, ssm_log_dt, ssm_b_re, ssm_b_im, ssm_c_re, ssm_c_im, ssm_d, ssm_w_glu, sb_g_q, sb_g_k, g_out_ssm, g_out_sb, w_out, g_xa, g_mem, xa_w_q, xa_w_kv, xa_g_q, xa_g_k, xa_w_o, g_mlp, w_up, w_down, loss_target, m_g_mix, m_w_in, m_ssm_a_re, m_ssm_a_im, m_ssm_log_dt, m_ssm_b_re, m_ssm_b_im, m_ssm_c_re, m_ssm_c_im, m_ssm_d, m_ssm_w_glu, m_sb_g_q, m_sb_g_k, m_g_out_ssm, m_g_out_sb, m_w_out, m_g_xa, m_g_mem, m_xa_w_q, m_xa_w_kv, m_xa_g_q, m_xa_g_k, m_xa_w_o, m_g_mlp, m_w_up, m_w_down, v_g_mix, v_w_in, v_ssm_a_re, v_ssm_a_im, v_ssm_log_dt, v_ssm_b_re, v_ssm_b_im, v_ssm_c_re, v_ssm_c_im, v_ssm_d, v_ssm_w_glu, v_sb_g_q, v_sb_g_k, v_g_out_ssm, v_g_out_sb, v_w_out, v_g_xa, v_g_mem, v_xa_w_q, v_xa_w_kv, v_xa_g_q, v_xa_g_k, v_xa_w_o, v_g_mlp, v_w_up, v_w_down):
    given = dict(locals())
    order = ["g_mix", "w_in", "ssm_a_re", "ssm_a_im", "ssm_log_dt", "ssm_b_re", "ssm_b_im", "ssm_c_re", "ssm_c_im",
             "ssm_d", "ssm_w_glu", "sb_g_q", "sb_g_k", "g_out_ssm", "g_out_sb", "w_out", "g_xa", "g_mem", "xa_w_q",
             "xa_w_kv", "xa_g_q", "xa_g_k", "xa_w_o", "g_mlp", "w_up", "w_down"]
    assert sorted([n for n, _, _ in BIG] + SMALL) == sorted(order)
    core = lax.axis_index("c").astype(jnp.int32).reshape(1)
    chip = (2 * lax.axis_index("x") + lax.axis_index("y")).astype(jnp.int32).reshape(1)

    shards = {n: given[n][0].astype(BF16) for n, _, _ in BIG}
    sm = {n: given[n][0] for n in SMALL}
    dx, everyone, reduced = _step(x[0], mem[0], loss_target[0], shards, sm, core)

    res = {}
    for n, _, _ in BIG:
        own, recv = reduced[n]
        outs = _adam_sharded(own, recv, given[n][0], given["m_" + n][0], given["v_" + n][0], chip, "adam_" + n)
        for kind, val in zip(("grad", "delta", "new_m", "new_v"), outs):
            res[kind + "_" + n] = val[None]

    sizes = [math.prod(sm[n].shape) for n in SMALL] + [1]
    nat = lambda a: a.reshape(_natural_2d(math.prod(a.shape)))
    outs = _adam_replicated(everyone, sizes, [nat(sm[n]) for n in SMALL], [nat(given["m_" + n][0]) for n in SMALL],
                            [nat(given["v_" + n][0]) for n in SMALL], "adam_replicated")
    for i, n in enumerate(SMALL):
        for kind, val in zip(("grad", "delta", "new_m", "new_v"), outs[4 * i:4 * i + 4]):
            res[kind + "_" + n] = val.reshape(given[n].shape)
    loss_out = outs[-1][0, 0]
    return (loss_out, dx[None], *[res["grad_" + n] for n in order], *[res["delta_" + n] for n in order],
            *[res["new_m_" + n] for n in order], *[res["new_v_" + n] for n in order])
```

```python
import functools
import math

import jax
import jax.numpy as jnp
from jax import lax
from jax.experimental import pallas as pl
from jax.experimental.pallas import tpu as pltpu

F32 = jnp.float32
BF16 = jnp.bfloat16
MESH = pl.DeviceIdType.MESH

N_DEV = 8
D_MODEL = 1024
SSM_WIDTH = 512
SSM_GROUP = 16
SSM_GROUPS = 32
SSM_STATE = 64
N_STATE = SSM_GROUPS * SSM_STATE
SB_HEADS = 8
SB_HEAD_DIM = 64
SB_WIDTH = 512
IN_WIDTH = 2048
XA_HEADS = 4
XA_HEAD_DIM = 128
XA_WIDTH = 512
D_FF = 4096
NORM_EPS = 1e-6
ADAM_LR = 0.001
ADAM_B1 = 0.9
ADAM_B2 = 0.999
ADAM_EPS = 1e-08
ADAM_WD = 0.01
ADAM_STEP = 10

LANES = 128
SUBLANES = 8
VMEM_LIMIT = 56 * 1024 * 1024
SCAN_LANES = 512
SB_BLOCK = 256
SB_UNDERFLOW = -110.0

NN = (((1,), (0,)), ((), ()))
NT = (((1,), (1,)), ((), ()))
TN = (((0,), (0,)), ((), ()))


def _params(sem=None):
    return pltpu.CompilerParams(dimension_semantics=sem, vmem_limit_bytes=VMEM_LIMIT)


def _dot(a, b, dims=NN):
    return lax.dot_general(a.astype(BF16), b.astype(BF16), dims, preferred_element_type=F32)


def _rms(x, g):
    return x * lax.rsqrt(jnp.mean(x * x, axis=-1, keepdims=True) + NORM_EPS) * g


ANY = pl.BlockSpec(memory_space=pl.ANY)


class _Side:
    def __init__(self, ins, out_shapes, n_sem, make, finish=None):
        self.ins, self.out_shapes, self.n_sem, self.make = list(ins), list(out_shapes), n_sem, make
        self.finish = finish

    def sems(self):
        return [pltpu.SemaphoreType.DMA((self.n_sem,)), pltpu.SemaphoreType.DMA((self.n_sem,))]


def _hosted(body, side, n_in, n_out, grid):
    if side is None:
        return body
    ns_in, ns_out = len(side.ins), len(side.out_shapes)

    def wrapped(*refs):
        ins, refs = refs[:n_in], refs[n_in:]
        s_ins, refs = refs[:ns_in], refs[ns_in:]
        outs, refs = refs[:n_out], refs[n_out:]
        s_outs, refs = refs[:ns_out], refs[ns_out:]
        scratch, sems = refs[:-2], refs[-2:]
        ids = [pl.program_id(d) for d in range(len(grid))]
        first = functools.reduce(jnp.logical_and, [i == 0 for i in ids])
        last = functools.reduce(jnp.logical_and, [i == n - 1 for i, n in zip(ids, grid)])

        @pl.when(first)
        def _():
            for cp in side.make(s_ins, s_outs, *sems):
                cp.start()

        body(*ins, *outs, *scratch)

        @pl.when(last)
        def _():
            if side.finish is not None:
                side.finish(s_ins, s_outs, *sems)
            else:
                for cp in side.make(s_ins, s_outs, *sems):
                    cp.wait()

    return wrapped


def _side_args(side):
    if side is None:
        return [], [], [], [], []
    return ([ANY] * len(side.ins), [ANY] * len(side.out_shapes), side.out_shapes, side.sems(), side.ins)


def _split_side(res, n_out, side):
    res = list(res)
    main = res[0] if n_out == 1 else res[:n_out]
    return main if side is None else (main, res[n_out:])


def _mm(a, b, mode, name, *, epi=None, extras=(), fulls=(), out_dtypes=(F32,), sums=(), tm=1024, tn=1024, tk=1024,
        side=None):
    if mode == "nn":
        (m, k), (k2, n) = a.shape, b.shape
    elif mode == "nt":
        (m, k), (n, k2) = a.shape, b.shape
    else:
        (k, m), (k2, n) = a.shape, b.shape
    assert k == k2, (name, a.shape, b.shape)
    tm, tn, tk = min(tm, m), min(tn, n), min(tk, k)
    assert m % tm == 0 and n % tn == 0 and k % tk == 0, (name, m, n, k)
    nk = k // tk
    dims = {"nn": NN, "nt": NT, "tn": TN}[mode]
    if mode == "tn":
        a_spec = pl.BlockSpec((tk, tm), lambda i, j, kk: (kk, i))
    else:
        a_spec = pl.BlockSpec((tm, tk), lambda i, j, kk: (i, kk))
    if mode == "nt":
        b_spec = pl.BlockSpec((tn, tk), lambda i, j, kk: (j, kk))
    else:
        b_spec = pl.BlockSpec((tk, tn), lambda i, j, kk: (kk, j))
    mn_spec = pl.BlockSpec((tm, tn), lambda i, j, kk: (i, j))
    n_ex, n_full, n_out, n_sum = len(extras), len(fulls), len(out_dtypes), len(sums)
    n_in = 2 + n_ex + n_full

    def body(*refs):
        a_ref, b_ref = refs[:2]
        ex = refs[2:n_in]
        outs = refs[n_in:n_in + n_out]
        sum_refs = refs[n_in + n_out:n_in + n_out + n_sum]
        kk = pl.program_id(2)
        first_tile = jnp.logical_and(pl.program_id(0) == 0, pl.program_id(1) == 0)

        def finish(r):
            vals = epi(r, *[e[...] for e in ex]) if epi is not None else (r,)
            if n_sum:
                vals, parts = vals

                @pl.when(first_tile)
                def _():
                    for sr in sum_refs:
                        sr[...] = jnp.zeros_like(sr)

                for sr, p in zip(sum_refs, parts):
                    sr[...] += p
            for o, v in zip(outs, vals):
                o[...] = v.astype(o.dtype)

        if nk == 1:
            finish(_dot(a_ref[...], b_ref[...], dims))
        else:
            acc = refs[n_in + n_out + n_sum]

            @pl.when(kk == 0)
            def _():
                acc[...] = jnp.zeros_like(acc)

            acc[...] += _dot(a_ref[...], b_ref[...], dims)

            @pl.when(kk == nk - 1)
            def _():
                finish(acc[...])

    grid = (m // tm, n // tn, nk)
    whole = lambda shape: pl.BlockSpec(shape, lambda i, j, kk: (0,) * len(shape))
    s_in, s_out, s_shape, s_scratch, s_ops = _side_args(side)
    seq = bool(side) or n_sum > 0
    res = pl.pallas_call(
        _hosted(body, side, n_in, n_out + n_sum, grid), name=name, grid=grid,
        in_specs=[a_spec, b_spec] + [mn_spec] * n_ex + [whole(f.shape) for f in fulls] + s_in,
        out_specs=[mn_spec] * n_out + [whole(shape) for shape in sums] + s_out,
        out_shape=[jax.ShapeDtypeStruct((m, n), dt) for dt in out_dtypes]
        + [jax.ShapeDtypeStruct(shape, F32) for shape in sums] + s_shape,
        scratch_shapes=([pltpu.VMEM((tm, tn), F32)] if nk > 1 else []) + s_scratch,
        compiler_params=_params(("arbitrary",) * 3 if seq else ("parallel", "parallel", "arbitrary")),
    )(a, b, *extras, *fulls, *s_ops)
    return _split_side(res, n_out + n_sum, side)


def _row_tile(s, target):
    if s <= target:
        return s
    return max(t for t in range(16, target + 1, 16) if s % t == 0)


def _rw(fn, rows, fulls, row_out, acc_out, name, tm=512, side=None):
    cols = [r[1:] if isinstance(r, tuple) else (r.shape[1], 0) for r in rows]
    rows = [r[0] if isinstance(r, tuple) else r for r in rows]
    s = rows[0].shape[0]
    tm = _row_tile(s, tm)
    nr, nf, nro, nao = len(rows), len(fulls), len(row_out), len(acc_out)

    def body(*refs):
        r = refs[:nr]
        f = refs[nr:nr + nf]
        ro = refs[nr + nf:nr + nf + nro]
        ao = refs[nr + nf + nro:]
        outs, accs = fn(*[x[...] for x in r], *[x[...] for x in f])
        for o, v in zip(ro, outs):
            o[...] = v.astype(o.dtype)
        if nao:
            @pl.when(pl.program_id(0) == 0)
            def _():
                for a in ao:
                    a[...] = jnp.zeros_like(a)

            for a, v in zip(ao, accs):
                a[...] += v

    full_spec = lambda shape: pl.BlockSpec(shape, lambda i: (0,) * len(shape))
    s_in, s_out, s_shape, s_scratch, s_ops = _side_args(side)
    res = pl.pallas_call(
        _hosted(body, side, nr + nf, nro + nao, (s // tm,)), name=name, grid=(s // tm,),
        in_specs=[pl.BlockSpec((tm, wd), functools.partial(lambda i, cb: (i, cb), cb=cb)) for wd, cb in cols]
        + [full_spec(x.shape) for x in fulls] + s_in,
        out_specs=[pl.BlockSpec((tm, d), lambda i: (i, 0)) for d, _ in row_out]
        + [full_spec(shape) for shape in acc_out] + s_out,
        out_shape=[jax.ShapeDtypeStruct((s, d), dt) for d, dt in row_out]
        + [jax.ShapeDtypeStruct(shape, F32) for shape in acc_out] + s_shape,
        scratch_shapes=s_scratch,
        compiler_params=_params(("arbitrary",)),
    )(*rows, *fulls, *s_ops)
    res = list(res)
    return res if side is None else (res[:nro + nao], res[nro + nao:])


def _norm_fwd(x, g, name, side=None):
    res = _rw(lambda xt, gt: ((_rms(xt, gt),), ()), [x], [g], [(x.shape[1], BF16)], [], name, side=side)
    return res[0] if side is None else (res[0][0], res[1])


def _norm_bwd(x, g, dh, dres, name, side=None):
    def fn(xt, dht, drt, gt):
        _, vjp = jax.vjp(_rms, xt, gt)
        dx, dg = vjp(dht)
        return (dx + drt,), (dg,)

    return _rw(fn, [x, dh, dres], [g], [(x.shape[1], F32)], [g.shape], name, side=side)


def _rms_groups(x, g, scale):
    lo = lax.broadcasted_iota(jnp.int32, (1, LANES), 1) < SB_HEAD_DIM
    x2 = x * x
    outs = []
    for cb in range(x.shape[1] // LANES):
        sl = slice(cb * LANES, (cb + 1) * LANES)
        s_lo = jnp.sum(jnp.where(lo, x2[:, sl], 0.0), axis=-1, keepdims=True)
        s_hi = jnp.sum(jnp.where(lo, 0.0, x2[:, sl]), axis=-1, keepdims=True)
        r = jnp.where(lo, lax.rsqrt(s_lo * (1.0 / SB_HEAD_DIM) + NORM_EPS),
                      lax.rsqrt(s_hi * (1.0 / SB_HEAD_DIM) + NORM_EPS))
        outs.append(x[:, sl] * r)
    return jnp.concatenate(outs, axis=1) * g * scale


def _log_sigmoid(z):
    return jnp.minimum(z, 0.0) - jnp.log(1.0 + jnp.exp(-jnp.abs(z)))


def _split_dot(x, u2):
    hi = x.astype(BF16)
    lo = (x - hi.astype(F32)).astype(BF16)
    return jnp.dot(jnp.concatenate([hi, lo], axis=1), u2, preferred_element_type=F32)


def _sb_consts(b):
    row = lax.broadcasted_iota(jnp.int32, (b, b), 0)
    col = lax.broadcasted_iota(jnp.int32, (b, b), 1)
    tri = col < row
    u_after = (row > col).astype(BF16)
    u_from = (row >= col).astype(BF16)
    stack = lambda u: jnp.concatenate([u, u], axis=0)
    lane_lo = lax.broadcasted_iota(jnp.int32, (b, LANES), 1) < SB_HEAD_DIM
    return tri, stack(u_after), stack(u_from), lane_lo


def _sb_scores(qh, kb, a_run, keep, u2_after, mask_l=True):
    z = lax.dot_general(qh, kb, NT, preferred_element_type=F32)
    lb = _log_sigmoid(z)
    l = lb - z
    if keep is not None and mask_l:
        l = jnp.where(keep, l, 0.0)
    w = jnp.exp(lb + (a_run + _split_dot(l, u2_after)))
    if keep is not None:
        w = jnp.where(keep, w, 0.0)
    return lb, l, w


def _sb_walk(qi, carry, step):
    def cond(state):
        n, c = state
        return jnp.logical_and(n <= qi, jnp.max(jnp.maximum(c[0], c[1])) > SB_UNDERFLOW)

    def body(state):
        n, c = state
        return n + 1, step(n, c)

    return lax.while_loop(cond, body, (jnp.int32(2), carry))[1]


def _two_heads(x, lane_lo):
    zero = jnp.zeros_like(x)
    return jnp.where(lane_lo, x, zero), jnp.where(lane_lo, zero, x)


def _sb_fwd(qs, ks, v, name, v_col=0, side=None):
    s, width = qs.shape
    b = min(SB_BLOCK, s)

    def body(q_ref, k_ref, v_ref, o_ref):
        qi = pl.program_id(1)
        tri, u2_after, _, lane_lo = _sb_consts(b)
        q_a, q_b = _two_heads(q_ref[...], lane_lo)

        def step(n, carry, keep, mask_l=True):
            a_a, a_b, acc = carry
            off = pl.multiple_of(jnp.maximum(qi - n, 0) * b, b)
            kb = k_ref[pl.ds(off, b), :]
            v_a, v_b = _two_heads(v_ref[pl.ds(off, b), :].astype(BF16), lane_lo)
            _, l_a, w_a = _sb_scores(q_a, kb, a_a, keep, u2_after, mask_l)
            _, l_b, w_b = _sb_scores(q_b, kb, a_b, keep, u2_after, mask_l)
            acc = acc + jnp.dot(jnp.concatenate([w_a.astype(BF16), w_b.astype(BF16)], axis=1),
                                jnp.concatenate([v_a, v_b], axis=0), preferred_element_type=F32)
            return (a_a + jnp.sum(l_a, axis=1, keepdims=True), a_b + jnp.sum(l_b, axis=1, keepdims=True), acc)

        zero = jnp.zeros((b, 1), F32)
        carry = step(0, (zero, zero, jnp.zeros((b, LANES), F32)), tri)
        carry = step(1, carry, jnp.broadcast_to(qi > 0, tri.shape), mask_l=False)
        carry = _sb_walk(qi, carry, lambda n, c: step(n, c, None))
        o_ref[...] = carry[2]

    blk = pl.BlockSpec((b, LANES), lambda hp, i: (i, hp))
    full = pl.BlockSpec((s, LANES), lambda hp, i: (0, hp))
    full_v = pl.BlockSpec((s, LANES), lambda hp, i: (0, hp + v_col))
    grid = (width // LANES, s // b)
    s_in, s_out, s_shape, s_scratch, s_ops = _side_args(side)
    res = pl.pallas_call(
        _hosted(body, side, 3, 1, grid), name=name, grid=grid,
        in_specs=[blk, full, full_v] + s_in, out_specs=[blk] + s_out,
        out_shape=[jax.ShapeDtypeStruct((s, width), F32)] + s_shape, scratch_shapes=s_scratch,
        compiler_params=_params(("arbitrary", "arbitrary")),
    )(qs, ks, v, *s_ops)
    return _split_side(res, 1, side)


def _sb_bwd(qs, ks, v, out, dout, name, v_col=0, side=None):
    s, width = qs.shape
    b = min(SB_BLOCK, s)
    nkb = s // b

    def body(q_ref, k_ref, v_ref, o_ref, do_ref, dq_ref, dk_ref, dv_ref):
        qi = pl.program_id(1)

        @pl.when(qi == 0)
        def _():
            dk_ref[...] = jnp.zeros_like(dk_ref)
            dv_ref[...] = jnp.zeros_like(dv_ref)

        tri, u2_after, u2_from, lane_lo = _sb_consts(b)
        q_a, q_b = _two_heads(q_ref[...], lane_lo)
        dob = do_ref[...].astype(BF16)
        do_a, do_b = _two_heads(dob, lane_lo)
        prod = dob.astype(F32) * o_ref[...]
        d_a = jnp.sum(jnp.where(lane_lo, prod, 0.0), axis=1, keepdims=True)
        d_b = jnp.sum(jnp.where(lane_lo, 0.0, prod), axis=1, keepdims=True)
        q_rows = jnp.concatenate([q_a, q_b], axis=0)
        do_rows = jnp.concatenate([do_a, do_b], axis=0)

        def head(qh, doh, kb, vb, a_run, d_rem, keep, mask_l):
            lb, l, w = _sb_scores(qh, kb, a_run, keep, u2_after, mask_l)
            wb = w.astype(BF16)
            g = lax.dot_general(doh, vb, NT, preferred_element_type=F32) * wb.astype(F32)
            g_before = d_rem - _split_dot(g, u2_from)
            dz = g - (g + g_before) * jnp.exp(lb)
            if keep is not None:
                dz = jnp.where(keep, dz, 0.0)
            return (dz.astype(BF16), wb, a_run + jnp.sum(l, axis=1, keepdims=True),
                    d_rem - jnp.sum(g, axis=1, keepdims=True))

        def step(n, carry, keep, mask_l=True):
            a_a, a_b, r_a, r_b, dq = carry
            jb = jnp.maximum(qi - n, 0)
            off = pl.multiple_of(jb * b, b)
            kb = k_ref[pl.ds(off, b), :]
            vb = v_ref[pl.ds(off, b), :].astype(BF16)
            k_a, k_b = _two_heads(kb, lane_lo)
            dz_a, w_a, a_a, r_a = head(q_a, do_a, kb, vb, a_a, r_a, keep, mask_l)
            dz_b, w_b, a_b, r_b = head(q_b, do_b, kb, vb, a_b, r_b, keep, mask_l)
            dq = dq + jnp.dot(jnp.concatenate([dz_a, dz_b], axis=1), jnp.concatenate([k_a, k_b], axis=0),
                              preferred_element_type=F32)
            dk_ref[pl.ds(off, b), :] += lax.dot_general(jnp.concatenate([dz_a, dz_b], axis=0), q_rows, TN,
                                                        preferred_element_type=F32)
            dv_ref[pl.ds(off, b), :] += lax.dot_general(jnp.concatenate([w_a, w_b], axis=0), do_rows, TN,
                                                        preferred_element_type=F32)
            return a_a, a_b, r_a, r_b, dq

        zero = jnp.zeros((b, 1), F32)
        carry = step(0, (zero, zero, d_a, d_b, jnp.zeros((b, LANES), F32)), tri)
        carry = step(1, carry, jnp.broadcast_to(qi > 0, tri.shape), mask_l=False)
        carry = _sb_walk(qi, carry, lambda n, c: step(n, c, None))
        dq_ref[...] = carry[4]

    blk = pl.BlockSpec((b, LANES), lambda hp, i: (i, hp))
    full = pl.BlockSpec((s, LANES), lambda hp, i: (0, hp))
    full_v = pl.BlockSpec((s, LANES), lambda hp, i: (0, hp + v_col))
    grid = (width // LANES, nkb)
    s_in, s_out, s_shape, s_scratch, s_ops = _side_args(side)
    res = pl.pallas_call(
        _hosted(body, side, 5, 3, grid), name=name, grid=grid,
        in_specs=[blk, full, full_v, blk, blk] + s_in, out_specs=[blk, full, full] + s_out,
        out_shape=[jax.ShapeDtypeStruct((s, width), F32)] * 3 + s_shape,
        scratch_shapes=s_scratch,
        compiler_params=_params(("arbitrary", "arbitrary")),
    )(qs, ks, v, out, dout, *s_ops)
    return _split_side(res, 3, side)


def _cmul(xr, xi, yr, yi):
    return xr * yr - xi * yi, xr * yi + xi * yr


def _scan_consts(ar, ai, reverse, lc):
    rowi = lax.broadcasted_iota(jnp.int32, (SUBLANES, lc), 0)
    pows = [(ar, ai)]
    for _ in range(SUBLANES - 1):
        pows.append(_cmul(*pows[-1], ar, ai))
    steps = []
    for d in (1, 2, 4):
        keep = (rowi < SUBLANES - d) if reverse else (rowi >= d)
        pr, pi = pows[d - 1]
        steps.append((SUBLANES - d if reverse else d, jnp.where(keep, pr, 0.0), jnp.where(keep, pi, 0.0)))
    cr = jnp.zeros((SUBLANES, lc), F32)
    ci = jnp.zeros((SUBLANES, lc), F32)
    for r in range(SUBLANES):
        pr, pi = pows[SUBLANES - 1 - r] if reverse else pows[r]
        cr = jnp.where(rowi == r, pr, cr)
        ci = jnp.where(rowi == r, pi, ci)
    return steps, cr, ci


def _scan_tile(xr, xi, steps, pr, pi, cr, ci):
    for shift, ar, ai in steps:
        rr = pltpu.roll(xr, shift, 0)
        ri = pltpu.roll(xi, shift, 0)
        xr, xi = xr + ar * rr - ai * ri, xi + ar * ri + ai * rr
    return xr + pr * cr - pi * ci, xi + pr * ci + pi * cr


SCAN_ROWS = 1024


def _scan_chunk(s):
    tt = min(SCAN_ROWS, s)
    seg = tt // SUBLANES
    assert s % tt == 0 and seg % SUBLANES == 0 and seg & (seg - 1) == 0, s
    return tt, seg


def _to_segments(a):
    s, wd = a.shape
    tt, seg = _scan_chunk(s)
    return jnp.transpose(a.reshape(s // tt, SUBLANES, seg, wd), (0, 2, 1, 3)).reshape(s, wd)


def _from_segments(a):
    s, wd = a.shape
    tt, seg = _scan_chunk(s)
    return jnp.transpose(a.reshape(s // tt, seg, SUBLANES, wd), (0, 2, 1, 3)).reshape(s, wd)


def _cpow2(xr, xi, k):
    for _ in range(k):
        xr, xi = _cmul(xr, xi, xr, xi)
    return xr, xi


def _fill_powers(pw_ref, ar, ai, seg, lc):
    _, p8r, p8i = _scan_consts(ar, ai, False, lc)
    a8r, a8i = _cpow2(ar, ai, 3)
    qr, qi = jnp.ones_like(ar), jnp.zeros_like(ai)
    for k in range(seg // SUBLANES):
        tr, ti = _cmul(p8r, p8i, qr, qi)
        for r in range(SUBLANES):
            rows = pl.ds((SUBLANES * k + r) * SUBLANES, SUBLANES)
            pw_ref[rows, :lc] = jnp.broadcast_to(tr[r:r + 1, :], (SUBLANES, lc))
            pw_ref[rows, lc:] = jnp.broadcast_to(ti[r:r + 1, :], (SUBLANES, lc))
        qr, qi = _cmul(qr, qi, a8r, a8i)


def _ssm_fwd(u, acat, bsup, csup, d_skip, name, side=None):
    s = u.shape[0]
    lc = SCAN_LANES
    tt, seg = _scan_chunk(s)
    nl, nt = N_STATE // lc, s // tt
    tile = lambda j: pl.ds(pl.multiple_of(j * SUBLANES, SUBLANES), SUBLANES)

    def body(u_ref, a_ref, b_ref, c_ref, d_ref, s_ref, y0_ref, y1_ref, carry, pw_ref):
        ar, ai = a_ref[:, :lc], a_ref[:, lc:]

        @pl.when(pl.program_id(1) == 0)
        def _():
            carry[...] = jnp.zeros_like(carry)
            _fill_powers(pw_ref, ar, ai, seg, lc)

        ut = u_ref[...]
        s_ref[...] = _dot(ut, b_ref[0])

        ar8, ai8 = jnp.broadcast_to(ar, (SUBLANES, lc)), jnp.broadcast_to(ai, (SUBLANES, lc))

        def local(j, x):
            xr = ar8 * x[0] - ai8 * x[1] + s_ref[tile(j), :lc]
            xi = ar8 * x[1] + ai8 * x[0] + s_ref[tile(j), lc:]
            s_ref[tile(j), :lc] = xr
            s_ref[tile(j), lc:] = xi
            return xr, xi

        zero = jnp.zeros((SUBLANES, lc), F32)
        er, ei = lax.fori_loop(0, seg, local, (zero, zero))
        steps, pr, pi = _scan_consts(*_cpow2(ar, ai, seg.bit_length() - 1), False, lc)
        cr, ci = carry[:, :lc], carry[:, lc:]
        tr, ti = _scan_tile(er, ei, steps, pr, pi, cr, ci)
        rowi = lax.broadcasted_iota(jnp.int32, (SUBLANES, lc), 0)
        before_r = jnp.where(rowi == 0, cr, pltpu.roll(tr, 1, 0))
        before_i = jnp.where(rowi == 0, ci, pltpu.roll(ti, 1, 0))
        carry[:, :lc] = jnp.broadcast_to(tr[SUBLANES - 1:, :], (SUBLANES, lc))
        carry[:, lc:] = jnp.broadcast_to(ti[SUBLANES - 1:, :], (SUBLANES, lc))

        def fix(j, _):
            pwr, pwi = pw_ref[tile(j), :lc], pw_ref[tile(j), lc:]
            s_ref[tile(j), :lc] += pwr * before_r - pwi * before_i
            s_ref[tile(j), lc:] += pwr * before_i + pwi * before_r
            return 0

        lax.fori_loop(0, seg, fix, 0)
        y0 = _dot(s_ref[...], c_ref[0], NT) + d_ref[...] * ut
        y0_ref[...] = y0
        y1_ref[...] = jax.nn.gelu(y0)

    chan = pl.BlockSpec((tt, LANES), lambda j, c: (c, j))
    sup = pl.BlockSpec((1, LANES, 2 * lc), lambda j, c: (j, 0, 0))
    s_in, s_out, s_shape, s_scratch, s_ops = _side_args(side)
    res = pl.pallas_call(
        _hosted(body, side, 5, 3, (nl, nt)), name=name, grid=(nl, nt),
        in_specs=[chan, pl.BlockSpec((1, 2 * lc), lambda j, c: (0, j)), sup, sup,
                  pl.BlockSpec((1, LANES), lambda j, c: (0, j))] + s_in,
        out_specs=[pl.BlockSpec((tt, 2 * lc), lambda j, c: (c, j)), chan, chan] + s_out,
        out_shape=[jax.ShapeDtypeStruct((s, 2 * N_STATE), F32), jax.ShapeDtypeStruct((s, SSM_WIDTH), F32),
                   jax.ShapeDtypeStruct((s, SSM_WIDTH), F32)] + s_shape,
        scratch_shapes=[pltpu.VMEM((SUBLANES, 2 * lc), F32), pltpu.VMEM((seg * SUBLANES, 2 * lc), F32)] + s_scratch,
        compiler_params=_params(("arbitrary", "arbitrary")),
    )(u, acat, bsup, csup, d_skip, *s_ops)
    return _split_side(res, 3, side)


def _ssm_bwd(dy0, states, u, acat, bsup, csup, d_skip, name, side=None):
    s = u.shape[0]
    lc = SCAN_LANES
    tt, seg = _scan_chunk(s)
    nl, nt = N_STATE // lc, s // tt
    tile = lambda j: pl.ds(pl.multiple_of(j * SUBLANES, SUBLANES), SUBLANES)

    def body(dy_ref, s_ref, sp_ref, u_ref, a_ref, b_ref, c_ref, d_ref,
             du_ref, da_ref, db_ref, dc_ref, dd_ref, lam_ref, carry, pw_ref):
        c = pl.program_id(1)
        ar, ai = a_ref[:, :lc], a_ref[:, lc:]

        @pl.when(c == 0)
        def _():
            carry[...] = jnp.zeros_like(carry)
            for r in (da_ref, db_ref, dc_ref, dd_ref):
                r[...] = jnp.zeros_like(r)
            _fill_powers(pw_ref, ar, ai, seg, lc)

        dy = dy_ref[...]
        ut = u_ref[...]
        lam_ref[...] = _dot(dy, c_ref[0])

        ar8, ai8 = jnp.broadcast_to(ar, (SUBLANES, lc)), jnp.broadcast_to(ai, (SUBLANES, lc))

        def local(i, x):
            j = seg - 1 - i
            xr = ar8 * x[0] + ai8 * x[1] + lam_ref[tile(j), :lc]
            xi = ar8 * x[1] - ai8 * x[0] + lam_ref[tile(j), lc:]
            lam_ref[tile(j), :lc] = xr
            lam_ref[tile(j), lc:] = xi
            return xr, xi

        zero = jnp.zeros((SUBLANES, lc), F32)
        er, ei = lax.fori_loop(0, seg, local, (zero, zero))
        big_r, big_i = _cpow2(ar, ai, seg.bit_length() - 1)
        steps, pr, pi = _scan_consts(big_r, -big_i, True, lc)
        cr, ci = carry[:, :lc], carry[:, lc:]
        tr, ti = _scan_tile(er, ei, steps, pr, pi, cr, ci)
        rowi = lax.broadcasted_iota(jnp.int32, (SUBLANES, lc), 0)
        after_r = jnp.where(rowi == SUBLANES - 1, cr, pltpu.roll(tr, SUBLANES - 1, 0))
        after_i = jnp.where(rowi == SUBLANES - 1, ci, pltpu.roll(ti, SUBLANES - 1, 0))
        carry[:, :lc] = jnp.broadcast_to(tr[:1, :], (SUBLANES, lc))
        carry[:, lc:] = jnp.broadcast_to(ti[:1, :], (SUBLANES, lc))

        start = c != nt - 1
        last_r = jnp.where(start, jnp.broadcast_to(sp_ref[SUBLANES - 1:, :lc], (SUBLANES, lc)), 0.0)
        last_i = jnp.where(start, jnp.broadcast_to(sp_ref[SUBLANES - 1:, lc:], (SUBLANES, lc)), 0.0)
        first_r = jnp.where(rowi == 0, last_r, pltpu.roll(s_ref[tile(seg - 1), :lc], 1, 0))
        first_i = jnp.where(rowi == 0, last_i, pltpu.roll(s_ref[tile(seg - 1), lc:], 1, 0))

        def fix(j, acc):
            dar, dai = acc
            k = seg - 1 - j
            pwr, pwi = pw_ref[tile(k), :lc], pw_ref[tile(k), lc:]
            lr = lam_ref[tile(j), :lc] + pwr * after_r + pwi * after_i
            li = lam_ref[tile(j), lc:] + pwr * after_i - pwi * after_r
            lam_ref[tile(j), :lc] = lr
            lam_ref[tile(j), lc:] = li
            jp = jnp.maximum(j - 1, 0)
            sr = jnp.where(j > 0, s_ref[tile(jp), :lc], first_r)
            si = jnp.where(j > 0, s_ref[tile(jp), lc:], first_i)
            return dar + lr * sr + li * si, dai + li * sr - lr * si

        dar, dai = lax.fori_loop(0, seg, fix, (zero, zero))
        da_ref[:, :lc] += dar
        da_ref[:, lc:] += dai
        lam = lam_ref[...].astype(BF16)
        du_ref[...] = (_dot(lam, b_ref[0], NT) + d_ref[...] * dy).astype(du_ref.dtype)
        db_ref[0] += _dot(ut, lam, TN)
        dc_ref[0] += _dot(dy, s_ref[...], TN)
        dd_ref[...] += jnp.sum(dy * ut, axis=0, keepdims=True)

    rev = lambda j, c: (nt - 1 - c, j)
    chan = pl.BlockSpec((tt, LANES), rev)
    sup = pl.BlockSpec((1, LANES, 2 * lc), lambda j, c: (j, 0, 0))
    row = pl.BlockSpec((1, LANES), lambda j, c: (0, j))
    s_in, s_out, s_shape, s_scratch, s_ops = _side_args(side)
    res = pl.pallas_call(
        _hosted(body, side, 8, 5, (nl, nt)), name=name, grid=(nl, nt),
        in_specs=[chan, pl.BlockSpec((tt, 2 * lc), rev),
                  pl.BlockSpec((SUBLANES, 2 * lc), lambda j, c: (jnp.maximum((nt - 1 - c) * seg - 1, 0), j)),
                  chan, pl.BlockSpec((1, 2 * lc), lambda j, c: (0, j)), sup, sup, row] + s_in,
        out_specs=[chan, pl.BlockSpec((SUBLANES, 2 * lc), lambda j, c: (0, j)), sup, sup, row] + s_out,
        out_shape=[jax.ShapeDtypeStruct((s, SSM_WIDTH), BF16), jax.ShapeDtypeStruct((SUBLANES, 2 * N_STATE), F32),
                   jax.ShapeDtypeStruct(bsup.shape, F32), jax.ShapeDtypeStruct(csup.shape, F32),
                   jax.ShapeDtypeStruct((1, SSM_WIDTH), F32)] + s_shape,
        scratch_shapes=[pltpu.VMEM((tt, 2 * lc), F32), pltpu.VMEM((SUBLANES, 2 * lc), F32),
                        pltpu.VMEM((seg * SUBLANES, 2 * lc), F32)] + s_scratch,
        compiler_params=_params(("arbitrary", "arbitrary")),
    )(dy0, states, states, u, acat, bsup, csup, d_skip, *s_ops)
    return _split_side(res, 5, side)


def _discretise(ar, ai, ldt, br, bi):
    dt = jnp.exp(ldt)
    lr, li = ar * dt, ai * dt
    e = jnp.exp(lr)
    abar_r, abar_i = e * jnp.cos(li), e * jnp.sin(li)
    den = ar * ar + ai * ai
    coef_r = ((abar_r - 1.0) * ar + abar_i * ai) / den
    coef_i = (abar_i * ar - (abar_r - 1.0) * ai) / den
    return abar_r, abar_i, coef_r * br - coef_i * bi, coef_r * bi + coef_i * br


def _group_mask():
    shape = (LANES, SCAN_LANES)
    return (lax.broadcasted_iota(jnp.int32, shape, 0) // SSM_GROUP
            == lax.broadcasted_iota(jnp.int32, shape, 1) // SSM_STATE)


def _ssm_mats_fwd(a_re, a_im, log_dt, b_re, b_im, c_re, c_im, name):
    nl = N_STATE // SCAN_LANES
    lc = SCAN_LANES

    def body(ar, ai, ldt, br, bi, cr, ci, acat, bsup, csup):
        abar_r, abar_i, bbar_r, bbar_i = _discretise(ar[...], ai[...], ldt[...], br[...], bi[...])
        same = _group_mask()
        spread = lambda m, j: jnp.where(same, jnp.tile(m[:, j * lc:(j + 1) * lc], (LANES // SSM_GROUP, 1)), 0.0)
        c_r, c_i = cr[...], -ci[...]
        for j in range(nl):
            acat[:, 2 * j * lc:(2 * j + 1) * lc] = abar_r[:, j * lc:(j + 1) * lc]
            acat[:, (2 * j + 1) * lc:(2 * j + 2) * lc] = abar_i[:, j * lc:(j + 1) * lc]
            bsup[j, :, :lc] = spread(bbar_r, j)
            bsup[j, :, lc:] = spread(bbar_i, j)
            csup[j, :, :lc] = spread(c_r, j)
            csup[j, :, lc:] = spread(c_i, j)

    return pl.pallas_call(
        body, name=name,
        out_shape=[jax.ShapeDtypeStruct((1, 2 * N_STATE), F32), jax.ShapeDtypeStruct((nl, LANES, 2 * lc), F32),
                   jax.ShapeDtypeStruct((nl, LANES, 2 * lc), F32)],
        compiler_params=_params(),
    )(a_re, a_im, log_dt, b_re, b_im, c_re, c_im)


def _ssm_mats_bwd(a_re, a_im, log_dt, b_re, b_im, d_acat, d_bsup, d_csup, name):
    nl = N_STATE // SCAN_LANES
    lc = SCAN_LANES

    def body(ar, ai, ldt, br, bi, dac, dbs, dcs, d_ar, d_ai, d_ldt, d_br, d_bi, d_cr, d_ci):
        same = _group_mask()

        def gather(ref, j, half):
            m = jnp.where(same, ref[j, :, half * lc:(half + 1) * lc], 0.0)
            tot = m[:SSM_GROUP]
            for k in range(1, LANES // SSM_GROUP):
                tot = tot + m[k * SSM_GROUP:(k + 1) * SSM_GROUP]
            return tot

        cols = lambda ref, half: jnp.concatenate([gather(ref, j, half) for j in range(nl)], axis=1)
        d_abar_r = jnp.concatenate([dac[:, 2 * j * lc:(2 * j + 1) * lc] for j in range(nl)], axis=1)
        d_abar_i = jnp.concatenate([dac[:, (2 * j + 1) * lc:(2 * j + 2) * lc] for j in range(nl)], axis=1)
        _, vjp = jax.vjp(_discretise, ar[...], ai[...], ldt[...], br[...], bi[...])
        outs = vjp((d_abar_r, d_abar_i, cols(dbs, 0), cols(dbs, 1)))
        for ref, val in zip((d_ar, d_ai, d_ldt, d_br, d_bi), outs):
            ref[...] = val
        d_cr[...] = cols(dcs, 0)
        d_ci[...] = -cols(dcs, 1)

    row = jax.ShapeDtypeStruct((1, N_STATE), F32)
    mat = jax.ShapeDtypeStruct((SSM_GROUP, N_STATE), F32)
    return pl.pallas_call(
        body, name=name, out_shape=[row, row, row, mat, mat, mat, mat], compiler_params=_params(),
    )(a_re, a_im, log_dt, b_re, b_im, d_acat, d_bsup, d_csup)


def _states_on_lanes(sm):
    flat = lambda a: a.reshape(1, N_STATE)
    chan_b = lambda b: jnp.transpose(b, (2, 0, 1)).reshape(SSM_GROUP, N_STATE)
    chan_c = lambda c: jnp.transpose(c, (1, 0, 2)).reshape(SSM_GROUP, N_STATE)
    return (flat(sm["ssm_a_re"]), flat(sm["ssm_a_im"]), flat(jnp.repeat(sm["ssm_log_dt"], SSM_STATE)),
            chan_b(sm["ssm_b_re"]), chan_b(sm["ssm_b_im"]), chan_c(sm["ssm_c_re"]), chan_c(sm["ssm_c_im"]))


def _from_states_on_lanes(d_ar, d_ai, d_ldt, d_br, d_bi, d_cr, d_ci):
    grp = lambda a: a.reshape(SSM_GROUPS, SSM_STATE)
    back_b = lambda b: jnp.transpose(b.reshape(SSM_GROUP, SSM_GROUPS, SSM_STATE), (1, 2, 0))
    back_c = lambda c: jnp.transpose(c.reshape(SSM_GROUP, SSM_GROUPS, SSM_STATE), (1, 0, 2))
    return (grp(d_ar), grp(d_ai), jnp.sum(grp(d_ldt), axis=1), back_b(d_br), back_b(d_bi), back_c(d_cr), back_c(d_ci))


def _mem_fwd(mem, g_mem, w_kv, g_k, name):
    ml = mem.shape[0]

    def body(mem_ref, gm_ref, w_ref, gk_ref, memn_ref, kv_ref, kn_ref, vv_ref):
        memn = _rms(mem_ref[...], gm_ref[...])
        memn_ref[...] = memn.astype(BF16)
        kv = _dot(memn, w_ref[...])
        kv_ref[...] = kv
        for hh in range(XA_HEADS):
            sl = slice(hh * XA_HEAD_DIM, (hh + 1) * XA_HEAD_DIM)
            kn_ref[:, sl] = _rms(kv[:, sl], gk_ref[...]).astype(BF16)
        vv_ref[...] = kv[:, XA_WIDTH:].astype(BF16)

    return pl.pallas_call(
        body, name=name,
        out_shape=[jax.ShapeDtypeStruct((ml, D_MODEL), BF16), jax.ShapeDtypeStruct((ml, 2 * XA_WIDTH), F32),
                   jax.ShapeDtypeStruct((ml, XA_WIDTH), BF16), jax.ShapeDtypeStruct((ml, XA_WIDTH), BF16)],
        compiler_params=_params(),
    )(mem, g_mem, w_kv, g_k)


def _mem_bwd(mem, g_mem, memn, w_kv, kv, g_k, dkn, dvv, name):
    def body(mem_ref, gm_ref, memn_ref, w_ref, kv_ref, gk_ref, dkn_ref, dvv_ref, dw_ref, dgm_ref, dgk_ref):
        kv = kv_ref[...]
        dgk = jnp.zeros(dgk_ref.shape, F32)
        parts = []
        for hh in range(XA_HEADS):
            sl = slice(hh * XA_HEAD_DIM, (hh + 1) * XA_HEAD_DIM)
            _, vjp = jax.vjp(_rms, kv[:, sl], gk_ref[...])
            dk, dg = vjp(dkn_ref[:, sl])
            parts.append(dk)
            dgk = dgk + dg
        dgk_ref[...] = dgk
        dkv = jnp.concatenate(parts + [dvv_ref[...]], axis=1)
        dw_ref[...] = _dot(memn_ref[...], dkv, TN)
        dmemn = _dot(dkv, w_ref[...], NT)
        _, vjp = jax.vjp(_rms, mem_ref[...], gm_ref[...])
        dgm_ref[...] = vjp(dmemn)[1]

    return pl.pallas_call(
        body, name=name,
        out_shape=[jax.ShapeDtypeStruct((D_MODEL, 2 * XA_WIDTH), F32), jax.ShapeDtypeStruct(g_mem.shape, F32),
                   jax.ShapeDtypeStruct(g_k.shape, F32)],
        compiler_params=_params(),
    )(mem, g_mem, memn, w_kv, kv, g_k, dkn, dvv)


def _xa_head(qx_h, g_q, kn_h, vv_h):
    qn = _rms(qx_h, g_q)
    sc = _dot(qn, kn_h, NT) * (XA_HEAD_DIM ** -0.5)
    sc = sc - jnp.max(sc, axis=-1, keepdims=True)
    e = jnp.exp(sc)
    p = e / jnp.sum(e, axis=-1, keepdims=True)
    return qn, p


def _xa_fwd(qx, g_q, kn, vv, name):
    def fn(qt, gq, knt, vvt):
        outs = []
        for hh in range(XA_HEADS):
            sl = slice(hh * XA_HEAD_DIM, (hh + 1) * XA_HEAD_DIM)
            _, p = _xa_head(qt[:, sl], gq, knt[:, sl], vvt[:, sl])
            outs.append(_dot(p, vvt[:, sl]))
        return (jnp.concatenate(outs, axis=1),), ()

    return _rw(fn, [qx], [g_q, kn, vv], [(XA_WIDTH, BF16)], [], name)[0]


def _xa_bwd(qx, g_q, kn, vv, do, name):
    def fn(qt, dot_, gq, knt, vvt):
        dqs, dks, dvs = [], [], []
        dgq = jnp.zeros_like(gq)
        for hh in range(XA_HEADS):
            sl = slice(hh * XA_HEAD_DIM, (hh + 1) * XA_HEAD_DIM)
            qn, p = _xa_head(qt[:, sl], gq, knt[:, sl], vvt[:, sl])
            doh = dot_[:, sl]
            dp = _dot(doh, vvt[:, sl], NT)
            dvs.append(_dot(p, doh, TN))
            ds = p * (dp - jnp.sum(dp * p, axis=-1, keepdims=True)) * (XA_HEAD_DIM ** -0.5)
            dqn = _dot(ds, knt[:, sl])
            dks.append(_dot(ds, qn, TN))
            _, vjp = jax.vjp(_rms, qt[:, sl], gq)
            dq, dg = vjp(dqn)
            dqs.append(dq)
            dgq = dgq + dg
        return ((jnp.concatenate(dqs, axis=1),),
                (jnp.concatenate(dks, axis=1), jnp.concatenate(dvs, axis=1), dgq))

    return _rw(fn, [qx, do], [g_q, kn, vv], [(XA_WIDTH, BF16)], [kn.shape, vv.shape, g_q.shape], name)


BIG = [
    ("w_in", (D_MODEL, IN_WIDTH), 1), ("ssm_w_glu", (SSM_WIDTH, SSM_WIDTH), 0), ("w_out", (D_MODEL, D_MODEL), 0),
    ("xa_w_q", (D_MODEL, XA_WIDTH), 0), ("xa_w_kv", (D_MODEL, 2 * XA_WIDTH), 0), ("xa_w_o", (XA_WIDTH, D_MODEL), 1),
    ("w_up", (D_MODEL, D_FF), 1), ("w_down", (D_FF, D_MODEL), 0),
]
BIG_INDEX = {n: i for i, (n, _, _) in enumerate(BIG)}


def _shard_shape(shape, axis):
    return tuple(d // N_DEV if i == axis else d for i, d in enumerate(shape))


def _shard_of(ref, axis, d):
    n = ref.shape[axis] // N_DEV
    return ref.at[pl.ds(d * n, n), :] if axis == 0 else ref.at[:, pl.ds(d * n, n)]


def _gather_side(names, shards):
    idxs = [BIG_INDEX[n] for n in names]

    def make(ins, outs, send_sems, recv_sems):
        x, y, c = lax.axis_index("x"), lax.axis_index("y"), lax.axis_index("c")
        cps = []
        for j, i in enumerate(idxs):
            mine = _shard_of(outs[j], BIG[i][2], 4 * x + 2 * y + c)
            cps.append(pltpu.make_async_copy(ins[j], mine, send_sems.at[N_DEV * j]))
            for rel in range(1, N_DEV):
                to = tuple(1 - p if rel >> bit & 1 else p for p, bit in ((x, 2), (y, 1), (c, 0)))
                cps.append(pltpu.make_async_remote_copy(
                    src_ref=ins[j], dst_ref=mine, send_sem=send_sems.at[N_DEV * j + rel],
                    recv_sem=recv_sems.at[N_DEV * j + rel], device_id=to, device_id_type=MESH))
        return cps

    return _Side(shards, [jax.ShapeDtypeStruct(BIG[i][1], BF16) for i in idxs], N_DEV * len(idxs), make)


def _gather_two_level_side(name, shard):
    i = BIG_INDEX[name]

    def parts(ins, outs, send_sems, recv_sems):
        x, y, c = lax.axis_index("x"), lax.axis_index("y"), lax.axis_index("c")
        sibling = (x, y, 1 - c)
        chips = [(1 - x, y), (x, 1 - y), (1 - x, 1 - y)]

        def place(dev):
            return _shard_of(outs[0], BIG[i][2], 4 * dev[0] + 2 * dev[1] + dev[2])

        def copy(k, blk, to, src=None):
            return pltpu.make_async_remote_copy(
                src_ref=place(blk) if src is None else src, dst_ref=place(blk), send_sem=send_sems.at[k],
                recv_sem=recv_sems.at[k], device_id=to, device_id_type=MESH)

        mine = pltpu.make_async_copy(ins[0], place((x, y, c)), send_sems.at[7])
        first = [copy(0, (x, y, c), sibling, src=ins[0])]
        first += [copy(1 + j, (x, y, c), (*chip, c), src=ins[0]) for j, chip in enumerate(chips)]
        passed = [copy(4 + j, (*chip, c), sibling) for j, chip in enumerate(chips)]
        arrived = [copy(1 + j, (*chip, c), (x, y, c)) for j, chip in enumerate(chips)]
        from_sibling = [copy(0, sibling, (x, y, c))] + [copy(4 + j, (*chip, 1 - c), (x, y, c))
                                                       for j, chip in enumerate(chips)]
        return mine, first, passed, arrived, from_sibling

    def make(ins, outs, send_sems, recv_sems):
        mine, first, _, _, _ = parts(ins, outs, send_sems, recv_sems)
        return [mine] + first

    def finish(ins, outs, send_sems, recv_sems):
        mine, first, passed, arrived, from_sibling = parts(ins, outs, send_sems, recv_sems)
        for got, onward in zip(arrived, passed):
            got.wait_recv()
            onward.start()
        for cp in from_sibling:
            cp.wait_recv()
        for cp in first + passed:
            cp.wait_send()
        mine.wait()

    return _Side([shard], [jax.ShapeDtypeStruct(BIG[i][1], BF16)], N_DEV, make, finish)


def _sibling_side(names, grads):
    idxs = [BIG_INDEX[n] for n in names]

    def make(ins, outs, send_sems, recv_sems):
        x, y, c = lax.axis_index("x"), lax.axis_index("y"), lax.axis_index("c")
        return [pltpu.make_async_remote_copy(
            src_ref=_shard_of(ins[j], BIG[i][2], 2 * k + (1 - c)), dst_ref=outs[j].at[k],
            send_sem=send_sems.at[4 * j + k], recv_sem=recv_sems.at[4 * j + k], device_id=(x, y, 1 - c),
            device_id_type=MESH) for j, i in enumerate(idxs) for k in range(4)]

    shapes = [jax.ShapeDtypeStruct((4,) + _shard_shape(BIG[i][1], BIG[i][2]), F32) for i in idxs]
    return _Side(grads, shapes, 4 * len(idxs), make)


def _chips_side(parts):
    def make(ins, outs, send_sems, recv_sems):
        x, y, c = lax.axis_index("x"), lax.axis_index("y"), lax.axis_index("c")
        chips = [(1 - x, y), (x, 1 - y), (1 - x, 1 - y)]
        return [pltpu.make_async_remote_copy(
            src_ref=ins[j].at[2 * cx + cy], dst_ref=outs[j].at[r], send_sem=send_sems.at[3 * j + r],
            recv_sem=recv_sems.at[3 * j + r], device_id=(cx, cy, c), device_id_type=MESH)
            for r, (cx, cy) in enumerate(chips) for j in range(len(parts))]

    return _Side(parts, [jax.ShapeDtypeStruct((3,) + p.shape[1:], p.dtype) for p in parts], 3 * len(parts), make)


def _reduce_add(grad, recv, axis, core, name):
    rs, cs = recv.shape[1:]
    rt = _row_tile(rs, 256)
    nt = rs // rt

    def body(c_ref, g_ref, r_ref, p_ref, pb_ref):
        sm = g_ref[...] + r_ref[0]
        p_ref[0] = sm
        pb_ref[0] = sm.astype(BF16)

    if axis == 0:
        g_spec = pl.BlockSpec((rt, cs), lambda k, t, c_ref: ((2 * k + c_ref[0]) * nt + t, 0))
    else:
        g_spec = pl.BlockSpec((rt, cs), lambda k, t, c_ref: (t, 2 * k + c_ref[0]))
    slab = pl.BlockSpec((1, rt, cs), lambda k, t, c_ref: (k, t, 0))
    return pl.pallas_call(
        body, name=name,
        grid_spec=pltpu.PrefetchScalarGridSpec(num_scalar_prefetch=1, grid=(4, nt), in_specs=[g_spec, slab],
                                               out_specs=[slab, slab]),
        out_shape=[jax.ShapeDtypeStruct(recv.shape, F32), jax.ShapeDtypeStruct(recv.shape, BF16)],
        compiler_params=_params(("parallel", "parallel")),
    )(core, grad, recv)


def _all_gather(block, name, side):
    m_per, n = block.shape
    ns_in, ns_out = len(side.ins), len(side.out_shapes)

    def body(*refs):
        x_ref, s_ins, out_ref = refs[0], refs[1:1 + ns_in], refs[1 + ns_in]
        s_outs = refs[2 + ns_in:2 + ns_in + ns_out]
        send_sems, recv_sems, local_sem, s_send, s_recv = refs[2 + ns_in + ns_out:]
        others = side.make(s_ins, s_outs, s_send, s_recv)
        for cp in others:
            cp.start()
        x, y, c = lax.axis_index("x"), lax.axis_index("y"), lax.axis_index("c")
        me, sibling = (x, y, c), (x, y, 1 - c)
        chips = [(1 - x, y), (x, 1 - y), (1 - x, 1 - y)]

        def rows(px, py, pc):
            return out_ref.at[pl.ds((4 * px + 2 * py + pc) * m_per, m_per), :]

        def copy(k, blk, to, src=None):
            return pltpu.make_async_remote_copy(
                src_ref=rows(*blk) if src is None else src, dst_ref=rows(*blk),
                send_sem=send_sems.at[k], recv_sem=recv_sems.at[k], device_id=to, device_id_type=MESH)

        mine = pltpu.make_async_copy(x_ref, rows(*me), local_sem)
        mine.start()
        first = [copy(0, me, sibling, src=x_ref)]
        first += [copy(1 + j, me, (*chip, c), src=x_ref) for j, chip in enumerate(chips)]
        for cp in first:
            cp.start()
        passed = [copy(4 + j, (*chip, c), sibling) for j, chip in enumerate(chips)]
        for j, chip in enumerate(chips):
            copy(1 + j, (*chip, c), me).wait_recv()
            passed[j].start()
        copy(0, sibling, me).wait_recv()
        for j, chip in enumerate(chips):
            copy(4 + j, (*chip, 1 - c), me).wait_recv()
        for cp in first + passed:
            cp.wait_send()
        mine.wait()
        for cp in others:
            cp.wait()

    res = pl.pallas_call(
        body, name=name, in_specs=[ANY] * (1 + ns_in), out_specs=[ANY] * (1 + ns_out),
        out_shape=[jax.ShapeDtypeStruct((N_DEV * m_per, n), block.dtype)] + side.out_shapes,
        scratch_shapes=[pltpu.SemaphoreType.DMA((7,)), pltpu.SemaphoreType.DMA((7,)), pltpu.SemaphoreType.DMA]
        + side.sems(),
    )(block, *side.ins)
    return res[0], list(res[1:])


def _adam_math(w, g, m, v):
    m = ADAM_B1 * m + (1.0 - ADAM_B1) * g
    v = ADAM_B2 * v + (1.0 - ADAM_B2) * (g * g)
    m_hat = m / (1.0 - ADAM_B1 ** ADAM_STEP)
    v_hat = v / (1.0 - ADAM_B2 ** ADAM_STEP)
    delta = -ADAM_LR * (m_hat / (jnp.sqrt(v_hat) + ADAM_EPS) + ADAM_WD * w)
    return delta, m, v


def _adam_sharded(own, recv, w, m, v, chip, name):
    rs, cs = w.shape
    rt = _row_tile(rs, 256)

    def body(chip_ref, p_ref, r_ref, w_ref, m_ref, v_ref, g_out, d_out, m_out, v_out):
        g = p_ref[0] + r_ref[0].astype(F32) + r_ref[1].astype(F32) + r_ref[2].astype(F32)
        d, mn, vn = _adam_math(w_ref[...], g, m_ref[...], v_ref[...])
        g_out[...] = g
        d_out[...] = d
        m_out[...] = mn
        v_out[...] = vn

    tile = pl.BlockSpec((rt, cs), lambda t, chip_ref: (t, 0))
    return pl.pallas_call(
        body, name=name,
        grid_spec=pltpu.PrefetchScalarGridSpec(
            num_scalar_prefetch=1, grid=(rs // rt,),
            in_specs=[pl.BlockSpec((1, rt, cs), lambda t, chip_ref: (chip_ref[0], t, 0)),
                      pl.BlockSpec((3, rt, cs), lambda t, chip_ref: (0, t, 0)), tile, tile, tile],
            out_specs=[tile] * 4),
        out_shape=[jax.ShapeDtypeStruct((rs, cs), F32)] * 4,
        compiler_params=_params(("parallel",)),
    )(chip, own, recv, w, m, v)


SMALL = ["g_mix", "ssm_a_re", "ssm_a_im", "ssm_log_dt", "ssm_b_re", "ssm_b_im", "ssm_c_re", "ssm_c_im", "ssm_d",
         "sb_g_q", "sb_g_k", "g_out_ssm", "g_out_sb", "g_xa", "g_mem", "xa_g_q", "xa_g_k", "g_mlp"]
PACK_TILE = SUBLANES * LANES


def _natural_2d(n):
    return (n // LANES, LANES) if n % LANES == 0 else (1, n)


def _pack_small(arrs):
    parts = []
    for a in arrs:
        flat = a.reshape(-1)
        parts.append(jnp.pad(flat, (0, (-flat.shape[0]) % PACK_TILE)))
    return jnp.concatenate(parts).reshape(-1, LANES)


def _adam_replicated(gathered, sizes, ws, ms, vs, name):
    n_w = len(ws)
    r_dev = gathered.shape[0] // N_DEV
    offs, off = [], 0
    for n in sizes:
        offs.append(off)
        off += (n + PACK_TILE - 1) // PACK_TILE * SUBLANES
    assert off == r_dev

    def body(*refs):
        g_ref = refs[0]
        w_refs, m_refs, v_refs = refs[1:1 + n_w], refs[1 + n_w:1 + 2 * n_w], refs[1 + 2 * n_w:1 + 3 * n_w]
        outs = refs[1 + 3 * n_w:]

        def total(i, shape):
            r, cdim = shape
            acc = g_ref[pl.ds(offs[i], r), :cdim]
            for d in range(1, N_DEV):
                acc = acc + g_ref[pl.ds(d * r_dev + offs[i], r), :cdim]
            return acc

        for i in range(n_w):
            g = total(i, w_refs[i].shape)
            d, mn, vn = _adam_math(w_refs[i][...], g, m_refs[i][...], v_refs[i][...])
            for o, val in zip(outs[4 * i:4 * i + 4], (g, d, mn, vn)):
                o[...] = val
        outs[4 * n_w][...] = total(n_w, (SUBLANES, LANES))

    shapes = [w.shape for w in ws]
    return pl.pallas_call(
        body, name=name,
        out_shape=[jax.ShapeDtypeStruct(shp, F32) for shp in shapes for _ in range(4)]
        + [jax.ShapeDtypeStruct((SUBLANES, LANES), F32)],
        compiler_params=_params(),
    )(gathered, *ws, *ms, *vs)


def _step(x, mem, target, shards, sm, core):
    g, w, sums, reduced = {}, {}, {}, {}

    def gather(names):
        return _gather_side(names, [shards[n] for n in names])

    def to_sibling(names):
        return _sibling_side(names, [g[n] for n in names])

    def add_sibling(names, received):
        for n, r in zip(names, received):
            sums[n] = _reduce_add(g[n], r, BIG[BIG_INDEX[n]][2], core, "reduce_add_" + n)

    def to_chips(names):
        return _chips_side([sums[n][1] for n in names])

    def keep(names, received):
        for n, r in zip(names, received):
            reduced[n] = (sums[n][0], r)

    row = lambda a: a.reshape(1, -1)
    g_mix, g_xa, g_mlp, g_mem = row(sm["g_mix"]), row(sm["g_xa"]), row(sm["g_mlp"]), row(sm["g_mem"])
    g_os, g_ob = row(sm["g_out_ssm"]), row(sm["g_out_sb"])
    sb_gq, sb_gk = jnp.tile(row(sm["sb_g_q"]), (1, SB_HEADS)), jnp.tile(row(sm["sb_g_k"]), (1, SB_HEADS))
    xa_gq, xa_gk = row(sm["xa_g_q"]), row(sm["xa_g_k"])
    d_skip = row(sm["ssm_d"])

    h1, (w["w_in"],) = _norm_fwd(x, g_mix, "norm_mix", side=_gather_two_level_side("w_in", shards["w_in"]))
    proj = _mm(h1, w["w_in"], "nn", "in_proj")
    u = _to_segments(proj[:, :SSM_WIDTH])
    q_raw, k_raw = (proj, SB_WIDTH, 1), (proj, SB_WIDTH, 2)
    v_col = (SSM_WIDTH + 2 * SB_WIDTH) // LANES
    sb_scale = SB_HEAD_DIM ** -0.5
    qs, ks = _rw(lambda qt, kt, gq, gk: ((_rms_groups(qt, gq, sb_scale), _rms_groups(kt, gk, 1.0)), ()),
                 [q_raw, k_raw], [sb_gq, sb_gk], [(SB_WIDTH, BF16)] * 2, [], "sb_qk_norm")
    early = ["ssm_w_glu", "w_out", "xa_w_q", "xa_w_kv", "xa_w_o", "w_up"]
    y_sb, got = _sb_fwd(qs, ks, proj, "sb_fwd", v_col=v_col, side=gather(early))
    w.update(zip(early, got))

    ssm_args = _states_on_lanes(sm)
    acat, bsup, csup = _ssm_mats_fwd(*ssm_args, "ssm_mats")
    (states, y0, y1), (w["w_down"],) = _ssm_fwd(u, acat, bsup, csup, d_skip, "ssm_fwd", side=gather(["w_down"]))
    z_glu, y_ssm = _mm(y1, w["ssm_w_glu"], "nn", "ssm_glu", epi=lambda r, yt: (r, yt * jax.nn.sigmoid(r)),
                       extras=(y1,), out_dtypes=(F32, F32))
    y_ssm = _from_segments(y_ssm)

    def cat_norm(a, b, ga, gb):
        return jnp.concatenate([_rms(a, ga), _rms(b, gb)], axis=1)

    ycat = _rw(lambda a, b, ga, gb: ((cat_norm(a, b, ga, gb),), ()), [y_ssm, y_sb], [g_os, g_ob],
               [(D_MODEL, BF16)], [], "norm_out")[0]

    def residual_norm_epi(r, xt, gt):
        xn = r + xt
        return xn, _rms(xn, gt)

    x1, h2 = _mm(ycat, w["w_out"], "nn", "out_proj", epi=residual_norm_epi, extras=(x,), fulls=(g_xa,),
                 out_dtypes=(F32, BF16))
    qx = _mm(h2, w["xa_w_q"], "nn", "xa_q")
    memn, kv, kn_x, vv_x = _mem_fwd(mem, g_mem, w["xa_w_kv"], xa_gk, "xa_mem")
    o_xa = _xa_fwd(qx, xa_gq, kn_x, vv_x, "xa_fwd")
    x2, h3 = _mm(o_xa, w["xa_w_o"], "nn", "xa_o", epi=residual_norm_epi, extras=(x1,), fulls=(g_mlp,),
                 out_dtypes=(F32, BF16))

    def up_epi(r):
        rl = jnp.maximum(r, 0.0)
        return (rl * rl,)

    r_up = _mm(h3, w["w_up"], "nn", "mlp_up", epi=up_epi, out_dtypes=(BF16,))

    def loss_epi(r, xt, tt):
        d = r + xt - tt
        return (d * (1.0 / D_MODEL),) * 2, (jnp.sum(d * d, axis=0, keepdims=True),)

    dx3, dx3_b, sq = _mm(r_up, w["w_down"], "nn", "mlp_down", epi=loss_epi, extras=(x2, target),
                         out_dtypes=(F32, BF16), sums=[(1, D_MODEL)])
    loss = jnp.sum(sq) * (0.5 / D_MODEL)

    def norm_bwd_epi(r, xt, drt, gt):
        _, vjp = jax.vjp(_rms, xt, gt)
        dx_, dg_ = vjp(r)
        return (dx_ + drt,) * 2, (dg_,)

    g["w_down"] = _mm(r_up, dx3_b, "tn", "d_w_down", tk=2048)
    da = _mm(dx3_b, w["w_down"], "nt", "d_r", epi=lambda r, rt: (r * 2.0 * jnp.sqrt(rt.astype(F32)),), extras=(r_up,),
             out_dtypes=(BF16,))
    g["w_up"] = _mm(h3, da, "tn", "d_w_up", tk=2048)
    mlp = ["w_down", "w_up"]
    (dx2, dx2_b, g["g_mlp"]), got = _mm(da, w["w_up"], "nt", "d_h3", epi=norm_bwd_epi, extras=(x2, dx3),
                                        fulls=(g_mlp,), out_dtypes=(F32, BF16), sums=[g_mlp.shape],
                                        side=to_sibling(mlp))
    add_sibling(mlp, got)
    g["xa_w_o"] = _mm(o_xa, dx2_b, "tn", "d_xa_w_o", tk=2048)
    do_xa = _mm(dx2_b, w["xa_w_o"], "nt", "d_o_xa")
    dqx, dkn_x, dvv_x, g["xa_g_q"] = _xa_bwd(qx, xa_gq, kn_x, vv_x, do_xa, "xa_bwd")
    g["xa_w_kv"], g["g_mem"], g["xa_g_k"] = _mem_bwd(mem, g_mem, memn, w["xa_w_kv"], kv, xa_gk, dkn_x, dvv_x,
                                                     "xa_mem_bwd")
    g["xa_w_q"] = _mm(h2, dqx, "tn", "d_xa_w_q", tk=2048)
    dx1, dx1_b, g["g_xa"] = _mm(dqx, w["xa_w_q"], "nt", "d_h2", epi=norm_bwd_epi, extras=(x1, dx2), fulls=(g_xa,),
                                out_dtypes=(F32, BF16), sums=[g_xa.shape])
    g["w_out"] = _mm(ycat, dx1_b, "tn", "d_w_out", tk=2048)
    dycat = _mm(dx1_b, w["w_out"], "nt", "d_ycat")

    def cat_bwd(a, b, dy, ga, gb):
        _, vjp = jax.vjp(cat_norm, a, b, ga, gb)
        da_, db_, dga, dgb = vjp(dy)
        return (da_, db_), (dga, dgb)

    dy_ssm, dy_sb, g["g_out_ssm"], g["g_out_sb"] = _rw(
        cat_bwd, [y_ssm, y_sb, dycat], [g_os, g_ob], [(SSM_WIDTH, F32), (SB_WIDTH, F32)], [g_os.shape, g_ob.shape],
        "d_norm_out")

    def glu_bwd(dy, yt, zt):
        sg = jax.nn.sigmoid(zt)
        return (dy * sg, dy * yt * sg * (1.0 - sg)), ()

    dy1_a, dz = _rw(glu_bwd, [_to_segments(dy_ssm), y1, z_glu], [], [(SSM_WIDTH, F32), (SSM_WIDTH, BF16)], [], "d_glu")
    g["ssm_w_glu"] = _mm(y1, dz, "tn", "d_w_glu", tk=2048)

    def gelu_bwd_epi(r, da_, y0t):
        _, vjp = jax.vjp(jax.nn.gelu, y0t)
        return (vjp(r + da_)[0],)

    mid = ["w_out", "xa_w_q", "xa_w_kv", "xa_w_o", "ssm_w_glu"]
    dy0, got = _mm(dz, w["ssm_w_glu"], "nt", "d_y1", epi=gelu_bwd_epi, extras=(dy1_a, y0), side=to_sibling(mid))
    add_sibling(mid, got)
    (du, da8, d_bsup, d_csup, g["ssm_d"]), got = _ssm_bwd(dy0, states, u, acat, bsup, csup, d_skip, "ssm_bwd",
                                                          side=to_chips(mlp))
    keep(mlp, got)
    d_acat = jnp.sum(da8, axis=0, keepdims=True)
    d_mats = _ssm_mats_bwd(*ssm_args[:5], d_acat, d_bsup, d_csup, "ssm_mats_bwd")
    for nm, val in zip(("ssm_a_re", "ssm_a_im", "ssm_log_dt", "ssm_b_re", "ssm_b_im", "ssm_c_re", "ssm_c_im"),
                       _from_states_on_lanes(*d_mats)):
        g[nm] = val

    (dqs, dks, dvs), got = _sb_bwd(qs, ks, proj, y_sb, dy_sb, "sb_bwd", v_col=v_col, side=to_chips(mid))
    keep(mid, got)

    def d_proj_rows(du_t, qt, dqt, kt, dkt, dvt, gq, gk):
        _, vjp_q = jax.vjp(lambda a, b_: _rms_groups(a, b_, sb_scale), qt, gq)
        _, vjp_k = jax.vjp(lambda a, b_: _rms_groups(a, b_, 1.0), kt, gk)
        (dq_, dgq_), (dk_, dgk_) = vjp_q(dqt), vjp_k(dkt)
        rows = jnp.concatenate([du_t, dq_.astype(BF16), dk_.astype(BF16), dvt.astype(BF16)], axis=1)
        return (rows,), (dgq_, dgk_)

    dproj, dgq, dgk = _rw(d_proj_rows, [_from_segments(du), q_raw, dqs, k_raw, dks, dvs], [sb_gq, sb_gk],
                          [(IN_WIDTH, BF16)], [sb_gq.shape, sb_gk.shape], "d_proj")
    g["sb_g_q"] = jnp.sum(dgq.reshape(SB_HEADS, SB_HEAD_DIM), axis=0)
    g["sb_g_k"] = jnp.sum(dgk.reshape(SB_HEADS, SB_HEAD_DIM), axis=0)
    g["w_in"] = _mm(h1, dproj, "tn", "d_w_in", tk=2048)
    dh1, got = _mm(dproj, w["w_in"], "nt", "d_h1", side=to_sibling(["w_in"]))
    add_sibling(["w_in"], got)
    dx, g["g_mix"] = _norm_bwd(x, g_mix, dh1, dx1, "d_norm_mix")

    packed = _pack_small([g[n] for n in SMALL] + [loss.reshape(1)])
    everyone, got = _all_gather(packed, "gather_small", to_chips(["w_in"]))
    keep(["w_in"], got)
    return dx, everyone, reduced


def kernel(x, mem, g_mix, w_in, ssm_a_re, ssm_a_im, ssm_log_dt, ssm_b_re, ssm_b_im, ssm_c_re, ssm_c_im, ssm_d, ssm_w_glu, sb_g_q, sb_g_k, g_out_ssm, g_out_sb, w_out, g_xa, g_mem, xa_w_q, xa_w_kv, xa_g_q, xa_g_k, xa_w_o, g_mlp, w_up, w_down, loss_target, m_g_mix, m_w_in, m_ssm_a_re, m_ssm_a_im, m_ssm_log_dt, m_ssm_b_re, m_ssm_b_im, m_ssm_c_re, m_ssm_c_im, m_ssm_d, m_ssm_w_glu, m_sb_g_q, m_sb_g_k, m_g_out_ssm, m_g_out_sb, m_w_out, m_g_xa, m_g_mem, m_xa_w_q, m_xa_w_kv, m_xa_g_q, m_xa_g_k, m_xa_w_o, m_g_mlp, m_w_up, m_w_down, v_g_mix, v_w_in, v_ssm_a_re, v_ssm_a_im, v_ssm_log_dt, v_ssm_b_re, v_ssm_b_im, v_ssm_c_re, v_ssm_c_im, v_ssm_d, v_ssm_w_glu, v_sb_g_q, v_sb_g_k, v_g_out_ssm, v_g_out_sb, v_w_out, v_g_xa, v_g_mem, v_xa_w_q, v_xa_w_kv, v_xa_g_q, v_xa_g_k, v_xa_w_o, v_g_mlp, v_w_up, v_w_down):
    given = dict(locals())
    order = ["g_mix", "w_in", "ssm_a_re", "ssm_a_im", "ssm_log_dt", "ssm_b_re", "ssm_b_im", "ssm_c_re", "ssm_c_im",
             "ssm_d", "ssm_w_glu", "sb_g_q", "sb_g_k", "g_out_ssm", "g_out_sb", "w_out", "g_xa", "g_mem", "xa_w_q",
             "xa_w_kv", "xa_g_q", "xa_g_k", "xa_w_o", "g_mlp", "w_up", "w_down"]
    assert sorted([n for n, _, _ in BIG] + SMALL) == sorted(order)
    core = lax.axis_index("c").astype(jnp.int32).reshape(1)
    chip = (2 * lax.axis_index("x") + lax.axis_index("y")).astype(jnp.int32).reshape(1)

    shards = {n: given[n][0].astype(BF16) for n, _, _ in BIG}
    sm = {n: given[n][0] for n in SMALL}
    dx, everyone, reduced = _step(x[0], mem[0], loss_target[0], shards, sm, core)

    res = {}
    for n, _, _ in BIG:
        own, recv = reduced[n]
        outs = _adam_sharded(own, recv, given[n][0], given["m_" + n][0], given["v_" + n][0], chip, "adam_" + n)
        for kind, val in zip(("grad", "delta", "new_m", "new_v"), outs):
            res[kind + "_" + n] = val[None]

    sizes = [math.prod(sm[n].shape) for n in SMALL] + [1]
    nat = lambda a: a.reshape(_natural_2d(math.prod(a.shape)))
    outs = _adam_replicated(everyone, sizes, [nat(sm[n]) for n in SMALL], [nat(given["m_" + n][0]) for n in SMALL],
                            [nat(given["v_" + n][0]) for n in SMALL], "adam_replicated")
    for i, n in enumerate(SMALL):
        for kind, val in zip(("grad", "delta", "new_m", "new_v"), outs[4 * i:4 * i + 4]):
            res[kind + "_" + n] = val.reshape(given[n].shape)
    loss_out = outs[-1][0, 0]
    return (loss_out, dx[None], *[res["grad_" + n] for n in order], *[res["delta_" + n] for n in order],
            *[res["new_m_" + n] for n in order], *[res["new_v_" + n] for n in order])
```

```python
import functools
import math

import jax
import jax.numpy as jnp
from jax import lax
from jax.experimental import pallas as pl
from jax.experimental.pallas import tpu as pltpu

F32 = jnp.float32
BF16 = jnp.bfloat16
MESH = pl.DeviceIdType.MESH

N_DEV = 8
D_MODEL = 1024
SSM_WIDTH = 512
SSM_GROUP = 16
SSM_GROUPS = 32
SSM_STATE = 64
N_STATE = SSM_GROUPS * SSM_STATE
SB_HEADS = 8
SB_HEAD_DIM = 64
SB_WIDTH = 512
IN_WIDTH = 2048
XA_HEADS = 4
XA_HEAD_DIM = 128
XA_WIDTH = 512
D_FF = 4096
NORM_EPS = 1e-6
ADAM_LR = 0.001
ADAM_B1 = 0.9
ADAM_B2 = 0.999
ADAM_EPS = 1e-08
ADAM_WD = 0.01
ADAM_STEP = 10

LANES = 128
SUBLANES = 8
VMEM_LIMIT = 56 * 1024 * 1024
SCAN_LANES = 512
SB_BLOCK = 256
SB_Q_BLOCKS = 2
SB_UNDERFLOW = -110.0

NN = (((1,), (0,)), ((), ()))
NT = (((1,), (1,)), ((), ()))
TN = (((0,), (0,)), ((), ()))


def _params(sem=None):
    return pltpu.CompilerParams(dimension_semantics=sem, vmem_limit_bytes=VMEM_LIMIT)


def _dot(a, b, dims=NN):
    return lax.dot_general(a.astype(BF16), b.astype(BF16), dims, preferred_element_type=F32)


def _rms(x, g):
    return x * lax.rsqrt(jnp.mean(x * x, axis=-1, keepdims=True) + NORM_EPS) * g


ANY = pl.BlockSpec(memory_space=pl.ANY)


class _Side:
    def __init__(self, ins, out_shapes, n_sem, make, finish=None):
        self.ins, self.out_shapes, self.n_sem, self.make = list(ins), list(out_shapes), n_sem, make
        self.finish = finish

    def sems(self):
        return [pltpu.SemaphoreType.DMA((self.n_sem,)), pltpu.SemaphoreType.DMA((self.n_sem,))]


def _hosted(body, side, n_in, n_out, grid):
    if side is None:
        return body
    ns_in, ns_out = len(side.ins), len(side.out_shapes)

    def wrapped(*refs):
        ins, refs = refs[:n_in], refs[n_in:]
        s_ins, refs = refs[:ns_in], refs[ns_in:]
        outs, refs = refs[:n_out], refs[n_out:]
        s_outs, refs = refs[:ns_out], refs[ns_out:]
        scratch, sems = refs[:-2], refs[-2:]
        ids = [pl.program_id(d) for d in range(len(grid))]
        first = functools.reduce(jnp.logical_and, [i == 0 for i in ids])
        last = functools.reduce(jnp.logical_and, [i == n - 1 for i, n in zip(ids, grid)])

        @pl.when(first)
        def _():
            for cp in side.make(s_ins, s_outs, *sems):
                cp.start()

        body(*ins, *outs, *scratch)

        @pl.when(last)
        def _():
            if side.finish is not None:
                side.finish(s_ins, s_outs, *sems)
            else:
                for cp in side.make(s_ins, s_outs, *sems):
                    cp.wait()

    return wrapped


def _side_args(side):
    if side is None:
        return [], [], [], [], []
    return ([ANY] * len(side.ins), [ANY] * len(side.out_shapes), side.out_shapes, side.sems(), side.ins)


def _split_side(res, n_out, side):
    res = list(res)
    main = res[0] if n_out == 1 else res[:n_out]
    return main if side is None else (main, res[n_out:])


def _mm(a, b, mode, name, *, epi=None, extras=(), fulls=(), out_dtypes=(F32,), sums=(), tm=1024, tn=1024, tk=1024,
        side=None):
    if mode == "nn":
        (m, k), (k2, n) = a.shape, b.shape
    elif mode == "nt":
        (m, k), (n, k2) = a.shape, b.shape
    else:
        (k, m), (k2, n) = a.shape, b.shape
    assert k == k2, (name, a.shape, b.shape)
    tm, tn, tk = min(tm, m), min(tn, n), min(tk, k)
    assert m % tm == 0 and n % tn == 0 and k % tk == 0, (name, m, n, k)
    nk = k // tk
    dims = {"nn": NN, "nt": NT, "tn": TN}[mode]
    if mode == "tn":
        a_spec = pl.BlockSpec((tk, tm), lambda i, j, kk: (kk, i))
    else:
        a_spec = pl.BlockSpec((tm, tk), lambda i, j, kk: (i, kk))
    if mode == "nt":
        b_spec = pl.BlockSpec((tn, tk), lambda i, j, kk: (j, kk))
    else:
        b_spec = pl.BlockSpec((tk, tn), lambda i, j, kk: (kk, j))
    mn_spec = pl.BlockSpec((tm, tn), lambda i, j, kk: (i, j))
    n_ex, n_full, n_out, n_sum = len(extras), len(fulls), len(out_dtypes), len(sums)
    n_in = 2 + n_ex + n_full

    def body(*refs):
        a_ref, b_ref = refs[:2]
        ex = refs[2:n_in]
        outs = refs[n_in:n_in + n_out]
        sum_refs = refs[n_in + n_out:n_in + n_out + n_sum]
        kk = pl.program_id(2)
        first_tile = jnp.logical_and(pl.program_id(0) == 0, pl.program_id(1) == 0)

        def finish(r):
            vals = epi(r, *[e[...] for e in ex]) if epi is not None else (r,)
            if n_sum:
                vals, parts = vals

                @pl.when(first_tile)
                def _():
                    for sr in sum_refs:
                        sr[...] = jnp.zeros_like(sr)

                for sr, p in zip(sum_refs, parts):
                    sr[...] += p
            for o, v in zip(outs, vals):
                o[...] = v.astype(o.dtype)

        if nk == 1:
            finish(_dot(a_ref[...], b_ref[...], dims))
        else:
            acc = refs[n_in + n_out + n_sum]

            @pl.when(kk == 0)
            def _():
                acc[...] = jnp.zeros_like(acc)

            acc[...] += _dot(a_ref[...], b_ref[...], dims)

            @pl.when(kk == nk - 1)
            def _():
                finish(acc[...])

    grid = (m // tm, n // tn, nk)
    whole = lambda shape: pl.BlockSpec(shape, lambda i, j, kk: (0,) * len(shape))
    s_in, s_out, s_shape, s_scratch, s_ops = _side_args(side)
    seq = bool(side) or n_sum > 0
    res = pl.pallas_call(
        _hosted(body, side, n_in, n_out + n_sum, grid), name=name, grid=grid,
        in_specs=[a_spec, b_spec] + [mn_spec] * n_ex + [whole(f.shape) for f in fulls] + s_in,
        out_specs=[mn_spec] * n_out + [whole(shape) for shape in sums] + s_out,
        out_shape=[jax.ShapeDtypeStruct((m, n), dt) for dt in out_dtypes]
        + [jax.ShapeDtypeStruct(shape, F32) for shape in sums] + s_shape,
        scratch_shapes=([pltpu.VMEM((tm, tn), F32)] if nk > 1 else []) + s_scratch,
        compiler_params=_params(("arbitrary",) * 3 if seq else ("parallel", "parallel", "arbitrary")),
    )(a, b, *extras, *fulls, *s_ops)
    return _split_side(res, n_out + n_sum, side)


def _row_tile(s, target):
    if s <= target:
        return s
    return max(t for t in range(16, target + 1, 16) if s % t == 0)


def _rw(fn, rows, fulls, row_out, acc_out, name, tm=512, side=None):
    cols = [r[1:] if isinstance(r, tuple) else (r.shape[1], 0) for r in rows]
    rows = [r[0] if isinstance(r, tuple) else r for r in rows]
    s = rows[0].shape[0]
    tm = _row_tile(s, tm)
    nr, nf, nro, nao = len(rows), len(fulls), len(row_out), len(acc_out)

    def body(*refs):
        r = refs[:nr]
        f = refs[nr:nr + nf]
        ro = refs[nr + nf:nr + nf + nro]
        ao = refs[nr + nf + nro:]
        outs, accs = fn(*[x[...] for x in r], *[x[...] for x in f])
        for o, v in zip(ro, outs):
            o[...] = v.astype(o.dtype)
        if nao:
            @pl.when(pl.program_id(0) == 0)
            def _():
                for a in ao:
                    a[...] = jnp.zeros_like(a)

            for a, v in zip(ao, accs):
                a[...] += v

    full_spec = lambda shape: pl.BlockSpec(shape, lambda i: (0,) * len(shape))
    s_in, s_out, s_shape, s_scratch, s_ops = _side_args(side)
    res = pl.pallas_call(
        _hosted(body, side, nr + nf, nro + nao, (s // tm,)), name=name, grid=(s // tm,),
        in_specs=[pl.BlockSpec((tm, wd), functools.partial(lambda i, cb: (i, cb), cb=cb)) for wd, cb in cols]
        + [full_spec(x.shape) for x in fulls] + s_in,
        out_specs=[pl.BlockSpec((tm, d), lambda i: (i, 0)) for d, _ in row_out]
        + [full_spec(shape) for shape in acc_out] + s_out,
        out_shape=[jax.ShapeDtypeStruct((s, d), dt) for d, dt in row_out]
        + [jax.ShapeDtypeStruct(shape, F32) for shape in acc_out] + s_shape,
        scratch_shapes=s_scratch,
        compiler_params=_params(("arbitrary",)),
    )(*rows, *fulls, *s_ops)
    res = list(res)
    return res if side is None else (res[:nro + nao], res[nro + nao:])


def _norm_fwd(x, g, name, side=None):
    res = _rw(lambda xt, gt: ((_rms(xt, gt),), ()), [x], [g], [(x.shape[1], BF16)], [], name, side=side)
    return res[0] if side is None else (res[0][0], res[1])


def _norm_bwd(x, g, dh, dres, name, side=None):
    def fn(xt, dht, drt, gt):
        _, vjp = jax.vjp(_rms, xt, gt)
        dx, dg = vjp(dht)
        return (dx + drt,), (dg,)

    return _rw(fn, [x, dh, dres], [g], [(x.shape[1], F32)], [g.shape], name, side=side)


def _rms_groups(x, g, scale):
    lo = lax.broadcasted_iota(jnp.int32, (1, LANES), 1) < SB_HEAD_DIM
    x2 = x * x
    outs = []
    for cb in range(x.shape[1] // LANES):
        sl = slice(cb * LANES, (cb + 1) * LANES)
        s_lo = jnp.sum(jnp.where(lo, x2[:, sl], 0.0), axis=-1, keepdims=True)
        s_hi = jnp.sum(jnp.where(lo, 0.0, x2[:, sl]), axis=-1, keepdims=True)
        r = jnp.where(lo, lax.rsqrt(s_lo * (1.0 / SB_HEAD_DIM) + NORM_EPS),
                      lax.rsqrt(s_hi * (1.0 / SB_HEAD_DIM) + NORM_EPS))
        outs.append(x[:, sl] * r)
    return jnp.concatenate(outs, axis=1) * g * scale


def _log_sigmoid(z):
    return jnp.minimum(z, 0.0) - jnp.log(1.0 + jnp.exp(-jnp.abs(z)))


def _split_dot(x, u2):
    hi = x.astype(BF16)
    lo = (x - hi.astype(F32)).astype(BF16)
    return jnp.dot(jnp.concatenate([hi, lo], axis=1), u2, preferred_element_type=F32)


def _sb_consts(b):
    row = lax.broadcasted_iota(jnp.int32, (b, b), 0)
    col = lax.broadcasted_iota(jnp.int32, (b, b), 1)
    tri = col < row
    u_after = (row > col).astype(BF16)
    u_from = (row >= col).astype(BF16)
    stack = lambda u: jnp.concatenate([u, u], axis=0)
    lane_lo = lax.broadcasted_iota(jnp.int32, (b, LANES), 1) < SB_HEAD_DIM
    return tri, stack(u_after), stack(u_from), lane_lo


def _sb_scores(qh, kb, a_run, keep, u2_after, mask_l=True):
    z = lax.dot_general(qh, kb, NT, preferred_element_type=F32)
    lb = _log_sigmoid(z)
    l = lb - z
    if keep is not None and mask_l:
        l = jnp.where(keep, l, 0.0)
    w = jnp.exp(lb + (a_run + _split_dot(l, u2_after)))
    if keep is not None:
        w = jnp.where(keep, w, 0.0)
    return lb, l, w


def _sb_walk(qi, carry, step):
    def cond(state):
        n, c = state
        return jnp.logical_and(n <= qi, jnp.max(jnp.maximum(c[0], c[1])) > SB_UNDERFLOW)

    def body(state):
        n, c = state
        return n + 1, step(n, c)

    return lax.while_loop(cond, body, (jnp.int32(2), carry))[1]


def _two_heads(x, lane_lo):
    zero = jnp.zeros_like(x)
    return jnp.where(lane_lo, x, zero), jnp.where(lane_lo, zero, x)


def _sb_fwd(qs, ks, v, name, v_col=0, side=None):
    s, width = qs.shape
    b = min(SB_BLOCK, s)
    nqb = min(SB_Q_BLOCKS, s // b)

    def body(q_ref, k_ref, v_ref, o_ref):
        tri, u2_after, _, lane_lo = _sb_consts(b)
        zero = jnp.zeros((b, 1), F32)
        started = []
        for h in range(nqb):
            qi = pl.program_id(1) * nqb + h
            q_a, q_b = _two_heads(q_ref[h * b:(h + 1) * b, :], lane_lo)

            def step(n, carry, keep, mask_l=True, qi=qi, q_a=q_a, q_b=q_b):
                a_a, a_b, acc = carry
                off = pl.multiple_of(jnp.maximum(qi - n, 0) * b, b)
                kb = k_ref[pl.ds(off, b), :]
                v_a, v_b = _two_heads(v_ref[pl.ds(off, b), :].astype(BF16), lane_lo)
                _, l_a, w_a = _sb_scores(q_a, kb, a_a, keep, u2_after, mask_l)
                _, l_b, w_b = _sb_scores(q_b, kb, a_b, keep, u2_after, mask_l)
                acc = acc + jnp.dot(jnp.concatenate([w_a.astype(BF16), w_b.astype(BF16)], axis=1),
                                    jnp.concatenate([v_a, v_b], axis=0), preferred_element_type=F32)
                return (a_a + jnp.sum(l_a, axis=1, keepdims=True), a_b + jnp.sum(l_b, axis=1, keepdims=True), acc)

            carry = step(0, (zero, zero, jnp.zeros((b, LANES), F32)), tri)
            carry = step(1, carry, jnp.broadcast_to(qi > 0, tri.shape), mask_l=False)
            started.append((qi, step, carry))
        for h, (qi, step, carry) in enumerate(started):
            carry = _sb_walk(qi, carry, lambda n, c, step=step: step(n, c, None))
            o_ref[h * b:(h + 1) * b, :] = carry[2]

    blk = pl.BlockSpec((nqb * b, LANES), lambda hp, i: (i, hp))
    full = pl.BlockSpec((s, LANES), lambda hp, i: (0, hp))
    full_v = pl.BlockSpec((s, LANES), lambda hp, i: (0, hp + v_col))
    grid = (width // LANES, s // (nqb * b))
    s_in, s_out, s_shape, s_scratch, s_ops = _side_args(side)
    res = pl.pallas_call(
        _hosted(body, side, 3, 1, grid), name=name, grid=grid,
        in_specs=[blk, full, full_v] + s_in, out_specs=[blk] + s_out,
        out_shape=[jax.ShapeDtypeStruct((s, width), F32)] + s_shape, scratch_shapes=s_scratch,
        compiler_params=_params(("arbitrary", "arbitrary")),
    )(qs, ks, v, *s_ops)
    return _split_side(res, 1, side)


def _sb_bwd(qs, ks, v, out, dout, name, v_col=0, side=None):
    s, width = qs.shape
    b = min(SB_BLOCK, s)
    nqb = min(SB_Q_BLOCKS, s // b)

    def body(q_ref, k_ref, v_ref, o_ref, do_ref, dq_ref, dk_ref, dv_ref):
        @pl.when(pl.program_id(1) == 0)
        def _():
            dk_ref[...] = jnp.zeros_like(dk_ref)
            dv_ref[...] = jnp.zeros_like(dv_ref)

        tri, u2_after, u2_from, lane_lo = _sb_consts(b)
        zero = jnp.zeros((b, 1), F32)

        def head(qh, doh, kb, vb, a_run, d_rem, keep, mask_l):
            lb, l, w = _sb_scores(qh, kb, a_run, keep, u2_after, mask_l)
            wb = w.astype(BF16)
            g = lax.dot_general(doh, vb, NT, preferred_element_type=F32) * wb.astype(F32)
            g_before = d_rem - _split_dot(g, u2_from)
            dz = g - (g + g_before) * jnp.exp(lb)
            if keep is not None:
                dz = jnp.where(keep, dz, 0.0)
            return (dz.astype(BF16), wb, a_run + jnp.sum(l, axis=1, keepdims=True),
                    d_rem - jnp.sum(g, axis=1, keepdims=True))

        started = []
        for h in range(nqb):
            qi = pl.program_id(1) * nqb + h
            rows = slice(h * b, (h + 1) * b)
            q_a, q_b = _two_heads(q_ref[rows, :], lane_lo)
            dob = do_ref[rows, :].astype(BF16)
            do_a, do_b = _two_heads(dob, lane_lo)
            prod = dob.astype(F32) * o_ref[rows, :]
            d_a = jnp.sum(jnp.where(lane_lo, prod, 0.0), axis=1, keepdims=True)
            d_b = jnp.sum(jnp.where(lane_lo, 0.0, prod), axis=1, keepdims=True)
            q_rows = jnp.concatenate([q_a, q_b], axis=0)
            do_rows = jnp.concatenate([do_a, do_b], axis=0)

            def step(n, carry, keep, mask_l=True, qi=qi, q_a=q_a, q_b=q_b, do_a=do_a, do_b=do_b, q_rows=q_rows,
                     do_rows=do_rows):
                a_a, a_b, r_a, r_b, dq = carry
                off = pl.multiple_of(jnp.maximum(qi - n, 0) * b, b)
                kb = k_ref[pl.ds(off, b), :]
                vb = v_ref[pl.ds(off, b), :].astype(BF16)
                k_a, k_b = _two_heads(kb, lane_lo)
                dz_a, w_a, a_a, r_a = head(q_a, do_a, kb, vb, a_a, r_a, keep, mask_l)
                dz_b, w_b, a_b, r_b = head(q_b, do_b, kb, vb, a_b, r_b, keep, mask_l)
                dq = dq + jnp.dot(jnp.concatenate([dz_a, dz_b], axis=1), jnp.concatenate([k_a, k_b], axis=0),
                                  preferred_element_type=F32)
                dk_ref[pl.ds(off, b), :] += lax.dot_general(jnp.concatenate([dz_a, dz_b], axis=0), q_rows, TN,
                                                            preferred_element_type=F32)
                dv_ref[pl.ds(off, b), :] += lax.dot_general(jnp.concatenate([w_a, w_b], axis=0), do_rows, TN,
                                                            preferred_element_type=F32)
                return a_a, a_b, r_a, r_b, dq

            carry = step(0, (zero, zero, d_a, d_b, jnp.zeros((b, LANES), F32)), tri)
            carry = step(1, carry, jnp.broadcast_to(qi > 0, tri.shape), mask_l=False)
            started.append((qi, step, carry))
        for h, (qi, step, carry) in enumerate(started):
            carry = _sb_walk(qi, carry, lambda n, c, step=step: step(n, c, None))
            dq_ref[h * b:(h + 1) * b, :] = carry[4]

    blk = pl.BlockSpec((nqb * b, LANES), lambda hp, i: (i, hp))
    full = pl.BlockSpec((s, LANES), lambda hp, i: (0, hp))
    full_v = pl.BlockSpec((s, LANES), lambda hp, i: (0, hp + v_col))
    grid = (width // LANES, s // (nqb * b))
    s_in, s_out, s_shape, s_scratch, s_ops = _side_args(side)
    res = pl.pallas_call(
        _hosted(body, side, 5, 3, grid), name=name, grid=grid,
        in_specs=[blk, full, full_v, blk, blk] + s_in, out_specs=[blk, full, full] + s_out,
        out_shape=[jax.ShapeDtypeStruct((s, width), F32)] * 3 + s_shape,
        scratch_shapes=s_scratch,
        compiler_params=_params(("arbitrary", "arbitrary")),
    )(qs, ks, v, out, dout, *s_ops)
    return _split_side(res, 3, side)


def _cmul(xr, xi, yr, yi):
    return xr * yr - xi * yi, xr * yi + xi * yr


def _scan_consts(ar, ai, reverse, lc):
    rowi = lax.broadcasted_iota(jnp.int32, (SUBLANES, lc), 0)
    pows = [(ar, ai)]
    for _ in range(SUBLANES - 1):
        pows.append(_cmul(*pows[-1], ar, ai))
    steps = []
    for d in (1, 2, 4):
        keep = (rowi < SUBLANES - d) if reverse else (rowi >= d)
        pr, pi = pows[d - 1]
        steps.append((SUBLANES - d if reverse else d, jnp.where(keep, pr, 0.0), jnp.where(keep, pi, 0.0)))
    cr = jnp.zeros((SUBLANES, lc), F32)
    ci = jnp.zeros((SUBLANES, lc), F32)
    for r in range(SUBLANES):
        pr, pi = pows[SUBLANES - 1 - r] if reverse else pows[r]
        cr = jnp.where(rowi == r, pr, cr)
        ci = jnp.where(rowi == r, pi, ci)
    return steps, cr, ci


def _scan_tile(xr, xi, steps, pr, pi, cr, ci):
    for shift, ar, ai in steps:
        rr = pltpu.roll(xr, shift, 0)
        ri = pltpu.roll(xi, shift, 0)
        xr, xi = xr + ar * rr - ai * ri, xi + ar * ri + ai * rr
    return xr + pr * cr - pi * ci, xi + pr * ci + pi * cr


SCAN_ROWS = 1024


def _scan_chunk(s):
    tt = min(SCAN_ROWS, s)
    seg = tt // SUBLANES
    assert s % tt == 0 and seg % SUBLANES == 0 and seg & (seg - 1) == 0, s
    return tt, seg


def _to_segments(a):
    s, wd = a.shape
    tt, seg = _scan_chunk(s)
    return jnp.transpose(a.reshape(s // tt, SUBLANES, seg, wd), (0, 2, 1, 3)).reshape(s, wd)


def _from_segments(a):
    s, wd = a.shape
    tt, seg = _scan_chunk(s)
    return jnp.transpose(a.reshape(s // tt, seg, SUBLANES, wd), (0, 2, 1, 3)).reshape(s, wd)


def _cpow2(xr, xi, k):
    for _ in range(k):
        xr, xi = _cmul(xr, xi, xr, xi)
    return xr, xi


def _fill_powers(pw_ref, ar, ai, seg, lc):
    _, p8r, p8i = _scan_consts(ar, ai, False, lc)
    a8r, a8i = _cpow2(ar, ai, 3)
    qr, qi = jnp.ones_like(ar), jnp.zeros_like(ai)
    for k in range(seg // SUBLANES):
        tr, ti = _cmul(p8r, p8i, qr, qi)
        for r in range(SUBLANES):
            rows = pl.ds((SUBLANES * k + r) * SUBLANES, SUBLANES)
            pw_ref[rows, :lc] = jnp.broadcast_to(tr[r:r + 1, :], (SUBLANES, lc))
            pw_ref[rows, lc:] = jnp.broadcast_to(ti[r:r + 1, :], (SUBLANES, lc))
        qr, qi = _cmul(qr, qi, a8r, a8i)


def _ssm_fwd(u, acat, bsup, csup, d_skip, name, side=None):
    s = u.shape[0]
    lc = SCAN_LANES
    tt, seg = _scan_chunk(s)
    nl, nt = N_STATE // lc, s // tt
    tile = lambda j: pl.ds(pl.multiple_of(j * SUBLANES, SUBLANES), SUBLANES)

    def body(u_ref, a_ref, b_ref, c_ref, d_ref, s_ref, y0_ref, y1_ref, carry, pw_ref):
        ar, ai = a_ref[:, :lc], a_ref[:, lc:]

        @pl.when(pl.program_id(1) == 0)
        def _():
            carry[...] = jnp.zeros_like(carry)
            _fill_powers(pw_ref, ar, ai, seg, lc)

        ut = u_ref[...]
        s_ref[...] = _dot(ut, b_ref[0])

        ar8, ai8 = jnp.broadcast_to(ar, (SUBLANES, lc)), jnp.broadcast_to(ai, (SUBLANES, lc))

        def local(j, x):
            xr = ar8 * x[0] - ai8 * x[1] + s_ref[tile(j), :lc]
            xi = ar8 * x[1] + ai8 * x[0] + s_ref[tile(j), lc:]
            s_ref[tile(j), :lc] = xr
            s_ref[tile(j), lc:] = xi
            return xr, xi

        zero = jnp.zeros((SUBLANES, lc), F32)
        er, ei = lax.fori_loop(0, seg, local, (zero, zero))
        steps, pr, pi = _scan_consts(*_cpow2(ar, ai, seg.bit_length() - 1), False, lc)
        cr, ci = carry[:, :lc], carry[:, lc:]
        tr, ti = _scan_tile(er, ei, steps, pr, pi, cr, ci)
        rowi = lax.broadcasted_iota(jnp.int32, (SUBLANES, lc), 0)
        before_r = jnp.where(rowi == 0, cr, pltpu.roll(tr, 1, 0))
        before_i = jnp.where(rowi == 0, ci, pltpu.roll(ti, 1, 0))
        carry[:, :lc] = jnp.broadcast_to(tr[SUBLANES - 1:, :], (SUBLANES, lc))
        carry[:, lc:] = jnp.broadcast_to(ti[SUBLANES - 1:, :], (SUBLANES, lc))

        def fix(j, _):
            pwr, pwi = pw_ref[tile(j), :lc], pw_ref[tile(j), lc:]
            s_ref[tile(j), :lc] += pwr * before_r - pwi * before_i
            s_ref[tile(j), lc:] += pwr * before_i + pwi * before_r
            return 0

        lax.fori_loop(0, seg, fix, 0)
        y0 = _dot(s_ref[...], c_ref[0], NT) + d_ref[...] * ut
        y0_ref[...] = y0
        y1_ref[...] = jax.nn.gelu(y0)

    chan = pl.BlockSpec((tt, LANES), lambda j, c: (c, j))
    sup = pl.BlockSpec((1, LANES, 2 * lc), lambda j, c: (j, 0, 0))
    s_in, s_out, s_shape, s_scratch, s_ops = _side_args(side)
    res = pl.pallas_call(
        _hosted(body, side, 5, 3, (nl, nt)), name=name, grid=(nl, nt),
        in_specs=[chan, pl.BlockSpec((1, 2 * lc), lambda j, c: (0, j)), sup, sup,
                  pl.BlockSpec((1, LANES), lambda j, c: (0, j))] + s_in,
        out_specs=[pl.BlockSpec((tt, 2 * lc), lambda j, c: (c, j)), chan, chan] + s_out,
        out_shape=[jax.ShapeDtypeStruct((s, 2 * N_STATE), F32), jax.ShapeDtypeStruct((s, SSM_WIDTH), F32),
                   jax.ShapeDtypeStruct((s, SSM_WIDTH), F32)] + s_shape,
        scratch_shapes=[pltpu.VMEM((SUBLANES, 2 * lc), F32), pltpu.VMEM((seg * SUBLANES, 2 * lc), F32)] + s_scratch,
        compiler_params=_params(("arbitrary", "arbitrary")),
    )(u, acat, bsup, csup, d_skip, *s_ops)
    return _split_side(res, 3, side)


def _ssm_bwd(dy0, states, u, acat, bsup, csup, d_skip, name, side=None):
    s = u.shape[0]
    lc = SCAN_LANES
    tt, seg = _scan_chunk(s)
    nl, nt = N_STATE // lc, s // tt
    tile = lambda j: pl.ds(pl.multiple_of(j * SUBLANES, SUBLANES), SUBLANES)

    def body(dy_ref, s_ref, sp_ref, u_ref, a_ref, b_ref, c_ref, d_ref,
             du_ref, da_ref, db_ref, dc_ref, dd_ref, lam_ref, carry, pw_ref):
        c = pl.program_id(1)
        ar, ai = a_ref[:, :lc], a_ref[:, lc:]

        @pl.when(c == 0)
        def _():
            carry[...] = jnp.zeros_like(carry)
            for r in (da_ref, db_ref, dc_ref, dd_ref):
                r[...] = jnp.zeros_like(r)
            _fill_powers(pw_ref, ar, ai, seg, lc)

        dy = dy_ref[...]
        ut = u_ref[...]
        lam_ref[...] = _dot(dy, c_ref[0])

        ar8, ai8 = jnp.broadcast_to(ar, (SUBLANES, lc)), jnp.broadcast_to(ai, (SUBLANES, lc))

        def local(i, x):
            j = seg - 1 - i
            xr = ar8 * x[0] + ai8 * x[1] + lam_ref[tile(j), :lc]
            xi = ar8 * x[1] - ai8 * x[0] + lam_ref[tile(j), lc:]
            lam_ref[tile(j), :lc] = xr
            lam_ref[tile(j), lc:] = xi
            return xr, xi

        zero = jnp.zeros((SUBLANES, lc), F32)
        er, ei = lax.fori_loop(0, seg, local, (zero, zero))
        big_r, big_i = _cpow2(ar, ai, seg.bit_length() - 1)
        steps, pr, pi = _scan_consts(big_r, -big_i, True, lc)
        cr, ci = carry[:, :lc], carry[:, lc:]
        tr, ti = _scan_tile(er, ei, steps, pr, pi, cr, ci)
        rowi = lax.broadcasted_iota(jnp.int32, (SUBLANES, lc), 0)
        after_r = jnp.where(rowi == SUBLANES - 1, cr, pltpu.roll(tr, SUBLANES - 1, 0))
        after_i = jnp.where(rowi == SUBLANES - 1, ci, pltpu.roll(ti, SUBLANES - 1, 0))
        carry[:, :lc] = jnp.broadcast_to(tr[:1, :], (SUBLANES, lc))
        carry[:, lc:] = jnp.broadcast_to(ti[:1, :], (SUBLANES, lc))

        start = c != nt - 1
        last_r = jnp.where(start, jnp.broadcast_to(sp_ref[SUBLANES - 1:, :lc], (SUBLANES, lc)), 0.0)
        last_i = jnp.where(start, jnp.broadcast_to(sp_ref[SUBLANES - 1:, lc:], (SUBLANES, lc)), 0.0)
        first_r = jnp.where(rowi == 0, last_r, pltpu.roll(s_ref[tile(seg - 1), :lc], 1, 0))
        first_i = jnp.where(rowi == 0, last_i, pltpu.roll(s_ref[tile(seg - 1), lc:], 1, 0))

        def fix(j, acc):
            dar, dai = acc
            k = seg - 1 - j
            pwr, pwi = pw_ref[tile(k), :lc], pw_ref[tile(k), lc:]
            lr = lam_ref[tile(j), :lc] + pwr * after_r + pwi * after_i
            li = lam_ref[tile(j), lc:] + pwr * after_i - pwi * after_r
            lam_ref[tile(j), :lc] = lr
            lam_ref[tile(j), lc:] = li
            jp = jnp.maximum(j - 1, 0)
            sr = jnp.where(j > 0, s_ref[tile(jp), :lc], first_r)
            si = jnp.where(j > 0, s_ref[tile(jp), lc:], first_i)
            return dar + lr * sr + li * si, dai + li * sr - lr * si

        dar, dai = lax.fori_loop(0, seg, fix, (zero, zero))
        da_ref[:, :lc] += dar
        da_ref[:, lc:] += dai
        lam = lam_ref[...].astype(BF16)
        du_ref[...] = (_dot(lam, b_ref[0], NT) + d_ref[...] * dy).astype(du_ref.dtype)
        db_ref[0] += _dot(ut, lam, TN)
        dc_ref[0] += _dot(dy, s_ref[...], TN)
        dd_ref[...] += jnp.sum(dy * ut, axis=0, keepdims=True)

    rev = lambda j, c: (nt - 1 - c, j)
    chan = pl.BlockSpec((tt, LANES), rev)
    sup = pl.BlockSpec((1, LANES, 2 * lc), lambda j, c: (j, 0, 0))
    row = pl.BlockSpec((1, LANES), lambda j, c: (0, j))
    s_in, s_out, s_shape, s_scratch, s_ops = _side_args(side)
    res = pl.pallas_call(
        _hosted(body, side, 8, 5, (nl, nt)), name=name, grid=(nl, nt),
        in_specs=[chan, pl.BlockSpec((tt, 2 * lc), rev),
                  pl.BlockSpec((SUBLANES, 2 * lc), lambda j, c: (jnp.maximum((nt - 1 - c) * seg - 1, 0), j)),
                  chan, pl.BlockSpec((1, 2 * lc), lambda j, c: (0, j)), sup, sup, row] + s_in,
        out_specs=[chan, pl.BlockSpec((SUBLANES, 2 * lc), lambda j, c: (0, j)), sup, sup, row] + s_out,
        out_shape=[jax.ShapeDtypeStruct((s, SSM_WIDTH), BF16), jax.ShapeDtypeStruct((SUBLANES, 2 * N_STATE), F32),
                   jax.ShapeDtypeStruct(bsup.shape, F32), jax.ShapeDtypeStruct(csup.shape, F32),
                   jax.ShapeDtypeStruct((1, SSM_WIDTH), F32)] + s_shape,
        scratch_shapes=[pltpu.VMEM((tt, 2 * lc), F32), pltpu.VMEM((SUBLANES, 2 * lc), F32),
                        pltpu.VMEM((seg * SUBLANES, 2 * lc), F32)] + s_scratch,
        compiler_params=_params(("arbitrary", "arbitrary")),
    )(dy0, states, states, u, acat, bsup, csup, d_skip, *s_ops)
    return _split_side(res, 5, side)


def _discretise(ar, ai, ldt, br, bi):
    dt = jnp.exp(ldt)
    lr, li = ar * dt, ai * dt
    e = jnp.exp(lr)
    abar_r, abar_i = e * jnp.cos(li), e * jnp.sin(li)
    den = ar * ar + ai * ai
    coef_r = ((abar_r - 1.0) * ar + abar_i * ai) / den
    coef_i = (abar_i * ar - (abar_r - 1.0) * ai) / den
    return abar_r, abar_i, coef_r * br - coef_i * bi, coef_r * bi + coef_i * br


def _group_mask():
    shape = (LANES, SCAN_LANES)
    return (lax.broadcasted_iota(jnp.int32, shape, 0) // SSM_GROUP
            == lax.broadcasted_iota(jnp.int32, shape, 1) // SSM_STATE)


def _ssm_mats_fwd(a_re, a_im, log_dt, b_re, b_im, c_re, c_im, name):
    nl = N_STATE // SCAN_LANES
    lc = SCAN_LANES

    def body(ar, ai, ldt, br, bi, cr, ci, acat, bsup, csup):
        abar_r, abar_i, bbar_r, bbar_i = _discretise(ar[...], ai[...], ldt[...], br[...], bi[...])
        same = _group_mask()
        spread = lambda m, j: jnp.where(same, jnp.tile(m[:, j * lc:(j + 1) * lc], (LANES // SSM_GROUP, 1)), 0.0)
        c_r, c_i = cr[...], -ci[...]
        for j in range(nl):
            acat[:, 2 * j * lc:(2 * j + 1) * lc] = abar_r[:, j * lc:(j + 1) * lc]
            acat[:, (2 * j + 1) * lc:(2 * j + 2) * lc] = abar_i[:, j * lc:(j + 1) * lc]
            bsup[j, :, :lc] = spread(bbar_r, j)
            bsup[j, :, lc:] = spread(bbar_i, j)
            csup[j, :, :lc] = spread(c_r, j)
            csup[j, :, lc:] = spread(c_i, j)

    return pl.pallas_call(
        body, name=name,
        out_shape=[jax.ShapeDtypeStruct((1, 2 * N_STATE), F32), jax.ShapeDtypeStruct((nl, LANES, 2 * lc), F32),
                   jax.ShapeDtypeStruct((nl, LANES, 2 * lc), F32)],
        compiler_params=_params(),
    )(a_re, a_im, log_dt, b_re, b_im, c_re, c_im)


def _ssm_mats_bwd(a_re, a_im, log_dt, b_re, b_im, d_acat, d_bsup, d_csup, name):
    nl = N_STATE // SCAN_LANES
    lc = SCAN_LANES

    def body(ar, ai, ldt, br, bi, dac, dbs, dcs, d_ar, d_ai, d_ldt, d_br, d_bi, d_cr, d_ci):
        same = _group_mask()

        def gather(ref, j, half):
            m = jnp.where(same, ref[j, :, half * lc:(half + 1) * lc], 0.0)
            tot = m[:SSM_GROUP]
            for k in range(1, LANES // SSM_GROUP):
                tot = tot + m[k * SSM_GROUP:(k + 1) * SSM_GROUP]
            return tot

        cols = lambda ref, half: jnp.concatenate([gather(ref, j, half) for j in range(nl)], axis=1)
        d_abar_r = jnp.concatenate([dac[:, 2 * j * lc:(2 * j + 1) * lc] for j in range(nl)], axis=1)
        d_abar_i = jnp.concatenate([dac[:, (2 * j + 1) * lc:(2 * j + 2) * lc] for j in range(nl)], axis=1)
        _, vjp = jax.vjp(_discretise, ar[...], ai[...], ldt[...], br[...], bi[...])
        outs = vjp((d_abar_r, d_abar_i, cols(dbs, 0), cols(dbs, 1)))
        for ref, val in zip((d_ar, d_ai, d_ldt, d_br, d_bi), outs):
            ref[...] = val
        d_cr[...] = cols(dcs, 0)
        d_ci[...] = -cols(dcs, 1)

    row = jax.ShapeDtypeStruct((1, N_STATE), F32)
    mat = jax.ShapeDtypeStruct((SSM_GROUP, N_STATE), F32)
    return pl.pallas_call(
        body, name=name, out_shape=[row, row, row, mat, mat, mat, mat], compiler_params=_params(),
    )(a_re, a_im, log_dt, b_re, b_im, d_acat, d_bsup, d_csup)


def _states_on_lanes(sm):
    flat = lambda a: a.reshape(1, N_STATE)
    chan_b = lambda b: jnp.transpose(b, (2, 0, 1)).reshape(SSM_GROUP, N_STATE)
    chan_c = lambda c: jnp.transpose(c, (1, 0, 2)).reshape(SSM_GROUP, N_STATE)
    return (flat(sm["ssm_a_re"]), flat(sm["ssm_a_im"]), flat(jnp.repeat(sm["ssm_log_dt"], SSM_STATE)),
            chan_b(sm["ssm_b_re"]), chan_b(sm["ssm_b_im"]), chan_c(sm["ssm_c_re"]), chan_c(sm["ssm_c_im"]))


def _from_states_on_lanes(d_ar, d_ai, d_ldt, d_br, d_bi, d_cr, d_ci):
    grp = lambda a: a.reshape(SSM_GROUPS, SSM_STATE)
    back_b = lambda b: jnp.transpose(b.reshape(SSM_GROUP, SSM_GROUPS, SSM_STATE), (1, 2, 0))
    back_c = lambda c: jnp.transpose(c.reshape(SSM_GROUP, SSM_GROUPS, SSM_STATE), (1, 0, 2))
    return (grp(d_ar), grp(d_ai), jnp.sum(grp(d_ldt), axis=1), back_b(d_br), back_b(d_bi), back_c(d_cr), back_c(d_ci))


def _mem_fwd(mem, g_mem, w_kv, g_k, name):
    ml = mem.shape[0]

    def body(mem_ref, gm_ref, w_ref, gk_ref, memn_ref, kv_ref, kn_ref, vv_ref):
        memn = _rms(mem_ref[...], gm_ref[...])
        memn_ref[...] = memn.astype(BF16)
        kv = _dot(memn, w_ref[...])
        kv_ref[...] = kv
        for hh in range(XA_HEADS):
            sl = slice(hh * XA_HEAD_DIM, (hh + 1) * XA_HEAD_DIM)
            kn_ref[:, sl] = _rms(kv[:, sl], gk_ref[...]).astype(BF16)
        vv_ref[...] = kv[:, XA_WIDTH:].astype(BF16)

    return pl.pallas_call(
        body, name=name,
        out_shape=[jax.ShapeDtypeStruct((ml, D_MODEL), BF16), jax.ShapeDtypeStruct((ml, 2 * XA_WIDTH), F32),
                   jax.ShapeDtypeStruct((ml, XA_WIDTH), BF16), jax.ShapeDtypeStruct((ml, XA_WIDTH), BF16)],
        compiler_params=_params(),
    )(mem, g_mem, w_kv, g_k)


def _mem_bwd(mem, g_mem, memn, w_kv, kv, g_k, dkn, dvv, name):
    def body(mem_ref, gm_ref, memn_ref, w_ref, kv_ref, gk_ref, dkn_ref, dvv_ref, dw_ref, dgm_ref, dgk_ref):
        kv = kv_ref[...]
        dgk = jnp.zeros(dgk_ref.shape, F32)
        parts = []
        for hh in range(XA_HEADS):
            sl = slice(hh * XA_HEAD_DIM, (hh + 1) * XA_HEAD_DIM)
            _, vjp = jax.vjp(_rms, kv[:, sl], gk_ref[...])
            dk, dg = vjp(dkn_ref[:, sl])
            parts.append(dk)
            dgk = dgk + dg
        dgk_ref[...] = dgk
        dkv = jnp.concatenate(parts + [dvv_ref[...]], axis=1)
        dw_ref[...] = _dot(memn_ref[...], dkv, TN)
        dmemn = _dot(dkv, w_ref[...], NT)
        _, vjp = jax.vjp(_rms, mem_ref[...], gm_ref[...])
        dgm_ref[...] = vjp(dmemn)[1]

    return pl.pallas_call(
        body, name=name,
        out_shape=[jax.ShapeDtypeStruct((D_MODEL, 2 * XA_WIDTH), F32), jax.ShapeDtypeStruct(g_mem.shape, F32),
                   jax.ShapeDtypeStruct(g_k.shape, F32)],
        compiler_params=_params(),
    )(mem, g_mem, memn, w_kv, kv, g_k, dkn, dvv)


def _xa_head(qx_h, g_q, kn_h, vv_h):
    qn = _rms(qx_h, g_q)
    sc = _dot(qn, kn_h, NT) * (XA_HEAD_DIM ** -0.5)
    sc = sc - jnp.max(sc, axis=-1, keepdims=True)
    e = jnp.exp(sc)
    p = e / jnp.sum(e, axis=-1, keepdims=True)
    return qn, p


def _xa_fwd(qx, g_q, kn, vv, name):
    def fn(qt, gq, knt, vvt):
        outs = []
        for hh in range(XA_HEADS):
            sl = slice(hh * XA_HEAD_DIM, (hh + 1) * XA_HEAD_DIM)
            _, p = _xa_head(qt[:, sl], gq, knt[:, sl], vvt[:, sl])
            outs.append(_dot(p, vvt[:, sl]))
        return (jnp.concatenate(outs, axis=1),), ()

    return _rw(fn, [qx], [g_q, kn, vv], [(XA_WIDTH, BF16)], [], name)[0]


def _xa_bwd(qx, g_q, kn, vv, do, name):
    def fn(qt, dot_, gq, knt, vvt):
        dqs, dks, dvs = [], [], []
        dgq = jnp.zeros_like(gq)
        for hh in range(XA_HEADS):
            sl = slice(hh * XA_HEAD_DIM, (hh + 1) * XA_HEAD_DIM)
            qn, p = _xa_head(qt[:, sl], gq, knt[:, sl], vvt[:, sl])
            doh = dot_[:, sl]
            dp = _dot(doh, vvt[:, sl], NT)
            dvs.append(_dot(p, doh, TN))
            ds = p * (dp - jnp.sum(dp * p, axis=-1, keepdims=True)) * (XA_HEAD_DIM ** -0.5)
            dqn = _dot(ds, knt[:, sl])
            dks.append(_dot(ds, qn, TN))
            _, vjp = jax.vjp(_rms, qt[:, sl], gq)
            dq, dg = vjp(dqn)
            dqs.append(dq)
            dgq = dgq + dg
        return ((jnp.concatenate(dqs, axis=1),),
                (jnp.concatenate(dks, axis=1), jnp.concatenate(dvs, axis=1), dgq))

    return _rw(fn, [qx, do], [g_q, kn, vv], [(XA_WIDTH, BF16)], [kn.shape, vv.shape, g_q.shape], name)


BIG = [
    ("w_in", (D_MODEL, IN_WIDTH), 1), ("ssm_w_glu", (SSM_WIDTH, SSM_WIDTH), 0), ("w_out", (D_MODEL, D_MODEL), 0),
    ("xa_w_q", (D_MODEL, XA_WIDTH), 0), ("xa_w_kv", (D_MODEL, 2 * XA_WIDTH), 0), ("xa_w_o", (XA_WIDTH, D_MODEL), 1),
    ("w_up", (D_MODEL, D_FF), 1), ("w_down", (D_FF, D_MODEL), 0),
]
BIG_INDEX = {n: i for i, (n, _, _) in enumerate(BIG)}


def _shard_shape(shape, axis):
    return tuple(d // N_DEV if i == axis else d for i, d in enumerate(shape))


def _shard_of(ref, axis, d):
    n = ref.shape[axis] // N_DEV
    return ref.at[pl.ds(d * n, n), :] if axis == 0 else ref.at[:, pl.ds(d * n, n)]


def _gather_side(names, shards):
    idxs = [BIG_INDEX[n] for n in names]

    def make(ins, outs, send_sems, recv_sems):
        x, y, c = lax.axis_index("x"), lax.axis_index("y"), lax.axis_index("c")
        cps = []
        for j, i in enumerate(idxs):
            mine = _shard_of(outs[j], BIG[i][2], 4 * x + 2 * y + c)
            cps.append(pltpu.make_async_copy(ins[j], mine, send_sems.at[N_DEV * j]))
            for rel in range(1, N_DEV):
                to = tuple(1 - p if rel >> bit & 1 else p for p, bit in ((x, 2), (y, 1), (c, 0)))
                cps.append(pltpu.make_async_remote_copy(
                    src_ref=ins[j], dst_ref=mine, send_sem=send_sems.at[N_DEV * j + rel],
                    recv_sem=recv_sems.at[N_DEV * j + rel], device_id=to, device_id_type=MESH))
        return cps

    return _Side(shards, [jax.ShapeDtypeStruct(BIG[i][1], BF16) for i in idxs], N_DEV * len(idxs), make)


def _gather_two_level_side(name, shard):
    i = BIG_INDEX[name]

    def parts(ins, outs, send_sems, recv_sems):
        x, y, c = lax.axis_index("x"), lax.axis_index("y"), lax.axis_index("c")
        sibling = (x, y, 1 - c)
        chips = [(1 - x, y), (x, 1 - y), (1 - x, 1 - y)]

        def place(dev):
            return _shard_of(outs[0], BIG[i][2], 4 * dev[0] + 2 * dev[1] + dev[2])

        def copy(k, blk, to, src=None):
            return pltpu.make_async_remote_copy(
                src_ref=place(blk) if src is None else src, dst_ref=place(blk), send_sem=send_sems.at[k],
                recv_sem=recv_sems.at[k], device_id=to, device_id_type=MESH)

        mine = pltpu.make_async_copy(ins[0], place((x, y, c)), send_sems.at[7])
        first = [copy(0, (x, y, c), sibling, src=ins[0])]
        first += [copy(1 + j, (x, y, c), (*chip, c), src=ins[0]) for j, chip in enumerate(chips)]
        passed = [copy(4 + j, (*chip, c), sibling) for j, chip in enumerate(chips)]
        arrived = [copy(1 + j, (*chip, c), (x, y, c)) for j, chip in enumerate(chips)]
        from_sibling = [copy(0, sibling, (x, y, c))] + [copy(4 + j, (*chip, 1 - c), (x, y, c))
                                                       for j, chip in enumerate(chips)]
        return mine, first, passed, arrived, from_sibling

    def make(ins, outs, send_sems, recv_sems):
        mine, first, _, _, _ = parts(ins, outs, send_sems, recv_sems)
        return [mine] + first

    def finish(ins, outs, send_sems, recv_sems):
        mine, first, passed, arrived, from_sibling = parts(ins, outs, send_sems, recv_sems)
        for got, onward in zip(arrived, passed):
            got.wait_recv()
            onward.start()
        for cp in from_sibling:
            cp.wait_recv()
        for cp in first + passed:
            cp.wait_send()
        mine.wait()

    return _Side([shard], [jax.ShapeDtypeStruct(BIG[i][1], BF16)], N_DEV, make, finish)


def _sibling_side(names, grads):
    idxs = [BIG_INDEX[n] for n in names]

    def make(ins, outs, send_sems, recv_sems):
        x, y, c = lax.axis_index("x"), lax.axis_index("y"), lax.axis_index("c")
        return [pltpu.make_async_remote_copy(
            src_ref=_shard_of(ins[j], BIG[i][2], 2 * k + (1 - c)), dst_ref=outs[j].at[k],
            send_sem=send_sems.at[4 * j + k], recv_sem=recv_sems.at[4 * j + k], device_id=(x, y, 1 - c),
            device_id_type=MESH) for j, i in enumerate(idxs) for k in range(4)]

    shapes = [jax.ShapeDtypeStruct((4,) + _shard_shape(BIG[i][1], BIG[i][2]), F32) for i in idxs]
    return _Side(grads, shapes, 4 * len(idxs), make)


def _chips_side(parts):
    def make(ins, outs, send_sems, recv_sems):
        x, y, c = lax.axis_index("x"), lax.axis_index("y"), lax.axis_index("c")
        chips = [(1 - x, y), (x, 1 - y), (1 - x, 1 - y)]
        return [pltpu.make_async_remote_copy(
            src_ref=ins[j].at[2 * cx + cy], dst_ref=outs[j].at[r], send_sem=send_sems.at[3 * j + r],
            recv_sem=recv_sems.at[3 * j + r], device_id=(cx, cy, c), device_id_type=MESH)
            for r, (cx, cy) in enumerate(chips) for j in range(len(parts))]

    return _Side(parts, [jax.ShapeDtypeStruct((3,) + p.shape[1:], p.dtype) for p in parts], 3 * len(parts), make)


def _reduce_add(grad, recv, axis, core, name):
    rs, cs = recv.shape[1:]
    rt = _row_tile(rs, 256)
    nt = rs // rt

    def body(c_ref, g_ref, r_ref, p_ref, pb_ref):
        sm = g_ref[...] + r_ref[0]
        p_ref[0] = sm
        pb_ref[0] = sm.astype(BF16)

    if axis == 0:
        g_spec = pl.BlockSpec((rt, cs), lambda k, t, c_ref: ((2 * k + c_ref[0]) * nt + t, 0))
    else:
        g_spec = pl.BlockSpec((rt, cs), lambda k, t, c_ref: (t, 2 * k + c_ref[0]))
    slab = pl.BlockSpec((1, rt, cs), lambda k, t, c_ref: (k, t, 0))
    return pl.pallas_call(
        body, name=name,
        grid_spec=pltpu.PrefetchScalarGridSpec(num_scalar_prefetch=1, grid=(4, nt), in_specs=[g_spec, slab],
                                               out_specs=[slab, slab]),
        out_shape=[jax.ShapeDtypeStruct(recv.shape, F32), jax.ShapeDtypeStruct(recv.shape, BF16)],
        compiler_params=_params(("parallel", "parallel")),
    )(core, grad, recv)


def _all_gather(block, name, side):
    m_per, n = block.shape
    ns_in, ns_out = len(side.ins), len(side.out_shapes)

    def body(*refs):
        x_ref, s_ins, out_ref = refs[0], refs[1:1 + ns_in], refs[1 + ns_in]
        s_outs = refs[2 + ns_in:2 + ns_in + ns_out]
        send_sems, recv_sems, local_sem, s_send, s_recv = refs[2 + ns_in + ns_out:]
        others = side.make(s_ins, s_outs, s_send, s_recv)
        for cp in others:
            cp.start()
        x, y, c = lax.axis_index("x"), lax.axis_index("y"), lax.axis_index("c")
        me, sibling = (x, y, c), (x, y, 1 - c)
        chips = [(1 - x, y), (x, 1 - y), (1 - x, 1 - y)]

        def rows(px, py, pc):
            return out_ref.at[pl.ds((4 * px + 2 * py + pc) * m_per, m_per), :]

        def copy(k, blk, to, src=None):
            return pltpu.make_async_remote_copy(
                src_ref=rows(*blk) if src is None else src, dst_ref=rows(*blk),
                send_sem=send_sems.at[k], recv_sem=recv_sems.at[k], device_id=to, device_id_type=MESH)

        mine = pltpu.make_async_copy(x_ref, rows(*me), local_sem)
        mine.start()
        first = [copy(0, me, sibling, src=x_ref)]
        first += [copy(1 + j, me, (*chip, c), src=x_ref) for j, chip in enumerate(chips)]
        for cp in first:
            cp.start()
        passed = [copy(4 + j, (*chip, c), sibling) for j, chip in enumerate(chips)]
        for j, chip in enumerate(chips):
            copy(1 + j, (*chip, c), me).wait_recv()
            passed[j].start()
        copy(0, sibling, me).wait_recv()
        for j, chip in enumerate(chips):
            copy(4 + j, (*chip, 1 - c), me).wait_recv()
        for cp in first + passed:
            cp.wait_send()
        mine.wait()
        for cp in others:
            cp.wait()

    res = pl.pallas_call(
        body, name=name, in_specs=[ANY] * (1 + ns_in), out_specs=[ANY] * (1 + ns_out),
        out_shape=[jax.ShapeDtypeStruct((N_DEV * m_per, n), block.dtype)] + side.out_shapes,
        scratch_shapes=[pltpu.SemaphoreType.DMA((7,)), pltpu.SemaphoreType.DMA((7,)), pltpu.SemaphoreType.DMA]
        + side.sems(),
    )(block, *side.ins)
    return res[0], list(res[1:])


def _adam_math(w, g, m, v):
    m = ADAM_B1 * m + (1.0 - ADAM_B1) * g
    v = ADAM_B2 * v + (1.0 - ADAM_B2) * (g * g)
    m_hat = m / (1.0 - ADAM_B1 ** ADAM_STEP)
    v_hat = v / (1.0 - ADAM_B2 ** ADAM_STEP)
    delta = -ADAM_LR * (m_hat / (jnp.sqrt(v_hat) + ADAM_EPS) + ADAM_WD * w)
    return delta, m, v


def _adam_sharded(own, recv, w, m, v, chip, name):
    rs, cs = w.shape
    rt = _row_tile(rs, 256)

    def body(chip_ref, p_ref, r_ref, w_ref, m_ref, v_ref, g_out, d_out, m_out, v_out):
        g = p_ref[0] + r_ref[0].astype(F32) + r_ref[1].astype(F32) + r_ref[2].astype(F32)
        d, mn, vn = _adam_math(w_ref[...], g, m_ref[...], v_ref[...])
        g_out[...] = g
        d_out[...] = d
        m_out[...] = mn
        v_out[...] = vn

    tile = pl.BlockSpec((rt, cs), lambda t, chip_ref: (t, 0))
    return pl.pallas_call(
        body, name=name,
        grid_spec=pltpu.PrefetchScalarGridSpec(
            num_scalar_prefetch=1, grid=(rs // rt,),
            in_specs=[pl.BlockSpec((1, rt, cs), lambda t, chip_ref: (chip_ref[0], t, 0)),
                      pl.BlockSpec((3, rt, cs), lambda t, chip_ref: (0, t, 0)), tile, tile, tile],
            out_specs=[tile] * 4),
        out_shape=[jax.ShapeDtypeStruct((rs, cs), F32)] * 4,
        compiler_params=_params(("parallel",)),
    )(chip, own, recv, w, m, v)


SMALL = ["g_mix", "ssm_a_re", "ssm_a_im", "ssm_log_dt", "ssm_b_re", "ssm_b_im", "ssm_c_re", "ssm_c_im", "ssm_d",
         "sb_g_q", "sb_g_k", "g_out_ssm", "g_out_sb", "g_xa", "g_mem", "xa_g_q", "xa_g_k", "g_mlp"]
PACK_TILE = SUBLANES * LANES


def _natural_2d(n):
    return (n // LANES, LANES) if n % LANES == 0 else (1, n)


def _pack_small(arrs):
    parts = []
    for a in arrs:
        flat = a.reshape(-1)
        parts.append(jnp.pad(flat, (0, (-flat.shape[0]) % PACK_TILE)))
    return jnp.concatenate(parts).reshape(-1, LANES)


def _adam_replicated(gathered, sizes, ws, ms, vs, name):
    n_w = len(ws)
    r_dev = gathered.shape[0] // N_DEV
    offs, off = [], 0
    for n in sizes:
        offs.append(off)
        off += (n + PACK_TILE - 1) // PACK_TILE * SUBLANES
    assert off == r_dev

    def body(*refs):
        g_ref = refs[0]
        w_refs, m_refs, v_refs = refs[1:1 + n_w], refs[1 + n_w:1 + 2 * n_w], refs[1 + 2 * n_w:1 + 3 * n_w]
        outs = refs[1 + 3 * n_w:]

        def total(i, shape):
            r, cdim = shape
            acc = g_ref[pl.ds(offs[i], r), :cdim]
            for d in range(1, N_DEV):
                acc = acc + g_ref[pl.ds(d * r_dev + offs[i], r), :cdim]
            return acc

        for i in range(n_w):
            g = total(i, w_refs[i].shape)
            d, mn, vn = _adam_math(w_refs[i][...], g, m_refs[i][...], v_refs[i][...])
            for o, val in zip(outs[4 * i:4 * i + 4], (g, d, mn, vn)):
                o[...] = val
        outs[4 * n_w][...] = total(n_w, (SUBLANES, LANES))

    shapes = [w.shape for w in ws]
    return pl.pallas_call(
        body, name=name,
        out_shape=[jax.ShapeDtypeStruct(shp, F32) for shp in shapes for _ in range(4)]
        + [jax.ShapeDtypeStruct((SUBLANES, LANES), F32)],
        compiler_params=_params(),
    )(gathered, *ws, *ms, *vs)


def _step(x, mem, target, shards, sm, core):
    g, w, sums, reduced = {}, {}, {}, {}

    def gather(names):
        return _gather_side(names, [shards[n] for n in names])

    def to_sibling(names):
        return _sibling_side(names, [g[n] for n in names])

    def add_sibling(names, received):
        for n, r in zip(names, received):
            sums[n] = _reduce_add(g[n], r, BIG[BIG_INDEX[n]][2], core, "reduce_add_" + n)

    def to_chips(names):
        return _chips_side([sums[n][1] for n in names])

    def keep(names, received):
        for n, r in zip(names, received):
            reduced[n] = (sums[n][0], r)

    row = lambda a: a.reshape(1, -1)
    g_mix, g_xa, g_mlp, g_mem = row(sm["g_mix"]), row(sm["g_xa"]), row(sm["g_mlp"]), row(sm["g_mem"])
    g_os, g_ob = row(sm["g_out_ssm"]), row(sm["g_out_sb"])
    sb_gq, sb_gk = jnp.tile(row(sm["sb_g_q"]), (1, SB_HEADS)), jnp.tile(row(sm["sb_g_k"]), (1, SB_HEADS))
    xa_gq, xa_gk = row(sm["xa_g_q"]), row(sm["xa_g_k"])
    d_skip = row(sm["ssm_d"])

    h1, (w["w_in"],) = _norm_fwd(x, g_mix, "norm_mix", side=_gather_two_level_side("w_in", shards["w_in"]))
    proj = _mm(h1, w["w_in"], "nn", "in_proj")
    u = _to_segments(proj[:, :SSM_WIDTH])
    q_raw, k_raw = (proj, SB_WIDTH, 1), (proj, SB_WIDTH, 2)
    v_col = (SSM_WIDTH + 2 * SB_WIDTH) // LANES
    sb_scale = SB_HEAD_DIM ** -0.5
    qs, ks = _rw(lambda qt, kt, gq, gk: ((_rms_groups(qt, gq, sb_scale), _rms_groups(kt, gk, 1.0)), ()),
                 [q_raw, k_raw], [sb_gq, sb_gk], [(SB_WIDTH, BF16)] * 2, [], "sb_qk_norm")
    early = ["ssm_w_glu", "w_out", "xa_w_q", "xa_w_kv", "xa_w_o", "w_up"]
    y_sb, got = _sb_fwd(qs, ks, proj, "sb_fwd", v_col=v_col, side=gather(early))
    w.update(zip(early, got))

    ssm_args = _states_on_lanes(sm)
    acat, bsup, csup = _ssm_mats_fwd(*ssm_args, "ssm_mats")
    (states, y0, y1), (w["w_down"],) = _ssm_fwd(u, acat, bsup, csup, d_skip, "ssm_fwd", side=gather(["w_down"]))
    z_glu, y_ssm = _mm(y1, w["ssm_w_glu"], "nn", "ssm_glu", epi=lambda r, yt: (r, yt * jax.nn.sigmoid(r)),
                       extras=(y1,), out_dtypes=(F32, F32))
    y_ssm = _from_segments(y_ssm)

    def cat_norm(a, b, ga, gb):
        return jnp.concatenate([_rms(a, ga), _rms(b, gb)], axis=1)

    ycat = _rw(lambda a, b, ga, gb: ((cat_norm(a, b, ga, gb),), ()), [y_ssm, y_sb], [g_os, g_ob],
               [(D_MODEL, BF16)], [], "norm_out")[0]

    def residual_norm_epi(r, xt, gt):
        xn = r + xt
        return xn, _rms(xn, gt)

    x1, h2 = _mm(ycat, w["w_out"], "nn", "out_proj", epi=residual_norm_epi, extras=(x,), fulls=(g_xa,),
                 out_dtypes=(F32, BF16))
    qx = _mm(h2, w["xa_w_q"], "nn", "xa_q")
    memn, kv, kn_x, vv_x = _mem_fwd(mem, g_mem, w["xa_w_kv"], xa_gk, "xa_mem")
    o_xa = _xa_fwd(qx, xa_gq, kn_x, vv_x, "xa_fwd")
    x2, h3 = _mm(o_xa, w["xa_w_o"], "nn", "xa_o", epi=residual_norm_epi, extras=(x1,), fulls=(g_mlp,),
                 out_dtypes=(F32, BF16))

    def up_epi(r):
        rl = jnp.maximum(r, 0.0)
        return (rl * rl,)

    r_up = _mm(h3, w["w_up"], "nn", "mlp_up", epi=up_epi, out_dtypes=(BF16,))

    def loss_epi(r, xt, tt):
        d = r + xt - tt
        return (d * (1.0 / D_MODEL),) * 2, (jnp.sum(d * d, axis=0, keepdims=True),)

    dx3, dx3_b, sq = _mm(r_up, w["w_down"], "nn", "mlp_down", epi=loss_epi, extras=(x2, target),
                         out_dtypes=(F32, BF16), sums=[(1, D_MODEL)])
    loss = jnp.sum(sq) * (0.5 / D_MODEL)

    def norm_bwd_epi(r, xt, drt, gt):
        _, vjp = jax.vjp(_rms, xt, gt)
        dx_, dg_ = vjp(r)
        return (dx_ + drt,) * 2, (dg_,)

    g["w_down"] = _mm(r_up, dx3_b, "tn", "d_w_down", tk=2048)
    da = _mm(dx3_b, w["w_down"], "nt", "d_r", epi=lambda r, rt: (r * 2.0 * jnp.sqrt(rt.astype(F32)),), extras=(r_up,),
             out_dtypes=(BF16,))
    g["w_up"] = _mm(h3, da, "tn", "d_w_up", tk=2048)
    mlp = ["w_down", "w_up"]
    (dx2, dx2_b, g["g_mlp"]), got = _mm(da, w["w_up"], "nt", "d_h3", epi=norm_bwd_epi, extras=(x2, dx3),
                                        fulls=(g_mlp,), out_dtypes=(F32, BF16), sums=[g_mlp.shape],
                                        side=to_sibling(mlp))
    add_sibling(mlp, got)
    g["xa_w_o"] = _mm(o_xa, dx2_b, "tn", "d_xa_w_o", tk=2048)
    do_xa = _mm(dx2_b, w["xa_w_o"], "nt", "d_o_xa")
    dqx, dkn_x, dvv_x, g["xa_g_q"] = _xa_bwd(qx, xa_gq, kn_x, vv_x, do_xa, "xa_bwd")
    g["xa_w_kv"], g["g_mem"], g["xa_g_k"] = _mem_bwd(mem, g_mem, memn, w["xa_w_kv"], kv, xa_gk, dkn_x, dvv_x,
                                                     "xa_mem_bwd")
    g["xa_w_q"] = _mm(h2, dqx, "tn", "d_xa_w_q", tk=2048)
    dx1, dx1_b, g["g_xa"] = _mm(dqx, w["xa_w_q"], "nt", "d_h2", epi=norm_bwd_epi, extras=(x1, dx2), fulls=(g_xa,),
                                out_dtypes=(F32, BF16), sums=[g_xa.shape])
    g["w_out"] = _mm(ycat, dx1_b, "tn", "d_w_out", tk=2048)
    dycat = _mm(dx1_b, w["w_out"], "nt", "d_ycat")

    def cat_bwd(a, b, dy, ga, gb):
        _, vjp = jax.vjp(cat_norm, a, b, ga, gb)
        da_, db_, dga, dgb = vjp(dy)
        return (da_, db_), (dga, dgb)

    dy_ssm, dy_sb, g["g_out_ssm"], g["g_out_sb"] = _rw(
        cat_bwd, [y_ssm, y_sb, dycat], [g_os, g_ob], [(SSM_WIDTH, F32), (SB_WIDTH, F32)], [g_os.shape, g_ob.shape],
        "d_norm_out")

    def glu_bwd(dy, yt, zt):
        sg = jax.nn.sigmoid(zt)
        return (dy * sg, dy * yt * sg * (1.0 - sg)), ()

    dy1_a, dz = _rw(glu_bwd, [_to_segments(dy_ssm), y1, z_glu], [], [(SSM_WIDTH, F32), (SSM_WIDTH, BF16)], [], "d_glu")
    g["ssm_w_glu"] = _mm(y1, dz, "tn", "d_w_glu", tk=2048)

    def gelu_bwd_epi(r, da_, y0t):
        _, vjp = jax.vjp(jax.nn.gelu, y0t)
        return (vjp(r + da_)[0],)

    mid = ["w_out", "xa_w_q", "xa_w_kv", "xa_w_o", "ssm_w_glu"]
    dy0, got = _mm(dz, w["ssm_w_glu"], "nt", "d_y1", epi=gelu_bwd_epi, extras=(dy1_a, y0), side=to_sibling(mid))
    add_sibling(mid, got)
    (du, da8, d_bsup, d_csup, g["ssm_d"]), got = _ssm_bwd(dy0, states, u, acat, bsup, csup, d_skip, "ssm_bwd",
                                                          side=to_chips(mlp))
    keep(mlp, got)
    d_acat = jnp.sum(da8, axis=0, keepdims=True)
    d_mats = _ssm_mats_bwd(*ssm_args[:5], d_acat, d_bsup, d_csup, "ssm_mats_bwd")
    for nm, val in zip(("ssm_a_re", "ssm_a_im", "ssm_log_dt", "ssm_b_re", "ssm_b_im", "ssm_c_re", "ssm_c_im"),
                       _from_states_on_lanes(*d_mats)):
        g[nm] = val

    (dqs, dks, dvs), got = _sb_bwd(qs, ks, proj, y_sb, dy_sb, "sb_bwd", v_col=v_col, side=to_chips(mid))
    keep(mid, got)

    def d_proj_rows(du_t, qt, dqt, kt, dkt, dvt, gq, gk):
        _, vjp_q = jax.vjp(lambda a, b_: _rms_groups(a, b_, sb_scale), qt, gq)
        _, vjp_k = jax.vjp(lambda a, b_: _rms_groups(a, b_, 1.0), kt, gk)
        (dq_, dgq_), (dk_, dgk_) = vjp_q(dqt), vjp_k(dkt)
        rows = jnp.concatenate([du_t, dq_.astype(BF16), dk_.astype(BF16), dvt.astype(BF16)], axis=1)
        return (rows,), (dgq_, dgk_)

    dproj, dgq, dgk = _rw(d_proj_rows, [_from_segments(du), q_raw, dqs, k_raw, dks, dvs], [sb_gq, sb_gk],
                          [(IN_WIDTH, BF16)], [sb_gq.shape, sb_gk.shape], "d_proj")
    g["sb_g_q"] = jnp.sum(dgq.reshape(SB_HEADS, SB_HEAD_DIM), axis=0)
    g["sb_g_k"] = jnp.sum(dgk.reshape(SB_HEADS, SB_HEAD_DIM), axis=0)
    g["w_in"] = _mm(h1, dproj, "tn", "d_w_in", tk=2048)
    dh1, got = _mm(dproj, w["w_in"], "nt", "d_h1", side=to_sibling(["w_in"]))
    add_sibling(["w_in"], got)
    dx, g["g_mix"] = _norm_bwd(x, g_mix, dh1, dx1, "d_norm_mix")

    packed = _pack_small([g[n] for n in SMALL] + [loss.reshape(1)])
    everyone, got = _all_gather(packed, "gather_small", to_chips(["w_in"]))
    keep(["w_in"], got)
    return dx, everyone, reduced


def kernel(x, mem, g_mix, w_in, ssm_a_re, ssm_a_im, ssm_log_dt, ssm_b_re, ssm_b_im, ssm_c_re, ssm_c_im, ssm_d, ssm_w_glu, sb_g_q, sb_g_k, g_out_ssm, g_out_sb, w_out, g_xa, g_mem, xa_w_q, xa_w_kv, xa_g_q, xa_g_k, xa_w_o, g_mlp, w_up, w_down, loss_target, m_g_mix, m_w_in, m_ssm_a_re, m_ssm_a_im, m_ssm_log_dt, m_ssm_b_re, m_ssm_b_im, m_ssm_c_re, m_ssm_c_im, m_ssm_d, m_ssm_w_glu, m_sb_g_q, m_sb_g_k, m_g_out_ssm, m_g_out_sb, m_w_out, m_g_xa, m_g_mem, m_xa_w_q, m_xa_w_kv, m_xa_g_q, m_xa_g_k, m_xa_w_o, m_g_mlp, m_w_up, m_w_down, v_g_mix, v_w_in, v_ssm_a_re, v_ssm_a_im, v_ssm_log_dt, v_ssm_b_re, v_ssm_b_im, v_ssm_c_re, v_ssm_c_im, v_ssm_d, v_ssm_w_glu, v_sb_g_q, v_sb_g_k, v_g_out_ssm, v_g_out_sb, v_w_out, v_g_xa, v_g_mem, v_xa_w_q, v_xa_w_kv, v_xa_g_q, v_xa_g_k, v_xa_w_o, v_g_mlp, v_w_up, v_w_down):
    given = dict(locals())
    order = ["g_mix", "w_in", "ssm_a_re", "ssm_a_im", "ssm_log_dt", "ssm_b_re", "ssm_b_im", "ssm_c_re", "ssm_c_im",
             "ssm_d", "ssm_w_glu", "sb_g_q", "sb_g_k", "g_out_ssm", "g_out_sb", "w_out", "g_xa", "g_mem", "xa_w_q",
             "xa_w_kv", "xa_g_q", "xa_g_k", "xa_w_o", "g_mlp", "w_up", "w_down"]
    assert sorted([n for n, _, _ in BIG] + SMALL) == sorted(order)
    core = lax.axis_index("c").astype(jnp.int32).reshape(1)
    chip = (2 * lax.axis_index("x") + lax.axis_index("y")).astype(jnp.int32).reshape(1)

    shards = {n: given[n][0].astype(BF16) for n, _, _ in BIG}
    sm = {n: given[n][0] for n in SMALL}
    dx, everyone, reduced = _step(x[0], mem[0], loss_target[0], shards, sm, core)

    res = {}
    for n, _, _ in BIG:
        own, recv = reduced[n]
        outs = _adam_sharded(own, recv, given[n][0], given["m_" + n][0], given["v_" + n][0], chip, "adam_" + n)
        for kind, val in zip(("grad", "delta", "new_m", "new_v"), outs):
            res[kind + "_" + n] = val[None]

    sizes = [math.prod(sm[n].shape) for n in SMALL] + [1]
    nat = lambda a: a.reshape(_natural_2d(math.prod(a.shape)))
    outs = _adam_replicated(everyone, sizes, [nat(sm[n]) for n in SMALL], [nat(given["m_" + n][0]) for n in SMALL],
                            [nat(given["v_" + n][0]) for n in SMALL], "adam_replicated")
    for i, n in enumerate(SMALL):
        for kind, val in zip(("grad", "delta", "new_m", "new_v"), outs[4 * i:4 * i + 4]):
            res[kind + "_" + n] = val.reshape(given[n].shape)
    loss_out = outs[-1][0, 0]
    return (loss_out, dx[None], *[res["grad_" + n] for n in order], *[res["delta_" + n] for n in order],
            *[res["new_m_" + n] for n in order], *[res["new_v_" + n] for n in order])
```

```python
import functools
import math

import jax
import jax.numpy as jnp
from jax import lax
from jax.experimental import pallas as pl
from jax.experimental.pallas import tpu as pltpu

F32 = jnp.float32
BF16 = jnp.bfloat16
MESH = pl.DeviceIdType.MESH

N_DEV = 8
D_MODEL = 1024
SSM_WIDTH = 512
SSM_GROUP = 16
SSM_GROUPS = 32
SSM_STATE = 64
N_STATE = SSM_GROUPS * SSM_STATE
SB_HEADS = 8
SB_HEAD_DIM = 64
SB_WIDTH = 512
IN_WIDTH = 2048
XA_HEADS = 4
XA_HEAD_DIM = 128
XA_WIDTH = 512
D_FF = 4096
NORM_EPS = 1e-6
ADAM_LR = 0.001
ADAM_B1 = 0.9
ADAM_B2 = 0.999
ADAM_EPS = 1e-08
ADAM_WD = 0.01
ADAM_STEP = 10

LANES = 128
SUBLANES = 8
VMEM_LIMIT = 56 * 1024 * 1024
SCAN_LANES = 512
SB_BLOCK = 256
SB_Q_BLOCKS = 4
SB_UNDERFLOW = -110.0

NN = (((1,), (0,)), ((), ()))
NT = (((1,), (1,)), ((), ()))
TN = (((0,), (0,)), ((), ()))


def _params(sem=None):
    return pltpu.CompilerParams(dimension_semantics=sem, vmem_limit_bytes=VMEM_LIMIT)


def _dot(a, b, dims=NN):
    return lax.dot_general(a.astype(BF16), b.astype(BF16), dims, preferred_element_type=F32)


def _rms(x, g):
    return x * lax.rsqrt(jnp.mean(x * x, axis=-1, keepdims=True) + NORM_EPS) * g


ANY = pl.BlockSpec(memory_space=pl.ANY)


class _Side:
    def __init__(self, ins, out_shapes, n_sem, make, finish=None):
        self.ins, self.out_shapes, self.n_sem, self.make = list(ins), list(out_shapes), n_sem, make
        self.finish = finish

    def sems(self):
        return [pltpu.SemaphoreType.DMA((self.n_sem,)), pltpu.SemaphoreType.DMA((self.n_sem,))]


def _hosted(body, side, n_in, n_out, grid):
    if side is None:
        return body
    ns_in, ns_out = len(side.ins), len(side.out_shapes)

    def wrapped(*refs):
        ins, refs = refs[:n_in], refs[n_in:]
        s_ins, refs = refs[:ns_in], refs[ns_in:]
        outs, refs = refs[:n_out], refs[n_out:]
        s_outs, refs = refs[:ns_out], refs[ns_out:]
        scratch, sems = refs[:-2], refs[-2:]
        ids = [pl.program_id(d) for d in range(len(grid))]
        first = functools.reduce(jnp.logical_and, [i == 0 for i in ids])
        last = functools.reduce(jnp.logical_and, [i == n - 1 for i, n in zip(ids, grid)])

        @pl.when(first)
        def _():
            for cp in side.make(s_ins, s_outs, *sems):
                cp.start()

        body(*ins, *outs, *scratch)

        @pl.when(last)
        def _():
            if side.finish is not None:
                side.finish(s_ins, s_outs, *sems)
            else:
                for cp in side.make(s_ins, s_outs, *sems):
                    cp.wait()

    return wrapped


def _side_args(side):
    if side is None:
        return [], [], [], [], []
    return ([ANY] * len(side.ins), [ANY] * len(side.out_shapes), side.out_shapes, side.sems(), side.ins)


def _split_side(res, n_out, side):
    res = list(res)
    main = res[0] if n_out == 1 else res[:n_out]
    return main if side is None else (main, res[n_out:])


def _mm(a, b, mode, name, *, epi=None, extras=(), fulls=(), out_dtypes=(F32,), sums=(), tm=1024, tn=1024, tk=1024,
        side=None):
    if mode == "nn":
        (m, k), (k2, n) = a.shape, b.shape
    elif mode == "nt":
        (m, k), (n, k2) = a.shape, b.shape
    else:
        (k, m), (k2, n) = a.shape, b.shape
    assert k == k2, (name, a.shape, b.shape)
    tm, tn, tk = min(tm, m), min(tn, n), min(tk, k)
    assert m % tm == 0 and n % tn == 0 and k % tk == 0, (name, m, n, k)
    nk = k // tk
    dims = {"nn": NN, "nt": NT, "tn": TN}[mode]
    if mode == "tn":
        a_spec = pl.BlockSpec((tk, tm), lambda i, j, kk: (kk, i))
    else:
        a_spec = pl.BlockSpec((tm, tk), lambda i, j, kk: (i, kk))
    if mode == "nt":
        b_spec = pl.BlockSpec((tn, tk), lambda i, j, kk: (j, kk))
    else:
        b_spec = pl.BlockSpec((tk, tn), lambda i, j, kk: (kk, j))
    mn_spec = pl.BlockSpec((tm, tn), lambda i, j, kk: (i, j))
    n_ex, n_full, n_out, n_sum = len(extras), len(fulls), len(out_dtypes), len(sums)
    n_in = 2 + n_ex + n_full

    def body(*refs):
        a_ref, b_ref = refs[:2]
        ex = refs[2:n_in]
        outs = refs[n_in:n_in + n_out]
        sum_refs = refs[n_in + n_out:n_in + n_out + n_sum]
        kk = pl.program_id(2)
        first_tile = jnp.logical_and(pl.program_id(0) == 0, pl.program_id(1) == 0)

        def finish(r):
            vals = epi(r, *[e[...] for e in ex]) if epi is not None else (r,)
            if n_sum:
                vals, parts = vals

                @pl.when(first_tile)
                def _():
                    for sr in sum_refs:
                        sr[...] = jnp.zeros_like(sr)

                for sr, p in zip(sum_refs, parts):
                    sr[...] += p
            for o, v in zip(outs, vals):
                o[...] = v.astype(o.dtype)

        if nk == 1:
            finish(_dot(a_ref[...], b_ref[...], dims))
        else:
            acc = refs[n_in + n_out + n_sum]

            @pl.when(kk == 0)
            def _():
                acc[...] = jnp.zeros_like(acc)

            acc[...] += _dot(a_ref[...], b_ref[...], dims)

            @pl.when(kk == nk - 1)
            def _():
                finish(acc[...])

    grid = (m // tm, n // tn, nk)
    whole = lambda shape: pl.BlockSpec(shape, lambda i, j, kk: (0,) * len(shape))
    s_in, s_out, s_shape, s_scratch, s_ops = _side_args(side)
    seq = bool(side) or n_sum > 0
    res = pl.pallas_call(
        _hosted(body, side, n_in, n_out + n_sum, grid), name=name, grid=grid,
        in_specs=[a_spec, b_spec] + [mn_spec] * n_ex + [whole(f.shape) for f in fulls] + s_in,
        out_specs=[mn_spec] * n_out + [whole(shape) for shape in sums] + s_out,
        out_shape=[jax.ShapeDtypeStruct((m, n), dt) for dt in out_dtypes]
        + [jax.ShapeDtypeStruct(shape, F32) for shape in sums] + s_shape,
        scratch_shapes=([pltpu.VMEM((tm, tn), F32)] if nk > 1 else []) + s_scratch,
        compiler_params=_params(("arbitrary",) * 3 if seq else ("parallel", "parallel", "arbitrary")),
    )(a, b, *extras, *fulls, *s_ops)
    return _split_side(res, n_out + n_sum, side)


def _row_tile(s, target):
    if s <= target:
        return s
    return max(t for t in range(16, target + 1, 16) if s % t == 0)


def _rw(fn, rows, fulls, row_out, acc_out, name, tm=512, side=None):
    cols = [r[1:] if isinstance(r, tuple) else (r.shape[1], 0) for r in rows]
    rows = [r[0] if isinstance(r, tuple) else r for r in rows]
    s = rows[0].shape[0]
    tm = _row_tile(s, tm)
    nr, nf, nro, nao = len(rows), len(fulls), len(row_out), len(acc_out)

    def body(*refs):
        r = refs[:nr]
        f = refs[nr:nr + nf]
        ro = refs[nr + nf:nr + nf + nro]
        ao = refs[nr + nf + nro:]
        outs, accs = fn(*[x[...] for x in r], *[x[...] for x in f])
        for o, v in zip(ro, outs):
            o[...] = v.astype(o.dtype)
        if nao:
            @pl.when(pl.program_id(0) == 0)
            def _():
                for a in ao:
                    a[...] = jnp.zeros_like(a)

            for a, v in zip(ao, accs):
                a[...] += v

    full_spec = lambda shape: pl.BlockSpec(shape, lambda i: (0,) * len(shape))
    s_in, s_out, s_shape, s_scratch, s_ops = _side_args(side)
    res = pl.pallas_call(
        _hosted(body, side, nr + nf, nro + nao, (s // tm,)), name=name, grid=(s // tm,),
        in_specs=[pl.BlockSpec((tm, wd), functools.partial(lambda i, cb: (i, cb), cb=cb)) for wd, cb in cols]
        + [full_spec(x.shape) for x in fulls] + s_in,
        out_specs=[pl.BlockSpec((tm, d), lambda i: (i, 0)) for d, _ in row_out]
        + [full_spec(shape) for shape in acc_out] + s_out,
        out_shape=[jax.ShapeDtypeStruct((s, d), dt) for d, dt in row_out]
        + [jax.ShapeDtypeStruct(shape, F32) for shape in acc_out] + s_shape,
        scratch_shapes=s_scratch,
        compiler_params=_params(("arbitrary",)),
    )(*rows, *fulls, *s_ops)
    res = list(res)
    return res if side is None else (res[:nro + nao], res[nro + nao:])


def _norm_fwd(x, g, name, side=None):
    res = _rw(lambda xt, gt: ((_rms(xt, gt),), ()), [x], [g], [(x.shape[1], BF16)], [], name, side=side)
    return res[0] if side is None else (res[0][0], res[1])


def _norm_bwd(x, g, dh, dres, name, side=None):
    def fn(xt, dht, drt, gt):
        _, vjp = jax.vjp(_rms, xt, gt)
        dx, dg = vjp(dht)
        return (dx + drt,), (dg,)

    return _rw(fn, [x, dh, dres], [g], [(x.shape[1], F32)], [g.shape], name, side=side)


def _rms_groups(x, g, scale):
    lo = lax.broadcasted_iota(jnp.int32, (1, LANES), 1) < SB_HEAD_DIM
    x2 = x * x
    outs = []
    for cb in range(x.shape[1] // LANES):
        sl = slice(cb * LANES, (cb + 1) * LANES)
        s_lo = jnp.sum(jnp.where(lo, x2[:, sl], 0.0), axis=-1, keepdims=True)
        s_hi = jnp.sum(jnp.where(lo, 0.0, x2[:, sl]), axis=-1, keepdims=True)
        r = jnp.where(lo, lax.rsqrt(s_lo * (1.0 / SB_HEAD_DIM) + NORM_EPS),
                      lax.rsqrt(s_hi * (1.0 / SB_HEAD_DIM) + NORM_EPS))
        outs.append(x[:, sl] * r)
    return jnp.concatenate(outs, axis=1) * g * scale


def _log_sigmoid(z):
    return jnp.minimum(z, 0.0) - jnp.log(1.0 + jnp.exp(-jnp.abs(z)))


def _split_dot(x, u2):
    hi = x.astype(BF16)
    lo = (x - hi.astype(F32)).astype(BF16)
    return jnp.dot(jnp.concatenate([hi, lo], axis=1), u2, preferred_element_type=F32)


def _sb_consts(b):
    row = lax.broadcasted_iota(jnp.int32, (b, b), 0)
    col = lax.broadcasted_iota(jnp.int32, (b, b), 1)
    tri = col < row
    u_after = (row > col).astype(BF16)
    u_from = (row >= col).astype(BF16)
    stack = lambda u: jnp.concatenate([u, u], axis=0)
    lane_lo = lax.broadcasted_iota(jnp.int32, (b, LANES), 1) < SB_HEAD_DIM
    return tri, stack(u_after), stack(u_from), lane_lo


def _sb_scores(qh, kb, a_run, keep, u2_after, mask_l=True):
    z = lax.dot_general(qh, kb, NT, preferred_element_type=F32)
    lb = _log_sigmoid(z)
    l = lb - z
    if keep is not None and mask_l:
        l = jnp.where(keep, l, 0.0)
    w = jnp.exp(lb + (a_run + _split_dot(l, u2_after)))
    if keep is not None:
        w = jnp.where(keep, w, 0.0)
    return lb, l, w


def _sb_walk(qi, carry, step):
    def cond(state):
        n, c = state
        return jnp.logical_and(n <= qi, jnp.max(jnp.maximum(c[0], c[1])) > SB_UNDERFLOW)

    def body(state):
        n, c = state
        return n + 1, step(n, c)

    return lax.while_loop(cond, body, (jnp.int32(2), carry))[1]


def _two_heads(x, lane_lo):
    zero = jnp.zeros_like(x)
    return jnp.where(lane_lo, x, zero), jnp.where(lane_lo, zero, x)


def _sb_fwd(qs, ks, v, name, v_col=0, side=None):
    s, width = qs.shape
    b = min(SB_BLOCK, s)
    nqb = min(SB_Q_BLOCKS, s // b)

    def body(q_ref, k_ref, v_ref, o_ref):
        tri, u2_after, _, lane_lo = _sb_consts(b)
        zero = jnp.zeros((b, 1), F32)
        started = []
        for h in range(nqb):
            qi = pl.program_id(1) * nqb + h
            q_a, q_b = _two_heads(q_ref[h * b:(h + 1) * b, :], lane_lo)

            def step(n, carry, keep, mask_l=True, qi=qi, q_a=q_a, q_b=q_b):
                a_a, a_b, acc = carry
                off = pl.multiple_of(jnp.maximum(qi - n, 0) * b, b)
                kb = k_ref[pl.ds(off, b), :]
                v_a, v_b = _two_heads(v_ref[pl.ds(off, b), :].astype(BF16), lane_lo)
                _, l_a, w_a = _sb_scores(q_a, kb, a_a, keep, u2_after, mask_l)
                _, l_b, w_b = _sb_scores(q_b, kb, a_b, keep, u2_after, mask_l)
                acc = acc + jnp.dot(jnp.concatenate([w_a.astype(BF16), w_b.astype(BF16)], axis=1),
                                    jnp.concatenate([v_a, v_b], axis=0), preferred_element_type=F32)
                return (a_a + jnp.sum(l_a, axis=1, keepdims=True), a_b + jnp.sum(l_b, axis=1, keepdims=True), acc)

            carry = step(0, (zero, zero, jnp.zeros((b, LANES), F32)), tri)
            carry = step(1, carry, jnp.broadcast_to(qi > 0, tri.shape), mask_l=False)
            started.append((qi, step, carry))
        for h, (qi, step, carry) in enumerate(started):
            carry = _sb_walk(qi, carry, lambda n, c, step=step: step(n, c, None))
            o_ref[h * b:(h + 1) * b, :] = carry[2]

    blk = pl.BlockSpec((nqb * b, LANES), lambda hp, i: (i, hp))
    full = pl.BlockSpec((s, LANES), lambda hp, i: (0, hp))
    full_v = pl.BlockSpec((s, LANES), lambda hp, i: (0, hp + v_col))
    grid = (width // LANES, s // (nqb * b))
    s_in, s_out, s_shape, s_scratch, s_ops = _side_args(side)
    res = pl.pallas_call(
        _hosted(body, side, 3, 1, grid), name=name, grid=grid,
        in_specs=[blk, full, full_v] + s_in, out_specs=[blk] + s_out,
        out_shape=[jax.ShapeDtypeStruct((s, width), F32)] + s_shape, scratch_shapes=s_scratch,
        compiler_params=_params(("arbitrary", "arbitrary")),
    )(qs, ks, v, *s_ops)
    return _split_side(res, 1, side)


def _sb_bwd(qs, ks, v, out, dout, name, v_col=0, side=None):
    s, width = qs.shape
    b = min(SB_BLOCK, s)
    nqb = min(SB_Q_BLOCKS, s // b)

    def body(q_ref, k_ref, v_ref, o_ref, do_ref, dq_ref, dk_ref, dv_ref):
        @pl.when(pl.program_id(1) == 0)
        def _():
            dk_ref[...] = jnp.zeros_like(dk_ref)
            dv_ref[...] = jnp.zeros_like(dv_ref)

        tri, u2_after, u2_from, lane_lo = _sb_consts(b)
        zero = jnp.zeros((b, 1), F32)

        def head(qh, doh, kb, vb, a_run, d_rem, keep, mask_l):
            lb, l, w = _sb_scores(qh, kb, a_run, keep, u2_after, mask_l)
            wb = w.astype(BF16)
            g = lax.dot_general(doh, vb, NT, preferred_element_type=F32) * wb.astype(F32)
            g_before = d_rem - _split_dot(g, u2_from)
            dz = g - (g + g_before) * jnp.exp(lb)
            if keep is not None:
                dz = jnp.where(keep, dz, 0.0)
            return (dz.astype(BF16), wb, a_run + jnp.sum(l, axis=1, keepdims=True),
                    d_rem - jnp.sum(g, axis=1, keepdims=True))

        started = []
        for h in range(nqb):
            qi = pl.program_id(1) * nqb + h
            rows = slice(h * b, (h + 1) * b)
            q_a, q_b = _two_heads(q_ref[rows, :], lane_lo)
            dob = do_ref[rows, :].astype(BF16)
            do_a, do_b = _two_heads(dob, lane_lo)
            prod = dob.astype(F32) * o_ref[rows, :]
            d_a = jnp.sum(jnp.where(lane_lo, prod, 0.0), axis=1, keepdims=True)
            d_b = jnp.sum(jnp.where(lane_lo, 0.0, prod), axis=1, keepdims=True)
            q_rows = jnp.concatenate([q_a, q_b], axis=0)
            do_rows = jnp.concatenate([do_a, do_b], axis=0)

            def step(n, carry, keep, mask_l=True, qi=qi, q_a=q_a, q_b=q_b, do_a=do_a, do_b=do_b, q_rows=q_rows,
                     do_rows=do_rows):
                a_a, a_b, r_a, r_b, dq = carry
                off = pl.multiple_of(jnp.maximum(qi - n, 0) * b, b)
                kb = k_ref[pl.ds(off, b), :]
                vb = v_ref[pl.ds(off, b), :].astype(BF16)
                k_a, k_b = _two_heads(kb, lane_lo)
                dz_a, w_a, a_a, r_a = head(q_a, do_a, kb, vb, a_a, r_a, keep, mask_l)
                dz_b, w_b, a_b, r_b = head(q_b, do_b, kb, vb, a_b, r_b, keep, mask_l)
                dq = dq + jnp.dot(jnp.concatenate([dz_a, dz_b], axis=1), jnp.concatenate([k_a, k_b], axis=0),
                                  preferred_element_type=F32)
                dk_ref[pl.ds(off, b), :] += lax.dot_general(jnp.concatenate([dz_a, dz_b], axis=0), q_rows, TN,
                                                            preferred_element_type=F32)
                dv_ref[pl.ds(off, b), :] += lax.dot_general(jnp.concatenate([w_a, w_b], axis=0), do_rows, TN,
                                                            preferred_element_type=F32)
                return a_a, a_b, r_a, r_b, dq

            carry = step(0, (zero, zero, d_a, d_b, jnp.zeros((b, LANES), F32)), tri)
            carry = step(1, carry, jnp.broadcast_to(qi > 0, tri.shape), mask_l=False)
            started.append((qi, step, carry))
        for h, (qi, step, carry) in enumerate(started):
            carry = _sb_walk(qi, carry, lambda n, c, step=step: step(n, c, None))
            dq_ref[h * b:(h + 1) * b, :] = carry[4]

    blk = pl.BlockSpec((nqb * b, LANES), lambda hp, i: (i, hp))
    full = pl.BlockSpec((s, LANES), lambda hp, i: (0, hp))
    full_v = pl.BlockSpec((s, LANES), lambda hp, i: (0, hp + v_col))
    grid = (width // LANES, s // (nqb * b))
    s_in, s_out, s_shape, s_scratch, s_ops = _side_args(side)
    res = pl.pallas_call(
        _hosted(body, side, 5, 3, grid), name=name, grid=grid,
        in_specs=[blk, full, full_v, blk, blk] + s_in, out_specs=[blk, full, full] + s_out,
        out_shape=[jax.ShapeDtypeStruct((s, width), F32)] * 3 + s_shape,
        scratch_shapes=s_scratch,
        compiler_params=_params(("arbitrary", "arbitrary")),
    )(qs, ks, v, out, dout, *s_ops)
    return _split_side(res, 3, side)


def _cmul(xr, xi, yr, yi):
    return xr * yr - xi * yi, xr * yi + xi * yr


def _scan_consts(ar, ai, reverse, lc):
    rowi = lax.broadcasted_iota(jnp.int32, (SUBLANES, lc), 0)
    pows = [(ar, ai)]
    for _ in range(SUBLANES - 1):
        pows.append(_cmul(*pows[-1], ar, ai))
    steps = []
    for d in (1, 2, 4):
        keep = (rowi < SUBLANES - d) if reverse else (rowi >= d)
        pr, pi = pows[d - 1]
        steps.append((SUBLANES - d if reverse else d, jnp.where(keep, pr, 0.0), jnp.where(keep, pi, 0.0)))
    cr = jnp.zeros((SUBLANES, lc), F32)
    ci = jnp.zeros((SUBLANES, lc), F32)
    for r in range(SUBLANES):
        pr, pi = pows[SUBLANES - 1 - r] if reverse else pows[r]
        cr = jnp.where(rowi == r, pr, cr)
        ci = jnp.where(rowi == r, pi, ci)
    return steps, cr, ci


def _scan_tile(xr, xi, steps, pr, pi, cr, ci):
    for shift, ar, ai in steps:
        rr = pltpu.roll(xr, shift, 0)
        ri = pltpu.roll(xi, shift, 0)
        xr, xi = xr + ar * rr - ai * ri, xi + ar * ri + ai * rr
    return xr + pr * cr - pi * ci, xi + pr * ci + pi * cr


SCAN_ROWS = 1024


def _scan_chunk(s):
    tt = min(SCAN_ROWS, s)
    seg = tt // SUBLANES
    assert s % tt == 0 and seg % SUBLANES == 0 and seg & (seg - 1) == 0, s
    return tt, seg


def _to_segments(a):
    s, wd = a.shape
    tt, seg = _scan_chunk(s)
    return jnp.transpose(a.reshape(s // tt, SUBLANES, seg, wd), (0, 2, 1, 3)).reshape(s, wd)


def _from_segments(a):
    s, wd = a.shape
    tt, seg = _scan_chunk(s)
    return jnp.transpose(a.reshape(s // tt, seg, SUBLANES, wd), (0, 2, 1, 3)).reshape(s, wd)


def _cpow2(xr, xi, k):
    for _ in range(k):
        xr, xi = _cmul(xr, xi, xr, xi)
    return xr, xi


def _fill_powers(pw_ref, ar, ai, seg, lc):
    _, p8r, p8i = _scan_consts(ar, ai, False, lc)
    a8r, a8i = _cpow2(ar, ai, 3)
    qr, qi = jnp.ones_like(ar), jnp.zeros_like(ai)
    for k in range(seg // SUBLANES):
        tr, ti = _cmul(p8r, p8i, qr, qi)
        for r in range(SUBLANES):
            rows = pl.ds((SUBLANES * k + r) * SUBLANES, SUBLANES)
            pw_ref[rows, :lc] = jnp.broadcast_to(tr[r:r + 1, :], (SUBLANES, lc))
            pw_ref[rows, lc:] = jnp.broadcast_to(ti[r:r + 1, :], (SUBLANES, lc))
        qr, qi = _cmul(qr, qi, a8r, a8i)


def _ssm_fwd(u, acat, bsup, csup, d_skip, name, side=None):
    s = u.shape[0]
    lc = SCAN_LANES
    tt, seg = _scan_chunk(s)
    nl, nt = N_STATE // lc, s // tt
    tile = lambda j: pl.ds(pl.multiple_of(j * SUBLANES, SUBLANES), SUBLANES)

    def body(u_ref, a_ref, b_ref, c_ref, d_ref, s_ref, y0_ref, y1_ref, carry, pw_ref):
        ar, ai = a_ref[:, :lc], a_ref[:, lc:]

        @pl.when(pl.program_id(1) == 0)
        def _():
            carry[...] = jnp.zeros_like(carry)
            _fill_powers(pw_ref, ar, ai, seg, lc)

        ut = u_ref[...]
        s_ref[...] = _dot(ut, b_ref[0])

        ar8, ai8 = jnp.broadcast_to(ar, (SUBLANES, lc)), jnp.broadcast_to(ai, (SUBLANES, lc))

        def local(j, x):
            xr = ar8 * x[0] - ai8 * x[1] + s_ref[tile(j), :lc]
            xi = ar8 * x[1] + ai8 * x[0] + s_ref[tile(j), lc:]
            s_ref[tile(j), :lc] = xr
            s_ref[tile(j), lc:] = xi
            return xr, xi

        zero = jnp.zeros((SUBLANES, lc), F32)
        er, ei = lax.fori_loop(0, seg, local, (zero, zero))
        steps, pr, pi = _scan_consts(*_cpow2(ar, ai, seg.bit_length() - 1), False, lc)
        cr, ci = carry[:, :lc], carry[:, lc:]
        tr, ti = _scan_tile(er, ei, steps, pr, pi, cr, ci)
        rowi = lax.broadcasted_iota(jnp.int32, (SUBLANES, lc), 0)
        before_r = jnp.where(rowi == 0, cr, pltpu.roll(tr, 1, 0))
        before_i = jnp.where(rowi == 0, ci, pltpu.roll(ti, 1, 0))
        carry[:, :lc] = jnp.broadcast_to(tr[SUBLANES - 1:, :], (SUBLANES, lc))
        carry[:, lc:] = jnp.broadcast_to(ti[SUBLANES - 1:, :], (SUBLANES, lc))

        def fix(j, _):
            pwr, pwi = pw_ref[tile(j), :lc], pw_ref[tile(j), lc:]
            s_ref[tile(j), :lc] += pwr * before_r - pwi * before_i
            s_ref[tile(j), lc:] += pwr * before_i + pwi * before_r
            return 0

        lax.fori_loop(0, seg, fix, 0)
        y0 = _dot(s_ref[...], c_ref[0], NT) + d_ref[...] * ut
        y0_ref[...] = y0
        y1_ref[...] = jax.nn.gelu(y0)

    chan = pl.BlockSpec((tt, LANES), lambda j, c: (c, j))
    sup = pl.BlockSpec((1, LANES, 2 * lc), lambda j, c: (j, 0, 0))
    s_in, s_out, s_shape, s_scratch, s_ops = _side_args(side)
    res = pl.pallas_call(
        _hosted(body, side, 5, 3, (nl, nt)), name=name, grid=(nl, nt),
        in_specs=[chan, pl.BlockSpec((1, 2 * lc), lambda j, c: (0, j)), sup, sup,
                  pl.BlockSpec((1, LANES), lambda j, c: (0, j))] + s_in,
        out_specs=[pl.BlockSpec((tt, 2 * lc), lambda j, c: (c, j)), chan, chan] + s_out,
        out_shape=[jax.ShapeDtypeStruct((s, 2 * N_STATE), F32), jax.ShapeDtypeStruct((s, SSM_WIDTH), F32),
                   jax.ShapeDtypeStruct((s, SSM_WIDTH), F32)] + s_shape,
        scratch_shapes=[pltpu.VMEM((SUBLANES, 2 * lc), F32), pltpu.VMEM((seg * SUBLANES, 2 * lc), F32)] + s_scratch,
        compiler_params=_params(("arbitrary", "arbitrary")),
    )(u, acat, bsup, csup, d_skip, *s_ops)
    return _split_side(res, 3, side)


def _ssm_bwd(dy0, states, u, acat, bsup, csup, d_skip, name, side=None):
    s = u.shape[0]
    lc = SCAN_LANES
    tt, seg = _scan_chunk(s)
    nl, nt = N_STATE // lc, s // tt
    tile = lambda j: pl.ds(pl.multiple_of(j * SUBLANES, SUBLANES), SUBLANES)

    def body(dy_ref, s_ref, sp_ref, u_ref, a_ref, b_ref, c_ref, d_ref,
             du_ref, da_ref, db_ref, dc_ref, dd_ref, lam_ref, carry, pw_ref):
        c = pl.program_id(1)
        ar, ai = a_ref[:, :lc], a_ref[:, lc:]

        @pl.when(c == 0)
        def _():
            carry[...] = jnp.zeros_like(carry)
            for r in (da_ref, db_ref, dc_ref, dd_ref):
                r[...] = jnp.zeros_like(r)
            _fill_powers(pw_ref, ar, ai, seg, lc)

        dy = dy_ref[...]
        ut = u_ref[...]
        lam_ref[...] = _dot(dy, c_ref[0])

        ar8, ai8 = jnp.broadcast_to(ar, (SUBLANES, lc)), jnp.broadcast_to(ai, (SUBLANES, lc))

        def local(i, x):
            j = seg - 1 - i
            xr = ar8 * x[0] + ai8 * x[1] + lam_ref[tile(j), :lc]
            xi = ar8 * x[1] - ai8 * x[0] + lam_ref[tile(j), lc:]
            lam_ref[tile(j), :lc] = xr
            lam_ref[tile(j), lc:] = xi
            return xr, xi

        zero = jnp.zeros((SUBLANES, lc), F32)
        er, ei = lax.fori_loop(0, seg, local, (zero, zero))
        big_r, big_i = _cpow2(ar, ai, seg.bit_length() - 1)
        steps, pr, pi = _scan_consts(big_r, -big_i, True, lc)
        cr, ci = carry[:, :lc], carry[:, lc:]
        tr, ti = _scan_tile(er, ei, steps, pr, pi, cr, ci)
        rowi = lax.broadcasted_iota(jnp.int32, (SUBLANES, lc), 0)
        after_r = jnp.where(rowi == SUBLANES - 1, cr, pltpu.roll(tr, SUBLANES - 1, 0))
        after_i = jnp.where(rowi == SUBLANES - 1, ci, pltpu.roll(ti, SUBLANES - 1, 0))
        carry[:, :lc] = jnp.broadcast_to(tr[:1, :], (SUBLANES, lc))
        carry[:, lc:] = jnp.broadcast_to(ti[:1, :], (SUBLANES, lc))

        start = c != nt - 1
        last_r = jnp.where(start, jnp.broadcast_to(sp_ref[SUBLANES - 1:, :lc], (SUBLANES, lc)), 0.0)
        last_i = jnp.where(start, jnp.broadcast_to(sp_ref[SUBLANES - 1:, lc:], (SUBLANES, lc)), 0.0)
        first_r = jnp.where(rowi == 0, last_r, pltpu.roll(s_ref[tile(seg - 1), :lc], 1, 0))
        first_i = jnp.where(rowi == 0, last_i, pltpu.roll(s_ref[tile(seg - 1), lc:], 1, 0))

        def fix(j, acc):
            dar, dai = acc
            k = seg - 1 - j
            pwr, pwi = pw_ref[tile(k), :lc], pw_ref[tile(k), lc:]
            lr = lam_ref[tile(j), :lc] + pwr * after_r + pwi * after_i
            li = lam_ref[tile(j), lc:] + pwr * after_i - pwi * after_r
            lam_ref[tile(j), :lc] = lr
            lam_ref[tile(j), lc:] = li
            jp = jnp.maximum(j - 1, 0)
            sr = jnp.where(j > 0, s_ref[tile(jp), :lc], first_r)
            si = jnp.where(j > 0, s_ref[tile(jp), lc:], first_i)
            return dar + lr * sr + li * si, dai + li * sr - lr * si

        dar, dai = lax.fori_loop(0, seg, fix, (zero, zero))
        da_ref[:, :lc] += dar
        da_ref[:, lc:] += dai
        lam = lam_ref[...].astype(BF16)
        du_ref[...] = (_dot(lam, b_ref[0], NT) + d_ref[...] * dy).astype(du_ref.dtype)
        db_ref[0] += _dot(ut, lam, TN)
        dc_ref[0] += _dot(dy, s_ref[...], TN)
        dd_ref[...] += jnp.sum(dy * ut, axis=0, keepdims=True)

    rev = lambda j, c: (nt - 1 - c, j)
    chan = pl.BlockSpec((tt, LANES), rev)
    sup = pl.BlockSpec((1, LANES, 2 * lc), lambda j, c: (j, 0, 0))
    row = pl.BlockSpec((1, LANES), lambda j, c: (0, j))
    s_in, s_out, s_shape, s_scratch, s_ops = _side_args(side)
    res = pl.pallas_call(
        _hosted(body, side, 8, 5, (nl, nt)), name=name, grid=(nl, nt),
        in_specs=[chan, pl.BlockSpec((tt, 2 * lc), rev),
                  pl.BlockSpec((SUBLANES, 2 * lc), lambda j, c: (jnp.maximum((nt - 1 - c) * seg - 1, 0), j)),
                  chan, pl.BlockSpec((1, 2 * lc), lambda j, c: (0, j)), sup, sup, row] + s_in,
        out_specs=[chan, pl.BlockSpec((SUBLANES, 2 * lc), lambda j, c: (0, j)), sup, sup, row] + s_out,
        out_shape=[jax.ShapeDtypeStruct((s, SSM_WIDTH), BF16), jax.ShapeDtypeStruct((SUBLANES, 2 * N_STATE), F32),
                   jax.ShapeDtypeStruct(bsup.shape, F32), jax.ShapeDtypeStruct(csup.shape, F32),
                   jax.ShapeDtypeStruct((1, SSM_WIDTH), F32)] + s_shape,
        scratch_shapes=[pltpu.VMEM((tt, 2 * lc), F32), pltpu.VMEM((SUBLANES, 2 * lc), F32),
                        pltpu.VMEM((seg * SUBLANES, 2 * lc), F32)] + s_scratch,
        compiler_params=_params(("arbitrary", "arbitrary")),
    )(dy0, states, states, u, acat, bsup, csup, d_skip, *s_ops)
    return _split_side(res, 5, side)


def _discretise(ar, ai, ldt, br, bi):
    dt = jnp.exp(ldt)
    lr, li = ar * dt, ai * dt
    e = jnp.exp(lr)
    abar_r, abar_i = e * jnp.cos(li), e * jnp.sin(li)
    den = ar * ar + ai * ai
    coef_r = ((abar_r - 1.0) * ar + abar_i * ai) / den
    coef_i = (abar_i * ar - (abar_r - 1.0) * ai) / den
    return abar_r, abar_i, coef_r * br - coef_i * bi, coef_r * bi + coef_i * br


def _group_mask():
    shape = (LANES, SCAN_LANES)
    return (lax.broadcasted_iota(jnp.int32, shape, 0) // SSM_GROUP
            == lax.broadcasted_iota(jnp.int32, shape, 1) // SSM_STATE)


def _ssm_mats_fwd(a_re, a_im, log_dt, b_re, b_im, c_re, c_im, name):
    nl = N_STATE // SCAN_LANES
    lc = SCAN_LANES

    def body(ar, ai, ldt, br, bi, cr, ci, acat, bsup, csup):
        abar_r, abar_i, bbar_r, bbar_i = _discretise(ar[...], ai[...], ldt[...], br[...], bi[...])
        same = _group_mask()
        spread = lambda m, j: jnp.where(same, jnp.tile(m[:, j * lc:(j + 1) * lc], (LANES // SSM_GROUP, 1)), 0.0)
        c_r, c_i = cr[...], -ci[...]
        for j in range(nl):
            acat[:, 2 * j * lc:(2 * j + 1) * lc] = abar_r[:, j * lc:(j + 1) * lc]
            acat[:, (2 * j + 1) * lc:(2 * j + 2) * lc] = abar_i[:, j * lc:(j + 1) * lc]
            bsup[j, :, :lc] = spread(bbar_r, j)
            bsup[j, :, lc:] = spread(bbar_i, j)
            csup[j, :, :lc] = spread(c_r, j)
            csup[j, :, lc:] = spread(c_i, j)

    return pl.pallas_call(
        body, name=name,
        out_shape=[jax.ShapeDtypeStruct((1, 2 * N_STATE), F32), jax.ShapeDtypeStruct((nl, LANES, 2 * lc), F32),
                   jax.ShapeDtypeStruct((nl, LANES, 2 * lc), F32)],
        compiler_params=_params(),
    )(a_re, a_im, log_dt, b_re, b_im, c_re, c_im)


def _ssm_mats_bwd(a_re, a_im, log_dt, b_re, b_im, d_acat, d_bsup, d_csup, name):
    nl = N_STATE // SCAN_LANES
    lc = SCAN_LANES

    def body(ar, ai, ldt, br, bi, dac, dbs, dcs, d_ar, d_ai, d_ldt, d_br, d_bi, d_cr, d_ci):
        same = _group_mask()

        def gather(ref, j, half):
            m = jnp.where(same, ref[j, :, half * lc:(half + 1) * lc], 0.0)
            tot = m[:SSM_GROUP]
            for k in range(1, LANES // SSM_GROUP):
                tot = tot + m[k * SSM_GROUP:(k + 1) * SSM_GROUP]
            return tot

        cols = lambda ref, half: jnp.concatenate([gather(ref, j, half) for j in range(nl)], axis=1)
        d_abar_r = jnp.concatenate([dac[:, 2 * j * lc:(2 * j + 1) * lc] for j in range(nl)], axis=1)
        d_abar_i = jnp.concatenate([dac[:, (2 * j + 1) * lc:(2 * j + 2) * lc] for j in range(nl)], axis=1)
        _, vjp = jax.vjp(_discretise, ar[...], ai[...], ldt[...], br[...], bi[...])
        outs = vjp((d_abar_r, d_abar_i, cols(dbs, 0), cols(dbs, 1)))
        for ref, val in zip((d_ar, d_ai, d_ldt, d_br, d_bi), outs):
            ref[...] = val
        d_cr[...] = cols(dcs, 0)
        d_ci[...] = -cols(dcs, 1)

    row = jax.ShapeDtypeStruct((1, N_STATE), F32)
    mat = jax.ShapeDtypeStruct((SSM_GROUP, N_STATE), F32)
    return pl.pallas_call(
        body, name=name, out_shape=[row, row, row, mat, mat, mat, mat], compiler_params=_params(),
    )(a_re, a_im, log_dt, b_re, b_im, d_acat, d_bsup, d_csup)


def _states_on_lanes(sm):
    flat = lambda a: a.reshape(1, N_STATE)
    chan_b = lambda b: jnp.transpose(b, (2, 0, 1)).reshape(SSM_GROUP, N_STATE)
    chan_c = lambda c: jnp.transpose(c, (1, 0, 2)).reshape(SSM_GROUP, N_STATE)
    return (flat(sm["ssm_a_re"]), flat(sm["ssm_a_im"]), flat(jnp.repeat(sm["ssm_log_dt"], SSM_STATE)),
            chan_b(sm["ssm_b_re"]), chan_b(sm["ssm_b_im"]), chan_c(sm["ssm_c_re"]), chan_c(sm["ssm_c_im"]))


def _from_states_on_lanes(d_ar, d_ai, d_ldt, d_br, d_bi, d_cr, d_ci):
    grp = lambda a: a.reshape(SSM_GROUPS, SSM_STATE)
    back_b = lambda b: jnp.transpose(b.reshape(SSM_GROUP, SSM_GROUPS, SSM_STATE), (1, 2, 0))
    back_c = lambda c: jnp.transpose(c.reshape(SSM_GROUP, SSM_GROUPS, SSM_STATE), (1, 0, 2))
    return (grp(d_ar), grp(d_ai), jnp.sum(grp(d_ldt), axis=1), back_b(d_br), back_b(d_bi), back_c(d_cr), back_c(d_ci))


def _mem_fwd(mem, g_mem, w_kv, g_k, name):
    ml = mem.shape[0]

    def body(mem_ref, gm_ref, w_ref, gk_ref, memn_ref, kv_ref, kn_ref, vv_ref):
        memn = _rms(mem_ref[...], gm_ref[...])
        memn_ref[...] = memn.astype(BF16)
        kv = _dot(memn, w_ref[...])
        kv_ref[...] = kv
        for hh in range(XA_HEADS):
            sl = slice(hh * XA_HEAD_DIM, (hh + 1) * XA_HEAD_DIM)
            kn_ref[:, sl] = _rms(kv[:, sl], gk_ref[...]).astype(BF16)
        vv_ref[...] = kv[:, XA_WIDTH:].astype(BF16)

    return pl.pallas_call(
        body, name=name,
        out_shape=[jax.ShapeDtypeStruct((ml, D_MODEL), BF16), jax.ShapeDtypeStruct((ml, 2 * XA_WIDTH), F32),
                   jax.ShapeDtypeStruct((ml, XA_WIDTH), BF16), jax.ShapeDtypeStruct((ml, XA_WIDTH), BF16)],
        compiler_params=_params(),
    )(mem, g_mem, w_kv, g_k)


def _mem_bwd(mem, g_mem, memn, w_kv, kv, g_k, dkn, dvv, name):
    def body(mem_ref, gm_ref, memn_ref, w_ref, kv_ref, gk_ref, dkn_ref, dvv_ref, dw_ref, dgm_ref, dgk_ref):
        kv = kv_ref[...]
        dgk = jnp.zeros(dgk_ref.shape, F32)
        parts = []
        for hh in range(XA_HEADS):
            sl = slice(hh * XA_HEAD_DIM, (hh + 1) * XA_HEAD_DIM)
            _, vjp = jax.vjp(_rms, kv[:, sl], gk_ref[...])
            dk, dg = vjp(dkn_ref[:, sl])
            parts.append(dk)
            dgk = dgk + dg
        dgk_ref[...] = dgk
        dkv = jnp.concatenate(parts + [dvv_ref[...]], axis=1)
        dw_ref[...] = _dot(memn_ref[...], dkv, TN)
        dmemn = _dot(dkv, w_ref[...], NT)
        _, vjp = jax.vjp(_rms, mem_ref[...], gm_ref[...])
        dgm_ref[...] = vjp(dmemn)[1]

    return pl.pallas_call(
        body, name=name,
        out_shape=[jax.ShapeDtypeStruct((D_MODEL, 2 * XA_WIDTH), F32), jax.ShapeDtypeStruct(g_mem.shape, F32),
                   jax.ShapeDtypeStruct(g_k.shape, F32)],
        compiler_params=_params(),
    )(mem, g_mem, memn, w_kv, kv, g_k, dkn, dvv)


def _xa_head(qx_h, g_q, kn_h, vv_h):
    qn = _rms(qx_h, g_q)
    sc = _dot(qn, kn_h, NT) * (XA_HEAD_DIM ** -0.5)
    sc = sc - jnp.max(sc, axis=-1, keepdims=True)
    e = jnp.exp(sc)
    p = e / jnp.sum(e, axis=-1, keepdims=True)
    return qn, p


def _xa_fwd(qx, g_q, kn, vv, name):
    def fn(qt, gq, knt, vvt):
        outs = []
        for hh in range(XA_HEADS):
            sl = slice(hh * XA_HEAD_DIM, (hh + 1) * XA_HEAD_DIM)
            _, p = _xa_head(qt[:, sl], gq, knt[:, sl], vvt[:, sl])
            outs.append(_dot(p, vvt[:, sl]))
        return (jnp.concatenate(outs, axis=1),), ()

    return _rw(fn, [qx], [g_q, kn, vv], [(XA_WIDTH, BF16)], [], name)[0]


def _xa_bwd(qx, g_q, kn, vv, do, name):
    def fn(qt, dot_, gq, knt, vvt):
        dqs, dks, dvs = [], [], []
        dgq = jnp.zeros_like(gq)
        for hh in range(XA_HEADS):
            sl = slice(hh * XA_HEAD_DIM, (hh + 1) * XA_HEAD_DIM)
            qn, p = _xa_head(qt[:, sl], gq, knt[:, sl], vvt[:, sl])
            doh = dot_[:, sl]
            dp = _dot(doh, vvt[:, sl], NT)
            dvs.append(_dot(p, doh, TN))
            ds = p * (dp - jnp.sum(dp * p, axis=-1, keepdims=True)) * (XA_HEAD_DIM ** -0.5)
            dqn = _dot(ds, knt[:, sl])
            dks.append(_dot(ds, qn, TN))
            _, vjp = jax.vjp(_rms, qt[:, sl], gq)
            dq, dg = vjp(dqn)
            dqs.append(dq)
            dgq = dgq + dg
        return ((jnp.concatenate(dqs, axis=1),),
                (jnp.concatenate(dks, axis=1), jnp.concatenate(dvs, axis=1), dgq))

    return _rw(fn, [qx, do], [g_q, kn, vv], [(XA_WIDTH, BF16)], [kn.shape, vv.shape, g_q.shape], name)


BIG = [
    ("w_in", (D_MODEL, IN_WIDTH), 1), ("ssm_w_glu", (SSM_WIDTH, SSM_WIDTH), 0), ("w_out", (D_MODEL, D_MODEL), 0),
    ("xa_w_q", (D_MODEL, XA_WIDTH), 0), ("xa_w_kv", (D_MODEL, 2 * XA_WIDTH), 0), ("xa_w_o", (XA_WIDTH, D_MODEL), 1),
    ("w_up", (D_MODEL, D_FF), 1), ("w_down", (D_FF, D_MODEL), 0),
]
BIG_INDEX = {n: i for i, (n, _, _) in enumerate(BIG)}


def _shard_shape(shape, axis):
    return tuple(d // N_DEV if i == axis else d for i, d in enumerate(shape))


def _shard_of(ref, axis, d):
    n = ref.shape[axis] // N_DEV
    return ref.at[pl.ds(d * n, n), :] if axis == 0 else ref.at[:, pl.ds(d * n, n)]


def _gather_side(names, shards):
    idxs = [BIG_INDEX[n] for n in names]

    def make(ins, outs, send_sems, recv_sems):
        x, y, c = lax.axis_index("x"), lax.axis_index("y"), lax.axis_index("c")
        cps = []
        for j, i in enumerate(idxs):
            mine = _shard_of(outs[j], BIG[i][2], 4 * x + 2 * y + c)
            cps.append(pltpu.make_async_copy(ins[j], mine, send_sems.at[N_DEV * j]))
            for rel in range(1, N_DEV):
                to = tuple(1 - p if rel >> bit & 1 else p for p, bit in ((x, 2), (y, 1), (c, 0)))
                cps.append(pltpu.make_async_remote_copy(
                    src_ref=ins[j], dst_ref=mine, send_sem=send_sems.at[N_DEV * j + rel],
                    recv_sem=recv_sems.at[N_DEV * j + rel], device_id=to, device_id_type=MESH))
        return cps

    return _Side(shards, [jax.ShapeDtypeStruct(BIG[i][1], BF16) for i in idxs], N_DEV * len(idxs), make)


def _gather_two_level_side(name, shard):
    i = BIG_INDEX[name]

    def parts(ins, outs, send_sems, recv_sems):
        x, y, c = lax.axis_index("x"), lax.axis_index("y"), lax.axis_index("c")
        sibling = (x, y, 1 - c)
        chips = [(1 - x, y), (x, 1 - y), (1 - x, 1 - y)]

        def place(dev):
            return _shard_of(outs[0], BIG[i][2], 4 * dev[0] + 2 * dev[1] + dev[2])

        def copy(k, blk, to, src=None):
            return pltpu.make_async_remote_copy(
                src_ref=place(blk) if src is None else src, dst_ref=place(blk), send_sem=send_sems.at[k],
                recv_sem=recv_sems.at[k], device_id=to, device_id_type=MESH)

        mine = pltpu.make_async_copy(ins[0], place((x, y, c)), send_sems.at[7])
        first = [copy(0, (x, y, c), sibling, src=ins[0])]
        first += [copy(1 + j, (x, y, c), (*chip, c), src=ins[0]) for j, chip in enumerate(chips)]
        passed = [copy(4 + j, (*chip, c), sibling) for j, chip in enumerate(chips)]
        arrived = [copy(1 + j, (*chip, c), (x, y, c)) for j, chip in enumerate(chips)]
        from_sibling = [copy(0, sibling, (x, y, c))] + [copy(4 + j, (*chip, 1 - c), (x, y, c))
                                                       for j, chip in enumerate(chips)]
        return mine, first, passed, arrived, from_sibling

    def make(ins, outs, send_sems, recv_sems):
        mine, first, _, _, _ = parts(ins, outs, send_sems, recv_sems)
        return [mine] + first

    def finish(ins, outs, send_sems, recv_sems):
        mine, first, passed, arrived, from_sibling = parts(ins, outs, send_sems, recv_sems)
        for got, onward in zip(arrived, passed):
            got.wait_recv()
            onward.start()
        for cp in from_sibling:
            cp.wait_recv()
        for cp in first + passed:
            cp.wait_send()
        mine.wait()

    return _Side([shard], [jax.ShapeDtypeStruct(BIG[i][1], BF16)], N_DEV, make, finish)


def _sibling_side(names, grads):
    idxs = [BIG_INDEX[n] for n in names]

    def make(ins, outs, send_sems, recv_sems):
        x, y, c = lax.axis_index("x"), lax.axis_index("y"), lax.axis_index("c")
        return [pltpu.make_async_remote_copy(
            src_ref=_shard_of(ins[j], BIG[i][2], 2 * k + (1 - c)), dst_ref=outs[j].at[k],
            send_sem=send_sems.at[4 * j + k], recv_sem=recv_sems.at[4 * j + k], device_id=(x, y, 1 - c),
            device_id_type=MESH) for j, i in enumerate(idxs) for k in range(4)]

    shapes = [jax.ShapeDtypeStruct((4,) + _shard_shape(BIG[i][1], BIG[i][2]), F32) for i in idxs]
    return _Side(grads, shapes, 4 * len(idxs), make)


def _chips_side(parts):
    def make(ins, outs, send_sems, recv_sems):
        x, y, c = lax.axis_index("x"), lax.axis_index("y"), lax.axis_index("c")
        chips = [(1 - x, y), (x, 1 - y), (1 - x, 1 - y)]
        return [pltpu.make_async_remote_copy(
            src_ref=ins[j].at[2 * cx + cy], dst_ref=outs[j].at[r], send_sem=send_sems.at[3 * j + r],
            recv_sem=recv_sems.at[3 * j + r], device_id=(cx, cy, c), device_id_type=MESH)
            for r, (cx, cy) in enumerate(chips) for j in range(len(parts))]

    return _Side(parts, [jax.ShapeDtypeStruct((3,) + p.shape[1:], p.dtype) for p in parts], 3 * len(parts), make)


def _reduce_add(grad, recv, axis, core, name):
    rs, cs = recv.shape[1:]
    rt = _row_tile(rs, 256)
    nt = rs // rt

    def body(c_ref, g_ref, r_ref, p_ref, pb_ref):
        sm = g_ref[...] + r_ref[0]
        p_ref[0] = sm
        pb_ref[0] = sm.astype(BF16)

    if axis == 0:
        g_spec = pl.BlockSpec((rt, cs), lambda k, t, c_ref: ((2 * k + c_ref[0]) * nt + t, 0))
    else:
        g_spec = pl.BlockSpec((rt, cs), lambda k, t, c_ref: (t, 2 * k + c_ref[0]))
    slab = pl.BlockSpec((1, rt, cs), lambda k, t, c_ref: (k, t, 0))
    return pl.pallas_call(
        body, name=name,
        grid_spec=pltpu.PrefetchScalarGridSpec(num_scalar_prefetch=1, grid=(4, nt), in_specs=[g_spec, slab],
                                               out_specs=[slab, slab]),
        out_shape=[jax.ShapeDtypeStruct(recv.shape, F32), jax.ShapeDtypeStruct(recv.shape, BF16)],
        compiler_params=_params(("parallel", "parallel")),
    )(core, grad, recv)


def _all_gather(block, name, side):
    m_per, n = block.shape
    ns_in, ns_out = len(side.ins), len(side.out_shapes)

    def body(*refs):
        x_ref, s_ins, out_ref = refs[0], refs[1:1 + ns_in], refs[1 + ns_in]
        s_outs = refs[2 + ns_in:2 + ns_in + ns_out]
        send_sems, recv_sems, local_sem, s_send, s_recv = refs[2 + ns_in + ns_out:]
        others = side.make(s_ins, s_outs, s_send, s_recv)
        for cp in others:
            cp.start()
        x, y, c = lax.axis_index("x"), lax.axis_index("y"), lax.axis_index("c")
        me, sibling = (x, y, c), (x, y, 1 - c)
        chips = [(1 - x, y), (x, 1 - y), (1 - x, 1 - y)]

        def rows(px, py, pc):
            return out_ref.at[pl.ds((4 * px + 2 * py + pc) * m_per, m_per), :]

        def copy(k, blk, to, src=None):
            return pltpu.make_async_remote_copy(
                src_ref=rows(*blk) if src is None else src, dst_ref=rows(*blk),
                send_sem=send_sems.at[k], recv_sem=recv_sems.at[k], device_id=to, device_id_type=MESH)

        mine = pltpu.make_async_copy(x_ref, rows(*me), local_sem)
        mine.start()
        first = [copy(0, me, sibling, src=x_ref)]
        first += [copy(1 + j, me, (*chip, c), src=x_ref) for j, chip in enumerate(chips)]
        for cp in first:
            cp.start()
        passed = [copy(4 + j, (*chip, c), sibling) for j, chip in enumerate(chips)]
        for j, chip in enumerate(chips):
            copy(1 + j, (*chip, c), me).wait_recv()
            passed[j].start()
        copy(0, sibling, me).wait_recv()
        for j, chip in enumerate(chips):
            copy(4 + j, (*chip, 1 - c), me).wait_recv()
        for cp in first + passed:
            cp.wait_send()
        mine.wait()
        for cp in others:
            cp.wait()

    res = pl.pallas_call(
        body, name=name, in_specs=[ANY] * (1 + ns_in), out_specs=[ANY] * (1 + ns_out),
        out_shape=[jax.ShapeDtypeStruct((N_DEV * m_per, n), block.dtype)] + side.out_shapes,
        scratch_shapes=[pltpu.SemaphoreType.DMA((7,)), pltpu.SemaphoreType.DMA((7,)), pltpu.SemaphoreType.DMA]
        + side.sems(),
    )(block, *side.ins)
    return res[0], list(res[1:])


def _adam_math(w, g, m, v):
    m = ADAM_B1 * m + (1.0 - ADAM_B1) * g
    v = ADAM_B2 * v + (1.0 - ADAM_B2) * (g * g)
    m_hat = m / (1.0 - ADAM_B1 ** ADAM_STEP)
    v_hat = v / (1.0 - ADAM_B2 ** ADAM_STEP)
    delta = -ADAM_LR * (m_hat / (jnp.sqrt(v_hat) + ADAM_EPS) + ADAM_WD * w)
    return delta, m, v


def _adam_sharded(own, recv, w, m, v, chip, name):
    rs, cs = w.shape
    rt = _row_tile(rs, 256)

    def body(chip_ref, p_ref, r_ref, w_ref, m_ref, v_ref, g_out, d_out, m_out, v_out):
        g = p_ref[0] + r_ref[0].astype(F32) + r_ref[1].astype(F32) + r_ref[2].astype(F32)
        d, mn, vn = _adam_math(w_ref[...], g, m_ref[...], v_ref[...])
        g_out[...] = g
        d_out[...] = d
        m_out[...] = mn
        v_out[...] = vn

    tile = pl.BlockSpec((rt, cs), lambda t, chip_ref: (t, 0))
    return pl.pallas_call(
        body, name=name,
        grid_spec=pltpu.PrefetchScalarGridSpec(
            num_scalar_prefetch=1, grid=(rs // rt,),
            in_specs=[pl.BlockSpec((1, rt, cs), lambda t, chip_ref: (chip_ref[0], t, 0)),
                      pl.BlockSpec((3, rt, cs), lambda t, chip_ref: (0, t, 0)), tile, tile, tile],
            out_specs=[tile] * 4),
        out_shape=[jax.ShapeDtypeStruct((rs, cs), F32)] * 4,
        compiler_params=_params(("parallel",)),
    )(chip, own, recv, w, m, v)


SMALL = ["g_mix", "ssm_a_re", "ssm_a_im", "ssm_log_dt", "ssm_b_re", "ssm_b_im", "ssm_c_re", "ssm_c_im", "ssm_d",
         "sb_g_q", "sb_g_k", "g_out_ssm", "g_out_sb", "g_xa", "g_mem", "xa_g_q", "xa_g_k", "g_mlp"]
PACK_TILE = SUBLANES * LANES


def _natural_2d(n):
    return (n // LANES, LANES) if n % LANES == 0 else (1, n)


def _pack_small(arrs):
    parts = []
    for a in arrs:
        flat = a.reshape(-1)
        parts.append(jnp.pad(flat, (0, (-flat.shape[0]) % PACK_TILE)))
    return jnp.concatenate(parts).reshape(-1, LANES)


def _adam_replicated(gathered, sizes, ws, ms, vs, name):
    n_w = len(ws)
    r_dev = gathered.shape[0] // N_DEV
    offs, off = [], 0
    for n in sizes:
        offs.append(off)
        off += (n + PACK_TILE - 1) // PACK_TILE * SUBLANES
    assert off == r_dev

    def body(*refs):
        g_ref = refs[0]
        w_refs, m_refs, v_refs = refs[1:1 + n_w], refs[1 + n_w:1 + 2 * n_w], refs[1 + 2 * n_w:1 + 3 * n_w]
        outs = refs[1 + 3 * n_w:]

        def total(i, shape):
            r, cdim = shape
            acc = g_ref[pl.ds(offs[i], r), :cdim]
            for d in range(1, N_DEV):
                acc = acc + g_ref[pl.ds(d * r_dev + offs[i], r), :cdim]
            return acc

        for i in range(n_w):
            g = total(i, w_refs[i].shape)
            d, mn, vn = _adam_math(w_refs[i][...], g, m_refs[i][...], v_refs[i][...])
            for o, val in zip(outs[4 * i:4 * i + 4], (g, d, mn, vn)):
                o[...] = val
        outs[4 * n_w][...] = total(n_w, (SUBLANES, LANES))

    shapes = [w.shape for w in ws]
    return pl.pallas_call(
        body, name=name,
        out_shape=[jax.ShapeDtypeStruct(shp, F32) for shp in shapes for _ in range(4)]
        + [jax.ShapeDtypeStruct((SUBLANES, LANES), F32)],
        compiler_params=_params(),
    )(gathered, *ws, *ms, *vs)


def _step(x, mem, target, shards, sm, core):
    g, w, sums, reduced = {}, {}, {}, {}

    def gather(names):
        return _gather_side(names, [shards[n] for n in names])

    def to_sibling(names):
        return _sibling_side(names, [g[n] for n in names])

    def add_sibling(names, received):
        for n, r in zip(names, received):
            sums[n] = _reduce_add(g[n], r, BIG[BIG_INDEX[n]][2], core, "reduce_add_" + n)

    def to_chips(names):
        return _chips_side([sums[n][1] for n in names])

    def keep(names, received):
        for n, r in zip(names, received):
            reduced[n] = (sums[n][0], r)

    row = lambda a: a.reshape(1, -1)
    g_mix, g_xa, g_mlp, g_mem = row(sm["g_mix"]), row(sm["g_xa"]), row(sm["g_mlp"]), row(sm["g_mem"])
    g_os, g_ob = row(sm["g_out_ssm"]), row(sm["g_out_sb"])
    sb_gq, sb_gk = jnp.tile(row(sm["sb_g_q"]), (1, SB_HEADS)), jnp.tile(row(sm["sb_g_k"]), (1, SB_HEADS))
    xa_gq, xa_gk = row(sm["xa_g_q"]), row(sm["xa_g_k"])
    d_skip = row(sm["ssm_d"])

    h1, (w["w_in"],) = _norm_fwd(x, g_mix, "norm_mix", side=_gather_two_level_side("w_in", shards["w_in"]))
    proj = _mm(h1, w["w_in"], "nn", "in_proj")
    u = _to_segments(proj[:, :SSM_WIDTH])
    q_raw, k_raw = (proj, SB_WIDTH, 1), (proj, SB_WIDTH, 2)
    v_col = (SSM_WIDTH + 2 * SB_WIDTH) // LANES
    sb_scale = SB_HEAD_DIM ** -0.5
    qs, ks = _rw(lambda qt, kt, gq, gk: ((_rms_groups(qt, gq, sb_scale), _rms_groups(kt, gk, 1.0)), ()),
                 [q_raw, k_raw], [sb_gq, sb_gk], [(SB_WIDTH, BF16)] * 2, [], "sb_qk_norm")
    early = ["ssm_w_glu", "w_out", "xa_w_q", "xa_w_kv", "xa_w_o", "w_up"]
    y_sb, got = _sb_fwd(qs, ks, proj, "sb_fwd", v_col=v_col, side=gather(early))
    w.update(zip(early, got))

    ssm_args = _states_on_lanes(sm)
    acat, bsup, csup = _ssm_mats_fwd(*ssm_args, "ssm_mats")
    (states, y0, y1), (w["w_down"],) = _ssm_fwd(u, acat, bsup, csup, d_skip, "ssm_fwd", side=gather(["w_down"]))
    z_glu, y_ssm = _mm(y1, w["ssm_w_glu"], "nn", "ssm_glu", epi=lambda r, yt: (r, yt * jax.nn.sigmoid(r)),
                       extras=(y1,), out_dtypes=(F32, F32))
    y_ssm = _from_segments(y_ssm)

    def cat_norm(a, b, ga, gb):
        return jnp.concatenate([_rms(a, ga), _rms(b, gb)], axis=1)

    ycat = _rw(lambda a, b, ga, gb: ((cat_norm(a, b, ga, gb),), ()), [y_ssm, y_sb], [g_os, g_ob],
               [(D_MODEL, BF16)], [], "norm_out")[0]

    def residual_norm_epi(r, xt, gt):
        xn = r + xt
        return xn, _rms(xn, gt)

    x1, h2 = _mm(ycat, w["w_out"], "nn", "out_proj", epi=residual_norm_epi, extras=(x,), fulls=(g_xa,),
                 out_dtypes=(F32, BF16))
    qx = _mm(h2, w["xa_w_q"], "nn", "xa_q")
    memn, kv, kn_x, vv_x = _mem_fwd(mem, g_mem, w["xa_w_kv"], xa_gk, "xa_mem")
    o_xa = _xa_fwd(qx, xa_gq, kn_x, vv_x, "xa_fwd")
    x2, h3 = _mm(o_xa, w["xa_w_o"], "nn", "xa_o", epi=residual_norm_epi, extras=(x1,), fulls=(g_mlp,),
                 out_dtypes=(F32, BF16))

    def up_epi(r):
        rl = jnp.maximum(r, 0.0)
        return (rl * rl,)

    r_up = _mm(h3, w["w_up"], "nn", "mlp_up", epi=up_epi, out_dtypes=(BF16,))

    def loss_epi(r, xt, tt):
        d = r + xt - tt
        return (d * (1.0 / D_MODEL),) * 2, (jnp.sum(d * d, axis=0, keepdims=True),)

    dx3, dx3_b, sq = _mm(r_up, w["w_down"], "nn", "mlp_down", epi=loss_epi, extras=(x2, target),
                         out_dtypes=(F32, BF16), sums=[(1, D_MODEL)])
    loss = jnp.sum(sq) * (0.5 / D_MODEL)

    def norm_bwd_epi(r, xt, drt, gt):
        _, vjp = jax.vjp(_rms, xt, gt)
        dx_, dg_ = vjp(r)
        return (dx_ + drt,) * 2, (dg_,)

    g["w_down"] = _mm(r_up, dx3_b, "tn", "d_w_down", tk=2048)
    da = _mm(dx3_b, w["w_down"], "nt", "d_r", epi=lambda r, rt: (r * 2.0 * jnp.sqrt(rt.astype(F32)),), extras=(r_up,),
             out_dtypes=(BF16,))
    g["w_up"] = _mm(h3, da, "tn", "d_w_up", tk=2048)
    mlp = ["w_down", "w_up"]
    (dx2, dx2_b, g["g_mlp"]), got = _mm(da, w["w_up"], "nt", "d_h3", epi=norm_bwd_epi, extras=(x2, dx3),
                                        fulls=(g_mlp,), out_dtypes=(F32, BF16), sums=[g_mlp.shape],
                                        side=to_sibling(mlp))
    add_sibling(mlp, got)
    g["xa_w_o"] = _mm(o_xa, dx2_b, "tn", "d_xa_w_o", tk=2048)
    do_xa = _mm(dx2_b, w["xa_w_o"], "nt", "d_o_xa")
    dqx, dkn_x, dvv_x, g["xa_g_q"] = _xa_bwd(qx, xa_gq, kn_x, vv_x, do_xa, "xa_bwd")
    g["xa_w_kv"], g["g_mem"], g["xa_g_k"] = _mem_bwd(mem, g_mem, memn, w["xa_w_kv"], kv, xa_gk, dkn_x, dvv_x,
                                                     "xa_mem_bwd")
    g["xa_w_q"] = _mm(h2, dqx, "tn", "d_xa_w_q", tk=2048)
    dx1, dx1_b, g["g_xa"] = _mm(dqx, w["xa_w_q"], "nt", "d_h2", epi=norm_bwd_epi, extras=(x1, dx2), fulls=(g_xa,),
                                out_dtypes=(F32, BF16), sums=[g_xa.shape])
    g["w_out"] = _mm(ycat, dx1_b, "tn", "d_w_out", tk=2048)
    dycat = _mm(dx1_b, w["w_out"], "nt", "d_ycat")

    def cat_bwd(a, b, dy, ga, gb):
        _, vjp = jax.vjp(cat_norm, a, b, ga, gb)
        da_, db_, dga, dgb = vjp(dy)
        return (da_, db_), (dga, dgb)

    dy_ssm, dy_sb, g["g_out_ssm"], g["g_out_sb"] = _rw(
        cat_bwd, [y_ssm, y_sb, dycat], [g_os, g_ob], [(SSM_WIDTH, F32), (SB_WIDTH, F32)], [g_os.shape, g_ob.shape],
        "d_norm_out")

    def glu_bwd(dy, yt, zt):
        sg = jax.nn.sigmoid(zt)
        return (dy * sg, dy * yt * sg * (1.0 - sg)), ()

    dy1_a, dz = _rw(glu_bwd, [_to_segments(dy_ssm), y1, z_glu], [], [(SSM_WIDTH, F32), (SSM_WIDTH, BF16)], [], "d_glu")
    g["ssm_w_glu"] = _mm(y1, dz, "tn", "d_w_glu", tk=2048)

    def gelu_bwd_epi(r, da_, y0t):
        _, vjp = jax.vjp(jax.nn.gelu, y0t)
        return (vjp(r + da_)[0],)

    mid = ["w_out", "xa_w_q", "xa_w_kv", "xa_w_o", "ssm_w_glu"]
    dy0, got = _mm(dz, w["ssm_w_glu"], "nt", "d_y1", epi=gelu_bwd_epi, extras=(dy1_a, y0), side=to_sibling(mid))
    add_sibling(mid, got)
    (du, da8, d_bsup, d_csup, g["ssm_d"]), got = _ssm_bwd(dy0, states, u, acat, bsup, csup, d_skip, "ssm_bwd",
                                                          side=to_chips(mlp))
    keep(mlp, got)
    d_acat = jnp.sum(da8, axis=0, keepdims=True)
    d_mats = _ssm_mats_bwd(*ssm_args[:5], d_acat, d_bsup, d_csup, "ssm_mats_bwd")
    for nm, val in zip(("ssm_a_re", "ssm_a_im", "ssm_log_dt", "ssm_b_re", "ssm_b_im", "ssm_c_re", "ssm_c_im"),
                       _from_states_on_lanes(*d_mats)):
        g[nm] = val

    (dqs, dks, dvs), got = _sb_bwd(qs, ks, proj, y_sb, dy_sb, "sb_bwd", v_col=v_col, side=to_chips(mid))
    keep(mid, got)

    def d_proj_rows(du_t, qt, dqt, kt, dkt, dvt, gq, gk):
        _, vjp_q = jax.vjp(lambda a, b_: _rms_groups(a, b_, sb_scale), qt, gq)
        _, vjp_k = jax.vjp(lambda a, b_: _rms_groups(a, b_, 1.0), kt, gk)
        (dq_, dgq_), (dk_, dgk_) = vjp_q(dqt), vjp_k(dkt)
        rows = jnp.concatenate([du_t, dq_.astype(BF16), dk_.astype(BF16), dvt.astype(BF16)], axis=1)
        return (rows,), (dgq_, dgk_)

    dproj, dgq, dgk = _rw(d_proj_rows, [_from_segments(du), q_raw, dqs, k_raw, dks, dvs], [sb_gq, sb_gk],
                          [(IN_WIDTH, BF16)], [sb_gq.shape, sb_gk.shape], "d_proj")
    g["sb_g_q"] = jnp.sum(dgq.reshape(SB_HEADS, SB_HEAD_DIM), axis=0)
    g["sb_g_k"] = jnp.sum(dgk.reshape(SB_HEADS, SB_HEAD_DIM), axis=0)
    g["w_in"] = _mm(h1, dproj, "tn", "d_w_in", tk=2048)
    dh1, got = _mm(dproj, w["w_in"], "nt", "d_h1", side=to_sibling(["w_in"]))
    add_sibling(["w_in"], got)
    dx, g["g_mix"] = _norm_bwd(x, g_mix, dh1, dx1, "d_norm_mix")

    packed = _pack_small([g[n] for n in SMALL] + [loss.reshape(1)])
    everyone, got = _all_gather(packed, "gather_small", to_chips(["w_in"]))
    keep(["w_in"], got)
    return dx, everyone, reduced


def kernel(x, mem, g_mix, w_in, ssm_a_re, ssm_a_im, ssm_log_dt, ssm_b_re, ssm_b_im, ssm_c_re, ssm_c_im, ssm_d, ssm_w_glu, sb_g_q, sb_g_k, g_out_ssm, g_out_sb, w_out, g_xa, g_mem, xa_w_q, xa_w_kv, xa_g_q, xa_g_k, xa_w_o, g_mlp, w_up, w_down, loss_target, m_g_mix, m_w_in, m_ssm_a_re, m_ssm_a_im, m_ssm_log_dt, m_ssm_b_re, m_ssm_b_im, m_ssm_c_re, m_ssm_c_im, m_ssm_d, m_ssm_w_glu, m_sb_g_q, m_sb_g_k, m_g_out_ssm, m_g_out_sb, m_w_out, m_g_xa, m_g_mem, m_xa_w_q, m_xa_w_kv, m_xa_g_q, m_xa_g_k, m_xa_w_o, m_g_mlp, m_w_up, m_w_down, v_g_mix, v_w_in, v_ssm_a_re, v_ssm_a_im, v_ssm_log_dt, v_ssm_b_re, v_ssm_b_im, v_ssm_c_re, v_ssm_c_im, v_ssm_d, v_ssm_w_glu, v_sb_g_q, v_sb_g_k, v_g_out_ssm, v_g_out_sb, v_w_out, v_g_xa, v_g_mem, v_xa_w_q, v_xa_w_kv, v_xa_g_q, v_xa_g_k, v_xa_w_o, v_g_mlp, v_w_up, v_w_down):
    given = dict(locals())
    order = ["g_mix", "w_in", "ssm_a_re", "ssm_a_im", "ssm_log_dt", "ssm_b_re", "ssm_b_im", "ssm_c_re", "ssm_c_im",
             "ssm_d", "ssm_w_glu", "sb_g_q", "sb_g_k", "g_out_ssm", "g_out_sb", "w_out", "g_xa", "g_mem", "xa_w_q",
             "xa_w_kv", "xa_g_q", "xa_g_k", "xa_w_o", "g_mlp", "w_up", "w_down"]
    assert sorted([n for n, _, _ in BIG] + SMALL) == sorted(order)
    core = lax.axis_index("c").astype(jnp.int32).reshape(1)
    chip = (2 * lax.axis_index("x") + lax.axis_index("y")).astype(jnp.int32).reshape(1)

    shards = {n: given[n][0].astype(BF16) for n, _, _ in BIG}
    sm = {n: given[n][0] for n in SMALL}
    dx, everyone, reduced = _step(x[0], mem[0], loss_target[0], shards, sm, core)

    res = {}
    for n, _, _ in BIG:
        own, recv = reduced[n]
        outs = _adam_sharded(own, recv, given[n][0], given["m_" + n][0], given["v_" + n][0], chip, "adam_" + n)
        for kind, val in zip(("grad", "delta", "new_m", "new_v"), outs):
            res[kind + "_" + n] = val[None]

    sizes = [math.prod(sm[n].shape) for n in SMALL] + [1]
    nat = lambda a: a.reshape(_natural_2d(math.prod(a.shape)))
    outs = _adam_replicated(everyone, sizes, [nat(sm[n]) for n in SMALL], [nat(given["m_" + n][0]) for n in SMALL],
                            [nat(given["v_" + n][0]) for n in SMALL], "adam_replicated")
    for i, n in enumerate(SMALL):
        for kind, val in zip(("grad", "delta", "new_m", "new_v"), outs[4 * i:4 * i + 4]):
            res[kind + "_" + n] = val.reshape(given[n].shape)
    loss_out = outs[-1][0, 0]
    return (loss_out, dx[None], *[res["grad_" + n] for n in order], *[res["delta_" + n] for n in order],
            *[res["new_m_" + n] for n in order], *[res["new_v_" + n] for n in order])
```

```python
import functools
import math

import jax
import jax.numpy as jnp
from jax import lax
from jax.experimental import pallas as pl
from jax.experimental.pallas import tpu as pltpu

F32 = jnp.float32
BF16 = jnp.bfloat16
MESH = pl.DeviceIdType.MESH

N_DEV = 8
D_MODEL = 1024
SSM_WIDTH = 512
SSM_GROUP = 16
SSM_GROUPS = 32
SSM_STATE = 64
N_STATE = SSM_GROUPS * SSM_STATE
SB_HEADS = 8
SB_HEAD_DIM = 64
SB_WIDTH = 512
IN_WIDTH = 2048
XA_HEADS = 4
XA_HEAD_DIM = 128
XA_WIDTH = 512
D_FF = 4096
NORM_EPS = 1e-6
ADAM_LR = 0.001
ADAM_B1 = 0.9
ADAM_B2 = 0.999
ADAM_EPS = 1e-08
ADAM_WD = 0.01
ADAM_STEP = 10

LANES = 128
SUBLANES = 8
VMEM_LIMIT = 56 * 1024 * 1024
SCAN_LANES = 512
SB_BLOCK = 256
SB_Q_BLOCKS = 4
SB_UNDERFLOW = -110.0

NN = (((1,), (0,)), ((), ()))
NT = (((1,), (1,)), ((), ()))
TN = (((0,), (0,)), ((), ()))


def _params(sem=None):
    return pltpu.CompilerParams(dimension_semantics=sem, vmem_limit_bytes=VMEM_LIMIT)


def _dot(a, b, dims=NN):
    return lax.dot_general(a.astype(BF16), b.astype(BF16), dims, preferred_element_type=F32)


def _rms(x, g):
    return x * lax.rsqrt(jnp.mean(x * x, axis=-1, keepdims=True) + NORM_EPS) * g


ANY = pl.BlockSpec(memory_space=pl.ANY)


class _Side:
    def __init__(self, ins, out_shapes, n_sem, make, finish=None):
        self.ins, self.out_shapes, self.n_sem, self.make = list(ins), list(out_shapes), n_sem, make
        self.finish = finish

    def sems(self):
        return [pltpu.SemaphoreType.DMA((self.n_sem,)), pltpu.SemaphoreType.DMA((self.n_sem,))]


def _hosted(body, side, n_in, n_out, grid):
    if side is None:
        return body
    ns_in, ns_out = len(side.ins), len(side.out_shapes)

    def wrapped(*refs):
        ins, refs = refs[:n_in], refs[n_in:]
        s_ins, refs = refs[:ns_in], refs[ns_in:]
        outs, refs = refs[:n_out], refs[n_out:]
        s_outs, refs = refs[:ns_out], refs[ns_out:]
        scratch, sems = refs[:-2], refs[-2:]
        ids = [pl.program_id(d) for d in range(len(grid))]
        first = functools.reduce(jnp.logical_and, [i == 0 for i in ids])
        last = functools.reduce(jnp.logical_and, [i == n - 1 for i, n in zip(ids, grid)])

        @pl.when(first)
        def _():
            for cp in side.make(s_ins, s_outs, *sems):
                cp.start()

        body(*ins, *outs, *scratch)

        @pl.when(last)
        def _():
            if side.finish is not None:
                side.finish(s_ins, s_outs, *sems)
            else:
                for cp in side.make(s_ins, s_outs, *sems):
                    cp.wait()

    return wrapped


def _side_args(side):
    if side is None:
        return [], [], [], [], []
    return ([ANY] * len(side.ins), [ANY] * len(side.out_shapes), side.out_shapes, side.sems(), side.ins)


def _split_side(res, n_out, side):
    res = list(res)
    main = res[0] if n_out == 1 else res[:n_out]
    return main if side is None else (main, res[n_out:])


def _mm(a, b, mode, name, *, epi=None, extras=(), fulls=(), out_dtypes=(F32,), sums=(), tm=1024, tn=1024, tk=1024,
        side=None):
    if mode == "nn":
        (m, k), (k2, n) = a.shape, b.shape
    elif mode == "nt":
        (m, k), (n, k2) = a.shape, b.shape
    else:
        (k, m), (k2, n) = a.shape, b.shape
    assert k == k2, (name, a.shape, b.shape)
    tm, tn, tk = min(tm, m), min(tn, n), min(tk, k)
    assert m % tm == 0 and n % tn == 0 and k % tk == 0, (name, m, n, k)
    nk = k // tk
    dims = {"nn": NN, "nt": NT, "tn": TN}[mode]
    if mode == "tn":
        a_spec = pl.BlockSpec((tk, tm), lambda i, j, kk: (kk, i))
    else:
        a_spec = pl.BlockSpec((tm, tk), lambda i, j, kk: (i, kk))
    if mode == "nt":
        b_spec = pl.BlockSpec((tn, tk), lambda i, j, kk: (j, kk))
    else:
        b_spec = pl.BlockSpec((tk, tn), lambda i, j, kk: (kk, j))
    mn_spec = pl.BlockSpec((tm, tn), lambda i, j, kk: (i, j))
    n_ex, n_full, n_out, n_sum = len(extras), len(fulls), len(out_dtypes), len(sums)
    n_in = 2 + n_ex + n_full

    def body(*refs):
        a_ref, b_ref = refs[:2]
        ex = refs[2:n_in]
        outs = refs[n_in:n_in + n_out]
        sum_refs = refs[n_in + n_out:n_in + n_out + n_sum]
        kk = pl.program_id(2)
        first_tile = jnp.logical_and(pl.program_id(0) == 0, pl.program_id(1) == 0)

        def finish(r):
            vals = epi(r, *[e[...] for e in ex]) if epi is not None else (r,)
            if n_sum:
                vals, parts = vals

                @pl.when(first_tile)
                def _():
                    for sr in sum_refs:
                        sr[...] = jnp.zeros_like(sr)

                for sr, p in zip(sum_refs, parts):
                    sr[...] += p
            for o, v in zip(outs, vals):
                o[...] = v.astype(o.dtype)

        if nk == 1:
            finish(_dot(a_ref[...], b_ref[...], dims))
        else:
            acc = refs[n_in + n_out + n_sum]

            @pl.when(kk == 0)
            def _():
                acc[...] = jnp.zeros_like(acc)

            acc[...] += _dot(a_ref[...], b_ref[...], dims)

            @pl.when(kk == nk - 1)
            def _():
                finish(acc[...])

    grid = (m // tm, n // tn, nk)
    whole = lambda shape: pl.BlockSpec(shape, lambda i, j, kk: (0,) * len(shape))
    s_in, s_out, s_shape, s_scratch, s_ops = _side_args(side)
    seq = bool(side) or n_sum > 0
    res = pl.pallas_call(
        _hosted(body, side, n_in, n_out + n_sum, grid), name=name, grid=grid,
        in_specs=[a_spec, b_spec] + [mn_spec] * n_ex + [whole(f.shape) for f in fulls] + s_in,
        out_specs=[mn_spec] * n_out + [whole(shape) for shape in sums] + s_out,
        out_shape=[jax.ShapeDtypeStruct((m, n), dt) for dt in out_dtypes]
        + [jax.ShapeDtypeStruct(shape, F32) for shape in sums] + s_shape,
        scratch_shapes=([pltpu.VMEM((tm, tn), F32)] if nk > 1 else []) + s_scratch,
        compiler_params=_params(("arbitrary",) * 3 if seq else ("parallel", "parallel", "arbitrary")),
    )(a, b, *extras, *fulls, *s_ops)
    return _split_side(res, n_out + n_sum, side)


def _row_tile(s, target):
    if s <= target:
        return s
    return max(t for t in range(16, target + 1, 16) if s % t == 0)


def _rw(fn, rows, fulls, row_out, acc_out, name, tm=1024, side=None):
    cols = [r[1:] if isinstance(r, tuple) else (r.shape[1], 0) for r in rows]
    rows = [r[0] if isinstance(r, tuple) else r for r in rows]
    s = rows[0].shape[0]
    tm = _row_tile(s, tm)
    nr, nf, nro, nao = len(rows), len(fulls), len(row_out), len(acc_out)

    def body(*refs):
        r = refs[:nr]
        f = refs[nr:nr + nf]
        ro = refs[nr + nf:nr + nf + nro]
        ao = refs[nr + nf + nro:]
        outs, accs = fn(*[x[...] for x in r], *[x[...] for x in f])
        for o, v in zip(ro, outs):
            o[...] = v.astype(o.dtype)
        if nao:
            @pl.when(pl.program_id(0) == 0)
            def _():
                for a in ao:
                    a[...] = jnp.zeros_like(a)

            for a, v in zip(ao, accs):
                a[...] += v

    full_spec = lambda shape: pl.BlockSpec(shape, lambda i: (0,) * len(shape))
    s_in, s_out, s_shape, s_scratch, s_ops = _side_args(side)
    res = pl.pallas_call(
        _hosted(body, side, nr + nf, nro + nao, (s // tm,)), name=name, grid=(s // tm,),
        in_specs=[pl.BlockSpec((tm, wd), functools.partial(lambda i, cb: (i, cb), cb=cb)) for wd, cb in cols]
        + [full_spec(x.shape) for x in fulls] + s_in,
        out_specs=[pl.BlockSpec((tm, d), lambda i: (i, 0)) for d, _ in row_out]
        + [full_spec(shape) for shape in acc_out] + s_out,
        out_shape=[jax.ShapeDtypeStruct((s, d), dt) for d, dt in row_out]
        + [jax.ShapeDtypeStruct(shape, F32) for shape in acc_out] + s_shape,
        scratch_shapes=s_scratch,
        compiler_params=_params(("arbitrary",)),
    )(*rows, *fulls, *s_ops)
    res = list(res)
    return res if side is None else (res[:nro + nao], res[nro + nao:])


def _norm_fwd(x, g, name, side=None):
    res = _rw(lambda xt, gt: ((_rms(xt, gt),), ()), [x], [g], [(x.shape[1], BF16)], [], name, side=side)
    return res[0] if side is None else (res[0][0], res[1])


def _norm_bwd(x, g, dh, dres, name, side=None):
    def fn(xt, dht, drt, gt):
        _, vjp = jax.vjp(_rms, xt, gt)
        dx, dg = vjp(dht)
        return (dx + drt,), (dg,)

    return _rw(fn, [x, dh, dres], [g], [(x.shape[1], F32)], [g.shape], name, side=side)


def _rms_groups(x, g, scale):
    lo = lax.broadcasted_iota(jnp.int32, (1, LANES), 1) < SB_HEAD_DIM
    x2 = x * x
    outs = []
    for cb in range(x.shape[1] // LANES):
        sl = slice(cb * LANES, (cb + 1) * LANES)
        s_lo = jnp.sum(jnp.where(lo, x2[:, sl], 0.0), axis=-1, keepdims=True)
        s_hi = jnp.sum(jnp.where(lo, 0.0, x2[:, sl]), axis=-1, keepdims=True)
        r = jnp.where(lo, lax.rsqrt(s_lo * (1.0 / SB_HEAD_DIM) + NORM_EPS),
                      lax.rsqrt(s_hi * (1.0 / SB_HEAD_DIM) + NORM_EPS))
        outs.append(x[:, sl] * r)
    return jnp.concatenate(outs, axis=1) * g * scale


def _log_sigmoid(z):
    return jnp.minimum(z, 0.0) - jnp.log(1.0 + jnp.exp(-jnp.abs(z)))


def _split_dot(x, u2):
    hi = x.astype(BF16)
    lo = (x - hi.astype(F32)).astype(BF16)
    return jnp.dot(jnp.concatenate([hi, lo], axis=1), u2, preferred_element_type=F32)


def _sb_consts(b):
    row = lax.broadcasted_iota(jnp.int32, (b, b), 0)
    col = lax.broadcasted_iota(jnp.int32, (b, b), 1)
    tri = col < row
    u_after = (row > col).astype(BF16)
    u_from = (row >= col).astype(BF16)
    stack = lambda u: jnp.concatenate([u, u], axis=0)
    lane_lo = lax.broadcasted_iota(jnp.int32, (b, LANES), 1) < SB_HEAD_DIM
    return tri, stack(u_after), stack(u_from), lane_lo


def _sb_scores(qh, kb, a_run, keep, u2_after, mask_l=True):
    z = lax.dot_general(qh, kb, NT, preferred_element_type=F32)
    lb = _log_sigmoid(z)
    l = lb - z
    if keep is not None and mask_l:
        l = jnp.where(keep, l, 0.0)
    w = jnp.exp(lb + (a_run + _split_dot(l, u2_after)))
    if keep is not None:
        w = jnp.where(keep, w, 0.0)
    return lb, l, w


def _sb_walk(qi, carry, step):
    def cond(state):
        n, c = state
        return jnp.logical_and(n <= qi, jnp.max(jnp.maximum(c[0], c[1])) > SB_UNDERFLOW)

    def body(state):
        n, c = state
        return n + 1, step(n, c)

    return lax.while_loop(cond, body, (jnp.int32(2), carry))[1]


def _two_heads(x, lane_lo):
    zero = jnp.zeros_like(x)
    return jnp.where(lane_lo, x, zero), jnp.where(lane_lo, zero, x)


def _sb_fwd(qs, ks, v, name, v_col=0, side=None):
    s, width = qs.shape
    b = min(SB_BLOCK, s)
    nqb = min(SB_Q_BLOCKS, s // b)

    def body(q_ref, k_ref, v_ref, o_ref):
        tri, u2_after, _, lane_lo = _sb_consts(b)
        zero = jnp.zeros((b, 1), F32)
        started = []
        for h in range(nqb):
            qi = pl.program_id(1) * nqb + h
            q_a, q_b = _two_heads(q_ref[h * b:(h + 1) * b, :], lane_lo)

            def step(n, carry, keep, mask_l=True, qi=qi, q_a=q_a, q_b=q_b):
                a_a, a_b, acc = carry
                off = pl.multiple_of(jnp.maximum(qi - n, 0) * b, b)
                kb = k_ref[pl.ds(off, b), :]
                v_a, v_b = _two_heads(v_ref[pl.ds(off, b), :].astype(BF16), lane_lo)
                _, l_a, w_a = _sb_scores(q_a, kb, a_a, keep, u2_after, mask_l)
                _, l_b, w_b = _sb_scores(q_b, kb, a_b, keep, u2_after, mask_l)
                acc = acc + jnp.dot(jnp.concatenate([w_a.astype(BF16), w_b.astype(BF16)], axis=1),
                                    jnp.concatenate([v_a, v_b], axis=0), preferred_element_type=F32)
                return (a_a + jnp.sum(l_a, axis=1, keepdims=True), a_b + jnp.sum(l_b, axis=1, keepdims=True), acc)

            carry = step(0, (zero, zero, jnp.zeros((b, LANES), F32)), tri)
            carry = step(1, carry, jnp.broadcast_to(qi > 0, tri.shape), mask_l=False)
            started.append((qi, step, carry))
        for h, (qi, step, carry) in enumerate(started):
            carry = _sb_walk(qi, carry, lambda n, c, step=step: step(n, c, None))
            o_ref[h * b:(h + 1) * b, :] = carry[2]

    blk = pl.BlockSpec((nqb * b, LANES), lambda hp, i: (i, hp))
    full = pl.BlockSpec((s, LANES), lambda hp, i: (0, hp))
    full_v = pl.BlockSpec((s, LANES), lambda hp, i: (0, hp + v_col))
    grid = (width // LANES, s // (nqb * b))
    s_in, s_out, s_shape, s_scratch, s_ops = _side_args(side)
    res = pl.pallas_call(
        _hosted(body, side, 3, 1, grid), name=name, grid=grid,
        in_specs=[blk, full, full_v] + s_in, out_specs=[blk] + s_out,
        out_shape=[jax.ShapeDtypeStruct((s, width), F32)] + s_shape, scratch_shapes=s_scratch,
        compiler_params=_params(("arbitrary", "arbitrary")),
    )(qs, ks, v, *s_ops)
    return _split_side(res, 1, side)


def _sb_bwd(qs, ks, v, out, dout, name, v_col=0, side=None):
    s, width = qs.shape
    b = min(SB_BLOCK, s)
    nqb = min(SB_Q_BLOCKS, s // b)

    def body(q_ref, k_ref, v_ref, o_ref, do_ref, dq_ref, dk_ref, dv_ref):
        @pl.when(pl.program_id(1) == 0)
        def _():
            dk_ref[...] = jnp.zeros_like(dk_ref)
            dv_ref[...] = jnp.zeros_like(dv_ref)

        tri, u2_after, u2_from, lane_lo = _sb_consts(b)
        zero = jnp.zeros((b, 1), F32)

        def head(qh, doh, kb, vb, a_run, d_rem, keep, mask_l):
            lb, l, w = _sb_scores(qh, kb, a_run, keep, u2_after, mask_l)
            wb = w.astype(BF16)
            g = lax.dot_general(doh, vb, NT, preferred_element_type=F32) * wb.astype(F32)
            g_before = d_rem - _split_dot(g, u2_from)
            dz = g - (g + g_before) * jnp.exp(lb)
            if keep is not None:
                dz = jnp.where(keep, dz, 0.0)
            return (dz.astype(BF16), wb, a_run + jnp.sum(l, axis=1, keepdims=True),
                    d_rem - jnp.sum(g, axis=1, keepdims=True))

        started = []
        for h in range(nqb):
            qi = pl.program_id(1) * nqb + h
            rows = slice(h * b, (h + 1) * b)
            q_a, q_b = _two_heads(q_ref[rows, :], lane_lo)
            dob = do_ref[rows, :].astype(BF16)
            do_a, do_b = _two_heads(dob, lane_lo)
            prod = dob.astype(F32) * o_ref[rows, :]
            d_a = jnp.sum(jnp.where(lane_lo, prod, 0.0), axis=1, keepdims=True)
            d_b = jnp.sum(jnp.where(lane_lo, 0.0, prod), axis=1, keepdims=True)
            q_rows = jnp.concatenate([q_a, q_b], axis=0)
            do_rows = jnp.concatenate([do_a, do_b], axis=0)

            def step(n, carry, keep, mask_l=True, qi=qi, q_a=q_a, q_b=q_b, do_a=do_a, do_b=do_b, q_rows=q_rows,
                     do_rows=do_rows):
                a_a, a_b, r_a, r_b, dq = carry
                off = pl.multiple_of(jnp.maximum(qi - n, 0) * b, b)
                kb = k_ref[pl.ds(off, b), :]
                vb = v_ref[pl.ds(off, b), :].astype(BF16)
                k_a, k_b = _two_heads(kb, lane_lo)
                dz_a, w_a, a_a, r_a = head(q_a, do_a, kb, vb, a_a, r_a, keep, mask_l)
                dz_b, w_b, a_b, r_b = head(q_b, do_b, kb, vb, a_b, r_b, keep, mask_l)
                dq = dq + jnp.dot(jnp.concatenate([dz_a, dz_b], axis=1), jnp.concatenate([k_a, k_b], axis=0),
                                  preferred_element_type=F32)
                dk_ref[pl.ds(off, b), :] += lax.dot_general(jnp.concatenate([dz_a, dz_b], axis=0), q_rows, TN,
                                                            preferred_element_type=F32)
                dv_ref[pl.ds(off, b), :] += lax.dot_general(jnp.concatenate([w_a, w_b], axis=0), do_rows, TN,
                                                            preferred_element_type=F32)
                return a_a, a_b, r_a, r_b, dq

            carry = step(0, (zero, zero, d_a, d_b, jnp.zeros((b, LANES), F32)), tri)
            carry = step(1, carry, jnp.broadcast_to(qi > 0, tri.shape), mask_l=False)
            started.append((qi, step, carry))
        for h, (qi, step, carry) in enumerate(started):
            carry = _sb_walk(qi, carry, lambda n, c, step=step: step(n, c, None))
            dq_ref[h * b:(h + 1) * b, :] = carry[4]

    blk = pl.BlockSpec((nqb * b, LANES), lambda hp, i: (i, hp))
    full = pl.BlockSpec((s, LANES), lambda hp, i: (0, hp))
    full_v = pl.BlockSpec((s, LANES), lambda hp, i: (0, hp + v_col))
    grid = (width // LANES, s // (nqb * b))
    s_in, s_out, s_shape, s_scratch, s_ops = _side_args(side)
    res = pl.pallas_call(
        _hosted(body, side, 5, 3, grid), name=name, grid=grid,
        in_specs=[blk, full, full_v, blk, blk] + s_in, out_specs=[blk, full, full] + s_out,
        out_shape=[jax.ShapeDtypeStruct((s, width), F32)] * 3 + s_shape,
        scratch_shapes=s_scratch,
        compiler_params=_params(("arbitrary", "arbitrary")),
    )(qs, ks, v, out, dout, *s_ops)
    return _split_side(res, 3, side)


def _cmul(xr, xi, yr, yi):
    return xr * yr - xi * yi, xr * yi + xi * yr


def _scan_consts(ar, ai, reverse, lc):
    rowi = lax.broadcasted_iota(jnp.int32, (SUBLANES, lc), 0)
    pows = [(ar, ai)]
    for _ in range(SUBLANES - 1):
        pows.append(_cmul(*pows[-1], ar, ai))
    steps = []
    for d in (1, 2, 4):
        keep = (rowi < SUBLANES - d) if reverse else (rowi >= d)
        pr, pi = pows[d - 1]
        steps.append((SUBLANES - d if reverse else d, jnp.where(keep, pr, 0.0), jnp.where(keep, pi, 0.0)))
    cr = jnp.zeros((SUBLANES, lc), F32)
    ci = jnp.zeros((SUBLANES, lc), F32)
    for r in range(SUBLANES):
        pr, pi = pows[SUBLANES - 1 - r] if reverse else pows[r]
        cr = jnp.where(rowi == r, pr, cr)
        ci = jnp.where(rowi == r, pi, ci)
    return steps, cr, ci


def _scan_tile(xr, xi, steps, pr, pi, cr, ci):
    for shift, ar, ai in steps:
        rr = pltpu.roll(xr, shift, 0)
        ri = pltpu.roll(xi, shift, 0)
        xr, xi = xr + ar * rr - ai * ri, xi + ar * ri + ai * rr
    return xr + pr * cr - pi * ci, xi + pr * ci + pi * cr


SCAN_ROWS = 1024


def _scan_chunk(s):
    tt = min(SCAN_ROWS, s)
    seg = tt // SUBLANES
    assert s % tt == 0 and seg % SUBLANES == 0 and seg & (seg - 1) == 0, s
    return tt, seg


def _to_segments(a):
    s, wd = a.shape
    tt, seg = _scan_chunk(s)
    return jnp.transpose(a.reshape(s // tt, SUBLANES, seg, wd), (0, 2, 1, 3)).reshape(s, wd)


def _from_segments(a):
    s, wd = a.shape
    tt, seg = _scan_chunk(s)
    return jnp.transpose(a.reshape(s // tt, seg, SUBLANES, wd), (0, 2, 1, 3)).reshape(s, wd)


def _cpow2(xr, xi, k):
    for _ in range(k):
        xr, xi = _cmul(xr, xi, xr, xi)
    return xr, xi


def _fill_powers(pw_ref, ar, ai, seg, lc):
    _, p8r, p8i = _scan_consts(ar, ai, False, lc)
    a8r, a8i = _cpow2(ar, ai, 3)
    qr, qi = jnp.ones_like(ar), jnp.zeros_like(ai)
    for k in range(seg // SUBLANES):
        tr, ti = _cmul(p8r, p8i, qr, qi)
        for r in range(SUBLANES):
            rows = pl.ds((SUBLANES * k + r) * SUBLANES, SUBLANES)
            pw_ref[rows, :lc] = jnp.broadcast_to(tr[r:r + 1, :], (SUBLANES, lc))
            pw_ref[rows, lc:] = jnp.broadcast_to(ti[r:r + 1, :], (SUBLANES, lc))
        qr, qi = _cmul(qr, qi, a8r, a8i)


def _ssm_fwd(u, acat, bsup, csup, d_skip, name, side=None):
    s = u.shape[0]
    lc = SCAN_LANES
    tt, seg = _scan_chunk(s)
    nl, nt = N_STATE // lc, s // tt
    tile = lambda j: pl.ds(pl.multiple_of(j * SUBLANES, SUBLANES), SUBLANES)

    def body(u_ref, a_ref, b_ref, c_ref, d_ref, s_ref, y0_ref, y1_ref, carry, pw_ref):
        ar, ai = a_ref[:, :lc], a_ref[:, lc:]

        @pl.when(pl.program_id(1) == 0)
        def _():
            carry[...] = jnp.zeros_like(carry)
            _fill_powers(pw_ref, ar, ai, seg, lc)

        ut = u_ref[...]
        s_ref[...] = _dot(ut, b_ref[0])

        ar8, ai8 = jnp.broadcast_to(ar, (SUBLANES, lc)), jnp.broadcast_to(ai, (SUBLANES, lc))

        def local(j, x):
            xr = ar8 * x[0] - ai8 * x[1] + s_ref[tile(j), :lc]
            xi = ar8 * x[1] + ai8 * x[0] + s_ref[tile(j), lc:]
            s_ref[tile(j), :lc] = xr
            s_ref[tile(j), lc:] = xi
            return xr, xi

        zero = jnp.zeros((SUBLANES, lc), F32)
        er, ei = lax.fori_loop(0, seg, local, (zero, zero))
        steps, pr, pi = _scan_consts(*_cpow2(ar, ai, seg.bit_length() - 1), False, lc)
        cr, ci = carry[:, :lc], carry[:, lc:]
        tr, ti = _scan_tile(er, ei, steps, pr, pi, cr, ci)
        rowi = lax.broadcasted_iota(jnp.int32, (SUBLANES, lc), 0)
        before_r = jnp.where(rowi == 0, cr, pltpu.roll(tr, 1, 0))
        before_i = jnp.where(rowi == 0, ci, pltpu.roll(ti, 1, 0))
        carry[:, :lc] = jnp.broadcast_to(tr[SUBLANES - 1:, :], (SUBLANES, lc))
        carry[:, lc:] = jnp.broadcast_to(ti[SUBLANES - 1:, :], (SUBLANES, lc))

        def fix(j, _):
            pwr, pwi = pw_ref[tile(j), :lc], pw_ref[tile(j), lc:]
            s_ref[tile(j), :lc] += pwr * before_r - pwi * before_i
            s_ref[tile(j), lc:] += pwr * before_i + pwi * before_r
            return 0

        lax.fori_loop(0, seg, fix, 0)
        y0 = _dot(s_ref[...], c_ref[0], NT) + d_ref[...] * ut
        y0_ref[...] = y0
        y1_ref[...] = jax.nn.gelu(y0)

    chan = pl.BlockSpec((tt, LANES), lambda j, c: (c, j))
    sup = pl.BlockSpec((1, LANES, 2 * lc), lambda j, c: (j, 0, 0))
    s_in, s_out, s_shape, s_scratch, s_ops = _side_args(side)
    res = pl.pallas_call(
        _hosted(body, side, 5, 3, (nl, nt)), name=name, grid=(nl, nt),
        in_specs=[chan, pl.BlockSpec((1, 2 * lc), lambda j, c: (0, j)), sup, sup,
                  pl.BlockSpec((1, LANES), lambda j, c: (0, j))] + s_in,
        out_specs=[pl.BlockSpec((tt, 2 * lc), lambda j, c: (c, j)), chan, chan] + s_out,
        out_shape=[jax.ShapeDtypeStruct((s, 2 * N_STATE), F32), jax.ShapeDtypeStruct((s, SSM_WIDTH), F32),
                   jax.ShapeDtypeStruct((s, SSM_WIDTH), F32)] + s_shape,
        scratch_shapes=[pltpu.VMEM((SUBLANES, 2 * lc), F32), pltpu.VMEM((seg * SUBLANES, 2 * lc), F32)] + s_scratch,
        compiler_params=_params(("arbitrary", "arbitrary")),
    )(u, acat, bsup, csup, d_skip, *s_ops)
    return _split_side(res, 3, side)


def _ssm_bwd(dy0, states, u, acat, bsup, csup, d_skip, name, side=None):
    s = u.shape[0]
    lc = SCAN_LANES
    tt, seg = _scan_chunk(s)
    nl, nt = N_STATE // lc, s // tt
    tile = lambda j: pl.ds(pl.multiple_of(j * SUBLANES, SUBLANES), SUBLANES)

    def body(dy_ref, s_ref, sp_ref, u_ref, a_ref, b_ref, c_ref, d_ref,
             du_ref, da_ref, db_ref, dc_ref, dd_ref, lam_ref, carry, pw_ref):
        c = pl.program_id(1)
        ar, ai = a_ref[:, :lc], a_ref[:, lc:]

        @pl.when(c == 0)
        def _():
            carry[...] = jnp.zeros_like(carry)
            for r in (da_ref, db_ref, dc_ref, dd_ref):
                r[...] = jnp.zeros_like(r)
            _fill_powers(pw_ref, ar, ai, seg, lc)

        dy = dy_ref[...]
        ut = u_ref[...]
        lam_ref[...] = _dot(dy, c_ref[0])

        ar8, ai8 = jnp.broadcast_to(ar, (SUBLANES, lc)), jnp.broadcast_to(ai, (SUBLANES, lc))

        def local(i, x):
            j = seg - 1 - i
            xr = ar8 * x[0] + ai8 * x[1] + lam_ref[tile(j), :lc]
            xi = ar8 * x[1] - ai8 * x[0] + lam_ref[tile(j), lc:]
            lam_ref[tile(j), :lc] = xr
            lam_ref[tile(j), lc:] = xi
            return xr, xi

        zero = jnp.zeros((SUBLANES, lc), F32)
        er, ei = lax.fori_loop(0, seg, local, (zero, zero))
        big_r, big_i = _cpow2(ar, ai, seg.bit_length() - 1)
        steps, pr, pi = _scan_consts(big_r, -big_i, True, lc)
        cr, ci = carry[:, :lc], carry[:, lc:]
        tr, ti = _scan_tile(er, ei, steps, pr, pi, cr, ci)
        rowi = lax.broadcasted_iota(jnp.int32, (SUBLANES, lc), 0)
        after_r = jnp.where(rowi == SUBLANES - 1, cr, pltpu.roll(tr, SUBLANES - 1, 0))
        after_i = jnp.where(rowi == SUBLANES - 1, ci, pltpu.roll(ti, SUBLANES - 1, 0))
        carry[:, :lc] = jnp.broadcast_to(tr[:1, :], (SUBLANES, lc))
        carry[:, lc:] = jnp.broadcast_to(ti[:1, :], (SUBLANES, lc))

        start = c != nt - 1
        last_r = jnp.where(start, jnp.broadcast_to(sp_ref[SUBLANES - 1:, :lc], (SUBLANES, lc)), 0.0)
        last_i = jnp.where(start, jnp.broadcast_to(sp_ref[SUBLANES - 1:, lc:], (SUBLANES, lc)), 0.0)
        first_r = jnp.where(rowi == 0, last_r, pltpu.roll(s_ref[tile(seg - 1), :lc], 1, 0))
        first_i = jnp.where(rowi == 0, last_i, pltpu.roll(s_ref[tile(seg - 1), lc:], 1, 0))

        def fix(j, acc):
            dar, dai = acc
            k = seg - 1 - j
            pwr, pwi = pw_ref[tile(k), :lc], pw_ref[tile(k), lc:]
            lr = lam_ref[tile(j), :lc] + pwr * after_r + pwi * after_i
            li = lam_ref[tile(j), lc:] + pwr * after_i - pwi * after_r
            lam_ref[tile(j), :lc] = lr
            lam_ref[tile(j), lc:] = li
            jp = jnp.maximum(j - 1, 0)
            sr = jnp.where(j > 0, s_ref[tile(jp), :lc], first_r)
            si = jnp.where(j > 0, s_ref[tile(jp), lc:], first_i)
            return dar + lr * sr + li * si, dai + li * sr - lr * si

        dar, dai = lax.fori_loop(0, seg, fix, (zero, zero))
        da_ref[:, :lc] += dar
        da_ref[:, lc:] += dai
        lam = lam_ref[...].astype(BF16)
        du_ref[...] = (_dot(lam, b_ref[0], NT) + d_ref[...] * dy).astype(du_ref.dtype)
        db_ref[0] += _dot(ut, lam, TN)
        dc_ref[0] += _dot(dy, s_ref[...], TN)
        dd_ref[...] += jnp.sum(dy * ut, axis=0, keepdims=True)

    rev = lambda j, c: (nt - 1 - c, j)
    chan = pl.BlockSpec((tt, LANES), rev)
    sup = pl.BlockSpec((1, LANES, 2 * lc), lambda j, c: (j, 0, 0))
    row = pl.BlockSpec((1, LANES), lambda j, c: (0, j))
    s_in, s_out, s_shape, s_scratch, s_ops = _side_args(side)
    res = pl.pallas_call(
        _hosted(body, side, 8, 5, (nl, nt)), name=name, grid=(nl, nt),
        in_specs=[chan, pl.BlockSpec((tt, 2 * lc), rev),
                  pl.BlockSpec((SUBLANES, 2 * lc), lambda j, c: (jnp.maximum((nt - 1 - c) * seg - 1, 0), j)),
                  chan, pl.BlockSpec((1, 2 * lc), lambda j, c: (0, j)), sup, sup, row] + s_in,
        out_specs=[chan, pl.BlockSpec((SUBLANES, 2 * lc), lambda j, c: (0, j)), sup, sup, row] + s_out,
        out_shape=[jax.ShapeDtypeStruct((s, SSM_WIDTH), BF16), jax.ShapeDtypeStruct((SUBLANES, 2 * N_STATE), F32),
                   jax.ShapeDtypeStruct(bsup.shape, F32), jax.ShapeDtypeStruct(csup.shape, F32),
                   jax.ShapeDtypeStruct((1, SSM_WIDTH), F32)] + s_shape,
        scratch_shapes=[pltpu.VMEM((tt, 2 * lc), F32), pltpu.VMEM((SUBLANES, 2 * lc), F32),
                        pltpu.VMEM((seg * SUBLANES, 2 * lc), F32)] + s_scratch,
        compiler_params=_params(("arbitrary", "arbitrary")),
    )(dy0, states, states, u, acat, bsup, csup, d_skip, *s_ops)
    return _split_side(res, 5, side)


def _discretise(ar, ai, ldt, br, bi):
    dt = jnp.exp(ldt)
    lr, li = ar * dt, ai * dt
    e = jnp.exp(lr)
    abar_r, abar_i = e * jnp.cos(li), e * jnp.sin(li)
    den = ar * ar + ai * ai
    coef_r = ((abar_r - 1.0) * ar + abar_i * ai) / den
    coef_i = (abar_i * ar - (abar_r - 1.0) * ai) / den
    return abar_r, abar_i, coef_r * br - coef_i * bi, coef_r * bi + coef_i * br


def _group_mask():
    shape = (LANES, SCAN_LANES)
    return (lax.broadcasted_iota(jnp.int32, shape, 0) // SSM_GROUP
            == lax.broadcasted_iota(jnp.int32, shape, 1) // SSM_STATE)


def _ssm_mats_fwd(a_re, a_im, log_dt, b_re, b_im, c_re, c_im, name):
    nl = N_STATE // SCAN_LANES
    lc = SCAN_LANES

    def body(ar, ai, ldt, br, bi, cr, ci, acat, bsup, csup):
        abar_r, abar_i, bbar_r, bbar_i = _discretise(ar[...], ai[...], ldt[...], br[...], bi[...])
        same = _group_mask()
        spread = lambda m, j: jnp.where(same, jnp.tile(m[:, j * lc:(j + 1) * lc], (LANES // SSM_GROUP, 1)), 0.0)
        c_r, c_i = cr[...], -ci[...]
        for j in range(nl):
            acat[:, 2 * j * lc:(2 * j + 1) * lc] = abar_r[:, j * lc:(j + 1) * lc]
            acat[:, (2 * j + 1) * lc:(2 * j + 2) * lc] = abar_i[:, j * lc:(j + 1) * lc]
            bsup[j, :, :lc] = spread(bbar_r, j)
            bsup[j, :, lc:] = spread(bbar_i, j)
            csup[j, :, :lc] = spread(c_r, j)
            csup[j, :, lc:] = spread(c_i, j)

    return pl.pallas_call(
        body, name=name,
        out_shape=[jax.ShapeDtypeStruct((1, 2 * N_STATE), F32), jax.ShapeDtypeStruct((nl, LANES, 2 * lc), F32),
                   jax.ShapeDtypeStruct((nl, LANES, 2 * lc), F32)],
        compiler_params=_params(),
    )(a_re, a_im, log_dt, b_re, b_im, c_re, c_im)


def _ssm_mats_bwd(a_re, a_im, log_dt, b_re, b_im, d_acat, d_bsup, d_csup, name):
    nl = N_STATE // SCAN_LANES
    lc = SCAN_LANES

    def body(ar, ai, ldt, br, bi, dac, dbs, dcs, d_ar, d_ai, d_ldt, d_br, d_bi, d_cr, d_ci):
        same = _group_mask()

        def gather(ref, j, half):
            m = jnp.where(same, ref[j, :, half * lc:(half + 1) * lc], 0.0)
            tot = m[:SSM_GROUP]
            for k in range(1, LANES // SSM_GROUP):
                tot = tot + m[k * SSM_GROUP:(k + 1) * SSM_GROUP]
            return tot

        cols = lambda ref, half: jnp.concatenate([gather(ref, j, half) for j in range(nl)], axis=1)
        d_abar_r = jnp.concatenate([dac[:, 2 * j * lc:(2 * j + 1) * lc] for j in range(nl)], axis=1)
        d_abar_i = jnp.concatenate([dac[:, (2 * j + 1) * lc:(2 * j + 2) * lc] for j in range(nl)], axis=1)
        _, vjp = jax.vjp(_discretise, ar[...], ai[...], ldt[...], br[...], bi[...])
        outs = vjp((d_abar_r, d_abar_i, cols(dbs, 0), cols(dbs, 1)))
        for ref, val in zip((d_ar, d_ai, d_ldt, d_br, d_bi), outs):
            ref[...] = val
        d_cr[...] = cols(dcs, 0)
        d_ci[...] = -cols(dcs, 1)

    row = jax.ShapeDtypeStruct((1, N_STATE), F32)
    mat = jax.ShapeDtypeStruct((SSM_GROUP, N_STATE), F32)
    return pl.pallas_call(
        body, name=name, out_shape=[row, row, row, mat, mat, mat, mat], compiler_params=_params(),
    )(a_re, a_im, log_dt, b_re, b_im, d_acat, d_bsup, d_csup)


def _states_on_lanes(sm):
    flat = lambda a: a.reshape(1, N_STATE)
    chan_b = lambda b: jnp.transpose(b, (2, 0, 1)).reshape(SSM_GROUP, N_STATE)
    chan_c = lambda c: jnp.transpose(c, (1, 0, 2)).reshape(SSM_GROUP, N_STATE)
    return (flat(sm["ssm_a_re"]), flat(sm["ssm_a_im"]), flat(jnp.repeat(sm["ssm_log_dt"], SSM_STATE)),
            chan_b(sm["ssm_b_re"]), chan_b(sm["ssm_b_im"]), chan_c(sm["ssm_c_re"]), chan_c(sm["ssm_c_im"]))


def _from_states_on_lanes(d_ar, d_ai, d_ldt, d_br, d_bi, d_cr, d_ci):
    grp = lambda a: a.reshape(SSM_GROUPS, SSM_STATE)
    back_b = lambda b: jnp.transpose(b.reshape(SSM_GROUP, SSM_GROUPS, SSM_STATE), (1, 2, 0))
    back_c = lambda c: jnp.transpose(c.reshape(SSM_GROUP, SSM_GROUPS, SSM_STATE), (1, 0, 2))
    return (grp(d_ar), grp(d_ai), jnp.sum(grp(d_ldt), axis=1), back_b(d_br), back_b(d_bi), back_c(d_cr), back_c(d_ci))


def _mem_fwd(mem, g_mem, w_kv, g_k, name):
    ml = mem.shape[0]

    def body(mem_ref, gm_ref, w_ref, gk_ref, memn_ref, kv_ref, kn_ref, vv_ref):
        memn = _rms(mem_ref[...], gm_ref[...])
        memn_ref[...] = memn.astype(BF16)
        kv = _dot(memn, w_ref[...])
        kv_ref[...] = kv
        for hh in range(XA_HEADS):
            sl = slice(hh * XA_HEAD_DIM, (hh + 1) * XA_HEAD_DIM)
            kn_ref[:, sl] = _rms(kv[:, sl], gk_ref[...]).astype(BF16)
        vv_ref[...] = kv[:, XA_WIDTH:].astype(BF16)

    return pl.pallas_call(
        body, name=name,
        out_shape=[jax.ShapeDtypeStruct((ml, D_MODEL), BF16), jax.ShapeDtypeStruct((ml, 2 * XA_WIDTH), F32),
                   jax.ShapeDtypeStruct((ml, XA_WIDTH), BF16), jax.ShapeDtypeStruct((ml, XA_WIDTH), BF16)],
        compiler_params=_params(),
    )(mem, g_mem, w_kv, g_k)


def _mem_bwd(mem, g_mem, memn, w_kv, kv, g_k, dkn, dvv, name):
    def body(mem_ref, gm_ref, memn_ref, w_ref, kv_ref, gk_ref, dkn_ref, dvv_ref, dw_ref, dgm_ref, dgk_ref):
        kv = kv_ref[...]
        dgk = jnp.zeros(dgk_ref.shape, F32)
        parts = []
        for hh in range(XA_HEADS):
            sl = slice(hh * XA_HEAD_DIM, (hh + 1) * XA_HEAD_DIM)
            _, vjp = jax.vjp(_rms, kv[:, sl], gk_ref[...])
            dk, dg = vjp(dkn_ref[:, sl])
            parts.append(dk)
            dgk = dgk + dg
        dgk_ref[...] = dgk
        dkv = jnp.concatenate(parts + [dvv_ref[...]], axis=1)
        dw_ref[...] = _dot(memn_ref[...], dkv, TN)
        dmemn = _dot(dkv, w_ref[...], NT)
        _, vjp = jax.vjp(_rms, mem_ref[...], gm_ref[...])
        dgm_ref[...] = vjp(dmemn)[1]

    return pl.pallas_call(
        body, name=name,
        out_shape=[jax.ShapeDtypeStruct((D_MODEL, 2 * XA_WIDTH), F32), jax.ShapeDtypeStruct(g_mem.shape, F32),
                   jax.ShapeDtypeStruct(g_k.shape, F32)],
        compiler_params=_params(),
    )(mem, g_mem, memn, w_kv, kv, g_k, dkn, dvv)


def _xa_head(qx_h, g_q, kn_h, vv_h):
    qn = _rms(qx_h, g_q)
    sc = _dot(qn, kn_h, NT) * (XA_HEAD_DIM ** -0.5)
    sc = sc - jnp.max(sc, axis=-1, keepdims=True)
    e = jnp.exp(sc)
    p = e / jnp.sum(e, axis=-1, keepdims=True)
    return qn, p


def _xa_fwd(qx, g_q, kn, vv, name):
    def fn(qt, gq, knt, vvt):
        outs = []
        for hh in range(XA_HEADS):
            sl = slice(hh * XA_HEAD_DIM, (hh + 1) * XA_HEAD_DIM)
            _, p = _xa_head(qt[:, sl], gq, knt[:, sl], vvt[:, sl])
            outs.append(_dot(p, vvt[:, sl]))
        return (jnp.concatenate(outs, axis=1),), ()

    return _rw(fn, [qx], [g_q, kn, vv], [(XA_WIDTH, BF16)], [], name, tm=512)[0]


def _xa_bwd(qx, g_q, kn, vv, do, name):
    def fn(qt, dot_, gq, knt, vvt):
        dqs, dks, dvs = [], [], []
        dgq = jnp.zeros_like(gq)
        for hh in range(XA_HEADS):
            sl = slice(hh * XA_HEAD_DIM, (hh + 1) * XA_HEAD_DIM)
            qn, p = _xa_head(qt[:, sl], gq, knt[:, sl], vvt[:, sl])
            doh = dot_[:, sl]
            dp = _dot(doh, vvt[:, sl], NT)
            dvs.append(_dot(p, doh, TN))
            ds = p * (dp - jnp.sum(dp * p, axis=-1, keepdims=True)) * (XA_HEAD_DIM ** -0.5)
            dqn = _dot(ds, knt[:, sl])
            dks.append(_dot(ds, qn, TN))
            _, vjp = jax.vjp(_rms, qt[:, sl], gq)
            dq, dg = vjp(dqn)
            dqs.append(dq)
            dgq = dgq + dg
        return ((jnp.concatenate(dqs, axis=1),),
                (jnp.concatenate(dks, axis=1), jnp.concatenate(dvs, axis=1), dgq))

    return _rw(fn, [qx, do], [g_q, kn, vv], [(XA_WIDTH, BF16)], [kn.shape, vv.shape, g_q.shape], name, tm=512)


BIG = [
    ("w_in", (D_MODEL, IN_WIDTH), 1), ("ssm_w_glu", (SSM_WIDTH, SSM_WIDTH), 0), ("w_out", (D_MODEL, D_MODEL), 0),
    ("xa_w_q", (D_MODEL, XA_WIDTH), 0), ("xa_w_kv", (D_MODEL, 2 * XA_WIDTH), 0), ("xa_w_o", (XA_WIDTH, D_MODEL), 1),
    ("w_up", (D_MODEL, D_FF), 1), ("w_down", (D_FF, D_MODEL), 0),
]
BIG_INDEX = {n: i for i, (n, _, _) in enumerate(BIG)}


def _shard_shape(shape, axis):
    return tuple(d // N_DEV if i == axis else d for i, d in enumerate(shape))


def _shard_of(ref, axis, d):
    n = ref.shape[axis] // N_DEV
    return ref.at[pl.ds(d * n, n), :] if axis == 0 else ref.at[:, pl.ds(d * n, n)]


def _gather_side(names, shards):
    idxs = [BIG_INDEX[n] for n in names]

    def make(ins, outs, send_sems, recv_sems):
        x, y, c = lax.axis_index("x"), lax.axis_index("y"), lax.axis_index("c")
        cps = []
        for j, i in enumerate(idxs):
            mine = _shard_of(outs[j], BIG[i][2], 4 * x + 2 * y + c)
            cps.append(pltpu.make_async_copy(ins[j], mine, send_sems.at[N_DEV * j]))
            for rel in range(1, N_DEV):
                to = tuple(1 - p if rel >> bit & 1 else p for p, bit in ((x, 2), (y, 1), (c, 0)))
                cps.append(pltpu.make_async_remote_copy(
                    src_ref=ins[j], dst_ref=mine, send_sem=send_sems.at[N_DEV * j + rel],
                    recv_sem=recv_sems.at[N_DEV * j + rel], device_id=to, device_id_type=MESH))
        return cps

    return _Side(shards, [jax.ShapeDtypeStruct(BIG[i][1], BF16) for i in idxs], N_DEV * len(idxs), make)


def _gather_two_level_side(name, shard):
    i = BIG_INDEX[name]

    def parts(ins, outs, send_sems, recv_sems):
        x, y, c = lax.axis_index("x"), lax.axis_index("y"), lax.axis_index("c")
        sibling = (x, y, 1 - c)
        chips = [(1 - x, y), (x, 1 - y), (1 - x, 1 - y)]

        def place(dev):
            return _shard_of(outs[0], BIG[i][2], 4 * dev[0] + 2 * dev[1] + dev[2])

        def copy(k, blk, to, src=None):
            return pltpu.make_async_remote_copy(
                src_ref=place(blk) if src is None else src, dst_ref=place(blk), send_sem=send_sems.at[k],
                recv_sem=recv_sems.at[k], device_id=to, device_id_type=MESH)

        mine = pltpu.make_async_copy(ins[0], place((x, y, c)), send_sems.at[7])
        first = [copy(0, (x, y, c), sibling, src=ins[0])]
        first += [copy(1 + j, (x, y, c), (*chip, c), src=ins[0]) for j, chip in enumerate(chips)]
        passed = [copy(4 + j, (*chip, c), sibling) for j, chip in enumerate(chips)]
        arrived = [copy(1 + j, (*chip, c), (x, y, c)) for j, chip in enumerate(chips)]
        from_sibling = [copy(0, sibling, (x, y, c))] + [copy(4 + j, (*chip, 1 - c), (x, y, c))
                                                       for j, chip in enumerate(chips)]
        return mine, first, passed, arrived, from_sibling

    def make(ins, outs, send_sems, recv_sems):
        mine, first, _, _, _ = parts(ins, outs, send_sems, recv_sems)
        return [mine] + first

    def finish(ins, outs, send_sems, recv_sems):
        mine, first, passed, arrived, from_sibling = parts(ins, outs, send_sems, recv_sems)
        for got, onward in zip(arrived, passed):
            got.wait_recv()
            onward.start()
        for cp in from_sibling:
            cp.wait_recv()
        for cp in first + passed:
            cp.wait_send()
        mine.wait()

    return _Side([shard], [jax.ShapeDtypeStruct(BIG[i][1], BF16)], N_DEV, make, finish)


def _sibling_side(names, grads):
    idxs = [BIG_INDEX[n] for n in names]

    def make(ins, outs, send_sems, recv_sems):
        x, y, c = lax.axis_index("x"), lax.axis_index("y"), lax.axis_index("c")
        return [pltpu.make_async_remote_copy(
            src_ref=_shard_of(ins[j], BIG[i][2], 2 * k + (1 - c)), dst_ref=outs[j].at[k],
            send_sem=send_sems.at[4 * j + k], recv_sem=recv_sems.at[4 * j + k], device_id=(x, y, 1 - c),
            device_id_type=MESH) for j, i in enumerate(idxs) for k in range(4)]

    shapes = [jax.ShapeDtypeStruct((4,) + _shard_shape(BIG[i][1], BIG[i][2]), F32) for i in idxs]
    return _Side(grads, shapes, 4 * len(idxs), make)


def _chips_side(parts):
    def make(ins, outs, send_sems, recv_sems):
        x, y, c = lax.axis_index("x"), lax.axis_index("y"), lax.axis_index("c")
        chips = [(1 - x, y), (x, 1 - y), (1 - x, 1 - y)]
        return [pltpu.make_async_remote_copy(
            src_ref=ins[j].at[2 * cx + cy], dst_ref=outs[j].at[r], send_sem=send_sems.at[3 * j + r],
            recv_sem=recv_sems.at[3 * j + r], device_id=(cx, cy, c), device_id_type=MESH)
            for r, (cx, cy) in enumerate(chips) for j in range(len(parts))]

    return _Side(parts, [jax.ShapeDtypeStruct((3,) + p.shape[1:], p.dtype) for p in parts], 3 * len(parts), make)


def _reduce_add(grad, recv, axis, core, name):
    rs, cs = recv.shape[1:]
    rt = _row_tile(rs, 256)
    nt = rs // rt

    def body(c_ref, g_ref, r_ref, p_ref, pb_ref):
        sm = g_ref[...] + r_ref[0]
        p_ref[0] = sm
        pb_ref[0] = sm.astype(BF16)

    if axis == 0:
        g_spec = pl.BlockSpec((rt, cs), lambda k, t, c_ref: ((2 * k + c_ref[0]) * nt + t, 0))
    else:
        g_spec = pl.BlockSpec((rt, cs), lambda k, t, c_ref: (t, 2 * k + c_ref[0]))
    slab = pl.BlockSpec((1, rt, cs), lambda k, t, c_ref: (k, t, 0))
    return pl.pallas_call(
        body, name=name,
        grid_spec=pltpu.PrefetchScalarGridSpec(num_scalar_prefetch=1, grid=(4, nt), in_specs=[g_spec, slab],
                                               out_specs=[slab, slab]),
        out_shape=[jax.ShapeDtypeStruct(recv.shape, F32), jax.ShapeDtypeStruct(recv.shape, BF16)],
        compiler_params=_params(("parallel", "parallel")),
    )(core, grad, recv)


def _all_gather(block, name, side):
    m_per, n = block.shape
    ns_in, ns_out = len(side.ins), len(side.out_shapes)

    def body(*refs):
        x_ref, s_ins, out_ref = refs[0], refs[1:1 + ns_in], refs[1 + ns_in]
        s_outs = refs[2 + ns_in:2 + ns_in + ns_out]
        send_sems, recv_sems, local_sem, s_send, s_recv = refs[2 + ns_in + ns_out:]
        others = side.make(s_ins, s_outs, s_send, s_recv)
        for cp in others:
            cp.start()
        x, y, c = lax.axis_index("x"), lax.axis_index("y"), lax.axis_index("c")
        me, sibling = (x, y, c), (x, y, 1 - c)
        chips = [(1 - x, y), (x, 1 - y), (1 - x, 1 - y)]

        def rows(px, py, pc):
            return out_ref.at[pl.ds((4 * px + 2 * py + pc) * m_per, m_per), :]

        def copy(k, blk, to, src=None):
            return pltpu.make_async_remote_copy(
                src_ref=rows(*blk) if src is None else src, dst_ref=rows(*blk),
                send_sem=send_sems.at[k], recv_sem=recv_sems.at[k], device_id=to, device_id_type=MESH)

        mine = pltpu.make_async_copy(x_ref, rows(*me), local_sem)
        mine.start()
        first = [copy(0, me, sibling, src=x_ref)]
        first += [copy(1 + j, me, (*chip, c), src=x_ref) for j, chip in enumerate(chips)]
        for cp in first:
            cp.start()
        passed = [copy(4 + j, (*chip, c), sibling) for j, chip in enumerate(chips)]
        for j, chip in enumerate(chips):
            copy(1 + j, (*chip, c), me).wait_recv()
            passed[j].start()
        copy(0, sibling, me).wait_recv()
        for j, chip in enumerate(chips):
            copy(4 + j, (*chip, 1 - c), me).wait_recv()
        for cp in first + passed:
            cp.wait_send()
        mine.wait()
        for cp in others:
            cp.wait()

    res = pl.pallas_call(
        body, name=name, in_specs=[ANY] * (1 + ns_in), out_specs=[ANY] * (1 + ns_out),
        out_shape=[jax.ShapeDtypeStruct((N_DEV * m_per, n), block.dtype)] + side.out_shapes,
        scratch_shapes=[pltpu.SemaphoreType.DMA((7,)), pltpu.SemaphoreType.DMA((7,)), pltpu.SemaphoreType.DMA]
        + side.sems(),
    )(block, *side.ins)
    return res[0], list(res[1:])


def _adam_math(w, g, m, v):
    m = ADAM_B1 * m + (1.0 - ADAM_B1) * g
    v = ADAM_B2 * v + (1.0 - ADAM_B2) * (g * g)
    m_hat = m / (1.0 - ADAM_B1 ** ADAM_STEP)
    v_hat = v / (1.0 - ADAM_B2 ** ADAM_STEP)
    delta = -ADAM_LR * (m_hat / (jnp.sqrt(v_hat) + ADAM_EPS) + ADAM_WD * w)
    return delta, m, v


def _adam_sharded(own, recv, w, m, v, chip, name):
    rs, cs = w.shape
    rt = _row_tile(rs, 256)

    def body(chip_ref, p_ref, r_ref, w_ref, m_ref, v_ref, g_out, d_out, m_out, v_out):
        g = p_ref[0] + r_ref[0].astype(F32) + r_ref[1].astype(F32) + r_ref[2].astype(F32)
        d, mn, vn = _adam_math(w_ref[...], g, m_ref[...], v_ref[...])
        g_out[...] = g
        d_out[...] = d
        m_out[...] = mn
        v_out[...] = vn

    tile = pl.BlockSpec((rt, cs), lambda t, chip_ref: (t, 0))
    return pl.pallas_call(
        body, name=name,
        grid_spec=pltpu.PrefetchScalarGridSpec(
            num_scalar_prefetch=1, grid=(rs // rt,),
            in_specs=[pl.BlockSpec((1, rt, cs), lambda t, chip_ref: (chip_ref[0], t, 0)),
                      pl.BlockSpec((3, rt, cs), lambda t, chip_ref: (0, t, 0)), tile, tile, tile],
            out_specs=[tile] * 4),
        out_shape=[jax.ShapeDtypeStruct((rs, cs), F32)] * 4,
        compiler_params=_params(("parallel",)),
    )(chip, own, recv, w, m, v)


SMALL = ["g_mix", "ssm_a_re", "ssm_a_im", "ssm_log_dt", "ssm_b_re", "ssm_b_im", "ssm_c_re", "ssm_c_im", "ssm_d",
         "sb_g_q", "sb_g_k", "g_out_ssm", "g_out_sb", "g_xa", "g_mem", "xa_g_q", "xa_g_k", "g_mlp"]
PACK_TILE = SUBLANES * LANES


def _natural_2d(n):
    return (n // LANES, LANES) if n % LANES == 0 else (1, n)


def _pack_small(arrs):
    parts = []
    for a in arrs:
        flat = a.reshape(-1)
        parts.append(jnp.pad(flat, (0, (-flat.shape[0]) % PACK_TILE)))
    return jnp.concatenate(parts).reshape(-1, LANES)


def _adam_replicated(gathered, sizes, ws, ms, vs, name):
    n_w = len(ws)
    r_dev = gathered.shape[0] // N_DEV
    offs, off = [], 0
    for n in sizes:
        offs.append(off)
        off += (n + PACK_TILE - 1) // PACK_TILE * SUBLANES
    assert off == r_dev

    def body(*refs):
        g_ref = refs[0]
        w_refs, m_refs, v_refs = refs[1:1 + n_w], refs[1 + n_w:1 + 2 * n_w], refs[1 + 2 * n_w:1 + 3 * n_w]
        outs = refs[1 + 3 * n_w:]

        def total(i, shape):
            r, cdim = shape
            acc = g_ref[pl.ds(offs[i], r), :cdim]
            for d in range(1, N_DEV):
                acc = acc + g_ref[pl.ds(d * r_dev + offs[i], r), :cdim]
            return acc

        for i in range(n_w):
            g = total(i, w_refs[i].shape)
            d, mn, vn = _adam_math(w_refs[i][...], g, m_refs[i][...], v_refs[i][...])
            for o, val in zip(outs[4 * i:4 * i + 4], (g, d, mn, vn)):
                o[...] = val
        outs[4 * n_w][...] = total(n_w, (SUBLANES, LANES))

    shapes = [w.shape for w in ws]
    return pl.pallas_call(
        body, name=name,
        out_shape=[jax.ShapeDtypeStruct(shp, F32) for shp in shapes for _ in range(4)]
        + [jax.ShapeDtypeStruct((SUBLANES, LANES), F32)],
        compiler_params=_params(),
    )(gathered, *ws, *ms, *vs)


def _step(x, mem, target, shards, sm, core):
    g, w, sums, reduced = {}, {}, {}, {}

    def gather(names):
        return _gather_side(names, [shards[n] for n in names])

    def to_sibling(names):
        return _sibling_side(names, [g[n] for n in names])

    def add_sibling(names, received):
        for n, r in zip(names, received):
            sums[n] = _reduce_add(g[n], r, BIG[BIG_INDEX[n]][2], core, "reduce_add_" + n)

    def to_chips(names):
        return _chips_side([sums[n][1] for n in names])

    def keep(names, received):
        for n, r in zip(names, received):
            reduced[n] = (sums[n][0], r)

    row = lambda a: a.reshape(1, -1)
    g_mix, g_xa, g_mlp, g_mem = row(sm["g_mix"]), row(sm["g_xa"]), row(sm["g_mlp"]), row(sm["g_mem"])
    g_os, g_ob = row(sm["g_out_ssm"]), row(sm["g_out_sb"])
    sb_gq, sb_gk = jnp.tile(row(sm["sb_g_q"]), (1, SB_HEADS)), jnp.tile(row(sm["sb_g_k"]), (1, SB_HEADS))
    xa_gq, xa_gk = row(sm["xa_g_q"]), row(sm["xa_g_k"])
    d_skip = row(sm["ssm_d"])

    h1, (w["w_in"],) = _norm_fwd(x, g_mix, "norm_mix", side=_gather_two_level_side("w_in", shards["w_in"]))
    proj = _mm(h1, w["w_in"], "nn", "in_proj", tn=IN_WIDTH)
    u = _to_segments(proj[:, :SSM_WIDTH])
    q_raw, k_raw = (proj, SB_WIDTH, 1), (proj, SB_WIDTH, 2)
    v_col = (SSM_WIDTH + 2 * SB_WIDTH) // LANES
    sb_scale = SB_HEAD_DIM ** -0.5
    qs, ks = _rw(lambda qt, kt, gq, gk: ((_rms_groups(qt, gq, sb_scale), _rms_groups(kt, gk, 1.0)), ()),
                 [q_raw, k_raw], [sb_gq, sb_gk], [(SB_WIDTH, BF16)] * 2, [], "sb_qk_norm")
    early = ["ssm_w_glu", "w_out", "xa_w_q", "xa_w_kv", "xa_w_o", "w_up"]
    y_sb, got = _sb_fwd(qs, ks, proj, "sb_fwd", v_col=v_col, side=gather(early))
    w.update(zip(early, got))

    ssm_args = _states_on_lanes(sm)
    acat, bsup, csup = _ssm_mats_fwd(*ssm_args, "ssm_mats")
    (states, y0, y1), (w["w_down"],) = _ssm_fwd(u, acat, bsup, csup, d_skip, "ssm_fwd", side=gather(["w_down"]))
    z_glu, y_ssm = _mm(y1, w["ssm_w_glu"], "nn", "ssm_glu", epi=lambda r, yt: (r, yt * jax.nn.sigmoid(r)),
                       extras=(y1,), out_dtypes=(F32, F32))
    y_ssm = _from_segments(y_ssm)

    def cat_norm(a, b, ga, gb):
        return jnp.concatenate([_rms(a, ga), _rms(b, gb)], axis=1)

    ycat = _rw(lambda a, b, ga, gb: ((cat_norm(a, b, ga, gb),), ()), [y_ssm, y_sb], [g_os, g_ob],
               [(D_MODEL, BF16)], [], "norm_out")[0]

    def residual_norm_epi(r, xt, gt):
        xn = r + xt
        return xn, _rms(xn, gt)

    x1, h2 = _mm(ycat, w["w_out"], "nn", "out_proj", epi=residual_norm_epi, extras=(x,), fulls=(g_xa,),
                 out_dtypes=(F32, BF16))
    qx = _mm(h2, w["xa_w_q"], "nn", "xa_q")
    memn, kv, kn_x, vv_x = _mem_fwd(mem, g_mem, w["xa_w_kv"], xa_gk, "xa_mem")
    o_xa = _xa_fwd(qx, xa_gq, kn_x, vv_x, "xa_fwd")
    x2, h3 = _mm(o_xa, w["xa_w_o"], "nn", "xa_o", epi=residual_norm_epi, extras=(x1,), fulls=(g_mlp,),
                 out_dtypes=(F32, BF16))

    def up_epi(r):
        rl = jnp.maximum(r, 0.0)
        return (rl * rl,)

    r_up = _mm(h3, w["w_up"], "nn", "mlp_up", epi=up_epi, out_dtypes=(BF16,), tn=2048)

    def loss_epi(r, xt, tt):
        d = r + xt - tt
        return (d * (1.0 / D_MODEL),) * 2, (jnp.sum(d * d, axis=0, keepdims=True),)

    dx3, dx3_b, sq = _mm(r_up, w["w_down"], "nn", "mlp_down", epi=loss_epi, extras=(x2, target),
                         out_dtypes=(F32, BF16), sums=[(1, D_MODEL)])
    loss = jnp.sum(sq) * (0.5 / D_MODEL)

    def norm_bwd_epi(r, xt, drt, gt):
        _, vjp = jax.vjp(_rms, xt, gt)
        dx_, dg_ = vjp(r)
        return (dx_ + drt,) * 2, (dg_,)

    g["w_down"] = _mm(r_up, dx3_b, "tn", "d_w_down", tk=2048)
    da = _mm(dx3_b, w["w_down"], "nt", "d_r", epi=lambda r, rt: (r * 2.0 * jnp.sqrt(rt.astype(F32)),), extras=(r_up,),
             out_dtypes=(BF16,))
    g["w_up"] = _mm(h3, da, "tn", "d_w_up", tk=2048)
    mlp = ["w_down", "w_up"]
    (dx2, dx2_b, g["g_mlp"]), got = _mm(da, w["w_up"], "nt", "d_h3", epi=norm_bwd_epi, extras=(x2, dx3),
                                        fulls=(g_mlp,), out_dtypes=(F32, BF16), sums=[g_mlp.shape],
                                        side=to_sibling(mlp))
    add_sibling(mlp, got)
    g["xa_w_o"] = _mm(o_xa, dx2_b, "tn", "d_xa_w_o", tk=2048)
    do_xa = _mm(dx2_b, w["xa_w_o"], "nt", "d_o_xa")
    dqx, dkn_x, dvv_x, g["xa_g_q"] = _xa_bwd(qx, xa_gq, kn_x, vv_x, do_xa, "xa_bwd")
    g["xa_w_kv"], g["g_mem"], g["xa_g_k"] = _mem_bwd(mem, g_mem, memn, w["xa_w_kv"], kv, xa_gk, dkn_x, dvv_x,
                                                     "xa_mem_bwd")
    g["xa_w_q"] = _mm(h2, dqx, "tn", "d_xa_w_q", tk=2048)
    dx1, dx1_b, g["g_xa"] = _mm(dqx, w["xa_w_q"], "nt", "d_h2", epi=norm_bwd_epi, extras=(x1, dx2), fulls=(g_xa,),
                                out_dtypes=(F32, BF16), sums=[g_xa.shape])
    g["w_out"] = _mm(ycat, dx1_b, "tn", "d_w_out", tk=2048)
    dycat = _mm(dx1_b, w["w_out"], "nt", "d_ycat")

    def cat_bwd(a, b, dy, ga, gb):
        _, vjp = jax.vjp(cat_norm, a, b, ga, gb)
        da_, db_, dga, dgb = vjp(dy)
        return (da_, db_), (dga, dgb)

    dy_ssm, dy_sb, g["g_out_ssm"], g["g_out_sb"] = _rw(
        cat_bwd, [y_ssm, y_sb, dycat], [g_os, g_ob], [(SSM_WIDTH, F32), (SB_WIDTH, F32)], [g_os.shape, g_ob.shape],
        "d_norm_out", tm=512)

    def glu_bwd(dy, yt, zt):
        sg = jax.nn.sigmoid(zt)
        return (dy * sg, dy * yt * sg * (1.0 - sg)), ()

    dy1_a, dz = _rw(glu_bwd, [_to_segments(dy_ssm), y1, z_glu], [], [(SSM_WIDTH, F32), (SSM_WIDTH, BF16)], [], "d_glu")
    g["ssm_w_glu"] = _mm(y1, dz, "tn", "d_w_glu", tk=2048)

    def gelu_bwd_epi(r, da_, y0t):
        _, vjp = jax.vjp(jax.nn.gelu, y0t)
        return (vjp(r + da_)[0],)

    mid = ["w_out", "xa_w_q", "xa_w_kv", "xa_w_o", "ssm_w_glu"]
    dy0, got = _mm(dz, w["ssm_w_glu"], "nt", "d_y1", epi=gelu_bwd_epi, extras=(dy1_a, y0), side=to_sibling(mid))
    add_sibling(mid, got)
    (du, da8, d_bsup, d_csup, g["ssm_d"]), got = _ssm_bwd(dy0, states, u, acat, bsup, csup, d_skip, "ssm_bwd",
                                                          side=to_chips(mlp))
    keep(mlp, got)
    d_acat = jnp.sum(da8, axis=0, keepdims=True)
    d_mats = _ssm_mats_bwd(*ssm_args[:5], d_acat, d_bsup, d_csup, "ssm_mats_bwd")
    for nm, val in zip(("ssm_a_re", "ssm_a_im", "ssm_log_dt", "ssm_b_re", "ssm_b_im", "ssm_c_re", "ssm_c_im"),
                       _from_states_on_lanes(*d_mats)):
        g[nm] = val

    (dqs, dks, dvs), got = _sb_bwd(qs, ks, proj, y_sb, dy_sb, "sb_bwd", v_col=v_col, side=to_chips(mid))
    keep(mid, got)

    def d_proj_rows(du_t, qt, dqt, kt, dkt, dvt, gq, gk):
        _, vjp_q = jax.vjp(lambda a, b_: _rms_groups(a, b_, sb_scale), qt, gq)
        _, vjp_k = jax.vjp(lambda a, b_: _rms_groups(a, b_, 1.0), kt, gk)
        (dq_, dgq_), (dk_, dgk_) = vjp_q(dqt), vjp_k(dkt)
        rows = jnp.concatenate([du_t, dq_.astype(BF16), dk_.astype(BF16), dvt.astype(BF16)], axis=1)
        return (rows,), (dgq_, dgk_)

    dproj, dgq, dgk = _rw(d_proj_rows, [_from_segments(du), q_raw, dqs, k_raw, dks, dvs], [sb_gq, sb_gk],
                          [(IN_WIDTH, BF16)], [sb_gq.shape, sb_gk.shape], "d_proj", tm=512)
    g["sb_g_q"] = jnp.sum(dgq.reshape(SB_HEADS, SB_HEAD_DIM), axis=0)
    g["sb_g_k"] = jnp.sum(dgk.reshape(SB_HEADS, SB_HEAD_DIM), axis=0)
    g["w_in"] = _mm(h1, dproj, "tn", "d_w_in", tk=2048)
    dh1, got = _mm(dproj, w["w_in"], "nt", "d_h1", side=to_sibling(["w_in"]))
    add_sibling(["w_in"], got)
    dx, g["g_mix"] = _norm_bwd(x, g_mix, dh1, dx1, "d_norm_mix")

    packed = _pack_small([g[n] for n in SMALL] + [loss.reshape(1)])
    everyone, got = _all_gather(packed, "gather_small", to_chips(["w_in"]))
    keep(["w_in"], got)
    return dx, everyone, reduced


def kernel(x, mem, g_mix, w_in, ssm_a_re, ssm_a_im, ssm_log_dt, ssm_b_re, ssm_b_im, ssm_c_re, ssm_c_im, ssm_d, ssm_w_glu, sb_g_q, sb_g_k, g_out_ssm, g_out_sb, w_out, g_xa, g_mem, xa_w_q, xa_w_kv, xa_g_q, xa_g_k, xa_w_o, g_mlp, w_up, w_down, loss_target, m_g_mix, m_w_in, m_ssm_a_re, m_ssm_a_im, m_ssm_log_dt, m_ssm_b_re, m_ssm_b_im, m_ssm_c_re, m_ssm_c_im, m_ssm_d, m_ssm_w_glu, m_sb_g_q, m_sb_g_k, m_g_out_ssm, m_g_out_sb, m_w_out, m_g_xa, m_g_mem, m_xa_w_q, m_xa_w_kv, m_xa_g_q, m_xa_g_k, m_xa_w_o, m_g_mlp, m_w_up, m_w_down, v_g_mix, v_w_in, v_ssm_a_re, v_ssm_a_im, v_ssm_log_dt, v_ssm_b_re, v_ssm_b_im, v_ssm_c_re, v_ssm_c_im, v_ssm_d, v_ssm_w_glu, v_sb_g_q, v_sb_g_k, v_g_out_ssm, v_g_out_sb, v_w_out, v_g_xa, v_g_mem, v_xa_w_q, v_xa_w_kv, v_xa_g_q, v_xa_g_k, v_xa_w_o, v_g_mlp, v_w_up, v_w_down):
    given = dict(locals())
    order = ["g_mix", "w_in", "ssm_a_re", "ssm_a_im", "ssm_log_dt", "ssm_b_re", "ssm_b_im", "ssm_c_re", "ssm_c_im",
             "ssm_d", "ssm_w_glu", "sb_g_q", "sb_g_k", "g_out_ssm", "g_out_sb", "w_out", "g_xa", "g_mem", "xa_w_q",
             "xa_w_kv", "xa_g_q", "xa_g_k", "xa_w_o", "g_mlp", "w_up", "w_down"]
    assert sorted([n for n, _, _ in BIG] + SMALL) == sorted(order)
    core = lax.axis_index("c").astype(jnp.int32).reshape(1)
    chip = (2 * lax.axis_index("x") + lax.axis_index("y")).astype(jnp.int32).reshape(1)

    shards = {n: given[n][0].astype(BF16) for n, _, _ in BIG}
    sm = {n: given[n][0] for n in SMALL}
    dx, everyone, reduced = _step(x[0], mem[0], loss_target[0], shards, sm, core)

    res = {}
    for n, _, _ in BIG:
        own, recv = reduced[n]
        outs = _adam_sharded(own, recv, given[n][0], given["m_" + n][0], given["v_" + n][0], chip, "adam_" + n)
        for kind, val in zip(("grad", "delta", "new_m", "new_v"), outs):
            res[kind + "_" + n] = val[None]

    sizes = [math.prod(sm[n].shape) for n in SMALL] + [1]
    nat = lambda a: a.reshape(_natural_2d(math.prod(a.shape)))
    outs = _adam_replicated(everyone, sizes, [nat(sm[n]) for n in SMALL], [nat(given["m_" + n][0]) for n in SMALL],
                            [nat(given["v_" + n][0]) for n in SMALL], "adam_replicated")
    for i, n in enumerate(SMALL):
        for kind, val in zip(("grad", "delta", "new_m", "new_v"), outs[4 * i:4 * i + 4]):
            res[kind + "_" + n] = val.reshape(given[n].shape)
    loss_out = outs[-1][0, 0]
    return (loss_out, dx[None], *[res["grad_" + n] for n in order], *[res["delta_" + n] for n in order],
            *[res["new_m_" + n] for n in order], *[res["new_v_" + n] for n in order])
```

```python
import functools
import math

import jax
import jax.numpy as jnp
from jax import lax
from jax.experimental import pallas as pl
from jax.experimental.pallas import tpu as pltpu

F32 = jnp.float32
BF16 = jnp.bfloat16
MESH = pl.DeviceIdType.MESH

N_DEV = 8
D_MODEL = 1024
SSM_WIDTH = 512
SSM_GROUP = 16
SSM_GROUPS = 32
SSM_STATE = 64
N_STATE = SSM_GROUPS * SSM_STATE
SB_HEADS = 8
SB_HEAD_DIM = 64
SB_WIDTH = 512
IN_WIDTH = 2048
XA_HEADS = 4
XA_HEAD_DIM = 128
XA_WIDTH = 512
D_FF = 4096
NORM_EPS = 1e-6
ADAM_LR = 0.001
ADAM_B1 = 0.9
ADAM_B2 = 0.999
ADAM_EPS = 1e-08
ADAM_WD = 0.01
ADAM_STEP = 10

LANES = 128
SUBLANES = 8
VMEM_LIMIT = 56 * 1024 * 1024
SCAN_LANES = 512
SB_BLOCK = 256
SB_Q_BLOCKS = 4
SB_UNDERFLOW = -110.0

NN = (((1,), (0,)), ((), ()))
NT = (((1,), (1,)), ((), ()))
TN = (((0,), (0,)), ((), ()))


def _params(sem=None):
    return pltpu.CompilerParams(dimension_semantics=sem, vmem_limit_bytes=VMEM_LIMIT)


def _dot(a, b, dims=NN):
    return lax.dot_general(a.astype(BF16), b.astype(BF16), dims, preferred_element_type=F32)


def _rms(x, g):
    return x * lax.rsqrt(jnp.mean(x * x, axis=-1, keepdims=True) + NORM_EPS) * g


ANY = pl.BlockSpec(memory_space=pl.ANY)


class _Side:
    def __init__(self, ins, out_shapes, n_sem, make, finish=None):
        self.ins, self.out_shapes, self.n_sem, self.make = list(ins), list(out_shapes), n_sem, make
        self.finish = finish

    def sems(self):
        return [pltpu.SemaphoreType.DMA((self.n_sem,)), pltpu.SemaphoreType.DMA((self.n_sem,))]


def _hosted(body, side, n_in, n_out, grid):
    if side is None:
        return body
    ns_in, ns_out = len(side.ins), len(side.out_shapes)

    def wrapped(*refs):
        ins, refs = refs[:n_in], refs[n_in:]
        s_ins, refs = refs[:ns_in], refs[ns_in:]
        outs, refs = refs[:n_out], refs[n_out:]
        s_outs, refs = refs[:ns_out], refs[ns_out:]
        scratch, sems = refs[:-2], refs[-2:]
        ids = [pl.program_id(d) for d in range(len(grid))]
        first = functools.reduce(jnp.logical_and, [i == 0 for i in ids])
        last = functools.reduce(jnp.logical_and, [i == n - 1 for i, n in zip(ids, grid)])

        @pl.when(first)
        def _():
            for cp in side.make(s_ins, s_outs, *sems):
                cp.start()

        body(*ins, *outs, *scratch)

        @pl.when(last)
        def _():
            if side.finish is not None:
                side.finish(s_ins, s_outs, *sems)
            else:
                for cp in side.make(s_ins, s_outs, *sems):
                    cp.wait()

    return wrapped


def _side_args(side):
    if side is None:
        return [], [], [], [], []
    return ([ANY] * len(side.ins), [ANY] * len(side.out_shapes), side.out_shapes, side.sems(), side.ins)


def _split_side(res, n_out, side):
    res = list(res)
    main = res[0] if n_out == 1 else res[:n_out]
    return main if side is None else (main, res[n_out:])


def _mm(a, b, mode, name, *, epi=None, extras=(), fulls=(), out_dtypes=(F32,), sums=(), tm=1024, tn=1024, tk=1024,
        side=None):
    if mode == "nn":
        (m, k), (k2, n) = a.shape, b.shape
    elif mode == "nt":
        (m, k), (n, k2) = a.shape, b.shape
    else:
        (k, m), (k2, n) = a.shape, b.shape
    assert k == k2, (name, a.shape, b.shape)
    tm, tn, tk = min(tm, m), min(tn, n), min(tk, k)
    assert m % tm == 0 and n % tn == 0 and k % tk == 0, (name, m, n, k)
    nk = k // tk
    dims = {"nn": NN, "nt": NT, "tn": TN}[mode]
    if mode == "tn":
        a_spec = pl.BlockSpec((tk, tm), lambda i, j, kk: (kk, i))
    else:
        a_spec = pl.BlockSpec((tm, tk), lambda i, j, kk: (i, kk))
    if mode == "nt":
        b_spec = pl.BlockSpec((tn, tk), lambda i, j, kk: (j, kk))
    else:
        b_spec = pl.BlockSpec((tk, tn), lambda i, j, kk: (kk, j))
    mn_spec = pl.BlockSpec((tm, tn), lambda i, j, kk: (i, j))
    n_ex, n_full, n_out, n_sum = len(extras), len(fulls), len(out_dtypes), len(sums)
    n_in = 2 + n_ex + n_full

    def body(*refs):
        a_ref, b_ref = refs[:2]
        ex = refs[2:n_in]
        outs = refs[n_in:n_in + n_out]
        sum_refs = refs[n_in + n_out:n_in + n_out + n_sum]
        kk = pl.program_id(2)
        first_tile = jnp.logical_and(pl.program_id(0) == 0, pl.program_id(1) == 0)

        def finish(r):
            vals = epi(r, *[e[...] for e in ex]) if epi is not None else (r,)
            if n_sum:
                vals, parts = vals

                @pl.when(first_tile)
                def _():
                    for sr in sum_refs:
                        sr[...] = jnp.zeros_like(sr)

                for sr, p in zip(sum_refs, parts):
                    sr[...] += p
            for o, v in zip(outs, vals):
                o[...] = v.astype(o.dtype)

        if nk == 1:
            finish(_dot(a_ref[...], b_ref[...], dims))
        else:
            acc = refs[n_in + n_out + n_sum]

            @pl.when(kk == 0)
            def _():
                acc[...] = jnp.zeros_like(acc)

            acc[...] += _dot(a_ref[...], b_ref[...], dims)

            @pl.when(kk == nk - 1)
            def _():
                finish(acc[...])

    grid = (m // tm, n // tn, nk)
    whole = lambda shape: pl.BlockSpec(shape, lambda i, j, kk: (0,) * len(shape))
    s_in, s_out, s_shape, s_scratch, s_ops = _side_args(side)
    seq = bool(side) or n_sum > 0
    res = pl.pallas_call(
        _hosted(body, side, n_in, n_out + n_sum, grid), name=name, grid=grid,
        in_specs=[a_spec, b_spec] + [mn_spec] * n_ex + [whole(f.shape) for f in fulls] + s_in,
        out_specs=[mn_spec] * n_out + [whole(shape) for shape in sums] + s_out,
        out_shape=[jax.ShapeDtypeStruct((m, n), dt) for dt in out_dtypes]
        + [jax.ShapeDtypeStruct(shape, F32) for shape in sums] + s_shape,
        scratch_shapes=([pltpu.VMEM((tm, tn), F32)] if nk > 1 else []) + s_scratch,
        compiler_params=_params(("arbitrary",) * 3 if seq else ("parallel", "parallel", "arbitrary")),
    )(a, b, *extras, *fulls, *s_ops)
    return _split_side(res, n_out + n_sum, side)


def _row_tile(s, target):
    if s <= target:
        return s
    return max(t for t in range(16, target + 1, 16) if s % t == 0)


def _rw(fn, rows, fulls, row_out, acc_out, name, tm=1024, side=None):
    cols = [r[1:] if isinstance(r, tuple) else (r.shape[1], 0) for r in rows]
    rows = [r[0] if isinstance(r, tuple) else r for r in rows]
    s = rows[0].shape[0]
    tm = _row_tile(s, tm)
    nr, nf, nro, nao = len(rows), len(fulls), len(row_out), len(acc_out)

    def body(*refs):
        r = refs[:nr]
        f = refs[nr:nr + nf]
        ro = refs[nr + nf:nr + nf + nro]
        ao = refs[nr + nf + nro:]
        outs, accs = fn(*[x[...] for x in r], *[x[...] for x in f])
        for o, v in zip(ro, outs):
            o[...] = v.astype(o.dtype)
        if nao:
            @pl.when(pl.program_id(0) == 0)
            def _():
                for a in ao:
                    a[...] = jnp.zeros_like(a)

            for a, v in zip(ao, accs):
                a[...] += v

    full_spec = lambda shape: pl.BlockSpec(shape, lambda i: (0,) * len(shape))
    s_in, s_out, s_shape, s_scratch, s_ops = _side_args(side)
    res = pl.pallas_call(
        _hosted(body, side, nr + nf, nro + nao, (s // tm,)), name=name, grid=(s // tm,),
        in_specs=[pl.BlockSpec((tm, wd), functools.partial(lambda i, cb: (i, cb), cb=cb)) for wd, cb in cols]
        + [full_spec(x.shape) for x in fulls] + s_in,
        out_specs=[pl.BlockSpec((tm, d), lambda i: (i, 0)) for d, _ in row_out]
        + [full_spec(shape) for shape in acc_out] + s_out,
        out_shape=[jax.ShapeDtypeStruct((s, d), dt) for d, dt in row_out]
        + [jax.ShapeDtypeStruct(shape, F32) for shape in acc_out] + s_shape,
        scratch_shapes=s_scratch,
        compiler_params=_params(("arbitrary",)),
    )(*rows, *fulls, *s_ops)
    res = list(res)
    return res if side is None else (res[:nro + nao], res[nro + nao:])


def _norm_fwd(x, g, name, side=None):
    res = _rw(lambda xt, gt: ((_rms(xt, gt),), ()), [x], [g], [(x.shape[1], BF16)], [], name, side=side)
    return res[0] if side is None else (res[0][0], res[1])


def _norm_bwd(x, g, dh, dres, name, side=None):
    def fn(xt, dht, drt, gt):
        _, vjp = jax.vjp(_rms, xt, gt)
        dx, dg = vjp(dht)
        return (dx + drt,), (dg,)

    return _rw(fn, [x, dh, dres], [g], [(x.shape[1], F32)], [g.shape], name, side=side)


def _rms_groups(x, g, scale):
    lo = lax.broadcasted_iota(jnp.int32, (1, LANES), 1) < SB_HEAD_DIM
    x2 = x * x
    outs = []
    for cb in range(x.shape[1] // LANES):
        sl = slice(cb * LANES, (cb + 1) * LANES)
        s_lo = jnp.sum(jnp.where(lo, x2[:, sl], 0.0), axis=-1, keepdims=True)
        s_hi = jnp.sum(jnp.where(lo, 0.0, x2[:, sl]), axis=-1, keepdims=True)
        r = jnp.where(lo, lax.rsqrt(s_lo * (1.0 / SB_HEAD_DIM) + NORM_EPS),
                      lax.rsqrt(s_hi * (1.0 / SB_HEAD_DIM) + NORM_EPS))
        outs.append(x[:, sl] * r)
    return jnp.concatenate(outs, axis=1) * g * scale


def _log_sigmoid(z):
    return jnp.minimum(z, 0.0) - jnp.log(1.0 + jnp.exp(-jnp.abs(z)))


def _split_dot(x, u2):
    hi = x.astype(BF16)
    lo = (x - hi.astype(F32)).astype(BF16)
    return jnp.dot(jnp.concatenate([hi, lo], axis=1), u2, preferred_element_type=F32)


def _sb_consts(b):
    row = lax.broadcasted_iota(jnp.int32, (b, b), 0)
    col = lax.broadcasted_iota(jnp.int32, (b, b), 1)
    tri = col < row
    u_after = (row > col).astype(BF16)
    u_from = (row >= col).astype(BF16)
    stack = lambda u: jnp.concatenate([u, u], axis=0)
    lane_lo = lax.broadcasted_iota(jnp.int32, (b, LANES), 1) < SB_HEAD_DIM
    return tri, stack(u_after), stack(u_from), lane_lo


def _sb_scores(qh, kb, a_run, keep, u2_after, mask_l=True):
    z = lax.dot_general(qh, kb, NT, preferred_element_type=F32)
    lb = _log_sigmoid(z)
    l = lb - z
    if keep is not None and mask_l:
        l = jnp.where(keep, l, 0.0)
    w = jnp.exp(lb + (a_run + _split_dot(l, u2_after)))
    if keep is not None:
        w = jnp.where(keep, w, 0.0)
    return lb, l, w


def _sb_walk(qi, carry, step):
    def cond(state):
        n, c = state
        return jnp.logical_and(n <= qi, jnp.max(jnp.maximum(c[0], c[1])) > SB_UNDERFLOW)

    def body(state):
        n, c = state
        return n + 1, step(n, c)

    return lax.while_loop(cond, body, (jnp.int32(2), carry))[1]


def _two_heads(x, lane_lo):
    zero = jnp.zeros_like(x)
    return jnp.where(lane_lo, x, zero), jnp.where(lane_lo, zero, x)


def _sb_fwd(qs, ks, v, name, v_col=0, side=None):
    s, width = qs.shape
    b = min(SB_BLOCK, s)
    nqb = min(SB_Q_BLOCKS, s // b)

    def body(q_ref, k_ref, v_ref, o_ref):
        tri, u2_after, _, lane_lo = _sb_consts(b)
        zero = jnp.zeros((b, 1), F32)
        started = []
        for h in range(nqb):
            qi = pl.program_id(1) * nqb + h
            q_a, q_b = _two_heads(q_ref[h * b:(h + 1) * b, :], lane_lo)

            def step(n, carry, keep, mask_l=True, qi=qi, q_a=q_a, q_b=q_b):
                a_a, a_b, acc = carry
                off = pl.multiple_of(jnp.maximum(qi - n, 0) * b, b)
                kb = k_ref[pl.ds(off, b), :]
                v_a, v_b = _two_heads(v_ref[pl.ds(off, b), :].astype(BF16), lane_lo)
                _, l_a, w_a = _sb_scores(q_a, kb, a_a, keep, u2_after, mask_l)
                _, l_b, w_b = _sb_scores(q_b, kb, a_b, keep, u2_after, mask_l)
                acc = acc + jnp.dot(jnp.concatenate([w_a.astype(BF16), w_b.astype(BF16)], axis=1),
                                    jnp.concatenate([v_a, v_b], axis=0), preferred_element_type=F32)
                return (a_a + jnp.sum(l_a, axis=1, keepdims=True), a_b + jnp.sum(l_b, axis=1, keepdims=True), acc)

            carry = step(0, (zero, zero, jnp.zeros((b, LANES), F32)), tri)
            carry = step(1, carry, jnp.broadcast_to(qi > 0, tri.shape), mask_l=False)
            started.append((qi, step, carry))
        for h, (qi, step, carry) in enumerate(started):
            carry = _sb_walk(qi, carry, lambda n, c, step=step: step(n, c, None))
            o_ref[h * b:(h + 1) * b, :] = carry[2]

    blk = pl.BlockSpec((nqb * b, LANES), lambda hp, i: (i, hp))
    full = pl.BlockSpec((s, LANES), lambda hp, i: (0, hp))
    full_v = pl.BlockSpec((s, LANES), lambda hp, i: (0, hp + v_col))
    grid = (width // LANES, s // (nqb * b))
    s_in, s_out, s_shape, s_scratch, s_ops = _side_args(side)
    res = pl.pallas_call(
        _hosted(body, side, 3, 1, grid), name=name, grid=grid,
        in_specs=[blk, full, full_v] + s_in, out_specs=[blk] + s_out,
        out_shape=[jax.ShapeDtypeStruct((s, width), F32)] + s_shape, scratch_shapes=s_scratch,
        compiler_params=_params(("arbitrary", "arbitrary")),
    )(qs, ks, v, *s_ops)
    return _split_side(res, 1, side)


def _sb_bwd(qs, ks, v, out, dout, name, v_col=0, side=None):
    s, width = qs.shape
    b = min(SB_BLOCK, s)
    nqb = min(SB_Q_BLOCKS, s // b)

    def body(q_ref, k_ref, v_ref, o_ref, do_ref, dq_ref, dk_ref, dv_ref):
        @pl.when(pl.program_id(1) == 0)
        def _():
            dk_ref[...] = jnp.zeros_like(dk_ref)
            dv_ref[...] = jnp.zeros_like(dv_ref)

        tri, u2_after, u2_from, lane_lo = _sb_consts(b)
        zero = jnp.zeros((b, 1), F32)

        def head(qh, doh, kb, vb, a_run, d_rem, keep, mask_l):
            lb, l, w = _sb_scores(qh, kb, a_run, keep, u2_after, mask_l)
            wb = w.astype(BF16)
            g = lax.dot_general(doh, vb, NT, preferred_element_type=F32) * wb.astype(F32)
            g_before = d_rem - _split_dot(g, u2_from)
            dz = g - (g + g_before) * jnp.exp(lb)
            if keep is not None:
                dz = jnp.where(keep, dz, 0.0)
            return (dz.astype(BF16), wb, a_run + jnp.sum(l, axis=1, keepdims=True),
                    d_rem - jnp.sum(g, axis=1, keepdims=True))

        started = []
        for h in range(nqb):
            qi = pl.program_id(1) * nqb + h
            rows = slice(h * b, (h + 1) * b)
            q_a, q_b = _two_heads(q_ref[rows, :], lane_lo)
            dob = do_ref[rows, :].astype(BF16)
            do_a, do_b = _two_heads(dob, lane_lo)
            prod = dob.astype(F32) * o_ref[rows, :]
            d_a = jnp.sum(jnp.where(lane_lo, prod, 0.0), axis=1, keepdims=True)
            d_b = jnp.sum(jnp.where(lane_lo, 0.0, prod), axis=1, keepdims=True)
            q_rows = jnp.concatenate([q_a, q_b], axis=0)
            do_rows = jnp.concatenate([do_a, do_b], axis=0)

            def step(n, carry, keep, mask_l=True, qi=qi, q_a=q_a, q_b=q_b, do_a=do_a, do_b=do_b, q_rows=q_rows,
                     do_rows=do_rows):
                a_a, a_b, r_a, r_b, dq = carry
                off = pl.multiple_of(jnp.maximum(qi - n, 0) * b, b)
                kb = k_ref[pl.ds(off, b), :]
                vb = v_ref[pl.ds(off, b), :].astype(BF16)
                k_a, k_b = _two_heads(kb, lane_lo)
                dz_a, w_a, a_a, r_a = head(q_a, do_a, kb, vb, a_a, r_a, keep, mask_l)
                dz_b, w_b, a_b, r_b = head(q_b, do_b, kb, vb, a_b, r_b, keep, mask_l)
                dq = dq + jnp.dot(jnp.concatenate([dz_a, dz_b], axis=1), jnp.concatenate([k_a, k_b], axis=0),
                                  preferred_element_type=F32)
                dk_ref[pl.ds(off, b), :] += lax.dot_general(jnp.concatenate([dz_a, dz_b], axis=0), q_rows, TN,
                                                            preferred_element_type=F32)
                dv_ref[pl.ds(off, b), :] += lax.dot_general(jnp.concatenate([w_a, w_b], axis=0), do_rows, TN,
                                                            preferred_element_type=F32)
                return a_a, a_b, r_a, r_b, dq

            carry = step(0, (zero, zero, d_a, d_b, jnp.zeros((b, LANES), F32)), tri)
            carry = step(1, carry, jnp.broadcast_to(qi > 0, tri.shape), mask_l=False)
            started.append((qi, step, carry))
        for h, (qi, step, carry) in enumerate(started):
            carry = _sb_walk(qi, carry, lambda n, c, step=step: step(n, c, None))
            dq_ref[h * b:(h + 1) * b, :] = carry[4]

    blk = pl.BlockSpec((nqb * b, LANES), lambda hp, i: (i, hp))
    full = pl.BlockSpec((s, LANES), lambda hp, i: (0, hp))
    full_v = pl.BlockSpec((s, LANES), lambda hp, i: (0, hp + v_col))
    grid = (width // LANES, s // (nqb * b))
    s_in, s_out, s_shape, s_scratch, s_ops = _side_args(side)
    res = pl.pallas_call(
        _hosted(body, side, 5, 3, grid), name=name, grid=grid,
        in_specs=[blk, full, full_v, blk, blk] + s_in, out_specs=[blk, full, full] + s_out,
        out_shape=[jax.ShapeDtypeStruct((s, width), F32)] * 3 + s_shape,
        scratch_shapes=s_scratch,
        compiler_params=_params(("arbitrary", "arbitrary")),
    )(qs, ks, v, out, dout, *s_ops)
    return _split_side(res, 3, side)


def _cmul(xr, xi, yr, yi):
    return xr * yr - xi * yi, xr * yi + xi * yr


def _scan_consts(ar, ai, reverse, lc):
    rowi = lax.broadcasted_iota(jnp.int32, (SUBLANES, lc), 0)
    pows = [(ar, ai)]
    for _ in range(SUBLANES - 1):
        pows.append(_cmul(*pows[-1], ar, ai))
    steps = []
    for d in (1, 2, 4):
        keep = (rowi < SUBLANES - d) if reverse else (rowi >= d)
        pr, pi = pows[d - 1]
        steps.append((SUBLANES - d if reverse else d, jnp.where(keep, pr, 0.0), jnp.where(keep, pi, 0.0)))
    cr = jnp.zeros((SUBLANES, lc), F32)
    ci = jnp.zeros((SUBLANES, lc), F32)
    for r in range(SUBLANES):
        pr, pi = pows[SUBLANES - 1 - r] if reverse else pows[r]
        cr = jnp.where(rowi == r, pr, cr)
        ci = jnp.where(rowi == r, pi, ci)
    return steps, cr, ci


def _scan_tile(xr, xi, steps, pr, pi, cr, ci):
    for shift, ar, ai in steps:
        rr = pltpu.roll(xr, shift, 0)
        ri = pltpu.roll(xi, shift, 0)
        xr, xi = xr + ar * rr - ai * ri, xi + ar * ri + ai * rr
    return xr + pr * cr - pi * ci, xi + pr * ci + pi * cr


SCAN_ROWS = 1024


def _scan_chunk(s):
    tt = min(SCAN_ROWS, s)
    seg = tt // SUBLANES
    assert s % tt == 0 and seg % SUBLANES == 0 and seg & (seg - 1) == 0, s
    return tt, seg


def _to_segments(a):
    s, wd = a.shape
    tt, seg = _scan_chunk(s)
    return jnp.transpose(a.reshape(s // tt, SUBLANES, seg, wd), (0, 2, 1, 3)).reshape(s, wd)


def _from_segments(a):
    s, wd = a.shape
    tt, seg = _scan_chunk(s)
    return jnp.transpose(a.reshape(s // tt, seg, SUBLANES, wd), (0, 2, 1, 3)).reshape(s, wd)


def _cpow2(xr, xi, k):
    for _ in range(k):
        xr, xi = _cmul(xr, xi, xr, xi)
    return xr, xi


def _fill_powers(pw_ref, ar, ai, seg, lc):
    _, p8r, p8i = _scan_consts(ar, ai, False, lc)
    a8r, a8i = _cpow2(ar, ai, 3)
    qr, qi = jnp.ones_like(ar), jnp.zeros_like(ai)
    for k in range(seg // SUBLANES):
        tr, ti = _cmul(p8r, p8i, qr, qi)
        for r in range(SUBLANES):
            rows = pl.ds((SUBLANES * k + r) * SUBLANES, SUBLANES)
            pw_ref[rows, :lc] = jnp.broadcast_to(tr[r:r + 1, :], (SUBLANES, lc))
            pw_ref[rows, lc:] = jnp.broadcast_to(ti[r:r + 1, :], (SUBLANES, lc))
        qr, qi = _cmul(qr, qi, a8r, a8i)


def _ssm_fwd(u, acat, bsup, csup, d_skip, name, side=None):
    s = u.shape[0]
    lc = SCAN_LANES
    tt, seg = _scan_chunk(s)
    nl, nt = N_STATE // lc, s // tt
    tile = lambda j: pl.ds(pl.multiple_of(j * SUBLANES, SUBLANES), SUBLANES)

    def body(u_ref, a_ref, b_ref, c_ref, d_ref, s_ref, y0_ref, y1_ref, carry, pw_ref):
        ar, ai = a_ref[:, :lc], a_ref[:, lc:]

        @pl.when(pl.program_id(1) == 0)
        def _():
            carry[...] = jnp.zeros_like(carry)
            _fill_powers(pw_ref, ar, ai, seg, lc)

        ut = u_ref[...]
        s_ref[...] = _dot(ut, b_ref[0])

        ar8, ai8 = jnp.broadcast_to(ar, (SUBLANES, lc)), jnp.broadcast_to(ai, (SUBLANES, lc))

        def local(j, x):
            xr = ar8 * x[0] - ai8 * x[1] + s_ref[tile(j), :lc]
            xi = ar8 * x[1] + ai8 * x[0] + s_ref[tile(j), lc:]
            s_ref[tile(j), :lc] = xr
            s_ref[tile(j), lc:] = xi
            return xr, xi

        zero = jnp.zeros((SUBLANES, lc), F32)
        er, ei = lax.fori_loop(0, seg, local, (zero, zero))
        steps, pr, pi = _scan_consts(*_cpow2(ar, ai, seg.bit_length() - 1), False, lc)
        cr, ci = carry[:, :lc], carry[:, lc:]
        tr, ti = _scan_tile(er, ei, steps, pr, pi, cr, ci)
        rowi = lax.broadcasted_iota(jnp.int32, (SUBLANES, lc), 0)
        before_r = jnp.where(rowi == 0, cr, pltpu.roll(tr, 1, 0))
        before_i = jnp.where(rowi == 0, ci, pltpu.roll(ti, 1, 0))
        carry[:, :lc] = jnp.broadcast_to(tr[SUBLANES - 1:, :], (SUBLANES, lc))
        carry[:, lc:] = jnp.broadcast_to(ti[SUBLANES - 1:, :], (SUBLANES, lc))

        def fix(j, _):
            pwr, pwi = pw_ref[tile(j), :lc], pw_ref[tile(j), lc:]
            s_ref[tile(j), :lc] += pwr * before_r - pwi * before_i
            s_ref[tile(j), lc:] += pwr * before_i + pwi * before_r
            return 0

        lax.fori_loop(0, seg, fix, 0)
        y0 = _dot(s_ref[...], c_ref[0], NT) + d_ref[...] * ut
        y0_ref[...] = y0
        y1_ref[...] = jax.nn.gelu(y0)

    chan = pl.BlockSpec((tt, LANES), lambda j, c: (c, j))
    sup = pl.BlockSpec((1, LANES, 2 * lc), lambda j, c: (j, 0, 0))
    s_in, s_out, s_shape, s_scratch, s_ops = _side_args(side)
    res = pl.pallas_call(
        _hosted(body, side, 5, 3, (nl, nt)), name=name, grid=(nl, nt),
        in_specs=[chan, pl.BlockSpec((1, 2 * lc), lambda j, c: (0, j)), sup, sup,
                  pl.BlockSpec((1, LANES), lambda j, c: (0, j))] + s_in,
        out_specs=[pl.BlockSpec((tt, 2 * lc), lambda j, c: (c, j)), chan, chan] + s_out,
        out_shape=[jax.ShapeDtypeStruct((s, 2 * N_STATE), F32), jax.ShapeDtypeStruct((s, SSM_WIDTH), F32),
                   jax.ShapeDtypeStruct((s, SSM_WIDTH), F32)] + s_shape,
        scratch_shapes=[pltpu.VMEM((SUBLANES, 2 * lc), F32), pltpu.VMEM((seg * SUBLANES, 2 * lc), F32)] + s_scratch,
        compiler_params=_params(("arbitrary", "arbitrary")),
    )(u, acat, bsup, csup, d_skip, *s_ops)
    return _split_side(res, 3, side)


def _ssm_bwd(dy0, states, u, acat, bsup, csup, d_skip, name, side=None):
    s = u.shape[0]
    lc = SCAN_LANES
    tt, seg = _scan_chunk(s)
    nl, nt = N_STATE // lc, s // tt
    tile = lambda j: pl.ds(pl.multiple_of(j * SUBLANES, SUBLANES), SUBLANES)

    def body(dy_ref, s_ref, sp_ref, u_ref, a_ref, b_ref, c_ref, d_ref,
             du_ref, da_ref, db_ref, dc_ref, dd_ref, lam_ref, carry, pw_ref):
        c = pl.program_id(1)
        ar, ai = a_ref[:, :lc], a_ref[:, lc:]

        @pl.when(c == 0)
        def _():
            carry[...] = jnp.zeros_like(carry)
            for r in (da_ref, db_ref, dc_ref, dd_ref):
                r[...] = jnp.zeros_like(r)
            _fill_powers(pw_ref, ar, ai, seg, lc)

        dy = dy_ref[...]
        ut = u_ref[...]
        lam_ref[...] = _dot(dy, c_ref[0])

        ar8, ai8 = jnp.broadcast_to(ar, (SUBLANES, lc)), jnp.broadcast_to(ai, (SUBLANES, lc))

        def local(i, x):
            j = seg - 1 - i
            xr = ar8 * x[0] + ai8 * x[1] + lam_ref[tile(j), :lc]
            xi = ar8 * x[1] - ai8 * x[0] + lam_ref[tile(j), lc:]
            lam_ref[tile(j), :lc] = xr
            lam_ref[tile(j), lc:] = xi
            return xr, xi

        zero = jnp.zeros((SUBLANES, lc), F32)
        er, ei = lax.fori_loop(0, seg, local, (zero, zero))
        big_r, big_i = _cpow2(ar, ai, seg.bit_length() - 1)
        steps, pr, pi = _scan_consts(big_r, -big_i, True, lc)
        cr, ci = carry[:, :lc], carry[:, lc:]
        tr, ti = _scan_tile(er, ei, steps, pr, pi, cr, ci)
        rowi = lax.broadcasted_iota(jnp.int32, (SUBLANES, lc), 0)
        after_r = jnp.where(rowi == SUBLANES - 1, cr, pltpu.roll(tr, SUBLANES - 1, 0))
        after_i = jnp.where(rowi == SUBLANES - 1, ci, pltpu.roll(ti, SUBLANES - 1, 0))
        carry[:, :lc] = jnp.broadcast_to(tr[:1, :], (SUBLANES, lc))
        carry[:, lc:] = jnp.broadcast_to(ti[:1, :], (SUBLANES, lc))

        start = c != nt - 1
        last_r = jnp.where(start, jnp.broadcast_to(sp_ref[SUBLANES - 1:, :lc], (SUBLANES, lc)), 0.0)
        last_i = jnp.where(start, jnp.broadcast_to(sp_ref[SUBLANES - 1:, lc:], (SUBLANES, lc)), 0.0)
        first_r = jnp.where(rowi == 0, last_r, pltpu.roll(s_ref[tile(seg - 1), :lc], 1, 0))
        first_i = jnp.where(rowi == 0, last_i, pltpu.roll(s_ref[tile(seg - 1), lc:], 1, 0))

        def fix(j, acc):
            dar, dai = acc
            k = seg - 1 - j
            pwr, pwi = pw_ref[tile(k), :lc], pw_ref[tile(k), lc:]
            lr = lam_ref[tile(j), :lc] + pwr * after_r + pwi * after_i
            li = lam_ref[tile(j), lc:] + pwr * after_i - pwi * after_r
            lam_ref[tile(j), :lc] = lr
            lam_ref[tile(j), lc:] = li
            jp = jnp.maximum(j - 1, 0)
            sr = jnp.where(j > 0, s_ref[tile(jp), :lc], first_r)
            si = jnp.where(j > 0, s_ref[tile(jp), lc:], first_i)
            return dar + lr * sr + li * si, dai + li * sr - lr * si

        dar, dai = lax.fori_loop(0, seg, fix, (zero, zero))
        da_ref[:, :lc] += dar
        da_ref[:, lc:] += dai
        lam = lam_ref[...].astype(BF16)
        du_ref[...] = (_dot(lam, b_ref[0], NT) + d_ref[...] * dy).astype(du_ref.dtype)
        db_ref[0] += _dot(ut, lam, TN)
        dc_ref[0] += _dot(dy, s_ref[...], TN)
        dd_ref[...] += jnp.sum(dy * ut, axis=0, keepdims=True)

    rev = lambda j, c: (nt - 1 - c, j)
    chan = pl.BlockSpec((tt, LANES), rev)
    sup = pl.BlockSpec((1, LANES, 2 * lc), lambda j, c: (j, 0, 0))
    row = pl.BlockSpec((1, LANES), lambda j, c: (0, j))
    s_in, s_out, s_shape, s_scratch, s_ops = _side_args(side)
    res = pl.pallas_call(
        _hosted(body, side, 8, 5, (nl, nt)), name=name, grid=(nl, nt),
        in_specs=[chan, pl.BlockSpec((tt, 2 * lc), rev),
                  pl.BlockSpec((SUBLANES, 2 * lc), lambda j, c: (jnp.maximum((nt - 1 - c) * seg - 1, 0), j)),
                  chan, pl.BlockSpec((1, 2 * lc), lambda j, c: (0, j)), sup, sup, row] + s_in,
        out_specs=[chan, pl.BlockSpec((SUBLANES, 2 * lc), lambda j, c: (0, j)), sup, sup, row] + s_out,
        out_shape=[jax.ShapeDtypeStruct((s, SSM_WIDTH), BF16), jax.ShapeDtypeStruct((SUBLANES, 2 * N_STATE), F32),
                   jax.ShapeDtypeStruct(bsup.shape, F32), jax.ShapeDtypeStruct(csup.shape, F32),
                   jax.ShapeDtypeStruct((1, SSM_WIDTH), F32)] + s_shape,
        scratch_shapes=[pltpu.VMEM((tt, 2 * lc), F32), pltpu.VMEM((SUBLANES, 2 * lc), F32),
                        pltpu.VMEM((seg * SUBLANES, 2 * lc), F32)] + s_scratch,
        compiler_params=_params(("arbitrary", "arbitrary")),
    )(dy0, states, states, u, acat, bsup, csup, d_skip, *s_ops)
    return _split_side(res, 5, side)


def _discretise(ar, ai, ldt, br, bi):
    dt = jnp.exp(ldt)
    lr, li = ar * dt, ai * dt
    e = jnp.exp(lr)
    abar_r, abar_i = e * jnp.cos(li), e * jnp.sin(li)
    den = ar * ar + ai * ai
    coef_r = ((abar_r - 1.0) * ar + abar_i * ai) / den
    coef_i = (abar_i * ar - (abar_r - 1.0) * ai) / den
    return abar_r, abar_i, coef_r * br - coef_i * bi, coef_r * bi + coef_i * br


def _group_mask():
    shape = (LANES, SCAN_LANES)
    return (lax.broadcasted_iota(jnp.int32, shape, 0) // SSM_GROUP
            == lax.broadcasted_iota(jnp.int32, shape, 1) // SSM_STATE)


def _ssm_mats_fwd(a_re, a_im, log_dt, b_re, b_im, c_re, c_im, name):
    nl = N_STATE // SCAN_LANES
    lc = SCAN_LANES

    def body(ar, ai, ldt, br, bi, cr, ci, acat, bsup, csup):
        abar_r, abar_i, bbar_r, bbar_i = _discretise(ar[...], ai[...], ldt[...], br[...], bi[...])
        same = _group_mask()
        spread = lambda m, j: jnp.where(same, jnp.tile(m[:, j * lc:(j + 1) * lc], (LANES // SSM_GROUP, 1)), 0.0)
        c_r, c_i = cr[...], -ci[...]
        for j in range(nl):
            acat[:, 2 * j * lc:(2 * j + 1) * lc] = abar_r[:, j * lc:(j + 1) * lc]
            acat[:, (2 * j + 1) * lc:(2 * j + 2) * lc] = abar_i[:, j * lc:(j + 1) * lc]
            bsup[j, :, :lc] = spread(bbar_r, j)
            bsup[j, :, lc:] = spread(bbar_i, j)
            csup[j, :, :lc] = spread(c_r, j)
            csup[j, :, lc:] = spread(c_i, j)

    return pl.pallas_call(
        body, name=name,
        out_shape=[jax.ShapeDtypeStruct((1, 2 * N_STATE), F32), jax.ShapeDtypeStruct((nl, LANES, 2 * lc), F32),
                   jax.ShapeDtypeStruct((nl, LANES, 2 * lc), F32)],
        compiler_params=_params(),
    )(a_re, a_im, log_dt, b_re, b_im, c_re, c_im)


def _ssm_mats_bwd(a_re, a_im, log_dt, b_re, b_im, d_acat, d_bsup, d_csup, name):
    nl = N_STATE // SCAN_LANES
    lc = SCAN_LANES

    def body(ar, ai, ldt, br, bi, dac, dbs, dcs, d_ar, d_ai, d_ldt, d_br, d_bi, d_cr, d_ci):
        same = _group_mask()

        def gather(ref, j, half):
            m = jnp.where(same, ref[j, :, half * lc:(half + 1) * lc], 0.0)
            tot = m[:SSM_GROUP]
            for k in range(1, LANES // SSM_GROUP):
                tot = tot + m[k * SSM_GROUP:(k + 1) * SSM_GROUP]
            return tot

        cols = lambda ref, half: jnp.concatenate([gather(ref, j, half) for j in range(nl)], axis=1)
        d_abar_r = jnp.concatenate([dac[:, 2 * j * lc:(2 * j + 1) * lc] for j in range(nl)], axis=1)
        d_abar_i = jnp.concatenate([dac[:, (2 * j + 1) * lc:(2 * j + 2) * lc] for j in range(nl)], axis=1)
        _, vjp = jax.vjp(_discretise, ar[...], ai[...], ldt[...], br[...], bi[...])
        outs = vjp((d_abar_r, d_abar_i, cols(dbs, 0), cols(dbs, 1)))
        for ref, val in zip((d_ar, d_ai, d_ldt, d_br, d_bi), outs):
            ref[...] = val
        d_cr[...] = cols(dcs, 0)
        d_ci[...] = -cols(dcs, 1)

    row = jax.ShapeDtypeStruct((1, N_STATE), F32)
    mat = jax.ShapeDtypeStruct((SSM_GROUP, N_STATE), F32)
    return pl.pallas_call(
        body, name=name, out_shape=[row, row, row, mat, mat, mat, mat], compiler_params=_params(),
    )(a_re, a_im, log_dt, b_re, b_im, d_acat, d_bsup, d_csup)


def _states_on_lanes(sm):
    flat = lambda a: a.reshape(1, N_STATE)
    chan_b = lambda b: jnp.transpose(b, (2, 0, 1)).reshape(SSM_GROUP, N_STATE)
    chan_c = lambda c: jnp.transpose(c, (1, 0, 2)).reshape(SSM_GROUP, N_STATE)
    return (flat(sm["ssm_a_re"]), flat(sm["ssm_a_im"]), flat(jnp.repeat(sm["ssm_log_dt"], SSM_STATE)),
            chan_b(sm["ssm_b_re"]), chan_b(sm["ssm_b_im"]), chan_c(sm["ssm_c_re"]), chan_c(sm["ssm_c_im"]))


def _from_states_on_lanes(d_ar, d_ai, d_ldt, d_br, d_bi, d_cr, d_ci):
    grp = lambda a: a.reshape(SSM_GROUPS, SSM_STATE)
    back_b = lambda b: jnp.transpose(b.reshape(SSM_GROUP, SSM_GROUPS, SSM_STATE), (1, 2, 0))
    back_c = lambda c: jnp.transpose(c.reshape(SSM_GROUP, SSM_GROUPS, SSM_STATE), (1, 0, 2))
    return (grp(d_ar), grp(d_ai), jnp.sum(grp(d_ldt), axis=1), back_b(d_br), back_b(d_bi), back_c(d_cr), back_c(d_ci))


def _mem_fwd(mem, g_mem, w_kv, g_k, name):
    ml = mem.shape[0]

    def body(mem_ref, gm_ref, w_ref, gk_ref, memn_ref, kv_ref, kn_ref, vv_ref):
        memn = _rms(mem_ref[...], gm_ref[...])
        memn_ref[...] = memn.astype(BF16)
        kv = _dot(memn, w_ref[...])
        kv_ref[...] = kv
        for hh in range(XA_HEADS):
            sl = slice(hh * XA_HEAD_DIM, (hh + 1) * XA_HEAD_DIM)
            kn_ref[:, sl] = _rms(kv[:, sl], gk_ref[...]).astype(BF16)
        vv_ref[...] = kv[:, XA_WIDTH:].astype(BF16)

    return pl.pallas_call(
        body, name=name,
        out_shape=[jax.ShapeDtypeStruct((ml, D_MODEL), BF16), jax.ShapeDtypeStruct((ml, 2 * XA_WIDTH), F32),
                   jax.ShapeDtypeStruct((ml, XA_WIDTH), BF16), jax.ShapeDtypeStruct((ml, XA_WIDTH), BF16)],
        compiler_params=_params(),
    )(mem, g_mem, w_kv, g_k)


def _mem_bwd(mem, g_mem, memn, w_kv, kv, g_k, dkn, dvv, name):
    def body(mem_ref, gm_ref, memn_ref, w_ref, kv_ref, gk_ref, dkn_ref, dvv_ref, dw_ref, dgm_ref, dgk_ref):
        kv = kv_ref[...]
        dgk = jnp.zeros(dgk_ref.shape, F32)
        parts = []
        for hh in range(XA_HEADS):
            sl = slice(hh * XA_HEAD_DIM, (hh + 1) * XA_HEAD_DIM)
            _, vjp = jax.vjp(_rms, kv[:, sl], gk_ref[...])
            dk, dg = vjp(dkn_ref[:, sl])
            parts.append(dk)
            dgk = dgk + dg
        dgk_ref[...] = dgk
        dkv = jnp.concatenate(parts + [dvv_ref[...]], axis=1)
        dw_ref[...] = _dot(memn_ref[...], dkv, TN)
        dmemn = _dot(dkv, w_ref[...], NT)
        _, vjp = jax.vjp(_rms, mem_ref[...], gm_ref[...])
        dgm_ref[...] = vjp(dmemn)[1]

    return pl.pallas_call(
        body, name=name,
        out_shape=[jax.ShapeDtypeStruct((D_MODEL, 2 * XA_WIDTH), F32), jax.ShapeDtypeStruct(g_mem.shape, F32),
                   jax.ShapeDtypeStruct(g_k.shape, F32)],
        compiler_params=_params(),
    )(mem, g_mem, memn, w_kv, kv, g_k, dkn, dvv)


def _xa_head(qx_h, g_q, kn_h, vv_h):
    qn = _rms(qx_h, g_q)
    sc = _dot(qn, kn_h, NT) * (XA_HEAD_DIM ** -0.5)
    sc = sc - jnp.max(sc, axis=-1, keepdims=True)
    e = jnp.exp(sc)
    p = e / jnp.sum(e, axis=-1, keepdims=True)
    return qn, p


def _xa_fwd(qx, g_q, kn, vv, name):
    def fn(qt, gq, knt, vvt):
        outs = []
        for hh in range(XA_HEADS):
            sl = slice(hh * XA_HEAD_DIM, (hh + 1) * XA_HEAD_DIM)
            _, p = _xa_head(qt[:, sl], gq, knt[:, sl], vvt[:, sl])
            outs.append(_dot(p, vvt[:, sl]))
        return (jnp.concatenate(outs, axis=1),), ()

    return _rw(fn, [qx], [g_q, kn, vv], [(XA_WIDTH, BF16)], [], name, tm=512)[0]


def _xa_bwd(qx, g_q, kn, vv, do, name):
    def fn(qt, dot_, gq, knt, vvt):
        dqs, dks, dvs = [], [], []
        dgq = jnp.zeros_like(gq)
        for hh in range(XA_HEADS):
            sl = slice(hh * XA_HEAD_DIM, (hh + 1) * XA_HEAD_DIM)
            qn, p = _xa_head(qt[:, sl], gq, knt[:, sl], vvt[:, sl])
            doh = dot_[:, sl]
            dp = _dot(doh, vvt[:, sl], NT)
            dvs.append(_dot(p, doh, TN))
            ds = p * (dp - jnp.sum(dp * p, axis=-1, keepdims=True)) * (XA_HEAD_DIM ** -0.5)
            dqn = _dot(ds, knt[:, sl])
            dks.append(_dot(ds, qn, TN))
            _, vjp = jax.vjp(_rms, qt[:, sl], gq)
            dq, dg = vjp(dqn)
            dqs.append(dq)
            dgq = dgq + dg
        return ((jnp.concatenate(dqs, axis=1),),
                (jnp.concatenate(dks, axis=1), jnp.concatenate(dvs, axis=1), dgq))

    return _rw(fn, [qx, do], [g_q, kn, vv], [(XA_WIDTH, BF16)], [kn.shape, vv.shape, g_q.shape], name, tm=512)


BIG = [
    ("w_in", (D_MODEL, IN_WIDTH), 1), ("ssm_w_glu", (SSM_WIDTH, SSM_WIDTH), 0), ("w_out", (D_MODEL, D_MODEL), 0),
    ("xa_w_q", (D_MODEL, XA_WIDTH), 0), ("xa_w_kv", (D_MODEL, 2 * XA_WIDTH), 0), ("xa_w_o", (XA_WIDTH, D_MODEL), 1),
    ("w_up", (D_MODEL, D_FF), 1), ("w_down", (D_FF, D_MODEL), 0),
]
BIG_INDEX = {n: i for i, (n, _, _) in enumerate(BIG)}


def _shard_shape(shape, axis):
    return tuple(d // N_DEV if i == axis else d for i, d in enumerate(shape))


def _shard_of(ref, axis, d):
    n = ref.shape[axis] // N_DEV
    return ref.at[pl.ds(d * n, n), :] if axis == 0 else ref.at[:, pl.ds(d * n, n)]


def _gather_side(names, shards):
    idxs = [BIG_INDEX[n] for n in names]

    def make(ins, outs, send_sems, recv_sems):
        x, y, c = lax.axis_index("x"), lax.axis_index("y"), lax.axis_index("c")
        cps = []
        for j, i in enumerate(idxs):
            mine = _shard_of(outs[j], BIG[i][2], 4 * x + 2 * y + c)
            cps.append(pltpu.make_async_copy(ins[j], mine, send_sems.at[N_DEV * j]))
            for rel in range(1, N_DEV):
                to = tuple(1 - p if rel >> bit & 1 else p for p, bit in ((x, 2), (y, 1), (c, 0)))
                cps.append(pltpu.make_async_remote_copy(
                    src_ref=ins[j], dst_ref=mine, send_sem=send_sems.at[N_DEV * j + rel],
                    recv_sem=recv_sems.at[N_DEV * j + rel], device_id=to, device_id_type=MESH))
        return cps

    return _Side(shards, [jax.ShapeDtypeStruct(BIG[i][1], BF16) for i in idxs], N_DEV * len(idxs), make)


def _gather_two_level_side(name, shard):
    i = BIG_INDEX[name]

    def parts(ins, outs, send_sems, recv_sems):
        x, y, c = lax.axis_index("x"), lax.axis_index("y"), lax.axis_index("c")
        sibling = (x, y, 1 - c)
        chips = [(1 - x, y), (x, 1 - y), (1 - x, 1 - y)]

        def place(dev):
            return _shard_of(outs[0], BIG[i][2], 4 * dev[0] + 2 * dev[1] + dev[2])

        def copy(k, blk, to, src=None):
            return pltpu.make_async_remote_copy(
                src_ref=place(blk) if src is None else src, dst_ref=place(blk), send_sem=send_sems.at[k],
                recv_sem=recv_sems.at[k], device_id=to, device_id_type=MESH)

        mine = pltpu.make_async_copy(ins[0], place((x, y, c)), send_sems.at[7])
        first = [copy(0, (x, y, c), sibling, src=ins[0])]
        first += [copy(1 + j, (x, y, c), (*chip, c), src=ins[0]) for j, chip in enumerate(chips)]
        passed = [copy(4 + j, (*chip, c), sibling) for j, chip in enumerate(chips)]
        arrived = [copy(1 + j, (*chip, c), (x, y, c)) for j, chip in enumerate(chips)]
        from_sibling = [copy(0, sibling, (x, y, c))] + [copy(4 + j, (*chip, 1 - c), (x, y, c))
                                                       for j, chip in enumerate(chips)]
        return mine, first, passed, arrived, from_sibling

    def make(ins, outs, send_sems, recv_sems):
        mine, first, _, _, _ = parts(ins, outs, send_sems, recv_sems)
        return [mine] + first

    def finish(ins, outs, send_sems, recv_sems):
        mine, first, passed, arrived, from_sibling = parts(ins, outs, send_sems, recv_sems)
        for got, onward in zip(arrived, passed):
            got.wait_recv()
            onward.start()
        for cp in from_sibling:
            cp.wait_recv()
        for cp in first + passed:
            cp.wait_send()
        mine.wait()

    return _Side([shard], [jax.ShapeDtypeStruct(BIG[i][1], BF16)], N_DEV, make, finish)


def _sibling_side(names, grads):
    idxs = [BIG_INDEX[n] for n in names]

    def make(ins, outs, send_sems, recv_sems):
        x, y, c = lax.axis_index("x"), lax.axis_index("y"), lax.axis_index("c")
        return [pltpu.make_async_remote_copy(
            src_ref=_shard_of(ins[j], BIG[i][2], 2 * k + (1 - c)), dst_ref=outs[j].at[k],
            send_sem=send_sems.at[4 * j + k], recv_sem=recv_sems.at[4 * j + k], device_id=(x, y, 1 - c),
            device_id_type=MESH) for j, i in enumerate(idxs) for k in range(4)]

    shapes = [jax.ShapeDtypeStruct((4,) + _shard_shape(BIG[i][1], BIG[i][2]), F32) for i in idxs]
    return _Side(grads, shapes, 4 * len(idxs), make)


def _chips_side(parts):
    def make(ins, outs, send_sems, recv_sems):
        x, y, c = lax.axis_index("x"), lax.axis_index("y"), lax.axis_index("c")
        chips = [(1 - x, y), (x, 1 - y), (1 - x, 1 - y)]
        return [pltpu.make_async_remote_copy(
            src_ref=ins[j].at[2 * cx + cy], dst_ref=outs[j].at[r], send_sem=send_sems.at[3 * j + r],
            recv_sem=recv_sems.at[3 * j + r], device_id=(cx, cy, c), device_id_type=MESH)
            for r, (cx, cy) in enumerate(chips) for j in range(len(parts))]

    return _Side(parts, [jax.ShapeDtypeStruct((3,) + p.shape[1:], p.dtype) for p in parts], 3 * len(parts), make)


def _reduce_add(grad, recv, axis, core, name):
    rs, cs = recv.shape[1:]
    rt = _row_tile(rs, 256)
    nt = rs // rt

    def body(c_ref, g_ref, r_ref, p_ref, pb_ref):
        sm = g_ref[...] + r_ref[0]
        p_ref[0] = sm
        pb_ref[0] = sm.astype(BF16)

    if axis == 0:
        g_spec = pl.BlockSpec((rt, cs), lambda k, t, c_ref: ((2 * k + c_ref[0]) * nt + t, 0))
    else:
        g_spec = pl.BlockSpec((rt, cs), lambda k, t, c_ref: (t, 2 * k + c_ref[0]))
    slab = pl.BlockSpec((1, rt, cs), lambda k, t, c_ref: (k, t, 0))
    return pl.pallas_call(
        body, name=name,
        grid_spec=pltpu.PrefetchScalarGridSpec(num_scalar_prefetch=1, grid=(4, nt), in_specs=[g_spec, slab],
                                               out_specs=[slab, slab]),
        out_shape=[jax.ShapeDtypeStruct(recv.shape, F32), jax.ShapeDtypeStruct(recv.shape, BF16)],
        compiler_params=_params(("parallel", "parallel")),
    )(core, grad, recv)


def _all_gather(block, name, side):
    m_per, n = block.shape
    ns_in, ns_out = len(side.ins), len(side.out_shapes)

    def body(*refs):
        x_ref, s_ins, out_ref = refs[0], refs[1:1 + ns_in], refs[1 + ns_in]
        s_outs = refs[2 + ns_in:2 + ns_in + ns_out]
        send_sems, recv_sems, local_sem, s_send, s_recv = refs[2 + ns_in + ns_out:]
        others = side.make(s_ins, s_outs, s_send, s_recv)
        for cp in others:
            cp.start()
        x, y, c = lax.axis_index("x"), lax.axis_index("y"), lax.axis_index("c")
        me, sibling = (x, y, c), (x, y, 1 - c)
        chips = [(1 - x, y), (x, 1 - y), (1 - x, 1 - y)]

        def rows(px, py, pc):
            return out_ref.at[pl.ds((4 * px + 2 * py + pc) * m_per, m_per), :]

        def copy(k, blk, to, src=None):
            return pltpu.make_async_remote_copy(
                src_ref=rows(*blk) if src is None else src, dst_ref=rows(*blk),
                send_sem=send_sems.at[k], recv_sem=recv_sems.at[k], device_id=to, device_id_type=MESH)

        mine = pltpu.make_async_copy(x_ref, rows(*me), local_sem)
        mine.start()
        first = [copy(0, me, sibling, src=x_ref)]
        first += [copy(1 + j, me, (*chip, c), src=x_ref) for j, chip in enumerate(chips)]
        for cp in first:
            cp.start()
        passed = [copy(4 + j, (*chip, c), sibling) for j, chip in enumerate(chips)]
        for j, chip in enumerate(chips):
            copy(1 + j, (*chip, c), me).wait_recv()
            passed[j].start()
        copy(0, sibling, me).wait_recv()
        for j, chip in enumerate(chips):
            copy(4 + j, (*chip, 1 - c), me).wait_recv()
        for cp in first + passed:
            cp.wait_send()
        mine.wait()
        for cp in others:
            cp.wait()

    res = pl.pallas_call(
        body, name=name, in_specs=[ANY] * (1 + ns_in), out_specs=[ANY] * (1 + ns_out),
        out_shape=[jax.ShapeDtypeStruct((N_DEV * m_per, n), block.dtype)] + side.out_shapes,
        scratch_shapes=[pltpu.SemaphoreType.DMA((7,)), pltpu.SemaphoreType.DMA((7,)), pltpu.SemaphoreType.DMA]
        + side.sems(),
    )(block, *side.ins)
    return res[0], list(res[1:])


def _adam_math(w, g, m, v):
    m = ADAM_B1 * m + (1.0 - ADAM_B1) * g
    v = ADAM_B2 * v + (1.0 - ADAM_B2) * (g * g)
    m_hat = m / (1.0 - ADAM_B1 ** ADAM_STEP)
    v_hat = v / (1.0 - ADAM_B2 ** ADAM_STEP)
    delta = -ADAM_LR * (m_hat / (jnp.sqrt(v_hat) + ADAM_EPS) + ADAM_WD * w)
    return delta, m, v


def _adam_sharded(own, recv, w, m, v, chip, name):
    rs, cs = w.shape
    rt = _row_tile(rs, 256)

    def body(chip_ref, p_ref, r_ref, w_ref, m_ref, v_ref, g_out, d_out, m_out, v_out):
        g = p_ref[0] + r_ref[0].astype(F32) + r_ref[1].astype(F32) + r_ref[2].astype(F32)
        d, mn, vn = _adam_math(w_ref[...], g, m_ref[...], v_ref[...])
        g_out[...] = g
        d_out[...] = d
        m_out[...] = mn
        v_out[...] = vn

    tile = pl.BlockSpec((rt, cs), lambda t, chip_ref: (t, 0))
    return pl.pallas_call(
        body, name=name,
        grid_spec=pltpu.PrefetchScalarGridSpec(
            num_scalar_prefetch=1, grid=(rs // rt,),
            in_specs=[pl.BlockSpec((1, rt, cs), lambda t, chip_ref: (chip_ref[0], t, 0)),
                      pl.BlockSpec((3, rt, cs), lambda t, chip_ref: (0, t, 0)), tile, tile, tile],
            out_specs=[tile] * 4),
        out_shape=[jax.ShapeDtypeStruct((rs, cs), F32)] * 4,
        compiler_params=_params(("parallel",)),
    )(chip, own, recv, w, m, v)


SMALL = ["g_mix", "ssm_a_re", "ssm_a_im", "ssm_log_dt", "ssm_b_re", "ssm_b_im", "ssm_c_re", "ssm_c_im", "ssm_d",
         "sb_g_q", "sb_g_k", "g_out_ssm", "g_out_sb", "g_xa", "g_mem", "xa_g_q", "xa_g_k", "g_mlp"]
PACK_TILE = SUBLANES * LANES


def _natural_2d(n):
    return (n // LANES, LANES) if n % LANES == 0 else (1, n)


def _pack_small(arrs):
    parts = []
    for a in arrs:
        flat = a.reshape(-1)
        parts.append(jnp.pad(flat, (0, (-flat.shape[0]) % PACK_TILE)))
    return jnp.concatenate(parts).reshape(-1, LANES)


def _adam_replicated(gathered, sizes, ws, ms, vs, name):
    n_w = len(ws)
    r_dev = gathered.shape[0] // N_DEV
    offs, off = [], 0
    for n in sizes:
        offs.append(off)
        off += (n + PACK_TILE - 1) // PACK_TILE * SUBLANES
    assert off == r_dev

    def body(*refs):
        g_ref = refs[0]
        w_refs, m_refs, v_refs = refs[1:1 + n_w], refs[1 + n_w:1 + 2 * n_w], refs[1 + 2 * n_w:1 + 3 * n_w]
        outs = refs[1 + 3 * n_w:]

        def total(i, shape):
            r, cdim = shape
            acc = g_ref[pl.ds(offs[i], r), :cdim]
            for d in range(1, N_DEV):
                acc = acc + g_ref[pl.ds(d * r_dev + offs[i], r), :cdim]
            return acc

        for i in range(n_w):
            g = total(i, w_refs[i].shape)
            d, mn, vn = _adam_math(w_refs[i][...], g, m_refs[i][...], v_refs[i][...])
            for o, val in zip(outs[4 * i:4 * i + 4], (g, d, mn, vn)):
                o[...] = val
        outs[4 * n_w][...] = total(n_w, (SUBLANES, LANES))

    shapes = [w.shape for w in ws]
    return pl.pallas_call(
        body, name=name,
        out_shape=[jax.ShapeDtypeStruct(shp, F32) for shp in shapes for _ in range(4)]
        + [jax.ShapeDtypeStruct((SUBLANES, LANES), F32)],
        compiler_params=_params(),
    )(gathered, *ws, *ms, *vs)


def _step(x, mem, target, shards, sm, core):
    g, w, sums, reduced = {}, {}, {}, {}

    def gather(names):
        return _gather_side(names, [shards[n] for n in names])

    def to_sibling(names):
        return _sibling_side(names, [g[n] for n in names])

    def add_sibling(names, received):
        for n, r in zip(names, received):
            sums[n] = _reduce_add(g[n], r, BIG[BIG_INDEX[n]][2], core, "reduce_add_" + n)

    def to_chips(names):
        return _chips_side([sums[n][1] for n in names])

    def keep(names, received):
        for n, r in zip(names, received):
            reduced[n] = (sums[n][0], r)

    row = lambda a: a.reshape(1, -1)
    g_mix, g_xa, g_mlp, g_mem = row(sm["g_mix"]), row(sm["g_xa"]), row(sm["g_mlp"]), row(sm["g_mem"])
    g_os, g_ob = row(sm["g_out_ssm"]), row(sm["g_out_sb"])
    sb_gq, sb_gk = jnp.tile(row(sm["sb_g_q"]), (1, SB_HEADS)), jnp.tile(row(sm["sb_g_k"]), (1, SB_HEADS))
    xa_gq, xa_gk = row(sm["xa_g_q"]), row(sm["xa_g_k"])
    d_skip = row(sm["ssm_d"])

    h1, (w["w_in"],) = _norm_fwd(x, g_mix, "norm_mix", side=_gather_two_level_side("w_in", shards["w_in"]))
    proj = _mm(h1, w["w_in"], "nn", "in_proj", tn=IN_WIDTH)
    u = _to_segments(proj[:, :SSM_WIDTH])
    q_raw, k_raw = (proj, SB_WIDTH, 1), (proj, SB_WIDTH, 2)
    v_col = (SSM_WIDTH + 2 * SB_WIDTH) // LANES
    sb_scale = SB_HEAD_DIM ** -0.5
    qs, ks = _rw(lambda qt, kt, gq, gk: ((_rms_groups(qt, gq, sb_scale), _rms_groups(kt, gk, 1.0)), ()),
                 [q_raw, k_raw], [sb_gq, sb_gk], [(SB_WIDTH, BF16)] * 2, [], "sb_qk_norm")
    early = ["ssm_w_glu", "w_out", "xa_w_q", "xa_w_kv", "xa_w_o", "w_up"]
    y_sb, got = _sb_fwd(qs, ks, proj, "sb_fwd", v_col=v_col, side=gather(early))
    w.update(zip(early, got))

    ssm_args = _states_on_lanes(sm)
    acat, bsup, csup = _ssm_mats_fwd(*ssm_args, "ssm_mats")
    (states, y0, y1), (w["w_down"],) = _ssm_fwd(u, acat, bsup, csup, d_skip, "ssm_fwd", side=gather(["w_down"]))
    z_glu, y_ssm = _mm(y1, w["ssm_w_glu"], "nn", "ssm_glu", epi=lambda r, yt: (r, yt * jax.nn.sigmoid(r)),
                       extras=(y1,), out_dtypes=(F32, F32))
    y_ssm = _from_segments(y_ssm)

    def cat_norm(a, b, ga, gb):
        return jnp.concatenate([_rms(a, ga), _rms(b, gb)], axis=1)

    ycat = _rw(lambda a, b, ga, gb: ((cat_norm(a, b, ga, gb),), ()), [y_ssm, y_sb], [g_os, g_ob],
               [(D_MODEL, BF16)], [], "norm_out")[0]

    def residual_norm_epi(r, xt, gt):
        xn = r + xt
        return xn, _rms(xn, gt)

    x1, h2 = _mm(ycat, w["w_out"], "nn", "out_proj", epi=residual_norm_epi, extras=(x,), fulls=(g_xa,),
                 out_dtypes=(F32, BF16))
    qx = _mm(h2, w["xa_w_q"], "nn", "xa_q")
    memn, kv, kn_x, vv_x = _mem_fwd(mem, g_mem, w["xa_w_kv"], xa_gk, "xa_mem")
    o_xa = _xa_fwd(qx, xa_gq, kn_x, vv_x, "xa_fwd")
    x2, h3 = _mm(o_xa, w["xa_w_o"], "nn", "xa_o", epi=residual_norm_epi, extras=(x1,), fulls=(g_mlp,),
                 out_dtypes=(F32, BF16))

    def up_epi(r):
        rl = jnp.maximum(r, 0.0)
        return (rl * rl,)

    r_up = _mm(h3, w["w_up"], "nn", "mlp_up", epi=up_epi, out_dtypes=(BF16,), tm=2048, tn=2048)

    def loss_epi(r, xt, tt):
        d = r + xt - tt
        return (d * (1.0 / D_MODEL),) * 2, (jnp.sum(d * d, axis=0, keepdims=True),)

    dx3, dx3_b, sq = _mm(r_up, w["w_down"], "nn", "mlp_down", epi=loss_epi, extras=(x2, target),
                         out_dtypes=(F32, BF16), sums=[(1, D_MODEL)])
    loss = jnp.sum(sq) * (0.5 / D_MODEL)

    def norm_bwd_epi(r, xt, drt, gt):
        _, vjp = jax.vjp(_rms, xt, gt)
        dx_, dg_ = vjp(r)
        return (dx_ + drt,) * 2, (dg_,)

    g["w_down"] = _mm(r_up, dx3_b, "tn", "d_w_down", tk=2048)
    da = _mm(dx3_b, w["w_down"], "nt", "d_r", epi=lambda r, rt: (r * 2.0 * jnp.sqrt(rt.astype(F32)),), extras=(r_up,),
             out_dtypes=(BF16,), tn=2048)
    g["w_up"] = _mm(h3, da, "tn", "d_w_up", tk=2048)
    mlp = ["w_down", "w_up"]
    (dx2, dx2_b, g["g_mlp"]), got = _mm(da, w["w_up"], "nt", "d_h3", epi=norm_bwd_epi, extras=(x2, dx3),
                                        fulls=(g_mlp,), out_dtypes=(F32, BF16), sums=[g_mlp.shape],
                                        side=to_sibling(mlp))
    add_sibling(mlp, got)
    g["xa_w_o"] = _mm(o_xa, dx2_b, "tn", "d_xa_w_o", tk=2048)
    do_xa = _mm(dx2_b, w["xa_w_o"], "nt", "d_o_xa")
    dqx, dkn_x, dvv_x, g["xa_g_q"] = _xa_bwd(qx, xa_gq, kn_x, vv_x, do_xa, "xa_bwd")
    g["xa_w_kv"], g["g_mem"], g["xa_g_k"] = _mem_bwd(mem, g_mem, memn, w["xa_w_kv"], kv, xa_gk, dkn_x, dvv_x,
                                                     "xa_mem_bwd")
    g["xa_w_q"] = _mm(h2, dqx, "tn", "d_xa_w_q", tk=2048)
    dx1, dx1_b, g["g_xa"] = _mm(dqx, w["xa_w_q"], "nt", "d_h2", epi=norm_bwd_epi, extras=(x1, dx2), fulls=(g_xa,),
                                out_dtypes=(F32, BF16), sums=[g_xa.shape])
    g["w_out"] = _mm(ycat, dx1_b, "tn", "d_w_out", tk=2048)
    dycat = _mm(dx1_b, w["w_out"], "nt", "d_ycat")

    def cat_bwd(a, b, dy, ga, gb):
        _, vjp = jax.vjp(cat_norm, a, b, ga, gb)
        da_, db_, dga, dgb = vjp(dy)
        return (da_, db_), (dga, dgb)

    dy_ssm, dy_sb, g["g_out_ssm"], g["g_out_sb"] = _rw(
        cat_bwd, [y_ssm, y_sb, dycat], [g_os, g_ob], [(SSM_WIDTH, F32), (SB_WIDTH, F32)], [g_os.shape, g_ob.shape],
        "d_norm_out", tm=512)

    def glu_bwd(dy, yt, zt):
        sg = jax.nn.sigmoid(zt)
        return (dy * sg, dy * yt * sg * (1.0 - sg)), ()

    dy1_a, dz = _rw(glu_bwd, [_to_segments(dy_ssm), y1, z_glu], [], [(SSM_WIDTH, F32), (SSM_WIDTH, BF16)], [], "d_glu")
    g["ssm_w_glu"] = _mm(y1, dz, "tn", "d_w_glu", tk=2048)

    def gelu_bwd_epi(r, da_, y0t):
        _, vjp = jax.vjp(jax.nn.gelu, y0t)
        return (vjp(r + da_)[0],)

    mid = ["w_out", "xa_w_q", "xa_w_kv", "xa_w_o", "ssm_w_glu"]
    dy0, got = _mm(dz, w["ssm_w_glu"], "nt", "d_y1", epi=gelu_bwd_epi, extras=(dy1_a, y0), side=to_sibling(mid))
    add_sibling(mid, got)
    (du, da8, d_bsup, d_csup, g["ssm_d"]), got = _ssm_bwd(dy0, states, u, acat, bsup, csup, d_skip, "ssm_bwd",
                                                          side=to_chips(mlp))
    keep(mlp, got)
    d_acat = jnp.sum(da8, axis=0, keepdims=True)
    d_mats = _ssm_mats_bwd(*ssm_args[:5], d_acat, d_bsup, d_csup, "ssm_mats_bwd")
    for nm, val in zip(("ssm_a_re", "ssm_a_im", "ssm_log_dt", "ssm_b_re", "ssm_b_im", "ssm_c_re", "ssm_c_im"),
                       _from_states_on_lanes(*d_mats)):
        g[nm] = val

    (dqs, dks, dvs), got = _sb_bwd(qs, ks, proj, y_sb, dy_sb, "sb_bwd", v_col=v_col, side=to_chips(mid))
    keep(mid, got)

    def d_proj_rows(du_t, qt, dqt, kt, dkt, dvt, gq, gk):
        _, vjp_q = jax.vjp(lambda a, b_: _rms_groups(a, b_, sb_scale), qt, gq)
        _, vjp_k = jax.vjp(lambda a, b_: _rms_groups(a, b_, 1.0), kt, gk)
        (dq_, dgq_), (dk_, dgk_) = vjp_q(dqt), vjp_k(dkt)
        rows = jnp.concatenate([du_t, dq_.astype(BF16), dk_.astype(BF16), dvt.astype(BF16)], axis=1)
        return (rows,), (dgq_, dgk_)

    dproj, dgq, dgk = _rw(d_proj_rows, [_from_segments(du), q_raw, dqs, k_raw, dks, dvs], [sb_gq, sb_gk],
                          [(IN_WIDTH, BF16)], [sb_gq.shape, sb_gk.shape], "d_proj", tm=512)
    g["sb_g_q"] = jnp.sum(dgq.reshape(SB_HEADS, SB_HEAD_DIM), axis=0)
    g["sb_g_k"] = jnp.sum(dgk.reshape(SB_HEADS, SB_HEAD_DIM), axis=0)
    g["w_in"] = _mm(h1, dproj, "tn", "d_w_in", tk=2048)
    dh1, got = _mm(dproj, w["w_in"], "nt", "d_h1", side=to_sibling(["w_in"]))
    add_sibling(["w_in"], got)
    dx, g["g_mix"] = _norm_bwd(x, g_mix, dh1, dx1, "d_norm_mix")

    packed = _pack_small([g[n] for n in SMALL] + [loss.reshape(1)])
    everyone, got = _all_gather(packed, "gather_small", to_chips(["w_in"]))
    keep(["w_in"], got)
    return dx, everyone, reduced


def kernel(x, mem, g_mix, w_in, ssm_a_re, ssm_a_im, ssm_log_dt, ssm_b_re, ssm_b_im, ssm_c_re, ssm_c_im, ssm_d, ssm_w_glu, sb_g_q, sb_g_k, g_out_ssm, g_out_sb, w_out, g_xa, g_mem, xa_w_q, xa_w_kv, xa_g_q, xa_g_k, xa_w_o, g_mlp, w_up, w_down, loss_target, m_g_mix, m_w_in, m_ssm_a_re, m_ssm_a_im, m_ssm_log_dt, m_ssm_b_re, m_ssm_b_im, m_ssm_c_re, m_ssm_c_im, m_ssm_d, m_ssm_w_glu, m_sb_g_q, m_sb_g_k, m_g_out_ssm, m_g_out_sb, m_w_out, m_g_xa, m_g_mem, m_xa_w_q, m_xa_w_kv, m_xa_g_q, m_xa_g_k, m_xa_w_o, m_g_mlp, m_w_up, m_w_down, v_g_mix, v_w_in, v_ssm_a_re, v_ssm_a_im, v_ssm_log_dt, v_ssm_b_re, v_ssm_b_im, v_ssm_c_re, v_ssm_c_im, v_ssm_d, v_ssm_w_glu, v_sb_g_q, v_sb_g_k, v_g_out_ssm, v_g_out_sb, v_w_out, v_g_xa, v_g_mem, v_xa_w_q, v_xa_w_kv, v_xa_g_q, v_xa_g_k, v_xa_w_o, v_g_mlp, v_w_up, v_w_down):
    given = dict(locals())
    order = ["g_mix", "w_in", "ssm_a_re", "ssm_a_im", "ssm_log_dt", "ssm_b_re", "ssm_b_im", "ssm_c_re", "ssm_c_im",
             "ssm_d", "ssm_w_glu", "sb_g_q", "sb_g_k", "g_out_ssm", "g_out_sb", "w_out", "g_xa", "g_mem", "xa_w_q",
             "xa_w_kv", "xa_g_q", "xa_g_k", "xa_w_o", "g_mlp", "w_up", "w_down"]
    assert sorted([n for n, _, _ in BIG] + SMALL) == sorted(order)
    core = lax.axis_index("c").astype(jnp.int32).reshape(1)
    chip = (2 * lax.axis_index("x") + lax.axis_index("y")).astype(jnp.int32).reshape(1)

    shards = {n: given[n][0].astype(BF16) for n, _, _ in BIG}
    sm = {n: given[n][0] for n in SMALL}
    dx, everyone, reduced = _step(x[0], mem[0], loss_target[0], shards, sm, core)

    res = {}
    for n, _, _ in BIG:
        own, recv = reduced[n]
        outs = _adam_sharded(own, recv, given[n][0], given["m_" + n][0], given["v_" + n][0], chip, "adam_" + n)
        for kind, val in zip(("grad", "delta", "new_m", "new_v"), outs):
            res[kind + "_" + n] = val[None]

    sizes = [math.prod(sm[n].shape) for n in SMALL] + [1]
    nat = lambda a: a.reshape(_natural_2d(math.prod(a.shape)))
    outs = _adam_replicated(everyone, sizes, [nat(sm[n]) for n in SMALL], [nat(given["m_" + n][0]) for n in SMALL],
                            [nat(given["v_" + n][0]) for n in SMALL], "adam_replicated")
    for i, n in enumerate(SMALL):
        for kind, val in zip(("grad", "delta", "new_m", "new_v"), outs[4 * i:4 * i + 4]):
            res[kind + "_" + n] = val.reshape(given[n].shape)
    loss_out = outs[-1][0, 0]
    return (loss_out, dx[None], *[res["grad_" + n] for n in order], *[res["delta_" + n] for n in order],
            *[res["new_m_" + n] for n in order], *[res["new_v_" + n] for n in order])
```

```python
import functools
import math

import jax
import jax.numpy as jnp
from jax import lax
from jax.experimental import pallas as pl
from jax.experimental.pallas import tpu as pltpu

F32 = jnp.float32
BF16 = jnp.bfloat16
MESH = pl.DeviceIdType.MESH

N_DEV = 8
D_MODEL = 1024
SSM_WIDTH = 512
SSM_GROUP = 16
SSM_GROUPS = 32
SSM_STATE = 64
N_STATE = SSM_GROUPS * SSM_STATE
SB_HEADS = 8
SB_HEAD_DIM = 64
SB_WIDTH = 512
IN_WIDTH = 2048
XA_HEADS = 4
XA_HEAD_DIM = 128
XA_WIDTH = 512
D_FF = 4096
NORM_EPS = 1e-6
ADAM_LR = 0.001
ADAM_B1 = 0.9
ADAM_B2 = 0.999
ADAM_EPS = 1e-08
ADAM_WD = 0.01
ADAM_STEP = 10

LANES = 128
SUBLANES = 8
VMEM_LIMIT = 56 * 1024 * 1024
SCAN_LANES = 512
SB_BLOCK = 256
SB_Q_BLOCKS = 4
SB_UNDERFLOW = -110.0

NN = (((1,), (0,)), ((), ()))
NT = (((1,), (1,)), ((), ()))
TN = (((0,), (0,)), ((), ()))


def _params(sem=None):
    return pltpu.CompilerParams(dimension_semantics=sem, vmem_limit_bytes=VMEM_LIMIT)


def _dot(a, b, dims=NN):
    return lax.dot_general(a.astype(BF16), b.astype(BF16), dims, preferred_element_type=F32)


def _rms(x, g):
    return x * lax.rsqrt(jnp.mean(x * x, axis=-1, keepdims=True) + NORM_EPS) * g


ANY = pl.BlockSpec(memory_space=pl.ANY)


class _Side:
    def __init__(self, ins, out_shapes, n_sem, make, finish=None):
        self.ins, self.out_shapes, self.n_sem, self.make = list(ins), list(out_shapes), n_sem, make
        self.finish = finish

    def sems(self):
        return [pltpu.SemaphoreType.DMA((self.n_sem,)), pltpu.SemaphoreType.DMA((self.n_sem,))]


def _hosted(body, side, n_in, n_out, grid):
    if side is None:
        return body
    ns_in, ns_out = len(side.ins), len(side.out_shapes)

    def wrapped(*refs):
        ins, refs = refs[:n_in], refs[n_in:]
        s_ins, refs = refs[:ns_in], refs[ns_in:]
        outs, refs = refs[:n_out], refs[n_out:]
        s_outs, refs = refs[:ns_out], refs[ns_out:]
        scratch, sems = refs[:-2], refs[-2:]
        ids = [pl.program_id(d) for d in range(len(grid))]
        first = functools.reduce(jnp.logical_and, [i == 0 for i in ids])
        last = functools.reduce(jnp.logical_and, [i == n - 1 for i, n in zip(ids, grid)])

        @pl.when(first)
        def _():
            for cp in side.make(s_ins, s_outs, *sems):
                cp.start()

        body(*ins, *outs, *scratch)

        @pl.when(last)
        def _():
            if side.finish is not None:
                side.finish(s_ins, s_outs, *sems)
            else:
                for cp in side.make(s_ins, s_outs, *sems):
                    cp.wait()

    return wrapped


def _side_args(side):
    if side is None:
        return [], [], [], [], []
    return ([ANY] * len(side.ins), [ANY] * len(side.out_shapes), side.out_shapes, side.sems(), side.ins)


def _split_side(res, n_out, side):
    res = list(res)
    main = res[0] if n_out == 1 else res[:n_out]
    return main if side is None else (main, res[n_out:])


def _mm(a, b, mode, name, *, epi=None, extras=(), fulls=(), out_dtypes=(F32,), sums=(), tm=1024, tn=1024, tk=1024,
        side=None):
    if mode == "nn":
        (m, k), (k2, n) = a.shape, b.shape
    elif mode == "nt":
        (m, k), (n, k2) = a.shape, b.shape
    else:
        (k, m), (k2, n) = a.shape, b.shape
    assert k == k2, (name, a.shape, b.shape)
    tm, tn, tk = min(tm, m), min(tn, n), min(tk, k)
    assert m % tm == 0 and n % tn == 0 and k % tk == 0, (name, m, n, k)
    nk = k // tk
    dims = {"nn": NN, "nt": NT, "tn": TN}[mode]
    if mode == "tn":
        a_spec = pl.BlockSpec((tk, tm), lambda i, j, kk: (kk, i))
    else:
        a_spec = pl.BlockSpec((tm, tk), lambda i, j, kk: (i, kk))
    if mode == "nt":
        b_spec = pl.BlockSpec((tn, tk), lambda i, j, kk: (j, kk))
    else:
        b_spec = pl.BlockSpec((tk, tn), lambda i, j, kk: (kk, j))
    mn_spec = pl.BlockSpec((tm, tn), lambda i, j, kk: (i, j))
    n_ex, n_full, n_out, n_sum = len(extras), len(fulls), len(out_dtypes), len(sums)
    n_in = 2 + n_ex + n_full

    def body(*refs):
        a_ref, b_ref = refs[:2]
        ex = refs[2:n_in]
        outs = refs[n_in:n_in + n_out]
        sum_refs = refs[n_in + n_out:n_in + n_out + n_sum]
        kk = pl.program_id(2)
        first_tile = jnp.logical_and(pl.program_id(0) == 0, pl.program_id(1) == 0)

        def finish(r):
            vals = epi(r, *[e[...] for e in ex]) if epi is not None else (r,)
            if n_sum:
                vals, parts = vals

                @pl.when(first_tile)
                def _():
                    for sr in sum_refs:
                        sr[...] = jnp.zeros_like(sr)

                for sr, p in zip(sum_refs, parts):
                    sr[...] += p
            for o, v in zip(outs, vals):
                o[...] = v.astype(o.dtype)

        if nk == 1:
            finish(_dot(a_ref[...], b_ref[...], dims))
        else:
            acc = refs[n_in + n_out + n_sum]

            @pl.when(kk == 0)
            def _():
                acc[...] = jnp.zeros_like(acc)

            acc[...] += _dot(a_ref[...], b_ref[...], dims)

            @pl.when(kk == nk - 1)
            def _():
                finish(acc[...])

    grid = (m // tm, n // tn, nk)
    whole = lambda shape: pl.BlockSpec(shape, lambda i, j, kk: (0,) * len(shape))
    s_in, s_out, s_shape, s_scratch, s_ops = _side_args(side)
    seq = bool(side) or n_sum > 0
    res = pl.pallas_call(
        _hosted(body, side, n_in, n_out + n_sum, grid), name=name, grid=grid,
        in_specs=[a_spec, b_spec] + [mn_spec] * n_ex + [whole(f.shape) for f in fulls] + s_in,
        out_specs=[mn_spec] * n_out + [whole(shape) for shape in sums] + s_out,
        out_shape=[jax.ShapeDtypeStruct((m, n), dt) for dt in out_dtypes]
        + [jax.ShapeDtypeStruct(shape, F32) for shape in sums] + s_shape,
        scratch_shapes=([pltpu.VMEM((tm, tn), F32)] if nk > 1 else []) + s_scratch,
        compiler_params=_params(("arbitrary",) * 3 if seq else ("parallel", "parallel", "arbitrary")),
    )(a, b, *extras, *fulls, *s_ops)
    return _split_side(res, n_out + n_sum, side)


def _row_tile(s, target):
    if s <= target:
        return s
    return max(t for t in range(16, target + 1, 16) if s % t == 0)


def _rw(fn, rows, fulls, row_out, acc_out, name, tm=1024, side=None):
    cols = [r[1:] if isinstance(r, tuple) else (r.shape[1], 0) for r in rows]
    rows = [r[0] if isinstance(r, tuple) else r for r in rows]
    s = rows[0].shape[0]
    tm = _row_tile(s, tm)
    nr, nf, nro, nao = len(rows), len(fulls), len(row_out), len(acc_out)

    def body(*refs):
        r = refs[:nr]
        f = refs[nr:nr + nf]
        ro = refs[nr + nf:nr + nf + nro]
        ao = refs[nr + nf + nro:]
        outs, accs = fn(*[x[...] for x in r], *[x[...] for x in f])
        for o, v in zip(ro, outs):
            o[...] = v.astype(o.dtype)
        if nao:
            @pl.when(pl.program_id(0) == 0)
            def _():
                for a in ao:
                    a[...] = jnp.zeros_like(a)

            for a, v in zip(ao, accs):
                a[...] += v

    full_spec = lambda shape: pl.BlockSpec(shape, lambda i: (0,) * len(shape))
    s_in, s_out, s_shape, s_scratch, s_ops = _side_args(side)
    res = pl.pallas_call(
        _hosted(body, side, nr + nf, nro + nao, (s // tm,)), name=name, grid=(s // tm,),
        in_specs=[pl.BlockSpec((tm, wd), functools.partial(lambda i, cb: (i, cb), cb=cb)) for wd, cb in cols]
        + [full_spec(x.shape) for x in fulls] + s_in,
        out_specs=[pl.BlockSpec((tm, d), lambda i: (i, 0)) for d, _ in row_out]
        + [full_spec(shape) for shape in acc_out] + s_out,
        out_shape=[jax.ShapeDtypeStruct((s, d), dt) for d, dt in row_out]
        + [jax.ShapeDtypeStruct(shape, F32) for shape in acc_out] + s_shape,
        scratch_shapes=s_scratch,
        compiler_params=_params(("arbitrary",)),
    )(*rows, *fulls, *s_ops)
    res = list(res)
    return res if side is None else (res[:nro + nao], res[nro + nao:])


def _norm_fwd(x, g, name, side=None):
    res = _rw(lambda xt, gt: ((_rms(xt, gt),), ()), [x], [g], [(x.shape[1], BF16)], [], name, side=side)
    return res[0] if side is None else (res[0][0], res[1])


def _norm_bwd(x, g, dh, dres, name, side=None):
    def fn(xt, dht, drt, gt):
        _, vjp = jax.vjp(_rms, xt, gt)
        dx, dg = vjp(dht)
        return (dx + drt,), (dg,)

    return _rw(fn, [x, dh, dres], [g], [(x.shape[1], F32)], [g.shape], name, side=side)


def _rms_groups(x, g, scale):
    lo = lax.broadcasted_iota(jnp.int32, (1, LANES), 1) < SB_HEAD_DIM
    x2 = x * x
    outs = []
    for cb in range(x.shape[1] // LANES):
        sl = slice(cb * LANES, (cb + 1) * LANES)
        s_lo = jnp.sum(jnp.where(lo, x2[:, sl], 0.0), axis=-1, keepdims=True)
        s_hi = jnp.sum(jnp.where(lo, 0.0, x2[:, sl]), axis=-1, keepdims=True)
        r = jnp.where(lo, lax.rsqrt(s_lo * (1.0 / SB_HEAD_DIM) + NORM_EPS),
                      lax.rsqrt(s_hi * (1.0 / SB_HEAD_DIM) + NORM_EPS))
        outs.append(x[:, sl] * r)
    return jnp.concatenate(outs, axis=1) * g * scale


def _log_sigmoid(z):
    return jnp.minimum(z, 0.0) - jnp.log(1.0 + jnp.exp(-jnp.abs(z)))


def _split_dot(x, u2):
    hi = x.astype(BF16)
    lo = (x - hi.astype(F32)).astype(BF16)
    return jnp.dot(jnp.concatenate([hi, lo], axis=1), u2, preferred_element_type=F32)


def _sb_consts(b):
    row = lax.broadcasted_iota(jnp.int32, (b, b), 0)
    col = lax.broadcasted_iota(jnp.int32, (b, b), 1)
    tri = col < row
    u_after = (row > col).astype(BF16)
    u_from = (row >= col).astype(BF16)
    stack = lambda u: jnp.concatenate([u, u], axis=0)
    lane_lo = lax.broadcasted_iota(jnp.int32, (b, LANES), 1) < SB_HEAD_DIM
    return tri, stack(u_after), stack(u_from), lane_lo


def _sb_scores(qh, kb, a_run, keep, u2_after, mask_l=True):
    z = lax.dot_general(qh, kb, NT, preferred_element_type=F32)
    lb = _log_sigmoid(z)
    l = lb - z
    if keep is not None and mask_l:
        l = jnp.where(keep, l, 0.0)
    w = jnp.exp(lb + (a_run + _split_dot(l, u2_after)))
    if keep is not None:
        w = jnp.where(keep, w, 0.0)
    return lb, l, w


def _sb_walk(qi, carry, step):
    def cond(state):
        n, c = state
        return jnp.logical_and(n <= qi, jnp.max(jnp.maximum(c[0], c[1])) > SB_UNDERFLOW)

    def body(state):
        n, c = state
        return n + 1, step(n, c)

    return lax.while_loop(cond, body, (jnp.int32(2), carry))[1]


def _two_heads(x, lane_lo):
    zero = jnp.zeros_like(x)
    return jnp.where(lane_lo, x, zero), jnp.where(lane_lo, zero, x)


def _sb_fwd(qs, ks, v, name, v_col=0, side=None):
    s, width = qs.shape
    b = min(SB_BLOCK, s)
    nqb = min(SB_Q_BLOCKS, s // b)

    def body(q_ref, k_ref, v_ref, o_ref):
        tri, u2_after, _, lane_lo = _sb_consts(b)
        zero = jnp.zeros((b, 1), F32)
        started = []
        for h in range(nqb):
            qi = pl.program_id(1) * nqb + h
            q_a, q_b = _two_heads(q_ref[h * b:(h + 1) * b, :], lane_lo)

            def step(n, carry, keep, mask_l=True, qi=qi, q_a=q_a, q_b=q_b):
                a_a, a_b, acc = carry
                off = pl.multiple_of(jnp.maximum(qi - n, 0) * b, b)
                kb = k_ref[pl.ds(off, b), :]
                v_a, v_b = _two_heads(v_ref[pl.ds(off, b), :].astype(BF16), lane_lo)
                _, l_a, w_a = _sb_scores(q_a, kb, a_a, keep, u2_after, mask_l)
                _, l_b, w_b = _sb_scores(q_b, kb, a_b, keep, u2_after, mask_l)
                acc = acc + jnp.dot(jnp.concatenate([w_a.astype(BF16), w_b.astype(BF16)], axis=1),
                                    jnp.concatenate([v_a, v_b], axis=0), preferred_element_type=F32)
                return (a_a + jnp.sum(l_a, axis=1, keepdims=True), a_b + jnp.sum(l_b, axis=1, keepdims=True), acc)

            carry = step(0, (zero, zero, jnp.zeros((b, LANES), F32)), tri)
            carry = step(1, carry, jnp.broadcast_to(qi > 0, tri.shape), mask_l=False)
            started.append((qi, step, carry))
        for h, (qi, step, carry) in enumerate(started):
            carry = _sb_walk(qi, carry, lambda n, c, step=step: step(n, c, None))
            o_ref[h * b:(h + 1) * b, :] = carry[2]

    blk = pl.BlockSpec((nqb * b, LANES), lambda hp, i: (i, hp))
    full = pl.BlockSpec((s, LANES), lambda hp, i: (0, hp))
    full_v = pl.BlockSpec((s, LANES), lambda hp, i: (0, hp + v_col))
    grid = (width // LANES, s // (nqb * b))
    s_in, s_out, s_shape, s_scratch, s_ops = _side_args(side)
    res = pl.pallas_call(
        _hosted(body, side, 3, 1, grid), name=name, grid=grid,
        in_specs=[blk, full, full_v] + s_in, out_specs=[blk] + s_out,
        out_shape=[jax.ShapeDtypeStruct((s, width), F32)] + s_shape, scratch_shapes=s_scratch,
        compiler_params=_params(("arbitrary", "arbitrary")),
    )(qs, ks, v, *s_ops)
    return _split_side(res, 1, side)


def _sb_bwd(qs, ks, v, out, dout, name, v_col=0, side=None):
    s, width = qs.shape
    b = min(SB_BLOCK, s)
    nqb = min(SB_Q_BLOCKS, s // b)

    def body(q_ref, k_ref, v_ref, o_ref, do_ref, dq_ref, dk_ref, dv_ref):
        @pl.when(pl.program_id(1) == 0)
        def _():
            dk_ref[...] = jnp.zeros_like(dk_ref)
            dv_ref[...] = jnp.zeros_like(dv_ref)

        tri, u2_after, u2_from, lane_lo = _sb_consts(b)
        zero = jnp.zeros((b, 1), F32)

        def head(qh, doh, kb, vb, a_run, d_rem, keep, mask_l):
            lb, l, w = _sb_scores(qh, kb, a_run, keep, u2_after, mask_l)
            wb = w.astype(BF16)
            g = lax.dot_general(doh, vb, NT, preferred_element_type=F32) * wb.astype(F32)
            g_before = d_rem - _split_dot(g, u2_from)
            dz = g - (g + g_before) * jnp.exp(lb)
            if keep is not None:
                dz = jnp.where(keep, dz, 0.0)
            return (dz.astype(BF16), wb, a_run + jnp.sum(l, axis=1, keepdims=True),
                    d_rem - jnp.sum(g, axis=1, keepdims=True))

        started = []
        for h in range(nqb):
            qi = pl.program_id(1) * nqb + h
            rows = slice(h * b, (h + 1) * b)
            q_a, q_b = _two_heads(q_ref[rows, :], lane_lo)
            dob = do_ref[rows, :].astype(BF16)
            do_a, do_b = _two_heads(dob, lane_lo)
            prod = dob.astype(F32) * o_ref[rows, :]
            d_a = jnp.sum(jnp.where(lane_lo, prod, 0.0), axis=1, keepdims=True)
            d_b = jnp.sum(jnp.where(lane_lo, 0.0, prod), axis=1, keepdims=True)
            q_rows = jnp.concatenate([q_a, q_b], axis=0)
            do_rows = jnp.concatenate([do_a, do_b], axis=0)

            def step(n, carry, keep, mask_l=True, qi=qi, q_a=q_a, q_b=q_b, do_a=do_a, do_b=do_b, q_rows=q_rows,
                     do_rows=do_rows):
                a_a, a_b, r_a, r_b, dq = carry
                off = pl.multiple_of(jnp.maximum(qi - n, 0) * b, b)
                kb = k_ref[pl.ds(off, b), :]
                vb = v_ref[pl.ds(off, b), :].astype(BF16)
                k_a, k_b = _two_heads(kb, lane_lo)
                dz_a, w_a, a_a, r_a = head(q_a, do_a, kb, vb, a_a, r_a, keep, mask_l)
                dz_b, w_b, a_b, r_b = head(q_b, do_b, kb, vb, a_b, r_b, keep, mask_l)
                dq = dq + jnp.dot(jnp.concatenate([dz_a, dz_b], axis=1), jnp.concatenate([k_a, k_b], axis=0),
                                  preferred_element_type=F32)
                dk_ref[pl.ds(off, b), :] += lax.dot_general(jnp.concatenate([dz_a, dz_b], axis=0), q_rows, TN,
                                                            preferred_element_type=F32)
                dv_ref[pl.ds(off, b), :] += lax.dot_general(jnp.concatenate([w_a, w_b], axis=0), do_rows, TN,
                                                            preferred_element_type=F32)
                return a_a, a_b, r_a, r_b, dq

            carry = step(0, (zero, zero, d_a, d_b, jnp.zeros((b, LANES), F32)), tri)
            carry = step(1, carry, jnp.broadcast_to(qi > 0, tri.shape), mask_l=False)
            started.append((qi, step, carry))
        for h, (qi, step, carry) in enumerate(started):
            carry = _sb_walk(qi, carry, lambda n, c, step=step: step(n, c, None))
            dq_ref[h * b:(h + 1) * b, :] = carry[4]

    blk = pl.BlockSpec((nqb * b, LANES), lambda hp, i: (i, hp))
    full = pl.BlockSpec((s, LANES), lambda hp, i: (0, hp))
    full_v = pl.BlockSpec((s, LANES), lambda hp, i: (0, hp + v_col))
    grid = (width // LANES, s // (nqb * b))
    s_in, s_out, s_shape, s_scratch, s_ops = _side_args(side)
    res = pl.pallas_call(
        _hosted(body, side, 5, 3, grid), name=name, grid=grid,
        in_specs=[blk, full, full_v, blk, blk] + s_in, out_specs=[blk, full, full] + s_out,
        out_shape=[jax.ShapeDtypeStruct((s, width), F32)] * 3 + s_shape,
        scratch_shapes=s_scratch,
        compiler_params=_params(("arbitrary", "arbitrary")),
    )(qs, ks, v, out, dout, *s_ops)
    return _split_side(res, 3, side)


def _cmul(xr, xi, yr, yi):
    return xr * yr - xi * yi, xr * yi + xi * yr


def _scan_consts(ar, ai, reverse, lc):
    rowi = lax.broadcasted_iota(jnp.int32, (SUBLANES, lc), 0)
    pows = [(ar, ai)]
    for _ in range(SUBLANES - 1):
        pows.append(_cmul(*pows[-1], ar, ai))
    steps = []
    for d in (1, 2, 4):
        keep = (rowi < SUBLANES - d) if reverse else (rowi >= d)
        pr, pi = pows[d - 1]
        steps.append((SUBLANES - d if reverse else d, jnp.where(keep, pr, 0.0), jnp.where(keep, pi, 0.0)))
    cr = jnp.zeros((SUBLANES, lc), F32)
    ci = jnp.zeros((SUBLANES, lc), F32)
    for r in range(SUBLANES):
        pr, pi = pows[SUBLANES - 1 - r] if reverse else pows[r]
        cr = jnp.where(rowi == r, pr, cr)
        ci = jnp.where(rowi == r, pi, ci)
    return steps, cr, ci


def _scan_tile(xr, xi, steps, pr, pi, cr, ci):
    for shift, ar, ai in steps:
        rr = pltpu.roll(xr, shift, 0)
        ri = pltpu.roll(xi, shift, 0)
        xr, xi = xr + ar * rr - ai * ri, xi + ar * ri + ai * rr
    return xr + pr * cr - pi * ci, xi + pr * ci + pi * cr


SCAN_ROWS = 1024


def _scan_chunk(s):
    tt = min(SCAN_ROWS, s)
    seg = tt // SUBLANES
    assert s % tt == 0 and seg % SUBLANES == 0 and seg & (seg - 1) == 0, s
    return tt, seg


def _to_segments(a):
    s, wd = a.shape
    tt, seg = _scan_chunk(s)
    return jnp.transpose(a.reshape(s // tt, SUBLANES, seg, wd), (0, 2, 1, 3)).reshape(s, wd)


def _from_segments(a):
    s, wd = a.shape
    tt, seg = _scan_chunk(s)
    return jnp.transpose(a.reshape(s // tt, seg, SUBLANES, wd), (0, 2, 1, 3)).reshape(s, wd)


def _cpow2(xr, xi, k):
    for _ in range(k):
        xr, xi = _cmul(xr, xi, xr, xi)
    return xr, xi


def _fill_powers(pw_ref, ar, ai, seg, lc):
    _, p8r, p8i = _scan_consts(ar, ai, False, lc)
    a8r, a8i = _cpow2(ar, ai, 3)
    qr, qi = jnp.ones_like(ar), jnp.zeros_like(ai)
    for k in range(seg // SUBLANES):
        tr, ti = _cmul(p8r, p8i, qr, qi)
        for r in range(SUBLANES):
            rows = pl.ds((SUBLANES * k + r) * SUBLANES, SUBLANES)
            pw_ref[rows, :lc] = jnp.broadcast_to(tr[r:r + 1, :], (SUBLANES, lc))
            pw_ref[rows, lc:] = jnp.broadcast_to(ti[r:r + 1, :], (SUBLANES, lc))
        qr, qi = _cmul(qr, qi, a8r, a8i)


def _ssm_fwd(u, acat, bsup, csup, d_skip, name, side=None):
    s = u.shape[0]
    lc = SCAN_LANES
    tt, seg = _scan_chunk(s)
    nl, nt = N_STATE // lc, s // tt
    tile = lambda j: pl.ds(pl.multiple_of(j * SUBLANES, SUBLANES), SUBLANES)

    def body(u_ref, a_ref, b_ref, c_ref, d_ref, s_ref, y0_ref, y1_ref, carry, pw_ref):
        ar, ai = a_ref[:, :lc], a_ref[:, lc:]

        @pl.when(pl.program_id(1) == 0)
        def _():
            carry[...] = jnp.zeros_like(carry)
            _fill_powers(pw_ref, ar, ai, seg, lc)

        ut = u_ref[...]
        s_ref[...] = _dot(ut, b_ref[0])

        ar8, ai8 = jnp.broadcast_to(ar, (SUBLANES, lc)), jnp.broadcast_to(ai, (SUBLANES, lc))

        def local(j, x):
            xr = ar8 * x[0] - ai8 * x[1] + s_ref[tile(j), :lc]
            xi = ar8 * x[1] + ai8 * x[0] + s_ref[tile(j), lc:]
            s_ref[tile(j), :lc] = xr
            s_ref[tile(j), lc:] = xi
            return xr, xi

        zero = jnp.zeros((SUBLANES, lc), F32)
        er, ei = lax.fori_loop(0, seg, local, (zero, zero))
        steps, pr, pi = _scan_consts(*_cpow2(ar, ai, seg.bit_length() - 1), False, lc)
        cr, ci = carry[:, :lc], carry[:, lc:]
        tr, ti = _scan_tile(er, ei, steps, pr, pi, cr, ci)
        rowi = lax.broadcasted_iota(jnp.int32, (SUBLANES, lc), 0)
        before_r = jnp.where(rowi == 0, cr, pltpu.roll(tr, 1, 0))
        before_i = jnp.where(rowi == 0, ci, pltpu.roll(ti, 1, 0))
        carry[:, :lc] = jnp.broadcast_to(tr[SUBLANES - 1:, :], (SUBLANES, lc))
        carry[:, lc:] = jnp.broadcast_to(ti[SUBLANES - 1:, :], (SUBLANES, lc))

        def fix(j, _):
            pwr, pwi = pw_ref[tile(j), :lc], pw_ref[tile(j), lc:]
            s_ref[tile(j), :lc] += pwr * before_r - pwi * before_i
            s_ref[tile(j), lc:] += pwr * before_i + pwi * before_r
            return 0

        lax.fori_loop(0, seg, fix, 0)
        y0 = _dot(s_ref[...], c_ref[0], NT) + d_ref[...] * ut
        y0_ref[...] = y0
        y1_ref[...] = jax.nn.gelu(y0)

    chan = pl.BlockSpec((tt, LANES), lambda j, c: (c, j))
    sup = pl.BlockSpec((1, LANES, 2 * lc), lambda j, c: (j, 0, 0))
    s_in, s_out, s_shape, s_scratch, s_ops = _side_args(side)
    res = pl.pallas_call(
        _hosted(body, side, 5, 3, (nl, nt)), name=name, grid=(nl, nt),
        in_specs=[chan, pl.BlockSpec((1, 2 * lc), lambda j, c: (0, j)), sup, sup,
                  pl.BlockSpec((1, LANES), lambda j, c: (0, j))] + s_in,
        out_specs=[pl.BlockSpec((tt, 2 * lc), lambda j, c: (c, j)), chan, chan] + s_out,
        out_shape=[jax.ShapeDtypeStruct((s, 2 * N_STATE), F32), jax.ShapeDtypeStruct((s, SSM_WIDTH), F32),
                   jax.ShapeDtypeStruct((s, SSM_WIDTH), F32)] + s_shape,
        scratch_shapes=[pltpu.VMEM((SUBLANES, 2 * lc), F32), pltpu.VMEM((seg * SUBLANES, 2 * lc), F32)] + s_scratch,
        compiler_params=_params(("arbitrary", "arbitrary")),
    )(u, acat, bsup, csup, d_skip, *s_ops)
    return _split_side(res, 3, side)


def _ssm_bwd(dy0, states, u, acat, bsup, csup, d_skip, name, side=None):
    s = u.shape[0]
    lc = SCAN_LANES
    tt, seg = _scan_chunk(s)
    nl, nt = N_STATE // lc, s // tt
    tile = lambda j: pl.ds(pl.multiple_of(j * SUBLANES, SUBLANES), SUBLANES)

    def body(dy_ref, s_ref, sp_ref, u_ref, a_ref, b_ref, c_ref, d_ref,
             du_ref, da_ref, db_ref, dc_ref, dd_ref, lam_ref, carry, pw_ref):
        c = pl.program_id(1)
        ar, ai = a_ref[:, :lc], a_ref[:, lc:]

        @pl.when(c == 0)
        def _():
            carry[...] = jnp.zeros_like(carry)
            for r in (da_ref, db_ref, dc_ref, dd_ref):
                r[...] = jnp.zeros_like(r)
            _fill_powers(pw_ref, ar, ai, seg, lc)

        dy = dy_ref[...]
        ut = u_ref[...]
        lam_ref[...] = _dot(dy, c_ref[0])

        ar8, ai8 = jnp.broadcast_to(ar, (SUBLANES, lc)), jnp.broadcast_to(ai, (SUBLANES, lc))

        def local(i, x):
            j = seg - 1 - i
            xr = ar8 * x[0] + ai8 * x[1] + lam_ref[tile(j), :lc]
            xi = ar8 * x[1] - ai8 * x[0] + lam_ref[tile(j), lc:]
            lam_ref[tile(j), :lc] = xr
            lam_ref[tile(j), lc:] = xi
            return xr, xi

        zero = jnp.zeros((SUBLANES, lc), F32)
        er, ei = lax.fori_loop(0, seg, local, (zero, zero))
        big_r, big_i = _cpow2(ar, ai, seg.bit_length() - 1)
        steps, pr, pi = _scan_consts(big_r, -big_i, True, lc)
        cr, ci = carry[:, :lc], carry[:, lc:]
        tr, ti = _scan_tile(er, ei, steps, pr, pi, cr, ci)
        rowi = lax.broadcasted_iota(jnp.int32, (SUBLANES, lc), 0)
        after_r = jnp.where(rowi == SUBLANES - 1, cr, pltpu.roll(tr, SUBLANES - 1, 0))
        after_i = jnp.where(rowi == SUBLANES - 1, ci, pltpu.roll(ti, SUBLANES - 1, 0))
        carry[:, :lc] = jnp.broadcast_to(tr[:1, :], (SUBLANES, lc))
        carry[:, lc:] = jnp.broadcast_to(ti[:1, :], (SUBLANES, lc))

        start = c != nt - 1
        last_r = jnp.where(start, jnp.broadcast_to(sp_ref[SUBLANES - 1:, :lc], (SUBLANES, lc)), 0.0)
        last_i = jnp.where(start, jnp.broadcast_to(sp_ref[SUBLANES - 1:, lc:], (SUBLANES, lc)), 0.0)
        first_r = jnp.where(rowi == 0, last_r, pltpu.roll(s_ref[tile(seg - 1), :lc], 1, 0))
        first_i = jnp.where(rowi == 0, last_i, pltpu.roll(s_ref[tile(seg - 1), lc:], 1, 0))

        def fix(j, acc):
            dar, dai = acc
            k = seg - 1 - j
            pwr, pwi = pw_ref[tile(k), :lc], pw_ref[tile(k), lc:]
            lr = lam_ref[tile(j), :lc] + pwr * after_r + pwi * after_i
            li = lam_ref[tile(j), lc:] + pwr * after_i - pwi * after_r
            lam_ref[tile(j), :lc] = lr
            lam_ref[tile(j), lc:] = li
            jp = jnp.maximum(j - 1, 0)
            sr = jnp.where(j > 0, s_ref[tile(jp), :lc], first_r)
            si = jnp.where(j > 0, s_ref[tile(jp), lc:], first_i)
            return dar + lr * sr + li * si, dai + li * sr - lr * si

        dar, dai = lax.fori_loop(0, seg, fix, (zero, zero))
        da_ref[:, :lc] += dar
        da_ref[:, lc:] += dai
        lam = lam_ref[...].astype(BF16)
        du_ref[...] = (_dot(lam, b_ref[0], NT) + d_ref[...] * dy).astype(du_ref.dtype)
        db_ref[0] += _dot(ut, lam, TN)
        dc_ref[0] += _dot(dy, s_ref[...], TN)
        dd_ref[...] += jnp.sum(dy * ut, axis=0, keepdims=True)

    rev = lambda j, c: (nt - 1 - c, j)
    chan = pl.BlockSpec((tt, LANES), rev)
    sup = pl.BlockSpec((1, LANES, 2 * lc), lambda j, c: (j, 0, 0))
    row = pl.BlockSpec((1, LANES), lambda j, c: (0, j))
    s_in, s_out, s_shape, s_scratch, s_ops = _side_args(side)
    res = pl.pallas_call(
        _hosted(body, side, 8, 5, (nl, nt)), name=name, grid=(nl, nt),
        in_specs=[chan, pl.BlockSpec((tt, 2 * lc), rev),
                  pl.BlockSpec((SUBLANES, 2 * lc), lambda j, c: (jnp.maximum((nt - 1 - c) * seg - 1, 0), j)),
                  chan, pl.BlockSpec((1, 2 * lc), lambda j, c: (0, j)), sup, sup, row] + s_in,
        out_specs=[chan, pl.BlockSpec((SUBLANES, 2 * lc), lambda j, c: (0, j)), sup, sup, row] + s_out,
        out_shape=[jax.ShapeDtypeStruct((s, SSM_WIDTH), BF16), jax.ShapeDtypeStruct((SUBLANES, 2 * N_STATE), F32),
                   jax.ShapeDtypeStruct(bsup.shape, F32), jax.ShapeDtypeStruct(csup.shape, F32),
                   jax.ShapeDtypeStruct((1, SSM_WIDTH), F32)] + s_shape,
        scratch_shapes=[pltpu.VMEM((tt, 2 * lc), F32), pltpu.VMEM((SUBLANES, 2 * lc), F32),
                        pltpu.VMEM((seg * SUBLANES, 2 * lc), F32)] + s_scratch,
        compiler_params=_params(("arbitrary", "arbitrary")),
    )(dy0, states, states, u, acat, bsup, csup, d_skip, *s_ops)
    return _split_side(res, 5, side)


def _discretise(ar, ai, ldt, br, bi):
    dt = jnp.exp(ldt)
    lr, li = ar * dt, ai * dt
    e = jnp.exp(lr)
    abar_r, abar_i = e * jnp.cos(li), e * jnp.sin(li)
    den = ar * ar + ai * ai
    coef_r = ((abar_r - 1.0) * ar + abar_i * ai) / den
    coef_i = (abar_i * ar - (abar_r - 1.0) * ai) / den
    return abar_r, abar_i, coef_r * br - coef_i * bi, coef_r * bi + coef_i * br


def _group_mask():
    shape = (LANES, SCAN_LANES)
    return (lax.broadcasted_iota(jnp.int32, shape, 0) // SSM_GROUP
            == lax.broadcasted_iota(jnp.int32, shape, 1) // SSM_STATE)


def _ssm_mats_fwd(a_re, a_im, log_dt, b_re, b_im, c_re, c_im, name):
    nl = N_STATE // SCAN_LANES
    lc = SCAN_LANES

    def body(ar, ai, ldt, br, bi, cr, ci, acat, bsup, csup):
        abar_r, abar_i, bbar_r, bbar_i = _discretise(ar[...], ai[...], ldt[...], br[...], bi[...])
        same = _group_mask()
        spread = lambda m, j: jnp.where(same, jnp.tile(m[:, j * lc:(j + 1) * lc], (LANES // SSM_GROUP, 1)), 0.0)
        c_r, c_i = cr[...], -ci[...]
        for j in range(nl):
            acat[:, 2 * j * lc:(2 * j + 1) * lc] = abar_r[:, j * lc:(j + 1) * lc]
            acat[:, (2 * j + 1) * lc:(2 * j + 2) * lc] = abar_i[:, j * lc:(j + 1) * lc]
            bsup[j, :, :lc] = spread(bbar_r, j)
            bsup[j, :, lc:] = spread(bbar_i, j)
            csup[j, :, :lc] = spread(c_r, j)
            csup[j, :, lc:] = spread(c_i, j)

    return pl.pallas_call(
        body, name=name,
        out_shape=[jax.ShapeDtypeStruct((1, 2 * N_STATE), F32), jax.ShapeDtypeStruct((nl, LANES, 2 * lc), F32),
                   jax.ShapeDtypeStruct((nl, LANES, 2 * lc), F32)],
        compiler_params=_params(),
    )(a_re, a_im, log_dt, b_re, b_im, c_re, c_im)


def _ssm_mats_bwd(a_re, a_im, log_dt, b_re, b_im, d_acat, d_bsup, d_csup, name):
    nl = N_STATE // SCAN_LANES
    lc = SCAN_LANES

    def body(ar, ai, ldt, br, bi, dac, dbs, dcs, d_ar, d_ai, d_ldt, d_br, d_bi, d_cr, d_ci):
        same = _group_mask()

        def gather(ref, j, half):
            m = jnp.where(same, ref[j, :, half * lc:(half + 1) * lc], 0.0)
            tot = m[:SSM_GROUP]
            for k in range(1, LANES // SSM_GROUP):
                tot = tot + m[k * SSM_GROUP:(k + 1) * SSM_GROUP]
            return tot

        cols = lambda ref, half: jnp.concatenate([gather(ref, j, half) for j in range(nl)], axis=1)
        d_abar_r = jnp.concatenate([dac[:, 2 * j * lc:(2 * j + 1) * lc] for j in range(nl)], axis=1)
        d_abar_i = jnp.concatenate([dac[:, (2 * j + 1) * lc:(2 * j + 2) * lc] for j in range(nl)], axis=1)
        _, vjp = jax.vjp(_discretise, ar[...], ai[...], ldt[...], br[...], bi[...])
        outs = vjp((d_abar_r, d_abar_i, cols(dbs, 0), cols(dbs, 1)))
        for ref, val in zip((d_ar, d_ai, d_ldt, d_br, d_bi), outs):
            ref[...] = val
        d_cr[...] = cols(dcs, 0)
        d_ci[...] = -cols(dcs, 1)

    row = jax.ShapeDtypeStruct((1, N_STATE), F32)
    mat = jax.ShapeDtypeStruct((SSM_GROUP, N_STATE), F32)
    return pl.pallas_call(
        body, name=name, out_shape=[row, row, row, mat, mat, mat, mat], compiler_params=_params(),
    )(a_re, a_im, log_dt, b_re, b_im, d_acat, d_bsup, d_csup)


def _states_on_lanes(sm):
    flat = lambda a: a.reshape(1, N_STATE)
    chan_b = lambda b: jnp.transpose(b, (2, 0, 1)).reshape(SSM_GROUP, N_STATE)
    chan_c = lambda c: jnp.transpose(c, (1, 0, 2)).reshape(SSM_GROUP, N_STATE)
    return (flat(sm["ssm_a_re"]), flat(sm["ssm_a_im"]), flat(jnp.repeat(sm["ssm_log_dt"], SSM_STATE)),
            chan_b(sm["ssm_b_re"]), chan_b(sm["ssm_b_im"]), chan_c(sm["ssm_c_re"]), chan_c(sm["ssm_c_im"]))


def _from_states_on_lanes(d_ar, d_ai, d_ldt, d_br, d_bi, d_cr, d_ci):
    grp = lambda a: a.reshape(SSM_GROUPS, SSM_STATE)
    back_b = lambda b: jnp.transpose(b.reshape(SSM_GROUP, SSM_GROUPS, SSM_STATE), (1, 2, 0))
    back_c = lambda c: jnp.transpose(c.reshape(SSM_GROUP, SSM_GROUPS, SSM_STATE), (1, 0, 2))
    return (grp(d_ar), grp(d_ai), jnp.sum(grp(d_ldt), axis=1), back_b(d_br), back_b(d_bi), back_c(d_cr), back_c(d_ci))


def _mem_fwd(mem, g_mem, w_kv, g_k, name):
    ml = mem.shape[0]

    def body(mem_ref, gm_ref, w_ref, gk_ref, memn_ref, kv_ref, kn_ref, vv_ref):
        memn = _rms(mem_ref[...], gm_ref[...])
        memn_ref[...] = memn.astype(BF16)
        kv = _dot(memn, w_ref[...])
        kv_ref[...] = kv
        for hh in range(XA_HEADS):
            sl = slice(hh * XA_HEAD_DIM, (hh + 1) * XA_HEAD_DIM)
            kn_ref[:, sl] = _rms(kv[:, sl], gk_ref[...]).astype(BF16)
        vv_ref[...] = kv[:, XA_WIDTH:].astype(BF16)

    return pl.pallas_call(
        body, name=name,
        out_shape=[jax.ShapeDtypeStruct((ml, D_MODEL), BF16), jax.ShapeDtypeStruct((ml, 2 * XA_WIDTH), F32),
                   jax.ShapeDtypeStruct((ml, XA_WIDTH), BF16), jax.ShapeDtypeStruct((ml, XA_WIDTH), BF16)],
        compiler_params=_params(),
    )(mem, g_mem, w_kv, g_k)


def _mem_bwd(mem, g_mem, memn, w_kv, kv, g_k, dkn, dvv, name):
    def body(mem_ref, gm_ref, memn_ref, w_ref, kv_ref, gk_ref, dkn_ref, dvv_ref, dw_ref, dgm_ref, dgk_ref):
        kv = kv_ref[...]
        dgk = jnp.zeros(dgk_ref.shape, F32)
        parts = []
        for hh in range(XA_HEADS):
            sl = slice(hh * XA_HEAD_DIM, (hh + 1) * XA_HEAD_DIM)
            _, vjp = jax.vjp(_rms, kv[:, sl], gk_ref[...])
            dk, dg = vjp(dkn_ref[:, sl])
            parts.append(dk)
            dgk = dgk + dg
        dgk_ref[...] = dgk
        dkv = jnp.concatenate(parts + [dvv_ref[...]], axis=1)
        dw_ref[...] = _dot(memn_ref[...], dkv, TN)
        dmemn = _dot(dkv, w_ref[...], NT)
        _, vjp = jax.vjp(_rms, mem_ref[...], gm_ref[...])
        dgm_ref[...] = vjp(dmemn)[1]

    return pl.pallas_call(
        body, name=name,
        out_shape=[jax.ShapeDtypeStruct((D_MODEL, 2 * XA_WIDTH), F32), jax.ShapeDtypeStruct(g_mem.shape, F32),
                   jax.ShapeDtypeStruct(g_k.shape, F32)],
        compiler_params=_params(),
    )(mem, g_mem, memn, w_kv, kv, g_k, dkn, dvv)


def _xa_head(qx_h, g_q, kn_h, vv_h):
    qn = _rms(qx_h, g_q)
    sc = _dot(qn, kn_h, NT) * (XA_HEAD_DIM ** -0.5)
    sc = sc - jnp.max(sc, axis=-1, keepdims=True)
    e = jnp.exp(sc)
    p = e / jnp.sum(e, axis=-1, keepdims=True)
    return qn, p


def _xa_fwd(qx, g_q, kn, vv, name):
    def fn(qt, gq, knt, vvt):
        outs = []
        for hh in range(XA_HEADS):
            sl = slice(hh * XA_HEAD_DIM, (hh + 1) * XA_HEAD_DIM)
            _, p = _xa_head(qt[:, sl], gq, knt[:, sl], vvt[:, sl])
            outs.append(_dot(p, vvt[:, sl]))
        return (jnp.concatenate(outs, axis=1),), ()

    return _rw(fn, [qx], [g_q, kn, vv], [(XA_WIDTH, BF16)], [], name, tm=512)[0]


def _xa_bwd(qx, g_q, kn, vv, do, name):
    def fn(qt, dot_, gq, knt, vvt):
        dqs, dks, dvs = [], [], []
        dgq = jnp.zeros_like(gq)
        for hh in range(XA_HEADS):
            sl = slice(hh * XA_HEAD_DIM, (hh + 1) * XA_HEAD_DIM)
            qn, p = _xa_head(qt[:, sl], gq, knt[:, sl], vvt[:, sl])
            doh = dot_[:, sl]
            dp = _dot(doh, vvt[:, sl], NT)
            dvs.append(_dot(p, doh, TN))
            ds = p * (dp - jnp.sum(dp * p, axis=-1, keepdims=True)) * (XA_HEAD_DIM ** -0.5)
            dqn = _dot(ds, knt[:, sl])
            dks.append(_dot(ds, qn, TN))
            _, vjp = jax.vjp(_rms, qt[:, sl], gq)
            dq, dg = vjp(dqn)
            dqs.append(dq)
            dgq = dgq + dg
        return ((jnp.concatenate(dqs, axis=1),),
                (jnp.concatenate(dks, axis=1), jnp.concatenate(dvs, axis=1), dgq))

    return _rw(fn, [qx, do], [g_q, kn, vv], [(XA_WIDTH, BF16)], [kn.shape, vv.shape, g_q.shape], name, tm=512)


BIG = [
    ("w_in", (D_MODEL, IN_WIDTH), 1), ("ssm_w_glu", (SSM_WIDTH, SSM_WIDTH), 0), ("w_out", (D_MODEL, D_MODEL), 0),
    ("xa_w_q", (D_MODEL, XA_WIDTH), 0), ("xa_w_kv", (D_MODEL, 2 * XA_WIDTH), 0), ("xa_w_o", (XA_WIDTH, D_MODEL), 1),
    ("w_up", (D_MODEL, D_FF), 1), ("w_down", (D_FF, D_MODEL), 0),
]
BIG_INDEX = {n: i for i, (n, _, _) in enumerate(BIG)}


def _shard_shape(shape, axis):
    return tuple(d // N_DEV if i == axis else d for i, d in enumerate(shape))


def _shard_of(ref, axis, d):
    n = ref.shape[axis] // N_DEV
    return ref.at[pl.ds(d * n, n), :] if axis == 0 else ref.at[:, pl.ds(d * n, n)]


def _gather_side(names, shards):
    idxs = [BIG_INDEX[n] for n in names]

    def make(ins, outs, send_sems, recv_sems):
        x, y, c = lax.axis_index("x"), lax.axis_index("y"), lax.axis_index("c")
        cps = []
        for j, i in enumerate(idxs):
            mine = _shard_of(outs[j], BIG[i][2], 4 * x + 2 * y + c)
            cps.append(pltpu.make_async_copy(ins[j], mine, send_sems.at[N_DEV * j]))
            for rel in range(1, N_DEV):
                to = tuple(1 - p if rel >> bit & 1 else p for p, bit in ((x, 2), (y, 1), (c, 0)))
                cps.append(pltpu.make_async_remote_copy(
                    src_ref=ins[j], dst_ref=mine, send_sem=send_sems.at[N_DEV * j + rel],
                    recv_sem=recv_sems.at[N_DEV * j + rel], device_id=to, device_id_type=MESH))
        return cps

    return _Side(shards, [jax.ShapeDtypeStruct(BIG[i][1], BF16) for i in idxs], N_DEV * len(idxs), make)


def _gather_two_level_side(name, shard):
    i = BIG_INDEX[name]

    def parts(ins, outs, send_sems, recv_sems):
        x, y, c = lax.axis_index("x"), lax.axis_index("y"), lax.axis_index("c")
        sibling = (x, y, 1 - c)
        chips = [(1 - x, y), (x, 1 - y), (1 - x, 1 - y)]

        def place(dev):
            return _shard_of(outs[0], BIG[i][2], 4 * dev[0] + 2 * dev[1] + dev[2])

        def copy(k, blk, to, src=None):
            return pltpu.make_async_remote_copy(
                src_ref=place(blk) if src is None else src, dst_ref=place(blk), send_sem=send_sems.at[k],
                recv_sem=recv_sems.at[k], device_id=to, device_id_type=MESH)

        mine = pltpu.make_async_copy(ins[0], place((x, y, c)), send_sems.at[7])
        first = [copy(0, (x, y, c), sibling, src=ins[0])]
        first += [copy(1 + j, (x, y, c), (*chip, c), src=ins[0]) for j, chip in enumerate(chips)]
        passed = [copy(4 + j, (*chip, c), sibling) for j, chip in enumerate(chips)]
        arrived = [copy(1 + j, (*chip, c), (x, y, c)) for j, chip in enumerate(chips)]
        from_sibling = [copy(0, sibling, (x, y, c))] + [copy(4 + j, (*chip, 1 - c), (x, y, c))
                                                       for j, chip in enumerate(chips)]
        return mine, first, passed, arrived, from_sibling

    def make(ins, outs, send_sems, recv_sems):
        mine, first, _, _, _ = parts(ins, outs, send_sems, recv_sems)
        return [mine] + first

    def finish(ins, outs, send_sems, recv_sems):
        mine, first, passed, arrived, from_sibling = parts(ins, outs, send_sems, recv_sems)
        for got, onward in zip(arrived, passed):
            got.wait_recv()
            onward.start()
        for cp in from_sibling:
            cp.wait_recv()
        for cp in first + passed:
            cp.wait_send()
        mine.wait()

    return _Side([shard], [jax.ShapeDtypeStruct(BIG[i][1], BF16)], N_DEV, make, finish)


def _sibling_side(names, grads):
    idxs = [BIG_INDEX[n] for n in names]

    def make(ins, outs, send_sems, recv_sems):
        x, y, c = lax.axis_index("x"), lax.axis_index("y"), lax.axis_index("c")
        return [pltpu.make_async_remote_copy(
            src_ref=_shard_of(ins[j], BIG[i][2], 2 * k + (1 - c)), dst_ref=outs[j].at[k],
            send_sem=send_sems.at[4 * j + k], recv_sem=recv_sems.at[4 * j + k], device_id=(x, y, 1 - c),
            device_id_type=MESH) for j, i in enumerate(idxs) for k in range(4)]

    shapes = [jax.ShapeDtypeStruct((4,) + _shard_shape(BIG[i][1], BIG[i][2]), F32) for i in idxs]
    return _Side(grads, shapes, 4 * len(idxs), make)


def _chips_side(parts):
    def make(ins, outs, send_sems, recv_sems):
        x, y, c = lax.axis_index("x"), lax.axis_index("y"), lax.axis_index("c")
        chips = [(1 - x, y), (x, 1 - y), (1 - x, 1 - y)]
        return [pltpu.make_async_remote_copy(
            src_ref=ins[j].at[2 * cx + cy], dst_ref=outs[j].at[r], send_sem=send_sems.at[3 * j + r],
            recv_sem=recv_sems.at[3 * j + r], device_id=(cx, cy, c), device_id_type=MESH)
            for r, (cx, cy) in enumerate(chips) for j in range(len(parts))]

    return _Side(parts, [jax.ShapeDtypeStruct((3,) + p.shape[1:], p.dtype) for p in parts], 3 * len(parts), make)


def _reduce_add(grad, recv, axis, core, name):
    rs, cs = recv.shape[1:]
    rt = _row_tile(rs, 256)
    nt = rs // rt

    def body(c_ref, g_ref, r_ref, p_ref, pb_ref):
        sm = g_ref[...] + r_ref[0]
        p_ref[0] = sm
        pb_ref[0] = sm.astype(BF16)

    if axis == 0:
        g_spec = pl.BlockSpec((rt, cs), lambda k, t, c_ref: ((2 * k + c_ref[0]) * nt + t, 0))
    else:
        g_spec = pl.BlockSpec((rt, cs), lambda k, t, c_ref: (t, 2 * k + c_ref[0]))
    slab = pl.BlockSpec((1, rt, cs), lambda k, t, c_ref: (k, t, 0))
    return pl.pallas_call(
        body, name=name,
        grid_spec=pltpu.PrefetchScalarGridSpec(num_scalar_prefetch=1, grid=(4, nt), in_specs=[g_spec, slab],
                                               out_specs=[slab, slab]),
        out_shape=[jax.ShapeDtypeStruct(recv.shape, F32), jax.ShapeDtypeStruct(recv.shape, BF16)],
        compiler_params=_params(("parallel", "parallel")),
    )(core, grad, recv)


def _all_gather(block, name, side):
    m_per, n = block.shape
    ns_in, ns_out = len(side.ins), len(side.out_shapes)

    def body(*refs):
        x_ref, s_ins, out_ref = refs[0], refs[1:1 + ns_in], refs[1 + ns_in]
        s_outs = refs[2 + ns_in:2 + ns_in + ns_out]
        send_sems, recv_sems, local_sem, s_send, s_recv = refs[2 + ns_in + ns_out:]
        others = side.make(s_ins, s_outs, s_send, s_recv)
        for cp in others:
            cp.start()
        x, y, c = lax.axis_index("x"), lax.axis_index("y"), lax.axis_index("c")
        me, sibling = (x, y, c), (x, y, 1 - c)
        chips = [(1 - x, y), (x, 1 - y), (1 - x, 1 - y)]

        def rows(px, py, pc):
            return out_ref.at[pl.ds((4 * px + 2 * py + pc) * m_per, m_per), :]

        def copy(k, blk, to, src=None):
            return pltpu.make_async_remote_copy(
                src_ref=rows(*blk) if src is None else src, dst_ref=rows(*blk),
                send_sem=send_sems.at[k], recv_sem=recv_sems.at[k], device_id=to, device_id_type=MESH)

        mine = pltpu.make_async_copy(x_ref, rows(*me), local_sem)
        mine.start()
        first = [copy(0, me, sibling, src=x_ref)]
        first += [copy(1 + j, me, (*chip, c), src=x_ref) for j, chip in enumerate(chips)]
        for cp in first:
            cp.start()
        passed = [copy(4 + j, (*chip, c), sibling) for j, chip in enumerate(chips)]
        for j, chip in enumerate(chips):
            copy(1 + j, (*chip, c), me).wait_recv()
            passed[j].start()
        copy(0, sibling, me).wait_recv()
        for j, chip in enumerate(chips):
            copy(4 + j, (*chip, 1 - c), me).wait_recv()
        for cp in first + passed:
            cp.wait_send()
        mine.wait()
        for cp in others:
            cp.wait()

    res = pl.pallas_call(
        body, name=name, in_specs=[ANY] * (1 + ns_in), out_specs=[ANY] * (1 + ns_out),
        out_shape=[jax.ShapeDtypeStruct((N_DEV * m_per, n), block.dtype)] + side.out_shapes,
        scratch_shapes=[pltpu.SemaphoreType.DMA((7,)), pltpu.SemaphoreType.DMA((7,)), pltpu.SemaphoreType.DMA]
        + side.sems(),
    )(block, *side.ins)
    return res[0], list(res[1:])


def _adam_math(w, g, m, v):
    m = ADAM_B1 * m + (1.0 - ADAM_B1) * g
    v = ADAM_B2 * v + (1.0 - ADAM_B2) * (g * g)
    m_hat = m / (1.0 - ADAM_B1 ** ADAM_STEP)
    v_hat = v / (1.0 - ADAM_B2 ** ADAM_STEP)
    delta = -ADAM_LR * (m_hat / (jnp.sqrt(v_hat) + ADAM_EPS) + ADAM_WD * w)
    return delta, m, v


def _adam_sharded(own, recv, w, m, v, chip, name):
    rs, cs = w.shape
    rt = _row_tile(rs, 256)

    def body(chip_ref, p_ref, r_ref, w_ref, m_ref, v_ref, g_out, d_out, m_out, v_out):
        g = p_ref[0] + r_ref[0].astype(F32) + r_ref[1].astype(F32) + r_ref[2].astype(F32)
        d, mn, vn = _adam_math(w_ref[...], g, m_ref[...], v_ref[...])
        g_out[...] = g
        d_out[...] = d
        m_out[...] = mn
        v_out[...] = vn

    tile = pl.BlockSpec((rt, cs), lambda t, chip_ref: (t, 0))
    return pl.pallas_call(
        body, name=name,
        grid_spec=pltpu.PrefetchScalarGridSpec(
            num_scalar_prefetch=1, grid=(rs // rt,),
            in_specs=[pl.BlockSpec((1, rt, cs), lambda t, chip_ref: (chip_ref[0], t, 0)),
                      pl.BlockSpec((3, rt, cs), lambda t, chip_ref: (0, t, 0)), tile, tile, tile],
            out_specs=[tile] * 4),
        out_shape=[jax.ShapeDtypeStruct((rs, cs), F32)] * 4,
        compiler_params=_params(("parallel",)),
    )(chip, own, recv, w, m, v)


SMALL = ["g_mix", "ssm_a_re", "ssm_a_im", "ssm_log_dt", "ssm_b_re", "ssm_b_im", "ssm_c_re", "ssm_c_im", "ssm_d",
         "sb_g_q", "sb_g_k", "g_out_ssm", "g_out_sb", "g_xa", "g_mem", "xa_g_q", "xa_g_k", "g_mlp"]
PACK_TILE = SUBLANES * LANES


def _natural_2d(n):
    return (n // LANES, LANES) if n % LANES == 0 else (1, n)


def _pack_small(arrs):
    parts = []
    for a in arrs:
        flat = a.reshape(-1)
        parts.append(jnp.pad(flat, (0, (-flat.shape[0]) % PACK_TILE)))
    return jnp.concatenate(parts).reshape(-1, LANES)


def _adam_replicated(gathered, sizes, ws, ms, vs, name):
    n_w = len(ws)
    r_dev = gathered.shape[0] // N_DEV
    offs, off = [], 0
    for n in sizes:
        offs.append(off)
        off += (n + PACK_TILE - 1) // PACK_TILE * SUBLANES
    assert off == r_dev

    def body(*refs):
        g_ref = refs[0]
        w_refs, m_refs, v_refs = refs[1:1 + n_w], refs[1 + n_w:1 + 2 * n_w], refs[1 + 2 * n_w:1 + 3 * n_w]
        outs = refs[1 + 3 * n_w:]

        def total(i, shape):
            r, cdim = shape
            acc = g_ref[pl.ds(offs[i], r), :cdim]
            for d in range(1, N_DEV):
                acc = acc + g_ref[pl.ds(d * r_dev + offs[i], r), :cdim]
            return acc

        for i in range(n_w):
            g = total(i, w_refs[i].shape)
            d, mn, vn = _adam_math(w_refs[i][...], g, m_refs[i][...], v_refs[i][...])
            for o, val in zip(outs[4 * i:4 * i + 4], (g, d, mn, vn)):
                o[...] = val
        outs[4 * n_w][...] = total(n_w, (SUBLANES, LANES))

    shapes = [w.shape for w in ws]
    return pl.pallas_call(
        body, name=name,
        out_shape=[jax.ShapeDtypeStruct(shp, F32) for shp in shapes for _ in range(4)]
        + [jax.ShapeDtypeStruct((SUBLANES, LANES), F32)],
        compiler_params=_params(),
    )(gathered, *ws, *ms, *vs)


def _step(x, mem, target, shards, sm, core):
    g, w, sums, reduced = {}, {}, {}, {}

    def gather(names):
        return _gather_side(names, [shards[n] for n in names])

    def to_sibling(names):
        return _sibling_side(names, [g[n] for n in names])

    def add_sibling(names, received):
        for n, r in zip(names, received):
            sums[n] = _reduce_add(g[n], r, BIG[BIG_INDEX[n]][2], core, "reduce_add_" + n)

    def to_chips(names):
        return _chips_side([sums[n][1] for n in names])

    def keep(names, received):
        for n, r in zip(names, received):
            reduced[n] = (sums[n][0], r)

    row = lambda a: a.reshape(1, -1)
    g_mix, g_xa, g_mlp, g_mem = row(sm["g_mix"]), row(sm["g_xa"]), row(sm["g_mlp"]), row(sm["g_mem"])
    g_os, g_ob = row(sm["g_out_ssm"]), row(sm["g_out_sb"])
    sb_gq, sb_gk = jnp.tile(row(sm["sb_g_q"]), (1, SB_HEADS)), jnp.tile(row(sm["sb_g_k"]), (1, SB_HEADS))
    xa_gq, xa_gk = row(sm["xa_g_q"]), row(sm["xa_g_k"])
    d_skip = row(sm["ssm_d"])

    h1, (w["w_in"],) = _norm_fwd(x, g_mix, "norm_mix", side=_gather_two_level_side("w_in", shards["w_in"]))
    proj = _mm(h1, w["w_in"], "nn", "in_proj", tn=IN_WIDTH)
    u = _to_segments(proj[:, :SSM_WIDTH])
    q_raw, k_raw = (proj, SB_WIDTH, 1), (proj, SB_WIDTH, 2)
    v_col = (SSM_WIDTH + 2 * SB_WIDTH) // LANES
    sb_scale = SB_HEAD_DIM ** -0.5
    qs, ks = _rw(lambda qt, kt, gq, gk: ((_rms_groups(qt, gq, sb_scale), _rms_groups(kt, gk, 1.0)), ()),
                 [q_raw, k_raw], [sb_gq, sb_gk], [(SB_WIDTH, BF16)] * 2, [], "sb_qk_norm")
    early = ["ssm_w_glu", "w_out", "xa_w_q", "xa_w_kv", "xa_w_o", "w_up"]
    y_sb, got = _sb_fwd(qs, ks, proj, "sb_fwd", v_col=v_col, side=gather(early))
    w.update(zip(early, got))

    ssm_args = _states_on_lanes(sm)
    acat, bsup, csup = _ssm_mats_fwd(*ssm_args, "ssm_mats")
    (states, y0, y1), (w["w_down"],) = _ssm_fwd(u, acat, bsup, csup, d_skip, "ssm_fwd", side=gather(["w_down"]))
    z_glu, y_ssm = _mm(y1, w["ssm_w_glu"], "nn", "ssm_glu", epi=lambda r, yt: (r, yt * jax.nn.sigmoid(r)),
                       extras=(y1,), out_dtypes=(F32, F32))
    y_ssm = _from_segments(y_ssm)

    def cat_norm(a, b, ga, gb):
        return jnp.concatenate([_rms(a, ga), _rms(b, gb)], axis=1)

    ycat = _rw(lambda a, b, ga, gb: ((cat_norm(a, b, ga, gb),), ()), [y_ssm, y_sb], [g_os, g_ob],
               [(D_MODEL, BF16)], [], "norm_out")[0]

    def residual_norm_epi(r, xt, gt):
        xn = r + xt
        return xn, _rms(xn, gt)

    x1, h2 = _mm(ycat, w["w_out"], "nn", "out_proj", epi=residual_norm_epi, extras=(x,), fulls=(g_xa,),
                 out_dtypes=(F32, BF16))
    qx = _mm(h2, w["xa_w_q"], "nn", "xa_q")
    memn, kv, kn_x, vv_x = _mem_fwd(mem, g_mem, w["xa_w_kv"], xa_gk, "xa_mem")
    o_xa = _xa_fwd(qx, xa_gq, kn_x, vv_x, "xa_fwd")
    x2, h3 = _mm(o_xa, w["xa_w_o"], "nn", "xa_o", epi=residual_norm_epi, extras=(x1,), fulls=(g_mlp,),
                 out_dtypes=(F32, BF16))

    def up_epi(r):
        rl = jnp.maximum(r, 0.0)
        return (rl * rl,)

    r_up = _mm(h3, w["w_up"], "nn", "mlp_up", epi=up_epi, out_dtypes=(BF16,), tm=2048, tn=2048)

    def loss_epi(r, xt, tt):
        d = r + xt - tt
        return (d * (1.0 / D_MODEL),) * 2, (jnp.sum(d * d, axis=0, keepdims=True),)

    dx3, dx3_b, sq = _mm(r_up, w["w_down"], "nn", "mlp_down", epi=loss_epi, extras=(x2, target),
                         out_dtypes=(F32, BF16), sums=[(1, D_MODEL)])
    loss = jnp.sum(sq) * (0.5 / D_MODEL)

    def norm_bwd_epi(r, xt, drt, gt):
        _, vjp = jax.vjp(_rms, xt, gt)
        dx_, dg_ = vjp(r)
        return (dx_ + drt,) * 2, (dg_,)

    g["w_down"] = _mm(r_up, dx3_b, "tn", "d_w_down", tk=2048)
    da = _mm(dx3_b, w["w_down"], "nt", "d_r", epi=lambda r, rt: (r * 2.0 * jnp.sqrt(rt.astype(F32)),), extras=(r_up,),
             out_dtypes=(BF16,), tn=2048)
    g["w_up"] = _mm(h3, da, "tn", "d_w_up", tk=2048)
    mlp = ["w_down", "w_up"]
    (dx2, dx2_b, g["g_mlp"]), got = _mm(da, w["w_up"], "nt", "d_h3", epi=norm_bwd_epi, extras=(x2, dx3),
                                        fulls=(g_mlp,), out_dtypes=(F32, BF16), sums=[g_mlp.shape],
                                        side=to_sibling(mlp))
    add_sibling(mlp, got)
    g["xa_w_o"] = _mm(o_xa, dx2_b, "tn", "d_xa_w_o", tk=2048)
    do_xa = _mm(dx2_b, w["xa_w_o"], "nt", "d_o_xa")
    dqx, dkn_x, dvv_x, g["xa_g_q"] = _xa_bwd(qx, xa_gq, kn_x, vv_x, do_xa, "xa_bwd")
    g["xa_w_kv"], g["g_mem"], g["xa_g_k"] = _mem_bwd(mem, g_mem, memn, w["xa_w_kv"], kv, xa_gk, dkn_x, dvv_x,
                                                     "xa_mem_bwd")
    g["xa_w_q"] = _mm(h2, dqx, "tn", "d_xa_w_q", tk=2048)
    dx1, dx1_b, g["g_xa"] = _mm(dqx, w["xa_w_q"], "nt", "d_h2", epi=norm_bwd_epi, extras=(x1, dx2), fulls=(g_xa,),
                                out_dtypes=(F32, BF16), sums=[g_xa.shape])
    g["w_out"] = _mm(ycat, dx1_b, "tn", "d_w_out", tk=2048)
    dycat = _mm(dx1_b, w["w_out"], "nt", "d_ycat")

    def cat_bwd(a, b, dy, ga, gb):
        _, vjp = jax.vjp(cat_norm, a, b, ga, gb)
        da_, db_, dga, dgb = vjp(dy)
        return (da_, db_), (dga, dgb)

    dy_ssm, dy_sb, g["g_out_ssm"], g["g_out_sb"] = _rw(
        cat_bwd, [y_ssm, y_sb, dycat], [g_os, g_ob], [(SSM_WIDTH, F32), (SB_WIDTH, F32)], [g_os.shape, g_ob.shape],
        "d_norm_out", tm=512)

    def glu_bwd(dy, yt, zt):
        sg = jax.nn.sigmoid(zt)
        return (dy * sg, dy * yt * sg * (1.0 - sg)), ()

    dy1_a, dz = _rw(glu_bwd, [_to_segments(dy_ssm), y1, z_glu], [], [(SSM_WIDTH, F32), (SSM_WIDTH, BF16)], [], "d_glu")
    g["ssm_w_glu"] = _mm(y1, dz, "tn", "d_w_glu", tk=2048)

    def gelu_bwd_epi(r, da_, y0t):
        _, vjp = jax.vjp(jax.nn.gelu, y0t)
        return (vjp(r + da_)[0],)

    mid = ["w_out", "xa_w_q", "xa_w_kv", "xa_w_o", "ssm_w_glu"]
    dy0, got = _mm(dz, w["ssm_w_glu"], "nt", "d_y1", epi=gelu_bwd_epi, extras=(dy1_a, y0), side=to_sibling(mid))
    add_sibling(mid, got)
    (du, da8, d_bsup, d_csup, g["ssm_d"]), got = _ssm_bwd(dy0, states, u, acat, bsup, csup, d_skip, "ssm_bwd",
                                                          side=to_chips(mlp))
    keep(mlp, got)
    d_acat = jnp.sum(da8, axis=0, keepdims=True)
    d_mats = _ssm_mats_bwd(*ssm_args[:5], d_acat, d_bsup, d_csup, "ssm_mats_bwd")
    for nm, val in zip(("ssm_a_re", "ssm_a_im", "ssm_log_dt", "ssm_b_re", "ssm_b_im", "ssm_c_re", "ssm_c_im"),
                       _from_states_on_lanes(*d_mats)):
        g[nm] = val

    (dqs, dks, dvs), got = _sb_bwd(qs, ks, proj, y_sb, dy_sb, "sb_bwd", v_col=v_col, side=to_chips(mid))
    keep(mid, got)

    def d_proj_rows(du_t, qt, dqt, kt, dkt, dvt, gq, gk):
        _, vjp_q = jax.vjp(lambda a, b_: _rms_groups(a, b_, sb_scale), qt, gq)
        _, vjp_k = jax.vjp(lambda a, b_: _rms_groups(a, b_, 1.0), kt, gk)
        (dq_, dgq_), (dk_, dgk_) = vjp_q(dqt), vjp_k(dkt)
        rows = jnp.concatenate([du_t, dq_.astype(BF16), dk_.astype(BF16), dvt.astype(BF16)], axis=1)
        return (rows,), (dgq_, dgk_)

    dproj, dgq, dgk = _rw(d_proj_rows, [_from_segments(du), q_raw, dqs, k_raw, dks, dvs], [sb_gq, sb_gk],
                          [(IN_WIDTH, BF16)], [sb_gq.shape, sb_gk.shape], "d_proj", tm=512)
    g["sb_g_q"] = jnp.sum(dgq.reshape(SB_HEADS, SB_HEAD_DIM), axis=0)
    g["sb_g_k"] = jnp.sum(dgk.reshape(SB_HEADS, SB_HEAD_DIM), axis=0)
    g["w_in"] = _mm(h1, dproj, "tn", "d_w_in", tn=IN_WIDTH)
    dh1, got = _mm(dproj, w["w_in"], "nt", "d_h1", tk=IN_WIDTH, side=to_sibling(["w_in"]))
    add_sibling(["w_in"], got)
    dx, g["g_mix"] = _norm_bwd(x, g_mix, dh1, dx1, "d_norm_mix")

    packed = _pack_small([g[n] for n in SMALL] + [loss.reshape(1)])
    everyone, got = _all_gather(packed, "gather_small", to_chips(["w_in"]))
    keep(["w_in"], got)
    return dx, everyone, reduced


def kernel(x, mem, g_mix, w_in, ssm_a_re, ssm_a_im, ssm_log_dt, ssm_b_re, ssm_b_im, ssm_c_re, ssm_c_im, ssm_d, ssm_w_glu, sb_g_q, sb_g_k, g_out_ssm, g_out_sb, w_out, g_xa, g_mem, xa_w_q, xa_w_kv, xa_g_q, xa_g_k, xa_w_o, g_mlp, w_up, w_down, loss_target, m_g_mix, m_w_in, m_ssm_a_re, m_ssm_a_im, m_ssm_log_dt, m_ssm_b_re, m_ssm_b_im, m_ssm_c_re, m_ssm_c_im, m_ssm_d, m_ssm_w_glu, m_sb_g_q, m_sb_g_k, m_g_out_ssm, m_g_out_sb, m_w_out, m_g_xa, m_g_mem, m_xa_w_q, m_xa_w_kv, m_xa_g_q, m_xa_g_k, m_xa_w_o, m_g_mlp, m_w_up, m_w_down, v_g_mix, v_w_in, v_ssm_a_re, v_ssm_a_im, v_ssm_log_dt, v_ssm_b_re, v_ssm_b_im, v_ssm_c_re, v_ssm_c_im, v_ssm_d, v_ssm_w_glu, v_sb_g_q, v_sb_g_k, v_g_out_ssm, v_g_out_sb, v_w_out, v_g_xa, v_g_mem, v_xa_w_q, v_xa_w_kv, v_xa_g_q, v_xa_g_k, v_xa_w_o, v_g_mlp, v_w_up, v_w_down):
    given = dict(locals())
    order = ["g_mix", "w_in", "ssm_a_re", "ssm_a_im", "ssm_log_dt", "ssm_b_re", "ssm_b_im", "ssm_c_re", "ssm_c_im",
             "ssm_d", "ssm_w_glu", "sb_g_q", "sb_g_k", "g_out_ssm", "g_out_sb", "w_out", "g_xa", "g_mem", "xa_w_q",
             "xa_w_kv", "xa_g_q", "xa_g_k", "xa_w_o", "g_mlp", "w_up", "w_down"]
    assert sorted([n for n, _, _ in BIG] + SMALL) == sorted(order)
    core = lax.axis_index("c").astype(jnp.int32).reshape(1)
    chip = (2 * lax.axis_index("x") + lax.axis_index("y")).astype(jnp.int32).reshape(1)

    shards = {n: given[n][0].astype(BF16) for n, _, _ in BIG}
    sm = {n: given[n][0] for n in SMALL}
    dx, everyone, reduced = _step(x[0], mem[0], loss_target[0], shards, sm, core)

    res = {}
    for n, _, _ in BIG:
        own, recv = reduced[n]
        outs = _adam_sharded(own, recv, given[n][0], given["m_" + n][0], given["v_" + n][0], chip, "adam_" + n)
        for kind, val in zip(("grad", "delta", "new_m", "new_v"), outs):
            res[kind + "_" + n] = val[None]

    sizes = [math.prod(sm[n].shape) for n in SMALL] + [1]
    nat = lambda a: a.reshape(_natural_2d(math.prod(a.shape)))
    outs = _adam_replicated(everyone, sizes, [nat(sm[n]) for n in SMALL], [nat(given["m_" + n][0]) for n in SMALL],
                            [nat(given["v_" + n][0]) for n in SMALL], "adam_replicated")
    for i, n in enumerate(SMALL):
        for kind, val in zip(("grad", "delta", "new_m", "new_v"), outs[4 * i:4 * i + 4]):
            res[kind + "_" + n] = val.reshape(given[n].shape)
    loss_out = outs[-1][0, 0]
    return (loss_out, dx[None], *[res["grad_" + n] for n in order], *[res["delta_" + n] for n in order],
            *[res["new_m_" + n] for n in order], *[res["new_v_" + n] for n in order])
```

```python
import functools
import math

import jax
import jax.numpy as jnp
from jax import lax
from jax.experimental import pallas as pl
from jax.experimental.pallas import tpu as pltpu

F32 = jnp.float32
BF16 = jnp.bfloat16
MESH = pl.DeviceIdType.MESH

N_DEV = 8
D_MODEL = 1024
SSM_WIDTH = 512
SSM_GROUP = 16
SSM_GROUPS = 32
SSM_STATE = 64
N_STATE = SSM_GROUPS * SSM_STATE
SB_HEADS = 8
SB_HEAD_DIM = 64
SB_WIDTH = 512
IN_WIDTH = 2048
XA_HEADS = 4
XA_HEAD_DIM = 128
XA_WIDTH = 512
D_FF = 4096
NORM_EPS = 1e-6
ADAM_LR = 0.001
ADAM_B1 = 0.9
ADAM_B2 = 0.999
ADAM_EPS = 1e-08
ADAM_WD = 0.01
ADAM_STEP = 10

LANES = 128
SUBLANES = 8
VMEM_LIMIT = 56 * 1024 * 1024
SCAN_LANES = 512
SB_BLOCK = 256
SB_Q_BLOCKS = 8
SB_UNDERFLOW = -110.0

NN = (((1,), (0,)), ((), ()))
NT = (((1,), (1,)), ((), ()))
TN = (((0,), (0,)), ((), ()))


def _params(sem=None):
    return pltpu.CompilerParams(dimension_semantics=sem, vmem_limit_bytes=VMEM_LIMIT)


def _dot(a, b, dims=NN):
    return lax.dot_general(a.astype(BF16), b.astype(BF16), dims, preferred_element_type=F32)


def _rms(x, g):
    return x * lax.rsqrt(jnp.mean(x * x, axis=-1, keepdims=True) + NORM_EPS) * g


ANY = pl.BlockSpec(memory_space=pl.ANY)


class _Side:
    def __init__(self, ins, out_shapes, n_sem, make, finish=None):
        self.ins, self.out_shapes, self.n_sem, self.make = list(ins), list(out_shapes), n_sem, make
        self.finish = finish

    def sems(self):
        return [pltpu.SemaphoreType.DMA((self.n_sem,)), pltpu.SemaphoreType.DMA((self.n_sem,))]


def _hosted(body, side, n_in, n_out, grid):
    if side is None:
        return body
    ns_in, ns_out = len(side.ins), len(side.out_shapes)

    def wrapped(*refs):
        ins, refs = refs[:n_in], refs[n_in:]
        s_ins, refs = refs[:ns_in], refs[ns_in:]
        outs, refs = refs[:n_out], refs[n_out:]
        s_outs, refs = refs[:ns_out], refs[ns_out:]
        scratch, sems = refs[:-2], refs[-2:]
        ids = [pl.program_id(d) for d in range(len(grid))]
        first = functools.reduce(jnp.logical_and, [i == 0 for i in ids])
        last = functools.reduce(jnp.logical_and, [i == n - 1 for i, n in zip(ids, grid)])

        @pl.when(first)
        def _():
            for cp in side.make(s_ins, s_outs, *sems):
                cp.start()

        body(*ins, *outs, *scratch)

        @pl.when(last)
        def _():
            if side.finish is not None:
                side.finish(s_ins, s_outs, *sems)
            else:
                for cp in side.make(s_ins, s_outs, *sems):
                    cp.wait()

    return wrapped


def _side_args(side):
    if side is None:
        return [], [], [], [], []
    return ([ANY] * len(side.ins), [ANY] * len(side.out_shapes), side.out_shapes, side.sems(), side.ins)


def _split_side(res, n_out, side):
    res = list(res)
    main = res[0] if n_out == 1 else res[:n_out]
    return main if side is None else (main, res[n_out:])


def _mm(a, b, mode, name, *, epi=None, extras=(), fulls=(), out_dtypes=(F32,), sums=(), tm=1024, tn=1024, tk=1024,
        side=None):
    if mode == "nn":
        (m, k), (k2, n) = a.shape, b.shape
    elif mode == "nt":
        (m, k), (n, k2) = a.shape, b.shape
    else:
        (k, m), (k2, n) = a.shape, b.shape
    assert k == k2, (name, a.shape, b.shape)
    tm, tn, tk = min(tm, m), min(tn, n), min(tk, k)
    assert m % tm == 0 and n % tn == 0 and k % tk == 0, (name, m, n, k)
    nk = k // tk
    dims = {"nn": NN, "nt": NT, "tn": TN}[mode]
    if mode == "tn":
        a_spec = pl.BlockSpec((tk, tm), lambda i, j, kk: (kk, i))
    else:
        a_spec = pl.BlockSpec((tm, tk), lambda i, j, kk: (i, kk))
    if mode == "nt":
        b_spec = pl.BlockSpec((tn, tk), lambda i, j, kk: (j, kk))
    else:
        b_spec = pl.BlockSpec((tk, tn), lambda i, j, kk: (kk, j))
    mn_spec = pl.BlockSpec((tm, tn), lambda i, j, kk: (i, j))
    n_ex, n_full, n_out, n_sum = len(extras), len(fulls), len(out_dtypes), len(sums)
    n_in = 2 + n_ex + n_full

    def body(*refs):
        a_ref, b_ref = refs[:2]
        ex = refs[2:n_in]
        outs = refs[n_in:n_in + n_out]
        sum_refs = refs[n_in + n_out:n_in + n_out + n_sum]
        kk = pl.program_id(2)
        first_tile = jnp.logical_and(pl.program_id(0) == 0, pl.program_id(1) == 0)

        def finish(r):
            vals = epi(r, *[e[...] for e in ex]) if epi is not None else (r,)
            if n_sum:
                vals, parts = vals

                @pl.when(first_tile)
                def _():
                    for sr in sum_refs:
                        sr[...] = jnp.zeros_like(sr)

                for sr, p in zip(sum_refs, parts):
                    sr[...] += p
            for o, v in zip(outs, vals):
                o[...] = v.astype(o.dtype)

        if nk == 1:
            finish(_dot(a_ref[...], b_ref[...], dims))
        else:
            acc = refs[n_in + n_out + n_sum]

            @pl.when(kk == 0)
            def _():
                acc[...] = jnp.zeros_like(acc)

            acc[...] += _dot(a_ref[...], b_ref[...], dims)

            @pl.when(kk == nk - 1)
            def _():
                finish(acc[...])

    grid = (m // tm, n // tn, nk)
    whole = lambda shape: pl.BlockSpec(shape, lambda i, j, kk: (0,) * len(shape))
    s_in, s_out, s_shape, s_scratch, s_ops = _side_args(side)
    seq = bool(side) or n_sum > 0
    res = pl.pallas_call(
        _hosted(body, side, n_in, n_out + n_sum, grid), name=name, grid=grid,
        in_specs=[a_spec, b_spec] + [mn_spec] * n_ex + [whole(f.shape) for f in fulls] + s_in,
        out_specs=[mn_spec] * n_out + [whole(shape) for shape in sums] + s_out,
        out_shape=[jax.ShapeDtypeStruct((m, n), dt) for dt in out_dtypes]
        + [jax.ShapeDtypeStruct(shape, F32) for shape in sums] + s_shape,
        scratch_shapes=([pltpu.VMEM((tm, tn), F32)] if nk > 1 else []) + s_scratch,
        compiler_params=_params(("arbitrary",) * 3 if seq else ("parallel", "parallel", "arbitrary")),
    )(a, b, *extras, *fulls, *s_ops)
    return _split_side(res, n_out + n_sum, side)


def _row_tile(s, target):
    if s <= target:
        return s
    return max(t for t in range(16, target + 1, 16) if s % t == 0)


def _rw(fn, rows, fulls, row_out, acc_out, name, tm=1024, side=None):
    cols = [r[1:] if isinstance(r, tuple) else (r.shape[1], 0) for r in rows]
    rows = [r[0] if isinstance(r, tuple) else r for r in rows]
    s = rows[0].shape[0]
    tm = _row_tile(s, tm)
    nr, nf, nro, nao = len(rows), len(fulls), len(row_out), len(acc_out)

    def body(*refs):
        r = refs[:nr]
        f = refs[nr:nr + nf]
        ro = refs[nr + nf:nr + nf + nro]
        ao = refs[nr + nf + nro:]
        outs, accs = fn(*[x[...] for x in r], *[x[...] for x in f])
        for o, v in zip(ro, outs):
            o[...] = v.astype(o.dtype)
        if nao:
            @pl.when(pl.program_id(0) == 0)
            def _():
                for a in ao:
                    a[...] = jnp.zeros_like(a)

            for a, v in zip(ao, accs):
                a[...] += v

    full_spec = lambda shape: pl.BlockSpec(shape, lambda i: (0,) * len(shape))
    s_in, s_out, s_shape, s_scratch, s_ops = _side_args(side)
    res = pl.pallas_call(
        _hosted(body, side, nr + nf, nro + nao, (s // tm,)), name=name, grid=(s // tm,),
        in_specs=[pl.BlockSpec((tm, wd), functools.partial(lambda i, cb: (i, cb), cb=cb)) for wd, cb in cols]
        + [full_spec(x.shape) for x in fulls] + s_in,
        out_specs=[pl.BlockSpec((tm, d), lambda i: (i, 0)) for d, _ in row_out]
        + [full_spec(shape) for shape in acc_out] + s_out,
        out_shape=[jax.ShapeDtypeStruct((s, d), dt) for d, dt in row_out]
        + [jax.ShapeDtypeStruct(shape, F32) for shape in acc_out] + s_shape,
        scratch_shapes=s_scratch,
        compiler_params=_params(("arbitrary",)),
    )(*rows, *fulls, *s_ops)
    res = list(res)
    return res if side is None else (res[:nro + nao], res[nro + nao:])


def _norm_fwd(x, g, name, side=None):
    res = _rw(lambda xt, gt: ((_rms(xt, gt),), ()), [x], [g], [(x.shape[1], BF16)], [], name, side=side)
    return res[0] if side is None else (res[0][0], res[1])


def _norm_bwd(x, g, dh, dres, name, side=None):
    def fn(xt, dht, drt, gt):
        _, vjp = jax.vjp(_rms, xt, gt)
        dx, dg = vjp(dht)
        return (dx + drt,), (dg,)

    return _rw(fn, [x, dh, dres], [g], [(x.shape[1], F32)], [g.shape], name, side=side)


def _rms_groups(x, g, scale):
    lo = lax.broadcasted_iota(jnp.int32, (1, LANES), 1) < SB_HEAD_DIM
    x2 = x * x
    outs = []
    for cb in range(x.shape[1] // LANES):
        sl = slice(cb * LANES, (cb + 1) * LANES)
        s_lo = jnp.sum(jnp.where(lo, x2[:, sl], 0.0), axis=-1, keepdims=True)
        s_hi = jnp.sum(jnp.where(lo, 0.0, x2[:, sl]), axis=-1, keepdims=True)
        r = jnp.where(lo, lax.rsqrt(s_lo * (1.0 / SB_HEAD_DIM) + NORM_EPS),
                      lax.rsqrt(s_hi * (1.0 / SB_HEAD_DIM) + NORM_EPS))
        outs.append(x[:, sl] * r)
    return jnp.concatenate(outs, axis=1) * g * scale


def _log_sigmoid(z):
    return jnp.minimum(z, 0.0) - jnp.log(1.0 + jnp.exp(-jnp.abs(z)))


def _split_dot(x, u2):
    hi = x.astype(BF16)
    lo = (x - hi.astype(F32)).astype(BF16)
    return jnp.dot(jnp.concatenate([hi, lo], axis=1), u2, preferred_element_type=F32)


def _sb_consts(b):
    row = lax.broadcasted_iota(jnp.int32, (b, b), 0)
    col = lax.broadcasted_iota(jnp.int32, (b, b), 1)
    tri = col < row
    u_after = (row > col).astype(BF16)
    u_from = (row >= col).astype(BF16)
    stack = lambda u: jnp.concatenate([u, u], axis=0)
    lane_lo = lax.broadcasted_iota(jnp.int32, (b, LANES), 1) < SB_HEAD_DIM
    return tri, stack(u_after), stack(u_from), lane_lo


def _sb_scores(qh, kb, a_run, keep, u2_after, mask_l=True):
    z = lax.dot_general(qh, kb, NT, preferred_element_type=F32)
    lb = _log_sigmoid(z)
    l = lb - z
    if keep is not None and mask_l:
        l = jnp.where(keep, l, 0.0)
    w = jnp.exp(lb + (a_run + _split_dot(l, u2_after)))
    if keep is not None:
        w = jnp.where(keep, w, 0.0)
    return lb, l, w


def _sb_walk(qi, carry, step):
    def cond(state):
        n, c = state
        return jnp.logical_and(n <= qi, jnp.max(jnp.maximum(c[0], c[1])) > SB_UNDERFLOW)

    def body(state):
        n, c = state
        return n + 1, step(n, c)

    return lax.while_loop(cond, body, (jnp.int32(2), carry))[1]


def _two_heads(x, lane_lo):
    zero = jnp.zeros_like(x)
    return jnp.where(lane_lo, x, zero), jnp.where(lane_lo, zero, x)


def _sb_fwd(qs, ks, v, name, v_col=0, side=None):
    s, width = qs.shape
    b = min(SB_BLOCK, s)
    nqb = min(SB_Q_BLOCKS, s // b)

    def body(q_ref, k_ref, v_ref, o_ref):
        tri, u2_after, _, lane_lo = _sb_consts(b)
        zero = jnp.zeros((b, 1), F32)
        started = []
        for h in range(nqb):
            qi = pl.program_id(1) * nqb + h
            q_a, q_b = _two_heads(q_ref[h * b:(h + 1) * b, :], lane_lo)

            def step(n, carry, keep, mask_l=True, qi=qi, q_a=q_a, q_b=q_b):
                a_a, a_b, acc = carry
                off = pl.multiple_of(jnp.maximum(qi - n, 0) * b, b)
                kb = k_ref[pl.ds(off, b), :]
                v_a, v_b = _two_heads(v_ref[pl.ds(off, b), :].astype(BF16), lane_lo)
                _, l_a, w_a = _sb_scores(q_a, kb, a_a, keep, u2_after, mask_l)
                _, l_b, w_b = _sb_scores(q_b, kb, a_b, keep, u2_after, mask_l)
                acc = acc + jnp.dot(jnp.concatenate([w_a.astype(BF16), w_b.astype(BF16)], axis=1),
                                    jnp.concatenate([v_a, v_b], axis=0), preferred_element_type=F32)
                return (a_a + jnp.sum(l_a, axis=1, keepdims=True), a_b + jnp.sum(l_b, axis=1, keepdims=True), acc)

            carry = step(0, (zero, zero, jnp.zeros((b, LANES), F32)), tri)
            carry = step(1, carry, jnp.broadcast_to(qi > 0, tri.shape), mask_l=False)
            started.append((qi, step, carry))
        for h, (qi, step, carry) in enumerate(started):
            carry = _sb_walk(qi, carry, lambda n, c, step=step: step(n, c, None))
            o_ref[h * b:(h + 1) * b, :] = carry[2]

    blk = pl.BlockSpec((nqb * b, LANES), lambda hp, i: (i, hp))
    full = pl.BlockSpec((s, LANES), lambda hp, i: (0, hp))
    full_v = pl.BlockSpec((s, LANES), lambda hp, i: (0, hp + v_col))
    grid = (width // LANES, s // (nqb * b))
    s_in, s_out, s_shape, s_scratch, s_ops = _side_args(side)
    res = pl.pallas_call(
        _hosted(body, side, 3, 1, grid), name=name, grid=grid,
        in_specs=[blk, full, full_v] + s_in, out_specs=[blk] + s_out,
        out_shape=[jax.ShapeDtypeStruct((s, width), F32)] + s_shape, scratch_shapes=s_scratch,
        compiler_params=_params(("arbitrary", "arbitrary")),
    )(qs, ks, v, *s_ops)
    return _split_side(res, 1, side)


def _sb_bwd(qs, ks, v, out, dout, name, v_col=0, side=None):
    s, width = qs.shape
    b = min(SB_BLOCK, s)
    nqb = min(SB_Q_BLOCKS, s // b)

    def body(q_ref, k_ref, v_ref, o_ref, do_ref, dq_ref, dk_ref, dv_ref):
        @pl.when(pl.program_id(1) == 0)
        def _():
            dk_ref[...] = jnp.zeros_like(dk_ref)
            dv_ref[...] = jnp.zeros_like(dv_ref)

        tri, u2_after, u2_from, lane_lo = _sb_consts(b)
        zero = jnp.zeros((b, 1), F32)

        def head(qh, doh, kb, vb, a_run, d_rem, keep, mask_l):
            lb, l, w = _sb_scores(qh, kb, a_run, keep, u2_after, mask_l)
            wb = w.astype(BF16)
            g = lax.dot_general(doh, vb, NT, preferred_element_type=F32) * wb.astype(F32)
            g_before = d_rem - _split_dot(g, u2_from)
            dz = g - (g + g_before) * jnp.exp(lb)
            if keep is not None:
                dz = jnp.where(keep, dz, 0.0)
            return (dz.astype(BF16), wb, a_run + jnp.sum(l, axis=1, keepdims=True),
                    d_rem - jnp.sum(g, axis=1, keepdims=True))

        started = []
        for h in range(nqb):
            qi = pl.program_id(1) * nqb + h
            rows = slice(h * b, (h + 1) * b)
            q_a, q_b = _two_heads(q_ref[rows, :], lane_lo)
            dob = do_ref[rows, :].astype(BF16)
            do_a, do_b = _two_heads(dob, lane_lo)
            prod = dob.astype(F32) * o_ref[rows, :]
            d_a = jnp.sum(jnp.where(lane_lo, prod, 0.0), axis=1, keepdims=True)
            d_b = jnp.sum(jnp.where(lane_lo, 0.0, prod), axis=1, keepdims=True)
            q_rows = jnp.concatenate([q_a, q_b], axis=0)
            do_rows = jnp.concatenate([do_a, do_b], axis=0)

            def step(n, carry, keep, mask_l=True, qi=qi, q_a=q_a, q_b=q_b, do_a=do_a, do_b=do_b, q_rows=q_rows,
                     do_rows=do_rows):
                a_a, a_b, r_a, r_b, dq = carry
                off = pl.multiple_of(jnp.maximum(qi - n, 0) * b, b)
                kb = k_ref[pl.ds(off, b), :]
                vb = v_ref[pl.ds(off, b), :].astype(BF16)
                k_a, k_b = _two_heads(kb, lane_lo)
                dz_a, w_a, a_a, r_a = head(q_a, do_a, kb, vb, a_a, r_a, keep, mask_l)
                dz_b, w_b, a_b, r_b = head(q_b, do_b, kb, vb, a_b, r_b, keep, mask_l)
                dq = dq + jnp.dot(jnp.concatenate([dz_a, dz_b], axis=1), jnp.concatenate([k_a, k_b], axis=0),
                                  preferred_element_type=F32)
                dk_ref[pl.ds(off, b), :] += lax.dot_general(jnp.concatenate([dz_a, dz_b], axis=0), q_rows, TN,
                                                            preferred_element_type=F32)
                dv_ref[pl.ds(off, b), :] += lax.dot_general(jnp.concatenate([w_a, w_b], axis=0), do_rows, TN,
                                                            preferred_element_type=F32)
                return a_a, a_b, r_a, r_b, dq

            carry = step(0, (zero, zero, d_a, d_b, jnp.zeros((b, LANES), F32)), tri)
            carry = step(1, carry, jnp.broadcast_to(qi > 0, tri.shape), mask_l=False)
            started.append((qi, step, carry))
        for h, (qi, step, carry) in enumerate(started):
            carry = _sb_walk(qi, carry, lambda n, c, step=step: step(n, c, None))
            dq_ref[h * b:(h + 1) * b, :] = carry[4]

    blk = pl.BlockSpec((nqb * b, LANES), lambda hp, i: (i, hp))
    full = pl.BlockSpec((s, LANES), lambda hp, i: (0, hp))
    full_v = pl.BlockSpec((s, LANES), lambda hp, i: (0, hp + v_col))
    grid = (width // LANES, s // (nqb * b))
    s_in, s_out, s_shape, s_scratch, s_ops = _side_args(side)
    res = pl.pallas_call(
        _hosted(body, side, 5, 3, grid), name=name, grid=grid,
        in_specs=[blk, full, full_v, blk, blk] + s_in, out_specs=[blk, full, full] + s_out,
        out_shape=[jax.ShapeDtypeStruct((s, width), F32)] * 3 + s_shape,
        scratch_shapes=s_scratch,
        compiler_params=_params(("arbitrary", "arbitrary")),
    )(qs, ks, v, out, dout, *s_ops)
    return _split_side(res, 3, side)


def _cmul(xr, xi, yr, yi):
    return xr * yr - xi * yi, xr * yi + xi * yr


def _scan_consts(ar, ai, reverse, lc):
    rowi = lax.broadcasted_iota(jnp.int32, (SUBLANES, lc), 0)
    pows = [(ar, ai)]
    for _ in range(SUBLANES - 1):
        pows.append(_cmul(*pows[-1], ar, ai))
    steps = []
    for d in (1, 2, 4):
        keep = (rowi < SUBLANES - d) if reverse else (rowi >= d)
        pr, pi = pows[d - 1]
        steps.append((SUBLANES - d if reverse else d, jnp.where(keep, pr, 0.0), jnp.where(keep, pi, 0.0)))
    cr = jnp.zeros((SUBLANES, lc), F32)
    ci = jnp.zeros((SUBLANES, lc), F32)
    for r in range(SUBLANES):
        pr, pi = pows[SUBLANES - 1 - r] if reverse else pows[r]
        cr = jnp.where(rowi == r, pr, cr)
        ci = jnp.where(rowi == r, pi, ci)
    return steps, cr, ci


def _scan_tile(xr, xi, steps, pr, pi, cr, ci):
    for shift, ar, ai in steps:
        rr = pltpu.roll(xr, shift, 0)
        ri = pltpu.roll(xi, shift, 0)
        xr, xi = xr + ar * rr - ai * ri, xi + ar * ri + ai * rr
    return xr + pr * cr - pi * ci, xi + pr * ci + pi * cr


SCAN_ROWS = 1024


def _scan_chunk(s):
    tt = min(SCAN_ROWS, s)
    seg = tt // SUBLANES
    assert s % tt == 0 and seg % SUBLANES == 0 and seg & (seg - 1) == 0, s
    return tt, seg


def _to_segments(a):
    s, wd = a.shape
    tt, seg = _scan_chunk(s)
    return jnp.transpose(a.reshape(s // tt, SUBLANES, seg, wd), (0, 2, 1, 3)).reshape(s, wd)


def _from_segments(a):
    s, wd = a.shape
    tt, seg = _scan_chunk(s)
    return jnp.transpose(a.reshape(s // tt, seg, SUBLANES, wd), (0, 2, 1, 3)).reshape(s, wd)


def _cpow2(xr, xi, k):
    for _ in range(k):
        xr, xi = _cmul(xr, xi, xr, xi)
    return xr, xi


def _fill_powers(pw_ref, ar, ai, seg, lc):
    _, p8r, p8i = _scan_consts(ar, ai, False, lc)
    a8r, a8i = _cpow2(ar, ai, 3)
    qr, qi = jnp.ones_like(ar), jnp.zeros_like(ai)
    for k in range(seg // SUBLANES):
        tr, ti = _cmul(p8r, p8i, qr, qi)
        for r in range(SUBLANES):
            rows = pl.ds((SUBLANES * k + r) * SUBLANES, SUBLANES)
            pw_ref[rows, :lc] = jnp.broadcast_to(tr[r:r + 1, :], (SUBLANES, lc))
            pw_ref[rows, lc:] = jnp.broadcast_to(ti[r:r + 1, :], (SUBLANES, lc))
        qr, qi = _cmul(qr, qi, a8r, a8i)


def _ssm_fwd(u, acat, bsup, csup, d_skip, name, side=None):
    s = u.shape[0]
    lc = SCAN_LANES
    tt, seg = _scan_chunk(s)
    nl, nt = N_STATE // lc, s // tt
    tile = lambda j: pl.ds(pl.multiple_of(j * SUBLANES, SUBLANES), SUBLANES)

    def body(u_ref, a_ref, b_ref, c_ref, d_ref, s_ref, y0_ref, y1_ref, carry, pw_ref):
        ar, ai = a_ref[:, :lc], a_ref[:, lc:]

        @pl.when(pl.program_id(1) == 0)
        def _():
            carry[...] = jnp.zeros_like(carry)
            _fill_powers(pw_ref, ar, ai, seg, lc)

        ut = u_ref[...]
        s_ref[...] = _dot(ut, b_ref[0])

        ar8, ai8 = jnp.broadcast_to(ar, (SUBLANES, lc)), jnp.broadcast_to(ai, (SUBLANES, lc))

        def local(j, x):
            xr = ar8 * x[0] - ai8 * x[1] + s_ref[tile(j), :lc]
            xi = ar8 * x[1] + ai8 * x[0] + s_ref[tile(j), lc:]
            s_ref[tile(j), :lc] = xr
            s_ref[tile(j), lc:] = xi
            return xr, xi

        zero = jnp.zeros((SUBLANES, lc), F32)
        er, ei = lax.fori_loop(0, seg, local, (zero, zero), unroll=4)
        steps, pr, pi = _scan_consts(*_cpow2(ar, ai, seg.bit_length() - 1), False, lc)
        cr, ci = carry[:, :lc], carry[:, lc:]
        tr, ti = _scan_tile(er, ei, steps, pr, pi, cr, ci)
        rowi = lax.broadcasted_iota(jnp.int32, (SUBLANES, lc), 0)
        before_r = jnp.where(rowi == 0, cr, pltpu.roll(tr, 1, 0))
        before_i = jnp.where(rowi == 0, ci, pltpu.roll(ti, 1, 0))
        carry[:, :lc] = jnp.broadcast_to(tr[SUBLANES - 1:, :], (SUBLANES, lc))
        carry[:, lc:] = jnp.broadcast_to(ti[SUBLANES - 1:, :], (SUBLANES, lc))

        def fix(j, _):
            pwr, pwi = pw_ref[tile(j), :lc], pw_ref[tile(j), lc:]
            s_ref[tile(j), :lc] += pwr * before_r - pwi * before_i
            s_ref[tile(j), lc:] += pwr * before_i + pwi * before_r
            return 0

        lax.fori_loop(0, seg, fix, 0, unroll=4)
        y0 = _dot(s_ref[...], c_ref[0], NT) + d_ref[...] * ut
        y0_ref[...] = y0
        y1_ref[...] = jax.nn.gelu(y0)

    chan = pl.BlockSpec((tt, LANES), lambda j, c: (c, j))
    sup = pl.BlockSpec((1, LANES, 2 * lc), lambda j, c: (j, 0, 0))
    s_in, s_out, s_shape, s_scratch, s_ops = _side_args(side)
    res = pl.pallas_call(
        _hosted(body, side, 5, 3, (nl, nt)), name=name, grid=(nl, nt),
        in_specs=[chan, pl.BlockSpec((1, 2 * lc), lambda j, c: (0, j)), sup, sup,
                  pl.BlockSpec((1, LANES), lambda j, c: (0, j))] + s_in,
        out_specs=[pl.BlockSpec((tt, 2 * lc), lambda j, c: (c, j)), chan, chan] + s_out,
        out_shape=[jax.ShapeDtypeStruct((s, 2 * N_STATE), F32), jax.ShapeDtypeStruct((s, SSM_WIDTH), F32),
                   jax.ShapeDtypeStruct((s, SSM_WIDTH), F32)] + s_shape,
        scratch_shapes=[pltpu.VMEM((SUBLANES, 2 * lc), F32), pltpu.VMEM((seg * SUBLANES, 2 * lc), F32)] + s_scratch,
        compiler_params=_params(("arbitrary", "arbitrary")),
    )(u, acat, bsup, csup, d_skip, *s_ops)
    return _split_side(res, 3, side)


def _ssm_bwd(dy0, states, u, acat, bsup, csup, d_skip, name, side=None):
    s = u.shape[0]
    lc = SCAN_LANES
    tt, seg = _scan_chunk(s)
    nl, nt = N_STATE // lc, s // tt
    tile = lambda j: pl.ds(pl.multiple_of(j * SUBLANES, SUBLANES), SUBLANES)

    def body(dy_ref, s_ref, sp_ref, u_ref, a_ref, b_ref, c_ref, d_ref,
             du_ref, da_ref, db_ref, dc_ref, dd_ref, lam_ref, carry, pw_ref):
        c = pl.program_id(1)
        ar, ai = a_ref[:, :lc], a_ref[:, lc:]

        @pl.when(c == 0)
        def _():
            carry[...] = jnp.zeros_like(carry)
            for r in (da_ref, db_ref, dc_ref, dd_ref):
                r[...] = jnp.zeros_like(r)
            _fill_powers(pw_ref, ar, ai, seg, lc)

        dy = dy_ref[...]
        ut = u_ref[...]
        lam_ref[...] = _dot(dy, c_ref[0])

        ar8, ai8 = jnp.broadcast_to(ar, (SUBLANES, lc)), jnp.broadcast_to(ai, (SUBLANES, lc))

        def local(i, x):
            j = seg - 1 - i
            xr = ar8 * x[0] + ai8 * x[1] + lam_ref[tile(j), :lc]
            xi = ar8 * x[1] - ai8 * x[0] + lam_ref[tile(j), lc:]
            lam_ref[tile(j), :lc] = xr
            lam_ref[tile(j), lc:] = xi
            return xr, xi

        zero = jnp.zeros((SUBLANES, lc), F32)
        er, ei = lax.fori_loop(0, seg, local, (zero, zero), unroll=4)
        big_r, big_i = _cpow2(ar, ai, seg.bit_length() - 1)
        steps, pr, pi = _scan_consts(big_r, -big_i, True, lc)
        cr, ci = carry[:, :lc], carry[:, lc:]
        tr, ti = _scan_tile(er, ei, steps, pr, pi, cr, ci)
        rowi = lax.broadcasted_iota(jnp.int32, (SUBLANES, lc), 0)
        after_r = jnp.where(rowi == SUBLANES - 1, cr, pltpu.roll(tr, SUBLANES - 1, 0))
        after_i = jnp.where(rowi == SUBLANES - 1, ci, pltpu.roll(ti, SUBLANES - 1, 0))
        carry[:, :lc] = jnp.broadcast_to(tr[:1, :], (SUBLANES, lc))
        carry[:, lc:] = jnp.broadcast_to(ti[:1, :], (SUBLANES, lc))

        start = c != nt - 1
        last_r = jnp.where(start, jnp.broadcast_to(sp_ref[SUBLANES - 1:, :lc], (SUBLANES, lc)), 0.0)
        last_i = jnp.where(start, jnp.broadcast_to(sp_ref[SUBLANES - 1:, lc:], (SUBLANES, lc)), 0.0)
        first_r = jnp.where(rowi == 0, last_r, pltpu.roll(s_ref[tile(seg - 1), :lc], 1, 0))
        first_i = jnp.where(rowi == 0, last_i, pltpu.roll(s_ref[tile(seg - 1), lc:], 1, 0))

        def fix(j, acc):
            dar, dai = acc
            k = seg - 1 - j
            pwr, pwi = pw_ref[tile(k), :lc], pw_ref[tile(k), lc:]
            lr = lam_ref[tile(j), :lc] + pwr * after_r + pwi * after_i
            li = lam_ref[tile(j), lc:] + pwr * after_i - pwi * after_r
            lam_ref[tile(j), :lc] = lr
            lam_ref[tile(j), lc:] = li
            jp = jnp.maximum(j - 1, 0)
            sr = jnp.where(j > 0, s_ref[tile(jp), :lc], first_r)
            si = jnp.where(j > 0, s_ref[tile(jp), lc:], first_i)
            return dar + lr * sr + li * si, dai + li * sr - lr * si

        dar, dai = lax.fori_loop(0, seg, fix, (zero, zero), unroll=4)
        da_ref[:, :lc] += dar
        da_ref[:, lc:] += dai
        lam = lam_ref[...].astype(BF16)
        du_ref[...] = (_dot(lam, b_ref[0], NT) + d_ref[...] * dy).astype(du_ref.dtype)
        db_ref[0] += _dot(ut, lam, TN)
        dc_ref[0] += _dot(dy, s_ref[...], TN)
        dd_ref[...] += jnp.sum(dy * ut, axis=0, keepdims=True)

    rev = lambda j, c: (nt - 1 - c, j)
    chan = pl.BlockSpec((tt, LANES), rev)
    sup = pl.BlockSpec((1, LANES, 2 * lc), lambda j, c: (j, 0, 0))
    row = pl.BlockSpec((1, LANES), lambda j, c: (0, j))
    s_in, s_out, s_shape, s_scratch, s_ops = _side_args(side)
    res = pl.pallas_call(
        _hosted(body, side, 8, 5, (nl, nt)), name=name, grid=(nl, nt),
        in_specs=[chan, pl.BlockSpec((tt, 2 * lc), rev),
                  pl.BlockSpec((SUBLANES, 2 * lc), lambda j, c: (jnp.maximum((nt - 1 - c) * seg - 1, 0), j)),
                  chan, pl.BlockSpec((1, 2 * lc), lambda j, c: (0, j)), sup, sup, row] + s_in,
        out_specs=[chan, pl.BlockSpec((SUBLANES, 2 * lc), lambda j, c: (0, j)), sup, sup, row] + s_out,
        out_shape=[jax.ShapeDtypeStruct((s, SSM_WIDTH), BF16), jax.ShapeDtypeStruct((SUBLANES, 2 * N_STATE), F32),
                   jax.ShapeDtypeStruct(bsup.shape, F32), jax.ShapeDtypeStruct(csup.shape, F32),
                   jax.ShapeDtypeStruct((1, SSM_WIDTH), F32)] + s_shape,
        scratch_shapes=[pltpu.VMEM((tt, 2 * lc), F32), pltpu.VMEM((SUBLANES, 2 * lc), F32),
                        pltpu.VMEM((seg * SUBLANES, 2 * lc), F32)] + s_scratch,
        compiler_params=_params(("arbitrary", "arbitrary")),
    )(dy0, states, states, u, acat, bsup, csup, d_skip, *s_ops)
    return _split_side(res, 5, side)


def _discretise(ar, ai, ldt, br, bi):
    dt = jnp.exp(ldt)
    lr, li = ar * dt, ai * dt
    e = jnp.exp(lr)
    abar_r, abar_i = e * jnp.cos(li), e * jnp.sin(li)
    den = ar * ar + ai * ai
    coef_r = ((abar_r - 1.0) * ar + abar_i * ai) / den
    coef_i = (abar_i * ar - (abar_r - 1.0) * ai) / den
    return abar_r, abar_i, coef_r * br - coef_i * bi, coef_r * bi + coef_i * br


def _group_mask():
    shape = (LANES, SCAN_LANES)
    return (lax.broadcasted_iota(jnp.int32, shape, 0) // SSM_GROUP
            == lax.broadcasted_iota(jnp.int32, shape, 1) // SSM_STATE)


def _ssm_mats_fwd(a_re, a_im, log_dt, b_re, b_im, c_re, c_im, name):
    nl = N_STATE // SCAN_LANES
    lc = SCAN_LANES

    def body(ar, ai, ldt, br, bi, cr, ci, acat, bsup, csup):
        abar_r, abar_i, bbar_r, bbar_i = _discretise(ar[...], ai[...], ldt[...], br[...], bi[...])
        same = _group_mask()
        spread = lambda m, j: jnp.where(same, jnp.tile(m[:, j * lc:(j + 1) * lc], (LANES // SSM_GROUP, 1)), 0.0)
        c_r, c_i = cr[...], -ci[...]
        for j in range(nl):
            acat[:, 2 * j * lc:(2 * j + 1) * lc] = abar_r[:, j * lc:(j + 1) * lc]
            acat[:, (2 * j + 1) * lc:(2 * j + 2) * lc] = abar_i[:, j * lc:(j + 1) * lc]
            bsup[j, :, :lc] = spread(bbar_r, j)
            bsup[j, :, lc:] = spread(bbar_i, j)
            csup[j, :, :lc] = spread(c_r, j)
            csup[j, :, lc:] = spread(c_i, j)

    return pl.pallas_call(
        body, name=name,
        out_shape=[jax.ShapeDtypeStruct((1, 2 * N_STATE), F32), jax.ShapeDtypeStruct((nl, LANES, 2 * lc), F32),
                   jax.ShapeDtypeStruct((nl, LANES, 2 * lc), F32)],
        compiler_params=_params(),
    )(a_re, a_im, log_dt, b_re, b_im, c_re, c_im)


def _ssm_mats_bwd(a_re, a_im, log_dt, b_re, b_im, d_acat, d_bsup, d_csup, name):
    nl = N_STATE // SCAN_LANES
    lc = SCAN_LANES

    def body(ar, ai, ldt, br, bi, dac, dbs, dcs, d_ar, d_ai, d_ldt, d_br, d_bi, d_cr, d_ci):
        same = _group_mask()

        def gather(ref, j, half):
            m = jnp.where(same, ref[j, :, half * lc:(half + 1) * lc], 0.0)
            tot = m[:SSM_GROUP]
            for k in range(1, LANES // SSM_GROUP):
                tot = tot + m[k * SSM_GROUP:(k + 1) * SSM_GROUP]
            return tot

        cols = lambda ref, half: jnp.concatenate([gather(ref, j, half) for j in range(nl)], axis=1)
        d_abar_r = jnp.concatenate([dac[:, 2 * j * lc:(2 * j + 1) * lc] for j in range(nl)], axis=1)
        d_abar_i = jnp.concatenate([dac[:, (2 * j + 1) * lc:(2 * j + 2) * lc] for j in range(nl)], axis=1)
        _, vjp = jax.vjp(_discretise, ar[...], ai[...], ldt[...], br[...], bi[...])
        outs = vjp((d_abar_r, d_abar_i, cols(dbs, 0), cols(dbs, 1)))
        for ref, val in zip((d_ar, d_ai, d_ldt, d_br, d_bi), outs):
            ref[...] = val
        d_cr[...] = cols(dcs, 0)
        d_ci[...] = -cols(dcs, 1)

    row = jax.ShapeDtypeStruct((1, N_STATE), F32)
    mat = jax.ShapeDtypeStruct((SSM_GROUP, N_STATE), F32)
    return pl.pallas_call(
        body, name=name, out_shape=[row, row, row, mat, mat, mat, mat], compiler_params=_params(),
    )(a_re, a_im, log_dt, b_re, b_im, d_acat, d_bsup, d_csup)


def _states_on_lanes(sm):
    flat = lambda a: a.reshape(1, N_STATE)
    chan_b = lambda b: jnp.transpose(b, (2, 0, 1)).reshape(SSM_GROUP, N_STATE)
    chan_c = lambda c: jnp.transpose(c, (1, 0, 2)).reshape(SSM_GROUP, N_STATE)
    return (flat(sm["ssm_a_re"]), flat(sm["ssm_a_im"]), flat(jnp.repeat(sm["ssm_log_dt"], SSM_STATE)),
            chan_b(sm["ssm_b_re"]), chan_b(sm["ssm_b_im"]), chan_c(sm["ssm_c_re"]), chan_c(sm["ssm_c_im"]))


def _from_states_on_lanes(d_ar, d_ai, d_ldt, d_br, d_bi, d_cr, d_ci):
    grp = lambda a: a.reshape(SSM_GROUPS, SSM_STATE)
    back_b = lambda b: jnp.transpose(b.reshape(SSM_GROUP, SSM_GROUPS, SSM_STATE), (1, 2, 0))
    back_c = lambda c: jnp.transpose(c.reshape(SSM_GROUP, SSM_GROUPS, SSM_STATE), (1, 0, 2))
    return (grp(d_ar), grp(d_ai), jnp.sum(grp(d_ldt), axis=1), back_b(d_br), back_b(d_bi), back_c(d_cr), back_c(d_ci))


def _mem_fwd(mem, g_mem, w_kv, g_k, name):
    ml = mem.shape[0]

    def body(mem_ref, gm_ref, w_ref, gk_ref, memn_ref, kv_ref, kn_ref, vv_ref):
        memn = _rms(mem_ref[...], gm_ref[...])
        memn_ref[...] = memn.astype(BF16)
        kv = _dot(memn, w_ref[...])
        kv_ref[...] = kv
        for hh in range(XA_HEADS):
            sl = slice(hh * XA_HEAD_DIM, (hh + 1) * XA_HEAD_DIM)
            kn_ref[:, sl] = _rms(kv[:, sl], gk_ref[...]).astype(BF16)
        vv_ref[...] = kv[:, XA_WIDTH:].astype(BF16)

    return pl.pallas_call(
        body, name=name,
        out_shape=[jax.ShapeDtypeStruct((ml, D_MODEL), BF16), jax.ShapeDtypeStruct((ml, 2 * XA_WIDTH), F32),
                   jax.ShapeDtypeStruct((ml, XA_WIDTH), BF16), jax.ShapeDtypeStruct((ml, XA_WIDTH), BF16)],
        compiler_params=_params(),
    )(mem, g_mem, w_kv, g_k)


def _mem_bwd(mem, g_mem, memn, w_kv, kv, g_k, dkn, dvv, name):
    def body(mem_ref, gm_ref, memn_ref, w_ref, kv_ref, gk_ref, dkn_ref, dvv_ref, dw_ref, dgm_ref, dgk_ref):
        kv = kv_ref[...]
        dgk = jnp.zeros(dgk_ref.shape, F32)
        parts = []
        for hh in range(XA_HEADS):
            sl = slice(hh * XA_HEAD_DIM, (hh + 1) * XA_HEAD_DIM)
            _, vjp = jax.vjp(_rms, kv[:, sl], gk_ref[...])
            dk, dg = vjp(dkn_ref[:, sl])
            parts.append(dk)
            dgk = dgk + dg
        dgk_ref[...] = dgk
        dkv = jnp.concatenate(parts + [dvv_ref[...]], axis=1)
        dw_ref[...] = _dot(memn_ref[...], dkv, TN)
        dmemn = _dot(dkv, w_ref[...], NT)
        _, vjp = jax.vjp(_rms, mem_ref[...], gm_ref[...])
        dgm_ref[...] = vjp(dmemn)[1]

    return pl.pallas_call(
        body, name=name,
        out_shape=[jax.ShapeDtypeStruct((D_MODEL, 2 * XA_WIDTH), F32), jax.ShapeDtypeStruct(g_mem.shape, F32),
                   jax.ShapeDtypeStruct(g_k.shape, F32)],
        compiler_params=_params(),
    )(mem, g_mem, memn, w_kv, kv, g_k, dkn, dvv)


def _xa_head(qx_h, g_q, kn_h, vv_h):
    qn = _rms(qx_h, g_q)
    sc = _dot(qn, kn_h, NT) * (XA_HEAD_DIM ** -0.5)
    sc = sc - jnp.max(sc, axis=-1, keepdims=True)
    e = jnp.exp(sc)
    p = e / jnp.sum(e, axis=-1, keepdims=True)
    return qn, p


def _xa_fwd(qx, g_q, kn, vv, name):
    def fn(qt, gq, knt, vvt):
        outs = []
        for hh in range(XA_HEADS):
            sl = slice(hh * XA_HEAD_DIM, (hh + 1) * XA_HEAD_DIM)
            _, p = _xa_head(qt[:, sl], gq, knt[:, sl], vvt[:, sl])
            outs.append(_dot(p, vvt[:, sl]))
        return (jnp.concatenate(outs, axis=1),), ()

    return _rw(fn, [qx], [g_q, kn, vv], [(XA_WIDTH, BF16)], [], name, tm=512)[0]


def _xa_bwd(qx, g_q, kn, vv, do, name):
    def fn(qt, dot_, gq, knt, vvt):
        dqs, dks, dvs = [], [], []
        dgq = jnp.zeros_like(gq)
        for hh in range(XA_HEADS):
            sl = slice(hh * XA_HEAD_DIM, (hh + 1) * XA_HEAD_DIM)
            qn, p = _xa_head(qt[:, sl], gq, knt[:, sl], vvt[:, sl])
            doh = dot_[:, sl]
            dp = _dot(doh, vvt[:, sl], NT)
            dvs.append(_dot(p, doh, TN))
            ds = p * (dp - jnp.sum(dp * p, axis=-1, keepdims=True)) * (XA_HEAD_DIM ** -0.5)
            dqn = _dot(ds, knt[:, sl])
            dks.append(_dot(ds, qn, TN))
            _, vjp = jax.vjp(_rms, qt[:, sl], gq)
            dq, dg = vjp(dqn)
            dqs.append(dq)
            dgq = dgq + dg
        return ((jnp.concatenate(dqs, axis=1),),
                (jnp.concatenate(dks, axis=1), jnp.concatenate(dvs, axis=1), dgq))

    return _rw(fn, [qx, do], [g_q, kn, vv], [(XA_WIDTH, BF16)], [kn.shape, vv.shape, g_q.shape], name, tm=512)


BIG = [
    ("w_in", (D_MODEL, IN_WIDTH), 1), ("ssm_w_glu", (SSM_WIDTH, SSM_WIDTH), 0), ("w_out", (D_MODEL, D_MODEL), 0),
    ("xa_w_q", (D_MODEL, XA_WIDTH), 0), ("xa_w_kv", (D_MODEL, 2 * XA_WIDTH), 0), ("xa_w_o", (XA_WIDTH, D_MODEL), 1),
    ("w_up", (D_MODEL, D_FF), 1), ("w_down", (D_FF, D_MODEL), 0),
]
BIG_INDEX = {n: i for i, (n, _, _) in enumerate(BIG)}


def _shard_shape(shape, axis):
    return tuple(d // N_DEV if i == axis else d for i, d in enumerate(shape))


def _shard_of(ref, axis, d):
    n = ref.shape[axis] // N_DEV
    return ref.at[pl.ds(d * n, n), :] if axis == 0 else ref.at[:, pl.ds(d * n, n)]


def _gather_side(names, shards):
    idxs = [BIG_INDEX[n] for n in names]

    def make(ins, outs, send_sems, recv_sems):
        x, y, c = lax.axis_index("x"), lax.axis_index("y"), lax.axis_index("c")
        cps = []
        for j, i in enumerate(idxs):
            mine = _shard_of(outs[j], BIG[i][2], 4 * x + 2 * y + c)
            cps.append(pltpu.make_async_copy(ins[j], mine, send_sems.at[N_DEV * j]))
            for rel in range(1, N_DEV):
                to = tuple(1 - p if rel >> bit & 1 else p for p, bit in ((x, 2), (y, 1), (c, 0)))
                cps.append(pltpu.make_async_remote_copy(
                    src_ref=ins[j], dst_ref=mine, send_sem=send_sems.at[N_DEV * j + rel],
                    recv_sem=recv_sems.at[N_DEV * j + rel], device_id=to, device_id_type=MESH))
        return cps

    return _Side(shards, [jax.ShapeDtypeStruct(BIG[i][1], BF16) for i in idxs], N_DEV * len(idxs), make)


def _gather_two_level_side(name, shard):
    i = BIG_INDEX[name]

    def parts(ins, outs, send_sems, recv_sems):
        x, y, c = lax.axis_index("x"), lax.axis_index("y"), lax.axis_index("c")
        sibling = (x, y, 1 - c)
        chips = [(1 - x, y), (x, 1 - y), (1 - x, 1 - y)]

        def place(dev):
            return _shard_of(outs[0], BIG[i][2], 4 * dev[0] + 2 * dev[1] + dev[2])

        def copy(k, blk, to, src=None):
            return pltpu.make_async_remote_copy(
                src_ref=place(blk) if src is None else src, dst_ref=place(blk), send_sem=send_sems.at[k],
                recv_sem=recv_sems.at[k], device_id=to, device_id_type=MESH)

        mine = pltpu.make_async_copy(ins[0], place((x, y, c)), send_sems.at[7])
        first = [copy(0, (x, y, c), sibling, src=ins[0])]
        first += [copy(1 + j, (x, y, c), (*chip, c), src=ins[0]) for j, chip in enumerate(chips)]
        passed = [copy(4 + j, (*chip, c), sibling) for j, chip in enumerate(chips)]
        arrived = [copy(1 + j, (*chip, c), (x, y, c)) for j, chip in enumerate(chips)]
        from_sibling = [copy(0, sibling, (x, y, c))] + [copy(4 + j, (*chip, 1 - c), (x, y, c))
                                                       for j, chip in enumerate(chips)]
        return mine, first, passed, arrived, from_sibling

    def make(ins, outs, send_sems, recv_sems):
        mine, first, _, _, _ = parts(ins, outs, send_sems, recv_sems)
        return [mine] + first

    def finish(ins, outs, send_sems, recv_sems):
        mine, first, passed, arrived, from_sibling = parts(ins, outs, send_sems, recv_sems)
        for got, onward in zip(arrived, passed):
            got.wait_recv()
            onward.start()
        for cp in from_sibling:
            cp.wait_recv()
        for cp in first + passed:
            cp.wait_send()
        mine.wait()

    return _Side([shard], [jax.ShapeDtypeStruct(BIG[i][1], BF16)], N_DEV, make, finish)


def _sibling_side(names, grads):
    idxs = [BIG_INDEX[n] for n in names]

    def make(ins, outs, send_sems, recv_sems):
        x, y, c = lax.axis_index("x"), lax.axis_index("y"), lax.axis_index("c")
        return [pltpu.make_async_remote_copy(
            src_ref=_shard_of(ins[j], BIG[i][2], 2 * k + (1 - c)), dst_ref=outs[j].at[k],
            send_sem=send_sems.at[4 * j + k], recv_sem=recv_sems.at[4 * j + k], device_id=(x, y, 1 - c),
            device_id_type=MESH) for j, i in enumerate(idxs) for k in range(4)]

    shapes = [jax.ShapeDtypeStruct((4,) + _shard_shape(BIG[i][1], BIG[i][2]), F32) for i in idxs]
    return _Side(grads, shapes, 4 * len(idxs), make)


def _chips_side(parts):
    def make(ins, outs, send_sems, recv_sems):
        x, y, c = lax.axis_index("x"), lax.axis_index("y"), lax.axis_index("c")
        chips = [(1 - x, y), (x, 1 - y), (1 - x, 1 - y)]
        return [pltpu.make_async_remote_copy(
            src_ref=ins[j].at[2 * cx + cy], dst_ref=outs[j].at[r], send_sem=send_sems.at[3 * j + r],
            recv_sem=recv_sems.at[3 * j + r], device_id=(cx, cy, c), device_id_type=MESH)
            for r, (cx, cy) in enumerate(chips) for j in range(len(parts))]

    return _Side(parts, [jax.ShapeDtypeStruct((3,) + p.shape[1:], p.dtype) for p in parts], 3 * len(parts), make)


def _reduce_add(grad, recv, axis, core, name):
    rs, cs = recv.shape[1:]
    rt = _row_tile(rs, 256)
    nt = rs // rt

    def body(c_ref, g_ref, r_ref, p_ref, pb_ref):
        sm = g_ref[...] + r_ref[0]
        p_ref[0] = sm
        pb_ref[0] = sm.astype(BF16)

    if axis == 0:
        g_spec = pl.BlockSpec((rt, cs), lambda k, t, c_ref: ((2 * k + c_ref[0]) * nt + t, 0))
    else:
        g_spec = pl.BlockSpec((rt, cs), lambda k, t, c_ref: (t, 2 * k + c_ref[0]))
    slab = pl.BlockSpec((1, rt, cs), lambda k, t, c_ref: (k, t, 0))
    return pl.pallas_call(
        body, name=name,
        grid_spec=pltpu.PrefetchScalarGridSpec(num_scalar_prefetch=1, grid=(4, nt), in_specs=[g_spec, slab],
                                               out_specs=[slab, slab]),
        out_shape=[jax.ShapeDtypeStruct(recv.shape, F32), jax.ShapeDtypeStruct(recv.shape, BF16)],
        compiler_params=_params(("parallel", "parallel")),
    )(core, grad, recv)


def _all_gather(block, name, side):
    m_per, n = block.shape
    ns_in, ns_out = len(side.ins), len(side.out_shapes)

    def body(*refs):
        x_ref, s_ins, out_ref = refs[0], refs[1:1 + ns_in], refs[1 + ns_in]
        s_outs = refs[2 + ns_in:2 + ns_in + ns_out]
        send_sems, recv_sems, local_sem, s_send, s_recv = refs[2 + ns_in + ns_out:]
        others = side.make(s_ins, s_outs, s_send, s_recv)
        for cp in others:
            cp.start()
        x, y, c = lax.axis_index("x"), lax.axis_index("y"), lax.axis_index("c")
        me, sibling = (x, y, c), (x, y, 1 - c)
        chips = [(1 - x, y), (x, 1 - y), (1 - x, 1 - y)]

        def rows(px, py, pc):
            return out_ref.at[pl.ds((4 * px + 2 * py + pc) * m_per, m_per), :]

        def copy(k, blk, to, src=None):
            return pltpu.make_async_remote_copy(
                src_ref=rows(*blk) if src is None else src, dst_ref=rows(*blk),
                send_sem=send_sems.at[k], recv_sem=recv_sems.at[k], device_id=to, device_id_type=MESH)

        mine = pltpu.make_async_copy(x_ref, rows(*me), local_sem)
        mine.start()
        first = [copy(0, me, sibling, src=x_ref)]
        first += [copy(1 + j, me, (*chip, c), src=x_ref) for j, chip in enumerate(chips)]
        for cp in first:
            cp.start()
        passed = [copy(4 + j, (*chip, c), sibling) for j, chip in enumerate(chips)]
        for j, chip in enumerate(chips):
            copy(1 + j, (*chip, c), me).wait_recv()
            passed[j].start()
        copy(0, sibling, me).wait_recv()
        for j, chip in enumerate(chips):
            copy(4 + j, (*chip, 1 - c), me).wait_recv()
        for cp in first + passed:
            cp.wait_send()
        mine.wait()
        for cp in others:
            cp.wait()

    res = pl.pallas_call(
        body, name=name, in_specs=[ANY] * (1 + ns_in), out_specs=[ANY] * (1 + ns_out),
        out_shape=[jax.ShapeDtypeStruct((N_DEV * m_per, n), block.dtype)] + side.out_shapes,
        scratch_shapes=[pltpu.SemaphoreType.DMA((7,)), pltpu.SemaphoreType.DMA((7,)), pltpu.SemaphoreType.DMA]
        + side.sems(),
    )(block, *side.ins)
    return res[0], list(res[1:])


def _adam_math(w, g, m, v):
    m = ADAM_B1 * m + (1.0 - ADAM_B1) * g
    v = ADAM_B2 * v + (1.0 - ADAM_B2) * (g * g)
    m_hat = m / (1.0 - ADAM_B1 ** ADAM_STEP)
    v_hat = v / (1.0 - ADAM_B2 ** ADAM_STEP)
    delta = -ADAM_LR * (m_hat / (jnp.sqrt(v_hat) + ADAM_EPS) + ADAM_WD * w)
    return delta, m, v


def _adam_sharded(own, recv, w, m, v, chip, name):
    rs, cs = w.shape
    rt = _row_tile(rs, 256)

    def body(chip_ref, p_ref, r_ref, w_ref, m_ref, v_ref, g_out, d_out, m_out, v_out):
        g = p_ref[0] + r_ref[0].astype(F32) + r_ref[1].astype(F32) + r_ref[2].astype(F32)
        d, mn, vn = _adam_math(w_ref[...], g, m_ref[...], v_ref[...])
        g_out[...] = g
        d_out[...] = d
        m_out[...] = mn
        v_out[...] = vn

    tile = pl.BlockSpec((rt, cs), lambda t, chip_ref: (t, 0))
    return pl.pallas_call(
        body, name=name,
        grid_spec=pltpu.PrefetchScalarGridSpec(
            num_scalar_prefetch=1, grid=(rs // rt,),
            in_specs=[pl.BlockSpec((1, rt, cs), lambda t, chip_ref: (chip_ref[0], t, 0)),
                      pl.BlockSpec((3, rt, cs), lambda t, chip_ref: (0, t, 0)), tile, tile, tile],
            out_specs=[tile] * 4),
        out_shape=[jax.ShapeDtypeStruct((rs, cs), F32)] * 4,
        compiler_params=_params(("parallel",)),
    )(chip, own, recv, w, m, v)


SMALL = ["g_mix", "ssm_a_re", "ssm_a_im", "ssm_log_dt", "ssm_b_re", "ssm_b_im", "ssm_c_re", "ssm_c_im", "ssm_d",
         "sb_g_q", "sb_g_k", "g_out_ssm", "g_out_sb", "g_xa", "g_mem", "xa_g_q", "xa_g_k", "g_mlp"]
PACK_TILE = SUBLANES * LANES


def _natural_2d(n):
    return (n // LANES, LANES) if n % LANES == 0 else (1, n)


def _pack_small(arrs):
    parts = []
    for a in arrs:
        flat = a.reshape(-1)
        parts.append(jnp.pad(flat, (0, (-flat.shape[0]) % PACK_TILE)))
    return jnp.concatenate(parts).reshape(-1, LANES)


def _adam_replicated(gathered, sizes, ws, ms, vs, name):
    n_w = len(ws)
    r_dev = gathered.shape[0] // N_DEV
    offs, off = [], 0
    for n in sizes:
        offs.append(off)
        off += (n + PACK_TILE - 1) // PACK_TILE * SUBLANES
    assert off == r_dev

    def body(*refs):
        g_ref = refs[0]
        w_refs, m_refs, v_refs = refs[1:1 + n_w], refs[1 + n_w:1 + 2 * n_w], refs[1 + 2 * n_w:1 + 3 * n_w]
        outs = refs[1 + 3 * n_w:]

        def total(i, shape):
            r, cdim = shape
            acc = g_ref[pl.ds(offs[i], r), :cdim]
            for d in range(1, N_DEV):
                acc = acc + g_ref[pl.ds(d * r_dev + offs[i], r), :cdim]
            return acc

        for i in range(n_w):
            g = total(i, w_refs[i].shape)
            d, mn, vn = _adam_math(w_refs[i][...], g, m_refs[i][...], v_refs[i][...])
            for o, val in zip(outs[4 * i:4 * i + 4], (g, d, mn, vn)):
                o[...] = val
        outs[4 * n_w][...] = total(n_w, (SUBLANES, LANES))

    shapes = [w.shape for w in ws]
    return pl.pallas_call(
        body, name=name,
        out_shape=[jax.ShapeDtypeStruct(shp, F32) for shp in shapes for _ in range(4)]
        + [jax.ShapeDtypeStruct((SUBLANES, LANES), F32)],
        compiler_params=_params(),
    )(gathered, *ws, *ms, *vs)


def _step(x, mem, target, shards, sm, core):
    g, w, sums, reduced = {}, {}, {}, {}

    def gather(names):
        return _gather_side(names, [shards[n] for n in names])

    def to_sibling(names):
        return _sibling_side(names, [g[n] for n in names])

    def add_sibling(names, received):
        for n, r in zip(names, received):
            sums[n] = _reduce_add(g[n], r, BIG[BIG_INDEX[n]][2], core, "reduce_add_" + n)

    def to_chips(names):
        return _chips_side([sums[n][1] for n in names])

    def keep(names, received):
        for n, r in zip(names, received):
            reduced[n] = (sums[n][0], r)

    row = lambda a: a.reshape(1, -1)
    g_mix, g_xa, g_mlp, g_mem = row(sm["g_mix"]), row(sm["g_xa"]), row(sm["g_mlp"]), row(sm["g_mem"])
    g_os, g_ob = row(sm["g_out_ssm"]), row(sm["g_out_sb"])
    sb_gq, sb_gk = jnp.tile(row(sm["sb_g_q"]), (1, SB_HEADS)), jnp.tile(row(sm["sb_g_k"]), (1, SB_HEADS))
    xa_gq, xa_gk = row(sm["xa_g_q"]), row(sm["xa_g_k"])
    d_skip = row(sm["ssm_d"])

    h1, (w["w_in"],) = _norm_fwd(x, g_mix, "norm_mix", side=_gather_two_level_side("w_in", shards["w_in"]))
    proj = _mm(h1, w["w_in"], "nn", "in_proj", tn=IN_WIDTH)
    u = _to_segments(proj[:, :SSM_WIDTH])
    q_raw, k_raw = (proj, SB_WIDTH, 1), (proj, SB_WIDTH, 2)
    v_col = (SSM_WIDTH + 2 * SB_WIDTH) // LANES
    sb_scale = SB_HEAD_DIM ** -0.5
    qs, ks = _rw(lambda qt, kt, gq, gk: ((_rms_groups(qt, gq, sb_scale), _rms_groups(kt, gk, 1.0)), ()),
                 [q_raw, k_raw], [sb_gq, sb_gk], [(SB_WIDTH, BF16)] * 2, [], "sb_qk_norm")
    early = ["ssm_w_glu", "w_out", "xa_w_q", "xa_w_kv", "xa_w_o", "w_up"]
    y_sb, got = _sb_fwd(qs, ks, proj, "sb_fwd", v_col=v_col, side=gather(early))
    w.update(zip(early, got))

    ssm_args = _states_on_lanes(sm)
    acat, bsup, csup = _ssm_mats_fwd(*ssm_args, "ssm_mats")
    (states, y0, y1), (w["w_down"],) = _ssm_fwd(u, acat, bsup, csup, d_skip, "ssm_fwd", side=gather(["w_down"]))
    z_glu, y_ssm = _mm(y1, w["ssm_w_glu"], "nn", "ssm_glu", epi=lambda r, yt: (r, yt * jax.nn.sigmoid(r)),
                       extras=(y1,), out_dtypes=(F32, F32))
    y_ssm = _from_segments(y_ssm)

    def cat_norm(a, b, ga, gb):
        return jnp.concatenate([_rms(a, ga), _rms(b, gb)], axis=1)

    ycat = _rw(lambda a, b, ga, gb: ((cat_norm(a, b, ga, gb),), ()), [y_ssm, y_sb], [g_os, g_ob],
               [(D_MODEL, BF16)], [], "norm_out")[0]

    def residual_norm_epi(r, xt, gt):
        xn = r + xt
        return xn, _rms(xn, gt)

    x1, h2 = _mm(ycat, w["w_out"], "nn", "out_proj", epi=residual_norm_epi, extras=(x,), fulls=(g_xa,),
                 out_dtypes=(F32, BF16))
    qx = _mm(h2, w["xa_w_q"], "nn", "xa_q")
    memn, kv, kn_x, vv_x = _mem_fwd(mem, g_mem, w["xa_w_kv"], xa_gk, "xa_mem")
    o_xa = _xa_fwd(qx, xa_gq, kn_x, vv_x, "xa_fwd")
    x2, h3 = _mm(o_xa, w["xa_w_o"], "nn", "xa_o", epi=residual_norm_epi, extras=(x1,), fulls=(g_mlp,),
                 out_dtypes=(F32, BF16))

    def up_epi(r):
        rl = jnp.maximum(r, 0.0)
        return (rl * rl,)

    r_up = _mm(h3, w["w_up"], "nn", "mlp_up", epi=up_epi, out_dtypes=(BF16,), tm=2048, tn=2048)

    def loss_epi(r, xt, tt):
        d = r + xt - tt
        return (d * (1.0 / D_MODEL),) * 2, (jnp.sum(d * d, axis=0, keepdims=True),)

    dx3, dx3_b, sq = _mm(r_up, w["w_down"], "nn", "mlp_down", epi=loss_epi, extras=(x2, target),
                         out_dtypes=(F32, BF16), sums=[(1, D_MODEL)])
    loss = jnp.sum(sq) * (0.5 / D_MODEL)

    def norm_bwd_epi(r, xt, drt, gt):
        _, vjp = jax.vjp(_rms, xt, gt)
        dx_, dg_ = vjp(r)
        return (dx_ + drt,) * 2, (dg_,)

    g["w_down"] = _mm(r_up, dx3_b, "tn", "d_w_down", tk=2048)
    da = _mm(dx3_b, w["w_down"], "nt", "d_r", epi=lambda r, rt: (r * 2.0 * jnp.sqrt(rt.astype(F32)),), extras=(r_up,),
             out_dtypes=(BF16,), tn=2048)
    g["w_up"] = _mm(h3, da, "tn", "d_w_up", tk=2048)
    mlp = ["w_down", "w_up"]
    (dx2, dx2_b, g["g_mlp"]), got = _mm(da, w["w_up"], "nt", "d_h3", epi=norm_bwd_epi, extras=(x2, dx3),
                                        fulls=(g_mlp,), out_dtypes=(F32, BF16), sums=[g_mlp.shape],
                                        side=to_sibling(mlp))
    add_sibling(mlp, got)
    g["xa_w_o"] = _mm(o_xa, dx2_b, "tn", "d_xa_w_o", tk=2048)
    do_xa = _mm(dx2_b, w["xa_w_o"], "nt", "d_o_xa")
    dqx, dkn_x, dvv_x, g["xa_g_q"] = _xa_bwd(qx, xa_gq, kn_x, vv_x, do_xa, "xa_bwd")
    g["xa_w_kv"], g["g_mem"], g["xa_g_k"] = _mem_bwd(mem, g_mem, memn, w["xa_w_kv"], kv, xa_gk, dkn_x, dvv_x,
                                                     "xa_mem_bwd")
    g["xa_w_q"] = _mm(h2, dqx, "tn", "d_xa_w_q", tk=2048)
    dx1, dx1_b, g["g_xa"] = _mm(dqx, w["xa_w_q"], "nt", "d_h2", epi=norm_bwd_epi, extras=(x1, dx2), fulls=(g_xa,),
                                out_dtypes=(F32, BF16), sums=[g_xa.shape])
    g["w_out"] = _mm(ycat, dx1_b, "tn", "d_w_out", tk=2048)
    dycat = _mm(dx1_b, w["w_out"], "nt", "d_ycat")

    def cat_bwd(a, b, dy, ga, gb):
        _, vjp = jax.vjp(cat_norm, a, b, ga, gb)
        da_, db_, dga, dgb = vjp(dy)
        return (da_, db_), (dga, dgb)

    dy_ssm, dy_sb, g["g_out_ssm"], g["g_out_sb"] = _rw(
        cat_bwd, [y_ssm, y_sb, dycat], [g_os, g_ob], [(SSM_WIDTH, F32), (SB_WIDTH, F32)], [g_os.shape, g_ob.shape],
        "d_norm_out", tm=512)

    def glu_bwd(dy, yt, zt):
        sg = jax.nn.sigmoid(zt)
        return (dy * sg, dy * yt * sg * (1.0 - sg)), ()

    dy1_a, dz = _rw(glu_bwd, [_to_segments(dy_ssm), y1, z_glu], [], [(SSM_WIDTH, F32), (SSM_WIDTH, BF16)], [], "d_glu")
    g["ssm_w_glu"] = _mm(y1, dz, "tn", "d_w_glu", tk=2048)

    def gelu_bwd_epi(r, da_, y0t):
        _, vjp = jax.vjp(jax.nn.gelu, y0t)
        return (vjp(r + da_)[0],)

    mid = ["w_out", "xa_w_q", "xa_w_kv", "xa_w_o", "ssm_w_glu"]
    dy0, got = _mm(dz, w["ssm_w_glu"], "nt", "d_y1", epi=gelu_bwd_epi, extras=(dy1_a, y0), side=to_sibling(mid))
    add_sibling(mid, got)
    (du, da8, d_bsup, d_csup, g["ssm_d"]), got = _ssm_bwd(dy0, states, u, acat, bsup, csup, d_skip, "ssm_bwd",
                                                          side=to_chips(mlp))
    keep(mlp, got)
    d_acat = jnp.sum(da8, axis=0, keepdims=True)
    d_mats = _ssm_mats_bwd(*ssm_args[:5], d_acat, d_bsup, d_csup, "ssm_mats_bwd")
    for nm, val in zip(("ssm_a_re", "ssm_a_im", "ssm_log_dt", "ssm_b_re", "ssm_b_im", "ssm_c_re", "ssm_c_im"),
                       _from_states_on_lanes(*d_mats)):
        g[nm] = val

    (dqs, dks, dvs), got = _sb_bwd(qs, ks, proj, y_sb, dy_sb, "sb_bwd", v_col=v_col, side=to_chips(mid))
    keep(mid, got)

    def d_proj_rows(du_t, qt, dqt, kt, dkt, dvt, gq, gk):
        _, vjp_q = jax.vjp(lambda a, b_: _rms_groups(a, b_, sb_scale), qt, gq)
        _, vjp_k = jax.vjp(lambda a, b_: _rms_groups(a, b_, 1.0), kt, gk)
        (dq_, dgq_), (dk_, dgk_) = vjp_q(dqt), vjp_k(dkt)
        rows = jnp.concatenate([du_t, dq_.astype(BF16), dk_.astype(BF16), dvt.astype(BF16)], axis=1)
        return (rows,), (dgq_, dgk_)

    dproj, dgq, dgk = _rw(d_proj_rows, [_from_segments(du), q_raw, dqs, k_raw, dks, dvs], [sb_gq, sb_gk],
                          [(IN_WIDTH, BF16)], [sb_gq.shape, sb_gk.shape], "d_proj", tm=512)
    g["sb_g_q"] = jnp.sum(dgq.reshape(SB_HEADS, SB_HEAD_DIM), axis=0)
    g["sb_g_k"] = jnp.sum(dgk.reshape(SB_HEADS, SB_HEAD_DIM), axis=0)
    g["w_in"] = _mm(h1, dproj, "tn", "d_w_in", tn=IN_WIDTH)
    dh1, got = _mm(dproj, w["w_in"], "nt", "d_h1", tk=IN_WIDTH, side=to_sibling(["w_in"]))
    add_sibling(["w_in"], got)
    dx, g["g_mix"] = _norm_bwd(x, g_mix, dh1, dx1, "d_norm_mix")

    packed = _pack_small([g[n] for n in SMALL] + [loss.reshape(1)])
    everyone, got = _all_gather(packed, "gather_small", to_chips(["w_in"]))
    keep(["w_in"], got)
    return dx, everyone, reduced


def kernel(x, mem, g_mix, w_in, ssm_a_re, ssm_a_im, ssm_log_dt, ssm_b_re, ssm_b_im, ssm_c_re, ssm_c_im, ssm_d, ssm_w_glu, sb_g_q, sb_g_k, g_out_ssm, g_out_sb, w_out, g_xa, g_mem, xa_w_q, xa_w_kv, xa_g_q, xa_g_k, xa_w_o, g_mlp, w_up, w_down, loss_target, m_g_mix, m_w_in, m_ssm_a_re, m_ssm_a_im, m_ssm_log_dt, m_ssm_b_re, m_ssm_b_im, m_ssm_c_re, m_ssm_c_im, m_ssm_d, m_ssm_w_glu, m_sb_g_q, m_sb_g_k, m_g_out_ssm, m_g_out_sb, m_w_out, m_g_xa, m_g_mem, m_xa_w_q, m_xa_w_kv, m_xa_g_q, m_xa_g_k, m_xa_w_o, m_g_mlp, m_w_up, m_w_down, v_g_mix, v_w_in, v_ssm_a_re, v_ssm_a_im, v_ssm_log_dt, v_ssm_b_re, v_ssm_b_im, v_ssm_c_re, v_ssm_c_im, v_ssm_d, v_ssm_w_glu, v_sb_g_q, v_sb_g_k, v_g_out_ssm, v_g_out_sb, v_w_out, v_g_xa, v_g_mem, v_xa_w_q, v_xa_w_kv, v_xa_g_q, v_xa_g_k, v_xa_w_o, v_g_mlp, v_w_up, v_w_down):
    given = dict(locals())
    order = ["g_mix", "w_in", "ssm_a_re", "ssm_a_im", "ssm_log_dt", "ssm_b_re", "ssm_b_im", "ssm_c_re", "ssm_c_im",
             "ssm_d", "ssm_w_glu", "sb_g_q", "sb_g_k", "g_out_ssm", "g_out_sb", "w_out", "g_xa", "g_mem", "xa_w_q",
             "xa_w_kv", "xa_g_q", "xa_g_k", "xa_w_o", "g_mlp", "w_up", "w_down"]
    assert sorted([n for n, _, _ in BIG] + SMALL) == sorted(order)
    core = lax.axis_index("c").astype(jnp.int32).reshape(1)
    chip = (2 * lax.axis_index("x") + lax.axis_index("y")).astype(jnp.int32).reshape(1)

    shards = {n: given[n][0].astype(BF16) for n, _, _ in BIG}
    sm = {n: given[n][0] for n in SMALL}
    dx, everyone, reduced = _step(x[0], mem[0], loss_target[0], shards, sm, core)

    res = {}
    for n, _, _ in BIG:
        own, recv = reduced[n]
        outs = _adam_sharded(own, recv, given[n][0], given["m_" + n][0], given["v_" + n][0], chip, "adam_" + n)
        for kind, val in zip(("grad", "delta", "new_m", "new_v"), outs):
            res[kind + "_" + n] = val[None]

    sizes = [math.prod(sm[n].shape) for n in SMALL] + [1]
    nat = lambda a: a.reshape(_natural_2d(math.prod(a.shape)))
    outs = _adam_replicated(everyone, sizes, [nat(sm[n]) for n in SMALL], [nat(given["m_" + n][0]) for n in SMALL],
                            [nat(given["v_" + n][0]) for n in SMALL], "adam_replicated")
    for i, n in enumerate(SMALL):
        for kind, val in zip(("grad", "delta", "new_m", "new_v"), outs[4 * i:4 * i + 4]):
            res[kind + "_" + n] = val.reshape(given[n].shape)
    loss_out = outs[-1][0, 0]
    return (loss_out, dx[None], *[res["grad_" + n] for n in order], *[res["delta_" + n] for n in order],
            *[res["new_m_" + n] for n in order], *[res["new_v_" + n] for n in order])
```

```python
import functools
import math

import jax
import jax.numpy as jnp
from jax import lax
from jax.experimental import pallas as pl
from jax.experimental.pallas import tpu as pltpu

F32 = jnp.float32
BF16 = jnp.bfloat16
MESH = pl.DeviceIdType.MESH

N_DEV = 8
D_MODEL = 1024
SSM_WIDTH = 512
SSM_GROUP = 16
SSM_GROUPS = 32
SSM_STATE = 64
N_STATE = SSM_GROUPS * SSM_STATE
SB_HEADS = 8
SB_HEAD_DIM = 64
SB_WIDTH = 512
IN_WIDTH = 2048
XA_HEADS = 4
XA_HEAD_DIM = 128
XA_WIDTH = 512
D_FF = 4096
NORM_EPS = 1e-6
ADAM_LR = 0.001
ADAM_B1 = 0.9
ADAM_B2 = 0.999
ADAM_EPS = 1e-08
ADAM_WD = 0.01
ADAM_STEP = 10

LANES = 128
SUBLANES = 8
VMEM_LIMIT = 56 * 1024 * 1024
SCAN_LANES = 512
SB_BLOCK = 256
SB_Q_BLOCKS = 4
SB_UNDERFLOW = -110.0

NN = (((1,), (0,)), ((), ()))
NT = (((1,), (1,)), ((), ()))
TN = (((0,), (0,)), ((), ()))


def _params(sem=None):
    return pltpu.CompilerParams(dimension_semantics=sem, vmem_limit_bytes=VMEM_LIMIT)


def _dot(a, b, dims=NN):
    return lax.dot_general(a.astype(BF16), b.astype(BF16), dims, preferred_element_type=F32)


def _rms(x, g):
    return x * lax.rsqrt(jnp.mean(x * x, axis=-1, keepdims=True) + NORM_EPS) * g


ANY = pl.BlockSpec(memory_space=pl.ANY)


class _Side:
    def __init__(self, ins, out_shapes, n_sem, make, finish=None):
        self.ins, self.out_shapes, self.n_sem, self.make = list(ins), list(out_shapes), n_sem, make
        self.finish = finish

    def sems(self):
        return [pltpu.SemaphoreType.DMA((self.n_sem,)), pltpu.SemaphoreType.DMA((self.n_sem,))]


def _hosted(body, side, n_in, n_out, grid):
    if side is None:
        return body
    ns_in, ns_out = len(side.ins), len(side.out_shapes)

    def wrapped(*refs):
        ins, refs = refs[:n_in], refs[n_in:]
        s_ins, refs = refs[:ns_in], refs[ns_in:]
        outs, refs = refs[:n_out], refs[n_out:]
        s_outs, refs = refs[:ns_out], refs[ns_out:]
        scratch, sems = refs[:-2], refs[-2:]
        ids = [pl.program_id(d) for d in range(len(grid))]
        first = functools.reduce(jnp.logical_and, [i == 0 for i in ids])
        last = functools.reduce(jnp.logical_and, [i == n - 1 for i, n in zip(ids, grid)])

        @pl.when(first)
        def _():
            for cp in side.make(s_ins, s_outs, *sems):
                cp.start()

        body(*ins, *outs, *scratch)

        @pl.when(last)
        def _():
            if side.finish is not None:
                side.finish(s_ins, s_outs, *sems)
            else:
                for cp in side.make(s_ins, s_outs, *sems):
                    cp.wait()

    return wrapped


def _side_args(side):
    if side is None:
        return [], [], [], [], []
    return ([ANY] * len(side.ins), [ANY] * len(side.out_shapes), side.out_shapes, side.sems(), side.ins)


def _split_side(res, n_out, side):
    res = list(res)
    main = res[0] if n_out == 1 else res[:n_out]
    return main if side is None else (main, res[n_out:])


def _mm(a, b, mode, name, *, epi=None, extras=(), fulls=(), out_dtypes=(F32,), sums=(), tm=1024, tn=1024, tk=1024,
        side=None):
    if mode == "nn":
        (m, k), (k2, n) = a.shape, b.shape
    elif mode == "nt":
        (m, k), (n, k2) = a.shape, b.shape
    else:
        (k, m), (k2, n) = a.shape, b.shape
    assert k == k2, (name, a.shape, b.shape)
    tm, tn, tk = min(tm, m), min(tn, n), min(tk, k)
    assert m % tm == 0 and n % tn == 0 and k % tk == 0, (name, m, n, k)
    nk = k // tk
    dims = {"nn": NN, "nt": NT, "tn": TN}[mode]
    if mode == "tn":
        a_spec = pl.BlockSpec((tk, tm), lambda i, j, kk: (kk, i))
    else:
        a_spec = pl.BlockSpec((tm, tk), lambda i, j, kk: (i, kk))
    if mode == "nt":
        b_spec = pl.BlockSpec((tn, tk), lambda i, j, kk: (j, kk))
    else:
        b_spec = pl.BlockSpec((tk, tn), lambda i, j, kk: (kk, j))
    mn_spec = pl.BlockSpec((tm, tn), lambda i, j, kk: (i, j))
    n_ex, n_full, n_out, n_sum = len(extras), len(fulls), len(out_dtypes), len(sums)
    n_in = 2 + n_ex + n_full

    def body(*refs):
        a_ref, b_ref = refs[:2]
        ex = refs[2:n_in]
        outs = refs[n_in:n_in + n_out]
        sum_refs = refs[n_in + n_out:n_in + n_out + n_sum]
        kk = pl.program_id(2)
        first_tile = jnp.logical_and(pl.program_id(0) == 0, pl.program_id(1) == 0)

        def finish(r):
            vals = epi(r, *[e[...] for e in ex]) if epi is not None else (r,)
            if n_sum:
                vals, parts = vals

                @pl.when(first_tile)
                def _():
                    for sr in sum_refs:
                        sr[...] = jnp.zeros_like(sr)

                for sr, p in zip(sum_refs, parts):
                    sr[...] += p
            for o, v in zip(outs, vals):
                o[...] = v.astype(o.dtype)

        if nk == 1:
            finish(_dot(a_ref[...], b_ref[...], dims))
        else:
            acc = refs[n_in + n_out + n_sum]

            @pl.when(kk == 0)
            def _():
                acc[...] = jnp.zeros_like(acc)

            acc[...] += _dot(a_ref[...], b_ref[...], dims)

            @pl.when(kk == nk - 1)
            def _():
                finish(acc[...])

    grid = (m // tm, n // tn, nk)
    whole = lambda shape: pl.BlockSpec(shape, lambda i, j, kk: (0,) * len(shape))
    s_in, s_out, s_shape, s_scratch, s_ops = _side_args(side)
    seq = bool(side) or n_sum > 0
    res = pl.pallas_call(
        _hosted(body, side, n_in, n_out + n_sum, grid), name=name, grid=grid,
        in_specs=[a_spec, b_spec] + [mn_spec] * n_ex + [whole(f.shape) for f in fulls] + s_in,
        out_specs=[mn_spec] * n_out + [whole(shape) for shape in sums] + s_out,
        out_shape=[jax.ShapeDtypeStruct((m, n), dt) for dt in out_dtypes]
        + [jax.ShapeDtypeStruct(shape, F32) for shape in sums] + s_shape,
        scratch_shapes=([pltpu.VMEM((tm, tn), F32)] if nk > 1 else []) + s_scratch,
        compiler_params=_params(("arbitrary",) * 3 if seq else ("parallel", "parallel", "arbitrary")),
    )(a, b, *extras, *fulls, *s_ops)
    return _split_side(res, n_out + n_sum, side)


def _row_tile(s, target):
    if s <= target:
        return s
    return max(t for t in range(16, target + 1, 16) if s % t == 0)


def _rw(fn, rows, fulls, row_out, acc_out, name, tm=1024, side=None):
    cols = [r[1:] if isinstance(r, tuple) else (r.shape[1], 0) for r in rows]
    rows = [r[0] if isinstance(r, tuple) else r for r in rows]
    s = rows[0].shape[0]
    tm = _row_tile(s, tm)
    nr, nf, nro, nao = len(rows), len(fulls), len(row_out), len(acc_out)

    def body(*refs):
        r = refs[:nr]
        f = refs[nr:nr + nf]
        ro = refs[nr + nf:nr + nf + nro]
        ao = refs[nr + nf + nro:]
        outs, accs = fn(*[x[...] for x in r], *[x[...] for x in f])
        for o, v in zip(ro, outs):
            o[...] = v.astype(o.dtype)
        if nao:
            @pl.when(pl.program_id(0) == 0)
            def _():
                for a in ao:
                    a[...] = jnp.zeros_like(a)

            for a, v in zip(ao, accs):
                a[...] += v

    full_spec = lambda shape: pl.BlockSpec(shape, lambda i: (0,) * len(shape))
    s_in, s_out, s_shape, s_scratch, s_ops = _side_args(side)
    res = pl.pallas_call(
        _hosted(body, side, nr + nf, nro + nao, (s // tm,)), name=name, grid=(s // tm,),
        in_specs=[pl.BlockSpec((tm, wd), functools.partial(lambda i, cb: (i, cb), cb=cb)) for wd, cb in cols]
        + [full_spec(x.shape) for x in fulls] + s_in,
        out_specs=[pl.BlockSpec((tm, d), lambda i: (i, 0)) for d, _ in row_out]
        + [full_spec(shape) for shape in acc_out] + s_out,
        out_shape=[jax.ShapeDtypeStruct((s, d), dt) for d, dt in row_out]
        + [jax.ShapeDtypeStruct(shape, F32) for shape in acc_out] + s_shape,
        scratch_shapes=s_scratch,
        compiler_params=_params(("arbitrary",)),
    )(*rows, *fulls, *s_ops)
    res = list(res)
    return res if side is None else (res[:nro + nao], res[nro + nao:])


def _norm_fwd(x, g, name, side=None):
    res = _rw(lambda xt, gt: ((_rms(xt, gt),), ()), [x], [g], [(x.shape[1], BF16)], [], name, side=side)
    return res[0] if side is None else (res[0][0], res[1])


def _norm_bwd(x, g, dh, dres, name, side=None):
    def fn(xt, dht, drt, gt):
        _, vjp = jax.vjp(_rms, xt, gt)
        dx, dg = vjp(dht)
        return (dx + drt,), (dg,)

    return _rw(fn, [x, dh, dres], [g], [(x.shape[1], F32)], [g.shape], name, side=side)


def _rms_groups(x, g, scale):
    lo = lax.broadcasted_iota(jnp.int32, (1, LANES), 1) < SB_HEAD_DIM
    x2 = x * x
    outs = []
    for cb in range(x.shape[1] // LANES):
        sl = slice(cb * LANES, (cb + 1) * LANES)
        s_lo = jnp.sum(jnp.where(lo, x2[:, sl], 0.0), axis=-1, keepdims=True)
        s_hi = jnp.sum(jnp.where(lo, 0.0, x2[:, sl]), axis=-1, keepdims=True)
        r = jnp.where(lo, lax.rsqrt(s_lo * (1.0 / SB_HEAD_DIM) + NORM_EPS),
                      lax.rsqrt(s_hi * (1.0 / SB_HEAD_DIM) + NORM_EPS))
        outs.append(x[:, sl] * r)
    return jnp.concatenate(outs, axis=1) * g * scale


def _log_sigmoid(z):
    return jnp.minimum(z, 0.0) - jnp.log(1.0 + jnp.exp(-jnp.abs(z)))


def _split_dot(x, u2):
    hi = x.astype(BF16)
    lo = (x - hi.astype(F32)).astype(BF16)
    return jnp.dot(jnp.concatenate([hi, lo], axis=1), u2, preferred_element_type=F32)


def _sb_consts(b):
    row = lax.broadcasted_iota(jnp.int32, (b, b), 0)
    col = lax.broadcasted_iota(jnp.int32, (b, b), 1)
    tri = col < row
    u_after = (row > col).astype(BF16)
    u_from = (row >= col).astype(BF16)
    stack = lambda u: jnp.concatenate([u, u], axis=0)
    lane_lo = lax.broadcasted_iota(jnp.int32, (b, LANES), 1) < SB_HEAD_DIM
    return tri, stack(u_after), stack(u_from), lane_lo


def _sb_scores(qh, kb, a_run, keep, u2_after, mask_l=True):
    z = lax.dot_general(qh, kb, NT, preferred_element_type=F32)
    lb = _log_sigmoid(z)
    l = lb - z
    if keep is not None and mask_l:
        l = jnp.where(keep, l, 0.0)
    w = jnp.exp(lb + (a_run + _split_dot(l, u2_after)))
    if keep is not None:
        w = jnp.where(keep, w, 0.0)
    return lb, l, w


def _sb_walk(qi, carry, step):
    def cond(state):
        n, c = state
        return jnp.logical_and(n <= qi, jnp.max(jnp.maximum(c[0], c[1])) > SB_UNDERFLOW)

    def body(state):
        n, c = state
        return n + 1, step(n, c)

    return lax.while_loop(cond, body, (jnp.int32(2), carry))[1]


def _two_heads(x, lane_lo):
    zero = jnp.zeros_like(x)
    return jnp.where(lane_lo, x, zero), jnp.where(lane_lo, zero, x)


def _sb_fwd(qs, ks, v, name, v_col=0, side=None):
    s, width = qs.shape
    b = min(SB_BLOCK, s)
    nqb = min(SB_Q_BLOCKS, s // b)

    def body(q_ref, k_ref, v_ref, o_ref):
        tri, u2_after, _, lane_lo = _sb_consts(b)
        zero = jnp.zeros((b, 1), F32)
        started = []
        for h in range(nqb):
            qi = pl.program_id(1) * nqb + h
            q_a, q_b = _two_heads(q_ref[h * b:(h + 1) * b, :], lane_lo)

            def step(n, carry, keep, mask_l=True, qi=qi, q_a=q_a, q_b=q_b):
                a_a, a_b, acc = carry
                off = pl.multiple_of(jnp.maximum(qi - n, 0) * b, b)
                kb = k_ref[pl.ds(off, b), :]
                v_a, v_b = _two_heads(v_ref[pl.ds(off, b), :].astype(BF16), lane_lo)
                _, l_a, w_a = _sb_scores(q_a, kb, a_a, keep, u2_after, mask_l)
                _, l_b, w_b = _sb_scores(q_b, kb, a_b, keep, u2_after, mask_l)
                acc = acc + jnp.dot(jnp.concatenate([w_a.astype(BF16), w_b.astype(BF16)], axis=1),
                                    jnp.concatenate([v_a, v_b], axis=0), preferred_element_type=F32)
                return (a_a + jnp.sum(l_a, axis=1, keepdims=True), a_b + jnp.sum(l_b, axis=1, keepdims=True), acc)

            carry = step(0, (zero, zero, jnp.zeros((b, LANES), F32)), tri)
            carry = step(1, carry, jnp.broadcast_to(qi > 0, tri.shape), mask_l=False)
            started.append((qi, step, carry))
        for h, (qi, step, carry) in enumerate(started):
            carry = _sb_walk(qi, carry, lambda n, c, step=step: step(n, c, None))
            o_ref[h * b:(h + 1) * b, :] = carry[2]

    blk = pl.BlockSpec((nqb * b, LANES), lambda hp, i: (i, hp))
    full = pl.BlockSpec((s, LANES), lambda hp, i: (0, hp))
    full_v = pl.BlockSpec((s, LANES), lambda hp, i: (0, hp + v_col))
    grid = (width // LANES, s // (nqb * b))
    s_in, s_out, s_shape, s_scratch, s_ops = _side_args(side)
    res = pl.pallas_call(
        _hosted(body, side, 3, 1, grid), name=name, grid=grid,
        in_specs=[blk, full, full_v] + s_in, out_specs=[blk] + s_out,
        out_shape=[jax.ShapeDtypeStruct((s, width), F32)] + s_shape, scratch_shapes=s_scratch,
        compiler_params=_params(("arbitrary", "arbitrary")),
    )(qs, ks, v, *s_ops)
    return _split_side(res, 1, side)


def _sb_bwd(qs, ks, v, out, dout, name, v_col=0, side=None):
    s, width = qs.shape
    b = min(SB_BLOCK, s)
    nqb = min(SB_Q_BLOCKS, s // b)

    def body(q_ref, k_ref, v_ref, o_ref, do_ref, dq_ref, dk_ref, dv_ref):
        @pl.when(pl.program_id(1) == 0)
        def _():
            dk_ref[...] = jnp.zeros_like(dk_ref)
            dv_ref[...] = jnp.zeros_like(dv_ref)

        tri, u2_after, u2_from, lane_lo = _sb_consts(b)
        zero = jnp.zeros((b, 1), F32)

        def head(qh, doh, kb, vb, a_run, d_rem, keep, mask_l):
            lb, l, w = _sb_scores(qh, kb, a_run, keep, u2_after, mask_l)
            wb = w.astype(BF16)
            g = lax.dot_general(doh, vb, NT, preferred_element_type=F32) * wb.astype(F32)
            g_before = d_rem - _split_dot(g, u2_from)
            dz = g - (g + g_before) * jnp.exp(lb)
            if keep is not None:
                dz = jnp.where(keep, dz, 0.0)
            return (dz.astype(BF16), wb, a_run + jnp.sum(l, axis=1, keepdims=True),
                    d_rem - jnp.sum(g, axis=1, keepdims=True))

        started = []
        for h in range(nqb):
            qi = pl.program_id(1) * nqb + h
            rows = slice(h * b, (h + 1) * b)
            q_a, q_b = _two_heads(q_ref[rows, :], lane_lo)
            dob = do_ref[rows, :].astype(BF16)
            do_a, do_b = _two_heads(dob, lane_lo)
            prod = dob.astype(F32) * o_ref[rows, :]
            d_a = jnp.sum(jnp.where(lane_lo, prod, 0.0), axis=1, keepdims=True)
            d_b = jnp.sum(jnp.where(lane_lo, 0.0, prod), axis=1, keepdims=True)
            q_rows = jnp.concatenate([q_a, q_b], axis=0)
            do_rows = jnp.concatenate([do_a, do_b], axis=0)

            def step(n, carry, keep, mask_l=True, qi=qi, q_a=q_a, q_b=q_b, do_a=do_a, do_b=do_b, q_rows=q_rows,
                     do_rows=do_rows):
                a_a, a_b, r_a, r_b, dq = carry
                off = pl.multiple_of(jnp.maximum(qi - n, 0) * b, b)
                kb = k_ref[pl.ds(off, b), :]
                vb = v_ref[pl.ds(off, b), :].astype(BF16)
                k_a, k_b = _two_heads(kb, lane_lo)
                dz_a, w_a, a_a, r_a = head(q_a, do_a, kb, vb, a_a, r_a, keep, mask_l)
                dz_b, w_b, a_b, r_b = head(q_b, do_b, kb, vb, a_b, r_b, keep, mask_l)
                dq = dq + jnp.dot(jnp.concatenate([dz_a, dz_b], axis=1), jnp.concatenate([k_a, k_b], axis=0),
                                  preferred_element_type=F32)
                dk_ref[pl.ds(off, b), :] += lax.dot_general(jnp.concatenate([dz_a, dz_b], axis=0), q_rows, TN,
                                                            preferred_element_type=F32)
                dv_ref[pl.ds(off, b), :] += lax.dot_general(jnp.concatenate([w_a, w_b], axis=0), do_rows, TN,
                                                            preferred_element_type=F32)
                return a_a, a_b, r_a, r_b, dq

            carry = step(0, (zero, zero, d_a, d_b, jnp.zeros((b, LANES), F32)), tri)
            carry = step(1, carry, jnp.broadcast_to(qi > 0, tri.shape), mask_l=False)
            started.append((qi, step, carry))
        for h, (qi, step, carry) in enumerate(started):
            carry = _sb_walk(qi, carry, lambda n, c, step=step: step(n, c, None))
            dq_ref[h * b:(h + 1) * b, :] = carry[4]

    blk = pl.BlockSpec((nqb * b, LANES), lambda hp, i: (i, hp))
    full = pl.BlockSpec((s, LANES), lambda hp, i: (0, hp))
    full_v = pl.BlockSpec((s, LANES), lambda hp, i: (0, hp + v_col))
    grid = (width // LANES, s // (nqb * b))
    s_in, s_out, s_shape, s_scratch, s_ops = _side_args(side)
    res = pl.pallas_call(
        _hosted(body, side, 5, 3, grid), name=name, grid=grid,
        in_specs=[blk, full, full_v, blk, blk] + s_in, out_specs=[blk, full, full] + s_out,
        out_shape=[jax.ShapeDtypeStruct((s, width), F32)] * 3 + s_shape,
        scratch_shapes=s_scratch,
        compiler_params=_params(("arbitrary", "arbitrary")),
    )(qs, ks, v, out, dout, *s_ops)
    return _split_side(res, 3, side)


def _cmul(xr, xi, yr, yi):
    return xr * yr - xi * yi, xr * yi + xi * yr


def _scan_consts(ar, ai, reverse, lc):
    rowi = lax.broadcasted_iota(jnp.int32, (SUBLANES, lc), 0)
    pows = [(ar, ai)]
    for _ in range(SUBLANES - 1):
        pows.append(_cmul(*pows[-1], ar, ai))
    steps = []
    for d in (1, 2, 4):
        keep = (rowi < SUBLANES - d) if reverse else (rowi >= d)
        pr, pi = pows[d - 1]
        steps.append((SUBLANES - d if reverse else d, jnp.where(keep, pr, 0.0), jnp.where(keep, pi, 0.0)))
    cr = jnp.zeros((SUBLANES, lc), F32)
    ci = jnp.zeros((SUBLANES, lc), F32)
    for r in range(SUBLANES):
        pr, pi = pows[SUBLANES - 1 - r] if reverse else pows[r]
        cr = jnp.where(rowi == r, pr, cr)
        ci = jnp.where(rowi == r, pi, ci)
    return steps, cr, ci


def _scan_tile(xr, xi, steps, pr, pi, cr, ci):
    for shift, ar, ai in steps:
        rr = pltpu.roll(xr, shift, 0)
        ri = pltpu.roll(xi, shift, 0)
        xr, xi = xr + ar * rr - ai * ri, xi + ar * ri + ai * rr
    return xr + pr * cr - pi * ci, xi + pr * ci + pi * cr


SCAN_ROWS = 1024


def _scan_chunk(s):
    tt = min(SCAN_ROWS, s)
    seg = tt // SUBLANES
    assert s % tt == 0 and seg % SUBLANES == 0 and seg & (seg - 1) == 0, s
    return tt, seg


def _to_segments(a):
    s, wd = a.shape
    tt, seg = _scan_chunk(s)
    return jnp.transpose(a.reshape(s // tt, SUBLANES, seg, wd), (0, 2, 1, 3)).reshape(s, wd)


def _from_segments(a):
    s, wd = a.shape
    tt, seg = _scan_chunk(s)
    return jnp.transpose(a.reshape(s // tt, seg, SUBLANES, wd), (0, 2, 1, 3)).reshape(s, wd)


def _cpow2(xr, xi, k):
    for _ in range(k):
        xr, xi = _cmul(xr, xi, xr, xi)
    return xr, xi


def _fill_powers(pw_ref, ar, ai, seg, lc):
    _, p8r, p8i = _scan_consts(ar, ai, False, lc)
    a8r, a8i = _cpow2(ar, ai, 3)
    qr, qi = jnp.ones_like(ar), jnp.zeros_like(ai)
    for k in range(seg // SUBLANES):
        tr, ti = _cmul(p8r, p8i, qr, qi)
        for r in range(SUBLANES):
            rows = pl.ds((SUBLANES * k + r) * SUBLANES, SUBLANES)
            pw_ref[rows, :lc] = jnp.broadcast_to(tr[r:r + 1, :], (SUBLANES, lc))
            pw_ref[rows, lc:] = jnp.broadcast_to(ti[r:r + 1, :], (SUBLANES, lc))
        qr, qi = _cmul(qr, qi, a8r, a8i)


def _ssm_fwd(u, acat, bsup, csup, d_skip, name, side=None):
    s = u.shape[0]
    lc = SCAN_LANES
    tt, seg = _scan_chunk(s)
    nl, nt = N_STATE // lc, s // tt
    tile = lambda j: pl.ds(pl.multiple_of(j * SUBLANES, SUBLANES), SUBLANES)

    def body(u_ref, a_ref, b_ref, c_ref, d_ref, s_ref, y0_ref, y1_ref, carry, pw_ref):
        ar, ai = a_ref[:, :lc], a_ref[:, lc:]

        @pl.when(pl.program_id(1) == 0)
        def _():
            carry[...] = jnp.zeros_like(carry)
            _fill_powers(pw_ref, ar, ai, seg, lc)

        ut = u_ref[...]
        s_ref[...] = _dot(ut, b_ref[0])

        ar8, ai8 = jnp.broadcast_to(ar, (SUBLANES, lc)), jnp.broadcast_to(ai, (SUBLANES, lc))

        def local(j, x):
            xr = ar8 * x[0] - ai8 * x[1] + s_ref[tile(j), :lc]
            xi = ar8 * x[1] + ai8 * x[0] + s_ref[tile(j), lc:]
            s_ref[tile(j), :lc] = xr
            s_ref[tile(j), lc:] = xi
            return xr, xi

        zero = jnp.zeros((SUBLANES, lc), F32)
        er, ei = lax.fori_loop(0, seg, local, (zero, zero), unroll=4)
        steps, pr, pi = _scan_consts(*_cpow2(ar, ai, seg.bit_length() - 1), False, lc)
        cr, ci = carry[:, :lc], carry[:, lc:]
        tr, ti = _scan_tile(er, ei, steps, pr, pi, cr, ci)
        rowi = lax.broadcasted_iota(jnp.int32, (SUBLANES, lc), 0)
        before_r = jnp.where(rowi == 0, cr, pltpu.roll(tr, 1, 0))
        before_i = jnp.where(rowi == 0, ci, pltpu.roll(ti, 1, 0))
        carry[:, :lc] = jnp.broadcast_to(tr[SUBLANES - 1:, :], (SUBLANES, lc))
        carry[:, lc:] = jnp.broadcast_to(ti[SUBLANES - 1:, :], (SUBLANES, lc))

        def fix(j, _):
            pwr, pwi = pw_ref[tile(j), :lc], pw_ref[tile(j), lc:]
            s_ref[tile(j), :lc] += pwr * before_r - pwi * before_i
            s_ref[tile(j), lc:] += pwr * before_i + pwi * before_r
            return 0

        lax.fori_loop(0, seg, fix, 0, unroll=4)
        y0 = _dot(s_ref[...], c_ref[0], NT) + d_ref[...] * ut
        y0_ref[...] = y0
        y1_ref[...] = jax.nn.gelu(y0)

    chan = pl.BlockSpec((tt, LANES), lambda j, c: (c, j))
    sup = pl.BlockSpec((1, LANES, 2 * lc), lambda j, c: (j, 0, 0))
    s_in, s_out, s_shape, s_scratch, s_ops = _side_args(side)
    res = pl.pallas_call(
        _hosted(body, side, 5, 3, (nl, nt)), name=name, grid=(nl, nt),
        in_specs=[chan, pl.BlockSpec((1, 2 * lc), lambda j, c: (0, j)), sup, sup,
                  pl.BlockSpec((1, LANES), lambda j, c: (0, j))] + s_in,
        out_specs=[pl.BlockSpec((tt, 2 * lc), lambda j, c: (c, j)), chan, chan] + s_out,
        out_shape=[jax.ShapeDtypeStruct((s, 2 * N_STATE), F32), jax.ShapeDtypeStruct((s, SSM_WIDTH), F32),
                   jax.ShapeDtypeStruct((s, SSM_WIDTH), F32)] + s_shape,
        scratch_shapes=[pltpu.VMEM((SUBLANES, 2 * lc), F32), pltpu.VMEM((seg * SUBLANES, 2 * lc), F32)] + s_scratch,
        compiler_params=_params(("arbitrary", "arbitrary")),
    )(u, acat, bsup, csup, d_skip, *s_ops)
    return _split_side(res, 3, side)


def _ssm_bwd(dy0, states, u, acat, bsup, csup, d_skip, name, side=None):
    s = u.shape[0]
    lc = SCAN_LANES
    tt, seg = _scan_chunk(s)
    nl, nt = N_STATE // lc, s // tt
    tile = lambda j: pl.ds(pl.multiple_of(j * SUBLANES, SUBLANES), SUBLANES)

    def body(dy_ref, s_ref, sp_ref, u_ref, a_ref, b_ref, c_ref, d_ref,
             du_ref, da_ref, db_ref, dc_ref, dd_ref, lam_ref, carry, pw_ref):
        c = pl.program_id(1)
        ar, ai = a_ref[:, :lc], a_ref[:, lc:]

        @pl.when(c == 0)
        def _():
            carry[...] = jnp.zeros_like(carry)
            for r in (da_ref, db_ref, dc_ref, dd_ref):
                r[...] = jnp.zeros_like(r)
            _fill_powers(pw_ref, ar, ai, seg, lc)

        dy = dy_ref[...]
        ut = u_ref[...]
        lam_ref[...] = _dot(dy, c_ref[0])

        ar8, ai8 = jnp.broadcast_to(ar, (SUBLANES, lc)), jnp.broadcast_to(ai, (SUBLANES, lc))

        def local(i, x):
            j = seg - 1 - i
            xr = ar8 * x[0] + ai8 * x[1] + lam_ref[tile(j), :lc]
            xi = ar8 * x[1] - ai8 * x[0] + lam_ref[tile(j), lc:]
            lam_ref[tile(j), :lc] = xr
            lam_ref[tile(j), lc:] = xi
            return xr, xi

        zero = jnp.zeros((SUBLANES, lc), F32)
        er, ei = lax.fori_loop(0, seg, local, (zero, zero), unroll=4)
        big_r, big_i = _cpow2(ar, ai, seg.bit_length() - 1)
        steps, pr, pi = _scan_consts(big_r, -big_i, True, lc)
        cr, ci = carry[:, :lc], carry[:, lc:]
        tr, ti = _scan_tile(er, ei, steps, pr, pi, cr, ci)
        rowi = lax.broadcasted_iota(jnp.int32, (SUBLANES, lc), 0)
        after_r = jnp.where(rowi == SUBLANES - 1, cr, pltpu.roll(tr, SUBLANES - 1, 0))
        after_i = jnp.where(rowi == SUBLANES - 1, ci, pltpu.roll(ti, SUBLANES - 1, 0))
        carry[:, :lc] = jnp.broadcast_to(tr[:1, :], (SUBLANES, lc))
        carry[:, lc:] = jnp.broadcast_to(ti[:1, :], (SUBLANES, lc))

        start = c != nt - 1
        last_r = jnp.where(start, jnp.broadcast_to(sp_ref[SUBLANES - 1:, :lc], (SUBLANES, lc)), 0.0)
        last_i = jnp.where(start, jnp.broadcast_to(sp_ref[SUBLANES - 1:, lc:], (SUBLANES, lc)), 0.0)
        first_r = jnp.where(rowi == 0, last_r, pltpu.roll(s_ref[tile(seg - 1), :lc], 1, 0))
        first_i = jnp.where(rowi == 0, last_i, pltpu.roll(s_ref[tile(seg - 1), lc:], 1, 0))

        def fix(j, acc):
            dar, dai = acc
            k = seg - 1 - j
            pwr, pwi = pw_ref[tile(k), :lc], pw_ref[tile(k), lc:]
            lr = lam_ref[tile(j), :lc] + pwr * after_r + pwi * after_i
            li = lam_ref[tile(j), lc:] + pwr * after_i - pwi * after_r
            lam_ref[tile(j), :lc] = lr
            lam_ref[tile(j), lc:] = li
            jp = jnp.maximum(j - 1, 0)
            sr = jnp.where(j > 0, s_ref[tile(jp), :lc], first_r)
            si = jnp.where(j > 0, s_ref[tile(jp), lc:], first_i)
            return dar + lr * sr + li * si, dai + li * sr - lr * si

        dar, dai = lax.fori_loop(0, seg, fix, (zero, zero), unroll=4)
        da_ref[:, :lc] += dar
        da_ref[:, lc:] += dai
        lam = lam_ref[...].astype(BF16)
        du_ref[...] = (_dot(lam, b_ref[0], NT) + d_ref[...] * dy).astype(du_ref.dtype)
        db_ref[0] += _dot(ut, lam, TN)
        dc_ref[0] += _dot(dy, s_ref[...], TN)
        dd_ref[...] += jnp.sum(dy * ut, axis=0, keepdims=True)

    rev = lambda j, c: (nt - 1 - c, j)
    chan = pl.BlockSpec((tt, LANES), rev)
    sup = pl.BlockSpec((1, LANES, 2 * lc), lambda j, c: (j, 0, 0))
    row = pl.BlockSpec((1, LANES), lambda j, c: (0, j))
    s_in, s_out, s_shape, s_scratch, s_ops = _side_args(side)
    res = pl.pallas_call(
        _hosted(body, side, 8, 5, (nl, nt)), name=name, grid=(nl, nt),
        in_specs=[chan, pl.BlockSpec((tt, 2 * lc), rev),
                  pl.BlockSpec((SUBLANES, 2 * lc), lambda j, c: (jnp.maximum((nt - 1 - c) * seg - 1, 0), j)),
                  chan, pl.BlockSpec((1, 2 * lc), lambda j, c: (0, j)), sup, sup, row] + s_in,
        out_specs=[chan, pl.BlockSpec((SUBLANES, 2 * lc), lambda j, c: (0, j)), sup, sup, row] + s_out,
        out_shape=[jax.ShapeDtypeStruct((s, SSM_WIDTH), BF16), jax.ShapeDtypeStruct((SUBLANES, 2 * N_STATE), F32),
                   jax.ShapeDtypeStruct(bsup.shape, F32), jax.ShapeDtypeStruct(csup.shape, F32),
                   jax.ShapeDtypeStruct((1, SSM_WIDTH), F32)] + s_shape,
        scratch_shapes=[pltpu.VMEM((tt, 2 * lc), F32), pltpu.VMEM((SUBLANES, 2 * lc), F32),
                        pltpu.VMEM((seg * SUBLANES, 2 * lc), F32)] + s_scratch,
        compiler_params=_params(("arbitrary", "arbitrary")),
    )(dy0, states, states, u, acat, bsup, csup, d_skip, *s_ops)
    return _split_side(res, 5, side)


def _discretise(ar, ai, ldt, br, bi):
    dt = jnp.exp(ldt)
    lr, li = ar * dt, ai * dt
    e = jnp.exp(lr)
    abar_r, abar_i = e * jnp.cos(li), e * jnp.sin(li)
    den = ar * ar + ai * ai
    coef_r = ((abar_r - 1.0) * ar + abar_i * ai) / den
    coef_i = (abar_i * ar - (abar_r - 1.0) * ai) / den
    return abar_r, abar_i, coef_r * br - coef_i * bi, coef_r * bi + coef_i * br


def _group_mask():
    shape = (LANES, SCAN_LANES)
    return (lax.broadcasted_iota(jnp.int32, shape, 0) // SSM_GROUP
            == lax.broadcasted_iota(jnp.int32, shape, 1) // SSM_STATE)


def _ssm_mats_fwd(a_re, a_im, log_dt, b_re, b_im, c_re, c_im, name):
    nl = N_STATE // SCAN_LANES
    lc = SCAN_LANES

    def body(ar, ai, ldt, br, bi, cr, ci, acat, bsup, csup):
        abar_r, abar_i, bbar_r, bbar_i = _discretise(ar[...], ai[...], ldt[...], br[...], bi[...])
        same = _group_mask()
        spread = lambda m, j: jnp.where(same, jnp.tile(m[:, j * lc:(j + 1) * lc], (LANES // SSM_GROUP, 1)), 0.0)
        c_r, c_i = cr[...], -ci[...]
        for j in range(nl):
            acat[:, 2 * j * lc:(2 * j + 1) * lc] = abar_r[:, j * lc:(j + 1) * lc]
            acat[:, (2 * j + 1) * lc:(2 * j + 2) * lc] = abar_i[:, j * lc:(j + 1) * lc]
            bsup[j, :, :lc] = spread(bbar_r, j)
            bsup[j, :, lc:] = spread(bbar_i, j)
            csup[j, :, :lc] = spread(c_r, j)
            csup[j, :, lc:] = spread(c_i, j)

    return pl.pallas_call(
        body, name=name,
        out_shape=[jax.ShapeDtypeStruct((1, 2 * N_STATE), F32), jax.ShapeDtypeStruct((nl, LANES, 2 * lc), F32),
                   jax.ShapeDtypeStruct((nl, LANES, 2 * lc), F32)],
        compiler_params=_params(),
    )(a_re, a_im, log_dt, b_re, b_im, c_re, c_im)


def _ssm_mats_bwd(a_re, a_im, log_dt, b_re, b_im, d_acat, d_bsup, d_csup, name):
    nl = N_STATE // SCAN_LANES
    lc = SCAN_LANES

    def body(ar, ai, ldt, br, bi, dac, dbs, dcs, d_ar, d_ai, d_ldt, d_br, d_bi, d_cr, d_ci):
        same = _group_mask()

        def gather(ref, j, half):
            m = jnp.where(same, ref[j, :, half * lc:(half + 1) * lc], 0.0)
            tot = m[:SSM_GROUP]
            for k in range(1, LANES // SSM_GROUP):
                tot = tot + m[k * SSM_GROUP:(k + 1) * SSM_GROUP]
            return tot

        cols = lambda ref, half: jnp.concatenate([gather(ref, j, half) for j in range(nl)], axis=1)
        d_abar_r = jnp.concatenate([dac[:, 2 * j * lc:(2 * j + 1) * lc] for j in range(nl)], axis=1)
        d_abar_i = jnp.concatenate([dac[:, (2 * j + 1) * lc:(2 * j + 2) * lc] for j in range(nl)], axis=1)
        _, vjp = jax.vjp(_discretise, ar[...], ai[...], ldt[...], br[...], bi[...])
        outs = vjp((d_abar_r, d_abar_i, cols(dbs, 0), cols(dbs, 1)))
        for ref, val in zip((d_ar, d_ai, d_ldt, d_br, d_bi), outs):
            ref[...] = val
        d_cr[...] = cols(dcs, 0)
        d_ci[...] = -cols(dcs, 1)

    row = jax.ShapeDtypeStruct((1, N_STATE), F32)
    mat = jax.ShapeDtypeStruct((SSM_GROUP, N_STATE), F32)
    return pl.pallas_call(
        body, name=name, out_shape=[row, row, row, mat, mat, mat, mat], compiler_params=_params(),
    )(a_re, a_im, log_dt, b_re, b_im, d_acat, d_bsup, d_csup)


def _states_on_lanes(sm):
    flat = lambda a: a.reshape(1, N_STATE)
    chan_b = lambda b: jnp.transpose(b, (2, 0, 1)).reshape(SSM_GROUP, N_STATE)
    chan_c = lambda c: jnp.transpose(c, (1, 0, 2)).reshape(SSM_GROUP, N_STATE)
    return (flat(sm["ssm_a_re"]), flat(sm["ssm_a_im"]), flat(jnp.repeat(sm["ssm_log_dt"], SSM_STATE)),
            chan_b(sm["ssm_b_re"]), chan_b(sm["ssm_b_im"]), chan_c(sm["ssm_c_re"]), chan_c(sm["ssm_c_im"]))


def _from_states_on_lanes(d_ar, d_ai, d_ldt, d_br, d_bi, d_cr, d_ci):
    grp = lambda a: a.reshape(SSM_GROUPS, SSM_STATE)
    back_b = lambda b: jnp.transpose(b.reshape(SSM_GROUP, SSM_GROUPS, SSM_STATE), (1, 2, 0))
    back_c = lambda c: jnp.transpose(c.reshape(SSM_GROUP, SSM_GROUPS, SSM_STATE), (1, 0, 2))
    return (grp(d_ar), grp(d_ai), jnp.sum(grp(d_ldt), axis=1), back_b(d_br), back_b(d_bi), back_c(d_cr), back_c(d_ci))


def _mem_fwd(mem, g_mem, w_kv, g_k, name):
    ml = mem.shape[0]

    def body(mem_ref, gm_ref, w_ref, gk_ref, memn_ref, kv_ref, kn_ref, vv_ref):
        memn = _rms(mem_ref[...], gm_ref[...])
        memn_ref[...] = memn.astype(BF16)
        kv = _dot(memn, w_ref[...])
        kv_ref[...] = kv
        for hh in range(XA_HEADS):
            sl = slice(hh * XA_HEAD_DIM, (hh + 1) * XA_HEAD_DIM)
            kn_ref[:, sl] = _rms(kv[:, sl], gk_ref[...]).astype(BF16)
        vv_ref[...] = kv[:, XA_WIDTH:].astype(BF16)

    return pl.pallas_call(
        body, name=name,
        out_shape=[jax.ShapeDtypeStruct((ml, D_MODEL), BF16), jax.ShapeDtypeStruct((ml, 2 * XA_WIDTH), F32),
                   jax.ShapeDtypeStruct((ml, XA_WIDTH), BF16), jax.ShapeDtypeStruct((ml, XA_WIDTH), BF16)],
        compiler_params=_params(),
    )(mem, g_mem, w_kv, g_k)


def _mem_bwd(mem, g_mem, memn, w_kv, kv, g_k, dkn, dvv, name):
    def body(mem_ref, gm_ref, memn_ref, w_ref, kv_ref, gk_ref, dkn_ref, dvv_ref, dw_ref, dgm_ref, dgk_ref):
        kv = kv_ref[...]
        dgk = jnp.zeros(dgk_ref.shape, F32)
        parts = []
        for hh in range(XA_HEADS):
            sl = slice(hh * XA_HEAD_DIM, (hh + 1) * XA_HEAD_DIM)
            _, vjp = jax.vjp(_rms, kv[:, sl], gk_ref[...])
            dk, dg = vjp(dkn_ref[:, sl])
            parts.append(dk)
            dgk = dgk + dg
        dgk_ref[...] = dgk
        dkv = jnp.concatenate(parts + [dvv_ref[...]], axis=1)
        dw_ref[...] = _dot(memn_ref[...], dkv, TN)
        dmemn = _dot(dkv, w_ref[...], NT)
        _, vjp = jax.vjp(_rms, mem_ref[...], gm_ref[...])
        dgm_ref[...] = vjp(dmemn)[1]

    return pl.pallas_call(
        body, name=name,
        out_shape=[jax.ShapeDtypeStruct((D_MODEL, 2 * XA_WIDTH), F32), jax.ShapeDtypeStruct(g_mem.shape, F32),
                   jax.ShapeDtypeStruct(g_k.shape, F32)],
        compiler_params=_params(),
    )(mem, g_mem, memn, w_kv, kv, g_k, dkn, dvv)


def _xa_head(qx_h, g_q, kn_h, vv_h):
    qn = _rms(qx_h, g_q)
    sc = _dot(qn, kn_h, NT) * (XA_HEAD_DIM ** -0.5)
    sc = sc - jnp.max(sc, axis=-1, keepdims=True)
    e = jnp.exp(sc)
    p = e / jnp.sum(e, axis=-1, keepdims=True)
    return qn, p


def _xa_fwd(qx, g_q, kn, vv, name):
    def fn(qt, gq, knt, vvt):
        outs = []
        for hh in range(XA_HEADS):
            sl = slice(hh * XA_HEAD_DIM, (hh + 1) * XA_HEAD_DIM)
            _, p = _xa_head(qt[:, sl], gq, knt[:, sl], vvt[:, sl])
            outs.append(_dot(p, vvt[:, sl]))
        return (jnp.concatenate(outs, axis=1),), ()

    return _rw(fn, [qx], [g_q, kn, vv], [(XA_WIDTH, BF16)], [], name, tm=512)[0]


def _xa_bwd(qx, g_q, kn, vv, do, name):
    def fn(qt, dot_, gq, knt, vvt):
        dqs, dks, dvs = [], [], []
        dgq = jnp.zeros_like(gq)
        for hh in range(XA_HEADS):
            sl = slice(hh * XA_HEAD_DIM, (hh + 1) * XA_HEAD_DIM)
            qn, p = _xa_head(qt[:, sl], gq, knt[:, sl], vvt[:, sl])
            doh = dot_[:, sl]
            dp = _dot(doh, vvt[:, sl], NT)
            dvs.append(_dot(p, doh, TN))
            ds = p * (dp - jnp.sum(dp * p, axis=-1, keepdims=True)) * (XA_HEAD_DIM ** -0.5)
            dqn = _dot(ds, knt[:, sl])
            dks.append(_dot(ds, qn, TN))
            _, vjp = jax.vjp(_rms, qt[:, sl], gq)
            dq, dg = vjp(dqn)
            dqs.append(dq)
            dgq = dgq + dg
        return ((jnp.concatenate(dqs, axis=1),),
                (jnp.concatenate(dks, axis=1), jnp.concatenate(dvs, axis=1), dgq))

    return _rw(fn, [qx, do], [g_q, kn, vv], [(XA_WIDTH, BF16)], [kn.shape, vv.shape, g_q.shape], name, tm=512)


BIG = [
    ("w_in", (D_MODEL, IN_WIDTH), 1), ("ssm_w_glu", (SSM_WIDTH, SSM_WIDTH), 0), ("w_out", (D_MODEL, D_MODEL), 0),
    ("xa_w_q", (D_MODEL, XA_WIDTH), 0), ("xa_w_kv", (D_MODEL, 2 * XA_WIDTH), 0), ("xa_w_o", (XA_WIDTH, D_MODEL), 1),
    ("w_up", (D_MODEL, D_FF), 1), ("w_down", (D_FF, D_MODEL), 0),
]
BIG_INDEX = {n: i for i, (n, _, _) in enumerate(BIG)}


def _shard_shape(shape, axis):
    return tuple(d // N_DEV if i == axis else d for i, d in enumerate(shape))


def _shard_of(ref, axis, d):
    n = ref.shape[axis] // N_DEV
    return ref.at[pl.ds(d * n, n), :] if axis == 0 else ref.at[:, pl.ds(d * n, n)]


def _gather_side(names, shards):
    idxs = [BIG_INDEX[n] for n in names]

    def make(ins, outs, send_sems, recv_sems):
        x, y, c = lax.axis_index("x"), lax.axis_index("y"), lax.axis_index("c")
        cps = []
        for j, i in enumerate(idxs):
            mine = _shard_of(outs[j], BIG[i][2], 4 * x + 2 * y + c)
            cps.append(pltpu.make_async_copy(ins[j], mine, send_sems.at[N_DEV * j]))
            for rel in range(1, N_DEV):
                to = tuple(1 - p if rel >> bit & 1 else p for p, bit in ((x, 2), (y, 1), (c, 0)))
                cps.append(pltpu.make_async_remote_copy(
                    src_ref=ins[j], dst_ref=mine, send_sem=send_sems.at[N_DEV * j + rel],
                    recv_sem=recv_sems.at[N_DEV * j + rel], device_id=to, device_id_type=MESH))
        return cps

    return _Side(shards, [jax.ShapeDtypeStruct(BIG[i][1], BF16) for i in idxs], N_DEV * len(idxs), make)


def _gather_two_level_side(name, shard):
    i = BIG_INDEX[name]

    def parts(ins, outs, send_sems, recv_sems):
        x, y, c = lax.axis_index("x"), lax.axis_index("y"), lax.axis_index("c")
        sibling = (x, y, 1 - c)
        chips = [(1 - x, y), (x, 1 - y), (1 - x, 1 - y)]

        def place(dev):
            return _shard_of(outs[0], BIG[i][2], 4 * dev[0] + 2 * dev[1] + dev[2])

        def copy(k, blk, to, src=None):
            return pltpu.make_async_remote_copy(
                src_ref=place(blk) if src is None else src, dst_ref=place(blk), send_sem=send_sems.at[k],
                recv_sem=recv_sems.at[k], device_id=to, device_id_type=MESH)

        mine = pltpu.make_async_copy(ins[0], place((x, y, c)), send_sems.at[7])
        first = [copy(0, (x, y, c), sibling, src=ins[0])]
        first += [copy(1 + j, (x, y, c), (*chip, c), src=ins[0]) for j, chip in enumerate(chips)]
        passed = [copy(4 + j, (*chip, c), sibling) for j, chip in enumerate(chips)]
        arrived = [copy(1 + j, (*chip, c), (x, y, c)) for j, chip in enumerate(chips)]
        from_sibling = [copy(0, sibling, (x, y, c))] + [copy(4 + j, (*chip, 1 - c), (x, y, c))
                                                       for j, chip in enumerate(chips)]
        return mine, first, passed, arrived, from_sibling

    def make(ins, outs, send_sems, recv_sems):
        mine, first, _, _, _ = parts(ins, outs, send_sems, recv_sems)
        return [mine] + first

    def finish(ins, outs, send_sems, recv_sems):
        mine, first, passed, arrived, from_sibling = parts(ins, outs, send_sems, recv_sems)
        for got, onward in zip(arrived, passed):
            got.wait_recv()
            onward.start()
        for cp in from_sibling:
            cp.wait_recv()
        for cp in first + passed:
            cp.wait_send()
        mine.wait()

    return _Side([shard], [jax.ShapeDtypeStruct(BIG[i][1], BF16)], N_DEV, make, finish)


def _sibling_side(names, grads):
    idxs = [BIG_INDEX[n] for n in names]

    def make(ins, outs, send_sems, recv_sems):
        x, y, c = lax.axis_index("x"), lax.axis_index("y"), lax.axis_index("c")
        return [pltpu.make_async_remote_copy(
            src_ref=_shard_of(ins[j], BIG[i][2], 2 * k + (1 - c)), dst_ref=outs[j].at[k],
            send_sem=send_sems.at[4 * j + k], recv_sem=recv_sems.at[4 * j + k], device_id=(x, y, 1 - c),
            device_id_type=MESH) for j, i in enumerate(idxs) for k in range(4)]

    shapes = [jax.ShapeDtypeStruct((4,) + _shard_shape(BIG[i][1], BIG[i][2]), F32) for i in idxs]
    return _Side(grads, shapes, 4 * len(idxs), make)


def _chips_side(parts):
    def make(ins, outs, send_sems, recv_sems):
        x, y, c = lax.axis_index("x"), lax.axis_index("y"), lax.axis_index("c")
        chips = [(1 - x, y), (x, 1 - y), (1 - x, 1 - y)]
        return [pltpu.make_async_remote_copy(
            src_ref=ins[j].at[2 * cx + cy], dst_ref=outs[j].at[r], send_sem=send_sems.at[3 * j + r],
            recv_sem=recv_sems.at[3 * j + r], device_id=(cx, cy, c), device_id_type=MESH)
            for r, (cx, cy) in enumerate(chips) for j in range(len(parts))]

    return _Side(parts, [jax.ShapeDtypeStruct((3,) + p.shape[1:], p.dtype) for p in parts], 3 * len(parts), make)


def _reduce_add(grad, recv, axis, core, name):
    rs, cs = recv.shape[1:]
    rt = _row_tile(rs, 256)
    nt = rs // rt

    def body(c_ref, g_ref, r_ref, p_ref, pb_ref):
        sm = g_ref[...] + r_ref[0]
        p_ref[0] = sm
        pb_ref[0] = sm.astype(BF16)

    if axis == 0:
        g_spec = pl.BlockSpec((rt, cs), lambda k, t, c_ref: ((2 * k + c_ref[0]) * nt + t, 0))
    else:
        g_spec = pl.BlockSpec((rt, cs), lambda k, t, c_ref: (t, 2 * k + c_ref[0]))
    slab = pl.BlockSpec((1, rt, cs), lambda k, t, c_ref: (k, t, 0))
    return pl.pallas_call(
        body, name=name,
        grid_spec=pltpu.PrefetchScalarGridSpec(num_scalar_prefetch=1, grid=(4, nt), in_specs=[g_spec, slab],
                                               out_specs=[slab, slab]),
        out_shape=[jax.ShapeDtypeStruct(recv.shape, F32), jax.ShapeDtypeStruct(recv.shape, BF16)],
        compiler_params=_params(("parallel", "parallel")),
    )(core, grad, recv)


def _all_gather(block, name, side):
    m_per, n = block.shape
    ns_in, ns_out = len(side.ins), len(side.out_shapes)

    def body(*refs):
        x_ref, s_ins, out_ref = refs[0], refs[1:1 + ns_in], refs[1 + ns_in]
        s_outs = refs[2 + ns_in:2 + ns_in + ns_out]
        send_sems, recv_sems, local_sem, s_send, s_recv = refs[2 + ns_in + ns_out:]
        others = side.make(s_ins, s_outs, s_send, s_recv)
        for cp in others:
            cp.start()
        x, y, c = lax.axis_index("x"), lax.axis_index("y"), lax.axis_index("c")
        me, sibling = (x, y, c), (x, y, 1 - c)
        chips = [(1 - x, y), (x, 1 - y), (1 - x, 1 - y)]

        def rows(px, py, pc):
            return out_ref.at[pl.ds((4 * px + 2 * py + pc) * m_per, m_per), :]

        def copy(k, blk, to, src=None):
            return pltpu.make_async_remote_copy(
                src_ref=rows(*blk) if src is None else src, dst_ref=rows(*blk),
                send_sem=send_sems.at[k], recv_sem=recv_sems.at[k], device_id=to, device_id_type=MESH)

        mine = pltpu.make_async_copy(x_ref, rows(*me), local_sem)
        mine.start()
        first = [copy(0, me, sibling, src=x_ref)]
        first += [copy(1 + j, me, (*chip, c), src=x_ref) for j, chip in enumerate(chips)]
        for cp in first:
            cp.start()
        passed = [copy(4 + j, (*chip, c), sibling) for j, chip in enumerate(chips)]
        for j, chip in enumerate(chips):
            copy(1 + j, (*chip, c), me).wait_recv()
            passed[j].start()
        copy(0, sibling, me).wait_recv()
        for j, chip in enumerate(chips):
            copy(4 + j, (*chip, 1 - c), me).wait_recv()
        for cp in first + passed:
            cp.wait_send()
        mine.wait()
        for cp in others:
            cp.wait()

    res = pl.pallas_call(
        body, name=name, in_specs=[ANY] * (1 + ns_in), out_specs=[ANY] * (1 + ns_out),
        out_shape=[jax.ShapeDtypeStruct((N_DEV * m_per, n), block.dtype)] + side.out_shapes,
        scratch_shapes=[pltpu.SemaphoreType.DMA((7,)), pltpu.SemaphoreType.DMA((7,)), pltpu.SemaphoreType.DMA]
        + side.sems(),
    )(block, *side.ins)
    return res[0], list(res[1:])


def _adam_math(w, g, m, v):
    m = ADAM_B1 * m + (1.0 - ADAM_B1) * g
    v = ADAM_B2 * v + (1.0 - ADAM_B2) * (g * g)
    m_hat = m / (1.0 - ADAM_B1 ** ADAM_STEP)
    v_hat = v / (1.0 - ADAM_B2 ** ADAM_STEP)
    delta = -ADAM_LR * (m_hat / (jnp.sqrt(v_hat) + ADAM_EPS) + ADAM_WD * w)
    return delta, m, v


def _adam_sharded(own, recv, w, m, v, chip, name):
    rs, cs = w.shape
    rt = _row_tile(rs, 256)

    def body(chip_ref, p_ref, r_ref, w_ref, m_ref, v_ref, g_out, d_out, m_out, v_out):
        g = p_ref[0] + r_ref[0].astype(F32) + r_ref[1].astype(F32) + r_ref[2].astype(F32)
        d, mn, vn = _adam_math(w_ref[...], g, m_ref[...], v_ref[...])
        g_out[...] = g
        d_out[...] = d
        m_out[...] = mn
        v_out[...] = vn

    tile = pl.BlockSpec((rt, cs), lambda t, chip_ref: (t, 0))
    return pl.pallas_call(
        body, name=name,
        grid_spec=pltpu.PrefetchScalarGridSpec(
            num_scalar_prefetch=1, grid=(rs // rt,),
            in_specs=[pl.BlockSpec((1, rt, cs), lambda t, chip_ref: (chip_ref[0], t, 0)),
                      pl.BlockSpec((3, rt, cs), lambda t, chip_ref: (0, t, 0)), tile, tile, tile],
            out_specs=[tile] * 4),
        out_shape=[jax.ShapeDtypeStruct((rs, cs), F32)] * 4,
        compiler_params=_params(("parallel",)),
    )(chip, own, recv, w, m, v)


SMALL = ["g_mix", "ssm_a_re", "ssm_a_im", "ssm_log_dt", "ssm_b_re", "ssm_b_im", "ssm_c_re", "ssm_c_im", "ssm_d",
         "sb_g_q", "sb_g_k", "g_out_ssm", "g_out_sb", "g_xa", "g_mem", "xa_g_q", "xa_g_k", "g_mlp"]
PACK_TILE = SUBLANES * LANES


def _natural_2d(n):
    return (n // LANES, LANES) if n % LANES == 0 else (1, n)


def _pack_small(arrs):
    parts = []
    for a in arrs:
        flat = a.reshape(-1)
        parts.append(jnp.pad(flat, (0, (-flat.shape[0]) % PACK_TILE)))
    return jnp.concatenate(parts).reshape(-1, LANES)


def _adam_replicated(gathered, sizes, ws, ms, vs, name):
    n_w = len(ws)
    r_dev = gathered.shape[0] // N_DEV
    offs, off = [], 0
    for n in sizes:
        offs.append(off)
        off += (n + PACK_TILE - 1) // PACK_TILE * SUBLANES
    assert off == r_dev

    def body(*refs):
        g_ref = refs[0]
        w_refs, m_refs, v_refs = refs[1:1 + n_w], refs[1 + n_w:1 + 2 * n_w], refs[1 + 2 * n_w:1 + 3 * n_w]
        outs = refs[1 + 3 * n_w:]

        def total(i, shape):
            r, cdim = shape
            acc = g_ref[pl.ds(offs[i], r), :cdim]
            for d in range(1, N_DEV):
                acc = acc + g_ref[pl.ds(d * r_dev + offs[i], r), :cdim]
            return acc

        for i in range(n_w):
            g = total(i, w_refs[i].shape)
            d, mn, vn = _adam_math(w_refs[i][...], g, m_refs[i][...], v_refs[i][...])
            for o, val in zip(outs[4 * i:4 * i + 4], (g, d, mn, vn)):
                o[...] = val
        outs[4 * n_w][...] = total(n_w, (SUBLANES, LANES))

    shapes = [w.shape for w in ws]
    return pl.pallas_call(
        body, name=name,
        out_shape=[jax.ShapeDtypeStruct(shp, F32) for shp in shapes for _ in range(4)]
        + [jax.ShapeDtypeStruct((SUBLANES, LANES), F32)],
        compiler_params=_params(),
    )(gathered, *ws, *ms, *vs)


def _step(x, mem, target, shards, sm, core):
    g, w, sums, reduced = {}, {}, {}, {}

    def gather(names):
        return _gather_side(names, [shards[n] for n in names])

    def to_sibling(names):
        return _sibling_side(names, [g[n] for n in names])

    def add_sibling(names, received):
        for n, r in zip(names, received):
            sums[n] = _reduce_add(g[n], r, BIG[BIG_INDEX[n]][2], core, "reduce_add_" + n)

    def to_chips(names):
        return _chips_side([sums[n][1] for n in names])

    def keep(names, received):
        for n, r in zip(names, received):
            reduced[n] = (sums[n][0], r)

    row = lambda a: a.reshape(1, -1)
    g_mix, g_xa, g_mlp, g_mem = row(sm["g_mix"]), row(sm["g_xa"]), row(sm["g_mlp"]), row(sm["g_mem"])
    g_os, g_ob = row(sm["g_out_ssm"]), row(sm["g_out_sb"])
    sb_gq, sb_gk = jnp.tile(row(sm["sb_g_q"]), (1, SB_HEADS)), jnp.tile(row(sm["sb_g_k"]), (1, SB_HEADS))
    xa_gq, xa_gk = row(sm["xa_g_q"]), row(sm["xa_g_k"])
    d_skip = row(sm["ssm_d"])

    h1, (w["w_in"],) = _norm_fwd(x, g_mix, "norm_mix", side=_gather_two_level_side("w_in", shards["w_in"]))
    proj = _mm(h1, w["w_in"], "nn", "in_proj", tn=IN_WIDTH)
    u = _to_segments(proj[:, :SSM_WIDTH])
    q_raw, k_raw = (proj, SB_WIDTH, 1), (proj, SB_WIDTH, 2)
    v_col = (SSM_WIDTH + 2 * SB_WIDTH) // LANES
    sb_scale = SB_HEAD_DIM ** -0.5
    qs, ks = _rw(lambda qt, kt, gq, gk: ((_rms_groups(qt, gq, sb_scale), _rms_groups(kt, gk, 1.0)), ()),
                 [q_raw, k_raw], [sb_gq, sb_gk], [(SB_WIDTH, BF16)] * 2, [], "sb_qk_norm")
    early = ["ssm_w_glu", "w_out", "xa_w_q", "xa_w_kv", "xa_w_o", "w_up"]
    y_sb, got = _sb_fwd(qs, ks, proj, "sb_fwd", v_col=v_col, side=gather(early))
    w.update(zip(early, got))

    ssm_args = _states_on_lanes(sm)
    acat, bsup, csup = _ssm_mats_fwd(*ssm_args, "ssm_mats")
    (states, y0, y1), (w["w_down"],) = _ssm_fwd(u, acat, bsup, csup, d_skip, "ssm_fwd", side=gather(["w_down"]))
    z_glu, y_ssm = _mm(y1, w["ssm_w_glu"], "nn", "ssm_glu", epi=lambda r, yt: (r, yt * jax.nn.sigmoid(r)),
                       extras=(y1,), out_dtypes=(F32, F32))
    y_ssm = _from_segments(y_ssm)

    def cat_norm(a, b, ga, gb):
        return jnp.concatenate([_rms(a, ga), _rms(b, gb)], axis=1)

    ycat = _rw(lambda a, b, ga, gb: ((cat_norm(a, b, ga, gb),), ()), [y_ssm, y_sb], [g_os, g_ob],
               [(D_MODEL, BF16)], [], "norm_out")[0]

    def residual_norm_epi(r, xt, gt):
        xn = r + xt
        return xn, _rms(xn, gt)

    x1, h2 = _mm(ycat, w["w_out"], "nn", "out_proj", epi=residual_norm_epi, extras=(x,), fulls=(g_xa,),
                 out_dtypes=(F32, BF16))
    qx = _mm(h2, w["xa_w_q"], "nn", "xa_q")
    memn, kv, kn_x, vv_x = _mem_fwd(mem, g_mem, w["xa_w_kv"], xa_gk, "xa_mem")
    o_xa = _xa_fwd(qx, xa_gq, kn_x, vv_x, "xa_fwd")
    x2, h3 = _mm(o_xa, w["xa_w_o"], "nn", "xa_o", epi=residual_norm_epi, extras=(x1,), fulls=(g_mlp,),
                 out_dtypes=(F32, BF16))

    def up_epi(r):
        rl = jnp.maximum(r, 0.0)
        return (rl * rl,)

    r_up = _mm(h3, w["w_up"], "nn", "mlp_up", epi=up_epi, out_dtypes=(BF16,), tm=2048, tn=2048)

    def loss_epi(r, xt, tt):
        d = r + xt - tt
        return (d * (1.0 / D_MODEL),) * 2, (jnp.sum(d * d, axis=0, keepdims=True),)

    dx3, dx3_b, sq = _mm(r_up, w["w_down"], "nn", "mlp_down", epi=loss_epi, extras=(x2, target),
                         out_dtypes=(F32, BF16), sums=[(1, D_MODEL)])
    loss = jnp.sum(sq) * (0.5 / D_MODEL)

    def norm_bwd_epi(r, xt, drt, gt):
        _, vjp = jax.vjp(_rms, xt, gt)
        dx_, dg_ = vjp(r)
        return (dx_ + drt,) * 2, (dg_,)

    g["w_down"] = _mm(r_up, dx3_b, "tn", "d_w_down", tk=2048)
    da = _mm(dx3_b, w["w_down"], "nt", "d_r", epi=lambda r, rt: (r * 2.0 * jnp.sqrt(rt.astype(F32)),), extras=(r_up,),
             out_dtypes=(BF16,), tn=2048)
    g["w_up"] = _mm(h3, da, "tn", "d_w_up", tk=2048)
    mlp = ["w_down", "w_up"]
    (dx2, dx2_b, g["g_mlp"]), got = _mm(da, w["w_up"], "nt", "d_h3", epi=norm_bwd_epi, extras=(x2, dx3),
                                        fulls=(g_mlp,), out_dtypes=(F32, BF16), sums=[g_mlp.shape],
                                        side=to_sibling(mlp))
    add_sibling(mlp, got)
    g["xa_w_o"] = _mm(o_xa, dx2_b, "tn", "d_xa_w_o", tk=2048)
    do_xa = _mm(dx2_b, w["xa_w_o"], "nt", "d_o_xa")
    dqx, dkn_x, dvv_x, g["xa_g_q"] = _xa_bwd(qx, xa_gq, kn_x, vv_x, do_xa, "xa_bwd")
    g["xa_w_kv"], g["g_mem"], g["xa_g_k"] = _mem_bwd(mem, g_mem, memn, w["xa_w_kv"], kv, xa_gk, dkn_x, dvv_x,
                                                     "xa_mem_bwd")
    g["xa_w_q"] = _mm(h2, dqx, "tn", "d_xa_w_q", tk=2048)
    dx1, dx1_b, g["g_xa"] = _mm(dqx, w["xa_w_q"], "nt", "d_h2", epi=norm_bwd_epi, extras=(x1, dx2), fulls=(g_xa,),
                                out_dtypes=(F32, BF16), sums=[g_xa.shape])
    g["w_out"] = _mm(ycat, dx1_b, "tn", "d_w_out", tk=2048)
    dycat = _mm(dx1_b, w["w_out"], "nt", "d_ycat")

    def cat_bwd(a, b, dy, ga, gb):
        _, vjp = jax.vjp(cat_norm, a, b, ga, gb)
        da_, db_, dga, dgb = vjp(dy)
        return (da_, db_), (dga, dgb)

    dy_ssm, dy_sb, g["g_out_ssm"], g["g_out_sb"] = _rw(
        cat_bwd, [y_ssm, y_sb, dycat], [g_os, g_ob], [(SSM_WIDTH, F32), (SB_WIDTH, F32)], [g_os.shape, g_ob.shape],
        "d_norm_out", tm=512)

    def glu_bwd(dy, yt, zt):
        sg = jax.nn.sigmoid(zt)
        return (dy * sg, dy * yt * sg * (1.0 - sg)), ()

    dy1_a, dz = _rw(glu_bwd, [_to_segments(dy_ssm), y1, z_glu], [], [(SSM_WIDTH, F32), (SSM_WIDTH, BF16)], [], "d_glu")
    g["ssm_w_glu"] = _mm(y1, dz, "tn", "d_w_glu", tk=2048)

    def gelu_bwd_epi(r, da_, y0t):
        _, vjp = jax.vjp(jax.nn.gelu, y0t)
        return (vjp(r + da_)[0],)

    mid = ["w_out", "xa_w_q", "xa_w_kv", "xa_w_o", "ssm_w_glu"]
    dy0, got = _mm(dz, w["ssm_w_glu"], "nt", "d_y1", epi=gelu_bwd_epi, extras=(dy1_a, y0), side=to_sibling(mid))
    add_sibling(mid, got)
    (du, da8, d_bsup, d_csup, g["ssm_d"]), got = _ssm_bwd(dy0, states, u, acat, bsup, csup, d_skip, "ssm_bwd",
                                                          side=to_chips(mlp))
    keep(mlp, got)
    d_acat = jnp.sum(da8, axis=0, keepdims=True)
    d_mats = _ssm_mats_bwd(*ssm_args[:5], d_acat, d_bsup, d_csup, "ssm_mats_bwd")
    for nm, val in zip(("ssm_a_re", "ssm_a_im", "ssm_log_dt", "ssm_b_re", "ssm_b_im", "ssm_c_re", "ssm_c_im"),
                       _from_states_on_lanes(*d_mats)):
        g[nm] = val

    (dqs, dks, dvs), got = _sb_bwd(qs, ks, proj, y_sb, dy_sb, "sb_bwd", v_col=v_col, side=to_chips(mid))
    keep(mid, got)

    def d_proj_rows(du_t, qt, dqt, kt, dkt, dvt, gq, gk):
        _, vjp_q = jax.vjp(lambda a, b_: _rms_groups(a, b_, sb_scale), qt, gq)
        _, vjp_k = jax.vjp(lambda a, b_: _rms_groups(a, b_, 1.0), kt, gk)
        (dq_, dgq_), (dk_, dgk_) = vjp_q(dqt), vjp_k(dkt)
        rows = jnp.concatenate([du_t, dq_.astype(BF16), dk_.astype(BF16), dvt.astype(BF16)], axis=1)
        return (rows,), (dgq_, dgk_)

    dproj, dgq, dgk = _rw(d_proj_rows, [_from_segments(du), q_raw, dqs, k_raw, dks, dvs], [sb_gq, sb_gk],
                          [(IN_WIDTH, BF16)], [sb_gq.shape, sb_gk.shape], "d_proj", tm=512)
    g["sb_g_q"] = jnp.sum(dgq.reshape(SB_HEADS, SB_HEAD_DIM), axis=0)
    g["sb_g_k"] = jnp.sum(dgk.reshape(SB_HEADS, SB_HEAD_DIM), axis=0)
    g["w_in"] = _mm(h1, dproj, "tn", "d_w_in", tn=IN_WIDTH)
    dh1, got = _mm(dproj, w["w_in"], "nt", "d_h1", tk=IN_WIDTH, side=to_sibling(["w_in"]))
    add_sibling(["w_in"], got)
    dx, g["g_mix"] = _norm_bwd(x, g_mix, dh1, dx1, "d_norm_mix")

    packed = _pack_small([g[n] for n in SMALL] + [loss.reshape(1)])
    everyone, got = _all_gather(packed, "gather_small", to_chips(["w_in"]))
    keep(["w_in"], got)
    return dx, everyone, reduced


def kernel(x, mem, g_mix, w_in, ssm_a_re, ssm_a_im, ssm_log_dt, ssm_b_re, ssm_b_im, ssm_c_re, ssm_c_im, ssm_d, ssm_w_glu, sb_g_q, sb_g_k, g_out_ssm, g_out_sb, w_out, g_xa, g_mem, xa_w_q, xa_w_kv, xa_g_q, xa_g_k, xa_w_o, g_mlp, w_up, w_down, loss_target, m_g_mix, m_w_in, m_ssm_a_re, m_ssm_a_im, m_ssm_log_dt, m_ssm_b_re, m_ssm_b_im, m_ssm_c_re, m_ssm_c_im, m_ssm_d, m_ssm_w_glu, m_sb_g_q, m_sb_g_k, m_g_out_ssm, m_g_out_sb, m_w_out, m_g_xa, m_g_mem, m_xa_w_q, m_xa_w_kv, m_xa_g_q, m_xa_g_k, m_xa_w_o, m_g_mlp, m_w_up, m_w_down, v_g_mix, v_w_in, v_ssm_a_re, v_ssm_a_im, v_ssm_log_dt, v_ssm_b_re, v_ssm_b_im, v_ssm_c_re, v_ssm_c_im, v_ssm_d, v_ssm_w_glu, v_sb_g_q, v_sb_g_k, v_g_out_ssm, v_g_out_sb, v_w_out, v_g_xa, v_g_mem, v_xa_w_q, v_xa_w_kv, v_xa_g_q, v_xa_g_k, v_xa_w_o, v_g_mlp, v_w_up, v_w_down):
    given = dict(locals())
    order = ["g_mix", "w_in", "ssm_a_re", "ssm_a_im", "ssm_log_dt", "ssm_b_re", "ssm_b_im", "ssm_c_re", "ssm_c_im",
             "ssm_d", "ssm_w_glu", "sb_g_q", "sb_g_k", "g_out_ssm", "g_out_sb", "w_out", "g_xa", "g_mem", "xa_w_q",
             "xa_w_kv", "xa_g_q", "xa_g_k", "xa_w_o", "g_mlp", "w_up", "w_down"]
    assert sorted([n for n, _, _ in BIG] + SMALL) == sorted(order)
    core = lax.axis_index("c").astype(jnp.int32).reshape(1)
    chip = (2 * lax.axis_index("x") + lax.axis_index("y")).astype(jnp.int32).reshape(1)

    shards = {n: given[n][0].astype(BF16) for n, _, _ in BIG}
    sm = {n: given[n][0] for n in SMALL}
    dx, everyone, reduced = _step(x[0], mem[0], loss_target[0], shards, sm, core)

    res = {}
    for n, _, _ in BIG:
        own, recv = reduced[n]
        outs = _adam_sharded(own, recv, given[n][0], given["m_" + n][0], given["v_" + n][0], chip, "adam_" + n)
        for kind, val in zip(("grad", "delta", "new_m", "new_v"), outs):
            res[kind + "_" + n] = val[None]

    sizes = [math.prod(sm[n].shape) for n in SMALL] + [1]
    nat = lambda a: a.reshape(_natural_2d(math.prod(a.shape)))
    outs = _adam_replicated(everyone, sizes, [nat(sm[n]) for n in SMALL], [nat(given["m_" + n][0]) for n in SMALL],
                            [nat(given["v_" + n][0]) for n in SMALL], "adam_replicated")
    for i, n in enumerate(SMALL):
        for kind, val in zip(("grad", "delta", "new_m", "new_v"), outs[4 * i:4 * i + 4]):
            res[kind + "_" + n] = val.reshape(given[n].shape)
    loss_out = outs[-1][0, 0]
    return (loss_out, dx[None], *[res["grad_" + n] for n in order], *[res["delta_" + n] for n in order],
            *[res["new_m_" + n] for n in order], *[res["new_v_" + n] for n in order])
```

```python
import functools
import math

import jax
import jax.numpy as jnp
from jax import lax
from jax.experimental import pallas as pl
from jax.experimental.pallas import tpu as pltpu

F32 = jnp.float32
BF16 = jnp.bfloat16
MESH = pl.DeviceIdType.MESH

N_DEV = 8
D_MODEL = 1024
SSM_WIDTH = 512
SSM_GROUP = 16
SSM_GROUPS = 32
SSM_STATE = 64
N_STATE = SSM_GROUPS * SSM_STATE
SB_HEADS = 8
SB_HEAD_DIM = 64
SB_WIDTH = 512
IN_WIDTH = 2048
XA_HEADS = 4
XA_HEAD_DIM = 128
XA_WIDTH = 512
D_FF = 4096
NORM_EPS = 1e-6
ADAM_LR = 0.001
ADAM_B1 = 0.9
ADAM_B2 = 0.999
ADAM_EPS = 1e-08
ADAM_WD = 0.01
ADAM_STEP = 10

LANES = 128
SUBLANES = 8
VMEM_LIMIT = 56 * 1024 * 1024
SCAN_LANES = 512
SB_BLOCK = 256
SB_Q_BLOCKS = 4
SB_UNDERFLOW = -110.0

NN = (((1,), (0,)), ((), ()))
NT = (((1,), (1,)), ((), ()))
TN = (((0,), (0,)), ((), ()))


def _params(sem=None):
    return pltpu.CompilerParams(dimension_semantics=sem, vmem_limit_bytes=VMEM_LIMIT)


def _dot(a, b, dims=NN):
    return lax.dot_general(a.astype(BF16), b.astype(BF16), dims, preferred_element_type=F32)


def _rms(x, g):
    return x * lax.rsqrt(jnp.mean(x * x, axis=-1, keepdims=True) + NORM_EPS) * g


ANY = pl.BlockSpec(memory_space=pl.ANY)


class _Side:
    def __init__(self, ins, out_shapes, n_sem, make, finish=None):
        self.ins, self.out_shapes, self.n_sem, self.make = list(ins), list(out_shapes), n_sem, make
        self.finish = finish

    def sems(self):
        return [pltpu.SemaphoreType.DMA((self.n_sem,)), pltpu.SemaphoreType.DMA((self.n_sem,))]


def _hosted(body, side, n_in, n_out, grid):
    if side is None:
        return body
    ns_in, ns_out = len(side.ins), len(side.out_shapes)

    def wrapped(*refs):
        ins, refs = refs[:n_in], refs[n_in:]
        s_ins, refs = refs[:ns_in], refs[ns_in:]
        outs, refs = refs[:n_out], refs[n_out:]
        s_outs, refs = refs[:ns_out], refs[ns_out:]
        scratch, sems = refs[:-2], refs[-2:]
        ids = [pl.program_id(d) for d in range(len(grid))]
        first = functools.reduce(jnp.logical_and, [i == 0 for i in ids])
        last = functools.reduce(jnp.logical_and, [i == n - 1 for i, n in zip(ids, grid)])

        @pl.when(first)
        def _():
            for cp in side.make(s_ins, s_outs, *sems):
                cp.start()

        body(*ins, *outs, *scratch)

        @pl.when(last)
        def _():
            if side.finish is not None:
                side.finish(s_ins, s_outs, *sems)
            else:
                for cp in side.make(s_ins, s_outs, *sems):
                    cp.wait()

    return wrapped


def _side_args(side):
    if side is None:
        return [], [], [], [], []
    return ([ANY] * len(side.ins), [ANY] * len(side.out_shapes), side.out_shapes, side.sems(), side.ins)


def _split_side(res, n_out, side):
    res = list(res)
    main = res[0] if n_out == 1 else res[:n_out]
    return main if side is None else (main, res[n_out:])


def _mm(a, b, mode, name, *, epi=None, extras=(), fulls=(), out_dtypes=(F32,), sums=(), tm=1024, tn=1024, tk=1024,
        side=None):
    if mode == "nn":
        (m, k), (k2, n) = a.shape, b.shape
    elif mode == "nt":
        (m, k), (n, k2) = a.shape, b.shape
    else:
        (k, m), (k2, n) = a.shape, b.shape
    assert k == k2, (name, a.shape, b.shape)
    tm, tn, tk = min(tm, m), min(tn, n), min(tk, k)
    assert m % tm == 0 and n % tn == 0 and k % tk == 0, (name, m, n, k)
    nk = k // tk
    dims = {"nn": NN, "nt": NT, "tn": TN}[mode]
    if mode == "tn":
        a_spec = pl.BlockSpec((tk, tm), lambda i, j, kk: (kk, i))
    else:
        a_spec = pl.BlockSpec((tm, tk), lambda i, j, kk: (i, kk))
    if mode == "nt":
        b_spec = pl.BlockSpec((tn, tk), lambda i, j, kk: (j, kk))
    else:
        b_spec = pl.BlockSpec((tk, tn), lambda i, j, kk: (kk, j))
    mn_spec = pl.BlockSpec((tm, tn), lambda i, j, kk: (i, j))
    n_ex, n_full, n_out, n_sum = len(extras), len(fulls), len(out_dtypes), len(sums)
    n_in = 2 + n_ex + n_full

    def body(*refs):
        a_ref, b_ref = refs[:2]
        ex = refs[2:n_in]
        outs = refs[n_in:n_in + n_out]
        sum_refs = refs[n_in + n_out:n_in + n_out + n_sum]
        kk = pl.program_id(2)
        first_tile = jnp.logical_and(pl.program_id(0) == 0, pl.program_id(1) == 0)

        def finish(r):
            vals = epi(r, *[e[...] for e in ex]) if epi is not None else (r,)
            if n_sum:
                vals, parts = vals

                @pl.when(first_tile)
                def _():
                    for sr in sum_refs:
                        sr[...] = jnp.zeros_like(sr)

                for sr, p in zip(sum_refs, parts):
                    sr[...] += p
            for o, v in zip(outs, vals):
                o[...] = v.astype(o.dtype)

        if nk == 1:
            finish(_dot(a_ref[...], b_ref[...], dims))
        else:
            acc = refs[n_in + n_out + n_sum]

            @pl.when(kk == 0)
            def _():
                acc[...] = jnp.zeros_like(acc)

            acc[...] += _dot(a_ref[...], b_ref[...], dims)

            @pl.when(kk == nk - 1)
            def _():
                finish(acc[...])

    grid = (m // tm, n // tn, nk)
    whole = lambda shape: pl.BlockSpec(shape, lambda i, j, kk: (0,) * len(shape))
    s_in, s_out, s_shape, s_scratch, s_ops = _side_args(side)
    seq = bool(side) or n_sum > 0
    res = pl.pallas_call(
        _hosted(body, side, n_in, n_out + n_sum, grid), name=name, grid=grid,
        in_specs=[a_spec, b_spec] + [mn_spec] * n_ex + [whole(f.shape) for f in fulls] + s_in,
        out_specs=[mn_spec] * n_out + [whole(shape) for shape in sums] + s_out,
        out_shape=[jax.ShapeDtypeStruct((m, n), dt) for dt in out_dtypes]
        + [jax.ShapeDtypeStruct(shape, F32) for shape in sums] + s_shape,
        scratch_shapes=([pltpu.VMEM((tm, tn), F32)] if nk > 1 else []) + s_scratch,
        compiler_params=_params(("arbitrary",) * 3 if seq else ("parallel", "parallel", "arbitrary")),
    )(a, b, *extras, *fulls, *s_ops)
    return _split_side(res, n_out + n_sum, side)


def _row_tile(s, target):
    if s <= target:
        return s
    return max(t for t in range(16, target + 1, 16) if s % t == 0)


def _rw(fn, rows, fulls, row_out, acc_out, name, tm=1024, side=None):
    cols = [r[1:] if isinstance(r, tuple) else (r.shape[1], 0) for r in rows]
    rows = [r[0] if isinstance(r, tuple) else r for r in rows]
    s = rows[0].shape[0]
    tm = _row_tile(s, tm)
    nr, nf, nro, nao = len(rows), len(fulls), len(row_out), len(acc_out)

    def body(*refs):
        r = refs[:nr]
        f = refs[nr:nr + nf]
        ro = refs[nr + nf:nr + nf + nro]
        ao = refs[nr + nf + nro:]
        outs, accs = fn(*[x[...] for x in r], *[x[...] for x in f])
        for o, v in zip(ro, outs):
            o[...] = v.astype(o.dtype)
        if nao:
            @pl.when(pl.program_id(0) == 0)
            def _():
                for a in ao:
                    a[...] = jnp.zeros_like(a)

            for a, v in zip(ao, accs):
                a[...] += v

    full_spec = lambda shape: pl.BlockSpec(shape, lambda i: (0,) * len(shape))
    s_in, s_out, s_shape, s_scratch, s_ops = _side_args(side)
    res = pl.pallas_call(
        _hosted(body, side, nr + nf, nro + nao, (s // tm,)), name=name, grid=(s // tm,),
        in_specs=[pl.BlockSpec((tm, wd), functools.partial(lambda i, cb: (i, cb), cb=cb)) for wd, cb in cols]
        + [full_spec(x.shape) for x in fulls] + s_in,
        out_specs=[pl.BlockSpec((tm, d), lambda i: (i, 0)) for d, _ in row_out]
        + [full_spec(shape) for shape in acc_out] + s_out,
        out_shape=[jax.ShapeDtypeStruct((s, d), dt) for d, dt in row_out]
        + [jax.ShapeDtypeStruct(shape, F32) for shape in acc_out] + s_shape,
        scratch_shapes=s_scratch,
        compiler_params=_params(("arbitrary",)),
    )(*rows, *fulls, *s_ops)
    res = list(res)
    return res if side is None else (res[:nro + nao], res[nro + nao:])


def _norm_fwd(x, g, name, side=None):
    res = _rw(lambda xt, gt: ((_rms(xt, gt),), ()), [x], [g], [(x.shape[1], BF16)], [], name, side=side)
    return res[0] if side is None else (res[0][0], res[1])


def _norm_bwd(x, g, dh, dres, name, side=None):
    def fn(xt, dht, drt, gt):
        _, vjp = jax.vjp(_rms, xt, gt)
        dx, dg = vjp(dht)
        return (dx + drt,), (dg,)

    return _rw(fn, [x, dh, dres], [g], [(x.shape[1], F32)], [g.shape], name, side=side)


def _rms_groups(x, g, scale):
    lo = lax.broadcasted_iota(jnp.int32, (1, LANES), 1) < SB_HEAD_DIM
    x2 = x * x
    outs = []
    for cb in range(x.shape[1] // LANES):
        sl = slice(cb * LANES, (cb + 1) * LANES)
        s_lo = jnp.sum(jnp.where(lo, x2[:, sl], 0.0), axis=-1, keepdims=True)
        s_hi = jnp.sum(jnp.where(lo, 0.0, x2[:, sl]), axis=-1, keepdims=True)
        r = jnp.where(lo, lax.rsqrt(s_lo * (1.0 / SB_HEAD_DIM) + NORM_EPS),
                      lax.rsqrt(s_hi * (1.0 / SB_HEAD_DIM) + NORM_EPS))
        outs.append(x[:, sl] * r)
    return jnp.concatenate(outs, axis=1) * g * scale


def _log_sigmoid(z):
    return jnp.minimum(z, 0.0) - jnp.log(1.0 + jnp.exp(-jnp.abs(z)))


def _split_dot(x, u2):
    hi = x.astype(BF16)
    lo = (x - hi.astype(F32)).astype(BF16)
    return jnp.dot(jnp.concatenate([hi, lo], axis=1), u2, preferred_element_type=F32)


def _sb_consts(b):
    row = lax.broadcasted_iota(jnp.int32, (b, b), 0)
    col = lax.broadcasted_iota(jnp.int32, (b, b), 1)
    tri = col < row
    u_after = (row > col).astype(BF16)
    u_from = (row >= col).astype(BF16)
    stack = lambda u: jnp.concatenate([u, u], axis=0)
    lane_lo = lax.broadcasted_iota(jnp.int32, (b, LANES), 1) < SB_HEAD_DIM
    return tri, stack(u_after), stack(u_from), lane_lo


def _sb_scores(qh, kb, a_run, keep, u2_after, mask_l=True):
    z = lax.dot_general(qh, kb, NT, preferred_element_type=F32)
    lb = _log_sigmoid(z)
    l = lb - z
    if keep is not None and mask_l:
        l = jnp.where(keep, l, 0.0)
    w = jnp.exp(lb + (a_run + _split_dot(l, u2_after)))
    if keep is not None:
        w = jnp.where(keep, w, 0.0)
    return lb, l, w


def _sb_walk(qi, carry, step):
    def cond(state):
        n, c = state
        return jnp.logical_and(n <= qi, jnp.max(jnp.maximum(c[0], c[1])) > SB_UNDERFLOW)

    def body(state):
        n, c = state
        return n + 1, step(n, c)

    return lax.while_loop(cond, body, (jnp.int32(2), carry))[1]


def _two_heads(x, lane_lo):
    zero = jnp.zeros_like(x)
    return jnp.where(lane_lo, x, zero), jnp.where(lane_lo, zero, x)


def _sb_fwd(qs, ks, v, name, v_col=0, side=None):
    s, width = qs.shape
    b = min(SB_BLOCK, s)
    nqb = min(SB_Q_BLOCKS, s // b)

    def body(q_ref, k_ref, v_ref, o_ref):
        tri, u2_after, _, lane_lo = _sb_consts(b)
        zero = jnp.zeros((b, 1), F32)
        started = []
        for h in range(nqb):
            qi = pl.program_id(1) * nqb + h
            q_a, q_b = _two_heads(q_ref[h * b:(h + 1) * b, :], lane_lo)

            def step(n, carry, keep, mask_l=True, qi=qi, q_a=q_a, q_b=q_b):
                a_a, a_b, acc = carry
                off = pl.multiple_of(jnp.maximum(qi - n, 0) * b, b)
                kb = k_ref[pl.ds(off, b), :]
                v_a, v_b = _two_heads(v_ref[pl.ds(off, b), :].astype(BF16), lane_lo)
                _, l_a, w_a = _sb_scores(q_a, kb, a_a, keep, u2_after, mask_l)
                _, l_b, w_b = _sb_scores(q_b, kb, a_b, keep, u2_after, mask_l)
                acc = acc + jnp.dot(jnp.concatenate([w_a.astype(BF16), w_b.astype(BF16)], axis=1),
                                    jnp.concatenate([v_a, v_b], axis=0), preferred_element_type=F32)
                return (a_a + jnp.sum(l_a, axis=1, keepdims=True), a_b + jnp.sum(l_b, axis=1, keepdims=True), acc)

            carry = step(0, (zero, zero, jnp.zeros((b, LANES), F32)), tri)
            carry = step(1, carry, jnp.broadcast_to(qi > 0, tri.shape), mask_l=False)
            started.append((qi, step, carry))
        for h, (qi, step, carry) in enumerate(started):
            carry = _sb_walk(qi, carry, lambda n, c, step=step: step(n, c, None))
            o_ref[h * b:(h + 1) * b, :] = carry[2]

    blk = pl.BlockSpec((nqb * b, LANES), lambda hp, i: (i, hp))
    full = pl.BlockSpec((s, LANES), lambda hp, i: (0, hp))
    full_v = pl.BlockSpec((s, LANES), lambda hp, i: (0, hp + v_col))
    grid = (width // LANES, s // (nqb * b))
    s_in, s_out, s_shape, s_scratch, s_ops = _side_args(side)
    res = pl.pallas_call(
        _hosted(body, side, 3, 1, grid), name=name, grid=grid,
        in_specs=[blk, full, full_v] + s_in, out_specs=[blk] + s_out,
        out_shape=[jax.ShapeDtypeStruct((s, width), F32)] + s_shape, scratch_shapes=s_scratch,
        compiler_params=_params(("arbitrary", "arbitrary")),
    )(qs, ks, v, *s_ops)
    return _split_side(res, 1, side)


def _sb_bwd(qs, ks, v, out, dout, name, v_col=0, side=None):
    s, width = qs.shape
    b = min(SB_BLOCK, s)
    nqb = min(SB_Q_BLOCKS, s // b)

    def body(q_ref, k_ref, v_ref, o_ref, do_ref, dq_ref, dk_ref, dv_ref):
        @pl.when(pl.program_id(1) == 0)
        def _():
            dk_ref[...] = jnp.zeros_like(dk_ref)
            dv_ref[...] = jnp.zeros_like(dv_ref)

        tri, u2_after, u2_from, lane_lo = _sb_consts(b)
        zero = jnp.zeros((b, 1), F32)

        def head(qh, doh, kb, vb, a_run, d_rem, keep, mask_l):
            lb, l, w = _sb_scores(qh, kb, a_run, keep, u2_after, mask_l)
            wb = w.astype(BF16)
            g = lax.dot_general(doh, vb, NT, preferred_element_type=F32) * wb.astype(F32)
            g_before = d_rem - _split_dot(g, u2_from)
            dz = g - (g + g_before) * jnp.exp(lb)
            if keep is not None:
                dz = jnp.where(keep, dz, 0.0)
            return (dz.astype(BF16), wb, a_run + jnp.sum(l, axis=1, keepdims=True),
                    d_rem - jnp.sum(g, axis=1, keepdims=True))

        started = []
        for h in range(nqb):
            qi = pl.program_id(1) * nqb + h
            rows = slice(h * b, (h + 1) * b)
            q_a, q_b = _two_heads(q_ref[rows, :], lane_lo)
            dob = do_ref[rows, :].astype(BF16)
            do_a, do_b = _two_heads(dob, lane_lo)
            prod = dob.astype(F32) * o_ref[rows, :]
            d_a = jnp.sum(jnp.where(lane_lo, prod, 0.0), axis=1, keepdims=True)
            d_b = jnp.sum(jnp.where(lane_lo, 0.0, prod), axis=1, keepdims=True)
            q_rows = jnp.concatenate([q_a, q_b], axis=0)
            do_rows = jnp.concatenate([do_a, do_b], axis=0)

            def step(n, carry, keep, mask_l=True, qi=qi, q_a=q_a, q_b=q_b, do_a=do_a, do_b=do_b, q_rows=q_rows,
                     do_rows=do_rows):
                a_a, a_b, r_a, r_b, dq = carry
                off = pl.multiple_of(jnp.maximum(qi - n, 0) * b, b)
                kb = k_ref[pl.ds(off, b), :]
                vb = v_ref[pl.ds(off, b), :].astype(BF16)
                k_a, k_b = _two_heads(kb, lane_lo)
                dz_a, w_a, a_a, r_a = head(q_a, do_a, kb, vb, a_a, r_a, keep, mask_l)
                dz_b, w_b, a_b, r_b = head(q_b, do_b, kb, vb, a_b, r_b, keep, mask_l)
                dq = dq + jnp.dot(jnp.concatenate([dz_a, dz_b], axis=1), jnp.concatenate([k_a, k_b], axis=0),
                                  preferred_element_type=F32)
                dk_ref[pl.ds(off, b), :] += lax.dot_general(jnp.concatenate([dz_a, dz_b], axis=0), q_rows, TN,
                                                            preferred_element_type=F32)
                dv_ref[pl.ds(off, b), :] += lax.dot_general(jnp.concatenate([w_a, w_b], axis=0), do_rows, TN,
                                                            preferred_element_type=F32)
                return a_a, a_b, r_a, r_b, dq

            carry = step(0, (zero, zero, d_a, d_b, jnp.zeros((b, LANES), F32)), tri)
            carry = step(1, carry, jnp.broadcast_to(qi > 0, tri.shape), mask_l=False)
            started.append((qi, step, carry))
        for h, (qi, step, carry) in enumerate(started):
            carry = _sb_walk(qi, carry, lambda n, c, step=step: step(n, c, None))
            dq_ref[h * b:(h + 1) * b, :] = carry[4]

    blk = pl.BlockSpec((nqb * b, LANES), lambda hp, i: (i, hp))
    full = pl.BlockSpec((s, LANES), lambda hp, i: (0, hp))
    full_v = pl.BlockSpec((s, LANES), lambda hp, i: (0, hp + v_col))
    grid = (width // LANES, s // (nqb * b))
    s_in, s_out, s_shape, s_scratch, s_ops = _side_args(side)
    res = pl.pallas_call(
        _hosted(body, side, 5, 3, grid), name=name, grid=grid,
        in_specs=[blk, full, full_v, blk, blk] + s_in, out_specs=[blk, full, full] + s_out,
        out_shape=[jax.ShapeDtypeStruct((s, width), F32)] * 3 + s_shape,
        scratch_shapes=s_scratch,
        compiler_params=_params(("arbitrary", "arbitrary")),
    )(qs, ks, v, out, dout, *s_ops)
    return _split_side(res, 3, side)


def _cmul(xr, xi, yr, yi):
    return xr * yr - xi * yi, xr * yi + xi * yr


def _scan_consts(ar, ai, reverse, lc):
    rowi = lax.broadcasted_iota(jnp.int32, (SUBLANES, lc), 0)
    pows = [(ar, ai)]
    for _ in range(SUBLANES - 1):
        pows.append(_cmul(*pows[-1], ar, ai))
    steps = []
    for d in (1, 2, 4):
        keep = (rowi < SUBLANES - d) if reverse else (rowi >= d)
        pr, pi = pows[d - 1]
        steps.append((SUBLANES - d if reverse else d, jnp.where(keep, pr, 0.0), jnp.where(keep, pi, 0.0)))
    cr = jnp.zeros((SUBLANES, lc), F32)
    ci = jnp.zeros((SUBLANES, lc), F32)
    for r in range(SUBLANES):
        pr, pi = pows[SUBLANES - 1 - r] if reverse else pows[r]
        cr = jnp.where(rowi == r, pr, cr)
        ci = jnp.where(rowi == r, pi, ci)
    return steps, cr, ci


def _scan_tile(xr, xi, steps, pr, pi, cr, ci):
    for shift, ar, ai in steps:
        rr = pltpu.roll(xr, shift, 0)
        ri = pltpu.roll(xi, shift, 0)
        xr, xi = xr + ar * rr - ai * ri, xi + ar * ri + ai * rr
    return xr + pr * cr - pi * ci, xi + pr * ci + pi * cr


SCAN_ROWS = 1024


def _scan_chunk(s):
    tt = min(SCAN_ROWS, s)
    seg = tt // SUBLANES
    assert s % tt == 0 and seg % SUBLANES == 0 and seg & (seg - 1) == 0, s
    return tt, seg


def _to_segments(a):
    s, wd = a.shape
    tt, seg = _scan_chunk(s)
    return jnp.transpose(a.reshape(s // tt, SUBLANES, seg, wd), (0, 2, 1, 3)).reshape(s, wd)


def _from_segments(a):
    s, wd = a.shape
    tt, seg = _scan_chunk(s)
    return jnp.transpose(a.reshape(s // tt, seg, SUBLANES, wd), (0, 2, 1, 3)).reshape(s, wd)


def _cpow2(xr, xi, k):
    for _ in range(k):
        xr, xi = _cmul(xr, xi, xr, xi)
    return xr, xi


def _fill_powers(pw_ref, ar, ai, seg, lc):
    _, p8r, p8i = _scan_consts(ar, ai, False, lc)
    a8r, a8i = _cpow2(ar, ai, 3)
    qr, qi = jnp.ones_like(ar), jnp.zeros_like(ai)
    for k in range(seg // SUBLANES):
        tr, ti = _cmul(p8r, p8i, qr, qi)
        for r in range(SUBLANES):
            rows = pl.ds((SUBLANES * k + r) * SUBLANES, SUBLANES)
            pw_ref[rows, :lc] = jnp.broadcast_to(tr[r:r + 1, :], (SUBLANES, lc))
            pw_ref[rows, lc:] = jnp.broadcast_to(ti[r:r + 1, :], (SUBLANES, lc))
        qr, qi = _cmul(qr, qi, a8r, a8i)


def _ssm_fwd(u, acat, bsup, csup, d_skip, name, side=None):
    s = u.shape[0]
    lc = SCAN_LANES
    tt, seg = _scan_chunk(s)
    nl, nt = N_STATE // lc, s // tt
    tile = lambda j: pl.ds(pl.multiple_of(j * SUBLANES, SUBLANES), SUBLANES)

    def body(u_ref, a_ref, b_ref, c_ref, d_ref, s_ref, y0_ref, y1_ref, carry, pw_ref):
        ar, ai = a_ref[:, :lc], a_ref[:, lc:]

        @pl.when(pl.program_id(1) == 0)
        def _():
            carry[...] = jnp.zeros_like(carry)
            _fill_powers(pw_ref, ar, ai, seg, lc)

        ut = u_ref[...]
        s_ref[...] = _dot(ut, b_ref[0])

        ar8, ai8 = jnp.broadcast_to(ar, (SUBLANES, lc)), jnp.broadcast_to(ai, (SUBLANES, lc))

        def local(j, x):
            xr = ar8 * x[0] - ai8 * x[1] + s_ref[tile(j), :lc]
            xi = ar8 * x[1] + ai8 * x[0] + s_ref[tile(j), lc:]
            s_ref[tile(j), :lc] = xr
            s_ref[tile(j), lc:] = xi
            return xr, xi

        zero = jnp.zeros((SUBLANES, lc), F32)
        er, ei = lax.fori_loop(0, seg, local, (zero, zero))
        steps, pr, pi = _scan_consts(*_cpow2(ar, ai, seg.bit_length() - 1), False, lc)
        cr, ci = carry[:, :lc], carry[:, lc:]
        tr, ti = _scan_tile(er, ei, steps, pr, pi, cr, ci)
        rowi = lax.broadcasted_iota(jnp.int32, (SUBLANES, lc), 0)
        before_r = jnp.where(rowi == 0, cr, pltpu.roll(tr, 1, 0))
        before_i = jnp.where(rowi == 0, ci, pltpu.roll(ti, 1, 0))
        carry[:, :lc] = jnp.broadcast_to(tr[SUBLANES - 1:, :], (SUBLANES, lc))
        carry[:, lc:] = jnp.broadcast_to(ti[SUBLANES - 1:, :], (SUBLANES, lc))

        def fix(j, _):
            pwr, pwi = pw_ref[tile(j), :lc], pw_ref[tile(j), lc:]
            s_ref[tile(j), :lc] += pwr * before_r - pwi * before_i
            s_ref[tile(j), lc:] += pwr * before_i + pwi * before_r
            return 0

        lax.fori_loop(0, seg, fix, 0)
        y0 = _dot(s_ref[...], c_ref[0], NT) + d_ref[...] * ut
        y0_ref[...] = y0
        y1_ref[...] = jax.nn.gelu(y0)

    chan = pl.BlockSpec((tt, LANES), lambda j, c: (c, j))
    sup = pl.BlockSpec((1, LANES, 2 * lc), lambda j, c: (j, 0, 0))
    s_in, s_out, s_shape, s_scratch, s_ops = _side_args(side)
    res = pl.pallas_call(
        _hosted(body, side, 5, 3, (nl, nt)), name=name, grid=(nl, nt),
        in_specs=[chan, pl.BlockSpec((1, 2 * lc), lambda j, c: (0, j)), sup, sup,
                  pl.BlockSpec((1, LANES), lambda j, c: (0, j))] + s_in,
        out_specs=[pl.BlockSpec((tt, 2 * lc), lambda j, c: (c, j)), chan, chan] + s_out,
        out_shape=[jax.ShapeDtypeStruct((s, 2 * N_STATE), F32), jax.ShapeDtypeStruct((s, SSM_WIDTH), F32),
                   jax.ShapeDtypeStruct((s, SSM_WIDTH), F32)] + s_shape,
        scratch_shapes=[pltpu.VMEM((SUBLANES, 2 * lc), F32), pltpu.VMEM((seg * SUBLANES, 2 * lc), F32)] + s_scratch,
        compiler_params=_params(("arbitrary", "arbitrary")),
    )(u, acat, bsup, csup, d_skip, *s_ops)
    return _split_side(res, 3, side)


def _ssm_bwd(dy0, states, u, acat, bsup, csup, d_skip, name, side=None):
    s = u.shape[0]
    lc = SCAN_LANES
    tt, seg = _scan_chunk(s)
    nl, nt = N_STATE // lc, s // tt
    tile = lambda j: pl.ds(pl.multiple_of(j * SUBLANES, SUBLANES), SUBLANES)

    def body(dy_ref, s_ref, sp_ref, u_ref, a_ref, b_ref, c_ref, d_ref,
             du_ref, da_ref, db_ref, dc_ref, dd_ref, lam_ref, carry, pw_ref):
        c = pl.program_id(1)
        ar, ai = a_ref[:, :lc], a_ref[:, lc:]

        @pl.when(c == 0)
        def _():
            carry[...] = jnp.zeros_like(carry)
            for r in (da_ref, db_ref, dc_ref, dd_ref):
                r[...] = jnp.zeros_like(r)
            _fill_powers(pw_ref, ar, ai, seg, lc)

        dy = dy_ref[...]
        ut = u_ref[...]
        lam_ref[...] = _dot(dy, c_ref[0])

        ar8, ai8 = jnp.broadcast_to(ar, (SUBLANES, lc)), jnp.broadcast_to(ai, (SUBLANES, lc))

        def local(i, x):
            j = seg - 1 - i
            xr = ar8 * x[0] + ai8 * x[1] + lam_ref[tile(j), :lc]
            xi = ar8 * x[1] - ai8 * x[0] + lam_ref[tile(j), lc:]
            lam_ref[tile(j), :lc] = xr
            lam_ref[tile(j), lc:] = xi
            return xr, xi

        zero = jnp.zeros((SUBLANES, lc), F32)
        er, ei = lax.fori_loop(0, seg, local, (zero, zero))
        big_r, big_i = _cpow2(ar, ai, seg.bit_length() - 1)
        steps, pr, pi = _scan_consts(big_r, -big_i, True, lc)
        cr, ci = carry[:, :lc], carry[:, lc:]
        tr, ti = _scan_tile(er, ei, steps, pr, pi, cr, ci)
        rowi = lax.broadcasted_iota(jnp.int32, (SUBLANES, lc), 0)
        after_r = jnp.where(rowi == SUBLANES - 1, cr, pltpu.roll(tr, SUBLANES - 1, 0))
        after_i = jnp.where(rowi == SUBLANES - 1, ci, pltpu.roll(ti, SUBLANES - 1, 0))
        carry[:, :lc] = jnp.broadcast_to(tr[:1, :], (SUBLANES, lc))
        carry[:, lc:] = jnp.broadcast_to(ti[:1, :], (SUBLANES, lc))

        start = c != nt - 1
        last_r = jnp.where(start, jnp.broadcast_to(sp_ref[SUBLANES - 1:, :lc], (SUBLANES, lc)), 0.0)
        last_i = jnp.where(start, jnp.broadcast_to(sp_ref[SUBLANES - 1:, lc:], (SUBLANES, lc)), 0.0)
        first_r = jnp.where(rowi == 0, last_r, pltpu.roll(s_ref[tile(seg - 1), :lc], 1, 0))
        first_i = jnp.where(rowi == 0, last_i, pltpu.roll(s_ref[tile(seg - 1), lc:], 1, 0))

        def fix(j, acc):
            dar, dai = acc
            k = seg - 1 - j
            pwr, pwi = pw_ref[tile(k), :lc], pw_ref[tile(k), lc:]
            lr = lam_ref[tile(j), :lc] + pwr * after_r + pwi * after_i
            li = lam_ref[tile(j), lc:] + pwr * after_i - pwi * after_r
            lam_ref[tile(j), :lc] = lr
            lam_ref[tile(j), lc:] = li
            jp = jnp.maximum(j - 1, 0)
            sr = jnp.where(j > 0, s_ref[tile(jp), :lc], first_r)
            si = jnp.where(j > 0, s_ref[tile(jp), lc:], first_i)
            return dar + lr * sr + li * si, dai + li * sr - lr * si

        dar, dai = lax.fori_loop(0, seg, fix, (zero, zero))
        da_ref[:, :lc] += dar
        da_ref[:, lc:] += dai
        lam = lam_ref[...].astype(BF16)
        du_ref[...] = (_dot(lam, b_ref[0], NT) + d_ref[...] * dy).astype(du_ref.dtype)
        db_ref[0] += _dot(ut, lam, TN)
        dc_ref[0] += _dot(dy, s_ref[...], TN)
        dd_ref[...] += jnp.sum(dy * ut, axis=0, keepdims=True)

    rev = lambda j, c: (nt - 1 - c, j)
    chan = pl.BlockSpec((tt, LANES), rev)
    sup = pl.BlockSpec((1, LANES, 2 * lc), lambda j, c: (j, 0, 0))
    row = pl.BlockSpec((1, LANES), lambda j, c: (0, j))
    s_in, s_out, s_shape, s_scratch, s_ops = _side_args(side)
    res = pl.pallas_call(
        _hosted(body, side, 8, 5, (nl, nt)), name=name, grid=(nl, nt),
        in_specs=[chan, pl.BlockSpec((tt, 2 * lc), rev),
                  pl.BlockSpec((SUBLANES, 2 * lc), lambda j, c: (jnp.maximum((nt - 1 - c) * seg - 1, 0), j)),
                  chan, pl.BlockSpec((1, 2 * lc), lambda j, c: (0, j)), sup, sup, row] + s_in,
        out_specs=[chan, pl.BlockSpec((SUBLANES, 2 * lc), lambda j, c: (0, j)), sup, sup, row] + s_out,
        out_shape=[jax.ShapeDtypeStruct((s, SSM_WIDTH), BF16), jax.ShapeDtypeStruct((SUBLANES, 2 * N_STATE), F32),
                   jax.ShapeDtypeStruct(bsup.shape, F32), jax.ShapeDtypeStruct(csup.shape, F32),
                   jax.ShapeDtypeStruct((1, SSM_WIDTH), F32)] + s_shape,
        scratch_shapes=[pltpu.VMEM((tt, 2 * lc), F32), pltpu.VMEM((SUBLANES, 2 * lc), F32),
                        pltpu.VMEM((seg * SUBLANES, 2 * lc), F32)] + s_scratch,
        compiler_params=_params(("arbitrary", "arbitrary")),
    )(dy0, states, states, u, acat, bsup, csup, d_skip, *s_ops)
    return _split_side(res, 5, side)


def _discretise(ar, ai, ldt, br, bi):
    dt = jnp.exp(ldt)
    lr, li = ar * dt, ai * dt
    e = jnp.exp(lr)
    abar_r, abar_i = e * jnp.cos(li), e * jnp.sin(li)
    den = ar * ar + ai * ai
    coef_r = ((abar_r - 1.0) * ar + abar_i * ai) / den
    coef_i = (abar_i * ar - (abar_r - 1.0) * ai) / den
    return abar_r, abar_i, coef_r * br - coef_i * bi, coef_r * bi + coef_i * br


def _group_mask():
    shape = (LANES, SCAN_LANES)
    return (lax.broadcasted_iota(jnp.int32, shape, 0) // SSM_GROUP
            == lax.broadcasted_iota(jnp.int32, shape, 1) // SSM_STATE)


def _ssm_mats_fwd(a_re, a_im, log_dt, b_re, b_im, c_re, c_im, name):
    nl = N_STATE // SCAN_LANES
    lc = SCAN_LANES

    def body(ar, ai, ldt, br, bi, cr, ci, acat, bsup, csup):
        abar_r, abar_i, bbar_r, bbar_i = _discretise(ar[...], ai[...], ldt[...], br[...], bi[...])
        same = _group_mask()
        spread = lambda m, j: jnp.where(same, jnp.tile(m[:, j * lc:(j + 1) * lc], (LANES // SSM_GROUP, 1)), 0.0)
        c_r, c_i = cr[...], -ci[...]
        for j in range(nl):
            acat[:, 2 * j * lc:(2 * j + 1) * lc] = abar_r[:, j * lc:(j + 1) * lc]
            acat[:, (2 * j + 1) * lc:(2 * j + 2) * lc] = abar_i[:, j * lc:(j + 1) * lc]
            bsup[j, :, :lc] = spread(bbar_r, j)
            bsup[j, :, lc:] = spread(bbar_i, j)
            csup[j, :, :lc] = spread(c_r, j)
            csup[j, :, lc:] = spread(c_i, j)

    return pl.pallas_call(
        body, name=name,
        out_shape=[jax.ShapeDtypeStruct((1, 2 * N_STATE), F32), jax.ShapeDtypeStruct((nl, LANES, 2 * lc), F32),
                   jax.ShapeDtypeStruct((nl, LANES, 2 * lc), F32)],
        compiler_params=_params(),
    )(a_re, a_im, log_dt, b_re, b_im, c_re, c_im)


def _ssm_mats_bwd(a_re, a_im, log_dt, b_re, b_im, d_acat, d_bsup, d_csup, name):
    nl = N_STATE // SCAN_LANES
    lc = SCAN_LANES

    def body(ar, ai, ldt, br, bi, dac, dbs, dcs, d_ar, d_ai, d_ldt, d_br, d_bi, d_cr, d_ci):
        same = _group_mask()

        def gather(ref, j, half):
            m = jnp.where(same, ref[j, :, half * lc:(half + 1) * lc], 0.0)
            tot = m[:SSM_GROUP]
            for k in range(1, LANES // SSM_GROUP):
                tot = tot + m[k * SSM_GROUP:(k + 1) * SSM_GROUP]
            return tot

        cols = lambda ref, half: jnp.concatenate([gather(ref, j, half) for j in range(nl)], axis=1)
        d_abar_r = jnp.concatenate([dac[:, 2 * j * lc:(2 * j + 1) * lc] for j in range(nl)], axis=1)
        d_abar_i = jnp.concatenate([dac[:, (2 * j + 1) * lc:(2 * j + 2) * lc] for j in range(nl)], axis=1)
        _, vjp = jax.vjp(_discretise, ar[...], ai[...], ldt[...], br[...], bi[...])
        outs = vjp((d_abar_r, d_abar_i, cols(dbs, 0), cols(dbs, 1)))
        for ref, val in zip((d_ar, d_ai, d_ldt, d_br, d_bi), outs):
            ref[...] = val
        d_cr[...] = cols(dcs, 0)
        d_ci[...] = -cols(dcs, 1)

    row = jax.ShapeDtypeStruct((1, N_STATE), F32)
    mat = jax.ShapeDtypeStruct((SSM_GROUP, N_STATE), F32)
    return pl.pallas_call(
        body, name=name, out_shape=[row, row, row, mat, mat, mat, mat], compiler_params=_params(),
    )(a_re, a_im, log_dt, b_re, b_im, d_acat, d_bsup, d_csup)


def _states_on_lanes(sm):
    flat = lambda a: a.reshape(1, N_STATE)
    chan_b = lambda b: jnp.transpose(b, (2, 0, 1)).reshape(SSM_GROUP, N_STATE)
    chan_c = lambda c: jnp.transpose(c, (1, 0, 2)).reshape(SSM_GROUP, N_STATE)
    return (flat(sm["ssm_a_re"]), flat(sm["ssm_a_im"]), flat(jnp.repeat(sm["ssm_log_dt"], SSM_STATE)),
            chan_b(sm["ssm_b_re"]), chan_b(sm["ssm_b_im"]), chan_c(sm["ssm_c_re"]), chan_c(sm["ssm_c_im"]))


def _from_states_on_lanes(d_ar, d_ai, d_ldt, d_br, d_bi, d_cr, d_ci):
    grp = lambda a: a.reshape(SSM_GROUPS, SSM_STATE)
    back_b = lambda b: jnp.transpose(b.reshape(SSM_GROUP, SSM_GROUPS, SSM_STATE), (1, 2, 0))
    back_c = lambda c: jnp.transpose(c.reshape(SSM_GROUP, SSM_GROUPS, SSM_STATE), (1, 0, 2))
    return (grp(d_ar), grp(d_ai), jnp.sum(grp(d_ldt), axis=1), back_b(d_br), back_b(d_bi), back_c(d_cr), back_c(d_ci))


def _mem_fwd(mem, g_mem, w_kv, g_k, name):
    ml = mem.shape[0]

    def body(mem_ref, gm_ref, w_ref, gk_ref, memn_ref, kv_ref, kn_ref, vv_ref):
        memn = _rms(mem_ref[...], gm_ref[...])
        memn_ref[...] = memn.astype(BF16)
        kv = _dot(memn, w_ref[...])
        kv_ref[...] = kv
        for hh in range(XA_HEADS):
            sl = slice(hh * XA_HEAD_DIM, (hh + 1) * XA_HEAD_DIM)
            kn_ref[:, sl] = _rms(kv[:, sl], gk_ref[...]).astype(BF16)
        vv_ref[...] = kv[:, XA_WIDTH:].astype(BF16)

    return pl.pallas_call(
        body, name=name,
        out_shape=[jax.ShapeDtypeStruct((ml, D_MODEL), BF16), jax.ShapeDtypeStruct((ml, 2 * XA_WIDTH), F32),
                   jax.ShapeDtypeStruct((ml, XA_WIDTH), BF16), jax.ShapeDtypeStruct((ml, XA_WIDTH), BF16)],
        compiler_params=_params(),
    )(mem, g_mem, w_kv, g_k)


def _mem_bwd(mem, g_mem, memn, w_kv, kv, g_k, dkn, dvv, name):
    def body(mem_ref, gm_ref, memn_ref, w_ref, kv_ref, gk_ref, dkn_ref, dvv_ref, dw_ref, dgm_ref, dgk_ref):
        kv = kv_ref[...]
        dgk = jnp.zeros(dgk_ref.shape, F32)
        parts = []
        for hh in range(XA_HEADS):
            sl = slice(hh * XA_HEAD_DIM, (hh + 1) * XA_HEAD_DIM)
            _, vjp = jax.vjp(_rms, kv[:, sl], gk_ref[...])
            dk, dg = vjp(dkn_ref[:, sl])
            parts.append(dk)
            dgk = dgk + dg
        dgk_ref[...] = dgk
        dkv = jnp.concatenate(parts + [dvv_ref[...]], axis=1)
        dw_ref[...] = _dot(memn_ref[...], dkv, TN)
        dmemn = _dot(dkv, w_ref[...], NT)
        _, vjp = jax.vjp(_rms, mem_ref[...], gm_ref[...])
        dgm_ref[...] = vjp(dmemn)[1]

    return pl.pallas_call(
        body, name=name,
        out_shape=[jax.ShapeDtypeStruct((D_MODEL, 2 * XA_WIDTH), F32), jax.ShapeDtypeStruct(g_mem.shape, F32),
                   jax.ShapeDtypeStruct(g_k.shape, F32)],
        compiler_params=_params(),
    )(mem, g_mem, memn, w_kv, kv, g_k, dkn, dvv)


def _xa_head(qx_h, g_q, kn_h, vv_h):
    qn = _rms(qx_h, g_q)
    sc = _dot(qn, kn_h, NT) * (XA_HEAD_DIM ** -0.5)
    sc = sc - jnp.max(sc, axis=-1, keepdims=True)
    e = jnp.exp(sc)
    p = e / jnp.sum(e, axis=-1, keepdims=True)
    return qn, p


def _xa_fwd(qx, g_q, kn, vv, name):
    def fn(qt, gq, knt, vvt):
        outs = []
        for hh in range(XA_HEADS):
            sl = slice(hh * XA_HEAD_DIM, (hh + 1) * XA_HEAD_DIM)
            _, p = _xa_head(qt[:, sl], gq, knt[:, sl], vvt[:, sl])
            outs.append(_dot(p, vvt[:, sl]))
        return (jnp.concatenate(outs, axis=1),), ()

    return _rw(fn, [qx], [g_q, kn, vv], [(XA_WIDTH, BF16)], [], name, tm=512)[0]


def _xa_bwd(qx, g_q, kn, vv, do, name):
    def fn(qt, dot_, gq, knt, vvt):
        dqs, dks, dvs = [], [], []
        dgq = jnp.zeros_like(gq)
        for hh in range(XA_HEADS):
            sl = slice(hh * XA_HEAD_DIM, (hh + 1) * XA_HEAD_DIM)
            qn, p = _xa_head(qt[:, sl], gq, knt[:, sl], vvt[:, sl])
            doh = dot_[:, sl]
            dp = _dot(doh, vvt[:, sl], NT)
            dvs.append(_dot(p, doh, TN))
            ds = p * (dp - jnp.sum(dp * p, axis=-1, keepdims=True)) * (XA_HEAD_DIM ** -0.5)
            dqn = _dot(ds, knt[:, sl])
            dks.append(_dot(ds, qn, TN))
            _, vjp = jax.vjp(_rms, qt[:, sl], gq)
            dq, dg = vjp(dqn)
            dqs.append(dq)
            dgq = dgq + dg
        return ((jnp.concatenate(dqs, axis=1),),
                (jnp.concatenate(dks, axis=1), jnp.concatenate(dvs, axis=1), dgq))

    return _rw(fn, [qx, do], [g_q, kn, vv], [(XA_WIDTH, BF16)], [kn.shape, vv.shape, g_q.shape], name, tm=512)


BIG = [
    ("w_in", (D_MODEL, IN_WIDTH), 1), ("ssm_w_glu", (SSM_WIDTH, SSM_WIDTH), 0), ("w_out", (D_MODEL, D_MODEL), 0),
    ("xa_w_q", (D_MODEL, XA_WIDTH), 0), ("xa_w_kv", (D_MODEL, 2 * XA_WIDTH), 0), ("xa_w_o", (XA_WIDTH, D_MODEL), 1),
    ("w_up", (D_MODEL, D_FF), 1), ("w_down", (D_FF, D_MODEL), 0),
]
BIG_INDEX = {n: i for i, (n, _, _) in enumerate(BIG)}


def _shard_shape(shape, axis):
    return tuple(d // N_DEV if i == axis else d for i, d in enumerate(shape))


def _shard_of(ref, axis, d):
    n = ref.shape[axis] // N_DEV
    return ref.at[pl.ds(d * n, n), :] if axis == 0 else ref.at[:, pl.ds(d * n, n)]


def _gather_side(names, shards):
    idxs = [BIG_INDEX[n] for n in names]

    def make(ins, outs, send_sems, recv_sems):
        x, y, c = lax.axis_index("x"), lax.axis_index("y"), lax.axis_index("c")
        cps = []
        for j, i in enumerate(idxs):
            mine = _shard_of(outs[j], BIG[i][2], 4 * x + 2 * y + c)
            cps.append(pltpu.make_async_copy(ins[j], mine, send_sems.at[N_DEV * j]))
            for rel in range(1, N_DEV):
                to = tuple(1 - p if rel >> bit & 1 else p for p, bit in ((x, 2), (y, 1), (c, 0)))
                cps.append(pltpu.make_async_remote_copy(
                    src_ref=ins[j], dst_ref=mine, send_sem=send_sems.at[N_DEV * j + rel],
                    recv_sem=recv_sems.at[N_DEV * j + rel], device_id=to, device_id_type=MESH))
        return cps

    return _Side(shards, [jax.ShapeDtypeStruct(BIG[i][1], BF16) for i in idxs], N_DEV * len(idxs), make)


def _gather_two_level_side(name, shard):
    i = BIG_INDEX[name]

    def parts(ins, outs, send_sems, recv_sems):
        x, y, c = lax.axis_index("x"), lax.axis_index("y"), lax.axis_index("c")
        sibling = (x, y, 1 - c)
        chips = [(1 - x, y), (x, 1 - y), (1 - x, 1 - y)]

        def place(dev):
            return _shard_of(outs[0], BIG[i][2], 4 * dev[0] + 2 * dev[1] + dev[2])

        def copy(k, blk, to, src=None):
            return pltpu.make_async_remote_copy(
                src_ref=place(blk) if src is None else src, dst_ref=place(blk), send_sem=send_sems.at[k],
                recv_sem=recv_sems.at[k], device_id=to, device_id_type=MESH)

        mine = pltpu.make_async_copy(ins[0], place((x, y, c)), send_sems.at[7])
        first = [copy(0, (x, y, c), sibling, src=ins[0])]
        first += [copy(1 + j, (x, y, c), (*chip, c), src=ins[0]) for j, chip in enumerate(chips)]
        passed = [copy(4 + j, (*chip, c), sibling) for j, chip in enumerate(chips)]
        arrived = [copy(1 + j, (*chip, c), (x, y, c)) for j, chip in enumerate(chips)]
        from_sibling = [copy(0, sibling, (x, y, c))] + [copy(4 + j, (*chip, 1 - c), (x, y, c))
                                                       for j, chip in enumerate(chips)]
        return mine, first, passed, arrived, from_sibling

    def make(ins, outs, send_sems, recv_sems):
        mine, first, _, _, _ = parts(ins, outs, send_sems, recv_sems)
        return [mine] + first

    def finish(ins, outs, send_sems, recv_sems):
        mine, first, passed, arrived, from_sibling = parts(ins, outs, send_sems, recv_sems)
        for got, onward in zip(arrived, passed):
            got.wait_recv()
            onward.start()
        for cp in from_sibling:
            cp.wait_recv()
        for cp in first + passed:
            cp.wait_send()
        mine.wait()

    return _Side([shard], [jax.ShapeDtypeStruct(BIG[i][1], BF16)], N_DEV, make, finish)


def _sibling_side(names, grads):
    idxs = [BIG_INDEX[n] for n in names]

    def make(ins, outs, send_sems, recv_sems):
        x, y, c = lax.axis_index("x"), lax.axis_index("y"), lax.axis_index("c")
        return [pltpu.make_async_remote_copy(
            src_ref=_shard_of(ins[j], BIG[i][2], 2 * k + (1 - c)), dst_ref=outs[j].at[k],
            send_sem=send_sems.at[4 * j + k], recv_sem=recv_sems.at[4 * j + k], device_id=(x, y, 1 - c),
            device_id_type=MESH) for j, i in enumerate(idxs) for k in range(4)]

    shapes = [jax.ShapeDtypeStruct((4,) + _shard_shape(BIG[i][1], BIG[i][2]), F32) for i in idxs]
    return _Side(grads, shapes, 4 * len(idxs), make)


def _chips_side(parts):
    def make(ins, outs, send_sems, recv_sems):
        x, y, c = lax.axis_index("x"), lax.axis_index("y"), lax.axis_index("c")
        chips = [(1 - x, y), (x, 1 - y), (1 - x, 1 - y)]
        return [pltpu.make_async_remote_copy(
            src_ref=ins[j].at[2 * cx + cy], dst_ref=outs[j].at[r], send_sem=send_sems.at[3 * j + r],
            recv_sem=recv_sems.at[3 * j + r], device_id=(cx, cy, c), device_id_type=MESH)
            for r, (cx, cy) in enumerate(chips) for j in range(len(parts))]

    return _Side(parts, [jax.ShapeDtypeStruct((3,) + p.shape[1:], p.dtype) for p in parts], 3 * len(parts), make)


def _reduce_add(grad, recv, axis, core, name):
    rs, cs = recv.shape[1:]
    rt = _row_tile(rs, 256)
    nt = rs // rt

    def body(c_ref, g_ref, r_ref, p_ref, pb_ref):
        sm = g_ref[...] + r_ref[0]
        p_ref[0] = sm
        pb_ref[0] = sm.astype(BF16)

    if axis == 0:
        g_spec = pl.BlockSpec((rt, cs), lambda k, t, c_ref: ((2 * k + c_ref[0]) * nt + t, 0))
    else:
        g_spec = pl.BlockSpec((rt, cs), lambda k, t, c_ref: (t, 2 * k + c_ref[0]))
    slab = pl.BlockSpec((1, rt, cs), lambda k, t, c_ref: (k, t, 0))
    return pl.pallas_call(
        body, name=name,
        grid_spec=pltpu.PrefetchScalarGridSpec(num_scalar_prefetch=1, grid=(4, nt), in_specs=[g_spec, slab],
                                               out_specs=[slab, slab]),
        out_shape=[jax.ShapeDtypeStruct(recv.shape, F32), jax.ShapeDtypeStruct(recv.shape, BF16)],
        compiler_params=_params(("parallel", "parallel")),
    )(core, grad, recv)


def _all_gather(block, name, side):
    m_per, n = block.shape
    ns_in, ns_out = len(side.ins), len(side.out_shapes)

    def body(*refs):
        x_ref, s_ins, out_ref = refs[0], refs[1:1 + ns_in], refs[1 + ns_in]
        s_outs = refs[2 + ns_in:2 + ns_in + ns_out]
        send_sems, recv_sems, local_sem, s_send, s_recv = refs[2 + ns_in + ns_out:]
        others = side.make(s_ins, s_outs, s_send, s_recv)
        for cp in others:
            cp.start()
        x, y, c = lax.axis_index("x"), lax.axis_index("y"), lax.axis_index("c")
        me, sibling = (x, y, c), (x, y, 1 - c)
        chips = [(1 - x, y), (x, 1 - y), (1 - x, 1 - y)]

        def rows(px, py, pc):
            return out_ref.at[pl.ds((4 * px + 2 * py + pc) * m_per, m_per), :]

        def copy(k, blk, to, src=None):
            return pltpu.make_async_remote_copy(
                src_ref=rows(*blk) if src is None else src, dst_ref=rows(*blk),
                send_sem=send_sems.at[k], recv_sem=recv_sems.at[k], device_id=to, device_id_type=MESH)

        mine = pltpu.make_async_copy(x_ref, rows(*me), local_sem)
        mine.start()
        first = [copy(0, me, sibling, src=x_ref)]
        first += [copy(1 + j, me, (*chip, c), src=x_ref) for j, chip in enumerate(chips)]
        for cp in first:
            cp.start()
        passed = [copy(4 + j, (*chip, c), sibling) for j, chip in enumerate(chips)]
        for j, chip in enumerate(chips):
            copy(1 + j, (*chip, c), me).wait_recv()
            passed[j].start()
        copy(0, sibling, me).wait_recv()
        for j, chip in enumerate(chips):
            copy(4 + j, (*chip, 1 - c), me).wait_recv()
        for cp in first + passed:
            cp.wait_send()
        mine.wait()
        for cp in others:
            cp.wait()

    res = pl.pallas_call(
        body, name=name, in_specs=[ANY] * (1 + ns_in), out_specs=[ANY] * (1 + ns_out),
        out_shape=[jax.ShapeDtypeStruct((N_DEV * m_per, n), block.dtype)] + side.out_shapes,
        scratch_shapes=[pltpu.SemaphoreType.DMA((7,)), pltpu.SemaphoreType.DMA((7,)), pltpu.SemaphoreType.DMA]
        + side.sems(),
    )(block, *side.ins)
    return res[0], list(res[1:])


def _adam_math(w, g, m, v):
    m = ADAM_B1 * m + (1.0 - ADAM_B1) * g
    v = ADAM_B2 * v + (1.0 - ADAM_B2) * (g * g)
    m_hat = m / (1.0 - ADAM_B1 ** ADAM_STEP)
    v_hat = v / (1.0 - ADAM_B2 ** ADAM_STEP)
    delta = -ADAM_LR * (m_hat / (jnp.sqrt(v_hat) + ADAM_EPS) + ADAM_WD * w)
    return delta, m, v


def _adam_sharded(own, recv, w, m, v, chip, name):
    rs, cs = w.shape
    rt = _row_tile(rs, 256)

    def body(chip_ref, p_ref, r_ref, w_ref, m_ref, v_ref, g_out, d_out, m_out, v_out):
        g = p_ref[0] + r_ref[0].astype(F32) + r_ref[1].astype(F32) + r_ref[2].astype(F32)
        d, mn, vn = _adam_math(w_ref[...], g, m_ref[...], v_ref[...])
        g_out[...] = g
        d_out[...] = d
        m_out[...] = mn
        v_out[...] = vn

    tile = pl.BlockSpec((rt, cs), lambda t, chip_ref: (t, 0))
    return pl.pallas_call(
        body, name=name,
        grid_spec=pltpu.PrefetchScalarGridSpec(
            num_scalar_prefetch=1, grid=(rs // rt,),
            in_specs=[pl.BlockSpec((1, rt, cs), lambda t, chip_ref: (chip_ref[0], t, 0)),
                      pl.BlockSpec((3, rt, cs), lambda t, chip_ref: (0, t, 0)), tile, tile, tile],
            out_specs=[tile] * 4),
        out_shape=[jax.ShapeDtypeStruct((rs, cs), F32)] * 4,
        compiler_params=_params(("parallel",)),
    )(chip, own, recv, w, m, v)


SMALL = ["g_mix", "ssm_a_re", "ssm_a_im", "ssm_log_dt", "ssm_b_re", "ssm_b_im", "ssm_c_re", "ssm_c_im", "ssm_d",
         "sb_g_q", "sb_g_k", "g_out_ssm", "g_out_sb", "g_xa", "g_mem", "xa_g_q", "xa_g_k", "g_mlp"]
PACK_TILE = SUBLANES * LANES


def _natural_2d(n):
    return (n // LANES, LANES) if n % LANES == 0 else (1, n)


def _pack_small(arrs):
    parts = []
    for a in arrs:
        flat = a.reshape(-1)
        parts.append(jnp.pad(flat, (0, (-flat.shape[0]) % PACK_TILE)))
    return jnp.concatenate(parts).reshape(-1, LANES)


def _adam_replicated(gathered, sizes, ws, ms, vs, name):
    n_w = len(ws)
    r_dev = gathered.shape[0] // N_DEV
    offs, off = [], 0
    for n in sizes:
        offs.append(off)
        off += (n + PACK_TILE - 1) // PACK_TILE * SUBLANES
    assert off == r_dev

    def body(*refs):
        g_ref = refs[0]
        w_refs, m_refs, v_refs = refs[1:1 + n_w], refs[1 + n_w:1 + 2 * n_w], refs[1 + 2 * n_w:1 + 3 * n_w]
        outs = refs[1 + 3 * n_w:]

        def total(i, shape):
            r, cdim = shape
            acc = g_ref[pl.ds(offs[i], r), :cdim]
            for d in range(1, N_DEV):
                acc = acc + g_ref[pl.ds(d * r_dev + offs[i], r), :cdim]
            return acc

        for i in range(n_w):
            g = total(i, w_refs[i].shape)
            d, mn, vn = _adam_math(w_refs[i][...], g, m_refs[i][...], v_refs[i][...])
            for o, val in zip(outs[4 * i:4 * i + 4], (g, d, mn, vn)):
                o[...] = val
        outs[4 * n_w][...] = total(n_w, (SUBLANES, LANES))

    shapes = [w.shape for w in ws]
    return pl.pallas_call(
        body, name=name,
        out_shape=[jax.ShapeDtypeStruct(shp, F32) for shp in shapes for _ in range(4)]
        + [jax.ShapeDtypeStruct((SUBLANES, LANES), F32)],
        compiler_params=_params(),
    )(gathered, *ws, *ms, *vs)


def _step(x, mem, target, shards, sm, core):
    g, w, sums, reduced = {}, {}, {}, {}

    def gather(names):
        return _gather_side(names, [shards[n] for n in names])

    def to_sibling(names):
        return _sibling_side(names, [g[n] for n in names])

    def add_sibling(names, received):
        for n, r in zip(names, received):
            sums[n] = _reduce_add(g[n], r, BIG[BIG_INDEX[n]][2], core, "reduce_add_" + n)

    def to_chips(names):
        return _chips_side([sums[n][1] for n in names])

    def keep(names, received):
        for n, r in zip(names, received):
            reduced[n] = (sums[n][0], r)

    row = lambda a: a.reshape(1, -1)
    g_mix, g_xa, g_mlp, g_mem = row(sm["g_mix"]), row(sm["g_xa"]), row(sm["g_mlp"]), row(sm["g_mem"])
    g_os, g_ob = row(sm["g_out_ssm"]), row(sm["g_out_sb"])
    sb_gq, sb_gk = jnp.tile(row(sm["sb_g_q"]), (1, SB_HEADS)), jnp.tile(row(sm["sb_g_k"]), (1, SB_HEADS))
    xa_gq, xa_gk = row(sm["xa_g_q"]), row(sm["xa_g_k"])
    d_skip = row(sm["ssm_d"])

    h1, (w["w_in"],) = _norm_fwd(x, g_mix, "norm_mix", side=_gather_two_level_side("w_in", shards["w_in"]))
    proj = _mm(h1, w["w_in"], "nn", "in_proj", tn=IN_WIDTH)
    u = _to_segments(proj[:, :SSM_WIDTH])
    q_raw, k_raw = (proj, SB_WIDTH, 1), (proj, SB_WIDTH, 2)
    v_col = (SSM_WIDTH + 2 * SB_WIDTH) // LANES
    sb_scale = SB_HEAD_DIM ** -0.5
    qs, ks = _rw(lambda qt, kt, gq, gk: ((_rms_groups(qt, gq, sb_scale), _rms_groups(kt, gk, 1.0)), ()),
                 [q_raw, k_raw], [sb_gq, sb_gk], [(SB_WIDTH, BF16)] * 2, [], "sb_qk_norm")
    early = ["ssm_w_glu", "w_out", "xa_w_q", "xa_w_kv", "xa_w_o", "w_up"]
    y_sb, got = _sb_fwd(qs, ks, proj, "sb_fwd", v_col=v_col, side=gather(early))
    w.update(zip(early, got))

    ssm_args = _states_on_lanes(sm)
    acat, bsup, csup = _ssm_mats_fwd(*ssm_args, "ssm_mats")
    (states, y0, y1), (w["w_down"],) = _ssm_fwd(u, acat, bsup, csup, d_skip, "ssm_fwd", side=gather(["w_down"]))
    z_glu, y_ssm = _mm(y1, w["ssm_w_glu"], "nn", "ssm_glu", epi=lambda r, yt: (r, yt * jax.nn.sigmoid(r)),
                       extras=(y1,), out_dtypes=(F32, F32))
    y_ssm = _from_segments(y_ssm)

    def cat_norm(a, b, ga, gb):
        return jnp.concatenate([_rms(a, ga), _rms(b, gb)], axis=1)

    ycat = _rw(lambda a, b, ga, gb: ((cat_norm(a, b, ga, gb),), ()), [y_ssm, y_sb], [g_os, g_ob],
               [(D_MODEL, BF16)], [], "norm_out")[0]

    def residual_norm_epi(r, xt, gt):
        xn = r + xt
        return xn, _rms(xn, gt)

    x1, h2 = _mm(ycat, w["w_out"], "nn", "out_proj", epi=residual_norm_epi, extras=(x,), fulls=(g_xa,),
                 out_dtypes=(F32, BF16))
    qx = _mm(h2, w["xa_w_q"], "nn", "xa_q")
    memn, kv, kn_x, vv_x = _mem_fwd(mem, g_mem, w["xa_w_kv"], xa_gk, "xa_mem")
    o_xa = _xa_fwd(qx, xa_gq, kn_x, vv_x, "xa_fwd")
    x2, h3 = _mm(o_xa, w["xa_w_o"], "nn", "xa_o", epi=residual_norm_epi, extras=(x1,), fulls=(g_mlp,),
                 out_dtypes=(F32, BF16))

    def up_epi(r):
        rl = jnp.maximum(r, 0.0)
        return (rl * rl,)

    r_up = _mm(h3, w["w_up"], "nn", "mlp_up", epi=up_epi, out_dtypes=(BF16,), tm=2048, tn=2048)

    def loss_epi(r, xt, tt):
        d = r + xt - tt
        return (d * (1.0 / D_MODEL),) * 2, (jnp.sum(d * d, axis=0, keepdims=True),)

    dx3, dx3_b, sq = _mm(r_up, w["w_down"], "nn", "mlp_down", epi=loss_epi, extras=(x2, target),
                         out_dtypes=(F32, BF16), sums=[(1, D_MODEL)])
    loss = jnp.sum(sq) * (0.5 / D_MODEL)

    def norm_bwd_epi(r, xt, drt, gt):
        _, vjp = jax.vjp(_rms, xt, gt)
        dx_, dg_ = vjp(r)
        return (dx_ + drt,) * 2, (dg_,)

    g["w_down"] = _mm(r_up, dx3_b, "tn", "d_w_down", tk=2048)
    da = _mm(dx3_b, w["w_down"], "nt", "d_r", epi=lambda r, rt: (r * 2.0 * jnp.sqrt(rt.astype(F32)),), extras=(r_up,),
             out_dtypes=(BF16,), tn=2048)
    g["w_up"] = _mm(h3, da, "tn", "d_w_up", tk=2048)
    mlp = ["w_down", "w_up"]
    (dx2, dx2_b, g["g_mlp"]), got = _mm(da, w["w_up"], "nt", "d_h3", epi=norm_bwd_epi, extras=(x2, dx3),
                                        fulls=(g_mlp,), out_dtypes=(F32, BF16), sums=[g_mlp.shape],
                                        side=to_sibling(mlp))
    add_sibling(mlp, got)
    g["xa_w_o"] = _mm(o_xa, dx2_b, "tn", "d_xa_w_o", tk=2048)
    do_xa = _mm(dx2_b, w["xa_w_o"], "nt", "d_o_xa")
    dqx, dkn_x, dvv_x, g["xa_g_q"] = _xa_bwd(qx, xa_gq, kn_x, vv_x, do_xa, "xa_bwd")
    g["xa_w_kv"], g["g_mem"], g["xa_g_k"] = _mem_bwd(mem, g_mem, memn, w["xa_w_kv"], kv, xa_gk, dkn_x, dvv_x,
                                                     "xa_mem_bwd")
    g["xa_w_q"] = _mm(h2, dqx, "tn", "d_xa_w_q", tk=2048)
    dx1, dx1_b, g["g_xa"] = _mm(dqx, w["xa_w_q"], "nt", "d_h2", epi=norm_bwd_epi, extras=(x1, dx2), fulls=(g_xa,),
                                out_dtypes=(F32, BF16), sums=[g_xa.shape])
    g["w_out"] = _mm(ycat, dx1_b, "tn", "d_w_out", tk=2048)
    dycat = _mm(dx1_b, w["w_out"], "nt", "d_ycat")

    def cat_bwd(a, b, dy, ga, gb):
        _, vjp = jax.vjp(cat_norm, a, b, ga, gb)
        da_, db_, dga, dgb = vjp(dy)
        return (da_, db_), (dga, dgb)

    dy_ssm, dy_sb, g["g_out_ssm"], g["g_out_sb"] = _rw(
        cat_bwd, [y_ssm, y_sb, dycat], [g_os, g_ob], [(SSM_WIDTH, F32), (SB_WIDTH, F32)], [g_os.shape, g_ob.shape],
        "d_norm_out")

    def glu_bwd(dy, yt, zt):
        sg = jax.nn.sigmoid(zt)
        return (dy * sg, dy * yt * sg * (1.0 - sg)), ()

    dy1_a, dz = _rw(glu_bwd, [_to_segments(dy_ssm), y1, z_glu], [], [(SSM_WIDTH, F32), (SSM_WIDTH, BF16)], [], "d_glu")
    g["ssm_w_glu"] = _mm(y1, dz, "tn", "d_w_glu", tk=2048)

    def gelu_bwd_epi(r, da_, y0t):
        _, vjp = jax.vjp(jax.nn.gelu, y0t)
        return (vjp(r + da_)[0],)

    mid = ["w_out", "xa_w_q", "xa_w_kv", "xa_w_o", "ssm_w_glu"]
    dy0, got = _mm(dz, w["ssm_w_glu"], "nt", "d_y1", epi=gelu_bwd_epi, extras=(dy1_a, y0), side=to_sibling(mid))
    add_sibling(mid, got)
    (du, da8, d_bsup, d_csup, g["ssm_d"]), got = _ssm_bwd(dy0, states, u, acat, bsup, csup, d_skip, "ssm_bwd",
                                                          side=to_chips(mlp))
    keep(mlp, got)
    d_acat = jnp.sum(da8, axis=0, keepdims=True)
    d_mats = _ssm_mats_bwd(*ssm_args[:5], d_acat, d_bsup, d_csup, "ssm_mats_bwd")
    for nm, val in zip(("ssm_a_re", "ssm_a_im", "ssm_log_dt", "ssm_b_re", "ssm_b_im", "ssm_c_re", "ssm_c_im"),
                       _from_states_on_lanes(*d_mats)):
        g[nm] = val

    (dqs, dks, dvs), got = _sb_bwd(qs, ks, proj, y_sb, dy_sb, "sb_bwd", v_col=v_col, side=to_chips(mid))
    keep(mid, got)

    def d_proj_rows(du_t, qt, dqt, kt, dkt, dvt, gq, gk):
        _, vjp_q = jax.vjp(lambda a, b_: _rms_groups(a, b_, sb_scale), qt, gq)
        _, vjp_k = jax.vjp(lambda a, b_: _rms_groups(a, b_, 1.0), kt, gk)
        (dq_, dgq_), (dk_, dgk_) = vjp_q(dqt), vjp_k(dkt)
        rows = jnp.concatenate([du_t, dq_.astype(BF16), dk_.astype(BF16), dvt.astype(BF16)], axis=1)
        return (rows,), (dgq_, dgk_)

    dproj, dgq, dgk = _rw(d_proj_rows, [_from_segments(du), q_raw, dqs, k_raw, dks, dvs], [sb_gq, sb_gk],
                          [(IN_WIDTH, BF16)], [sb_gq.shape, sb_gk.shape], "d_proj")
    g["sb_g_q"] = jnp.sum(dgq.reshape(SB_HEADS, SB_HEAD_DIM), axis=0)
    g["sb_g_k"] = jnp.sum(dgk.reshape(SB_HEADS, SB_HEAD_DIM), axis=0)
    g["w_in"] = _mm(h1, dproj, "tn", "d_w_in", tn=IN_WIDTH)
    dh1, got = _mm(dproj, w["w_in"], "nt", "d_h1", tk=IN_WIDTH, side=to_sibling(["w_in"]))
    add_sibling(["w_in"], got)
    dx, g["g_mix"] = _norm_bwd(x, g_mix, dh1, dx1, "d_norm_mix")

    packed = _pack_small([g[n] for n in SMALL] + [loss.reshape(1)])
    everyone, got = _all_gather(packed, "gather_small", to_chips(["w_in"]))
    keep(["w_in"], got)
    return dx, everyone, reduced


def kernel(x, mem, g_mix, w_in, ssm_a_re, ssm_a_im, ssm_log_dt, ssm_b_re, ssm_b_im, ssm_c_re, ssm_c_im, ssm_d, ssm_w_glu, sb_g_q, sb_g_k, g_out_ssm, g_out_sb, w_out, g_xa, g_mem, xa_w_q, xa_w_kv, xa_g_q, xa_g_k, xa_w_o, g_mlp, w_up, w_down, loss_target, m_g_mix, m_w_in, m_ssm_a_re, m_ssm_a_im, m_ssm_log_dt, m_ssm_b_re, m_ssm_b_im, m_ssm_c_re, m_ssm_c_im, m_ssm_d, m_ssm_w_glu, m_sb_g_q, m_sb_g_k, m_g_out_ssm, m_g_out_sb, m_w_out, m_g_xa, m_g_mem, m_xa_w_q, m_xa_w_kv, m_xa_g_q, m_xa_g_k, m_xa_w_o, m_g_mlp, m_w_up, m_w_down, v_g_mix, v_w_in, v_ssm_a_re, v_ssm_a_im, v_ssm_log_dt, v_ssm_b_re, v_ssm_b_im, v_ssm_c_re, v_ssm_c_im, v_ssm_d, v_ssm_w_glu, v_sb_g_q, v_sb_g_k, v_g_out_ssm, v_g_out_sb, v_w_out, v_g_xa, v_g_mem, v_xa_w_q, v_xa_w_kv, v_xa_g_q, v_xa_g_k, v_xa_w_o, v_g_mlp, v_w_up, v_w_down):
    given = dict(locals())
    order = ["g_mix", "w_in", "ssm_a_re", "ssm_a_im", "ssm_log_dt", "ssm_b_re", "ssm_b_im", "ssm_c_re", "ssm_c_im",
             "ssm_d", "ssm_w_glu", "sb_g_q", "sb_g_k", "g_out_ssm", "g_out_sb", "w_out", "g_xa", "g_mem", "xa_w_q",
             "xa_w_kv", "xa_g_q", "xa_g_k", "xa_w_o", "g_mlp", "w_up", "w_down"]
    assert sorted([n for n, _, _ in BIG] + SMALL) == sorted(order)
    core = lax.axis_index("c").astype(jnp.int32).reshape(1)
    chip = (2 * lax.axis_index("x") + lax.axis_index("y")).astype(jnp.int32).reshape(1)

    shards = {n: given[n][0].astype(BF16) for n, _, _ in BIG}
    sm = {n: given[n][0] for n in SMALL}
    dx, everyone, reduced = _step(x[0], mem[0], loss_target[0], shards, sm, core)

    res = {}
    for n, _, _ in BIG:
        own, recv = reduced[n]
        outs = _adam_sharded(own, recv, given[n][0], given["m_" + n][0], given["v_" + n][0], chip, "adam_" + n)
        for kind, val in zip(("grad", "delta", "new_m", "new_v"), outs):
            res[kind + "_" + n] = val[None]

    sizes = [math.prod(sm[n].shape) for n in SMALL] + [1]
    nat = lambda a: a.reshape(_natural_2d(math.prod(a.shape)))
    outs = _adam_replicated(everyone, sizes, [nat(sm[n]) for n in SMALL], [nat(given["m_" + n][0]) for n in SMALL],
                            [nat(given["v_" + n][0]) for n in SMALL], "adam_replicated")
    for i, n in enumerate(SMALL):
        for kind, val in zip(("grad", "delta", "new_m", "new_v"), outs[4 * i:4 * i + 4]):
            res[kind + "_" + n] = val.reshape(given[n].shape)
    loss_out = outs[-1][0, 0]
    return (loss_out, dx[None], *[res["grad_" + n] for n in order], *[res["delta_" + n] for n in order],
            *[res["new_m_" + n] for n in order], *[res["new_v_" + n] for n in order])
```

```python
import functools
import math

import jax
import jax.numpy as jnp
from jax import lax
from jax.experimental import pallas as pl
from jax.experimental.pallas import tpu as pltpu

F32 = jnp.float32
BF16 = jnp.bfloat16
MESH = pl.DeviceIdType.MESH

N_DEV = 8
D_MODEL = 1024
SSM_WIDTH = 512
SSM_GROUP = 16
SSM_GROUPS = 32
SSM_STATE = 64
N_STATE = SSM_GROUPS * SSM_STATE
SB_HEADS = 8
SB_HEAD_DIM = 64
SB_WIDTH = 512
IN_WIDTH = 2048
XA_HEADS = 4
XA_HEAD_DIM = 128
XA_WIDTH = 512
D_FF = 4096
NORM_EPS = 1e-6
ADAM_LR = 0.001
ADAM_B1 = 0.9
ADAM_B2 = 0.999
ADAM_EPS = 1e-08
ADAM_WD = 0.01
ADAM_STEP = 10

LANES = 128
SUBLANES = 8
VMEM_LIMIT = 56 * 1024 * 1024
SCAN_LANES = 512
SB_BLOCK = 256
SB_Q_BLOCKS = 4
SB_UNDERFLOW = -110.0

NN = (((1,), (0,)), ((), ()))
NT = (((1,), (1,)), ((), ()))
TN = (((0,), (0,)), ((), ()))


def _params(sem=None):
    return pltpu.CompilerParams(dimension_semantics=sem, vmem_limit_bytes=VMEM_LIMIT)


def _dot(a, b, dims=NN):
    return lax.dot_general(a.astype(BF16), b.astype(BF16), dims, preferred_element_type=F32)


def _rms(x, g):
    return x * lax.rsqrt(jnp.mean(x * x, axis=-1, keepdims=True) + NORM_EPS) * g


ANY = pl.BlockSpec(memory_space=pl.ANY)


class _Side:
    def __init__(self, ins, out_shapes, n_sem, make, finish=None):
        self.ins, self.out_shapes, self.n_sem, self.make = list(ins), list(out_shapes), n_sem, make
        self.finish = finish

    def sems(self):
        return [pltpu.SemaphoreType.DMA((self.n_sem,)), pltpu.SemaphoreType.DMA((self.n_sem,))]


def _hosted(body, side, n_in, n_out, grid):
    if side is None:
        return body
    ns_in, ns_out = len(side.ins), len(side.out_shapes)

    def wrapped(*refs):
        ins, refs = refs[:n_in], refs[n_in:]
        s_ins, refs = refs[:ns_in], refs[ns_in:]
        outs, refs = refs[:n_out], refs[n_out:]
        s_outs, refs = refs[:ns_out], refs[ns_out:]
        scratch, sems = refs[:-2], refs[-2:]
        ids = [pl.program_id(d) for d in range(len(grid))]
        first = functools.reduce(jnp.logical_and, [i == 0 for i in ids])
        last = functools.reduce(jnp.logical_and, [i == n - 1 for i, n in zip(ids, grid)])

        @pl.when(first)
        def _():
            for cp in side.make(s_ins, s_outs, *sems):
                cp.start()

        body(*ins, *outs, *scratch)

        @pl.when(last)
        def _():
            if side.finish is not None:
                side.finish(s_ins, s_outs, *sems)
            else:
                for cp in side.make(s_ins, s_outs, *sems):
                    cp.wait()

    return wrapped


def _side_args(side):
    if side is None:
        return [], [], [], [], []
    return ([ANY] * len(side.ins), [ANY] * len(side.out_shapes), side.out_shapes, side.sems(), side.ins)


def _split_side(res, n_out, side):
    res = list(res)
    main = res[0] if n_out == 1 else res[:n_out]
    return main if side is None else (main, res[n_out:])


def _mm(a, b, mode, name, *, epi=None, extras=(), fulls=(), out_dtypes=(F32,), sums=(), tm=1024, tn=1024, tk=1024,
        side=None):
    if mode == "nn":
        (m, k), (k2, n) = a.shape, b.shape
    elif mode == "nt":
        (m, k), (n, k2) = a.shape, b.shape
    else:
        (k, m), (k2, n) = a.shape, b.shape
    assert k == k2, (name, a.shape, b.shape)
    tm, tn, tk = min(tm, m), min(tn, n), min(tk, k)
    assert m % tm == 0 and n % tn == 0 and k % tk == 0, (name, m, n, k)
    nk = k // tk
    dims = {"nn": NN, "nt": NT, "tn": TN}[mode]
    if mode == "tn":
        a_spec = pl.BlockSpec((tk, tm), lambda i, j, kk: (kk, i))
    else:
        a_spec = pl.BlockSpec((tm, tk), lambda i, j, kk: (i, kk))
    if mode == "nt":
        b_spec = pl.BlockSpec((tn, tk), lambda i, j, kk: (j, kk))
    else:
        b_spec = pl.BlockSpec((tk, tn), lambda i, j, kk: (kk, j))
    mn_spec = pl.BlockSpec((tm, tn), lambda i, j, kk: (i, j))
    n_ex, n_full, n_out, n_sum = len(extras), len(fulls), len(out_dtypes), len(sums)
    n_in = 2 + n_ex + n_full

    def body(*refs):
        a_ref, b_ref = refs[:2]
        ex = refs[2:n_in]
        outs = refs[n_in:n_in + n_out]
        sum_refs = refs[n_in + n_out:n_in + n_out + n_sum]
        kk = pl.program_id(2)
        first_tile = jnp.logical_and(pl.program_id(0) == 0, pl.program_id(1) == 0)

        def finish(r):
            vals = epi(r, *[e[...] for e in ex]) if epi is not None else (r,)
            if n_sum:
                vals, parts = vals

                @pl.when(first_tile)
                def _():
                    for sr in sum_refs:
                        sr[...] = jnp.zeros_like(sr)

                for sr, p in zip(sum_refs, parts):
                    sr[...] += p
            for o, v in zip(outs, vals):
                o[...] = v.astype(o.dtype)

        if nk == 1:
            finish(_dot(a_ref[...], b_ref[...], dims))
        else:
            acc = refs[n_in + n_out + n_sum]

            @pl.when(kk == 0)
            def _():
                acc[...] = jnp.zeros_like(acc)

            acc[...] += _dot(a_ref[...], b_ref[...], dims)

            @pl.when(kk == nk - 1)
            def _():
                finish(acc[...])

    grid = (m // tm, n // tn, nk)
    whole = lambda shape: pl.BlockSpec(shape, lambda i, j, kk: (0,) * len(shape))
    s_in, s_out, s_shape, s_scratch, s_ops = _side_args(side)
    seq = bool(side) or n_sum > 0
    res = pl.pallas_call(
        _hosted(body, side, n_in, n_out + n_sum, grid), name=name, grid=grid,
        in_specs=[a_spec, b_spec] + [mn_spec] * n_ex + [whole(f.shape) for f in fulls] + s_in,
        out_specs=[mn_spec] * n_out + [whole(shape) for shape in sums] + s_out,
        out_shape=[jax.ShapeDtypeStruct((m, n), dt) for dt in out_dtypes]
        + [jax.ShapeDtypeStruct(shape, F32) for shape in sums] + s_shape,
        scratch_shapes=([pltpu.VMEM((tm, tn), F32)] if nk > 1 else []) + s_scratch,
        compiler_params=_params(("arbitrary",) * 3 if seq else ("parallel", "parallel", "arbitrary")),
    )(a, b, *extras, *fulls, *s_ops)
    return _split_side(res, n_out + n_sum, side)


def _row_tile(s, target):
    if s <= target:
        return s
    return max(t for t in range(16, target + 1, 16) if s % t == 0)


def _rw(fn, rows, fulls, row_out, acc_out, name, tm=1024, side=None):
    cols = [r[1:] if isinstance(r, tuple) else (r.shape[1], 0) for r in rows]
    rows = [r[0] if isinstance(r, tuple) else r for r in rows]
    s = rows[0].shape[0]
    tm = _row_tile(s, tm)
    nr, nf, nro, nao = len(rows), len(fulls), len(row_out), len(acc_out)

    def body(*refs):
        r = refs[:nr]
        f = refs[nr:nr + nf]
        ro = refs[nr + nf:nr + nf + nro]
        ao = refs[nr + nf + nro:]
        outs, accs = fn(*[x[...] for x in r], *[x[...] for x in f])
        for o, v in zip(ro, outs):
            o[...] = v.astype(o.dtype)
        if nao:
            @pl.when(pl.program_id(0) == 0)
            def _():
                for a in ao:
                    a[...] = jnp.zeros_like(a)

            for a, v in zip(ao, accs):
                a[...] += v

    full_spec = lambda shape: pl.BlockSpec(shape, lambda i: (0,) * len(shape))
    s_in, s_out, s_shape, s_scratch, s_ops = _side_args(side)
    res = pl.pallas_call(
        _hosted(body, side, nr + nf, nro + nao, (s // tm,)), name=name, grid=(s // tm,),
        in_specs=[pl.BlockSpec((tm, wd), functools.partial(lambda i, cb: (i, cb), cb=cb)) for wd, cb in cols]
        + [full_spec(x.shape) for x in fulls] + s_in,
        out_specs=[pl.BlockSpec((tm, d), lambda i: (i, 0)) for d, _ in row_out]
        + [full_spec(shape) for shape in acc_out] + s_out,
        out_shape=[jax.ShapeDtypeStruct((s, d), dt) for d, dt in row_out]
        + [jax.ShapeDtypeStruct(shape, F32) for shape in acc_out] + s_shape,
        scratch_shapes=s_scratch,
        compiler_params=_params(("arbitrary",)),
    )(*rows, *fulls, *s_ops)
    res = list(res)
    return res if side is None else (res[:nro + nao], res[nro + nao:])


def _norm_fwd(x, g, name, side=None):
    res = _rw(lambda xt, gt: ((_rms(xt, gt),), ()), [x], [g], [(x.shape[1], BF16)], [], name, side=side)
    return res[0] if side is None else (res[0][0], res[1])


def _norm_bwd(x, g, dh, dres, name, side=None):
    def fn(xt, dht, drt, gt):
        _, vjp = jax.vjp(_rms, xt, gt)
        dx, dg = vjp(dht)
        return (dx + drt,), (dg,)

    return _rw(fn, [x, dh, dres], [g], [(x.shape[1], F32)], [g.shape], name, side=side)


def _rms_groups(x, g, scale):
    lo = lax.broadcasted_iota(jnp.int32, (1, LANES), 1) < SB_HEAD_DIM
    x2 = x * x
    outs = []
    for cb in range(x.shape[1] // LANES):
        sl = slice(cb * LANES, (cb + 1) * LANES)
        s_lo = jnp.sum(jnp.where(lo, x2[:, sl], 0.0), axis=-1, keepdims=True)
        s_hi = jnp.sum(jnp.where(lo, 0.0, x2[:, sl]), axis=-1, keepdims=True)
        r = jnp.where(lo, lax.rsqrt(s_lo * (1.0 / SB_HEAD_DIM) + NORM_EPS),
                      lax.rsqrt(s_hi * (1.0 / SB_HEAD_DIM) + NORM_EPS))
        outs.append(x[:, sl] * r)
    return jnp.concatenate(outs, axis=1) * g * scale


def _log_sigmoid(z):
    return jnp.minimum(z, 0.0) - jnp.log(1.0 + jnp.exp(-jnp.abs(z)))


def _split_dot(x, u2):
    hi = x.astype(BF16)
    lo = (x - hi.astype(F32)).astype(BF16)
    return jnp.dot(jnp.concatenate([hi, lo], axis=1), u2, preferred_element_type=F32)


def _sb_consts(b):
    row = lax.broadcasted_iota(jnp.int32, (b, b), 0)
    col = lax.broadcasted_iota(jnp.int32, (b, b), 1)
    tri = col < row
    u_after = (row > col).astype(BF16)
    u_from = (row >= col).astype(BF16)
    stack = lambda u: jnp.concatenate([u, u], axis=0)
    lane_lo = lax.broadcasted_iota(jnp.int32, (b, LANES), 1) < SB_HEAD_DIM
    return tri, stack(u_after), stack(u_from), lane_lo


def _sb_scores(qh, kb, a_run, keep, u2_after, mask_l=True):
    z = lax.dot_general(qh, kb, NT, preferred_element_type=F32)
    lb = _log_sigmoid(z)
    l = lb - z
    if keep is not None and mask_l:
        l = jnp.where(keep, l, 0.0)
    w = jnp.exp(lb + (a_run + _split_dot(l, u2_after)))
    if keep is not None:
        w = jnp.where(keep, w, 0.0)
    return lb, l, w


def _sb_walk(qi, carry, step):
    def cond(state):
        n, c = state
        return jnp.logical_and(n <= qi, jnp.max(jnp.maximum(c[0], c[1])) > SB_UNDERFLOW)

    def body(state):
        n, c = state
        return n + 1, step(n, c)

    return lax.while_loop(cond, body, (jnp.int32(2), carry))[1]


def _two_heads(x, lane_lo):
    zero = jnp.zeros_like(x)
    return jnp.where(lane_lo, x, zero), jnp.where(lane_lo, zero, x)


def _sb_fwd(qs, ks, v, name, v_col=0, side=None):
    s, width = qs.shape
    b = min(SB_BLOCK, s)
    nqb = min(SB_Q_BLOCKS, s // b)

    def body(q_ref, k_ref, v_ref, o_ref):
        tri, u2_after, _, lane_lo = _sb_consts(b)
        zero = jnp.zeros((b, 1), F32)
        started = []
        for h in range(nqb):
            qi = pl.program_id(1) * nqb + h
            q_a, q_b = _two_heads(q_ref[h * b:(h + 1) * b, :], lane_lo)

            def step(n, carry, keep, mask_l=True, qi=qi, q_a=q_a, q_b=q_b):
                a_a, a_b, acc = carry
                off = pl.multiple_of(jnp.maximum(qi - n, 0) * b, b)
                kb = k_ref[pl.ds(off, b), :]
                v_a, v_b = _two_heads(v_ref[pl.ds(off, b), :].astype(BF16), lane_lo)
                _, l_a, w_a = _sb_scores(q_a, kb, a_a, keep, u2_after, mask_l)
                _, l_b, w_b = _sb_scores(q_b, kb, a_b, keep, u2_after, mask_l)
                acc = acc + jnp.dot(jnp.concatenate([w_a.astype(BF16), w_b.astype(BF16)], axis=1),
                                    jnp.concatenate([v_a, v_b], axis=0), preferred_element_type=F32)
                return (a_a + jnp.sum(l_a, axis=1, keepdims=True), a_b + jnp.sum(l_b, axis=1, keepdims=True), acc)

            carry = step(0, (zero, zero, jnp.zeros((b, LANES), F32)), tri)
            carry = step(1, carry, jnp.broadcast_to(qi > 0, tri.shape), mask_l=False)
            started.append((qi, step, carry))
        for h, (qi, step, carry) in enumerate(started):
            carry = _sb_walk(qi, carry, lambda n, c, step=step: step(n, c, None))
            o_ref[h * b:(h + 1) * b, :] = carry[2]

    blk = pl.BlockSpec((nqb * b, LANES), lambda hp, i: (i, hp))
    full = pl.BlockSpec((s, LANES), lambda hp, i: (0, hp))
    full_v = pl.BlockSpec((s, LANES), lambda hp, i: (0, hp + v_col))
    grid = (width // LANES, s // (nqb * b))
    s_in, s_out, s_shape, s_scratch, s_ops = _side_args(side)
    res = pl.pallas_call(
        _hosted(body, side, 3, 1, grid), name=name, grid=grid,
        in_specs=[blk, full, full_v] + s_in, out_specs=[blk] + s_out,
        out_shape=[jax.ShapeDtypeStruct((s, width), F32)] + s_shape, scratch_shapes=s_scratch,
        compiler_params=_params(("arbitrary", "arbitrary")),
    )(qs, ks, v, *s_ops)
    return _split_side(res, 1, side)


def _sb_bwd(qs, ks, v, out, dout, name, v_col=0, side=None):
    s, width = qs.shape
    b = min(SB_BLOCK, s)
    nqb = min(SB_Q_BLOCKS, s // b)

    def body(q_ref, k_ref, v_ref, o_ref, do_ref, dq_ref, dk_ref, dv_ref):
        @pl.when(pl.program_id(1) == 0)
        def _():
            dk_ref[...] = jnp.zeros_like(dk_ref)
            dv_ref[...] = jnp.zeros_like(dv_ref)

        tri, u2_after, u2_from, lane_lo = _sb_consts(b)
        zero = jnp.zeros((b, 1), F32)

        def head(qh, doh, kb, vb, a_run, d_rem, keep, mask_l):
            lb, l, w = _sb_scores(qh, kb, a_run, keep, u2_after, mask_l)
            wb = w.astype(BF16)
            g = lax.dot_general(doh, vb, NT, preferred_element_type=F32) * wb.astype(F32)
            g_before = d_rem - _split_dot(g, u2_from)
            dz = g - (g + g_before) * jnp.exp(lb)
            if keep is not None:
                dz = jnp.where(keep, dz, 0.0)
            return (dz.astype(BF16), wb, a_run + jnp.sum(l, axis=1, keepdims=True),
                    d_rem - jnp.sum(g, axis=1, keepdims=True))

        started = []
        for h in range(nqb):
            qi = pl.program_id(1) * nqb + h
            rows = slice(h * b, (h + 1) * b)
            q_a, q_b = _two_heads(q_ref[rows, :], lane_lo)
            dob = do_ref[rows, :].astype(BF16)
            do_a, do_b = _two_heads(dob, lane_lo)
            prod = dob.astype(F32) * o_ref[rows, :]
            d_a = jnp.sum(jnp.where(lane_lo, prod, 0.0), axis=1, keepdims=True)
            d_b = jnp.sum(jnp.where(lane_lo, 0.0, prod), axis=1, keepdims=True)
            q_rows = jnp.concatenate([q_a, q_b], axis=0)
            do_rows = jnp.concatenate([do_a, do_b], axis=0)

            def step(n, carry, keep, mask_l=True, qi=qi, q_a=q_a, q_b=q_b, do_a=do_a, do_b=do_b, q_rows=q_rows,
                     do_rows=do_rows):
                a_a, a_b, r_a, r_b, dq = carry
                off = pl.multiple_of(jnp.maximum(qi - n, 0) * b, b)
                kb = k_ref[pl.ds(off, b), :]
                vb = v_ref[pl.ds(off, b), :].astype(BF16)
                k_a, k_b = _two_heads(kb, lane_lo)
                dz_a, w_a, a_a, r_a = head(q_a, do_a, kb, vb, a_a, r_a, keep, mask_l)
                dz_b, w_b, a_b, r_b = head(q_b, do_b, kb, vb, a_b, r_b, keep, mask_l)
                dq = dq + jnp.dot(jnp.concatenate([dz_a, dz_b], axis=1), jnp.concatenate([k_a, k_b], axis=0),
                                  preferred_element_type=F32)
                dk_ref[pl.ds(off, b), :] += lax.dot_general(jnp.concatenate([dz_a, dz_b], axis=0), q_rows, TN,
                                                            preferred_element_type=F32)
                dv_ref[pl.ds(off, b), :] += lax.dot_general(jnp.concatenate([w_a, w_b], axis=0), do_rows, TN,
                                                            preferred_element_type=F32)
                return a_a, a_b, r_a, r_b, dq

            carry = step(0, (zero, zero, d_a, d_b, jnp.zeros((b, LANES), F32)), tri)
            carry = step(1, carry, jnp.broadcast_to(qi > 0, tri.shape), mask_l=False)
            started.append((qi, step, carry))
        for h, (qi, step, carry) in enumerate(started):
            carry = _sb_walk(qi, carry, lambda n, c, step=step: step(n, c, None))
            dq_ref[h * b:(h + 1) * b, :] = carry[4]

    blk = pl.BlockSpec((nqb * b, LANES), lambda hp, i: (i, hp))
    full = pl.BlockSpec((s, LANES), lambda hp, i: (0, hp))
    full_v = pl.BlockSpec((s, LANES), lambda hp, i: (0, hp + v_col))
    grid = (width // LANES, s // (nqb * b))
    s_in, s_out, s_shape, s_scratch, s_ops = _side_args(side)
    res = pl.pallas_call(
        _hosted(body, side, 5, 3, grid), name=name, grid=grid,
        in_specs=[blk, full, full_v, blk, blk] + s_in, out_specs=[blk, full, full] + s_out,
        out_shape=[jax.ShapeDtypeStruct((s, width), F32)] * 3 + s_shape,
        scratch_shapes=s_scratch,
        compiler_params=_params(("arbitrary", "arbitrary")),
    )(qs, ks, v, out, dout, *s_ops)
    return _split_side(res, 3, side)


def _cmul(xr, xi, yr, yi):
    return xr * yr - xi * yi, xr * yi + xi * yr


def _scan_consts(ar, ai, reverse, lc):
    rowi = lax.broadcasted_iota(jnp.int32, (SUBLANES, lc), 0)
    pows = [(ar, ai)]
    for _ in range(SUBLANES - 1):
        pows.append(_cmul(*pows[-1], ar, ai))
    steps = []
    for d in (1, 2, 4):
        keep = (rowi < SUBLANES - d) if reverse else (rowi >= d)
        pr, pi = pows[d - 1]
        steps.append((SUBLANES - d if reverse else d, jnp.where(keep, pr, 0.0), jnp.where(keep, pi, 0.0)))
    cr = jnp.zeros((SUBLANES, lc), F32)
    ci = jnp.zeros((SUBLANES, lc), F32)
    for r in range(SUBLANES):
        pr, pi = pows[SUBLANES - 1 - r] if reverse else pows[r]
        cr = jnp.where(rowi == r, pr, cr)
        ci = jnp.where(rowi == r, pi, ci)
    return steps, cr, ci


def _scan_tile(xr, xi, steps, pr, pi, cr, ci):
    for shift, ar, ai in steps:
        rr = pltpu.roll(xr, shift, 0)
        ri = pltpu.roll(xi, shift, 0)
        xr, xi = xr + ar * rr - ai * ri, xi + ar * ri + ai * rr
    return xr + pr * cr - pi * ci, xi + pr * ci + pi * cr


SCAN_ROWS = 1024


def _scan_chunk(s):
    tt = min(SCAN_ROWS, s)
    seg = tt // SUBLANES
    assert s % tt == 0 and seg % SUBLANES == 0 and seg & (seg - 1) == 0, s
    return tt, seg


def _to_segments(a):
    s, wd = a.shape
    tt, seg = _scan_chunk(s)
    return jnp.transpose(a.reshape(s // tt, SUBLANES, seg, wd), (0, 2, 1, 3)).reshape(s, wd)


def _from_segments(a):
    s, wd = a.shape
    tt, seg = _scan_chunk(s)
    return jnp.transpose(a.reshape(s // tt, seg, SUBLANES, wd), (0, 2, 1, 3)).reshape(s, wd)


def _cpow2(xr, xi, k):
    for _ in range(k):
        xr, xi = _cmul(xr, xi, xr, xi)
    return xr, xi


def _fill_powers(pw_ref, ar, ai, seg, lc):
    _, p8r, p8i = _scan_consts(ar, ai, False, lc)
    a8r, a8i = _cpow2(ar, ai, 3)
    qr, qi = jnp.ones_like(ar), jnp.zeros_like(ai)
    for k in range(seg // SUBLANES):
        tr, ti = _cmul(p8r, p8i, qr, qi)
        for r in range(SUBLANES):
            rows = pl.ds((SUBLANES * k + r) * SUBLANES, SUBLANES)
            pw_ref[rows, :lc] = jnp.broadcast_to(tr[r:r + 1, :], (SUBLANES, lc))
            pw_ref[rows, lc:] = jnp.broadcast_to(ti[r:r + 1, :], (SUBLANES, lc))
        qr, qi = _cmul(qr, qi, a8r, a8i)


def _ssm_fwd(u, acat, bsup, csup, d_skip, name, side=None):
    s = u.shape[0]
    lc = SCAN_LANES
    tt, seg = _scan_chunk(s)
    nl, nt = N_STATE // lc, s // tt
    tile = lambda j: pl.ds(pl.multiple_of(j * SUBLANES, SUBLANES), SUBLANES)

    def body(u_ref, a_ref, b_ref, c_ref, d_ref, s_ref, y0_ref, y1_ref, carry, pw_ref):
        ar, ai = a_ref[:, :lc], a_ref[:, lc:]

        @pl.when(pl.program_id(1) == 0)
        def _():
            carry[...] = jnp.zeros_like(carry)
            _fill_powers(pw_ref, ar, ai, seg, lc)

        ut = u_ref[...]
        s_ref[...] = _dot(ut, b_ref[0])

        ar8, ai8 = jnp.broadcast_to(ar, (SUBLANES, lc)), jnp.broadcast_to(ai, (SUBLANES, lc))

        def local(j, x):
            xr = ar8 * x[0] - ai8 * x[1] + s_ref[tile(j), :lc]
            xi = ar8 * x[1] + ai8 * x[0] + s_ref[tile(j), lc:]
            s_ref[tile(j), :lc] = xr
            s_ref[tile(j), lc:] = xi
            return xr, xi

        zero = jnp.zeros((SUBLANES, lc), F32)
        er, ei = lax.fori_loop(0, seg, local, (zero, zero))
        steps, pr, pi = _scan_consts(*_cpow2(ar, ai, seg.bit_length() - 1), False, lc)
        cr, ci = carry[:, :lc], carry[:, lc:]
        tr, ti = _scan_tile(er, ei, steps, pr, pi, cr, ci)
        rowi = lax.broadcasted_iota(jnp.int32, (SUBLANES, lc), 0)
        before_r = jnp.where(rowi == 0, cr, pltpu.roll(tr, 1, 0))
        before_i = jnp.where(rowi == 0, ci, pltpu.roll(ti, 1, 0))
        carry[:, :lc] = jnp.broadcast_to(tr[SUBLANES - 1:, :], (SUBLANES, lc))
        carry[:, lc:] = jnp.broadcast_to(ti[SUBLANES - 1:, :], (SUBLANES, lc))

        def fix(j, _):
            pwr, pwi = pw_ref[tile(j), :lc], pw_ref[tile(j), lc:]
            s_ref[tile(j), :lc] += pwr * before_r - pwi * before_i
            s_ref[tile(j), lc:] += pwr * before_i + pwi * before_r
            return 0

        lax.fori_loop(0, seg, fix, 0)
        y0 = _dot(s_ref[...], c_ref[0], NT) + d_ref[...] * ut
        y0_ref[...] = y0
        y1_ref[...] = jax.nn.gelu(y0)

    chan = pl.BlockSpec((tt, LANES), lambda j, c: (c, j))
    sup = pl.BlockSpec((1, LANES, 2 * lc), lambda j, c: (j, 0, 0))
    s_in, s_out, s_shape, s_scratch, s_ops = _side_args(side)
    res = pl.pallas_call(
        _hosted(body, side, 5, 3, (nl, nt)), name=name, grid=(nl, nt),
        in_specs=[chan, pl.BlockSpec((1, 2 * lc), lambda j, c: (0, j)), sup, sup,
                  pl.BlockSpec((1, LANES), lambda j, c: (0, j))] + s_in,
        out_specs=[pl.BlockSpec((tt, 2 * lc), lambda j, c: (c, j)), chan, chan] + s_out,
        out_shape=[jax.ShapeDtypeStruct((s, 2 * N_STATE), F32), jax.ShapeDtypeStruct((s, SSM_WIDTH), F32),
                   jax.ShapeDtypeStruct((s, SSM_WIDTH), F32)] + s_shape,
        scratch_shapes=[pltpu.VMEM((SUBLANES, 2 * lc), F32), pltpu.VMEM((seg * SUBLANES, 2 * lc), F32)] + s_scratch,
        compiler_params=_params(("arbitrary", "arbitrary")),
    )(u, acat, bsup, csup, d_skip, *s_ops)
    return _split_side(res, 3, side)


def _ssm_bwd(dy0, states, u, acat, bsup, csup, d_skip, name, side=None):
    s = u.shape[0]
    lc = SCAN_LANES
    tt, seg = _scan_chunk(s)
    nl, nt = N_STATE // lc, s // tt
    tile = lambda j: pl.ds(pl.multiple_of(j * SUBLANES, SUBLANES), SUBLANES)

    def body(dy_ref, s_ref, sp_ref, u_ref, a_ref, b_ref, c_ref, d_ref,
             du_ref, da_ref, db_ref, dc_ref, dd_ref, lam_ref, carry, pw_ref):
        c = pl.program_id(1)
        ar, ai = a_ref[:, :lc], a_ref[:, lc:]

        @pl.when(c == 0)
        def _():
            carry[...] = jnp.zeros_like(carry)
            for r in (da_ref, db_ref, dc_ref, dd_ref):
                r[...] = jnp.zeros_like(r)
            _fill_powers(pw_ref, ar, ai, seg, lc)

        dy = dy_ref[...]
        ut = u_ref[...]
        lam_ref[...] = _dot(dy, c_ref[0])

        ar8, ai8 = jnp.broadcast_to(ar, (SUBLANES, lc)), jnp.broadcast_to(ai, (SUBLANES, lc))

        def local(i, x):
            j = seg - 1 - i
            xr = ar8 * x[0] + ai8 * x[1] + lam_ref[tile(j), :lc]
            xi = ar8 * x[1] - ai8 * x[0] + lam_ref[tile(j), lc:]
            lam_ref[tile(j), :lc] = xr
            lam_ref[tile(j), lc:] = xi
            return xr, xi

        zero = jnp.zeros((SUBLANES, lc), F32)
        er, ei = lax.fori_loop(0, seg, local, (zero, zero))
        big_r, big_i = _cpow2(ar, ai, seg.bit_length() - 1)
        steps, pr, pi = _scan_consts(big_r, -big_i, True, lc)
        cr, ci = carry[:, :lc], carry[:, lc:]
        tr, ti = _scan_tile(er, ei, steps, pr, pi, cr, ci)
        rowi = lax.broadcasted_iota(jnp.int32, (SUBLANES, lc), 0)
        after_r = jnp.where(rowi == SUBLANES - 1, cr, pltpu.roll(tr, SUBLANES - 1, 0))
        after_i = jnp.where(rowi == SUBLANES - 1, ci, pltpu.roll(ti, SUBLANES - 1, 0))
        carry[:, :lc] = jnp.broadcast_to(tr[:1, :], (SUBLANES, lc))
        carry[:, lc:] = jnp.broadcast_to(ti[:1, :], (SUBLANES, lc))

        start = c != nt - 1
        last_r = jnp.where(start, jnp.broadcast_to(sp_ref[SUBLANES - 1:, :lc], (SUBLANES, lc)), 0.0)
        last_i = jnp.where(start, jnp.broadcast_to(sp_ref[SUBLANES - 1:, lc:], (SUBLANES, lc)), 0.0)
        first_r = jnp.where(rowi == 0, last_r, pltpu.roll(s_ref[tile(seg - 1), :lc], 1, 0))
        first_i = jnp.where(rowi == 0, last_i, pltpu.roll(s_ref[tile(seg - 1), lc:], 1, 0))

        def fix(j, acc):
            dar, dai = acc
            k = seg - 1 - j
            pwr, pwi = pw_ref[tile(k), :lc], pw_ref[tile(k), lc:]
            lr = lam_ref[tile(j), :lc] + pwr * after_r + pwi * after_i
            li = lam_ref[tile(j), lc:] + pwr * after_i - pwi * after_r
            lam_ref[tile(j), :lc] = lr
            lam_ref[tile(j), lc:] = li
            jp = jnp.maximum(j - 1, 0)
            sr = jnp.where(j > 0, s_ref[tile(jp), :lc], first_r)
            si = jnp.where(j > 0, s_ref[tile(jp), lc:], first_i)
            return dar + lr * sr + li * si, dai + li * sr - lr * si

        dar, dai = lax.fori_loop(0, seg, fix, (zero, zero))
        da_ref[:, :lc] += dar
        da_ref[:, lc:] += dai
        lam = lam_ref[...].astype(BF16)
        du_ref[...] = (_dot(lam, b_ref[0], NT) + d_ref[...] * dy).astype(du_ref.dtype)
        db_ref[0] += _dot(ut, lam, TN)
        dc_ref[0] += _dot(dy, s_ref[...], TN)
        dd_ref[...] += jnp.sum(dy * ut, axis=0, keepdims=True)

    rev = lambda j, c: (nt - 1 - c, j)
    chan = pl.BlockSpec((tt, LANES), rev)
    sup = pl.BlockSpec((1, LANES, 2 * lc), lambda j, c: (j, 0, 0))
    row = pl.BlockSpec((1, LANES), lambda j, c: (0, j))
    s_in, s_out, s_shape, s_scratch, s_ops = _side_args(side)
    res = pl.pallas_call(
        _hosted(body, side, 8, 5, (nl, nt)), name=name, grid=(nl, nt),
        in_specs=[chan, pl.BlockSpec((tt, 2 * lc), rev),
                  pl.BlockSpec((SUBLANES, 2 * lc), lambda j, c: (jnp.maximum((nt - 1 - c) * seg - 1, 0), j)),
                  chan, pl.BlockSpec((1, 2 * lc), lambda j, c: (0, j)), sup, sup, row] + s_in,
        out_specs=[chan, pl.BlockSpec((SUBLANES, 2 * lc), lambda j, c: (0, j)), sup, sup, row] + s_out,
        out_shape=[jax.ShapeDtypeStruct((s, SSM_WIDTH), BF16), jax.ShapeDtypeStruct((SUBLANES, 2 * N_STATE), F32),
                   jax.ShapeDtypeStruct(bsup.shape, F32), jax.ShapeDtypeStruct(csup.shape, F32),
                   jax.ShapeDtypeStruct((1, SSM_WIDTH), F32)] + s_shape,
        scratch_shapes=[pltpu.VMEM((tt, 2 * lc), F32), pltpu.VMEM((SUBLANES, 2 * lc), F32),
                        pltpu.VMEM((seg * SUBLANES, 2 * lc), F32)] + s_scratch,
        compiler_params=_params(("arbitrary", "arbitrary")),
    )(dy0, states, states, u, acat, bsup, csup, d_skip, *s_ops)
    return _split_side(res, 5, side)


def _discretise(ar, ai, ldt, br, bi):
    dt = jnp.exp(ldt)
    lr, li = ar * dt, ai * dt
    e = jnp.exp(lr)
    abar_r, abar_i = e * jnp.cos(li), e * jnp.sin(li)
    den = ar * ar + ai * ai
    coef_r = ((abar_r - 1.0) * ar + abar_i * ai) / den
    coef_i = (abar_i * ar - (abar_r - 1.0) * ai) / den
    return abar_r, abar_i, coef_r * br - coef_i * bi, coef_r * bi + coef_i * br


def _group_mask():
    shape = (LANES, SCAN_LANES)
    return (lax.broadcasted_iota(jnp.int32, shape, 0) // SSM_GROUP
            == lax.broadcasted_iota(jnp.int32, shape, 1) // SSM_STATE)


def _ssm_mats_fwd(a_re, a_im, log_dt, b_re, b_im, c_re, c_im, name):
    nl = N_STATE // SCAN_LANES
    lc = SCAN_LANES

    def body(ar, ai, ldt, br, bi, cr, ci, acat, bsup, csup):
        abar_r, abar_i, bbar_r, bbar_i = _discretise(ar[...], ai[...], ldt[...], br[...], bi[...])
        same = _group_mask()
        spread = lambda m, j: jnp.where(same, jnp.tile(m[:, j * lc:(j + 1) * lc], (LANES // SSM_GROUP, 1)), 0.0)
        c_r, c_i = cr[...], -ci[...]
        for j in range(nl):
            acat[:, 2 * j * lc:(2 * j + 1) * lc] = abar_r[:, j * lc:(j + 1) * lc]
            acat[:, (2 * j + 1) * lc:(2 * j + 2) * lc] = abar_i[:, j * lc:(j + 1) * lc]
            bsup[j, :, :lc] = spread(bbar_r, j)
            bsup[j, :, lc:] = spread(bbar_i, j)
            csup[j, :, :lc] = spread(c_r, j)
            csup[j, :, lc:] = spread(c_i, j)

    return pl.pallas_call(
        body, name=name,
        out_shape=[jax.ShapeDtypeStruct((1, 2 * N_STATE), F32), jax.ShapeDtypeStruct((nl, LANES, 2 * lc), F32),
                   jax.ShapeDtypeStruct((nl, LANES, 2 * lc), F32)],
        compiler_params=_params(),
    )(a_re, a_im, log_dt, b_re, b_im, c_re, c_im)


def _ssm_mats_bwd(a_re, a_im, log_dt, b_re, b_im, d_acat, d_bsup, d_csup, name):
    nl = N_STATE // SCAN_LANES
    lc = SCAN_LANES

    def body(ar, ai, ldt, br, bi, dac, dbs, dcs, d_ar, d_ai, d_ldt, d_br, d_bi, d_cr, d_ci):
        same = _group_mask()

        def gather(ref, j, half):
            m = jnp.where(same, ref[j, :, half * lc:(half + 1) * lc], 0.0)
            tot = m[:SSM_GROUP]
            for k in range(1, LANES // SSM_GROUP):
                tot = tot + m[k * SSM_GROUP:(k + 1) * SSM_GROUP]
            return tot

        cols = lambda ref, half: jnp.concatenate([gather(ref, j, half) for j in range(nl)], axis=1)
        d_abar_r = jnp.concatenate([dac[:, 2 * j * lc:(2 * j + 1) * lc] for j in range(nl)], axis=1)
        d_abar_i = jnp.concatenate([dac[:, (2 * j + 1) * lc:(2 * j + 2) * lc] for j in range(nl)], axis=1)
        _, vjp = jax.vjp(_discretise, ar[...], ai[...], ldt[...], br[...], bi[...])
        outs = vjp((d_abar_r, d_abar_i, cols(dbs, 0), cols(dbs, 1)))
        for ref, val in zip((d_ar, d_ai, d_ldt, d_br, d_bi), outs):
            ref[...] = val
        d_cr[...] = cols(dcs, 0)
        d_ci[...] = -cols(dcs, 1)

    row = jax.ShapeDtypeStruct((1, N_STATE), F32)
    mat = jax.ShapeDtypeStruct((SSM_GROUP, N_STATE), F32)
    return pl.pallas_call(
        body, name=name, out_shape=[row, row, row, mat, mat, mat, mat], compiler_params=_params(),
    )(a_re, a_im, log_dt, b_re, b_im, d_acat, d_bsup, d_csup)


def _states_on_lanes(sm):
    flat = lambda a: a.reshape(1, N_STATE)
    chan_b = lambda b: jnp.transpose(b, (2, 0, 1)).reshape(SSM_GROUP, N_STATE)
    chan_c = lambda c: jnp.transpose(c, (1, 0, 2)).reshape(SSM_GROUP, N_STATE)
    return (flat(sm["ssm_a_re"]), flat(sm["ssm_a_im"]), flat(jnp.repeat(sm["ssm_log_dt"], SSM_STATE)),
            chan_b(sm["ssm_b_re"]), chan_b(sm["ssm_b_im"]), chan_c(sm["ssm_c_re"]), chan_c(sm["ssm_c_im"]))


def _from_states_on_lanes(d_ar, d_ai, d_ldt, d_br, d_bi, d_cr, d_ci):
    grp = lambda a: a.reshape(SSM_GROUPS, SSM_STATE)
    back_b = lambda b: jnp.transpose(b.reshape(SSM_GROUP, SSM_GROUPS, SSM_STATE), (1, 2, 0))
    back_c = lambda c: jnp.transpose(c.reshape(SSM_GROUP, SSM_GROUPS, SSM_STATE), (1, 0, 2))
    return (grp(d_ar), grp(d_ai), jnp.sum(grp(d_ldt), axis=1), back_b(d_br), back_b(d_bi), back_c(d_cr), back_c(d_ci))


def _mem_fwd(mem, g_mem, w_kv, g_k, name):
    ml = mem.shape[0]

    def body(mem_ref, gm_ref, w_ref, gk_ref, memn_ref, kv_ref, kn_ref, vv_ref):
        memn = _rms(mem_ref[...], gm_ref[...])
        memn_ref[...] = memn.astype(BF16)
        kv = _dot(memn, w_ref[...])
        kv_ref[...] = kv
        for hh in range(XA_HEADS):
            sl = slice(hh * XA_HEAD_DIM, (hh + 1) * XA_HEAD_DIM)
            kn_ref[:, sl] = _rms(kv[:, sl], gk_ref[...]).astype(BF16)
        vv_ref[...] = kv[:, XA_WIDTH:].astype(BF16)

    return pl.pallas_call(
        body, name=name,
        out_shape=[jax.ShapeDtypeStruct((ml, D_MODEL), BF16), jax.ShapeDtypeStruct((ml, 2 * XA_WIDTH), F32),
                   jax.ShapeDtypeStruct((ml, XA_WIDTH), BF16), jax.ShapeDtypeStruct((ml, XA_WIDTH), BF16)],
        compiler_params=_params(),
    )(mem, g_mem, w_kv, g_k)


def _mem_bwd(mem, g_mem, memn, w_kv, kv, g_k, dkn, dvv, name):
    def body(mem_ref, gm_ref, memn_ref, w_ref, kv_ref, gk_ref, dkn_ref, dvv_ref, dw_ref, dgm_ref, dgk_ref):
        kv = kv_ref[...]
        dgk = jnp.zeros(dgk_ref.shape, F32)
        parts = []
        for hh in range(XA_HEADS):
            sl = slice(hh * XA_HEAD_DIM, (hh + 1) * XA_HEAD_DIM)
            _, vjp = jax.vjp(_rms, kv[:, sl], gk_ref[...])
            dk, dg = vjp(dkn_ref[:, sl])
            parts.append(dk)
            dgk = dgk + dg
        dgk_ref[...] = dgk
        dkv = jnp.concatenate(parts + [dvv_ref[...]], axis=1)
        dw_ref[...] = _dot(memn_ref[...], dkv, TN)
        dmemn = _dot(dkv, w_ref[...], NT)
        _, vjp = jax.vjp(_rms, mem_ref[...], gm_ref[...])
        dgm_ref[...] = vjp(dmemn)[1]

    return pl.pallas_call(
        body, name=name,
        out_shape=[jax.ShapeDtypeStruct((D_MODEL, 2 * XA_WIDTH), F32), jax.ShapeDtypeStruct(g_mem.shape, F32),
                   jax.ShapeDtypeStruct(g_k.shape, F32)],
        compiler_params=_params(),
    )(mem, g_mem, memn, w_kv, kv, g_k, dkn, dvv)


def _xa_head(qx_h, g_q, kn_h, vv_h):
    qn = _rms(qx_h, g_q)
    sc = _dot(qn, kn_h, NT) * (XA_HEAD_DIM ** -0.5)
    sc = sc - jnp.max(sc, axis=-1, keepdims=True)
    e = jnp.exp(sc)
    p = e / jnp.sum(e, axis=-1, keepdims=True)
    return qn, p


def _xa_fwd(qx, g_q, kn, vv, name):
    def fn(qt, gq, knt, vvt):
        outs = []
        for hh in range(XA_HEADS):
            sl = slice(hh * XA_HEAD_DIM, (hh + 1) * XA_HEAD_DIM)
            _, p = _xa_head(qt[:, sl], gq, knt[:, sl], vvt[:, sl])
            outs.append(_dot(p, vvt[:, sl]))
        return (jnp.concatenate(outs, axis=1),), ()

    return _rw(fn, [qx], [g_q, kn, vv], [(XA_WIDTH, BF16)], [], name, tm=512)[0]


def _xa_bwd(qx, g_q, kn, vv, do, name):
    def fn(qt, dot_, gq, knt, vvt):
        dqs, dks, dvs = [], [], []
        dgq = jnp.zeros_like(gq)
        for hh in range(XA_HEADS):
            sl = slice(hh * XA_HEAD_DIM, (hh + 1) * XA_HEAD_DIM)
            qn, p = _xa_head(qt[:, sl], gq, knt[:, sl], vvt[:, sl])
            doh = dot_[:, sl]
            dp = _dot(doh, vvt[:, sl], NT)
            dvs.append(_dot(p, doh, TN))
            ds = p * (dp - jnp.sum(dp * p, axis=-1, keepdims=True)) * (XA_HEAD_DIM ** -0.5)
            dqn = _dot(ds, knt[:, sl])
            dks.append(_dot(ds, qn, TN))
            _, vjp = jax.vjp(_rms, qt[:, sl], gq)
            dq, dg = vjp(dqn)
            dqs.append(dq)
            dgq = dgq + dg
        return ((jnp.concatenate(dqs, axis=1),),
                (jnp.concatenate(dks, axis=1), jnp.concatenate(dvs, axis=1), dgq))

    return _rw(fn, [qx, do], [g_q, kn, vv], [(XA_WIDTH, BF16)], [kn.shape, vv.shape, g_q.shape], name, tm=512)


BIG = [
    ("w_in", (D_MODEL, IN_WIDTH), 1), ("ssm_w_glu", (SSM_WIDTH, SSM_WIDTH), 0), ("w_out", (D_MODEL, D_MODEL), 0),
    ("xa_w_q", (D_MODEL, XA_WIDTH), 0), ("xa_w_kv", (D_MODEL, 2 * XA_WIDTH), 0), ("xa_w_o", (XA_WIDTH, D_MODEL), 1),
    ("w_up", (D_MODEL, D_FF), 1), ("w_down", (D_FF, D_MODEL), 0),
]
BIG_INDEX = {n: i for i, (n, _, _) in enumerate(BIG)}


def _shard_shape(shape, axis):
    return tuple(d // N_DEV if i == axis else d for i, d in enumerate(shape))


def _shard_of(ref, axis, d):
    n = ref.shape[axis] // N_DEV
    return ref.at[pl.ds(d * n, n), :] if axis == 0 else ref.at[:, pl.ds(d * n, n)]


def _gather_two_level_side(names, shards):
    idxs = [BIG_INDEX[n] for n in names]

    def parts(ins, outs, send_sems, recv_sems):
        x, y, c = lax.axis_index("x"), lax.axis_index("y"), lax.axis_index("c")
        sibling = (x, y, 1 - c)
        chips = [(1 - x, y), (x, 1 - y), (1 - x, 1 - y)]
        mine, first, passed, arrived, from_sibling = [], [], [], [], []
        for w, i in enumerate(idxs):
            def place(dev, w=w, i=i):
                return _shard_of(outs[w], BIG[i][2], 4 * dev[0] + 2 * dev[1] + dev[2])

            def copy(k, blk, to, src=None, w=w, place=place):
                return pltpu.make_async_remote_copy(
                    src_ref=place(blk) if src is None else src, dst_ref=place(blk),
                    send_sem=send_sems.at[N_DEV * w + k], recv_sem=recv_sems.at[N_DEV * w + k], device_id=to,
                    device_id_type=MESH)

            mine.append(pltpu.make_async_copy(ins[w], place((x, y, c)), send_sems.at[N_DEV * w + 7]))
            first.append(copy(0, (x, y, c), sibling, src=ins[w]))
            first += [copy(1 + j, (x, y, c), (*chip, c), src=ins[w]) for j, chip in enumerate(chips)]
            passed += [copy(4 + j, (*chip, c), sibling) for j, chip in enumerate(chips)]
            arrived += [copy(1 + j, (*chip, c), (x, y, c)) for j, chip in enumerate(chips)]
            from_sibling.append(copy(0, sibling, (x, y, c)))
            from_sibling += [copy(4 + j, (*chip, 1 - c), (x, y, c)) for j, chip in enumerate(chips)]
        return mine, first, passed, arrived, from_sibling

    def make(ins, outs, send_sems, recv_sems):
        mine, first, _, _, _ = parts(ins, outs, send_sems, recv_sems)
        return mine + first

    def finish(ins, outs, send_sems, recv_sems):
        mine, first, passed, arrived, from_sibling = parts(ins, outs, send_sems, recv_sems)
        for got, onward in zip(arrived, passed):
            got.wait_recv()
            onward.start()
        for cp in from_sibling:
            cp.wait_recv()
        for cp in first + passed:
            cp.wait_send()
        for cp in mine:
            cp.wait()

    return _Side(shards, [jax.ShapeDtypeStruct(BIG[i][1], BF16) for i in idxs], N_DEV * len(idxs), make, finish)


def _sibling_side(names, grads):
    idxs = [BIG_INDEX[n] for n in names]

    def make(ins, outs, send_sems, recv_sems):
        x, y, c = lax.axis_index("x"), lax.axis_index("y"), lax.axis_index("c")
        return [pltpu.make_async_remote_copy(
            src_ref=_shard_of(ins[j], BIG[i][2], 2 * k + (1 - c)), dst_ref=outs[j].at[k],
            send_sem=send_sems.at[4 * j + k], recv_sem=recv_sems.at[4 * j + k], device_id=(x, y, 1 - c),
            device_id_type=MESH) for j, i in enumerate(idxs) for k in range(4)]

    shapes = [jax.ShapeDtypeStruct((4,) + _shard_shape(BIG[i][1], BIG[i][2]), F32) for i in idxs]
    return _Side(grads, shapes, 4 * len(idxs), make)


def _chips_side(parts):
    def make(ins, outs, send_sems, recv_sems):
        x, y, c = lax.axis_index("x"), lax.axis_index("y"), lax.axis_index("c")
        chips = [(1 - x, y), (x, 1 - y), (1 - x, 1 - y)]
        return [pltpu.make_async_remote_copy(
            src_ref=ins[j].at[2 * cx + cy], dst_ref=outs[j].at[r], send_sem=send_sems.at[3 * j + r],
            recv_sem=recv_sems.at[3 * j + r], device_id=(cx, cy, c), device_id_type=MESH)
            for r, (cx, cy) in enumerate(chips) for j in range(len(parts))]

    return _Side(parts, [jax.ShapeDtypeStruct((3,) + p.shape[1:], p.dtype) for p in parts], 3 * len(parts), make)


def _reduce_add(grad, recv, axis, core, name):
    rs, cs = recv.shape[1:]
    rt = _row_tile(rs, 256)
    nt = rs // rt

    def body(c_ref, g_ref, r_ref, p_ref, pb_ref):
        sm = g_ref[...] + r_ref[0]
        p_ref[0] = sm
        pb_ref[0] = sm.astype(BF16)

    if axis == 0:
        g_spec = pl.BlockSpec((rt, cs), lambda k, t, c_ref: ((2 * k + c_ref[0]) * nt + t, 0))
    else:
        g_spec = pl.BlockSpec((rt, cs), lambda k, t, c_ref: (t, 2 * k + c_ref[0]))
    slab = pl.BlockSpec((1, rt, cs), lambda k, t, c_ref: (k, t, 0))
    return pl.pallas_call(
        body, name=name,
        grid_spec=pltpu.PrefetchScalarGridSpec(num_scalar_prefetch=1, grid=(4, nt), in_specs=[g_spec, slab],
                                               out_specs=[slab, slab]),
        out_shape=[jax.ShapeDtypeStruct(recv.shape, F32), jax.ShapeDtypeStruct(recv.shape, BF16)],
        compiler_params=_params(("parallel", "parallel")),
    )(core, grad, recv)


def _all_gather(block, name, side):
    m_per, n = block.shape
    ns_in, ns_out = len(side.ins), len(side.out_shapes)

    def body(*refs):
        x_ref, s_ins, out_ref = refs[0], refs[1:1 + ns_in], refs[1 + ns_in]
        s_outs = refs[2 + ns_in:2 + ns_in + ns_out]
        send_sems, recv_sems, local_sem, s_send, s_recv = refs[2 + ns_in + ns_out:]
        others = side.make(s_ins, s_outs, s_send, s_recv)
        for cp in others:
            cp.start()
        x, y, c = lax.axis_index("x"), lax.axis_index("y"), lax.axis_index("c")
        me, sibling = (x, y, c), (x, y, 1 - c)
        chips = [(1 - x, y), (x, 1 - y), (1 - x, 1 - y)]

        def rows(px, py, pc):
            return out_ref.at[pl.ds((4 * px + 2 * py + pc) * m_per, m_per), :]

        def copy(k, blk, to, src=None):
            return pltpu.make_async_remote_copy(
                src_ref=rows(*blk) if src is None else src, dst_ref=rows(*blk),
                send_sem=send_sems.at[k], recv_sem=recv_sems.at[k], device_id=to, device_id_type=MESH)

        mine = pltpu.make_async_copy(x_ref, rows(*me), local_sem)
        mine.start()
        first = [copy(0, me, sibling, src=x_ref)]
        first += [copy(1 + j, me, (*chip, c), src=x_ref) for j, chip in enumerate(chips)]
        for cp in first:
            cp.start()
        passed = [copy(4 + j, (*chip, c), sibling) for j, chip in enumerate(chips)]
        for j, chip in enumerate(chips):
            copy(1 + j, (*chip, c), me).wait_recv()
            passed[j].start()
        copy(0, sibling, me).wait_recv()
        for j, chip in enumerate(chips):
            copy(4 + j, (*chip, 1 - c), me).wait_recv()
        for cp in first + passed:
            cp.wait_send()
        mine.wait()
        for cp in others:
            cp.wait()

    res = pl.pallas_call(
        body, name=name, in_specs=[ANY] * (1 + ns_in), out_specs=[ANY] * (1 + ns_out),
        out_shape=[jax.ShapeDtypeStruct((N_DEV * m_per, n), block.dtype)] + side.out_shapes,
        scratch_shapes=[pltpu.SemaphoreType.DMA((7,)), pltpu.SemaphoreType.DMA((7,)), pltpu.SemaphoreType.DMA]
        + side.sems(),
    )(block, *side.ins)
    return res[0], list(res[1:])


def _adam_math(w, g, m, v):
    m = ADAM_B1 * m + (1.0 - ADAM_B1) * g
    v = ADAM_B2 * v + (1.0 - ADAM_B2) * (g * g)
    m_hat = m / (1.0 - ADAM_B1 ** ADAM_STEP)
    v_hat = v / (1.0 - ADAM_B2 ** ADAM_STEP)
    delta = -ADAM_LR * (m_hat / (jnp.sqrt(v_hat) + ADAM_EPS) + ADAM_WD * w)
    return delta, m, v


def _adam_sharded(own, recv, w, m, v, chip, name):
    rs, cs = w.shape
    rt = _row_tile(rs, 256)

    def body(chip_ref, p_ref, r_ref, w_ref, m_ref, v_ref, g_out, d_out, m_out, v_out):
        g = p_ref[0] + r_ref[0].astype(F32) + r_ref[1].astype(F32) + r_ref[2].astype(F32)
        d, mn, vn = _adam_math(w_ref[...], g, m_ref[...], v_ref[...])
        g_out[...] = g
        d_out[...] = d
        m_out[...] = mn
        v_out[...] = vn

    tile = pl.BlockSpec((rt, cs), lambda t, chip_ref: (t, 0))
    return pl.pallas_call(
        body, name=name,
        grid_spec=pltpu.PrefetchScalarGridSpec(
            num_scalar_prefetch=1, grid=(rs // rt,),
            in_specs=[pl.BlockSpec((1, rt, cs), lambda t, chip_ref: (chip_ref[0], t, 0)),
                      pl.BlockSpec((3, rt, cs), lambda t, chip_ref: (0, t, 0)), tile, tile, tile],
            out_specs=[tile] * 4),
        out_shape=[jax.ShapeDtypeStruct((rs, cs), F32)] * 4,
        compiler_params=_params(("parallel",)),
    )(chip, own, recv, w, m, v)


SMALL = ["g_mix", "ssm_a_re", "ssm_a_im", "ssm_log_dt", "ssm_b_re", "ssm_b_im", "ssm_c_re", "ssm_c_im", "ssm_d",
         "sb_g_q", "sb_g_k", "g_out_ssm", "g_out_sb", "g_xa", "g_mem", "xa_g_q", "xa_g_k", "g_mlp"]
PACK_TILE = SUBLANES * LANES


def _natural_2d(n):
    return (n // LANES, LANES) if n % LANES == 0 else (1, n)


def _pack_small(arrs):
    parts = []
    for a in arrs:
        flat = a.reshape(-1)
        parts.append(jnp.pad(flat, (0, (-flat.shape[0]) % PACK_TILE)))
    return jnp.concatenate(parts).reshape(-1, LANES)


def _adam_replicated(gathered, sizes, ws, ms, vs, name):
    n_w = len(ws)
    r_dev = gathered.shape[0] // N_DEV
    offs, off = [], 0
    for n in sizes:
        offs.append(off)
        off += (n + PACK_TILE - 1) // PACK_TILE * SUBLANES
    assert off == r_dev

    def body(*refs):
        g_ref = refs[0]
        w_refs, m_refs, v_refs = refs[1:1 + n_w], refs[1 + n_w:1 + 2 * n_w], refs[1 + 2 * n_w:1 + 3 * n_w]
        outs = refs[1 + 3 * n_w:]

        def total(i, shape):
            r, cdim = shape
            acc = g_ref[pl.ds(offs[i], r), :cdim]
            for d in range(1, N_DEV):
                acc = acc + g_ref[pl.ds(d * r_dev + offs[i], r), :cdim]
            return acc

        for i in range(n_w):
            g = total(i, w_refs[i].shape)
            d, mn, vn = _adam_math(w_refs[i][...], g, m_refs[i][...], v_refs[i][...])
            for o, val in zip(outs[4 * i:4 * i + 4], (g, d, mn, vn)):
                o[...] = val
        outs[4 * n_w][...] = total(n_w, (SUBLANES, LANES))

    shapes = [w.shape for w in ws]
    return pl.pallas_call(
        body, name=name,
        out_shape=[jax.ShapeDtypeStruct(shp, F32) for shp in shapes for _ in range(4)]
        + [jax.ShapeDtypeStruct((SUBLANES, LANES), F32)],
        compiler_params=_params(),
    )(gathered, *ws, *ms, *vs)


def _step(x, mem, target, shards, sm, core):
    g, w, sums, reduced = {}, {}, {}, {}

    def gather(names):
        return _gather_two_level_side(names, [shards[n] for n in names])

    def to_sibling(names):
        return _sibling_side(names, [g[n] for n in names])

    def add_sibling(names, received):
        for n, r in zip(names, received):
            sums[n] = _reduce_add(g[n], r, BIG[BIG_INDEX[n]][2], core, "reduce_add_" + n)

    def to_chips(names):
        return _chips_side([sums[n][1] for n in names])

    def keep(names, received):
        for n, r in zip(names, received):
            reduced[n] = (sums[n][0], r)

    row = lambda a: a.reshape(1, -1)
    g_mix, g_xa, g_mlp, g_mem = row(sm["g_mix"]), row(sm["g_xa"]), row(sm["g_mlp"]), row(sm["g_mem"])
    g_os, g_ob = row(sm["g_out_ssm"]), row(sm["g_out_sb"])
    sb_gq, sb_gk = jnp.tile(row(sm["sb_g_q"]), (1, SB_HEADS)), jnp.tile(row(sm["sb_g_k"]), (1, SB_HEADS))
    xa_gq, xa_gk = row(sm["xa_g_q"]), row(sm["xa_g_k"])
    d_skip = row(sm["ssm_d"])

    h1, (w["w_in"],) = _norm_fwd(x, g_mix, "norm_mix", side=gather(["w_in"]))
    proj = _mm(h1, w["w_in"], "nn", "in_proj", tn=IN_WIDTH)
    u = _to_segments(proj[:, :SSM_WIDTH])
    q_raw, k_raw = (proj, SB_WIDTH, 1), (proj, SB_WIDTH, 2)
    v_col = (SSM_WIDTH + 2 * SB_WIDTH) // LANES
    sb_scale = SB_HEAD_DIM ** -0.5
    qs, ks = _rw(lambda qt, kt, gq, gk: ((_rms_groups(qt, gq, sb_scale), _rms_groups(kt, gk, 1.0)), ()),
                 [q_raw, k_raw], [sb_gq, sb_gk], [(SB_WIDTH, BF16)] * 2, [], "sb_qk_norm")
    early = ["ssm_w_glu", "w_out", "xa_w_q", "xa_w_kv", "xa_w_o", "w_up"]
    y_sb, got = _sb_fwd(qs, ks, proj, "sb_fwd", v_col=v_col, side=gather(early))
    w.update(zip(early, got))

    ssm_args = _states_on_lanes(sm)
    acat, bsup, csup = _ssm_mats_fwd(*ssm_args, "ssm_mats")
    (states, y0, y1), (w["w_down"],) = _ssm_fwd(u, acat, bsup, csup, d_skip, "ssm_fwd", side=gather(["w_down"]))
    z_glu, y_ssm = _mm(y1, w["ssm_w_glu"], "nn", "ssm_glu", epi=lambda r, yt: (r, yt * jax.nn.sigmoid(r)),
                       extras=(y1,), out_dtypes=(F32, F32))
    y_ssm = _from_segments(y_ssm)

    def cat_norm(a, b, ga, gb):
        return jnp.concatenate([_rms(a, ga), _rms(b, gb)], axis=1)

    ycat = _rw(lambda a, b, ga, gb: ((cat_norm(a, b, ga, gb),), ()), [y_ssm, y_sb], [g_os, g_ob],
               [(D_MODEL, BF16)], [], "norm_out")[0]

    def residual_norm_epi(r, xt, gt):
        xn = r + xt
        return xn, _rms(xn, gt)

    x1, h2 = _mm(ycat, w["w_out"], "nn", "out_proj", epi=residual_norm_epi, extras=(x,), fulls=(g_xa,),
                 out_dtypes=(F32, BF16))
    qx = _mm(h2, w["xa_w_q"], "nn", "xa_q")
    memn, kv, kn_x, vv_x = _mem_fwd(mem, g_mem, w["xa_w_kv"], xa_gk, "xa_mem")
    o_xa = _xa_fwd(qx, xa_gq, kn_x, vv_x, "xa_fwd")
    x2, h3 = _mm(o_xa, w["xa_w_o"], "nn", "xa_o", epi=residual_norm_epi, extras=(x1,), fulls=(g_mlp,),
                 out_dtypes=(F32, BF16))

    def up_epi(r):
        rl = jnp.maximum(r, 0.0)
        return (rl * rl,)

    r_up = _mm(h3, w["w_up"], "nn", "mlp_up", epi=up_epi, out_dtypes=(BF16,), tm=2048, tn=2048)

    def loss_epi(r, xt, tt):
        d = r + xt - tt
        return (d * (1.0 / D_MODEL),) * 2, (jnp.sum(d * d, axis=0, keepdims=True),)

    dx3, dx3_b, sq = _mm(r_up, w["w_down"], "nn", "mlp_down", epi=loss_epi, extras=(x2, target),
                         out_dtypes=(F32, BF16), sums=[(1, D_MODEL)])
    loss = jnp.sum(sq) * (0.5 / D_MODEL)

    def norm_bwd_epi(r, xt, drt, gt):
        _, vjp = jax.vjp(_rms, xt, gt)
        dx_, dg_ = vjp(r)
        return (dx_ + drt,) * 2, (dg_,)

    g["w_down"] = _mm(r_up, dx3_b, "tn", "d_w_down", tk=2048)
    da = _mm(dx3_b, w["w_down"], "nt", "d_r", epi=lambda r, rt: (r * 2.0 * jnp.sqrt(rt.astype(F32)),), extras=(r_up,),
             out_dtypes=(BF16,), tn=2048)
    g["w_up"] = _mm(h3, da, "tn", "d_w_up", tk=2048)
    mlp = ["w_down", "w_up"]
    (dx2, dx2_b, g["g_mlp"]), got = _mm(da, w["w_up"], "nt", "d_h3", epi=norm_bwd_epi, extras=(x2, dx3),
                                        fulls=(g_mlp,), out_dtypes=(F32, BF16), sums=[g_mlp.shape],
                                        side=to_sibling(mlp))
    add_sibling(mlp, got)
    g["xa_w_o"] = _mm(o_xa, dx2_b, "tn", "d_xa_w_o", tk=2048)
    do_xa = _mm(dx2_b, w["xa_w_o"], "nt", "d_o_xa")
    dqx, dkn_x, dvv_x, g["xa_g_q"] = _xa_bwd(qx, xa_gq, kn_x, vv_x, do_xa, "xa_bwd")
    g["xa_w_kv"], g["g_mem"], g["xa_g_k"] = _mem_bwd(mem, g_mem, memn, w["xa_w_kv"], kv, xa_gk, dkn_x, dvv_x,
                                                     "xa_mem_bwd")
    g["xa_w_q"] = _mm(h2, dqx, "tn", "d_xa_w_q", tk=2048)
    dx1, dx1_b, g["g_xa"] = _mm(dqx, w["xa_w_q"], "nt", "d_h2", epi=norm_bwd_epi, extras=(x1, dx2), fulls=(g_xa,),
                                out_dtypes=(F32, BF16), sums=[g_xa.shape])
    g["w_out"] = _mm(ycat, dx1_b, "tn", "d_w_out", tk=2048)
    dycat = _mm(dx1_b, w["w_out"], "nt", "d_ycat")

    def cat_bwd(a, b, dy, ga, gb):
        _, vjp = jax.vjp(cat_norm, a, b, ga, gb)
        da_, db_, dga, dgb = vjp(dy)
        return (da_, db_), (dga, dgb)

    dy_ssm, dy_sb, g["g_out_ssm"], g["g_out_sb"] = _rw(
        cat_bwd, [y_ssm, y_sb, dycat], [g_os, g_ob], [(SSM_WIDTH, F32), (SB_WIDTH, F32)], [g_os.shape, g_ob.shape],
        "d_norm_out", tm=512)

    def glu_bwd(dy, yt, zt):
        sg = jax.nn.sigmoid(zt)
        return (dy * sg, dy * yt * sg * (1.0 - sg)), ()

    dy1_a, dz = _rw(glu_bwd, [_to_segments(dy_ssm), y1, z_glu], [], [(SSM_WIDTH, F32), (SSM_WIDTH, BF16)], [], "d_glu")
    g["ssm_w_glu"] = _mm(y1, dz, "tn", "d_w_glu", tk=2048)

    def gelu_bwd_epi(r, da_, y0t):
        _, vjp = jax.vjp(jax.nn.gelu, y0t)
        return (vjp(r + da_)[0],)

    mid = ["w_out", "xa_w_q", "xa_w_kv", "xa_w_o", "ssm_w_glu"]
    dy0, got = _mm(dz, w["ssm_w_glu"], "nt", "d_y1", epi=gelu_bwd_epi, extras=(dy1_a, y0), side=to_sibling(mid))
    add_sibling(mid, got)
    (du, da8, d_bsup, d_csup, g["ssm_d"]), got = _ssm_bwd(dy0, states, u, acat, bsup, csup, d_skip, "ssm_bwd",
                                                          side=to_chips(mlp))
    keep(mlp, got)
    d_acat = jnp.sum(da8, axis=0, keepdims=True)
    d_mats = _ssm_mats_bwd(*ssm_args[:5], d_acat, d_bsup, d_csup, "ssm_mats_bwd")
    for nm, val in zip(("ssm_a_re", "ssm_a_im", "ssm_log_dt", "ssm_b_re", "ssm_b_im", "ssm_c_re", "ssm_c_im"),
                       _from_states_on_lanes(*d_mats)):
        g[nm] = val

    (dqs, dks, dvs), got = _sb_bwd(qs, ks, proj, y_sb, dy_sb, "sb_bwd", v_col=v_col, side=to_chips(mid))
    keep(mid, got)

    def d_proj_rows(du_t, qt, dqt, kt, dkt, dvt, gq, gk):
        _, vjp_q = jax.vjp(lambda a, b_: _rms_groups(a, b_, sb_scale), qt, gq)
        _, vjp_k = jax.vjp(lambda a, b_: _rms_groups(a, b_, 1.0), kt, gk)
        (dq_, dgq_), (dk_, dgk_) = vjp_q(dqt), vjp_k(dkt)
        rows = jnp.concatenate([du_t, dq_.astype(BF16), dk_.astype(BF16), dvt.astype(BF16)], axis=1)
        return (rows,), (dgq_, dgk_)

    dproj, dgq, dgk = _rw(d_proj_rows, [_from_segments(du), q_raw, dqs, k_raw, dks, dvs], [sb_gq, sb_gk],
                          [(IN_WIDTH, BF16)], [sb_gq.shape, sb_gk.shape], "d_proj", tm=512)
    g["sb_g_q"] = jnp.sum(dgq.reshape(SB_HEADS, SB_HEAD_DIM), axis=0)
    g["sb_g_k"] = jnp.sum(dgk.reshape(SB_HEADS, SB_HEAD_DIM), axis=0)
    g["w_in"] = _mm(h1, dproj, "tn", "d_w_in", tn=IN_WIDTH)
    dh1, got = _mm(dproj, w["w_in"], "nt", "d_h1", tk=IN_WIDTH, side=to_sibling(["w_in"]))
    add_sibling(["w_in"], got)
    dx, g["g_mix"] = _norm_bwd(x, g_mix, dh1, dx1, "d_norm_mix")

    packed = _pack_small([g[n] for n in SMALL] + [loss.reshape(1)])
    everyone, got = _all_gather(packed, "gather_small", to_chips(["w_in"]))
    keep(["w_in"], got)
    return dx, everyone, reduced


def kernel(x, mem, g_mix, w_in, ssm_a_re, ssm_a_im, ssm_log_dt, ssm_b_re, ssm_b_im, ssm_c_re, ssm_c_im, ssm_d, ssm_w_glu, sb_g_q, sb_g_k, g_out_ssm, g_out_sb, w_out, g_xa, g_mem, xa_w_q, xa_w_kv, xa_g_q, xa_g_k, xa_w_o, g_mlp, w_up, w_down, loss_target, m_g_mix, m_w_in, m_ssm_a_re, m_ssm_a_im, m_ssm_log_dt, m_ssm_b_re, m_ssm_b_im, m_ssm_c_re, m_ssm_c_im, m_ssm_d, m_ssm_w_glu, m_sb_g_q, m_sb_g_k, m_g_out_ssm, m_g_out_sb, m_w_out, m_g_xa, m_g_mem, m_xa_w_q, m_xa_w_kv, m_xa_g_q, m_xa_g_k, m_xa_w_o, m_g_mlp, m_w_up, m_w_down, v_g_mix, v_w_in, v_ssm_a_re, v_ssm_a_im, v_ssm_log_dt, v_ssm_b_re, v_ssm_b_im, v_ssm_c_re, v_ssm_c_im, v_ssm_d, v_ssm_w_glu, v_sb_g_q, v_sb_g_k, v_g_out_ssm, v_g_out_sb, v_w_out, v_g_xa, v_g_mem, v_xa_w_q, v_xa_w_kv, v_xa_g_q, v_xa_g_k, v_xa_w_o, v_g_mlp, v_w_up, v_w_down):
    given = dict(locals())
    order = ["g_mix", "w_in", "ssm_a_re", "ssm_a_im", "ssm_log_dt", "ssm_b_re", "ssm_b_im", "ssm_c_re", "ssm_c_im",
             "ssm_d", "ssm_w_glu", "sb_g_q", "sb_g_k", "g_out_ssm", "g_out_sb", "w_out", "g_xa", "g_mem", "xa_w_q",
             "xa_w_kv", "xa_g_q", "xa_g_k", "xa_w_o", "g_mlp", "w_up", "w_down"]
    assert sorted([n for n, _, _ in BIG] + SMALL) == sorted(order)
    core = lax.axis_index("c").astype(jnp.int32).reshape(1)
    chip = (2 * lax.axis_index("x") + lax.axis_index("y")).astype(jnp.int32).reshape(1)

    shards = {n: given[n][0].astype(BF16) for n, _, _ in BIG}
    sm = {n: given[n][0] for n in SMALL}
    dx, everyone, reduced = _step(x[0], mem[0], loss_target[0], shards, sm, core)

    res = {}
    for n, _, _ in BIG:
        own, recv = reduced[n]
        outs = _adam_sharded(own, recv, given[n][0], given["m_" + n][0], given["v_" + n][0], chip, "adam_" + n)
        for kind, val in zip(("grad", "delta", "new_m", "new_v"), outs):
            res[kind + "_" + n] = val[None]

    sizes = [math.prod(sm[n].shape) for n in SMALL] + [1]
    nat = lambda a: a.reshape(_natural_2d(math.prod(a.shape)))
    outs = _adam_replicated(everyone, sizes, [nat(sm[n]) for n in SMALL], [nat(given["m_" + n][0]) for n in SMALL],
                            [nat(given["v_" + n][0]) for n in SMALL], "adam_replicated")
    for i, n in enumerate(SMALL):
        for kind, val in zip(("grad", "delta", "new_m", "new_v"), outs[4 * i:4 * i + 4]):
            res[kind + "_" + n] = val.reshape(given[n].shape)
    loss_out = outs[-1][0, 0]
    return (loss_out, dx[None], *[res["grad_" + n] for n in order], *[res["delta_" + n] for n in order],
            *[res["new_m_" + n] for n in order], *[res["new_v_" + n] for n in order])
```

```python
import functools
import math

import jax
import jax.numpy as jnp
from jax import lax
from jax.experimental import pallas as pl
from jax.experimental.pallas import tpu as pltpu

F32 = jnp.float32
BF16 = jnp.bfloat16
MESH = pl.DeviceIdType.MESH

N_DEV = 8
D_MODEL = 1024
SSM_WIDTH = 512
SSM_GROUP = 16
SSM_GROUPS = 32
SSM_STATE = 64
N_STATE = SSM_GROUPS * SSM_STATE
SB_HEADS = 8
SB_HEAD_DIM = 64
SB_WIDTH = 512
IN_WIDTH = 2048
XA_HEADS = 4
XA_HEAD_DIM = 128
XA_WIDTH = 512
D_FF = 4096
NORM_EPS = 1e-6
ADAM_LR = 0.001
ADAM_B1 = 0.9
ADAM_B2 = 0.999
ADAM_EPS = 1e-08
ADAM_WD = 0.01
ADAM_STEP = 10

LANES = 128
SUBLANES = 8
VMEM_LIMIT = 56 * 1024 * 1024
SCAN_LANES = 512
SB_BLOCK = 256
SB_Q_BLOCKS = 4
SB_UNDERFLOW = -110.0

NN = (((1,), (0,)), ((), ()))
NT = (((1,), (1,)), ((), ()))
TN = (((0,), (0,)), ((), ()))


def _params(sem=None):
    return pltpu.CompilerParams(dimension_semantics=sem, vmem_limit_bytes=VMEM_LIMIT)


def _dot(a, b, dims=NN):
    return lax.dot_general(a.astype(BF16), b.astype(BF16), dims, preferred_element_type=F32)


def _rms(x, g):
    return x * lax.rsqrt(jnp.mean(x * x, axis=-1, keepdims=True) + NORM_EPS) * g


ANY = pl.BlockSpec(memory_space=pl.ANY)


class _Side:
    def __init__(self, ins, out_shapes, n_sem, make, finish=None):
        self.ins, self.out_shapes, self.n_sem, self.make = list(ins), list(out_shapes), n_sem, make
        self.finish = finish

    def sems(self):
        return [pltpu.SemaphoreType.DMA((self.n_sem,)), pltpu.SemaphoreType.DMA((self.n_sem,))]


def _hosted(body, side, n_in, n_out, grid):
    if side is None:
        return body
    ns_in, ns_out = len(side.ins), len(side.out_shapes)

    def wrapped(*refs):
        ins, refs = refs[:n_in], refs[n_in:]
        s_ins, refs = refs[:ns_in], refs[ns_in:]
        outs, refs = refs[:n_out], refs[n_out:]
        s_outs, refs = refs[:ns_out], refs[ns_out:]
        scratch, sems = refs[:-2], refs[-2:]
        ids = [pl.program_id(d) for d in range(len(grid))]
        first = functools.reduce(jnp.logical_and, [i == 0 for i in ids])
        last = functools.reduce(jnp.logical_and, [i == n - 1 for i, n in zip(ids, grid)])

        @pl.when(first)
        def _():
            for cp in side.make(s_ins, s_outs, *sems):
                cp.start()

        body(*ins, *outs, *scratch)

        @pl.when(last)
        def _():
            if side.finish is not None:
                side.finish(s_ins, s_outs, *sems)
            else:
                for cp in side.make(s_ins, s_outs, *sems):
                    cp.wait()

    return wrapped


def _side_args(side):
    if side is None:
        return [], [], [], [], []
    return ([ANY] * len(side.ins), [ANY] * len(side.out_shapes), side.out_shapes, side.sems(), side.ins)


def _split_side(res, n_out, side):
    res = list(res)
    main = res[0] if n_out == 1 else res[:n_out]
    return main if side is None else (main, res[n_out:])


def _mm(a, b, mode, name, *, epi=None, extras=(), fulls=(), out_dtypes=(F32,), sums=(), tm=1024, tn=1024, tk=1024,
        side=None):
    if mode == "nn":
        (m, k), (k2, n) = a.shape, b.shape
    elif mode == "nt":
        (m, k), (n, k2) = a.shape, b.shape
    else:
        (k, m), (k2, n) = a.shape, b.shape
    assert k == k2, (name, a.shape, b.shape)
    tm, tn, tk = min(tm, m), min(tn, n), min(tk, k)
    assert m % tm == 0 and n % tn == 0 and k % tk == 0, (name, m, n, k)
    nk = k // tk
    dims = {"nn": NN, "nt": NT, "tn": TN}[mode]
    if mode == "tn":
        a_spec = pl.BlockSpec((tk, tm), lambda i, j, kk: (kk, i))
    else:
        a_spec = pl.BlockSpec((tm, tk), lambda i, j, kk: (i, kk))
    if mode == "nt":
        b_spec = pl.BlockSpec((tn, tk), lambda i, j, kk: (j, kk))
    else:
        b_spec = pl.BlockSpec((tk, tn), lambda i, j, kk: (kk, j))
    mn_spec = pl.BlockSpec((tm, tn), lambda i, j, kk: (i, j))
    n_ex, n_full, n_out, n_sum = len(extras), len(fulls), len(out_dtypes), len(sums)
    n_in = 2 + n_ex + n_full

    def body(*refs):
        a_ref, b_ref = refs[:2]
        ex = refs[2:n_in]
        outs = refs[n_in:n_in + n_out]
        sum_refs = refs[n_in + n_out:n_in + n_out + n_sum]
        kk = pl.program_id(2)
        first_tile = jnp.logical_and(pl.program_id(0) == 0, pl.program_id(1) == 0)

        def finish(r):
            vals = epi(r, *[e[...] for e in ex]) if epi is not None else (r,)
            if n_sum:
                vals, parts = vals

                @pl.when(first_tile)
                def _():
                    for sr in sum_refs:
                        sr[...] = jnp.zeros_like(sr)

                for sr, p in zip(sum_refs, parts):
                    sr[...] += p
            for o, v in zip(outs, vals):
                o[...] = v.astype(o.dtype)

        if nk == 1:
            finish(_dot(a_ref[...], b_ref[...], dims))
        else:
            acc = refs[n_in + n_out + n_sum]

            @pl.when(kk == 0)
            def _():
                acc[...] = jnp.zeros_like(acc)

            acc[...] += _dot(a_ref[...], b_ref[...], dims)

            @pl.when(kk == nk - 1)
            def _():
                finish(acc[...])

    grid = (m // tm, n // tn, nk)
    whole = lambda shape: pl.BlockSpec(shape, lambda i, j, kk: (0,) * len(shape))
    s_in, s_out, s_shape, s_scratch, s_ops = _side_args(side)
    seq = bool(side) or n_sum > 0
    res = pl.pallas_call(
        _hosted(body, side, n_in, n_out + n_sum, grid), name=name, grid=grid,
        in_specs=[a_spec, b_spec] + [mn_spec] * n_ex + [whole(f.shape) for f in fulls] + s_in,
        out_specs=[mn_spec] * n_out + [whole(shape) for shape in sums] + s_out,
        out_shape=[jax.ShapeDtypeStruct((m, n), dt) for dt in out_dtypes]
        + [jax.ShapeDtypeStruct(shape, F32) for shape in sums] + s_shape,
        scratch_shapes=([pltpu.VMEM((tm, tn), F32)] if nk > 1 else []) + s_scratch,
        compiler_params=_params(("arbitrary",) * 3 if seq else ("parallel", "parallel", "arbitrary")),
    )(a, b, *extras, *fulls, *s_ops)
    return _split_side(res, n_out + n_sum, side)


def _row_tile(s, target):
    if s <= target:
        return s
    return max(t for t in range(16, target + 1, 16) if s % t == 0)


def _rw(fn, rows, fulls, row_out, acc_out, name, tm=1024, side=None):
    cols = [r[1:] if isinstance(r, tuple) else (r.shape[1], 0) for r in rows]
    rows = [r[0] if isinstance(r, tuple) else r for r in rows]
    s = rows[0].shape[0]
    tm = _row_tile(s, tm)
    nr, nf, nro, nao = len(rows), len(fulls), len(row_out), len(acc_out)

    def body(*refs):
        r = refs[:nr]
        f = refs[nr:nr + nf]
        ro = refs[nr + nf:nr + nf + nro]
        ao = refs[nr + nf + nro:]
        outs, accs = fn(*[x[...] for x in r], *[x[...] for x in f])
        for o, v in zip(ro, outs):
            o[...] = v.astype(o.dtype)
        if nao:
            @pl.when(pl.program_id(0) == 0)
            def _():
                for a in ao:
                    a[...] = jnp.zeros_like(a)

            for a, v in zip(ao, accs):
                a[...] += v

    full_spec = lambda shape: pl.BlockSpec(shape, lambda i: (0,) * len(shape))
    s_in, s_out, s_shape, s_scratch, s_ops = _side_args(side)
    res = pl.pallas_call(
        _hosted(body, side, nr + nf, nro + nao, (s // tm,)), name=name, grid=(s // tm,),
        in_specs=[pl.BlockSpec((tm, wd), functools.partial(lambda i, cb: (i, cb), cb=cb)) for wd, cb in cols]
        + [full_spec(x.shape) for x in fulls] + s_in,
        out_specs=[pl.BlockSpec((tm, d), lambda i: (i, 0)) for d, _ in row_out]
        + [full_spec(shape) for shape in acc_out] + s_out,
        out_shape=[jax.ShapeDtypeStruct((s, d), dt) for d, dt in row_out]
        + [jax.ShapeDtypeStruct(shape, F32) for shape in acc_out] + s_shape,
        scratch_shapes=s_scratch,
        compiler_params=_params(("arbitrary",)),
    )(*rows, *fulls, *s_ops)
    res = list(res)
    return res if side is None else (res[:nro + nao], res[nro + nao:])


def _norm_fwd(x, g, name, side=None):
    res = _rw(lambda xt, gt: ((_rms(xt, gt),), ()), [x], [g], [(x.shape[1], BF16)], [], name, side=side)
    return res[0] if side is None else (res[0][0], res[1])


def _norm_bwd(x, g, dh, dres, name, side=None):
    def fn(xt, dht, drt, gt):
        _, vjp = jax.vjp(_rms, xt, gt)
        dx, dg = vjp(dht)
        return (dx + drt,), (dg,)

    return _rw(fn, [x, dh, dres], [g], [(x.shape[1], F32)], [g.shape], name, side=side)


def _rms_groups(x, g, scale):
    lo = lax.broadcasted_iota(jnp.int32, (1, LANES), 1) < SB_HEAD_DIM
    x2 = x * x
    outs = []
    for cb in range(x.shape[1] // LANES):
        sl = slice(cb * LANES, (cb + 1) * LANES)
        s_lo = jnp.sum(jnp.where(lo, x2[:, sl], 0.0), axis=-1, keepdims=True)
        s_hi = jnp.sum(jnp.where(lo, 0.0, x2[:, sl]), axis=-1, keepdims=True)
        r = jnp.where(lo, lax.rsqrt(s_lo * (1.0 / SB_HEAD_DIM) + NORM_EPS),
                      lax.rsqrt(s_hi * (1.0 / SB_HEAD_DIM) + NORM_EPS))
        outs.append(x[:, sl] * r)
    return jnp.concatenate(outs, axis=1) * g * scale


def _log_sigmoid(z):
    return jnp.minimum(z, 0.0) - jnp.log(1.0 + jnp.exp(-jnp.abs(z)))


def _split_dot(x, u2):
    hi = x.astype(BF16)
    lo = (x - hi.astype(F32)).astype(BF16)
    return jnp.dot(jnp.concatenate([hi, lo], axis=1), u2, preferred_element_type=F32)


def _sb_consts(b):
    row = lax.broadcasted_iota(jnp.int32, (b, b), 0)
    col = lax.broadcasted_iota(jnp.int32, (b, b), 1)
    tri = col < row
    u_after = (row > col).astype(BF16)
    u_from = (row >= col).astype(BF16)
    stack = lambda u: jnp.concatenate([u, u], axis=0)
    lane_lo = lax.broadcasted_iota(jnp.int32, (b, LANES), 1) < SB_HEAD_DIM
    return tri, stack(u_after), stack(u_from), lane_lo


def _sb_scores(qh, kb, a_run, keep, u2_after, mask_l=True):
    z = lax.dot_general(qh, kb, NT, preferred_element_type=F32)
    lb = _log_sigmoid(z)
    l = lb - z
    if keep is not None and mask_l:
        l = jnp.where(keep, l, 0.0)
    w = jnp.exp(lb + (a_run + _split_dot(l, u2_after)))
    if keep is not None:
        w = jnp.where(keep, w, 0.0)
    return lb, l, w


def _sb_walk(qi, carry, step):
    def cond(state):
        n, c = state
        return jnp.logical_and(n <= qi, jnp.max(jnp.maximum(c[0], c[1])) > SB_UNDERFLOW)

    def body(state):
        n, c = state
        return n + 1, step(n, c)

    return lax.while_loop(cond, body, (jnp.int32(2), carry))[1]


def _two_heads(x, lane_lo):
    zero = jnp.zeros_like(x)
    return jnp.where(lane_lo, x, zero), jnp.where(lane_lo, zero, x)


def _sb_fwd(qs, ks, v, name, v_col=0, side=None):
    s, width = qs.shape
    b = min(SB_BLOCK, s)
    nqb = min(SB_Q_BLOCKS, s // b)

    def body(q_ref, k_ref, v_ref, o_ref):
        tri, u2_after, _, lane_lo = _sb_consts(b)
        zero = jnp.zeros((b, 1), F32)
        started = []
        for h in range(nqb):
            qi = pl.program_id(1) * nqb + h
            q_a, q_b = _two_heads(q_ref[h * b:(h + 1) * b, :], lane_lo)

            def step(n, carry, keep, mask_l=True, qi=qi, q_a=q_a, q_b=q_b):
                a_a, a_b, acc = carry
                off = pl.multiple_of(jnp.maximum(qi - n, 0) * b, b)
                kb = k_ref[pl.ds(off, b), :]
                v_a, v_b = _two_heads(v_ref[pl.ds(off, b), :].astype(BF16), lane_lo)
                _, l_a, w_a = _sb_scores(q_a, kb, a_a, keep, u2_after, mask_l)
                _, l_b, w_b = _sb_scores(q_b, kb, a_b, keep, u2_after, mask_l)
                acc = acc + jnp.dot(jnp.concatenate([w_a.astype(BF16), w_b.astype(BF16)], axis=1),
                                    jnp.concatenate([v_a, v_b], axis=0), preferred_element_type=F32)
                return (a_a + jnp.sum(l_a, axis=1, keepdims=True), a_b + jnp.sum(l_b, axis=1, keepdims=True), acc)

            carry = step(0, (zero, zero, jnp.zeros((b, LANES), F32)), tri)
            carry = step(1, carry, jnp.broadcast_to(qi > 0, tri.shape), mask_l=False)
            started.append((qi, step, carry))
        for h, (qi, step, carry) in enumerate(started):
            carry = _sb_walk(qi, carry, lambda n, c, step=step: step(n, c, None))
            o_ref[h * b:(h + 1) * b, :] = carry[2]

    blk = pl.BlockSpec((nqb * b, LANES), lambda hp, i: (i, hp))
    full = pl.BlockSpec((s, LANES), lambda hp, i: (0, hp))
    full_v = pl.BlockSpec((s, LANES), lambda hp, i: (0, hp + v_col))
    grid = (width // LANES, s // (nqb * b))
    s_in, s_out, s_shape, s_scratch, s_ops = _side_args(side)
    res = pl.pallas_call(
        _hosted(body, side, 3, 1, grid), name=name, grid=grid,
        in_specs=[blk, full, full_v] + s_in, out_specs=[blk] + s_out,
        out_shape=[jax.ShapeDtypeStruct((s, width), F32)] + s_shape, scratch_shapes=s_scratch,
        compiler_params=_params(("arbitrary", "arbitrary")),
    )(qs, ks, v, *s_ops)
    return _split_side(res, 1, side)


def _sb_bwd(qs, ks, v, out, dout, name, v_col=0, side=None):
    s, width = qs.shape
    b = min(SB_BLOCK, s)
    nqb = min(SB_Q_BLOCKS, s // b)

    def body(q_ref, k_ref, v_ref, o_ref, do_ref, dq_ref, dk_ref, dv_ref):
        @pl.when(pl.program_id(1) == 0)
        def _():
            dk_ref[...] = jnp.zeros_like(dk_ref)
            dv_ref[...] = jnp.zeros_like(dv_ref)

        tri, u2_after, u2_from, lane_lo = _sb_consts(b)
        zero = jnp.zeros((b, 1), F32)

        def head(qh, doh, kb, vb, a_run, d_rem, keep, mask_l):
            lb, l, w = _sb_scores(qh, kb, a_run, keep, u2_after, mask_l)
            wb = w.astype(BF16)
            g = lax.dot_general(doh, vb, NT, preferred_element_type=F32) * wb.astype(F32)
            g_before = d_rem - _split_dot(g, u2_from)
            dz = g - (g + g_before) * jnp.exp(lb)
            if keep is not None:
                dz = jnp.where(keep, dz, 0.0)
            return (dz.astype(BF16), wb, a_run + jnp.sum(l, axis=1, keepdims=True),
                    d_rem - jnp.sum(g, axis=1, keepdims=True))

        started = []
        for h in range(nqb):
            qi = pl.program_id(1) * nqb + h
            rows = slice(h * b, (h + 1) * b)
            q_a, q_b = _two_heads(q_ref[rows, :], lane_lo)
            dob = do_ref[rows, :].astype(BF16)
            do_a, do_b = _two_heads(dob, lane_lo)
            prod = dob.astype(F32) * o_ref[rows, :]
            d_a = jnp.sum(jnp.where(lane_lo, prod, 0.0), axis=1, keepdims=True)
            d_b = jnp.sum(jnp.where(lane_lo, 0.0, prod), axis=1, keepdims=True)
            q_rows = jnp.concatenate([q_a, q_b], axis=0)
            do_rows = jnp.concatenate([do_a, do_b], axis=0)

            def step(n, carry, keep, mask_l=True, qi=qi, q_a=q_a, q_b=q_b, do_a=do_a, do_b=do_b, q_rows=q_rows,
                     do_rows=do_rows):
                a_a, a_b, r_a, r_b, dq = carry
                off = pl.multiple_of(jnp.maximum(qi - n, 0) * b, b)
                kb = k_ref[pl.ds(off, b), :]
                vb = v_ref[pl.ds(off, b), :].astype(BF16)
                k_a, k_b = _two_heads(kb, lane_lo)
                dz_a, w_a, a_a, r_a = head(q_a, do_a, kb, vb, a_a, r_a, keep, mask_l)
                dz_b, w_b, a_b, r_b = head(q_b, do_b, kb, vb, a_b, r_b, keep, mask_l)
                dq = dq + jnp.dot(jnp.concatenate([dz_a, dz_b], axis=1), jnp.concatenate([k_a, k_b], axis=0),
                                  preferred_element_type=F32)
                dk_ref[pl.ds(off, b), :] += lax.dot_general(jnp.concatenate([dz_a, dz_b], axis=0), q_rows, TN,
                                                            preferred_element_type=F32)
                dv_ref[pl.ds(off, b), :] += lax.dot_general(jnp.concatenate([w_a, w_b], axis=0), do_rows, TN,
                                                            preferred_element_type=F32)
                return a_a, a_b, r_a, r_b, dq

            carry = step(0, (zero, zero, d_a, d_b, jnp.zeros((b, LANES), F32)), tri)
            carry = step(1, carry, jnp.broadcast_to(qi > 0, tri.shape), mask_l=False)
            started.append((qi, step, carry))
        for h, (qi, step, carry) in enumerate(started):
            carry = _sb_walk(qi, carry, lambda n, c, step=step: step(n, c, None))
            dq_ref[h * b:(h + 1) * b, :] = carry[4]

    blk = pl.BlockSpec((nqb * b, LANES), lambda hp, i: (i, hp))
    full = pl.BlockSpec((s, LANES), lambda hp, i: (0, hp))
    full_v = pl.BlockSpec((s, LANES), lambda hp, i: (0, hp + v_col))
    grid = (width // LANES, s // (nqb * b))
    s_in, s_out, s_shape, s_scratch, s_ops = _side_args(side)
    res = pl.pallas_call(
        _hosted(body, side, 5, 3, grid), name=name, grid=grid,
        in_specs=[blk, full, full_v, blk, blk] + s_in, out_specs=[blk, full, full] + s_out,
        out_shape=[jax.ShapeDtypeStruct((s, width), F32)] * 3 + s_shape,
        scratch_shapes=s_scratch,
        compiler_params=_params(("arbitrary", "arbitrary")),
    )(qs, ks, v, out, dout, *s_ops)
    return _split_side(res, 3, side)


def _cmul(xr, xi, yr, yi):
    return xr * yr - xi * yi, xr * yi + xi * yr


def _scan_consts(ar, ai, reverse, lc):
    rowi = lax.broadcasted_iota(jnp.int32, (SUBLANES, lc), 0)
    pows = [(ar, ai)]
    for _ in range(SUBLANES - 1):
        pows.append(_cmul(*pows[-1], ar, ai))
    steps = []
    for d in (1, 2, 4):
        keep = (rowi < SUBLANES - d) if reverse else (rowi >= d)
        pr, pi = pows[d - 1]
        steps.append((SUBLANES - d if reverse else d, jnp.where(keep, pr, 0.0), jnp.where(keep, pi, 0.0)))
    cr = jnp.zeros((SUBLANES, lc), F32)
    ci = jnp.zeros((SUBLANES, lc), F32)
    for r in range(SUBLANES):
        pr, pi = pows[SUBLANES - 1 - r] if reverse else pows[r]
        cr = jnp.where(rowi == r, pr, cr)
        ci = jnp.where(rowi == r, pi, ci)
    return steps, cr, ci


def _scan_tile(xr, xi, steps, pr, pi, cr, ci):
    for shift, ar, ai in steps:
        rr = pltpu.roll(xr, shift, 0)
        ri = pltpu.roll(xi, shift, 0)
        xr, xi = xr + ar * rr - ai * ri, xi + ar * ri + ai * rr
    return xr + pr * cr - pi * ci, xi + pr * ci + pi * cr


SCAN_ROWS = 1024


def _scan_chunk(s):
    tt = min(SCAN_ROWS, s)
    seg = tt // SUBLANES
    assert s % tt == 0 and seg % SUBLANES == 0 and seg & (seg - 1) == 0, s
    return tt, seg


def _to_segments(a):
    s, wd = a.shape
    tt, seg = _scan_chunk(s)
    return jnp.transpose(a.reshape(s // tt, SUBLANES, seg, wd), (0, 2, 1, 3)).reshape(s, wd)


def _from_segments(a):
    s, wd = a.shape
    tt, seg = _scan_chunk(s)
    return jnp.transpose(a.reshape(s // tt, seg, SUBLANES, wd), (0, 2, 1, 3)).reshape(s, wd)


def _cpow2(xr, xi, k):
    for _ in range(k):
        xr, xi = _cmul(xr, xi, xr, xi)
    return xr, xi


def _fill_powers(pw_ref, ar, ai, seg, lc):
    _, p8r, p8i = _scan_consts(ar, ai, False, lc)
    a8r, a8i = _cpow2(ar, ai, 3)
    qr, qi = jnp.ones_like(ar), jnp.zeros_like(ai)
    for k in range(seg // SUBLANES):
        tr, ti = _cmul(p8r, p8i, qr, qi)
        for r in range(SUBLANES):
            rows = pl.ds((SUBLANES * k + r) * SUBLANES, SUBLANES)
            pw_ref[rows, :lc] = jnp.broadcast_to(tr[r:r + 1, :], (SUBLANES, lc))
            pw_ref[rows, lc:] = jnp.broadcast_to(ti[r:r + 1, :], (SUBLANES, lc))
        qr, qi = _cmul(qr, qi, a8r, a8i)


def _ssm_fwd(u, acat, bsup, csup, d_skip, name, side=None):
    s = u.shape[0]
    lc = SCAN_LANES
    tt, seg = _scan_chunk(s)
    nl, nt = N_STATE // lc, s // tt
    tile = lambda j: pl.ds(pl.multiple_of(j * SUBLANES, SUBLANES), SUBLANES)

    def body(u_ref, a_ref, b_ref, c_ref, d_ref, s_ref, y0_ref, y1_ref, carry, pw_ref):
        ar, ai = a_ref[:, :lc], a_ref[:, lc:]

        @pl.when(pl.program_id(1) == 0)
        def _():
            carry[...] = jnp.zeros_like(carry)
            _fill_powers(pw_ref, ar, ai, seg, lc)

        ut = u_ref[...]
        s_ref[...] = _dot(ut, b_ref[0])

        ar8, ai8 = jnp.broadcast_to(ar, (SUBLANES, lc)), jnp.broadcast_to(ai, (SUBLANES, lc))

        def local(j, x):
            xr = ar8 * x[0] - ai8 * x[1] + s_ref[tile(j), :lc]
            xi = ar8 * x[1] + ai8 * x[0] + s_ref[tile(j), lc:]
            s_ref[tile(j), :lc] = xr
            s_ref[tile(j), lc:] = xi
            return xr, xi

        zero = jnp.zeros((SUBLANES, lc), F32)
        er, ei = lax.fori_loop(0, seg, local, (zero, zero), unroll=4)
        steps, pr, pi = _scan_consts(*_cpow2(ar, ai, seg.bit_length() - 1), False, lc)
        cr, ci = carry[:, :lc], carry[:, lc:]
        tr, ti = _scan_tile(er, ei, steps, pr, pi, cr, ci)
        rowi = lax.broadcasted_iota(jnp.int32, (SUBLANES, lc), 0)
        before_r = jnp.where(rowi == 0, cr, pltpu.roll(tr, 1, 0))
        before_i = jnp.where(rowi == 0, ci, pltpu.roll(ti, 1, 0))
        carry[:, :lc] = jnp.broadcast_to(tr[SUBLANES - 1:, :], (SUBLANES, lc))
        carry[:, lc:] = jnp.broadcast_to(ti[SUBLANES - 1:, :], (SUBLANES, lc))

        def fix(j, _):
            pwr, pwi = pw_ref[tile(j), :lc], pw_ref[tile(j), lc:]
            s_ref[tile(j), :lc] += pwr * before_r - pwi * before_i
            s_ref[tile(j), lc:] += pwr * before_i + pwi * before_r
            return 0

        lax.fori_loop(0, seg, fix, 0, unroll=4)
        y0 = _dot(s_ref[...], c_ref[0], NT) + d_ref[...] * ut
        y0_ref[...] = y0
        y1_ref[...] = jax.nn.gelu(y0)

    chan = pl.BlockSpec((tt, LANES), lambda j, c: (c, j))
    sup = pl.BlockSpec((1, LANES, 2 * lc), lambda j, c: (j, 0, 0))
    s_in, s_out, s_shape, s_scratch, s_ops = _side_args(side)
    res = pl.pallas_call(
        _hosted(body, side, 5, 3, (nl, nt)), name=name, grid=(nl, nt),
        in_specs=[chan, pl.BlockSpec((1, 2 * lc), lambda j, c: (0, j)), sup, sup,
                  pl.BlockSpec((1, LANES), lambda j, c: (0, j))] + s_in,
        out_specs=[pl.BlockSpec((tt, 2 * lc), lambda j, c: (c, j)), chan, chan] + s_out,
        out_shape=[jax.ShapeDtypeStruct((s, 2 * N_STATE), F32), jax.ShapeDtypeStruct((s, SSM_WIDTH), F32),
                   jax.ShapeDtypeStruct((s, SSM_WIDTH), F32)] + s_shape,
        scratch_shapes=[pltpu.VMEM((SUBLANES, 2 * lc), F32), pltpu.VMEM((seg * SUBLANES, 2 * lc), F32)] + s_scratch,
        compiler_params=_params(("arbitrary", "arbitrary")),
    )(u, acat, bsup, csup, d_skip, *s_ops)
    return _split_side(res, 3, side)


def _ssm_bwd(dy0, states, u, acat, bsup, csup, d_skip, name, side=None):
    s = u.shape[0]
    lc = SCAN_LANES
    tt, seg = _scan_chunk(s)
    nl, nt = N_STATE // lc, s // tt
    tile = lambda j: pl.ds(pl.multiple_of(j * SUBLANES, SUBLANES), SUBLANES)

    def body(dy_ref, s_ref, sp_ref, u_ref, a_ref, b_ref, c_ref, d_ref,
             du_ref, da_ref, db_ref, dc_ref, dd_ref, lam_ref, carry, pw_ref):
        c = pl.program_id(1)
        ar, ai = a_ref[:, :lc], a_ref[:, lc:]

        @pl.when(c == 0)
        def _():
            carry[...] = jnp.zeros_like(carry)
            for r in (da_ref, db_ref, dc_ref, dd_ref):
                r[...] = jnp.zeros_like(r)
            _fill_powers(pw_ref, ar, ai, seg, lc)

        dy = dy_ref[...]
        ut = u_ref[...]
        lam_ref[...] = _dot(dy, c_ref[0])

        ar8, ai8 = jnp.broadcast_to(ar, (SUBLANES, lc)), jnp.broadcast_to(ai, (SUBLANES, lc))

        def local(i, x):
            j = seg - 1 - i
            xr = ar8 * x[0] + ai8 * x[1] + lam_ref[tile(j), :lc]
            xi = ar8 * x[1] - ai8 * x[0] + lam_ref[tile(j), lc:]
            lam_ref[tile(j), :lc] = xr
            lam_ref[tile(j), lc:] = xi
            return xr, xi

        zero = jnp.zeros((SUBLANES, lc), F32)
        er, ei = lax.fori_loop(0, seg, local, (zero, zero), unroll=4)
        big_r, big_i = _cpow2(ar, ai, seg.bit_length() - 1)
        steps, pr, pi = _scan_consts(big_r, -big_i, True, lc)
        cr, ci = carry[:, :lc], carry[:, lc:]
        tr, ti = _scan_tile(er, ei, steps, pr, pi, cr, ci)
        rowi = lax.broadcasted_iota(jnp.int32, (SUBLANES, lc), 0)
        after_r = jnp.where(rowi == SUBLANES - 1, cr, pltpu.roll(tr, SUBLANES - 1, 0))
        after_i = jnp.where(rowi == SUBLANES - 1, ci, pltpu.roll(ti, SUBLANES - 1, 0))
        carry[:, :lc] = jnp.broadcast_to(tr[:1, :], (SUBLANES, lc))
        carry[:, lc:] = jnp.broadcast_to(ti[:1, :], (SUBLANES, lc))

        start = c != nt - 1
        last_r = jnp.where(start, jnp.broadcast_to(sp_ref[SUBLANES - 1:, :lc], (SUBLANES, lc)), 0.0)
        last_i = jnp.where(start, jnp.broadcast_to(sp_ref[SUBLANES - 1:, lc:], (SUBLANES, lc)), 0.0)
        first_r = jnp.where(rowi == 0, last_r, pltpu.roll(s_ref[tile(seg - 1), :lc], 1, 0))
        first_i = jnp.where(rowi == 0, last_i, pltpu.roll(s_ref[tile(seg - 1), lc:], 1, 0))

        def fix(j, acc):
            dar, dai = acc
            k = seg - 1 - j
            pwr, pwi = pw_ref[tile(k), :lc], pw_ref[tile(k), lc:]
            lr = lam_ref[tile(j), :lc] + pwr * after_r + pwi * after_i
            li = lam_ref[tile(j), lc:] + pwr * after_i - pwi * after_r
            lam_ref[tile(j), :lc] = lr
            lam_ref[tile(j), lc:] = li
            jp = jnp.maximum(j - 1, 0)
            sr = jnp.where(j > 0, s_ref[tile(jp), :lc], first_r)
            si = jnp.where(j > 0, s_ref[tile(jp), lc:], first_i)
            return dar + lr * sr + li * si, dai + li * sr - lr * si

        dar, dai = lax.fori_loop(0, seg, fix, (zero, zero), unroll=4)
        da_ref[:, :lc] += dar
        da_ref[:, lc:] += dai
        lam = lam_ref[...].astype(BF16)
        du_ref[...] = (_dot(lam, b_ref[0], NT) + d_ref[...] * dy).astype(du_ref.dtype)
        db_ref[0] += _dot(ut, lam, TN)
        dc_ref[0] += _dot(dy, s_ref[...], TN)
        dd_ref[...] += jnp.sum(dy * ut, axis=0, keepdims=True)

    rev = lambda j, c: (nt - 1 - c, j)
    chan = pl.BlockSpec((tt, LANES), rev)
    sup = pl.BlockSpec((1, LANES, 2 * lc), lambda j, c: (j, 0, 0))
    row = pl.BlockSpec((1, LANES), lambda j, c: (0, j))
    s_in, s_out, s_shape, s_scratch, s_ops = _side_args(side)
    res = pl.pallas_call(
        _hosted(body, side, 8, 5, (nl, nt)), name=name, grid=(nl, nt),
        in_specs=[chan, pl.BlockSpec((tt, 2 * lc), rev),
                  pl.BlockSpec((SUBLANES, 2 * lc), lambda j, c: (jnp.maximum((nt - 1 - c) * seg - 1, 0), j)),
                  chan, pl.BlockSpec((1, 2 * lc), lambda j, c: (0, j)), sup, sup, row] + s_in,
        out_specs=[chan, pl.BlockSpec((SUBLANES, 2 * lc), lambda j, c: (0, j)), sup, sup, row] + s_out,
        out_shape=[jax.ShapeDtypeStruct((s, SSM_WIDTH), BF16), jax.ShapeDtypeStruct((SUBLANES, 2 * N_STATE), F32),
                   jax.ShapeDtypeStruct(bsup.shape, F32), jax.ShapeDtypeStruct(csup.shape, F32),
                   jax.ShapeDtypeStruct((1, SSM_WIDTH), F32)] + s_shape,
        scratch_shapes=[pltpu.VMEM((tt, 2 * lc), F32), pltpu.VMEM((SUBLANES, 2 * lc), F32),
                        pltpu.VMEM((seg * SUBLANES, 2 * lc), F32)] + s_scratch,
        compiler_params=_params(("arbitrary", "arbitrary")),
    )(dy0, states, states, u, acat, bsup, csup, d_skip, *s_ops)
    return _split_side(res, 5, side)


def _discretise(ar, ai, ldt, br, bi):
    dt = jnp.exp(ldt)
    lr, li = ar * dt, ai * dt
    e = jnp.exp(lr)
    abar_r, abar_i = e * jnp.cos(li), e * jnp.sin(li)
    den = ar * ar + ai * ai
    coef_r = ((abar_r - 1.0) * ar + abar_i * ai) / den
    coef_i = (abar_i * ar - (abar_r - 1.0) * ai) / den
    return abar_r, abar_i, coef_r * br - coef_i * bi, coef_r * bi + coef_i * br


def _group_mask():
    shape = (LANES, SCAN_LANES)
    return (lax.broadcasted_iota(jnp.int32, shape, 0) // SSM_GROUP
            == lax.broadcasted_iota(jnp.int32, shape, 1) // SSM_STATE)


def _ssm_mats_fwd(a_re, a_im, log_dt, b_re, b_im, c_re, c_im, name):
    nl = N_STATE // SCAN_LANES
    lc = SCAN_LANES

    def body(ar, ai, ldt, br, bi, cr, ci, acat, bsup, csup):
        abar_r, abar_i, bbar_r, bbar_i = _discretise(ar[...], ai[...], ldt[...], br[...], bi[...])
        same = _group_mask()
        spread = lambda m, j: jnp.where(same, jnp.tile(m[:, j * lc:(j + 1) * lc], (LANES // SSM_GROUP, 1)), 0.0)
        c_r, c_i = cr[...], -ci[...]
        for j in range(nl):
            acat[:, 2 * j * lc:(2 * j + 1) * lc] = abar_r[:, j * lc:(j + 1) * lc]
            acat[:, (2 * j + 1) * lc:(2 * j + 2) * lc] = abar_i[:, j * lc:(j + 1) * lc]
            bsup[j, :, :lc] = spread(bbar_r, j)
            bsup[j, :, lc:] = spread(bbar_i, j)
            csup[j, :, :lc] = spread(c_r, j)
            csup[j, :, lc:] = spread(c_i, j)

    return pl.pallas_call(
        body, name=name,
        out_shape=[jax.ShapeDtypeStruct((1, 2 * N_STATE), F32), jax.ShapeDtypeStruct((nl, LANES, 2 * lc), F32),
                   jax.ShapeDtypeStruct((nl, LANES, 2 * lc), F32)],
        compiler_params=_params(),
    )(a_re, a_im, log_dt, b_re, b_im, c_re, c_im)


def _ssm_mats_bwd(a_re, a_im, log_dt, b_re, b_im, d_acat, d_bsup, d_csup, name):
    nl = N_STATE // SCAN_LANES
    lc = SCAN_LANES

    def body(ar, ai, ldt, br, bi, dac, dbs, dcs, d_ar, d_ai, d_ldt, d_br, d_bi, d_cr, d_ci):
        same = _group_mask()

        def gather(ref, j, half):
            m = jnp.where(same, ref[j, :, half * lc:(half + 1) * lc], 0.0)
            tot = m[:SSM_GROUP]
            for k in range(1, LANES // SSM_GROUP):
                tot = tot + m[k * SSM_GROUP:(k + 1) * SSM_GROUP]
            return tot

        cols = lambda ref, half: jnp.concatenate([gather(ref, j, half) for j in range(nl)], axis=1)
        d_abar_r = jnp.concatenate([dac[:, 2 * j * lc:(2 * j + 1) * lc] for j in range(nl)], axis=1)
        d_abar_i = jnp.concatenate([dac[:, (2 * j + 1) * lc:(2 * j + 2) * lc] for j in range(nl)], axis=1)
        _, vjp = jax.vjp(_discretise, ar[...], ai[...], ldt[...], br[...], bi[...])
        outs = vjp((d_abar_r, d_abar_i, cols(dbs, 0), cols(dbs, 1)))
        for ref, val in zip((d_ar, d_ai, d_ldt, d_br, d_bi), outs):
            ref[...] = val
        d_cr[...] = cols(dcs, 0)
        d_ci[...] = -cols(dcs, 1)

    row = jax.ShapeDtypeStruct((1, N_STATE), F32)
    mat = jax.ShapeDtypeStruct((SSM_GROUP, N_STATE), F32)
    return pl.pallas_call(
        body, name=name, out_shape=[row, row, row, mat, mat, mat, mat], compiler_params=_params(),
    )(a_re, a_im, log_dt, b_re, b_im, d_acat, d_bsup, d_csup)


def _states_on_lanes(sm):
    flat = lambda a: a.reshape(1, N_STATE)
    chan_b = lambda b: jnp.transpose(b, (2, 0, 1)).reshape(SSM_GROUP, N_STATE)
    chan_c = lambda c: jnp.transpose(c, (1, 0, 2)).reshape(SSM_GROUP, N_STATE)
    return (flat(sm["ssm_a_re"]), flat(sm["ssm_a_im"]), flat(jnp.repeat(sm["ssm_log_dt"], SSM_STATE)),
            chan_b(sm["ssm_b_re"]), chan_b(sm["ssm_b_im"]), chan_c(sm["ssm_c_re"]), chan_c(sm["ssm_c_im"]))


def _from_states_on_lanes(d_ar, d_ai, d_ldt, d_br, d_bi, d_cr, d_ci):
    grp = lambda a: a.reshape(SSM_GROUPS, SSM_STATE)
    back_b = lambda b: jnp.transpose(b.reshape(SSM_GROUP, SSM_GROUPS, SSM_STATE), (1, 2, 0))
    back_c = lambda c: jnp.transpose(c.reshape(SSM_GROUP, SSM_GROUPS, SSM_STATE), (1, 0, 2))
    return (grp(d_ar), grp(d_ai), jnp.sum(grp(d_ldt), axis=1), back_b(d_br), back_b(d_bi), back_c(d_cr), back_c(d_ci))


def _mem_fwd(mem, g_mem, w_kv, g_k, name):
    ml = mem.shape[0]

    def body(mem_ref, gm_ref, w_ref, gk_ref, memn_ref, kv_ref, kn_ref, vv_ref):
        memn = _rms(mem_ref[...], gm_ref[...])
        memn_ref[...] = memn.astype(BF16)
        kv = _dot(memn, w_ref[...])
        kv_ref[...] = kv
        for hh in range(XA_HEADS):
            sl = slice(hh * XA_HEAD_DIM, (hh + 1) * XA_HEAD_DIM)
            kn_ref[:, sl] = _rms(kv[:, sl], gk_ref[...]).astype(BF16)
        vv_ref[...] = kv[:, XA_WIDTH:].astype(BF16)

    return pl.pallas_call(
        body, name=name,
        out_shape=[jax.ShapeDtypeStruct((ml, D_MODEL), BF16), jax.ShapeDtypeStruct((ml, 2 * XA_WIDTH), F32),
                   jax.ShapeDtypeStruct((ml, XA_WIDTH), BF16), jax.ShapeDtypeStruct((ml, XA_WIDTH), BF16)],
        compiler_params=_params(),
    )(mem, g_mem, w_kv, g_k)


def _mem_bwd(mem, g_mem, memn, w_kv, kv, g_k, dkn, dvv, name):
    def body(mem_ref, gm_ref, memn_ref, w_ref, kv_ref, gk_ref, dkn_ref, dvv_ref, dw_ref, dgm_ref, dgk_ref):
        kv = kv_ref[...]
        dgk = jnp.zeros(dgk_ref.shape, F32)
        parts = []
        for hh in range(XA_HEADS):
            sl = slice(hh * XA_HEAD_DIM, (hh + 1) * XA_HEAD_DIM)
            _, vjp = jax.vjp(_rms, kv[:, sl], gk_ref[...])
            dk, dg = vjp(dkn_ref[:, sl])
            parts.append(dk)
            dgk = dgk + dg
        dgk_ref[...] = dgk
        dkv = jnp.concatenate(parts + [dvv_ref[...]], axis=1)
        dw_ref[...] = _dot(memn_ref[...], dkv, TN)
        dmemn = _dot(dkv, w_ref[...], NT)
        _, vjp = jax.vjp(_rms, mem_ref[...], gm_ref[...])
        dgm_ref[...] = vjp(dmemn)[1]

    return pl.pallas_call(
        body, name=name,
        out_shape=[jax.ShapeDtypeStruct((D_MODEL, 2 * XA_WIDTH), F32), jax.ShapeDtypeStruct(g_mem.shape, F32),
                   jax.ShapeDtypeStruct(g_k.shape, F32)],
        compiler_params=_params(),
    )(mem, g_mem, memn, w_kv, kv, g_k, dkn, dvv)


def _xa_head(qx_h, g_q, kn_h, vv_h):
    qn = _rms(qx_h, g_q)
    sc = _dot(qn, kn_h, NT) * (XA_HEAD_DIM ** -0.5)
    sc = sc - jnp.max(sc, axis=-1, keepdims=True)
    e = jnp.exp(sc)
    p = e / jnp.sum(e, axis=-1, keepdims=True)
    return qn, p


def _xa_fwd(qx, g_q, kn, vv, name):
    def fn(qt, gq, knt, vvt):
        outs = []
        for hh in range(XA_HEADS):
            sl = slice(hh * XA_HEAD_DIM, (hh + 1) * XA_HEAD_DIM)
            _, p = _xa_head(qt[:, sl], gq, knt[:, sl], vvt[:, sl])
            outs.append(_dot(p, vvt[:, sl]))
        return (jnp.concatenate(outs, axis=1),), ()

    return _rw(fn, [qx], [g_q, kn, vv], [(XA_WIDTH, BF16)], [], name, tm=512)[0]


def _xa_bwd(qx, g_q, kn, vv, do, name):
    def fn(qt, dot_, gq, knt, vvt):
        dqs, dks, dvs = [], [], []
        dgq = jnp.zeros_like(gq)
        for hh in range(XA_HEADS):
            sl = slice(hh * XA_HEAD_DIM, (hh + 1) * XA_HEAD_DIM)
            qn, p = _xa_head(qt[:, sl], gq, knt[:, sl], vvt[:, sl])
            doh = dot_[:, sl]
            dp = _dot(doh, vvt[:, sl], NT)
            dvs.append(_dot(p, doh, TN))
            ds = p * (dp - jnp.sum(dp * p, axis=-1, keepdims=True)) * (XA_HEAD_DIM ** -0.5)
            dqn = _dot(ds, knt[:, sl])
            dks.append(_dot(ds, qn, TN))
            _, vjp = jax.vjp(_rms, qt[:, sl], gq)
            dq, dg = vjp(dqn)
            dqs.append(dq)
            dgq = dgq + dg
        return ((jnp.concatenate(dqs, axis=1),),
                (jnp.concatenate(dks, axis=1), jnp.concatenate(dvs, axis=1), dgq))

    return _rw(fn, [qx, do], [g_q, kn, vv], [(XA_WIDTH, BF16)], [kn.shape, vv.shape, g_q.shape], name, tm=512)


BIG = [
    ("w_in", (D_MODEL, IN_WIDTH), 1), ("ssm_w_glu", (SSM_WIDTH, SSM_WIDTH), 0), ("w_out", (D_MODEL, D_MODEL), 0),
    ("xa_w_q", (D_MODEL, XA_WIDTH), 0), ("xa_w_kv", (D_MODEL, 2 * XA_WIDTH), 0), ("xa_w_o", (XA_WIDTH, D_MODEL), 1),
    ("w_up", (D_MODEL, D_FF), 1), ("w_down", (D_FF, D_MODEL), 0),
]
BIG_INDEX = {n: i for i, (n, _, _) in enumerate(BIG)}


def _shard_shape(shape, axis):
    return tuple(d // N_DEV if i == axis else d for i, d in enumerate(shape))


def _shard_of(ref, axis, d):
    n = ref.shape[axis] // N_DEV
    return ref.at[pl.ds(d * n, n), :] if axis == 0 else ref.at[:, pl.ds(d * n, n)]


def _gather_side(names, shards):
    idxs = [BIG_INDEX[n] for n in names]

    def make(ins, outs, send_sems, recv_sems):
        x, y, c = lax.axis_index("x"), lax.axis_index("y"), lax.axis_index("c")
        cps = []
        for j, i in enumerate(idxs):
            mine = _shard_of(outs[j], BIG[i][2], 4 * x + 2 * y + c)
            cps.append(pltpu.make_async_copy(ins[j], mine, send_sems.at[N_DEV * j]))
            for rel in range(1, N_DEV):
                to = tuple(1 - p if rel >> bit & 1 else p for p, bit in ((x, 2), (y, 1), (c, 0)))
                cps.append(pltpu.make_async_remote_copy(
                    src_ref=ins[j], dst_ref=mine, send_sem=send_sems.at[N_DEV * j + rel],
                    recv_sem=recv_sems.at[N_DEV * j + rel], device_id=to, device_id_type=MESH))
        return cps

    return _Side(shards, [jax.ShapeDtypeStruct(BIG[i][1], BF16) for i in idxs], N_DEV * len(idxs), make)


def _gather_two_level_side(names, shards):
    idxs = [BIG_INDEX[n] for n in names]

    def parts(ins, outs, send_sems, recv_sems):
        x, y, c = lax.axis_index("x"), lax.axis_index("y"), lax.axis_index("c")
        sibling = (x, y, 1 - c)
        chips = [(1 - x, y), (x, 1 - y), (1 - x, 1 - y)]
        mine, first, passed, arrived, from_sibling = [], [], [], [], []
        for w, i in enumerate(idxs):
            def place(dev, w=w, i=i):
                return _shard_of(outs[w], BIG[i][2], 4 * dev[0] + 2 * dev[1] + dev[2])

            def copy(k, blk, to, src=None, w=w, place=place):
                return pltpu.make_async_remote_copy(
                    src_ref=place(blk) if src is None else src, dst_ref=place(blk),
                    send_sem=send_sems.at[N_DEV * w + k], recv_sem=recv_sems.at[N_DEV * w + k], device_id=to,
                    device_id_type=MESH)

            mine.append(pltpu.make_async_copy(ins[w], place((x, y, c)), send_sems.at[N_DEV * w + 7]))
            first.append(copy(0, (x, y, c), sibling, src=ins[w]))
            first += [copy(1 + j, (x, y, c), (*chip, c), src=ins[w]) for j, chip in enumerate(chips)]
            passed += [copy(4 + j, (*chip, c), sibling) for j, chip in enumerate(chips)]
            arrived += [copy(1 + j, (*chip, c), (x, y, c)) for j, chip in enumerate(chips)]
            from_sibling.append(copy(0, sibling, (x, y, c)))
            from_sibling += [copy(4 + j, (*chip, 1 - c), (x, y, c)) for j, chip in enumerate(chips)]
        return mine, first, passed, arrived, from_sibling

    def make(ins, outs, send_sems, recv_sems):
        mine, first, _, _, _ = parts(ins, outs, send_sems, recv_sems)
        return mine + first

    def finish(ins, outs, send_sems, recv_sems):
        mine, first, passed, arrived, from_sibling = parts(ins, outs, send_sems, recv_sems)
        for got, onward in zip(arrived, passed):
            got.wait_recv()
            onward.start()
        for cp in from_sibling:
            cp.wait_recv()
        for cp in first + passed:
            cp.wait_send()
        for cp in mine:
            cp.wait()

    return _Side(shards, [jax.ShapeDtypeStruct(BIG[i][1], BF16) for i in idxs], N_DEV * len(idxs), make, finish)


def _sibling_side(names, grads):
    idxs = [BIG_INDEX[n] for n in names]

    def make(ins, outs, send_sems, recv_sems):
        x, y, c = lax.axis_index("x"), lax.axis_index("y"), lax.axis_index("c")
        return [pltpu.make_async_remote_copy(
            src_ref=_shard_of(ins[j], BIG[i][2], 2 * k + (1 - c)), dst_ref=outs[j].at[k],
            send_sem=send_sems.at[4 * j + k], recv_sem=recv_sems.at[4 * j + k], device_id=(x, y, 1 - c),
            device_id_type=MESH) for j, i in enumerate(idxs) for k in range(4)]

    shapes = [jax.ShapeDtypeStruct((4,) + _shard_shape(BIG[i][1], BIG[i][2]), F32) for i in idxs]
    return _Side(grads, shapes, 4 * len(idxs), make)


def _chips_side(parts):
    def make(ins, outs, send_sems, recv_sems):
        x, y, c = lax.axis_index("x"), lax.axis_index("y"), lax.axis_index("c")
        chips = [(1 - x, y), (x, 1 - y), (1 - x, 1 - y)]
        return [pltpu.make_async_remote_copy(
            src_ref=ins[j].at[2 * cx + cy], dst_ref=outs[j].at[r], send_sem=send_sems.at[3 * j + r],
            recv_sem=recv_sems.at[3 * j + r], device_id=(cx, cy, c), device_id_type=MESH)
            for r, (cx, cy) in enumerate(chips) for j in range(len(parts))]

    return _Side(parts, [jax.ShapeDtypeStruct((3,) + p.shape[1:], p.dtype) for p in parts], 3 * len(parts), make)


def _reduce_add(grad, recv, axis, core, name):
    rs, cs = recv.shape[1:]
    rt = _row_tile(rs, 256)
    nt = rs // rt

    def body(c_ref, g_ref, r_ref, p_ref, pb_ref):
        sm = g_ref[...] + r_ref[0]
        p_ref[0] = sm
        pb_ref[0] = sm.astype(BF16)

    if axis == 0:
        g_spec = pl.BlockSpec((rt, cs), lambda k, t, c_ref: ((2 * k + c_ref[0]) * nt + t, 0))
    else:
        g_spec = pl.BlockSpec((rt, cs), lambda k, t, c_ref: (t, 2 * k + c_ref[0]))
    slab = pl.BlockSpec((1, rt, cs), lambda k, t, c_ref: (k, t, 0))
    return pl.pallas_call(
        body, name=name,
        grid_spec=pltpu.PrefetchScalarGridSpec(num_scalar_prefetch=1, grid=(4, nt), in_specs=[g_spec, slab],
                                               out_specs=[slab, slab]),
        out_shape=[jax.ShapeDtypeStruct(recv.shape, F32), jax.ShapeDtypeStruct(recv.shape, BF16)],
        compiler_params=_params(("parallel", "parallel")),
    )(core, grad, recv)


def _all_gather(block, name, side):
    m_per, n = block.shape
    ns_in, ns_out = len(side.ins), len(side.out_shapes)

    def body(*refs):
        x_ref, s_ins, out_ref = refs[0], refs[1:1 + ns_in], refs[1 + ns_in]
        s_outs = refs[2 + ns_in:2 + ns_in + ns_out]
        send_sems, recv_sems, local_sem, s_send, s_recv = refs[2 + ns_in + ns_out:]
        others = side.make(s_ins, s_outs, s_send, s_recv)
        for cp in others:
            cp.start()
        x, y, c = lax.axis_index("x"), lax.axis_index("y"), lax.axis_index("c")
        me, sibling = (x, y, c), (x, y, 1 - c)
        chips = [(1 - x, y), (x, 1 - y), (1 - x, 1 - y)]

        def rows(px, py, pc):
            return out_ref.at[pl.ds((4 * px + 2 * py + pc) * m_per, m_per), :]

        def copy(k, blk, to, src=None):
            return pltpu.make_async_remote_copy(
                src_ref=rows(*blk) if src is None else src, dst_ref=rows(*blk),
                send_sem=send_sems.at[k], recv_sem=recv_sems.at[k], device_id=to, device_id_type=MESH)

        mine = pltpu.make_async_copy(x_ref, rows(*me), local_sem)
        mine.start()
        first = [copy(0, me, sibling, src=x_ref)]
        first += [copy(1 + j, me, (*chip, c), src=x_ref) for j, chip in enumerate(chips)]
        for cp in first:
            cp.start()
        passed = [copy(4 + j, (*chip, c), sibling) for j, chip in enumerate(chips)]
        for j, chip in enumerate(chips):
            copy(1 + j, (*chip, c), me).wait_recv()
            passed[j].start()
        copy(0, sibling, me).wait_recv()
        for j, chip in enumerate(chips):
            copy(4 + j, (*chip, 1 - c), me).wait_recv()
        for cp in first + passed:
            cp.wait_send()
        mine.wait()
        for cp in others:
            cp.wait()

    res = pl.pallas_call(
        body, name=name, in_specs=[ANY] * (1 + ns_in), out_specs=[ANY] * (1 + ns_out),
        out_shape=[jax.ShapeDtypeStruct((N_DEV * m_per, n), block.dtype)] + side.out_shapes,
        scratch_shapes=[pltpu.SemaphoreType.DMA((7,)), pltpu.SemaphoreType.DMA((7,)), pltpu.SemaphoreType.DMA]
        + side.sems(),
    )(block, *side.ins)
    return res[0], list(res[1:])


def _adam_math(w, g, m, v):
    m = ADAM_B1 * m + (1.0 - ADAM_B1) * g
    v = ADAM_B2 * v + (1.0 - ADAM_B2) * (g * g)
    m_hat = m / (1.0 - ADAM_B1 ** ADAM_STEP)
    v_hat = v / (1.0 - ADAM_B2 ** ADAM_STEP)
    delta = -ADAM_LR * (m_hat / (jnp.sqrt(v_hat) + ADAM_EPS) + ADAM_WD * w)
    return delta, m, v


def _adam_sharded(own, recv, w, m, v, chip, name):
    rs, cs = w.shape
    rt = _row_tile(rs, 256)

    def body(chip_ref, p_ref, r_ref, w_ref, m_ref, v_ref, g_out, d_out, m_out, v_out):
        g = p_ref[0] + r_ref[0].astype(F32) + r_ref[1].astype(F32) + r_ref[2].astype(F32)
        d, mn, vn = _adam_math(w_ref[...], g, m_ref[...], v_ref[...])
        g_out[...] = g
        d_out[...] = d
        m_out[...] = mn
        v_out[...] = vn

    tile = pl.BlockSpec((rt, cs), lambda t, chip_ref: (t, 0))
    return pl.pallas_call(
        body, name=name,
        grid_spec=pltpu.PrefetchScalarGridSpec(
            num_scalar_prefetch=1, grid=(rs // rt,),
            in_specs=[pl.BlockSpec((1, rt, cs), lambda t, chip_ref: (chip_ref[0], t, 0)),
                      pl.BlockSpec((3, rt, cs), lambda t, chip_ref: (0, t, 0)), tile, tile, tile],
            out_specs=[tile] * 4),
        out_shape=[jax.ShapeDtypeStruct((rs, cs), F32)] * 4,
        compiler_params=_params(("parallel",)),
    )(chip, own, recv, w, m, v)


SMALL = ["g_mix", "ssm_a_re", "ssm_a_im", "ssm_log_dt", "ssm_b_re", "ssm_b_im", "ssm_c_re", "ssm_c_im", "ssm_d",
         "sb_g_q", "sb_g_k", "g_out_ssm", "g_out_sb", "g_xa", "g_mem", "xa_g_q", "xa_g_k", "g_mlp"]
PACK_TILE = SUBLANES * LANES


def _natural_2d(n):
    return (n // LANES, LANES) if n % LANES == 0 else (1, n)


def _pack_small(arrs):
    parts = []
    for a in arrs:
        flat = a.reshape(-1)
        parts.append(jnp.pad(flat, (0, (-flat.shape[0]) % PACK_TILE)))
    return jnp.concatenate(parts).reshape(-1, LANES)


def _adam_replicated(gathered, sizes, ws, ms, vs, name):
    n_w = len(ws)
    r_dev = gathered.shape[0] // N_DEV
    offs, off = [], 0
    for n in sizes:
        offs.append(off)
        off += (n + PACK_TILE - 1) // PACK_TILE * SUBLANES
    assert off == r_dev

    def body(*refs):
        g_ref = refs[0]
        w_refs, m_refs, v_refs = refs[1:1 + n_w], refs[1 + n_w:1 + 2 * n_w], refs[1 + 2 * n_w:1 + 3 * n_w]
        outs = refs[1 + 3 * n_w:]

        def total(i, shape):
            r, cdim = shape
            acc = g_ref[pl.ds(offs[i], r), :cdim]
            for d in range(1, N_DEV):
                acc = acc + g_ref[pl.ds(d * r_dev + offs[i], r), :cdim]
            return acc

        for i in range(n_w):
            g = total(i, w_refs[i].shape)
            d, mn, vn = _adam_math(w_refs[i][...], g, m_refs[i][...], v_refs[i][...])
            for o, val in zip(outs[4 * i:4 * i + 4], (g, d, mn, vn)):
                o[...] = val
        outs[4 * n_w][...] = total(n_w, (SUBLANES, LANES))

    shapes = [w.shape for w in ws]
    return pl.pallas_call(
        body, name=name,
        out_shape=[jax.ShapeDtypeStruct(shp, F32) for shp in shapes for _ in range(4)]
        + [jax.ShapeDtypeStruct((SUBLANES, LANES), F32)],
        compiler_params=_params(),
    )(gathered, *ws, *ms, *vs)


def _step(x, mem, target, shards, sm, core):
    g, w, sums, reduced = {}, {}, {}, {}

    def gather(names):
        return _gather_two_level_side(names, [shards[n] for n in names])

    def to_sibling(names):
        return _sibling_side(names, [g[n] for n in names])

    def add_sibling(names, received):
        for n, r in zip(names, received):
            sums[n] = _reduce_add(g[n], r, BIG[BIG_INDEX[n]][2], core, "reduce_add_" + n)

    def to_chips(names):
        return _chips_side([sums[n][1] for n in names])

    def keep(names, received):
        for n, r in zip(names, received):
            reduced[n] = (sums[n][0], r)

    row = lambda a: a.reshape(1, -1)
    g_mix, g_xa, g_mlp, g_mem = row(sm["g_mix"]), row(sm["g_xa"]), row(sm["g_mlp"]), row(sm["g_mem"])
    g_os, g_ob = row(sm["g_out_ssm"]), row(sm["g_out_sb"])
    sb_gq, sb_gk = jnp.tile(row(sm["sb_g_q"]), (1, SB_HEADS)), jnp.tile(row(sm["sb_g_k"]), (1, SB_HEADS))
    xa_gq, xa_gk = row(sm["xa_g_q"]), row(sm["xa_g_k"])
    d_skip = row(sm["ssm_d"])

    h1, (w["w_in"],) = _norm_fwd(x, g_mix, "norm_mix", side=gather(["w_in"]))
    proj = _mm(h1, w["w_in"], "nn", "in_proj", tn=IN_WIDTH)
    u = _to_segments(proj[:, :SSM_WIDTH])
    q_raw, k_raw = (proj, SB_WIDTH, 1), (proj, SB_WIDTH, 2)
    v_col = (SSM_WIDTH + 2 * SB_WIDTH) // LANES
    sb_scale = SB_HEAD_DIM ** -0.5
    qs, ks = _rw(lambda qt, kt, gq, gk: ((_rms_groups(qt, gq, sb_scale), _rms_groups(kt, gk, 1.0)), ()),
                 [q_raw, k_raw], [sb_gq, sb_gk], [(SB_WIDTH, BF16)] * 2, [], "sb_qk_norm")
    early = ["ssm_w_glu", "w_out", "xa_w_q", "xa_w_kv", "xa_w_o", "w_up"]
    y_sb, got = _sb_fwd(qs, ks, proj, "sb_fwd", v_col=v_col, side=gather(early))
    w.update(zip(early, got))

    ssm_args = _states_on_lanes(sm)
    acat, bsup, csup = _ssm_mats_fwd(*ssm_args, "ssm_mats")
    (states, y0, y1), (w["w_down"],) = _ssm_fwd(u, acat, bsup, csup, d_skip, "ssm_fwd",
                                                side=_gather_side(["w_down"], [shards["w_down"]]))
    z_glu, y_ssm = _mm(y1, w["ssm_w_glu"], "nn", "ssm_glu", epi=lambda r, yt: (r, yt * jax.nn.sigmoid(r)),
                       extras=(y1,), out_dtypes=(F32, F32))
    y_ssm = _from_segments(y_ssm)

    def cat_norm(a, b, ga, gb):
        return jnp.concatenate([_rms(a, ga), _rms(b, gb)], axis=1)

    ycat = _rw(lambda a, b, ga, gb: ((cat_norm(a, b, ga, gb),), ()), [y_ssm, y_sb], [g_os, g_ob],
               [(D_MODEL, BF16)], [], "norm_out")[0]

    def residual_norm_epi(r, xt, gt):
        xn = r + xt
        return xn, _rms(xn, gt)

    x1, h2 = _mm(ycat, w["w_out"], "nn", "out_proj", epi=residual_norm_epi, extras=(x,), fulls=(g_xa,),
                 out_dtypes=(F32, BF16))
    qx = _mm(h2, w["xa_w_q"], "nn", "xa_q")
    memn, kv, kn_x, vv_x = _mem_fwd(mem, g_mem, w["xa_w_kv"], xa_gk, "xa_mem")
    o_xa = _xa_fwd(qx, xa_gq, kn_x, vv_x, "xa_fwd")
    x2, h3 = _mm(o_xa, w["xa_w_o"], "nn", "xa_o", epi=residual_norm_epi, extras=(x1,), fulls=(g_mlp,),
                 out_dtypes=(F32, BF16))

    def up_epi(r):
        rl = jnp.maximum(r, 0.0)
        return (rl * rl,)

    r_up = _mm(h3, w["w_up"], "nn", "mlp_up", epi=up_epi, out_dtypes=(BF16,), tm=2048, tn=2048)

    def loss_epi(r, xt, tt):
        d = r + xt - tt
        return (d * (1.0 / D_MODEL),) * 2, (jnp.sum(d * d, axis=0, keepdims=True),)

    dx3, dx3_b, sq = _mm(r_up, w["w_down"], "nn", "mlp_down", epi=loss_epi, extras=(x2, target),
                         out_dtypes=(F32, BF16), sums=[(1, D_MODEL)])
    loss = jnp.sum(sq) * (0.5 / D_MODEL)

    def norm_bwd_epi(r, xt, drt, gt):
        _, vjp = jax.vjp(_rms, xt, gt)
        dx_, dg_ = vjp(r)
        return (dx_ + drt,) * 2, (dg_,)

    g["w_down"] = _mm(r_up, dx3_b, "tn", "d_w_down", tk=2048)
    da = _mm(dx3_b, w["w_down"], "nt", "d_r", epi=lambda r, rt: (r * 2.0 * jnp.sqrt(rt.astype(F32)),), extras=(r_up,),
             out_dtypes=(BF16,), tn=2048)
    g["w_up"] = _mm(h3, da, "tn", "d_w_up", tk=2048)
    mlp = ["w_down", "w_up"]
    (dx2, dx2_b, g["g_mlp"]), got = _mm(da, w["w_up"], "nt", "d_h3", epi=norm_bwd_epi, extras=(x2, dx3),
                                        fulls=(g_mlp,), out_dtypes=(F32, BF16), sums=[g_mlp.shape],
                                        side=to_sibling(mlp))
    add_sibling(mlp, got)
    g["xa_w_o"] = _mm(o_xa, dx2_b, "tn", "d_xa_w_o", tk=2048)
    do_xa = _mm(dx2_b, w["xa_w_o"], "nt", "d_o_xa")
    dqx, dkn_x, dvv_x, g["xa_g_q"] = _xa_bwd(qx, xa_gq, kn_x, vv_x, do_xa, "xa_bwd")
    g["xa_w_kv"], g["g_mem"], g["xa_g_k"] = _mem_bwd(mem, g_mem, memn, w["xa_w_kv"], kv, xa_gk, dkn_x, dvv_x,
                                                     "xa_mem_bwd")
    g["xa_w_q"] = _mm(h2, dqx, "tn", "d_xa_w_q", tk=2048)
    dx1, dx1_b, g["g_xa"] = _mm(dqx, w["xa_w_q"], "nt", "d_h2", epi=norm_bwd_epi, extras=(x1, dx2), fulls=(g_xa,),
                                out_dtypes=(F32, BF16), sums=[g_xa.shape])
    g["w_out"] = _mm(ycat, dx1_b, "tn", "d_w_out", tk=2048)
    dycat = _mm(dx1_b, w["w_out"], "nt", "d_ycat")

    def cat_bwd(a, b, dy, ga, gb):
        _, vjp = jax.vjp(cat_norm, a, b, ga, gb)
        da_, db_, dga, dgb = vjp(dy)
        return (da_, db_), (dga, dgb)

    dy_ssm, dy_sb, g["g_out_ssm"], g["g_out_sb"] = _rw(
        cat_bwd, [y_ssm, y_sb, dycat], [g_os, g_ob], [(SSM_WIDTH, F32), (SB_WIDTH, F32)], [g_os.shape, g_ob.shape],
        "d_norm_out", tm=512)

    def glu_bwd(dy, yt, zt):
        sg = jax.nn.sigmoid(zt)
        return (dy * sg, dy * yt * sg * (1.0 - sg)), ()

    dy1_a, dz = _rw(glu_bwd, [_to_segments(dy_ssm), y1, z_glu], [], [(SSM_WIDTH, F32), (SSM_WIDTH, BF16)], [], "d_glu")
    g["ssm_w_glu"] = _mm(y1, dz, "tn", "d_w_glu", tk=2048)

    def gelu_bwd_epi(r, da_, y0t):
        _, vjp = jax.vjp(jax.nn.gelu, y0t)
        return (vjp(r + da_)[0],)

    mid = ["w_out", "xa_w_q", "xa_w_kv", "xa_w_o", "ssm_w_glu"]
    dy0, got = _mm(dz, w["ssm_w_glu"], "nt", "d_y1", epi=gelu_bwd_epi, extras=(dy1_a, y0), side=to_sibling(mid))
    add_sibling(mid, got)
    (du, da8, d_bsup, d_csup, g["ssm_d"]), got = _ssm_bwd(dy0, states, u, acat, bsup, csup, d_skip, "ssm_bwd",
                                                          side=to_chips(mlp))
    keep(mlp, got)
    d_acat = jnp.sum(da8, axis=0, keepdims=True)
    d_mats = _ssm_mats_bwd(*ssm_args[:5], d_acat, d_bsup, d_csup, "ssm_mats_bwd")
    for nm, val in zip(("ssm_a_re", "ssm_a_im", "ssm_log_dt", "ssm_b_re", "ssm_b_im", "ssm_c_re", "ssm_c_im"),
                       _from_states_on_lanes(*d_mats)):
        g[nm] = val

    (dqs, dks, dvs), got = _sb_bwd(qs, ks, proj, y_sb, dy_sb, "sb_bwd", v_col=v_col, side=to_chips(mid))
    keep(mid, got)

    def d_proj_rows(du_t, qt, dqt, kt, dkt, dvt, gq, gk):
        _, vjp_q = jax.vjp(lambda a, b_: _rms_groups(a, b_, sb_scale), qt, gq)
        _, vjp_k = jax.vjp(lambda a, b_: _rms_groups(a, b_, 1.0), kt, gk)
        (dq_, dgq_), (dk_, dgk_) = vjp_q(dqt), vjp_k(dkt)
        rows = jnp.concatenate([du_t, dq_.astype(BF16), dk_.astype(BF16), dvt.astype(BF16)], axis=1)
        return (rows,), (dgq_, dgk_)

    dproj, dgq, dgk = _rw(d_proj_rows, [_from_segments(du), q_raw, dqs, k_raw, dks, dvs], [sb_gq, sb_gk],
                          [(IN_WIDTH, BF16)], [sb_gq.shape, sb_gk.shape], "d_proj", tm=512)
    g["sb_g_q"] = jnp.sum(dgq.reshape(SB_HEADS, SB_HEAD_DIM), axis=0)
    g["sb_g_k"] = jnp.sum(dgk.reshape(SB_HEADS, SB_HEAD_DIM), axis=0)
    g["w_in"] = _mm(h1, dproj, "tn", "d_w_in", tn=IN_WIDTH)
    dh1, got = _mm(dproj, w["w_in"], "nt", "d_h1", tk=IN_WIDTH, side=to_sibling(["w_in"]))
    add_sibling(["w_in"], got)
    dx, g["g_mix"] = _norm_bwd(x, g_mix, dh1, dx1, "d_norm_mix")

    packed = _pack_small([g[n] for n in SMALL] + [loss.reshape(1)])
    everyone, got = _all_gather(packed, "gather_small", to_chips(["w_in"]))
    keep(["w_in"], got)
    return dx, everyone, reduced


def kernel(x, mem, g_mix, w_in, ssm_a_re, ssm_a_im, ssm_log_dt, ssm_b_re, ssm_b_im, ssm_c_re, ssm_c_im, ssm_d, ssm_w_glu, sb_g_q, sb_g_k, g_out_ssm, g_out_sb, w_out, g_xa, g_mem, xa_w_q, xa_w_kv, xa_g_q, xa_g_k, xa_w_o, g_mlp, w_up, w_down, loss_target, m_g_mix, m_w_in, m_ssm_a_re, m_ssm_a_im, m_ssm_log_dt, m_ssm_b_re, m_ssm_b_im, m_ssm_c_re, m_ssm_c_im, m_ssm_d, m_ssm_w_glu, m_sb_g_q, m_sb_g_k, m_g_out_ssm, m_g_out_sb, m_w_out, m_g_xa, m_g_mem, m_xa_w_q, m_xa_w_kv, m_xa_g_q, m_xa_g_k, m_xa_w_o, m_g_mlp, m_w_up, m_w_down, v_g_mix, v_w_in, v_ssm_a_re, v_ssm_a_im, v_ssm_log_dt, v_ssm_b_re, v_ssm_b_im, v_ssm_c_re, v_ssm_c_im, v_ssm_d, v_ssm_w_glu, v_sb_g_q, v_sb_g_k, v_g_out_ssm, v_g_out_sb, v_w_out, v_g_xa, v_g_mem, v_xa_w_q, v_xa_w_kv, v_xa_g_q, v_xa_g_k, v_xa_w_o, v_g_mlp, v_w_up, v_w_down):
    given = dict(locals())
    order = ["g_mix", "w_in", "ssm_a_re", "ssm_a_im", "ssm_log_dt", "ssm_b_re", "ssm_b_im", "ssm_c_re", "ssm_c_im",
             "ssm_d", "ssm_w_glu", "sb_g_q", "sb_g_k", "g_out_ssm", "g_out_sb", "w_out", "g_xa", "g_mem", "xa_w_q",
             "xa_w_kv", "xa_g_q", "xa_g_k", "xa_w_o", "g_mlp", "w_up", "w_down"]
    assert sorted([n for n, _, _ in BIG] + SMALL) == sorted(order)
    core = lax.axis_index("c").astype(jnp.int32).reshape(1)
    chip = (2 * lax.axis_index("x") + lax.axis_index("y")).astype(jnp.int32).reshape(1)

    shards = {n: given[n][0].astype(BF16) for n, _, _ in BIG}
    sm = {n: given[n][0] for n in SMALL}
    dx, everyone, reduced = _step(x[0], mem[0], loss_target[0], shards, sm, core)

    res = {}
    for n, _, _ in BIG:
        own, recv = reduced[n]
        outs = _adam_sharded(own, recv, given[n][0], given["m_" + n][0], given["v_" + n][0], chip, "adam_" + n)
        for kind, val in zip(("grad", "delta", "new_m", "new_v"), outs):
            res[kind + "_" + n] = val[None]

    sizes = [math.prod(sm[n].shape) for n in SMALL] + [1]
    nat = lambda a: a.reshape(_natural_2d(math.prod(a.shape)))
    outs = _adam_replicated(everyone, sizes, [nat(sm[n]) for n in SMALL], [nat(given["m_" + n][0]) for n in SMALL],
                            [nat(given["v_" + n][0]) for n in SMALL], "adam_replicated")
    for i, n in enumerate(SMALL):
        for kind, val in zip(("grad", "delta", "new_m", "new_v"), outs[4 * i:4 * i + 4]):
            res[kind + "_" + n] = val.reshape(given[n].shape)
    loss_out = outs[-1][0, 0]
    return (loss_out, dx[None], *[res["grad_" + n] for n in order], *[res["delta_" + n] for n in order],
            *[res["new_m_" + n] for n in order], *[res["new_v_" + n] for n in order])
```

```python
import functools
import math

import jax
import jax.numpy as jnp
from jax import lax
from jax.experimental import pallas as pl
from jax.experimental.pallas import tpu as pltpu

F32 = jnp.float32
BF16 = jnp.bfloat16
MESH = pl.DeviceIdType.MESH

N_DEV = 8
D_MODEL = 1024
SSM_WIDTH = 512
SSM_GROUP = 16
SSM_GROUPS = 32
SSM_STATE = 64
N_STATE = SSM_GROUPS * SSM_STATE
SB_HEADS = 8
SB_HEAD_DIM = 64
SB_WIDTH = 512
IN_WIDTH = 2048
XA_HEADS = 4
XA_HEAD_DIM = 128
XA_WIDTH = 512
D_FF = 4096
NORM_EPS = 1e-6
ADAM_LR = 0.001
ADAM_B1 = 0.9
ADAM_B2 = 0.999
ADAM_EPS = 1e-08
ADAM_WD = 0.01
ADAM_STEP = 10

LANES = 128
SUBLANES = 8
VMEM_LIMIT = 56 * 1024 * 1024
SCAN_LANES = 512
SB_BLOCK = 256
SB_Q_BLOCKS = 4
SB_UNDERFLOW = -110.0

NN = (((1,), (0,)), ((), ()))
NT = (((1,), (1,)), ((), ()))
TN = (((0,), (0,)), ((), ()))


def _params(sem=None):
    return pltpu.CompilerParams(dimension_semantics=sem, vmem_limit_bytes=VMEM_LIMIT)


def _dot(a, b, dims=NN):
    return lax.dot_general(a.astype(BF16), b.astype(BF16), dims, preferred_element_type=F32)


def _rms(x, g):
    return x * lax.rsqrt(jnp.mean(x * x, axis=-1, keepdims=True) + NORM_EPS) * g


ANY = pl.BlockSpec(memory_space=pl.ANY)


class _Side:
    def __init__(self, ins, out_shapes, n_sem, make, finish=None):
        self.ins, self.out_shapes, self.n_sem, self.make = list(ins), list(out_shapes), n_sem, make
        self.finish = finish

    def sems(self):
        return [pltpu.SemaphoreType.DMA((self.n_sem,)), pltpu.SemaphoreType.DMA((self.n_sem,))]


def _hosted(body, side, n_in, n_out, grid):
    if side is None:
        return body
    ns_in, ns_out = len(side.ins), len(side.out_shapes)

    def wrapped(*refs):
        ins, refs = refs[:n_in], refs[n_in:]
        s_ins, refs = refs[:ns_in], refs[ns_in:]
        outs, refs = refs[:n_out], refs[n_out:]
        s_outs, refs = refs[:ns_out], refs[ns_out:]
        scratch, sems = refs[:-2], refs[-2:]
        ids = [pl.program_id(d) for d in range(len(grid))]
        first = functools.reduce(jnp.logical_and, [i == 0 for i in ids])
        last = functools.reduce(jnp.logical_and, [i == n - 1 for i, n in zip(ids, grid)])

        @pl.when(first)
        def _():
            for cp in side.make(s_ins, s_outs, *sems):
                cp.start()

        body(*ins, *outs, *scratch)

        @pl.when(last)
        def _():
            if side.finish is not None:
                side.finish(s_ins, s_outs, *sems)
            else:
                for cp in side.make(s_ins, s_outs, *sems):
                    cp.wait()

    return wrapped


def _side_args(side):
    if side is None:
        return [], [], [], [], []
    return ([ANY] * len(side.ins), [ANY] * len(side.out_shapes), side.out_shapes, side.sems(), side.ins)


def _split_side(res, n_out, side):
    res = list(res)
    main = res[0] if n_out == 1 else res[:n_out]
    return main if side is None else (main, res[n_out:])


def _mm(a, b, mode, name, *, epi=None, extras=(), fulls=(), out_dtypes=(F32,), sums=(), tm=1024, tn=1024, tk=1024,
        side=None):
    if mode == "nn":
        (m, k), (k2, n) = a.shape, b.shape
    elif mode == "nt":
        (m, k), (n, k2) = a.shape, b.shape
    else:
        (k, m), (k2, n) = a.shape, b.shape
    assert k == k2, (name, a.shape, b.shape)
    tm, tn, tk = min(tm, m), min(tn, n), min(tk, k)
    assert m % tm == 0 and n % tn == 0 and k % tk == 0, (name, m, n, k)
    nk = k // tk
    dims = {"nn": NN, "nt": NT, "tn": TN}[mode]
    if mode == "tn":
        a_spec = pl.BlockSpec((tk, tm), lambda i, j, kk: (kk, i))
    else:
        a_spec = pl.BlockSpec((tm, tk), lambda i, j, kk: (i, kk))
    if mode == "nt":
        b_spec = pl.BlockSpec((tn, tk), lambda i, j, kk: (j, kk))
    else:
        b_spec = pl.BlockSpec((tk, tn), lambda i, j, kk: (kk, j))
    mn_spec = pl.BlockSpec((tm, tn), lambda i, j, kk: (i, j))
    n_ex, n_full, n_out, n_sum = len(extras), len(fulls), len(out_dtypes), len(sums)
    n_in = 2 + n_ex + n_full

    def body(*refs):
        a_ref, b_ref = refs[:2]
        ex = refs[2:n_in]
        outs = refs[n_in:n_in + n_out]
        sum_refs = refs[n_in + n_out:n_in + n_out + n_sum]
        kk = pl.program_id(2)
        first_tile = jnp.logical_and(pl.program_id(0) == 0, pl.program_id(1) == 0)

        def finish(r):
            vals = epi(r, *[e[...] for e in ex]) if epi is not None else (r,)
            if n_sum:
                vals, parts = vals

                @pl.when(first_tile)
                def _():
                    for sr in sum_refs:
                        sr[...] = jnp.zeros_like(sr)

                for sr, p in zip(sum_refs, parts):
                    sr[...] += p
            for o, v in zip(outs, vals):
                o[...] = v.astype(o.dtype)

        if nk == 1:
            finish(_dot(a_ref[...], b_ref[...], dims))
        else:
            acc = refs[n_in + n_out + n_sum]

            @pl.when(kk == 0)
            def _():
                acc[...] = jnp.zeros_like(acc)

            acc[...] += _dot(a_ref[...], b_ref[...], dims)

            @pl.when(kk == nk - 1)
            def _():
                finish(acc[...])

    grid = (m // tm, n // tn, nk)
    whole = lambda shape: pl.BlockSpec(shape, lambda i, j, kk: (0,) * len(shape))
    s_in, s_out, s_shape, s_scratch, s_ops = _side_args(side)
    seq = bool(side) or n_sum > 0
    res = pl.pallas_call(
        _hosted(body, side, n_in, n_out + n_sum, grid), name=name, grid=grid,
        in_specs=[a_spec, b_spec] + [mn_spec] * n_ex + [whole(f.shape) for f in fulls] + s_in,
        out_specs=[mn_spec] * n_out + [whole(shape) for shape in sums] + s_out,
        out_shape=[jax.ShapeDtypeStruct((m, n), dt) for dt in out_dtypes]
        + [jax.ShapeDtypeStruct(shape, F32) for shape in sums] + s_shape,
        scratch_shapes=([pltpu.VMEM((tm, tn), F32)] if nk > 1 else []) + s_scratch,
        compiler_params=_params(("arbitrary",) * 3 if seq else ("parallel", "parallel", "arbitrary")),
    )(a, b, *extras, *fulls, *s_ops)
    return _split_side(res, n_out + n_sum, side)


def _row_tile(s, target):
    if s <= target:
        return s
    return max(t for t in range(16, target + 1, 16) if s % t == 0)


def _rw(fn, rows, fulls, row_out, acc_out, name, tm=1024, side=None):
    cols = [r[1:] if isinstance(r, tuple) else (r.shape[1], 0) for r in rows]
    rows = [r[0] if isinstance(r, tuple) else r for r in rows]
    s = rows[0].shape[0]
    tm = _row_tile(s, tm)
    nr, nf, nro, nao = len(rows), len(fulls), len(row_out), len(acc_out)

    def body(*refs):
        r = refs[:nr]
        f = refs[nr:nr + nf]
        ro = refs[nr + nf:nr + nf + nro]
        ao = refs[nr + nf + nro:]
        outs, accs = fn(*[x[...] for x in r], *[x[...] for x in f])
        for o, v in zip(ro, outs):
            o[...] = v.astype(o.dtype)
        if nao:
            @pl.when(pl.program_id(0) == 0)
            def _():
                for a in ao:
                    a[...] = jnp.zeros_like(a)

            for a, v in zip(ao, accs):
                a[...] += v

    full_spec = lambda shape: pl.BlockSpec(shape, lambda i: (0,) * len(shape))
    s_in, s_out, s_shape, s_scratch, s_ops = _side_args(side)
    res = pl.pallas_call(
        _hosted(body, side, nr + nf, nro + nao, (s // tm,)), name=name, grid=(s // tm,),
        in_specs=[pl.BlockSpec((tm, wd), functools.partial(lambda i, cb: (i, cb), cb=cb)) for wd, cb in cols]
        + [full_spec(x.shape) for x in fulls] + s_in,
        out_specs=[pl.BlockSpec((tm, d), lambda i: (i, 0)) for d, _ in row_out]
        + [full_spec(shape) for shape in acc_out] + s_out,
        out_shape=[jax.ShapeDtypeStruct((s, d), dt) for d, dt in row_out]
        + [jax.ShapeDtypeStruct(shape, F32) for shape in acc_out] + s_shape,
        scratch_shapes=s_scratch,
        compiler_params=_params(("arbitrary",)),
    )(*rows, *fulls, *s_ops)
    res = list(res)
    return res if side is None else (res[:nro + nao], res[nro + nao:])


def _norm_fwd(x, g, name, side=None):
    res = _rw(lambda xt, gt: ((_rms(xt, gt),), ()), [x], [g], [(x.shape[1], BF16)], [], name, side=side)
    return res[0] if side is None else (res[0][0], res[1])


def _norm_bwd(x, g, dh, dres, name, side=None):
    def fn(xt, dht, drt, gt):
        _, vjp = jax.vjp(_rms, xt, gt)
        dx, dg = vjp(dht)
        return (dx + drt,), (dg,)

    return _rw(fn, [x, dh, dres], [g], [(x.shape[1], F32)], [g.shape], name, side=side)


def _rms_groups(x, g, scale):
    lo = lax.broadcasted_iota(jnp.int32, (1, LANES), 1) < SB_HEAD_DIM
    x2 = x * x
    outs = []
    for cb in range(x.shape[1] // LANES):
        sl = slice(cb * LANES, (cb + 1) * LANES)
        s_lo = jnp.sum(jnp.where(lo, x2[:, sl], 0.0), axis=-1, keepdims=True)
        s_hi = jnp.sum(jnp.where(lo, 0.0, x2[:, sl]), axis=-1, keepdims=True)
        r = jnp.where(lo, lax.rsqrt(s_lo * (1.0 / SB_HEAD_DIM) + NORM_EPS),
                      lax.rsqrt(s_hi * (1.0 / SB_HEAD_DIM) + NORM_EPS))
        outs.append(x[:, sl] * r)
    return jnp.concatenate(outs, axis=1) * g * scale


def _log_sigmoid(z):
    return jnp.minimum(z, 0.0) - jnp.log(1.0 + jnp.exp(-jnp.abs(z)))


def _split_dot(x, u2):
    hi = x.astype(BF16)
    lo = (x - hi.astype(F32)).astype(BF16)
    return jnp.dot(jnp.concatenate([hi, lo], axis=1), u2, preferred_element_type=F32)


def _sb_consts(b):
    row = lax.broadcasted_iota(jnp.int32, (b, b), 0)
    col = lax.broadcasted_iota(jnp.int32, (b, b), 1)
    tri = col < row
    u_after = (row > col).astype(BF16)
    u_from = (row >= col).astype(BF16)
    stack = lambda u: jnp.concatenate([u, u], axis=0)
    lane_lo = lax.broadcasted_iota(jnp.int32, (b, LANES), 1) < SB_HEAD_DIM
    return tri, stack(u_after), stack(u_from), lane_lo


def _sb_scores(qh, kb, a_run, keep, u2_after, mask_l=True):
    z = lax.dot_general(qh, kb, NT, preferred_element_type=F32)
    lb = _log_sigmoid(z)
    l = lb - z
    if keep is not None and mask_l:
        l = jnp.where(keep, l, 0.0)
    w = jnp.exp(lb + (a_run + _split_dot(l, u2_after)))
    if keep is not None:
        w = jnp.where(keep, w, 0.0)
    return lb, l, w


def _sb_walk(qi, carry, step):
    def cond(state):
        n, c = state
        return jnp.logical_and(n <= qi, jnp.max(jnp.maximum(c[0], c[1])) > SB_UNDERFLOW)

    def body(state):
        n, c = state
        return n + 1, step(n, c)

    return lax.while_loop(cond, body, (jnp.int32(2), carry))[1]


def _two_heads(x, lane_lo):
    zero = jnp.zeros_like(x)
    return jnp.where(lane_lo, x, zero), jnp.where(lane_lo, zero, x)


def _sb_fwd(qs, ks, v, name, v_col=0, side=None):
    s, width = qs.shape
    b = min(SB_BLOCK, s)
    nqb = min(SB_Q_BLOCKS, s // b)

    def body(q_ref, k_ref, v_ref, o_ref):
        tri, u2_after, _, lane_lo = _sb_consts(b)
        zero = jnp.zeros((b, 1), F32)
        started = []
        for h in range(nqb):
            qi = pl.program_id(1) * nqb + h
            q_a, q_b = _two_heads(q_ref[h * b:(h + 1) * b, :], lane_lo)

            def step(n, carry, keep, mask_l=True, qi=qi, q_a=q_a, q_b=q_b):
                a_a, a_b, acc = carry
                off = pl.multiple_of(jnp.maximum(qi - n, 0) * b, b)
                kb = k_ref[pl.ds(off, b), :]
                v_a, v_b = _two_heads(v_ref[pl.ds(off, b), :].astype(BF16), lane_lo)
                _, l_a, w_a = _sb_scores(q_a, kb, a_a, keep, u2_after, mask_l)
                _, l_b, w_b = _sb_scores(q_b, kb, a_b, keep, u2_after, mask_l)
                acc = acc + jnp.dot(jnp.concatenate([w_a.astype(BF16), w_b.astype(BF16)], axis=1),
                                    jnp.concatenate([v_a, v_b], axis=0), preferred_element_type=F32)
                return (a_a + jnp.sum(l_a, axis=1, keepdims=True), a_b + jnp.sum(l_b, axis=1, keepdims=True), acc)

            carry = step(0, (zero, zero, jnp.zeros((b, LANES), F32)), tri)
            carry = step(1, carry, jnp.broadcast_to(qi > 0, tri.shape), mask_l=False)
            started.append((qi, step, carry))
        for h, (qi, step, carry) in enumerate(started):
            carry = _sb_walk(qi, carry, lambda n, c, step=step: step(n, c, None))
            o_ref[h * b:(h + 1) * b, :] = carry[2]

    blk = pl.BlockSpec((nqb * b, LANES), lambda hp, i: (i, hp))
    full = pl.BlockSpec((s, LANES), lambda hp, i: (0, hp))
    full_v = pl.BlockSpec((s, LANES), lambda hp, i: (0, hp + v_col))
    grid = (width // LANES, s // (nqb * b))
    s_in, s_out, s_shape, s_scratch, s_ops = _side_args(side)
    res = pl.pallas_call(
        _hosted(body, side, 3, 1, grid), name=name, grid=grid,
        in_specs=[blk, full, full_v] + s_in, out_specs=[blk] + s_out,
        out_shape=[jax.ShapeDtypeStruct((s, width), F32)] + s_shape, scratch_shapes=s_scratch,
        compiler_params=_params(("arbitrary", "arbitrary")),
    )(qs, ks, v, *s_ops)
    return _split_side(res, 1, side)


def _sb_bwd(qs, ks, v, out, dout, name, v_col=0, side=None):
    s, width = qs.shape
    b = min(SB_BLOCK, s)
    nqb = min(SB_Q_BLOCKS, s // b)

    def body(q_ref, k_ref, v_ref, o_ref, do_ref, dq_ref, dk_ref, dv_ref):
        @pl.when(pl.program_id(1) == 0)
        def _():
            dk_ref[...] = jnp.zeros_like(dk_ref)
            dv_ref[...] = jnp.zeros_like(dv_ref)

        tri, u2_after, u2_from, lane_lo = _sb_consts(b)
        zero = jnp.zeros((b, 1), F32)

        def head(qh, doh, kb, vb, a_run, d_rem, keep, mask_l):
            lb, l, w = _sb_scores(qh, kb, a_run, keep, u2_after, mask_l)
            wb = w.astype(BF16)
            g = lax.dot_general(doh, vb, NT, preferred_element_type=F32) * wb.astype(F32)
            g_before = d_rem - _split_dot(g, u2_from)
            dz = g - (g + g_before) * jnp.exp(lb)
            if keep is not None:
                dz = jnp.where(keep, dz, 0.0)
            return (dz.astype(BF16), wb, a_run + jnp.sum(l, axis=1, keepdims=True),
                    d_rem - jnp.sum(g, axis=1, keepdims=True))

        started = []
        for h in range(nqb):
            qi = pl.program_id(1) * nqb + h
            rows = slice(h * b, (h + 1) * b)
            q_a, q_b = _two_heads(q_ref[rows, :], lane_lo)
            dob = do_ref[rows, :].astype(BF16)
            do_a, do_b = _two_heads(dob, lane_lo)
            prod = dob.astype(F32) * o_ref[rows, :]
            d_a = jnp.sum(jnp.where(lane_lo, prod, 0.0), axis=1, keepdims=True)
            d_b = jnp.sum(jnp.where(lane_lo, 0.0, prod), axis=1, keepdims=True)
            q_rows = jnp.concatenate([q_a, q_b], axis=0)
            do_rows = jnp.concatenate([do_a, do_b], axis=0)

            def step(n, carry, keep, mask_l=True, qi=qi, q_a=q_a, q_b=q_b, do_a=do_a, do_b=do_b, q_rows=q_rows,
                     do_rows=do_rows):
                a_a, a_b, r_a, r_b, dq = carry
                off = pl.multiple_of(jnp.maximum(qi - n, 0) * b, b)
                kb = k_ref[pl.ds(off, b), :]
                vb = v_ref[pl.ds(off, b), :].astype(BF16)
                k_a, k_b = _two_heads(kb, lane_lo)
                dz_a, w_a, a_a, r_a = head(q_a, do_a, kb, vb, a_a, r_a, keep, mask_l)
                dz_b, w_b, a_b, r_b = head(q_b, do_b, kb, vb, a_b, r_b, keep, mask_l)
                dq = dq + jnp.dot(jnp.concatenate([dz_a, dz_b], axis=1), jnp.concatenate([k_a, k_b], axis=0),
                                  preferred_element_type=F32)
                dk_ref[pl.ds(off, b), :] += lax.dot_general(jnp.concatenate([dz_a, dz_b], axis=0), q_rows, TN,
                                                            preferred_element_type=F32)
                dv_ref[pl.ds(off, b), :] += lax.dot_general(jnp.concatenate([w_a, w_b], axis=0), do_rows, TN,
                                                            preferred_element_type=F32)
                return a_a, a_b, r_a, r_b, dq

            carry = step(0, (zero, zero, d_a, d_b, jnp.zeros((b, LANES), F32)), tri)
            carry = step(1, carry, jnp.broadcast_to(qi > 0, tri.shape), mask_l=False)
            started.append((qi, step, carry))
        for h, (qi, step, carry) in enumerate(started):
            carry = _sb_walk(qi, carry, lambda n, c, step=step: step(n, c, None))
            dq_ref[h * b:(h + 1) * b, :] = carry[4]

    blk = pl.BlockSpec((nqb * b, LANES), lambda hp, i: (i, hp))
    full = pl.BlockSpec((s, LANES), lambda hp, i: (0, hp))
    full_v = pl.BlockSpec((s, LANES), lambda hp, i: (0, hp + v_col))
    grid = (width // LANES, s // (nqb * b))
    s_in, s_out, s_shape, s_scratch, s_ops = _side_args(side)
    res = pl.pallas_call(
        _hosted(body, side, 5, 3, grid), name=name, grid=grid,
        in_specs=[blk, full, full_v, blk, blk] + s_in, out_specs=[blk, full, full] + s_out,
        out_shape=[jax.ShapeDtypeStruct((s, width), F32)] * 3 + s_shape,
        scratch_shapes=s_scratch,
        compiler_params=_params(("arbitrary", "arbitrary")),
    )(qs, ks, v, out, dout, *s_ops)
    return _split_side(res, 3, side)


def _cmul(xr, xi, yr, yi):
    return xr * yr - xi * yi, xr * yi + xi * yr


def _scan_consts(ar, ai, reverse, lc):
    rowi = lax.broadcasted_iota(jnp.int32, (SUBLANES, lc), 0)
    pows = [(ar, ai)]
    for _ in range(SUBLANES - 1):
        pows.append(_cmul(*pows[-1], ar, ai))
    steps = []
    for d in (1, 2, 4):
        keep = (rowi < SUBLANES - d) if reverse else (rowi >= d)
        pr, pi = pows[d - 1]
        steps.append((SUBLANES - d if reverse else d, jnp.where(keep, pr, 0.0), jnp.where(keep, pi, 0.0)))
    cr = jnp.zeros((SUBLANES, lc), F32)
    ci = jnp.zeros((SUBLANES, lc), F32)
    for r in range(SUBLANES):
        pr, pi = pows[SUBLANES - 1 - r] if reverse else pows[r]
        cr = jnp.where(rowi == r, pr, cr)
        ci = jnp.where(rowi == r, pi, ci)
    return steps, cr, ci


def _scan_tile(xr, xi, steps, pr, pi, cr, ci):
    for shift, ar, ai in steps:
        rr = pltpu.roll(xr, shift, 0)
        ri = pltpu.roll(xi, shift, 0)
        xr, xi = xr + ar * rr - ai * ri, xi + ar * ri + ai * rr
    return xr + pr * cr - pi * ci, xi + pr * ci + pi * cr


SCAN_ROWS = 2048


def _scan_chunk(s):
    tt = min(SCAN_ROWS, s)
    seg = tt // SUBLANES
    assert s % tt == 0 and seg % SUBLANES == 0 and seg & (seg - 1) == 0, s
    return tt, seg


def _to_segments(a):
    s, wd = a.shape
    tt, seg = _scan_chunk(s)
    return jnp.transpose(a.reshape(s // tt, SUBLANES, seg, wd), (0, 2, 1, 3)).reshape(s, wd)


def _from_segments(a):
    s, wd = a.shape
    tt, seg = _scan_chunk(s)
    return jnp.transpose(a.reshape(s // tt, seg, SUBLANES, wd), (0, 2, 1, 3)).reshape(s, wd)


def _cpow2(xr, xi, k):
    for _ in range(k):
        xr, xi = _cmul(xr, xi, xr, xi)
    return xr, xi


def _fill_powers(pw_ref, ar, ai, seg, lc):
    _, p8r, p8i = _scan_consts(ar, ai, False, lc)
    a8r, a8i = _cpow2(ar, ai, 3)
    qr, qi = jnp.ones_like(ar), jnp.zeros_like(ai)
    for k in range(seg // SUBLANES):
        tr, ti = _cmul(p8r, p8i, qr, qi)
        for r in range(SUBLANES):
            rows = pl.ds((SUBLANES * k + r) * SUBLANES, SUBLANES)
            pw_ref[rows, :lc] = jnp.broadcast_to(tr[r:r + 1, :], (SUBLANES, lc))
            pw_ref[rows, lc:] = jnp.broadcast_to(ti[r:r + 1, :], (SUBLANES, lc))
        qr, qi = _cmul(qr, qi, a8r, a8i)


def _ssm_fwd(u, acat, bsup, csup, d_skip, name, side=None):
    s = u.shape[0]
    lc = SCAN_LANES
    tt, seg = _scan_chunk(s)
    nl, nt = N_STATE // lc, s // tt
    tile = lambda j: pl.ds(pl.multiple_of(j * SUBLANES, SUBLANES), SUBLANES)

    def body(u_ref, a_ref, b_ref, c_ref, d_ref, s_ref, y0_ref, y1_ref, carry, pw_ref):
        ar, ai = a_ref[:, :lc], a_ref[:, lc:]

        @pl.when(pl.program_id(1) == 0)
        def _():
            carry[...] = jnp.zeros_like(carry)
            _fill_powers(pw_ref, ar, ai, seg, lc)

        ut = u_ref[...]
        s_ref[...] = _dot(ut, b_ref[0])

        ar8, ai8 = jnp.broadcast_to(ar, (SUBLANES, lc)), jnp.broadcast_to(ai, (SUBLANES, lc))

        def local(j, x):
            xr = ar8 * x[0] - ai8 * x[1] + s_ref[tile(j), :lc]
            xi = ar8 * x[1] + ai8 * x[0] + s_ref[tile(j), lc:]
            s_ref[tile(j), :lc] = xr
            s_ref[tile(j), lc:] = xi
            return xr, xi

        zero = jnp.zeros((SUBLANES, lc), F32)
        er, ei = lax.fori_loop(0, seg, local, (zero, zero), unroll=4)
        steps, pr, pi = _scan_consts(*_cpow2(ar, ai, seg.bit_length() - 1), False, lc)
        cr, ci = carry[:, :lc], carry[:, lc:]
        tr, ti = _scan_tile(er, ei, steps, pr, pi, cr, ci)
        rowi = lax.broadcasted_iota(jnp.int32, (SUBLANES, lc), 0)
        before_r = jnp.where(rowi == 0, cr, pltpu.roll(tr, 1, 0))
        before_i = jnp.where(rowi == 0, ci, pltpu.roll(ti, 1, 0))
        carry[:, :lc] = jnp.broadcast_to(tr[SUBLANES - 1:, :], (SUBLANES, lc))
        carry[:, lc:] = jnp.broadcast_to(ti[SUBLANES - 1:, :], (SUBLANES, lc))

        def fix(j, _):
            pwr, pwi = pw_ref[tile(j), :lc], pw_ref[tile(j), lc:]
            s_ref[tile(j), :lc] += pwr * before_r - pwi * before_i
            s_ref[tile(j), lc:] += pwr * before_i + pwi * before_r
            return 0

        lax.fori_loop(0, seg, fix, 0, unroll=4)
        y0 = _dot(s_ref[...], c_ref[0], NT) + d_ref[...] * ut
        y0_ref[...] = y0
        y1_ref[...] = jax.nn.gelu(y0)

    chan = pl.BlockSpec((tt, LANES), lambda j, c: (c, j))
    sup = pl.BlockSpec((1, LANES, 2 * lc), lambda j, c: (j, 0, 0))
    s_in, s_out, s_shape, s_scratch, s_ops = _side_args(side)
    res = pl.pallas_call(
        _hosted(body, side, 5, 3, (nl, nt)), name=name, grid=(nl, nt),
        in_specs=[chan, pl.BlockSpec((1, 2 * lc), lambda j, c: (0, j)), sup, sup,
                  pl.BlockSpec((1, LANES), lambda j, c: (0, j))] + s_in,
        out_specs=[pl.BlockSpec((tt, 2 * lc), lambda j, c: (c, j)), chan, chan] + s_out,
        out_shape=[jax.ShapeDtypeStruct((s, 2 * N_STATE), F32), jax.ShapeDtypeStruct((s, SSM_WIDTH), F32),
                   jax.ShapeDtypeStruct((s, SSM_WIDTH), F32)] + s_shape,
        scratch_shapes=[pltpu.VMEM((SUBLANES, 2 * lc), F32), pltpu.VMEM((seg * SUBLANES, 2 * lc), F32)] + s_scratch,
        compiler_params=_params(("arbitrary", "arbitrary")),
    )(u, acat, bsup, csup, d_skip, *s_ops)
    return _split_side(res, 3, side)


def _ssm_bwd(dy0, states, u, acat, bsup, csup, d_skip, name, side=None):
    s = u.shape[0]
    lc = SCAN_LANES
    tt, seg = _scan_chunk(s)
    nl, nt = N_STATE // lc, s // tt
    tile = lambda j: pl.ds(pl.multiple_of(j * SUBLANES, SUBLANES), SUBLANES)

    def body(dy_ref, s_ref, sp_ref, u_ref, a_ref, b_ref, c_ref, d_ref,
             du_ref, da_ref, db_ref, dc_ref, dd_ref, lam_ref, carry, pw_ref):
        c = pl.program_id(1)
        ar, ai = a_ref[:, :lc], a_ref[:, lc:]

        @pl.when(c == 0)
        def _():
            carry[...] = jnp.zeros_like(carry)
            for r in (da_ref, db_ref, dc_ref, dd_ref):
                r[...] = jnp.zeros_like(r)
            _fill_powers(pw_ref, ar, ai, seg, lc)

        dy = dy_ref[...]
        ut = u_ref[...]
        lam_ref[...] = _dot(dy, c_ref[0])

        ar8, ai8 = jnp.broadcast_to(ar, (SUBLANES, lc)), jnp.broadcast_to(ai, (SUBLANES, lc))

        def local(i, x):
            j = seg - 1 - i
            xr = ar8 * x[0] + ai8 * x[1] + lam_ref[tile(j), :lc]
            xi = ar8 * x[1] - ai8 * x[0] + lam_ref[tile(j), lc:]
            lam_ref[tile(j), :lc] = xr
            lam_ref[tile(j), lc:] = xi
            return xr, xi

        zero = jnp.zeros((SUBLANES, lc), F32)
        er, ei = lax.fori_loop(0, seg, local, (zero, zero), unroll=4)
        big_r, big_i = _cpow2(ar, ai, seg.bit_length() - 1)
        steps, pr, pi = _scan_consts(big_r, -big_i, True, lc)
        cr, ci = carry[:, :lc], carry[:, lc:]
        tr, ti = _scan_tile(er, ei, steps, pr, pi, cr, ci)
        rowi = lax.broadcasted_iota(jnp.int32, (SUBLANES, lc), 0)
        after_r = jnp.where(rowi == SUBLANES - 1, cr, pltpu.roll(tr, SUBLANES - 1, 0))
        after_i = jnp.where(rowi == SUBLANES - 1, ci, pltpu.roll(ti, SUBLANES - 1, 0))
        carry[:, :lc] = jnp.broadcast_to(tr[:1, :], (SUBLANES, lc))
        carry[:, lc:] = jnp.broadcast_to(ti[:1, :], (SUBLANES, lc))

        start = c != nt - 1
        last_r = jnp.where(start, jnp.broadcast_to(sp_ref[SUBLANES - 1:, :lc], (SUBLANES, lc)), 0.0)
        last_i = jnp.where(start, jnp.broadcast_to(sp_ref[SUBLANES - 1:, lc:], (SUBLANES, lc)), 0.0)
        first_r = jnp.where(rowi == 0, last_r, pltpu.roll(s_ref[tile(seg - 1), :lc], 1, 0))
        first_i = jnp.where(rowi == 0, last_i, pltpu.roll(s_ref[tile(seg - 1), lc:], 1, 0))

        def fix(j, acc):
            dar, dai = acc
            k = seg - 1 - j
            pwr, pwi = pw_ref[tile(k), :lc], pw_ref[tile(k), lc:]
            lr = lam_ref[tile(j), :lc] + pwr * after_r + pwi * after_i
            li = lam_ref[tile(j), lc:] + pwr * after_i - pwi * after_r
            lam_ref[tile(j), :lc] = lr
            lam_ref[tile(j), lc:] = li
            jp = jnp.maximum(j - 1, 0)
            sr = jnp.where(j > 0, s_ref[tile(jp), :lc], first_r)
            si = jnp.where(j > 0, s_ref[tile(jp), lc:], first_i)
            return dar + lr * sr + li * si, dai + li * sr - lr * si

        dar, dai = lax.fori_loop(0, seg, fix, (zero, zero), unroll=4)
        da_ref[:, :lc] += dar
        da_ref[:, lc:] += dai
        lam = lam_ref[...].astype(BF16)
        du_ref[...] = (_dot(lam, b_ref[0], NT) + d_ref[...] * dy).astype(du_ref.dtype)
        db_ref[0] += _dot(ut, lam, TN)
        dc_ref[0] += _dot(dy, s_ref[...], TN)
        dd_ref[...] += jnp.sum(dy * ut, axis=0, keepdims=True)

    rev = lambda j, c: (nt - 1 - c, j)
    chan = pl.BlockSpec((tt, LANES), rev)
    sup = pl.BlockSpec((1, LANES, 2 * lc), lambda j, c: (j, 0, 0))
    row = pl.BlockSpec((1, LANES), lambda j, c: (0, j))
    s_in, s_out, s_shape, s_scratch, s_ops = _side_args(side)
    res = pl.pallas_call(
        _hosted(body, side, 8, 5, (nl, nt)), name=name, grid=(nl, nt),
        in_specs=[chan, pl.BlockSpec((tt, 2 * lc), rev),
                  pl.BlockSpec((SUBLANES, 2 * lc), lambda j, c: (jnp.maximum((nt - 1 - c) * seg - 1, 0), j)),
                  chan, pl.BlockSpec((1, 2 * lc), lambda j, c: (0, j)), sup, sup, row] + s_in,
        out_specs=[chan, pl.BlockSpec((SUBLANES, 2 * lc), lambda j, c: (0, j)), sup, sup, row] + s_out,
        out_shape=[jax.ShapeDtypeStruct((s, SSM_WIDTH), BF16), jax.ShapeDtypeStruct((SUBLANES, 2 * N_STATE), F32),
                   jax.ShapeDtypeStruct(bsup.shape, F32), jax.ShapeDtypeStruct(csup.shape, F32),
                   jax.ShapeDtypeStruct((1, SSM_WIDTH), F32)] + s_shape,
        scratch_shapes=[pltpu.VMEM((tt, 2 * lc), F32), pltpu.VMEM((SUBLANES, 2 * lc), F32),
                        pltpu.VMEM((seg * SUBLANES, 2 * lc), F32)] + s_scratch,
        compiler_params=_params(("arbitrary", "arbitrary")),
    )(dy0, states, states, u, acat, bsup, csup, d_skip, *s_ops)
    return _split_side(res, 5, side)


def _discretise(ar, ai, ldt, br, bi):
    dt = jnp.exp(ldt)
    lr, li = ar * dt, ai * dt
    e = jnp.exp(lr)
    abar_r, abar_i = e * jnp.cos(li), e * jnp.sin(li)
    den = ar * ar + ai * ai
    coef_r = ((abar_r - 1.0) * ar + abar_i * ai) / den
    coef_i = (abar_i * ar - (abar_r - 1.0) * ai) / den
    return abar_r, abar_i, coef_r * br - coef_i * bi, coef_r * bi + coef_i * br


def _group_mask():
    shape = (LANES, SCAN_LANES)
    return (lax.broadcasted_iota(jnp.int32, shape, 0) // SSM_GROUP
            == lax.broadcasted_iota(jnp.int32, shape, 1) // SSM_STATE)


def _ssm_mats_fwd(a_re, a_im, log_dt, b_re, b_im, c_re, c_im, name):
    nl = N_STATE // SCAN_LANES
    lc = SCAN_LANES

    def body(ar, ai, ldt, br, bi, cr, ci, acat, bsup, csup):
        abar_r, abar_i, bbar_r, bbar_i = _discretise(ar[...], ai[...], ldt[...], br[...], bi[...])
        same = _group_mask()
        spread = lambda m, j: jnp.where(same, jnp.tile(m[:, j * lc:(j + 1) * lc], (LANES // SSM_GROUP, 1)), 0.0)
        c_r, c_i = cr[...], -ci[...]
        for j in range(nl):
            acat[:, 2 * j * lc:(2 * j + 1) * lc] = abar_r[:, j * lc:(j + 1) * lc]
            acat[:, (2 * j + 1) * lc:(2 * j + 2) * lc] = abar_i[:, j * lc:(j + 1) * lc]
            bsup[j, :, :lc] = spread(bbar_r, j)
            bsup[j, :, lc:] = spread(bbar_i, j)
            csup[j, :, :lc] = spread(c_r, j)
            csup[j, :, lc:] = spread(c_i, j)

    return pl.pallas_call(
        body, name=name,
        out_shape=[jax.ShapeDtypeStruct((1, 2 * N_STATE), F32), jax.ShapeDtypeStruct((nl, LANES, 2 * lc), F32),
                   jax.ShapeDtypeStruct((nl, LANES, 2 * lc), F32)],
        compiler_params=_params(),
    )(a_re, a_im, log_dt, b_re, b_im, c_re, c_im)


def _ssm_mats_bwd(a_re, a_im, log_dt, b_re, b_im, d_acat, d_bsup, d_csup, name):
    nl = N_STATE // SCAN_LANES
    lc = SCAN_LANES

    def body(ar, ai, ldt, br, bi, dac, dbs, dcs, d_ar, d_ai, d_ldt, d_br, d_bi, d_cr, d_ci):
        same = _group_mask()

        def gather(ref, j, half):
            m = jnp.where(same, ref[j, :, half * lc:(half + 1) * lc], 0.0)
            tot = m[:SSM_GROUP]
            for k in range(1, LANES // SSM_GROUP):
                tot = tot + m[k * SSM_GROUP:(k + 1) * SSM_GROUP]
            return tot

        cols = lambda ref, half: jnp.concatenate([gather(ref, j, half) for j in range(nl)], axis=1)
        d_abar_r = jnp.concatenate([dac[:, 2 * j * lc:(2 * j + 1) * lc] for j in range(nl)], axis=1)
        d_abar_i = jnp.concatenate([dac[:, (2 * j + 1) * lc:(2 * j + 2) * lc] for j in range(nl)], axis=1)
        _, vjp = jax.vjp(_discretise, ar[...], ai[...], ldt[...], br[...], bi[...])
        outs = vjp((d_abar_r, d_abar_i, cols(dbs, 0), cols(dbs, 1)))
        for ref, val in zip((d_ar, d_ai, d_ldt, d_br, d_bi), outs):
            ref[...] = val
        d_cr[...] = cols(dcs, 0)
        d_ci[...] = -cols(dcs, 1)

    row = jax.ShapeDtypeStruct((1, N_STATE), F32)
    mat = jax.ShapeDtypeStruct((SSM_GROUP, N_STATE), F32)
    return pl.pallas_call(
        body, name=name, out_shape=[row, row, row, mat, mat, mat, mat], compiler_params=_params(),
    )(a_re, a_im, log_dt, b_re, b_im, d_acat, d_bsup, d_csup)


def _states_on_lanes(sm):
    flat = lambda a: a.reshape(1, N_STATE)
    chan_b = lambda b: jnp.transpose(b, (2, 0, 1)).reshape(SSM_GROUP, N_STATE)
    chan_c = lambda c: jnp.transpose(c, (1, 0, 2)).reshape(SSM_GROUP, N_STATE)
    return (flat(sm["ssm_a_re"]), flat(sm["ssm_a_im"]), flat(jnp.repeat(sm["ssm_log_dt"], SSM_STATE)),
            chan_b(sm["ssm_b_re"]), chan_b(sm["ssm_b_im"]), chan_c(sm["ssm_c_re"]), chan_c(sm["ssm_c_im"]))


def _from_states_on_lanes(d_ar, d_ai, d_ldt, d_br, d_bi, d_cr, d_ci):
    grp = lambda a: a.reshape(SSM_GROUPS, SSM_STATE)
    back_b = lambda b: jnp.transpose(b.reshape(SSM_GROUP, SSM_GROUPS, SSM_STATE), (1, 2, 0))
    back_c = lambda c: jnp.transpose(c.reshape(SSM_GROUP, SSM_GROUPS, SSM_STATE), (1, 0, 2))
    return (grp(d_ar), grp(d_ai), jnp.sum(grp(d_ldt), axis=1), back_b(d_br), back_b(d_bi), back_c(d_cr), back_c(d_ci))


def _mem_fwd(mem, g_mem, w_kv, g_k, name):
    ml = mem.shape[0]

    def body(mem_ref, gm_ref, w_ref, gk_ref, memn_ref, kv_ref, kn_ref, vv_ref):
        memn = _rms(mem_ref[...], gm_ref[...])
        memn_ref[...] = memn.astype(BF16)
        kv = _dot(memn, w_ref[...])
        kv_ref[...] = kv
        for hh in range(XA_HEADS):
            sl = slice(hh * XA_HEAD_DIM, (hh + 1) * XA_HEAD_DIM)
            kn_ref[:, sl] = _rms(kv[:, sl], gk_ref[...]).astype(BF16)
        vv_ref[...] = kv[:, XA_WIDTH:].astype(BF16)

    return pl.pallas_call(
        body, name=name,
        out_shape=[jax.ShapeDtypeStruct((ml, D_MODEL), BF16), jax.ShapeDtypeStruct((ml, 2 * XA_WIDTH), F32),
                   jax.ShapeDtypeStruct((ml, XA_WIDTH), BF16), jax.ShapeDtypeStruct((ml, XA_WIDTH), BF16)],
        compiler_params=_params(),
    )(mem, g_mem, w_kv, g_k)


def _mem_bwd(mem, g_mem, memn, w_kv, kv, g_k, dkn, dvv, name):
    def body(mem_ref, gm_ref, memn_ref, w_ref, kv_ref, gk_ref, dkn_ref, dvv_ref, dw_ref, dgm_ref, dgk_ref):
        kv = kv_ref[...]
        dgk = jnp.zeros(dgk_ref.shape, F32)
        parts = []
        for hh in range(XA_HEADS):
            sl = slice(hh * XA_HEAD_DIM, (hh + 1) * XA_HEAD_DIM)
            _, vjp = jax.vjp(_rms, kv[:, sl], gk_ref[...])
            dk, dg = vjp(dkn_ref[:, sl])
            parts.append(dk)
            dgk = dgk + dg
        dgk_ref[...] = dgk
        dkv = jnp.concatenate(parts + [dvv_ref[...]], axis=1)
        dw_ref[...] = _dot(memn_ref[...], dkv, TN)
        dmemn = _dot(dkv, w_ref[...], NT)
        _, vjp = jax.vjp(_rms, mem_ref[...], gm_ref[...])
        dgm_ref[...] = vjp(dmemn)[1]

    return pl.pallas_call(
        body, name=name,
        out_shape=[jax.ShapeDtypeStruct((D_MODEL, 2 * XA_WIDTH), F32), jax.ShapeDtypeStruct(g_mem.shape, F32),
                   jax.ShapeDtypeStruct(g_k.shape, F32)],
        compiler_params=_params(),
    )(mem, g_mem, memn, w_kv, kv, g_k, dkn, dvv)


def _xa_head(qx_h, g_q, kn_h, vv_h):
    qn = _rms(qx_h, g_q)
    sc = _dot(qn, kn_h, NT) * (XA_HEAD_DIM ** -0.5)
    sc = sc - jnp.max(sc, axis=-1, keepdims=True)
    e = jnp.exp(sc)
    p = e / jnp.sum(e, axis=-1, keepdims=True)
    return qn, p


def _xa_fwd(qx, g_q, kn, vv, name):
    def fn(qt, gq, knt, vvt):
        outs = []
        for hh in range(XA_HEADS):
            sl = slice(hh * XA_HEAD_DIM, (hh + 1) * XA_HEAD_DIM)
            _, p = _xa_head(qt[:, sl], gq, knt[:, sl], vvt[:, sl])
            outs.append(_dot(p, vvt[:, sl]))
        return (jnp.concatenate(outs, axis=1),), ()

    return _rw(fn, [qx], [g_q, kn, vv], [(XA_WIDTH, BF16)], [], name, tm=512)[0]


def _xa_bwd(qx, g_q, kn, vv, do, name):
    def fn(qt, dot_, gq, knt, vvt):
        dqs, dks, dvs = [], [], []
        dgq = jnp.zeros_like(gq)
        for hh in range(XA_HEADS):
            sl = slice(hh * XA_HEAD_DIM, (hh + 1) * XA_HEAD_DIM)
            qn, p = _xa_head(qt[:, sl], gq, knt[:, sl], vvt[:, sl])
            doh = dot_[:, sl]
            dp = _dot(doh, vvt[:, sl], NT)
            dvs.append(_dot(p, doh, TN))
            ds = p * (dp - jnp.sum(dp * p, axis=-1, keepdims=True)) * (XA_HEAD_DIM ** -0.5)
            dqn = _dot(ds, knt[:, sl])
            dks.append(_dot(ds, qn, TN))
            _, vjp = jax.vjp(_rms, qt[:, sl], gq)
            dq, dg = vjp(dqn)
            dqs.append(dq)
            dgq = dgq + dg
        return ((jnp.concatenate(dqs, axis=1),),
                (jnp.concatenate(dks, axis=1), jnp.concatenate(dvs, axis=1), dgq))

    return _rw(fn, [qx, do], [g_q, kn, vv], [(XA_WIDTH, BF16)], [kn.shape, vv.shape, g_q.shape], name, tm=512)


BIG = [
    ("w_in", (D_MODEL, IN_WIDTH), 1), ("ssm_w_glu", (SSM_WIDTH, SSM_WIDTH), 0), ("w_out", (D_MODEL, D_MODEL), 0),
    ("xa_w_q", (D_MODEL, XA_WIDTH), 0), ("xa_w_kv", (D_MODEL, 2 * XA_WIDTH), 0), ("xa_w_o", (XA_WIDTH, D_MODEL), 1),
    ("w_up", (D_MODEL, D_FF), 1), ("w_down", (D_FF, D_MODEL), 0),
]
BIG_INDEX = {n: i for i, (n, _, _) in enumerate(BIG)}


def _shard_shape(shape, axis):
    return tuple(d // N_DEV if i == axis else d for i, d in enumerate(shape))


def _shard_of(ref, axis, d):
    n = ref.shape[axis] // N_DEV
    return ref.at[pl.ds(d * n, n), :] if axis == 0 else ref.at[:, pl.ds(d * n, n)]


def _gather_side(names, shards):
    idxs = [BIG_INDEX[n] for n in names]

    def make(ins, outs, send_sems, recv_sems):
        x, y, c = lax.axis_index("x"), lax.axis_index("y"), lax.axis_index("c")
        cps = []
        for j, i in enumerate(idxs):
            mine = _shard_of(outs[j], BIG[i][2], 4 * x + 2 * y + c)
            cps.append(pltpu.make_async_copy(ins[j], mine, send_sems.at[N_DEV * j]))
            for rel in range(1, N_DEV):
                to = tuple(1 - p if rel >> bit & 1 else p for p, bit in ((x, 2), (y, 1), (c, 0)))
                cps.append(pltpu.make_async_remote_copy(
                    src_ref=ins[j], dst_ref=mine, send_sem=send_sems.at[N_DEV * j + rel],
                    recv_sem=recv_sems.at[N_DEV * j + rel], device_id=to, device_id_type=MESH))
        return cps

    return _Side(shards, [jax.ShapeDtypeStruct(BIG[i][1], BF16) for i in idxs], N_DEV * len(idxs), make)


def _gather_two_level_side(names, shards):
    idxs = [BIG_INDEX[n] for n in names]

    def parts(ins, outs, send_sems, recv_sems):
        x, y, c = lax.axis_index("x"), lax.axis_index("y"), lax.axis_index("c")
        sibling = (x, y, 1 - c)
        chips = [(1 - x, y), (x, 1 - y), (1 - x, 1 - y)]
        mine, first, passed, arrived, from_sibling = [], [], [], [], []
        for w, i in enumerate(idxs):
            def place(dev, w=w, i=i):
                return _shard_of(outs[w], BIG[i][2], 4 * dev[0] + 2 * dev[1] + dev[2])

            def copy(k, blk, to, src=None, w=w, place=place):
                return pltpu.make_async_remote_copy(
                    src_ref=place(blk) if src is None else src, dst_ref=place(blk),
                    send_sem=send_sems.at[N_DEV * w + k], recv_sem=recv_sems.at[N_DEV * w + k], device_id=to,
                    device_id_type=MESH)

            mine.append(pltpu.make_async_copy(ins[w], place((x, y, c)), send_sems.at[N_DEV * w + 7]))
            first.append(copy(0, (x, y, c), sibling, src=ins[w]))
            first += [copy(1 + j, (x, y, c), (*chip, c), src=ins[w]) for j, chip in enumerate(chips)]
            passed += [copy(4 + j, (*chip, c), sibling) for j, chip in enumerate(chips)]
            arrived += [copy(1 + j, (*chip, c), (x, y, c)) for j, chip in enumerate(chips)]
            from_sibling.append(copy(0, sibling, (x, y, c)))
            from_sibling += [copy(4 + j, (*chip, 1 - c), (x, y, c)) for j, chip in enumerate(chips)]
        return mine, first, passed, arrived, from_sibling

    def make(ins, outs, send_sems, recv_sems):
        mine, first, _, _, _ = parts(ins, outs, send_sems, recv_sems)
        return mine + first

    def finish(ins, outs, send_sems, recv_sems):
        mine, first, passed, arrived, from_sibling = parts(ins, outs, send_sems, recv_sems)
        for got, onward in zip(arrived, passed):
            got.wait_recv()
            onward.start()
        for cp in from_sibling:
            cp.wait_recv()
        for cp in first + passed:
            cp.wait_send()
        for cp in mine:
            cp.wait()

    return _Side(shards, [jax.ShapeDtypeStruct(BIG[i][1], BF16) for i in idxs], N_DEV * len(idxs), make, finish)


def _sibling_side(names, grads):
    idxs = [BIG_INDEX[n] for n in names]

    def make(ins, outs, send_sems, recv_sems):
        x, y, c = lax.axis_index("x"), lax.axis_index("y"), lax.axis_index("c")
        return [pltpu.make_async_remote_copy(
            src_ref=_shard_of(ins[j], BIG[i][2], 2 * k + (1 - c)), dst_ref=outs[j].at[k],
            send_sem=send_sems.at[4 * j + k], recv_sem=recv_sems.at[4 * j + k], device_id=(x, y, 1 - c),
            device_id_type=MESH) for j, i in enumerate(idxs) for k in range(4)]

    shapes = [jax.ShapeDtypeStruct((4,) + _shard_shape(BIG[i][1], BIG[i][2]), F32) for i in idxs]
    return _Side(grads, shapes, 4 * len(idxs), make)


def _chips_side(parts):
    def make(ins, outs, send_sems, recv_sems):
        x, y, c = lax.axis_index("x"), lax.axis_index("y"), lax.axis_index("c")
        chips = [(1 - x, y), (x, 1 - y), (1 - x, 1 - y)]
        return [pltpu.make_async_remote_copy(
            src_ref=ins[j].at[2 * cx + cy], dst_ref=outs[j].at[r], send_sem=send_sems.at[3 * j + r],
            recv_sem=recv_sems.at[3 * j + r], device_id=(cx, cy, c), device_id_type=MESH)
            for r, (cx, cy) in enumerate(chips) for j in range(len(parts))]

    return _Side(parts, [jax.ShapeDtypeStruct((3,) + p.shape[1:], p.dtype) for p in parts], 3 * len(parts), make)


def _reduce_add(grad, recv, axis, core, name):
    rs, cs = recv.shape[1:]
    rt = _row_tile(rs, 256)
    nt = rs // rt

    def body(c_ref, g_ref, r_ref, p_ref, pb_ref):
        sm = g_ref[...] + r_ref[0]
        p_ref[0] = sm
        pb_ref[0] = sm.astype(BF16)

    if axis == 0:
        g_spec = pl.BlockSpec((rt, cs), lambda k, t, c_ref: ((2 * k + c_ref[0]) * nt + t, 0))
    else:
        g_spec = pl.BlockSpec((rt, cs), lambda k, t, c_ref: (t, 2 * k + c_ref[0]))
    slab = pl.BlockSpec((1, rt, cs), lambda k, t, c_ref: (k, t, 0))
    return pl.pallas_call(
        body, name=name,
        grid_spec=pltpu.PrefetchScalarGridSpec(num_scalar_prefetch=1, grid=(4, nt), in_specs=[g_spec, slab],
                                               out_specs=[slab, slab]),
        out_shape=[jax.ShapeDtypeStruct(recv.shape, F32), jax.ShapeDtypeStruct(recv.shape, BF16)],
        compiler_params=_params(("parallel", "parallel")),
    )(core, grad, recv)


def _all_gather(block, name, side):
    m_per, n = block.shape
    ns_in, ns_out = len(side.ins), len(side.out_shapes)

    def body(*refs):
        x_ref, s_ins, out_ref = refs[0], refs[1:1 + ns_in], refs[1 + ns_in]
        s_outs = refs[2 + ns_in:2 + ns_in + ns_out]
        send_sems, recv_sems, local_sem, s_send, s_recv = refs[2 + ns_in + ns_out:]
        others = side.make(s_ins, s_outs, s_send, s_recv)
        for cp in others:
            cp.start()
        x, y, c = lax.axis_index("x"), lax.axis_index("y"), lax.axis_index("c")
        me, sibling = (x, y, c), (x, y, 1 - c)
        chips = [(1 - x, y), (x, 1 - y), (1 - x, 1 - y)]

        def rows(px, py, pc):
            return out_ref.at[pl.ds((4 * px + 2 * py + pc) * m_per, m_per), :]

        def copy(k, blk, to, src=None):
            return pltpu.make_async_remote_copy(
                src_ref=rows(*blk) if src is None else src, dst_ref=rows(*blk),
                send_sem=send_sems.at[k], recv_sem=recv_sems.at[k], device_id=to, device_id_type=MESH)

        mine = pltpu.make_async_copy(x_ref, rows(*me), local_sem)
        mine.start()
        first = [copy(0, me, sibling, src=x_ref)]
        first += [copy(1 + j, me, (*chip, c), src=x_ref) for j, chip in enumerate(chips)]
        for cp in first:
            cp.start()
        passed = [copy(4 + j, (*chip, c), sibling) for j, chip in enumerate(chips)]
        for j, chip in enumerate(chips):
            copy(1 + j, (*chip, c), me).wait_recv()
            passed[j].start()
        copy(0, sibling, me).wait_recv()
        for j, chip in enumerate(chips):
            copy(4 + j, (*chip, 1 - c), me).wait_recv()
        for cp in first + passed:
            cp.wait_send()
        mine.wait()
        for cp in others:
            cp.wait()

    res = pl.pallas_call(
        body, name=name, in_specs=[ANY] * (1 + ns_in), out_specs=[ANY] * (1 + ns_out),
        out_shape=[jax.ShapeDtypeStruct((N_DEV * m_per, n), block.dtype)] + side.out_shapes,
        scratch_shapes=[pltpu.SemaphoreType.DMA((7,)), pltpu.SemaphoreType.DMA((7,)), pltpu.SemaphoreType.DMA]
        + side.sems(),
    )(block, *side.ins)
    return res[0], list(res[1:])


def _adam_math(w, g, m, v):
    m = ADAM_B1 * m + (1.0 - ADAM_B1) * g
    v = ADAM_B2 * v + (1.0 - ADAM_B2) * (g * g)
    m_hat = m / (1.0 - ADAM_B1 ** ADAM_STEP)
    v_hat = v / (1.0 - ADAM_B2 ** ADAM_STEP)
    delta = -ADAM_LR * (m_hat / (jnp.sqrt(v_hat) + ADAM_EPS) + ADAM_WD * w)
    return delta, m, v


def _adam_sharded(own, recv, w, m, v, chip, name):
    rs, cs = w.shape
    rt = _row_tile(rs, 256)

    def body(chip_ref, p_ref, r_ref, w_ref, m_ref, v_ref, g_out, d_out, m_out, v_out):
        g = p_ref[0] + r_ref[0].astype(F32) + r_ref[1].astype(F32) + r_ref[2].astype(F32)
        d, mn, vn = _adam_math(w_ref[...], g, m_ref[...], v_ref[...])
        g_out[...] = g
        d_out[...] = d
        m_out[...] = mn
        v_out[...] = vn

    tile = pl.BlockSpec((rt, cs), lambda t, chip_ref: (t, 0))
    return pl.pallas_call(
        body, name=name,
        grid_spec=pltpu.PrefetchScalarGridSpec(
            num_scalar_prefetch=1, grid=(rs // rt,),
            in_specs=[pl.BlockSpec((1, rt, cs), lambda t, chip_ref: (chip_ref[0], t, 0)),
                      pl.BlockSpec((3, rt, cs), lambda t, chip_ref: (0, t, 0)), tile, tile, tile],
            out_specs=[tile] * 4),
        out_shape=[jax.ShapeDtypeStruct((rs, cs), F32)] * 4,
        compiler_params=_params(("parallel",)),
    )(chip, own, recv, w, m, v)


SMALL = ["g_mix", "ssm_a_re", "ssm_a_im", "ssm_log_dt", "ssm_b_re", "ssm_b_im", "ssm_c_re", "ssm_c_im", "ssm_d",
         "sb_g_q", "sb_g_k", "g_out_ssm", "g_out_sb", "g_xa", "g_mem", "xa_g_q", "xa_g_k", "g_mlp"]
PACK_TILE = SUBLANES * LANES


def _natural_2d(n):
    return (n // LANES, LANES) if n % LANES == 0 else (1, n)


def _pack_small(arrs):
    parts = []
    for a in arrs:
        flat = a.reshape(-1)
        parts.append(jnp.pad(flat, (0, (-flat.shape[0]) % PACK_TILE)))
    return jnp.concatenate(parts).reshape(-1, LANES)


def _adam_replicated(gathered, sizes, ws, ms, vs, name):
    n_w = len(ws)
    r_dev = gathered.shape[0] // N_DEV
    offs, off = [], 0
    for n in sizes:
        offs.append(off)
        off += (n + PACK_TILE - 1) // PACK_TILE * SUBLANES
    assert off == r_dev

    def body(*refs):
        g_ref = refs[0]
        w_refs, m_refs, v_refs = refs[1:1 + n_w], refs[1 + n_w:1 + 2 * n_w], refs[1 + 2 * n_w:1 + 3 * n_w]
        outs = refs[1 + 3 * n_w:]

        def total(i, shape):
            r, cdim = shape
            acc = g_ref[pl.ds(offs[i], r), :cdim]
            for d in range(1, N_DEV):
                acc = acc + g_ref[pl.ds(d * r_dev + offs[i], r), :cdim]
            return acc

        for i in range(n_w):
            g = total(i, w_refs[i].shape)
            d, mn, vn = _adam_math(w_refs[i][...], g, m_refs[i][...], v_refs[i][...])
            for o, val in zip(outs[4 * i:4 * i + 4], (g, d, mn, vn)):
                o[...] = val
        outs[4 * n_w][...] = total(n_w, (SUBLANES, LANES))

    shapes = [w.shape for w in ws]
    return pl.pallas_call(
        body, name=name,
        out_shape=[jax.ShapeDtypeStruct(shp, F32) for shp in shapes for _ in range(4)]
        + [jax.ShapeDtypeStruct((SUBLANES, LANES), F32)],
        compiler_params=_params(),
    )(gathered, *ws, *ms, *vs)


def _step(x, mem, target, shards, sm, core):
    g, w, sums, reduced = {}, {}, {}, {}

    def gather(names):
        return _gather_two_level_side(names, [shards[n] for n in names])

    def to_sibling(names):
        return _sibling_side(names, [g[n] for n in names])

    def add_sibling(names, received):
        for n, r in zip(names, received):
            sums[n] = _reduce_add(g[n], r, BIG[BIG_INDEX[n]][2], core, "reduce_add_" + n)

    def to_chips(names):
        return _chips_side([sums[n][1] for n in names])

    def keep(names, received):
        for n, r in zip(names, received):
            reduced[n] = (sums[n][0], r)

    row = lambda a: a.reshape(1, -1)
    g_mix, g_xa, g_mlp, g_mem = row(sm["g_mix"]), row(sm["g_xa"]), row(sm["g_mlp"]), row(sm["g_mem"])
    g_os, g_ob = row(sm["g_out_ssm"]), row(sm["g_out_sb"])
    sb_gq, sb_gk = jnp.tile(row(sm["sb_g_q"]), (1, SB_HEADS)), jnp.tile(row(sm["sb_g_k"]), (1, SB_HEADS))
    xa_gq, xa_gk = row(sm["xa_g_q"]), row(sm["xa_g_k"])
    d_skip = row(sm["ssm_d"])

    h1, (w["w_in"],) = _norm_fwd(x, g_mix, "norm_mix", side=gather(["w_in"]))
    proj = _mm(h1, w["w_in"], "nn", "in_proj", tn=IN_WIDTH)
    u = _to_segments(proj[:, :SSM_WIDTH])
    q_raw, k_raw = (proj, SB_WIDTH, 1), (proj, SB_WIDTH, 2)
    v_col = (SSM_WIDTH + 2 * SB_WIDTH) // LANES
    sb_scale = SB_HEAD_DIM ** -0.5
    qs, ks = _rw(lambda qt, kt, gq, gk: ((_rms_groups(qt, gq, sb_scale), _rms_groups(kt, gk, 1.0)), ()),
                 [q_raw, k_raw], [sb_gq, sb_gk], [(SB_WIDTH, BF16)] * 2, [], "sb_qk_norm")
    early = ["ssm_w_glu", "w_out", "xa_w_q", "xa_w_kv", "xa_w_o", "w_up"]
    y_sb, got = _sb_fwd(qs, ks, proj, "sb_fwd", v_col=v_col, side=gather(early))
    w.update(zip(early, got))

    ssm_args = _states_on_lanes(sm)
    acat, bsup, csup = _ssm_mats_fwd(*ssm_args, "ssm_mats")
    (states, y0, y1), (w["w_down"],) = _ssm_fwd(u, acat, bsup, csup, d_skip, "ssm_fwd",
                                                side=_gather_side(["w_down"], [shards["w_down"]]))
    z_glu, y_ssm = _mm(y1, w["ssm_w_glu"], "nn", "ssm_glu", epi=lambda r, yt: (r, yt * jax.nn.sigmoid(r)),
                       extras=(y1,), out_dtypes=(F32, F32))
    y_ssm = _from_segments(y_ssm)

    def cat_norm(a, b, ga, gb):
        return jnp.concatenate([_rms(a, ga), _rms(b, gb)], axis=1)

    ycat = _rw(lambda a, b, ga, gb: ((cat_norm(a, b, ga, gb),), ()), [y_ssm, y_sb], [g_os, g_ob],
               [(D_MODEL, BF16)], [], "norm_out")[0]

    def residual_norm_epi(r, xt, gt):
        xn = r + xt
        return xn, _rms(xn, gt)

    x1, h2 = _mm(ycat, w["w_out"], "nn", "out_proj", epi=residual_norm_epi, extras=(x,), fulls=(g_xa,),
                 out_dtypes=(F32, BF16))
    qx = _mm(h2, w["xa_w_q"], "nn", "xa_q")
    memn, kv, kn_x, vv_x = _mem_fwd(mem, g_mem, w["xa_w_kv"], xa_gk, "xa_mem")
    o_xa = _xa_fwd(qx, xa_gq, kn_x, vv_x, "xa_fwd")
    x2, h3 = _mm(o_xa, w["xa_w_o"], "nn", "xa_o", epi=residual_norm_epi, extras=(x1,), fulls=(g_mlp,),
                 out_dtypes=(F32, BF16))

    def up_epi(r):
        rl = jnp.maximum(r, 0.0)
        return (rl * rl,)

    r_up = _mm(h3, w["w_up"], "nn", "mlp_up", epi=up_epi, out_dtypes=(BF16,), tm=2048, tn=2048)

    def loss_epi(r, xt, tt):
        d = r + xt - tt
        return (d * (1.0 / D_MODEL),) * 2, (jnp.sum(d * d, axis=0, keepdims=True),)

    dx3, dx3_b, sq = _mm(r_up, w["w_down"], "nn", "mlp_down", epi=loss_epi, extras=(x2, target),
                         out_dtypes=(F32, BF16), sums=[(1, D_MODEL)])
    loss = jnp.sum(sq) * (0.5 / D_MODEL)

    def norm_bwd_epi(r, xt, drt, gt):
        _, vjp = jax.vjp(_rms, xt, gt)
        dx_, dg_ = vjp(r)
        return (dx_ + drt,) * 2, (dg_,)

    g["w_down"] = _mm(r_up, dx3_b, "tn", "d_w_down", tk=2048)
    da = _mm(dx3_b, w["w_down"], "nt", "d_r", epi=lambda r, rt: (r * 2.0 * jnp.sqrt(rt.astype(F32)),), extras=(r_up,),
             out_dtypes=(BF16,), tn=2048)
    g["w_up"] = _mm(h3, da, "tn", "d_w_up", tk=2048)
    mlp = ["w_down", "w_up"]
    (dx2, dx2_b, g["g_mlp"]), got = _mm(da, w["w_up"], "nt", "d_h3", epi=norm_bwd_epi, extras=(x2, dx3),
                                        fulls=(g_mlp,), out_dtypes=(F32, BF16), sums=[g_mlp.shape],
                                        side=to_sibling(mlp))
    add_sibling(mlp, got)
    g["xa_w_o"] = _mm(o_xa, dx2_b, "tn", "d_xa_w_o", tk=2048)
    do_xa = _mm(dx2_b, w["xa_w_o"], "nt", "d_o_xa")
    dqx, dkn_x, dvv_x, g["xa_g_q"] = _xa_bwd(qx, xa_gq, kn_x, vv_x, do_xa, "xa_bwd")
    g["xa_w_kv"], g["g_mem"], g["xa_g_k"] = _mem_bwd(mem, g_mem, memn, w["xa_w_kv"], kv, xa_gk, dkn_x, dvv_x,
                                                     "xa_mem_bwd")
    g["xa_w_q"] = _mm(h2, dqx, "tn", "d_xa_w_q", tk=2048)
    dx1, dx1_b, g["g_xa"] = _mm(dqx, w["xa_w_q"], "nt", "d_h2", epi=norm_bwd_epi, extras=(x1, dx2), fulls=(g_xa,),
                                out_dtypes=(F32, BF16), sums=[g_xa.shape])
    g["w_out"] = _mm(ycat, dx1_b, "tn", "d_w_out", tk=2048)
    dycat = _mm(dx1_b, w["w_out"], "nt", "d_ycat")

    def cat_bwd(a, b, dy, ga, gb):
        _, vjp = jax.vjp(cat_norm, a, b, ga, gb)
        da_, db_, dga, dgb = vjp(dy)
        return (da_, db_), (dga, dgb)

    dy_ssm, dy_sb, g["g_out_ssm"], g["g_out_sb"] = _rw(
        cat_bwd, [y_ssm, y_sb, dycat], [g_os, g_ob], [(SSM_WIDTH, F32), (SB_WIDTH, F32)], [g_os.shape, g_ob.shape],
        "d_norm_out", tm=512)

    def glu_bwd(dy, yt, zt):
        sg = jax.nn.sigmoid(zt)
        return (dy * sg, dy * yt * sg * (1.0 - sg)), ()

    dy1_a, dz = _rw(glu_bwd, [_to_segments(dy_ssm), y1, z_glu], [], [(SSM_WIDTH, F32), (SSM_WIDTH, BF16)], [], "d_glu")
    g["ssm_w_glu"] = _mm(y1, dz, "tn", "d_w_glu", tk=2048)

    def gelu_bwd_epi(r, da_, y0t):
        _, vjp = jax.vjp(jax.nn.gelu, y0t)
        return (vjp(r + da_)[0],)

    mid = ["w_out", "xa_w_q", "xa_w_kv", "xa_w_o", "ssm_w_glu"]
    dy0, got = _mm(dz, w["ssm_w_glu"], "nt", "d_y1", epi=gelu_bwd_epi, extras=(dy1_a, y0), side=to_sibling(mid))
    add_sibling(mid, got)
    (du, da8, d_bsup, d_csup, g["ssm_d"]), got = _ssm_bwd(dy0, states, u, acat, bsup, csup, d_skip, "ssm_bwd",
                                                          side=to_chips(mlp))
    keep(mlp, got)
    d_acat = jnp.sum(da8, axis=0, keepdims=True)
    d_mats = _ssm_mats_bwd(*ssm_args[:5], d_acat, d_bsup, d_csup, "ssm_mats_bwd")
    for nm, val in zip(("ssm_a_re", "ssm_a_im", "ssm_log_dt", "ssm_b_re", "ssm_b_im", "ssm_c_re", "ssm_c_im"),
                       _from_states_on_lanes(*d_mats)):
        g[nm] = val

    (dqs, dks, dvs), got = _sb_bwd(qs, ks, proj, y_sb, dy_sb, "sb_bwd", v_col=v_col, side=to_chips(mid))
    keep(mid, got)

    def d_proj_rows(du_t, qt, dqt, kt, dkt, dvt, gq, gk):
        _, vjp_q = jax.vjp(lambda a, b_: _rms_groups(a, b_, sb_scale), qt, gq)
        _, vjp_k = jax.vjp(lambda a, b_: _rms_groups(a, b_, 1.0), kt, gk)
        (dq_, dgq_), (dk_, dgk_) = vjp_q(dqt), vjp_k(dkt)
        rows = jnp.concatenate([du_t, dq_.astype(BF16), dk_.astype(BF16), dvt.astype(BF16)], axis=1)
        return (rows,), (dgq_, dgk_)

    dproj, dgq, dgk = _rw(d_proj_rows, [_from_segments(du), q_raw, dqs, k_raw, dks, dvs], [sb_gq, sb_gk],
                          [(IN_WIDTH, BF16)], [sb_gq.shape, sb_gk.shape], "d_proj", tm=512)
    g["sb_g_q"] = jnp.sum(dgq.reshape(SB_HEADS, SB_HEAD_DIM), axis=0)
    g["sb_g_k"] = jnp.sum(dgk.reshape(SB_HEADS, SB_HEAD_DIM), axis=0)
    g["w_in"] = _mm(h1, dproj, "tn", "d_w_in", tn=IN_WIDTH)
    dh1, got = _mm(dproj, w["w_in"], "nt", "d_h1", tk=IN_WIDTH, side=to_sibling(["w_in"]))
    add_sibling(["w_in"], got)
    dx, g["g_mix"] = _norm_bwd(x, g_mix, dh1, dx1, "d_norm_mix")

    packed = _pack_small([g[n] for n in SMALL] + [loss.reshape(1)])
    everyone, got = _all_gather(packed, "gather_small", to_chips(["w_in"]))
    keep(["w_in"], got)
    return dx, everyone, reduced


def kernel(x, mem, g_mix, w_in, ssm_a_re, ssm_a_im, ssm_log_dt, ssm_b_re, ssm_b_im, ssm_c_re, ssm_c_im, ssm_d, ssm_w_glu, sb_g_q, sb_g_k, g_out_ssm, g_out_sb, w_out, g_xa, g_mem, xa_w_q, xa_w_kv, xa_g_q, xa_g_k, xa_w_o, g_mlp, w_up, w_down, loss_target, m_g_mix, m_w_in, m_ssm_a_re, m_ssm_a_im, m_ssm_log_dt, m_ssm_b_re, m_ssm_b_im, m_ssm_c_re, m_ssm_c_im, m_ssm_d, m_ssm_w_glu, m_sb_g_q, m_sb_g_k, m_g_out_ssm, m_g_out_sb, m_w_out, m_g_xa, m_g_mem, m_xa_w_q, m_xa_w_kv, m_xa_g_q, m_xa_g_k, m_xa_w_o, m_g_mlp, m_w_up, m_w_down, v_g_mix, v_w_in, v_ssm_a_re, v_ssm_a_im, v_ssm_log_dt, v_ssm_b_re, v_ssm_b_im, v_ssm_c_re, v_ssm_c_im, v_ssm_d, v_ssm_w_glu, v_sb_g_q, v_sb_g_k, v_g_out_ssm, v_g_out_sb, v_w_out, v_g_xa, v_g_mem, v_xa_w_q, v_xa_w_kv, v_xa_g_q, v_xa_g_k, v_xa_w_o, v_g_mlp, v_w_up, v_w_down):
    given = dict(locals())
    order = ["g_mix", "w_in", "ssm_a_re", "ssm_a_im", "ssm_log_dt", "ssm_b_re", "ssm_b_im", "ssm_c_re", "ssm_c_im",
             "ssm_d", "ssm_w_glu", "sb_g_q", "sb_g_k", "g_out_ssm", "g_out_sb", "w_out", "g_xa", "g_mem", "xa_w_q",
             "xa_w_kv", "xa_g_q", "xa_g_k", "xa_w_o", "g_mlp", "w_up", "w_down"]
    assert sorted([n for n, _, _ in BIG] + SMALL) == sorted(order)
    core = lax.axis_index("c").astype(jnp.int32).reshape(1)
    chip = (2 * lax.axis_index("x") + lax.axis_index("y")).astype(jnp.int32).reshape(1)

    shards = {n: given[n][0].astype(BF16) for n, _, _ in BIG}
    sm = {n: given[n][0] for n in SMALL}
    dx, everyone, reduced = _step(x[0], mem[0], loss_target[0], shards, sm, core)

    res = {}
    for n, _, _ in BIG:
        own, recv = reduced[n]
        outs = _adam_sharded(own, recv, given[n][0], given["m_" + n][0], given["v_" + n][0], chip, "adam_" + n)
        for kind, val in zip(("grad", "delta", "new_m", "new_v"), outs):
            res[kind + "_" + n] = val[None]

    sizes = [math.prod(sm[n].shape) for n in SMALL] + [1]
    nat = lambda a: a.reshape(_natural_2d(math.prod(a.shape)))
    outs = _adam_replicated(everyone, sizes, [nat(sm[n]) for n in SMALL], [nat(given["m_" + n][0]) for n in SMALL],
                            [nat(given["v_" + n][0]) for n in SMALL], "adam_replicated")
    for i, n in enumerate(SMALL):
        for kind, val in zip(("grad", "delta", "new_m", "new_v"), outs[4 * i:4 * i + 4]):
            res[kind + "_" + n] = val.reshape(given[n].shape)
    loss_out = outs[-1][0, 0]
    return (loss_out, dx[None], *[res["grad_" + n] for n in order], *[res["delta_" + n] for n in order],
            *[res["new_m_" + n] for n in order], *[res["new_v_" + n] for n in order])
```

```python
import functools
import math

import jax
import jax.numpy as jnp
from jax import lax
from jax.experimental import pallas as pl
from jax.experimental.pallas import tpu as pltpu

F32 = jnp.float32
BF16 = jnp.bfloat16
MESH = pl.DeviceIdType.MESH

N_DEV = 8
D_MODEL = 1024
SSM_WIDTH = 512
SSM_GROUP = 16
SSM_GROUPS = 32
SSM_STATE = 64
N_STATE = SSM_GROUPS * SSM_STATE
SB_HEADS = 8
SB_HEAD_DIM = 64
SB_WIDTH = 512
IN_WIDTH = 2048
XA_HEADS = 4
XA_HEAD_DIM = 128
XA_WIDTH = 512
D_FF = 4096
NORM_EPS = 1e-6
ADAM_LR = 0.001
ADAM_B1 = 0.9
ADAM_B2 = 0.999
ADAM_EPS = 1e-08
ADAM_WD = 0.01
ADAM_STEP = 10

LANES = 128
SUBLANES = 8
VMEM_LIMIT = 56 * 1024 * 1024
SCAN_LANES = 512
SB_BLOCK = 256
SB_Q_BLOCKS = 4
SB_UNDERFLOW = -110.0

NN = (((1,), (0,)), ((), ()))
NT = (((1,), (1,)), ((), ()))
TN = (((0,), (0,)), ((), ()))


def _params(sem=None):
    return pltpu.CompilerParams(dimension_semantics=sem, vmem_limit_bytes=VMEM_LIMIT)


def _dot(a, b, dims=NN):
    return lax.dot_general(a.astype(BF16), b.astype(BF16), dims, preferred_element_type=F32)


def _rms(x, g):
    return x * lax.rsqrt(jnp.mean(x * x, axis=-1, keepdims=True) + NORM_EPS) * g


ANY = pl.BlockSpec(memory_space=pl.ANY)


class _Side:
    def __init__(self, ins, out_shapes, n_sem, make, finish=None):
        self.ins, self.out_shapes, self.n_sem, self.make = list(ins), list(out_shapes), n_sem, make
        self.finish = finish

    def sems(self):
        return [pltpu.SemaphoreType.DMA((self.n_sem,)), pltpu.SemaphoreType.DMA((self.n_sem,))]


def _hosted(body, side, n_in, n_out, grid):
    if side is None:
        return body
    ns_in, ns_out = len(side.ins), len(side.out_shapes)

    def wrapped(*refs):
        ins, refs = refs[:n_in], refs[n_in:]
        s_ins, refs = refs[:ns_in], refs[ns_in:]
        outs, refs = refs[:n_out], refs[n_out:]
        s_outs, refs = refs[:ns_out], refs[ns_out:]
        scratch, sems = refs[:-2], refs[-2:]
        ids = [pl.program_id(d) for d in range(len(grid))]
        first = functools.reduce(jnp.logical_and, [i == 0 for i in ids])
        last = functools.reduce(jnp.logical_and, [i == n - 1 for i, n in zip(ids, grid)])

        @pl.when(first)
        def _():
            for cp in side.make(s_ins, s_outs, *sems):
                cp.start()

        body(*ins, *outs, *scratch)

        @pl.when(last)
        def _():
            if side.finish is not None:
                side.finish(s_ins, s_outs, *sems)
            else:
                for cp in side.make(s_ins, s_outs, *sems):
                    cp.wait()

    return wrapped


def _side_args(side):
    if side is None:
        return [], [], [], [], []
    return ([ANY] * len(side.ins), [ANY] * len(side.out_shapes), side.out_shapes, side.sems(), side.ins)


def _split_side(res, n_out, side):
    res = list(res)
    main = res[0] if n_out == 1 else res[:n_out]
    return main if side is None else (main, res[n_out:])


def _mm(a, b, mode, name, *, epi=None, extras=(), fulls=(), out_dtypes=(F32,), sums=(), tm=1024, tn=1024, tk=1024,
        epi_slabs=1, side=None):
    if mode == "nn":
        (m, k), (k2, n) = a.shape, b.shape
    elif mode == "nt":
        (m, k), (n, k2) = a.shape, b.shape
    else:
        (k, m), (k2, n) = a.shape, b.shape
    assert k == k2, (name, a.shape, b.shape)
    tm, tn, tk = min(tm, m), min(tn, n), min(tk, k)
    assert m % tm == 0 and n % tn == 0 and k % tk == 0, (name, m, n, k)
    nk = k // tk
    dims = {"nn": NN, "nt": NT, "tn": TN}[mode]
    if mode == "tn":
        a_spec = pl.BlockSpec((tk, tm), lambda i, j, kk: (kk, i))
    else:
        a_spec = pl.BlockSpec((tm, tk), lambda i, j, kk: (i, kk))
    if mode == "nt":
        b_spec = pl.BlockSpec((tn, tk), lambda i, j, kk: (j, kk))
    else:
        b_spec = pl.BlockSpec((tk, tn), lambda i, j, kk: (kk, j))
    mn_spec = pl.BlockSpec((tm, tn), lambda i, j, kk: (i, j))
    n_ex, n_full, n_out, n_sum = len(extras), len(fulls), len(out_dtypes), len(sums)
    n_in = 2 + n_ex + n_full

    def body(*refs):
        a_ref, b_ref = refs[:2]
        ex = refs[2:n_in]
        outs = refs[n_in:n_in + n_out]
        sum_refs = refs[n_in + n_out:n_in + n_out + n_sum]
        kk = pl.program_id(2)
        first_tile = jnp.logical_and(pl.program_id(0) == 0, pl.program_id(1) == 0)

        def finish(r):
            if n_sum:
                @pl.when(first_tile)
                def _():
                    for sr in sum_refs:
                        sr[...] = jnp.zeros_like(sr)

            for sl in range(epi_slabs):
                rows = slice(sl * tm // epi_slabs, (sl + 1) * tm // epi_slabs)
                args = [e[rows, :] for e in ex[:n_ex]] + [e[...] for e in ex[n_ex:]]
                vals = epi(r[rows, :], *args) if epi is not None else (r[rows, :],)
                if n_sum:
                    vals, parts = vals
                    for sr, p in zip(sum_refs, parts):
                        sr[...] += p
                for o, v in zip(outs, vals):
                    o[rows, :] = v.astype(o.dtype)

        if nk == 1:
            finish(_dot(a_ref[...], b_ref[...], dims))
        else:
            acc = refs[n_in + n_out + n_sum]

            @pl.when(kk == 0)
            def _():
                acc[...] = jnp.zeros_like(acc)

            acc[...] += _dot(a_ref[...], b_ref[...], dims)

            @pl.when(kk == nk - 1)
            def _():
                finish(acc)

    grid = (m // tm, n // tn, nk)
    whole = lambda shape: pl.BlockSpec(shape, lambda i, j, kk: (0,) * len(shape))
    s_in, s_out, s_shape, s_scratch, s_ops = _side_args(side)
    seq = bool(side) or n_sum > 0
    res = pl.pallas_call(
        _hosted(body, side, n_in, n_out + n_sum, grid), name=name, grid=grid,
        in_specs=[a_spec, b_spec] + [mn_spec] * n_ex + [whole(f.shape) for f in fulls] + s_in,
        out_specs=[mn_spec] * n_out + [whole(shape) for shape in sums] + s_out,
        out_shape=[jax.ShapeDtypeStruct((m, n), dt) for dt in out_dtypes]
        + [jax.ShapeDtypeStruct(shape, F32) for shape in sums] + s_shape,
        scratch_shapes=([pltpu.VMEM((tm, tn), F32)] if nk > 1 else []) + s_scratch,
        compiler_params=_params(("arbitrary",) * 3 if seq else ("parallel", "parallel", "arbitrary")),
    )(a, b, *extras, *fulls, *s_ops)
    return _split_side(res, n_out + n_sum, side)


def _row_tile(s, target):
    if s <= target:
        return s
    return max(t for t in range(16, target + 1, 16) if s % t == 0)


def _rw(fn, rows, fulls, row_out, acc_out, name, tm=1024, side=None):
    cols = [r[1:] if isinstance(r, tuple) else (r.shape[1], 0) for r in rows]
    rows = [r[0] if isinstance(r, tuple) else r for r in rows]
    s = rows[0].shape[0]
    tm = _row_tile(s, tm)
    nr, nf, nro, nao = len(rows), len(fulls), len(row_out), len(acc_out)

    def body(*refs):
        r = refs[:nr]
        f = refs[nr:nr + nf]
        ro = refs[nr + nf:nr + nf + nro]
        ao = refs[nr + nf + nro:]
        outs, accs = fn(*[x[...] for x in r], *[x[...] for x in f])
        for o, v in zip(ro, outs):
            o[...] = v.astype(o.dtype)
        if nao:
            @pl.when(pl.program_id(0) == 0)
            def _():
                for a in ao:
                    a[...] = jnp.zeros_like(a)

            for a, v in zip(ao, accs):
                a[...] += v

    full_spec = lambda shape: pl.BlockSpec(shape, lambda i: (0,) * len(shape))
    s_in, s_out, s_shape, s_scratch, s_ops = _side_args(side)
    res = pl.pallas_call(
        _hosted(body, side, nr + nf, nro + nao, (s // tm,)), name=name, grid=(s // tm,),
        in_specs=[pl.BlockSpec((tm, wd), functools.partial(lambda i, cb: (i, cb), cb=cb)) for wd, cb in cols]
        + [full_spec(x.shape) for x in fulls] + s_in,
        out_specs=[pl.BlockSpec((tm, d), lambda i: (i, 0)) for d, _ in row_out]
        + [full_spec(shape) for shape in acc_out] + s_out,
        out_shape=[jax.ShapeDtypeStruct((s, d), dt) for d, dt in row_out]
        + [jax.ShapeDtypeStruct(shape, F32) for shape in acc_out] + s_shape,
        scratch_shapes=s_scratch,
        compiler_params=_params(("arbitrary",)),
    )(*rows, *fulls, *s_ops)
    res = list(res)
    return res if side is None else (res[:nro + nao], res[nro + nao:])


def _norm_fwd(x, g, name, side=None):
    res = _rw(lambda xt, gt: ((_rms(xt, gt),), ()), [x], [g], [(x.shape[1], BF16)], [], name, side=side)
    return res[0] if side is None else (res[0][0], res[1])


def _norm_bwd(x, g, dh, dres, name, side=None):
    def fn(xt, dht, drt, gt):
        _, vjp = jax.vjp(_rms, xt, gt)
        dx, dg = vjp(dht)
        return (dx + drt,), (dg,)

    return _rw(fn, [x, dh, dres], [g], [(x.shape[1], F32)], [g.shape], name, side=side)


def _rms_groups(x, g, scale):
    lo = lax.broadcasted_iota(jnp.int32, (1, LANES), 1) < SB_HEAD_DIM
    x2 = x * x
    outs = []
    for cb in range(x.shape[1] // LANES):
        sl = slice(cb * LANES, (cb + 1) * LANES)
        s_lo = jnp.sum(jnp.where(lo, x2[:, sl], 0.0), axis=-1, keepdims=True)
        s_hi = jnp.sum(jnp.where(lo, 0.0, x2[:, sl]), axis=-1, keepdims=True)
        r = jnp.where(lo, lax.rsqrt(s_lo * (1.0 / SB_HEAD_DIM) + NORM_EPS),
                      lax.rsqrt(s_hi * (1.0 / SB_HEAD_DIM) + NORM_EPS))
        outs.append(x[:, sl] * r)
    return jnp.concatenate(outs, axis=1) * g * scale


def _log_sigmoid(z):
    return jnp.minimum(z, 0.0) - jnp.log(1.0 + jnp.exp(-jnp.abs(z)))


def _split_dot(x, u2):
    hi = x.astype(BF16)
    lo = (x - hi.astype(F32)).astype(BF16)
    return jnp.dot(jnp.concatenate([hi, lo], axis=1), u2, preferred_element_type=F32)


def _sb_consts(b):
    row = lax.broadcasted_iota(jnp.int32, (b, b), 0)
    col = lax.broadcasted_iota(jnp.int32, (b, b), 1)
    tri = col < row
    u_after = (row > col).astype(BF16)
    u_from = (row >= col).astype(BF16)
    stack = lambda u: jnp.concatenate([u, u], axis=0)
    lane_lo = lax.broadcasted_iota(jnp.int32, (b, LANES), 1) < SB_HEAD_DIM
    return tri, stack(u_after), stack(u_from), lane_lo


def _sb_scores(qh, kb, a_run, keep, u2_after, mask_l=True):
    z = lax.dot_general(qh, kb, NT, preferred_element_type=F32)
    lb = _log_sigmoid(z)
    l = lb - z
    if keep is not None and mask_l:
        l = jnp.where(keep, l, 0.0)
    w = jnp.exp(lb + (a_run + _split_dot(l, u2_after)))
    if keep is not None:
        w = jnp.where(keep, w, 0.0)
    return lb, l, w


def _sb_walk(qi, carry, step):
    def cond(state):
        n, c = state
        return jnp.logical_and(n <= qi, jnp.max(jnp.maximum(c[0], c[1])) > SB_UNDERFLOW)

    def body(state):
        n, c = state
        return n + 1, step(n, c)

    return lax.while_loop(cond, body, (jnp.int32(2), carry))[1]


def _two_heads(x, lane_lo):
    zero = jnp.zeros_like(x)
    return jnp.where(lane_lo, x, zero), jnp.where(lane_lo, zero, x)


def _sb_fwd(qs, ks, v, name, v_col=0, side=None):
    s, width = qs.shape
    b = min(SB_BLOCK, s)
    nqb = min(SB_Q_BLOCKS, s // b)

    def body(q_ref, k_ref, v_ref, o_ref):
        tri, u2_after, _, lane_lo = _sb_consts(b)
        zero = jnp.zeros((b, 1), F32)
        started = []
        for h in range(nqb):
            qi = pl.program_id(1) * nqb + h
            q_a, q_b = _two_heads(q_ref[h * b:(h + 1) * b, :], lane_lo)

            def step(n, carry, keep, mask_l=True, qi=qi, q_a=q_a, q_b=q_b):
                a_a, a_b, acc = carry
                off = pl.multiple_of(jnp.maximum(qi - n, 0) * b, b)
                kb = k_ref[pl.ds(off, b), :]
                v_a, v_b = _two_heads(v_ref[pl.ds(off, b), :].astype(BF16), lane_lo)
                _, l_a, w_a = _sb_scores(q_a, kb, a_a, keep, u2_after, mask_l)
                _, l_b, w_b = _sb_scores(q_b, kb, a_b, keep, u2_after, mask_l)
                acc = acc + jnp.dot(jnp.concatenate([w_a.astype(BF16), w_b.astype(BF16)], axis=1),
                                    jnp.concatenate([v_a, v_b], axis=0), preferred_element_type=F32)
                return (a_a + jnp.sum(l_a, axis=1, keepdims=True), a_b + jnp.sum(l_b, axis=1, keepdims=True), acc)

            carry = step(0, (zero, zero, jnp.zeros((b, LANES), F32)), tri)
            carry = step(1, carry, jnp.broadcast_to(qi > 0, tri.shape), mask_l=False)
            started.append((qi, step, carry))
        for h, (qi, step, carry) in enumerate(started):
            carry = _sb_walk(qi, carry, lambda n, c, step=step: step(n, c, None))
            o_ref[h * b:(h + 1) * b, :] = carry[2]

    blk = pl.BlockSpec((nqb * b, LANES), lambda hp, i: (i, hp))
    full = pl.BlockSpec((s, LANES), lambda hp, i: (0, hp))
    full_v = pl.BlockSpec((s, LANES), lambda hp, i: (0, hp + v_col))
    grid = (width // LANES, s // (nqb * b))
    s_in, s_out, s_shape, s_scratch, s_ops = _side_args(side)
    res = pl.pallas_call(
        _hosted(body, side, 3, 1, grid), name=name, grid=grid,
        in_specs=[blk, full, full_v] + s_in, out_specs=[blk] + s_out,
        out_shape=[jax.ShapeDtypeStruct((s, width), F32)] + s_shape, scratch_shapes=s_scratch,
        compiler_params=_params(("arbitrary", "arbitrary")),
    )(qs, ks, v, *s_ops)
    return _split_side(res, 1, side)


def _sb_bwd(qs, ks, v, out, dout, name, v_col=0, side=None):
    s, width = qs.shape
    b = min(SB_BLOCK, s)
    nqb = min(SB_Q_BLOCKS, s // b)

    def body(q_ref, k_ref, v_ref, o_ref, do_ref, dq_ref, dk_ref, dv_ref):
        @pl.when(pl.program_id(1) == 0)
        def _():
            dk_ref[...] = jnp.zeros_like(dk_ref)
            dv_ref[...] = jnp.zeros_like(dv_ref)

        tri, u2_after, u2_from, lane_lo = _sb_consts(b)
        zero = jnp.zeros((b, 1), F32)

        def head(qh, doh, kb, vb, a_run, d_rem, keep, mask_l):
            lb, l, w = _sb_scores(qh, kb, a_run, keep, u2_after, mask_l)
            wb = w.astype(BF16)
            g = lax.dot_general(doh, vb, NT, preferred_element_type=F32) * wb.astype(F32)
            g_before = d_rem - _split_dot(g, u2_from)
            dz = g - (g + g_before) * jnp.exp(lb)
            if keep is not None:
                dz = jnp.where(keep, dz, 0.0)
            return (dz.astype(BF16), wb, a_run + jnp.sum(l, axis=1, keepdims=True),
                    d_rem - jnp.sum(g, axis=1, keepdims=True))

        started = []
        for h in range(nqb):
            qi = pl.program_id(1) * nqb + h
            rows = slice(h * b, (h + 1) * b)
            q_a, q_b = _two_heads(q_ref[rows, :], lane_lo)
            dob = do_ref[rows, :].astype(BF16)
            do_a, do_b = _two_heads(dob, lane_lo)
            prod = dob.astype(F32) * o_ref[rows, :]
            d_a = jnp.sum(jnp.where(lane_lo, prod, 0.0), axis=1, keepdims=True)
            d_b = jnp.sum(jnp.where(lane_lo, 0.0, prod), axis=1, keepdims=True)
            q_rows = jnp.concatenate([q_a, q_b], axis=0)
            do_rows = jnp.concatenate([do_a, do_b], axis=0)

            def step(n, carry, keep, mask_l=True, qi=qi, q_a=q_a, q_b=q_b, do_a=do_a, do_b=do_b, q_rows=q_rows,
                     do_rows=do_rows):
                a_a, a_b, r_a, r_b, dq = carry
                off = pl.multiple_of(jnp.maximum(qi - n, 0) * b, b)
                kb = k_ref[pl.ds(off, b), :]
                vb = v_ref[pl.ds(off, b), :].astype(BF16)
                k_a, k_b = _two_heads(kb, lane_lo)
                dz_a, w_a, a_a, r_a = head(q_a, do_a, kb, vb, a_a, r_a, keep, mask_l)
                dz_b, w_b, a_b, r_b = head(q_b, do_b, kb, vb, a_b, r_b, keep, mask_l)
                dq = dq + jnp.dot(jnp.concatenate([dz_a, dz_b], axis=1), jnp.concatenate([k_a, k_b], axis=0),
                                  preferred_element_type=F32)
                dk_ref[pl.ds(off, b), :] += lax.dot_general(jnp.concatenate([dz_a, dz_b], axis=0), q_rows, TN,
                                                            preferred_element_type=F32)
                dv_ref[pl.ds(off, b), :] += lax.dot_general(jnp.concatenate([w_a, w_b], axis=0), do_rows, TN,
                                                            preferred_element_type=F32)
                return a_a, a_b, r_a, r_b, dq

            carry = step(0, (zero, zero, d_a, d_b, jnp.zeros((b, LANES), F32)), tri)
            carry = step(1, carry, jnp.broadcast_to(qi > 0, tri.shape), mask_l=False)
            started.append((qi, step, carry))
        for h, (qi, step, carry) in enumerate(started):
            carry = _sb_walk(qi, carry, lambda n, c, step=step: step(n, c, None))
            dq_ref[h * b:(h + 1) * b, :] = carry[4]

    blk = pl.BlockSpec((nqb * b, LANES), lambda hp, i: (i, hp))
    full = pl.BlockSpec((s, LANES), lambda hp, i: (0, hp))
    full_v = pl.BlockSpec((s, LANES), lambda hp, i: (0, hp + v_col))
    grid = (width // LANES, s // (nqb * b))
    s_in, s_out, s_shape, s_scratch, s_ops = _side_args(side)
    res = pl.pallas_call(
        _hosted(body, side, 5, 3, grid), name=name, grid=grid,
        in_specs=[blk, full, full_v, blk, blk] + s_in, out_specs=[blk, full, full] + s_out,
        out_shape=[jax.ShapeDtypeStruct((s, width), F32)] * 3 + s_shape,
        scratch_shapes=s_scratch,
        compiler_params=_params(("arbitrary", "arbitrary")),
    )(qs, ks, v, out, dout, *s_ops)
    return _split_side(res, 3, side)


def _cmul(xr, xi, yr, yi):
    return xr * yr - xi * yi, xr * yi + xi * yr


def _scan_consts(ar, ai, reverse, lc):
    rowi = lax.broadcasted_iota(jnp.int32, (SUBLANES, lc), 0)
    pows = [(ar, ai)]
    for _ in range(SUBLANES - 1):
        pows.append(_cmul(*pows[-1], ar, ai))
    steps = []
    for d in (1, 2, 4):
        keep = (rowi < SUBLANES - d) if reverse else (rowi >= d)
        pr, pi = pows[d - 1]
        steps.append((SUBLANES - d if reverse else d, jnp.where(keep, pr, 0.0), jnp.where(keep, pi, 0.0)))
    cr = jnp.zeros((SUBLANES, lc), F32)
    ci = jnp.zeros((SUBLANES, lc), F32)
    for r in range(SUBLANES):
        pr, pi = pows[SUBLANES - 1 - r] if reverse else pows[r]
        cr = jnp.where(rowi == r, pr, cr)
        ci = jnp.where(rowi == r, pi, ci)
    return steps, cr, ci


def _scan_tile(xr, xi, steps, pr, pi, cr, ci):
    for shift, ar, ai in steps:
        rr = pltpu.roll(xr, shift, 0)
        ri = pltpu.roll(xi, shift, 0)
        xr, xi = xr + ar * rr - ai * ri, xi + ar * ri + ai * rr
    return xr + pr * cr - pi * ci, xi + pr * ci + pi * cr


SCAN_ROWS = 2048


def _scan_chunk(s):
    tt = min(SCAN_ROWS, s)
    seg = tt // SUBLANES
    assert s % tt == 0 and seg % SUBLANES == 0 and seg & (seg - 1) == 0, s
    return tt, seg


def _to_segments(a):
    s, wd = a.shape
    tt, seg = _scan_chunk(s)
    return jnp.transpose(a.reshape(s // tt, SUBLANES, seg, wd), (0, 2, 1, 3)).reshape(s, wd)


def _from_segments(a):
    s, wd = a.shape
    tt, seg = _scan_chunk(s)
    return jnp.transpose(a.reshape(s // tt, seg, SUBLANES, wd), (0, 2, 1, 3)).reshape(s, wd)


def _cpow2(xr, xi, k):
    for _ in range(k):
        xr, xi = _cmul(xr, xi, xr, xi)
    return xr, xi


def _fill_powers(pw_ref, ar, ai, seg, lc):
    _, p8r, p8i = _scan_consts(ar, ai, False, lc)
    a8r, a8i = _cpow2(ar, ai, 3)
    qr, qi = jnp.ones_like(ar), jnp.zeros_like(ai)
    for k in range(seg // SUBLANES):
        tr, ti = _cmul(p8r, p8i, qr, qi)
        for r in range(SUBLANES):
            rows = pl.ds((SUBLANES * k + r) * SUBLANES, SUBLANES)
            pw_ref[rows, :lc] = jnp.broadcast_to(tr[r:r + 1, :], (SUBLANES, lc))
            pw_ref[rows, lc:] = jnp.broadcast_to(ti[r:r + 1, :], (SUBLANES, lc))
        qr, qi = _cmul(qr, qi, a8r, a8i)


def _ssm_fwd(u, acat, bsup, csup, d_skip, name, side=None):
    s = u.shape[0]
    lc = SCAN_LANES
    tt, seg = _scan_chunk(s)
    nl, nt = N_STATE // lc, s // tt
    tile = lambda j: pl.ds(pl.multiple_of(j * SUBLANES, SUBLANES), SUBLANES)

    def body(u_ref, a_ref, b_ref, c_ref, d_ref, s_ref, y0_ref, y1_ref, carry, pw_ref):
        ar, ai = a_ref[:, :lc], a_ref[:, lc:]

        @pl.when(pl.program_id(1) == 0)
        def _():
            carry[...] = jnp.zeros_like(carry)
            _fill_powers(pw_ref, ar, ai, seg, lc)

        ut = u_ref[...]
        s_ref[...] = _dot(ut, b_ref[0])

        ar8, ai8 = jnp.broadcast_to(ar, (SUBLANES, lc)), jnp.broadcast_to(ai, (SUBLANES, lc))

        def local(j, x):
            xr = ar8 * x[0] - ai8 * x[1] + s_ref[tile(j), :lc]
            xi = ar8 * x[1] + ai8 * x[0] + s_ref[tile(j), lc:]
            s_ref[tile(j), :lc] = xr
            s_ref[tile(j), lc:] = xi
            return xr, xi

        zero = jnp.zeros((SUBLANES, lc), F32)
        er, ei = lax.fori_loop(0, seg, local, (zero, zero), unroll=4)
        steps, pr, pi = _scan_consts(*_cpow2(ar, ai, seg.bit_length() - 1), False, lc)
        cr, ci = carry[:, :lc], carry[:, lc:]
        tr, ti = _scan_tile(er, ei, steps, pr, pi, cr, ci)
        rowi = lax.broadcasted_iota(jnp.int32, (SUBLANES, lc), 0)
        before_r = jnp.where(rowi == 0, cr, pltpu.roll(tr, 1, 0))
        before_i = jnp.where(rowi == 0, ci, pltpu.roll(ti, 1, 0))
        carry[:, :lc] = jnp.broadcast_to(tr[SUBLANES - 1:, :], (SUBLANES, lc))
        carry[:, lc:] = jnp.broadcast_to(ti[SUBLANES - 1:, :], (SUBLANES, lc))

        def fix(j, _):
            pwr, pwi = pw_ref[tile(j), :lc], pw_ref[tile(j), lc:]
            s_ref[tile(j), :lc] += pwr * before_r - pwi * before_i
            s_ref[tile(j), lc:] += pwr * before_i + pwi * before_r
            return 0

        lax.fori_loop(0, seg, fix, 0, unroll=4)
        y0 = _dot(s_ref[...], c_ref[0], NT) + d_ref[...] * ut
        y0_ref[...] = y0
        y1_ref[...] = jax.nn.gelu(y0)

    chan = pl.BlockSpec((tt, LANES), lambda j, c: (c, j))
    sup = pl.BlockSpec((1, LANES, 2 * lc), lambda j, c: (j, 0, 0))
    s_in, s_out, s_shape, s_scratch, s_ops = _side_args(side)
    res = pl.pallas_call(
        _hosted(body, side, 5, 3, (nl, nt)), name=name, grid=(nl, nt),
        in_specs=[chan, pl.BlockSpec((1, 2 * lc), lambda j, c: (0, j)), sup, sup,
                  pl.BlockSpec((1, LANES), lambda j, c: (0, j))] + s_in,
        out_specs=[pl.BlockSpec((tt, 2 * lc), lambda j, c: (c, j)), chan, chan] + s_out,
        out_shape=[jax.ShapeDtypeStruct((s, 2 * N_STATE), F32), jax.ShapeDtypeStruct((s, SSM_WIDTH), F32),
                   jax.ShapeDtypeStruct((s, SSM_WIDTH), F32)] + s_shape,
        scratch_shapes=[pltpu.VMEM((SUBLANES, 2 * lc), F32), pltpu.VMEM((seg * SUBLANES, 2 * lc), F32)] + s_scratch,
        compiler_params=_params(("arbitrary", "arbitrary")),
    )(u, acat, bsup, csup, d_skip, *s_ops)
    return _split_side(res, 3, side)


def _ssm_bwd(dy0, states, u, acat, bsup, csup, d_skip, name, side=None):
    s = u.shape[0]
    lc = SCAN_LANES
    tt, seg = _scan_chunk(s)
    nl, nt = N_STATE // lc, s // tt
    tile = lambda j: pl.ds(pl.multiple_of(j * SUBLANES, SUBLANES), SUBLANES)

    def body(dy_ref, s_ref, sp_ref, u_ref, a_ref, b_ref, c_ref, d_ref,
             du_ref, da_ref, db_ref, dc_ref, dd_ref, lam_ref, carry, pw_ref):
        c = pl.program_id(1)
        ar, ai = a_ref[:, :lc], a_ref[:, lc:]

        @pl.when(c == 0)
        def _():
            carry[...] = jnp.zeros_like(carry)
            for r in (da_ref, db_ref, dc_ref, dd_ref):
                r[...] = jnp.zeros_like(r)
            _fill_powers(pw_ref, ar, ai, seg, lc)

        dy = dy_ref[...]
        ut = u_ref[...]
        lam_ref[...] = _dot(dy, c_ref[0])

        ar8, ai8 = jnp.broadcast_to(ar, (SUBLANES, lc)), jnp.broadcast_to(ai, (SUBLANES, lc))

        def local(i, x):
            j = seg - 1 - i
            xr = ar8 * x[0] + ai8 * x[1] + lam_ref[tile(j), :lc]
            xi = ar8 * x[1] - ai8 * x[0] + lam_ref[tile(j), lc:]
            lam_ref[tile(j), :lc] = xr
            lam_ref[tile(j), lc:] = xi
            return xr, xi

        zero = jnp.zeros((SUBLANES, lc), F32)
        er, ei = lax.fori_loop(0, seg, local, (zero, zero), unroll=4)
        big_r, big_i = _cpow2(ar, ai, seg.bit_length() - 1)
        steps, pr, pi = _scan_consts(big_r, -big_i, True, lc)
        cr, ci = carry[:, :lc], carry[:, lc:]
        tr, ti = _scan_tile(er, ei, steps, pr, pi, cr, ci)
        rowi = lax.broadcasted_iota(jnp.int32, (SUBLANES, lc), 0)
        after_r = jnp.where(rowi == SUBLANES - 1, cr, pltpu.roll(tr, SUBLANES - 1, 0))
        after_i = jnp.where(rowi == SUBLANES - 1, ci, pltpu.roll(ti, SUBLANES - 1, 0))
        carry[:, :lc] = jnp.broadcast_to(tr[:1, :], (SUBLANES, lc))
        carry[:, lc:] = jnp.broadcast_to(ti[:1, :], (SUBLANES, lc))

        start = c != nt - 1
        last_r = jnp.where(start, jnp.broadcast_to(sp_ref[SUBLANES - 1:, :lc], (SUBLANES, lc)), 0.0)
        last_i = jnp.where(start, jnp.broadcast_to(sp_ref[SUBLANES - 1:, lc:], (SUBLANES, lc)), 0.0)
        first_r = jnp.where(rowi == 0, last_r, pltpu.roll(s_ref[tile(seg - 1), :lc], 1, 0))
        first_i = jnp.where(rowi == 0, last_i, pltpu.roll(s_ref[tile(seg - 1), lc:], 1, 0))

        def fix(j, acc):
            dar, dai = acc
            k = seg - 1 - j
            pwr, pwi = pw_ref[tile(k), :lc], pw_ref[tile(k), lc:]
            lr = lam_ref[tile(j), :lc] + pwr * after_r + pwi * after_i
            li = lam_ref[tile(j), lc:] + pwr * after_i - pwi * after_r
            lam_ref[tile(j), :lc] = lr
            lam_ref[tile(j), lc:] = li
            jp = jnp.maximum(j - 1, 0)
            sr = jnp.where(j > 0, s_ref[tile(jp), :lc], first_r)
            si = jnp.where(j > 0, s_ref[tile(jp), lc:], first_i)
            return dar + lr * sr + li * si, dai + li * sr - lr * si

        dar, dai = lax.fori_loop(0, seg, fix, (zero, zero), unroll=4)
        da_ref[:, :lc] += dar
        da_ref[:, lc:] += dai
        lam = lam_ref[...].astype(BF16)
        du_ref[...] = (_dot(lam, b_ref[0], NT) + d_ref[...] * dy).astype(du_ref.dtype)
        db_ref[0] += _dot(ut, lam, TN)
        dc_ref[0] += _dot(dy, s_ref[...], TN)
        dd_ref[...] += jnp.sum(dy * ut, axis=0, keepdims=True)

    rev = lambda j, c: (nt - 1 - c, j)
    chan = pl.BlockSpec((tt, LANES), rev)
    sup = pl.BlockSpec((1, LANES, 2 * lc), lambda j, c: (j, 0, 0))
    row = pl.BlockSpec((1, LANES), lambda j, c: (0, j))
    s_in, s_out, s_shape, s_scratch, s_ops = _side_args(side)
    res = pl.pallas_call(
        _hosted(body, side, 8, 5, (nl, nt)), name=name, grid=(nl, nt),
        in_specs=[chan, pl.BlockSpec((tt, 2 * lc), rev),
                  pl.BlockSpec((SUBLANES, 2 * lc), lambda j, c: (jnp.maximum((nt - 1 - c) * seg - 1, 0), j)),
                  chan, pl.BlockSpec((1, 2 * lc), lambda j, c: (0, j)), sup, sup, row] + s_in,
        out_specs=[chan, pl.BlockSpec((SUBLANES, 2 * lc), lambda j, c: (0, j)), sup, sup, row] + s_out,
        out_shape=[jax.ShapeDtypeStruct((s, SSM_WIDTH), BF16), jax.ShapeDtypeStruct((SUBLANES, 2 * N_STATE), F32),
                   jax.ShapeDtypeStruct(bsup.shape, F32), jax.ShapeDtypeStruct(csup.shape, F32),
                   jax.ShapeDtypeStruct((1, SSM_WIDTH), F32)] + s_shape,
        scratch_shapes=[pltpu.VMEM((tt, 2 * lc), F32), pltpu.VMEM((SUBLANES, 2 * lc), F32),
                        pltpu.VMEM((seg * SUBLANES, 2 * lc), F32)] + s_scratch,
        compiler_params=_params(("arbitrary", "arbitrary")),
    )(dy0, states, states, u, acat, bsup, csup, d_skip, *s_ops)
    return _split_side(res, 5, side)


def _discretise(ar, ai, ldt, br, bi):
    dt = jnp.exp(ldt)
    lr, li = ar * dt, ai * dt
    e = jnp.exp(lr)
    abar_r, abar_i = e * jnp.cos(li), e * jnp.sin(li)
    den = ar * ar + ai * ai
    coef_r = ((abar_r - 1.0) * ar + abar_i * ai) / den
    coef_i = (abar_i * ar - (abar_r - 1.0) * ai) / den
    return abar_r, abar_i, coef_r * br - coef_i * bi, coef_r * bi + coef_i * br


def _group_mask():
    shape = (LANES, SCAN_LANES)
    return (lax.broadcasted_iota(jnp.int32, shape, 0) // SSM_GROUP
            == lax.broadcasted_iota(jnp.int32, shape, 1) // SSM_STATE)


def _ssm_mats_fwd(a_re, a_im, log_dt, b_re, b_im, c_re, c_im, name):
    nl = N_STATE // SCAN_LANES
    lc = SCAN_LANES

    def body(ar, ai, ldt, br, bi, cr, ci, acat, bsup, csup):
        abar_r, abar_i, bbar_r, bbar_i = _discretise(ar[...], ai[...], ldt[...], br[...], bi[...])
        same = _group_mask()
        spread = lambda m, j: jnp.where(same, jnp.tile(m[:, j * lc:(j + 1) * lc], (LANES // SSM_GROUP, 1)), 0.0)
        c_r, c_i = cr[...], -ci[...]
        for j in range(nl):
            acat[:, 2 * j * lc:(2 * j + 1) * lc] = abar_r[:, j * lc:(j + 1) * lc]
            acat[:, (2 * j + 1) * lc:(2 * j + 2) * lc] = abar_i[:, j * lc:(j + 1) * lc]
            bsup[j, :, :lc] = spread(bbar_r, j)
            bsup[j, :, lc:] = spread(bbar_i, j)
            csup[j, :, :lc] = spread(c_r, j)
            csup[j, :, lc:] = spread(c_i, j)

    return pl.pallas_call(
        body, name=name,
        out_shape=[jax.ShapeDtypeStruct((1, 2 * N_STATE), F32), jax.ShapeDtypeStruct((nl, LANES, 2 * lc), F32),
                   jax.ShapeDtypeStruct((nl, LANES, 2 * lc), F32)],
        compiler_params=_params(),
    )(a_re, a_im, log_dt, b_re, b_im, c_re, c_im)


def _ssm_mats_bwd(a_re, a_im, log_dt, b_re, b_im, d_acat, d_bsup, d_csup, name):
    nl = N_STATE // SCAN_LANES
    lc = SCAN_LANES

    def body(ar, ai, ldt, br, bi, dac, dbs, dcs, d_ar, d_ai, d_ldt, d_br, d_bi, d_cr, d_ci):
        same = _group_mask()

        def gather(ref, j, half):
            m = jnp.where(same, ref[j, :, half * lc:(half + 1) * lc], 0.0)
            tot = m[:SSM_GROUP]
            for k in range(1, LANES // SSM_GROUP):
                tot = tot + m[k * SSM_GROUP:(k + 1) * SSM_GROUP]
            return tot

        cols = lambda ref, half: jnp.concatenate([gather(ref, j, half) for j in range(nl)], axis=1)
        d_abar_r = jnp.concatenate([dac[:, 2 * j * lc:(2 * j + 1) * lc] for j in range(nl)], axis=1)
        d_abar_i = jnp.concatenate([dac[:, (2 * j + 1) * lc:(2 * j + 2) * lc] for j in range(nl)], axis=1)
        _, vjp = jax.vjp(_discretise, ar[...], ai[...], ldt[...], br[...], bi[...])
        outs = vjp((d_abar_r, d_abar_i, cols(dbs, 0), cols(dbs, 1)))
        for ref, val in zip((d_ar, d_ai, d_ldt, d_br, d_bi), outs):
            ref[...] = val
        d_cr[...] = cols(dcs, 0)
        d_ci[...] = -cols(dcs, 1)

    row = jax.ShapeDtypeStruct((1, N_STATE), F32)
    mat = jax.ShapeDtypeStruct((SSM_GROUP, N_STATE), F32)
    return pl.pallas_call(
        body, name=name, out_shape=[row, row, row, mat, mat, mat, mat], compiler_params=_params(),
    )(a_re, a_im, log_dt, b_re, b_im, d_acat, d_bsup, d_csup)


def _states_on_lanes(sm):
    flat = lambda a: a.reshape(1, N_STATE)
    chan_b = lambda b: jnp.transpose(b, (2, 0, 1)).reshape(SSM_GROUP, N_STATE)
    chan_c = lambda c: jnp.transpose(c, (1, 0, 2)).reshape(SSM_GROUP, N_STATE)
    return (flat(sm["ssm_a_re"]), flat(sm["ssm_a_im"]), flat(jnp.repeat(sm["ssm_log_dt"], SSM_STATE)),
            chan_b(sm["ssm_b_re"]), chan_b(sm["ssm_b_im"]), chan_c(sm["ssm_c_re"]), chan_c(sm["ssm_c_im"]))


def _from_states_on_lanes(d_ar, d_ai, d_ldt, d_br, d_bi, d_cr, d_ci):
    grp = lambda a: a.reshape(SSM_GROUPS, SSM_STATE)
    back_b = lambda b: jnp.transpose(b.reshape(SSM_GROUP, SSM_GROUPS, SSM_STATE), (1, 2, 0))
    back_c = lambda c: jnp.transpose(c.reshape(SSM_GROUP, SSM_GROUPS, SSM_STATE), (1, 0, 2))
    return (grp(d_ar), grp(d_ai), jnp.sum(grp(d_ldt), axis=1), back_b(d_br), back_b(d_bi), back_c(d_cr), back_c(d_ci))


def _mem_fwd(mem, g_mem, w_kv, g_k, name):
    ml = mem.shape[0]

    def body(mem_ref, gm_ref, w_ref, gk_ref, memn_ref, kv_ref, kn_ref, vv_ref):
        memn = _rms(mem_ref[...], gm_ref[...])
        memn_ref[...] = memn.astype(BF16)
        kv = _dot(memn, w_ref[...])
        kv_ref[...] = kv
        for hh in range(XA_HEADS):
            sl = slice(hh * XA_HEAD_DIM, (hh + 1) * XA_HEAD_DIM)
            kn_ref[:, sl] = _rms(kv[:, sl], gk_ref[...]).astype(BF16)
        vv_ref[...] = kv[:, XA_WIDTH:].astype(BF16)

    return pl.pallas_call(
        body, name=name,
        out_shape=[jax.ShapeDtypeStruct((ml, D_MODEL), BF16), jax.ShapeDtypeStruct((ml, 2 * XA_WIDTH), F32),
                   jax.ShapeDtypeStruct((ml, XA_WIDTH), BF16), jax.ShapeDtypeStruct((ml, XA_WIDTH), BF16)],
        compiler_params=_params(),
    )(mem, g_mem, w_kv, g_k)


def _mem_bwd(mem, g_mem, memn, w_kv, kv, g_k, dkn, dvv, name):
    def body(mem_ref, gm_ref, memn_ref, w_ref, kv_ref, gk_ref, dkn_ref, dvv_ref, dw_ref, dgm_ref, dgk_ref):
        kv = kv_ref[...]
        dgk = jnp.zeros(dgk_ref.shape, F32)
        parts = []
        for hh in range(XA_HEADS):
            sl = slice(hh * XA_HEAD_DIM, (hh + 1) * XA_HEAD_DIM)
            _, vjp = jax.vjp(_rms, kv[:, sl], gk_ref[...])
            dk, dg = vjp(dkn_ref[:, sl])
            parts.append(dk)
            dgk = dgk + dg
        dgk_ref[...] = dgk
        dkv = jnp.concatenate(parts + [dvv_ref[...]], axis=1)
        dw_ref[...] = _dot(memn_ref[...], dkv, TN)
        dmemn = _dot(dkv, w_ref[...], NT)
        _, vjp = jax.vjp(_rms, mem_ref[...], gm_ref[...])
        dgm_ref[...] = vjp(dmemn)[1]

    return pl.pallas_call(
        body, name=name,
        out_shape=[jax.ShapeDtypeStruct((D_MODEL, 2 * XA_WIDTH), F32), jax.ShapeDtypeStruct(g_mem.shape, F32),
                   jax.ShapeDtypeStruct(g_k.shape, F32)],
        compiler_params=_params(),
    )(mem, g_mem, memn, w_kv, kv, g_k, dkn, dvv)


def _xa_head(qx_h, g_q, kn_h, vv_h):
    qn = _rms(qx_h, g_q)
    sc = _dot(qn, kn_h, NT) * (XA_HEAD_DIM ** -0.5)
    sc = sc - jnp.max(sc, axis=-1, keepdims=True)
    e = jnp.exp(sc)
    p = e / jnp.sum(e, axis=-1, keepdims=True)
    return qn, p


def _xa_fwd(qx, g_q, kn, vv, name):
    def fn(qt, gq, knt, vvt):
        outs = []
        for hh in range(XA_HEADS):
            sl = slice(hh * XA_HEAD_DIM, (hh + 1) * XA_HEAD_DIM)
            _, p = _xa_head(qt[:, sl], gq, knt[:, sl], vvt[:, sl])
            outs.append(_dot(p, vvt[:, sl]))
        return (jnp.concatenate(outs, axis=1),), ()

    return _rw(fn, [qx], [g_q, kn, vv], [(XA_WIDTH, BF16)], [], name, tm=512)[0]


def _xa_bwd(qx, g_q, kn, vv, do, name):
    def fn(qt, dot_, gq, knt, vvt):
        dqs, dks, dvs = [], [], []
        dgq = jnp.zeros_like(gq)
        for hh in range(XA_HEADS):
            sl = slice(hh * XA_HEAD_DIM, (hh + 1) * XA_HEAD_DIM)
            qn, p = _xa_head(qt[:, sl], gq, knt[:, sl], vvt[:, sl])
            doh = dot_[:, sl]
            dp = _dot(doh, vvt[:, sl], NT)
            dvs.append(_dot(p, doh, TN))
            ds = p * (dp - jnp.sum(dp * p, axis=-1, keepdims=True)) * (XA_HEAD_DIM ** -0.5)
            dqn = _dot(ds, knt[:, sl])
            dks.append(_dot(ds, qn, TN))
            _, vjp = jax.vjp(_rms, qt[:, sl], gq)
            dq, dg = vjp(dqn)
            dqs.append(dq)
            dgq = dgq + dg
        return ((jnp.concatenate(dqs, axis=1),),
                (jnp.concatenate(dks, axis=1), jnp.concatenate(dvs, axis=1), dgq))

    return _rw(fn, [qx, do], [g_q, kn, vv], [(XA_WIDTH, BF16)], [kn.shape, vv.shape, g_q.shape], name, tm=512)


BIG = [
    ("w_in", (D_MODEL, IN_WIDTH), 1), ("ssm_w_glu", (SSM_WIDTH, SSM_WIDTH), 0), ("w_out", (D_MODEL, D_MODEL), 0),
    ("xa_w_q", (D_MODEL, XA_WIDTH), 0), ("xa_w_kv", (D_MODEL, 2 * XA_WIDTH), 0), ("xa_w_o", (XA_WIDTH, D_MODEL), 1),
    ("w_up", (D_MODEL, D_FF), 1), ("w_down", (D_FF, D_MODEL), 0),
]
BIG_INDEX = {n: i for i, (n, _, _) in enumerate(BIG)}


def _shard_shape(shape, axis):
    return tuple(d // N_DEV if i == axis else d for i, d in enumerate(shape))


def _shard_of(ref, axis, d):
    n = ref.shape[axis] // N_DEV
    return ref.at[pl.ds(d * n, n), :] if axis == 0 else ref.at[:, pl.ds(d * n, n)]


def _gather_side(names, shards):
    idxs = [BIG_INDEX[n] for n in names]

    def make(ins, outs, send_sems, recv_sems):
        x, y, c = lax.axis_index("x"), lax.axis_index("y"), lax.axis_index("c")
        cps = []
        for j, i in enumerate(idxs):
            mine = _shard_of(outs[j], BIG[i][2], 4 * x + 2 * y + c)
            cps.append(pltpu.make_async_copy(ins[j], mine, send_sems.at[N_DEV * j]))
            for rel in range(1, N_DEV):
                to = tuple(1 - p if rel >> bit & 1 else p for p, bit in ((x, 2), (y, 1), (c, 0)))
                cps.append(pltpu.make_async_remote_copy(
                    src_ref=ins[j], dst_ref=mine, send_sem=send_sems.at[N_DEV * j + rel],
                    recv_sem=recv_sems.at[N_DEV * j + rel], device_id=to, device_id_type=MESH))
        return cps

    return _Side(shards, [jax.ShapeDtypeStruct(BIG[i][1], BF16) for i in idxs], N_DEV * len(idxs), make)


def _gather_two_level_side(names, shards):
    idxs = [BIG_INDEX[n] for n in names]

    def parts(ins, outs, send_sems, recv_sems):
        x, y, c = lax.axis_index("x"), lax.axis_index("y"), lax.axis_index("c")
        sibling = (x, y, 1 - c)
        chips = [(1 - x, y), (x, 1 - y), (1 - x, 1 - y)]
        mine, first, passed, arrived, from_sibling = [], [], [], [], []
        for w, i in enumerate(idxs):
            def place(dev, w=w, i=i):
                return _shard_of(outs[w], BIG[i][2], 4 * dev[0] + 2 * dev[1] + dev[2])

            def copy(k, blk, to, src=None, w=w, place=place):
                return pltpu.make_async_remote_copy(
                    src_ref=place(blk) if src is None else src, dst_ref=place(blk),
                    send_sem=send_sems.at[N_DEV * w + k], recv_sem=recv_sems.at[N_DEV * w + k], device_id=to,
                    device_id_type=MESH)

            mine.append(pltpu.make_async_copy(ins[w], place((x, y, c)), send_sems.at[N_DEV * w + 7]))
            first.append(copy(0, (x, y, c), sibling, src=ins[w]))
            first += [copy(1 + j, (x, y, c), (*chip, c), src=ins[w]) for j, chip in enumerate(chips)]
            passed += [copy(4 + j, (*chip, c), sibling) for j, chip in enumerate(chips)]
            arrived += [copy(1 + j, (*chip, c), (x, y, c)) for j, chip in enumerate(chips)]
            from_sibling.append(copy(0, sibling, (x, y, c)))
            from_sibling += [copy(4 + j, (*chip, 1 - c), (x, y, c)) for j, chip in enumerate(chips)]
        return mine, first, passed, arrived, from_sibling

    def make(ins, outs, send_sems, recv_sems):
        mine, first, _, _, _ = parts(ins, outs, send_sems, recv_sems)
        return mine + first

    def finish(ins, outs, send_sems, recv_sems):
        mine, first, passed, arrived, from_sibling = parts(ins, outs, send_sems, recv_sems)
        for got, onward in zip(arrived, passed):
            got.wait_recv()
            onward.start()
        for cp in from_sibling:
            cp.wait_recv()
        for cp in first + passed:
            cp.wait_send()
        for cp in mine:
            cp.wait()

    return _Side(shards, [jax.ShapeDtypeStruct(BIG[i][1], BF16) for i in idxs], N_DEV * len(idxs), make, finish)


def _sibling_side(names, grads):
    idxs = [BIG_INDEX[n] for n in names]

    def make(ins, outs, send_sems, recv_sems):
        x, y, c = lax.axis_index("x"), lax.axis_index("y"), lax.axis_index("c")
        return [pltpu.make_async_remote_copy(
            src_ref=_shard_of(ins[j], BIG[i][2], 2 * k + (1 - c)), dst_ref=outs[j].at[k],
            send_sem=send_sems.at[4 * j + k], recv_sem=recv_sems.at[4 * j + k], device_id=(x, y, 1 - c),
            device_id_type=MESH) for j, i in enumerate(idxs) for k in range(4)]

    shapes = [jax.ShapeDtypeStruct((4,) + _shard_shape(BIG[i][1], BIG[i][2]), F32) for i in idxs]
    return _Side(grads, shapes, 4 * len(idxs), make)


def _chips_side(parts):
    def make(ins, outs, send_sems, recv_sems):
        x, y, c = lax.axis_index("x"), lax.axis_index("y"), lax.axis_index("c")
        chips = [(1 - x, y), (x, 1 - y), (1 - x, 1 - y)]
        return [pltpu.make_async_remote_copy(
            src_ref=ins[j].at[2 * cx + cy], dst_ref=outs[j].at[r], send_sem=send_sems.at[3 * j + r],
            recv_sem=recv_sems.at[3 * j + r], device_id=(cx, cy, c), device_id_type=MESH)
            for r, (cx, cy) in enumerate(chips) for j in range(len(parts))]

    return _Side(parts, [jax.ShapeDtypeStruct((3,) + p.shape[1:], p.dtype) for p in parts], 3 * len(parts), make)


def _reduce_add(grad, recv, axis, core, name):
    rs, cs = recv.shape[1:]
    rt = _row_tile(rs, 256)
    nt = rs // rt

    def body(c_ref, g_ref, r_ref, p_ref, pb_ref):
        sm = g_ref[...] + r_ref[0]
        p_ref[0] = sm
        pb_ref[0] = sm.astype(BF16)

    if axis == 0:
        g_spec = pl.BlockSpec((rt, cs), lambda k, t, c_ref: ((2 * k + c_ref[0]) * nt + t, 0))
    else:
        g_spec = pl.BlockSpec((rt, cs), lambda k, t, c_ref: (t, 2 * k + c_ref[0]))
    slab = pl.BlockSpec((1, rt, cs), lambda k, t, c_ref: (k, t, 0))
    return pl.pallas_call(
        body, name=name,
        grid_spec=pltpu.PrefetchScalarGridSpec(num_scalar_prefetch=1, grid=(4, nt), in_specs=[g_spec, slab],
                                               out_specs=[slab, slab]),
        out_shape=[jax.ShapeDtypeStruct(recv.shape, F32), jax.ShapeDtypeStruct(recv.shape, BF16)],
        compiler_params=_params(("parallel", "parallel")),
    )(core, grad, recv)


def _all_gather(block, name, side):
    m_per, n = block.shape
    ns_in, ns_out = len(side.ins), len(side.out_shapes)

    def body(*refs):
        x_ref, s_ins, out_ref = refs[0], refs[1:1 + ns_in], refs[1 + ns_in]
        s_outs = refs[2 + ns_in:2 + ns_in + ns_out]
        send_sems, recv_sems, local_sem, s_send, s_recv = refs[2 + ns_in + ns_out:]
        others = side.make(s_ins, s_outs, s_send, s_recv)
        for cp in others:
            cp.start()
        x, y, c = lax.axis_index("x"), lax.axis_index("y"), lax.axis_index("c")
        me, sibling = (x, y, c), (x, y, 1 - c)
        chips = [(1 - x, y), (x, 1 - y), (1 - x, 1 - y)]

        def rows(px, py, pc):
            return out_ref.at[pl.ds((4 * px + 2 * py + pc) * m_per, m_per), :]

        def copy(k, blk, to, src=None):
            return pltpu.make_async_remote_copy(
                src_ref=rows(*blk) if src is None else src, dst_ref=rows(*blk),
                send_sem=send_sems.at[k], recv_sem=recv_sems.at[k], device_id=to, device_id_type=MESH)

        mine = pltpu.make_async_copy(x_ref, rows(*me), local_sem)
        mine.start()
        first = [copy(0, me, sibling, src=x_ref)]
        first += [copy(1 + j, me, (*chip, c), src=x_ref) for j, chip in enumerate(chips)]
        for cp in first:
            cp.start()
        passed = [copy(4 + j, (*chip, c), sibling) for j, chip in enumerate(chips)]
        for j, chip in enumerate(chips):
            copy(1 + j, (*chip, c), me).wait_recv()
            passed[j].start()
        copy(0, sibling, me).wait_recv()
        for j, chip in enumerate(chips):
            copy(4 + j, (*chip, 1 - c), me).wait_recv()
        for cp in first + passed:
            cp.wait_send()
        mine.wait()
        for cp in others:
            cp.wait()

    res = pl.pallas_call(
        body, name=name, in_specs=[ANY] * (1 + ns_in), out_specs=[ANY] * (1 + ns_out),
        out_shape=[jax.ShapeDtypeStruct((N_DEV * m_per, n), block.dtype)] + side.out_shapes,
        scratch_shapes=[pltpu.SemaphoreType.DMA((7,)), pltpu.SemaphoreType.DMA((7,)), pltpu.SemaphoreType.DMA]
        + side.sems(),
    )(block, *side.ins)
    return res[0], list(res[1:])


def _adam_math(w, g, m, v):
    m = ADAM_B1 * m + (1.0 - ADAM_B1) * g
    v = ADAM_B2 * v + (1.0 - ADAM_B2) * (g * g)
    m_hat = m / (1.0 - ADAM_B1 ** ADAM_STEP)
    v_hat = v / (1.0 - ADAM_B2 ** ADAM_STEP)
    delta = -ADAM_LR * (m_hat / (jnp.sqrt(v_hat) + ADAM_EPS) + ADAM_WD * w)
    return delta, m, v


def _adam_sharded(own, recv, w, m, v, chip, name):
    rs, cs = w.shape
    rt = _row_tile(rs, 256)

    def body(chip_ref, p_ref, r_ref, w_ref, m_ref, v_ref, g_out, d_out, m_out, v_out):
        g = p_ref[0] + r_ref[0].astype(F32) + r_ref[1].astype(F32) + r_ref[2].astype(F32)
        d, mn, vn = _adam_math(w_ref[...], g, m_ref[...], v_ref[...])
        g_out[...] = g
        d_out[...] = d
        m_out[...] = mn
        v_out[...] = vn

    tile = pl.BlockSpec((rt, cs), lambda t, chip_ref: (t, 0))
    return pl.pallas_call(
        body, name=name,
        grid_spec=pltpu.PrefetchScalarGridSpec(
            num_scalar_prefetch=1, grid=(rs // rt,),
            in_specs=[pl.BlockSpec((1, rt, cs), lambda t, chip_ref: (chip_ref[0], t, 0)),
                      pl.BlockSpec((3, rt, cs), lambda t, chip_ref: (0, t, 0)), tile, tile, tile],
            out_specs=[tile] * 4),
        out_shape=[jax.ShapeDtypeStruct((rs, cs), F32)] * 4,
        compiler_params=_params(("parallel",)),
    )(chip, own, recv, w, m, v)


SMALL = ["g_mix", "ssm_a_re", "ssm_a_im", "ssm_log_dt", "ssm_b_re", "ssm_b_im", "ssm_c_re", "ssm_c_im", "ssm_d",
         "sb_g_q", "sb_g_k", "g_out_ssm", "g_out_sb", "g_xa", "g_mem", "xa_g_q", "xa_g_k", "g_mlp"]
PACK_TILE = SUBLANES * LANES


def _natural_2d(n):
    return (n // LANES, LANES) if n % LANES == 0 else (1, n)


def _pack_small(arrs):
    parts = []
    for a in arrs:
        flat = a.reshape(-1)
        parts.append(jnp.pad(flat, (0, (-flat.shape[0]) % PACK_TILE)))
    return jnp.concatenate(parts).reshape(-1, LANES)


def _adam_replicated(gathered, sizes, ws, ms, vs, name):
    n_w = len(ws)
    r_dev = gathered.shape[0] // N_DEV
    offs, off = [], 0
    for n in sizes:
        offs.append(off)
        off += (n + PACK_TILE - 1) // PACK_TILE * SUBLANES
    assert off == r_dev

    def body(*refs):
        g_ref = refs[0]
        w_refs, m_refs, v_refs = refs[1:1 + n_w], refs[1 + n_w:1 + 2 * n_w], refs[1 + 2 * n_w:1 + 3 * n_w]
        outs = refs[1 + 3 * n_w:]

        def total(i, shape):
            r, cdim = shape
            acc = g_ref[pl.ds(offs[i], r), :cdim]
            for d in range(1, N_DEV):
                acc = acc + g_ref[pl.ds(d * r_dev + offs[i], r), :cdim]
            return acc

        for i in range(n_w):
            g = total(i, w_refs[i].shape)
            d, mn, vn = _adam_math(w_refs[i][...], g, m_refs[i][...], v_refs[i][...])
            for o, val in zip(outs[4 * i:4 * i + 4], (g, d, mn, vn)):
                o[...] = val
        outs[4 * n_w][...] = total(n_w, (SUBLANES, LANES))

    shapes = [w.shape for w in ws]
    return pl.pallas_call(
        body, name=name,
        out_shape=[jax.ShapeDtypeStruct(shp, F32) for shp in shapes for _ in range(4)]
        + [jax.ShapeDtypeStruct((SUBLANES, LANES), F32)],
        compiler_params=_params(),
    )(gathered, *ws, *ms, *vs)


def _step(x, mem, target, shards, sm, core):
    g, w, sums, reduced = {}, {}, {}, {}

    def gather(names):
        return _gather_two_level_side(names, [shards[n] for n in names])

    def to_sibling(names):
        return _sibling_side(names, [g[n] for n in names])

    def add_sibling(names, received):
        for n, r in zip(names, received):
            sums[n] = _reduce_add(g[n], r, BIG[BIG_INDEX[n]][2], core, "reduce_add_" + n)

    def to_chips(names):
        return _chips_side([sums[n][1] for n in names])

    def keep(names, received):
        for n, r in zip(names, received):
            reduced[n] = (sums[n][0], r)

    row = lambda a: a.reshape(1, -1)
    g_mix, g_xa, g_mlp, g_mem = row(sm["g_mix"]), row(sm["g_xa"]), row(sm["g_mlp"]), row(sm["g_mem"])
    g_os, g_ob = row(sm["g_out_ssm"]), row(sm["g_out_sb"])
    sb_gq, sb_gk = jnp.tile(row(sm["sb_g_q"]), (1, SB_HEADS)), jnp.tile(row(sm["sb_g_k"]), (1, SB_HEADS))
    xa_gq, xa_gk = row(sm["xa_g_q"]), row(sm["xa_g_k"])
    d_skip = row(sm["ssm_d"])

    h1, (w["w_in"],) = _norm_fwd(x, g_mix, "norm_mix", side=gather(["w_in"]))
    proj = _mm(h1, w["w_in"], "nn", "in_proj", tn=IN_WIDTH)
    u = _to_segments(proj[:, :SSM_WIDTH])
    q_raw, k_raw = (proj, SB_WIDTH, 1), (proj, SB_WIDTH, 2)
    v_col = (SSM_WIDTH + 2 * SB_WIDTH) // LANES
    sb_scale = SB_HEAD_DIM ** -0.5
    qs, ks = _rw(lambda qt, kt, gq, gk: ((_rms_groups(qt, gq, sb_scale), _rms_groups(kt, gk, 1.0)), ()),
                 [q_raw, k_raw], [sb_gq, sb_gk], [(SB_WIDTH, BF16)] * 2, [], "sb_qk_norm")
    early = ["ssm_w_glu", "w_out", "xa_w_q", "xa_w_kv", "xa_w_o", "w_up"]
    y_sb, got = _sb_fwd(qs, ks, proj, "sb_fwd", v_col=v_col, side=gather(early))
    w.update(zip(early, got))

    ssm_args = _states_on_lanes(sm)
    acat, bsup, csup = _ssm_mats_fwd(*ssm_args, "ssm_mats")
    (states, y0, y1), (w["w_down"],) = _ssm_fwd(u, acat, bsup, csup, d_skip, "ssm_fwd",
                                                side=_gather_side(["w_down"], [shards["w_down"]]))
    z_glu, y_ssm = _mm(y1, w["ssm_w_glu"], "nn", "ssm_glu", epi=lambda r, yt: (r, yt * jax.nn.sigmoid(r)),
                       extras=(y1,), out_dtypes=(F32, F32))
    y_ssm = _from_segments(y_ssm)

    def cat_norm(a, b, ga, gb):
        return jnp.concatenate([_rms(a, ga), _rms(b, gb)], axis=1)

    ycat = _rw(lambda a, b, ga, gb: ((cat_norm(a, b, ga, gb),), ()), [y_ssm, y_sb], [g_os, g_ob],
               [(D_MODEL, BF16)], [], "norm_out")[0]

    def residual_norm_epi(r, xt, gt):
        xn = r + xt
        return xn, _rms(xn, gt)

    x1, h2 = _mm(ycat, w["w_out"], "nn", "out_proj", epi=residual_norm_epi, extras=(x,), fulls=(g_xa,),
                 out_dtypes=(F32, BF16))
    qx = _mm(h2, w["xa_w_q"], "nn", "xa_q")
    memn, kv, kn_x, vv_x = _mem_fwd(mem, g_mem, w["xa_w_kv"], xa_gk, "xa_mem")
    o_xa = _xa_fwd(qx, xa_gq, kn_x, vv_x, "xa_fwd")
    x2, h3 = _mm(o_xa, w["xa_w_o"], "nn", "xa_o", epi=residual_norm_epi, extras=(x1,), fulls=(g_mlp,),
                 out_dtypes=(F32, BF16))

    def up_epi(r):
        rl = jnp.maximum(r, 0.0)
        return (rl * rl,)

    r_up = _mm(h3, w["w_up"], "nn", "mlp_up", epi=up_epi, out_dtypes=(BF16,), tm=2048, tn=2048)

    def loss_epi(r, xt, tt):
        d = r + xt - tt
        return (d * (1.0 / D_MODEL),) * 2, (jnp.sum(d * d, axis=0, keepdims=True),)

    dx3, dx3_b, sq = _mm(r_up, w["w_down"], "nn", "mlp_down", epi=loss_epi, extras=(x2, target),
                         out_dtypes=(F32, BF16), sums=[(1, D_MODEL)], epi_slabs=4)
    loss = jnp.sum(sq) * (0.5 / D_MODEL)

    def norm_bwd_epi(r, xt, drt, gt):
        _, vjp = jax.vjp(_rms, xt, gt)
        dx_, dg_ = vjp(r)
        return (dx_ + drt,) * 2, (dg_,)

    g["w_down"] = _mm(r_up, dx3_b, "tn", "d_w_down", tk=2048)
    da = _mm(dx3_b, w["w_down"], "nt", "d_r", epi=lambda r, rt: (r * 2.0 * jnp.sqrt(rt.astype(F32)),), extras=(r_up,),
             out_dtypes=(BF16,), tn=2048)
    g["w_up"] = _mm(h3, da, "tn", "d_w_up", tk=2048)
    mlp = ["w_down", "w_up"]
    (dx2, dx2_b, g["g_mlp"]), got = _mm(da, w["w_up"], "nt", "d_h3", epi=norm_bwd_epi, extras=(x2, dx3),
                                        fulls=(g_mlp,), out_dtypes=(F32, BF16), sums=[g_mlp.shape], epi_slabs=4,
                                        side=to_sibling(mlp))
    add_sibling(mlp, got)
    g["xa_w_o"] = _mm(o_xa, dx2_b, "tn", "d_xa_w_o", tk=2048)
    do_xa = _mm(dx2_b, w["xa_w_o"], "nt", "d_o_xa")
    dqx, dkn_x, dvv_x, g["xa_g_q"] = _xa_bwd(qx, xa_gq, kn_x, vv_x, do_xa, "xa_bwd")
    g["xa_w_kv"], g["g_mem"], g["xa_g_k"] = _mem_bwd(mem, g_mem, memn, w["xa_w_kv"], kv, xa_gk, dkn_x, dvv_x,
                                                     "xa_mem_bwd")
    g["xa_w_q"] = _mm(h2, dqx, "tn", "d_xa_w_q", tk=2048)
    dx1, dx1_b, g["g_xa"] = _mm(dqx, w["xa_w_q"], "nt", "d_h2", epi=norm_bwd_epi, extras=(x1, dx2), fulls=(g_xa,),
                                out_dtypes=(F32, BF16), sums=[g_xa.shape], epi_slabs=4)
    g["w_out"] = _mm(ycat, dx1_b, "tn", "d_w_out", tk=2048)
    dycat = _mm(dx1_b, w["w_out"], "nt", "d_ycat")

    def cat_bwd(a, b, dy, ga, gb):
        _, vjp = jax.vjp(cat_norm, a, b, ga, gb)
        da_, db_, dga, dgb = vjp(dy)
        return (da_, db_), (dga, dgb)

    dy_ssm, dy_sb, g["g_out_ssm"], g["g_out_sb"] = _rw(
        cat_bwd, [y_ssm, y_sb, dycat], [g_os, g_ob], [(SSM_WIDTH, F32), (SB_WIDTH, F32)], [g_os.shape, g_ob.shape],
        "d_norm_out", tm=512)

    def glu_bwd(dy, yt, zt):
        sg = jax.nn.sigmoid(zt)
        return (dy * sg, dy * yt * sg * (1.0 - sg)), ()

    dy1_a, dz = _rw(glu_bwd, [_to_segments(dy_ssm), y1, z_glu], [], [(SSM_WIDTH, F32), (SSM_WIDTH, BF16)], [], "d_glu")
    g["ssm_w_glu"] = _mm(y1, dz, "tn", "d_w_glu", tk=2048)

    def gelu_bwd_epi(r, da_, y0t):
        _, vjp = jax.vjp(jax.nn.gelu, y0t)
        return (vjp(r + da_)[0],)

    mid = ["w_out", "xa_w_q", "xa_w_kv", "xa_w_o", "ssm_w_glu"]
    dy0, got = _mm(dz, w["ssm_w_glu"], "nt", "d_y1", epi=gelu_bwd_epi, extras=(dy1_a, y0), side=to_sibling(mid))
    add_sibling(mid, got)
    (du, da8, d_bsup, d_csup, g["ssm_d"]), got = _ssm_bwd(dy0, states, u, acat, bsup, csup, d_skip, "ssm_bwd",
                                                          side=to_chips(mlp))
    keep(mlp, got)
    d_acat = jnp.sum(da8, axis=0, keepdims=True)
    d_mats = _ssm_mats_bwd(*ssm_args[:5], d_acat, d_bsup, d_csup, "ssm_mats_bwd")
    for nm, val in zip(("ssm_a_re", "ssm_a_im", "ssm_log_dt", "ssm_b_re", "ssm_b_im", "ssm_c_re", "ssm_c_im"),
                       _from_states_on_lanes(*d_mats)):
        g[nm] = val

    (dqs, dks, dvs), got = _sb_bwd(qs, ks, proj, y_sb, dy_sb, "sb_bwd", v_col=v_col, side=to_chips(mid))
    keep(mid, got)

    def d_proj_rows(du_t, qt, dqt, kt, dkt, dvt, gq, gk):
        _, vjp_q = jax.vjp(lambda a, b_: _rms_groups(a, b_, sb_scale), qt, gq)
        _, vjp_k = jax.vjp(lambda a, b_: _rms_groups(a, b_, 1.0), kt, gk)
        (dq_, dgq_), (dk_, dgk_) = vjp_q(dqt), vjp_k(dkt)
        rows = jnp.concatenate([du_t, dq_.astype(BF16), dk_.astype(BF16), dvt.astype(BF16)], axis=1)
        return (rows,), (dgq_, dgk_)

    dproj, dgq, dgk = _rw(d_proj_rows, [_from_segments(du), q_raw, dqs, k_raw, dks, dvs], [sb_gq, sb_gk],
                          [(IN_WIDTH, BF16)], [sb_gq.shape, sb_gk.shape], "d_proj", tm=512)
    g["sb_g_q"] = jnp.sum(dgq.reshape(SB_HEADS, SB_HEAD_DIM), axis=0)
    g["sb_g_k"] = jnp.sum(dgk.reshape(SB_HEADS, SB_HEAD_DIM), axis=0)
    g["w_in"] = _mm(h1, dproj, "tn", "d_w_in", tn=IN_WIDTH)
    dh1, got = _mm(dproj, w["w_in"], "nt", "d_h1", tk=IN_WIDTH, side=to_sibling(["w_in"]))
    add_sibling(["w_in"], got)
    dx, g["g_mix"] = _norm_bwd(x, g_mix, dh1, dx1, "d_norm_mix")

    packed = _pack_small([g[n] for n in SMALL] + [loss.reshape(1)])
    everyone, got = _all_gather(packed, "gather_small", to_chips(["w_in"]))
    keep(["w_in"], got)
    return dx, everyone, reduced


def kernel(x, mem, g_mix, w_in, ssm_a_re, ssm_a_im, ssm_log_dt, ssm_b_re, ssm_b_im, ssm_c_re, ssm_c_im, ssm_d, ssm_w_glu, sb_g_q, sb_g_k, g_out_ssm, g_out_sb, w_out, g_xa, g_mem, xa_w_q, xa_w_kv, xa_g_q, xa_g_k, xa_w_o, g_mlp, w_up, w_down, loss_target, m_g_mix, m_w_in, m_ssm_a_re, m_ssm_a_im, m_ssm_log_dt, m_ssm_b_re, m_ssm_b_im, m_ssm_c_re, m_ssm_c_im, m_ssm_d, m_ssm_w_glu, m_sb_g_q, m_sb_g_k, m_g_out_ssm, m_g_out_sb, m_w_out, m_g_xa, m_g_mem, m_xa_w_q, m_xa_w_kv, m_xa_g_q, m_xa_g_k, m_xa_w_o, m_g_mlp, m_w_up, m_w_down, v_g_mix, v_w_in, v_ssm_a_re, v_ssm_a_im, v_ssm_log_dt, v_ssm_b_re, v_ssm_b_im, v_ssm_c_re, v_ssm_c_im, v_ssm_d, v_ssm_w_glu, v_sb_g_q, v_sb_g_k, v_g_out_ssm, v_g_out_sb, v_w_out, v_g_xa, v_g_mem, v_xa_w_q, v_xa_w_kv, v_xa_g_q, v_xa_g_k, v_xa_w_o, v_g_mlp, v_w_up, v_w_down):
    given = dict(locals())
    order = ["g_mix", "w_in", "ssm_a_re", "ssm_a_im", "ssm_log_dt", "ssm_b_re", "ssm_b_im", "ssm_c_re", "ssm_c_im",
             "ssm_d", "ssm_w_glu", "sb_g_q", "sb_g_k", "g_out_ssm", "g_out_sb", "w_out", "g_xa", "g_mem", "xa_w_q",
             "xa_w_kv", "xa_g_q", "xa_g_k", "xa_w_o", "g_mlp", "w_up", "w_down"]
    assert sorted([n for n, _, _ in BIG] + SMALL) == sorted(order)
    core = lax.axis_index("c").astype(jnp.int32).reshape(1)
    chip = (2 * lax.axis_index("x") + lax.axis_index("y")).astype(jnp.int32).reshape(1)

    shards = {n: given[n][0].astype(BF16) for n, _, _ in BIG}
    sm = {n: given[n][0] for n in SMALL}
    dx, everyone, reduced = _step(x[0], mem[0], loss_target[0], shards, sm, core)

    res = {}
    for n, _, _ in BIG:
        own, recv = reduced[n]
        outs = _adam_sharded(own, recv, given[n][0], given["m_" + n][0], given["v_" + n][0], chip, "adam_" + n)
        for kind, val in zip(("grad", "delta", "new_m", "new_v"), outs):
            res[kind + "_" + n] = val[None]

    sizes = [math.prod(sm[n].shape) for n in SMALL] + [1]
    nat = lambda a: a.reshape(_natural_2d(math.prod(a.shape)))
    outs = _adam_replicated(everyone, sizes, [nat(sm[n]) for n in SMALL], [nat(given["m_" + n][0]) for n in SMALL],
                            [nat(given["v_" + n][0]) for n in SMALL], "adam_replicated")
    for i, n in enumerate(SMALL):
        for kind, val in zip(("grad", "delta", "new_m", "new_v"), outs[4 * i:4 * i + 4]):
            res[kind + "_" + n] = val.reshape(given[n].shape)
    loss_out = outs[-1][0, 0]
    return (loss_out, dx[None], *[res["grad_" + n] for n in order], *[res["delta_" + n] for n in order],
            *[res["new_m_" + n] for n in order], *[res["new_v_" + n] for n in order])
```
